```python
import jax
import jax.numpy as jnp
from jax import lax
import numpy as np

D_MODEL = 1024
BATCH = 32
SEQ = 2048
DEPTH = 2

CHUNK = 64
Q_BLOCK = 128
EPS = 1e-6
ROPE_THETA = 10000.0

RET_HEADS = 4
RET_HEAD_DIM = 64
RET_WIDTH = RET_HEADS * RET_HEAD_DIM

MLA_HEADS = 8
MLA_NOPE = 64
MLA_ROPE = 32
MLA_V = 64
MLA_WIDTH = MLA_HEADS * MLA_V
MLA_Q_RANK = 256
MLA_KV_RANK = 128

GLA_HEADS = 4
GLA_DK = 32
GLA_DV = 64
GLA_KWIDTH = GLA_HEADS * GLA_DK
GLA_WIDTH = GLA_HEADS * GLA_DV
GLA_GATE_RANK = 16
GLA_TAU = 16.0

MIX_WIDTH = RET_WIDTH + MLA_WIDTH + GLA_WIDTH

RET_SPLIT = (RET_WIDTH, RET_WIDTH, RET_WIDTH, RET_WIDTH)
MLA_SPLIT = (MLA_Q_RANK, MLA_KV_RANK, MLA_ROPE, MLA_WIDTH)
GLA_SPLIT = (GLA_KWIDTH, GLA_KWIDTH, GLA_WIDTH, GLA_GATE_RANK, GLA_WIDTH)
IN_COLS = sum(RET_SPLIT) + sum(MLA_SPLIT) + sum(GLA_SPLIT)

kernel_name = "hybrid_retention_mla_gla_streaming_block"


def rms_norm(x, w=None):
    xf = x.astype(jnp.float32)
    y = xf * lax.rsqrt(jnp.mean(xf * xf, axis=-1, keepdims=True) + EPS)
    if w is not None:
        y = y * w.astype(jnp.float32)
    return y.astype(x.dtype)


def split_cols(t, sizes):
    cuts = [int(s) for s in np.cumsum(sizes)[:-1]]
    return jnp.split(t, cuts, axis=-1)


def to_heads(t, n_heads):
    b, s, _ = t.shape
    return t.reshape(b, s, n_heads, -1)


def rope(x, pos):
    half = x.shape[-1] // 2
    inv = ROPE_THETA ** (-jnp.arange(half, dtype=jnp.float32) / half)
    ang = pos.astype(jnp.float32)[..., None] * inv
    cos = jnp.cos(ang)[:, :, None, :]
    sin = jnp.sin(ang)[:, :, None, :]
    x1 = x[..., :half].astype(jnp.float32)
    x2 = x[..., half:].astype(jnp.float32)
    out = jnp.concatenate([x1 * cos - x2 * sin, x2 * cos + x1 * sin], axis=-1)
    return out.astype(x.dtype)


def chunk_states(u, a):
    u_t = jnp.moveaxis(u, 1, 0)
    a_t = jnp.moveaxis(a, 1, 0)

    def step(s, inp):
        ui, ai = inp
        return ai[..., None] * s + ui, s

    _, s_prev = lax.scan(step, jnp.zeros_like(u_t[0]), (u_t, a_t))
    return jnp.moveaxis(s_prev, 0, 1)


def retention(q, k, v):
    b, s, h, d = q.shape
    dv = v.shape[-1]
    nc = s // CHUNK
    f32 = jnp.float32
    log_gamma = jnp.log1p(-jnp.exp2(-5.0 - jnp.arange(h, dtype=f32)))
    q = q.astype(f32).reshape(b, nc, CHUNK, h, d)
    k = k.astype(f32).reshape(b, nc, CHUNK, h, d) * (d ** -0.5)
    v = v.astype(f32).reshape(b, nc, CHUNK, h, dv)
    idx = jnp.arange(CHUNK, dtype=f32)
    decay = jnp.exp(log_gamma[:, None, None] * jnp.abs(idx[:, None] - idx[None, :]))
    scores = jnp.einsum('bnihd,bnjhd->bnhij', q, k) * decay
    intra = jnp.einsum('bnhij,bnjhe->bnihe', scores, v)
    k_w = jnp.exp((CHUNK - 1.0 - idx)[:, None] * log_gamma[None, :])
    u = jnp.einsum('bnjhd,jh,bnjhe->bnhde', k, k_w, v)
    a = jnp.broadcast_to(jnp.exp(CHUNK * log_gamma)[:, None], (b, nc, h, d))
    s_prev = chunk_states(u, a)
    q_w = jnp.exp((idx + 1.0)[:, None] * log_gamma[None, :])
    inter = jnp.einsum('bnihd,ih,bnhde->bnihe', q, q_w, s_prev)
    return (intra + inter).reshape(b, s, h, dv)


def mla(q_lat, kv_lat, k_rope, q_norm_w, w_uq, kv_norm_w, w_ukv, pos):
    b, s, _ = q_lat.shape
    q = (rms_norm(q_lat, q_norm_w) @ w_uq).reshape(b, s, MLA_HEADS, MLA_NOPE + MLA_ROPE)
    q_nope = q[..., :MLA_NOPE]
    q_pe = rope(q[..., MLA_NOPE:], pos)
    kv = (rms_norm(kv_lat, kv_norm_w) @ w_ukv).reshape(b, s, MLA_HEADS, MLA_NOPE + MLA_V)
    k_nope = kv[..., :MLA_NOPE]
    v = kv[..., MLA_NOPE:]
    k_pe = rope(k_rope[:, :, None, :], pos)[:, :, 0, :]
    scale = (MLA_NOPE + MLA_ROPE) ** -0.5
    key_chunk = jnp.arange(s) // CHUNK
    nb = s // Q_BLOCK
    qn_b = jnp.moveaxis(q_nope.reshape(b, nb, Q_BLOCK, MLA_HEADS, MLA_NOPE), 1, 0)
    qp_b = jnp.moveaxis(q_pe.reshape(b, nb, Q_BLOCK, MLA_HEADS, MLA_ROPE), 1, 0)
    starts = jnp.arange(nb, dtype=jnp.int32) * Q_BLOCK

    def block(args):
        qn, qp, i0 = args
        sc = (jnp.einsum('bqhd,bkhd->bhqk', qn, k_nope)
              + jnp.einsum('bqhd,bkd->bhqk', qp, k_pe)).astype(jnp.float32) * scale
        q_chunk = (i0 + jnp.arange(Q_BLOCK)) // CHUNK
        mask = key_chunk[None, :] <= q_chunk[:, None]
        sc = jnp.where(mask, sc, -jnp.inf)
        p = jax.nn.softmax(sc, axis=-1).astype(v.dtype)
        return jnp.einsum('bhqk,bkhe->bqhe', p, v)

    out = lax.map(block, (qn_b, qp_b, starts))
    return jnp.moveaxis(out, 0, 1).reshape(b, s, MLA_WIDTH)


def gla(q, k, v, g_low, w_g2, b_g2):
    b, s, h, dk = q.shape
    dv = v.shape[-1]
    nc = s // CHUNK
    f32 = jnp.float32
    log_a = jax.nn.log_sigmoid((g_low @ w_g2 + b_g2).astype(f32)) / GLA_TAU
    cum = jnp.cumsum(log_a.reshape(b, nc, CHUNK, h, dk), axis=2)
    q = q.astype(f32).reshape(b, nc, CHUNK, h, dk)
    k = k.astype(f32).reshape(b, nc, CHUNK, h, dk) * (dk ** -0.5)
    v = v.astype(f32).reshape(b, nc, CHUNK, h, dv)
    e_pos = jnp.exp(cum)
    e_neg = jnp.exp(-cum)
    q_pos = q * e_pos
    past = jnp.einsum('bnihd,bnjhd->bnhij', q_pos, k * e_neg)
    fut = jnp.einsum('bnihd,bnjhd->bnhij', q * e_neg, k * e_pos)
    idx = jnp.arange(CHUNK)
    attn = jnp.where(idx[:, None] >= idx[None, :], past, fut)
    intra = jnp.einsum('bnhij,bnjhe->bnihe', attn, v)
    last = cum[:, :, -1]
    u = jnp.einsum('bnjhd,bnjhe->bnhde', k * jnp.exp(last[:, :, None] - cum), v)
    s_prev = chunk_states(u, jnp.exp(last))
    inter = jnp.einsum('bnihd,bnhde->bnihe', q_pos, s_prev)
    return (intra + inter).reshape(b, s, h, dv)


def _fwd_setup_inputs(seed: int = 0) -> dict:
    key = jax.random.key(seed)
    ks = jax.random.split(key, 16)
    f32 = jnp.float32

    def normal(k, shape, scale):
        return jax.random.normal(k, shape, f32) * scale

    def gain(k, shape):
        return 1.0 + 0.02 * jax.random.normal(k, shape, f32)

    x = normal(ks[0], (BATCH, SEQ, D_MODEL), 1.0)
    c = normal(ks[1], (BATCH, D_MODEL), 1.0)
    offset = jax.random.randint(ks[2], (BATCH, 1), 0, 4096, dtype=jnp.int32)
    positions = offset + jnp.arange(SEQ, dtype=jnp.int32)[None, :]
    norm_w = gain(ks[3], (DEPTH, D_MODEL))
    ada_w = normal(ks[4], (DEPTH, D_MODEL, 3 * D_MODEL), 0.5 * D_MODEL ** -0.5)
    ada_b = normal(ks[5], (DEPTH, 3 * D_MODEL), 0.02)
    w_in = normal(ks[6], (DEPTH, D_MODEL, IN_COLS), D_MODEL ** -0.5)
    mla_q_norm = gain(ks[7], (DEPTH, MLA_Q_RANK))
    w_uq = normal(ks[8], (DEPTH, MLA_Q_RANK, MLA_HEADS * (MLA_NOPE + MLA_ROPE)), MLA_Q_RANK ** -0.5)
    mla_kv_norm = gain(ks[9], (DEPTH, MLA_KV_RANK))
    w_ukv = normal(ks[10], (DEPTH, MLA_KV_RANK, MLA_HEADS * (MLA_NOPE + MLA_V)), MLA_KV_RANK ** -0.5)
    gla_w_g2 = normal(ks[11], (DEPTH, GLA_GATE_RANK, GLA_KWIDTH), GLA_GATE_RANK ** -0.5)
    gla_b_g2 = normal(ks[12], (DEPTH, GLA_KWIDTH), 0.02)
    gla_norm = gain(ks[13], (DEPTH, GLA_DV))
    w_out = normal(ks[14], (DEPTH, MIX_WIDTH, D_MODEL), MIX_WIDTH ** -0.5)
    final_norm = gain(ks[15], (D_MODEL,))
    return {"x": x, "c": c, "positions": positions, "norm_w": norm_w, "ada_w": ada_w,
            "ada_b": ada_b, "w_in": w_in, "mla_q_norm": mla_q_norm, "w_uq": w_uq,
            "mla_kv_norm": mla_kv_norm, "w_ukv": w_ukv, "gla_w_g2": gla_w_g2,
            "gla_b_g2": gla_b_g2, "gla_norm": gla_norm, "w_out": w_out,
            "final_norm": final_norm}


def _fwd_reference(x, c, positions, norm_w, ada_w, ada_b, w_in, mla_q_norm, w_uq, mla_kv_norm,
              w_ukv, gla_w_g2, gla_b_g2, gla_norm, w_out, final_norm):
    b, s, _ = x.shape
    c_act = jax.nn.silu(c)
    for l in range(DEPTH):
        shift, scale, gate = jnp.split(c_act @ ada_w[l] + ada_b[l], 3, axis=-1)
        h = rms_norm(x, norm_w[l]) * (1.0 + scale[:, None, :]) + shift[:, None, :]
        proj = h @ w_in[l]
        ret_p, mla_p, gla_p = split_cols(proj, (sum(RET_SPLIT), sum(MLA_SPLIT), sum(GLA_SPLIT)))

        rq, rk, rv, rz = split_cols(ret_p, RET_SPLIT)
        r_o = retention(rope(to_heads(rq, RET_HEADS), positions),
                        rope(to_heads(rk, RET_HEADS), positions),
                        to_heads(rv, RET_HEADS))
        r_o = rms_norm(r_o).astype(x.dtype).reshape(b, s, RET_WIDTH)

        mq, mkv, mkr, mz = split_cols(mla_p, MLA_SPLIT)
        m_o = mla(mq, mkv, mkr, mla_q_norm[l], w_uq[l], mla_kv_norm[l], w_ukv[l], positions)

        gq, gk, gv, gg, gz = split_cols(gla_p, GLA_SPLIT)
        g_o = gla(to_heads(gq, GLA_HEADS), to_heads(gk, GLA_HEADS), to_heads(gv, GLA_HEADS),
                  gg, gla_w_g2[l], gla_b_g2[l])
        g_o = rms_norm(g_o, gla_norm[l]).astype(x.dtype).reshape(b, s, GLA_WIDTH)

        mixed = jnp.concatenate([r_o * jax.nn.silu(rz), m_o * jax.nn.silu(mz),
                                 g_o * jax.nn.silu(gz)], axis=-1)
        x = x + gate[:, None, :] * (mixed @ w_out[l])
    return rms_norm(x, final_norm)


import jax as _jax
import jax.numpy as _jnp

TWIN_FORMAT = 'train_step'
FWD_PARAMS = ['x', 'c', 'positions', 'norm_w', 'ada_w', 'ada_b', 'w_in', 'mla_q_norm', 'w_uq', 'mla_kv_norm', 'w_ukv', 'gla_w_g2', 'gla_b_g2', 'gla_norm', 'w_out', 'final_norm']
TWIN_WEIGHTS = ['norm_w', 'ada_w', 'ada_b', 'w_in', 'mla_q_norm', 'w_uq', 'mla_kv_norm', 'w_ukv', 'gla_w_g2', 'gla_b_g2', 'gla_norm', 'w_out', 'final_norm']
TWIN_DIFF_INPUT = 'x'
TWIN_INPUTS = ['x', 'c', 'positions', 'norm_w', 'ada_w', 'ada_b', 'w_in', 'mla_q_norm', 'w_uq', 'mla_kv_norm', 'w_ukv', 'gla_w_g2', 'gla_b_g2', 'gla_norm', 'w_out', 'final_norm', 'loss_target', 'm_norm_w', 'm_ada_w', 'm_ada_b', 'm_w_in', 'm_mla_q_norm', 'm_w_uq', 'm_mla_kv_norm', 'm_w_ukv', 'm_gla_w_g2', 'm_gla_b_g2', 'm_gla_norm', 'm_w_out', 'm_final_norm', 'v_norm_w', 'v_ada_w', 'v_ada_b', 'v_w_in', 'v_mla_q_norm', 'v_w_uq', 'v_mla_kv_norm', 'v_w_ukv', 'v_gla_w_g2', 'v_gla_b_g2', 'v_gla_norm', 'v_w_out', 'v_final_norm']
TWIN_OUTPUTS = ['loss', 'grad_x', 'grad_norm_w', 'grad_ada_w', 'grad_ada_b', 'grad_w_in', 'grad_mla_q_norm', 'grad_w_uq', 'grad_mla_kv_norm', 'grad_w_ukv', 'grad_gla_w_g2', 'grad_gla_b_g2', 'grad_gla_norm', 'grad_w_out', 'grad_final_norm', 'delta_norm_w', 'delta_ada_w', 'delta_ada_b', 'delta_w_in', 'delta_mla_q_norm', 'delta_w_uq', 'delta_mla_kv_norm', 'delta_w_ukv', 'delta_gla_w_g2', 'delta_gla_b_g2', 'delta_gla_norm', 'delta_w_out', 'delta_final_norm', 'new_m_norm_w', 'new_m_ada_w', 'new_m_ada_b', 'new_m_w_in', 'new_m_mla_q_norm', 'new_m_w_uq', 'new_m_mla_kv_norm', 'new_m_w_ukv', 'new_m_gla_w_g2', 'new_m_gla_b_g2', 'new_m_gla_norm', 'new_m_w_out', 'new_m_final_norm', 'new_v_norm_w', 'new_v_ada_w', 'new_v_ada_b', 'new_v_w_in', 'new_v_mla_q_norm', 'new_v_w_uq', 'new_v_mla_kv_norm', 'new_v_w_ukv', 'new_v_gla_w_g2', 'new_v_gla_b_g2', 'new_v_gla_norm', 'new_v_w_out', 'new_v_final_norm']
TWIN_LEAF_KINDS = {'loss': 'loss', 'grad_x': 'grad_x', 'grad_norm_w': 'grad_w', 'grad_ada_w': 'grad_w', 'grad_ada_b': 'grad_w', 'grad_w_in': 'grad_w', 'grad_mla_q_norm': 'grad_w', 'grad_w_uq': 'grad_w', 'grad_mla_kv_norm': 'grad_w', 'grad_w_ukv': 'grad_w', 'grad_gla_w_g2': 'grad_w', 'grad_gla_b_g2': 'grad_w', 'grad_gla_norm': 'grad_w', 'grad_w_out': 'grad_w', 'grad_final_norm': 'grad_w', 'delta_norm_w': 'delta_w', 'delta_ada_w': 'delta_w', 'delta_ada_b': 'delta_w', 'delta_w_in': 'delta_w', 'delta_mla_q_norm': 'delta_w', 'delta_w_uq': 'delta_w', 'delta_mla_kv_norm': 'delta_w', 'delta_w_ukv': 'delta_w', 'delta_gla_w_g2': 'delta_w', 'delta_gla_b_g2': 'delta_w', 'delta_gla_norm': 'delta_w', 'delta_w_out': 'delta_w', 'delta_final_norm': 'delta_w', 'new_m_norm_w': 'new_m', 'new_m_ada_w': 'new_m', 'new_m_ada_b': 'new_m', 'new_m_w_in': 'new_m', 'new_m_mla_q_norm': 'new_m', 'new_m_w_uq': 'new_m', 'new_m_mla_kv_norm': 'new_m', 'new_m_w_ukv': 'new_m', 'new_m_gla_w_g2': 'new_m', 'new_m_gla_b_g2': 'new_m', 'new_m_gla_norm': 'new_m', 'new_m_w_out': 'new_m', 'new_m_final_norm': 'new_m', 'new_v_norm_w': 'new_v', 'new_v_ada_w': 'new_v', 'new_v_ada_b': 'new_v', 'new_v_w_in': 'new_v', 'new_v_mla_q_norm': 'new_v', 'new_v_w_uq': 'new_v', 'new_v_mla_kv_norm': 'new_v', 'new_v_w_ukv': 'new_v', 'new_v_gla_w_g2': 'new_v', 'new_v_gla_b_g2': 'new_v', 'new_v_gla_norm': 'new_v', 'new_v_w_out': 'new_v', 'new_v_final_norm': 'new_v'}


def _forward(args):
    return _fwd_reference(*[args[k] for k in FWD_PARAMS])


def _output_shape():
    out = _jax.eval_shape(lambda: _forward(_fwd_setup_inputs(0)))
    return out.shape, out.dtype

N_MICROBATCH = 1
ADAM_LR = 0.001
ADAM_B1 = 0.9
ADAM_B2 = 0.999
ADAM_EPS = 1e-08
ADAM_WD = 0.01
ADAM_STEP = 10
PER_EXAMPLE_BATCH_AXIS = {'x': 0, 'c': 0, 'positions': 0, 'loss_target': 0}
SHARED_INPUTS = []
_WEIGHT_DTYPES = {'norm_w': _jnp.float32, 'ada_w': _jnp.float32, 'ada_b': _jnp.float32, 'w_in': _jnp.float32, 'mla_q_norm': _jnp.float32, 'w_uq': _jnp.float32, 'mla_kv_norm': _jnp.float32, 'w_ukv': _jnp.float32, 'gla_w_g2': _jnp.float32, 'gla_b_g2': _jnp.float32, 'gla_norm': _jnp.float32, 'w_out': _jnp.float32, 'final_norm': _jnp.float32}
MOMENT_SCALE = {'norm_w': 6.998697e-02, 'ada_w': 6.686440e-02, 'ada_b': 1.131555e-01, 'w_in': 4.695265e-02, 'mla_q_norm': 1.091723e-02, 'w_uq': 6.067362e-03, 'mla_kv_norm': 3.915149e-02, 'w_ukv': 1.230418e-02, 'gla_w_g2': 1.308393e-02, 'gla_b_g2': 3.390739e-02, 'gla_norm': 1.105460e-01, 'w_out': 3.731365e-02, 'final_norm': 6.386308e+01}


def _to_microbatches(a, axis):
    t = _jnp.moveaxis(a, axis, 0)
    t = t.reshape((N_MICROBATCH, t.shape[0] // N_MICROBATCH) + t.shape[1:])
    return _jnp.moveaxis(t, 1, axis + 1)


def setup_inputs(seed: int = 0) -> dict:
    inp = _fwd_setup_inputs(seed)
    key = _jax.random.fold_in(_jax.random.key(seed), 7919)
    shape, _ = _output_shape()
    out = dict(inp)
    out["loss_target"] = _jax.random.normal(_jax.random.fold_in(key, 0), shape, _jnp.float32)
    for i, name in enumerate(TWIN_WEIGHTS):
        w = inp[name].astype(_jnp.float32)
        if MOMENT_SCALE is None:
            s = _jnp.sqrt(_jnp.mean(_jnp.square(w)) + 1e-30)
        else:
            s = MOMENT_SCALE[name]
        km, kv = _jax.random.split(_jax.random.fold_in(key, i + 1))
        out[name] = w
        out["m_" + name] = s * _jax.random.normal(km, w.shape, _jnp.float32)
        out["v_" + name] = (s * s) * _jax.random.uniform(kv, w.shape, _jnp.float32, 0.5, 1.5)
    if N_MICROBATCH > 1:
        for name, axis in PER_EXAMPLE_BATCH_AXIS.items():
            out[name] = _to_microbatches(out[name], axis)
    return {'x': out['x'], 'c': out['c'], 'positions': out['positions'], 'norm_w': out['norm_w'], 'ada_w': out['ada_w'], 'ada_b': out['ada_b'], 'w_in': out['w_in'], 'mla_q_norm': out['mla_q_norm'], 'w_uq': out['w_uq'], 'mla_kv_norm': out['mla_kv_norm'], 'w_ukv': out['w_ukv'], 'gla_w_g2': out['gla_w_g2'], 'gla_b_g2': out['gla_b_g2'], 'gla_norm': out['gla_norm'], 'w_out': out['w_out'], 'final_norm': out['final_norm'], 'loss_target': out['loss_target'], 'm_norm_w': out['m_norm_w'], 'm_ada_w': out['m_ada_w'], 'm_ada_b': out['m_ada_b'], 'm_w_in': out['m_w_in'], 'm_mla_q_norm': out['m_mla_q_norm'], 'm_w_uq': out['m_w_uq'], 'm_mla_kv_norm': out['m_mla_kv_norm'], 'm_w_ukv': out['m_w_ukv'], 'm_gla_w_g2': out['m_gla_w_g2'], 'm_gla_b_g2': out['m_gla_b_g2'], 'm_gla_norm': out['m_gla_norm'], 'm_w_out': out['m_w_out'], 'm_final_norm': out['m_final_norm'], 'v_norm_w': out['v_norm_w'], 'v_ada_w': out['v_ada_w'], 'v_ada_b': out['v_ada_b'], 'v_w_in': out['v_w_in'], 'v_mla_q_norm': out['v_mla_q_norm'], 'v_w_uq': out['v_w_uq'], 'v_mla_kv_norm': out['v_mla_kv_norm'], 'v_w_ukv': out['v_w_ukv'], 'v_gla_w_g2': out['v_gla_w_g2'], 'v_gla_b_g2': out['v_gla_b_g2'], 'v_gla_norm': out['v_gla_norm'], 'v_w_out': out['v_w_out'], 'v_final_norm': out['v_final_norm']}


def _loss(weights, diff, rest, loss_target):
    with _jax.named_scope("forward"):
        args = {**rest, TWIN_DIFF_INPUT: diff, **{k: w.astype(_WEIGHT_DTYPES[k]) for k, w in weights.items()}}
        y = _forward(args)
    with _jax.named_scope("loss_head"):
        err = _jnp.square(y.astype(_jnp.float32) - loss_target)
        return 0.5 * _jnp.sum(_jnp.mean(err, axis=-1)) if err.ndim else 0.5 * err


def _adamw(w, g, m, v):
    m = ADAM_B1 * m + (1.0 - ADAM_B1) * g
    v = ADAM_B2 * v + (1.0 - ADAM_B2) * _jnp.square(g)
    m_hat = m / (1.0 - ADAM_B1 ** ADAM_STEP)
    v_hat = v / (1.0 - ADAM_B2 ** ADAM_STEP)
    delta = -ADAM_LR * (m_hat / (_jnp.sqrt(v_hat) + ADAM_EPS) + ADAM_WD * w)
    return delta, m, v


def reference(x, c, positions, norm_w, ada_w, ada_b, w_in, mla_q_norm, w_uq, mla_kv_norm, w_ukv, gla_w_g2, gla_b_g2, gla_norm, w_out, final_norm, loss_target, m_norm_w, m_ada_w, m_ada_b, m_w_in, m_mla_q_norm, m_w_uq, m_mla_kv_norm, m_w_ukv, m_gla_w_g2, m_gla_b_g2, m_gla_norm, m_w_out, m_final_norm, v_norm_w, v_ada_w, v_ada_b, v_w_in, v_mla_q_norm, v_w_uq, v_mla_kv_norm, v_w_ukv, v_gla_w_g2, v_gla_b_g2, v_gla_norm, v_w_out, v_final_norm):
    given = dict(x=x, c=c, positions=positions, norm_w=norm_w, ada_w=ada_w, ada_b=ada_b, w_in=w_in, mla_q_norm=mla_q_norm, w_uq=w_uq, mla_kv_norm=mla_kv_norm, w_ukv=w_ukv, gla_w_g2=gla_w_g2, gla_b_g2=gla_b_g2, gla_norm=gla_norm, w_out=w_out, final_norm=final_norm, loss_target=loss_target, m_norm_w=m_norm_w, m_ada_w=m_ada_w, m_ada_b=m_ada_b, m_w_in=m_w_in, m_mla_q_norm=m_mla_q_norm, m_w_uq=m_w_uq, m_mla_kv_norm=m_mla_kv_norm, m_w_ukv=m_w_ukv, m_gla_w_g2=m_gla_w_g2, m_gla_b_g2=m_gla_b_g2, m_gla_norm=m_gla_norm, m_w_out=m_w_out, m_final_norm=m_final_norm, v_norm_w=v_norm_w, v_ada_w=v_ada_w, v_ada_b=v_ada_b, v_w_in=v_w_in, v_mla_q_norm=v_mla_q_norm, v_w_uq=v_w_uq, v_mla_kv_norm=v_mla_kv_norm, v_w_ukv=v_w_ukv, v_gla_w_g2=v_gla_w_g2, v_gla_b_g2=v_gla_b_g2, v_gla_norm=v_gla_norm, v_w_out=v_w_out, v_final_norm=v_final_norm)
    weights = {n: given[n] for n in TWIN_WEIGHTS}
    shared = {n: given[n] for n in SHARED_INPUTS}
    per_example = {n: given[n] for n in ['x', 'c', 'positions']}
    grad_fn = _jax.value_and_grad(_loss, argnums=(0, 1))

    def one_microbatch(ex, loss_target):
        ex = dict(ex)
        diff = ex.pop(TWIN_DIFF_INPUT)
        return grad_fn(weights, diff, {**shared, **ex}, loss_target)

    if N_MICROBATCH == 1:
        loss, (grad_w, grad_x) = one_microbatch(per_example, given["loss_target"])
    else:
        def body(carry, xs):
            loss_sum, grad_sum = carry
            l_k, (gw_k, gx_k) = one_microbatch(xs[0], xs[1])
            with _jax.named_scope("update"):
                return (loss_sum + l_k, _jax.tree.map(_jnp.add, grad_sum, gw_k)), gx_k

        init = (_jnp.zeros((), _jnp.float32), _jax.tree.map(_jnp.zeros_like, weights))
        (loss, grad_w), grad_x = _jax.lax.scan(body, init, (per_example, given["loss_target"]))
    with _jax.named_scope("update"):
        delta_w, new_m, new_v = {}, {}, {}
        for n in TWIN_WEIGHTS:
            delta_w[n], new_m[n], new_v[n] = _adamw(weights[n], grad_w[n], given["m_" + n], given["v_" + n])
    return (loss, grad_x, *[grad_w[n] for n in TWIN_WEIGHTS], *[delta_w[n] for n in TWIN_WEIGHTS],
            *[new_m[n] for n in TWIN_WEIGHTS], *[new_v[n] for n in TWIN_WEIGHTS])
```

```python
import functools
import math

import numpy as np
import jax
import jax.numpy as jnp
from jax import lax
from jax.experimental import pallas as pl
from jax.experimental.pallas import tpu as pltpu

F32 = jnp.float32
_MXU = jnp.bfloat16

D_MODEL = 1024
DEPTH = 2
CHUNK = 64
EPS = 1e-6
ROPE_THETA = 10000.0
N_DEV = 8

MLA_SCALE = 96.0 ** -0.5
RET_KSCALE = 64.0 ** -0.5
GLA_KSCALE = 32.0 ** -0.5
GLA_TAU = 16.0

ADAM_LR = 0.001
ADAM_B1 = 0.9
ADAM_B2 = 0.999
ADAM_EPS = 1e-08
ADAM_WD = 0.01
ADAM_STEP = 10

RET_W, MLA_W, GLA_W = 1024, 1024, 896
ARR_W = RET_W + MLA_W + GLA_W
VMEM_MB = 1024 * 1024


def _cp(sem, vmem_mb=48):
    return pltpu.CompilerParams(dimension_semantics=sem, vmem_limit_bytes=vmem_mb * VMEM_MB)


def _mm(a, b):
    return jnp.dot(a.astype(_MXU), b.astype(_MXU), preferred_element_type=F32)


def _mm_nt(a, b):
    return lax.dot_general(a.astype(_MXU), b.astype(_MXU), (((1,), (1,)), ((), ())),
                           preferred_element_type=F32)


def _mm_tn(a, b):
    return lax.dot_general(a.astype(_MXU), b.astype(_MXU), (((0,), (0,)), ((), ())),
                           preferred_element_type=F32)


def _mm_f32(a, b):
    return jnp.dot(a, b, precision=lax.Precision.HIGHEST, preferred_element_type=F32)


def _sig(z):
    return 1.0 / (1.0 + jnp.exp(-z))


def _silu(z):
    return z * _sig(z)


def _dsilu(z):
    s = _sig(z)
    return s * (1.0 + z * (1.0 - s))


def _full(shape):
    nd = len(shape)
    return pl.BlockSpec(shape, lambda *_: (0,) * nd)


def _qk_perm(blk):
    r = blk.shape[0]
    return jnp.transpose(blk.reshape(r, 4, 2, 32), (0, 2, 1, 3)).reshape(r, 256)


def _qk_unperm(blk):
    r = blk.shape[0]
    return jnp.transpose(blk.reshape(r, 2, 4, 32), (0, 2, 1, 3)).reshape(r, 256)


def _arrange_w_in(w):
    z = lambda n: jnp.zeros((w.shape[0], n), w.dtype)
    ret = [_qk_perm(w[:, 0:256]), _qk_perm(w[:, 256:512]), w[:, 512:768], w[:, 768:1024]]
    mla = [w[:, 1024:1280], w[:, 1280:1408], z(64), w[:, 1408:1440], z(32), w[:, 1440:1952]]
    gla = [w[:, 1952:2080], w[:, 2080:2208], w[:, 2208:2464], w[:, 2464:2480], z(112), w[:, 2480:2736]]
    return jnp.concatenate(ret + mla + gla, axis=1)


def _unarrange_w_in(a):
    m, g = RET_W, RET_W + MLA_W
    parts = [_qk_unperm(a[:, 0:256]), _qk_unperm(a[:, 256:512]), a[:, 512:1024],
             a[:, m:m + 384], a[:, m + 448:m + 480], a[:, m + 512:m + 1024],
             a[:, g:g + 528], a[:, g + 640:g + 896]]
    return jnp.concatenate(parts, axis=1)


def _arrange_w_uq(w):
    return jnp.pad(w.reshape(256, 8, 96), ((0, 0), (0, 0), (0, 32))).reshape(256, 1024)


def _unarrange_w_uq(a):
    return a.reshape(256, 8, 128)[:, :, :96].reshape(256, 768)


def _arrange_w_ukv(w):
    r = w.reshape(128, 8, 128)
    k = jnp.pad(r[:, :, :64], ((0, 0), (0, 0), (0, 64))).reshape(128, 1024)
    return jnp.concatenate([k, r[:, :, 64:].reshape(128, 512)], axis=1)


def _unarrange_w_ukv(a):
    k = a[:, :1024].reshape(128, 8, 128)[:, :, :64]
    v = a[:, 1024:].reshape(128, 8, 64)
    return jnp.concatenate([k, v], axis=2).reshape(128, 1024)


def _rope_tables(pos3):
    B, S, _ = pos3.shape
    ts = min(S, 512)
    inv32 = (np.float32(ROPE_THETA) ** (-(np.arange(32, dtype=np.float32) / 32))).astype(np.float32)
    inv16 = (np.float32(ROPE_THETA) ** (-(np.arange(16, dtype=np.float32) / 16))).astype(np.float32)
    inv_r = np.tile(inv32, 4)[None, :]
    inv_m = np.zeros((1, 128), np.float32)
    inv_m[0, 64:80] = inv16
    inv_m[0, 80:96] = inv16

    def body(pos_ref, ir_ref, im_ref, cr, sr, cm, sm):
        p = pos_ref[0].astype(F32)
        ar = p * ir_ref[...]
        cr[0] = jnp.cos(ar)
        sr[0] = jnp.sin(ar)
        am = p * im_ref[...]
        cm[0] = jnp.cos(am)
        sm[0] = jnp.sin(am)

    tab = jax.ShapeDtypeStruct((B, S, 128), F32)
    blk = pl.BlockSpec((1, ts, 128), lambda b, i: (b, i, 0))
    return pl.pallas_call(
        body, name="rope_tables", grid=(B, S // ts),
        in_specs=[pl.BlockSpec((1, ts, 1), lambda b, i: (b, i, 0)), _full((1, 128)), _full((1, 128))],
        out_specs=[blk, blk, blk, blk], out_shape=[tab, tab, tab, tab],
        compiler_params=_cp(("parallel", "parallel")),
    )(pos3, jnp.asarray(inv_r), jnp.asarray(inv_m))


def _rope128(x, cos, sin):
    lane = lax.broadcasted_iota(jnp.int32, (1, 128), 1)
    rp = pltpu.roll(x, 16, 1)
    rm = pltpu.roll(x, 112, 1)
    return x * cos + jnp.where(lane < 80, -rm, rp) * sin


def _rope128_t(d, cos, sin):
    lane = lax.broadcasted_iota(jnp.int32, (1, 128), 1)
    y = d * sin
    yp = pltpu.roll(y, 16, 1)
    ym = pltpu.roll(y, 112, 1)
    return d * cos + jnp.where(lane < 64, 0.0, jnp.where(lane < 80, ym, jnp.where(lane < 96, -yp, 0.0)))


def _proj_fwd(x, shift, scale, nw, w_arr):
    B, S, D = x.shape
    tm = min(S, 512)

    def body(x_ref, sh_ref, sc_ref, nw_ref, w_ref, ret_ref, mla_ref, gla_ref, h_ref):
        xv = x_ref[0]
        rstd = lax.rsqrt(jnp.mean(xv * xv, axis=-1, keepdims=True) + EPS)
        h = (xv * rstd * nw_ref[...]) * (1.0 + sc_ref[0]) + sh_ref[0]
        hb = h.astype(_MXU)
        h_ref[0] = hb
        ret_ref[0] = jnp.dot(hb, w_ref[:, 0:RET_W], preferred_element_type=F32)
        mla_ref[0] = jnp.dot(hb, w_ref[:, RET_W:RET_W + MLA_W], preferred_element_type=F32)
        gla_ref[0] = jnp.dot(hb, w_ref[:, RET_W + MLA_W:ARR_W], preferred_element_type=F32)

    tok = lambda w: pl.BlockSpec((1, tm, w), lambda b, i: (b, i, 0))
    per_seq = pl.BlockSpec((1, 1, D), lambda b, i: (b, 0, 0))
    return pl.pallas_call(
        body, name="proj_fwd", grid=(B, S // tm),
        in_specs=[tok(D), per_seq, per_seq, _full((1, D)), _full((D, ARR_W))],
        out_specs=[tok(RET_W), tok(MLA_W), tok(GLA_W), tok(D)],
        out_shape=[jax.ShapeDtypeStruct((B, S, RET_W), F32), jax.ShapeDtypeStruct((B, S, MLA_W), F32),
                   jax.ShapeDtypeStruct((B, S, GLA_W), F32), jax.ShapeDtypeStruct((B, S, D), _MXU)],
        compiler_params=_cp(("parallel", "parallel")),
    )(x, shift, scale, nw, w_arr)


RET_L = 256


def _ret_consts(L):
    lg = np.log1p(-np.exp2(-5.0 - np.arange(4, dtype=np.float32))).astype(np.float32)
    i = np.arange(L)
    ci = i // CHUNK
    diff = (i[:, None] - i[None, :]).astype(np.float32)
    same = ci[:, None] == ci[None, :]
    past = ci[None, :] < ci[:, None]
    expo = np.where(same, np.abs(diff), np.where(past, diff, 0.0)).astype(np.float32)
    dec = np.where((same | past)[None], np.exp(lg[:, None, None] * expo[None]), 0.0).astype(np.float32)
    head = (np.arange(256) % 128) // 32
    qw = np.exp((i + 1.0)[:, None] * lg[head][None, :]).astype(np.float32)
    kw = np.exp((L - 1.0 - i)[:, None] * lg[head][None, :]).astype(np.float32)
    a_row = np.exp(np.float32(L) * lg[head])[None, :].astype(np.float32)
    return [jnp.asarray(t) for t in (dec.reshape(4 * L, L), qw, kw, a_row)]


def _ret_masks():
    lane = lax.broadcasted_iota(jnp.int32, (1, 256), 1)
    mh = [((lane % 128) // 32) == h for h in range(4)]
    mv = [(lane // 64) == h for h in range(4)]
    vi = lax.broadcasted_iota(jnp.int32, (256, 256), 0)
    ki = lax.broadcasted_iota(jnp.int32, (256, 256), 1)
    bd = (vi // 64) == ((ki % 128) // 32)
    return mh, mv, bd


def _ret_rope(p, cs, sn):
    q1, q2, k1, k2 = p[:, 0:128], p[:, 128:256], p[:, 256:384], p[:, 384:512]
    qr = jnp.concatenate([q1 * cs - q2 * sn, q2 * cs + q1 * sn], axis=1)
    kr = jnp.concatenate([k1 * cs - k2 * sn, k2 * cs + k1 * sn], axis=1) * RET_KSCALE
    return qr, kr


def _head_mean(x, mv, width):
    out = jnp.zeros_like(x)
    for m in mv:
        s = jnp.sum(jnp.where(m, x, 0.0), axis=-1, keepdims=True) * (1.0 / width)
        out = jnp.where(m, s, out)
    return out


def _stack_heads(x, masks):
    return jnp.concatenate([jnp.where(m, x, 0.0) for m in masks], axis=0)


def _fold_heads(xs, masks, L):
    out = jnp.where(masks[0], xs[0:L], 0.0)
    for h in range(1, 4):
        out = out + jnp.where(masks[h], xs[h * L:(h + 1) * L], 0.0)
    return out


def _ret_fwd(ret_p, cos, sin):
    B, S, _ = ret_p.shape
    L = min(RET_L, S)
    NB = S // L
    consts = _ret_consts(L)

    def body(p_ref, c_ref, s_ref, ds_ref, qw_ref, kw_ref, a_ref, out_ref, raw_ref, st_ref, st_sc):
        @pl.when(pl.program_id(1) == 0)
        def _():
            st_sc[...] = jnp.zeros_like(st_sc)

        mh, mv, bd = _ret_masks()
        p = p_ref[0]
        qr, kr = _ret_rope(p, c_ref[0], s_ref[0])
        v = p[:, 512:768]
        z = p[:, 768:1024]
        a_s = _mm_nt(_stack_heads(qr, mh), kr) * ds_ref[...]
        intra = _fold_heads(_mm(a_s, v), mv, L)
        st = st_sc[...]
        st_ref[0, 0] = st
        r = intra + _mm_nt(qr * qw_ref[...], st)
        raw_ref[0] = r
        st_sc[...] = st * a_ref[...] + jnp.where(bd, _mm_tn(v, kr * kw_ref[...]), 0.0)
        rstd = lax.rsqrt(_head_mean(r * r, mv, 64.0) + EPS)
        out_ref[0] = (r * rstd * _silu(z)).astype(_MXU)

    tok = lambda w: pl.BlockSpec((1, L, w), lambda b, n: (b, n, 0))
    return pl.pallas_call(
        body, name="ret_fwd", grid=(B, NB),
        in_specs=[tok(RET_W), tok(128), tok(128), _full((4 * L, L)), _full((L, 256)), _full((L, 256)),
                  _full((1, 256))],
        out_specs=[tok(256), tok(256), pl.BlockSpec((1, 1, 256, 256), lambda b, n: (b, n, 0, 0))],
        out_shape=[jax.ShapeDtypeStruct((B, S, 256), _MXU), jax.ShapeDtypeStruct((B, S, 256), F32),
                   jax.ShapeDtypeStruct((B, NB, 256, 256), F32)],
        scratch_shapes=[pltpu.VMEM((256, 256), F32)],
        compiler_params=_cp(("parallel", "arbitrary")),
    )(ret_p, cos, sin, *consts)


def _ret_bwd(ret_p, cos, sin, raw, states, d_mix):
    B, S, _ = ret_p.shape
    L = min(RET_L, S)
    NB = S // L
    consts = _ret_consts(L)

    def body(p_ref, c_ref, s_ref, raw_ref, st_ref, dm_ref, ds_ref, qw_ref, kw_ref, a_ref, dp_ref, dst_sc):
        @pl.when(pl.program_id(1) == 0)
        def _():
            dst_sc[...] = jnp.zeros_like(dst_sc)

        mh, mv, bd = _ret_masks()
        p = p_ref[0]
        cs, sn = c_ref[0], s_ref[0]
        qr, kr = _ret_rope(p, cs, sn)
        v = p[:, 512:768]
        z = p[:, 768:1024]
        qs = _stack_heads(qr, mh)
        dec = ds_ref[...]
        a_s = _mm_nt(qs, kr) * dec
        r = raw_ref[0]
        rstd = lax.rsqrt(_head_mean(r * r, mv, 64.0) + EPS)
        rn = r * rstd
        dm = dm_ref[0]
        d_rn = dm * _silu(z)
        dz = dm * rn * _dsilu(z)
        dr = rstd * (d_rn - rn * _head_mean(d_rn * rn, mv, 64.0))
        do_s = _stack_heads(dr, mv)
        da_s = _mm_nt(do_s, v) * dec
        dv = _mm_tn(a_s, do_s)
        dqr = _fold_heads(_mm(da_s, kr), mh, L)
        dkr = _mm_tn(da_s, qs)
        st = st_ref[0, 0]
        qw, kw = qw_ref[...], kw_ref[...]
        dqr = dqr + _mm(dr, st) * qw
        dst_next = dst_sc[...]
        g = jnp.where(bd, dst_next, 0.0)
        kk = kr * kw
        dv = dv + _mm_nt(kk, g)
        dkr = dkr + _mm(v, g) * kw
        dst_sc[...] = dst_next * a_ref[...] + jnp.where(bd, _mm_tn(dr, qr * qw), 0.0)
        dkr = dkr * RET_KSCALE
        dq1, dq2 = dqr[:, 0:128], dqr[:, 128:256]
        dk1, dk2 = dkr[:, 0:128], dkr[:, 128:256]
        dp_ref[0] = jnp.concatenate(
            [dq1 * cs + dq2 * sn, dq2 * cs - dq1 * sn, dk1 * cs + dk2 * sn, dk2 * cs - dk1 * sn, dv, dz], axis=1)

    tok = lambda w: pl.BlockSpec((1, L, w), lambda b, i: (b, NB - 1 - i, 0))
    return pl.pallas_call(
        body, name="ret_bwd", grid=(B, NB),
        in_specs=[tok(RET_W), tok(128), tok(128), tok(256),
                  pl.BlockSpec((1, 1, 256, 256), lambda b, i: (b, NB - 1 - i, 0, 0)), tok(256),
                  _full((4 * L, L)), _full((L, 256)), _full((L, 256)), _full((1, 256))],
        out_specs=tok(RET_W), out_shape=jax.ShapeDtypeStruct((B, S, RET_W), F32),
        scratch_shapes=[pltpu.VMEM((256, 256), F32)],
        compiler_params=_cp(("parallel", "arbitrary")),
    )(ret_p, cos, sin, raw, states, d_mix, *consts)


def _gla_masks():
    C = CHUNK
    lk = lax.broadcasted_iota(jnp.int32, (1, 128), 1)
    lv = lax.broadcasted_iota(jnp.int32, (1, 256), 1)
    mk = [(lk // 32) == h for h in range(4)]
    mv = [(lv // 64) == h for h in range(4)]
    vi = lax.broadcasted_iota(jnp.int32, (256, 128), 0)
    ki = lax.broadcasted_iota(jnp.int32, (256, 128), 1)
    bd = (vi // 64) == (ki // 32)
    ri = lax.broadcasted_iota(jnp.int32, (4 * C, C), 0) % C
    cj = lax.broadcasted_iota(jnp.int32, (4 * C, C), 1)
    lower = ri >= cj
    ti = lax.broadcasted_iota(jnp.int32, (C, C), 0)
    tj = lax.broadcasted_iota(jnp.int32, (C, C), 1)
    ltri = jnp.where(ti >= tj, 1.0, 0.0).astype(F32)
    utri = jnp.where(ti <= tj, 1.0, 0.0).astype(F32)
    return mk, mv, bd, lower, ltri, utri


def _gla_gate(p, w_ref, b_ref, ltri):
    pre = _mm(p[:, 512:640], w_ref[...]) + b_ref[...]
    la = (jnp.minimum(pre, 0.0) - jnp.log(1.0 + jnp.exp(-jnp.abs(pre)))) * (1.0 / GLA_TAU)
    cum = _mm_f32(ltri, la)
    return pre, cum


def _gla_fwd(gla_p, w_g2p, b_g2, gnw):
    B, S, _ = gla_p.shape
    C = CHUNK
    NC = S // C

    def body(p_ref, w_ref, b_ref, gn_ref, out_ref, raw_ref, st_ref, st_sc):
        @pl.when(pl.program_id(1) == 0)
        def _():
            st_sc[...] = jnp.zeros_like(st_sc)

        mk, mv, bd, lower, ltri, _ = _gla_masks()
        p = p_ref[0]
        q = p[:, 0:128]
        k = p[:, 128:256] * GLA_KSCALE
        v = p[:, 256:512]
        z = p[:, 640:896]
        _, cum = _gla_gate(p, w_ref, b_ref, ltri)
        last = cum[C - 1:C, :]
        e_pos = jnp.exp(cum)
        e_neg = jnp.exp(-cum)
        q_pos = q * e_pos
        past = _mm_nt(_stack_heads(q_pos, mk), k * e_neg)
        fut = _mm_nt(_stack_heads(q * e_neg, mk), k * e_pos)
        attn = jnp.where(lower, past, fut)
        intra = _fold_heads(_mm(attn, v), mv, C)
        st = st_sc[...]
        st_ref[0, 0] = st
        g = intra + _mm_nt(q_pos, st)
        raw_ref[0] = g
        kd = k * jnp.exp(last - cum)
        st_sc[...] = st * jnp.exp(last) + jnp.where(bd, _mm_tn(v, kd), 0.0)
        rstd = lax.rsqrt(_head_mean(g * g, mv, 64.0) + EPS)
        out_ref[0] = (g * rstd * gn_ref[...] * _silu(z)).astype(_MXU)

    tok = lambda w: pl.BlockSpec((1, C, w), lambda b, n: (b, n, 0))
    return pl.pallas_call(
        body, name="gla_fwd", grid=(B, NC),
        in_specs=[tok(GLA_W), _full((128, 128)), _full((1, 128)), _full((1, 256))],
        out_specs=[tok(256), tok(256), pl.BlockSpec((1, 1, 256, 128), lambda b, n: (b, n, 0, 0))],
        out_shape=[jax.ShapeDtypeStruct((B, S, 256), _MXU), jax.ShapeDtypeStruct((B, S, 256), F32),
                   jax.ShapeDtypeStruct((B, NC, 256, 128), F32)],
        scratch_shapes=[pltpu.VMEM((256, 128), F32)],
        compiler_params=_cp(("parallel", "arbitrary")),
    )(gla_p, w_g2p, b_g2, gnw)


def _gla_bwd(gla_p, w_g2p, b_g2, gnw, raw, states, d_mix):
    B, S, _ = gla_p.shape
    C = CHUNK
    NC = S // C

    def body(p_ref, w_ref, b_ref, gn_ref, raw_ref, st_ref, dm_ref, dp_ref, dw_ref, db_ref, dgn_ref, dst_sc):
        first = (pl.program_id(0) == 0) & (pl.program_id(1) == 0)

        @pl.when(first)
        def _():
            dw_ref[...] = jnp.zeros_like(dw_ref)
            db_ref[...] = jnp.zeros_like(db_ref)
            dgn_ref[...] = jnp.zeros_like(dgn_ref)

        @pl.when(pl.program_id(1) == 0)
        def _():
            dst_sc[...] = jnp.zeros_like(dst_sc)

        mk, mv, bd, lower, ltri, utri = _gla_masks()
        p = p_ref[0]
        q = p[:, 0:128]
        k = p[:, 128:256] * GLA_KSCALE
        v = p[:, 256:512]
        gg = p[:, 512:640]
        z = p[:, 640:896]
        pre, cum = _gla_gate(p, w_ref, b_ref, ltri)
        last = cum[C - 1:C, :]
        e_pos = jnp.exp(cum)
        e_neg = jnp.exp(-cum)
        q_pos, q_neg = q * e_pos, q * e_neg
        k_pos, k_neg = k * e_pos, k * e_neg
        qp_s = _stack_heads(q_pos, mk)
        qn_s = _stack_heads(q_neg, mk)
        attn = jnp.where(lower, _mm_nt(qp_s, k_neg), _mm_nt(qn_s, k_pos))
        g = raw_ref[0]
        rstd = lax.rsqrt(_head_mean(g * g, mv, 64.0) + EPS)
        gh = g * rstd
        gn = gn_ref[...]
        dm = dm_ref[0]
        d_gn = dm * _silu(z)
        dz = dm * gh * gn * _dsilu(z)
        dgn_ref[...] += jnp.sum(d_gn * gh, axis=0, keepdims=True)
        d_gh = d_gn * gn
        dg = rstd * (d_gh - gh * _head_mean(d_gh * gh, mv, 64.0))
        do_s = _stack_heads(dg, mv)
        dattn = _mm_nt(do_s, v)
        dv = _mm_tn(attn, do_s)
        dpast = jnp.where(lower, dattn, 0.0)
        dfut = jnp.where(lower, 0.0, dattn)
        dq_pos = _fold_heads(_mm(dpast, k_neg), mk, C)
        dk_neg = _mm_tn(dpast, qp_s)
        dq_neg = _fold_heads(_mm(dfut, k_pos), mk, C)
        dk_pos = _mm_tn(dfut, qn_s)
        st = st_ref[0, 0]
        dq_pos = dq_pos + _mm(dg, st)
        dst_next = dst_sc[...]
        a_row = jnp.exp(last)
        d_a = jnp.sum(dst_next * st, axis=0, keepdims=True)
        gmat = jnp.where(bd, dst_next, 0.0)
        w_dec = jnp.exp(last - cum)
        kd = k * w_dec
        d_kd = _mm(v, gmat)
        dv = dv + _mm_nt(kd, gmat)
        dst_sc[...] = dst_next * a_row + jnp.where(bd, _mm_tn(dg, q_pos), 0.0)
        t = d_kd * kd
        dk = d_kd * w_dec + dk_neg * e_neg + dk_pos * e_pos
        dq = dq_pos * e_pos + dq_neg * e_neg
        d_last = jnp.sum(t, axis=0, keepdims=True) + d_a * a_row
        d_cum = dq_pos * q_pos - dk_neg * k_neg - dq_neg * q_neg + dk_pos * k_pos - t
        row = lax.broadcasted_iota(jnp.int32, (C, 128), 0)
        d_cum = d_cum + jnp.where(row == C - 1, d_last, 0.0)
        d_la = _mm_f32(utri, d_cum)
        d_pre = d_la * _sig(-pre) * (1.0 / GLA_TAU)
        d_gg = _mm_nt(d_pre, w_ref[...])
        dw_ref[...] += _mm_tn(gg, d_pre)
        db_ref[...] += jnp.sum(d_pre, axis=0, keepdims=True)
        dp_ref[0] = jnp.concatenate([dq, dk * GLA_KSCALE, dv, d_gg, dz], axis=1)

        @pl.when((pl.program_id(0) == B - 1) & (pl.program_id(1) == NC - 1))
        def _():
            s1 = dgn_ref[...]
            s1 = s1 + pltpu.roll(s1, 128, 1)
            dgn_ref[...] = s1 + pltpu.roll(s1, 64, 1)

    tok = lambda w: pl.BlockSpec((1, C, w), lambda b, i: (b, NC - 1 - i, 0))
    return pl.pallas_call(
        body, name="gla_bwd", grid=(B, NC),
        in_specs=[tok(GLA_W), _full((128, 128)), _full((1, 128)), _full((1, 256)), tok(256),
                  pl.BlockSpec((1, 1, 256, 128), lambda b, i: (b, NC - 1 - i, 0, 0)), tok(256)],
        out_specs=[tok(GLA_W), _full((128, 128)), _full((1, 128)), _full((1, 256))],
        out_shape=[jax.ShapeDtypeStruct((B, S, GLA_W), F32), jax.ShapeDtypeStruct((128, 128), F32),
                   jax.ShapeDtypeStruct((1, 128), F32), jax.ShapeDtypeStruct((1, 256), F32)],
        scratch_shapes=[pltpu.VMEM((256, 128), F32)],
        compiler_params=_cp(("arbitrary", "arbitrary")),
    )(gla_p, w_g2p, b_g2, gnw, raw, states, d_mix)


def _rms(x, w):
    rstd = lax.rsqrt(jnp.mean(x * x, axis=-1, keepdims=True) + EPS)
    xh = x * rstd
    return xh, rstd, xh * w


def _rms_bwd(dy, xh, rstd, w):
    dxh = dy * w
    return rstd * (dxh - xh * jnp.mean(dxh * xh, axis=-1, keepdims=True))


def _mla_prep_fwd(mla_p, cos, sin, qnw, kvnw, w_uq, w_ukv):
    B, S, _ = mla_p.shape
    tm = min(S, 512)

    def body(p_ref, c_ref, s_ref, qn_ref, kn_ref, wq_ref, wkv_ref, q_ref, k_ref, v_ref):
        p = p_ref[0]
        cs, sn = c_ref[0], s_ref[0]
        _, _, qn = _rms(p[:, 0:256], qn_ref[...])
        qpre = _mm(qn, wq_ref[...])
        _, _, kvn = _rms(p[:, 256:384], kn_ref[...])
        kv = _mm(kvn, wkv_ref[...])
        kpe = _rope128(p[:, 384:512], cs, sn)
        for h in range(8):
            sl = slice(128 * h, 128 * h + 128)
            q_ref[0, :, sl] = _rope128(qpre[:, sl], cs, sn).astype(_MXU)
            k_ref[0, :, sl] = (kv[:, sl] + kpe).astype(_MXU)
        v_ref[0] = kv[:, 1024:1536].astype(_MXU)

    tok = lambda w: pl.BlockSpec((1, tm, w), lambda b, i: (b, i, 0))
    return pl.pallas_call(
        body, name="mla_prep_fwd", grid=(B, S // tm),
        in_specs=[tok(512), tok(128), tok(128), _full((1, 256)), _full((1, 128)), _full((256, 1024)),
                  _full((128, 1536))],
        out_specs=[tok(1024), tok(1024), tok(512)],
        out_shape=[jax.ShapeDtypeStruct((B, S, 1024), _MXU), jax.ShapeDtypeStruct((B, S, 1024), _MXU),
                   jax.ShapeDtypeStruct((B, S, 512), _MXU)],
        compiler_params=_cp(("parallel", "parallel")),
    )(mla_p, cos, sin, qnw, kvnw, w_uq, w_ukv)


MLA_T = 256


def _chunk_mask(t):
    qi = lax.broadcasted_iota(jnp.int32, (t, t), 0) // CHUNK
    kj = lax.broadcasted_iota(jnp.int32, (t, t), 1) // CHUNK
    return kj <= qi


def _mla_attn_fwd(q, k, v):
    B, S, _ = q.shape
    t = min(MLA_T, S)
    nq = S // t

    def body(q_ref, k_ref, v_ref, o_ref, lse_ref):
        i = pl.program_id(2)
        lane = lax.broadcasted_iota(jnp.int32, (1, 128), 1)
        low = lane < 64
        qb = q_ref[0]
        mask = _chunk_mask(t)

        def step(j, carry, masked):
            m_e, l_e, m_o, l_o, acc = carry
            kb = k_ref[0, pl.ds(pl.multiple_of(j * t, t), t), :]
            vb = v_ref[0, pl.ds(pl.multiple_of(j * t, t), t), :]
            new = []
            pv = []
            for hh, (m_h, l_h) in enumerate(((m_e, l_e), (m_o, l_o))):
                s = _mm_nt(qb[:, 128 * hh:128 * hh + 128], kb[:, 128 * hh:128 * hh + 128]) * MLA_SCALE
                if masked:
                    s = jnp.where(mask, s, -jnp.inf)
                m_n = jnp.maximum(m_h, jnp.max(s, axis=-1, keepdims=True))
                alpha = jnp.exp(m_h - m_n)
                pr = jnp.exp(s - m_n)
                l_n = alpha * l_h + jnp.sum(pr, axis=-1, keepdims=True)
                vh = jnp.where(low if hh == 0 else ~low, vb, jnp.zeros_like(vb))
                pv.append(_mm(pr, vh))
                new.append((m_n, l_n, alpha))
            acc = acc * jnp.where(low, new[0][2], new[1][2]) + pv[0] + pv[1]
            return new[0][0], new[0][1], new[1][0], new[1][1], acc

        neg = jnp.full((t, 1), -jnp.inf, F32)
        zero = jnp.zeros((t, 1), F32)
        carry = (neg, zero, neg, zero, jnp.zeros((t, 128), F32))
        carry = step(i, carry, True)
        carry = lax.fori_loop(0, i, lambda j, c: step(j, c, False), carry)
        m_e, l_e, m_o, l_o, acc = carry
        o_ref[0] = acc / jnp.where(low, l_e, l_o)
        lse_ref[0] = jnp.where(low, m_e + jnp.log(l_e), m_o + jnp.log(l_o))

    return pl.pallas_call(
        body, name="mla_attn_fwd", grid=(B, 4, nq),
        in_specs=[pl.BlockSpec((1, t, 256), lambda b, p, i: (b, i, p)),
                  pl.BlockSpec((1, S, 256), lambda b, p, i: (b, 0, p)),
                  pl.BlockSpec((1, S, 128), lambda b, p, i: (b, 0, p))],
        out_specs=[pl.BlockSpec((1, t, 128), lambda b, p, i: (b, i, p)),
                   pl.BlockSpec((1, t, 128), lambda b, p, i: (b, i, p))],
        out_shape=[jax.ShapeDtypeStruct((B, S, 512), F32), jax.ShapeDtypeStruct((B, S, 512), F32)],
        compiler_params=_cp(("parallel", "parallel", "arbitrary")),
    )(q, k, v)


def _mla_gate_bwd(d_mix, o, mla_p):
    B, S, _ = o.shape
    tm = min(S, 512)

    def body(dm_ref, o_ref, z_ref, do_ref, dz_ref, dl_ref):
        lane = lax.broadcasted_iota(jnp.int32, (1, 128), 1)
        low = lane < 64
        dm, ov, z = dm_ref[0], o_ref[0], z_ref[0]
        do = dm * _silu(z)
        dz_ref[0] = dm * ov * _dsilu(z)
        do_ref[0] = do.astype(_MXU)
        prod = do * ov
        for pr in range(4):
            blk = prod[:, 128 * pr:128 * pr + 128]
            se = jnp.sum(jnp.where(low, blk, 0.0), axis=-1, keepdims=True)
            so = jnp.sum(jnp.where(low, 0.0, blk), axis=-1, keepdims=True)
            dl_ref[0, :, 128 * pr:128 * pr + 128] = jnp.where(low, se, so)

    tok = lambda c: pl.BlockSpec((1, tm, 512), lambda b, i: (b, i, c))
    return pl.pallas_call(
        body, name="mla_gate_bwd", grid=(B, S // tm),
        in_specs=[tok(0), tok(0), tok(1)],
        out_specs=[tok(0), tok(0), tok(0)],
        out_shape=[jax.ShapeDtypeStruct((B, S, 512), _MXU), jax.ShapeDtypeStruct((B, S, 512), F32),
                   jax.ShapeDtypeStruct((B, S, 512), F32)],
        compiler_params=_cp(("parallel", "parallel")),
    )(d_mix, o, mla_p)


def _mla_attn_bwd(q, k, v, do, lse, dl):
    B, S, _ = q.shape
    t = min(MLA_T, S)
    nk = S // t

    def body(q_ref, k_ref, v_ref, do_ref, lse_ref, dl_ref, dq_ref, dk_ref, dv_ref):
        j = pl.program_id(2)

        @pl.when(j == 0)
        def _():
            dq_ref[...] = jnp.zeros_like(dq_ref)

        lane = lax.broadcasted_iota(jnp.int32, (1, 128), 1)
        low = lane < 64
        kb = k_ref[0]
        vb = v_ref[0]
        mask = _chunk_mask(t)
        vhs = [jnp.where(low, vb, jnp.zeros_like(vb)), jnp.where(low, jnp.zeros_like(vb), vb)]

        def step(i, carry, masked):
            dk_e, dk_o, dv = carry
            rows = pl.ds(pl.multiple_of(i * t, t), t)
            qb = q_ref[0, rows, :]
            dob = do_ref[0, rows, :]
            lseb = lse_ref[0, rows, :]
            dlb = dl_ref[0, rows, :]
            dks = []
            for hh, dk_h in enumerate((dk_e, dk_o)):
                cols = slice(128 * hh, 128 * hh + 128)
                qh = qb[:, cols]
                s = _mm_nt(qh, kb[:, cols]) * MLA_SCALE
                pr = jnp.exp(s - lseb[:, 64 * hh:64 * hh + 1])
                if masked:
                    pr = jnp.where(mask, pr, 0.0)
                doh = jnp.where(low if hh == 0 else ~low, dob, jnp.zeros_like(dob))
                dv = dv + _mm_tn(pr, doh)
                dp = _mm_nt(dob, vhs[hh])
                ds = pr * (dp - dlb[:, 64 * hh:64 * hh + 1]) * MLA_SCALE
                dq_ref[0, rows, cols] += _mm(ds, kb[:, cols])
                dks.append(dk_h + _mm_tn(ds, qh))
            return dks[0], dks[1], dv

        carry = (jnp.zeros((t, 128), F32), jnp.zeros((t, 128), F32), jnp.zeros((t, 128), F32))
        carry = step(j, carry, True)
        carry = lax.fori_loop(j + 1, nk, lambda i, c: step(i, c, False), carry)
        dk_ref[0, :, 0:128] = carry[0]
        dk_ref[0, :, 128:256] = carry[1]
        dv_ref[0] = carry[2]

    seq = lambda w: pl.BlockSpec((1, S, w), lambda b, p, j: (b, 0, p))
    blk = lambda w: pl.BlockSpec((1, t, w), lambda b, p, j: (b, j, p))
    return pl.pallas_call(
        body, name="mla_attn_bwd", grid=(B, 4, nk),
        in_specs=[seq(256), blk(256), blk(128), seq(128), seq(128), seq(128)],
        out_specs=[seq(256), blk(256), blk(128)],
        out_shape=[jax.ShapeDtypeStruct((B, S, 1024), F32), jax.ShapeDtypeStruct((B, S, 1024), F32),
                   jax.ShapeDtypeStruct((B, S, 512), F32)],
        compiler_params=_cp(("parallel", "parallel", "arbitrary")),
    )(q, k, v, do, lse, dl)


def _mla_prep_bwd(mla_p, cos, sin, qnw, kvnw, w_uq, w_ukv, dq, dk, dv):
    B, S, _ = mla_p.shape
    tm = min(S, 512)

    def body(p_ref, c_ref, s_ref, qn_ref, kn_ref, wq_ref, wkv_ref, dq_ref, dk_ref, dv_ref,
             dp_ref, dwq_ref, dwkv_ref, dqn_ref, dkn_ref):
        first = (pl.program_id(0) == 0) & (pl.program_id(1) == 0)

        @pl.when(first)
        def _():
            dwq_ref[...] = jnp.zeros_like(dwq_ref)
            dwkv_ref[...] = jnp.zeros_like(dwkv_ref)
            dqn_ref[...] = jnp.zeros_like(dqn_ref)
            dkn_ref[...] = jnp.zeros_like(dkn_ref)

        p = p_ref[0]
        cs, sn = c_ref[0], s_ref[0]
        lane = lax.broadcasted_iota(jnp.int32, (1, 128), 1)
        pe = (lane >= 64) & (lane < 96)
        qh, q_rstd, qn = _rms(p[:, 0:256], qn_ref[...])
        kvh, kv_rstd, kvn = _rms(p[:, 256:384], kn_ref[...])
        dqv = dq_ref[0]
        dkv = dk_ref[0]
        dqpre = jnp.concatenate(
            [_rope128_t(dqv[:, 128 * h:128 * h + 128], cs, sn) for h in range(8)], axis=1)
        dkpe = jnp.zeros((tm, 128), F32)
        for h in range(8):
            dkpe = dkpe + jnp.where(pe, dkv[:, 128 * h:128 * h + 128], 0.0)
        dkr = _rope128_t(dkpe, cs, sn)
        dkv_all = jnp.concatenate([dkv, dv_ref[0]], axis=1)
        d_qn = _mm_nt(dqpre, wq_ref[...])
        d_kvn = _mm_nt(dkv_all, wkv_ref[...])
        dwq_ref[...] += _mm_tn(qn, dqpre)
        dwkv_ref[...] += _mm_tn(kvn, dkv_all)
        dqn_ref[...] += jnp.sum(d_qn * qh, axis=0, keepdims=True)
        dkn_ref[...] += jnp.sum(d_kvn * kvh, axis=0, keepdims=True)
        dp_ref[0] = jnp.concatenate([_rms_bwd(d_qn, qh, q_rstd, qn_ref[...]),
                                     _rms_bwd(d_kvn, kvh, kv_rstd, kn_ref[...]), dkr], axis=1)

    tok = lambda w: pl.BlockSpec((1, tm, w), lambda b, i: (b, i, 0))
    return pl.pallas_call(
        body, name="mla_prep_bwd", grid=(B, S // tm),
        in_specs=[tok(512), tok(128), tok(128), _full((1, 256)), _full((1, 128)), _full((256, 1024)),
                  _full((128, 1536)), tok(1024), tok(1024), tok(512)],
        out_specs=[tok(512), _full((256, 1024)), _full((128, 1536)), _full((1, 256)), _full((1, 128))],
        out_shape=[jax.ShapeDtypeStruct((B, S, 512), F32), jax.ShapeDtypeStruct((256, 1024), F32),
                   jax.ShapeDtypeStruct((128, 1536), F32), jax.ShapeDtypeStruct((1, 256), F32),
                   jax.ShapeDtypeStruct((1, 128), F32)],
        compiler_params=_cp(("arbitrary", "arbitrary")),
    )(mla_p, cos, sin, qnw, kvnw, w_uq, w_ukv, dq, dk, dv)


def _out_fwd(x, gate, r_g, o_mla, mla_p, g_g, w_out):
    B, S, D = x.shape
    tm = min(S, 512)

    def body(x_ref, g_ref, r_ref, o_ref, z_ref, gg_ref, w_ref, xn_ref, y_ref, mm_ref):
        mm = (o_ref[0] * _silu(z_ref[0])).astype(_MXU)
        mm_ref[0] = mm
        y = (jnp.dot(r_ref[0], w_ref[0:256, :], preferred_element_type=F32)
             + jnp.dot(mm, w_ref[256:768, :], preferred_element_type=F32)
             + jnp.dot(gg_ref[0], w_ref[768:1024, :], preferred_element_type=F32))
        y_ref[0] = y
        xn_ref[0] = x_ref[0] + g_ref[0] * y

    tok = lambda w, c=0: pl.BlockSpec((1, tm, w), lambda b, i: (b, i, c))
    return pl.pallas_call(
        body, name="out_fwd", grid=(B, S // tm),
        in_specs=[tok(D), pl.BlockSpec((1, 1, D), lambda b, i: (b, 0, 0)), tok(256), tok(512), tok(512, 1),
                  tok(256), _full((D, D))],
        out_specs=[tok(D), tok(D), tok(512)],
        out_shape=[jax.ShapeDtypeStruct((B, S, D), F32), jax.ShapeDtypeStruct((B, S, D), F32),
                   jax.ShapeDtypeStruct((B, S, 512), _MXU)],
        compiler_params=_cp(("parallel", "parallel")),
    )(x, gate, r_g, o_mla, mla_p, g_g, w_out)


def _out_bwd(dx, y, gate, r_g, mm, g_g, w_out):
    B, S, D = dx.shape
    tm = min(S, 512)

    def body(dx_ref, y_ref, g_ref, r_ref, mm_ref, gg_ref, w_ref, dr_ref, dmm_ref, dg_ref, dw_ref, dgate_ref):
        first = (pl.program_id(0) == 0) & (pl.program_id(1) == 0)

        @pl.when(first)
        def _():
            dw_ref[...] = jnp.zeros_like(dw_ref)

        @pl.when(pl.program_id(1) == 0)
        def _():
            dgate_ref[...] = jnp.zeros_like(dgate_ref)

        dxv = dx_ref[0]
        dgate_ref[0] += jnp.sum(dxv * y_ref[0], axis=0, keepdims=True)
        dy = (dxv * g_ref[0]).astype(_MXU)
        dr_ref[0] = _mm_nt(dy, w_ref[0:256, :])
        dmm_ref[0] = _mm_nt(dy, w_ref[256:768, :])
        dg_ref[0] = _mm_nt(dy, w_ref[768:1024, :])
        dw_ref[0:256, :] += _mm_tn(r_ref[0], dy)
        dw_ref[256:768, :] += _mm_tn(mm_ref[0], dy)
        dw_ref[768:1024, :] += _mm_tn(gg_ref[0], dy)

    tok = lambda w: pl.BlockSpec((1, tm, w), lambda b, i: (b, i, 0))
    per_seq = pl.BlockSpec((1, 1, D), lambda b, i: (b, 0, 0))
    return pl.pallas_call(
        body, name="out_bwd", grid=(B, S // tm),
        in_specs=[tok(D), tok(D), per_seq, tok(256), tok(512), tok(256), _full((D, D))],
        out_specs=[tok(256), tok(512), tok(256), _full((D, D)), per_seq],
        out_shape=[jax.ShapeDtypeStruct((B, S, 256), F32), jax.ShapeDtypeStruct((B, S, 512), F32),
                   jax.ShapeDtypeStruct((B, S, 256), F32), jax.ShapeDtypeStruct((D, D), F32),
                   jax.ShapeDtypeStruct((B, 1, D), F32)],
        compiler_params=_cp(("arbitrary", "arbitrary")),
    )(dx, y, gate, r_g, mm, g_g, w_out)


def _proj_bwd_x(x, shift, scale, nw, w_arr, d_ret, d_mla, d_mz, d_gla, dx_out):
    B, S, D = x.shape
    tm = min(S, 256)

    def body(x_ref, sc_ref, nw_ref, w_ref, dr_ref, dm_ref, dz_ref, dg_ref, dxo_ref,
             dx_ref, dp_ref, dsh_ref, dsc_ref, dnw_ref):
        first = (pl.program_id(0) == 0) & (pl.program_id(1) == 0)

        @pl.when(first)
        def _():
            dnw_ref[...] = jnp.zeros_like(dnw_ref)

        @pl.when(pl.program_id(1) == 0)
        def _():
            dsh_ref[...] = jnp.zeros_like(dsh_ref)
            dsc_ref[...] = jnp.zeros_like(dsc_ref)

        dp = jnp.concatenate([dr_ref[0], dm_ref[0], dz_ref[0], dg_ref[0]], axis=1).astype(_MXU)
        dp_ref[0] = dp
        dh = lax.dot_general(dp, w_ref[...], (((1,), (1,)), ((), ())), preferred_element_type=F32)
        xv = x_ref[0]
        rstd = lax.rsqrt(jnp.mean(xv * xv, axis=-1, keepdims=True) + EPS)
        xh = xv * rstd
        nwv = nw_ref[...]
        mod = 1.0 + sc_ref[0]
        dsh_ref[0] += jnp.sum(dh, axis=0, keepdims=True)
        dsc_ref[0] += jnp.sum(dh * xh * nwv, axis=0, keepdims=True)
        dnw_ref[...] += jnp.sum(dh * xh * mod, axis=0, keepdims=True)
        dxh = dh * nwv * mod
        dx_ref[0] = dxo_ref[0] + rstd * (dxh - xh * jnp.mean(dxh * xh, axis=-1, keepdims=True))

    tok = lambda w: pl.BlockSpec((1, tm, w), lambda b, i: (b, i, 0))
    per_seq = pl.BlockSpec((1, 1, D), lambda b, i: (b, 0, 0))
    return pl.pallas_call(
        body, name="proj_bwd_x", grid=(B, S // tm),
        in_specs=[tok(D), per_seq, _full((1, D)), _full((D, ARR_W)), tok(RET_W), tok(512), tok(512),
                  tok(GLA_W), tok(D)],
        out_specs=[tok(D), tok(ARR_W), per_seq, per_seq, _full((1, D))],
        out_shape=[jax.ShapeDtypeStruct((B, S, D), F32), jax.ShapeDtypeStruct((B, S, ARR_W), _MXU),
                   jax.ShapeDtypeStruct((B, 1, D), F32), jax.ShapeDtypeStruct((B, 1, D), F32),
                   jax.ShapeDtypeStruct((1, D), F32)],
        compiler_params=_cp(("arbitrary", "arbitrary")),
    )(x, scale, nw, w_arr, d_ret, d_mla, d_mz, d_gla, dx_out)


def _proj_bwd_w(h, dp):
    B, S, D = h.shape
    tm = min(S, 512)
    tn = 128 * 23 // 1
    assert ARR_W == tn

    def body(h_ref, dp_ref, dw_ref):
        first = (pl.program_id(0) == 0) & (pl.program_id(1) == 0)

        @pl.when(first)
        def _():
            dw_ref[...] = jnp.zeros_like(dw_ref)

        dw_ref[...] += lax.dot_general(h_ref[0], dp_ref[0], (((0,), (0,)), ((), ())),
                                       preferred_element_type=F32)

    tok = lambda w: pl.BlockSpec((1, tm, w), lambda b, i: (b, i, 0))
    return pl.pallas_call(
        body, name="proj_bwd_w", grid=(B, S // tm),
        in_specs=[tok(D), tok(ARR_W)],
        out_specs=_full((D, ARR_W)), out_shape=jax.ShapeDtypeStruct((D, ARR_W), F32),
        compiler_params=_cp(("arbitrary", "arbitrary"), 56),
    )(h, dp)


def _final_loss(x, fw, target):
    B, S, D = x.shape
    tm = min(S, 512)

    def body(x_ref, fw_ref, t_ref, dx_ref, loss_ref, dfw_ref):
        first = (pl.program_id(0) == 0) & (pl.program_id(1) == 0)

        @pl.when(first)
        def _():
            loss_ref[...] = jnp.zeros_like(loss_ref)
            dfw_ref[...] = jnp.zeros_like(dfw_ref)

        xv = x_ref[0]
        fwv = fw_ref[...]
        rstd = lax.rsqrt(jnp.mean(xv * xv, axis=-1, keepdims=True) + EPS)
        xh = xv * rstd
        err = xh * fwv - t_ref[0]
        loss_ref[...] += 0.5 * jnp.sum(jnp.mean(err * err, axis=-1, keepdims=True), axis=0, keepdims=True)
        dy = err * (1.0 / D)
        dfw_ref[...] += jnp.sum(dy * xh, axis=0, keepdims=True)
        dxh = dy * fwv
        dx_ref[0] = rstd * (dxh - xh * jnp.mean(dxh * xh, axis=-1, keepdims=True))

    tok = pl.BlockSpec((1, tm, D), lambda b, i: (b, i, 0))
    return pl.pallas_call(
        body, name="final_loss", grid=(B, S // tm),
        in_specs=[tok, _full((1, D)), tok],
        out_specs=[tok, _full((1, 1)), _full((1, D))],
        out_shape=[jax.ShapeDtypeStruct((B, S, D), F32), jax.ShapeDtypeStruct((1, 1), F32),
                   jax.ShapeDtypeStruct((1, D), F32)],
        compiler_params=_cp(("arbitrary", "arbitrary")),
    )(x, fw, target)


def _local_step(x, pos3, mod, loss_target, small, w_in_a, w_uq_a, w_ukv_a, w_out_b):
    B, S, D = x.shape
    cr, sr, cm, sm = _rope_tables(pos3)
    saved = []
    for l in range(DEPTH):
        shift = mod[l, :, 0:D].reshape(B, 1, D)
        scale = mod[l, :, D:2 * D].reshape(B, 1, D)
        gate = mod[l, :, 2 * D:3 * D].reshape(B, 1, D)
        nw = small["norm_w"][l].reshape(1, D)
        qnw = small["mla_q_norm"][l].reshape(1, 256)
        kvnw = small["mla_kv_norm"][l].reshape(1, 128)
        w_g2p = jnp.pad(small["gla_w_g2"][l], ((0, 112), (0, 0)))
        b_g2 = small["gla_b_g2"][l].reshape(1, 128)
        gnw = jnp.tile(small["gla_norm"][l], 4).reshape(1, 256)
        ret_p, mla_p, gla_p, h = _proj_fwd(x, shift, scale, nw, w_in_a[l])
        r_g, r_raw, r_st = _ret_fwd(ret_p, cr, sr)
        q, k, v = _mla_prep_fwd(mla_p, cm, sm, qnw, kvnw, w_uq_a[l], w_ukv_a[l])
        o_mla, lse = _mla_attn_fwd(q, k, v)
        g_g, g_raw, g_st = _gla_fwd(gla_p, w_g2p, b_g2, gnw)
        x_new, y, mm = _out_fwd(x, gate, r_g, o_mla, mla_p, g_g, w_out_b[l])
        saved.append(dict(x=x, shift=shift, scale=scale, gate=gate, nw=nw, qnw=qnw, kvnw=kvnw, w_g2p=w_g2p,
                          b_g2=b_g2, gnw=gnw, ret_p=ret_p, mla_p=mla_p, gla_p=gla_p, h=h, r_g=r_g, r_raw=r_raw,
                          r_st=r_st, q=q, k=k, v=v, o_mla=o_mla, lse=lse, g_g=g_g, g_raw=g_raw, g_st=g_st,
                          y=y, mm=mm))
        x = x_new

    dx, loss, d_fw = _final_loss(x, small["final_norm"].reshape(1, D), loss_target)
    grads = dict(final_norm=d_fw.reshape(D))
    per_layer = []
    for l in reversed(range(DEPTH)):
        s = saved[l]
        d_r, d_mm, d_g, dw_out, d_gate = _out_bwd(dx, s["y"], s["gate"], s["r_g"], s["mm"], s["g_g"], w_out_b[l])
        d_ret = _ret_bwd(s["ret_p"], cr, sr, s["r_raw"], s["r_st"], d_r)
        do, d_mz, dl = _mla_gate_bwd(d_mm, s["o_mla"], s["mla_p"])
        dq, dk, dv = _mla_attn_bwd(s["q"], s["k"], s["v"], do, s["lse"], dl)
        d_mla, dw_uq, dw_ukv, d_qnw, d_kvnw = _mla_prep_bwd(
            s["mla_p"], cm, sm, s["qnw"], s["kvnw"], w_uq_a[l], w_ukv_a[l], dq, dk, dv)
        d_gla, dw_g2p, db_g2, d_gnw = _gla_bwd(s["gla_p"], s["w_g2p"], s["b_g2"], s["gnw"], s["g_raw"],
                                                s["g_st"], d_g)
        dx, dp, d_shift, d_scale, d_nw = _proj_bwd_x(s["x"], s["shift"], s["scale"], s["nw"], w_in_a[l],
                                                     d_ret, d_mla, d_mz, d_gla, dx)
        dw_in = _proj_bwd_w(s["h"], dp)
        per_layer.append(dict(
            d_mod=jnp.concatenate([d_shift, d_scale, d_gate], axis=2).reshape(B, 3 * D),
            norm_w=d_nw.reshape(D), mla_q_norm=d_qnw.reshape(256), mla_kv_norm=d_kvnw.reshape(128),
            gla_w_g2=dw_g2p[0:16], gla_b_g2=db_g2.reshape(128), gla_norm256=d_gnw.reshape(256),
            w_in_a=dw_in, w_uq_a=dw_uq, w_ukv_a=dw_ukv, w_out=dw_out))
    per_layer = per_layer[::-1]
    for name in per_layer[0]:
        grads[name] = jnp.stack([per_layer[l][name] for l in range(DEPTH)])
    return loss, dx, grads


def _exchange(arrs, gather, name):
    n = len(arrs)
    out_shape = [jax.ShapeDtypeStruct(((N_DEV,) + a.shape) if g else a.shape, a.dtype)
                 for a, g in zip(arrs, gather)]

    def body(*refs):
        ins, outs = refs[:n], refs[n:2 * n]
        send_sems, recv_sems, local_sems = refs[2 * n:]
        ix, iy, ic = lax.axis_index("x"), lax.axis_index("y"), lax.axis_index("c")
        me = 4 * ix + 2 * iy + ic
        copies = []
        for a in range(n):
            mine = ins[a] if gather[a] else ins[a].at[me]
            loc = pltpu.make_async_copy(mine, outs[a].at[me], local_sems.at[a])
            loc.start()
            copies.append(loc)
            for d in range(1, N_DEV):
                px = 1 - ix if d & 4 else ix
                py = 1 - iy if d & 2 else iy
                pc = 1 - ic if d & 1 else ic
                src = ins[a] if gather[a] else ins[a].at[4 * px + 2 * py + pc]
                cp = pltpu.make_async_remote_copy(
                    src_ref=src, dst_ref=outs[a].at[me], send_sem=send_sems.at[a, d - 1],
                    recv_sem=recv_sems.at[a, d - 1], device_id=(px, py, pc), device_id_type=pl.DeviceIdType.MESH)
                cp.start()
                copies.append(cp)
        for cp in copies:
            cp.wait()

    any_spec = pl.BlockSpec(memory_space=pl.ANY)
    outs = pl.pallas_call(
        body, name=name, in_specs=[any_spec] * n, out_specs=[any_spec] * n, out_shape=out_shape,
        scratch_shapes=[pltpu.SemaphoreType.DMA((n, N_DEV - 1)), pltpu.SemaphoreType.DMA((n, N_DEV - 1)),
                        pltpu.SemaphoreType.DMA((n,))],
    )(*arrs)
    return list(outs)


def _ada_fwd(c_all, ada_w, ada_b_cols):
    nb, D = c_all.shape
    cols = ada_w.shape[2]

    def body(c_ref, w_ref, b_ref, out_ref):
        ca = _silu(c_ref[...])
        for l in range(DEPTH):
            out_ref[l] = _mm(ca, w_ref[l]) + b_ref[l:l + 1, :]

    return pl.pallas_call(
        body, name="ada_fwd", out_shape=jax.ShapeDtypeStruct((DEPTH, nb, cols), F32),
        in_specs=[pl.BlockSpec(memory_space=pltpu.VMEM)] * 3, out_specs=pl.BlockSpec(memory_space=pltpu.VMEM),
        compiler_params=pltpu.CompilerParams(vmem_limit_bytes=32 * VMEM_MB),
    )(c_all, ada_w, ada_b_cols)


def _ada_bwd(c_all, d_mod_cols):
    nb, D = c_all.shape
    cols = d_mod_cols.shape[2]

    def body(c_ref, dm_ref, out_ref):
        ca = _silu(c_ref[...])
        for l in range(DEPTH):
            out_ref[l] = _mm_tn(ca, dm_ref[l])

    return pl.pallas_call(
        body, name="ada_bwd", out_shape=jax.ShapeDtypeStruct((DEPTH, D, cols), F32),
        in_specs=[pl.BlockSpec(memory_space=pltpu.VMEM)] * 2, out_specs=pl.BlockSpec(memory_space=pltpu.VMEM),
        compiler_params=pltpu.CompilerParams(vmem_limit_bytes=32 * VMEM_MB),
    )(c_all, d_mod_cols)


def _sum_adamw(parts, w, m, v, name):
    P, R, C = parts.shape
    tr = 256 if (R % 256 == 0 and R > 256) else R

    def body(p_ref, w_ref, m_ref, v_ref, g_ref, d_ref, nm_ref, nv_ref):
        g = p_ref[0]
        for k in range(1, P):
            g = g + p_ref[k]
        g_ref[...] = g
        nm = ADAM_B1 * m_ref[...] + (1.0 - ADAM_B1) * g
        nv = ADAM_B2 * v_ref[...] + (1.0 - ADAM_B2) * (g * g)
        nm_ref[...] = nm
        nv_ref[...] = nv
        m_hat = nm / (1.0 - ADAM_B1 ** ADAM_STEP)
        v_hat = nv / (1.0 - ADAM_B2 ** ADAM_STEP)
        d_ref[...] = -ADAM_LR * (m_hat / (jnp.sqrt(v_hat) + ADAM_EPS) + ADAM_WD * w_ref[...])

    blk = pl.BlockSpec((tr, C), lambda i: (i, 0))
    shp = jax.ShapeDtypeStruct((R, C), F32)
    return pl.pallas_call(
        body, name=name, grid=(R // tr,),
        in_specs=[pl.BlockSpec((P, tr, C), lambda i: (0, i, 0)), blk, blk, blk],
        out_specs=[blk, blk, blk, blk], out_shape=[shp, shp, shp, shp],
        compiler_params=_cp(("parallel",)),
    )(parts, w, m, v)


SMALL = [("norm_w", DEPTH * 1024), ("mla_q_norm", DEPTH * 256), ("mla_kv_norm", DEPTH * 128),
         ("gla_w_g2", DEPTH * 16 * 128), ("gla_b_g2", DEPTH * 128), ("gla_norm", DEPTH * 64), ("final_norm", 1024)]
SMALL_ROWS = 72


def _pack_small(first_row, vals):
    flat = [first_row.reshape(128)] + [vals[n].reshape(-1) for n, _ in SMALL]
    used = 128 + sum(s for _, s in SMALL)
    flat.append(jnp.zeros((SMALL_ROWS * 128 - used,), F32))
    return jnp.concatenate(flat).reshape(SMALL_ROWS, 128)


def _unpack_small(packed, shapes):
    flat = packed.reshape(-1)
    out, off = {}, 128
    for n, s in SMALL:
        out[n] = flat[off:off + s].reshape(shapes[n])
        off += s
    return out


WEIGHTS = ["norm_w", "ada_w", "ada_b", "w_in", "mla_q_norm", "w_uq", "mla_kv_norm", "w_ukv", "gla_w_g2",
           "gla_b_g2", "gla_norm", "w_out", "final_norm"]


def kernel(x, c, positions, norm_w, ada_w, ada_b, w_in, mla_q_norm, w_uq, mla_kv_norm, w_ukv, gla_w_g2, gla_b_g2, gla_norm, w_out, final_norm, loss_target, m_norm_w, m_ada_w, m_ada_b, m_w_in, m_mla_q_norm, m_w_uq, m_mla_kv_norm, m_w_ukv, m_gla_w_g2, m_gla_b_g2, m_gla_norm, m_w_out, m_final_norm, v_norm_w, v_ada_w, v_ada_b, v_w_in, v_mla_q_norm, v_w_uq, v_mla_kv_norm, v_w_ukv, v_gla_w_g2, v_gla_b_g2, v_gla_norm, v_w_out, v_final_norm):
    w = dict(norm_w=norm_w, ada_w=ada_w, ada_b=ada_b, w_in=w_in, mla_q_norm=mla_q_norm, w_uq=w_uq,
             mla_kv_norm=mla_kv_norm, w_ukv=w_ukv, gla_w_g2=gla_w_g2, gla_b_g2=gla_b_g2, gla_norm=gla_norm,
             w_out=w_out, final_norm=final_norm)
    m = dict(norm_w=m_norm_w, ada_w=m_ada_w, ada_b=m_ada_b, w_in=m_w_in, mla_q_norm=m_mla_q_norm, w_uq=m_w_uq,
             mla_kv_norm=m_mla_kv_norm, w_ukv=m_w_ukv, gla_w_g2=m_gla_w_g2, gla_b_g2=m_gla_b_g2,
             gla_norm=m_gla_norm, w_out=m_w_out, final_norm=m_final_norm)
    v = dict(norm_w=v_norm_w, ada_w=v_ada_w, ada_b=v_ada_b, w_in=v_w_in, mla_q_norm=v_mla_q_norm, w_uq=v_w_uq,
             mla_kv_norm=v_mla_kv_norm, w_ukv=v_w_ukv, gla_w_g2=v_gla_w_g2, gla_b_g2=v_gla_b_g2,
             gla_norm=v_gla_norm, w_out=v_w_out, final_norm=v_final_norm)
    B, S, D = x.shape
    me = 4 * lax.axis_index("x") + 2 * lax.axis_index("y") + lax.axis_index("c")
    ada_cols = ada_w.shape[2]
    cast = lambda a: a.astype(_MXU)

    c_g, w_in_g, w_uq_g, w_ukv_g, w_out_g = _exchange(
        [c, cast(w_in), cast(w_uq), cast(w_ukv), cast(w_out)], [True] * 5, "gather_weights")
    c_all = c_g.reshape(N_DEV * B, D)
    w_in_full = jnp.transpose(w_in_g, (1, 2, 0, 3)).reshape(DEPTH, D, -1)
    w_uq_full = jnp.transpose(w_uq_g, (1, 2, 0, 3)).reshape(DEPTH, 256, -1)
    w_ukv_full = jnp.transpose(w_ukv_g, (1, 2, 0, 3)).reshape(DEPTH, 128, -1)
    w_out_full = jnp.transpose(w_out_g, (1, 0, 2, 3)).reshape(DEPTH, D, D)
    w_in_a = jnp.stack([_arrange_w_in(w_in_full[l]) for l in range(DEPTH)])
    w_uq_a = jnp.stack([_arrange_w_uq(w_uq_full[l]) for l in range(DEPTH)])
    w_ukv_a = jnp.stack([_arrange_w_ukv(w_ukv_full[l]) for l in range(DEPTH)])

    ada_b_cols = lax.dynamic_slice(ada_b, (0, me * ada_cols), (DEPTH, ada_cols))
    mod_cols = _ada_fwd(c_all, ada_w, ada_b_cols)
    mod_send = jnp.transpose(mod_cols.reshape(DEPTH, N_DEV, B, ada_cols), (1, 0, 2, 3))
    (mod_recv,) = _exchange([mod_send], [False], "scatter_mod")
    mod = jnp.transpose(mod_recv, (1, 2, 0, 3)).reshape(DEPTH, B, 3 * D)

    small_w = {n: w[n] for n, _ in SMALL}
    loss, grad_x, g = _local_step(x, positions.reshape(B, S, 1), mod, loss_target, small_w,
                                  w_in_a, w_uq_a, w_ukv_a, w_out_full)

    d_mod = g["d_mod"]
    part = dict(norm_w=g["norm_w"], mla_q_norm=g["mla_q_norm"], mla_kv_norm=g["mla_kv_norm"],
                gla_w_g2=g["gla_w_g2"], gla_b_g2=g["gla_b_g2"], gla_norm=g["gla_norm256"][:, 0:64],
                final_norm=g["final_norm"])
    small_part = _pack_small(jnp.pad(loss.reshape(1), (0, 127)), part)
    dw_in = jnp.stack([_unarrange_w_in(g["w_in_a"][l]) for l in range(DEPTH)])
    dw_uq = jnp.stack([_unarrange_w_uq(g["w_uq_a"][l]) for l in range(DEPTH)])
    dw_ukv = jnp.stack([_unarrange_w_ukv(g["w_ukv_a"][l]) for l in range(DEPTH)])
    to_cols = lambda a: jnp.transpose(a.reshape(a.shape[0], a.shape[1], N_DEV, -1), (2, 0, 1, 3))
    dw_out_send = jnp.transpose(g["w_out"].reshape(DEPTH, N_DEV, D // N_DEV, D), (1, 0, 2, 3))
    d_mod_g, small_g, dw_in_g, dw_uq_g, dw_ukv_g, dw_out_g = _exchange(
        [d_mod, small_part, to_cols(dw_in), to_cols(dw_uq), to_cols(dw_ukv), dw_out_send],
        [True, True, False, False, False, False], "exchange_grads")

    d_mod_all = jnp.transpose(d_mod_g, (1, 0, 2, 3)).reshape(DEPTH, N_DEV * B, 3 * D)
    d_mod_cols = lax.dynamic_slice(d_mod_all, (0, 0, me * ada_cols), (DEPTH, N_DEV * B, ada_cols))
    g_ada_w = _ada_bwd(c_all, d_mod_cols)

    res = {}

    def update(name, parts2d):
        shp = w[name].shape
        two = lambda a: a.reshape(parts2d.shape[1:])
        out = _sum_adamw(parts2d, two(w[name]), two(m[name]), two(v[name]), "adamw_" + name)
        res[name] = [o.reshape(shp) for o in out]

    update("ada_w", g_ada_w.reshape(1, DEPTH * D, ada_cols))
    update("ada_b", jnp.transpose(d_mod_g, (0, 2, 1, 3)).reshape(N_DEV * B, DEPTH * 3 * D // 128, 128))
    update("w_in", dw_in_g.reshape(N_DEV, DEPTH * D, -1))
    update("w_uq", dw_uq_g.reshape(N_DEV, DEPTH * 256, -1))
    update("w_ukv", dw_ukv_g.reshape(N_DEV, DEPTH * 128, -1))
    update("w_out", dw_out_g.reshape(N_DEV, DEPTH * (D // N_DEV), D))
    zero_row = jnp.zeros((128,), F32)
    small_out = _sum_adamw(small_g, _pack_small(zero_row, small_w), _pack_small(zero_row, {n: m[n] for n, _ in SMALL}),
                           _pack_small(zero_row, {n: v[n] for n, _ in SMALL}), "adamw_small")
    shapes = {n: w[n].shape for n, _ in SMALL}
    unpacked = [_unpack_small(o, shapes) for o in small_out]
    for n, _ in SMALL:
        res[n] = [u[n] for u in unpacked]
    loss_out = small_out[0][0, 0]
    return (loss_out, grad_x, *[res[n][0] for n in WEIGHTS], *[res[n][1] for n in WEIGHTS],
            *[res[n][2] for n in WEIGHTS], *[res[n][3] for n in WEIGHTS])
```

```python
import functools
import math

import numpy as np
import jax
import jax.numpy as jnp
from jax import lax
from jax.experimental import pallas as pl
from jax.experimental.pallas import tpu as pltpu

F32 = jnp.float32
_MXU = jnp.bfloat16

D_MODEL = 1024
DEPTH = 2
CHUNK = 64
EPS = 1e-6
ROPE_THETA = 10000.0
N_DEV = 8

MLA_SCALE = 96.0 ** -0.5
RET_KSCALE = 64.0 ** -0.5
GLA_KSCALE = 32.0 ** -0.5
GLA_TAU = 16.0

ADAM_LR = 0.001
ADAM_B1 = 0.9
ADAM_B2 = 0.999
ADAM_EPS = 1e-08
ADAM_WD = 0.01
ADAM_STEP = 10

RET_W, MLA_W, GLA_W = 1024, 1024, 896
ARR_W = RET_W + MLA_W + GLA_W
VMEM_MB = 1024 * 1024


def _cp(sem, vmem_mb=48):
    return pltpu.CompilerParams(dimension_semantics=sem, vmem_limit_bytes=vmem_mb * VMEM_MB)


def _mm(a, b):
    return jnp.dot(a.astype(_MXU), b.astype(_MXU), preferred_element_type=F32)


def _mm_nt(a, b):
    return lax.dot_general(a.astype(_MXU), b.astype(_MXU), (((1,), (1,)), ((), ())),
                           preferred_element_type=F32)


def _mm_tn(a, b):
    return lax.dot_general(a.astype(_MXU), b.astype(_MXU), (((0,), (0,)), ((), ())),
                           preferred_element_type=F32)


def _mm_f32(a, b):
    return jnp.dot(a, b, precision=lax.Precision.HIGHEST, preferred_element_type=F32)


def _sig(z):
    return 1.0 / (1.0 + jnp.exp(-z))


def _silu(z):
    return z * _sig(z)


def _dsilu(z):
    s = _sig(z)
    return s * (1.0 + z * (1.0 - s))


def _full(shape):
    nd = len(shape)
    return pl.BlockSpec(shape, lambda *_: (0,) * nd)


def _qk_perm(blk):
    r = blk.shape[0]
    return jnp.transpose(blk.reshape(r, 4, 2, 32), (0, 2, 1, 3)).reshape(r, 256)


def _qk_unperm(blk):
    r = blk.shape[0]
    return jnp.transpose(blk.reshape(r, 2, 4, 32), (0, 2, 1, 3)).reshape(r, 256)


def _arrange_w_in(w):
    z = lambda n: jnp.zeros((w.shape[0], n), w.dtype)
    ret = [_qk_perm(w[:, 0:256]), _qk_perm(w[:, 256:512]), w[:, 512:768], w[:, 768:1024]]
    mla = [w[:, 1024:1280], w[:, 1280:1408], z(64), w[:, 1408:1440], z(32), w[:, 1440:1952]]
    gla = [w[:, 1952:2080], w[:, 2080:2208], w[:, 2208:2464], w[:, 2464:2480], z(112), w[:, 2480:2736]]
    return jnp.concatenate(ret + mla + gla, axis=1)


def _unarrange_w_in(a):
    m, g = RET_W, RET_W + MLA_W
    parts = [_qk_unperm(a[:, 0:256]), _qk_unperm(a[:, 256:512]), a[:, 512:1024],
             a[:, m:m + 384], a[:, m + 448:m + 480], a[:, m + 512:m + 1024],
             a[:, g:g + 528], a[:, g + 640:g + 896]]
    return jnp.concatenate(parts, axis=1)


def _arrange_w_uq(w):
    return jnp.pad(w.reshape(256, 8, 96), ((0, 0), (0, 0), (0, 32))).reshape(256, 1024)


def _unarrange_w_uq(a):
    return a.reshape(256, 8, 128)[:, :, :96].reshape(256, 768)


def _arrange_w_ukv(w):
    r = w.reshape(128, 8, 128)
    k = jnp.pad(r[:, :, :64], ((0, 0), (0, 0), (0, 64))).reshape(128, 1024)
    return jnp.concatenate([k, r[:, :, 64:].reshape(128, 512)], axis=1)


def _unarrange_w_ukv(a):
    k = a[:, :1024].reshape(128, 8, 128)[:, :, :64]
    v = a[:, 1024:].reshape(128, 8, 64)
    return jnp.concatenate([k, v], axis=2).reshape(128, 1024)


def _rope_tables(pos3):
    B, S, _ = pos3.shape
    ts = min(S, 512)
    inv32 = (np.float32(ROPE_THETA) ** (-(np.arange(32, dtype=np.float32) / 32))).astype(np.float32)
    inv16 = (np.float32(ROPE_THETA) ** (-(np.arange(16, dtype=np.float32) / 16))).astype(np.float32)
    inv_r = np.tile(inv32, 4)[None, :]
    inv_m = np.zeros((1, 128), np.float32)
    inv_m[0, 64:80] = inv16
    inv_m[0, 80:96] = inv16

    def body(pos_ref, ir_ref, im_ref, cr, sr, cm, sm):
        p = pos_ref[0].astype(F32)
        ar = p * ir_ref[...]
        cr[0] = jnp.cos(ar)
        sr[0] = jnp.sin(ar)
        am = p * im_ref[...]
        cm[0] = jnp.cos(am)
        sm[0] = jnp.sin(am)

    tab = jax.ShapeDtypeStruct((B, S, 128), F32)
    blk = pl.BlockSpec((1, ts, 128), lambda b, i: (b, i, 0))
    return pl.pallas_call(
        body, name="rope_tables", grid=(B, S // ts),
        in_specs=[pl.BlockSpec((1, ts, 1), lambda b, i: (b, i, 0)), _full((1, 128)), _full((1, 128))],
        out_specs=[blk, blk, blk, blk], out_shape=[tab, tab, tab, tab],
        compiler_params=_cp(("parallel", "parallel")),
    )(pos3, jnp.asarray(inv_r), jnp.asarray(inv_m))


def _rope128(x, cos, sin):
    lane = lax.broadcasted_iota(jnp.int32, (1, 128), 1)
    rp = pltpu.roll(x, 16, 1)
    rm = pltpu.roll(x, 112, 1)
    return x * cos + jnp.where(lane < 80, -rm, rp) * sin


def _rope128_t(d, cos, sin):
    lane = lax.broadcasted_iota(jnp.int32, (1, 128), 1)
    y = d * sin
    yp = pltpu.roll(y, 16, 1)
    ym = pltpu.roll(y, 112, 1)
    return d * cos + jnp.where(lane < 64, 0.0, jnp.where(lane < 80, ym, jnp.where(lane < 96, -yp, 0.0)))


def _proj_fwd(x, shift, scale, nw, w_arr):
    B, S, D = x.shape
    tm = min(S, 512)

    def body(x_ref, sh_ref, sc_ref, nw_ref, w_ref, ret_ref, mla_ref, gla_ref, h_ref):
        xv = x_ref[0]
        rstd = lax.rsqrt(jnp.mean(xv * xv, axis=-1, keepdims=True) + EPS)
        h = (xv * rstd * nw_ref[...]) * (1.0 + sc_ref[0]) + sh_ref[0]
        hb = h.astype(_MXU)
        h_ref[0] = hb
        ret_ref[0] = jnp.dot(hb, w_ref[:, 0:RET_W], preferred_element_type=F32)
        mla_ref[0] = jnp.dot(hb, w_ref[:, RET_W:RET_W + MLA_W], preferred_element_type=F32)
        gla_ref[0] = jnp.dot(hb, w_ref[:, RET_W + MLA_W:ARR_W], preferred_element_type=F32)

    tok = lambda w: pl.BlockSpec((1, tm, w), lambda b, i: (b, i, 0))
    per_seq = pl.BlockSpec((1, 1, D), lambda b, i: (b, 0, 0))
    return pl.pallas_call(
        body, name="proj_fwd", grid=(B, S // tm),
        in_specs=[tok(D), per_seq, per_seq, _full((1, D)), _full((D, ARR_W))],
        out_specs=[tok(RET_W), tok(MLA_W), tok(GLA_W), tok(D)],
        out_shape=[jax.ShapeDtypeStruct((B, S, RET_W), F32), jax.ShapeDtypeStruct((B, S, MLA_W), F32),
                   jax.ShapeDtypeStruct((B, S, GLA_W), F32), jax.ShapeDtypeStruct((B, S, D), _MXU)],
        compiler_params=_cp(("parallel", "parallel")),
    )(x, shift, scale, nw, w_arr)


RET_L = 256


def _ret_consts(L):
    lg = np.log1p(-np.exp2(-5.0 - np.arange(4, dtype=np.float32))).astype(np.float32)
    i = np.arange(L)
    ci = i // CHUNK
    diff = (i[:, None] - i[None, :]).astype(np.float32)
    same = ci[:, None] == ci[None, :]
    past = ci[None, :] < ci[:, None]
    expo = np.where(same, np.abs(diff), np.where(past, diff, 0.0)).astype(np.float32)
    dec = np.where((same | past)[None], np.exp(lg[:, None, None] * expo[None]), 0.0).astype(np.float32)
    head = (np.arange(256) % 128) // 32
    qw = np.exp((i + 1.0)[:, None] * lg[head][None, :]).astype(np.float32)
    kw = np.exp((L - 1.0 - i)[:, None] * lg[head][None, :]).astype(np.float32)
    a_row = np.exp(np.float32(L) * lg[head])[None, :].astype(np.float32)
    return [jnp.asarray(t) for t in (dec.reshape(4 * L, L), qw, kw, a_row)]


def _ret_masks():
    lane = lax.broadcasted_iota(jnp.int32, (1, 256), 1)
    mh = [((lane % 128) // 32) == h for h in range(4)]
    mv = [(lane // 64) == h for h in range(4)]
    vi = lax.broadcasted_iota(jnp.int32, (256, 256), 0)
    ki = lax.broadcasted_iota(jnp.int32, (256, 256), 1)
    bd = (vi // 64) == ((ki % 128) // 32)
    return mh, mv, bd


def _ret_rope(p, cs, sn):
    q1, q2, k1, k2 = p[:, 0:128], p[:, 128:256], p[:, 256:384], p[:, 384:512]
    qr = jnp.concatenate([q1 * cs - q2 * sn, q2 * cs + q1 * sn], axis=1)
    kr = jnp.concatenate([k1 * cs - k2 * sn, k2 * cs + k1 * sn], axis=1) * RET_KSCALE
    return qr, kr


def _head_mean(x, mv, width):
    out = jnp.zeros_like(x)
    for m in mv:
        s = jnp.sum(jnp.where(m, x, 0.0), axis=-1, keepdims=True) * (1.0 / width)
        out = jnp.where(m, s, out)
    return out


def _stack_heads(x, masks):
    return jnp.concatenate([jnp.where(m, x, 0.0) for m in masks], axis=0)


def _fold_heads(xs, masks, L):
    out = jnp.where(masks[0], xs[0:L], 0.0)
    for h in range(1, 4):
        out = out + jnp.where(masks[h], xs[h * L:(h + 1) * L], 0.0)
    return out


def _ret_fwd(ret_p, cos, sin):
    B, S, _ = ret_p.shape
    L = min(RET_L, S)
    NB = S // L
    consts = _ret_consts(L)

    def body(p_ref, c_ref, s_ref, ds_ref, qw_ref, kw_ref, a_ref, out_ref, raw_ref, st_ref, st_sc):
        @pl.when(pl.program_id(1) == 0)
        def _():
            st_sc[...] = jnp.zeros_like(st_sc)

        mh, mv, bd = _ret_masks()
        p = p_ref[0]
        qr, kr = _ret_rope(p, c_ref[0], s_ref[0])
        v = p[:, 512:768]
        z = p[:, 768:1024]
        a_s = _mm_nt(_stack_heads(qr, mh), kr) * ds_ref[...]
        intra = _fold_heads(_mm(a_s, v), mv, L)
        st = st_sc[...]
        st_ref[0, 0] = st
        r = intra + _mm_nt(qr * qw_ref[...], st)
        raw_ref[0] = r
        st_sc[...] = st * a_ref[...] + jnp.where(bd, _mm_tn(v, kr * kw_ref[...]), 0.0)
        rstd = lax.rsqrt(_head_mean(r * r, mv, 64.0) + EPS)
        out_ref[0] = (r * rstd * _silu(z)).astype(_MXU)

    tok = lambda w: pl.BlockSpec((1, L, w), lambda b, n: (b, n, 0))
    return pl.pallas_call(
        body, name="ret_fwd", grid=(B, NB),
        in_specs=[tok(RET_W), tok(128), tok(128), _full((4 * L, L)), _full((L, 256)), _full((L, 256)),
                  _full((1, 256))],
        out_specs=[tok(256), tok(256), pl.BlockSpec((1, 1, 256, 256), lambda b, n: (b, n, 0, 0))],
        out_shape=[jax.ShapeDtypeStruct((B, S, 256), _MXU), jax.ShapeDtypeStruct((B, S, 256), F32),
                   jax.ShapeDtypeStruct((B, NB, 256, 256), F32)],
        scratch_shapes=[pltpu.VMEM((256, 256), F32)],
        compiler_params=_cp(("parallel", "arbitrary")),
    )(ret_p, cos, sin, *consts)


def _ret_bwd(ret_p, cos, sin, raw, states, d_mix):
    B, S, _ = ret_p.shape
    L = min(RET_L, S)
    NB = S // L
    consts = _ret_consts(L)

    def body(p_ref, c_ref, s_ref, raw_ref, st_ref, dm_ref, ds_ref, qw_ref, kw_ref, a_ref, dp_ref, dst_sc):
        @pl.when(pl.program_id(1) == 0)
        def _():
            dst_sc[...] = jnp.zeros_like(dst_sc)

        mh, mv, bd = _ret_masks()
        p = p_ref[0]
        cs, sn = c_ref[0], s_ref[0]
        qr, kr = _ret_rope(p, cs, sn)
        v = p[:, 512:768]
        z = p[:, 768:1024]
        qs = _stack_heads(qr, mh)
        dec = ds_ref[...]
        a_s = _mm_nt(qs, kr) * dec
        r = raw_ref[0]
        rstd = lax.rsqrt(_head_mean(r * r, mv, 64.0) + EPS)
        rn = r * rstd
        dm = dm_ref[0]
        d_rn = dm * _silu(z)
        dz = dm * rn * _dsilu(z)
        dr = rstd * (d_rn - rn * _head_mean(d_rn * rn, mv, 64.0))
        do_s = _stack_heads(dr, mv)
        da_s = _mm_nt(do_s, v) * dec
        dv = _mm_tn(a_s, do_s)
        dqr = _fold_heads(_mm(da_s, kr), mh, L)
        dkr = _mm_tn(da_s, qs)
        st = st_ref[0, 0]
        qw, kw = qw_ref[...], kw_ref[...]
        dqr = dqr + _mm(dr, st) * qw
        dst_next = dst_sc[...]
        g = jnp.where(bd, dst_next, 0.0)
        kk = kr * kw
        dv = dv + _mm_nt(kk, g)
        dkr = dkr + _mm(v, g) * kw
        dst_sc[...] = dst_next * a_ref[...] + jnp.where(bd, _mm_tn(dr, qr * qw), 0.0)
        dkr = dkr * RET_KSCALE
        dq1, dq2 = dqr[:, 0:128], dqr[:, 128:256]
        dk1, dk2 = dkr[:, 0:128], dkr[:, 128:256]
        dp_ref[0] = jnp.concatenate(
            [dq1 * cs + dq2 * sn, dq2 * cs - dq1 * sn, dk1 * cs + dk2 * sn, dk2 * cs - dk1 * sn, dv, dz], axis=1)

    tok = lambda w: pl.BlockSpec((1, L, w), lambda b, i: (b, NB - 1 - i, 0))
    return pl.pallas_call(
        body, name="ret_bwd", grid=(B, NB),
        in_specs=[tok(RET_W), tok(128), tok(128), tok(256),
                  pl.BlockSpec((1, 1, 256, 256), lambda b, i: (b, NB - 1 - i, 0, 0)), tok(256),
                  _full((4 * L, L)), _full((L, 256)), _full((L, 256)), _full((1, 256))],
        out_specs=tok(RET_W), out_shape=jax.ShapeDtypeStruct((B, S, RET_W), F32),
        scratch_shapes=[pltpu.VMEM((256, 256), F32)],
        compiler_params=_cp(("parallel", "arbitrary")),
    )(ret_p, cos, sin, raw, states, d_mix, *consts)


def _gla_masks():
    C = CHUNK
    lk = lax.broadcasted_iota(jnp.int32, (1, 128), 1)
    lv = lax.broadcasted_iota(jnp.int32, (1, 256), 1)
    mk = [(lk // 32) == h for h in range(4)]
    mv = [(lv // 64) == h for h in range(4)]
    vi = lax.broadcasted_iota(jnp.int32, (256, 128), 0)
    ki = lax.broadcasted_iota(jnp.int32, (256, 128), 1)
    bd = (vi // 64) == (ki // 32)
    ri = lax.broadcasted_iota(jnp.int32, (4 * C, C), 0) % C
    cj = lax.broadcasted_iota(jnp.int32, (4 * C, C), 1)
    lower = ri >= cj
    ti = lax.broadcasted_iota(jnp.int32, (C, C), 0)
    tj = lax.broadcasted_iota(jnp.int32, (C, C), 1)
    ltri = jnp.where(ti >= tj, 1.0, 0.0).astype(F32)
    utri = jnp.where(ti <= tj, 1.0, 0.0).astype(F32)
    return mk, mv, bd, lower, ltri, utri


def _gla_gate(p, w_ref, b_ref, ltri):
    pre = _mm(p[:, 512:640], w_ref[...]) + b_ref[...]
    la = (jnp.minimum(pre, 0.0) - jnp.log(1.0 + jnp.exp(-jnp.abs(pre)))) * (1.0 / GLA_TAU)
    cum = _mm_f32(ltri, la)
    return pre, cum


GLA_G = 4


def _gla_fwd(gla_p, w_g2p, b_g2, gnw):
    B, S, _ = gla_p.shape
    C = CHUNK
    NC = S // C
    G = min(GLA_G, NC)
    NG = NC // G

    def body(p_ref, w_ref, b_ref, gn_ref, out_ref, raw_ref, st_ref, st_sc):
        @pl.when(pl.program_id(1) == 0)
        def _():
            st_sc[...] = jnp.zeros_like(st_sc)

        mk, mv, bd, lower, ltri, _ = _gla_masks()
        st = st_sc[...]
        for c in range(G):
            rows = slice(c * C, (c + 1) * C)
            p = p_ref[0, rows, :]
            q = p[:, 0:128]
            k = p[:, 128:256] * GLA_KSCALE
            v = p[:, 256:512]
            z = p[:, 640:896]
            _, cum = _gla_gate(p, w_ref, b_ref, ltri)
            last = cum[C - 1:C, :]
            e_pos = jnp.exp(cum)
            e_neg = jnp.exp(-cum)
            q_pos = q * e_pos
            past = _mm_nt(_stack_heads(q_pos, mk), k * e_neg)
            fut = _mm_nt(_stack_heads(q * e_neg, mk), k * e_pos)
            attn = jnp.where(lower, past, fut)
            intra = _fold_heads(_mm(attn, v), mv, C)
            st_ref[0, c] = st
            g = intra + _mm_nt(q_pos, st)
            raw_ref[0, rows, :] = g
            kd = k * jnp.exp(last - cum)
            st = st * jnp.exp(last) + jnp.where(bd, _mm_tn(v, kd), 0.0)
            rstd = lax.rsqrt(_head_mean(g * g, mv, 64.0) + EPS)
            out_ref[0, rows, :] = (g * rstd * gn_ref[...] * _silu(z)).astype(_MXU)
        st_sc[...] = st

    tok = lambda w: pl.BlockSpec((1, G * C, w), lambda b, n: (b, n, 0))
    return pl.pallas_call(
        body, name="gla_fwd", grid=(B, NG),
        in_specs=[tok(GLA_W), _full((128, 128)), _full((1, 128)), _full((1, 256))],
        out_specs=[tok(256), tok(256), pl.BlockSpec((1, G, 256, 128), lambda b, n: (b, n, 0, 0))],
        out_shape=[jax.ShapeDtypeStruct((B, S, 256), _MXU), jax.ShapeDtypeStruct((B, S, 256), F32),
                   jax.ShapeDtypeStruct((B, NC, 256, 128), F32)],
        scratch_shapes=[pltpu.VMEM((256, 128), F32)],
        compiler_params=_cp(("parallel", "arbitrary")),
    )(gla_p, w_g2p, b_g2, gnw)


def _gla_bwd(gla_p, w_g2p, b_g2, gnw, raw, states, d_mix):
    B, S, _ = gla_p.shape
    C = CHUNK
    NC = S // C
    G = min(GLA_G, NC)
    NG = NC // G

    def body(p_ref, w_ref, b_ref, gn_ref, raw_ref, st_ref, dm_ref, dp_ref, dw_ref, db_ref, dgn_ref, dst_sc):
        first = (pl.program_id(0) == 0) & (pl.program_id(1) == 0)

        @pl.when(first)
        def _():
            dw_ref[...] = jnp.zeros_like(dw_ref)
            db_ref[...] = jnp.zeros_like(db_ref)
            dgn_ref[...] = jnp.zeros_like(dgn_ref)

        @pl.when(pl.program_id(1) == 0)
        def _():
            dst_sc[...] = jnp.zeros_like(dst_sc)

        mk, mv, bd, lower, ltri, utri = _gla_masks()
        gn = gn_ref[...]
        dst_next = dst_sc[...]
        dw_acc = jnp.zeros((128, 128), F32)
        db_acc = jnp.zeros((1, 128), F32)
        dgn_acc = jnp.zeros((1, 256), F32)
        for c in reversed(range(G)):
            rows = slice(c * C, (c + 1) * C)
            p = p_ref[0, rows, :]
            q = p[:, 0:128]
            k = p[:, 128:256] * GLA_KSCALE
            v = p[:, 256:512]
            gg = p[:, 512:640]
            z = p[:, 640:896]
            pre, cum = _gla_gate(p, w_ref, b_ref, ltri)
            last = cum[C - 1:C, :]
            e_pos = jnp.exp(cum)
            e_neg = jnp.exp(-cum)
            q_pos, q_neg = q * e_pos, q * e_neg
            k_pos, k_neg = k * e_pos, k * e_neg
            qp_s = _stack_heads(q_pos, mk)
            qn_s = _stack_heads(q_neg, mk)
            attn = jnp.where(lower, _mm_nt(qp_s, k_neg), _mm_nt(qn_s, k_pos))
            g = raw_ref[0, rows, :]
            rstd = lax.rsqrt(_head_mean(g * g, mv, 64.0) + EPS)
            gh = g * rstd
            dm = dm_ref[0, rows, :]
            d_gn = dm * _silu(z)
            dz = dm * gh * gn * _dsilu(z)
            dgn_acc = dgn_acc + jnp.sum(d_gn * gh, axis=0, keepdims=True)
            d_gh = d_gn * gn
            dg = rstd * (d_gh - gh * _head_mean(d_gh * gh, mv, 64.0))
            do_s = _stack_heads(dg, mv)
            dattn = _mm_nt(do_s, v)
            dv = _mm_tn(attn, do_s)
            dpast = jnp.where(lower, dattn, 0.0)
            dfut = jnp.where(lower, 0.0, dattn)
            dq_pos = _fold_heads(_mm(dpast, k_neg), mk, C)
            dk_neg = _mm_tn(dpast, qp_s)
            dq_neg = _fold_heads(_mm(dfut, k_pos), mk, C)
            dk_pos = _mm_tn(dfut, qn_s)
            st = st_ref[0, c]
            dq_pos = dq_pos + _mm(dg, st)
            a_row = jnp.exp(last)
            d_a = jnp.sum(dst_next * st, axis=0, keepdims=True)
            gmat = jnp.where(bd, dst_next, 0.0)
            w_dec = jnp.exp(last - cum)
            kd = k * w_dec
            d_kd = _mm(v, gmat)
            dv = dv + _mm_nt(kd, gmat)
            dst_next = dst_next * a_row + jnp.where(bd, _mm_tn(dg, q_pos), 0.0)
            t = d_kd * kd
            dk = d_kd * w_dec + dk_neg * e_neg + dk_pos * e_pos
            dq = dq_pos * e_pos + dq_neg * e_neg
            d_last = jnp.sum(t, axis=0, keepdims=True) + d_a * a_row
            d_cum = dq_pos * q_pos - dk_neg * k_neg - dq_neg * q_neg + dk_pos * k_pos - t
            row = lax.broadcasted_iota(jnp.int32, (C, 128), 0)
            d_cum = d_cum + jnp.where(row == C - 1, d_last, 0.0)
            d_la = _mm_f32(utri, d_cum)
            d_pre = d_la * _sig(-pre) * (1.0 / GLA_TAU)
            d_gg = _mm_nt(d_pre, w_ref[...])
            dw_acc = dw_acc + _mm_tn(gg, d_pre)
            db_acc = db_acc + jnp.sum(d_pre, axis=0, keepdims=True)
            dp_ref[0, rows, :] = jnp.concatenate([dq, dk * GLA_KSCALE, dv, d_gg, dz], axis=1)
        dst_sc[...] = dst_next
        dw_ref[...] += dw_acc
        db_ref[...] += db_acc
        dgn_ref[...] += dgn_acc

        @pl.when((pl.program_id(0) == B - 1) & (pl.program_id(1) == NG - 1))
        def _():
            s1 = dgn_ref[...]
            s1 = s1 + pltpu.roll(s1, 128, 1)
            dgn_ref[...] = s1 + pltpu.roll(s1, 64, 1)

    tok = lambda w: pl.BlockSpec((1, G * C, w), lambda b, i: (b, NG - 1 - i, 0))
    return pl.pallas_call(
        body, name="gla_bwd", grid=(B, NG),
        in_specs=[tok(GLA_W), _full((128, 128)), _full((1, 128)), _full((1, 256)), tok(256),
                  pl.BlockSpec((1, G, 256, 128), lambda b, i: (b, NG - 1 - i, 0, 0)), tok(256)],
        out_specs=[tok(GLA_W), _full((128, 128)), _full((1, 128)), _full((1, 256))],
        out_shape=[jax.ShapeDtypeStruct((B, S, GLA_W), F32), jax.ShapeDtypeStruct((128, 128), F32),
                   jax.ShapeDtypeStruct((1, 128), F32), jax.ShapeDtypeStruct((1, 256), F32)],
        scratch_shapes=[pltpu.VMEM((256, 128), F32)],
        compiler_params=_cp(("arbitrary", "arbitrary")),
    )(gla_p, w_g2p, b_g2, gnw, raw, states, d_mix)


def _rms(x, w):
    rstd = lax.rsqrt(jnp.mean(x * x, axis=-1, keepdims=True) + EPS)
    xh = x * rstd
    return xh, rstd, xh * w


def _rms_bwd(dy, xh, rstd, w):
    dxh = dy * w
    return rstd * (dxh - xh * jnp.mean(dxh * xh, axis=-1, keepdims=True))


MLA_T = 256


def _mla_prep_fwd(mla_p, cos, sin, qnw, kvnw, w_uq, w_ukv):
    B, S, _ = mla_p.shape
    tm = min(S, 512)

    t = min(MLA_T, S)
    nt = tm // t

    def body(p_ref, c_ref, s_ref, qn_ref, kn_ref, wq_ref, wkv_ref, q_ref, k_ref, v_ref, kt_ref, vt_ref):
        p = p_ref[0]
        cs, sn = c_ref[0], s_ref[0]
        _, _, qn = _rms(p[:, 0:256], qn_ref[...])
        qpre = _mm(qn, wq_ref[...])
        _, _, kvn = _rms(p[:, 256:384], kn_ref[...])
        kv = _mm(kvn, wkv_ref[...])
        kpe = _rope128(p[:, 384:512], cs, sn)
        for h in range(8):
            sl = slice(128 * h, 128 * h + 128)
            q_ref[0, :, sl] = _rope128(qpre[:, sl], cs, sn).astype(_MXU)
            kh = kv[:, sl] + kpe
            k_ref[0, :, sl] = kh.astype(_MXU)
            kht = kh.T
            for n in range(nt):
                kt_ref[0, n, sl, :] = kht[:, n * t:(n + 1) * t].astype(_MXU)
        v_ref[0] = kv[:, 1024:1536].astype(_MXU)
        for pr in range(4):
            vht = kv[:, 1024 + 128 * pr:1152 + 128 * pr].T
            for n in range(nt):
                vt_ref[0, n, 128 * pr:128 * pr + 128, :] = vht[:, n * t:(n + 1) * t].astype(_MXU)

    tok = lambda w: pl.BlockSpec((1, tm, w), lambda b, i: (b, i, 0))
    tr = lambda w: pl.BlockSpec((1, nt, w, t), lambda b, i: (b, i, 0, 0))
    return pl.pallas_call(
        body, name="mla_prep_fwd", grid=(B, S // tm),
        in_specs=[tok(512), tok(128), tok(128), _full((1, 256)), _full((1, 128)), _full((256, 1024)),
                  _full((128, 1536))],
        out_specs=[tok(1024), tok(1024), tok(512), tr(1024), tr(512)],
        out_shape=[jax.ShapeDtypeStruct((B, S, 1024), _MXU), jax.ShapeDtypeStruct((B, S, 1024), _MXU),
                   jax.ShapeDtypeStruct((B, S, 512), _MXU), jax.ShapeDtypeStruct((B, S // t, 1024, t), _MXU),
                   jax.ShapeDtypeStruct((B, S // t, 512, t), _MXU)],
        compiler_params=_cp(("parallel", "parallel")),
    )(mla_p, cos, sin, qnw, kvnw, w_uq, w_ukv)


def _chunk_mask_t(t):
    kj = lax.broadcasted_iota(jnp.int32, (t, t), 0) // CHUNK
    qi = lax.broadcasted_iota(jnp.int32, (t, t), 1) // CHUNK
    return kj <= qi


def _mla_attn_fwd(q, k, vt):
    B, S, _ = q.shape
    t = min(MLA_T, S)
    nq = S // t

    def body(q_ref, k_ref, vt_ref, o_ref, lse_ref):
        i = pl.program_id(2)
        row = lax.broadcasted_iota(jnp.int32, (128, 1), 0)
        low = row < 64
        qb = q_ref[0]
        mask = _chunk_mask_t(t)

        def step(j, carry, masked):
            m_e, l_e, m_o, l_o, acc = carry
            kb = k_ref[0, pl.ds(pl.multiple_of(j * t, t), t), :]
            vtb = vt_ref[0, j]
            new = []
            pv = []
            for hh, (m_h, l_h) in enumerate(((m_e, l_e), (m_o, l_o))):
                s = _mm_nt(kb[:, 128 * hh:128 * hh + 128], qb[:, 128 * hh:128 * hh + 128]) * MLA_SCALE
                if masked:
                    s = jnp.where(mask, s, -jnp.inf)
                m_n = jnp.maximum(m_h, jnp.max(s, axis=0, keepdims=True))
                alpha = jnp.exp(m_h - m_n)
                pr = jnp.exp(s - m_n)
                l_n = alpha * l_h + jnp.sum(pr, axis=0, keepdims=True)
                vth = jnp.where(low if hh == 0 else ~low, vtb, jnp.zeros_like(vtb))
                pv.append(_mm(vth, pr))
                new.append((m_n, l_n, alpha))
            acc = acc * jnp.where(low, new[0][2], new[1][2]) + pv[0] + pv[1]
            return new[0][0], new[0][1], new[1][0], new[1][1], acc

        neg = jnp.full((1, t), -jnp.inf, F32)
        zero = jnp.zeros((1, t), F32)
        carry = (neg, zero, neg, zero, jnp.zeros((128, t), F32))
        carry = step(i, carry, True)
        carry = lax.fori_loop(0, i, lambda j, c: step(j, c, False), carry)
        m_e, l_e, m_o, l_o, acc = carry
        o_ref[0] = (acc / jnp.where(low, l_e, l_o)).T
        lse_ref[0, 0, 0, 0:1, :] = m_e + jnp.log(l_e)
        lse_ref[0, 0, 0, 1:2, :] = m_o + jnp.log(l_o)

    return pl.pallas_call(
        body, name="mla_attn_fwd", grid=(B, 4, nq),
        in_specs=[pl.BlockSpec((1, t, 256), lambda b, p, i: (b, i, p)),
                  pl.BlockSpec((1, S, 256), lambda b, p, i: (b, 0, p)),
                  pl.BlockSpec((1, nq, 128, t), lambda b, p, i: (b, 0, p, 0))],
        out_specs=[pl.BlockSpec((1, t, 128), lambda b, p, i: (b, i, p)),
                   pl.BlockSpec((1, 1, 1, 2, t), lambda b, p, i: (b, p, i, 0, 0))],
        out_shape=[jax.ShapeDtypeStruct((B, S, 512), F32), jax.ShapeDtypeStruct((B, 4, nq, 2, t), F32)],
        compiler_params=_cp(("parallel", "parallel", "arbitrary")),
    )(q, k, vt)


def _mla_gate_bwd(d_mix, o, mla_p):
    B, S, _ = o.shape
    tm = min(S, 512)
    t = min(MLA_T, S)
    nt = tm // t

    def body(dm_ref, o_ref, z_ref, do_ref, dz_ref, dl_ref):
        dm, ov, z = dm_ref[0], o_ref[0], z_ref[0]
        do = dm * _silu(z)
        dz_ref[0] = dm * ov * _dsilu(z)
        do_ref[0] = do.astype(_MXU)
        prod = do * ov
        for pr in range(4):
            pt = prod[:, 128 * pr:128 * pr + 128].T
            se = jnp.sum(pt[0:64], axis=0, keepdims=True)
            so = jnp.sum(pt[64:128], axis=0, keepdims=True)
            for n in range(nt):
                dl_ref[0, pr, n, 0:1, :] = se[:, n * t:(n + 1) * t]
                dl_ref[0, pr, n, 1:2, :] = so[:, n * t:(n + 1) * t]

    tok = lambda c: pl.BlockSpec((1, tm, 512), lambda b, i: (b, i, c))
    return pl.pallas_call(
        body, name="mla_gate_bwd", grid=(B, S // tm),
        in_specs=[tok(0), tok(0), tok(1)],
        out_specs=[tok(0), tok(0), pl.BlockSpec((1, 4, nt, 2, t), lambda b, i: (b, 0, i, 0, 0))],
        out_shape=[jax.ShapeDtypeStruct((B, S, 512), _MXU), jax.ShapeDtypeStruct((B, S, 512), F32),
                   jax.ShapeDtypeStruct((B, 4, S // t, 2, t), F32)],
        compiler_params=_cp(("parallel", "parallel")),
    )(d_mix, o, mla_p)


def _mla_attn_bwd(q, k, v, kt, do, lse, dl):
    B, S, _ = q.shape
    t = min(MLA_T, S)
    nk = S // t

    def body(q_ref, k_ref, v_ref, kt_ref, do_ref, lse_ref, dl_ref, dq_ref, dk_ref, dv_ref, dqt_sc):
        j = pl.program_id(2)

        @pl.when(j == 0)
        def _():
            dqt_sc[...] = jnp.zeros_like(dqt_sc)

        lane = lax.broadcasted_iota(jnp.int32, (1, 128), 1)
        low = lane < 64
        kb = k_ref[0]
        vb = v_ref[0]
        ktb = kt_ref[0, 0]
        mask = _chunk_mask_t(t)
        vhs = [jnp.where(low, vb, jnp.zeros_like(vb)), jnp.where(low, jnp.zeros_like(vb), vb)]

        def step(i, carry, masked):
            dk_e, dk_o, dv = carry
            rows = pl.ds(pl.multiple_of(i * t, t), t)
            qb = q_ref[0, rows, :]
            dob = do_ref[0, rows, :]
            lseb = lse_ref[0, 0, i]
            dlb = dl_ref[0, 0, i]
            dks = []
            for hh, dk_h in enumerate((dk_e, dk_o)):
                cols = slice(128 * hh, 128 * hh + 128)
                qh = qb[:, cols]
                s = _mm_nt(kb[:, cols], qh) * MLA_SCALE
                pr = jnp.exp(s - lseb[hh:hh + 1, :])
                if masked:
                    pr = jnp.where(mask, pr, 0.0)
                doh = jnp.where(low if hh == 0 else ~low, dob, jnp.zeros_like(dob))
                dv = dv + _mm(pr, doh)
                dp = _mm_nt(vhs[hh], dob)
                ds = pr * (dp - dlb[hh:hh + 1, :])
                dqt_sc[i, cols, :] += _mm(ktb[cols, :], ds)
                dks.append(dk_h + _mm(ds, qh))
            return dks[0], dks[1], dv

        carry = (jnp.zeros((t, 128), F32), jnp.zeros((t, 128), F32), jnp.zeros((t, 128), F32))
        carry = step(j, carry, True)
        carry = lax.fori_loop(j + 1, nk, lambda i, c: step(i, c, False), carry)
        dk_ref[0, :, 0:128] = carry[0] * MLA_SCALE
        dk_ref[0, :, 128:256] = carry[1] * MLA_SCALE
        dv_ref[0] = carry[2]

        @pl.when(j == nk - 1)
        def _():
            for i in range(nk):
                dq_ref[0, i * t:(i + 1) * t, :] = dqt_sc[i].T * MLA_SCALE

    seq = lambda w: pl.BlockSpec((1, S, w), lambda b, p, j: (b, 0, p))
    blk = lambda w: pl.BlockSpec((1, t, w), lambda b, p, j: (b, j, p))
    stat = pl.BlockSpec((1, 1, nk, 2, t), lambda b, p, j: (b, p, 0, 0, 0))
    return pl.pallas_call(
        body, name="mla_attn_bwd", grid=(B, 4, nk),
        in_specs=[seq(256), blk(256), blk(128), pl.BlockSpec((1, 1, 256, t), lambda b, p, j: (b, j, p, 0)),
                  seq(128), stat, stat],
        out_specs=[seq(256), blk(256), blk(128)],
        out_shape=[jax.ShapeDtypeStruct((B, S, 1024), F32), jax.ShapeDtypeStruct((B, S, 1024), F32),
                   jax.ShapeDtypeStruct((B, S, 512), F32)],
        scratch_shapes=[pltpu.VMEM((nk, 256, t), F32)],
        compiler_params=_cp(("parallel", "parallel", "arbitrary")),
    )(q, k, v, kt, do, lse, dl)


def _mla_prep_bwd(mla_p, cos, sin, qnw, kvnw, w_uq, w_ukv, dq, dk, dv):
    B, S, _ = mla_p.shape
    tm = min(S, 512)

    def body(p_ref, c_ref, s_ref, qn_ref, kn_ref, wq_ref, wkv_ref, dq_ref, dk_ref, dv_ref,
             dp_ref, dwq_ref, dwkv_ref, dqn_ref, dkn_ref):
        first = (pl.program_id(0) == 0) & (pl.program_id(1) == 0)

        @pl.when(first)
        def _():
            dwq_ref[...] = jnp.zeros_like(dwq_ref)
            dwkv_ref[...] = jnp.zeros_like(dwkv_ref)
            dqn_ref[...] = jnp.zeros_like(dqn_ref)
            dkn_ref[...] = jnp.zeros_like(dkn_ref)

        p = p_ref[0]
        cs, sn = c_ref[0], s_ref[0]
        lane = lax.broadcasted_iota(jnp.int32, (1, 128), 1)
        pe = (lane >= 64) & (lane < 96)
        qh, q_rstd, qn = _rms(p[:, 0:256], qn_ref[...])
        kvh, kv_rstd, kvn = _rms(p[:, 256:384], kn_ref[...])
        dqv = dq_ref[0]
        dkv = dk_ref[0]
        dqpre = jnp.concatenate(
            [_rope128_t(dqv[:, 128 * h:128 * h + 128], cs, sn) for h in range(8)], axis=1)
        dkpe = jnp.zeros((tm, 128), F32)
        for h in range(8):
            dkpe = dkpe + jnp.where(pe, dkv[:, 128 * h:128 * h + 128], 0.0)
        dkr = _rope128_t(dkpe, cs, sn)
        dkv_all = jnp.concatenate([dkv, dv_ref[0]], axis=1)
        d_qn = _mm_nt(dqpre, wq_ref[...])
        d_kvn = _mm_nt(dkv_all, wkv_ref[...])
        dwq_ref[...] += _mm_tn(qn, dqpre)
        dwkv_ref[...] += _mm_tn(kvn, dkv_all)
        dqn_ref[...] += jnp.sum(d_qn * qh, axis=0, keepdims=True)
        dkn_ref[...] += jnp.sum(d_kvn * kvh, axis=0, keepdims=True)
        dp_ref[0] = jnp.concatenate([_rms_bwd(d_qn, qh, q_rstd, qn_ref[...]),
                                     _rms_bwd(d_kvn, kvh, kv_rstd, kn_ref[...]), dkr], axis=1)

    tok = lambda w: pl.BlockSpec((1, tm, w), lambda b, i: (b, i, 0))
    return pl.pallas_call(
        body, name="mla_prep_bwd", grid=(B, S // tm),
        in_specs=[tok(512), tok(128), tok(128), _full((1, 256)), _full((1, 128)), _full((256, 1024)),
                  _full((128, 1536)), tok(1024), tok(1024), tok(512)],
        out_specs=[tok(512), _full((256, 1024)), _full((128, 1536)), _full((1, 256)), _full((1, 128))],
        out_shape=[jax.ShapeDtypeStruct((B, S, 512), F32), jax.ShapeDtypeStruct((256, 1024), F32),
                   jax.ShapeDtypeStruct((128, 1536), F32), jax.ShapeDtypeStruct((1, 256), F32),
                   jax.ShapeDtypeStruct((1, 128), F32)],
        compiler_params=_cp(("arbitrary", "arbitrary")),
    )(mla_p, cos, sin, qnw, kvnw, w_uq, w_ukv, dq, dk, dv)


def _out_fwd(x, gate, r_g, o_mla, mla_p, g_g, w_out):
    B, S, D = x.shape
    tm = min(S, 512)

    def body(x_ref, g_ref, r_ref, o_ref, z_ref, gg_ref, w_ref, xn_ref, y_ref, mm_ref):
        mm = (o_ref[0] * _silu(z_ref[0])).astype(_MXU)
        mm_ref[0] = mm
        y = (jnp.dot(r_ref[0], w_ref[0:256, :], preferred_element_type=F32)
             + jnp.dot(mm, w_ref[256:768, :], preferred_element_type=F32)
             + jnp.dot(gg_ref[0], w_ref[768:1024, :], preferred_element_type=F32))
        y_ref[0] = y
        xn_ref[0] = x_ref[0] + g_ref[0] * y

    tok = lambda w, c=0: pl.BlockSpec((1, tm, w), lambda b, i: (b, i, c))
    return pl.pallas_call(
        body, name="out_fwd", grid=(B, S // tm),
        in_specs=[tok(D), pl.BlockSpec((1, 1, D), lambda b, i: (b, 0, 0)), tok(256), tok(512), tok(512, 1),
                  tok(256), _full((D, D))],
        out_specs=[tok(D), tok(D), tok(512)],
        out_shape=[jax.ShapeDtypeStruct((B, S, D), F32), jax.ShapeDtypeStruct((B, S, D), F32),
                   jax.ShapeDtypeStruct((B, S, 512), _MXU)],
        compiler_params=_cp(("parallel", "parallel")),
    )(x, gate, r_g, o_mla, mla_p, g_g, w_out)


def _out_bwd(dx, y, gate, r_g, mm, g_g, w_out):
    B, S, D = dx.shape
    tm = min(S, 512)

    def body(dx_ref, y_ref, g_ref, r_ref, mm_ref, gg_ref, w_ref, dr_ref, dmm_ref, dg_ref, dw_ref, dgate_ref):
        first = (pl.program_id(0) == 0) & (pl.program_id(1) == 0)

        @pl.when(first)
        def _():
            dw_ref[...] = jnp.zeros_like(dw_ref)

        @pl.when(pl.program_id(1) == 0)
        def _():
            dgate_ref[...] = jnp.zeros_like(dgate_ref)

        dxv = dx_ref[0]
        dgate_ref[0] += jnp.sum(dxv * y_ref[0], axis=0, keepdims=True)
        dy = (dxv * g_ref[0]).astype(_MXU)
        dr_ref[0] = _mm_nt(dy, w_ref[0:256, :])
        dmm_ref[0] = _mm_nt(dy, w_ref[256:768, :])
        dg_ref[0] = _mm_nt(dy, w_ref[768:1024, :])
        dw_ref[0:256, :] += _mm_tn(r_ref[0], dy)
        dw_ref[256:768, :] += _mm_tn(mm_ref[0], dy)
        dw_ref[768:1024, :] += _mm_tn(gg_ref[0], dy)

    tok = lambda w: pl.BlockSpec((1, tm, w), lambda b, i: (b, i, 0))
    per_seq = pl.BlockSpec((1, 1, D), lambda b, i: (b, 0, 0))
    return pl.pallas_call(
        body, name="out_bwd", grid=(B, S // tm),
        in_specs=[tok(D), tok(D), per_seq, tok(256), tok(512), tok(256), _full((D, D))],
        out_specs=[tok(256), tok(512), tok(256), _full((D, D)), per_seq],
        out_shape=[jax.ShapeDtypeStruct((B, S, 256), F32), jax.ShapeDtypeStruct((B, S, 512), F32),
                   jax.ShapeDtypeStruct((B, S, 256), F32), jax.ShapeDtypeStruct((D, D), F32),
                   jax.ShapeDtypeStruct((B, 1, D), F32)],
        compiler_params=_cp(("arbitrary", "arbitrary")),
    )(dx, y, gate, r_g, mm, g_g, w_out)


def _proj_bwd_x(x, shift, scale, nw, w_arr, d_ret, d_mla, d_mz, d_gla, dx_out):
    B, S, D = x.shape
    tm = min(S, 256)

    def body(x_ref, sc_ref, nw_ref, w_ref, dr_ref, dm_ref, dz_ref, dg_ref, dxo_ref,
             dx_ref, dp_ref, dsh_ref, dsc_ref, dnw_ref):
        first = (pl.program_id(0) == 0) & (pl.program_id(1) == 0)

        @pl.when(first)
        def _():
            dnw_ref[...] = jnp.zeros_like(dnw_ref)

        @pl.when(pl.program_id(1) == 0)
        def _():
            dsh_ref[...] = jnp.zeros_like(dsh_ref)
            dsc_ref[...] = jnp.zeros_like(dsc_ref)

        dp = jnp.concatenate([dr_ref[0], dm_ref[0], dz_ref[0], dg_ref[0]], axis=1).astype(_MXU)
        dp_ref[0] = dp
        dh = lax.dot_general(dp, w_ref[...], (((1,), (1,)), ((), ())), preferred_element_type=F32)
        xv = x_ref[0]
        rstd = lax.rsqrt(jnp.mean(xv * xv, axis=-1, keepdims=True) + EPS)
        xh = xv * rstd
        nwv = nw_ref[...]
        mod = 1.0 + sc_ref[0]
        dsh_ref[0] += jnp.sum(dh, axis=0, keepdims=True)
        dsc_ref[0] += jnp.sum(dh * xh * nwv, axis=0, keepdims=True)
        dnw_ref[...] += jnp.sum(dh * xh * mod, axis=0, keepdims=True)
        dxh = dh * nwv * mod
        dx_ref[0] = dxo_ref[0] + rstd * (dxh - xh * jnp.mean(dxh * xh, axis=-1, keepdims=True))

    tok = lambda w: pl.BlockSpec((1, tm, w), lambda b, i: (b, i, 0))
    per_seq = pl.BlockSpec((1, 1, D), lambda b, i: (b, 0, 0))
    return pl.pallas_call(
        body, name="proj_bwd_x", grid=(B, S // tm),
        in_specs=[tok(D), per_seq, _full((1, D)), _full((D, ARR_W)), tok(RET_W), tok(512), tok(512),
                  tok(GLA_W), tok(D)],
        out_specs=[tok(D), tok(ARR_W), per_seq, per_seq, _full((1, D))],
        out_shape=[jax.ShapeDtypeStruct((B, S, D), F32), jax.ShapeDtypeStruct((B, S, ARR_W), _MXU),
                   jax.ShapeDtypeStruct((B, 1, D), F32), jax.ShapeDtypeStruct((B, 1, D), F32),
                   jax.ShapeDtypeStruct((1, D), F32)],
        compiler_params=_cp(("arbitrary", "arbitrary")),
    )(x, scale, nw, w_arr, d_ret, d_mla, d_mz, d_gla, dx_out)


def _proj_bwd_w(h, dp):
    B, S, D = h.shape
    tm = min(S, 512)
    tn = 128 * 23 // 1
    assert ARR_W == tn

    def body(h_ref, dp_ref, dw_ref):
        first = (pl.program_id(0) == 0) & (pl.program_id(1) == 0)

        @pl.when(first)
        def _():
            dw_ref[...] = jnp.zeros_like(dw_ref)

        dw_ref[...] += lax.dot_general(h_ref[0], dp_ref[0], (((0,), (0,)), ((), ())),
                                       preferred_element_type=F32)

    tok = lambda w: pl.BlockSpec((1, tm, w), lambda b, i: (b, i, 0))
    return pl.pallas_call(
        body, name="proj_bwd_w", grid=(B, S // tm),
        in_specs=[tok(D), tok(ARR_W)],
        out_specs=_full((D, ARR_W)), out_shape=jax.ShapeDtypeStruct((D, ARR_W), F32),
        compiler_params=_cp(("arbitrary", "arbitrary"), 56),
    )(h, dp)


def _final_loss(x, fw, target):
    B, S, D = x.shape
    tm = min(S, 512)

    def body(x_ref, fw_ref, t_ref, dx_ref, loss_ref, dfw_ref):
        first = (pl.program_id(0) == 0) & (pl.program_id(1) == 0)

        @pl.when(first)
        def _():
            loss_ref[...] = jnp.zeros_like(loss_ref)
            dfw_ref[...] = jnp.zeros_like(dfw_ref)

        xv = x_ref[0]
        fwv = fw_ref[...]
        rstd = lax.rsqrt(jnp.mean(xv * xv, axis=-1, keepdims=True) + EPS)
        xh = xv * rstd
        err = xh * fwv - t_ref[0]
        loss_ref[...] += 0.5 * jnp.sum(jnp.mean(err * err, axis=-1, keepdims=True), axis=0, keepdims=True)
        dy = err * (1.0 / D)
        dfw_ref[...] += jnp.sum(dy * xh, axis=0, keepdims=True)
        dxh = dy * fwv
        dx_ref[0] = rstd * (dxh - xh * jnp.mean(dxh * xh, axis=-1, keepdims=True))

    tok = pl.BlockSpec((1, tm, D), lambda b, i: (b, i, 0))
    return pl.pallas_call(
        body, name="final_loss", grid=(B, S // tm),
        in_specs=[tok, _full((1, D)), tok],
        out_specs=[tok, _full((1, 1)), _full((1, D))],
        out_shape=[jax.ShapeDtypeStruct((B, S, D), F32), jax.ShapeDtypeStruct((1, 1), F32),
                   jax.ShapeDtypeStruct((1, D), F32)],
        compiler_params=_cp(("arbitrary", "arbitrary")),
    )(x, fw, target)


def _local_step(x, pos3, mod, loss_target, small, w_in_a, w_uq_a, w_ukv_a, w_out_b):
    B, S, D = x.shape
    cr, sr, cm, sm = _rope_tables(pos3)
    saved = []
    for l in range(DEPTH):
        shift = mod[l, :, 0:D].reshape(B, 1, D)
        scale = mod[l, :, D:2 * D].reshape(B, 1, D)
        gate = mod[l, :, 2 * D:3 * D].reshape(B, 1, D)
        nw = small["norm_w"][l].reshape(1, D)
        qnw = small["mla_q_norm"][l].reshape(1, 256)
        kvnw = small["mla_kv_norm"][l].reshape(1, 128)
        w_g2p = jnp.pad(small["gla_w_g2"][l], ((0, 112), (0, 0)))
        b_g2 = small["gla_b_g2"][l].reshape(1, 128)
        gnw = jnp.tile(small["gla_norm"][l], 4).reshape(1, 256)
        ret_p, mla_p, gla_p, h = _proj_fwd(x, shift, scale, nw, w_in_a[l])
        r_g, r_raw, r_st = _ret_fwd(ret_p, cr, sr)
        q, k, v, kt, vt = _mla_prep_fwd(mla_p, cm, sm, qnw, kvnw, w_uq_a[l], w_ukv_a[l])
        o_mla, lse = _mla_attn_fwd(q, k, vt)
        g_g, g_raw, g_st = _gla_fwd(gla_p, w_g2p, b_g2, gnw)
        x_new, y, mm = _out_fwd(x, gate, r_g, o_mla, mla_p, g_g, w_out_b[l])
        saved.append(dict(x=x, shift=shift, scale=scale, gate=gate, nw=nw, qnw=qnw, kvnw=kvnw, w_g2p=w_g2p,
                          b_g2=b_g2, gnw=gnw, ret_p=ret_p, mla_p=mla_p, gla_p=gla_p, h=h, r_g=r_g, r_raw=r_raw,
                          r_st=r_st, q=q, k=k, v=v, kt=kt, o_mla=o_mla, lse=lse, g_g=g_g, g_raw=g_raw, g_st=g_st,
                          y=y, mm=mm))
        x = x_new

    dx, loss, d_fw = _final_loss(x, small["final_norm"].reshape(1, D), loss_target)
    grads = dict(final_norm=d_fw.reshape(D))
    per_layer = []
    for l in reversed(range(DEPTH)):
        s = saved[l]
        d_r, d_mm, d_g, dw_out, d_gate = _out_bwd(dx, s["y"], s["gate"], s["r_g"], s["mm"], s["g_g"], w_out_b[l])
        d_ret = _ret_bwd(s["ret_p"], cr, sr, s["r_raw"], s["r_st"], d_r)
        do, d_mz, dl = _mla_gate_bwd(d_mm, s["o_mla"], s["mla_p"])
        dq, dk, dv = _mla_attn_bwd(s["q"], s["k"], s["v"], s["kt"], do, s["lse"], dl)
        d_mla, dw_uq, dw_ukv, d_qnw, d_kvnw = _mla_prep_bwd(
            s["mla_p"], cm, sm, s["qnw"], s["kvnw"], w_uq_a[l], w_ukv_a[l], dq, dk, dv)
        d_gla, dw_g2p, db_g2, d_gnw = _gla_bwd(s["gla_p"], s["w_g2p"], s["b_g2"], s["gnw"], s["g_raw"],
                                                s["g_st"], d_g)
        dx, dp, d_shift, d_scale, d_nw = _proj_bwd_x(s["x"], s["shift"], s["scale"], s["nw"], w_in_a[l],
                                                     d_ret, d_mla, d_mz, d_gla, dx)
        dw_in = _proj_bwd_w(s["h"], dp)
        per_layer.append(dict(
            d_mod=jnp.concatenate([d_shift, d_scale, d_gate], axis=2).reshape(B, 3 * D),
            norm_w=d_nw.reshape(D), mla_q_norm=d_qnw.reshape(256), mla_kv_norm=d_kvnw.reshape(128),
            gla_w_g2=dw_g2p[0:16], gla_b_g2=db_g2.reshape(128), gla_norm256=d_gnw.reshape(256),
            w_in_a=dw_in, w_uq_a=dw_uq, w_ukv_a=dw_ukv, w_out=dw_out))
    per_layer = per_layer[::-1]
    for name in per_layer[0]:
        grads[name] = jnp.stack([per_layer[l][name] for l in range(DEPTH)])
    return loss, dx, grads


def _exchange(arrs, gather, name):
    n = len(arrs)
    out_shape = [jax.ShapeDtypeStruct(((N_DEV,) + a.shape) if g else a.shape, a.dtype)
                 for a, g in zip(arrs, gather)]

    def body(*refs):
        ins, outs = refs[:n], refs[n:2 * n]
        send_sems, recv_sems, local_sems = refs[2 * n:]
        ix, iy, ic = lax.axis_index("x"), lax.axis_index("y"), lax.axis_index("c")
        me = 4 * ix + 2 * iy + ic
        copies = []
        for a in range(n):
            mine = ins[a] if gather[a] else ins[a].at[me]
            loc = pltpu.make_async_copy(mine, outs[a].at[me], local_sems.at[a])
            loc.start()
            copies.append(loc)
            for d in range(1, N_DEV):
                px = 1 - ix if d & 4 else ix
                py = 1 - iy if d & 2 else iy
                pc = 1 - ic if d & 1 else ic
                src = ins[a] if gather[a] else ins[a].at[4 * px + 2 * py + pc]
                cp = pltpu.make_async_remote_copy(
                    src_ref=src, dst_ref=outs[a].at[me], send_sem=send_sems.at[a, d - 1],
                    recv_sem=recv_sems.at[a, d - 1], device_id=(px, py, pc), device_id_type=pl.DeviceIdType.MESH)
                cp.start()
                copies.append(cp)
        for cp in copies:
            cp.wait()

    any_spec = pl.BlockSpec(memory_space=pl.ANY)
    outs = pl.pallas_call(
        body, name=name, in_specs=[any_spec] * n, out_specs=[any_spec] * n, out_shape=out_shape,
        scratch_shapes=[pltpu.SemaphoreType.DMA((n, N_DEV - 1)), pltpu.SemaphoreType.DMA((n, N_DEV - 1)),
                        pltpu.SemaphoreType.DMA((n,))],
    )(*arrs)
    return list(outs)


def _ada_fwd(c_all, ada_w, ada_b_cols):
    nb, D = c_all.shape
    cols = ada_w.shape[2]

    def body(c_ref, w_ref, b_ref, out_ref):
        ca = _silu(c_ref[...])
        for l in range(DEPTH):
            out_ref[l] = _mm(ca, w_ref[l]) + b_ref[l:l + 1, :]

    return pl.pallas_call(
        body, name="ada_fwd", out_shape=jax.ShapeDtypeStruct((DEPTH, nb, cols), F32),
        in_specs=[pl.BlockSpec(memory_space=pltpu.VMEM)] * 3, out_specs=pl.BlockSpec(memory_space=pltpu.VMEM),
        compiler_params=pltpu.CompilerParams(vmem_limit_bytes=32 * VMEM_MB),
    )(c_all, ada_w, ada_b_cols)


def _ada_bwd(c_all, d_mod_cols):
    nb, D = c_all.shape
    cols = d_mod_cols.shape[2]

    def body(c_ref, dm_ref, out_ref):
        ca = _silu(c_ref[...])
        for l in range(DEPTH):
            out_ref[l] = _mm_tn(ca, dm_ref[l])

    return pl.pallas_call(
        body, name="ada_bwd", out_shape=jax.ShapeDtypeStruct((DEPTH, D, cols), F32),
        in_specs=[pl.BlockSpec(memory_space=pltpu.VMEM)] * 2, out_specs=pl.BlockSpec(memory_space=pltpu.VMEM),
        compiler_params=pltpu.CompilerParams(vmem_limit_bytes=32 * VMEM_MB),
    )(c_all, d_mod_cols)


def _sum_adamw(parts, w, m, v, name):
    P, R, C = parts.shape
    tr = 256 if (R % 256 == 0 and R > 256) else R

    def body(p_ref, w_ref, m_ref, v_ref, g_ref, d_ref, nm_ref, nv_ref):
        g = p_ref[0].astype(F32)
        for k in range(1, P):
            g = g + p_ref[k].astype(F32)
        g_ref[...] = g
        nm = ADAM_B1 * m_ref[...] + (1.0 - ADAM_B1) * g
        nv = ADAM_B2 * v_ref[...] + (1.0 - ADAM_B2) * (g * g)
        nm_ref[...] = nm
        nv_ref[...] = nv
        m_hat = nm / (1.0 - ADAM_B1 ** ADAM_STEP)
        v_hat = nv / (1.0 - ADAM_B2 ** ADAM_STEP)
        d_ref[...] = -ADAM_LR * (m_hat / (jnp.sqrt(v_hat) + ADAM_EPS) + ADAM_WD * w_ref[...])

    blk = pl.BlockSpec((tr, C), lambda i: (i, 0))
    shp = jax.ShapeDtypeStruct((R, C), F32)
    return pl.pallas_call(
        body, name=name, grid=(R // tr,),
        in_specs=[pl.BlockSpec((P, tr, C), lambda i: (0, i, 0)), blk, blk, blk],
        out_specs=[blk, blk, blk, blk], out_shape=[shp, shp, shp, shp],
        compiler_params=_cp(("parallel",)),
    )(parts, w, m, v)


SMALL = [("norm_w", DEPTH * 1024), ("mla_q_norm", DEPTH * 256), ("mla_kv_norm", DEPTH * 128),
         ("gla_w_g2", DEPTH * 16 * 128), ("gla_b_g2", DEPTH * 128), ("gla_norm", DEPTH * 64), ("final_norm", 1024)]
SMALL_ROWS = 72


def _pack_small(first_row, vals):
    flat = [first_row.reshape(128)] + [vals[n].reshape(-1) for n, _ in SMALL]
    used = 128 + sum(s for _, s in SMALL)
    flat.append(jnp.zeros((SMALL_ROWS * 128 - used,), F32))
    return jnp.concatenate(flat).reshape(SMALL_ROWS, 128)


def _unpack_small(packed, shapes):
    flat = packed.reshape(-1)
    out, off = {}, 128
    for n, s in SMALL:
        out[n] = flat[off:off + s].reshape(shapes[n])
        off += s
    return out


WEIGHTS = ["norm_w", "ada_w", "ada_b", "w_in", "mla_q_norm", "w_uq", "mla_kv_norm", "w_ukv", "gla_w_g2",
           "gla_b_g2", "gla_norm", "w_out", "final_norm"]


def kernel(x, c, positions, norm_w, ada_w, ada_b, w_in, mla_q_norm, w_uq, mla_kv_norm, w_ukv, gla_w_g2, gla_b_g2, gla_norm, w_out, final_norm, loss_target, m_norm_w, m_ada_w, m_ada_b, m_w_in, m_mla_q_norm, m_w_uq, m_mla_kv_norm, m_w_ukv, m_gla_w_g2, m_gla_b_g2, m_gla_norm, m_w_out, m_final_norm, v_norm_w, v_ada_w, v_ada_b, v_w_in, v_mla_q_norm, v_w_uq, v_mla_kv_norm, v_w_ukv, v_gla_w_g2, v_gla_b_g2, v_gla_norm, v_w_out, v_final_norm):
    w = dict(norm_w=norm_w, ada_w=ada_w, ada_b=ada_b, w_in=w_in, mla_q_norm=mla_q_norm, w_uq=w_uq,
             mla_kv_norm=mla_kv_norm, w_ukv=w_ukv, gla_w_g2=gla_w_g2, gla_b_g2=gla_b_g2, gla_norm=gla_norm,
             w_out=w_out, final_norm=final_norm)
    m = dict(norm_w=m_norm_w, ada_w=m_ada_w, ada_b=m_ada_b, w_in=m_w_in, mla_q_norm=m_mla_q_norm, w_uq=m_w_uq,
             mla_kv_norm=m_mla_kv_norm, w_ukv=m_w_ukv, gla_w_g2=m_gla_w_g2, gla_b_g2=m_gla_b_g2,
             gla_norm=m_gla_norm, w_out=m_w_out, final_norm=m_final_norm)
    v = dict(norm_w=v_norm_w, ada_w=v_ada_w, ada_b=v_ada_b, w_in=v_w_in, mla_q_norm=v_mla_q_norm, w_uq=v_w_uq,
             mla_kv_norm=v_mla_kv_norm, w_ukv=v_w_ukv, gla_w_g2=v_gla_w_g2, gla_b_g2=v_gla_b_g2,
             gla_norm=v_gla_norm, w_out=v_w_out, final_norm=v_final_norm)
    B, S, D = x.shape
    me = 4 * lax.axis_index("x") + 2 * lax.axis_index("y") + lax.axis_index("c")
    ada_cols = ada_w.shape[2]
    cast = lambda a: a.astype(_MXU)

    c_g, w_in_g, w_uq_g, w_ukv_g, w_out_g = _exchange(
        [c, cast(w_in), cast(w_uq), cast(w_ukv), cast(w_out)], [True] * 5, "gather_weights")
    c_all = c_g.reshape(N_DEV * B, D)
    w_in_full = jnp.transpose(w_in_g, (1, 2, 0, 3)).reshape(DEPTH, D, -1)
    w_uq_full = jnp.transpose(w_uq_g, (1, 2, 0, 3)).reshape(DEPTH, 256, -1)
    w_ukv_full = jnp.transpose(w_ukv_g, (1, 2, 0, 3)).reshape(DEPTH, 128, -1)
    w_out_full = jnp.transpose(w_out_g, (1, 0, 2, 3)).reshape(DEPTH, D, D)
    w_in_a = jnp.stack([_arrange_w_in(w_in_full[l]) for l in range(DEPTH)])
    w_uq_a = jnp.stack([_arrange_w_uq(w_uq_full[l]) for l in range(DEPTH)])
    w_ukv_a = jnp.stack([_arrange_w_ukv(w_ukv_full[l]) for l in range(DEPTH)])

    ada_b_cols = lax.dynamic_slice(ada_b, (0, me * ada_cols), (DEPTH, ada_cols))
    mod_cols = _ada_fwd(c_all, ada_w, ada_b_cols)
    mod_send = jnp.transpose(mod_cols.reshape(DEPTH, N_DEV, B, ada_cols), (1, 0, 2, 3))
    (mod_recv,) = _exchange([mod_send], [False], "scatter_mod")
    mod = jnp.transpose(mod_recv, (1, 2, 0, 3)).reshape(DEPTH, B, 3 * D)

    small_w = {n: w[n] for n, _ in SMALL}
    loss, grad_x, g = _local_step(x, positions.reshape(B, S, 1), mod, loss_target, small_w,
                                  w_in_a, w_uq_a, w_ukv_a, w_out_full)

    d_mod = g["d_mod"]
    part = dict(norm_w=g["norm_w"], mla_q_norm=g["mla_q_norm"], mla_kv_norm=g["mla_kv_norm"],
                gla_w_g2=g["gla_w_g2"], gla_b_g2=g["gla_b_g2"], gla_norm=g["gla_norm256"][:, 0:64],
                final_norm=g["final_norm"])
    small_part = _pack_small(jnp.pad(loss.reshape(1), (0, 127)), part)
    dw_in = jnp.stack([_unarrange_w_in(g["w_in_a"][l]) for l in range(DEPTH)])
    dw_uq = jnp.stack([_unarrange_w_uq(g["w_uq_a"][l]) for l in range(DEPTH)])
    dw_ukv = jnp.stack([_unarrange_w_ukv(g["w_ukv_a"][l]) for l in range(DEPTH)])
    to_cols = lambda a: jnp.transpose(a.reshape(a.shape[0], a.shape[1], N_DEV, -1), (2, 0, 1, 3)).astype(jnp.bfloat16)
    dw_out_send = jnp.transpose(g["w_out"].reshape(DEPTH, N_DEV, D // N_DEV, D), (1, 0, 2, 3)).astype(jnp.bfloat16)
    d_mod_g, small_g, dw_in_g, dw_uq_g, dw_ukv_g, dw_out_g = _exchange(
        [d_mod, small_part, to_cols(dw_in), to_cols(dw_uq), to_cols(dw_ukv), dw_out_send],
        [True, True, False, False, False, False], "exchange_grads")

    d_mod_all = jnp.transpose(d_mod_g, (1, 0, 2, 3)).reshape(DEPTH, N_DEV * B, 3 * D)
    d_mod_cols = lax.dynamic_slice(d_mod_all, (0, 0, me * ada_cols), (DEPTH, N_DEV * B, ada_cols))
    g_ada_w = _ada_bwd(c_all, d_mod_cols)

    res = {}

    def update(name, parts2d):
        shp = w[name].shape
        two = lambda a: a.reshape(parts2d.shape[1:])
        out = _sum_adamw(parts2d, two(w[name]), two(m[name]), two(v[name]), "adamw_" + name)
        res[name] = [o.reshape(shp) for o in out]

    update("ada_w", g_ada_w.reshape(1, DEPTH * D, ada_cols))
    update("ada_b", jnp.transpose(d_mod_g, (0, 2, 1, 3)).reshape(N_DEV * B, DEPTH * 3 * D // 128, 128))
    update("w_in", dw_in_g.reshape(N_DEV, DEPTH * D, -1))
    update("w_uq", dw_uq_g.reshape(N_DEV, DEPTH * 256, -1))
    update("w_ukv", dw_ukv_g.reshape(N_DEV, DEPTH * 128, -1))
    update("w_out", dw_out_g.reshape(N_DEV, DEPTH * (D // N_DEV), D))
    zero_row = jnp.zeros((128,), F32)
    small_out = _sum_adamw(small_g, _pack_small(zero_row, small_w), _pack_small(zero_row, {n: m[n] for n, _ in SMALL}),
                           _pack_small(zero_row, {n: v[n] for n, _ in SMALL}), "adamw_small")
    shapes = {n: w[n].shape for n, _ in SMALL}
    unpacked = [_unpack_small(o, shapes) for o in small_out]
    for n, _ in SMALL:
        res[n] = [u[n] for u in unpacked]
    loss_out = small_out[0][0, 0]
    return (loss_out, grad_x, *[res[n][0] for n in WEIGHTS], *[res[n][1] for n in WEIGHTS],
            *[res[n][2] for n in WEIGHTS], *[res[n][3] for n in WEIGHTS])
```

```python
import functools
import math

import numpy as np
import jax
import jax.numpy as jnp
from jax import lax
from jax.experimental import pallas as pl
from jax.experimental.pallas import tpu as pltpu

F32 = jnp.float32
_MXU = jnp.bfloat16

D_MODEL = 1024
DEPTH = 2
CHUNK = 64
EPS = 1e-6
ROPE_THETA = 10000.0
N_DEV = 8

MLA_SCALE = 96.0 ** -0.5
RET_KSCALE = 64.0 ** -0.5
GLA_KSCALE = 32.0 ** -0.5
GLA_TAU = 16.0

ADAM_LR = 0.001
ADAM_B1 = 0.9
ADAM_B2 = 0.999
ADAM_EPS = 1e-08
ADAM_WD = 0.01
ADAM_STEP = 10

RET_W, MLA_W, GLA_W = 1024, 1024, 896
ARR_W = RET_W + MLA_W + GLA_W
VMEM_MB = 1024 * 1024


def _cp(sem, vmem_mb=48):
    return pltpu.CompilerParams(dimension_semantics=sem, vmem_limit_bytes=vmem_mb * VMEM_MB)


def _mm(a, b):
    return jnp.dot(a.astype(_MXU), b.astype(_MXU), preferred_element_type=F32)


def _mm_nt(a, b):
    return lax.dot_general(a.astype(_MXU), b.astype(_MXU), (((1,), (1,)), ((), ())),
                           preferred_element_type=F32)


def _mm_tn(a, b):
    return lax.dot_general(a.astype(_MXU), b.astype(_MXU), (((0,), (0,)), ((), ())),
                           preferred_element_type=F32)


def _mm_f32(a, b):
    return jnp.dot(a, b, precision=lax.Precision.HIGHEST, preferred_element_type=F32)


def _sig(z):
    return 1.0 / (1.0 + jnp.exp(-z))


def _silu(z):
    return z * _sig(z)


def _dsilu(z):
    s = _sig(z)
    return s * (1.0 + z * (1.0 - s))


def _full(shape):
    nd = len(shape)
    return pl.BlockSpec(shape, lambda *_: (0,) * nd)


def _qk_perm(blk):
    r = blk.shape[0]
    return jnp.transpose(blk.reshape(r, 4, 2, 32), (0, 2, 1, 3)).reshape(r, 256)


def _qk_unperm(blk):
    r = blk.shape[0]
    return jnp.transpose(blk.reshape(r, 2, 4, 32), (0, 2, 1, 3)).reshape(r, 256)


def _arrange_w_in(w):
    z = lambda n: jnp.zeros((w.shape[0], n), w.dtype)
    ret = [_qk_perm(w[:, 0:256]), _qk_perm(w[:, 256:512]), w[:, 512:768], w[:, 768:1024]]
    mla = [w[:, 1024:1280], w[:, 1280:1408], z(64), w[:, 1408:1440], z(32), w[:, 1440:1952]]
    gla = [w[:, 1952:2080], w[:, 2080:2208], w[:, 2208:2464], w[:, 2464:2480], z(112), w[:, 2480:2736]]
    return jnp.concatenate(ret + mla + gla, axis=1)


def _unarrange_w_in(a):
    m, g = RET_W, RET_W + MLA_W
    parts = [_qk_unperm(a[:, 0:256]), _qk_unperm(a[:, 256:512]), a[:, 512:1024],
             a[:, m:m + 384], a[:, m + 448:m + 480], a[:, m + 512:m + 1024],
             a[:, g:g + 528], a[:, g + 640:g + 896]]
    return jnp.concatenate(parts, axis=1)


def _arrange_w_uq(w):
    return jnp.pad(w.reshape(256, 8, 96), ((0, 0), (0, 0), (0, 32))).reshape(256, 1024)


def _unarrange_w_uq(a):
    return a.reshape(256, 8, 128)[:, :, :96].reshape(256, 768)


def _arrange_w_ukv(w):
    r = w.reshape(128, 8, 128)
    k = jnp.pad(r[:, :, :64], ((0, 0), (0, 0), (0, 64))).reshape(128, 1024)
    return jnp.concatenate([k, r[:, :, 64:].reshape(128, 512)], axis=1)


def _unarrange_w_ukv(a):
    k = a[:, :1024].reshape(128, 8, 128)[:, :, :64]
    v = a[:, 1024:].reshape(128, 8, 64)
    return jnp.concatenate([k, v], axis=2).reshape(128, 1024)


def _rope_tables(pos3):
    B, S, _ = pos3.shape
    ts = min(S, 512)
    inv32 = (np.float32(ROPE_THETA) ** (-(np.arange(32, dtype=np.float32) / 32))).astype(np.float32)
    inv16 = (np.float32(ROPE_THETA) ** (-(np.arange(16, dtype=np.float32) / 16))).astype(np.float32)
    inv_r = np.tile(inv32, 4)[None, :]
    inv_m = np.zeros((1, 128), np.float32)
    inv_m[0, 64:80] = inv16
    inv_m[0, 80:96] = inv16

    def body(pos_ref, ir_ref, im_ref, cr, sr, cm, sm):
        p = pos_ref[0].astype(F32)
        ar = p * ir_ref[...]
        cr[0] = jnp.cos(ar)
        sr[0] = jnp.sin(ar)
        am = p * im_ref[...]
        cm[0] = jnp.cos(am)
        sm[0] = jnp.sin(am)

    tab = jax.ShapeDtypeStruct((B, S, 128), F32)
    blk = pl.BlockSpec((1, ts, 128), lambda b, i: (b, i, 0))
    return pl.pallas_call(
        body, name="rope_tables", grid=(B, S // ts),
        in_specs=[pl.BlockSpec((1, ts, 1), lambda b, i: (b, i, 0)), _full((1, 128)), _full((1, 128))],
        out_specs=[blk, blk, blk, blk], out_shape=[tab, tab, tab, tab],
        compiler_params=_cp(("parallel", "parallel")),
    )(pos3, jnp.asarray(inv_r), jnp.asarray(inv_m))


def _rope128(x, cos, sin):
    lane = lax.broadcasted_iota(jnp.int32, (1, 128), 1)
    rp = pltpu.roll(x, 16, 1)
    rm = pltpu.roll(x, 112, 1)
    return x * cos + jnp.where(lane < 80, -rm, rp) * sin


def _rope128_t(d, cos, sin):
    lane = lax.broadcasted_iota(jnp.int32, (1, 128), 1)
    y = d * sin
    yp = pltpu.roll(y, 16, 1)
    ym = pltpu.roll(y, 112, 1)
    return d * cos + jnp.where(lane < 64, 0.0, jnp.where(lane < 80, ym, jnp.where(lane < 96, -yp, 0.0)))


def _proj_fwd(x, shift, scale, nw, w_arr):
    B, S, D = x.shape
    tm = min(S, 512)

    def body(x_ref, sh_ref, sc_ref, nw_ref, w_ref, ret_ref, mla_ref, gla_ref, h_ref):
        xv = x_ref[0]
        rstd = lax.rsqrt(jnp.mean(xv * xv, axis=-1, keepdims=True) + EPS)
        h = (xv * rstd * nw_ref[...]) * (1.0 + sc_ref[0]) + sh_ref[0]
        hb = h.astype(_MXU)
        h_ref[0] = hb
        ret_ref[0] = jnp.dot(hb, w_ref[:, 0:RET_W], preferred_element_type=F32)
        mla_ref[0] = jnp.dot(hb, w_ref[:, RET_W:RET_W + MLA_W], preferred_element_type=F32)
        gla_ref[0] = jnp.dot(hb, w_ref[:, RET_W + MLA_W:ARR_W], preferred_element_type=F32)

    tok = lambda w: pl.BlockSpec((1, tm, w), lambda b, i: (b, i, 0))
    per_seq = pl.BlockSpec((1, 1, D), lambda b, i: (b, 0, 0))
    return pl.pallas_call(
        body, name="proj_fwd", grid=(B, S // tm),
        in_specs=[tok(D), per_seq, per_seq, _full((1, D)), _full((D, ARR_W))],
        out_specs=[tok(RET_W), tok(MLA_W), tok(GLA_W), tok(D)],
        out_shape=[jax.ShapeDtypeStruct((B, S, RET_W), F32), jax.ShapeDtypeStruct((B, S, MLA_W), F32),
                   jax.ShapeDtypeStruct((B, S, GLA_W), F32), jax.ShapeDtypeStruct((B, S, D), _MXU)],
        compiler_params=_cp(("parallel", "parallel")),
    )(x, shift, scale, nw, w_arr)


RET_L = 256


def _ret_consts(L):
    lg = np.log1p(-np.exp2(-5.0 - np.arange(4, dtype=np.float32))).astype(np.float32)
    i = np.arange(L)
    ci = i // CHUNK
    diff = (i[:, None] - i[None, :]).astype(np.float32)
    same = ci[:, None] == ci[None, :]
    past = ci[None, :] < ci[:, None]
    expo = np.where(same, np.abs(diff), np.where(past, diff, 0.0)).astype(np.float32)
    dec = np.where((same | past)[None], np.exp(lg[:, None, None] * expo[None]), 0.0).astype(np.float32)
    head = (np.arange(256) % 128) // 32
    qw = np.exp((i + 1.0)[:, None] * lg[head][None, :]).astype(np.float32)
    kw = np.exp((L - 1.0 - i)[:, None] * lg[head][None, :]).astype(np.float32)
    a_row = np.exp(np.float32(L) * lg[head])[None, :].astype(np.float32)
    return [jnp.asarray(t) for t in (dec.reshape(4 * L, L), qw, kw, a_row)]


def _ret_masks():
    lane = lax.broadcasted_iota(jnp.int32, (1, 256), 1)
    mh = [((lane % 128) // 32) == h for h in range(4)]
    mv = [(lane // 64) == h for h in range(4)]
    vi = lax.broadcasted_iota(jnp.int32, (256, 256), 0)
    ki = lax.broadcasted_iota(jnp.int32, (256, 256), 1)
    bd = (vi // 64) == ((ki % 128) // 32)
    return mh, mv, bd


def _ret_rope(p, cs, sn):
    q1, q2, k1, k2 = p[:, 0:128], p[:, 128:256], p[:, 256:384], p[:, 384:512]
    qr = jnp.concatenate([q1 * cs - q2 * sn, q2 * cs + q1 * sn], axis=1)
    kr = jnp.concatenate([k1 * cs - k2 * sn, k2 * cs + k1 * sn], axis=1) * RET_KSCALE
    return qr, kr


def _head_mean(x, mv, width):
    out = jnp.zeros_like(x)
    for m in mv:
        s = jnp.sum(jnp.where(m, x, 0.0), axis=-1, keepdims=True) * (1.0 / width)
        out = jnp.where(m, s, out)
    return out


def _stack_heads(x, masks):
    return jnp.concatenate([jnp.where(m, x, 0.0) for m in masks], axis=0)


def _fold_heads(xs, masks, L):
    out = jnp.where(masks[0], xs[0:L], 0.0)
    for h in range(1, 4):
        out = out + jnp.where(masks[h], xs[h * L:(h + 1) * L], 0.0)
    return out


def _ret_fwd(ret_p, cos, sin):
    B, S, _ = ret_p.shape
    L = min(RET_L, S)
    NB = S // L
    consts = _ret_consts(L)

    def body(p_ref, c_ref, s_ref, ds_ref, qw_ref, kw_ref, a_ref, out_ref, raw_ref, st_ref, st_sc):
        @pl.when(pl.program_id(1) == 0)
        def _():
            st_sc[...] = jnp.zeros_like(st_sc)

        mh, mv, bd = _ret_masks()
        p = p_ref[0]
        qr, kr = _ret_rope(p, c_ref[0], s_ref[0])
        v = p[:, 512:768]
        z = p[:, 768:1024]
        a_s = _mm_nt(_stack_heads(qr, mh), kr) * ds_ref[...]
        intra = _fold_heads(_mm(a_s, v), mv, L)
        st = st_sc[...]
        st_ref[0, 0] = st
        r = intra + _mm_nt(qr * qw_ref[...], st)
        raw_ref[0] = r
        st_sc[...] = st * a_ref[...] + jnp.where(bd, _mm_tn(v, kr * kw_ref[...]), 0.0)
        rstd = lax.rsqrt(_head_mean(r * r, mv, 64.0) + EPS)
        out_ref[0] = (r * rstd * _silu(z)).astype(_MXU)

    tok = lambda w: pl.BlockSpec((1, L, w), lambda b, n: (b, n, 0))
    return pl.pallas_call(
        body, name="ret_fwd", grid=(B, NB),
        in_specs=[tok(RET_W), tok(128), tok(128), _full((4 * L, L)), _full((L, 256)), _full((L, 256)),
                  _full((1, 256))],
        out_specs=[tok(256), tok(256), pl.BlockSpec((1, 1, 256, 256), lambda b, n: (b, n, 0, 0))],
        out_shape=[jax.ShapeDtypeStruct((B, S, 256), _MXU), jax.ShapeDtypeStruct((B, S, 256), F32),
                   jax.ShapeDtypeStruct((B, NB, 256, 256), F32)],
        scratch_shapes=[pltpu.VMEM((256, 256), F32)],
        compiler_params=_cp(("parallel", "arbitrary")),
    )(ret_p, cos, sin, *consts)


def _ret_bwd(ret_p, cos, sin, raw, states, d_mix):
    B, S, _ = ret_p.shape
    L = min(RET_L, S)
    NB = S // L
    consts = _ret_consts(L)

    def body(p_ref, c_ref, s_ref, raw_ref, st_ref, dm_ref, ds_ref, qw_ref, kw_ref, a_ref, dp_ref, dst_sc):
        @pl.when(pl.program_id(1) == 0)
        def _():
            dst_sc[...] = jnp.zeros_like(dst_sc)

        mh, mv, bd = _ret_masks()
        p = p_ref[0]
        cs, sn = c_ref[0], s_ref[0]
        qr, kr = _ret_rope(p, cs, sn)
        v = p[:, 512:768]
        z = p[:, 768:1024]
        qs = _stack_heads(qr, mh)
        dec = ds_ref[...]
        a_s = _mm_nt(qs, kr) * dec
        r = raw_ref[0]
        rstd = lax.rsqrt(_head_mean(r * r, mv, 64.0) + EPS)
        rn = r * rstd
        dm = dm_ref[0]
        d_rn = dm * _silu(z)
        dz = dm * rn * _dsilu(z)
        dr = rstd * (d_rn - rn * _head_mean(d_rn * rn, mv, 64.0))
        do_s = _stack_heads(dr, mv)
        da_s = _mm_nt(do_s, v) * dec
        dv = _mm_tn(a_s, do_s)
        dqr = _fold_heads(_mm(da_s, kr), mh, L)
        dkr = _mm_tn(da_s, qs)
        st = st_ref[0, 0]
        qw, kw = qw_ref[...], kw_ref[...]
        dqr = dqr + _mm(dr, st) * qw
        dst_next = dst_sc[...]
        g = jnp.where(bd, dst_next, 0.0)
        kk = kr * kw
        dv = dv + _mm_nt(kk, g)
        dkr = dkr + _mm(v, g) * kw
        dst_sc[...] = dst_next * a_ref[...] + jnp.where(bd, _mm_tn(dr, qr * qw), 0.0)
        dkr = dkr * RET_KSCALE
        dq1, dq2 = dqr[:, 0:128], dqr[:, 128:256]
        dk1, dk2 = dkr[:, 0:128], dkr[:, 128:256]
        dp_ref[0] = jnp.concatenate(
            [dq1 * cs + dq2 * sn, dq2 * cs - dq1 * sn, dk1 * cs + dk2 * sn, dk2 * cs - dk1 * sn, dv, dz], axis=1)

    tok = lambda w: pl.BlockSpec((1, L, w), lambda b, i: (b, NB - 1 - i, 0))
    return pl.pallas_call(
        body, name="ret_bwd", grid=(B, NB),
        in_specs=[tok(RET_W), tok(128), tok(128), tok(256),
                  pl.BlockSpec((1, 1, 256, 256), lambda b, i: (b, NB - 1 - i, 0, 0)), tok(256),
                  _full((4 * L, L)), _full((L, 256)), _full((L, 256)), _full((1, 256))],
        out_specs=tok(RET_W), out_shape=jax.ShapeDtypeStruct((B, S, RET_W), F32),
        scratch_shapes=[pltpu.VMEM((256, 256), F32)],
        compiler_params=_cp(("parallel", "arbitrary")),
    )(ret_p, cos, sin, raw, states, d_mix, *consts)


def _gla_masks():
    C = CHUNK
    lk = lax.broadcasted_iota(jnp.int32, (1, 128), 1)
    lv = lax.broadcasted_iota(jnp.int32, (1, 256), 1)
    mk = [(lk // 32) == h for h in range(4)]
    mv = [(lv // 64) == h for h in range(4)]
    vi = lax.broadcasted_iota(jnp.int32, (256, 128), 0)
    ki = lax.broadcasted_iota(jnp.int32, (256, 128), 1)
    bd = (vi // 64) == (ki // 32)
    ri = lax.broadcasted_iota(jnp.int32, (4 * C, C), 0) % C
    cj = lax.broadcasted_iota(jnp.int32, (4 * C, C), 1)
    lower = ri >= cj
    ti = lax.broadcasted_iota(jnp.int32, (C, C), 0)
    tj = lax.broadcasted_iota(jnp.int32, (C, C), 1)
    ltri = jnp.where(ti >= tj, 1.0, 0.0).astype(F32)
    utri = jnp.where(ti <= tj, 1.0, 0.0).astype(F32)
    return mk, mv, bd, lower, ltri, utri


def _gla_gate(p, w_ref, b_ref, ltri):
    pre = _mm(p[:, 512:640], w_ref[...]) + b_ref[...]
    la = (jnp.minimum(pre, 0.0) - jnp.log(1.0 + jnp.exp(-jnp.abs(pre)))) * (1.0 / GLA_TAU)
    cum = _mm_f32(ltri, la)
    return pre, cum


GLA_G = 4


def _gla_fwd(gla_p, w_g2p, b_g2, gnw):
    B, S, _ = gla_p.shape
    C = CHUNK
    NC = S // C
    G = min(GLA_G, NC)
    NG = NC // G

    def body(p_ref, w_ref, b_ref, gn_ref, out_ref, raw_ref, st_ref, st_sc):
        @pl.when(pl.program_id(1) == 0)
        def _():
            st_sc[...] = jnp.zeros_like(st_sc)

        mk, mv, bd, lower, ltri, _ = _gla_masks()
        st = st_sc[...]
        for c in range(G):
            rows = slice(c * C, (c + 1) * C)
            p = p_ref[0, rows, :]
            q = p[:, 0:128]
            k = p[:, 128:256] * GLA_KSCALE
            v = p[:, 256:512]
            z = p[:, 640:896]
            _, cum = _gla_gate(p, w_ref, b_ref, ltri)
            last = cum[C - 1:C, :]
            e_pos = jnp.exp(cum)
            e_neg = jnp.exp(-cum)
            q_pos = q * e_pos
            past = _mm_nt(_stack_heads(q_pos, mk), k * e_neg)
            fut = _mm_nt(_stack_heads(q * e_neg, mk), k * e_pos)
            attn = jnp.where(lower, past, fut)
            intra = _fold_heads(_mm(attn, v), mv, C)
            st_ref[0, c] = st
            g = intra + _mm_nt(q_pos, st)
            raw_ref[0, rows, :] = g
            kd = k * jnp.exp(last - cum)
            st = st * jnp.exp(last) + jnp.where(bd, _mm_tn(v, kd), 0.0)
            rstd = lax.rsqrt(_head_mean(g * g, mv, 64.0) + EPS)
            out_ref[0, rows, :] = (g * rstd * gn_ref[...] * _silu(z)).astype(_MXU)
        st_sc[...] = st

    tok = lambda w: pl.BlockSpec((1, G * C, w), lambda b, n: (b, n, 0))
    return pl.pallas_call(
        body, name="gla_fwd", grid=(B, NG),
        in_specs=[tok(GLA_W), _full((128, 128)), _full((1, 128)), _full((1, 256))],
        out_specs=[tok(256), tok(256), pl.BlockSpec((1, G, 256, 128), lambda b, n: (b, n, 0, 0))],
        out_shape=[jax.ShapeDtypeStruct((B, S, 256), _MXU), jax.ShapeDtypeStruct((B, S, 256), F32),
                   jax.ShapeDtypeStruct((B, NC, 256, 128), F32)],
        scratch_shapes=[pltpu.VMEM((256, 128), F32)],
        compiler_params=_cp(("parallel", "arbitrary")),
    )(gla_p, w_g2p, b_g2, gnw)


def _gla_bwd(gla_p, w_g2p, b_g2, gnw, raw, states, d_mix):
    B, S, _ = gla_p.shape
    C = CHUNK
    NC = S // C
    G = min(GLA_G, NC)
    NG = NC // G

    def body(p_ref, w_ref, b_ref, gn_ref, raw_ref, st_ref, dm_ref, dp_ref, dw_ref, db_ref, dgn_ref, dst_sc):
        first = (pl.program_id(0) == 0) & (pl.program_id(1) == 0)

        @pl.when(first)
        def _():
            dw_ref[...] = jnp.zeros_like(dw_ref)
            db_ref[...] = jnp.zeros_like(db_ref)
            dgn_ref[...] = jnp.zeros_like(dgn_ref)

        @pl.when(pl.program_id(1) == 0)
        def _():
            dst_sc[...] = jnp.zeros_like(dst_sc)

        mk, mv, bd, lower, ltri, utri = _gla_masks()
        gn = gn_ref[...]
        dst_next = dst_sc[...]
        dw_acc = jnp.zeros((128, 128), F32)
        db_acc = jnp.zeros((1, 128), F32)
        dgn_acc = jnp.zeros((1, 256), F32)
        for c in reversed(range(G)):
            rows = slice(c * C, (c + 1) * C)
            p = p_ref[0, rows, :]
            q = p[:, 0:128]
            k = p[:, 128:256] * GLA_KSCALE
            v = p[:, 256:512]
            gg = p[:, 512:640]
            z = p[:, 640:896]
            pre, cum = _gla_gate(p, w_ref, b_ref, ltri)
            last = cum[C - 1:C, :]
            e_pos = jnp.exp(cum)
            e_neg = jnp.exp(-cum)
            q_pos, q_neg = q * e_pos, q * e_neg
            k_pos, k_neg = k * e_pos, k * e_neg
            qp_s = _stack_heads(q_pos, mk)
            qn_s = _stack_heads(q_neg, mk)
            attn = jnp.where(lower, _mm_nt(qp_s, k_neg), _mm_nt(qn_s, k_pos))
            g = raw_ref[0, rows, :]
            rstd = lax.rsqrt(_head_mean(g * g, mv, 64.0) + EPS)
            gh = g * rstd
            dm = dm_ref[0, rows, :]
            d_gn = dm * _silu(z)
            dz = dm * gh * gn * _dsilu(z)
            dgn_acc = dgn_acc + jnp.sum(d_gn * gh, axis=0, keepdims=True)
            d_gh = d_gn * gn
            dg = rstd * (d_gh - gh * _head_mean(d_gh * gh, mv, 64.0))
            do_s = _stack_heads(dg, mv)
            dattn = _mm_nt(do_s, v)
            dv = _mm_tn(attn, do_s)
            dpast = jnp.where(lower, dattn, 0.0)
            dfut = jnp.where(lower, 0.0, dattn)
            dq_pos = _fold_heads(_mm(dpast, k_neg), mk, C)
            dk_neg = _mm_tn(dpast, qp_s)
            dq_neg = _fold_heads(_mm(dfut, k_pos), mk, C)
            dk_pos = _mm_tn(dfut, qn_s)
            st = st_ref[0, c]
            dq_pos = dq_pos + _mm(dg, st)
            a_row = jnp.exp(last)
            d_a = jnp.sum(dst_next * st, axis=0, keepdims=True)
            gmat = jnp.where(bd, dst_next, 0.0)
            w_dec = jnp.exp(last - cum)
            kd = k * w_dec
            d_kd = _mm(v, gmat)
            dv = dv + _mm_nt(kd, gmat)
            dst_next = dst_next * a_row + jnp.where(bd, _mm_tn(dg, q_pos), 0.0)
            t = d_kd * kd
            dk = d_kd * w_dec + dk_neg * e_neg + dk_pos * e_pos
            dq = dq_pos * e_pos + dq_neg * e_neg
            d_last = jnp.sum(t, axis=0, keepdims=True) + d_a * a_row
            d_cum = dq_pos * q_pos - dk_neg * k_neg - dq_neg * q_neg + dk_pos * k_pos - t
            row = lax.broadcasted_iota(jnp.int32, (C, 128), 0)
            d_cum = d_cum + jnp.where(row == C - 1, d_last, 0.0)
            d_la = _mm_f32(utri, d_cum)
            d_pre = d_la * _sig(-pre) * (1.0 / GLA_TAU)
            d_gg = _mm_nt(d_pre, w_ref[...])
            dw_acc = dw_acc + _mm_tn(gg, d_pre)
            db_acc = db_acc + jnp.sum(d_pre, axis=0, keepdims=True)
            dp_ref[0, rows, :] = jnp.concatenate([dq, dk * GLA_KSCALE, dv, d_gg, dz], axis=1)
        dst_sc[...] = dst_next
        dw_ref[...] += dw_acc
        db_ref[...] += db_acc
        dgn_ref[...] += dgn_acc

        @pl.when((pl.program_id(0) == B - 1) & (pl.program_id(1) == NG - 1))
        def _():
            s1 = dgn_ref[...]
            s1 = s1 + pltpu.roll(s1, 128, 1)
            dgn_ref[...] = s1 + pltpu.roll(s1, 64, 1)

    tok = lambda w: pl.BlockSpec((1, G * C, w), lambda b, i: (b, NG - 1 - i, 0))
    return pl.pallas_call(
        body, name="gla_bwd", grid=(B, NG),
        in_specs=[tok(GLA_W), _full((128, 128)), _full((1, 128)), _full((1, 256)), tok(256),
                  pl.BlockSpec((1, G, 256, 128), lambda b, i: (b, NG - 1 - i, 0, 0)), tok(256)],
        out_specs=[tok(GLA_W), _full((128, 128)), _full((1, 128)), _full((1, 256))],
        out_shape=[jax.ShapeDtypeStruct((B, S, GLA_W), F32), jax.ShapeDtypeStruct((128, 128), F32),
                   jax.ShapeDtypeStruct((1, 128), F32), jax.ShapeDtypeStruct((1, 256), F32)],
        scratch_shapes=[pltpu.VMEM((256, 128), F32)],
        compiler_params=_cp(("arbitrary", "arbitrary")),
    )(gla_p, w_g2p, b_g2, gnw, raw, states, d_mix)


def _rms(x, w):
    rstd = lax.rsqrt(jnp.mean(x * x, axis=-1, keepdims=True) + EPS)
    xh = x * rstd
    return xh, rstd, xh * w


def _rms_bwd(dy, xh, rstd, w):
    dxh = dy * w
    return rstd * (dxh - xh * jnp.mean(dxh * xh, axis=-1, keepdims=True))


MLA_T = 256


def _mla_prep_fwd(mla_p, cos, sin, qnw, kvnw, w_uq, w_ukv):
    B, S, _ = mla_p.shape
    tm = min(S, 512)

    t = min(MLA_T, S)
    nt = tm // t

    def body(p_ref, c_ref, s_ref, qn_ref, kn_ref, wq_ref, wkv_ref, q_ref, k_ref, v_ref, kt_ref, vt_ref):
        p = p_ref[0]
        cs, sn = c_ref[0], s_ref[0]
        _, _, qn = _rms(p[:, 0:256], qn_ref[...])
        qpre = _mm(qn, wq_ref[...])
        _, _, kvn = _rms(p[:, 256:384], kn_ref[...])
        kv = _mm(kvn, wkv_ref[...])
        kpe = _rope128(p[:, 384:512], cs, sn)
        for h in range(8):
            sl = slice(128 * h, 128 * h + 128)
            q_ref[0, :, sl] = _rope128(qpre[:, sl], cs, sn).astype(_MXU)
            kh = kv[:, sl] + kpe
            k_ref[0, :, sl] = kh.astype(_MXU)
            kht = kh.T
            for n in range(nt):
                kt_ref[0, n, sl, :] = kht[:, n * t:(n + 1) * t].astype(_MXU)
        v_ref[0] = kv[:, 1024:1536].astype(_MXU)
        for pr in range(4):
            vht = kv[:, 1024 + 128 * pr:1152 + 128 * pr].T
            for n in range(nt):
                vt_ref[0, n, 128 * pr:128 * pr + 128, :] = vht[:, n * t:(n + 1) * t].astype(_MXU)

    tok = lambda w: pl.BlockSpec((1, tm, w), lambda b, i: (b, i, 0))
    tr = lambda w: pl.BlockSpec((1, nt, w, t), lambda b, i: (b, i, 0, 0))
    return pl.pallas_call(
        body, name="mla_prep_fwd", grid=(B, S // tm),
        in_specs=[tok(512), tok(128), tok(128), _full((1, 256)), _full((1, 128)), _full((256, 1024)),
                  _full((128, 1536))],
        out_specs=[tok(1024), tok(1024), tok(512), tr(1024), tr(512)],
        out_shape=[jax.ShapeDtypeStruct((B, S, 1024), _MXU), jax.ShapeDtypeStruct((B, S, 1024), _MXU),
                   jax.ShapeDtypeStruct((B, S, 512), _MXU), jax.ShapeDtypeStruct((B, S // t, 1024, t), _MXU),
                   jax.ShapeDtypeStruct((B, S // t, 512, t), _MXU)],
        compiler_params=_cp(("parallel", "parallel")),
    )(mla_p, cos, sin, qnw, kvnw, w_uq, w_ukv)


def _chunk_mask_t(t):
    kj = lax.broadcasted_iota(jnp.int32, (t, t), 0) // CHUNK
    qi = lax.broadcasted_iota(jnp.int32, (t, t), 1) // CHUNK
    return kj <= qi


MLA_HG = 4
LOG2E = 1.4426950408889634
MLA_C2 = MLA_SCALE * LOG2E


def _mla_attn_fwd(q, k, vt):
    B, S, _ = q.shape
    t = min(MLA_T, S)
    nq = S // t
    HG = MLA_HG
    NP = HG // 2

    def body(q_ref, k_ref, vt_ref, o_ref, lse_ref, sa, sb, m_sc, l_sc, acc_sc):
        i = pl.program_id(2)
        row = lax.broadcasted_iota(jnp.int32, (128, 1), 0)
        low = row < 64
        mask = _chunk_mask_t(t)
        m_sc[...] = jnp.full(m_sc.shape, -jnp.inf, F32)
        l_sc[...] = jnp.zeros_like(l_sc)
        acc_sc[...] = jnp.zeros_like(acc_sc)

        def scores(j, buf):
            kb = k_ref[0, pl.ds(pl.multiple_of(j * t, t), t), :]
            for h in range(HG):
                cols = slice(128 * h, 128 * h + 128)
                buf[h] = _mm_nt(kb[:, cols], q_ref[0, :, cols]) * MLA_C2

        def absorb(j, buf, masked):
            vtb = vt_ref[0, j]
            for pr in range(NP):
                alphas, pvs = [], []
                for hh in range(2):
                    h = 2 * pr + hh
                    s = buf[h]
                    if masked:
                        s = jnp.where(mask, s, -jnp.inf)
                    m_old = m_sc[h]
                    m_new = jnp.maximum(m_old, jnp.max(s, axis=0, keepdims=True))
                    alpha = jnp.exp2(m_old - m_new)
                    p = jnp.exp2(s - m_new)
                    l_sc[h] = alpha * l_sc[h] + jnp.sum(p, axis=0, keepdims=True)
                    m_sc[h] = m_new
                    vth = vtb[128 * pr:128 * pr + 128, :]
                    vth = jnp.where(low if hh == 0 else ~low, vth, jnp.zeros_like(vth))
                    pvs.append(_mm(vth, p))
                    alphas.append(alpha)
                acc_sc[pr] = acc_sc[pr] * jnp.where(low, alphas[0], alphas[1]) + pvs[0] + pvs[1]

        scores(i, sa)

        @pl.when(i > 0)
        def _():
            scores(0, sb)

        absorb(i, sa, True)

        def pair(jj, carry):
            j0 = 2 * jj
            scores(jnp.minimum(j0 + 1, i - 1), sa)
            absorb(j0, sb, False)
            scores(jnp.minimum(j0 + 2, i - 1), sb)
            absorb(j0 + 1, sa, False)
            return carry

        lax.fori_loop(0, i // 2, pair, 0)

        @pl.when(i % 2 == 1)
        def _():
            absorb(i - 1, sb, False)

        for pr in range(NP):
            l_e, l_o = l_sc[2 * pr], l_sc[2 * pr + 1]
            o_ref[0, :, 128 * pr:128 * pr + 128] = (acc_sc[pr] / jnp.where(low, l_e, l_o)).T
            lse_ref[0, pr, 0, 0:1, :] = m_sc[2 * pr] + jnp.log(l_e) * LOG2E
            lse_ref[0, pr, 0, 1:2, :] = m_sc[2 * pr + 1] + jnp.log(l_o) * LOG2E

    return pl.pallas_call(
        body, name="mla_attn_fwd", grid=(B, 8 // HG, nq),
        in_specs=[pl.BlockSpec((1, t, 128 * HG), lambda b, g, i: (b, i, g)),
                  pl.BlockSpec((1, S, 128 * HG), lambda b, g, i: (b, 0, g)),
                  pl.BlockSpec((1, nq, 64 * HG, t), lambda b, g, i: (b, 0, g, 0))],
        out_specs=[pl.BlockSpec((1, t, 64 * HG), lambda b, g, i: (b, i, g)),
                   pl.BlockSpec((1, NP, 1, 2, t), lambda b, g, i: (b, g, i, 0, 0))],
        out_shape=[jax.ShapeDtypeStruct((B, S, 512), F32), jax.ShapeDtypeStruct((B, 4, nq, 2, t), F32)],
        scratch_shapes=[pltpu.VMEM((HG, t, t), F32), pltpu.VMEM((HG, t, t), F32), pltpu.VMEM((HG, 1, t), F32),
                        pltpu.VMEM((HG, 1, t), F32), pltpu.VMEM((NP, 128, t), F32)],
        compiler_params=_cp(("parallel", "parallel", "arbitrary")),
    )(q, k, vt)


def _mla_gate_bwd(d_mix, o, mla_p):
    B, S, _ = o.shape
    tm = min(S, 512)
    t = min(MLA_T, S)
    nt = tm // t

    def body(dm_ref, o_ref, z_ref, do_ref, dz_ref, dl_ref):
        dm, ov, z = dm_ref[0], o_ref[0], z_ref[0]
        do = dm * _silu(z)
        dz_ref[0] = dm * ov * _dsilu(z)
        do_ref[0] = do.astype(_MXU)
        prod = do * ov
        for pr in range(4):
            pt = prod[:, 128 * pr:128 * pr + 128].T
            se = jnp.sum(pt[0:64], axis=0, keepdims=True)
            so = jnp.sum(pt[64:128], axis=0, keepdims=True)
            for n in range(nt):
                dl_ref[0, pr, n, 0:1, :] = se[:, n * t:(n + 1) * t]
                dl_ref[0, pr, n, 1:2, :] = so[:, n * t:(n + 1) * t]

    tok = lambda c: pl.BlockSpec((1, tm, 512), lambda b, i: (b, i, c))
    return pl.pallas_call(
        body, name="mla_gate_bwd", grid=(B, S // tm),
        in_specs=[tok(0), tok(0), tok(1)],
        out_specs=[tok(0), tok(0), pl.BlockSpec((1, 4, nt, 2, t), lambda b, i: (b, 0, i, 0, 0))],
        out_shape=[jax.ShapeDtypeStruct((B, S, 512), _MXU), jax.ShapeDtypeStruct((B, S, 512), F32),
                   jax.ShapeDtypeStruct((B, 4, S // t, 2, t), F32)],
        compiler_params=_cp(("parallel", "parallel")),
    )(d_mix, o, mla_p)


def _mla_attn_bwd(q, k, v, kt, do, lse, dl):
    B, S, _ = q.shape
    t = min(MLA_T, S)
    nk = S // t

    HG = MLA_HG
    NP = HG // 2

    def body(q_ref, k_ref, v_ref, kt_ref, do_ref, lse_ref, dl_ref, dq_ref, dk_ref, dv_ref,
             sa, da, sb, db, dqt_sc, dk_sc, dv_sc):
        j = pl.program_id(2)

        @pl.when(j == 0)
        def _():
            dqt_sc[...] = jnp.zeros_like(dqt_sc)

        dk_sc[...] = jnp.zeros_like(dk_sc)
        dv_sc[...] = jnp.zeros_like(dv_sc)
        lane = lax.broadcasted_iota(jnp.int32, (1, 128), 1)
        low = lane < 64
        mask = _chunk_mask_t(t)

        def half(x, hh):
            return jnp.where(low if hh == 0 else ~low, x, jnp.zeros_like(x))

        def prepare(i, sbuf, dbuf):
            rows = pl.ds(pl.multiple_of(i * t, t), t)
            for h in range(HG):
                cols = slice(128 * h, 128 * h + 128)
                pc = slice(128 * (h // 2), 128 * (h // 2) + 128)
                sbuf[h] = _mm_nt(k_ref[0, :, cols], q_ref[0, rows, cols]) * MLA_C2
                dbuf[h] = _mm_nt(half(v_ref[0, :, pc], h % 2), do_ref[0, rows, pc])

        def absorb(i, sbuf, dbuf, masked):
            rows = pl.ds(pl.multiple_of(i * t, t), t)
            for h in range(HG):
                pr, hh = h // 2, h % 2
                cols = slice(128 * h, 128 * h + 128)
                pc = slice(128 * pr, 128 * pr + 128)
                p = jnp.exp2(sbuf[h] - lse_ref[0, pr, i][hh:hh + 1, :])
                if masked:
                    p = jnp.where(mask, p, 0.0)
                dv_sc[pr] += _mm(p, half(do_ref[0, rows, pc], hh))
                ds = p * (dbuf[h] - dl_ref[0, pr, i][hh:hh + 1, :])
                dqt_sc[i, cols, :] += _mm(kt_ref[0, 0, cols, :], ds)
                dk_sc[h] += _mm(ds, q_ref[0, rows, cols])

        n = nk - 1 - j
        prepare(j, sa, da)

        @pl.when(n > 0)
        def _():
            prepare(j + 1, sb, db)

        absorb(j, sa, da, True)

        def pair(jj, carry):
            i0 = j + 1 + 2 * jj
            prepare(jnp.minimum(i0 + 1, nk - 1), sa, da)
            absorb(i0, sb, db, False)
            prepare(jnp.minimum(i0 + 2, nk - 1), sb, db)
            absorb(i0 + 1, sa, da, False)
            return carry

        lax.fori_loop(0, n // 2, pair, 0)

        @pl.when(n % 2 == 1)
        def _():
            absorb(nk - 1, sb, db, False)

        for h in range(HG):
            dk_ref[0, :, 128 * h:128 * h + 128] = dk_sc[h] * MLA_SCALE
        for pr in range(NP):
            dv_ref[0, :, 128 * pr:128 * pr + 128] = dv_sc[pr]

        @pl.when(j == nk - 1)
        def _():
            for i in range(nk):
                dq_ref[0, i * t:(i + 1) * t, :] = dqt_sc[i].T * MLA_SCALE

    seq = lambda w: pl.BlockSpec((1, S, w), lambda b, g, j: (b, 0, g))
    blk = lambda w: pl.BlockSpec((1, t, w), lambda b, g, j: (b, j, g))
    stat = pl.BlockSpec((1, NP, nk, 2, t), lambda b, g, j: (b, g, 0, 0, 0))
    return pl.pallas_call(
        body, name="mla_attn_bwd", grid=(B, 8 // HG, nk),
        in_specs=[seq(128 * HG), blk(128 * HG), blk(64 * HG),
                  pl.BlockSpec((1, 1, 128 * HG, t), lambda b, g, j: (b, j, g, 0)), seq(64 * HG), stat, stat],
        out_specs=[seq(128 * HG), blk(128 * HG), blk(64 * HG)],
        out_shape=[jax.ShapeDtypeStruct((B, S, 1024), F32), jax.ShapeDtypeStruct((B, S, 1024), F32),
                   jax.ShapeDtypeStruct((B, S, 512), F32)],
        scratch_shapes=[pltpu.VMEM((HG, t, t), F32), pltpu.VMEM((HG, t, t), F32), pltpu.VMEM((HG, t, t), F32),
                        pltpu.VMEM((HG, t, t), F32), pltpu.VMEM((nk, 128 * HG, t), F32),
                        pltpu.VMEM((HG, t, 128), F32), pltpu.VMEM((NP, t, 128), F32)],
        compiler_params=_cp(("parallel", "parallel", "arbitrary")),
    )(q, k, v, kt, do, lse, dl)


def _mla_prep_bwd(mla_p, cos, sin, qnw, kvnw, w_uq, w_ukv, dq, dk, dv):
    B, S, _ = mla_p.shape
    tm = min(S, 512)

    def body(p_ref, c_ref, s_ref, qn_ref, kn_ref, wq_ref, wkv_ref, dq_ref, dk_ref, dv_ref,
             dp_ref, dwq_ref, dwkv_ref, dqn_ref, dkn_ref):
        first = (pl.program_id(0) == 0) & (pl.program_id(1) == 0)

        @pl.when(first)
        def _():
            dwq_ref[...] = jnp.zeros_like(dwq_ref)
            dwkv_ref[...] = jnp.zeros_like(dwkv_ref)
            dqn_ref[...] = jnp.zeros_like(dqn_ref)
            dkn_ref[...] = jnp.zeros_like(dkn_ref)

        p = p_ref[0]
        cs, sn = c_ref[0], s_ref[0]
        lane = lax.broadcasted_iota(jnp.int32, (1, 128), 1)
        pe = (lane >= 64) & (lane < 96)
        qh, q_rstd, qn = _rms(p[:, 0:256], qn_ref[...])
        kvh, kv_rstd, kvn = _rms(p[:, 256:384], kn_ref[...])
        dqv = dq_ref[0]
        dkv = dk_ref[0]
        dqpre = jnp.concatenate(
            [_rope128_t(dqv[:, 128 * h:128 * h + 128], cs, sn) for h in range(8)], axis=1)
        dkpe = jnp.zeros((tm, 128), F32)
        for h in range(8):
            dkpe = dkpe + jnp.where(pe, dkv[:, 128 * h:128 * h + 128], 0.0)
        dkr = _rope128_t(dkpe, cs, sn)
        dkv_all = jnp.concatenate([dkv, dv_ref[0]], axis=1)
        d_qn = _mm_nt(dqpre, wq_ref[...])
        d_kvn = _mm_nt(dkv_all, wkv_ref[...])
        dwq_ref[...] += _mm_tn(qn, dqpre)
        dwkv_ref[...] += _mm_tn(kvn, dkv_all)
        dqn_ref[...] += jnp.sum(d_qn * qh, axis=0, keepdims=True)
        dkn_ref[...] += jnp.sum(d_kvn * kvh, axis=0, keepdims=True)
        dp_ref[0] = jnp.concatenate([_rms_bwd(d_qn, qh, q_rstd, qn_ref[...]),
                                     _rms_bwd(d_kvn, kvh, kv_rstd, kn_ref[...]), dkr], axis=1)

    tok = lambda w: pl.BlockSpec((1, tm, w), lambda b, i: (b, i, 0))
    return pl.pallas_call(
        body, name="mla_prep_bwd", grid=(B, S // tm),
        in_specs=[tok(512), tok(128), tok(128), _full((1, 256)), _full((1, 128)), _full((256, 1024)),
                  _full((128, 1536)), tok(1024), tok(1024), tok(512)],
        out_specs=[tok(512), _full((256, 1024)), _full((128, 1536)), _full((1, 256)), _full((1, 128))],
        out_shape=[jax.ShapeDtypeStruct((B, S, 512), F32), jax.ShapeDtypeStruct((256, 1024), F32),
                   jax.ShapeDtypeStruct((128, 1536), F32), jax.ShapeDtypeStruct((1, 256), F32),
                   jax.ShapeDtypeStruct((1, 128), F32)],
        compiler_params=_cp(("arbitrary", "arbitrary")),
    )(mla_p, cos, sin, qnw, kvnw, w_uq, w_ukv, dq, dk, dv)


def _out_fwd(x, gate, r_g, o_mla, mla_p, g_g, w_out):
    B, S, D = x.shape
    tm = min(S, 512)

    def body(x_ref, g_ref, r_ref, o_ref, z_ref, gg_ref, w_ref, xn_ref, y_ref, mm_ref):
        mm = (o_ref[0] * _silu(z_ref[0])).astype(_MXU)
        mm_ref[0] = mm
        y = (jnp.dot(r_ref[0], w_ref[0:256, :], preferred_element_type=F32)
             + jnp.dot(mm, w_ref[256:768, :], preferred_element_type=F32)
             + jnp.dot(gg_ref[0], w_ref[768:1024, :], preferred_element_type=F32))
        y_ref[0] = y
        xn_ref[0] = x_ref[0] + g_ref[0] * y

    tok = lambda w, c=0: pl.BlockSpec((1, tm, w), lambda b, i: (b, i, c))
    return pl.pallas_call(
        body, name="out_fwd", grid=(B, S // tm),
        in_specs=[tok(D), pl.BlockSpec((1, 1, D), lambda b, i: (b, 0, 0)), tok(256), tok(512), tok(512, 1),
                  tok(256), _full((D, D))],
        out_specs=[tok(D), tok(D), tok(512)],
        out_shape=[jax.ShapeDtypeStruct((B, S, D), F32), jax.ShapeDtypeStruct((B, S, D), F32),
                   jax.ShapeDtypeStruct((B, S, 512), _MXU)],
        compiler_params=_cp(("parallel", "parallel")),
    )(x, gate, r_g, o_mla, mla_p, g_g, w_out)


def _out_bwd(dx, y, gate, r_g, mm, g_g, w_out):
    B, S, D = dx.shape
    tm = min(S, 512)

    def body(dx_ref, y_ref, g_ref, r_ref, mm_ref, gg_ref, w_ref, dr_ref, dmm_ref, dg_ref, dw_ref, dgate_ref):
        first = (pl.program_id(0) == 0) & (pl.program_id(1) == 0)

        @pl.when(first)
        def _():
            dw_ref[...] = jnp.zeros_like(dw_ref)

        @pl.when(pl.program_id(1) == 0)
        def _():
            dgate_ref[...] = jnp.zeros_like(dgate_ref)

        dxv = dx_ref[0]
        dgate_ref[0] += jnp.sum(dxv * y_ref[0], axis=0, keepdims=True)
        dy = (dxv * g_ref[0]).astype(_MXU)
        dr_ref[0] = _mm_nt(dy, w_ref[0:256, :])
        dmm_ref[0] = _mm_nt(dy, w_ref[256:768, :])
        dg_ref[0] = _mm_nt(dy, w_ref[768:1024, :])
        dw_ref[0:256, :] += _mm_tn(r_ref[0], dy)
        dw_ref[256:768, :] += _mm_tn(mm_ref[0], dy)
        dw_ref[768:1024, :] += _mm_tn(gg_ref[0], dy)

    tok = lambda w: pl.BlockSpec((1, tm, w), lambda b, i: (b, i, 0))
    per_seq = pl.BlockSpec((1, 1, D), lambda b, i: (b, 0, 0))
    return pl.pallas_call(
        body, name="out_bwd", grid=(B, S // tm),
        in_specs=[tok(D), tok(D), per_seq, tok(256), tok(512), tok(256), _full((D, D))],
        out_specs=[tok(256), tok(512), tok(256), _full((D, D)), per_seq],
        out_shape=[jax.ShapeDtypeStruct((B, S, 256), F32), jax.ShapeDtypeStruct((B, S, 512), F32),
                   jax.ShapeDtypeStruct((B, S, 256), F32), jax.ShapeDtypeStruct((D, D), F32),
                   jax.ShapeDtypeStruct((B, 1, D), F32)],
        compiler_params=_cp(("arbitrary", "arbitrary")),
    )(dx, y, gate, r_g, mm, g_g, w_out)


def _proj_bwd_x(x, shift, scale, nw, w_arr, d_ret, d_mla, d_mz, d_gla, dx_out):
    B, S, D = x.shape
    tm = min(S, 256)

    def body(x_ref, sc_ref, nw_ref, w_ref, dr_ref, dm_ref, dz_ref, dg_ref, dxo_ref,
             dx_ref, dp_ref, dsh_ref, dsc_ref, dnw_ref):
        first = (pl.program_id(0) == 0) & (pl.program_id(1) == 0)

        @pl.when(first)
        def _():
            dnw_ref[...] = jnp.zeros_like(dnw_ref)

        @pl.when(pl.program_id(1) == 0)
        def _():
            dsh_ref[...] = jnp.zeros_like(dsh_ref)
            dsc_ref[...] = jnp.zeros_like(dsc_ref)

        dp = jnp.concatenate([dr_ref[0], dm_ref[0], dz_ref[0], dg_ref[0]], axis=1).astype(_MXU)
        dp_ref[0] = dp
        dh = lax.dot_general(dp, w_ref[...], (((1,), (1,)), ((), ())), preferred_element_type=F32)
        xv = x_ref[0]
        rstd = lax.rsqrt(jnp.mean(xv * xv, axis=-1, keepdims=True) + EPS)
        xh = xv * rstd
        nwv = nw_ref[...]
        mod = 1.0 + sc_ref[0]
        dsh_ref[0] += jnp.sum(dh, axis=0, keepdims=True)
        dsc_ref[0] += jnp.sum(dh * xh * nwv, axis=0, keepdims=True)
        dnw_ref[...] += jnp.sum(dh * xh * mod, axis=0, keepdims=True)
        dxh = dh * nwv * mod
        dx_ref[0] = dxo_ref[0] + rstd * (dxh - xh * jnp.mean(dxh * xh, axis=-1, keepdims=True))

    tok = lambda w: pl.BlockSpec((1, tm, w), lambda b, i: (b, i, 0))
    per_seq = pl.BlockSpec((1, 1, D), lambda b, i: (b, 0, 0))
    return pl.pallas_call(
        body, name="proj_bwd_x", grid=(B, S // tm),
        in_specs=[tok(D), per_seq, _full((1, D)), _full((D, ARR_W)), tok(RET_W), tok(512), tok(512),
                  tok(GLA_W), tok(D)],
        out_specs=[tok(D), tok(ARR_W), per_seq, per_seq, _full((1, D))],
        out_shape=[jax.ShapeDtypeStruct((B, S, D), F32), jax.ShapeDtypeStruct((B, S, ARR_W), _MXU),
                   jax.ShapeDtypeStruct((B, 1, D), F32), jax.ShapeDtypeStruct((B, 1, D), F32),
                   jax.ShapeDtypeStruct((1, D), F32)],
        compiler_params=_cp(("arbitrary", "arbitrary")),
    )(x, scale, nw, w_arr, d_ret, d_mla, d_mz, d_gla, dx_out)


def _proj_bwd_w(h, dp):
    B, S, D = h.shape
    tm = min(S, 512)
    tn = 128 * 23 // 1
    assert ARR_W == tn

    def body(h_ref, dp_ref, dw_ref):
        first = (pl.program_id(0) == 0) & (pl.program_id(1) == 0)

        @pl.when(first)
        def _():
            dw_ref[...] = jnp.zeros_like(dw_ref)

        dw_ref[...] += lax.dot_general(h_ref[0], dp_ref[0], (((0,), (0,)), ((), ())),
                                       preferred_element_type=F32)

    tok = lambda w: pl.BlockSpec((1, tm, w), lambda b, i: (b, i, 0))
    return pl.pallas_call(
        body, name="proj_bwd_w", grid=(B, S // tm),
        in_specs=[tok(D), tok(ARR_W)],
        out_specs=_full((D, ARR_W)), out_shape=jax.ShapeDtypeStruct((D, ARR_W), F32),
        compiler_params=_cp(("arbitrary", "arbitrary"), 56),
    )(h, dp)


def _final_loss(x, fw, target):
    B, S, D = x.shape
    tm = min(S, 512)

    def body(x_ref, fw_ref, t_ref, dx_ref, loss_ref, dfw_ref):
        first = (pl.program_id(0) == 0) & (pl.program_id(1) == 0)

        @pl.when(first)
        def _():
            loss_ref[...] = jnp.zeros_like(loss_ref)
            dfw_ref[...] = jnp.zeros_like(dfw_ref)

        xv = x_ref[0]
        fwv = fw_ref[...]
        rstd = lax.rsqrt(jnp.mean(xv * xv, axis=-1, keepdims=True) + EPS)
        xh = xv * rstd
        err = xh * fwv - t_ref[0]
        loss_ref[...] += 0.5 * jnp.sum(jnp.mean(err * err, axis=-1, keepdims=True), axis=0, keepdims=True)
        dy = err * (1.0 / D)
        dfw_ref[...] += jnp.sum(dy * xh, axis=0, keepdims=True)
        dxh = dy * fwv
        dx_ref[0] = rstd * (dxh - xh * jnp.mean(dxh * xh, axis=-1, keepdims=True))

    tok = pl.BlockSpec((1, tm, D), lambda b, i: (b, i, 0))
    return pl.pallas_call(
        body, name="final_loss", grid=(B, S // tm),
        in_specs=[tok, _full((1, D)), tok],
        out_specs=[tok, _full((1, 1)), _full((1, D))],
        out_shape=[jax.ShapeDtypeStruct((B, S, D), F32), jax.ShapeDtypeStruct((1, 1), F32),
                   jax.ShapeDtypeStruct((1, D), F32)],
        compiler_params=_cp(("arbitrary", "arbitrary")),
    )(x, fw, target)


def _local_step(x, pos3, mod, loss_target, small, w_in_a, w_uq_a, w_ukv_a, w_out_b):
    B, S, D = x.shape
    cr, sr, cm, sm = _rope_tables(pos3)
    saved = []
    for l in range(DEPTH):
        shift = mod[l, :, 0:D].reshape(B, 1, D)
        scale = mod[l, :, D:2 * D].reshape(B, 1, D)
        gate = mod[l, :, 2 * D:3 * D].reshape(B, 1, D)
        nw = small["norm_w"][l].reshape(1, D)
        qnw = small["mla_q_norm"][l].reshape(1, 256)
        kvnw = small["mla_kv_norm"][l].reshape(1, 128)
        w_g2p = jnp.pad(small["gla_w_g2"][l], ((0, 112), (0, 0)))
        b_g2 = small["gla_b_g2"][l].reshape(1, 128)
        gnw = jnp.tile(small["gla_norm"][l], 4).reshape(1, 256)
        ret_p, mla_p, gla_p, h = _proj_fwd(x, shift, scale, nw, w_in_a[l])
        r_g, r_raw, r_st = _ret_fwd(ret_p, cr, sr)
        q, k, v, kt, vt = _mla_prep_fwd(mla_p, cm, sm, qnw, kvnw, w_uq_a[l], w_ukv_a[l])
        o_mla, lse = _mla_attn_fwd(q, k, vt)
        g_g, g_raw, g_st = _gla_fwd(gla_p, w_g2p, b_g2, gnw)
        x_new, y, mm = _out_fwd(x, gate, r_g, o_mla, mla_p, g_g, w_out_b[l])
        saved.append(dict(x=x, shift=shift, scale=scale, gate=gate, nw=nw, qnw=qnw, kvnw=kvnw, w_g2p=w_g2p,
                          b_g2=b_g2, gnw=gnw, ret_p=ret_p, mla_p=mla_p, gla_p=gla_p, h=h, r_g=r_g, r_raw=r_raw,
                          r_st=r_st, q=q, k=k, v=v, kt=kt, o_mla=o_mla, lse=lse, g_g=g_g, g_raw=g_raw, g_st=g_st,
                          y=y, mm=mm))
        x = x_new

    dx, loss, d_fw = _final_loss(x, small["final_norm"].reshape(1, D), loss_target)
    grads = dict(final_norm=d_fw.reshape(D))
    per_layer = []
    for l in reversed(range(DEPTH)):
        s = saved[l]
        d_r, d_mm, d_g, dw_out, d_gate = _out_bwd(dx, s["y"], s["gate"], s["r_g"], s["mm"], s["g_g"], w_out_b[l])
        d_ret = _ret_bwd(s["ret_p"], cr, sr, s["r_raw"], s["r_st"], d_r)
        do, d_mz, dl = _mla_gate_bwd(d_mm, s["o_mla"], s["mla_p"])
        dq, dk, dv = _mla_attn_bwd(s["q"], s["k"], s["v"], s["kt"], do, s["lse"], dl)
        d_mla, dw_uq, dw_ukv, d_qnw, d_kvnw = _mla_prep_bwd(
            s["mla_p"], cm, sm, s["qnw"], s["kvnw"], w_uq_a[l], w_ukv_a[l], dq, dk, dv)
        d_gla, dw_g2p, db_g2, d_gnw = _gla_bwd(s["gla_p"], s["w_g2p"], s["b_g2"], s["gnw"], s["g_raw"],
                                                s["g_st"], d_g)
        dx, dp, d_shift, d_scale, d_nw = _proj_bwd_x(s["x"], s["shift"], s["scale"], s["nw"], w_in_a[l],
                                                     d_ret, d_mla, d_mz, d_gla, dx)
        dw_in = _proj_bwd_w(s["h"], dp)
        per_layer.append(dict(
            d_mod=jnp.concatenate([d_shift, d_scale, d_gate], axis=2).reshape(B, 3 * D),
            norm_w=d_nw.reshape(D), mla_q_norm=d_qnw.reshape(256), mla_kv_norm=d_kvnw.reshape(128),
            gla_w_g2=dw_g2p[0:16], gla_b_g2=db_g2.reshape(128), gla_norm256=d_gnw.reshape(256),
            w_in_a=dw_in, w_uq_a=dw_uq, w_ukv_a=dw_ukv, w_out=dw_out))
    per_layer = per_layer[::-1]
    for name in per_layer[0]:
        grads[name] = jnp.stack([per_layer[l][name] for l in range(DEPTH)])
    return loss, dx, grads


def _exchange(arrs, gather, name):
    n = len(arrs)
    out_shape = [jax.ShapeDtypeStruct(((N_DEV,) + a.shape) if g else a.shape, a.dtype)
                 for a, g in zip(arrs, gather)]

    def body(*refs):
        ins, outs = refs[:n], refs[n:2 * n]
        send_sems, recv_sems, local_sems = refs[2 * n:]
        ix, iy, ic = lax.axis_index("x"), lax.axis_index("y"), lax.axis_index("c")
        me = 4 * ix + 2 * iy + ic
        copies = []
        for a in range(n):
            mine = ins[a] if gather[a] else ins[a].at[me]
            loc = pltpu.make_async_copy(mine, outs[a].at[me], local_sems.at[a])
            loc.start()
            copies.append(loc)
            for d in range(1, N_DEV):
                px = 1 - ix if d & 4 else ix
                py = 1 - iy if d & 2 else iy
                pc = 1 - ic if d & 1 else ic
                src = ins[a] if gather[a] else ins[a].at[4 * px + 2 * py + pc]
                cp = pltpu.make_async_remote_copy(
                    src_ref=src, dst_ref=outs[a].at[me], send_sem=send_sems.at[a, d - 1],
                    recv_sem=recv_sems.at[a, d - 1], device_id=(px, py, pc), device_id_type=pl.DeviceIdType.MESH)
                cp.start()
                copies.append(cp)
        for cp in copies:
            cp.wait()

    any_spec = pl.BlockSpec(memory_space=pl.ANY)
    outs = pl.pallas_call(
        body, name=name, in_specs=[any_spec] * n, out_specs=[any_spec] * n, out_shape=out_shape,
        scratch_shapes=[pltpu.SemaphoreType.DMA((n, N_DEV - 1)), pltpu.SemaphoreType.DMA((n, N_DEV - 1)),
                        pltpu.SemaphoreType.DMA((n,))],
    )(*arrs)
    return list(outs)


def _ada_fwd(c_all, ada_w, ada_b_cols):
    nb, D = c_all.shape
    cols = ada_w.shape[2]

    def body(c_ref, w_ref, b_ref, out_ref):
        ca = _silu(c_ref[...])
        for l in range(DEPTH):
            out_ref[l] = _mm(ca, w_ref[l]) + b_ref[l:l + 1, :]

    return pl.pallas_call(
        body, name="ada_fwd", out_shape=jax.ShapeDtypeStruct((DEPTH, nb, cols), F32),
        in_specs=[pl.BlockSpec(memory_space=pltpu.VMEM)] * 3, out_specs=pl.BlockSpec(memory_space=pltpu.VMEM),
        compiler_params=pltpu.CompilerParams(vmem_limit_bytes=32 * VMEM_MB),
    )(c_all, ada_w, ada_b_cols)


def _ada_bwd(c_all, d_mod_cols):
    nb, D = c_all.shape
    cols = d_mod_cols.shape[2]

    def body(c_ref, dm_ref, out_ref):
        ca = _silu(c_ref[...])
        for l in range(DEPTH):
            out_ref[l] = _mm_tn(ca, dm_ref[l])

    return pl.pallas_call(
        body, name="ada_bwd", out_shape=jax.ShapeDtypeStruct((DEPTH, D, cols), F32),
        in_specs=[pl.BlockSpec(memory_space=pltpu.VMEM)] * 2, out_specs=pl.BlockSpec(memory_space=pltpu.VMEM),
        compiler_params=pltpu.CompilerParams(vmem_limit_bytes=32 * VMEM_MB),
    )(c_all, d_mod_cols)


def _sum_adamw(parts, w, m, v, name):
    P, R, C = parts.shape
    tr = 256 if (R % 256 == 0 and R > 256) else R

    def body(p_ref, w_ref, m_ref, v_ref, g_ref, d_ref, nm_ref, nv_ref):
        g = p_ref[0].astype(F32)
        for k in range(1, P):
            g = g + p_ref[k].astype(F32)
        g_ref[...] = g
        nm = ADAM_B1 * m_ref[...] + (1.0 - ADAM_B1) * g
        nv = ADAM_B2 * v_ref[...] + (1.0 - ADAM_B2) * (g * g)
        nm_ref[...] = nm
        nv_ref[...] = nv
        m_hat = nm / (1.0 - ADAM_B1 ** ADAM_STEP)
        v_hat = nv / (1.0 - ADAM_B2 ** ADAM_STEP)
        d_ref[...] = -ADAM_LR * (m_hat / (jnp.sqrt(v_hat) + ADAM_EPS) + ADAM_WD * w_ref[...])

    blk = pl.BlockSpec((tr, C), lambda i: (i, 0))
    shp = jax.ShapeDtypeStruct((R, C), F32)
    return pl.pallas_call(
        body, name=name, grid=(R // tr,),
        in_specs=[pl.BlockSpec((P, tr, C), lambda i: (0, i, 0)), blk, blk, blk],
        out_specs=[blk, blk, blk, blk], out_shape=[shp, shp, shp, shp],
        compiler_params=_cp(("parallel",)),
    )(parts, w, m, v)


SMALL = [("norm_w", DEPTH * 1024), ("mla_q_norm", DEPTH * 256), ("mla_kv_norm", DEPTH * 128),
         ("gla_w_g2", DEPTH * 16 * 128), ("gla_b_g2", DEPTH * 128), ("gla_norm", DEPTH * 64), ("final_norm", 1024)]
SMALL_ROWS = 72


def _pack_small(first_row, vals):
    flat = [first_row.reshape(128)] + [vals[n].reshape(-1) for n, _ in SMALL]
    used = 128 + sum(s for _, s in SMALL)
    flat.append(jnp.zeros((SMALL_ROWS * 128 - used,), F32))
    return jnp.concatenate(flat).reshape(SMALL_ROWS, 128)


def _unpack_small(packed, shapes):
    flat = packed.reshape(-1)
    out, off = {}, 128
    for n, s in SMALL:
        out[n] = flat[off:off + s].reshape(shapes[n])
        off += s
    return out


WEIGHTS = ["norm_w", "ada_w", "ada_b", "w_in", "mla_q_norm", "w_uq", "mla_kv_norm", "w_ukv", "gla_w_g2",
           "gla_b_g2", "gla_norm", "w_out", "final_norm"]


def kernel(x, c, positions, norm_w, ada_w, ada_b, w_in, mla_q_norm, w_uq, mla_kv_norm, w_ukv, gla_w_g2, gla_b_g2, gla_norm, w_out, final_norm, loss_target, m_norm_w, m_ada_w, m_ada_b, m_w_in, m_mla_q_norm, m_w_uq, m_mla_kv_norm, m_w_ukv, m_gla_w_g2, m_gla_b_g2, m_gla_norm, m_w_out, m_final_norm, v_norm_w, v_ada_w, v_ada_b, v_w_in, v_mla_q_norm, v_w_uq, v_mla_kv_norm, v_w_ukv, v_gla_w_g2, v_gla_b_g2, v_gla_norm, v_w_out, v_final_norm):
    w = dict(norm_w=norm_w, ada_w=ada_w, ada_b=ada_b, w_in=w_in, mla_q_norm=mla_q_norm, w_uq=w_uq,
             mla_kv_norm=mla_kv_norm, w_ukv=w_ukv, gla_w_g2=gla_w_g2, gla_b_g2=gla_b_g2, gla_norm=gla_norm,
             w_out=w_out, final_norm=final_norm)
    m = dict(norm_w=m_norm_w, ada_w=m_ada_w, ada_b=m_ada_b, w_in=m_w_in, mla_q_norm=m_mla_q_norm, w_uq=m_w_uq,
             mla_kv_norm=m_mla_kv_norm, w_ukv=m_w_ukv, gla_w_g2=m_gla_w_g2, gla_b_g2=m_gla_b_g2,
             gla_norm=m_gla_norm, w_out=m_w_out, final_norm=m_final_norm)
    v = dict(norm_w=v_norm_w, ada_w=v_ada_w, ada_b=v_ada_b, w_in=v_w_in, mla_q_norm=v_mla_q_norm, w_uq=v_w_uq,
             mla_kv_norm=v_mla_kv_norm, w_ukv=v_w_ukv, gla_w_g2=v_gla_w_g2, gla_b_g2=v_gla_b_g2,
             gla_norm=v_gla_norm, w_out=v_w_out, final_norm=v_final_norm)
    B, S, D = x.shape
    me = 4 * lax.axis_index("x") + 2 * lax.axis_index("y") + lax.axis_index("c")
    ada_cols = ada_w.shape[2]
    cast = lambda a: a.astype(_MXU)

    c_g, w_in_g, w_uq_g, w_ukv_g, w_out_g = _exchange(
        [c, cast(w_in), cast(w_uq), cast(w_ukv), cast(w_out)], [True] * 5, "gather_weights")
    c_all = c_g.reshape(N_DEV * B, D)
    w_in_full = jnp.transpose(w_in_g, (1, 2, 0, 3)).reshape(DEPTH, D, -1)
    w_uq_full = jnp.transpose(w_uq_g, (1, 2, 0, 3)).reshape(DEPTH, 256, -1)
    w_ukv_full = jnp.transpose(w_ukv_g, (1, 2, 0, 3)).reshape(DEPTH, 128, -1)
    w_out_full = jnp.transpose(w_out_g, (1, 0, 2, 3)).reshape(DEPTH, D, D)
    w_in_a = jnp.stack([_arrange_w_in(w_in_full[l]) for l in range(DEPTH)])
    w_uq_a = jnp.stack([_arrange_w_uq(w_uq_full[l]) for l in range(DEPTH)])
    w_ukv_a = jnp.stack([_arrange_w_ukv(w_ukv_full[l]) for l in range(DEPTH)])

    ada_b_cols = lax.dynamic_slice(ada_b, (0, me * ada_cols), (DEPTH, ada_cols))
    mod_cols = _ada_fwd(c_all, ada_w, ada_b_cols)
    mod_send = jnp.transpose(mod_cols.reshape(DEPTH, N_DEV, B, ada_cols), (1, 0, 2, 3))
    (mod_recv,) = _exchange([mod_send], [False], "scatter_mod")
    mod = jnp.transpose(mod_recv, (1, 2, 0, 3)).reshape(DEPTH, B, 3 * D)

    small_w = {n: w[n] for n, _ in SMALL}
    loss, grad_x, g = _local_step(x, positions.reshape(B, S, 1), mod, loss_target, small_w,
                                  w_in_a, w_uq_a, w_ukv_a, w_out_full)

    d_mod = g["d_mod"]
    part = dict(norm_w=g["norm_w"], mla_q_norm=g["mla_q_norm"], mla_kv_norm=g["mla_kv_norm"],
                gla_w_g2=g["gla_w_g2"], gla_b_g2=g["gla_b_g2"], gla_norm=g["gla_norm256"][:, 0:64],
                final_norm=g["final_norm"])
    small_part = _pack_small(jnp.pad(loss.reshape(1), (0, 127)), part)
    dw_in = jnp.stack([_unarrange_w_in(g["w_in_a"][l]) for l in range(DEPTH)])
    dw_uq = jnp.stack([_unarrange_w_uq(g["w_uq_a"][l]) for l in range(DEPTH)])
    dw_ukv = jnp.stack([_unarrange_w_ukv(g["w_ukv_a"][l]) for l in range(DEPTH)])
    to_cols = lambda a: jnp.transpose(a.reshape(a.shape[0], a.shape[1], N_DEV, -1), (2, 0, 1, 3)).astype(jnp.bfloat16)
    dw_out_send = jnp.transpose(g["w_out"].reshape(DEPTH, N_DEV, D // N_DEV, D), (1, 0, 2, 3)).astype(jnp.bfloat16)
    d_mod_g, small_g, dw_in_g, dw_uq_g, dw_ukv_g, dw_out_g = _exchange(
        [d_mod, small_part, to_cols(dw_in), to_cols(dw_uq), to_cols(dw_ukv), dw_out_send],
        [True, True, False, False, False, False], "exchange_grads")

    d_mod_all = jnp.transpose(d_mod_g, (1, 0, 2, 3)).reshape(DEPTH, N_DEV * B, 3 * D)
    d_mod_cols = lax.dynamic_slice(d_mod_all, (0, 0, me * ada_cols), (DEPTH, N_DEV * B, ada_cols))
    g_ada_w = _ada_bwd(c_all, d_mod_cols)

    res = {}

    def update(name, parts2d):
        shp = w[name].shape
        two = lambda a: a.reshape(parts2d.shape[1:])
        out = _sum_adamw(parts2d, two(w[name]), two(m[name]), two(v[name]), "adamw_" + name)
        res[name] = [o.reshape(shp) for o in out]

    update("ada_w", g_ada_w.reshape(1, DEPTH * D, ada_cols))
    update("ada_b", jnp.transpose(d_mod_g, (0, 2, 1, 3)).reshape(N_DEV * B, DEPTH * 3 * D // 128, 128))
    update("w_in", dw_in_g.reshape(N_DEV, DEPTH * D, -1))
    update("w_uq", dw_uq_g.reshape(N_DEV, DEPTH * 256, -1))
    update("w_ukv", dw_ukv_g.reshape(N_DEV, DEPTH * 128, -1))
    update("w_out", dw_out_g.reshape(N_DEV, DEPTH * (D // N_DEV), D))
    zero_row = jnp.zeros((128,), F32)
    small_out = _sum_adamw(small_g, _pack_small(zero_row, small_w), _pack_small(zero_row, {n: m[n] for n, _ in SMALL}),
                           _pack_small(zero_row, {n: v[n] for n, _ in SMALL}), "adamw_small")
    shapes = {n: w[n].shape for n, _ in SMALL}
    unpacked = [_unpack_small(o, shapes) for o in small_out]
    for n, _ in SMALL:
        res[n] = [u[n] for u in unpacked]
    loss_out = small_out[0][0, 0]
    return (loss_out, grad_x, *[res[n][0] for n in WEIGHTS], *[res[n][1] for n in WEIGHTS],
            *[res[n][2] for n in WEIGHTS], *[res[n][3] for n in WEIGHTS])
```

```python
import functools
import math

import numpy as np
import jax
import jax.numpy as jnp
from jax import lax
from jax.experimental import pallas as pl
from jax.experimental.pallas import tpu as pltpu

F32 = jnp.float32
_MXU = jnp.bfloat16

D_MODEL = 1024
DEPTH = 2
CHUNK = 64
EPS = 1e-6
ROPE_THETA = 10000.0
N_DEV = 8

MLA_SCALE = 96.0 ** -0.5
RET_KSCALE = 64.0 ** -0.5
GLA_KSCALE = 32.0 ** -0.5
GLA_TAU = 16.0

ADAM_LR = 0.001
ADAM_B1 = 0.9
ADAM_B2 = 0.999
ADAM_EPS = 1e-08
ADAM_WD = 0.01
ADAM_STEP = 10

RET_W, MLA_W, GLA_W = 1024, 1024, 896
ARR_W = RET_W + MLA_W + GLA_W
VMEM_MB = 1024 * 1024


def _cp(sem, vmem_mb=48):
    return pltpu.CompilerParams(dimension_semantics=sem, vmem_limit_bytes=vmem_mb * VMEM_MB)


def _mm(a, b):
    return jnp.dot(a.astype(_MXU), b.astype(_MXU), preferred_element_type=F32)


def _mm_nt(a, b):
    return lax.dot_general(a.astype(_MXU), b.astype(_MXU), (((1,), (1,)), ((), ())),
                           preferred_element_type=F32)


def _mm_tn(a, b):
    return lax.dot_general(a.astype(_MXU), b.astype(_MXU), (((0,), (0,)), ((), ())),
                           preferred_element_type=F32)


def _mm_f32(a, b):
    return jnp.dot(a, b, precision=lax.Precision.HIGHEST, preferred_element_type=F32)


def _sig(z):
    return 1.0 / (1.0 + jnp.exp(-z))


def _silu(z):
    return z * _sig(z)


def _dsilu(z):
    s = _sig(z)
    return s * (1.0 + z * (1.0 - s))


def _full(shape):
    nd = len(shape)
    return pl.BlockSpec(shape, lambda *_: (0,) * nd)


def _qk_perm(blk):
    r = blk.shape[0]
    return jnp.transpose(blk.reshape(r, 4, 2, 32), (0, 2, 1, 3)).reshape(r, 256)


def _qk_unperm(blk):
    r = blk.shape[0]
    return jnp.transpose(blk.reshape(r, 2, 4, 32), (0, 2, 1, 3)).reshape(r, 256)


def _arrange_w_in(w):
    z = lambda n: jnp.zeros((w.shape[0], n), w.dtype)
    ret = [_qk_perm(w[:, 0:256]), _qk_perm(w[:, 256:512]), w[:, 512:768], w[:, 768:1024]]
    mla = [w[:, 1024:1280], w[:, 1280:1408], z(64), w[:, 1408:1440], z(32), w[:, 1440:1952]]
    gla = [w[:, 1952:2080], w[:, 2080:2208], w[:, 2208:2464], w[:, 2464:2480], z(112), w[:, 2480:2736]]
    return jnp.concatenate(ret + mla + gla, axis=1)


def _unarrange_w_in(a):
    m, g = RET_W, RET_W + MLA_W
    parts = [_qk_unperm(a[:, 0:256]), _qk_unperm(a[:, 256:512]), a[:, 512:1024],
             a[:, m:m + 384], a[:, m + 448:m + 480], a[:, m + 512:m + 1024],
             a[:, g:g + 528], a[:, g + 640:g + 896]]
    return jnp.concatenate(parts, axis=1)


def _arrange_w_uq(w):
    return jnp.pad(w.reshape(256, 8, 96), ((0, 0), (0, 0), (0, 32))).reshape(256, 1024)


def _unarrange_w_uq(a):
    return a.reshape(256, 8, 128)[:, :, :96].reshape(256, 768)


def _arrange_w_ukv(w):
    r = w.reshape(128, 8, 128)
    k = jnp.pad(r[:, :, :64], ((0, 0), (0, 0), (0, 64))).reshape(128, 1024)
    return jnp.concatenate([k, r[:, :, 64:].reshape(128, 512)], axis=1)


def _unarrange_w_ukv(a):
    k = a[:, :1024].reshape(128, 8, 128)[:, :, :64]
    v = a[:, 1024:].reshape(128, 8, 64)
    return jnp.concatenate([k, v], axis=2).reshape(128, 1024)


def _rope_tables(pos3):
    B, S, _ = pos3.shape
    ts = min(S, 512)
    inv32 = (np.float32(ROPE_THETA) ** (-(np.arange(32, dtype=np.float32) / 32))).astype(np.float32)
    inv16 = (np.float32(ROPE_THETA) ** (-(np.arange(16, dtype=np.float32) / 16))).astype(np.float32)
    inv_r = np.tile(inv32, 4)[None, :]
    inv_m = np.zeros((1, 128), np.float32)
    inv_m[0, 64:80] = inv16
    inv_m[0, 80:96] = inv16

    def body(pos_ref, ir_ref, im_ref, cr, sr, cm, sm):
        p = pos_ref[0].astype(F32)
        ar = p * ir_ref[...]
        cr[0] = jnp.cos(ar)
        sr[0] = jnp.sin(ar)
        am = p * im_ref[...]
        cm[0] = jnp.cos(am)
        sm[0] = jnp.sin(am)

    tab = jax.ShapeDtypeStruct((B, S, 128), F32)
    blk = pl.BlockSpec((1, ts, 128), lambda b, i: (b, i, 0))
    return pl.pallas_call(
        body, name="rope_tables", grid=(B, S // ts),
        in_specs=[pl.BlockSpec((1, ts, 1), lambda b, i: (b, i, 0)), _full((1, 128)), _full((1, 128))],
        out_specs=[blk, blk, blk, blk], out_shape=[tab, tab, tab, tab],
        compiler_params=_cp(("parallel", "parallel")),
    )(pos3, jnp.asarray(inv_r), jnp.asarray(inv_m))


def _rope128(x, cos, sin):
    lane = lax.broadcasted_iota(jnp.int32, (1, 128), 1)
    rp = pltpu.roll(x, 16, 1)
    rm = pltpu.roll(x, 112, 1)
    return x * cos + jnp.where(lane < 80, -rm, rp) * sin


def _rope128_t(d, cos, sin):
    lane = lax.broadcasted_iota(jnp.int32, (1, 128), 1)
    y = d * sin
    yp = pltpu.roll(y, 16, 1)
    ym = pltpu.roll(y, 112, 1)
    return d * cos + jnp.where(lane < 64, 0.0, jnp.where(lane < 80, ym, jnp.where(lane < 96, -yp, 0.0)))


def _proj_fwd(x, shift, scale, nw, w_arr):
    B, S, D = x.shape
    tm = min(S, 512)

    def body(x_ref, sh_ref, sc_ref, nw_ref, w_ref, ret_ref, mla_ref, gla_ref, h_ref):
        xv = x_ref[0]
        rstd = lax.rsqrt(jnp.mean(xv * xv, axis=-1, keepdims=True) + EPS)
        h = (xv * rstd * nw_ref[...]) * (1.0 + sc_ref[0]) + sh_ref[0]
        hb = h.astype(_MXU)
        h_ref[0] = hb
        ret_ref[0] = jnp.dot(hb, w_ref[:, 0:RET_W], preferred_element_type=F32)
        mla_ref[0] = jnp.dot(hb, w_ref[:, RET_W:RET_W + MLA_W], preferred_element_type=F32)
        gla_ref[0] = jnp.dot(hb, w_ref[:, RET_W + MLA_W:ARR_W], preferred_element_type=F32)

    tok = lambda w: pl.BlockSpec((1, tm, w), lambda b, i: (b, i, 0))
    per_seq = pl.BlockSpec((1, 1, D), lambda b, i: (b, 0, 0))
    return pl.pallas_call(
        body, name="proj_fwd", grid=(B, S // tm),
        in_specs=[tok(D), per_seq, per_seq, _full((1, D)), _full((D, ARR_W))],
        out_specs=[tok(RET_W), tok(MLA_W), tok(GLA_W), tok(D)],
        out_shape=[jax.ShapeDtypeStruct((B, S, RET_W), F32), jax.ShapeDtypeStruct((B, S, MLA_W), F32),
                   jax.ShapeDtypeStruct((B, S, GLA_W), F32), jax.ShapeDtypeStruct((B, S, D), _MXU)],
        compiler_params=_cp(("parallel", "parallel")),
    )(x, shift, scale, nw, w_arr)


RET_L = 256


def _ret_consts(L):
    lg = np.log1p(-np.exp2(-5.0 - np.arange(4, dtype=np.float32))).astype(np.float32)
    i = np.arange(L)
    ci = i // CHUNK
    diff = (i[:, None] - i[None, :]).astype(np.float32)
    same = ci[:, None] == ci[None, :]
    past = ci[None, :] < ci[:, None]
    expo = np.where(same, np.abs(diff), np.where(past, diff, 0.0)).astype(np.float32)
    dec = np.where((same | past)[None], np.exp(lg[:, None, None] * expo[None]), 0.0).astype(np.float32)
    head = (np.arange(256) % 128) // 32
    qw = np.exp((i + 1.0)[:, None] * lg[head][None, :]).astype(np.float32)
    kw = np.exp((L - 1.0 - i)[:, None] * lg[head][None, :]).astype(np.float32)
    a_row = np.exp(np.float32(L) * lg[head])[None, :].astype(np.float32)
    return [jnp.asarray(t) for t in (dec.reshape(4 * L, L), qw, kw, a_row)]


def _ret_masks():
    lane = lax.broadcasted_iota(jnp.int32, (1, 256), 1)
    mh = [((lane % 128) // 32) == h for h in range(4)]
    mv = [(lane // 64) == h for h in range(4)]
    vi = lax.broadcasted_iota(jnp.int32, (256, 256), 0)
    ki = lax.broadcasted_iota(jnp.int32, (256, 256), 1)
    bd = (vi // 64) == ((ki % 128) // 32)
    return mh, mv, bd


def _ret_rope(p, cs, sn):
    q1, q2, k1, k2 = p[:, 0:128], p[:, 128:256], p[:, 256:384], p[:, 384:512]
    qr = jnp.concatenate([q1 * cs - q2 * sn, q2 * cs + q1 * sn], axis=1)
    kr = jnp.concatenate([k1 * cs - k2 * sn, k2 * cs + k1 * sn], axis=1) * RET_KSCALE
    return qr, kr


def _head_mean(x, mv, width):
    out = jnp.zeros_like(x)
    for m in mv:
        s = jnp.sum(jnp.where(m, x, 0.0), axis=-1, keepdims=True) * (1.0 / width)
        out = jnp.where(m, s, out)
    return out


def _stack_heads(x, masks):
    return jnp.concatenate([jnp.where(m, x, 0.0) for m in masks], axis=0)


def _fold_heads(xs, masks, L):
    out = jnp.where(masks[0], xs[0:L], 0.0)
    for h in range(1, 4):
        out = out + jnp.where(masks[h], xs[h * L:(h + 1) * L], 0.0)
    return out


def _ret_fwd(ret_p, cos, sin):
    B, S, _ = ret_p.shape
    L = min(RET_L, S)
    NB = S // L
    consts = _ret_consts(L)

    def body(p_ref, c_ref, s_ref, ds_ref, qw_ref, kw_ref, a_ref, out_ref, raw_ref, st_ref, st_sc):
        @pl.when(pl.program_id(1) == 0)
        def _():
            st_sc[...] = jnp.zeros_like(st_sc)

        mh, mv, bd = _ret_masks()
        p = p_ref[0]
        qr, kr = _ret_rope(p, c_ref[0], s_ref[0])
        v = p[:, 512:768]
        z = p[:, 768:1024]
        a_s = _mm_nt(_stack_heads(qr, mh), kr) * ds_ref[...]
        intra = _fold_heads(_mm(a_s, v), mv, L)
        st = st_sc[...]
        st_ref[0, 0] = st
        r = intra + _mm_nt(qr * qw_ref[...], st)
        raw_ref[0] = r
        st_sc[...] = st * a_ref[...] + jnp.where(bd, _mm_tn(v, kr * kw_ref[...]), 0.0)
        rstd = lax.rsqrt(_head_mean(r * r, mv, 64.0) + EPS)
        out_ref[0] = (r * rstd * _silu(z)).astype(_MXU)

    tok = lambda w: pl.BlockSpec((1, L, w), lambda b, n: (b, n, 0))
    return pl.pallas_call(
        body, name="ret_fwd", grid=(B, NB),
        in_specs=[tok(RET_W), tok(128), tok(128), _full((4 * L, L)), _full((L, 256)), _full((L, 256)),
                  _full((1, 256))],
        out_specs=[tok(256), tok(256), pl.BlockSpec((1, 1, 256, 256), lambda b, n: (b, n, 0, 0))],
        out_shape=[jax.ShapeDtypeStruct((B, S, 256), _MXU), jax.ShapeDtypeStruct((B, S, 256), F32),
                   jax.ShapeDtypeStruct((B, NB, 256, 256), F32)],
        scratch_shapes=[pltpu.VMEM((256, 256), F32)],
        compiler_params=_cp(("parallel", "arbitrary")),
    )(ret_p, cos, sin, *consts)


def _ret_bwd(ret_p, cos, sin, raw, states, d_mix):
    B, S, _ = ret_p.shape
    L = min(RET_L, S)
    NB = S // L
    consts = _ret_consts(L)

    def body(p_ref, c_ref, s_ref, raw_ref, st_ref, dm_ref, ds_ref, qw_ref, kw_ref, a_ref, dp_ref, dst_sc):
        @pl.when(pl.program_id(1) == 0)
        def _():
            dst_sc[...] = jnp.zeros_like(dst_sc)

        mh, mv, bd = _ret_masks()
        p = p_ref[0]
        cs, sn = c_ref[0], s_ref[0]
        qr, kr = _ret_rope(p, cs, sn)
        v = p[:, 512:768]
        z = p[:, 768:1024]
        qs = _stack_heads(qr, mh)
        dec = ds_ref[...]
        a_s = _mm_nt(qs, kr) * dec
        r = raw_ref[0]
        rstd = lax.rsqrt(_head_mean(r * r, mv, 64.0) + EPS)
        rn = r * rstd
        dm = dm_ref[0]
        d_rn = dm * _silu(z)
        dz = dm * rn * _dsilu(z)
        dr = rstd * (d_rn - rn * _head_mean(d_rn * rn, mv, 64.0))
        do_s = _stack_heads(dr, mv)
        da_s = _mm_nt(do_s, v) * dec
        dv = _mm_tn(a_s, do_s)
        dqr = _fold_heads(_mm(da_s, kr), mh, L)
        dkr = _mm_tn(da_s, qs)
        st = st_ref[0, 0]
        qw, kw = qw_ref[...], kw_ref[...]
        dqr = dqr + _mm(dr, st) * qw
        dst_next = dst_sc[...]
        g = jnp.where(bd, dst_next, 0.0)
        kk = kr * kw
        dv = dv + _mm_nt(kk, g)
        dkr = dkr + _mm(v, g) * kw
        dst_sc[...] = dst_next * a_ref[...] + jnp.where(bd, _mm_tn(dr, qr * qw), 0.0)
        dkr = dkr * RET_KSCALE
        dq1, dq2 = dqr[:, 0:128], dqr[:, 128:256]
        dk1, dk2 = dkr[:, 0:128], dkr[:, 128:256]
        dp_ref[0] = jnp.concatenate(
            [dq1 * cs + dq2 * sn, dq2 * cs - dq1 * sn, dk1 * cs + dk2 * sn, dk2 * cs - dk1 * sn, dv, dz], axis=1)

    tok = lambda w: pl.BlockSpec((1, L, w), lambda b, i: (b, NB - 1 - i, 0))
    return pl.pallas_call(
        body, name="ret_bwd", grid=(B, NB),
        in_specs=[tok(RET_W), tok(128), tok(128), tok(256),
                  pl.BlockSpec((1, 1, 256, 256), lambda b, i: (b, NB - 1 - i, 0, 0)), tok(256),
                  _full((4 * L, L)), _full((L, 256)), _full((L, 256)), _full((1, 256))],
        out_specs=tok(RET_W), out_shape=jax.ShapeDtypeStruct((B, S, RET_W), F32),
        scratch_shapes=[pltpu.VMEM((256, 256), F32)],
        compiler_params=_cp(("parallel", "arbitrary")),
    )(ret_p, cos, sin, raw, states, d_mix, *consts)


def _gla_masks():
    C = CHUNK
    lk = lax.broadcasted_iota(jnp.int32, (1, 128), 1)
    lv = lax.broadcasted_iota(jnp.int32, (1, 256), 1)
    mk = [(lk // 32) == h for h in range(4)]
    mv = [(lv // 64) == h for h in range(4)]
    vi = lax.broadcasted_iota(jnp.int32, (256, 128), 0)
    ki = lax.broadcasted_iota(jnp.int32, (256, 128), 1)
    bd = (vi // 64) == (ki // 32)
    ri = lax.broadcasted_iota(jnp.int32, (4 * C, C), 0) % C
    cj = lax.broadcasted_iota(jnp.int32, (4 * C, C), 1)
    lower = ri >= cj
    ti = lax.broadcasted_iota(jnp.int32, (C, C), 0)
    tj = lax.broadcasted_iota(jnp.int32, (C, C), 1)
    ltri = jnp.where(ti >= tj, 1.0, 0.0).astype(F32)
    utri = jnp.where(ti <= tj, 1.0, 0.0).astype(F32)
    return mk, mv, bd, lower, ltri, utri


def _gla_gate(p, w_ref, b_ref, ltri):
    pre = _mm(p[:, 512:640], w_ref[...]) + b_ref[...]
    la = (jnp.minimum(pre, 0.0) - jnp.log(1.0 + jnp.exp(-jnp.abs(pre)))) * (1.0 / GLA_TAU)
    cum = _mm_f32(ltri, la)
    return pre, cum


GLA_G = 4


def _gla_fwd(gla_p, w_g2p, b_g2, gnw):
    B, S, _ = gla_p.shape
    C = CHUNK
    NC = S // C
    G = min(GLA_G, NC)
    NG = NC // G

    def body(p_ref, w_ref, b_ref, gn_ref, out_ref, raw_ref, st_ref, st_sc):
        @pl.when(pl.program_id(1) == 0)
        def _():
            st_sc[...] = jnp.zeros_like(st_sc)

        mk, mv, bd, lower, ltri, _ = _gla_masks()
        st = st_sc[...]
        for c in range(G):
            rows = slice(c * C, (c + 1) * C)
            p = p_ref[0, rows, :]
            q = p[:, 0:128]
            k = p[:, 128:256] * GLA_KSCALE
            v = p[:, 256:512]
            z = p[:, 640:896]
            _, cum = _gla_gate(p, w_ref, b_ref, ltri)
            last = cum[C - 1:C, :]
            e_pos = jnp.exp(cum)
            e_neg = jnp.exp(-cum)
            q_pos = q * e_pos
            past = _mm_nt(_stack_heads(q_pos, mk), k * e_neg)
            fut = _mm_nt(_stack_heads(q * e_neg, mk), k * e_pos)
            attn = jnp.where(lower, past, fut)
            intra = _fold_heads(_mm(attn, v), mv, C)
            st_ref[0, c] = st
            g = intra + _mm_nt(q_pos, st)
            raw_ref[0, rows, :] = g
            kd = k * jnp.exp(last - cum)
            st = st * jnp.exp(last) + jnp.where(bd, _mm_tn(v, kd), 0.0)
            rstd = lax.rsqrt(_head_mean(g * g, mv, 64.0) + EPS)
            out_ref[0, rows, :] = (g * rstd * gn_ref[...] * _silu(z)).astype(_MXU)
        st_sc[...] = st

    tok = lambda w: pl.BlockSpec((1, G * C, w), lambda b, n: (b, n, 0))
    return pl.pallas_call(
        body, name="gla_fwd", grid=(B, NG),
        in_specs=[tok(GLA_W), _full((128, 128)), _full((1, 128)), _full((1, 256))],
        out_specs=[tok(256), tok(256), pl.BlockSpec((1, G, 256, 128), lambda b, n: (b, n, 0, 0))],
        out_shape=[jax.ShapeDtypeStruct((B, S, 256), _MXU), jax.ShapeDtypeStruct((B, S, 256), F32),
                   jax.ShapeDtypeStruct((B, NC, 256, 128), F32)],
        scratch_shapes=[pltpu.VMEM((256, 128), F32)],
        compiler_params=_cp(("parallel", "arbitrary")),
    )(gla_p, w_g2p, b_g2, gnw)


def _gla_bwd(gla_p, w_g2p, b_g2, gnw, raw, states, d_mix):
    B, S, _ = gla_p.shape
    C = CHUNK
    NC = S // C
    G = min(GLA_G, NC)
    NG = NC // G

    def body(p_ref, w_ref, b_ref, gn_ref, raw_ref, st_ref, dm_ref, dp_ref, dw_ref, db_ref, dgn_ref, dst_sc):
        first = (pl.program_id(0) == 0) & (pl.program_id(1) == 0)

        @pl.when(first)
        def _():
            dw_ref[...] = jnp.zeros_like(dw_ref)
            db_ref[...] = jnp.zeros_like(db_ref)
            dgn_ref[...] = jnp.zeros_like(dgn_ref)

        @pl.when(pl.program_id(1) == 0)
        def _():
            dst_sc[...] = jnp.zeros_like(dst_sc)

        mk, mv, bd, lower, ltri, utri = _gla_masks()
        gn = gn_ref[...]
        dst_next = dst_sc[...]
        dw_acc = jnp.zeros((128, 128), F32)
        db_acc = jnp.zeros((1, 128), F32)
        dgn_acc = jnp.zeros((1, 256), F32)
        for c in reversed(range(G)):
            rows = slice(c * C, (c + 1) * C)
            p = p_ref[0, rows, :]
            q = p[:, 0:128]
            k = p[:, 128:256] * GLA_KSCALE
            v = p[:, 256:512]
            gg = p[:, 512:640]
            z = p[:, 640:896]
            pre, cum = _gla_gate(p, w_ref, b_ref, ltri)
            last = cum[C - 1:C, :]
            e_pos = jnp.exp(cum)
            e_neg = jnp.exp(-cum)
            q_pos, q_neg = q * e_pos, q * e_neg
            k_pos, k_neg = k * e_pos, k * e_neg
            qp_s = _stack_heads(q_pos, mk)
            qn_s = _stack_heads(q_neg, mk)
            attn = jnp.where(lower, _mm_nt(qp_s, k_neg), _mm_nt(qn_s, k_pos))
            g = raw_ref[0, rows, :]
            rstd = lax.rsqrt(_head_mean(g * g, mv, 64.0) + EPS)
            gh = g * rstd
            dm = dm_ref[0, rows, :]
            d_gn = dm * _silu(z)
            dz = dm * gh * gn * _dsilu(z)
            dgn_acc = dgn_acc + jnp.sum(d_gn * gh, axis=0, keepdims=True)
            d_gh = d_gn * gn
            dg = rstd * (d_gh - gh * _head_mean(d_gh * gh, mv, 64.0))
            do_s = _stack_heads(dg, mv)
            dattn = _mm_nt(do_s, v)
            dv = _mm_tn(attn, do_s)
            dpast = jnp.where(lower, dattn, 0.0)
            dfut = jnp.where(lower, 0.0, dattn)
            dq_pos = _fold_heads(_mm(dpast, k_neg), mk, C)
            dk_neg = _mm_tn(dpast, qp_s)
            dq_neg = _fold_heads(_mm(dfut, k_pos), mk, C)
            dk_pos = _mm_tn(dfut, qn_s)
            st = st_ref[0, c]
            dq_pos = dq_pos + _mm(dg, st)
            a_row = jnp.exp(last)
            d_a = jnp.sum(dst_next * st, axis=0, keepdims=True)
            gmat = jnp.where(bd, dst_next, 0.0)
            w_dec = jnp.exp(last - cum)
            kd = k * w_dec
            d_kd = _mm(v, gmat)
            dv = dv + _mm_nt(kd, gmat)
            dst_next = dst_next * a_row + jnp.where(bd, _mm_tn(dg, q_pos), 0.0)
            t = d_kd * kd
            dk = d_kd * w_dec + dk_neg * e_neg + dk_pos * e_pos
            dq = dq_pos * e_pos + dq_neg * e_neg
            d_last = jnp.sum(t, axis=0, keepdims=True) + d_a * a_row
            d_cum = dq_pos * q_pos - dk_neg * k_neg - dq_neg * q_neg + dk_pos * k_pos - t
            row = lax.broadcasted_iota(jnp.int32, (C, 128), 0)
            d_cum = d_cum + jnp.where(row == C - 1, d_last, 0.0)
            d_la = _mm_f32(utri, d_cum)
            d_pre = d_la * _sig(-pre) * (1.0 / GLA_TAU)
            d_gg = _mm_nt(d_pre, w_ref[...])
            dw_acc = dw_acc + _mm_tn(gg, d_pre)
            db_acc = db_acc + jnp.sum(d_pre, axis=0, keepdims=True)
            dp_ref[0, rows, :] = jnp.concatenate([dq, dk * GLA_KSCALE, dv, d_gg, dz], axis=1)
        dst_sc[...] = dst_next
        dw_ref[...] += dw_acc
        db_ref[...] += db_acc
        dgn_ref[...] += dgn_acc

        @pl.when((pl.program_id(0) == B - 1) & (pl.program_id(1) == NG - 1))
        def _():
            s1 = dgn_ref[...]
            s1 = s1 + pltpu.roll(s1, 128, 1)
            dgn_ref[...] = s1 + pltpu.roll(s1, 64, 1)

    tok = lambda w: pl.BlockSpec((1, G * C, w), lambda b, i: (b, NG - 1 - i, 0))
    return pl.pallas_call(
        body, name="gla_bwd", grid=(B, NG),
        in_specs=[tok(GLA_W), _full((128, 128)), _full((1, 128)), _full((1, 256)), tok(256),
                  pl.BlockSpec((1, G, 256, 128), lambda b, i: (b, NG - 1 - i, 0, 0)), tok(256)],
        out_specs=[tok(GLA_W), _full((128, 128)), _full((1, 128)), _full((1, 256))],
        out_shape=[jax.ShapeDtypeStruct((B, S, GLA_W), F32), jax.ShapeDtypeStruct((128, 128), F32),
                   jax.ShapeDtypeStruct((1, 128), F32), jax.ShapeDtypeStruct((1, 256), F32)],
        scratch_shapes=[pltpu.VMEM((256, 128), F32)],
        compiler_params=_cp(("arbitrary", "arbitrary")),
    )(gla_p, w_g2p, b_g2, gnw, raw, states, d_mix)


def _rms(x, w):
    rstd = lax.rsqrt(jnp.mean(x * x, axis=-1, keepdims=True) + EPS)
    xh = x * rstd
    return xh, rstd, xh * w


def _rms_bwd(dy, xh, rstd, w):
    dxh = dy * w
    return rstd * (dxh - xh * jnp.mean(dxh * xh, axis=-1, keepdims=True))


MLA_T = 256


def _mla_prep_fwd(mla_p, cos, sin, qnw, kvnw, w_uq, w_ukv):
    B, S, _ = mla_p.shape
    tm = min(S, 512)

    t = min(MLA_T, S)
    nt = tm // t

    def body(p_ref, c_ref, s_ref, qn_ref, kn_ref, wq_ref, wkv_ref, q_ref, k_ref, v_ref, kt_ref, vt_ref):
        p = p_ref[0]
        cs, sn = c_ref[0], s_ref[0]
        _, _, qn = _rms(p[:, 0:256], qn_ref[...])
        qpre = _mm(qn, wq_ref[...])
        _, _, kvn = _rms(p[:, 256:384], kn_ref[...])
        kv = _mm(kvn, wkv_ref[...])
        kpe = _rope128(p[:, 384:512], cs, sn)
        for h in range(8):
            sl = slice(128 * h, 128 * h + 128)
            q_ref[0, :, sl] = _rope128(qpre[:, sl], cs, sn).astype(_MXU)
            kh = kv[:, sl] + kpe
            k_ref[0, :, sl] = kh.astype(_MXU)
            kht = kh.T
            for n in range(nt):
                kt_ref[0, n, sl, :] = kht[:, n * t:(n + 1) * t].astype(_MXU)
        v_ref[0] = kv[:, 1024:1536].astype(_MXU)
        for pr in range(4):
            vht = kv[:, 1024 + 128 * pr:1152 + 128 * pr].T
            for n in range(nt):
                vt_ref[0, n, 128 * pr:128 * pr + 128, :] = vht[:, n * t:(n + 1) * t].astype(_MXU)

    tok = lambda w: pl.BlockSpec((1, tm, w), lambda b, i: (b, i, 0))
    tr = lambda w: pl.BlockSpec((1, nt, w, t), lambda b, i: (b, i, 0, 0))
    return pl.pallas_call(
        body, name="mla_prep_fwd", grid=(B, S // tm),
        in_specs=[tok(512), tok(128), tok(128), _full((1, 256)), _full((1, 128)), _full((256, 1024)),
                  _full((128, 1536))],
        out_specs=[tok(1024), tok(1024), tok(512), tr(1024), tr(512)],
        out_shape=[jax.ShapeDtypeStruct((B, S, 1024), _MXU), jax.ShapeDtypeStruct((B, S, 1024), _MXU),
                   jax.ShapeDtypeStruct((B, S, 512), _MXU), jax.ShapeDtypeStruct((B, S // t, 1024, t), _MXU),
                   jax.ShapeDtypeStruct((B, S // t, 512, t), _MXU)],
        compiler_params=_cp(("parallel", "parallel")),
    )(mla_p, cos, sin, qnw, kvnw, w_uq, w_ukv)


def _chunk_mask_t(t):
    kj = lax.broadcasted_iota(jnp.int32, (t, t), 0) // CHUNK
    qi = lax.broadcasted_iota(jnp.int32, (t, t), 1) // CHUNK
    return kj <= qi


MLA_HG = 4
LOG2E = 1.4426950408889634
MLA_C2 = MLA_SCALE * LOG2E


def _mla_attn_fwd(q, k, vt):
    B, S, _ = q.shape
    t = min(MLA_T, S)
    nq = S // t
    HG = MLA_HG
    NP = HG // 2

    def body(q_ref, k_ref, vt_ref, o_ref, lse_ref, sa, sb, m_sc, l_sc, acc_sc):
        i = pl.program_id(2)
        row = lax.broadcasted_iota(jnp.int32, (128, 1), 0)
        low = row < 64
        mask = _chunk_mask_t(t)
        m_sc[...] = jnp.full(m_sc.shape, -jnp.inf, F32)
        l_sc[...] = jnp.zeros_like(l_sc)
        acc_sc[...] = jnp.zeros_like(acc_sc)

        def scores(j, buf):
            kb = k_ref[0, pl.ds(pl.multiple_of(j * t, t), t), :]
            for h in range(HG):
                cols = slice(128 * h, 128 * h + 128)
                buf[h] = _mm_nt(kb[:, cols], q_ref[0, :, cols]) * MLA_C2

        def absorb(j, buf, masked):
            vtb = vt_ref[0, j]
            for pr in range(NP):
                alphas, pvs = [], []
                for hh in range(2):
                    h = 2 * pr + hh
                    s = buf[h]
                    if masked:
                        s = jnp.where(mask, s, -jnp.inf)
                    m_old = m_sc[h]
                    m_new = jnp.maximum(m_old, jnp.max(s, axis=0, keepdims=True))
                    alpha = jnp.exp2(m_old - m_new)
                    p = jnp.exp2(s - m_new)
                    l_sc[h] = alpha * l_sc[h] + jnp.sum(p, axis=0, keepdims=True)
                    m_sc[h] = m_new
                    vth = vtb[128 * pr:128 * pr + 128, :]
                    vth = jnp.where(low if hh == 0 else ~low, vth, jnp.zeros_like(vth))
                    pvs.append(_mm(vth, p))
                    alphas.append(alpha)
                acc_sc[pr] = acc_sc[pr] * jnp.where(low, alphas[0], alphas[1]) + pvs[0] + pvs[1]

        scores(i, sa)

        @pl.when(i > 0)
        def _():
            scores(0, sb)

        absorb(i, sa, True)

        def pair(jj, carry):
            j0 = 2 * jj
            scores(jnp.minimum(j0 + 1, i - 1), sa)
            absorb(j0, sb, False)
            scores(jnp.minimum(j0 + 2, i - 1), sb)
            absorb(j0 + 1, sa, False)
            return carry

        lax.fori_loop(0, i // 2, pair, 0)

        @pl.when(i % 2 == 1)
        def _():
            absorb(i - 1, sb, False)

        for pr in range(NP):
            l_e, l_o = l_sc[2 * pr], l_sc[2 * pr + 1]
            o_ref[0, :, 128 * pr:128 * pr + 128] = (acc_sc[pr] / jnp.where(low, l_e, l_o)).T
            lse_ref[0, pr, 0, 0:1, :] = m_sc[2 * pr] + jnp.log(l_e) * LOG2E
            lse_ref[0, pr, 0, 1:2, :] = m_sc[2 * pr + 1] + jnp.log(l_o) * LOG2E

    return pl.pallas_call(
        body, name="mla_attn_fwd", grid=(B, 8 // HG, nq),
        in_specs=[pl.BlockSpec((1, t, 128 * HG), lambda b, g, i: (b, i, g)),
                  pl.BlockSpec((1, S, 128 * HG), lambda b, g, i: (b, 0, g)),
                  pl.BlockSpec((1, nq, 64 * HG, t), lambda b, g, i: (b, 0, g, 0))],
        out_specs=[pl.BlockSpec((1, t, 64 * HG), lambda b, g, i: (b, i, g)),
                   pl.BlockSpec((1, NP, 1, 2, t), lambda b, g, i: (b, g, i, 0, 0))],
        out_shape=[jax.ShapeDtypeStruct((B, S, 512), F32), jax.ShapeDtypeStruct((B, 4, nq, 2, t), F32)],
        scratch_shapes=[pltpu.VMEM((HG, t, t), F32), pltpu.VMEM((HG, t, t), F32), pltpu.VMEM((HG, 1, t), F32),
                        pltpu.VMEM((HG, 1, t), F32), pltpu.VMEM((NP, 128, t), F32)],
        compiler_params=_cp(("parallel", "parallel", "arbitrary")),
    )(q, k, vt)


def _mla_gate_bwd(d_mix, o, mla_p):
    B, S, _ = o.shape
    tm = min(S, 512)
    t = min(MLA_T, S)
    nt = tm // t

    def body(dm_ref, o_ref, z_ref, do_ref, dz_ref, dl_ref):
        dm, ov, z = dm_ref[0], o_ref[0], z_ref[0]
        do = dm * _silu(z)
        dz_ref[0] = dm * ov * _dsilu(z)
        do_ref[0] = do.astype(_MXU)
        prod = do * ov
        for pr in range(4):
            pt = prod[:, 128 * pr:128 * pr + 128].T
            se = jnp.sum(pt[0:64], axis=0, keepdims=True)
            so = jnp.sum(pt[64:128], axis=0, keepdims=True)
            for n in range(nt):
                dl_ref[0, pr, n, 0:1, :] = se[:, n * t:(n + 1) * t]
                dl_ref[0, pr, n, 1:2, :] = so[:, n * t:(n + 1) * t]

    tok = lambda c: pl.BlockSpec((1, tm, 512), lambda b, i: (b, i, c))
    return pl.pallas_call(
        body, name="mla_gate_bwd", grid=(B, S // tm),
        in_specs=[tok(0), tok(0), tok(1)],
        out_specs=[tok(0), tok(0), pl.BlockSpec((1, 4, nt, 2, t), lambda b, i: (b, 0, i, 0, 0))],
        out_shape=[jax.ShapeDtypeStruct((B, S, 512), _MXU), jax.ShapeDtypeStruct((B, S, 512), F32),
                   jax.ShapeDtypeStruct((B, 4, S // t, 2, t), F32)],
        compiler_params=_cp(("parallel", "parallel")),
    )(d_mix, o, mla_p)


def _mla_attn_bwd(q, k, v, kt, do, lse, dl):
    B, S, _ = q.shape
    t = min(MLA_T, S)
    nk = S // t

    HG = MLA_HG
    NP = HG // 2

    def body(q_ref, k_ref, v_ref, kt_ref, do_ref, lse_ref, dl_ref, dq_ref, dk_ref, dv_ref,
             sa, da, sb, db, dqt_sc, dk_sc, dv_sc):
        j = pl.program_id(2)

        @pl.when(j == 0)
        def _():
            dqt_sc[...] = jnp.zeros_like(dqt_sc)

        dk_sc[...] = jnp.zeros_like(dk_sc)
        dv_sc[...] = jnp.zeros_like(dv_sc)
        lane = lax.broadcasted_iota(jnp.int32, (1, 128), 1)
        low = lane < 64
        mask = _chunk_mask_t(t)

        def half(x, hh):
            return jnp.where(low if hh == 0 else ~low, x, jnp.zeros_like(x))

        def prepare(i, sbuf, dbuf):
            rows = pl.ds(pl.multiple_of(i * t, t), t)
            for h in range(HG):
                cols = slice(128 * h, 128 * h + 128)
                pc = slice(128 * (h // 2), 128 * (h // 2) + 128)
                sbuf[h] = _mm_nt(k_ref[0, :, cols], q_ref[0, rows, cols]) * MLA_C2
                dbuf[h] = _mm_nt(half(v_ref[0, :, pc], h % 2), do_ref[0, rows, pc])

        def absorb(i, sbuf, dbuf, masked):
            rows = pl.ds(pl.multiple_of(i * t, t), t)
            for h in range(HG):
                pr, hh = h // 2, h % 2
                cols = slice(128 * h, 128 * h + 128)
                pc = slice(128 * pr, 128 * pr + 128)
                p = jnp.exp2(sbuf[h] - lse_ref[0, pr, i][hh:hh + 1, :])
                if masked:
                    p = jnp.where(mask, p, 0.0)
                dv_sc[pr] += _mm(p, half(do_ref[0, rows, pc], hh))
                ds = p * (dbuf[h] - dl_ref[0, pr, i][hh:hh + 1, :])
                dqt_sc[i, cols, :] += _mm(kt_ref[0, 0, cols, :], ds)
                dk_sc[h] += _mm(ds, q_ref[0, rows, cols])

        n = nk - 1 - j
        prepare(j, sa, da)

        @pl.when(n > 0)
        def _():
            prepare(j + 1, sb, db)

        absorb(j, sa, da, True)

        def pair(jj, carry):
            i0 = j + 1 + 2 * jj
            prepare(jnp.minimum(i0 + 1, nk - 1), sa, da)
            absorb(i0, sb, db, False)
            prepare(jnp.minimum(i0 + 2, nk - 1), sb, db)
            absorb(i0 + 1, sa, da, False)
            return carry

        lax.fori_loop(0, n // 2, pair, 0)

        @pl.when(n % 2 == 1)
        def _():
            absorb(nk - 1, sb, db, False)

        for h in range(HG):
            dk_ref[0, :, 128 * h:128 * h + 128] = dk_sc[h] * MLA_SCALE
        for pr in range(NP):
            dv_ref[0, :, 128 * pr:128 * pr + 128] = dv_sc[pr]

        @pl.when(j == nk - 1)
        def _():
            for i in range(nk):
                dq_ref[0, i * t:(i + 1) * t, :] = dqt_sc[i].T * MLA_SCALE

    seq = lambda w: pl.BlockSpec((1, S, w), lambda b, g, j: (b, 0, g))
    blk = lambda w: pl.BlockSpec((1, t, w), lambda b, g, j: (b, j, g))
    stat = pl.BlockSpec((1, NP, nk, 2, t), lambda b, g, j: (b, g, 0, 0, 0))
    return pl.pallas_call(
        body, name="mla_attn_bwd", grid=(B, 8 // HG, nk),
        in_specs=[seq(128 * HG), blk(128 * HG), blk(64 * HG),
                  pl.BlockSpec((1, 1, 128 * HG, t), lambda b, g, j: (b, j, g, 0)), seq(64 * HG), stat, stat],
        out_specs=[seq(128 * HG), blk(128 * HG), blk(64 * HG)],
        out_shape=[jax.ShapeDtypeStruct((B, S, 1024), F32), jax.ShapeDtypeStruct((B, S, 1024), F32),
                   jax.ShapeDtypeStruct((B, S, 512), F32)],
        scratch_shapes=[pltpu.VMEM((HG, t, t), F32), pltpu.VMEM((HG, t, t), F32), pltpu.VMEM((HG, t, t), F32),
                        pltpu.VMEM((HG, t, t), F32), pltpu.VMEM((nk, 128 * HG, t), F32),
                        pltpu.VMEM((HG, t, 128), F32), pltpu.VMEM((NP, t, 128), F32)],
        compiler_params=_cp(("parallel", "parallel", "arbitrary")),
    )(q, k, v, kt, do, lse, dl)


def _mla_prep_bwd(mla_p, cos, sin, qnw, kvnw, w_uq, w_ukv, dq, dk, dv):
    B, S, _ = mla_p.shape
    tm = min(S, 512)

    def body(p_ref, c_ref, s_ref, qn_ref, kn_ref, wq_ref, wkv_ref, dq_ref, dk_ref, dv_ref,
             dp_ref, dwq_ref, dwkv_ref, dqn_ref, dkn_ref):
        first = (pl.program_id(0) == 0) & (pl.program_id(1) == 0)

        @pl.when(first)
        def _():
            dwq_ref[...] = jnp.zeros_like(dwq_ref)
            dwkv_ref[...] = jnp.zeros_like(dwkv_ref)
            dqn_ref[...] = jnp.zeros_like(dqn_ref)
            dkn_ref[...] = jnp.zeros_like(dkn_ref)

        p = p_ref[0]
        cs, sn = c_ref[0], s_ref[0]
        lane = lax.broadcasted_iota(jnp.int32, (1, 128), 1)
        pe = (lane >= 64) & (lane < 96)
        qh, q_rstd, qn = _rms(p[:, 0:256], qn_ref[...])
        kvh, kv_rstd, kvn = _rms(p[:, 256:384], kn_ref[...])
        dqv = dq_ref[0]
        dkv = dk_ref[0]
        dqpre = jnp.concatenate(
            [_rope128_t(dqv[:, 128 * h:128 * h + 128], cs, sn) for h in range(8)], axis=1)
        dkpe = jnp.zeros((tm, 128), F32)
        for h in range(8):
            dkpe = dkpe + jnp.where(pe, dkv[:, 128 * h:128 * h + 128], 0.0)
        dkr = _rope128_t(dkpe, cs, sn)
        dkv_all = jnp.concatenate([dkv, dv_ref[0]], axis=1)
        d_qn = _mm_nt(dqpre, wq_ref[...])
        d_kvn = _mm_nt(dkv_all, wkv_ref[...])
        dwq_ref[...] += _mm_tn(qn, dqpre)
        dwkv_ref[...] += _mm_tn(kvn, dkv_all)
        dqn_ref[...] += jnp.sum(d_qn * qh, axis=0, keepdims=True)
        dkn_ref[...] += jnp.sum(d_kvn * kvh, axis=0, keepdims=True)
        dp_ref[0] = jnp.concatenate([_rms_bwd(d_qn, qh, q_rstd, qn_ref[...]),
                                     _rms_bwd(d_kvn, kvh, kv_rstd, kn_ref[...]), dkr], axis=1)

    tok = lambda w: pl.BlockSpec((1, tm, w), lambda b, i: (b, i, 0))
    return pl.pallas_call(
        body, name="mla_prep_bwd", grid=(B, S // tm),
        in_specs=[tok(512), tok(128), tok(128), _full((1, 256)), _full((1, 128)), _full((256, 1024)),
                  _full((128, 1536)), tok(1024), tok(1024), tok(512)],
        out_specs=[tok(512), _full((256, 1024)), _full((128, 1536)), _full((1, 256)), _full((1, 128))],
        out_shape=[jax.ShapeDtypeStruct((B, S, 512), F32), jax.ShapeDtypeStruct((256, 1024), F32),
                   jax.ShapeDtypeStruct((128, 1536), F32), jax.ShapeDtypeStruct((1, 256), F32),
                   jax.ShapeDtypeStruct((1, 128), F32)],
        compiler_params=_cp(("arbitrary", "arbitrary")),
    )(mla_p, cos, sin, qnw, kvnw, w_uq, w_ukv, dq, dk, dv)


def _out_fwd(x, gate, r_g, o_mla, mla_p, g_g, w_out):
    B, S, D = x.shape
    tm = min(S, 512)

    def body(x_ref, g_ref, r_ref, o_ref, z_ref, gg_ref, w_ref, xn_ref, y_ref, mm_ref):
        mm = (o_ref[0] * _silu(z_ref[0])).astype(_MXU)
        mm_ref[0] = mm
        y = (jnp.dot(r_ref[0], w_ref[0:256, :], preferred_element_type=F32)
             + jnp.dot(mm, w_ref[256:768, :], preferred_element_type=F32)
             + jnp.dot(gg_ref[0], w_ref[768:1024, :], preferred_element_type=F32))
        y_ref[0] = y
        xn_ref[0] = x_ref[0] + g_ref[0] * y

    tok = lambda w, c=0: pl.BlockSpec((1, tm, w), lambda b, i: (b, i, c))
    return pl.pallas_call(
        body, name="out_fwd", grid=(B, S // tm),
        in_specs=[tok(D), pl.BlockSpec((1, 1, D), lambda b, i: (b, 0, 0)), tok(256), tok(512), tok(512, 1),
                  tok(256), _full((D, D))],
        out_specs=[tok(D), tok(D), tok(512)],
        out_shape=[jax.ShapeDtypeStruct((B, S, D), F32), jax.ShapeDtypeStruct((B, S, D), F32),
                   jax.ShapeDtypeStruct((B, S, 512), _MXU)],
        compiler_params=_cp(("parallel", "parallel")),
    )(x, gate, r_g, o_mla, mla_p, g_g, w_out)


def _out_bwd(dx, y, gate, r_g, mm, g_g, w_out):
    B, S, D = dx.shape
    tm = min(S, 512)

    def body(dx_ref, y_ref, g_ref, r_ref, mm_ref, gg_ref, w_ref, dr_ref, dmm_ref, dg_ref, dw_ref, dgate_ref):
        first = (pl.program_id(0) == 0) & (pl.program_id(1) == 0)

        @pl.when(first)
        def _():
            dw_ref[...] = jnp.zeros_like(dw_ref)

        @pl.when(pl.program_id(1) == 0)
        def _():
            dgate_ref[...] = jnp.zeros_like(dgate_ref)

        dxv = dx_ref[0]
        dgate_ref[0] += jnp.sum(dxv * y_ref[0], axis=0, keepdims=True)
        dy = (dxv * g_ref[0]).astype(_MXU)
        dr_ref[0] = _mm_nt(dy, w_ref[0:256, :])
        dmm_ref[0] = _mm_nt(dy, w_ref[256:768, :])
        dg_ref[0] = _mm_nt(dy, w_ref[768:1024, :])
        dw_ref[0:256, :] += _mm_tn(r_ref[0], dy)
        dw_ref[256:768, :] += _mm_tn(mm_ref[0], dy)
        dw_ref[768:1024, :] += _mm_tn(gg_ref[0], dy)

    tok = lambda w: pl.BlockSpec((1, tm, w), lambda b, i: (b, i, 0))
    per_seq = pl.BlockSpec((1, 1, D), lambda b, i: (b, 0, 0))
    return pl.pallas_call(
        body, name="out_bwd", grid=(B, S // tm),
        in_specs=[tok(D), tok(D), per_seq, tok(256), tok(512), tok(256), _full((D, D))],
        out_specs=[tok(256), tok(512), tok(256), _full((D, D)), per_seq],
        out_shape=[jax.ShapeDtypeStruct((B, S, 256), F32), jax.ShapeDtypeStruct((B, S, 512), F32),
                   jax.ShapeDtypeStruct((B, S, 256), F32), jax.ShapeDtypeStruct((D, D), F32),
                   jax.ShapeDtypeStruct((B, 1, D), F32)],
        compiler_params=_cp(("arbitrary", "arbitrary")),
    )(dx, y, gate, r_g, mm, g_g, w_out)


def _proj_bwd_x(x, shift, scale, nw, w_arr, d_ret, d_mla, d_mz, d_gla, dx_out):
    B, S, D = x.shape
    tm = min(S, 256)

    def body(x_ref, sc_ref, nw_ref, w_ref, dr_ref, dm_ref, dz_ref, dg_ref, dxo_ref,
             dx_ref, dp_ref, dsh_ref, dsc_ref, dnw_ref):
        first = (pl.program_id(0) == 0) & (pl.program_id(1) == 0)

        @pl.when(first)
        def _():
            dnw_ref[...] = jnp.zeros_like(dnw_ref)

        @pl.when(pl.program_id(1) == 0)
        def _():
            dsh_ref[...] = jnp.zeros_like(dsh_ref)
            dsc_ref[...] = jnp.zeros_like(dsc_ref)

        dp = jnp.concatenate([dr_ref[0], dm_ref[0], dz_ref[0], dg_ref[0]], axis=1).astype(_MXU)
        dp_ref[0] = dp
        dh = lax.dot_general(dp, w_ref[...], (((1,), (1,)), ((), ())), preferred_element_type=F32)
        xv = x_ref[0]
        rstd = lax.rsqrt(jnp.mean(xv * xv, axis=-1, keepdims=True) + EPS)
        xh = xv * rstd
        nwv = nw_ref[...]
        mod = 1.0 + sc_ref[0]
        dsh_ref[0] += jnp.sum(dh, axis=0, keepdims=True)
        dsc_ref[0] += jnp.sum(dh * xh * nwv, axis=0, keepdims=True)
        dnw_ref[...] += jnp.sum(dh * xh * mod, axis=0, keepdims=True)
        dxh = dh * nwv * mod
        dx_ref[0] = dxo_ref[0] + rstd * (dxh - xh * jnp.mean(dxh * xh, axis=-1, keepdims=True))

    tok = lambda w: pl.BlockSpec((1, tm, w), lambda b, i: (b, i, 0))
    per_seq = pl.BlockSpec((1, 1, D), lambda b, i: (b, 0, 0))
    return pl.pallas_call(
        body, name="proj_bwd_x", grid=(B, S // tm),
        in_specs=[tok(D), per_seq, _full((1, D)), _full((D, ARR_W)), tok(RET_W), tok(512), tok(512),
                  tok(GLA_W), tok(D)],
        out_specs=[tok(D), tok(ARR_W), per_seq, per_seq, _full((1, D))],
        out_shape=[jax.ShapeDtypeStruct((B, S, D), F32), jax.ShapeDtypeStruct((B, S, ARR_W), _MXU),
                   jax.ShapeDtypeStruct((B, 1, D), F32), jax.ShapeDtypeStruct((B, 1, D), F32),
                   jax.ShapeDtypeStruct((1, D), F32)],
        compiler_params=_cp(("arbitrary", "arbitrary")),
    )(x, scale, nw, w_arr, d_ret, d_mla, d_mz, d_gla, dx_out)


def _proj_bwd_w(h, dp):
    B, S, D = h.shape
    tm = min(S, 512)
    tn = 128 * 23 // 1
    assert ARR_W == tn

    def body(h_ref, dp_ref, dw_ref):
        first = (pl.program_id(0) == 0) & (pl.program_id(1) == 0)

        @pl.when(first)
        def _():
            dw_ref[...] = jnp.zeros_like(dw_ref)

        dw_ref[...] += lax.dot_general(h_ref[0], dp_ref[0], (((0,), (0,)), ((), ())),
                                       preferred_element_type=F32)

    tok = lambda w: pl.BlockSpec((1, tm, w), lambda b, i: (b, i, 0))
    return pl.pallas_call(
        body, name="proj_bwd_w", grid=(B, S // tm),
        in_specs=[tok(D), tok(ARR_W)],
        out_specs=_full((D, ARR_W)), out_shape=jax.ShapeDtypeStruct((D, ARR_W), F32),
        compiler_params=_cp(("arbitrary", "arbitrary"), 56),
    )(h, dp)


def _final_loss(x, fw, target):
    B, S, D = x.shape
    tm = min(S, 512)

    def body(x_ref, fw_ref, t_ref, dx_ref, loss_ref, dfw_ref):
        first = (pl.program_id(0) == 0) & (pl.program_id(1) == 0)

        @pl.when(first)
        def _():
            loss_ref[...] = jnp.zeros_like(loss_ref)
            dfw_ref[...] = jnp.zeros_like(dfw_ref)

        xv = x_ref[0]
        fwv = fw_ref[...]
        rstd = lax.rsqrt(jnp.mean(xv * xv, axis=-1, keepdims=True) + EPS)
        xh = xv * rstd
        err = xh * fwv - t_ref[0]
        loss_ref[...] += 0.5 * jnp.sum(jnp.mean(err * err, axis=-1, keepdims=True), axis=0, keepdims=True)
        dy = err * (1.0 / D)
        dfw_ref[...] += jnp.sum(dy * xh, axis=0, keepdims=True)
        dxh = dy * fwv
        dx_ref[0] = rstd * (dxh - xh * jnp.mean(dxh * xh, axis=-1, keepdims=True))

    tok = pl.BlockSpec((1, tm, D), lambda b, i: (b, i, 0))
    return pl.pallas_call(
        body, name="final_loss", grid=(B, S // tm),
        in_specs=[tok, _full((1, D)), tok],
        out_specs=[tok, _full((1, 1)), _full((1, D))],
        out_shape=[jax.ShapeDtypeStruct((B, S, D), F32), jax.ShapeDtypeStruct((1, 1), F32),
                   jax.ShapeDtypeStruct((1, D), F32)],
        compiler_params=_cp(("arbitrary", "arbitrary")),
    )(x, fw, target)


def _local_step(x, pos3, mod, loss_target, small, w_in_a, w_uq_a, w_ukv_a, w_out_b):
    B, S, D = x.shape
    tabs = _rope_tables(pos3)
    saved = []
    for l in range(DEPTH):
        x, s = _layer_fwd(x, tabs, mod[l], {n: a[l] for n, a in small.items() if n != "final_norm"},
                          w_in_a[l], w_uq_a[l], w_ukv_a[l], w_out_b[l])
        saved.append(s)
    dx, loss, d_fw = _final_loss(x, small["final_norm"].reshape(1, D), loss_target)
    grads = dict(final_norm=d_fw.reshape(D))
    per_layer = [None] * DEPTH
    for l in reversed(range(DEPTH)):
        dx, per_layer[l] = _layer_bwd(dx, saved[l], tabs)
    for name in per_layer[0]:
        grads[name] = jnp.stack([per_layer[l][name] for l in range(DEPTH)])
    return loss, dx, grads


def _layer_fwd(x, tabs, mod_l, small_l, w_in_a, w_uq_a, w_ukv_a, w_out_b):
    B, S, D = x.shape
    cr, sr, cm, sm = tabs
    shift = mod_l[:, 0:D].reshape(B, 1, D)
    scale = mod_l[:, D:2 * D].reshape(B, 1, D)
    gate = mod_l[:, 2 * D:3 * D].reshape(B, 1, D)
    nw = small_l["norm_w"].reshape(1, D)
    qnw = small_l["mla_q_norm"].reshape(1, 256)
    kvnw = small_l["mla_kv_norm"].reshape(1, 128)
    w_g2p = jnp.pad(small_l["gla_w_g2"], ((0, 112), (0, 0)))
    b_g2 = small_l["gla_b_g2"].reshape(1, 128)
    gnw = jnp.tile(small_l["gla_norm"], 4).reshape(1, 256)
    ret_p, mla_p, gla_p, h = _proj_fwd(x, shift, scale, nw, w_in_a)
    r_g, r_raw, r_st = _ret_fwd(ret_p, cr, sr)
    q, k, v, kt, vt = _mla_prep_fwd(mla_p, cm, sm, qnw, kvnw, w_uq_a, w_ukv_a)
    o_mla, lse = _mla_attn_fwd(q, k, vt)
    g_g, g_raw, g_st = _gla_fwd(gla_p, w_g2p, b_g2, gnw)
    x_new, y, mm = _out_fwd(x, gate, r_g, o_mla, mla_p, g_g, w_out_b)
    saved = dict(x=x, shift=shift, scale=scale, gate=gate, nw=nw, qnw=qnw, kvnw=kvnw, w_g2p=w_g2p, b_g2=b_g2,
                 gnw=gnw, ret_p=ret_p, mla_p=mla_p, gla_p=gla_p, h=h, r_g=r_g, r_raw=r_raw, r_st=r_st, q=q, k=k,
                 v=v, kt=kt, o_mla=o_mla, lse=lse, g_g=g_g, g_raw=g_raw, g_st=g_st, y=y, mm=mm,
                 w_in_a=w_in_a, w_uq_a=w_uq_a, w_ukv_a=w_ukv_a, w_out_b=w_out_b)
    return x_new, saved


def _layer_bwd(dx, s, tabs):
    B, S, D = dx.shape
    cr, sr, cm, sm = tabs
    d_r, d_mm, d_g, dw_out, d_gate = _out_bwd(dx, s["y"], s["gate"], s["r_g"], s["mm"], s["g_g"], s["w_out_b"])
    d_ret = _ret_bwd(s["ret_p"], cr, sr, s["r_raw"], s["r_st"], d_r)
    do, d_mz, dl = _mla_gate_bwd(d_mm, s["o_mla"], s["mla_p"])
    dq, dk, dv = _mla_attn_bwd(s["q"], s["k"], s["v"], s["kt"], do, s["lse"], dl)
    d_mla, dw_uq, dw_ukv, d_qnw, d_kvnw = _mla_prep_bwd(
        s["mla_p"], cm, sm, s["qnw"], s["kvnw"], s["w_uq_a"], s["w_ukv_a"], dq, dk, dv)
    d_gla, dw_g2p, db_g2, d_gnw = _gla_bwd(s["gla_p"], s["w_g2p"], s["b_g2"], s["gnw"], s["g_raw"], s["g_st"], d_g)
    dx, dp, d_shift, d_scale, d_nw = _proj_bwd_x(s["x"], s["shift"], s["scale"], s["nw"], s["w_in_a"],
                                                 d_ret, d_mla, d_mz, d_gla, dx)
    dw_in = _proj_bwd_w(s["h"], dp)
    grads = dict(
        d_mod=jnp.concatenate([d_shift, d_scale, d_gate], axis=2).reshape(B, 3 * D),
        norm_w=d_nw.reshape(D), mla_q_norm=d_qnw.reshape(256), mla_kv_norm=d_kvnw.reshape(128),
        gla_w_g2=dw_g2p[0:16], gla_b_g2=db_g2.reshape(128), gla_norm256=d_gnw.reshape(256),
        w_in_a=dw_in, w_uq_a=dw_uq, w_ukv_a=dw_ukv, w_out=dw_out)
    return dx, grads


def _exchange(arrs, gather, name):
    n = len(arrs)
    out_shape = [jax.ShapeDtypeStruct(((N_DEV,) + a.shape) if g else a.shape, a.dtype)
                 for a, g in zip(arrs, gather)]

    def body(*refs):
        ins, outs = refs[:n], refs[n:2 * n]
        send_sems, recv_sems, local_sems = refs[2 * n:]
        ix, iy, ic = lax.axis_index("x"), lax.axis_index("y"), lax.axis_index("c")
        me = 4 * ix + 2 * iy + ic
        copies = []
        for a in range(n):
            mine = ins[a] if gather[a] else ins[a].at[me]
            loc = pltpu.make_async_copy(mine, outs[a].at[me], local_sems.at[a])
            loc.start()
            copies.append(loc)
            for d in range(1, N_DEV):
                px = 1 - ix if d & 4 else ix
                py = 1 - iy if d & 2 else iy
                pc = 1 - ic if d & 1 else ic
                src = ins[a] if gather[a] else ins[a].at[4 * px + 2 * py + pc]
                cp = pltpu.make_async_remote_copy(
                    src_ref=src, dst_ref=outs[a].at[me], send_sem=send_sems.at[a, d - 1],
                    recv_sem=recv_sems.at[a, d - 1], device_id=(px, py, pc), device_id_type=pl.DeviceIdType.MESH)
                cp.start()
                copies.append(cp)
        for cp in copies:
            cp.wait()

    any_spec = pl.BlockSpec(memory_space=pl.ANY)
    outs = pl.pallas_call(
        body, name=name, in_specs=[any_spec] * n, out_specs=[any_spec] * n, out_shape=out_shape,
        scratch_shapes=[pltpu.SemaphoreType.DMA((n, N_DEV - 1)), pltpu.SemaphoreType.DMA((n, N_DEV - 1)),
                        pltpu.SemaphoreType.DMA((n,))],
    )(*arrs)
    return list(outs)


def _peers(ix, iy, ic):
    out = []
    for d in range(1, N_DEV):
        px = 1 - ix if d & 4 else ix
        py = 1 - iy if d & 2 else iy
        pc = 1 - ic if d & 1 else ic
        out.append((d - 1, (px, py, pc), 4 * px + 2 * py + pc))
    return out


def _exchange_start(arrs, gather, name):
    n = len(arrs)
    lands = [lax.empty(((N_DEV,) + a.shape) if g else a.shape, a.dtype) for a, g in zip(arrs, gather)]

    def body(*refs):
        ins, land_refs = refs[:n], refs[n:2 * n]
        send_sems, recv_sems = refs[2 * n], refs[2 * n + 1]
        token = refs[-1]
        ix, iy, ic = lax.axis_index("x"), lax.axis_index("y"), lax.axis_index("c")
        me = 4 * ix + 2 * iy + ic
        for a in range(n):
            for k, peer, peer_idx in _peers(ix, iy, ic):
                pltpu.make_async_remote_copy(
                    src_ref=ins[a] if gather[a] else ins[a].at[peer_idx], dst_ref=land_refs[a].at[me],
                    send_sem=send_sems.at[7 * a + k], recv_sem=recv_sems.at[7 * a + k], device_id=peer,
                    device_id_type=pl.DeviceIdType.MESH).start()
        token[...] = jnp.zeros_like(token)

    hbm = pl.BlockSpec(memory_space=pltpu.HBM)
    sem = pl.BlockSpec(memory_space=pltpu.SEMAPHORE)
    held = [pltpu.with_memory_space_constraint(a, pltpu.HBM) for a in list(arrs) + lands]
    outs = pl.pallas_call(
        body, name=name,
        out_shape=(pltpu.SemaphoreType.DMA((7 * n,)), pltpu.SemaphoreType.DMA((7 * n,)),
                   *[pltpu.HBM(a.shape, a.dtype) for a in held], jax.ShapeDtypeStruct((8, 128), F32)),
        in_specs=[hbm] * (2 * n), out_specs=(sem, sem, *[hbm] * (2 * n), pl.BlockSpec(memory_space=pltpu.VMEM)),
        input_output_aliases={a: 2 + a for a in range(2 * n)},
        compiler_params=pltpu.CompilerParams(has_side_effects=pltpu.SideEffectType.DATAFLOW_SIDE_EFFECTING),
    )(*held)
    return dict(send=outs[0], recv=outs[1], srcs=list(outs[2:2 + n]), lands=list(outs[2 + n:2 + 2 * n]),
                token=outs[-1], gather=list(gather))


def _exchange_wait(flight, after, me, name):
    n = len(flight["srcs"])
    gather = flight["gather"]

    def body(*refs):
        srcs, land_refs = refs[:n], refs[n:2 * n]
        send_sems, recv_sems = refs[2 * n], refs[2 * n + 1]
        ix, iy, ic = lax.axis_index("x"), lax.axis_index("y"), lax.axis_index("c")
        mine = 4 * ix + 2 * iy + ic
        for a in range(n):
            for k, peer, peer_idx in _peers(ix, iy, ic):
                cp = pltpu.make_async_remote_copy(
                    src_ref=srcs[a] if gather[a] else srcs[a].at[peer_idx], dst_ref=land_refs[a].at[mine],
                    send_sem=send_sems.at[7 * a + k], recv_sem=recv_sems.at[7 * a + k], device_id=peer,
                    device_id_type=pl.DeviceIdType.MESH)
                cp.wait_send()
                cp.wait_recv()

    hbm = pl.BlockSpec(memory_space=pltpu.HBM)
    sem = pl.BlockSpec(memory_space=pltpu.SEMAPHORE)
    held = flight["srcs"] + flight["lands"]
    outs = pl.pallas_call(
        body, name=name, out_shape=tuple(pltpu.HBM(a.shape, a.dtype) for a in held),
        in_specs=[hbm] * (2 * n) + [sem, sem, pl.BlockSpec(memory_space=pl.ANY)], out_specs=tuple([hbm] * (2 * n)),
        input_output_aliases={a: a for a in range(2 * n)},
        compiler_params=pltpu.CompilerParams(has_side_effects=pltpu.SideEffectType.DATAFLOW_SIDE_EFFECTING),
    )(*held, flight["send"], flight["recv"], after)
    got = []
    for a in range(n):
        src, land = outs[a], outs[n + a]
        own = src if gather[a] else lax.dynamic_index_in_dim(src, me, axis=0, keepdims=False)
        got.append(lax.dynamic_update_index_in_dim(land, own, me, axis=0))
    return got


def _ada_fwd(c_all, ada_w, ada_b_cols):
    nb, D = c_all.shape
    cols = ada_w.shape[2]

    def body(c_ref, w_ref, b_ref, out_ref):
        ca = _silu(c_ref[...])
        for l in range(DEPTH):
            out_ref[l] = _mm(ca, w_ref[l]) + b_ref[l:l + 1, :]

    return pl.pallas_call(
        body, name="ada_fwd", out_shape=jax.ShapeDtypeStruct((DEPTH, nb, cols), F32),
        in_specs=[pl.BlockSpec(memory_space=pltpu.VMEM)] * 3, out_specs=pl.BlockSpec(memory_space=pltpu.VMEM),
        compiler_params=pltpu.CompilerParams(vmem_limit_bytes=32 * VMEM_MB),
    )(c_all, ada_w, ada_b_cols)


def _ada_bwd(c_all, d_mod_cols):
    nb, D = c_all.shape
    cols = d_mod_cols.shape[2]

    def body(c_ref, dm_ref, out_ref):
        ca = _silu(c_ref[...])
        for l in range(DEPTH):
            out_ref[l] = _mm_tn(ca, dm_ref[l])

    return pl.pallas_call(
        body, name="ada_bwd", out_shape=jax.ShapeDtypeStruct((DEPTH, D, cols), F32),
        in_specs=[pl.BlockSpec(memory_space=pltpu.VMEM)] * 2, out_specs=pl.BlockSpec(memory_space=pltpu.VMEM),
        compiler_params=pltpu.CompilerParams(vmem_limit_bytes=32 * VMEM_MB),
    )(c_all, d_mod_cols)


def _sum_adamw(parts, w, m, v, name):
    P, R, C = parts.shape
    tr = 256 if (R % 256 == 0 and R > 256) else R

    def body(p_ref, w_ref, m_ref, v_ref, g_ref, d_ref, nm_ref, nv_ref):
        g = p_ref[0].astype(F32)
        for k in range(1, P):
            g = g + p_ref[k].astype(F32)
        g_ref[...] = g
        nm = ADAM_B1 * m_ref[...] + (1.0 - ADAM_B1) * g
        nv = ADAM_B2 * v_ref[...] + (1.0 - ADAM_B2) * (g * g)
        nm_ref[...] = nm
        nv_ref[...] = nv
        m_hat = nm / (1.0 - ADAM_B1 ** ADAM_STEP)
        v_hat = nv / (1.0 - ADAM_B2 ** ADAM_STEP)
        d_ref[...] = -ADAM_LR * (m_hat / (jnp.sqrt(v_hat) + ADAM_EPS) + ADAM_WD * w_ref[...])

    blk = pl.BlockSpec((tr, C), lambda i: (i, 0))
    shp = jax.ShapeDtypeStruct((R, C), F32)
    return pl.pallas_call(
        body, name=name, grid=(R // tr,),
        in_specs=[pl.BlockSpec((P, tr, C), lambda i: (0, i, 0)), blk, blk, blk],
        out_specs=[blk, blk, blk, blk], out_shape=[shp, shp, shp, shp],
        compiler_params=_cp(("parallel",)),
    )(parts, w, m, v)


SMALL = [("norm_w", DEPTH * 1024), ("mla_q_norm", DEPTH * 256), ("mla_kv_norm", DEPTH * 128),
         ("gla_w_g2", DEPTH * 16 * 128), ("gla_b_g2", DEPTH * 128), ("gla_norm", DEPTH * 64), ("final_norm", 1024)]
SMALL_ROWS = 72


def _pack_small(first_row, vals):
    flat = [first_row.reshape(128)] + [vals[n].reshape(-1) for n, _ in SMALL]
    used = 128 + sum(s for _, s in SMALL)
    flat.append(jnp.zeros((SMALL_ROWS * 128 - used,), F32))
    return jnp.concatenate(flat).reshape(SMALL_ROWS, 128)


def _unpack_small(packed, shapes):
    flat = packed.reshape(-1)
    out, off = {}, 128
    for n, s in SMALL:
        out[n] = flat[off:off + s].reshape(shapes[n])
        off += s
    return out


WEIGHTS = ["norm_w", "ada_w", "ada_b", "w_in", "mla_q_norm", "w_uq", "mla_kv_norm", "w_ukv", "gla_w_g2",
           "gla_b_g2", "gla_norm", "w_out", "final_norm"]


def kernel(x, c, positions, norm_w, ada_w, ada_b, w_in, mla_q_norm, w_uq, mla_kv_norm, w_ukv, gla_w_g2, gla_b_g2, gla_norm, w_out, final_norm, loss_target, m_norm_w, m_ada_w, m_ada_b, m_w_in, m_mla_q_norm, m_w_uq, m_mla_kv_norm, m_w_ukv, m_gla_w_g2, m_gla_b_g2, m_gla_norm, m_w_out, m_final_norm, v_norm_w, v_ada_w, v_ada_b, v_w_in, v_mla_q_norm, v_w_uq, v_mla_kv_norm, v_w_ukv, v_gla_w_g2, v_gla_b_g2, v_gla_norm, v_w_out, v_final_norm):
    w = dict(norm_w=norm_w, ada_w=ada_w, ada_b=ada_b, w_in=w_in, mla_q_norm=mla_q_norm, w_uq=w_uq,
             mla_kv_norm=mla_kv_norm, w_ukv=w_ukv, gla_w_g2=gla_w_g2, gla_b_g2=gla_b_g2, gla_norm=gla_norm,
             w_out=w_out, final_norm=final_norm)
    m = dict(norm_w=m_norm_w, ada_w=m_ada_w, ada_b=m_ada_b, w_in=m_w_in, mla_q_norm=m_mla_q_norm, w_uq=m_w_uq,
             mla_kv_norm=m_mla_kv_norm, w_ukv=m_w_ukv, gla_w_g2=m_gla_w_g2, gla_b_g2=m_gla_b_g2,
             gla_norm=m_gla_norm, w_out=m_w_out, final_norm=m_final_norm)
    v = dict(norm_w=v_norm_w, ada_w=v_ada_w, ada_b=v_ada_b, w_in=v_w_in, mla_q_norm=v_mla_q_norm, w_uq=v_w_uq,
             mla_kv_norm=v_mla_kv_norm, w_ukv=v_w_ukv, gla_w_g2=v_gla_w_g2, gla_b_g2=v_gla_b_g2,
             gla_norm=v_gla_norm, w_out=v_w_out, final_norm=v_final_norm)
    B, S, D = x.shape
    me = 4 * lax.axis_index("x") + 2 * lax.axis_index("y") + lax.axis_index("c")
    ada_cols = ada_w.shape[2]
    cast = lambda a: a.astype(_MXU)

    sharded = ["w_in", "w_uq", "w_ukv", "w_out"]

    def whole(blocks):
        cols = lambda a: jnp.transpose(a, (1, 0, 2)).reshape(a.shape[1], -1)
        return (_arrange_w_in(cols(blocks[0])), _arrange_w_uq(cols(blocks[1])), _arrange_w_ukv(cols(blocks[2])),
                blocks[3].reshape(D, D))

    def to_blocks(g_l):
        cols = lambda a: jnp.transpose(a.reshape(a.shape[0], N_DEV, -1), (1, 0, 2)).astype(jnp.bfloat16)
        return [cols(_unarrange_w_in(g_l["w_in_a"])), cols(_unarrange_w_uq(g_l["w_uq_a"])),
                cols(_unarrange_w_ukv(g_l["w_ukv_a"])), g_l["w_out"].reshape(N_DEV, D // N_DEV, D).astype(jnp.bfloat16)]

    got0 = _exchange([c] + [cast(w[n][0]) for n in sharded], [True] * 5, "gather_layer0")
    c_all = got0[0].reshape(N_DEV * B, D)
    flight_w = _exchange_start([cast(w[n][1]) for n in sharded], [True] * 4, "gather_start_layer1")

    ada_b_cols = lax.dynamic_slice(ada_b, (0, me * ada_cols), (DEPTH, ada_cols))
    mod_cols = _ada_fwd(c_all, ada_w, ada_b_cols)
    mod_send = jnp.transpose(mod_cols.reshape(DEPTH, N_DEV, B, ada_cols), (1, 0, 2, 3))
    (mod_recv,) = _exchange([mod_send], [False], "scatter_mod")
    mod = jnp.transpose(mod_recv, (1, 2, 0, 3)).reshape(DEPTH, B, 3 * D)

    small_w = {n: w[n] for n, _ in SMALL}
    layer_small = lambda l: {n: a[l] for n, a in small_w.items() if n != "final_norm"}
    tabs = _rope_tables(positions.reshape(B, S, 1))
    x1, saved0 = _layer_fwd(x, tabs, mod[0] + flight_w["token"][0, 0], layer_small(0), *whole(got0[1:]))
    got1 = _exchange_wait(flight_w, x1, me, "gather_wait_layer1")
    x2, saved1 = _layer_fwd(x1, tabs, mod[1], layer_small(1), *whole(got1))
    dx, loss, d_fw = _final_loss(x2, final_norm.reshape(1, D), loss_target)

    dx, g1 = _layer_bwd(dx, saved1, tabs)
    flight_g = _exchange_start(to_blocks(g1), [False] * 4, "grads_start_layer1")
    saved0 = dict(saved0, gate=saved0["gate"] + flight_g["token"][0, 0])
    grad_x, g0 = _layer_bwd(dx, saved0, tabs)
    parts1 = _exchange_wait(flight_g, grad_x, me, "grads_wait_layer1")

    both = lambda n: jnp.stack([g0[n], g1[n]])
    d_mod = both("d_mod")
    part = dict(norm_w=both("norm_w"), mla_q_norm=both("mla_q_norm"), mla_kv_norm=both("mla_kv_norm"),
                gla_w_g2=both("gla_w_g2"), gla_b_g2=both("gla_b_g2"), gla_norm=both("gla_norm256")[:, 0:64],
                final_norm=d_fw.reshape(D))
    small_part = _pack_small(jnp.pad(loss.reshape(1), (0, 127)), part)
    last = _exchange([d_mod, small_part] + to_blocks(g0), [True, True, False, False, False, False], "exchange_layer0")
    d_mod_g, small_g, parts0 = last[0], last[1], last[2:]

    d_mod_all = jnp.transpose(d_mod_g, (1, 0, 2, 3)).reshape(DEPTH, N_DEV * B, 3 * D)
    d_mod_cols = lax.dynamic_slice(d_mod_all, (0, 0, me * ada_cols), (DEPTH, N_DEV * B, ada_cols))
    g_ada_w = _ada_bwd(c_all, d_mod_cols)

    res = {}

    def update(name, parts2d):
        shp = w[name].shape
        two = lambda a: a.reshape(parts2d.shape[1:])
        out = _sum_adamw(parts2d, two(w[name]), two(m[name]), two(v[name]), "adamw_" + name)
        res[name] = [o.reshape(shp) for o in out]

    update("ada_w", g_ada_w.reshape(1, DEPTH * D, ada_cols))
    update("ada_b", jnp.transpose(d_mod_g, (0, 2, 1, 3)).reshape(N_DEV * B, DEPTH * 3 * D // 128, 128))
    for a, name in enumerate(sharded):
        update(name, jnp.concatenate([parts0[a], parts1[a]], axis=1))
    zero_row = jnp.zeros((128,), F32)
    small_out = _sum_adamw(small_g, _pack_small(zero_row, small_w), _pack_small(zero_row, {n: m[n] for n, _ in SMALL}),
                           _pack_small(zero_row, {n: v[n] for n, _ in SMALL}), "adamw_small")
    shapes = {n: w[n].shape for n, _ in SMALL}
    unpacked = [_unpack_small(o, shapes) for o in small_out]
    for n, _ in SMALL:
        res[n] = [u[n] for u in unpacked]
    loss_out = small_out[0][0, 0]
    return (loss_out, grad_x, *[res[n][0] for n in WEIGHTS], *[res[n][1] for n in WEIGHTS],
            *[res[n][2] for n in WEIGHTS], *[res[n][3] for n in WEIGHTS])
```

```python
import functools
import math

import numpy as np
import jax
import jax.numpy as jnp
from jax import lax
from jax.experimental import pallas as pl
from jax.experimental.pallas import tpu as pltpu

F32 = jnp.float32
_MXU = jnp.bfloat16

D_MODEL = 1024
DEPTH = 2
CHUNK = 64
EPS = 1e-6
ROPE_THETA = 10000.0
N_DEV = 8

MLA_SCALE = 96.0 ** -0.5
RET_KSCALE = 64.0 ** -0.5
GLA_KSCALE = 32.0 ** -0.5
GLA_TAU = 16.0

ADAM_LR = 0.001
ADAM_B1 = 0.9
ADAM_B2 = 0.999
ADAM_EPS = 1e-08
ADAM_WD = 0.01
ADAM_STEP = 10

RET_W, MLA_W, GLA_W = 1024, 1024, 896
ARR_W = RET_W + MLA_W + GLA_W
VMEM_MB = 1024 * 1024


def _cp(sem, vmem_mb=48):
    return pltpu.CompilerParams(dimension_semantics=sem, vmem_limit_bytes=vmem_mb * VMEM_MB)


def _mm(a, b):
    return jnp.dot(a.astype(_MXU), b.astype(_MXU), preferred_element_type=F32)


def _mm_nt(a, b):
    return lax.dot_general(a.astype(_MXU), b.astype(_MXU), (((1,), (1,)), ((), ())),
                           preferred_element_type=F32)


def _mm_tn(a, b):
    return lax.dot_general(a.astype(_MXU), b.astype(_MXU), (((0,), (0,)), ((), ())),
                           preferred_element_type=F32)


def _mm_f32(a, b):
    return jnp.dot(a, b, precision=lax.Precision.HIGHEST, preferred_element_type=F32)


def _sig(z):
    return 1.0 / (1.0 + jnp.exp(-z))


def _silu(z):
    return z * _sig(z)


def _dsilu(z):
    s = _sig(z)
    return s * (1.0 + z * (1.0 - s))


def _full(shape):
    nd = len(shape)
    return pl.BlockSpec(shape, lambda *_: (0,) * nd)


def _qk_perm(blk):
    r = blk.shape[0]
    return jnp.transpose(blk.reshape(r, 4, 2, 32), (0, 2, 1, 3)).reshape(r, 256)


def _qk_unperm(blk):
    r = blk.shape[0]
    return jnp.transpose(blk.reshape(r, 2, 4, 32), (0, 2, 1, 3)).reshape(r, 256)


def _arrange_w_in(w):
    z = lambda n: jnp.zeros((w.shape[0], n), w.dtype)
    ret = [_qk_perm(w[:, 0:256]), _qk_perm(w[:, 256:512]), w[:, 512:768], w[:, 768:1024]]
    mla = [w[:, 1024:1280], w[:, 1280:1408], z(64), w[:, 1408:1440], z(32), w[:, 1440:1952]]
    gla = [w[:, 1952:2080], w[:, 2080:2208], w[:, 2208:2464], w[:, 2464:2480], z(112), w[:, 2480:2736]]
    return jnp.concatenate(ret + mla + gla, axis=1)


def _unarrange_w_in(a):
    m, g = RET_W, RET_W + MLA_W
    parts = [_qk_unperm(a[:, 0:256]), _qk_unperm(a[:, 256:512]), a[:, 512:1024],
             a[:, m:m + 384], a[:, m + 448:m + 480], a[:, m + 512:m + 1024],
             a[:, g:g + 528], a[:, g + 640:g + 896]]
    return jnp.concatenate(parts, axis=1)


def _arrange_w_uq(w):
    return jnp.pad(w.reshape(256, 8, 96), ((0, 0), (0, 0), (0, 32))).reshape(256, 1024)


def _unarrange_w_uq(a):
    return a.reshape(256, 8, 128)[:, :, :96].reshape(256, 768)


def _arrange_w_ukv(w):
    r = w.reshape(128, 8, 128)
    k = jnp.pad(r[:, :, :64], ((0, 0), (0, 0), (0, 64))).reshape(128, 1024)
    return jnp.concatenate([k, r[:, :, 64:].reshape(128, 512)], axis=1)


def _unarrange_w_ukv(a):
    k = a[:, :1024].reshape(128, 8, 128)[:, :, :64]
    v = a[:, 1024:].reshape(128, 8, 64)
    return jnp.concatenate([k, v], axis=2).reshape(128, 1024)


def _rope_tables(pos3):
    B, S, _ = pos3.shape
    ts = min(S, 512)
    inv32 = (np.float32(ROPE_THETA) ** (-(np.arange(32, dtype=np.float32) / 32))).astype(np.float32)
    inv16 = (np.float32(ROPE_THETA) ** (-(np.arange(16, dtype=np.float32) / 16))).astype(np.float32)
    inv_r = np.tile(inv32, 4)[None, :]
    inv_m = np.zeros((1, 128), np.float32)
    inv_m[0, 64:80] = inv16
    inv_m[0, 80:96] = inv16

    def body(pos_ref, ir_ref, im_ref, cr, sr, cm, sm):
        p = pos_ref[0].astype(F32)
        ar = p * ir_ref[...]
        cr[0] = jnp.cos(ar)
        sr[0] = jnp.sin(ar)
        am = p * im_ref[...]
        cm[0] = jnp.cos(am)
        sm[0] = jnp.sin(am)

    tab = jax.ShapeDtypeStruct((B, S, 128), F32)
    blk = pl.BlockSpec((1, ts, 128), lambda b, i: (b, i, 0))
    return pl.pallas_call(
        body, name="rope_tables", grid=(B, S // ts),
        in_specs=[pl.BlockSpec((1, ts, 1), lambda b, i: (b, i, 0)), _full((1, 128)), _full((1, 128))],
        out_specs=[blk, blk, blk, blk], out_shape=[tab, tab, tab, tab],
        compiler_params=_cp(("parallel", "parallel")),
    )(pos3, jnp.asarray(inv_r), jnp.asarray(inv_m))


def _rope128(x, cos, sin):
    lane = lax.broadcasted_iota(jnp.int32, (1, 128), 1)
    rp = pltpu.roll(x, 16, 1)
    rm = pltpu.roll(x, 112, 1)
    return x * cos + jnp.where(lane < 80, -rm, rp) * sin


def _rope128_t(d, cos, sin):
    lane = lax.broadcasted_iota(jnp.int32, (1, 128), 1)
    y = d * sin
    yp = pltpu.roll(y, 16, 1)
    ym = pltpu.roll(y, 112, 1)
    return d * cos + jnp.where(lane < 64, 0.0, jnp.where(lane < 80, ym, jnp.where(lane < 96, -yp, 0.0)))


def _proj_fwd(x, shift, scale, nw, w_arr):
    B, S, D = x.shape
    tm = min(S, 512)

    def body(x_ref, sh_ref, sc_ref, nw_ref, w_ref, ret_ref, mla_ref, gla_ref, h_ref):
        xv = x_ref[0]
        rstd = lax.rsqrt(jnp.mean(xv * xv, axis=-1, keepdims=True) + EPS)
        h = (xv * rstd * nw_ref[...]) * (1.0 + sc_ref[0]) + sh_ref[0]
        hb = h.astype(_MXU)
        h_ref[0] = hb
        ret_ref[0] = jnp.dot(hb, w_ref[:, 0:RET_W], preferred_element_type=F32)
        mla_ref[0] = jnp.dot(hb, w_ref[:, RET_W:RET_W + MLA_W], preferred_element_type=F32)
        gla_ref[0] = jnp.dot(hb, w_ref[:, RET_W + MLA_W:ARR_W], preferred_element_type=F32)

    tok = lambda w: pl.BlockSpec((1, tm, w), lambda b, i: (b, i, 0))
    per_seq = pl.BlockSpec((1, 1, D), lambda b, i: (b, 0, 0))
    return pl.pallas_call(
        body, name="proj_fwd", grid=(B, S // tm),
        in_specs=[tok(D), per_seq, per_seq, _full((1, D)), _full((D, ARR_W))],
        out_specs=[tok(RET_W), tok(MLA_W), tok(GLA_W), tok(D)],
        out_shape=[jax.ShapeDtypeStruct((B, S, RET_W), F32), jax.ShapeDtypeStruct((B, S, MLA_W), F32),
                   jax.ShapeDtypeStruct((B, S, GLA_W), F32), jax.ShapeDtypeStruct((B, S, D), _MXU)],
        compiler_params=_cp(("parallel", "parallel")),
    )(x, shift, scale, nw, w_arr)


RET_L = 256


def _ret_consts(L):
    lg = np.log1p(-np.exp2(-5.0 - np.arange(4, dtype=np.float32))).astype(np.float32)
    i = np.arange(L)
    ci = i // CHUNK
    diff = (i[:, None] - i[None, :]).astype(np.float32)
    same = ci[:, None] == ci[None, :]
    past = ci[None, :] < ci[:, None]
    expo = np.where(same, np.abs(diff), np.where(past, diff, 0.0)).astype(np.float32)
    dec = np.where((same | past)[None], np.exp(lg[:, None, None] * expo[None]), 0.0).astype(np.float32)
    head = (np.arange(256) % 128) // 32
    qw = np.exp((i + 1.0)[:, None] * lg[head][None, :]).astype(np.float32)
    kw = np.exp((L - 1.0 - i)[:, None] * lg[head][None, :]).astype(np.float32)
    a_row = np.exp(np.float32(L) * lg[head])[None, :].astype(np.float32)
    return [jnp.asarray(t) for t in (dec.reshape(4 * L, L), qw, kw, a_row)]


def _ret_masks():
    lane = lax.broadcasted_iota(jnp.int32, (1, 256), 1)
    mh = [((lane % 128) // 32) == h for h in range(4)]
    mv = [(lane // 64) == h for h in range(4)]
    vi = lax.broadcasted_iota(jnp.int32, (256, 256), 0)
    ki = lax.broadcasted_iota(jnp.int32, (256, 256), 1)
    bd = (vi // 64) == ((ki % 128) // 32)
    return mh, mv, bd


def _ret_rope(p, cs, sn):
    q1, q2, k1, k2 = p[:, 0:128], p[:, 128:256], p[:, 256:384], p[:, 384:512]
    qr = jnp.concatenate([q1 * cs - q2 * sn, q2 * cs + q1 * sn], axis=1)
    kr = jnp.concatenate([k1 * cs - k2 * sn, k2 * cs + k1 * sn], axis=1) * RET_KSCALE
    return qr, kr


def _head_mean(x, mv, width):
    out = jnp.zeros_like(x)
    for m in mv:
        s = jnp.sum(jnp.where(m, x, 0.0), axis=-1, keepdims=True) * (1.0 / width)
        out = jnp.where(m, s, out)
    return out


def _stack_heads(x, masks):
    return jnp.concatenate([jnp.where(m, x, 0.0) for m in masks], axis=0)


def _fold_heads(xs, masks, L):
    out = jnp.where(masks[0], xs[0:L], 0.0)
    for h in range(1, 4):
        out = out + jnp.where(masks[h], xs[h * L:(h + 1) * L], 0.0)
    return out


def _ret_fwd(ret_p, cos, sin):
    B, S, _ = ret_p.shape
    L = min(RET_L, S)
    NB = S // L
    consts = _ret_consts(L)

    def body(p_ref, c_ref, s_ref, ds_ref, qw_ref, kw_ref, a_ref, out_ref, raw_ref, st_ref, st_sc):
        @pl.when(pl.program_id(1) == 0)
        def _():
            st_sc[...] = jnp.zeros_like(st_sc)

        mh, mv, bd = _ret_masks()
        p = p_ref[0]
        qr, kr = _ret_rope(p, c_ref[0], s_ref[0])
        v = p[:, 512:768]
        z = p[:, 768:1024]
        a_s = _mm_nt(_stack_heads(qr, mh), kr) * ds_ref[...]
        intra = _fold_heads(_mm(a_s, v), mv, L)
        st = st_sc[...]
        st_ref[0, 0] = st
        r = intra + _mm_nt(qr * qw_ref[...], st)
        raw_ref[0] = r
        st_sc[...] = st * a_ref[...] + jnp.where(bd, _mm_tn(v, kr * kw_ref[...]), 0.0)
        rstd = lax.rsqrt(_head_mean(r * r, mv, 64.0) + EPS)
        out_ref[0] = (r * rstd * _silu(z)).astype(_MXU)

    tok = lambda w: pl.BlockSpec((1, L, w), lambda b, n: (b, n, 0))
    return pl.pallas_call(
        body, name="ret_fwd", grid=(B, NB),
        in_specs=[tok(RET_W), tok(128), tok(128), _full((4 * L, L)), _full((L, 256)), _full((L, 256)),
                  _full((1, 256))],
        out_specs=[tok(256), tok(256), pl.BlockSpec((1, 1, 256, 256), lambda b, n: (b, n, 0, 0))],
        out_shape=[jax.ShapeDtypeStruct((B, S, 256), _MXU), jax.ShapeDtypeStruct((B, S, 256), F32),
                   jax.ShapeDtypeStruct((B, NB, 256, 256), F32)],
        scratch_shapes=[pltpu.VMEM((256, 256), F32)],
        compiler_params=_cp(("parallel", "arbitrary")),
    )(ret_p, cos, sin, *consts)


def _ret_bwd(ret_p, cos, sin, raw, states, d_mix):
    B, S, _ = ret_p.shape
    L = min(RET_L, S)
    NB = S // L
    consts = _ret_consts(L)

    def body(p_ref, c_ref, s_ref, raw_ref, st_ref, dm_ref, ds_ref, qw_ref, kw_ref, a_ref, dp_ref, dst_sc):
        @pl.when(pl.program_id(1) == 0)
        def _():
            dst_sc[...] = jnp.zeros_like(dst_sc)

        mh, mv, bd = _ret_masks()
        p = p_ref[0]
        cs, sn = c_ref[0], s_ref[0]
        qr, kr = _ret_rope(p, cs, sn)
        v = p[:, 512:768]
        z = p[:, 768:1024]
        qs = _stack_heads(qr, mh)
        dec = ds_ref[...]
        a_s = _mm_nt(qs, kr) * dec
        r = raw_ref[0]
        rstd = lax.rsqrt(_head_mean(r * r, mv, 64.0) + EPS)
        rn = r * rstd
        dm = dm_ref[0]
        d_rn = dm * _silu(z)
        dz = dm * rn * _dsilu(z)
        dr = rstd * (d_rn - rn * _head_mean(d_rn * rn, mv, 64.0))
        do_s = _stack_heads(dr, mv)
        da_s = _mm_nt(do_s, v) * dec
        dv = _mm_tn(a_s, do_s)
        dqr = _fold_heads(_mm(da_s, kr), mh, L)
        dkr = _mm_tn(da_s, qs)
        st = st_ref[0, 0]
        qw, kw = qw_ref[...], kw_ref[...]
        dqr = dqr + _mm(dr, st) * qw
        dst_next = dst_sc[...]
        g = jnp.where(bd, dst_next, 0.0)
        kk = kr * kw
        dv = dv + _mm_nt(kk, g)
        dkr = dkr + _mm(v, g) * kw
        dst_sc[...] = dst_next * a_ref[...] + jnp.where(bd, _mm_tn(dr, qr * qw), 0.0)
        dkr = dkr * RET_KSCALE
        dq1, dq2 = dqr[:, 0:128], dqr[:, 128:256]
        dk1, dk2 = dkr[:, 0:128], dkr[:, 128:256]
        dp_ref[0] = jnp.concatenate(
            [dq1 * cs + dq2 * sn, dq2 * cs - dq1 * sn, dk1 * cs + dk2 * sn, dk2 * cs - dk1 * sn, dv, dz],
            axis=1).astype(_MXU)

    tok = lambda w: pl.BlockSpec((1, L, w), lambda b, i: (b, NB - 1 - i, 0))
    return pl.pallas_call(
        body, name="ret_bwd", grid=(B, NB),
        in_specs=[tok(RET_W), tok(128), tok(128), tok(256),
                  pl.BlockSpec((1, 1, 256, 256), lambda b, i: (b, NB - 1 - i, 0, 0)), tok(256),
                  _full((4 * L, L)), _full((L, 256)), _full((L, 256)), _full((1, 256))],
        out_specs=tok(RET_W), out_shape=jax.ShapeDtypeStruct((B, S, RET_W), _MXU),
        scratch_shapes=[pltpu.VMEM((256, 256), F32)],
        compiler_params=_cp(("parallel", "arbitrary")),
    )(ret_p, cos, sin, raw, states, d_mix, *consts)


def _gla_masks():
    C = CHUNK
    lk = lax.broadcasted_iota(jnp.int32, (1, 128), 1)
    lv = lax.broadcasted_iota(jnp.int32, (1, 256), 1)
    mk = [(lk // 32) == h for h in range(4)]
    mv = [(lv // 64) == h for h in range(4)]
    vi = lax.broadcasted_iota(jnp.int32, (256, 128), 0)
    ki = lax.broadcasted_iota(jnp.int32, (256, 128), 1)
    bd = (vi // 64) == (ki // 32)
    ri = lax.broadcasted_iota(jnp.int32, (4 * C, C), 0) % C
    cj = lax.broadcasted_iota(jnp.int32, (4 * C, C), 1)
    lower = ri >= cj
    ti = lax.broadcasted_iota(jnp.int32, (C, C), 0)
    tj = lax.broadcasted_iota(jnp.int32, (C, C), 1)
    ltri = jnp.where(ti >= tj, 1.0, 0.0).astype(F32)
    utri = jnp.where(ti <= tj, 1.0, 0.0).astype(F32)
    return mk, mv, bd, lower, ltri, utri


def _log_sigmoid(x):
    return jnp.minimum(x, 0.0) - jnp.log(1.0 + jnp.exp(-jnp.abs(x)))


GLA_G = 8


def _gla_fwd(gla_p, w_g2p, b_g2, gnw):
    B, S, _ = gla_p.shape
    C = CHUNK
    NC = S // C
    G = min(GLA_G, NC)
    NG = NC // G

    def body(p_ref, w_ref, b_ref, gn_ref, out_ref, raw_ref, st_ref, st_sc):
        @pl.when(pl.program_id(1) == 0)
        def _():
            st_sc[...] = jnp.zeros_like(st_sc)

        mk, mv, bd, lower, ltri, _ = _gla_masks()
        cs = range(G)
        rows = [slice(c * C, (c + 1) * C) for c in cs]
        ps = [p_ref[0, rows[c], :] for c in cs]
        pre = [_mm(ps[c][:, 512:640], w_ref[...]) + b_ref[...] for c in cs]
        cum = [_mm_f32(ltri, _log_sigmoid(pre[c]) * (1.0 / GLA_TAU)) for c in cs]
        past, fut, upd, q_pos, a_row = [], [], [], [], []
        for c in cs:
            q = ps[c][:, 0:128]
            k = ps[c][:, 128:256] * GLA_KSCALE
            last = cum[c][C - 1:C, :]
            e_pos = jnp.exp(cum[c])
            e_neg = jnp.exp(-cum[c])
            q_pos.append(q * e_pos)
            a_row.append(jnp.exp(last))
            past.append(_mm_nt(_stack_heads(q_pos[c], mk), k * e_neg))
            fut.append(_mm_nt(_stack_heads(q * e_neg, mk), k * e_pos))
            upd.append(_mm_tn(ps[c][:, 256:512], k * jnp.exp(last - cum[c])))
        o_s = [_mm(jnp.where(lower, past[c], fut[c]), ps[c][:, 256:512]) for c in cs]
        st = st_sc[...]
        inter = []
        for c in cs:
            st_ref[0, c] = st
            inter.append(_mm_nt(q_pos[c], st))
            st = st * a_row[c] + jnp.where(bd, upd[c], 0.0)
        st_sc[...] = st
        for c in cs:
            g = _fold_heads(o_s[c], mv, C) + inter[c]
            raw_ref[0, rows[c], :] = g
            rstd = lax.rsqrt(_head_mean(g * g, mv, 64.0) + EPS)
            out_ref[0, rows[c], :] = (g * rstd * gn_ref[...] * _silu(ps[c][:, 640:896])).astype(_MXU)

    tok = lambda w: pl.BlockSpec((1, G * C, w), lambda b, n: (b, n, 0))
    return pl.pallas_call(
        body, name="gla_fwd", grid=(B, NG),
        in_specs=[tok(GLA_W), _full((128, 128)), _full((1, 128)), _full((1, 256))],
        out_specs=[tok(256), tok(256), pl.BlockSpec((1, G, 256, 128), lambda b, n: (b, n, 0, 0))],
        out_shape=[jax.ShapeDtypeStruct((B, S, 256), _MXU), jax.ShapeDtypeStruct((B, S, 256), F32),
                   jax.ShapeDtypeStruct((B, NC, 256, 128), F32)],
        scratch_shapes=[pltpu.VMEM((256, 128), F32)],
        compiler_params=_cp(("parallel", "arbitrary")),
    )(gla_p, w_g2p, b_g2, gnw)


def _gla_bwd(gla_p, w_g2p, b_g2, gnw, raw, states, d_mix):
    B, S, _ = gla_p.shape
    C = CHUNK
    NC = S // C
    G = min(GLA_G, NC)
    NG = NC // G

    def body(p_ref, w_ref, b_ref, gn_ref, raw_ref, st_ref, dm_ref, dp_ref, dw_ref, db_ref, dgn_ref, dst_sc):
        first = (pl.program_id(0) == 0) & (pl.program_id(1) == 0)

        @pl.when(first)
        def _():
            dw_ref[...] = jnp.zeros_like(dw_ref)
            db_ref[...] = jnp.zeros_like(db_ref)
            dgn_ref[...] = jnp.zeros_like(dgn_ref)

        @pl.when(pl.program_id(1) == 0)
        def _():
            dst_sc[...] = jnp.zeros_like(dst_sc)

        mk, mv, bd, lower, ltri, utri = _gla_masks()
        gn = gn_ref[...]
        cs = range(G)
        rows = [slice(c * C, (c + 1) * C) for c in cs]
        ps = [p_ref[0, rows[c], :] for c in cs]
        vs = [ps[c][:, 256:512] for c in cs]
        pre = [_mm(ps[c][:, 512:640], w_ref[...]) + b_ref[...] for c in cs]
        cum = [_mm_f32(ltri, _log_sigmoid(pre[c]) * (1.0 / GLA_TAU)) for c in cs]
        dg, dz, dgn_acc = [], [], jnp.zeros((1, 256), F32)
        for c in cs:
            g = raw_ref[0, rows[c], :]
            z = ps[c][:, 640:896]
            rstd = lax.rsqrt(_head_mean(g * g, mv, 64.0) + EPS)
            gh = g * rstd
            dm = dm_ref[0, rows[c], :]
            d_gn = dm * _silu(z)
            dz.append(dm * gh * gn * _dsilu(z))
            dgn_acc = dgn_acc + jnp.sum(d_gn * gh, axis=0, keepdims=True)
            d_gh = d_gn * gn
            dg.append(rstd * (d_gh - gh * _head_mean(d_gh * gh, mv, 64.0)))
        do_s = [_stack_heads(dg[c], mv) for c in cs]
        dattn = [_mm_nt(do_s[c], vs[c]) for c in cs]
        ks, e_pos, e_neg, q_pos, q_neg, k_pos, k_neg, qp_s, qn_s, past, fut, a_row, w_dec, kd = ([] for _ in range(14))
        for c in cs:
            q = ps[c][:, 0:128]
            k = ps[c][:, 128:256] * GLA_KSCALE
            last = cum[c][C - 1:C, :]
            ep, en = jnp.exp(cum[c]), jnp.exp(-cum[c])
            ks.append(k), e_pos.append(ep), e_neg.append(en)
            q_pos.append(q * ep), q_neg.append(q * en), k_pos.append(k * ep), k_neg.append(k * en)
            qp_s.append(_stack_heads(q_pos[c], mk)), qn_s.append(_stack_heads(q_neg[c], mk))
            past.append(_mm_nt(qp_s[c], k_neg[c]))
            fut.append(_mm_nt(qn_s[c], k_pos[c]))
            a_row.append(jnp.exp(last))
            w_dec.append(jnp.exp(last - cum[c]))
            kd.append(k * w_dec[c])
        sts = [st_ref[0, c] for c in cs]
        dq_st = [_mm(dg[c], sts[c]) for c in cs]
        dst_in = [_mm_tn(dg[c], q_pos[c]) for c in cs]
        dv, dq_pos, dk_neg, dq_neg, dk_pos = [], [], [], [], []
        for c in cs:
            attn = jnp.where(lower, past[c], fut[c])
            dpast = jnp.where(lower, dattn[c], 0.0)
            dfut = jnp.where(lower, 0.0, dattn[c])
            dv.append(_mm_tn(attn, do_s[c]))
            dq_pos.append(_fold_heads(_mm(dpast, k_neg[c]), mk, C) + dq_st[c])
            dk_neg.append(_mm_tn(dpast, qp_s[c]))
            dq_neg.append(_fold_heads(_mm(dfut, k_pos[c]), mk, C))
            dk_pos.append(_mm_tn(dfut, qn_s[c]))
        dst_next = dst_sc[...]
        d_a, d_kd = [None] * G, [None] * G
        for c in reversed(cs):
            d_a[c] = jnp.sum(dst_next * sts[c], axis=0, keepdims=True)
            gmat = jnp.where(bd, dst_next, 0.0)
            d_kd[c] = _mm(vs[c], gmat)
            dv[c] = dv[c] + _mm_nt(kd[c], gmat)
            dst_next = dst_next * a_row[c] + jnp.where(bd, dst_in[c], 0.0)
        dst_sc[...] = dst_next
        row = lax.broadcasted_iota(jnp.int32, (C, 128), 0)
        d_la, dk, dq = [], [], []
        for c in cs:
            t = d_kd[c] * kd[c]
            dk.append(d_kd[c] * w_dec[c] + dk_neg[c] * e_neg[c] + dk_pos[c] * e_pos[c])
            dq.append(dq_pos[c] * e_pos[c] + dq_neg[c] * e_neg[c])
            d_last = jnp.sum(t, axis=0, keepdims=True) + d_a[c] * a_row[c]
            d_cum = (dq_pos[c] * q_pos[c] - dk_neg[c] * k_neg[c] - dq_neg[c] * q_neg[c] + dk_pos[c] * k_pos[c] - t)
            d_la.append(_mm_f32(utri, d_cum + jnp.where(row == C - 1, d_last, 0.0)))
        d_pre = [d_la[c] * _sig(-pre[c]) * (1.0 / GLA_TAU) for c in cs]
        d_gg = [_mm_nt(d_pre[c], w_ref[...]) for c in cs]
        dw_acc = _mm_tn(ps[0][:, 512:640], d_pre[0])
        db_acc = jnp.sum(d_pre[0], axis=0, keepdims=True)
        for c in cs[1:]:
            dw_acc = dw_acc + _mm_tn(ps[c][:, 512:640], d_pre[c])
            db_acc = db_acc + jnp.sum(d_pre[c], axis=0, keepdims=True)
        for c in cs:
            dp_ref[0, rows[c], :] = jnp.concatenate([dq[c], dk[c] * GLA_KSCALE, dv[c], d_gg[c], dz[c]],
                                                    axis=1).astype(_MXU)
        dw_ref[...] += dw_acc
        db_ref[...] += db_acc
        dgn_ref[...] += dgn_acc

        @pl.when((pl.program_id(0) == B - 1) & (pl.program_id(1) == NG - 1))
        def _():
            s1 = dgn_ref[...]
            s1 = s1 + pltpu.roll(s1, 128, 1)
            dgn_ref[...] = s1 + pltpu.roll(s1, 64, 1)

    tok = lambda w: pl.BlockSpec((1, G * C, w), lambda b, i: (b, NG - 1 - i, 0))
    return pl.pallas_call(
        body, name="gla_bwd", grid=(B, NG),
        in_specs=[tok(GLA_W), _full((128, 128)), _full((1, 128)), _full((1, 256)), tok(256),
                  pl.BlockSpec((1, G, 256, 128), lambda b, i: (b, NG - 1 - i, 0, 0)), tok(256)],
        out_specs=[tok(GLA_W), _full((128, 128)), _full((1, 128)), _full((1, 256))],
        out_shape=[jax.ShapeDtypeStruct((B, S, GLA_W), _MXU), jax.ShapeDtypeStruct((128, 128), F32),
                   jax.ShapeDtypeStruct((1, 128), F32), jax.ShapeDtypeStruct((1, 256), F32)],
        scratch_shapes=[pltpu.VMEM((256, 128), F32)],
        compiler_params=_cp(("arbitrary", "arbitrary")),
    )(gla_p, w_g2p, b_g2, gnw, raw, states, d_mix)


def _rms(x, w):
    rstd = lax.rsqrt(jnp.mean(x * x, axis=-1, keepdims=True) + EPS)
    xh = x * rstd
    return xh, rstd, xh * w


def _rms_bwd(dy, xh, rstd, w):
    dxh = dy * w
    return rstd * (dxh - xh * jnp.mean(dxh * xh, axis=-1, keepdims=True))


MLA_T = 256


def _mla_prep_fwd(mla_p, cos, sin, qnw, kvnw, w_uq, w_ukv):
    B, S, _ = mla_p.shape
    tm = min(S, 512)

    t = min(MLA_T, S)
    nt = tm // t

    def body(p_ref, c_ref, s_ref, qn_ref, kn_ref, wq_ref, wkv_ref, q_ref, k_ref, v_ref, kt_ref, vt_ref):
        p = p_ref[0]
        cs, sn = c_ref[0], s_ref[0]
        _, _, qn = _rms(p[:, 0:256], qn_ref[...])
        qpre = _mm(qn, wq_ref[...])
        _, _, kvn = _rms(p[:, 256:384], kn_ref[...])
        kv = _mm(kvn, wkv_ref[...])
        kpe = _rope128(p[:, 384:512], cs, sn)
        for h in range(8):
            sl = slice(128 * h, 128 * h + 128)
            q_ref[0, :, sl] = _rope128(qpre[:, sl], cs, sn).astype(_MXU)
            kh = kv[:, sl] + kpe
            k_ref[0, :, sl] = kh.astype(_MXU)
            kht = kh.T
            for n in range(nt):
                kt_ref[0, n, sl, :] = kht[:, n * t:(n + 1) * t].astype(_MXU)
        v_ref[0] = kv[:, 1024:1536].astype(_MXU)
        for pr in range(4):
            vht = kv[:, 1024 + 128 * pr:1152 + 128 * pr].T
            for n in range(nt):
                vt_ref[0, n, 128 * pr:128 * pr + 128, :] = vht[:, n * t:(n + 1) * t].astype(_MXU)

    tok = lambda w: pl.BlockSpec((1, tm, w), lambda b, i: (b, i, 0))
    tr = lambda w: pl.BlockSpec((1, nt, w, t), lambda b, i: (b, i, 0, 0))
    return pl.pallas_call(
        body, name="mla_prep_fwd", grid=(B, S // tm),
        in_specs=[tok(512), tok(128), tok(128), _full((1, 256)), _full((1, 128)), _full((256, 1024)),
                  _full((128, 1536))],
        out_specs=[tok(1024), tok(1024), tok(512), tr(1024), tr(512)],
        out_shape=[jax.ShapeDtypeStruct((B, S, 1024), _MXU), jax.ShapeDtypeStruct((B, S, 1024), _MXU),
                   jax.ShapeDtypeStruct((B, S, 512), _MXU), jax.ShapeDtypeStruct((B, S // t, 1024, t), _MXU),
                   jax.ShapeDtypeStruct((B, S // t, 512, t), _MXU)],
        compiler_params=_cp(("parallel", "parallel")),
    )(mla_p, cos, sin, qnw, kvnw, w_uq, w_ukv)


def _chunk_mask_t(t):
    kj = lax.broadcasted_iota(jnp.int32, (t, t), 0) // CHUNK
    qi = lax.broadcasted_iota(jnp.int32, (t, t), 1) // CHUNK
    return kj <= qi


MLA_HG = 4
LOG2E = 1.4426950408889634
MLA_C2 = MLA_SCALE * LOG2E


def _mla_attn_fwd(q, k, vt):
    B, S, _ = q.shape
    t = min(MLA_T, S)
    nq = S // t
    HG = MLA_HG
    NP = HG // 2

    def body(q_ref, k_ref, vt_ref, o_ref, lse_ref, sa, sb, m_sc, l_sc, acc_sc):
        i = pl.program_id(2)
        row = lax.broadcasted_iota(jnp.int32, (128, 1), 0)
        low = row < 64
        mask = _chunk_mask_t(t)
        m_sc[...] = jnp.full(m_sc.shape, -jnp.inf, F32)
        l_sc[...] = jnp.zeros_like(l_sc)
        acc_sc[...] = jnp.zeros_like(acc_sc)

        def scores(j, buf):
            kb = k_ref[0, pl.ds(pl.multiple_of(j * t, t), t), :]
            for h in range(HG):
                cols = slice(128 * h, 128 * h + 128)
                buf[h] = _mm_nt(kb[:, cols], q_ref[0, :, cols]) * MLA_C2

        def absorb(j, buf, masked):
            vtb = vt_ref[0, j]
            for pr in range(NP):
                alphas, pvs = [], []
                for hh in range(2):
                    h = 2 * pr + hh
                    s = buf[h]
                    if masked:
                        s = jnp.where(mask, s, -jnp.inf)
                    m_old = m_sc[h]
                    m_new = jnp.maximum(m_old, jnp.max(s, axis=0, keepdims=True))
                    alpha = jnp.exp2(m_old - m_new)
                    p = jnp.exp2(s - m_new)
                    l_sc[h] = alpha * l_sc[h] + jnp.sum(p, axis=0, keepdims=True)
                    m_sc[h] = m_new
                    vth = vtb[128 * pr:128 * pr + 128, :]
                    vth = jnp.where(low if hh == 0 else ~low, vth, jnp.zeros_like(vth))
                    pvs.append(_mm(vth, p))
                    alphas.append(alpha)
                acc_sc[pr] = acc_sc[pr] * jnp.where(low, alphas[0], alphas[1]) + pvs[0] + pvs[1]

        scores(i, sa)

        @pl.when(i > 0)
        def _():
            scores(0, sb)

        absorb(i, sa, True)

        def pair(jj, carry):
            j0 = 2 * jj
            scores(jnp.minimum(j0 + 1, i - 1), sa)
            absorb(j0, sb, False)
            scores(jnp.minimum(j0 + 2, i - 1), sb)
            absorb(j0 + 1, sa, False)
            return carry

        lax.fori_loop(0, i // 2, pair, 0)

        @pl.when(i % 2 == 1)
        def _():
            absorb(i - 1, sb, False)

        for pr in range(NP):
            l_e, l_o = l_sc[2 * pr], l_sc[2 * pr + 1]
            o_ref[0, :, 128 * pr:128 * pr + 128] = (acc_sc[pr] / jnp.where(low, l_e, l_o)).T
            lse_ref[0, pr, 0, 0:1, :] = m_sc[2 * pr] + jnp.log(l_e) * LOG2E
            lse_ref[0, pr, 0, 1:2, :] = m_sc[2 * pr + 1] + jnp.log(l_o) * LOG2E

    return pl.pallas_call(
        body, name="mla_attn_fwd", grid=(B, 8 // HG, nq),
        in_specs=[pl.BlockSpec((1, t, 128 * HG), lambda b, g, i: (b, i, g)),
                  pl.BlockSpec((1, S, 128 * HG), lambda b, g, i: (b, 0, g)),
                  pl.BlockSpec((1, nq, 64 * HG, t), lambda b, g, i: (b, 0, g, 0))],
        out_specs=[pl.BlockSpec((1, t, 64 * HG), lambda b, g, i: (b, i, g)),
                   pl.BlockSpec((1, NP, 1, 2, t), lambda b, g, i: (b, g, i, 0, 0))],
        out_shape=[jax.ShapeDtypeStruct((B, S, 512), F32), jax.ShapeDtypeStruct((B, 4, nq, 2, t), F32)],
        scratch_shapes=[pltpu.VMEM((HG, t, t), F32), pltpu.VMEM((HG, t, t), F32), pltpu.VMEM((HG, 1, t), F32),
                        pltpu.VMEM((HG, 1, t), F32), pltpu.VMEM((NP, 128, t), F32)],
        compiler_params=_cp(("parallel", "parallel", "arbitrary")),
    )(q, k, vt)


def _mla_gate_bwd(d_mix, o, mla_p):
    B, S, _ = o.shape
    tm = min(S, 512)
    t = min(MLA_T, S)
    nt = tm // t

    def body(dm_ref, o_ref, z_ref, do_ref, dz_ref, dl_ref):
        dm, ov, z = dm_ref[0], o_ref[0], z_ref[0]
        do = dm * _silu(z)
        dz_ref[0] = (dm * ov * _dsilu(z)).astype(_MXU)
        do_ref[0] = do.astype(_MXU)
        prod = do * ov
        for pr in range(4):
            pt = prod[:, 128 * pr:128 * pr + 128].T
            se = jnp.sum(pt[0:64], axis=0, keepdims=True)
            so = jnp.sum(pt[64:128], axis=0, keepdims=True)
            for n in range(nt):
                dl_ref[0, pr, n, 0:1, :] = se[:, n * t:(n + 1) * t]
                dl_ref[0, pr, n, 1:2, :] = so[:, n * t:(n + 1) * t]

    tok = lambda c: pl.BlockSpec((1, tm, 512), lambda b, i: (b, i, c))
    return pl.pallas_call(
        body, name="mla_gate_bwd", grid=(B, S // tm),
        in_specs=[tok(0), tok(0), tok(1)],
        out_specs=[tok(0), tok(0), pl.BlockSpec((1, 4, nt, 2, t), lambda b, i: (b, 0, i, 0, 0))],
        out_shape=[jax.ShapeDtypeStruct((B, S, 512), _MXU), jax.ShapeDtypeStruct((B, S, 512), _MXU),
                   jax.ShapeDtypeStruct((B, 4, S // t, 2, t), F32)],
        compiler_params=_cp(("parallel", "parallel")),
    )(d_mix, o, mla_p)


def _mla_attn_bwd(q, k, v, kt, do, lse, dl):
    B, S, _ = q.shape
    t = min(MLA_T, S)
    nk = S // t

    HG = MLA_HG
    NP = HG // 2

    def body(q_ref, k_ref, v_ref, kt_ref, do_ref, lse_ref, dl_ref, dq_ref, dk_ref, dv_ref,
             sa, da, sb, db, dqt_sc, dk_sc, dv_sc):
        j = pl.program_id(2)

        @pl.when(j == 0)
        def _():
            dqt_sc[...] = jnp.zeros_like(dqt_sc)

        dk_sc[...] = jnp.zeros_like(dk_sc)
        dv_sc[...] = jnp.zeros_like(dv_sc)
        lane = lax.broadcasted_iota(jnp.int32, (1, 128), 1)
        low = lane < 64
        mask = _chunk_mask_t(t)

        def half(x, hh):
            return jnp.where(low if hh == 0 else ~low, x, jnp.zeros_like(x))

        def prepare(i, sbuf, dbuf):
            rows = pl.ds(pl.multiple_of(i * t, t), t)
            for h in range(HG):
                cols = slice(128 * h, 128 * h + 128)
                pc = slice(128 * (h // 2), 128 * (h // 2) + 128)
                sbuf[h] = _mm_nt(k_ref[0, :, cols], q_ref[0, rows, cols]) * MLA_C2
                dbuf[h] = _mm_nt(half(v_ref[0, :, pc], h % 2), do_ref[0, rows, pc])

        def absorb(i, sbuf, dbuf, masked):
            rows = pl.ds(pl.multiple_of(i * t, t), t)
            for h in range(HG):
                pr, hh = h // 2, h % 2
                cols = slice(128 * h, 128 * h + 128)
                pc = slice(128 * pr, 128 * pr + 128)
                p = jnp.exp2(sbuf[h] - lse_ref[0, pr, i][hh:hh + 1, :])
                if masked:
                    p = jnp.where(mask, p, 0.0)
                dv_sc[pr] += _mm(p, half(do_ref[0, rows, pc], hh))
                ds = p * (dbuf[h] - dl_ref[0, pr, i][hh:hh + 1, :])
                dqt_sc[i, cols, :] += _mm(kt_ref[0, 0, cols, :], ds)
                dk_sc[h] += _mm(ds, q_ref[0, rows, cols])

        n = nk - 1 - j
        prepare(j, sa, da)

        @pl.when(n > 0)
        def _():
            prepare(j + 1, sb, db)

        absorb(j, sa, da, True)

        def pair(jj, carry):
            i0 = j + 1 + 2 * jj
            prepare(jnp.minimum(i0 + 1, nk - 1), sa, da)
            absorb(i0, sb, db, False)
            prepare(jnp.minimum(i0 + 2, nk - 1), sb, db)
            absorb(i0 + 1, sa, da, False)
            return carry

        lax.fori_loop(0, n // 2, pair, 0)

        @pl.when(n % 2 == 1)
        def _():
            absorb(nk - 1, sb, db, False)

        for h in range(HG):
            dk_ref[0, :, 128 * h:128 * h + 128] = dk_sc[h] * MLA_SCALE
        for pr in range(NP):
            dv_ref[0, :, 128 * pr:128 * pr + 128] = dv_sc[pr]

        @pl.when(j == nk - 1)
        def _():
            for i in range(nk):
                dq_ref[0, i * t:(i + 1) * t, :] = dqt_sc[i].T * MLA_SCALE

    seq = lambda w: pl.BlockSpec((1, S, w), lambda b, g, j: (b, 0, g))
    blk = lambda w: pl.BlockSpec((1, t, w), lambda b, g, j: (b, j, g))
    stat = pl.BlockSpec((1, NP, nk, 2, t), lambda b, g, j: (b, g, 0, 0, 0))
    return pl.pallas_call(
        body, name="mla_attn_bwd", grid=(B, 8 // HG, nk),
        in_specs=[seq(128 * HG), blk(128 * HG), blk(64 * HG),
                  pl.BlockSpec((1, 1, 128 * HG, t), lambda b, g, j: (b, j, g, 0)), seq(64 * HG), stat, stat],
        out_specs=[seq(128 * HG), blk(128 * HG), blk(64 * HG)],
        out_shape=[jax.ShapeDtypeStruct((B, S, 1024), F32), jax.ShapeDtypeStruct((B, S, 1024), F32),
                   jax.ShapeDtypeStruct((B, S, 512), F32)],
        scratch_shapes=[pltpu.VMEM((HG, t, t), F32), pltpu.VMEM((HG, t, t), F32), pltpu.VMEM((HG, t, t), F32),
                        pltpu.VMEM((HG, t, t), F32), pltpu.VMEM((nk, 128 * HG, t), F32),
                        pltpu.VMEM((HG, t, 128), F32), pltpu.VMEM((NP, t, 128), F32)],
        compiler_params=_cp(("parallel", "parallel", "arbitrary")),
    )(q, k, v, kt, do, lse, dl)


def _mla_prep_bwd(mla_p, cos, sin, qnw, kvnw, w_uq, w_ukv, dq, dk, dv):
    B, S, _ = mla_p.shape
    tm = min(S, 512)

    def body(p_ref, c_ref, s_ref, qn_ref, kn_ref, wq_ref, wkv_ref, dq_ref, dk_ref, dv_ref,
             dp_ref, dwq_ref, dwkv_ref, dqn_ref, dkn_ref):
        first = (pl.program_id(0) == 0) & (pl.program_id(1) == 0)

        @pl.when(first)
        def _():
            dwq_ref[...] = jnp.zeros_like(dwq_ref)
            dwkv_ref[...] = jnp.zeros_like(dwkv_ref)
            dqn_ref[...] = jnp.zeros_like(dqn_ref)
            dkn_ref[...] = jnp.zeros_like(dkn_ref)

        p = p_ref[0]
        cs, sn = c_ref[0], s_ref[0]
        lane = lax.broadcasted_iota(jnp.int32, (1, 128), 1)
        pe = (lane >= 64) & (lane < 96)
        qh, q_rstd, qn = _rms(p[:, 0:256], qn_ref[...])
        kvh, kv_rstd, kvn = _rms(p[:, 256:384], kn_ref[...])
        dqv = dq_ref[0]
        dkv = dk_ref[0]
        dqpre = jnp.concatenate(
            [_rope128_t(dqv[:, 128 * h:128 * h + 128], cs, sn) for h in range(8)], axis=1)
        dkpe = jnp.zeros((tm, 128), F32)
        for h in range(8):
            dkpe = dkpe + jnp.where(pe, dkv[:, 128 * h:128 * h + 128], 0.0)
        dkr = _rope128_t(dkpe, cs, sn)
        dkv_all = jnp.concatenate([dkv, dv_ref[0]], axis=1)
        d_qn = _mm_nt(dqpre, wq_ref[...])
        d_kvn = _mm_nt(dkv_all, wkv_ref[...])
        dwq_ref[...] += _mm_tn(qn, dqpre)
        dwkv_ref[...] += _mm_tn(kvn, dkv_all)
        dqn_ref[...] += jnp.sum(d_qn * qh, axis=0, keepdims=True)
        dkn_ref[...] += jnp.sum(d_kvn * kvh, axis=0, keepdims=True)
        dp_ref[0] = jnp.concatenate([_rms_bwd(d_qn, qh, q_rstd, qn_ref[...]),
                                     _rms_bwd(d_kvn, kvh, kv_rstd, kn_ref[...]), dkr], axis=1).astype(_MXU)

    tok = lambda w: pl.BlockSpec((1, tm, w), lambda b, i: (b, i, 0))
    return pl.pallas_call(
        body, name="mla_prep_bwd", grid=(B, S // tm),
        in_specs=[tok(512), tok(128), tok(128), _full((1, 256)), _full((1, 128)), _full((256, 1024)),
                  _full((128, 1536)), tok(1024), tok(1024), tok(512)],
        out_specs=[tok(512), _full((256, 1024)), _full((128, 1536)), _full((1, 256)), _full((1, 128))],
        out_shape=[jax.ShapeDtypeStruct((B, S, 512), _MXU), jax.ShapeDtypeStruct((256, 1024), F32),
                   jax.ShapeDtypeStruct((128, 1536), F32), jax.ShapeDtypeStruct((1, 256), F32),
                   jax.ShapeDtypeStruct((1, 128), F32)],
        compiler_params=_cp(("arbitrary", "arbitrary")),
    )(mla_p, cos, sin, qnw, kvnw, w_uq, w_ukv, dq, dk, dv)


def _out_fwd(x, gate, r_g, o_mla, mla_p, g_g, w_out):
    B, S, D = x.shape
    tm = min(S, 512)

    def body(x_ref, g_ref, r_ref, o_ref, z_ref, gg_ref, w_ref, xn_ref, y_ref, mm_ref):
        mm = (o_ref[0] * _silu(z_ref[0])).astype(_MXU)
        mm_ref[0] = mm
        y = (jnp.dot(r_ref[0], w_ref[0:256, :], preferred_element_type=F32)
             + jnp.dot(mm, w_ref[256:768, :], preferred_element_type=F32)
             + jnp.dot(gg_ref[0], w_ref[768:1024, :], preferred_element_type=F32))
        y_ref[0] = y
        xn_ref[0] = x_ref[0] + g_ref[0] * y

    tok = lambda w, c=0: pl.BlockSpec((1, tm, w), lambda b, i: (b, i, c))
    return pl.pallas_call(
        body, name="out_fwd", grid=(B, S // tm),
        in_specs=[tok(D), pl.BlockSpec((1, 1, D), lambda b, i: (b, 0, 0)), tok(256), tok(512), tok(512, 1),
                  tok(256), _full((D, D))],
        out_specs=[tok(D), tok(D), tok(512)],
        out_shape=[jax.ShapeDtypeStruct((B, S, D), F32), jax.ShapeDtypeStruct((B, S, D), F32),
                   jax.ShapeDtypeStruct((B, S, 512), _MXU)],
        compiler_params=_cp(("parallel", "parallel")),
    )(x, gate, r_g, o_mla, mla_p, g_g, w_out)


def _out_bwd(dx, y, gate, r_g, mm, g_g, w_out):
    B, S, D = dx.shape
    tm = min(S, 512)

    def body(dx_ref, y_ref, g_ref, r_ref, mm_ref, gg_ref, w_ref, dr_ref, dmm_ref, dg_ref, dw_ref, dgate_ref):
        first = (pl.program_id(0) == 0) & (pl.program_id(1) == 0)

        @pl.when(first)
        def _():
            dw_ref[...] = jnp.zeros_like(dw_ref)

        @pl.when(pl.program_id(1) == 0)
        def _():
            dgate_ref[...] = jnp.zeros_like(dgate_ref)

        dxv = dx_ref[0]
        dgate_ref[0] += jnp.sum(dxv * y_ref[0], axis=0, keepdims=True)
        dy = (dxv * g_ref[0]).astype(_MXU)
        dr_ref[0] = _mm_nt(dy, w_ref[0:256, :])
        dmm_ref[0] = _mm_nt(dy, w_ref[256:768, :])
        dg_ref[0] = _mm_nt(dy, w_ref[768:1024, :])
        dw_ref[0:256, :] += _mm_tn(r_ref[0], dy)
        dw_ref[256:768, :] += _mm_tn(mm_ref[0], dy)
        dw_ref[768:1024, :] += _mm_tn(gg_ref[0], dy)

    tok = lambda w: pl.BlockSpec((1, tm, w), lambda b, i: (b, i, 0))
    per_seq = pl.BlockSpec((1, 1, D), lambda b, i: (b, 0, 0))
    return pl.pallas_call(
        body, name="out_bwd", grid=(B, S // tm),
        in_specs=[tok(D), tok(D), per_seq, tok(256), tok(512), tok(256), _full((D, D))],
        out_specs=[tok(256), tok(512), tok(256), _full((D, D)), per_seq],
        out_shape=[jax.ShapeDtypeStruct((B, S, 256), F32), jax.ShapeDtypeStruct((B, S, 512), F32),
                   jax.ShapeDtypeStruct((B, S, 256), F32), jax.ShapeDtypeStruct((D, D), F32),
                   jax.ShapeDtypeStruct((B, 1, D), F32)],
        compiler_params=_cp(("arbitrary", "arbitrary")),
    )(dx, y, gate, r_g, mm, g_g, w_out)


def _proj_bwd_x(x, shift, scale, nw, w_arr, d_ret, d_mla, d_mz, d_gla, dx_out):
    B, S, D = x.shape
    tm = min(S, 256)

    def body(x_ref, sc_ref, nw_ref, w_ref, dr_ref, dm_ref, dz_ref, dg_ref, dxo_ref,
             dx_ref, dsh_ref, dsc_ref, dnw_ref):
        first = (pl.program_id(0) == 0) & (pl.program_id(1) == 0)

        @pl.when(first)
        def _():
            dnw_ref[...] = jnp.zeros_like(dnw_ref)

        @pl.when(pl.program_id(1) == 0)
        def _():
            dsh_ref[...] = jnp.zeros_like(dsh_ref)
            dsc_ref[...] = jnp.zeros_like(dsc_ref)

        dp = jnp.concatenate([dr_ref[0], dm_ref[0], dz_ref[0], dg_ref[0]], axis=1)
        dh = lax.dot_general(dp, w_ref[...], (((1,), (1,)), ((), ())), preferred_element_type=F32)
        xv = x_ref[0]
        rstd = lax.rsqrt(jnp.mean(xv * xv, axis=-1, keepdims=True) + EPS)
        xh = xv * rstd
        nwv = nw_ref[...]
        mod = 1.0 + sc_ref[0]
        dsh_ref[0] += jnp.sum(dh, axis=0, keepdims=True)
        dsc_ref[0] += jnp.sum(dh * xh * nwv, axis=0, keepdims=True)
        dnw_ref[...] += jnp.sum(dh * xh * mod, axis=0, keepdims=True)
        dxh = dh * nwv * mod
        dx_ref[0] = dxo_ref[0] + rstd * (dxh - xh * jnp.mean(dxh * xh, axis=-1, keepdims=True))

    tok = lambda w: pl.BlockSpec((1, tm, w), lambda b, i: (b, i, 0))
    per_seq = pl.BlockSpec((1, 1, D), lambda b, i: (b, 0, 0))
    return pl.pallas_call(
        body, name="proj_bwd_x", grid=(B, S // tm),
        in_specs=[tok(D), per_seq, _full((1, D)), _full((D, ARR_W)), tok(RET_W), tok(512), tok(512),
                  tok(GLA_W), tok(D)],
        out_specs=[tok(D), per_seq, per_seq, _full((1, D))],
        out_shape=[jax.ShapeDtypeStruct((B, S, D), F32), jax.ShapeDtypeStruct((B, 1, D), F32),
                   jax.ShapeDtypeStruct((B, 1, D), F32), jax.ShapeDtypeStruct((1, D), F32)],
        compiler_params=_cp(("arbitrary", "arbitrary")),
    )(x, scale, nw, w_arr, d_ret, d_mla, d_mz, d_gla, dx_out)


def _proj_bwd_w(h, d_ret, d_mla, d_mz, d_gla):
    B, S, D = h.shape
    tm = min(S, 512)

    def body(h_ref, dr_ref, dm_ref, dz_ref, dg_ref, dw_ref):
        first = (pl.program_id(0) == 0) & (pl.program_id(1) == 0)

        @pl.when(first)
        def _():
            dw_ref[...] = jnp.zeros_like(dw_ref)

        hv = h_ref[0]
        tn = lambda d_ref: lax.dot_general(hv, d_ref[0], (((0,), (0,)), ((), ())), preferred_element_type=F32)
        dw_ref[:, 0:RET_W] += tn(dr_ref)
        dw_ref[:, RET_W:RET_W + 512] += tn(dm_ref)
        dw_ref[:, RET_W + 512:RET_W + MLA_W] += tn(dz_ref)
        dw_ref[:, RET_W + MLA_W:ARR_W] += tn(dg_ref)

    tok = lambda w: pl.BlockSpec((1, tm, w), lambda b, i: (b, i, 0))
    return pl.pallas_call(
        body, name="proj_bwd_w", grid=(B, S // tm),
        in_specs=[tok(D), tok(RET_W), tok(512), tok(512), tok(GLA_W)],
        out_specs=_full((D, ARR_W)), out_shape=jax.ShapeDtypeStruct((D, ARR_W), F32),
        compiler_params=_cp(("arbitrary", "arbitrary"), 56),
    )(h, d_ret, d_mla, d_mz, d_gla)


def _final_loss(x, fw, target):
    B, S, D = x.shape
    tm = min(S, 512)

    def body(x_ref, fw_ref, t_ref, dx_ref, loss_ref, dfw_ref):
        first = (pl.program_id(0) == 0) & (pl.program_id(1) == 0)

        @pl.when(first)
        def _():
            loss_ref[...] = jnp.zeros_like(loss_ref)
            dfw_ref[...] = jnp.zeros_like(dfw_ref)

        xv = x_ref[0]
        fwv = fw_ref[...]
        rstd = lax.rsqrt(jnp.mean(xv * xv, axis=-1, keepdims=True) + EPS)
        xh = xv * rstd
        err = xh * fwv - t_ref[0]
        loss_ref[...] += 0.5 * jnp.sum(jnp.mean(err * err, axis=-1, keepdims=True), axis=0, keepdims=True)
        dy = err * (1.0 / D)
        dfw_ref[...] += jnp.sum(dy * xh, axis=0, keepdims=True)
        dxh = dy * fwv
        dx_ref[0] = rstd * (dxh - xh * jnp.mean(dxh * xh, axis=-1, keepdims=True))

    tok = pl.BlockSpec((1, tm, D), lambda b, i: (b, i, 0))
    return pl.pallas_call(
        body, name="final_loss", grid=(B, S // tm),
        in_specs=[tok, _full((1, D)), tok],
        out_specs=[tok, _full((1, 1)), _full((1, D))],
        out_shape=[jax.ShapeDtypeStruct((B, S, D), F32), jax.ShapeDtypeStruct((1, 1), F32),
                   jax.ShapeDtypeStruct((1, D), F32)],
        compiler_params=_cp(("arbitrary", "arbitrary")),
    )(x, fw, target)


def _local_step(x, pos3, mod, loss_target, small, w_in_a, w_uq_a, w_ukv_a, w_out_b):
    B, S, D = x.shape
    tabs = _rope_tables(pos3)
    saved = []
    for l in range(DEPTH):
        x, s = _layer_fwd(x, tabs, mod[l], {n: a[l] for n, a in small.items() if n != "final_norm"},
                          w_in_a[l], w_uq_a[l], w_ukv_a[l], w_out_b[l])
        saved.append(s)
    dx, loss, d_fw = _final_loss(x, small["final_norm"].reshape(1, D), loss_target)
    grads = dict(final_norm=d_fw.reshape(D))
    per_layer = [None] * DEPTH
    for l in reversed(range(DEPTH)):
        dx, per_layer[l] = _layer_bwd(dx, saved[l], tabs)
    for name in per_layer[0]:
        grads[name] = jnp.stack([per_layer[l][name] for l in range(DEPTH)])
    return loss, dx, grads


def _layer_fwd(x, tabs, mod_l, small_l, w_in_a, w_uq_a, w_ukv_a, w_out_b):
    B, S, D = x.shape
    cr, sr, cm, sm = tabs
    shift = mod_l[:, 0:D].reshape(B, 1, D)
    scale = mod_l[:, D:2 * D].reshape(B, 1, D)
    gate = mod_l[:, 2 * D:3 * D].reshape(B, 1, D)
    nw = small_l["norm_w"].reshape(1, D)
    qnw = small_l["mla_q_norm"].reshape(1, 256)
    kvnw = small_l["mla_kv_norm"].reshape(1, 128)
    w_g2p = jnp.pad(small_l["gla_w_g2"], ((0, 112), (0, 0)))
    b_g2 = small_l["gla_b_g2"].reshape(1, 128)
    gnw = jnp.tile(small_l["gla_norm"], 4).reshape(1, 256)
    ret_p, mla_p, gla_p, h = _proj_fwd(x, shift, scale, nw, w_in_a)
    r_g, r_raw, r_st = _ret_fwd(ret_p, cr, sr)
    q, k, v, kt, vt = _mla_prep_fwd(mla_p, cm, sm, qnw, kvnw, w_uq_a, w_ukv_a)
    o_mla, lse = _mla_attn_fwd(q, k, vt)
    g_g, g_raw, g_st = _gla_fwd(gla_p, w_g2p, b_g2, gnw)
    x_new, y, mm = _out_fwd(x, gate, r_g, o_mla, mla_p, g_g, w_out_b)
    saved = dict(x=x, shift=shift, scale=scale, gate=gate, nw=nw, qnw=qnw, kvnw=kvnw, w_g2p=w_g2p, b_g2=b_g2,
                 gnw=gnw, ret_p=ret_p, mla_p=mla_p, gla_p=gla_p, h=h, r_g=r_g, r_raw=r_raw, r_st=r_st, q=q, k=k,
                 v=v, kt=kt, o_mla=o_mla, lse=lse, g_g=g_g, g_raw=g_raw, g_st=g_st, y=y, mm=mm,
                 w_in_a=w_in_a, w_uq_a=w_uq_a, w_ukv_a=w_ukv_a, w_out_b=w_out_b)
    return x_new, saved


def _layer_bwd(dx, s, tabs):
    B, S, D = dx.shape
    cr, sr, cm, sm = tabs
    d_r, d_mm, d_g, dw_out, d_gate = _out_bwd(dx, s["y"], s["gate"], s["r_g"], s["mm"], s["g_g"], s["w_out_b"])
    d_ret = _ret_bwd(s["ret_p"], cr, sr, s["r_raw"], s["r_st"], d_r)
    do, d_mz, dl = _mla_gate_bwd(d_mm, s["o_mla"], s["mla_p"])
    dq, dk, dv = _mla_attn_bwd(s["q"], s["k"], s["v"], s["kt"], do, s["lse"], dl)
    d_mla, dw_uq, dw_ukv, d_qnw, d_kvnw = _mla_prep_bwd(
        s["mla_p"], cm, sm, s["qnw"], s["kvnw"], s["w_uq_a"], s["w_ukv_a"], dq, dk, dv)
    d_gla, dw_g2p, db_g2, d_gnw = _gla_bwd(s["gla_p"], s["w_g2p"], s["b_g2"], s["gnw"], s["g_raw"], s["g_st"], d_g)
    dx, d_shift, d_scale, d_nw = _proj_bwd_x(s["x"], s["shift"], s["scale"], s["nw"], s["w_in_a"],
                                             d_ret, d_mla, d_mz, d_gla, dx)
    dw_in = _proj_bwd_w(s["h"], d_ret, d_mla, d_mz, d_gla)
    grads = dict(
        d_mod=jnp.concatenate([d_shift, d_scale, d_gate], axis=2).reshape(B, 3 * D),
        norm_w=d_nw.reshape(D), mla_q_norm=d_qnw.reshape(256), mla_kv_norm=d_kvnw.reshape(128),
        gla_w_g2=dw_g2p[0:16], gla_b_g2=db_g2.reshape(128), gla_norm256=d_gnw.reshape(256),
        w_in_a=dw_in, w_uq_a=dw_uq, w_ukv_a=dw_ukv, w_out=dw_out)
    return dx, grads


def _exchange(arrs, gather, name):
    n = len(arrs)
    out_shape = [jax.ShapeDtypeStruct(((N_DEV,) + a.shape) if g else a.shape, a.dtype)
                 for a, g in zip(arrs, gather)]

    def body(*refs):
        ins, outs = refs[:n], refs[n:2 * n]
        send_sems, recv_sems, local_sems = refs[2 * n:]
        ix, iy, ic = lax.axis_index("x"), lax.axis_index("y"), lax.axis_index("c")
        me = 4 * ix + 2 * iy + ic
        copies = []
        for a in range(n):
            mine = ins[a] if gather[a] else ins[a].at[me]
            loc = pltpu.make_async_copy(mine, outs[a].at[me], local_sems.at[a])
            loc.start()
            copies.append(loc)
            for d in range(1, N_DEV):
                px = 1 - ix if d & 4 else ix
                py = 1 - iy if d & 2 else iy
                pc = 1 - ic if d & 1 else ic
                src = ins[a] if gather[a] else ins[a].at[4 * px + 2 * py + pc]
                cp = pltpu.make_async_remote_copy(
                    src_ref=src, dst_ref=outs[a].at[me], send_sem=send_sems.at[a, d - 1],
                    recv_sem=recv_sems.at[a, d - 1], device_id=(px, py, pc), device_id_type=pl.DeviceIdType.MESH)
                cp.start()
                copies.append(cp)
        for cp in copies:
            cp.wait()

    any_spec = pl.BlockSpec(memory_space=pl.ANY)
    outs = pl.pallas_call(
        body, name=name, in_specs=[any_spec] * n, out_specs=[any_spec] * n, out_shape=out_shape,
        scratch_shapes=[pltpu.SemaphoreType.DMA((n, N_DEV - 1)), pltpu.SemaphoreType.DMA((n, N_DEV - 1)),
                        pltpu.SemaphoreType.DMA((n,))],
    )(*arrs)
    return list(outs)


def _peers(ix, iy, ic):
    out = []
    for d in range(1, N_DEV):
        px = 1 - ix if d & 4 else ix
        py = 1 - iy if d & 2 else iy
        pc = 1 - ic if d & 1 else ic
        out.append((d - 1, (px, py, pc), 4 * px + 2 * py + pc))
    return out


def _exchange_start(arrs, gather, name):
    n = len(arrs)
    lands = [lax.empty(((N_DEV,) + a.shape) if g else a.shape, a.dtype) for a, g in zip(arrs, gather)]

    def body(*refs):
        ins, land_refs = refs[:n], refs[n:2 * n]
        send_sems, recv_sems = refs[2 * n], refs[2 * n + 1]
        token = refs[-1]
        ix, iy, ic = lax.axis_index("x"), lax.axis_index("y"), lax.axis_index("c")
        me = 4 * ix + 2 * iy + ic
        for a in range(n):
            for k, peer, peer_idx in _peers(ix, iy, ic):
                pltpu.make_async_remote_copy(
                    src_ref=ins[a] if gather[a] else ins[a].at[peer_idx], dst_ref=land_refs[a].at[me],
                    send_sem=send_sems.at[7 * a + k], recv_sem=recv_sems.at[7 * a + k], device_id=peer,
                    device_id_type=pl.DeviceIdType.MESH).start()
        token[...] = jnp.zeros_like(token)

    hbm = pl.BlockSpec(memory_space=pltpu.HBM)
    sem = pl.BlockSpec(memory_space=pltpu.SEMAPHORE)
    held = [pltpu.with_memory_space_constraint(a, pltpu.HBM) for a in list(arrs) + lands]
    outs = pl.pallas_call(
        body, name=name,
        out_shape=(pltpu.SemaphoreType.DMA((7 * n,)), pltpu.SemaphoreType.DMA((7 * n,)),
                   *[pltpu.HBM(a.shape, a.dtype) for a in held], jax.ShapeDtypeStruct((8, 128), F32)),
        in_specs=[hbm] * (2 * n), out_specs=(sem, sem, *[hbm] * (2 * n), pl.BlockSpec(memory_space=pltpu.VMEM)),
        input_output_aliases={a: 2 + a for a in range(2 * n)},
        compiler_params=pltpu.CompilerParams(has_side_effects=pltpu.SideEffectType.DATAFLOW_SIDE_EFFECTING),
    )(*held)
    return dict(send=outs[0], recv=outs[1], srcs=list(outs[2:2 + n]), lands=list(outs[2 + n:2 + 2 * n]),
                token=outs[-1], gather=list(gather))


def _exchange_wait(flight, after, me, name):
    n = len(flight["srcs"])
    gather = flight["gather"]

    def body(*refs):
        srcs, land_refs = refs[:n], refs[n:2 * n]
        send_sems, recv_sems = refs[2 * n], refs[2 * n + 1]
        ix, iy, ic = lax.axis_index("x"), lax.axis_index("y"), lax.axis_index("c")
        mine = 4 * ix + 2 * iy + ic
        for a in range(n):
            for k, peer, peer_idx in _peers(ix, iy, ic):
                cp = pltpu.make_async_remote_copy(
                    src_ref=srcs[a] if gather[a] else srcs[a].at[peer_idx], dst_ref=land_refs[a].at[mine],
                    send_sem=send_sems.at[7 * a + k], recv_sem=recv_sems.at[7 * a + k], device_id=peer,
                    device_id_type=pl.DeviceIdType.MESH)
                cp.wait_send()
                cp.wait_recv()

    hbm = pl.BlockSpec(memory_space=pltpu.HBM)
    sem = pl.BlockSpec(memory_space=pltpu.SEMAPHORE)
    held = flight["srcs"] + flight["lands"]
    outs = pl.pallas_call(
        body, name=name, out_shape=tuple(pltpu.HBM(a.shape, a.dtype) for a in held),
        in_specs=[hbm] * (2 * n) + [sem, sem, pl.BlockSpec(memory_space=pl.ANY)], out_specs=tuple([hbm] * (2 * n)),
        input_output_aliases={a: a for a in range(2 * n)},
        compiler_params=pltpu.CompilerParams(has_side_effects=pltpu.SideEffectType.DATAFLOW_SIDE_EFFECTING),
    )(*held, flight["send"], flight["recv"], after)
    got = []
    for a in range(n):
        src, land = outs[a], outs[n + a]
        own = src if gather[a] else lax.dynamic_index_in_dim(src, me, axis=0, keepdims=False)
        got.append(lax.dynamic_update_index_in_dim(land, own, me, axis=0))
    return got


def _ada_fwd(c_all, ada_w, ada_b_cols):
    nb, D = c_all.shape
    cols = ada_w.shape[2]

    def body(c_ref, w_ref, b_ref, out_ref):
        ca = _silu(c_ref[...])
        for l in range(DEPTH):
            out_ref[l] = _mm(ca, w_ref[l]) + b_ref[l:l + 1, :]

    return pl.pallas_call(
        body, name="ada_fwd", out_shape=jax.ShapeDtypeStruct((DEPTH, nb, cols), F32),
        in_specs=[pl.BlockSpec(memory_space=pltpu.VMEM)] * 3, out_specs=pl.BlockSpec(memory_space=pltpu.VMEM),
        compiler_params=pltpu.CompilerParams(vmem_limit_bytes=32 * VMEM_MB),
    )(c_all, ada_w, ada_b_cols)


def _ada_bwd(c_all, d_mod_cols):
    nb, D = c_all.shape
    cols = d_mod_cols.shape[2]

    def body(c_ref, dm_ref, out_ref):
        ca = _silu(c_ref[...])
        for l in range(DEPTH):
            out_ref[l] = _mm_tn(ca, dm_ref[l])

    return pl.pallas_call(
        body, name="ada_bwd", out_shape=jax.ShapeDtypeStruct((DEPTH, D, cols), F32),
        in_specs=[pl.BlockSpec(memory_space=pltpu.VMEM)] * 2, out_specs=pl.BlockSpec(memory_space=pltpu.VMEM),
        compiler_params=pltpu.CompilerParams(vmem_limit_bytes=32 * VMEM_MB),
    )(c_all, d_mod_cols)


def _sum_adamw(parts, w, m, v, name):
    P, R, C = parts.shape
    tr = 256 if (R % 256 == 0 and R > 256) else R

    def body(p_ref, w_ref, m_ref, v_ref, g_ref, d_ref, nm_ref, nv_ref):
        g = p_ref[0].astype(F32)
        for k in range(1, P):
            g = g + p_ref[k].astype(F32)
        g_ref[...] = g
        nm = ADAM_B1 * m_ref[...] + (1.0 - ADAM_B1) * g
        nv = ADAM_B2 * v_ref[...] + (1.0 - ADAM_B2) * (g * g)
        nm_ref[...] = nm
        nv_ref[...] = nv
        m_hat = nm / (1.0 - ADAM_B1 ** ADAM_STEP)
        v_hat = nv / (1.0 - ADAM_B2 ** ADAM_STEP)
        d_ref[...] = -ADAM_LR * (m_hat / (jnp.sqrt(v_hat) + ADAM_EPS) + ADAM_WD * w_ref[...])

    blk = pl.BlockSpec((tr, C), lambda i: (i, 0))
    shp = jax.ShapeDtypeStruct((R, C), F32)
    return pl.pallas_call(
        body, name=name, grid=(R // tr,),
        in_specs=[pl.BlockSpec((P, tr, C), lambda i: (0, i, 0)), blk, blk, blk],
        out_specs=[blk, blk, blk, blk], out_shape=[shp, shp, shp, shp],
        compiler_params=_cp(("parallel",)),
    )(parts, w, m, v)


SMALL = [("norm_w", DEPTH * 1024), ("mla_q_norm", DEPTH * 256), ("mla_kv_norm", DEPTH * 128),
         ("gla_w_g2", DEPTH * 16 * 128), ("gla_b_g2", DEPTH * 128), ("gla_norm", DEPTH * 64), ("final_norm", 1024)]
SMALL_ROWS = 72


def _pack_small(first_row, vals):
    flat = [first_row.reshape(128)] + [vals[n].reshape(-1) for n, _ in SMALL]
    used = 128 + sum(s for _, s in SMALL)
    flat.append(jnp.zeros((SMALL_ROWS * 128 - used,), F32))
    return jnp.concatenate(flat).reshape(SMALL_ROWS, 128)


def _unpack_small(packed, shapes):
    flat = packed.reshape(-1)
    out, off = {}, 128
    for n, s in SMALL:
        out[n] = flat[off:off + s].reshape(shapes[n])
        off += s
    return out


WEIGHTS = ["norm_w", "ada_w", "ada_b", "w_in", "mla_q_norm", "w_uq", "mla_kv_norm", "w_ukv", "gla_w_g2",
           "gla_b_g2", "gla_norm", "w_out", "final_norm"]


def kernel(x, c, positions, norm_w, ada_w, ada_b, w_in, mla_q_norm, w_uq, mla_kv_norm, w_ukv, gla_w_g2, gla_b_g2, gla_norm, w_out, final_norm, loss_target, m_norm_w, m_ada_w, m_ada_b, m_w_in, m_mla_q_norm, m_w_uq, m_mla_kv_norm, m_w_ukv, m_gla_w_g2, m_gla_b_g2, m_gla_norm, m_w_out, m_final_norm, v_norm_w, v_ada_w, v_ada_b, v_w_in, v_mla_q_norm, v_w_uq, v_mla_kv_norm, v_w_ukv, v_gla_w_g2, v_gla_b_g2, v_gla_norm, v_w_out, v_final_norm):
    w = dict(norm_w=norm_w, ada_w=ada_w, ada_b=ada_b, w_in=w_in, mla_q_norm=mla_q_norm, w_uq=w_uq,
             mla_kv_norm=mla_kv_norm, w_ukv=w_ukv, gla_w_g2=gla_w_g2, gla_b_g2=gla_b_g2, gla_norm=gla_norm,
             w_out=w_out, final_norm=final_norm)
    m = dict(norm_w=m_norm_w, ada_w=m_ada_w, ada_b=m_ada_b, w_in=m_w_in, mla_q_norm=m_mla_q_norm, w_uq=m_w_uq,
             mla_kv_norm=m_mla_kv_norm, w_ukv=m_w_ukv, gla_w_g2=m_gla_w_g2, gla_b_g2=m_gla_b_g2,
             gla_norm=m_gla_norm, w_out=m_w_out, final_norm=m_final_norm)
    v = dict(norm_w=v_norm_w, ada_w=v_ada_w, ada_b=v_ada_b, w_in=v_w_in, mla_q_norm=v_mla_q_norm, w_uq=v_w_uq,
             mla_kv_norm=v_mla_kv_norm, w_ukv=v_w_ukv, gla_w_g2=v_gla_w_g2, gla_b_g2=v_gla_b_g2,
             gla_norm=v_gla_norm, w_out=v_w_out, final_norm=v_final_norm)
    B, S, D = x.shape
    me = 4 * lax.axis_index("x") + 2 * lax.axis_index("y") + lax.axis_index("c")
    ada_cols = ada_w.shape[2]
    cast = lambda a: a.astype(_MXU)

    sharded = ["w_in", "w_uq", "w_ukv", "w_out"]

    def whole(blocks):
        cols = lambda a: jnp.transpose(a, (1, 0, 2)).reshape(a.shape[1], -1)
        return (_arrange_w_in(cols(blocks[0])), _arrange_w_uq(cols(blocks[1])), _arrange_w_ukv(cols(blocks[2])),
                blocks[3].reshape(D, D))

    def to_blocks(g_l):
        cols = lambda a: jnp.transpose(a.reshape(a.shape[0], N_DEV, -1), (1, 0, 2)).astype(jnp.bfloat16)
        return [cols(_unarrange_w_in(g_l["w_in_a"])), cols(_unarrange_w_uq(g_l["w_uq_a"])),
                cols(_unarrange_w_ukv(g_l["w_ukv_a"])), g_l["w_out"].reshape(N_DEV, D // N_DEV, D).astype(jnp.bfloat16)]

    got0 = _exchange([c] + [cast(w[n][0]) for n in sharded], [True] * 5, "gather_layer0")
    c_all = got0[0].reshape(N_DEV * B, D)
    flight_w = _exchange_start([cast(w[n][1]) for n in sharded], [True] * 4, "gather_start_layer1")

    ada_b_cols = lax.dynamic_slice(ada_b, (0, me * ada_cols), (DEPTH, ada_cols))
    mod_cols = _ada_fwd(c_all, ada_w, ada_b_cols)
    mod_send = jnp.transpose(mod_cols.reshape(DEPTH, N_DEV, B, ada_cols), (1, 0, 2, 3))
    (mod_recv,) = _exchange([mod_send], [False], "scatter_mod")
    mod = jnp.transpose(mod_recv, (1, 2, 0, 3)).reshape(DEPTH, B, 3 * D)

    small_w = {n: w[n] for n, _ in SMALL}
    layer_small = lambda l: {n: a[l] for n, a in small_w.items() if n != "final_norm"}
    tabs = _rope_tables(positions.reshape(B, S, 1))
    x1, saved0 = _layer_fwd(x, tabs, mod[0] + flight_w["token"][0, 0], layer_small(0), *whole(got0[1:]))
    got1 = _exchange_wait(flight_w, x1, me, "gather_wait_layer1")
    x2, saved1 = _layer_fwd(x1, tabs, mod[1], layer_small(1), *whole(got1))
    dx, loss, d_fw = _final_loss(x2, final_norm.reshape(1, D), loss_target)

    dx, g1 = _layer_bwd(dx, saved1, tabs)
    flight_g = _exchange_start(to_blocks(g1), [False] * 4, "grads_start_layer1")
    saved0 = dict(saved0, gate=saved0["gate"] + flight_g["token"][0, 0])
    grad_x, g0 = _layer_bwd(dx, saved0, tabs)
    parts1 = _exchange_wait(flight_g, grad_x, me, "grads_wait_layer1")

    both = lambda n: jnp.stack([g0[n], g1[n]])
    d_mod = both("d_mod")
    part = dict(norm_w=both("norm_w"), mla_q_norm=both("mla_q_norm"), mla_kv_norm=both("mla_kv_norm"),
                gla_w_g2=both("gla_w_g2"), gla_b_g2=both("gla_b_g2"), gla_norm=both("gla_norm256")[:, 0:64],
                final_norm=d_fw.reshape(D))
    small_part = _pack_small(jnp.pad(loss.reshape(1), (0, 127)), part)
    last = _exchange([d_mod, small_part] + to_blocks(g0), [True, True, False, False, False, False], "exchange_layer0")
    d_mod_g, small_g, parts0 = last[0], last[1], last[2:]

    d_mod_all = jnp.transpose(d_mod_g, (1, 0, 2, 3)).reshape(DEPTH, N_DEV * B, 3 * D)
    d_mod_cols = lax.dynamic_slice(d_mod_all, (0, 0, me * ada_cols), (DEPTH, N_DEV * B, ada_cols))
    g_ada_w = _ada_bwd(c_all, d_mod_cols)

    res = {}

    def update(name, parts2d):
        shp = w[name].shape
        two = lambda a: a.reshape(parts2d.shape[1:])
        out = _sum_adamw(parts2d, two(w[name]), two(m[name]), two(v[name]), "adamw_" + name)
        res[name] = [o.reshape(shp) for o in out]

    update("ada_w", g_ada_w.reshape(1, DEPTH * D, ada_cols))
    update("ada_b", jnp.transpose(d_mod_g, (0, 2, 1, 3)).reshape(N_DEV * B, DEPTH * 3 * D // 128, 128))
    for a, name in enumerate(sharded):
        update(name, jnp.concatenate([parts0[a], parts1[a]], axis=1))
    zero_row = jnp.zeros((128,), F32)
    small_out = _sum_adamw(small_g, _pack_small(zero_row, small_w), _pack_small(zero_row, {n: m[n] for n, _ in SMALL}),
                           _pack_small(zero_row, {n: v[n] for n, _ in SMALL}), "adamw_small")
    shapes = {n: w[n].shape for n, _ in SMALL}
    unpacked = [_unpack_small(o, shapes) for o in small_out]
    for n, _ in SMALL:
        res[n] = [u[n] for u in unpacked]
    loss_out = small_out[0][0, 0]
    return (loss_out, grad_x, *[res[n][0] for n in WEIGHTS], *[res[n][1] for n in WEIGHTS],
            *[res[n][2] for n in WEIGHTS], *[res[n][3] for n in WEIGHTS])
```

```python
import functools
import math

import numpy as np
import jax
import jax.numpy as jnp
from jax import lax
from jax.experimental import pallas as pl
from jax.experimental.pallas import tpu as pltpu

F32 = jnp.float32
_MXU = jnp.bfloat16

D_MODEL = 1024
DEPTH = 2
CHUNK = 64
EPS = 1e-6
ROPE_THETA = 10000.0
N_DEV = 8

MLA_SCALE = 96.0 ** -0.5
RET_KSCALE = 64.0 ** -0.5
GLA_KSCALE = 32.0 ** -0.5
GLA_TAU = 16.0

ADAM_LR = 0.001
ADAM_B1 = 0.9
ADAM_B2 = 0.999
ADAM_EPS = 1e-08
ADAM_WD = 0.01
ADAM_STEP = 10

RET_W, MLA_W, GLA_W = 1024, 1024, 896
ARR_W = RET_W + MLA_W + GLA_W
VMEM_MB = 1024 * 1024


def _cp(sem, vmem_mb=48):
    return pltpu.CompilerParams(dimension_semantics=sem, vmem_limit_bytes=vmem_mb * VMEM_MB)


def _mm(a, b):
    return jnp.dot(a.astype(_MXU), b.astype(_MXU), preferred_element_type=F32)


def _mm_nt(a, b):
    return lax.dot_general(a.astype(_MXU), b.astype(_MXU), (((1,), (1,)), ((), ())),
                           preferred_element_type=F32)


def _mm_tn(a, b):
    return lax.dot_general(a.astype(_MXU), b.astype(_MXU), (((0,), (0,)), ((), ())),
                           preferred_element_type=F32)


def _mm_f32(a, b):
    return jnp.dot(a, b, precision=lax.Precision.HIGHEST, preferred_element_type=F32)


def _sig(z):
    return 1.0 / (1.0 + jnp.exp(-z))


def _silu(z):
    return z * _sig(z)


def _dsilu(z):
    s = _sig(z)
    return s * (1.0 + z * (1.0 - s))


def _full(shape):
    nd = len(shape)
    return pl.BlockSpec(shape, lambda *_: (0,) * nd)


def _qk_perm(blk):
    r = blk.shape[0]
    return jnp.transpose(blk.reshape(r, 4, 2, 32), (0, 2, 1, 3)).reshape(r, 256)


def _qk_unperm(blk):
    r = blk.shape[0]
    return jnp.transpose(blk.reshape(r, 2, 4, 32), (0, 2, 1, 3)).reshape(r, 256)


def _arrange_w_in(w):
    z = lambda n: jnp.zeros((w.shape[0], n), w.dtype)
    ret = [_qk_perm(w[:, 0:256]), _qk_perm(w[:, 256:512]), w[:, 512:768], w[:, 768:1024]]
    mla = [w[:, 1024:1280], w[:, 1280:1408], z(64), w[:, 1408:1440], z(32), w[:, 1440:1952]]
    gla = [w[:, 1952:2080], w[:, 2080:2208], w[:, 2208:2464], w[:, 2464:2480], z(112), w[:, 2480:2736]]
    return jnp.concatenate(ret + mla + gla, axis=1)


def _unarrange_w_in(a):
    m, g = RET_W, RET_W + MLA_W
    parts = [_qk_unperm(a[:, 0:256]), _qk_unperm(a[:, 256:512]), a[:, 512:1024],
             a[:, m:m + 384], a[:, m + 448:m + 480], a[:, m + 512:m + 1024],
             a[:, g:g + 528], a[:, g + 640:g + 896]]
    return jnp.concatenate(parts, axis=1)


def _arrange_w_uq(w):
    return jnp.pad(w.reshape(256, 8, 96), ((0, 0), (0, 0), (0, 32))).reshape(256, 1024)


def _unarrange_w_uq(a):
    return a.reshape(256, 8, 128)[:, :, :96].reshape(256, 768)


def _arrange_w_ukv(w):
    r = w.reshape(128, 8, 128)
    k = jnp.pad(r[:, :, :64], ((0, 0), (0, 0), (0, 64))).reshape(128, 1024)
    return jnp.concatenate([k, r[:, :, 64:].reshape(128, 512)], axis=1)


def _unarrange_w_ukv(a):
    k = a[:, :1024].reshape(128, 8, 128)[:, :, :64]
    v = a[:, 1024:].reshape(128, 8, 64)
    return jnp.concatenate([k, v], axis=2).reshape(128, 1024)


def _rope_tables(pos3):
    B, S, _ = pos3.shape
    ts = min(S, 512)
    inv32 = (np.float32(ROPE_THETA) ** (-(np.arange(32, dtype=np.float32) / 32))).astype(np.float32)
    inv16 = (np.float32(ROPE_THETA) ** (-(np.arange(16, dtype=np.float32) / 16))).astype(np.float32)
    inv_r = np.tile(inv32, 4)[None, :]
    inv_m = np.zeros((1, 128), np.float32)
    inv_m[0, 64:80] = inv16
    inv_m[0, 80:96] = inv16

    def body(pos_ref, ir_ref, im_ref, cr, sr, cm, sm):
        p = pos_ref[0].astype(F32)
        ar = p * ir_ref[...]
        cr[0] = jnp.cos(ar)
        sr[0] = jnp.sin(ar)
        am = p * im_ref[...]
        cm[0] = jnp.cos(am)
        sm[0] = jnp.sin(am)

    tab = jax.ShapeDtypeStruct((B, S, 128), F32)
    blk = pl.BlockSpec((1, ts, 128), lambda b, i: (b, i, 0))
    return pl.pallas_call(
        body, name="rope_tables", grid=(B, S // ts),
        in_specs=[pl.BlockSpec((1, ts, 1), lambda b, i: (b, i, 0)), _full((1, 128)), _full((1, 128))],
        out_specs=[blk, blk, blk, blk], out_shape=[tab, tab, tab, tab],
        compiler_params=_cp(("parallel", "parallel")),
    )(pos3, jnp.asarray(inv_r), jnp.asarray(inv_m))


def _rope128(x, cos, sin):
    lane = lax.broadcasted_iota(jnp.int32, (1, 128), 1)
    rp = pltpu.roll(x, 16, 1)
    rm = pltpu.roll(x, 112, 1)
    return x * cos + jnp.where(lane < 80, -rm, rp) * sin


def _rope128_t(d, cos, sin):
    lane = lax.broadcasted_iota(jnp.int32, (1, 128), 1)
    y = d * sin
    yp = pltpu.roll(y, 16, 1)
    ym = pltpu.roll(y, 112, 1)
    return d * cos + jnp.where(lane < 64, 0.0, jnp.where(lane < 80, ym, jnp.where(lane < 96, -yp, 0.0)))


def _proj_fwd(x, shift, scale, nw, w_arr):
    B, S, D = x.shape
    tm = min(S, 512)

    def body(x_ref, sh_ref, sc_ref, nw_ref, w_ref, ret_ref, mla_ref, gla_ref, h_ref):
        xv = x_ref[0]
        rstd = lax.rsqrt(jnp.mean(xv * xv, axis=-1, keepdims=True) + EPS)
        h = (xv * rstd * nw_ref[...]) * (1.0 + sc_ref[0]) + sh_ref[0]
        hb = h.astype(_MXU)
        h_ref[0] = hb
        ret_ref[0] = jnp.dot(hb, w_ref[:, 0:RET_W], preferred_element_type=F32)
        mla_ref[0] = jnp.dot(hb, w_ref[:, RET_W:RET_W + MLA_W], preferred_element_type=F32)
        gla_ref[0] = jnp.dot(hb, w_ref[:, RET_W + MLA_W:ARR_W], preferred_element_type=F32)

    tok = lambda w: pl.BlockSpec((1, tm, w), lambda b, i: (b, i, 0))
    per_seq = pl.BlockSpec((1, 1, D), lambda b, i: (b, 0, 0))
    return pl.pallas_call(
        body, name="proj_fwd", grid=(B, S // tm),
        in_specs=[tok(D), per_seq, per_seq, _full((1, D)), _full((D, ARR_W))],
        out_specs=[tok(RET_W), tok(MLA_W), tok(GLA_W), tok(D)],
        out_shape=[jax.ShapeDtypeStruct((B, S, RET_W), F32), jax.ShapeDtypeStruct((B, S, MLA_W), F32),
                   jax.ShapeDtypeStruct((B, S, GLA_W), F32), jax.ShapeDtypeStruct((B, S, D), _MXU)],
        compiler_params=_cp(("parallel", "parallel")),
    )(x, shift, scale, nw, w_arr)


RET_L = 256


def _ret_consts(L):
    lg = np.log1p(-np.exp2(-5.0 - np.arange(4, dtype=np.float32))).astype(np.float32)
    i = np.arange(L)
    ci = i // CHUNK
    diff = (i[:, None] - i[None, :]).astype(np.float32)
    same = ci[:, None] == ci[None, :]
    past = ci[None, :] < ci[:, None]
    expo = np.where(same, np.abs(diff), np.where(past, diff, 0.0)).astype(np.float32)
    dec = np.where((same | past)[None], np.exp(lg[:, None, None] * expo[None]), 0.0).astype(np.float32)
    head = (np.arange(256) % 128) // 32
    qw = np.exp((i + 1.0)[:, None] * lg[head][None, :]).astype(np.float32)
    kw = np.exp((L - 1.0 - i)[:, None] * lg[head][None, :]).astype(np.float32)
    a_row = np.exp(np.float32(L) * lg[head])[None, :].astype(np.float32)
    return [jnp.asarray(t) for t in (dec.reshape(4 * L, L), qw, kw, a_row)]


def _ret_masks():
    lane = lax.broadcasted_iota(jnp.int32, (1, 256), 1)
    mh = [((lane % 128) // 32) == h for h in range(4)]
    mv = [(lane // 64) == h for h in range(4)]
    vi = lax.broadcasted_iota(jnp.int32, (256, 256), 0)
    ki = lax.broadcasted_iota(jnp.int32, (256, 256), 1)
    bd = (vi // 64) == ((ki % 128) // 32)
    return mh, mv, bd


def _ret_rope(p, cs, sn):
    q1, q2, k1, k2 = p[:, 0:128], p[:, 128:256], p[:, 256:384], p[:, 384:512]
    qr = jnp.concatenate([q1 * cs - q2 * sn, q2 * cs + q1 * sn], axis=1)
    kr = jnp.concatenate([k1 * cs - k2 * sn, k2 * cs + k1 * sn], axis=1) * RET_KSCALE
    return qr, kr


def _head_mean(x, mv, width):
    out = jnp.zeros_like(x)
    for m in mv:
        s = jnp.sum(jnp.where(m, x, 0.0), axis=-1, keepdims=True) * (1.0 / width)
        out = jnp.where(m, s, out)
    return out


def _stack_heads(x, masks):
    return jnp.concatenate([jnp.where(m, x, 0.0) for m in masks], axis=0)


def _fold_heads(xs, masks, L):
    out = jnp.where(masks[0], xs[0:L], 0.0)
    for h in range(1, 4):
        out = out + jnp.where(masks[h], xs[h * L:(h + 1) * L], 0.0)
    return out


def _ret_fwd(ret_p, cos, sin):
    B, S, _ = ret_p.shape
    L = min(RET_L, S)
    NB = S // L
    consts = _ret_consts(L)

    def body(p_ref, c_ref, s_ref, ds_ref, qw_ref, kw_ref, a_ref, out_ref, raw_ref, st_ref, st_sc):
        @pl.when(pl.program_id(1) == 0)
        def _():
            st_sc[...] = jnp.zeros_like(st_sc)

        mh, mv, bd = _ret_masks()
        p = p_ref[0]
        qr, kr = _ret_rope(p, c_ref[0], s_ref[0])
        v = p[:, 512:768]
        z = p[:, 768:1024]
        a_s = _mm_nt(_stack_heads(qr, mh), kr) * ds_ref[...]
        intra = _fold_heads(_mm(a_s, v), mv, L)
        st = st_sc[...]
        st_ref[0, 0] = st
        r = intra + _mm_nt(qr * qw_ref[...], st)
        raw_ref[0] = r
        st_sc[...] = st * a_ref[...] + jnp.where(bd, _mm_tn(v, kr * kw_ref[...]), 0.0)
        rstd = lax.rsqrt(_head_mean(r * r, mv, 64.0) + EPS)
        out_ref[0] = (r * rstd * _silu(z)).astype(_MXU)

    tok = lambda w: pl.BlockSpec((1, L, w), lambda b, n: (b, n, 0))
    return pl.pallas_call(
        body, name="ret_fwd", grid=(B, NB),
        in_specs=[tok(RET_W), tok(128), tok(128), _full((4 * L, L)), _full((L, 256)), _full((L, 256)),
                  _full((1, 256))],
        out_specs=[tok(256), tok(256), pl.BlockSpec((1, 1, 256, 256), lambda b, n: (b, n, 0, 0))],
        out_shape=[jax.ShapeDtypeStruct((B, S, 256), _MXU), jax.ShapeDtypeStruct((B, S, 256), F32),
                   jax.ShapeDtypeStruct((B, NB, 256, 256), F32)],
        scratch_shapes=[pltpu.VMEM((256, 256), F32)],
        compiler_params=_cp(("parallel", "arbitrary")),
    )(ret_p, cos, sin, *consts)


def _ret_bwd(ret_p, cos, sin, raw, states, d_mix):
    B, S, _ = ret_p.shape
    L = min(RET_L, S)
    NB = S // L
    consts = _ret_consts(L)

    def body(p_ref, c_ref, s_ref, raw_ref, st_ref, dm_ref, ds_ref, qw_ref, kw_ref, a_ref, dp_ref, dst_sc):
        @pl.when(pl.program_id(1) == 0)
        def _():
            dst_sc[...] = jnp.zeros_like(dst_sc)

        mh, mv, bd = _ret_masks()
        p = p_ref[0]
        cs, sn = c_ref[0], s_ref[0]
        qr, kr = _ret_rope(p, cs, sn)
        v = p[:, 512:768]
        z = p[:, 768:1024]
        qs = _stack_heads(qr, mh)
        dec = ds_ref[...]
        a_s = _mm_nt(qs, kr) * dec
        r = raw_ref[0]
        rstd = lax.rsqrt(_head_mean(r * r, mv, 64.0) + EPS)
        rn = r * rstd
        dm = dm_ref[0]
        d_rn = dm * _silu(z)
        dz = dm * rn * _dsilu(z)
        dr = rstd * (d_rn - rn * _head_mean(d_rn * rn, mv, 64.0))
        do_s = _stack_heads(dr, mv)
        da_s = _mm_nt(do_s, v) * dec
        dv = _mm_tn(a_s, do_s)
        dqr = _fold_heads(_mm(da_s, kr), mh, L)
        dkr = _mm_tn(da_s, qs)
        st = st_ref[0, 0]
        qw, kw = qw_ref[...], kw_ref[...]
        dqr = dqr + _mm(dr, st) * qw
        dst_next = dst_sc[...]
        g = jnp.where(bd, dst_next, 0.0)
        kk = kr * kw
        dv = dv + _mm_nt(kk, g)
        dkr = dkr + _mm(v, g) * kw
        dst_sc[...] = dst_next * a_ref[...] + jnp.where(bd, _mm_tn(dr, qr * qw), 0.0)
        dkr = dkr * RET_KSCALE
        dq1, dq2 = dqr[:, 0:128], dqr[:, 128:256]
        dk1, dk2 = dkr[:, 0:128], dkr[:, 128:256]
        dp_ref[0] = jnp.concatenate(
            [dq1 * cs + dq2 * sn, dq2 * cs - dq1 * sn, dk1 * cs + dk2 * sn, dk2 * cs - dk1 * sn, dv, dz],
            axis=1).astype(_MXU)

    tok = lambda w: pl.BlockSpec((1, L, w), lambda b, i: (b, NB - 1 - i, 0))
    return pl.pallas_call(
        body, name="ret_bwd", grid=(B, NB),
        in_specs=[tok(RET_W), tok(128), tok(128), tok(256),
                  pl.BlockSpec((1, 1, 256, 256), lambda b, i: (b, NB - 1 - i, 0, 0)), tok(256),
                  _full((4 * L, L)), _full((L, 256)), _full((L, 256)), _full((1, 256))],
        out_specs=tok(RET_W), out_shape=jax.ShapeDtypeStruct((B, S, RET_W), _MXU),
        scratch_shapes=[pltpu.VMEM((256, 256), F32)],
        compiler_params=_cp(("parallel", "arbitrary")),
    )(ret_p, cos, sin, raw, states, d_mix, *consts)


def _gla_masks():
    C = CHUNK
    lk = lax.broadcasted_iota(jnp.int32, (1, 128), 1)
    lv = lax.broadcasted_iota(jnp.int32, (1, 256), 1)
    mk = [(lk // 32) == h for h in range(4)]
    mv = [(lv // 64) == h for h in range(4)]
    vi = lax.broadcasted_iota(jnp.int32, (256, 128), 0)
    ki = lax.broadcasted_iota(jnp.int32, (256, 128), 1)
    bd = (vi // 64) == (ki // 32)
    ri = lax.broadcasted_iota(jnp.int32, (4 * C, C), 0) % C
    cj = lax.broadcasted_iota(jnp.int32, (4 * C, C), 1)
    lower = ri >= cj
    ti = lax.broadcasted_iota(jnp.int32, (C, C), 0)
    tj = lax.broadcasted_iota(jnp.int32, (C, C), 1)
    ltri = jnp.where(ti >= tj, 1.0, 0.0).astype(F32)
    utri = jnp.where(ti <= tj, 1.0, 0.0).astype(F32)
    return mk, mv, bd, lower, ltri, utri


def _log_sigmoid(x):
    return jnp.minimum(x, 0.0) - jnp.log(1.0 + jnp.exp(-jnp.abs(x)))


GLA_G = 8


def _gla_fwd(gla_p, w_g2p, b_g2, gnw):
    B, S, _ = gla_p.shape
    C = CHUNK
    NC = S // C
    G = min(GLA_G, NC)
    NG = NC // G

    def body(p_ref, w_ref, b_ref, gn_ref, out_ref, raw_ref, st_ref, st_sc):
        @pl.when(pl.program_id(1) == 0)
        def _():
            st_sc[...] = jnp.zeros_like(st_sc)

        mk, mv, bd, lower, ltri, _ = _gla_masks()
        cs = range(G)
        rows = [slice(c * C, (c + 1) * C) for c in cs]
        ps = [p_ref[0, rows[c], :] for c in cs]
        pre = [_mm(ps[c][:, 512:640], w_ref[...]) + b_ref[...] for c in cs]
        cum = [_mm_f32(ltri, _log_sigmoid(pre[c]) * (1.0 / GLA_TAU)) for c in cs]
        past, fut, upd, q_pos, a_row = [], [], [], [], []
        for c in cs:
            q = ps[c][:, 0:128]
            k = ps[c][:, 128:256] * GLA_KSCALE
            last = cum[c][C - 1:C, :]
            e_pos = jnp.exp(cum[c])
            e_neg = jnp.exp(-cum[c])
            q_pos.append(q * e_pos)
            a_row.append(jnp.exp(last))
            past.append(_mm_nt(_stack_heads(q_pos[c], mk), k * e_neg))
            fut.append(_mm_nt(_stack_heads(q * e_neg, mk), k * e_pos))
            upd.append(_mm_tn(ps[c][:, 256:512], k * jnp.exp(last - cum[c])))
        o_s = [_mm(jnp.where(lower, past[c], fut[c]), ps[c][:, 256:512]) for c in cs]
        st = st_sc[...]
        inter = []
        for c in cs:
            st_ref[0, c] = st
            inter.append(_mm_nt(q_pos[c], st))
            st = st * a_row[c] + jnp.where(bd, upd[c], 0.0)
        st_sc[...] = st
        for c in cs:
            g = _fold_heads(o_s[c], mv, C) + inter[c]
            raw_ref[0, rows[c], :] = g
            rstd = lax.rsqrt(_head_mean(g * g, mv, 64.0) + EPS)
            out_ref[0, rows[c], :] = (g * rstd * gn_ref[...] * _silu(ps[c][:, 640:896])).astype(_MXU)

    tok = lambda w: pl.BlockSpec((1, G * C, w), lambda b, n: (b, n, 0))
    return pl.pallas_call(
        body, name="gla_fwd", grid=(B, NG),
        in_specs=[tok(GLA_W), _full((128, 128)), _full((1, 128)), _full((1, 256))],
        out_specs=[tok(256), tok(256), pl.BlockSpec((1, G, 256, 128), lambda b, n: (b, n, 0, 0))],
        out_shape=[jax.ShapeDtypeStruct((B, S, 256), _MXU), jax.ShapeDtypeStruct((B, S, 256), F32),
                   jax.ShapeDtypeStruct((B, NC, 256, 128), F32)],
        scratch_shapes=[pltpu.VMEM((256, 128), F32)],
        compiler_params=_cp(("parallel", "arbitrary")),
    )(gla_p, w_g2p, b_g2, gnw)


def _gla_bwd(gla_p, w_g2p, b_g2, gnw, raw, states, d_mix):
    B, S, _ = gla_p.shape
    C = CHUNK
    NC = S // C
    G = min(GLA_G, NC)
    NG = NC // G

    def body(p_ref, w_ref, b_ref, gn_ref, raw_ref, st_ref, dm_ref, dp_ref, dw_ref, db_ref, dgn_ref, dst_sc):
        first = (pl.program_id(0) == 0) & (pl.program_id(1) == 0)

        @pl.when(first)
        def _():
            dw_ref[...] = jnp.zeros_like(dw_ref)
            db_ref[...] = jnp.zeros_like(db_ref)
            dgn_ref[...] = jnp.zeros_like(dgn_ref)

        @pl.when(pl.program_id(1) == 0)
        def _():
            dst_sc[...] = jnp.zeros_like(dst_sc)

        mk, mv, bd, lower, ltri, utri = _gla_masks()
        gn = gn_ref[...]
        cs = range(G)
        rows = [slice(c * C, (c + 1) * C) for c in cs]
        ps = [p_ref[0, rows[c], :] for c in cs]
        vs = [ps[c][:, 256:512] for c in cs]
        pre = [_mm(ps[c][:, 512:640], w_ref[...]) + b_ref[...] for c in cs]
        cum = [_mm_f32(ltri, _log_sigmoid(pre[c]) * (1.0 / GLA_TAU)) for c in cs]
        dg, dz, dgn_acc = [], [], jnp.zeros((1, 256), F32)
        for c in cs:
            g = raw_ref[0, rows[c], :]
            z = ps[c][:, 640:896]
            rstd = lax.rsqrt(_head_mean(g * g, mv, 64.0) + EPS)
            gh = g * rstd
            dm = dm_ref[0, rows[c], :]
            d_gn = dm * _silu(z)
            dz.append(dm * gh * gn * _dsilu(z))
            dgn_acc = dgn_acc + jnp.sum(d_gn * gh, axis=0, keepdims=True)
            d_gh = d_gn * gn
            dg.append(rstd * (d_gh - gh * _head_mean(d_gh * gh, mv, 64.0)))
        do_s = [_stack_heads(dg[c], mv) for c in cs]
        dattn = [_mm_nt(do_s[c], vs[c]) for c in cs]
        ks, e_pos, e_neg, q_pos, q_neg, k_pos, k_neg, qp_s, qn_s, past, fut, a_row, w_dec, kd = ([] for _ in range(14))
        for c in cs:
            q = ps[c][:, 0:128]
            k = ps[c][:, 128:256] * GLA_KSCALE
            last = cum[c][C - 1:C, :]
            ep, en = jnp.exp(cum[c]), jnp.exp(-cum[c])
            ks.append(k), e_pos.append(ep), e_neg.append(en)
            q_pos.append(q * ep), q_neg.append(q * en), k_pos.append(k * ep), k_neg.append(k * en)
            qp_s.append(_stack_heads(q_pos[c], mk)), qn_s.append(_stack_heads(q_neg[c], mk))
            past.append(_mm_nt(qp_s[c], k_neg[c]))
            fut.append(_mm_nt(qn_s[c], k_pos[c]))
            a_row.append(jnp.exp(last))
            w_dec.append(jnp.exp(last - cum[c]))
            kd.append(k * w_dec[c])
        sts = [st_ref[0, c] for c in cs]
        dq_st = [_mm(dg[c], sts[c]) for c in cs]
        dst_in = [_mm_tn(dg[c], q_pos[c]) for c in cs]
        dv, dq_pos, dk_neg, dq_neg, dk_pos = [], [], [], [], []
        for c in cs:
            attn = jnp.where(lower, past[c], fut[c])
            dpast = jnp.where(lower, dattn[c], 0.0)
            dfut = jnp.where(lower, 0.0, dattn[c])
            dv.append(_mm_tn(attn, do_s[c]))
            dq_pos.append(_fold_heads(_mm(dpast, k_neg[c]), mk, C) + dq_st[c])
            dk_neg.append(_mm_tn(dpast, qp_s[c]))
            dq_neg.append(_fold_heads(_mm(dfut, k_pos[c]), mk, C))
            dk_pos.append(_mm_tn(dfut, qn_s[c]))
        dst_next = dst_sc[...]
        d_a, d_kd = [None] * G, [None] * G
        for c in reversed(cs):
            d_a[c] = jnp.sum(dst_next * sts[c], axis=0, keepdims=True)
            gmat = jnp.where(bd, dst_next, 0.0)
            d_kd[c] = _mm(vs[c], gmat)
            dv[c] = dv[c] + _mm_nt(kd[c], gmat)
            dst_next = dst_next * a_row[c] + jnp.where(bd, dst_in[c], 0.0)
        dst_sc[...] = dst_next
        row = lax.broadcasted_iota(jnp.int32, (C, 128), 0)
        d_la, dk, dq = [], [], []
        for c in cs:
            t = d_kd[c] * kd[c]
            dk.append(d_kd[c] * w_dec[c] + dk_neg[c] * e_neg[c] + dk_pos[c] * e_pos[c])
            dq.append(dq_pos[c] * e_pos[c] + dq_neg[c] * e_neg[c])
            d_last = jnp.sum(t, axis=0, keepdims=True) + d_a[c] * a_row[c]
            d_cum = (dq_pos[c] * q_pos[c] - dk_neg[c] * k_neg[c] - dq_neg[c] * q_neg[c] + dk_pos[c] * k_pos[c] - t)
            d_la.append(_mm_f32(utri, d_cum + jnp.where(row == C - 1, d_last, 0.0)))
        d_pre = [d_la[c] * _sig(-pre[c]) * (1.0 / GLA_TAU) for c in cs]
        d_gg = [_mm_nt(d_pre[c], w_ref[...]) for c in cs]
        dw_acc = _mm_tn(ps[0][:, 512:640], d_pre[0])
        db_acc = jnp.sum(d_pre[0], axis=0, keepdims=True)
        for c in cs[1:]:
            dw_acc = dw_acc + _mm_tn(ps[c][:, 512:640], d_pre[c])
            db_acc = db_acc + jnp.sum(d_pre[c], axis=0, keepdims=True)
        for c in cs:
            dp_ref[0, rows[c], :] = jnp.concatenate([dq[c], dk[c] * GLA_KSCALE, dv[c], d_gg[c], dz[c]],
                                                    axis=1).astype(_MXU)
        dw_ref[...] += dw_acc
        db_ref[...] += db_acc
        dgn_ref[...] += dgn_acc

        @pl.when((pl.program_id(0) == B - 1) & (pl.program_id(1) == NG - 1))
        def _():
            s1 = dgn_ref[...]
            s1 = s1 + pltpu.roll(s1, 128, 1)
            dgn_ref[...] = s1 + pltpu.roll(s1, 64, 1)

    tok = lambda w: pl.BlockSpec((1, G * C, w), lambda b, i: (b, NG - 1 - i, 0))
    return pl.pallas_call(
        body, name="gla_bwd", grid=(B, NG),
        in_specs=[tok(GLA_W), _full((128, 128)), _full((1, 128)), _full((1, 256)), tok(256),
                  pl.BlockSpec((1, G, 256, 128), lambda b, i: (b, NG - 1 - i, 0, 0)), tok(256)],
        out_specs=[tok(GLA_W), _full((128, 128)), _full((1, 128)), _full((1, 256))],
        out_shape=[jax.ShapeDtypeStruct((B, S, GLA_W), _MXU), jax.ShapeDtypeStruct((128, 128), F32),
                   jax.ShapeDtypeStruct((1, 128), F32), jax.ShapeDtypeStruct((1, 256), F32)],
        scratch_shapes=[pltpu.VMEM((256, 128), F32)],
        compiler_params=_cp(("arbitrary", "arbitrary")),
    )(gla_p, w_g2p, b_g2, gnw, raw, states, d_mix)


def _rms(x, w):
    rstd = lax.rsqrt(jnp.mean(x * x, axis=-1, keepdims=True) + EPS)
    xh = x * rstd
    return xh, rstd, xh * w


def _rms_bwd(dy, xh, rstd, w):
    dxh = dy * w
    return rstd * (dxh - xh * jnp.mean(dxh * xh, axis=-1, keepdims=True))


MLA_T = 256


def _mla_prep_fwd(mla_p, cos, sin, qnw, kvnw, w_uq, w_ukv):
    B, S, _ = mla_p.shape
    tm = min(S, 512)

    t = min(MLA_T, S)
    nt = tm // t

    def body(p_ref, c_ref, s_ref, qn_ref, kn_ref, wq_ref, wkv_ref, q_ref, k_ref, v_ref, kt_ref, vt_ref):
        p = p_ref[0]
        cs, sn = c_ref[0], s_ref[0]
        _, _, qn = _rms(p[:, 0:256], qn_ref[...])
        qpre = _mm(qn, wq_ref[...])
        _, _, kvn = _rms(p[:, 256:384], kn_ref[...])
        kv = _mm(kvn, wkv_ref[...])
        kpe = _rope128(p[:, 384:512], cs, sn)
        for h in range(8):
            sl = slice(128 * h, 128 * h + 128)
            q_ref[0, :, sl] = _rope128(qpre[:, sl], cs, sn).astype(_MXU)
            kh = kv[:, sl] + kpe
            k_ref[0, :, sl] = kh.astype(_MXU)
            kht = kh.T
            for n in range(nt):
                kt_ref[0, n, sl, :] = kht[:, n * t:(n + 1) * t].astype(_MXU)
        v_ref[0] = kv[:, 1024:1536].astype(_MXU)
        for pr in range(4):
            vht = kv[:, 1024 + 128 * pr:1152 + 128 * pr].T
            for n in range(nt):
                vt_ref[0, n, 128 * pr:128 * pr + 128, :] = vht[:, n * t:(n + 1) * t].astype(_MXU)

    tok = lambda w: pl.BlockSpec((1, tm, w), lambda b, i: (b, i, 0))
    tr = lambda w: pl.BlockSpec((1, nt, w, t), lambda b, i: (b, i, 0, 0))
    return pl.pallas_call(
        body, name="mla_prep_fwd", grid=(B, S // tm),
        in_specs=[tok(512), tok(128), tok(128), _full((1, 256)), _full((1, 128)), _full((256, 1024)),
                  _full((128, 1536))],
        out_specs=[tok(1024), tok(1024), tok(512), tr(1024), tr(512)],
        out_shape=[jax.ShapeDtypeStruct((B, S, 1024), _MXU), jax.ShapeDtypeStruct((B, S, 1024), _MXU),
                   jax.ShapeDtypeStruct((B, S, 512), _MXU), jax.ShapeDtypeStruct((B, S // t, 1024, t), _MXU),
                   jax.ShapeDtypeStruct((B, S // t, 512, t), _MXU)],
        compiler_params=_cp(("parallel", "parallel")),
    )(mla_p, cos, sin, qnw, kvnw, w_uq, w_ukv)


def _chunk_mask_t(t):
    kj = lax.broadcasted_iota(jnp.int32, (t, t), 0) // CHUNK
    qi = lax.broadcasted_iota(jnp.int32, (t, t), 1) // CHUNK
    return kj <= qi


MLA_HG = 4
LOG2E = 1.4426950408889634
MLA_C2 = MLA_SCALE * LOG2E


def _mla_attn_fwd(q, k, vt):
    B, S, _ = q.shape
    t = min(MLA_T, S)
    nq = S // t
    HG = MLA_HG
    NP = HG // 2

    def body(q_ref, k_ref, vt_ref, o_ref, lse_ref, sa, sb, m_sc, l_sc, acc_sc):
        i = pl.program_id(2)
        row = lax.broadcasted_iota(jnp.int32, (128, 1), 0)
        low = row < 64
        mask = _chunk_mask_t(t)
        m_sc[...] = jnp.full(m_sc.shape, -jnp.inf, F32)
        l_sc[...] = jnp.zeros_like(l_sc)
        acc_sc[...] = jnp.zeros_like(acc_sc)

        def scores(j, buf):
            kb = k_ref[0, pl.ds(pl.multiple_of(j * t, t), t), :]
            for h in range(HG):
                cols = slice(128 * h, 128 * h + 128)
                buf[h] = _mm_nt(kb[:, cols], q_ref[0, :, cols]) * MLA_C2

        def absorb(j, buf, masked):
            vtb = vt_ref[0, j]
            for pr in range(NP):
                alphas, pvs = [], []
                for hh in range(2):
                    h = 2 * pr + hh
                    s = buf[h]
                    if masked:
                        s = jnp.where(mask, s, -jnp.inf)
                    m_old = m_sc[h]
                    m_new = jnp.maximum(m_old, jnp.max(s, axis=0, keepdims=True))
                    alpha = jnp.exp2(m_old - m_new)
                    p = jnp.exp2(s - m_new)
                    l_sc[h] = alpha * l_sc[h] + jnp.sum(p, axis=0, keepdims=True)
                    m_sc[h] = m_new
                    vth = vtb[128 * pr:128 * pr + 128, :]
                    vth = jnp.where(low if hh == 0 else ~low, vth, jnp.zeros_like(vth))
                    pvs.append(_mm(vth, p))
                    alphas.append(alpha)
                acc_sc[pr] = acc_sc[pr] * jnp.where(low, alphas[0], alphas[1]) + pvs[0] + pvs[1]

        scores(0, sb)

        def pair(jj, carry):
            j0 = 2 * jj
            scores(j0 + 1, sa)
            absorb(j0, sb, False)
            scores(j0 + 2, sb)
            absorb(j0 + 1, sa, False)
            return carry

        lax.fori_loop(0, i // 2, pair, 0)

        @pl.when(i % 2 == 1)
        def _():
            scores(i, sa)
            absorb(i - 1, sb, False)
            absorb(i, sa, True)

        @pl.when(i % 2 == 0)
        def _():
            absorb(i, sb, True)

        for pr in range(NP):
            l_e, l_o = l_sc[2 * pr], l_sc[2 * pr + 1]
            o_ref[0, :, 128 * pr:128 * pr + 128] = (acc_sc[pr] / jnp.where(low, l_e, l_o)).T
            lse_ref[0, pr, 0, 0:1, :] = m_sc[2 * pr] + jnp.log(l_e) * LOG2E
            lse_ref[0, pr, 0, 1:2, :] = m_sc[2 * pr + 1] + jnp.log(l_o) * LOG2E

    return pl.pallas_call(
        body, name="mla_attn_fwd", grid=(B, 8 // HG, nq),
        in_specs=[pl.BlockSpec((1, t, 128 * HG), lambda b, g, i: (b, i, g)),
                  pl.BlockSpec((1, S, 128 * HG), lambda b, g, i: (b, 0, g)),
                  pl.BlockSpec((1, nq, 64 * HG, t), lambda b, g, i: (b, 0, g, 0))],
        out_specs=[pl.BlockSpec((1, t, 64 * HG), lambda b, g, i: (b, i, g)),
                   pl.BlockSpec((1, NP, 1, 2, t), lambda b, g, i: (b, g, i, 0, 0))],
        out_shape=[jax.ShapeDtypeStruct((B, S, 512), F32), jax.ShapeDtypeStruct((B, 4, nq, 2, t), F32)],
        scratch_shapes=[pltpu.VMEM((HG, t, t), F32), pltpu.VMEM((HG, t, t), F32), pltpu.VMEM((HG, 1, t), F32),
                        pltpu.VMEM((HG, 1, t), F32), pltpu.VMEM((NP, 128, t), F32)],
        compiler_params=_cp(("parallel", "parallel", "arbitrary")),
    )(q, k, vt)


def _mla_gate_bwd(d_mix, o, mla_p):
    B, S, _ = o.shape
    tm = min(S, 512)
    t = min(MLA_T, S)
    nt = tm // t

    def body(dm_ref, o_ref, z_ref, do_ref, dz_ref, dl_ref):
        dm, ov, z = dm_ref[0], o_ref[0], z_ref[0]
        do = dm * _silu(z)
        dz_ref[0] = (dm * ov * _dsilu(z)).astype(_MXU)
        do_ref[0] = do.astype(_MXU)
        prod = do * ov
        for pr in range(4):
            pt = prod[:, 128 * pr:128 * pr + 128].T
            se = jnp.sum(pt[0:64], axis=0, keepdims=True)
            so = jnp.sum(pt[64:128], axis=0, keepdims=True)
            for n in range(nt):
                dl_ref[0, pr, n, 0:1, :] = se[:, n * t:(n + 1) * t]
                dl_ref[0, pr, n, 1:2, :] = so[:, n * t:(n + 1) * t]

    tok = lambda c: pl.BlockSpec((1, tm, 512), lambda b, i: (b, i, c))
    return pl.pallas_call(
        body, name="mla_gate_bwd", grid=(B, S // tm),
        in_specs=[tok(0), tok(0), tok(1)],
        out_specs=[tok(0), tok(0), pl.BlockSpec((1, 4, nt, 2, t), lambda b, i: (b, 0, i, 0, 0))],
        out_shape=[jax.ShapeDtypeStruct((B, S, 512), _MXU), jax.ShapeDtypeStruct((B, S, 512), _MXU),
                   jax.ShapeDtypeStruct((B, 4, S // t, 2, t), F32)],
        compiler_params=_cp(("parallel", "parallel")),
    )(d_mix, o, mla_p)


def _mla_attn_bwd(q, k, v, kt, do, lse, dl):
    B, S, _ = q.shape
    t = min(MLA_T, S)
    nk = S // t

    HG = MLA_HG
    NP = HG // 2

    def body(q_ref, k_ref, v_ref, kt_ref, do_ref, lse_ref, dl_ref, dq_ref, dk_ref, dv_ref,
             sa, da, sb, db, dqt_sc, dk_sc, dv_sc):
        j = pl.program_id(2)

        @pl.when(j == 0)
        def _():
            dqt_sc[...] = jnp.zeros_like(dqt_sc)

        dk_sc[...] = jnp.zeros_like(dk_sc)
        dv_sc[...] = jnp.zeros_like(dv_sc)
        lane = lax.broadcasted_iota(jnp.int32, (1, 128), 1)
        low = lane < 64
        mask = _chunk_mask_t(t)

        def half(x, hh):
            return jnp.where(low if hh == 0 else ~low, x, jnp.zeros_like(x))

        def prepare(i, sbuf, dbuf):
            rows = pl.ds(pl.multiple_of(i * t, t), t)
            for h in range(HG):
                cols = slice(128 * h, 128 * h + 128)
                pc = slice(128 * (h // 2), 128 * (h // 2) + 128)
                sbuf[h] = _mm_nt(k_ref[0, :, cols], q_ref[0, rows, cols]) * MLA_C2
                dbuf[h] = _mm_nt(half(v_ref[0, :, pc], h % 2), do_ref[0, rows, pc])

        def absorb(i, sbuf, dbuf, masked):
            rows = pl.ds(pl.multiple_of(i * t, t), t)
            for h in range(HG):
                pr, hh = h // 2, h % 2
                cols = slice(128 * h, 128 * h + 128)
                pc = slice(128 * pr, 128 * pr + 128)
                p = jnp.exp2(sbuf[h] - lse_ref[0, pr, i][hh:hh + 1, :])
                if masked:
                    p = jnp.where(mask, p, 0.0)
                dv_sc[pr] += _mm(p, half(do_ref[0, rows, pc], hh))
                ds = p * (dbuf[h] - dl_ref[0, pr, i][hh:hh + 1, :])
                dqt_sc[i, cols, :] += _mm(kt_ref[0, 0, cols, :], ds)
                dk_sc[h] += _mm(ds, q_ref[0, rows, cols])

        n = nk - 1 - j
        prepare(jnp.minimum(j + 1, nk - 1), sb, db)

        def pair(jj, carry):
            i0 = j + 1 + 2 * jj
            prepare(i0 + 1, sa, da)
            absorb(i0, sb, db, False)
            prepare(jnp.where(i0 + 2 <= nk - 1, i0 + 2, j), sb, db)
            absorb(i0 + 1, sa, da, False)
            return carry

        lax.fori_loop(0, n // 2, pair, 0)

        @pl.when(n % 2 == 1)
        def _():
            prepare(j, sa, da)
            absorb(nk - 1, sb, db, False)
            absorb(j, sa, da, True)

        @pl.when(n % 2 == 0)
        def _():
            absorb(j, sb, db, True)

        for h in range(HG):
            dk_ref[0, :, 128 * h:128 * h + 128] = dk_sc[h] * MLA_SCALE
        for pr in range(NP):
            dv_ref[0, :, 128 * pr:128 * pr + 128] = dv_sc[pr]

        @pl.when(j == nk - 1)
        def _():
            for i in range(nk):
                dq_ref[0, i * t:(i + 1) * t, :] = dqt_sc[i].T * MLA_SCALE

    seq = lambda w: pl.BlockSpec((1, S, w), lambda b, g, j: (b, 0, g))
    blk = lambda w: pl.BlockSpec((1, t, w), lambda b, g, j: (b, j, g))
    stat = pl.BlockSpec((1, NP, nk, 2, t), lambda b, g, j: (b, g, 0, 0, 0))
    return pl.pallas_call(
        body, name="mla_attn_bwd", grid=(B, 8 // HG, nk),
        in_specs=[seq(128 * HG), blk(128 * HG), blk(64 * HG),
                  pl.BlockSpec((1, 1, 128 * HG, t), lambda b, g, j: (b, j, g, 0)), seq(64 * HG), stat, stat],
        out_specs=[seq(128 * HG), blk(128 * HG), blk(64 * HG)],
        out_shape=[jax.ShapeDtypeStruct((B, S, 1024), F32), jax.ShapeDtypeStruct((B, S, 1024), F32),
                   jax.ShapeDtypeStruct((B, S, 512), F32)],
        scratch_shapes=[pltpu.VMEM((HG, t, t), F32), pltpu.VMEM((HG, t, t), F32), pltpu.VMEM((HG, t, t), F32),
                        pltpu.VMEM((HG, t, t), F32), pltpu.VMEM((nk, 128 * HG, t), F32),
                        pltpu.VMEM((HG, t, 128), F32), pltpu.VMEM((NP, t, 128), F32)],
        compiler_params=_cp(("parallel", "parallel", "arbitrary")),
    )(q, k, v, kt, do, lse, dl)


def _mla_prep_bwd(mla_p, cos, sin, qnw, kvnw, w_uq, w_ukv, dq, dk, dv):
    B, S, _ = mla_p.shape
    tm = min(S, 512)

    def body(p_ref, c_ref, s_ref, qn_ref, kn_ref, wq_ref, wkv_ref, dq_ref, dk_ref, dv_ref,
             dp_ref, dwq_ref, dwkv_ref, dqn_ref, dkn_ref):
        first = (pl.program_id(0) == 0) & (pl.program_id(1) == 0)

        @pl.when(first)
        def _():
            dwq_ref[...] = jnp.zeros_like(dwq_ref)
            dwkv_ref[...] = jnp.zeros_like(dwkv_ref)
            dqn_ref[...] = jnp.zeros_like(dqn_ref)
            dkn_ref[...] = jnp.zeros_like(dkn_ref)

        p = p_ref[0]
        cs, sn = c_ref[0], s_ref[0]
        lane = lax.broadcasted_iota(jnp.int32, (1, 128), 1)
        pe = (lane >= 64) & (lane < 96)
        qh, q_rstd, qn = _rms(p[:, 0:256], qn_ref[...])
        kvh, kv_rstd, kvn = _rms(p[:, 256:384], kn_ref[...])
        dqv = dq_ref[0]
        dkv = dk_ref[0]
        dqpre = jnp.concatenate(
            [_rope128_t(dqv[:, 128 * h:128 * h + 128], cs, sn) for h in range(8)], axis=1)
        dkpe = jnp.zeros((tm, 128), F32)
        for h in range(8):
            dkpe = dkpe + jnp.where(pe, dkv[:, 128 * h:128 * h + 128], 0.0)
        dkr = _rope128_t(dkpe, cs, sn)
        dkv_all = jnp.concatenate([dkv, dv_ref[0]], axis=1)
        d_qn = _mm_nt(dqpre, wq_ref[...])
        d_kvn = _mm_nt(dkv_all, wkv_ref[...])
        dwq_ref[...] += _mm_tn(qn, dqpre)
        dwkv_ref[...] += _mm_tn(kvn, dkv_all)
        dqn_ref[...] += jnp.sum(d_qn * qh, axis=0, keepdims=True)
        dkn_ref[...] += jnp.sum(d_kvn * kvh, axis=0, keepdims=True)
        dp_ref[0] = jnp.concatenate([_rms_bwd(d_qn, qh, q_rstd, qn_ref[...]),
                                     _rms_bwd(d_kvn, kvh, kv_rstd, kn_ref[...]), dkr], axis=1).astype(_MXU)

    tok = lambda w: pl.BlockSpec((1, tm, w), lambda b, i: (b, i, 0))
    return pl.pallas_call(
        body, name="mla_prep_bwd", grid=(B, S // tm),
        in_specs=[tok(512), tok(128), tok(128), _full((1, 256)), _full((1, 128)), _full((256, 1024)),
                  _full((128, 1536)), tok(1024), tok(1024), tok(512)],
        out_specs=[tok(512), _full((256, 1024)), _full((128, 1536)), _full((1, 256)), _full((1, 128))],
        out_shape=[jax.ShapeDtypeStruct((B, S, 512), _MXU), jax.ShapeDtypeStruct((256, 1024), F32),
                   jax.ShapeDtypeStruct((128, 1536), F32), jax.ShapeDtypeStruct((1, 256), F32),
                   jax.ShapeDtypeStruct((1, 128), F32)],
        compiler_params=_cp(("arbitrary", "arbitrary")),
    )(mla_p, cos, sin, qnw, kvnw, w_uq, w_ukv, dq, dk, dv)


def _out_fwd(x, gate, r_g, o_mla, mla_p, g_g, w_out):
    B, S, D = x.shape
    tm = min(S, 512)

    def body(x_ref, g_ref, r_ref, o_ref, z_ref, gg_ref, w_ref, xn_ref, y_ref, mm_ref):
        mm = (o_ref[0] * _silu(z_ref[0])).astype(_MXU)
        mm_ref[0] = mm
        y = (jnp.dot(r_ref[0], w_ref[0:256, :], preferred_element_type=F32)
             + jnp.dot(mm, w_ref[256:768, :], preferred_element_type=F32)
             + jnp.dot(gg_ref[0], w_ref[768:1024, :], preferred_element_type=F32))
        y_ref[0] = y
        xn_ref[0] = x_ref[0] + g_ref[0] * y

    tok = lambda w, c=0: pl.BlockSpec((1, tm, w), lambda b, i: (b, i, c))
    return pl.pallas_call(
        body, name="out_fwd", grid=(B, S // tm),
        in_specs=[tok(D), pl.BlockSpec((1, 1, D), lambda b, i: (b, 0, 0)), tok(256), tok(512), tok(512, 1),
                  tok(256), _full((D, D))],
        out_specs=[tok(D), tok(D), tok(512)],
        out_shape=[jax.ShapeDtypeStruct((B, S, D), F32), jax.ShapeDtypeStruct((B, S, D), F32),
                   jax.ShapeDtypeStruct((B, S, 512), _MXU)],
        compiler_params=_cp(("parallel", "parallel")),
    )(x, gate, r_g, o_mla, mla_p, g_g, w_out)


def _out_bwd(dx, y, gate, r_g, mm, g_g, w_out):
    B, S, D = dx.shape
    tm = min(S, 512)

    def body(dx_ref, y_ref, g_ref, r_ref, mm_ref, gg_ref, w_ref, dr_ref, dmm_ref, dg_ref, dw_ref, dgate_ref):
        first = (pl.program_id(0) == 0) & (pl.program_id(1) == 0)

        @pl.when(first)
        def _():
            dw_ref[...] = jnp.zeros_like(dw_ref)

        @pl.when(pl.program_id(1) == 0)
        def _():
            dgate_ref[...] = jnp.zeros_like(dgate_ref)

        dxv = dx_ref[0]
        dgate_ref[0] += jnp.sum(dxv * y_ref[0], axis=0, keepdims=True)
        dy = (dxv * g_ref[0]).astype(_MXU)
        dr_ref[0] = _mm_nt(dy, w_ref[0:256, :])
        dmm_ref[0] = _mm_nt(dy, w_ref[256:768, :])
        dg_ref[0] = _mm_nt(dy, w_ref[768:1024, :])
        dw_ref[0:256, :] += _mm_tn(r_ref[0], dy)
        dw_ref[256:768, :] += _mm_tn(mm_ref[0], dy)
        dw_ref[768:1024, :] += _mm_tn(gg_ref[0], dy)

    tok = lambda w: pl.BlockSpec((1, tm, w), lambda b, i: (b, i, 0))
    per_seq = pl.BlockSpec((1, 1, D), lambda b, i: (b, 0, 0))
    return pl.pallas_call(
        body, name="out_bwd", grid=(B, S // tm),
        in_specs=[tok(D), tok(D), per_seq, tok(256), tok(512), tok(256), _full((D, D))],
        out_specs=[tok(256), tok(512), tok(256), _full((D, D)), per_seq],
        out_shape=[jax.ShapeDtypeStruct((B, S, 256), F32), jax.ShapeDtypeStruct((B, S, 512), F32),
                   jax.ShapeDtypeStruct((B, S, 256), F32), jax.ShapeDtypeStruct((D, D), F32),
                   jax.ShapeDtypeStruct((B, 1, D), F32)],
        compiler_params=_cp(("arbitrary", "arbitrary")),
    )(dx, y, gate, r_g, mm, g_g, w_out)


def _proj_bwd_x(x, shift, scale, nw, w_arr, d_ret, d_mla, d_mz, d_gla, dx_out):
    B, S, D = x.shape
    tm = min(S, 256)

    def body(x_ref, sc_ref, nw_ref, w_ref, dr_ref, dm_ref, dz_ref, dg_ref, dxo_ref,
             dx_ref, dsh_ref, dsc_ref, dnw_ref):
        first = (pl.program_id(0) == 0) & (pl.program_id(1) == 0)

        @pl.when(first)
        def _():
            dnw_ref[...] = jnp.zeros_like(dnw_ref)

        @pl.when(pl.program_id(1) == 0)
        def _():
            dsh_ref[...] = jnp.zeros_like(dsh_ref)
            dsc_ref[...] = jnp.zeros_like(dsc_ref)

        dp = jnp.concatenate([dr_ref[0], dm_ref[0], dz_ref[0], dg_ref[0]], axis=1)
        dh = lax.dot_general(dp, w_ref[...], (((1,), (1,)), ((), ())), preferred_element_type=F32)
        xv = x_ref[0]
        rstd = lax.rsqrt(jnp.mean(xv * xv, axis=-1, keepdims=True) + EPS)
        xh = xv * rstd
        nwv = nw_ref[...]
        mod = 1.0 + sc_ref[0]
        dsh_ref[0] += jnp.sum(dh, axis=0, keepdims=True)
        dsc_ref[0] += jnp.sum(dh * xh * nwv, axis=0, keepdims=True)
        dnw_ref[...] += jnp.sum(dh * xh * mod, axis=0, keepdims=True)
        dxh = dh * nwv * mod
        dx_ref[0] = dxo_ref[0] + rstd * (dxh - xh * jnp.mean(dxh * xh, axis=-1, keepdims=True))

    tok = lambda w: pl.BlockSpec((1, tm, w), lambda b, i: (b, i, 0))
    per_seq = pl.BlockSpec((1, 1, D), lambda b, i: (b, 0, 0))
    return pl.pallas_call(
        body, name="proj_bwd_x", grid=(B, S // tm),
        in_specs=[tok(D), per_seq, _full((1, D)), _full((D, ARR_W)), tok(RET_W), tok(512), tok(512),
                  tok(GLA_W), tok(D)],
        out_specs=[tok(D), per_seq, per_seq, _full((1, D))],
        out_shape=[jax.ShapeDtypeStruct((B, S, D), F32), jax.ShapeDtypeStruct((B, 1, D), F32),
                   jax.ShapeDtypeStruct((B, 1, D), F32), jax.ShapeDtypeStruct((1, D), F32)],
        compiler_params=_cp(("arbitrary", "arbitrary")),
    )(x, scale, nw, w_arr, d_ret, d_mla, d_mz, d_gla, dx_out)


def _proj_bwd_w(h, d_ret, d_mla, d_mz, d_gla):
    B, S, D = h.shape
    tm = min(S, 512)

    def body(h_ref, dr_ref, dm_ref, dz_ref, dg_ref, dw_ref):
        first = (pl.program_id(0) == 0) & (pl.program_id(1) == 0)

        @pl.when(first)
        def _():
            dw_ref[...] = jnp.zeros_like(dw_ref)

        hv = h_ref[0]
        tn = lambda d_ref: lax.dot_general(hv, d_ref[0], (((0,), (0,)), ((), ())), preferred_element_type=F32)
        dw_ref[:, 0:RET_W] += tn(dr_ref)
        dw_ref[:, RET_W:RET_W + 512] += tn(dm_ref)
        dw_ref[:, RET_W + 512:RET_W + MLA_W] += tn(dz_ref)
        dw_ref[:, RET_W + MLA_W:ARR_W] += tn(dg_ref)

    tok = lambda w: pl.BlockSpec((1, tm, w), lambda b, i: (b, i, 0))
    return pl.pallas_call(
        body, name="proj_bwd_w", grid=(B, S // tm),
        in_specs=[tok(D), tok(RET_W), tok(512), tok(512), tok(GLA_W)],
        out_specs=_full((D, ARR_W)), out_shape=jax.ShapeDtypeStruct((D, ARR_W), F32),
        compiler_params=_cp(("arbitrary", "arbitrary"), 56),
    )(h, d_ret, d_mla, d_mz, d_gla)


def _final_loss(x, fw, target):
    B, S, D = x.shape
    tm = min(S, 512)

    def body(x_ref, fw_ref, t_ref, dx_ref, loss_ref, dfw_ref):
        first = (pl.program_id(0) == 0) & (pl.program_id(1) == 0)

        @pl.when(first)
        def _():
            loss_ref[...] = jnp.zeros_like(loss_ref)
            dfw_ref[...] = jnp.zeros_like(dfw_ref)

        xv = x_ref[0]
        fwv = fw_ref[...]
        rstd = lax.rsqrt(jnp.mean(xv * xv, axis=-1, keepdims=True) + EPS)
        xh = xv * rstd
        err = xh * fwv - t_ref[0]
        loss_ref[...] += 0.5 * jnp.sum(jnp.mean(err * err, axis=-1, keepdims=True), axis=0, keepdims=True)
        dy = err * (1.0 / D)
        dfw_ref[...] += jnp.sum(dy * xh, axis=0, keepdims=True)
        dxh = dy * fwv
        dx_ref[0] = rstd * (dxh - xh * jnp.mean(dxh * xh, axis=-1, keepdims=True))

    tok = pl.BlockSpec((1, tm, D), lambda b, i: (b, i, 0))
    return pl.pallas_call(
        body, name="final_loss", grid=(B, S // tm),
        in_specs=[tok, _full((1, D)), tok],
        out_specs=[tok, _full((1, 1)), _full((1, D))],
        out_shape=[jax.ShapeDtypeStruct((B, S, D), F32), jax.ShapeDtypeStruct((1, 1), F32),
                   jax.ShapeDtypeStruct((1, D), F32)],
        compiler_params=_cp(("arbitrary", "arbitrary")),
    )(x, fw, target)


def _local_step(x, pos3, mod, loss_target, small, w_in_a, w_uq_a, w_ukv_a, w_out_b):
    B, S, D = x.shape
    tabs = _rope_tables(pos3)
    saved = []
    for l in range(DEPTH):
        x, s = _layer_fwd(x, tabs, mod[l], {n: a[l] for n, a in small.items() if n != "final_norm"},
                          w_in_a[l], w_uq_a[l], w_ukv_a[l], w_out_b[l])
        saved.append(s)
    dx, loss, d_fw = _final_loss(x, small["final_norm"].reshape(1, D), loss_target)
    grads = dict(final_norm=d_fw.reshape(D))
    per_layer = [None] * DEPTH
    for l in reversed(range(DEPTH)):
        dx, per_layer[l] = _layer_bwd(dx, saved[l], tabs)
    for name in per_layer[0]:
        grads[name] = jnp.stack([per_layer[l][name] for l in range(DEPTH)])
    return loss, dx, grads


def _layer_fwd(x, tabs, mod_l, small_l, w_in_a, w_uq_a=None, w_ukv_a=None, w_out_b=None, late_weights=None):
    B, S, D = x.shape
    cr, sr, cm, sm = tabs
    shift = mod_l[:, 0:D].reshape(B, 1, D)
    scale = mod_l[:, D:2 * D].reshape(B, 1, D)
    gate = mod_l[:, 2 * D:3 * D].reshape(B, 1, D)
    nw = small_l["norm_w"].reshape(1, D)
    qnw = small_l["mla_q_norm"].reshape(1, 256)
    kvnw = small_l["mla_kv_norm"].reshape(1, 128)
    w_g2p = jnp.pad(small_l["gla_w_g2"], ((0, 112), (0, 0)))
    b_g2 = small_l["gla_b_g2"].reshape(1, 128)
    gnw = jnp.tile(small_l["gla_norm"], 4).reshape(1, 256)
    ret_p, mla_p, gla_p, h = _proj_fwd(x, shift, scale, nw, w_in_a)
    r_g, r_raw, r_st = _ret_fwd(ret_p, cr, sr)
    if late_weights is not None:
        w_uq_a, w_ukv_a, w_out_b = late_weights(r_raw)
    q, k, v, kt, vt = _mla_prep_fwd(mla_p, cm, sm, qnw, kvnw, w_uq_a, w_ukv_a)
    o_mla, lse = _mla_attn_fwd(q, k, vt)
    g_g, g_raw, g_st = _gla_fwd(gla_p, w_g2p, b_g2, gnw)
    x_new, y, mm = _out_fwd(x, gate, r_g, o_mla, mla_p, g_g, w_out_b)
    saved = dict(x=x, shift=shift, scale=scale, gate=gate, nw=nw, qnw=qnw, kvnw=kvnw, w_g2p=w_g2p, b_g2=b_g2,
                 gnw=gnw, ret_p=ret_p, mla_p=mla_p, gla_p=gla_p, h=h, r_g=r_g, r_raw=r_raw, r_st=r_st, q=q, k=k,
                 v=v, kt=kt, o_mla=o_mla, lse=lse, g_g=g_g, g_raw=g_raw, g_st=g_st, y=y, mm=mm,
                 w_in_a=w_in_a, w_uq_a=w_uq_a, w_ukv_a=w_ukv_a, w_out_b=w_out_b)
    return x_new, saved


def _layer_bwd(dx, s, tabs, early_grads=None):
    B, S, D = dx.shape
    cr, sr, cm, sm = tabs
    d_r, d_mm, d_g, dw_out, d_gate = _out_bwd(dx, s["y"], s["gate"], s["r_g"], s["mm"], s["g_g"], s["w_out_b"])
    d_ret = _ret_bwd(s["ret_p"], cr, sr, s["r_raw"], s["r_st"], d_r)
    do, d_mz, dl = _mla_gate_bwd(d_mm, s["o_mla"], s["mla_p"])
    dq, dk, dv = _mla_attn_bwd(s["q"], s["k"], s["v"], s["kt"], do, s["lse"], dl)
    d_mla, dw_uq, dw_ukv, d_qnw, d_kvnw = _mla_prep_bwd(
        s["mla_p"], cm, sm, s["qnw"], s["kvnw"], s["w_uq_a"], s["w_ukv_a"], dq, dk, dv)
    gnw = s["gnw"] if early_grads is None else s["gnw"] + early_grads(dw_out, dw_uq, dw_ukv)
    d_gla, dw_g2p, db_g2, d_gnw = _gla_bwd(s["gla_p"], s["w_g2p"], s["b_g2"], gnw, s["g_raw"], s["g_st"], d_g)
    dx, d_shift, d_scale, d_nw = _proj_bwd_x(s["x"], s["shift"], s["scale"], s["nw"], s["w_in_a"],
                                             d_ret, d_mla, d_mz, d_gla, dx)
    dw_in = _proj_bwd_w(s["h"], d_ret, d_mla, d_mz, d_gla)
    grads = dict(
        d_mod=jnp.concatenate([d_shift, d_scale, d_gate], axis=2).reshape(B, 3 * D),
        norm_w=d_nw.reshape(D), mla_q_norm=d_qnw.reshape(256), mla_kv_norm=d_kvnw.reshape(128),
        gla_w_g2=dw_g2p[0:16], gla_b_g2=db_g2.reshape(128), gla_norm256=d_gnw.reshape(256),
        w_in_a=dw_in, w_uq_a=dw_uq, w_ukv_a=dw_ukv, w_out=dw_out)
    return dx, grads


def _exchange(arrs, gather, name):
    n = len(arrs)
    out_shape = [jax.ShapeDtypeStruct(((N_DEV,) + a.shape) if g else a.shape, a.dtype)
                 for a, g in zip(arrs, gather)]

    def body(*refs):
        ins, outs = refs[:n], refs[n:2 * n]
        send_sems, recv_sems, local_sems = refs[2 * n:]
        ix, iy, ic = lax.axis_index("x"), lax.axis_index("y"), lax.axis_index("c")
        me = 4 * ix + 2 * iy + ic
        copies = []
        for a in range(n):
            mine = ins[a] if gather[a] else ins[a].at[me]
            loc = pltpu.make_async_copy(mine, outs[a].at[me], local_sems.at[a])
            loc.start()
            copies.append(loc)
            for d in range(1, N_DEV):
                px = 1 - ix if d & 4 else ix
                py = 1 - iy if d & 2 else iy
                pc = 1 - ic if d & 1 else ic
                src = ins[a] if gather[a] else ins[a].at[4 * px + 2 * py + pc]
                cp = pltpu.make_async_remote_copy(
                    src_ref=src, dst_ref=outs[a].at[me], send_sem=send_sems.at[a, d - 1],
                    recv_sem=recv_sems.at[a, d - 1], device_id=(px, py, pc), device_id_type=pl.DeviceIdType.MESH)
                cp.start()
                copies.append(cp)
        for cp in copies:
            cp.wait()

    any_spec = pl.BlockSpec(memory_space=pl.ANY)
    outs = pl.pallas_call(
        body, name=name, in_specs=[any_spec] * n, out_specs=[any_spec] * n, out_shape=out_shape,
        scratch_shapes=[pltpu.SemaphoreType.DMA((n, N_DEV - 1)), pltpu.SemaphoreType.DMA((n, N_DEV - 1)),
                        pltpu.SemaphoreType.DMA((n,))],
    )(*arrs)
    return list(outs)


def _peers(ix, iy, ic):
    out = []
    for d in range(1, N_DEV):
        px = 1 - ix if d & 4 else ix
        py = 1 - iy if d & 2 else iy
        pc = 1 - ic if d & 1 else ic
        out.append((d - 1, (px, py, pc), 4 * px + 2 * py + pc))
    return out


def _exchange_start(arrs, gather, name):
    n = len(arrs)
    lands = [lax.empty(((N_DEV,) + a.shape) if g else a.shape, a.dtype) for a, g in zip(arrs, gather)]

    def body(*refs):
        ins, land_refs = refs[:n], refs[n:2 * n]
        send_sems, recv_sems = refs[2 * n], refs[2 * n + 1]
        token = refs[-1]
        ix, iy, ic = lax.axis_index("x"), lax.axis_index("y"), lax.axis_index("c")
        me = 4 * ix + 2 * iy + ic
        for a in range(n):
            for k, peer, peer_idx in _peers(ix, iy, ic):
                pltpu.make_async_remote_copy(
                    src_ref=ins[a] if gather[a] else ins[a].at[peer_idx], dst_ref=land_refs[a].at[me],
                    send_sem=send_sems.at[7 * a + k], recv_sem=recv_sems.at[7 * a + k], device_id=peer,
                    device_id_type=pl.DeviceIdType.MESH).start()
        token[...] = jnp.zeros_like(token)

    hbm = pl.BlockSpec(memory_space=pltpu.HBM)
    sem = pl.BlockSpec(memory_space=pltpu.SEMAPHORE)
    held = [pltpu.with_memory_space_constraint(a, pltpu.HBM) for a in list(arrs) + lands]
    outs = pl.pallas_call(
        body, name=name,
        out_shape=(pltpu.SemaphoreType.DMA((7 * n,)), pltpu.SemaphoreType.DMA((7 * n,)),
                   *[pltpu.HBM(a.shape, a.dtype) for a in held], jax.ShapeDtypeStruct((8, 128), F32)),
        in_specs=[hbm] * (2 * n), out_specs=(sem, sem, *[hbm] * (2 * n), pl.BlockSpec(memory_space=pltpu.VMEM)),
        input_output_aliases={a: 2 + a for a in range(2 * n)},
        compiler_params=pltpu.CompilerParams(has_side_effects=pltpu.SideEffectType.DATAFLOW_SIDE_EFFECTING),
    )(*held)
    return dict(send=outs[0], recv=outs[1], srcs=list(outs[2:2 + n]), lands=list(outs[2 + n:2 + 2 * n]),
                token=outs[-1], gather=list(gather))


def _exchange_wait(flight, after, me, name):
    n = len(flight["srcs"])
    gather = flight["gather"]

    def body(*refs):
        srcs, land_refs = refs[:n], refs[n:2 * n]
        send_sems, recv_sems = refs[2 * n], refs[2 * n + 1]
        ix, iy, ic = lax.axis_index("x"), lax.axis_index("y"), lax.axis_index("c")
        mine = 4 * ix + 2 * iy + ic
        for a in range(n):
            for k, peer, peer_idx in _peers(ix, iy, ic):
                cp = pltpu.make_async_remote_copy(
                    src_ref=srcs[a] if gather[a] else srcs[a].at[peer_idx], dst_ref=land_refs[a].at[mine],
                    send_sem=send_sems.at[7 * a + k], recv_sem=recv_sems.at[7 * a + k], device_id=peer,
                    device_id_type=pl.DeviceIdType.MESH)
                cp.wait_send()
                cp.wait_recv()

    hbm = pl.BlockSpec(memory_space=pltpu.HBM)
    sem = pl.BlockSpec(memory_space=pltpu.SEMAPHORE)
    held = flight["srcs"] + flight["lands"]
    outs = pl.pallas_call(
        body, name=name, out_shape=tuple(pltpu.HBM(a.shape, a.dtype) for a in held),
        in_specs=[hbm] * (2 * n) + [sem, sem, pl.BlockSpec(memory_space=pl.ANY)], out_specs=tuple([hbm] * (2 * n)),
        input_output_aliases={a: a for a in range(2 * n)},
        compiler_params=pltpu.CompilerParams(has_side_effects=pltpu.SideEffectType.DATAFLOW_SIDE_EFFECTING),
    )(*held, flight["send"], flight["recv"], after)
    got = []
    for a in range(n):
        src, land = outs[a], outs[n + a]
        own = src if gather[a] else lax.dynamic_index_in_dim(src, me, axis=0, keepdims=False)
        got.append(lax.dynamic_update_index_in_dim(land, own, me, axis=0))
    return got


def _ada_fwd(c_all, ada_w, ada_b_cols):
    nb, D = c_all.shape
    cols = ada_w.shape[2]

    def body(c_ref, w_ref, b_ref, out_ref):
        ca = _silu(c_ref[...])
        for l in range(DEPTH):
            out_ref[l] = _mm(ca, w_ref[l]) + b_ref[l:l + 1, :]

    return pl.pallas_call(
        body, name="ada_fwd", out_shape=jax.ShapeDtypeStruct((DEPTH, nb, cols), F32),
        in_specs=[pl.BlockSpec(memory_space=pltpu.VMEM)] * 3, out_specs=pl.BlockSpec(memory_space=pltpu.VMEM),
        compiler_params=pltpu.CompilerParams(vmem_limit_bytes=32 * VMEM_MB),
    )(c_all, ada_w, ada_b_cols)


def _ada_bwd(c_all, d_mod_cols):
    nb, D = c_all.shape
    cols = d_mod_cols.shape[2]

    def body(c_ref, dm_ref, out_ref):
        ca = _silu(c_ref[...])
        for l in range(DEPTH):
            out_ref[l] = _mm_tn(ca, dm_ref[l])

    return pl.pallas_call(
        body, name="ada_bwd", out_shape=jax.ShapeDtypeStruct((DEPTH, D, cols), F32),
        in_specs=[pl.BlockSpec(memory_space=pltpu.VMEM)] * 2, out_specs=pl.BlockSpec(memory_space=pltpu.VMEM),
        compiler_params=pltpu.CompilerParams(vmem_limit_bytes=32 * VMEM_MB),
    )(c_all, d_mod_cols)


def _sum_adamw(parts, w, m, v, name):
    P, R, C = parts.shape
    tr = 256 if (R % 256 == 0 and R > 256) else R

    def body(p_ref, w_ref, m_ref, v_ref, g_ref, d_ref, nm_ref, nv_ref):
        g = p_ref[0].astype(F32)
        for k in range(1, P):
            g = g + p_ref[k].astype(F32)
        g_ref[...] = g
        nm = ADAM_B1 * m_ref[...] + (1.0 - ADAM_B1) * g
        nv = ADAM_B2 * v_ref[...] + (1.0 - ADAM_B2) * (g * g)
        nm_ref[...] = nm
        nv_ref[...] = nv
        m_hat = nm / (1.0 - ADAM_B1 ** ADAM_STEP)
        v_hat = nv / (1.0 - ADAM_B2 ** ADAM_STEP)
        d_ref[...] = -ADAM_LR * (m_hat / (jnp.sqrt(v_hat) + ADAM_EPS) + ADAM_WD * w_ref[...])

    blk = pl.BlockSpec((tr, C), lambda i: (i, 0))
    shp = jax.ShapeDtypeStruct((R, C), F32)
    return pl.pallas_call(
        body, name=name, grid=(R // tr,),
        in_specs=[pl.BlockSpec((P, tr, C), lambda i: (0, i, 0)), blk, blk, blk],
        out_specs=[blk, blk, blk, blk], out_shape=[shp, shp, shp, shp],
        compiler_params=_cp(("parallel",)),
    )(parts, w, m, v)


SMALL = ["norm_w", "mla_q_norm", "mla_kv_norm", "gla_w_g2", "gla_b_g2", "gla_norm", "final_norm"]


SMALL_ROWS = 72


def _pack_small(loss, part):
    flat = [jnp.pad(loss.reshape(1), (0, 127))] + [part[n].reshape(-1) for n in SMALL]
    used = sum(f.shape[0] for f in flat)
    flat.append(jnp.zeros((SMALL_ROWS * 128 - used,), F32))
    return jnp.concatenate(flat).reshape(SMALL_ROWS, 128)


def _small_adamw(packed_parts, w, m, v):
    n = len(w)

    def body(*refs):
        p_ref = refs[0]
        w_refs, m_refs, v_refs = refs[1:1 + n], refs[1 + n:1 + 2 * n], refs[1 + 2 * n:1 + 3 * n]
        outs, acc = refs[1 + 3 * n:-1], refs[-1]
        total = p_ref[0]
        for k in range(1, N_DEV):
            total = total + p_ref[k]
        acc[...] = total
        outs[0][...] = acc[0:1, :]
        r0 = 1
        for i in range(n):
            shp = w_refs[i].shape
            if len(shp) == 3:
                g = acc[r0:r0 + shp[0] * shp[1], :].reshape(shp)
                r0 += shp[0] * shp[1]
            elif shp[1] < 128:
                g = acc[r0:r0 + shp[0], 0:shp[1]]
                r0 += shp[0]
            else:
                k = shp[1] // 128
                g = jnp.concatenate(
                    [jnp.concatenate([acc[r0 + l * k + j:r0 + l * k + j + 1, :] for j in range(k)], axis=1)
                     for l in range(shp[0])], axis=0)
                r0 += shp[0] * k
            nm = ADAM_B1 * m_refs[i][...] + (1.0 - ADAM_B1) * g
            nv = ADAM_B2 * v_refs[i][...] + (1.0 - ADAM_B2) * (g * g)
            m_hat = nm / (1.0 - ADAM_B1 ** ADAM_STEP)
            v_hat = nv / (1.0 - ADAM_B2 ** ADAM_STEP)
            outs[1 + 4 * i][...] = g
            outs[2 + 4 * i][...] = -ADAM_LR * (m_hat / (jnp.sqrt(v_hat) + ADAM_EPS) + ADAM_WD * w_refs[i][...])
            outs[3 + 4 * i][...] = nm
            outs[4 + 4 * i][...] = nv

    vmem = pl.BlockSpec(memory_space=pltpu.VMEM)
    out_shape = [jax.ShapeDtypeStruct((1, 128), F32)]
    for a in w:
        out_shape += [jax.ShapeDtypeStruct(a.shape, F32)] * 4
    outs = pl.pallas_call(
        body, name="adamw_small", in_specs=[vmem] * (1 + 3 * n), out_specs=[vmem] * (1 + 4 * n), out_shape=out_shape,
        scratch_shapes=[pltpu.VMEM((SMALL_ROWS, 128), F32)],
    )(packed_parts, *w, *m, *v)
    return outs[0], [outs[1 + 4 * i:5 + 4 * i] for i in range(n)]


WEIGHTS = ["norm_w", "ada_w", "ada_b", "w_in", "mla_q_norm", "w_uq", "mla_kv_norm", "w_ukv", "gla_w_g2",
           "gla_b_g2", "gla_norm", "w_out", "final_norm"]


def kernel(x, c, positions, norm_w, ada_w, ada_b, w_in, mla_q_norm, w_uq, mla_kv_norm, w_ukv, gla_w_g2, gla_b_g2, gla_norm, w_out, final_norm, loss_target, m_norm_w, m_ada_w, m_ada_b, m_w_in, m_mla_q_norm, m_w_uq, m_mla_kv_norm, m_w_ukv, m_gla_w_g2, m_gla_b_g2, m_gla_norm, m_w_out, m_final_norm, v_norm_w, v_ada_w, v_ada_b, v_w_in, v_mla_q_norm, v_w_uq, v_mla_kv_norm, v_w_ukv, v_gla_w_g2, v_gla_b_g2, v_gla_norm, v_w_out, v_final_norm):
    w = dict(norm_w=norm_w, ada_w=ada_w, ada_b=ada_b, w_in=w_in, mla_q_norm=mla_q_norm, w_uq=w_uq,
             mla_kv_norm=mla_kv_norm, w_ukv=w_ukv, gla_w_g2=gla_w_g2, gla_b_g2=gla_b_g2, gla_norm=gla_norm,
             w_out=w_out, final_norm=final_norm)
    m = dict(norm_w=m_norm_w, ada_w=m_ada_w, ada_b=m_ada_b, w_in=m_w_in, mla_q_norm=m_mla_q_norm, w_uq=m_w_uq,
             mla_kv_norm=m_mla_kv_norm, w_ukv=m_w_ukv, gla_w_g2=m_gla_w_g2, gla_b_g2=m_gla_b_g2,
             gla_norm=m_gla_norm, w_out=m_w_out, final_norm=m_final_norm)
    v = dict(norm_w=v_norm_w, ada_w=v_ada_w, ada_b=v_ada_b, w_in=v_w_in, mla_q_norm=v_mla_q_norm, w_uq=v_w_uq,
             mla_kv_norm=v_mla_kv_norm, w_ukv=v_w_ukv, gla_w_g2=v_gla_w_g2, gla_b_g2=v_gla_b_g2,
             gla_norm=v_gla_norm, w_out=v_w_out, final_norm=v_final_norm)
    B, S, D = x.shape
    me = 4 * lax.axis_index("x") + 2 * lax.axis_index("y") + lax.axis_index("c")
    ada_cols = ada_w.shape[2]
    cast = lambda a: a.astype(_MXU)

    sharded = ["w_in", "w_uq", "w_ukv", "w_out"]

    whole_cols = lambda a: jnp.transpose(a, (1, 0, 2)).reshape(a.shape[1], -1)
    whole_in = lambda blk: _arrange_w_in(whole_cols(blk))
    whole_rest = lambda blks: (_arrange_w_uq(whole_cols(blks[0])), _arrange_w_ukv(whole_cols(blks[1])),
                               blks[2].reshape(D, D))
    col_blocks = lambda a: jnp.transpose(a.reshape(a.shape[0], N_DEV, -1), (1, 0, 2)).astype(jnp.bfloat16)
    blocks_in = lambda dw_in_a: col_blocks(_unarrange_w_in(dw_in_a))
    blocks_rest = lambda dw_out, dw_uq_a, dw_ukv_a: [
        col_blocks(_unarrange_w_uq(dw_uq_a)), col_blocks(_unarrange_w_ukv(dw_ukv_a)),
        dw_out.reshape(N_DEV, D // N_DEV, D).astype(jnp.bfloat16)]

    got0 = _exchange([c, cast(w_in[0])], [True, True], "gather_first")
    c_all = got0[0].reshape(N_DEV * B, D)
    flight_r = _exchange_start([cast(w[n][0]) for n in sharded[1:]], [True] * 3, "gather_start_layer0")
    flight_w = _exchange_start([cast(w[n][1]) for n in sharded], [True] * 4, "gather_start_layer1")
    started = flight_r["token"][0, 0] + flight_w["token"][0, 0]

    ada_b_cols = lax.dynamic_slice(ada_b, (0, me * ada_cols), (DEPTH, ada_cols))
    mod_cols = _ada_fwd(c_all, ada_w, ada_b_cols)
    mod_send = jnp.transpose(mod_cols.reshape(DEPTH, N_DEV, B, ada_cols), (1, 0, 2, 3))
    (mod_recv,) = _exchange([mod_send], [False], "scatter_mod")
    mod = jnp.transpose(mod_recv, (1, 2, 0, 3)).reshape(DEPTH, B, 3 * D)

    small_w = {n: w[n] for n in SMALL}
    layer_small = lambda l: {n: a[l] for n, a in small_w.items() if n != "final_norm"}
    tabs = _rope_tables(positions.reshape(B, S, 1))
    late0 = lambda after: whole_rest(_exchange_wait(flight_r, after, me, "gather_wait_layer0"))
    x1, saved0 = _layer_fwd(x, tabs, mod[0] + started, layer_small(0), whole_in(got0[1]), late_weights=late0)
    got1 = _exchange_wait(flight_w, x1, me, "gather_wait_layer1")
    x2, saved1 = _layer_fwd(x1, tabs, mod[1], layer_small(1), whole_in(got1[0]), *whole_rest(got1[1:]))
    dx, loss, d_fw = _final_loss(x2, final_norm.reshape(1, D), loss_target)

    dx, g1 = _layer_bwd(dx, saved1, tabs)
    flight_g = _exchange_start([blocks_in(g1["w_in_a"])] + blocks_rest(g1["w_out"], g1["w_uq_a"], g1["w_ukv_a"]),
                               [False] * 4, "grads_start_layer1")
    flights = {}

    def early0(dw_out, dw_uq_a, dw_ukv_a):
        flights["rest0"] = _exchange_start(blocks_rest(dw_out, dw_uq_a, dw_ukv_a), [False] * 3, "grads_start_layer0")
        return flights["rest0"]["token"][0, 0]

    saved0 = dict(saved0, gate=saved0["gate"] + flight_g["token"][0, 0])
    grad_x, g0 = _layer_bwd(dx, saved0, tabs, early_grads=early0)
    parts1 = _exchange_wait(flight_g, grad_x, me, "grads_wait_layer1")
    rest0 = _exchange_wait(flights["rest0"], g0["w_in_a"], me, "grads_wait_layer0")

    both = lambda n: jnp.stack([g0[n], g1[n]])
    d_mod = both("d_mod")
    part = dict(norm_w=both("norm_w"), mla_q_norm=both("mla_q_norm"), mla_kv_norm=both("mla_kv_norm"),
                gla_w_g2=both("gla_w_g2"), gla_b_g2=both("gla_b_g2"), gla_norm=both("gla_norm256")[:, 0:128],
                final_norm=d_fw)
    d_mod_g, small_g, in0 = _exchange([d_mod, _pack_small(loss, part), blocks_in(g0["w_in_a"])],
                                      [True, True, False], "exchange_last")
    parts0 = [in0] + rest0

    d_mod_all = jnp.transpose(d_mod_g, (1, 0, 2, 3)).reshape(DEPTH, N_DEV * B, 3 * D)
    d_mod_cols = lax.dynamic_slice(d_mod_all, (0, 0, me * ada_cols), (DEPTH, N_DEV * B, ada_cols))
    g_ada_w = _ada_bwd(c_all, d_mod_cols)

    res = {}

    def update(name, parts2d):
        shp = w[name].shape
        two = lambda a: a.reshape(parts2d.shape[1:])
        out = _sum_adamw(parts2d, two(w[name]), two(m[name]), two(v[name]), "adamw_" + name)
        res[name] = [o.reshape(shp) for o in out]

    update("ada_w", g_ada_w.reshape(1, DEPTH * D, ada_cols))
    update("ada_b", jnp.transpose(d_mod_g, (0, 2, 1, 3)).reshape(N_DEV * B, DEPTH * 3 * D // 128, 128))
    for a, name in enumerate(sharded):
        update(name, jnp.concatenate([parts0[a], parts1[a]], axis=1))
    row = lambda a: a.reshape(1, D) if a.ndim == 1 else a
    loss_sum, small_out = _small_adamw(small_g, [row(w[n]) for n in SMALL], [row(m[n]) for n in SMALL],
                                       [row(v[n]) for n in SMALL])
    for n, outs in zip(SMALL, small_out):
        res[n] = [o.reshape(w[n].shape) for o in outs]
    loss_out = loss_sum[0, 0]
    return (loss_out, grad_x, *[res[n][0] for n in WEIGHTS], *[res[n][1] for n in WEIGHTS],
            *[res[n][2] for n in WEIGHTS], *[res[n][3] for n in WEIGHTS])
```

```python
import functools
import math

import numpy as np
import jax
import jax.numpy as jnp
from jax import lax
from jax.experimental import pallas as pl
from jax.experimental.pallas import tpu as pltpu

F32 = jnp.float32
_MXU = jnp.bfloat16

D_MODEL = 1024
DEPTH = 2
CHUNK = 64
EPS = 1e-6
ROPE_THETA = 10000.0
N_DEV = 8

MLA_SCALE = 96.0 ** -0.5
RET_KSCALE = 64.0 ** -0.5
GLA_KSCALE = 32.0 ** -0.5
GLA_TAU = 16.0

ADAM_LR = 0.001
ADAM_B1 = 0.9
ADAM_B2 = 0.999
ADAM_EPS = 1e-08
ADAM_WD = 0.01
ADAM_STEP = 10

RET_W, MLA_W, GLA_W = 1024, 1024, 896
ARR_W = RET_W + MLA_W + GLA_W
VMEM_MB = 1024 * 1024


def _cp(sem, vmem_mb=48):
    return pltpu.CompilerParams(dimension_semantics=sem, vmem_limit_bytes=vmem_mb * VMEM_MB)


def _mm(a, b):
    return jnp.dot(a.astype(_MXU), b.astype(_MXU), preferred_element_type=F32)


def _mm_nt(a, b):
    return lax.dot_general(a.astype(_MXU), b.astype(_MXU), (((1,), (1,)), ((), ())),
                           preferred_element_type=F32)


def _mm_tn(a, b):
    return lax.dot_general(a.astype(_MXU), b.astype(_MXU), (((0,), (0,)), ((), ())),
                           preferred_element_type=F32)


def _mm_f32(a, b):
    return jnp.dot(a, b, precision=lax.Precision.HIGHEST, preferred_element_type=F32)


def _sig(z):
    return 1.0 / (1.0 + jnp.exp(-z))


def _silu(z):
    return z * _sig(z)


def _dsilu(z):
    s = _sig(z)
    return s * (1.0 + z * (1.0 - s))


def _full(shape):
    nd = len(shape)
    return pl.BlockSpec(shape, lambda *_: (0,) * nd)


def _qk_perm(blk):
    r = blk.shape[0]
    return jnp.transpose(blk.reshape(r, 4, 2, 32), (0, 2, 1, 3)).reshape(r, 256)


def _qk_unperm(blk):
    r = blk.shape[0]
    return jnp.transpose(blk.reshape(r, 2, 4, 32), (0, 2, 1, 3)).reshape(r, 256)


def _arrange_w_in(w):
    z = lambda n: jnp.zeros((w.shape[0], n), w.dtype)
    ret = [_qk_perm(w[:, 0:256]), _qk_perm(w[:, 256:512]), w[:, 512:768], w[:, 768:1024]]
    mla = [w[:, 1024:1280], w[:, 1280:1408], z(64), w[:, 1408:1440], z(32), w[:, 1440:1952]]
    gla = [w[:, 1952:2080], w[:, 2080:2208], w[:, 2208:2464], w[:, 2464:2480], z(112), w[:, 2480:2736]]
    return jnp.concatenate(ret + mla + gla, axis=1)


def _unarrange_w_in(a):
    m, g = RET_W, RET_W + MLA_W
    parts = [_qk_unperm(a[:, 0:256]), _qk_unperm(a[:, 256:512]), a[:, 512:1024],
             a[:, m:m + 384], a[:, m + 448:m + 480], a[:, m + 512:m + 1024],
             a[:, g:g + 528], a[:, g + 640:g + 896]]
    return jnp.concatenate(parts, axis=1)


def _arrange_w_uq(w):
    return jnp.pad(w.reshape(256, 8, 96), ((0, 0), (0, 0), (0, 32))).reshape(256, 1024)


def _unarrange_w_uq(a):
    return a.reshape(256, 8, 128)[:, :, :96].reshape(256, 768)


def _arrange_w_ukv(w):
    r = w.reshape(128, 8, 128)
    k = jnp.pad(r[:, :, :64], ((0, 0), (0, 0), (0, 64))).reshape(128, 1024)
    return jnp.concatenate([k, r[:, :, 64:].reshape(128, 512)], axis=1)


def _unarrange_w_ukv(a):
    k = a[:, :1024].reshape(128, 8, 128)[:, :, :64]
    v = a[:, 1024:].reshape(128, 8, 64)
    return jnp.concatenate([k, v], axis=2).reshape(128, 1024)


def _rope_tables(pos3):
    B, S, _ = pos3.shape
    ts = min(S, 512)
    inv32 = (np.float32(ROPE_THETA) ** (-(np.arange(32, dtype=np.float32) / 32))).astype(np.float32)
    inv16 = (np.float32(ROPE_THETA) ** (-(np.arange(16, dtype=np.float32) / 16))).astype(np.float32)
    inv_r = np.tile(inv32, 4)[None, :]
    inv_m = np.zeros((1, 128), np.float32)
    inv_m[0, 64:80] = inv16
    inv_m[0, 80:96] = inv16

    def body(pos_ref, ir_ref, im_ref, cr, sr, cm, sm):
        p = pos_ref[0].astype(F32)
        ar = p * ir_ref[...]
        cr[0] = jnp.cos(ar)
        sr[0] = jnp.sin(ar)
        am = p * im_ref[...]
        cm[0] = jnp.cos(am)
        sm[0] = jnp.sin(am)

    tab = jax.ShapeDtypeStruct((B, S, 128), F32)
    blk = pl.BlockSpec((1, ts, 128), lambda b, i: (b, i, 0))
    return pl.pallas_call(
        body, name="rope_tables", grid=(B, S // ts),
        in_specs=[pl.BlockSpec((1, ts, 1), lambda b, i: (b, i, 0)), _full((1, 128)), _full((1, 128))],
        out_specs=[blk, blk, blk, blk], out_shape=[tab, tab, tab, tab],
        compiler_params=_cp(("parallel", "parallel")),
    )(pos3, jnp.asarray(inv_r), jnp.asarray(inv_m))


def _rope128(x, cos, sin):
    lane = lax.broadcasted_iota(jnp.int32, (1, 128), 1)
    rp = pltpu.roll(x, 16, 1)
    rm = pltpu.roll(x, 112, 1)
    return x * cos + jnp.where(lane < 80, -rm, rp) * sin


def _rope128_t(d, cos, sin):
    lane = lax.broadcasted_iota(jnp.int32, (1, 128), 1)
    y = d * sin
    yp = pltpu.roll(y, 16, 1)
    ym = pltpu.roll(y, 112, 1)
    return d * cos + jnp.where(lane < 64, 0.0, jnp.where(lane < 80, ym, jnp.where(lane < 96, -yp, 0.0)))


def _proj_fwd(x, shift, scale, nw, w_arr):
    B, S, D = x.shape
    tm = min(S, 512)

    def body(x_ref, sh_ref, sc_ref, nw_ref, w_ref, ret_ref, mla_ref, gla_ref, h_ref):
        xv = x_ref[0]
        rstd = lax.rsqrt(jnp.mean(xv * xv, axis=-1, keepdims=True) + EPS)
        h = (xv * rstd * nw_ref[...]) * (1.0 + sc_ref[0]) + sh_ref[0]
        hb = h.astype(_MXU)
        h_ref[0] = hb
        ret_ref[0] = jnp.dot(hb, w_ref[:, 0:RET_W], preferred_element_type=F32).astype(_MXU)
        mla_ref[0] = jnp.dot(hb, w_ref[:, RET_W:RET_W + MLA_W], preferred_element_type=F32).astype(_MXU)
        gla_ref[0] = jnp.dot(hb, w_ref[:, RET_W + MLA_W:ARR_W], preferred_element_type=F32).astype(_MXU)

    tok = lambda w: pl.BlockSpec((1, tm, w), lambda b, i: (b, i, 0))
    per_seq = pl.BlockSpec((1, 1, D), lambda b, i: (b, 0, 0))
    return pl.pallas_call(
        body, name="proj_fwd", grid=(B, S // tm),
        in_specs=[tok(D), per_seq, per_seq, _full((1, D)), _full((D, ARR_W))],
        out_specs=[tok(RET_W), tok(MLA_W), tok(GLA_W), tok(D)],
        out_shape=[jax.ShapeDtypeStruct((B, S, RET_W), _MXU), jax.ShapeDtypeStruct((B, S, MLA_W), _MXU),
                   jax.ShapeDtypeStruct((B, S, GLA_W), _MXU), jax.ShapeDtypeStruct((B, S, D), _MXU)],
        compiler_params=_cp(("parallel", "parallel")),
    )(x, shift, scale, nw, w_arr)


RET_L = 256


def _ret_consts(L):
    lg = np.log1p(-np.exp2(-5.0 - np.arange(4, dtype=np.float32))).astype(np.float32)
    i = np.arange(L)
    ci = i // CHUNK
    diff = (i[:, None] - i[None, :]).astype(np.float32)
    same = ci[:, None] == ci[None, :]
    past = ci[None, :] < ci[:, None]
    expo = np.where(same, np.abs(diff), np.where(past, diff, 0.0)).astype(np.float32)
    dec = np.where((same | past)[None], np.exp(lg[:, None, None] * expo[None]), 0.0).astype(np.float32)
    head = (np.arange(256) % 128) // 32
    qw = np.exp((i + 1.0)[:, None] * lg[head][None, :]).astype(np.float32)
    kw = np.exp((L - 1.0 - i)[:, None] * lg[head][None, :]).astype(np.float32)
    a_row = np.exp(np.float32(L) * lg[head])[None, :].astype(np.float32)
    return [jnp.asarray(t) for t in (dec.reshape(4 * L, L), qw, kw, a_row)]


def _ret_masks():
    lane = lax.broadcasted_iota(jnp.int32, (1, 256), 1)
    mh = [((lane % 128) // 32) == h for h in range(4)]
    mv = [(lane // 64) == h for h in range(4)]
    vi = lax.broadcasted_iota(jnp.int32, (256, 256), 0)
    ki = lax.broadcasted_iota(jnp.int32, (256, 256), 1)
    bd = (vi // 64) == ((ki % 128) // 32)
    return mh, mv, bd


def _ret_rope(p, cs, sn):
    q1, q2, k1, k2 = p[:, 0:128], p[:, 128:256], p[:, 256:384], p[:, 384:512]
    qr = jnp.concatenate([q1 * cs - q2 * sn, q2 * cs + q1 * sn], axis=1)
    kr = jnp.concatenate([k1 * cs - k2 * sn, k2 * cs + k1 * sn], axis=1) * RET_KSCALE
    return qr, kr


def _head_mean(x, mv, width):
    out = jnp.zeros_like(x)
    for m in mv:
        s = jnp.sum(jnp.where(m, x, 0.0), axis=-1, keepdims=True) * (1.0 / width)
        out = jnp.where(m, s, out)
    return out


def _stack_heads(x, masks):
    return jnp.concatenate([jnp.where(m, x, 0.0) for m in masks], axis=0)


def _fold_heads(xs, masks, L):
    out = jnp.where(masks[0], xs[0:L], 0.0)
    for h in range(1, 4):
        out = out + jnp.where(masks[h], xs[h * L:(h + 1) * L], 0.0)
    return out


def _ret_fwd(ret_p, cos, sin):
    B, S, _ = ret_p.shape
    L = min(RET_L, S)
    NB = S // L
    consts = _ret_consts(L)

    def body(p_ref, c_ref, s_ref, ds_ref, qw_ref, kw_ref, a_ref, out_ref, raw_ref, st_ref, st_sc):
        @pl.when(pl.program_id(1) == 0)
        def _():
            st_sc[...] = jnp.zeros_like(st_sc)

        mh, mv, bd = _ret_masks()
        p = p_ref[0].astype(F32)
        qr, kr = _ret_rope(p, c_ref[0], s_ref[0])
        v = p[:, 512:768]
        z = p[:, 768:1024]
        a_s = _mm_nt(_stack_heads(qr, mh), kr) * ds_ref[...]
        intra = _fold_heads(_mm(a_s, v), mv, L)
        st = st_sc[...]
        st_ref[0, 0] = st
        r = intra + _mm_nt(qr * qw_ref[...], st)
        raw_ref[0] = r
        st_sc[...] = st * a_ref[...] + jnp.where(bd, _mm_tn(v, kr * kw_ref[...]), 0.0)
        rstd = lax.rsqrt(_head_mean(r * r, mv, 64.0) + EPS)
        out_ref[0] = (r * rstd * _silu(z)).astype(_MXU)

    tok = lambda w: pl.BlockSpec((1, L, w), lambda b, n: (b, n, 0))
    return pl.pallas_call(
        body, name="ret_fwd", grid=(B, NB),
        in_specs=[tok(RET_W), tok(128), tok(128), _full((4 * L, L)), _full((L, 256)), _full((L, 256)),
                  _full((1, 256))],
        out_specs=[tok(256), tok(256), pl.BlockSpec((1, 1, 256, 256), lambda b, n: (b, n, 0, 0))],
        out_shape=[jax.ShapeDtypeStruct((B, S, 256), _MXU), jax.ShapeDtypeStruct((B, S, 256), F32),
                   jax.ShapeDtypeStruct((B, NB, 256, 256), F32)],
        scratch_shapes=[pltpu.VMEM((256, 256), F32)],
        compiler_params=_cp(("parallel", "arbitrary")),
    )(ret_p, cos, sin, *consts)


def _ret_bwd(ret_p, cos, sin, raw, states, d_mix):
    B, S, _ = ret_p.shape
    L = min(RET_L, S)
    NB = S // L
    consts = _ret_consts(L)

    def body(p_ref, c_ref, s_ref, raw_ref, st_ref, dm_ref, ds_ref, qw_ref, kw_ref, a_ref, dp_ref, dst_sc):
        @pl.when(pl.program_id(1) == 0)
        def _():
            dst_sc[...] = jnp.zeros_like(dst_sc)

        mh, mv, bd = _ret_masks()
        p = p_ref[0].astype(F32)
        cs, sn = c_ref[0], s_ref[0]
        qr, kr = _ret_rope(p, cs, sn)
        v = p[:, 512:768]
        z = p[:, 768:1024]
        qs = _stack_heads(qr, mh)
        dec = ds_ref[...]
        a_s = _mm_nt(qs, kr) * dec
        r = raw_ref[0]
        rstd = lax.rsqrt(_head_mean(r * r, mv, 64.0) + EPS)
        rn = r * rstd
        dm = dm_ref[0]
        d_rn = dm * _silu(z)
        dz = dm * rn * _dsilu(z)
        dr = rstd * (d_rn - rn * _head_mean(d_rn * rn, mv, 64.0))
        do_s = _stack_heads(dr, mv)
        da_s = _mm_nt(do_s, v) * dec
        dv = _mm_tn(a_s, do_s)
        dqr = _fold_heads(_mm(da_s, kr), mh, L)
        dkr = _mm_tn(da_s, qs)
        st = st_ref[0, 0]
        qw, kw = qw_ref[...], kw_ref[...]
        dqr = dqr + _mm(dr, st) * qw
        dst_next = dst_sc[...]
        g = jnp.where(bd, dst_next, 0.0)
        kk = kr * kw
        dv = dv + _mm_nt(kk, g)
        dkr = dkr + _mm(v, g) * kw
        dst_sc[...] = dst_next * a_ref[...] + jnp.where(bd, _mm_tn(dr, qr * qw), 0.0)
        dkr = dkr * RET_KSCALE
        dq1, dq2 = dqr[:, 0:128], dqr[:, 128:256]
        dk1, dk2 = dkr[:, 0:128], dkr[:, 128:256]
        dp_ref[0] = jnp.concatenate(
            [dq1 * cs + dq2 * sn, dq2 * cs - dq1 * sn, dk1 * cs + dk2 * sn, dk2 * cs - dk1 * sn, dv, dz],
            axis=1).astype(_MXU)

    tok = lambda w: pl.BlockSpec((1, L, w), lambda b, i: (b, NB - 1 - i, 0))
    return pl.pallas_call(
        body, name="ret_bwd", grid=(B, NB),
        in_specs=[tok(RET_W), tok(128), tok(128), tok(256),
                  pl.BlockSpec((1, 1, 256, 256), lambda b, i: (b, NB - 1 - i, 0, 0)), tok(256),
                  _full((4 * L, L)), _full((L, 256)), _full((L, 256)), _full((1, 256))],
        out_specs=tok(RET_W), out_shape=jax.ShapeDtypeStruct((B, S, RET_W), _MXU),
        scratch_shapes=[pltpu.VMEM((256, 256), F32)],
        compiler_params=_cp(("parallel", "arbitrary")),
    )(ret_p, cos, sin, raw, states, d_mix, *consts)


def _gla_masks():
    C = CHUNK
    lk = lax.broadcasted_iota(jnp.int32, (1, 128), 1)
    lv = lax.broadcasted_iota(jnp.int32, (1, 256), 1)
    mk = [(lk // 32) == h for h in range(4)]
    mv = [(lv // 64) == h for h in range(4)]
    vi = lax.broadcasted_iota(jnp.int32, (256, 128), 0)
    ki = lax.broadcasted_iota(jnp.int32, (256, 128), 1)
    bd = (vi // 64) == (ki // 32)
    ri = lax.broadcasted_iota(jnp.int32, (4 * C, C), 0) % C
    cj = lax.broadcasted_iota(jnp.int32, (4 * C, C), 1)
    lower = ri >= cj
    ti = lax.broadcasted_iota(jnp.int32, (C, C), 0)
    tj = lax.broadcasted_iota(jnp.int32, (C, C), 1)
    ltri = jnp.where(ti >= tj, 1.0, 0.0).astype(F32)
    utri = jnp.where(ti <= tj, 1.0, 0.0).astype(F32)
    return mk, mv, bd, lower, ltri, utri


def _log_sigmoid(x):
    return jnp.minimum(x, 0.0) - jnp.log(1.0 + jnp.exp(-jnp.abs(x)))


GLA_G = 8


def _gla_fwd(gla_p, w_g2p, b_g2, gnw):
    B, S, _ = gla_p.shape
    C = CHUNK
    NC = S // C
    G = min(GLA_G, NC)
    NG = NC // G

    def body(p_ref, w_ref, b_ref, gn_ref, out_ref, raw_ref, st_ref, st_sc):
        @pl.when(pl.program_id(1) == 0)
        def _():
            st_sc[...] = jnp.zeros_like(st_sc)

        mk, mv, bd, lower, ltri, _ = _gla_masks()
        cs = range(G)
        rows = [slice(c * C, (c + 1) * C) for c in cs]
        ps = [p_ref[0, rows[c], :].astype(F32) for c in cs]
        pre = [_mm(ps[c][:, 512:640], w_ref[...]) + b_ref[...] for c in cs]
        cum = [_mm_f32(ltri, _log_sigmoid(pre[c]) * (1.0 / GLA_TAU)) for c in cs]
        past, fut, upd, q_pos, a_row = [], [], [], [], []
        for c in cs:
            q = ps[c][:, 0:128]
            k = ps[c][:, 128:256] * GLA_KSCALE
            last = cum[c][C - 1:C, :]
            e_pos = jnp.exp(cum[c])
            e_neg = jnp.exp(-cum[c])
            q_pos.append(q * e_pos)
            a_row.append(jnp.exp(last))
            past.append(_mm_nt(_stack_heads(q_pos[c], mk), k * e_neg))
            fut.append(_mm_nt(_stack_heads(q * e_neg, mk), k * e_pos))
            upd.append(_mm_tn(ps[c][:, 256:512], k * jnp.exp(last - cum[c])))
        o_s = [_mm(jnp.where(lower, past[c], fut[c]), ps[c][:, 256:512]) for c in cs]
        st = st_sc[...]
        inter = []
        for c in cs:
            st_ref[0, c] = st
            inter.append(_mm_nt(q_pos[c], st))
            st = st * a_row[c] + jnp.where(bd, upd[c], 0.0)
        st_sc[...] = st
        for c in cs:
            g = _fold_heads(o_s[c], mv, C) + inter[c]
            raw_ref[0, rows[c], :] = g
            rstd = lax.rsqrt(_head_mean(g * g, mv, 64.0) + EPS)
            out_ref[0, rows[c], :] = (g * rstd * gn_ref[...] * _silu(ps[c][:, 640:896])).astype(_MXU)

    tok = lambda w: pl.BlockSpec((1, G * C, w), lambda b, n: (b, n, 0))
    return pl.pallas_call(
        body, name="gla_fwd", grid=(B, NG),
        in_specs=[tok(GLA_W), _full((128, 128)), _full((1, 128)), _full((1, 256))],
        out_specs=[tok(256), tok(256), pl.BlockSpec((1, G, 256, 128), lambda b, n: (b, n, 0, 0))],
        out_shape=[jax.ShapeDtypeStruct((B, S, 256), _MXU), jax.ShapeDtypeStruct((B, S, 256), F32),
                   jax.ShapeDtypeStruct((B, NC, 256, 128), F32)],
        scratch_shapes=[pltpu.VMEM((256, 128), F32)],
        compiler_params=_cp(("parallel", "arbitrary")),
    )(gla_p, w_g2p, b_g2, gnw)


def _gla_bwd(gla_p, w_g2p, b_g2, gnw, raw, states, d_mix):
    B, S, _ = gla_p.shape
    C = CHUNK
    NC = S // C
    G = min(GLA_G, NC)
    NG = NC // G

    def body(p_ref, w_ref, b_ref, gn_ref, raw_ref, st_ref, dm_ref, dp_ref, dw_ref, db_ref, dgn_ref, dst_sc):
        first = (pl.program_id(0) == 0) & (pl.program_id(1) == 0)

        @pl.when(first)
        def _():
            dw_ref[...] = jnp.zeros_like(dw_ref)
            db_ref[...] = jnp.zeros_like(db_ref)
            dgn_ref[...] = jnp.zeros_like(dgn_ref)

        @pl.when(pl.program_id(1) == 0)
        def _():
            dst_sc[...] = jnp.zeros_like(dst_sc)

        mk, mv, bd, lower, ltri, utri = _gla_masks()
        gn = gn_ref[...]
        cs = range(G)
        rows = [slice(c * C, (c + 1) * C) for c in cs]
        ps = [p_ref[0, rows[c], :].astype(F32) for c in cs]
        vs = [ps[c][:, 256:512] for c in cs]
        pre = [_mm(ps[c][:, 512:640], w_ref[...]) + b_ref[...] for c in cs]
        cum = [_mm_f32(ltri, _log_sigmoid(pre[c]) * (1.0 / GLA_TAU)) for c in cs]
        dg, dz, dgn_acc = [], [], jnp.zeros((1, 256), F32)
        for c in cs:
            g = raw_ref[0, rows[c], :]
            z = ps[c][:, 640:896]
            rstd = lax.rsqrt(_head_mean(g * g, mv, 64.0) + EPS)
            gh = g * rstd
            dm = dm_ref[0, rows[c], :]
            d_gn = dm * _silu(z)
            dz.append(dm * gh * gn * _dsilu(z))
            dgn_acc = dgn_acc + jnp.sum(d_gn * gh, axis=0, keepdims=True)
            d_gh = d_gn * gn
            dg.append(rstd * (d_gh - gh * _head_mean(d_gh * gh, mv, 64.0)))
        do_s = [_stack_heads(dg[c], mv) for c in cs]
        dattn = [_mm_nt(do_s[c], vs[c]) for c in cs]
        ks, e_pos, e_neg, q_pos, q_neg, k_pos, k_neg, qp_s, qn_s, past, fut, a_row, w_dec, kd = ([] for _ in range(14))
        for c in cs:
            q = ps[c][:, 0:128]
            k = ps[c][:, 128:256] * GLA_KSCALE
            last = cum[c][C - 1:C, :]
            ep, en = jnp.exp(cum[c]), jnp.exp(-cum[c])
            ks.append(k), e_pos.append(ep), e_neg.append(en)
            q_pos.append(q * ep), q_neg.append(q * en), k_pos.append(k * ep), k_neg.append(k * en)
            qp_s.append(_stack_heads(q_pos[c], mk)), qn_s.append(_stack_heads(q_neg[c], mk))
            past.append(_mm_nt(qp_s[c], k_neg[c]))
            fut.append(_mm_nt(qn_s[c], k_pos[c]))
            a_row.append(jnp.exp(last))
            w_dec.append(jnp.exp(last - cum[c]))
            kd.append(k * w_dec[c])
        sts = [st_ref[0, c] for c in cs]
        dq_st = [_mm(dg[c], sts[c]) for c in cs]
        dst_in = [_mm_tn(dg[c], q_pos[c]) for c in cs]
        dv, dq_pos, dk_neg, dq_neg, dk_pos = [], [], [], [], []
        for c in cs:
            attn = jnp.where(lower, past[c], fut[c])
            dpast = jnp.where(lower, dattn[c], 0.0)
            dfut = jnp.where(lower, 0.0, dattn[c])
            dv.append(_mm_tn(attn, do_s[c]))
            dq_pos.append(_fold_heads(_mm(dpast, k_neg[c]), mk, C) + dq_st[c])
            dk_neg.append(_mm_tn(dpast, qp_s[c]))
            dq_neg.append(_fold_heads(_mm(dfut, k_pos[c]), mk, C))
            dk_pos.append(_mm_tn(dfut, qn_s[c]))
        dst_next = dst_sc[...]
        d_a, d_kd = [None] * G, [None] * G
        for c in reversed(cs):
            d_a[c] = jnp.sum(dst_next * sts[c], axis=0, keepdims=True)
            gmat = jnp.where(bd, dst_next, 0.0)
            d_kd[c] = _mm(vs[c], gmat)
            dv[c] = dv[c] + _mm_nt(kd[c], gmat)
            dst_next = dst_next * a_row[c] + jnp.where(bd, dst_in[c], 0.0)
        dst_sc[...] = dst_next
        row = lax.broadcasted_iota(jnp.int32, (C, 128), 0)
        d_la, dk, dq = [], [], []
        for c in cs:
            t = d_kd[c] * kd[c]
            dk.append(d_kd[c] * w_dec[c] + dk_neg[c] * e_neg[c] + dk_pos[c] * e_pos[c])
            dq.append(dq_pos[c] * e_pos[c] + dq_neg[c] * e_neg[c])
            d_last = jnp.sum(t, axis=0, keepdims=True) + d_a[c] * a_row[c]
            d_cum = (dq_pos[c] * q_pos[c] - dk_neg[c] * k_neg[c] - dq_neg[c] * q_neg[c] + dk_pos[c] * k_pos[c] - t)
            d_la.append(_mm_f32(utri, d_cum + jnp.where(row == C - 1, d_last, 0.0)))
        d_pre = [d_la[c] * _sig(-pre[c]) * (1.0 / GLA_TAU) for c in cs]
        d_gg = [_mm_nt(d_pre[c], w_ref[...]) for c in cs]
        dw_acc = _mm_tn(ps[0][:, 512:640], d_pre[0])
        db_acc = jnp.sum(d_pre[0], axis=0, keepdims=True)
        for c in cs[1:]:
            dw_acc = dw_acc + _mm_tn(ps[c][:, 512:640], d_pre[c])
            db_acc = db_acc + jnp.sum(d_pre[c], axis=0, keepdims=True)
        for c in cs:
            dp_ref[0, rows[c], :] = jnp.concatenate([dq[c], dk[c] * GLA_KSCALE, dv[c], d_gg[c], dz[c]],
                                                    axis=1).astype(_MXU)
        dw_ref[...] += dw_acc
        db_ref[...] += db_acc
        dgn_ref[...] += dgn_acc

        @pl.when((pl.program_id(0) == B - 1) & (pl.program_id(1) == NG - 1))
        def _():
            s1 = dgn_ref[...]
            s1 = s1 + pltpu.roll(s1, 128, 1)
            dgn_ref[...] = s1 + pltpu.roll(s1, 64, 1)

    tok = lambda w: pl.BlockSpec((1, G * C, w), lambda b, i: (b, NG - 1 - i, 0))
    return pl.pallas_call(
        body, name="gla_bwd", grid=(B, NG),
        in_specs=[tok(GLA_W), _full((128, 128)), _full((1, 128)), _full((1, 256)), tok(256),
                  pl.BlockSpec((1, G, 256, 128), lambda b, i: (b, NG - 1 - i, 0, 0)), tok(256)],
        out_specs=[tok(GLA_W), _full((128, 128)), _full((1, 128)), _full((1, 256))],
        out_shape=[jax.ShapeDtypeStruct((B, S, GLA_W), _MXU), jax.ShapeDtypeStruct((128, 128), F32),
                   jax.ShapeDtypeStruct((1, 128), F32), jax.ShapeDtypeStruct((1, 256), F32)],
        scratch_shapes=[pltpu.VMEM((256, 128), F32)],
        compiler_params=_cp(("arbitrary", "arbitrary")),
    )(gla_p, w_g2p, b_g2, gnw, raw, states, d_mix)


def _rms(x, w):
    rstd = lax.rsqrt(jnp.mean(x * x, axis=-1, keepdims=True) + EPS)
    xh = x * rstd
    return xh, rstd, xh * w


def _rms_bwd(dy, xh, rstd, w):
    dxh = dy * w
    return rstd * (dxh - xh * jnp.mean(dxh * xh, axis=-1, keepdims=True))


MLA_T = 256


def _mla_prep_fwd(mla_p, cos, sin, qnw, kvnw, w_uq, w_ukv):
    B, S, _ = mla_p.shape
    tm = min(S, 512)

    t = min(MLA_T, S)
    nt = tm // t

    def body(p_ref, c_ref, s_ref, qn_ref, kn_ref, wq_ref, wkv_ref, q_ref, k_ref, v_ref, kt_ref, vt_ref):
        p = p_ref[0].astype(F32)
        cs, sn = c_ref[0], s_ref[0]
        _, _, qn = _rms(p[:, 0:256], qn_ref[...])
        qpre = _mm(qn, wq_ref[...])
        _, _, kvn = _rms(p[:, 256:384], kn_ref[...])
        kv = _mm(kvn, wkv_ref[...])
        kpe = _rope128(p[:, 384:512], cs, sn)
        for h in range(8):
            sl = slice(128 * h, 128 * h + 128)
            q_ref[0, :, sl] = _rope128(qpre[:, sl], cs, sn).astype(_MXU)
            kh = kv[:, sl] + kpe
            k_ref[0, :, sl] = kh.astype(_MXU)
            kht = kh.T
            for n in range(nt):
                kt_ref[0, n, sl, :] = kht[:, n * t:(n + 1) * t].astype(_MXU)
        v_ref[0] = kv[:, 1024:1536].astype(_MXU)
        for pr in range(4):
            vht = kv[:, 1024 + 128 * pr:1152 + 128 * pr].T
            for n in range(nt):
                vt_ref[0, n, 128 * pr:128 * pr + 128, :] = vht[:, n * t:(n + 1) * t].astype(_MXU)

    tok = lambda w: pl.BlockSpec((1, tm, w), lambda b, i: (b, i, 0))
    tr = lambda w: pl.BlockSpec((1, nt, w, t), lambda b, i: (b, i, 0, 0))
    return pl.pallas_call(
        body, name="mla_prep_fwd", grid=(B, S // tm),
        in_specs=[tok(512), tok(128), tok(128), _full((1, 256)), _full((1, 128)), _full((256, 1024)),
                  _full((128, 1536))],
        out_specs=[tok(1024), tok(1024), tok(512), tr(1024), tr(512)],
        out_shape=[jax.ShapeDtypeStruct((B, S, 1024), _MXU), jax.ShapeDtypeStruct((B, S, 1024), _MXU),
                   jax.ShapeDtypeStruct((B, S, 512), _MXU), jax.ShapeDtypeStruct((B, S // t, 1024, t), _MXU),
                   jax.ShapeDtypeStruct((B, S // t, 512, t), _MXU)],
        compiler_params=_cp(("parallel", "parallel")),
    )(mla_p, cos, sin, qnw, kvnw, w_uq, w_ukv)


def _chunk_mask_t(t):
    kj = lax.broadcasted_iota(jnp.int32, (t, t), 0) // CHUNK
    qi = lax.broadcasted_iota(jnp.int32, (t, t), 1) // CHUNK
    return kj <= qi


MLA_HG = 4
LOG2E = 1.4426950408889634
MLA_C2 = MLA_SCALE * LOG2E


def _mla_attn_fwd(q, k, vt):
    B, S, _ = q.shape
    t = min(MLA_T, S)
    nq = S // t
    HG = MLA_HG
    NP = HG // 2

    def body(q_ref, k_ref, vt_ref, o_ref, lse_ref, sa, sb, m_sc, l_sc, acc_sc):
        i = pl.program_id(2)
        row = lax.broadcasted_iota(jnp.int32, (128, 1), 0)
        low = row < 64
        mask = _chunk_mask_t(t)
        m_sc[...] = jnp.full(m_sc.shape, -jnp.inf, F32)
        l_sc[...] = jnp.zeros_like(l_sc)
        acc_sc[...] = jnp.zeros_like(acc_sc)

        def scores(j, buf):
            kb = k_ref[0, pl.ds(pl.multiple_of(j * t, t), t), :]
            for h in range(HG):
                cols = slice(128 * h, 128 * h + 128)
                buf[h] = _mm_nt(kb[:, cols], q_ref[0, :, cols]) * MLA_C2

        def absorb(j, buf, masked):
            vtb = vt_ref[0, j]
            for pr in range(NP):
                alphas, pvs = [], []
                for hh in range(2):
                    h = 2 * pr + hh
                    s = buf[h]
                    if masked:
                        s = jnp.where(mask, s, -jnp.inf)
                    m_old = m_sc[h]
                    m_new = jnp.maximum(m_old, jnp.max(s, axis=0, keepdims=True))
                    alpha = jnp.exp2(m_old - m_new)
                    p = jnp.exp2(s - m_new)
                    l_sc[h] = alpha * l_sc[h] + jnp.sum(p, axis=0, keepdims=True)
                    m_sc[h] = m_new
                    vth = vtb[128 * pr:128 * pr + 128, :]
                    vth = jnp.where(low if hh == 0 else ~low, vth, jnp.zeros_like(vth))
                    pvs.append(_mm(vth, p))
                    alphas.append(alpha)
                acc_sc[pr] = acc_sc[pr] * jnp.where(low, alphas[0], alphas[1]) + pvs[0] + pvs[1]

        scores(0, sb)

        def pair(jj, carry):
            j0 = 2 * jj
            scores(j0 + 1, sa)
            absorb(j0, sb, False)
            scores(j0 + 2, sb)
            absorb(j0 + 1, sa, False)
            return carry

        lax.fori_loop(0, i // 2, pair, 0)

        @pl.when(i % 2 == 1)
        def _():
            scores(i, sa)
            absorb(i - 1, sb, False)
            absorb(i, sa, True)

        @pl.when(i % 2 == 0)
        def _():
            absorb(i, sb, True)

        for pr in range(NP):
            l_e, l_o = l_sc[2 * pr], l_sc[2 * pr + 1]
            o_ref[0, :, 128 * pr:128 * pr + 128] = (acc_sc[pr] / jnp.where(low, l_e, l_o)).T
            lse_ref[0, pr, 0, 0:1, :] = m_sc[2 * pr] + jnp.log(l_e) * LOG2E
            lse_ref[0, pr, 0, 1:2, :] = m_sc[2 * pr + 1] + jnp.log(l_o) * LOG2E

    return pl.pallas_call(
        body, name="mla_attn_fwd", grid=(B, 8 // HG, nq),
        in_specs=[pl.BlockSpec((1, t, 128 * HG), lambda b, g, i: (b, i, g)),
                  pl.BlockSpec((1, S, 128 * HG), lambda b, g, i: (b, 0, g)),
                  pl.BlockSpec((1, nq, 64 * HG, t), lambda b, g, i: (b, 0, g, 0))],
        out_specs=[pl.BlockSpec((1, t, 64 * HG), lambda b, g, i: (b, i, g)),
                   pl.BlockSpec((1, NP, 1, 2, t), lambda b, g, i: (b, g, i, 0, 0))],
        out_shape=[jax.ShapeDtypeStruct((B, S, 512), F32), jax.ShapeDtypeStruct((B, 4, nq, 2, t), F32)],
        scratch_shapes=[pltpu.VMEM((HG, t, t), F32), pltpu.VMEM((HG, t, t), F32), pltpu.VMEM((HG, 1, t), F32),
                        pltpu.VMEM((HG, 1, t), F32), pltpu.VMEM((NP, 128, t), F32)],
        compiler_params=_cp(("parallel", "parallel", "arbitrary")),
    )(q, k, vt)


def _mla_gate_bwd(d_mix, o, mla_p):
    B, S, _ = o.shape
    tm = min(S, 512)
    t = min(MLA_T, S)
    nt = tm // t

    def body(dm_ref, o_ref, z_ref, do_ref, dz_ref, dl_ref):
        dm, ov, z = dm_ref[0], o_ref[0], z_ref[0].astype(F32)
        do = dm * _silu(z)
        dz_ref[0] = (dm * ov * _dsilu(z)).astype(_MXU)
        do_ref[0] = do.astype(_MXU)
        prod = do * ov
        for pr in range(4):
            pt = prod[:, 128 * pr:128 * pr + 128].T
            se = jnp.sum(pt[0:64], axis=0, keepdims=True)
            so = jnp.sum(pt[64:128], axis=0, keepdims=True)
            for n in range(nt):
                dl_ref[0, pr, n, 0:1, :] = se[:, n * t:(n + 1) * t]
                dl_ref[0, pr, n, 1:2, :] = so[:, n * t:(n + 1) * t]

    tok = lambda c: pl.BlockSpec((1, tm, 512), lambda b, i: (b, i, c))
    return pl.pallas_call(
        body, name="mla_gate_bwd", grid=(B, S // tm),
        in_specs=[tok(0), tok(0), tok(1)],
        out_specs=[tok(0), tok(0), pl.BlockSpec((1, 4, nt, 2, t), lambda b, i: (b, 0, i, 0, 0))],
        out_shape=[jax.ShapeDtypeStruct((B, S, 512), _MXU), jax.ShapeDtypeStruct((B, S, 512), _MXU),
                   jax.ShapeDtypeStruct((B, 4, S // t, 2, t), F32)],
        compiler_params=_cp(("parallel", "parallel")),
    )(d_mix, o, mla_p)


def _mla_attn_bwd(q, k, v, kt, do, lse, dl):
    B, S, _ = q.shape
    t = min(MLA_T, S)
    nk = S // t

    HG = MLA_HG
    NP = HG // 2

    def body(q_ref, k_ref, v_ref, kt_ref, do_ref, lse_ref, dl_ref, dq_ref, dk_ref, dv_ref,
             sa, da, sb, db, dqt_sc, dk_sc, dv_sc):
        j = pl.program_id(2)

        @pl.when(j == 0)
        def _():
            dqt_sc[...] = jnp.zeros_like(dqt_sc)

        dk_sc[...] = jnp.zeros_like(dk_sc)
        dv_sc[...] = jnp.zeros_like(dv_sc)
        lane = lax.broadcasted_iota(jnp.int32, (1, 128), 1)
        low = lane < 64
        mask = _chunk_mask_t(t)

        def half(x, hh):
            return jnp.where(low if hh == 0 else ~low, x, jnp.zeros_like(x))

        def prepare(i, sbuf, dbuf):
            rows = pl.ds(pl.multiple_of(i * t, t), t)
            for h in range(HG):
                cols = slice(128 * h, 128 * h + 128)
                pc = slice(128 * (h // 2), 128 * (h // 2) + 128)
                sbuf[h] = _mm_nt(k_ref[0, :, cols], q_ref[0, rows, cols]) * MLA_C2
                dbuf[h] = _mm_nt(half(v_ref[0, :, pc], h % 2), do_ref[0, rows, pc])

        def absorb(i, sbuf, dbuf, masked):
            rows = pl.ds(pl.multiple_of(i * t, t), t)
            for h in range(HG):
                pr, hh = h // 2, h % 2
                cols = slice(128 * h, 128 * h + 128)
                pc = slice(128 * pr, 128 * pr + 128)
                p = jnp.exp2(sbuf[h] - lse_ref[0, pr, i][hh:hh + 1, :])
                if masked:
                    p = jnp.where(mask, p, 0.0)
                dv_sc[pr] += _mm(p, half(do_ref[0, rows, pc], hh))
                ds = p * (dbuf[h] - dl_ref[0, pr, i][hh:hh + 1, :])
                dqt_sc[i, cols, :] += _mm(kt_ref[0, 0, cols, :], ds)
                dk_sc[h] += _mm(ds, q_ref[0, rows, cols])

        n = nk - 1 - j
        prepare(jnp.minimum(j + 1, nk - 1), sb, db)

        def pair(jj, carry):
            i0 = j + 1 + 2 * jj
            prepare(i0 + 1, sa, da)
            absorb(i0, sb, db, False)
            prepare(jnp.where(i0 + 2 <= nk - 1, i0 + 2, j), sb, db)
            absorb(i0 + 1, sa, da, False)
            return carry

        lax.fori_loop(0, n // 2, pair, 0)

        @pl.when(n % 2 == 1)
        def _():
            prepare(j, sa, da)
            absorb(nk - 1, sb, db, False)
            absorb(j, sa, da, True)

        @pl.when(n % 2 == 0)
        def _():
            absorb(j, sb, db, True)

        for h in range(HG):
            dk_ref[0, :, 128 * h:128 * h + 128] = (dk_sc[h] * MLA_SCALE).astype(_MXU)
        for pr in range(NP):
            dv_ref[0, :, 128 * pr:128 * pr + 128] = dv_sc[pr].astype(_MXU)

        @pl.when(j == nk - 1)
        def _():
            for i in range(nk):
                dq_ref[0, i * t:(i + 1) * t, :] = (dqt_sc[i].T * MLA_SCALE).astype(_MXU)

    seq = lambda w: pl.BlockSpec((1, S, w), lambda b, g, j: (b, 0, g))
    blk = lambda w: pl.BlockSpec((1, t, w), lambda b, g, j: (b, j, g))
    stat = pl.BlockSpec((1, NP, nk, 2, t), lambda b, g, j: (b, g, 0, 0, 0))
    return pl.pallas_call(
        body, name="mla_attn_bwd", grid=(B, 8 // HG, nk),
        in_specs=[seq(128 * HG), blk(128 * HG), blk(64 * HG),
                  pl.BlockSpec((1, 1, 128 * HG, t), lambda b, g, j: (b, j, g, 0)), seq(64 * HG), stat, stat],
        out_specs=[seq(128 * HG), blk(128 * HG), blk(64 * HG)],
        out_shape=[jax.ShapeDtypeStruct((B, S, 1024), _MXU), jax.ShapeDtypeStruct((B, S, 1024), _MXU),
                   jax.ShapeDtypeStruct((B, S, 512), _MXU)],
        scratch_shapes=[pltpu.VMEM((HG, t, t), F32), pltpu.VMEM((HG, t, t), F32), pltpu.VMEM((HG, t, t), F32),
                        pltpu.VMEM((HG, t, t), F32), pltpu.VMEM((nk, 128 * HG, t), F32),
                        pltpu.VMEM((HG, t, 128), F32), pltpu.VMEM((NP, t, 128), F32)],
        compiler_params=_cp(("parallel", "parallel", "arbitrary")),
    )(q, k, v, kt, do, lse, dl)


def _mla_prep_bwd(mla_p, cos, sin, qnw, kvnw, w_uq, w_ukv, dq, dk, dv):
    B, S, _ = mla_p.shape
    tm = min(S, 512)

    def body(p_ref, c_ref, s_ref, qn_ref, kn_ref, wq_ref, wkv_ref, dq_ref, dk_ref, dv_ref,
             dp_ref, dwq_ref, dwkv_ref, dqn_ref, dkn_ref):
        first = (pl.program_id(0) == 0) & (pl.program_id(1) == 0)

        @pl.when(first)
        def _():
            dwq_ref[...] = jnp.zeros_like(dwq_ref)
            dwkv_ref[...] = jnp.zeros_like(dwkv_ref)
            dqn_ref[...] = jnp.zeros_like(dqn_ref)
            dkn_ref[...] = jnp.zeros_like(dkn_ref)

        p = p_ref[0].astype(F32)
        cs, sn = c_ref[0], s_ref[0]
        lane = lax.broadcasted_iota(jnp.int32, (1, 128), 1)
        pe = (lane >= 64) & (lane < 96)
        qh, q_rstd, qn = _rms(p[:, 0:256], qn_ref[...])
        kvh, kv_rstd, kvn = _rms(p[:, 256:384], kn_ref[...])
        dqv = dq_ref[0].astype(F32)
        dkv = dk_ref[0].astype(F32)
        dqpre = jnp.concatenate(
            [_rope128_t(dqv[:, 128 * h:128 * h + 128], cs, sn) for h in range(8)], axis=1)
        dkpe = jnp.zeros((tm, 128), F32)
        for h in range(8):
            dkpe = dkpe + jnp.where(pe, dkv[:, 128 * h:128 * h + 128], 0.0)
        dkr = _rope128_t(dkpe, cs, sn)
        dkv_all = jnp.concatenate([dkv, dv_ref[0].astype(F32)], axis=1)
        d_qn = _mm_nt(dqpre, wq_ref[...])
        d_kvn = _mm_nt(dkv_all, wkv_ref[...])
        dwq_ref[...] += _mm_tn(qn, dqpre)
        dwkv_ref[...] += _mm_tn(kvn, dkv_all)
        dqn_ref[...] += jnp.sum(d_qn * qh, axis=0, keepdims=True)
        dkn_ref[...] += jnp.sum(d_kvn * kvh, axis=0, keepdims=True)
        dp_ref[0] = jnp.concatenate([_rms_bwd(d_qn, qh, q_rstd, qn_ref[...]),
                                     _rms_bwd(d_kvn, kvh, kv_rstd, kn_ref[...]), dkr], axis=1).astype(_MXU)

    tok = lambda w: pl.BlockSpec((1, tm, w), lambda b, i: (b, i, 0))
    return pl.pallas_call(
        body, name="mla_prep_bwd", grid=(B, S // tm),
        in_specs=[tok(512), tok(128), tok(128), _full((1, 256)), _full((1, 128)), _full((256, 1024)),
                  _full((128, 1536)), tok(1024), tok(1024), tok(512)],
        out_specs=[tok(512), _full((256, 1024)), _full((128, 1536)), _full((1, 256)), _full((1, 128))],
        out_shape=[jax.ShapeDtypeStruct((B, S, 512), _MXU), jax.ShapeDtypeStruct((256, 1024), F32),
                   jax.ShapeDtypeStruct((128, 1536), F32), jax.ShapeDtypeStruct((1, 256), F32),
                   jax.ShapeDtypeStruct((1, 128), F32)],
        compiler_params=_cp(("arbitrary", "arbitrary")),
    )(mla_p, cos, sin, qnw, kvnw, w_uq, w_ukv, dq, dk, dv)


def _out_fwd(x, gate, r_g, o_mla, mla_p, g_g, w_out):
    B, S, D = x.shape
    tm = min(S, 512)

    def body(x_ref, g_ref, r_ref, o_ref, z_ref, gg_ref, w_ref, xn_ref, y_ref, mm_ref):
        mm = (o_ref[0] * _silu(z_ref[0].astype(F32))).astype(_MXU)
        mm_ref[0] = mm
        y = (jnp.dot(r_ref[0], w_ref[0:256, :], preferred_element_type=F32)
             + jnp.dot(mm, w_ref[256:768, :], preferred_element_type=F32)
             + jnp.dot(gg_ref[0], w_ref[768:1024, :], preferred_element_type=F32))
        y_ref[0] = y
        xn_ref[0] = x_ref[0] + g_ref[0] * y

    tok = lambda w, c=0: pl.BlockSpec((1, tm, w), lambda b, i: (b, i, c))
    return pl.pallas_call(
        body, name="out_fwd", grid=(B, S // tm),
        in_specs=[tok(D), pl.BlockSpec((1, 1, D), lambda b, i: (b, 0, 0)), tok(256), tok(512), tok(512, 1),
                  tok(256), _full((D, D))],
        out_specs=[tok(D), tok(D), tok(512)],
        out_shape=[jax.ShapeDtypeStruct((B, S, D), F32), jax.ShapeDtypeStruct((B, S, D), F32),
                   jax.ShapeDtypeStruct((B, S, 512), _MXU)],
        compiler_params=_cp(("parallel", "parallel")),
    )(x, gate, r_g, o_mla, mla_p, g_g, w_out)


def _out_bwd(dx, y, gate, r_g, mm, g_g, w_out):
    B, S, D = dx.shape
    tm = min(S, 512)

    def body(dx_ref, y_ref, g_ref, r_ref, mm_ref, gg_ref, w_ref, dr_ref, dmm_ref, dg_ref, dw_ref, dgate_ref):
        first = (pl.program_id(0) == 0) & (pl.program_id(1) == 0)

        @pl.when(first)
        def _():
            dw_ref[...] = jnp.zeros_like(dw_ref)

        @pl.when(pl.program_id(1) == 0)
        def _():
            dgate_ref[...] = jnp.zeros_like(dgate_ref)

        dxv = dx_ref[0]
        dgate_ref[0] += jnp.sum(dxv * y_ref[0], axis=0, keepdims=True)
        dy = (dxv * g_ref[0]).astype(_MXU)
        dr_ref[0] = _mm_nt(dy, w_ref[0:256, :])
        dmm_ref[0] = _mm_nt(dy, w_ref[256:768, :])
        dg_ref[0] = _mm_nt(dy, w_ref[768:1024, :])
        dw_ref[0:256, :] += _mm_tn(r_ref[0], dy)
        dw_ref[256:768, :] += _mm_tn(mm_ref[0], dy)
        dw_ref[768:1024, :] += _mm_tn(gg_ref[0], dy)

    tok = lambda w: pl.BlockSpec((1, tm, w), lambda b, i: (b, i, 0))
    per_seq = pl.BlockSpec((1, 1, D), lambda b, i: (b, 0, 0))
    return pl.pallas_call(
        body, name="out_bwd", grid=(B, S // tm),
        in_specs=[tok(D), tok(D), per_seq, tok(256), tok(512), tok(256), _full((D, D))],
        out_specs=[tok(256), tok(512), tok(256), _full((D, D)), per_seq],
        out_shape=[jax.ShapeDtypeStruct((B, S, 256), F32), jax.ShapeDtypeStruct((B, S, 512), F32),
                   jax.ShapeDtypeStruct((B, S, 256), F32), jax.ShapeDtypeStruct((D, D), F32),
                   jax.ShapeDtypeStruct((B, 1, D), F32)],
        compiler_params=_cp(("arbitrary", "arbitrary")),
    )(dx, y, gate, r_g, mm, g_g, w_out)


def _proj_bwd_x(x, shift, scale, nw, w_arr, d_ret, d_mla, d_mz, d_gla, dx_out):
    B, S, D = x.shape
    tm = min(S, 512)

    def body(x_ref, sc_ref, nw_ref, w_ref, dr_ref, dm_ref, dz_ref, dg_ref, dxo_ref,
             dx_ref, dsh_ref, dsc_ref, dnw_ref):
        first = (pl.program_id(0) == 0) & (pl.program_id(1) == 0)

        @pl.when(first)
        def _():
            dnw_ref[...] = jnp.zeros_like(dnw_ref)

        @pl.when(pl.program_id(1) == 0)
        def _():
            dsh_ref[...] = jnp.zeros_like(dsh_ref)
            dsc_ref[...] = jnp.zeros_like(dsc_ref)

        dp = jnp.concatenate([dr_ref[0], dm_ref[0], dz_ref[0], dg_ref[0]], axis=1)
        dh = lax.dot_general(dp, w_ref[...], (((1,), (1,)), ((), ())), preferred_element_type=F32)
        xv = x_ref[0]
        rstd = lax.rsqrt(jnp.mean(xv * xv, axis=-1, keepdims=True) + EPS)
        xh = xv * rstd
        nwv = nw_ref[...]
        mod = 1.0 + sc_ref[0]
        dsh_ref[0] += jnp.sum(dh, axis=0, keepdims=True)
        dsc_ref[0] += jnp.sum(dh * xh * nwv, axis=0, keepdims=True)
        dnw_ref[...] += jnp.sum(dh * xh * mod, axis=0, keepdims=True)
        dxh = dh * nwv * mod
        dx_ref[0] = dxo_ref[0] + rstd * (dxh - xh * jnp.mean(dxh * xh, axis=-1, keepdims=True))

    tok = lambda w: pl.BlockSpec((1, tm, w), lambda b, i: (b, i, 0))
    per_seq = pl.BlockSpec((1, 1, D), lambda b, i: (b, 0, 0))
    return pl.pallas_call(
        body, name="proj_bwd_x", grid=(B, S // tm),
        in_specs=[tok(D), per_seq, _full((1, D)), _full((D, ARR_W)), tok(RET_W), tok(512), tok(512),
                  tok(GLA_W), tok(D)],
        out_specs=[tok(D), per_seq, per_seq, _full((1, D))],
        out_shape=[jax.ShapeDtypeStruct((B, S, D), F32), jax.ShapeDtypeStruct((B, 1, D), F32),
                   jax.ShapeDtypeStruct((B, 1, D), F32), jax.ShapeDtypeStruct((1, D), F32)],
        compiler_params=_cp(("arbitrary", "arbitrary")),
    )(x, scale, nw, w_arr, d_ret, d_mla, d_mz, d_gla, dx_out)


def _proj_bwd_w(h, d_ret, d_mla, d_mz, d_gla):
    B, S, D = h.shape
    tm = min(S, 512)

    def body(h_ref, dr_ref, dm_ref, dz_ref, dg_ref, dw_ref):
        first = (pl.program_id(0) == 0) & (pl.program_id(1) == 0)

        @pl.when(first)
        def _():
            dw_ref[...] = jnp.zeros_like(dw_ref)

        hv = h_ref[0]
        tn = lambda d_ref: lax.dot_general(hv, d_ref[0], (((0,), (0,)), ((), ())), preferred_element_type=F32)
        dw_ref[:, 0:RET_W] += tn(dr_ref)
        dw_ref[:, RET_W:RET_W + 512] += tn(dm_ref)
        dw_ref[:, RET_W + 512:RET_W + MLA_W] += tn(dz_ref)
        dw_ref[:, RET_W + MLA_W:ARR_W] += tn(dg_ref)

    tok = lambda w: pl.BlockSpec((1, tm, w), lambda b, i: (b, i, 0))
    return pl.pallas_call(
        body, name="proj_bwd_w", grid=(B, S // tm),
        in_specs=[tok(D), tok(RET_W), tok(512), tok(512), tok(GLA_W)],
        out_specs=_full((D, ARR_W)), out_shape=jax.ShapeDtypeStruct((D, ARR_W), F32),
        compiler_params=_cp(("arbitrary", "arbitrary"), 56),
    )(h, d_ret, d_mla, d_mz, d_gla)


def _final_loss(x, fw, target):
    B, S, D = x.shape
    tm = min(S, 512)

    def body(x_ref, fw_ref, t_ref, dx_ref, loss_ref, dfw_ref):
        first = (pl.program_id(0) == 0) & (pl.program_id(1) == 0)

        @pl.when(first)
        def _():
            loss_ref[...] = jnp.zeros_like(loss_ref)
            dfw_ref[...] = jnp.zeros_like(dfw_ref)

        xv = x_ref[0]
        fwv = fw_ref[...]
        rstd = lax.rsqrt(jnp.mean(xv * xv, axis=-1, keepdims=True) + EPS)
        xh = xv * rstd
        err = xh * fwv - t_ref[0]
        loss_ref[...] += 0.5 * jnp.sum(jnp.mean(err * err, axis=-1, keepdims=True), axis=0, keepdims=True)
        dy = err * (1.0 / D)
        dfw_ref[...] += jnp.sum(dy * xh, axis=0, keepdims=True)
        dxh = dy * fwv
        dx_ref[0] = rstd * (dxh - xh * jnp.mean(dxh * xh, axis=-1, keepdims=True))

    tok = pl.BlockSpec((1, tm, D), lambda b, i: (b, i, 0))
    return pl.pallas_call(
        body, name="final_loss", grid=(B, S // tm),
        in_specs=[tok, _full((1, D)), tok],
        out_specs=[tok, _full((1, 1)), _full((1, D))],
        out_shape=[jax.ShapeDtypeStruct((B, S, D), F32), jax.ShapeDtypeStruct((1, 1), F32),
                   jax.ShapeDtypeStruct((1, D), F32)],
        compiler_params=_cp(("arbitrary", "arbitrary")),
    )(x, fw, target)


def _local_step(x, pos3, mod, loss_target, small, w_in_a, w_uq_a, w_ukv_a, w_out_b):
    B, S, D = x.shape
    tabs = _rope_tables(pos3)
    saved = []
    for l in range(DEPTH):
        x, s = _layer_fwd(x, tabs, mod[l], {n: a[l] for n, a in small.items() if n != "final_norm"},
                          w_in_a[l], w_uq_a[l], w_ukv_a[l], w_out_b[l])
        saved.append(s)
    dx, loss, d_fw = _final_loss(x, small["final_norm"].reshape(1, D), loss_target)
    grads = dict(final_norm=d_fw.reshape(D))
    per_layer = [None] * DEPTH
    for l in reversed(range(DEPTH)):
        dx, per_layer[l] = _layer_bwd(dx, saved[l], tabs)
    for name in per_layer[0]:
        grads[name] = jnp.stack([per_layer[l][name] for l in range(DEPTH)])
    return loss, dx, grads


def _layer_fwd(x, tabs, mod_l, small_l, w_in_a, w_uq_a=None, w_ukv_a=None, w_out_b=None, late_weights=None):
    B, S, D = x.shape
    cr, sr, cm, sm = tabs
    shift = mod_l[:, 0:D].reshape(B, 1, D)
    scale = mod_l[:, D:2 * D].reshape(B, 1, D)
    gate = mod_l[:, 2 * D:3 * D].reshape(B, 1, D)
    nw = small_l["norm_w"].reshape(1, D)
    qnw = small_l["mla_q_norm"].reshape(1, 256)
    kvnw = small_l["mla_kv_norm"].reshape(1, 128)
    w_g2p = jnp.pad(small_l["gla_w_g2"], ((0, 112), (0, 0)))
    b_g2 = small_l["gla_b_g2"].reshape(1, 128)
    gnw = jnp.tile(small_l["gla_norm"], 4).reshape(1, 256)
    ret_p, mla_p, gla_p, h = _proj_fwd(x, shift, scale, nw, w_in_a)
    r_g, r_raw, r_st = _ret_fwd(ret_p, cr, sr)
    if late_weights is not None:
        w_uq_a, w_ukv_a, w_out_b = late_weights(r_raw)
    q, k, v, kt, vt = _mla_prep_fwd(mla_p, cm, sm, qnw, kvnw, w_uq_a, w_ukv_a)
    o_mla, lse = _mla_attn_fwd(q, k, vt)
    g_g, g_raw, g_st = _gla_fwd(gla_p, w_g2p, b_g2, gnw)
    x_new, y, mm = _out_fwd(x, gate, r_g, o_mla, mla_p, g_g, w_out_b)
    saved = dict(x=x, shift=shift, scale=scale, gate=gate, nw=nw, qnw=qnw, kvnw=kvnw, w_g2p=w_g2p, b_g2=b_g2,
                 gnw=gnw, ret_p=ret_p, mla_p=mla_p, gla_p=gla_p, h=h, r_g=r_g, r_raw=r_raw, r_st=r_st, q=q, k=k,
                 v=v, kt=kt, o_mla=o_mla, lse=lse, g_g=g_g, g_raw=g_raw, g_st=g_st, y=y, mm=mm,
                 w_in_a=w_in_a, w_uq_a=w_uq_a, w_ukv_a=w_ukv_a, w_out_b=w_out_b)
    return x_new, saved


def _layer_bwd(dx, s, tabs, early_grads=None):
    B, S, D = dx.shape
    cr, sr, cm, sm = tabs
    d_r, d_mm, d_g, dw_out, d_gate = _out_bwd(dx, s["y"], s["gate"], s["r_g"], s["mm"], s["g_g"], s["w_out_b"])
    d_ret = _ret_bwd(s["ret_p"], cr, sr, s["r_raw"], s["r_st"], d_r)
    do, d_mz, dl = _mla_gate_bwd(d_mm, s["o_mla"], s["mla_p"])
    dq, dk, dv = _mla_attn_bwd(s["q"], s["k"], s["v"], s["kt"], do, s["lse"], dl)
    d_mla, dw_uq, dw_ukv, d_qnw, d_kvnw = _mla_prep_bwd(
        s["mla_p"], cm, sm, s["qnw"], s["kvnw"], s["w_uq_a"], s["w_ukv_a"], dq, dk, dv)
    gnw = s["gnw"] if early_grads is None else s["gnw"] + early_grads(dw_out, dw_uq, dw_ukv)
    d_gla, dw_g2p, db_g2, d_gnw = _gla_bwd(s["gla_p"], s["w_g2p"], s["b_g2"], gnw, s["g_raw"], s["g_st"], d_g)
    dx, d_shift, d_scale, d_nw = _proj_bwd_x(s["x"], s["shift"], s["scale"], s["nw"], s["w_in_a"],
                                             d_ret, d_mla, d_mz, d_gla, dx)
    dw_in = _proj_bwd_w(s["h"], d_ret, d_mla, d_mz, d_gla)
    grads = dict(
        d_mod=jnp.concatenate([d_shift, d_scale, d_gate], axis=2).reshape(B, 3 * D),
        norm_w=d_nw.reshape(D), mla_q_norm=d_qnw.reshape(256), mla_kv_norm=d_kvnw.reshape(128),
        gla_w_g2=dw_g2p[0:16], gla_b_g2=db_g2.reshape(128), gla_norm256=d_gnw.reshape(256),
        w_in_a=dw_in, w_uq_a=dw_uq, w_ukv_a=dw_ukv, w_out=dw_out)
    return dx, grads


def _exchange(arrs, gather, name):
    n = len(arrs)
    out_shape = [jax.ShapeDtypeStruct(((N_DEV,) + a.shape) if g else a.shape, a.dtype)
                 for a, g in zip(arrs, gather)]

    def body(*refs):
        ins, outs = refs[:n], refs[n:2 * n]
        send_sems, recv_sems, local_sems = refs[2 * n:]
        ix, iy, ic = lax.axis_index("x"), lax.axis_index("y"), lax.axis_index("c")
        me = 4 * ix + 2 * iy + ic
        copies = []
        for a in range(n):
            mine = ins[a] if gather[a] else ins[a].at[me]
            loc = pltpu.make_async_copy(mine, outs[a].at[me], local_sems.at[a])
            loc.start()
            copies.append(loc)
            for d in range(1, N_DEV):
                px = 1 - ix if d & 4 else ix
                py = 1 - iy if d & 2 else iy
                pc = 1 - ic if d & 1 else ic
                src = ins[a] if gather[a] else ins[a].at[4 * px + 2 * py + pc]
                cp = pltpu.make_async_remote_copy(
                    src_ref=src, dst_ref=outs[a].at[me], send_sem=send_sems.at[a, d - 1],
                    recv_sem=recv_sems.at[a, d - 1], device_id=(px, py, pc), device_id_type=pl.DeviceIdType.MESH)
                cp.start()
                copies.append(cp)
        for cp in copies:
            cp.wait()

    any_spec = pl.BlockSpec(memory_space=pl.ANY)
    outs = pl.pallas_call(
        body, name=name, in_specs=[any_spec] * n, out_specs=[any_spec] * n, out_shape=out_shape,
        scratch_shapes=[pltpu.SemaphoreType.DMA((n, N_DEV - 1)), pltpu.SemaphoreType.DMA((n, N_DEV - 1)),
                        pltpu.SemaphoreType.DMA((n,))],
    )(*arrs)
    return list(outs)


def _peers(ix, iy, ic):
    out = []
    for d in range(1, N_DEV):
        px = 1 - ix if d & 4 else ix
        py = 1 - iy if d & 2 else iy
        pc = 1 - ic if d & 1 else ic
        out.append((d - 1, (px, py, pc), 4 * px + 2 * py + pc))
    return out


def _exchange_start(arrs, gather, name):
    n = len(arrs)
    lands = [lax.empty(((N_DEV,) + a.shape) if g else a.shape, a.dtype) for a, g in zip(arrs, gather)]

    def body(*refs):
        ins, land_refs = refs[:n], refs[n:2 * n]
        send_sems, recv_sems = refs[2 * n], refs[2 * n + 1]
        token = refs[-1]
        ix, iy, ic = lax.axis_index("x"), lax.axis_index("y"), lax.axis_index("c")
        me = 4 * ix + 2 * iy + ic
        for a in range(n):
            for k, peer, peer_idx in _peers(ix, iy, ic):
                pltpu.make_async_remote_copy(
                    src_ref=ins[a] if gather[a] else ins[a].at[peer_idx], dst_ref=land_refs[a].at[me],
                    send_sem=send_sems.at[7 * a + k], recv_sem=recv_sems.at[7 * a + k], device_id=peer,
                    device_id_type=pl.DeviceIdType.MESH).start()
        token[...] = jnp.zeros_like(token)

    hbm = pl.BlockSpec(memory_space=pltpu.HBM)
    sem = pl.BlockSpec(memory_space=pltpu.SEMAPHORE)
    held = [pltpu.with_memory_space_constraint(a, pltpu.HBM) for a in list(arrs) + lands]
    outs = pl.pallas_call(
        body, name=name,
        out_shape=(pltpu.SemaphoreType.DMA((7 * n,)), pltpu.SemaphoreType.DMA((7 * n,)),
                   *[pltpu.HBM(a.shape, a.dtype) for a in held], jax.ShapeDtypeStruct((8, 128), F32)),
        in_specs=[hbm] * (2 * n), out_specs=(sem, sem, *[hbm] * (2 * n), pl.BlockSpec(memory_space=pltpu.VMEM)),
        input_output_aliases={a: 2 + a for a in range(2 * n)},
        compiler_params=pltpu.CompilerParams(has_side_effects=pltpu.SideEffectType.DATAFLOW_SIDE_EFFECTING),
    )(*held)
    return dict(send=outs[0], recv=outs[1], srcs=list(outs[2:2 + n]), lands=list(outs[2 + n:2 + 2 * n]),
                token=outs[-1], gather=list(gather))


def _exchange_wait(flight, after, me, name):
    n = len(flight["srcs"])
    gather = flight["gather"]

    def body(*refs):
        srcs, land_refs = refs[:n], refs[n:2 * n]
        send_sems, recv_sems = refs[2 * n], refs[2 * n + 1]
        ix, iy, ic = lax.axis_index("x"), lax.axis_index("y"), lax.axis_index("c")
        mine = 4 * ix + 2 * iy + ic
        for a in range(n):
            for k, peer, peer_idx in _peers(ix, iy, ic):
                cp = pltpu.make_async_remote_copy(
                    src_ref=srcs[a] if gather[a] else srcs[a].at[peer_idx], dst_ref=land_refs[a].at[mine],
                    send_sem=send_sems.at[7 * a + k], recv_sem=recv_sems.at[7 * a + k], device_id=peer,
                    device_id_type=pl.DeviceIdType.MESH)
                cp.wait_send()
                cp.wait_recv()

    hbm = pl.BlockSpec(memory_space=pltpu.HBM)
    sem = pl.BlockSpec(memory_space=pltpu.SEMAPHORE)
    held = flight["srcs"] + flight["lands"]
    outs = pl.pallas_call(
        body, name=name, out_shape=tuple(pltpu.HBM(a.shape, a.dtype) for a in held),
        in_specs=[hbm] * (2 * n) + [sem, sem, pl.BlockSpec(memory_space=pl.ANY)], out_specs=tuple([hbm] * (2 * n)),
        input_output_aliases={a: a for a in range(2 * n)},
        compiler_params=pltpu.CompilerParams(has_side_effects=pltpu.SideEffectType.DATAFLOW_SIDE_EFFECTING),
    )(*held, flight["send"], flight["recv"], after)
    got = []
    for a in range(n):
        src, land = outs[a], outs[n + a]
        own = src if gather[a] else lax.dynamic_index_in_dim(src, me, axis=0, keepdims=False)
        got.append(lax.dynamic_update_index_in_dim(land, own, me, axis=0))
    return got


def _ada_fwd(c_all, ada_w, ada_b_cols):
    nb, D = c_all.shape
    cols = ada_w.shape[2]

    def body(c_ref, w_ref, b_ref, out_ref):
        ca = _silu(c_ref[...])
        for l in range(DEPTH):
            out_ref[l] = _mm(ca, w_ref[l]) + b_ref[l:l + 1, :]

    return pl.pallas_call(
        body, name="ada_fwd", out_shape=jax.ShapeDtypeStruct((DEPTH, nb, cols), F32),
        in_specs=[pl.BlockSpec(memory_space=pltpu.VMEM)] * 3, out_specs=pl.BlockSpec(memory_space=pltpu.VMEM),
        compiler_params=pltpu.CompilerParams(vmem_limit_bytes=32 * VMEM_MB),
    )(c_all, ada_w, ada_b_cols)


def _ada_bwd(c_all, d_mod_cols):
    nb, D = c_all.shape
    cols = d_mod_cols.shape[2]

    def body(c_ref, dm_ref, out_ref):
        ca = _silu(c_ref[...])
        for l in range(DEPTH):
            out_ref[l] = _mm_tn(ca, dm_ref[l])

    return pl.pallas_call(
        body, name="ada_bwd", out_shape=jax.ShapeDtypeStruct((DEPTH, D, cols), F32),
        in_specs=[pl.BlockSpec(memory_space=pltpu.VMEM)] * 2, out_specs=pl.BlockSpec(memory_space=pltpu.VMEM),
        compiler_params=pltpu.CompilerParams(vmem_limit_bytes=32 * VMEM_MB),
    )(c_all, d_mod_cols)


def _sum_adamw(parts, w, m, v, name):
    P, R, C = parts.shape
    tr = 256 if (R % 256 == 0 and R > 256) else R

    def body(p_ref, w_ref, m_ref, v_ref, g_ref, d_ref, nm_ref, nv_ref):
        g = p_ref[0].astype(F32)
        for k in range(1, P):
            g = g + p_ref[k].astype(F32)
        g_ref[...] = g
        nm = ADAM_B1 * m_ref[...] + (1.0 - ADAM_B1) * g
        nv = ADAM_B2 * v_ref[...] + (1.0 - ADAM_B2) * (g * g)
        nm_ref[...] = nm
        nv_ref[...] = nv
        m_hat = nm / (1.0 - ADAM_B1 ** ADAM_STEP)
        v_hat = nv / (1.0 - ADAM_B2 ** ADAM_STEP)
        d_ref[...] = -ADAM_LR * (m_hat / (jnp.sqrt(v_hat) + ADAM_EPS) + ADAM_WD * w_ref[...])

    blk = pl.BlockSpec((tr, C), lambda i: (i, 0))
    shp = jax.ShapeDtypeStruct((R, C), F32)
    return pl.pallas_call(
        body, name=name, grid=(R // tr,),
        in_specs=[pl.BlockSpec((P, tr, C), lambda i: (0, i, 0)), blk, blk, blk],
        out_specs=[blk, blk, blk, blk], out_shape=[shp, shp, shp, shp],
        compiler_params=_cp(("parallel",)),
    )(parts, w, m, v)


SMALL = ["norm_w", "mla_q_norm", "mla_kv_norm", "gla_w_g2", "gla_b_g2", "gla_norm", "final_norm"]


SMALL_ROWS = 72


def _pack_small(loss, part):
    flat = [jnp.pad(loss.reshape(1), (0, 127))] + [part[n].reshape(-1) for n in SMALL]
    used = sum(f.shape[0] for f in flat)
    flat.append(jnp.zeros((SMALL_ROWS * 128 - used,), F32))
    return jnp.concatenate(flat).reshape(SMALL_ROWS, 128)


def _small_adamw(packed_parts, w, m, v):
    n = len(w)

    def body(*refs):
        p_ref = refs[0]
        w_refs, m_refs, v_refs = refs[1:1 + n], refs[1 + n:1 + 2 * n], refs[1 + 2 * n:1 + 3 * n]
        outs, acc = refs[1 + 3 * n:-1], refs[-1]
        total = p_ref[0]
        for k in range(1, N_DEV):
            total = total + p_ref[k]
        acc[...] = total
        outs[0][...] = acc[0:1, :]
        r0 = 1
        for i in range(n):
            shp = w_refs[i].shape
            if len(shp) == 3:
                g = acc[r0:r0 + shp[0] * shp[1], :].reshape(shp)
                r0 += shp[0] * shp[1]
            elif shp[1] < 128:
                g = acc[r0:r0 + shp[0], 0:shp[1]]
                r0 += shp[0]
            else:
                k = shp[1] // 128
                g = jnp.concatenate(
                    [jnp.concatenate([acc[r0 + l * k + j:r0 + l * k + j + 1, :] for j in range(k)], axis=1)
                     for l in range(shp[0])], axis=0)
                r0 += shp[0] * k
            nm = ADAM_B1 * m_refs[i][...] + (1.0 - ADAM_B1) * g
            nv = ADAM_B2 * v_refs[i][...] + (1.0 - ADAM_B2) * (g * g)
            m_hat = nm / (1.0 - ADAM_B1 ** ADAM_STEP)
            v_hat = nv / (1.0 - ADAM_B2 ** ADAM_STEP)
            outs[1 + 4 * i][...] = g
            outs[2 + 4 * i][...] = -ADAM_LR * (m_hat / (jnp.sqrt(v_hat) + ADAM_EPS) + ADAM_WD * w_refs[i][...])
            outs[3 + 4 * i][...] = nm
            outs[4 + 4 * i][...] = nv

    vmem = pl.BlockSpec(memory_space=pltpu.VMEM)
    out_shape = [jax.ShapeDtypeStruct((1, 128), F32)]
    for a in w:
        out_shape += [jax.ShapeDtypeStruct(a.shape, F32)] * 4
    outs = pl.pallas_call(
        body, name="adamw_small", in_specs=[vmem] * (1 + 3 * n), out_specs=[vmem] * (1 + 4 * n), out_shape=out_shape,
        scratch_shapes=[pltpu.VMEM((SMALL_ROWS, 128), F32)],
    )(packed_parts, *w, *m, *v)
    return outs[0], [outs[1 + 4 * i:5 + 4 * i] for i in range(n)]


WEIGHTS = ["norm_w", "ada_w", "ada_b", "w_in", "mla_q_norm", "w_uq", "mla_kv_norm", "w_ukv", "gla_w_g2",
           "gla_b_g2", "gla_norm", "w_out", "final_norm"]


def kernel(x, c, positions, norm_w, ada_w, ada_b, w_in, mla_q_norm, w_uq, mla_kv_norm, w_ukv, gla_w_g2, gla_b_g2, gla_norm, w_out, final_norm, loss_target, m_norm_w, m_ada_w, m_ada_b, m_w_in, m_mla_q_norm, m_w_uq, m_mla_kv_norm, m_w_ukv, m_gla_w_g2, m_gla_b_g2, m_gla_norm, m_w_out, m_final_norm, v_norm_w, v_ada_w, v_ada_b, v_w_in, v_mla_q_norm, v_w_uq, v_mla_kv_norm, v_w_ukv, v_gla_w_g2, v_gla_b_g2, v_gla_norm, v_w_out, v_final_norm):
    w = dict(norm_w=norm_w, ada_w=ada_w, ada_b=ada_b, w_in=w_in, mla_q_norm=mla_q_norm, w_uq=w_uq,
             mla_kv_norm=mla_kv_norm, w_ukv=w_ukv, gla_w_g2=gla_w_g2, gla_b_g2=gla_b_g2, gla_norm=gla_norm,
             w_out=w_out, final_norm=final_norm)
    m = dict(norm_w=m_norm_w, ada_w=m_ada_w, ada_b=m_ada_b, w_in=m_w_in, mla_q_norm=m_mla_q_norm, w_uq=m_w_uq,
             mla_kv_norm=m_mla_kv_norm, w_ukv=m_w_ukv, gla_w_g2=m_gla_w_g2, gla_b_g2=m_gla_b_g2,
             gla_norm=m_gla_norm, w_out=m_w_out, final_norm=m_final_norm)
    v = dict(norm_w=v_norm_w, ada_w=v_ada_w, ada_b=v_ada_b, w_in=v_w_in, mla_q_norm=v_mla_q_norm, w_uq=v_w_uq,
             mla_kv_norm=v_mla_kv_norm, w_ukv=v_w_ukv, gla_w_g2=v_gla_w_g2, gla_b_g2=v_gla_b_g2,
             gla_norm=v_gla_norm, w_out=v_w_out, final_norm=v_final_norm)
    B, S, D = x.shape
    me = 4 * lax.axis_index("x") + 2 * lax.axis_index("y") + lax.axis_index("c")
    ada_cols = ada_w.shape[2]
    cast = lambda a: a.astype(_MXU)

    sharded = ["w_in", "w_uq", "w_ukv", "w_out"]

    whole_cols = lambda a: jnp.transpose(a, (1, 0, 2)).reshape(a.shape[1], -1)
    whole_in = lambda blk: _arrange_w_in(whole_cols(blk))
    whole_rest = lambda blks: (_arrange_w_uq(whole_cols(blks[0])), _arrange_w_ukv(whole_cols(blks[1])),
                               blks[2].reshape(D, D))
    col_blocks = lambda a: jnp.transpose(a.reshape(a.shape[0], N_DEV, -1), (1, 0, 2)).astype(jnp.bfloat16)
    blocks_in = lambda dw_in_a: col_blocks(_unarrange_w_in(dw_in_a))
    blocks_rest = lambda dw_out, dw_uq_a, dw_ukv_a: [
        col_blocks(_unarrange_w_uq(dw_uq_a)), col_blocks(_unarrange_w_ukv(dw_ukv_a)),
        dw_out.reshape(N_DEV, D // N_DEV, D).astype(jnp.bfloat16)]

    got0 = _exchange([c, cast(w_in[0])], [True, True], "gather_first")
    c_all = got0[0].reshape(N_DEV * B, D)
    flight_r = _exchange_start([cast(w[n][0]) for n in sharded[1:]], [True] * 3, "gather_start_layer0")
    flight_w = _exchange_start([cast(w[n][1]) for n in sharded], [True] * 4, "gather_start_layer1")
    started = flight_r["token"][0, 0] + flight_w["token"][0, 0]

    ada_b_cols = lax.dynamic_slice(ada_b, (0, me * ada_cols), (DEPTH, ada_cols))
    mod_cols = _ada_fwd(c_all, ada_w, ada_b_cols)
    mod_send = jnp.transpose(mod_cols.reshape(DEPTH, N_DEV, B, ada_cols), (1, 0, 2, 3))
    (mod_recv,) = _exchange([mod_send], [False], "scatter_mod")
    mod = jnp.transpose(mod_recv, (1, 2, 0, 3)).reshape(DEPTH, B, 3 * D)

    small_w = {n: w[n] for n in SMALL}
    layer_small = lambda l: {n: a[l] for n, a in small_w.items() if n != "final_norm"}
    tabs = _rope_tables(positions.reshape(B, S, 1))
    late0 = lambda after: whole_rest(_exchange_wait(flight_r, after, me, "gather_wait_layer0"))
    x1, saved0 = _layer_fwd(x, tabs, mod[0] + started, layer_small(0), whole_in(got0[1]), late_weights=late0)
    got1 = _exchange_wait(flight_w, x1, me, "gather_wait_layer1")
    x2, saved1 = _layer_fwd(x1, tabs, mod[1], layer_small(1), whole_in(got1[0]), *whole_rest(got1[1:]))
    dx, loss, d_fw = _final_loss(x2, final_norm.reshape(1, D), loss_target)

    dx, g1 = _layer_bwd(dx, saved1, tabs)
    flight_g = _exchange_start([blocks_in(g1["w_in_a"])] + blocks_rest(g1["w_out"], g1["w_uq_a"], g1["w_ukv_a"]),
                               [False] * 4, "grads_start_layer1")
    flights = {}

    def early0(dw_out, dw_uq_a, dw_ukv_a):
        flights["rest0"] = _exchange_start(blocks_rest(dw_out, dw_uq_a, dw_ukv_a), [False] * 3, "grads_start_layer0")
        return flights["rest0"]["token"][0, 0]

    saved0 = dict(saved0, gate=saved0["gate"] + flight_g["token"][0, 0])
    grad_x, g0 = _layer_bwd(dx, saved0, tabs, early_grads=early0)
    parts1 = _exchange_wait(flight_g, grad_x, me, "grads_wait_layer1")
    rest0 = _exchange_wait(flights["rest0"], g0["w_in_a"], me, "grads_wait_layer0")

    both = lambda n: jnp.stack([g0[n], g1[n]])
    d_mod = both("d_mod")
    part = dict(norm_w=both("norm_w"), mla_q_norm=both("mla_q_norm"), mla_kv_norm=both("mla_kv_norm"),
                gla_w_g2=both("gla_w_g2"), gla_b_g2=both("gla_b_g2"), gla_norm=both("gla_norm256")[:, 0:128],
                final_norm=d_fw)
    d_mod_g, small_g, in0 = _exchange([d_mod, _pack_small(loss, part), blocks_in(g0["w_in_a"])],
                                      [True, True, False], "exchange_last")
    parts0 = [in0] + rest0

    d_mod_all = jnp.transpose(d_mod_g, (1, 0, 2, 3)).reshape(DEPTH, N_DEV * B, 3 * D)
    d_mod_cols = lax.dynamic_slice(d_mod_all, (0, 0, me * ada_cols), (DEPTH, N_DEV * B, ada_cols))
    g_ada_w = _ada_bwd(c_all, d_mod_cols)

    res = {}

    def update(name, parts2d):
        shp = w[name].shape
        two = lambda a: a.reshape(parts2d.shape[1:])
        out = _sum_adamw(parts2d, two(w[name]), two(m[name]), two(v[name]), "adamw_" + name)
        res[name] = [o.reshape(shp) for o in out]

    update("ada_w", g_ada_w.reshape(1, DEPTH * D, ada_cols))
    update("ada_b", jnp.transpose(d_mod_g, (0, 2, 1, 3)).reshape(N_DEV * B, DEPTH * 3 * D // 128, 128))
    for a, name in enumerate(sharded):
        update(name, jnp.concatenate([parts0[a], parts1[a]], axis=1))
    row = lambda a: a.reshape(1, D) if a.ndim == 1 else a
    loss_sum, small_out = _small_adamw(small_g, [row(w[n]) for n in SMALL], [row(m[n]) for n in SMALL],
                                       [row(v[n]) for n in SMALL])
    for n, outs in zip(SMALL, small_out):
        res[n] = [o.reshape(w[n].shape) for o in outs]
    loss_out = loss_sum[0, 0]
    return (loss_out, grad_x, *[res[n][0] for n in WEIGHTS], *[res[n][1] for n in WEIGHTS],
            *[res[n][2] for n in WEIGHTS], *[res[n][3] for n in WEIGHTS])
```

```python
import functools
import math

import numpy as np
import jax
import jax.numpy as jnp
from jax import lax
from jax.experimental import pallas as pl
from jax.experimental.pallas import tpu as pltpu

F32 = jnp.float32
_MXU = jnp.bfloat16

D_MODEL = 1024
DEPTH = 2
CHUNK = 64
EPS = 1e-6
ROPE_THETA = 10000.0
N_DEV = 8

MLA_SCALE = 96.0 ** -0.5
RET_KSCALE = 64.0 ** -0.5
GLA_KSCALE = 32.0 ** -0.5
GLA_TAU = 16.0

ADAM_LR = 0.001
ADAM_B1 = 0.9
ADAM_B2 = 0.999
ADAM_EPS = 1e-08
ADAM_WD = 0.01
ADAM_STEP = 10

RET_W, MLA_W, GLA_W = 1024, 1024, 896
ARR_W = RET_W + MLA_W + GLA_W
VMEM_MB = 1024 * 1024


def _cp(sem, vmem_mb=48):
    return pltpu.CompilerParams(dimension_semantics=sem, vmem_limit_bytes=vmem_mb * VMEM_MB)


def _mm(a, b):
    return jnp.dot(a.astype(_MXU), b.astype(_MXU), preferred_element_type=F32)


def _mm_nt(a, b):
    return lax.dot_general(a.astype(_MXU), b.astype(_MXU), (((1,), (1,)), ((), ())),
                           preferred_element_type=F32)


def _mm_tn(a, b):
    return lax.dot_general(a.astype(_MXU), b.astype(_MXU), (((0,), (0,)), ((), ())),
                           preferred_element_type=F32)


def _mm_f32(a, b):
    return jnp.dot(a, b, precision=lax.Precision.HIGHEST, preferred_element_type=F32)


def _sig(z):
    return 1.0 / (1.0 + jnp.exp(-z))


def _silu(z):
    return z * _sig(z)


def _dsilu(z):
    s = _sig(z)
    return s * (1.0 + z * (1.0 - s))


def _full(shape):
    nd = len(shape)
    return pl.BlockSpec(shape, lambda *_: (0,) * nd)


def _qk_perm(blk):
    r = blk.shape[0]
    return jnp.transpose(blk.reshape(r, 4, 2, 32), (0, 2, 1, 3)).reshape(r, 256)


def _qk_unperm(blk):
    r = blk.shape[0]
    return jnp.transpose(blk.reshape(r, 2, 4, 32), (0, 2, 1, 3)).reshape(r, 256)


def _arrange_w_in(w):
    z = lambda n: jnp.zeros((w.shape[0], n), w.dtype)
    ret = [_qk_perm(w[:, 0:256]), _qk_perm(w[:, 256:512]), w[:, 512:768], w[:, 768:1024]]
    mla = [w[:, 1024:1280], w[:, 1280:1408], z(64), w[:, 1408:1440], z(32), w[:, 1440:1952]]
    gla = [w[:, 1952:2080], w[:, 2080:2208], w[:, 2208:2464], w[:, 2464:2480], z(112), w[:, 2480:2736]]
    return jnp.concatenate(ret + mla + gla, axis=1)


def _unarrange_w_in(a):
    m, g = RET_W, RET_W + MLA_W
    parts = [_qk_unperm(a[:, 0:256]), _qk_unperm(a[:, 256:512]), a[:, 512:1024],
             a[:, m:m + 384], a[:, m + 448:m + 480], a[:, m + 512:m + 1024],
             a[:, g:g + 528], a[:, g + 640:g + 896]]
    return jnp.concatenate(parts, axis=1)


def _arrange_w_uq(w):
    return jnp.pad(w.reshape(256, 8, 96), ((0, 0), (0, 0), (0, 32))).reshape(256, 1024)


def _unarrange_w_uq(a):
    return a.reshape(256, 8, 128)[:, :, :96].reshape(256, 768)


def _arrange_w_ukv(w):
    r = w.reshape(128, 8, 128)
    k = jnp.pad(r[:, :, :64], ((0, 0), (0, 0), (0, 64))).reshape(128, 1024)
    return jnp.concatenate([k, r[:, :, 64:].reshape(128, 512)], axis=1)


def _unarrange_w_ukv(a):
    k = a[:, :1024].reshape(128, 8, 128)[:, :, :64]
    v = a[:, 1024:].reshape(128, 8, 64)
    return jnp.concatenate([k, v], axis=2).reshape(128, 1024)


def _rope_tables(pos3):
    B, S, _ = pos3.shape
    ts = min(S, 512)
    inv32 = (np.float32(ROPE_THETA) ** (-(np.arange(32, dtype=np.float32) / 32))).astype(np.float32)
    inv16 = (np.float32(ROPE_THETA) ** (-(np.arange(16, dtype=np.float32) / 16))).astype(np.float32)
    inv = np.zeros((1, 128), np.float32)
    inv[0, 0:32] = inv32
    inv[0, 32:48] = inv16

    def body(pos_ref, inv_ref, cr, sr, cm, sm):
        ang = pos_ref[0].astype(F32) * inv_ref[...]
        lane = lax.broadcasted_iota(jnp.int32, (1, 128), 1)

        def every_head(x):
            y = jnp.where(lane < 32, x, pltpu.roll(x, 32, 1))
            return jnp.where(lane < 64, y, pltpu.roll(y, 64, 1))

        def rotary_pair(x, fill):
            return jnp.where((lane >= 64) & (lane < 80), pltpu.roll(x, 32, 1),
                             jnp.where((lane >= 80) & (lane < 96), pltpu.roll(x, 48, 1), fill))

        c, s = jnp.cos(ang), jnp.sin(ang)
        cr[0] = every_head(c)
        sr[0] = every_head(s)
        cm[0] = rotary_pair(c, 1.0)
        sm[0] = rotary_pair(s, 0.0)

    tab = jax.ShapeDtypeStruct((B, S, 128), F32)
    blk = pl.BlockSpec((1, ts, 128), lambda b, i: (b, i, 0))
    return pl.pallas_call(
        body, name="rope_tables", grid=(B, S // ts),
        in_specs=[pl.BlockSpec((1, ts, 1), lambda b, i: (b, i, 0)), _full((1, 128))],
        out_specs=[blk, blk, blk, blk], out_shape=[tab, tab, tab, tab],
        compiler_params=_cp(("parallel", "parallel")),
    )(pos3, jnp.asarray(inv))


def _rope128(x, cos, sin):
    lane = lax.broadcasted_iota(jnp.int32, (1, 128), 1)
    rp = pltpu.roll(x, 16, 1)
    rm = pltpu.roll(x, 112, 1)
    return x * cos + jnp.where(lane < 80, -rm, rp) * sin


def _rope128_t(d, cos, sin):
    lane = lax.broadcasted_iota(jnp.int32, (1, 128), 1)
    y = d * sin
    yp = pltpu.roll(y, 16, 1)
    ym = pltpu.roll(y, 112, 1)
    return d * cos + jnp.where(lane < 64, 0.0, jnp.where(lane < 80, ym, jnp.where(lane < 96, -yp, 0.0)))


def _proj_fwd(x, shift, scale, nw, w_arr):
    B, S, D = x.shape
    tm = min(S, 512)

    def body(x_ref, sh_ref, sc_ref, nw_ref, w_ref, ret_ref, mla_ref, gla_ref, h_ref):
        xv = x_ref[0]
        rstd = lax.rsqrt(jnp.mean(xv * xv, axis=-1, keepdims=True) + EPS)
        h = (xv * rstd * nw_ref[...]) * (1.0 + sc_ref[0]) + sh_ref[0]
        hb = h.astype(_MXU)
        h_ref[0] = hb
        ret_ref[0] = jnp.dot(hb, w_ref[:, 0:RET_W], preferred_element_type=F32).astype(_MXU)
        mla_ref[0] = jnp.dot(hb, w_ref[:, RET_W:RET_W + MLA_W], preferred_element_type=F32).astype(_MXU)
        gla_ref[0] = jnp.dot(hb, w_ref[:, RET_W + MLA_W:ARR_W], preferred_element_type=F32).astype(_MXU)

    tok = lambda w: pl.BlockSpec((1, tm, w), lambda b, i: (b, i, 0))
    per_seq = pl.BlockSpec((1, 1, D), lambda b, i: (b, 0, 0))
    return pl.pallas_call(
        body, name="proj_fwd", grid=(B, S // tm),
        in_specs=[tok(D), per_seq, per_seq, _full((1, D)), _full((D, ARR_W))],
        out_specs=[tok(RET_W), tok(MLA_W), tok(GLA_W), tok(D)],
        out_shape=[jax.ShapeDtypeStruct((B, S, RET_W), _MXU), jax.ShapeDtypeStruct((B, S, MLA_W), _MXU),
                   jax.ShapeDtypeStruct((B, S, GLA_W), _MXU), jax.ShapeDtypeStruct((B, S, D), _MXU)],
        compiler_params=_cp(("parallel", "parallel")),
    )(x, shift, scale, nw, w_arr)


RET_L = 256


def _ret_consts(L):
    lg = np.log1p(-np.exp2(-5.0 - np.arange(4, dtype=np.float32))).astype(np.float32)
    i = np.arange(L)
    ci = i // CHUNK
    diff = (i[:, None] - i[None, :]).astype(np.float32)
    same = ci[:, None] == ci[None, :]
    past = ci[None, :] < ci[:, None]
    expo = np.where(same, np.abs(diff), np.where(past, diff, 0.0)).astype(np.float32)
    dec = np.where((same | past)[None], np.exp(lg[:, None, None] * expo[None]), 0.0).astype(np.float32)
    head = (np.arange(256) % 128) // 32
    qw = np.exp((i + 1.0)[:, None] * lg[head][None, :]).astype(np.float32)
    kw = np.exp((L - 1.0 - i)[:, None] * lg[head][None, :]).astype(np.float32)
    a_row = np.exp(np.float32(L) * lg[head])[None, :].astype(np.float32)
    return [jnp.asarray(t) for t in (dec.reshape(4 * L, L), qw, kw, a_row)]


def _ret_masks():
    lane = lax.broadcasted_iota(jnp.int32, (1, 256), 1)
    mh = [((lane % 128) // 32) == h for h in range(4)]
    mv = [(lane // 64) == h for h in range(4)]
    vi = lax.broadcasted_iota(jnp.int32, (256, 256), 0)
    ki = lax.broadcasted_iota(jnp.int32, (256, 256), 1)
    bd = (vi // 64) == ((ki % 128) // 32)
    return mh, mv, bd


def _ret_rope(p, cs, sn):
    q1, q2, k1, k2 = p[:, 0:128], p[:, 128:256], p[:, 256:384], p[:, 384:512]
    qr = jnp.concatenate([q1 * cs - q2 * sn, q2 * cs + q1 * sn], axis=1)
    kr = jnp.concatenate([k1 * cs - k2 * sn, k2 * cs + k1 * sn], axis=1) * RET_KSCALE
    return qr, kr


def _head_mean(x, mv, width):
    out = jnp.zeros_like(x)
    for m in mv:
        s = jnp.sum(jnp.where(m, x, 0.0), axis=-1, keepdims=True) * (1.0 / width)
        out = jnp.where(m, s, out)
    return out


def _stack_heads(x, masks):
    return jnp.concatenate([jnp.where(m, x, 0.0) for m in masks], axis=0)


def _fold_heads(xs, masks, L):
    out = jnp.where(masks[0], xs[0:L], 0.0)
    for h in range(1, 4):
        out = out + jnp.where(masks[h], xs[h * L:(h + 1) * L], 0.0)
    return out


def _ret_fwd(ret_p, cos, sin):
    B, S, _ = ret_p.shape
    L = min(RET_L, S)
    NB = S // L
    consts = _ret_consts(L)

    def body(p_ref, c_ref, s_ref, ds_ref, qw_ref, kw_ref, a_ref, out_ref, raw_ref, st_ref, st_sc):
        @pl.when(pl.program_id(1) == 0)
        def _():
            st_sc[...] = jnp.zeros_like(st_sc)

        mh, mv, bd = _ret_masks()
        p = p_ref[0].astype(F32)
        qr, kr = _ret_rope(p, c_ref[0], s_ref[0])
        v = p[:, 512:768]
        z = p[:, 768:1024]
        a_s = _mm_nt(_stack_heads(qr, mh), kr) * ds_ref[...]
        intra = _fold_heads(_mm(a_s, v), mv, L)
        st = st_sc[...]
        st_ref[0, 0] = st
        r = intra + _mm_nt(qr * qw_ref[...], st)
        raw_ref[0] = r
        st_sc[...] = st * a_ref[...] + jnp.where(bd, _mm_tn(v, kr * kw_ref[...]), 0.0)
        rstd = lax.rsqrt(_head_mean(r * r, mv, 64.0) + EPS)
        out_ref[0] = (r * rstd * _silu(z)).astype(_MXU)

    tok = lambda w: pl.BlockSpec((1, L, w), lambda b, n: (b, n, 0))
    return pl.pallas_call(
        body, name="ret_fwd", grid=(B, NB),
        in_specs=[tok(RET_W), tok(128), tok(128), _full((4 * L, L)), _full((L, 256)), _full((L, 256)),
                  _full((1, 256))],
        out_specs=[tok(256), tok(256), pl.BlockSpec((1, 1, 256, 256), lambda b, n: (b, n, 0, 0))],
        out_shape=[jax.ShapeDtypeStruct((B, S, 256), _MXU), jax.ShapeDtypeStruct((B, S, 256), F32),
                   jax.ShapeDtypeStruct((B, NB, 256, 256), F32)],
        scratch_shapes=[pltpu.VMEM((256, 256), F32)],
        compiler_params=_cp(("parallel", "arbitrary")),
    )(ret_p, cos, sin, *consts)


def _ret_bwd(ret_p, cos, sin, raw, states, d_mix):
    B, S, _ = ret_p.shape
    L = min(RET_L, S)
    NB = S // L
    consts = _ret_consts(L)

    def body(p_ref, c_ref, s_ref, raw_ref, st_ref, dm_ref, ds_ref, qw_ref, kw_ref, a_ref, dp_ref, dst_sc):
        @pl.when(pl.program_id(1) == 0)
        def _():
            dst_sc[...] = jnp.zeros_like(dst_sc)

        mh, mv, bd = _ret_masks()
        p = p_ref[0].astype(F32)
        cs, sn = c_ref[0], s_ref[0]
        qr, kr = _ret_rope(p, cs, sn)
        v = p[:, 512:768]
        z = p[:, 768:1024]
        qs = _stack_heads(qr, mh)
        dec = ds_ref[...]
        a_s = _mm_nt(qs, kr) * dec
        r = raw_ref[0]
        rstd = lax.rsqrt(_head_mean(r * r, mv, 64.0) + EPS)
        rn = r * rstd
        dm = dm_ref[0]
        d_rn = dm * _silu(z)
        dz = dm * rn * _dsilu(z)
        dr = rstd * (d_rn - rn * _head_mean(d_rn * rn, mv, 64.0))
        do_s = _stack_heads(dr, mv)
        da_s = _mm_nt(do_s, v) * dec
        dv = _mm_tn(a_s, do_s)
        dqr = _fold_heads(_mm(da_s, kr), mh, L)
        dkr = _mm_tn(da_s, qs)
        st = st_ref[0, 0]
        qw, kw = qw_ref[...], kw_ref[...]
        dqr = dqr + _mm(dr, st) * qw
        dst_next = dst_sc[...]
        g = jnp.where(bd, dst_next, 0.0)
        kk = kr * kw
        dv = dv + _mm_nt(kk, g)
        dkr = dkr + _mm(v, g) * kw
        dst_sc[...] = dst_next * a_ref[...] + jnp.where(bd, _mm_tn(dr, qr * qw), 0.0)
        dkr = dkr * RET_KSCALE
        dq1, dq2 = dqr[:, 0:128], dqr[:, 128:256]
        dk1, dk2 = dkr[:, 0:128], dkr[:, 128:256]
        dp_ref[0] = jnp.concatenate(
            [dq1 * cs + dq2 * sn, dq2 * cs - dq1 * sn, dk1 * cs + dk2 * sn, dk2 * cs - dk1 * sn, dv, dz],
            axis=1).astype(_MXU)

    tok = lambda w: pl.BlockSpec((1, L, w), lambda b, i: (b, NB - 1 - i, 0))
    return pl.pallas_call(
        body, name="ret_bwd", grid=(B, NB),
        in_specs=[tok(RET_W), tok(128), tok(128), tok(256),
                  pl.BlockSpec((1, 1, 256, 256), lambda b, i: (b, NB - 1 - i, 0, 0)), tok(256),
                  _full((4 * L, L)), _full((L, 256)), _full((L, 256)), _full((1, 256))],
        out_specs=tok(RET_W), out_shape=jax.ShapeDtypeStruct((B, S, RET_W), _MXU),
        scratch_shapes=[pltpu.VMEM((256, 256), F32)],
        compiler_params=_cp(("parallel", "arbitrary")),
    )(ret_p, cos, sin, raw, states, d_mix, *consts)


def _gla_masks():
    C = CHUNK
    lk = lax.broadcasted_iota(jnp.int32, (1, 128), 1)
    lv = lax.broadcasted_iota(jnp.int32, (1, 256), 1)
    mk = [(lk // 32) == h for h in range(4)]
    mv = [(lv // 64) == h for h in range(4)]
    vi = lax.broadcasted_iota(jnp.int32, (256, 128), 0)
    ki = lax.broadcasted_iota(jnp.int32, (256, 128), 1)
    bd = (vi // 64) == (ki // 32)
    ri = lax.broadcasted_iota(jnp.int32, (4 * C, C), 0) % C
    cj = lax.broadcasted_iota(jnp.int32, (4 * C, C), 1)
    lower = ri >= cj
    ti = lax.broadcasted_iota(jnp.int32, (C, C), 0)
    tj = lax.broadcasted_iota(jnp.int32, (C, C), 1)
    ltri = jnp.where(ti >= tj, 1.0, 0.0).astype(F32)
    utri = jnp.where(ti <= tj, 1.0, 0.0).astype(F32)
    return mk, mv, bd, lower, ltri, utri


def _log_sigmoid(x):
    return jnp.minimum(x, 0.0) - jnp.log(1.0 + jnp.exp(-jnp.abs(x)))


GLA_G = 8


def _gla_fwd(gla_p, w_g2p, b_g2, gnw):
    B, S, _ = gla_p.shape
    C = CHUNK
    NC = S // C
    G = min(GLA_G, NC)
    NG = NC // G

    def body(p_ref, w_ref, b_ref, gn_ref, out_ref, raw_ref, st_ref, st_sc):
        @pl.when(pl.program_id(1) == 0)
        def _():
            st_sc[...] = jnp.zeros_like(st_sc)

        mk, mv, bd, lower, ltri, _ = _gla_masks()
        cs = range(G)
        rows = [slice(c * C, (c + 1) * C) for c in cs]
        ps = [p_ref[0, rows[c], :].astype(F32) for c in cs]
        pre = [_mm(ps[c][:, 512:640], w_ref[...]) + b_ref[...] for c in cs]
        cum = [_mm_f32(ltri, _log_sigmoid(pre[c]) * (1.0 / GLA_TAU)) for c in cs]
        past, fut, upd, q_pos, a_row = [], [], [], [], []
        for c in cs:
            q = ps[c][:, 0:128]
            k = ps[c][:, 128:256] * GLA_KSCALE
            last = cum[c][C - 1:C, :]
            e_pos = jnp.exp(cum[c])
            e_neg = jnp.exp(-cum[c])
            q_pos.append(q * e_pos)
            a_row.append(jnp.exp(last))
            past.append(_mm_nt(_stack_heads(q_pos[c], mk), k * e_neg))
            fut.append(_mm_nt(_stack_heads(q * e_neg, mk), k * e_pos))
            upd.append(_mm_tn(ps[c][:, 256:512], k * jnp.exp(last - cum[c])))
        o_s = [_mm(jnp.where(lower, past[c], fut[c]), ps[c][:, 256:512]) for c in cs]
        st = st_sc[...]
        inter = []
        for c in cs:
            st_ref[0, c] = st
            inter.append(_mm_nt(q_pos[c], st))
            st = st * a_row[c] + jnp.where(bd, upd[c], 0.0)
        st_sc[...] = st
        for c in cs:
            g = _fold_heads(o_s[c], mv, C) + inter[c]
            raw_ref[0, rows[c], :] = g
            rstd = lax.rsqrt(_head_mean(g * g, mv, 64.0) + EPS)
            out_ref[0, rows[c], :] = (g * rstd * gn_ref[...] * _silu(ps[c][:, 640:896])).astype(_MXU)

    tok = lambda w: pl.BlockSpec((1, G * C, w), lambda b, n: (b, n, 0))
    return pl.pallas_call(
        body, name="gla_fwd", grid=(B, NG),
        in_specs=[tok(GLA_W), _full((128, 128)), _full((1, 128)), _full((1, 256))],
        out_specs=[tok(256), tok(256), pl.BlockSpec((1, G, 256, 128), lambda b, n: (b, n, 0, 0))],
        out_shape=[jax.ShapeDtypeStruct((B, S, 256), _MXU), jax.ShapeDtypeStruct((B, S, 256), F32),
                   jax.ShapeDtypeStruct((B, NC, 256, 128), F32)],
        scratch_shapes=[pltpu.VMEM((256, 128), F32)],
        compiler_params=_cp(("parallel", "arbitrary")),
    )(gla_p, w_g2p, b_g2, gnw)


def _gla_bwd(gla_p, w_g2p, b_g2, gnw, raw, states, d_mix):
    B, S, _ = gla_p.shape
    C = CHUNK
    NC = S // C
    G = min(GLA_G, NC)
    NG = NC // G

    def body(p_ref, w_ref, b_ref, gn_ref, raw_ref, st_ref, dm_ref, dp_ref, dw_ref, db_ref, dgn_ref, dst_sc):
        first = (pl.program_id(0) == 0) & (pl.program_id(1) == 0)

        @pl.when(first)
        def _():
            dw_ref[...] = jnp.zeros_like(dw_ref)
            db_ref[...] = jnp.zeros_like(db_ref)
            dgn_ref[...] = jnp.zeros_like(dgn_ref)

        @pl.when(pl.program_id(1) == 0)
        def _():
            dst_sc[...] = jnp.zeros_like(dst_sc)

        mk, mv, bd, lower, ltri, utri = _gla_masks()
        gn = gn_ref[...]
        cs = range(G)
        rows = [slice(c * C, (c + 1) * C) for c in cs]
        ps = [p_ref[0, rows[c], :].astype(F32) for c in cs]
        vs = [ps[c][:, 256:512] for c in cs]
        pre = [_mm(ps[c][:, 512:640], w_ref[...]) + b_ref[...] for c in cs]
        cum = [_mm_f32(ltri, _log_sigmoid(pre[c]) * (1.0 / GLA_TAU)) for c in cs]
        dg, dz, dgn_acc = [], [], jnp.zeros((1, 256), F32)
        for c in cs:
            g = raw_ref[0, rows[c], :]
            z = ps[c][:, 640:896]
            rstd = lax.rsqrt(_head_mean(g * g, mv, 64.0) + EPS)
            gh = g * rstd
            dm = dm_ref[0, rows[c], :]
            d_gn = dm * _silu(z)
            dz.append(dm * gh * gn * _dsilu(z))
            dgn_acc = dgn_acc + jnp.sum(d_gn * gh, axis=0, keepdims=True)
            d_gh = d_gn * gn
            dg.append(rstd * (d_gh - gh * _head_mean(d_gh * gh, mv, 64.0)))
        do_s = [_stack_heads(dg[c], mv) for c in cs]
        dattn = [_mm_nt(do_s[c], vs[c]) for c in cs]
        ks, e_pos, e_neg, q_pos, q_neg, k_pos, k_neg, qp_s, qn_s, past, fut, a_row, w_dec, kd = ([] for _ in range(14))
        for c in cs:
            q = ps[c][:, 0:128]
            k = ps[c][:, 128:256] * GLA_KSCALE
            last = cum[c][C - 1:C, :]
            ep, en = jnp.exp(cum[c]), jnp.exp(-cum[c])
            ks.append(k), e_pos.append(ep), e_neg.append(en)
            q_pos.append(q * ep), q_neg.append(q * en), k_pos.append(k * ep), k_neg.append(k * en)
            qp_s.append(_stack_heads(q_pos[c], mk)), qn_s.append(_stack_heads(q_neg[c], mk))
            past.append(_mm_nt(qp_s[c], k_neg[c]))
            fut.append(_mm_nt(qn_s[c], k_pos[c]))
            a_row.append(jnp.exp(last))
            w_dec.append(jnp.exp(last - cum[c]))
            kd.append(k * w_dec[c])
        sts = [st_ref[0, c] for c in cs]
        dq_st = [_mm(dg[c], sts[c]) for c in cs]
        dst_in = [_mm_tn(dg[c], q_pos[c]) for c in cs]
        dv, dq_pos, dk_neg, dq_neg, dk_pos = [], [], [], [], []
        for c in cs:
            attn = jnp.where(lower, past[c], fut[c])
            dpast = jnp.where(lower, dattn[c], 0.0)
            dfut = jnp.where(lower, 0.0, dattn[c])
            dv.append(_mm_tn(attn, do_s[c]))
            dq_pos.append(_fold_heads(_mm(dpast, k_neg[c]), mk, C) + dq_st[c])
            dk_neg.append(_mm_tn(dpast, qp_s[c]))
            dq_neg.append(_fold_heads(_mm(dfut, k_pos[c]), mk, C))
            dk_pos.append(_mm_tn(dfut, qn_s[c]))
        dst_next = dst_sc[...]
        d_a, d_kd = [None] * G, [None] * G
        for c in reversed(cs):
            d_a[c] = jnp.sum(dst_next * sts[c], axis=0, keepdims=True)
            gmat = jnp.where(bd, dst_next, 0.0)
            d_kd[c] = _mm(vs[c], gmat)
            dv[c] = dv[c] + _mm_nt(kd[c], gmat)
            dst_next = dst_next * a_row[c] + jnp.where(bd, dst_in[c], 0.0)
        dst_sc[...] = dst_next
        row = lax.broadcasted_iota(jnp.int32, (C, 128), 0)
        d_la, dk, dq = [], [], []
        for c in cs:
            t = d_kd[c] * kd[c]
            dk.append(d_kd[c] * w_dec[c] + dk_neg[c] * e_neg[c] + dk_pos[c] * e_pos[c])
            dq.append(dq_pos[c] * e_pos[c] + dq_neg[c] * e_neg[c])
            d_last = jnp.sum(t, axis=0, keepdims=True) + d_a[c] * a_row[c]
            d_cum = (dq_pos[c] * q_pos[c] - dk_neg[c] * k_neg[c] - dq_neg[c] * q_neg[c] + dk_pos[c] * k_pos[c] - t)
            d_la.append(_mm_f32(utri, d_cum + jnp.where(row == C - 1, d_last, 0.0)))
        d_pre = [d_la[c] * _sig(-pre[c]) * (1.0 / GLA_TAU) for c in cs]
        d_gg = [_mm_nt(d_pre[c], w_ref[...]) for c in cs]
        dw_acc = _mm_tn(ps[0][:, 512:640], d_pre[0])
        db_acc = jnp.sum(d_pre[0], axis=0, keepdims=True)
        for c in cs[1:]:
            dw_acc = dw_acc + _mm_tn(ps[c][:, 512:640], d_pre[c])
            db_acc = db_acc + jnp.sum(d_pre[c], axis=0, keepdims=True)
        for c in cs:
            dp_ref[0, rows[c], :] = jnp.concatenate([dq[c], dk[c] * GLA_KSCALE, dv[c], d_gg[c], dz[c]],
                                                    axis=1).astype(_MXU)
        dw_ref[...] += dw_acc
        db_ref[...] += db_acc
        dgn_ref[...] += dgn_acc

        @pl.when((pl.program_id(0) == B - 1) & (pl.program_id(1) == NG - 1))
        def _():
            s1 = dgn_ref[...]
            s1 = s1 + pltpu.roll(s1, 128, 1)
            dgn_ref[...] = s1 + pltpu.roll(s1, 64, 1)

    tok = lambda w: pl.BlockSpec((1, G * C, w), lambda b, i: (b, NG - 1 - i, 0))
    return pl.pallas_call(
        body, name="gla_bwd", grid=(B, NG),
        in_specs=[tok(GLA_W), _full((128, 128)), _full((1, 128)), _full((1, 256)), tok(256),
                  pl.BlockSpec((1, G, 256, 128), lambda b, i: (b, NG - 1 - i, 0, 0)), tok(256)],
        out_specs=[tok(GLA_W), _full((128, 128)), _full((1, 128)), _full((1, 256))],
        out_shape=[jax.ShapeDtypeStruct((B, S, GLA_W), _MXU), jax.ShapeDtypeStruct((128, 128), F32),
                   jax.ShapeDtypeStruct((1, 128), F32), jax.ShapeDtypeStruct((1, 256), F32)],
        scratch_shapes=[pltpu.VMEM((256, 128), F32)],
        compiler_params=_cp(("arbitrary", "arbitrary")),
    )(gla_p, w_g2p, b_g2, gnw, raw, states, d_mix)


def _rms(x, w):
    rstd = lax.rsqrt(jnp.mean(x * x, axis=-1, keepdims=True) + EPS)
    xh = x * rstd
    return xh, rstd, xh * w


def _rms_bwd(dy, xh, rstd, w):
    dxh = dy * w
    return rstd * (dxh - xh * jnp.mean(dxh * xh, axis=-1, keepdims=True))


MLA_T = 256


def _mla_prep_fwd(mla_p, cos, sin, qnw, kvnw, w_uq, w_ukv):
    B, S, _ = mla_p.shape
    tm = min(S, 512)

    t = min(MLA_T, S)
    nt = tm // t

    def body(p_ref, c_ref, s_ref, qn_ref, kn_ref, wq_ref, wkv_ref, q_ref, k_ref, v_ref, kt_ref, vt_ref):
        p = p_ref[0].astype(F32)
        cs, sn = c_ref[0], s_ref[0]
        _, _, qn = _rms(p[:, 0:256], qn_ref[...])
        qpre = _mm(qn, wq_ref[...])
        _, _, kvn = _rms(p[:, 256:384], kn_ref[...])
        kv = _mm(kvn, wkv_ref[...])
        kpe = _rope128(p[:, 384:512], cs, sn)
        for h in range(8):
            sl = slice(128 * h, 128 * h + 128)
            q_ref[0, :, sl] = _rope128(qpre[:, sl], cs, sn).astype(_MXU)
            kh = kv[:, sl] + kpe
            k_ref[0, :, sl] = kh.astype(_MXU)
            kht = kh.T
            for n in range(nt):
                kt_ref[0, n, sl, :] = kht[:, n * t:(n + 1) * t].astype(_MXU)
        v_ref[0] = kv[:, 1024:1536].astype(_MXU)
        for pr in range(4):
            vht = kv[:, 1024 + 128 * pr:1152 + 128 * pr].T
            for n in range(nt):
                vt_ref[0, n, 128 * pr:128 * pr + 128, :] = vht[:, n * t:(n + 1) * t].astype(_MXU)

    tok = lambda w: pl.BlockSpec((1, tm, w), lambda b, i: (b, i, 0))
    tr = lambda w: pl.BlockSpec((1, nt, w, t), lambda b, i: (b, i, 0, 0))
    return pl.pallas_call(
        body, name="mla_prep_fwd", grid=(B, S // tm),
        in_specs=[tok(512), tok(128), tok(128), _full((1, 256)), _full((1, 128)), _full((256, 1024)),
                  _full((128, 1536))],
        out_specs=[tok(1024), tok(1024), tok(512), tr(1024), tr(512)],
        out_shape=[jax.ShapeDtypeStruct((B, S, 1024), _MXU), jax.ShapeDtypeStruct((B, S, 1024), _MXU),
                   jax.ShapeDtypeStruct((B, S, 512), _MXU), jax.ShapeDtypeStruct((B, S // t, 1024, t), _MXU),
                   jax.ShapeDtypeStruct((B, S // t, 512, t), _MXU)],
        compiler_params=_cp(("parallel", "parallel")),
    )(mla_p, cos, sin, qnw, kvnw, w_uq, w_ukv)


def _chunk_mask_t(t):
    kj = lax.broadcasted_iota(jnp.int32, (t, t), 0) // CHUNK
    qi = lax.broadcasted_iota(jnp.int32, (t, t), 1) // CHUNK
    return kj <= qi


MLA_HG = 4
MLA_HG_FWD = 8
LOG2E = 1.4426950408889634
MLA_C2 = MLA_SCALE * LOG2E


def _mla_attn_fwd(q, k, vt):
    B, S, _ = q.shape
    t = min(MLA_T, S)
    nq = S // t
    HG = MLA_HG_FWD
    NP = HG // 2

    def body(q_ref, k_ref, vt_ref, o_ref, lse_ref, sa, sb, m_sc, l_sc, acc_sc):
        i = pl.program_id(2)
        row = lax.broadcasted_iota(jnp.int32, (128, 1), 0)
        low = row < 64
        mask = _chunk_mask_t(t)
        m_sc[...] = jnp.full(m_sc.shape, -jnp.inf, F32)
        l_sc[...] = jnp.zeros_like(l_sc)
        acc_sc[...] = jnp.zeros_like(acc_sc)

        ones = jnp.ones((8, t), _MXU)

        def scores(j, buf):
            kb = k_ref[0, pl.ds(pl.multiple_of(j * t, t), t), :]
            for h in range(HG):
                cols = slice(128 * h, 128 * h + 128)
                buf[h] = (_mm_nt(kb[:, cols], q_ref[0, :, cols]) * MLA_C2).astype(_MXU)

        def absorb(j, buf, masked):
            vtb = vt_ref[0, j]
            for pr in range(NP):
                alphas, pvs = [], []
                for hh in range(2):
                    h = 2 * pr + hh
                    s = buf[h]
                    if masked:
                        s = jnp.where(mask, s, jnp.full_like(s, -jnp.inf))
                    m_old = m_sc[h]
                    m_new = jnp.maximum(m_old, jnp.max(s, axis=0, keepdims=True).astype(F32))
                    alpha = jnp.exp2(m_old - m_new)
                    p = jnp.exp2(s - m_new.astype(_MXU))
                    l_sc[h] = alpha * l_sc[h] + _mm(ones, p)[0:1, :]
                    m_sc[h] = m_new
                    vth = vtb[128 * pr:128 * pr + 128, :]
                    vth = jnp.where(low if hh == 0 else ~low, vth, jnp.zeros_like(vth))
                    pvs.append(_mm(vth, p))
                    alphas.append(alpha)
                acc_sc[pr] = acc_sc[pr] * jnp.where(low, alphas[0], alphas[1]) + pvs[0] + pvs[1]

        scores(0, sb)

        def pair(jj, carry):
            j0 = 2 * jj
            scores(j0 + 1, sa)
            absorb(j0, sb, False)
            scores(j0 + 2, sb)
            absorb(j0 + 1, sa, False)
            return carry

        lax.fori_loop(0, i // 2, pair, 0)

        @pl.when(i % 2 == 1)
        def _():
            scores(i, sa)
            absorb(i - 1, sb, False)
            absorb(i, sa, True)

        @pl.when(i % 2 == 0)
        def _():
            absorb(i, sb, True)

        for pr in range(NP):
            l_e, l_o = l_sc[2 * pr], l_sc[2 * pr + 1]
            o_ref[0, :, 128 * pr:128 * pr + 128] = (acc_sc[pr] / jnp.where(low, l_e, l_o)).T
            lse_ref[0, pr, 0, 0:1, :] = m_sc[2 * pr] + jnp.log(l_e) * LOG2E
            lse_ref[0, pr, 0, 1:2, :] = m_sc[2 * pr + 1] + jnp.log(l_o) * LOG2E

    return pl.pallas_call(
        body, name="mla_attn_fwd", grid=(B, 8 // HG, nq),
        in_specs=[pl.BlockSpec((1, t, 128 * HG), lambda b, g, i: (b, i, g)),
                  pl.BlockSpec((1, S, 128 * HG), lambda b, g, i: (b, 0, g)),
                  pl.BlockSpec((1, nq, 64 * HG, t), lambda b, g, i: (b, 0, g, 0))],
        out_specs=[pl.BlockSpec((1, t, 64 * HG), lambda b, g, i: (b, i, g)),
                   pl.BlockSpec((1, NP, 1, 2, t), lambda b, g, i: (b, g, i, 0, 0))],
        out_shape=[jax.ShapeDtypeStruct((B, S, 512), F32), jax.ShapeDtypeStruct((B, 4, nq, 2, t), F32)],
        scratch_shapes=[pltpu.VMEM((HG, t, t), _MXU), pltpu.VMEM((HG, t, t), _MXU), pltpu.VMEM((HG, 1, t), F32),
                        pltpu.VMEM((HG, 1, t), F32), pltpu.VMEM((NP, 128, t), F32)],
        compiler_params=_cp(("parallel", "parallel", "arbitrary")),
    )(q, k, vt)


def _mla_gate_bwd(d_mix, o, mla_p):
    B, S, _ = o.shape
    tm = min(S, 512)
    t = min(MLA_T, S)
    nt = tm // t

    def body(dm_ref, o_ref, z_ref, do_ref, dz_ref, dl_ref):
        dm, ov, z = dm_ref[0], o_ref[0], z_ref[0].astype(F32)
        do = dm * _silu(z)
        dz_ref[0] = (dm * ov * _dsilu(z)).astype(_MXU)
        do_ref[0] = do.astype(_MXU)
        prod = do * ov
        for pr in range(4):
            pt = prod[:, 128 * pr:128 * pr + 128].T
            se = jnp.sum(pt[0:64], axis=0, keepdims=True)
            so = jnp.sum(pt[64:128], axis=0, keepdims=True)
            for n in range(nt):
                dl_ref[0, pr, n, 0:1, :] = se[:, n * t:(n + 1) * t]
                dl_ref[0, pr, n, 1:2, :] = so[:, n * t:(n + 1) * t]

    tok = lambda c: pl.BlockSpec((1, tm, 512), lambda b, i: (b, i, c))
    return pl.pallas_call(
        body, name="mla_gate_bwd", grid=(B, S // tm),
        in_specs=[tok(0), tok(0), tok(1)],
        out_specs=[tok(0), tok(0), pl.BlockSpec((1, 4, nt, 2, t), lambda b, i: (b, 0, i, 0, 0))],
        out_shape=[jax.ShapeDtypeStruct((B, S, 512), _MXU), jax.ShapeDtypeStruct((B, S, 512), _MXU),
                   jax.ShapeDtypeStruct((B, 4, S // t, 2, t), F32)],
        compiler_params=_cp(("parallel", "parallel")),
    )(d_mix, o, mla_p)


def _mla_attn_bwd(q, k, v, kt, do, lse, dl):
    B, S, _ = q.shape
    t = min(MLA_T, S)
    nk = S // t

    HG = MLA_HG
    NP = HG // 2

    def body(q_ref, k_ref, v_ref, kt_ref, do_ref, lse_ref, dl_ref, dq_ref, dk_ref, dv_ref,
             sa, da, sb, db, dqt_sc, dk_sc, dv_sc):
        j = pl.program_id(2)

        @pl.when(j == 0)
        def _():
            dqt_sc[...] = jnp.zeros_like(dqt_sc)

        dk_sc[...] = jnp.zeros_like(dk_sc)
        dv_sc[...] = jnp.zeros_like(dv_sc)
        lane = lax.broadcasted_iota(jnp.int32, (1, 128), 1)
        low = lane < 64
        mask = _chunk_mask_t(t)

        def half(x, hh):
            return jnp.where(low if hh == 0 else ~low, x, jnp.zeros_like(x))

        def prepare(i, sbuf, dbuf):
            rows = pl.ds(pl.multiple_of(i * t, t), t)
            for h in range(HG):
                cols = slice(128 * h, 128 * h + 128)
                pc = slice(128 * (h // 2), 128 * (h // 2) + 128)
                sbuf[h] = _mm_nt(k_ref[0, :, cols], q_ref[0, rows, cols]) * MLA_C2
                dbuf[h] = _mm_nt(half(v_ref[0, :, pc], h % 2), do_ref[0, rows, pc])

        def absorb(i, sbuf, dbuf, masked):
            rows = pl.ds(pl.multiple_of(i * t, t), t)
            for h in range(HG):
                pr, hh = h // 2, h % 2
                cols = slice(128 * h, 128 * h + 128)
                pc = slice(128 * pr, 128 * pr + 128)
                p = jnp.exp2(sbuf[h] - lse_ref[0, pr, i][hh:hh + 1, :])
                if masked:
                    p = jnp.where(mask, p, 0.0)
                dv_sc[pr] += _mm(p, half(do_ref[0, rows, pc], hh))
                ds = p * (dbuf[h] - dl_ref[0, pr, i][hh:hh + 1, :])
                dqt_sc[i, cols, :] += _mm(kt_ref[0, 0, cols, :], ds)
                dk_sc[h] += _mm(ds, q_ref[0, rows, cols])

        n = nk - 1 - j
        prepare(jnp.minimum(j + 1, nk - 1), sb, db)

        def pair(jj, carry):
            i0 = j + 1 + 2 * jj
            prepare(i0 + 1, sa, da)
            absorb(i0, sb, db, False)
            prepare(jnp.where(i0 + 2 <= nk - 1, i0 + 2, j), sb, db)
            absorb(i0 + 1, sa, da, False)
            return carry

        lax.fori_loop(0, n // 2, pair, 0)

        @pl.when(n % 2 == 1)
        def _():
            prepare(j, sa, da)
            absorb(nk - 1, sb, db, False)
            absorb(j, sa, da, True)

        @pl.when(n % 2 == 0)
        def _():
            absorb(j, sb, db, True)

        for h in range(HG):
            dk_ref[0, :, 128 * h:128 * h + 128] = (dk_sc[h] * MLA_SCALE).astype(_MXU)
        for pr in range(NP):
            dv_ref[0, :, 128 * pr:128 * pr + 128] = dv_sc[pr].astype(_MXU)

        @pl.when(j == nk - 1)
        def _():
            for i in range(nk):
                dq_ref[0, i * t:(i + 1) * t, :] = (dqt_sc[i].T * MLA_SCALE).astype(_MXU)

    seq = lambda w: pl.BlockSpec((1, S, w), lambda b, g, j: (b, 0, g))
    blk = lambda w: pl.BlockSpec((1, t, w), lambda b, g, j: (b, j, g))
    stat = pl.BlockSpec((1, NP, nk, 2, t), lambda b, g, j: (b, g, 0, 0, 0))
    return pl.pallas_call(
        body, name="mla_attn_bwd", grid=(B, 8 // HG, nk),
        in_specs=[seq(128 * HG), blk(128 * HG), blk(64 * HG),
                  pl.BlockSpec((1, 1, 128 * HG, t), lambda b, g, j: (b, j, g, 0)), seq(64 * HG), stat, stat],
        out_specs=[seq(128 * HG), blk(128 * HG), blk(64 * HG)],
        out_shape=[jax.ShapeDtypeStruct((B, S, 1024), _MXU), jax.ShapeDtypeStruct((B, S, 1024), _MXU),
                   jax.ShapeDtypeStruct((B, S, 512), _MXU)],
        scratch_shapes=[pltpu.VMEM((HG, t, t), F32), pltpu.VMEM((HG, t, t), F32), pltpu.VMEM((HG, t, t), F32),
                        pltpu.VMEM((HG, t, t), F32), pltpu.VMEM((nk, 128 * HG, t), F32),
                        pltpu.VMEM((HG, t, 128), F32), pltpu.VMEM((NP, t, 128), F32)],
        compiler_params=_cp(("parallel", "parallel", "arbitrary")),
    )(q, k, v, kt, do, lse, dl)


def _mla_prep_bwd(mla_p, cos, sin, qnw, kvnw, w_uq, w_ukv, dq, dk, dv):
    B, S, _ = mla_p.shape
    tm = min(S, 512)

    def body(p_ref, c_ref, s_ref, qn_ref, kn_ref, wq_ref, wkv_ref, dq_ref, dk_ref, dv_ref,
             dp_ref, dwq_ref, dwkv_ref, dqn_ref, dkn_ref):
        first = (pl.program_id(0) == 0) & (pl.program_id(1) == 0)

        @pl.when(first)
        def _():
            dwq_ref[...] = jnp.zeros_like(dwq_ref)
            dwkv_ref[...] = jnp.zeros_like(dwkv_ref)
            dqn_ref[...] = jnp.zeros_like(dqn_ref)
            dkn_ref[...] = jnp.zeros_like(dkn_ref)

        p = p_ref[0].astype(F32)
        cs, sn = c_ref[0], s_ref[0]
        lane = lax.broadcasted_iota(jnp.int32, (1, 128), 1)
        pe = (lane >= 64) & (lane < 96)
        qh, q_rstd, qn = _rms(p[:, 0:256], qn_ref[...])
        kvh, kv_rstd, kvn = _rms(p[:, 256:384], kn_ref[...])
        dqv = dq_ref[0].astype(F32)
        dkv = dk_ref[0].astype(F32)
        dqpre = jnp.concatenate(
            [_rope128_t(dqv[:, 128 * h:128 * h + 128], cs, sn) for h in range(8)], axis=1)
        dkpe = jnp.zeros((tm, 128), F32)
        for h in range(8):
            dkpe = dkpe + jnp.where(pe, dkv[:, 128 * h:128 * h + 128], 0.0)
        dkr = _rope128_t(dkpe, cs, sn)
        dkv_all = jnp.concatenate([dkv, dv_ref[0].astype(F32)], axis=1)
        d_qn = _mm_nt(dqpre, wq_ref[...])
        d_kvn = _mm_nt(dkv_all, wkv_ref[...])
        dwq_ref[...] += _mm_tn(qn, dqpre)
        dwkv_ref[...] += _mm_tn(kvn, dkv_all)
        dqn_ref[...] += jnp.sum(d_qn * qh, axis=0, keepdims=True)
        dkn_ref[...] += jnp.sum(d_kvn * kvh, axis=0, keepdims=True)
        dp_ref[0] = jnp.concatenate([_rms_bwd(d_qn, qh, q_rstd, qn_ref[...]),
                                     _rms_bwd(d_kvn, kvh, kv_rstd, kn_ref[...]), dkr], axis=1).astype(_MXU)

    tok = lambda w: pl.BlockSpec((1, tm, w), lambda b, i: (b, i, 0))
    return pl.pallas_call(
        body, name="mla_prep_bwd", grid=(B, S // tm),
        in_specs=[tok(512), tok(128), tok(128), _full((1, 256)), _full((1, 128)), _full((256, 1024)),
                  _full((128, 1536)), tok(1024), tok(1024), tok(512)],
        out_specs=[tok(512), _full((256, 1024)), _full((128, 1536)), _full((1, 256)), _full((1, 128))],
        out_shape=[jax.ShapeDtypeStruct((B, S, 512), _MXU), jax.ShapeDtypeStruct((256, 1024), F32),
                   jax.ShapeDtypeStruct((128, 1536), F32), jax.ShapeDtypeStruct((1, 256), F32),
                   jax.ShapeDtypeStruct((1, 128), F32)],
        compiler_params=_cp(("arbitrary", "arbitrary")),
    )(mla_p, cos, sin, qnw, kvnw, w_uq, w_ukv, dq, dk, dv)


def _out_fwd(x, gate, r_g, o_mla, mla_p, g_g, w_out):
    B, S, D = x.shape
    tm = min(S, 512)

    def body(x_ref, g_ref, r_ref, o_ref, z_ref, gg_ref, w_ref, xn_ref, y_ref, mm_ref):
        mm = (o_ref[0] * _silu(z_ref[0].astype(F32))).astype(_MXU)
        mm_ref[0] = mm
        y = (jnp.dot(r_ref[0], w_ref[0:256, :], preferred_element_type=F32)
             + jnp.dot(mm, w_ref[256:768, :], preferred_element_type=F32)
             + jnp.dot(gg_ref[0], w_ref[768:1024, :], preferred_element_type=F32))
        y_ref[0] = y
        xn_ref[0] = x_ref[0] + g_ref[0] * y

    tok = lambda w, c=0: pl.BlockSpec((1, tm, w), lambda b, i: (b, i, c))
    return pl.pallas_call(
        body, name="out_fwd", grid=(B, S // tm),
        in_specs=[tok(D), pl.BlockSpec((1, 1, D), lambda b, i: (b, 0, 0)), tok(256), tok(512), tok(512, 1),
                  tok(256), _full((D, D))],
        out_specs=[tok(D), tok(D), tok(512)],
        out_shape=[jax.ShapeDtypeStruct((B, S, D), F32), jax.ShapeDtypeStruct((B, S, D), F32),
                   jax.ShapeDtypeStruct((B, S, 512), _MXU)],
        compiler_params=_cp(("parallel", "parallel")),
    )(x, gate, r_g, o_mla, mla_p, g_g, w_out)


def _out_bwd(dx, y, gate, r_g, mm, g_g, w_out):
    B, S, D = dx.shape
    tm = min(S, 512)

    def body(dx_ref, y_ref, g_ref, r_ref, mm_ref, gg_ref, w_ref, dr_ref, dmm_ref, dg_ref, dw_ref, dgate_ref):
        first = (pl.program_id(0) == 0) & (pl.program_id(1) == 0)

        @pl.when(first)
        def _():
            dw_ref[...] = jnp.zeros_like(dw_ref)

        @pl.when(pl.program_id(1) == 0)
        def _():
            dgate_ref[...] = jnp.zeros_like(dgate_ref)

        dxv = dx_ref[0]
        dgate_ref[0] += jnp.sum(dxv * y_ref[0], axis=0, keepdims=True)
        dy = (dxv * g_ref[0]).astype(_MXU)
        dr_ref[0] = _mm_nt(dy, w_ref[0:256, :])
        dmm_ref[0] = _mm_nt(dy, w_ref[256:768, :])
        dg_ref[0] = _mm_nt(dy, w_ref[768:1024, :])
        dw_ref[0:256, :] += _mm_tn(r_ref[0], dy)
        dw_ref[256:768, :] += _mm_tn(mm_ref[0], dy)
        dw_ref[768:1024, :] += _mm_tn(gg_ref[0], dy)

    tok = lambda w: pl.BlockSpec((1, tm, w), lambda b, i: (b, i, 0))
    per_seq = pl.BlockSpec((1, 1, D), lambda b, i: (b, 0, 0))
    return pl.pallas_call(
        body, name="out_bwd", grid=(B, S // tm),
        in_specs=[tok(D), tok(D), per_seq, tok(256), tok(512), tok(256), _full((D, D))],
        out_specs=[tok(256), tok(512), tok(256), _full((D, D)), per_seq],
        out_shape=[jax.ShapeDtypeStruct((B, S, 256), F32), jax.ShapeDtypeStruct((B, S, 512), F32),
                   jax.ShapeDtypeStruct((B, S, 256), F32), jax.ShapeDtypeStruct((D, D), F32),
                   jax.ShapeDtypeStruct((B, 1, D), F32)],
        compiler_params=_cp(("arbitrary", "arbitrary")),
    )(dx, y, gate, r_g, mm, g_g, w_out)


def _proj_bwd_x(x, shift, scale, nw, w_arr, d_ret, d_mla, d_mz, d_gla, dx_out):
    B, S, D = x.shape
    tm = min(S, 512)

    def body(x_ref, sc_ref, nw_ref, w_ref, dr_ref, dm_ref, dz_ref, dg_ref, dxo_ref,
             dx_ref, dsh_ref, dsc_ref, dnw_ref):
        first = (pl.program_id(0) == 0) & (pl.program_id(1) == 0)

        @pl.when(first)
        def _():
            dnw_ref[...] = jnp.zeros_like(dnw_ref)

        @pl.when(pl.program_id(1) == 0)
        def _():
            dsh_ref[...] = jnp.zeros_like(dsh_ref)
            dsc_ref[...] = jnp.zeros_like(dsc_ref)

        dp = jnp.concatenate([dr_ref[0], dm_ref[0], dz_ref[0], dg_ref[0]], axis=1)
        dh = lax.dot_general(dp, w_ref[...], (((1,), (1,)), ((), ())), preferred_element_type=F32)
        xv = x_ref[0]
        rstd = lax.rsqrt(jnp.mean(xv * xv, axis=-1, keepdims=True) + EPS)
        xh = xv * rstd
        nwv = nw_ref[...]
        mod = 1.0 + sc_ref[0]
        dsh_ref[0] += jnp.sum(dh, axis=0, keepdims=True)
        dsc_ref[0] += jnp.sum(dh * xh * nwv, axis=0, keepdims=True)
        dnw_ref[...] += jnp.sum(dh * xh * mod, axis=0, keepdims=True)
        dxh = dh * nwv * mod
        dx_ref[0] = dxo_ref[0] + rstd * (dxh - xh * jnp.mean(dxh * xh, axis=-1, keepdims=True))

    tok = lambda w: pl.BlockSpec((1, tm, w), lambda b, i: (b, i, 0))
    per_seq = pl.BlockSpec((1, 1, D), lambda b, i: (b, 0, 0))
    return pl.pallas_call(
        body, name="proj_bwd_x", grid=(B, S // tm),
        in_specs=[tok(D), per_seq, _full((1, D)), _full((D, ARR_W)), tok(RET_W), tok(512), tok(512),
                  tok(GLA_W), tok(D)],
        out_specs=[tok(D), per_seq, per_seq, _full((1, D))],
        out_shape=[jax.ShapeDtypeStruct((B, S, D), F32), jax.ShapeDtypeStruct((B, 1, D), F32),
                   jax.ShapeDtypeStruct((B, 1, D), F32), jax.ShapeDtypeStruct((1, D), F32)],
        compiler_params=_cp(("arbitrary", "arbitrary")),
    )(x, scale, nw, w_arr, d_ret, d_mla, d_mz, d_gla, dx_out)


def _proj_bwd_w(h, d_ret, d_mla, d_mz, d_gla):
    B, S, D = h.shape
    tm = min(S, 512)

    def body(h_ref, dr_ref, dm_ref, dz_ref, dg_ref, dw_ref):
        first = (pl.program_id(0) == 0) & (pl.program_id(1) == 0)

        @pl.when(first)
        def _():
            dw_ref[...] = jnp.zeros_like(dw_ref)

        hv = h_ref[0]
        tn = lambda d_ref: lax.dot_general(hv, d_ref[0], (((0,), (0,)), ((), ())), preferred_element_type=F32)
        dw_ref[:, 0:RET_W] += tn(dr_ref)
        dw_ref[:, RET_W:RET_W + 512] += tn(dm_ref)
        dw_ref[:, RET_W + 512:RET_W + MLA_W] += tn(dz_ref)
        dw_ref[:, RET_W + MLA_W:ARR_W] += tn(dg_ref)

    tok = lambda w: pl.BlockSpec((1, tm, w), lambda b, i: (b, i, 0))
    return pl.pallas_call(
        body, name="proj_bwd_w", grid=(B, S // tm),
        in_specs=[tok(D), tok(RET_W), tok(512), tok(512), tok(GLA_W)],
        out_specs=_full((D, ARR_W)), out_shape=jax.ShapeDtypeStruct((D, ARR_W), F32),
        compiler_params=_cp(("arbitrary", "arbitrary"), 56),
    )(h, d_ret, d_mla, d_mz, d_gla)


def _final_loss(x, fw, target):
    B, S, D = x.shape
    tm = min(S, 512)

    def body(x_ref, fw_ref, t_ref, dx_ref, loss_ref, dfw_ref):
        first = (pl.program_id(0) == 0) & (pl.program_id(1) == 0)

        @pl.when(first)
        def _():
            loss_ref[...] = jnp.zeros_like(loss_ref)
            dfw_ref[...] = jnp.zeros_like(dfw_ref)

        xv = x_ref[0]
        fwv = fw_ref[...]
        rstd = lax.rsqrt(jnp.mean(xv * xv, axis=-1, keepdims=True) + EPS)
        xh = xv * rstd
        err = xh * fwv - t_ref[0]
        loss_ref[...] += 0.5 * jnp.sum(jnp.mean(err * err, axis=-1, keepdims=True), axis=0, keepdims=True)
        dy = err * (1.0 / D)
        dfw_ref[...] += jnp.sum(dy * xh, axis=0, keepdims=True)
        dxh = dy * fwv
        dx_ref[0] = rstd * (dxh - xh * jnp.mean(dxh * xh, axis=-1, keepdims=True))

    tok = pl.BlockSpec((1, tm, D), lambda b, i: (b, i, 0))
    return pl.pallas_call(
        body, name="final_loss", grid=(B, S // tm),
        in_specs=[tok, _full((1, D)), tok],
        out_specs=[tok, _full((1, 1)), _full((1, D))],
        out_shape=[jax.ShapeDtypeStruct((B, S, D), F32), jax.ShapeDtypeStruct((1, 1), F32),
                   jax.ShapeDtypeStruct((1, D), F32)],
        compiler_params=_cp(("arbitrary", "arbitrary")),
    )(x, fw, target)


def _local_step(x, pos3, mod, loss_target, small, w_in_a, w_uq_a, w_ukv_a, w_out_b):
    B, S, D = x.shape
    tabs = _rope_tables(pos3)
    saved = []
    for l in range(DEPTH):
        x, s = _layer_fwd(x, tabs, mod[l], {n: a[l] for n, a in small.items() if n != "final_norm"},
                          w_in_a[l], w_uq_a[l], w_ukv_a[l], w_out_b[l])
        saved.append(s)
    dx, loss, d_fw = _final_loss(x, small["final_norm"].reshape(1, D), loss_target)
    grads = dict(final_norm=d_fw.reshape(D))
    per_layer = [None] * DEPTH
    for l in reversed(range(DEPTH)):
        dx, per_layer[l] = _layer_bwd(dx, saved[l], tabs)
    for name in per_layer[0]:
        grads[name] = jnp.stack([per_layer[l][name] for l in range(DEPTH)])
    return loss, dx, grads


def _layer_fwd(x, tabs, mod_l, small_l, w_in_a, w_uq_a=None, w_ukv_a=None, w_out_b=None, late_weights=None):
    B, S, D = x.shape
    cr, sr, cm, sm = tabs
    shift = mod_l[:, 0:D].reshape(B, 1, D)
    scale = mod_l[:, D:2 * D].reshape(B, 1, D)
    gate = mod_l[:, 2 * D:3 * D].reshape(B, 1, D)
    nw = small_l["norm_w"].reshape(1, D)
    qnw = small_l["mla_q_norm"].reshape(1, 256)
    kvnw = small_l["mla_kv_norm"].reshape(1, 128)
    w_g2p = jnp.pad(small_l["gla_w_g2"], ((0, 112), (0, 0)))
    b_g2 = small_l["gla_b_g2"].reshape(1, 128)
    gnw = jnp.tile(small_l["gla_norm"], 4).reshape(1, 256)
    ret_p, mla_p, gla_p, h = _proj_fwd(x, shift, scale, nw, w_in_a)
    r_g, r_raw, r_st = _ret_fwd(ret_p, cr, sr)
    if late_weights is not None:
        w_uq_a, w_ukv_a, w_out_b = late_weights(r_raw)
    q, k, v, kt, vt = _mla_prep_fwd(mla_p, cm, sm, qnw, kvnw, w_uq_a, w_ukv_a)
    o_mla, lse = _mla_attn_fwd(q, k, vt)
    g_g, g_raw, g_st = _gla_fwd(gla_p, w_g2p, b_g2, gnw)
    x_new, y, mm = _out_fwd(x, gate, r_g, o_mla, mla_p, g_g, w_out_b)
    saved = dict(x=x, shift=shift, scale=scale, gate=gate, nw=nw, qnw=qnw, kvnw=kvnw, w_g2p=w_g2p, b_g2=b_g2,
                 gnw=gnw, ret_p=ret_p, mla_p=mla_p, gla_p=gla_p, h=h, r_g=r_g, r_raw=r_raw, r_st=r_st, q=q, k=k,
                 v=v, kt=kt, o_mla=o_mla, lse=lse, g_g=g_g, g_raw=g_raw, g_st=g_st, y=y, mm=mm,
                 w_in_a=w_in_a, w_uq_a=w_uq_a, w_ukv_a=w_ukv_a, w_out_b=w_out_b)
    return x_new, saved


def _layer_bwd(dx, s, tabs, early_grads=None):
    B, S, D = dx.shape
    cr, sr, cm, sm = tabs
    d_r, d_mm, d_g, dw_out, d_gate = _out_bwd(dx, s["y"], s["gate"], s["r_g"], s["mm"], s["g_g"], s["w_out_b"])
    d_ret = _ret_bwd(s["ret_p"], cr, sr, s["r_raw"], s["r_st"], d_r)
    do, d_mz, dl = _mla_gate_bwd(d_mm, s["o_mla"], s["mla_p"])
    dq, dk, dv = _mla_attn_bwd(s["q"], s["k"], s["v"], s["kt"], do, s["lse"], dl)
    d_mla, dw_uq, dw_ukv, d_qnw, d_kvnw = _mla_prep_bwd(
        s["mla_p"], cm, sm, s["qnw"], s["kvnw"], s["w_uq_a"], s["w_ukv_a"], dq, dk, dv)
    gnw = s["gnw"] if early_grads is None else s["gnw"] + early_grads(dw_out, dw_uq, dw_ukv)
    d_gla, dw_g2p, db_g2, d_gnw = _gla_bwd(s["gla_p"], s["w_g2p"], s["b_g2"], gnw, s["g_raw"], s["g_st"], d_g)
    dx, d_shift, d_scale, d_nw = _proj_bwd_x(s["x"], s["shift"], s["scale"], s["nw"], s["w_in_a"],
                                             d_ret, d_mla, d_mz, d_gla, dx)
    dw_in = _proj_bwd_w(s["h"], d_ret, d_mla, d_mz, d_gla)
    grads = dict(
        d_mod=jnp.concatenate([d_shift, d_scale, d_gate], axis=2).reshape(B, 3 * D),
        norm_w=d_nw.reshape(D), mla_q_norm=d_qnw.reshape(256), mla_kv_norm=d_kvnw.reshape(128),
        gla_w_g2=dw_g2p[0:16], gla_b_g2=db_g2.reshape(128), gla_norm256=d_gnw.reshape(256),
        w_in_a=dw_in, w_uq_a=dw_uq, w_ukv_a=dw_ukv, w_out=dw_out)
    return dx, grads


def _exchange(arrs, gather, name):
    n = len(arrs)
    out_shape = [jax.ShapeDtypeStruct(((N_DEV,) + a.shape) if g else a.shape, a.dtype)
                 for a, g in zip(arrs, gather)]

    def body(*refs):
        ins, outs = refs[:n], refs[n:2 * n]
        send_sems, recv_sems, local_sems = refs[2 * n:]
        ix, iy, ic = lax.axis_index("x"), lax.axis_index("y"), lax.axis_index("c")
        me = 4 * ix + 2 * iy + ic
        copies = []
        for a in range(n):
            mine = ins[a] if gather[a] else ins[a].at[me]
            loc = pltpu.make_async_copy(mine, outs[a].at[me], local_sems.at[a])
            loc.start()
            copies.append(loc)
            for d in range(1, N_DEV):
                px = 1 - ix if d & 4 else ix
                py = 1 - iy if d & 2 else iy
                pc = 1 - ic if d & 1 else ic
                src = ins[a] if gather[a] else ins[a].at[4 * px + 2 * py + pc]
                cp = pltpu.make_async_remote_copy(
                    src_ref=src, dst_ref=outs[a].at[me], send_sem=send_sems.at[a, d - 1],
                    recv_sem=recv_sems.at[a, d - 1], device_id=(px, py, pc), device_id_type=pl.DeviceIdType.MESH)
                cp.start()
                copies.append(cp)
        for cp in copies:
            cp.wait()

    any_spec = pl.BlockSpec(memory_space=pl.ANY)
    outs = pl.pallas_call(
        body, name=name, in_specs=[any_spec] * n, out_specs=[any_spec] * n, out_shape=out_shape,
        scratch_shapes=[pltpu.SemaphoreType.DMA((n, N_DEV - 1)), pltpu.SemaphoreType.DMA((n, N_DEV - 1)),
                        pltpu.SemaphoreType.DMA((n,))],
    )(*arrs)
    return list(outs)


def _peers(ix, iy, ic):
    out = []
    for d in range(1, N_DEV):
        px = 1 - ix if d & 4 else ix
        py = 1 - iy if d & 2 else iy
        pc = 1 - ic if d & 1 else ic
        out.append((d - 1, (px, py, pc), 4 * px + 2 * py + pc))
    return out


def _exchange_start(arrs, gather, name):
    n = len(arrs)
    lands = [lax.empty(((N_DEV,) + a.shape) if g else a.shape, a.dtype) for a, g in zip(arrs, gather)]

    def body(*refs):
        ins, land_refs = refs[:n], refs[n:2 * n]
        send_sems, recv_sems = refs[2 * n], refs[2 * n + 1]
        token = refs[-1]
        ix, iy, ic = lax.axis_index("x"), lax.axis_index("y"), lax.axis_index("c")
        me = 4 * ix + 2 * iy + ic
        for a in range(n):
            for k, peer, peer_idx in _peers(ix, iy, ic):
                pltpu.make_async_remote_copy(
                    src_ref=ins[a] if gather[a] else ins[a].at[peer_idx], dst_ref=land_refs[a].at[me],
                    send_sem=send_sems.at[7 * a + k], recv_sem=recv_sems.at[7 * a + k], device_id=peer,
                    device_id_type=pl.DeviceIdType.MESH).start()
        token[...] = jnp.zeros_like(token)

    hbm = pl.BlockSpec(memory_space=pltpu.HBM)
    sem = pl.BlockSpec(memory_space=pltpu.SEMAPHORE)
    held = [pltpu.with_memory_space_constraint(a, pltpu.HBM) for a in list(arrs) + lands]
    outs = pl.pallas_call(
        body, name=name,
        out_shape=(pltpu.SemaphoreType.DMA((7 * n,)), pltpu.SemaphoreType.DMA((7 * n,)),
                   *[pltpu.HBM(a.shape, a.dtype) for a in held], jax.ShapeDtypeStruct((8, 128), F32)),
        in_specs=[hbm] * (2 * n), out_specs=(sem, sem, *[hbm] * (2 * n), pl.BlockSpec(memory_space=pltpu.VMEM)),
        input_output_aliases={a: 2 + a for a in range(2 * n)},
        compiler_params=pltpu.CompilerParams(has_side_effects=pltpu.SideEffectType.DATAFLOW_SIDE_EFFECTING),
    )(*held)
    return dict(send=outs[0], recv=outs[1], srcs=list(outs[2:2 + n]), lands=list(outs[2 + n:2 + 2 * n]),
                token=outs[-1], gather=list(gather))


def _exchange_wait(flight, after, me, name):
    n = len(flight["srcs"])
    gather = flight["gather"]

    def body(*refs):
        srcs, land_refs = refs[:n], refs[n:2 * n]
        send_sems, recv_sems = refs[2 * n], refs[2 * n + 1]
        ix, iy, ic = lax.axis_index("x"), lax.axis_index("y"), lax.axis_index("c")
        mine = 4 * ix + 2 * iy + ic
        for a in range(n):
            for k, peer, peer_idx in _peers(ix, iy, ic):
                cp = pltpu.make_async_remote_copy(
                    src_ref=srcs[a] if gather[a] else srcs[a].at[peer_idx], dst_ref=land_refs[a].at[mine],
                    send_sem=send_sems.at[7 * a + k], recv_sem=recv_sems.at[7 * a + k], device_id=peer,
                    device_id_type=pl.DeviceIdType.MESH)
                cp.wait_send()
                cp.wait_recv()

    hbm = pl.BlockSpec(memory_space=pltpu.HBM)
    sem = pl.BlockSpec(memory_space=pltpu.SEMAPHORE)
    held = flight["srcs"] + flight["lands"]
    outs = pl.pallas_call(
        body, name=name, out_shape=tuple(pltpu.HBM(a.shape, a.dtype) for a in held),
        in_specs=[hbm] * (2 * n) + [sem, sem, pl.BlockSpec(memory_space=pl.ANY)], out_specs=tuple([hbm] * (2 * n)),
        input_output_aliases={a: a for a in range(2 * n)},
        compiler_params=pltpu.CompilerParams(has_side_effects=pltpu.SideEffectType.DATAFLOW_SIDE_EFFECTING),
    )(*held, flight["send"], flight["recv"], after)
    got = []
    for a in range(n):
        src, land = outs[a], outs[n + a]
        own = src if gather[a] else lax.dynamic_index_in_dim(src, me, axis=0, keepdims=False)
        got.append(lax.dynamic_update_index_in_dim(land, own, me, axis=0))
    return got


def _ada_fwd(c_all, ada_w, ada_b_cols):
    nb, D = c_all.shape
    cols = ada_w.shape[2]

    def body(c_ref, w_ref, b_ref, out_ref):
        ca = _silu(c_ref[...])
        for l in range(DEPTH):
            out_ref[l] = _mm(ca, w_ref[l]) + b_ref[l:l + 1, :]

    return pl.pallas_call(
        body, name="ada_fwd", out_shape=jax.ShapeDtypeStruct((DEPTH, nb, cols), F32),
        in_specs=[pl.BlockSpec(memory_space=pltpu.VMEM)] * 3, out_specs=pl.BlockSpec(memory_space=pltpu.VMEM),
        compiler_params=pltpu.CompilerParams(vmem_limit_bytes=32 * VMEM_MB),
    )(c_all, ada_w, ada_b_cols)


def _ada_bwd(c_all, d_mod_cols):
    nb, D = c_all.shape
    cols = d_mod_cols.shape[2]

    def body(c_ref, dm_ref, out_ref):
        ca = _silu(c_ref[...])
        for l in range(DEPTH):
            out_ref[l] = _mm_tn(ca, dm_ref[l])

    return pl.pallas_call(
        body, name="ada_bwd", out_shape=jax.ShapeDtypeStruct((DEPTH, D, cols), F32),
        in_specs=[pl.BlockSpec(memory_space=pltpu.VMEM)] * 2, out_specs=pl.BlockSpec(memory_space=pltpu.VMEM),
        compiler_params=pltpu.CompilerParams(vmem_limit_bytes=32 * VMEM_MB),
    )(c_all, d_mod_cols)


def _sum_adamw(parts, w, m, v, name):
    P, R, C = parts.shape
    tr = 256 if (R % 256 == 0 and R > 256) else R

    def body(p_ref, w_ref, m_ref, v_ref, g_ref, d_ref, nm_ref, nv_ref):
        g = p_ref[0].astype(F32)
        for k in range(1, P):
            g = g + p_ref[k].astype(F32)
        g_ref[...] = g
        nm = ADAM_B1 * m_ref[...] + (1.0 - ADAM_B1) * g
        nv = ADAM_B2 * v_ref[...] + (1.0 - ADAM_B2) * (g * g)
        nm_ref[...] = nm
        nv_ref[...] = nv
        m_hat = nm / (1.0 - ADAM_B1 ** ADAM_STEP)
        v_hat = nv / (1.0 - ADAM_B2 ** ADAM_STEP)
        d_ref[...] = -ADAM_LR * (m_hat / (jnp.sqrt(v_hat) + ADAM_EPS) + ADAM_WD * w_ref[...])

    blk = pl.BlockSpec((tr, C), lambda i: (i, 0))
    shp = jax.ShapeDtypeStruct((R, C), F32)
    return pl.pallas_call(
        body, name=name, grid=(R // tr,),
        in_specs=[pl.BlockSpec((P, tr, C), lambda i: (0, i, 0)), blk, blk, blk],
        out_specs=[blk, blk, blk, blk], out_shape=[shp, shp, shp, shp],
        compiler_params=_cp(("parallel",)),
    )(parts, w, m, v)


SMALL = ["norm_w", "mla_q_norm", "mla_kv_norm", "gla_w_g2", "gla_b_g2", "gla_norm", "final_norm"]


SMALL_ROWS = 72


def _pack_small(loss, part):
    flat = [jnp.pad(loss.reshape(1), (0, 127))] + [part[n].reshape(-1) for n in SMALL]
    used = sum(f.shape[0] for f in flat)
    flat.append(jnp.zeros((SMALL_ROWS * 128 - used,), F32))
    return jnp.concatenate(flat).reshape(SMALL_ROWS, 128)


def _small_adamw(packed_parts, w, m, v):
    n = len(w)

    def body(*refs):
        p_ref = refs[0]
        w_refs, m_refs, v_refs = refs[1:1 + n], refs[1 + n:1 + 2 * n], refs[1 + 2 * n:1 + 3 * n]
        outs, acc = refs[1 + 3 * n:-1], refs[-1]
        total = p_ref[0]
        for k in range(1, N_DEV):
            total = total + p_ref[k]
        acc[...] = total
        outs[0][...] = acc[0:1, :]
        r0 = 1
        for i in range(n):
            shp = w_refs[i].shape
            if len(shp) == 3:
                g = acc[r0:r0 + shp[0] * shp[1], :].reshape(shp)
                r0 += shp[0] * shp[1]
            elif shp[1] < 128:
                g = acc[r0:r0 + shp[0], 0:shp[1]]
                r0 += shp[0]
            else:
                k = shp[1] // 128
                g = jnp.concatenate(
                    [jnp.concatenate([acc[r0 + l * k + j:r0 + l * k + j + 1, :] for j in range(k)], axis=1)
                     for l in range(shp[0])], axis=0)
                r0 += shp[0] * k
            nm = ADAM_B1 * m_refs[i][...] + (1.0 - ADAM_B1) * g
            nv = ADAM_B2 * v_refs[i][...] + (1.0 - ADAM_B2) * (g * g)
            m_hat = nm / (1.0 - ADAM_B1 ** ADAM_STEP)
            v_hat = nv / (1.0 - ADAM_B2 ** ADAM_STEP)
            outs[1 + 4 * i][...] = g
            outs[2 + 4 * i][...] = -ADAM_LR * (m_hat / (jnp.sqrt(v_hat) + ADAM_EPS) + ADAM_WD * w_refs[i][...])
            outs[3 + 4 * i][...] = nm
            outs[4 + 4 * i][...] = nv

    vmem = pl.BlockSpec(memory_space=pltpu.VMEM)
    out_shape = [jax.ShapeDtypeStruct((1, 128), F32)]
    for a in w:
        out_shape += [jax.ShapeDtypeStruct(a.shape, F32)] * 4
    outs = pl.pallas_call(
        body, name="adamw_small", in_specs=[vmem] * (1 + 3 * n), out_specs=[vmem] * (1 + 4 * n), out_shape=out_shape,
        scratch_shapes=[pltpu.VMEM((SMALL_ROWS, 128), F32)],
    )(packed_parts, *w, *m, *v)
    return outs[0], [outs[1 + 4 * i:5 + 4 * i] for i in range(n)]


WEIGHTS = ["norm_w", "ada_w", "ada_b", "w_in", "mla_q_norm", "w_uq", "mla_kv_norm", "w_ukv", "gla_w_g2",
           "gla_b_g2", "gla_norm", "w_out", "final_norm"]


def kernel(x, c, positions, norm_w, ada_w, ada_b, w_in, mla_q_norm, w_uq, mla_kv_norm, w_ukv, gla_w_g2, gla_b_g2, gla_norm, w_out, final_norm, loss_target, m_norm_w, m_ada_w, m_ada_b, m_w_in, m_mla_q_norm, m_w_uq, m_mla_kv_norm, m_w_ukv, m_gla_w_g2, m_gla_b_g2, m_gla_norm, m_w_out, m_final_norm, v_norm_w, v_ada_w, v_ada_b, v_w_in, v_mla_q_norm, v_w_uq, v_mla_kv_norm, v_w_ukv, v_gla_w_g2, v_gla_b_g2, v_gla_norm, v_w_out, v_final_norm):
    w = dict(norm_w=norm_w, ada_w=ada_w, ada_b=ada_b, w_in=w_in, mla_q_norm=mla_q_norm, w_uq=w_uq,
             mla_kv_norm=mla_kv_norm, w_ukv=w_ukv, gla_w_g2=gla_w_g2, gla_b_g2=gla_b_g2, gla_norm=gla_norm,
             w_out=w_out, final_norm=final_norm)
    m = dict(norm_w=m_norm_w, ada_w=m_ada_w, ada_b=m_ada_b, w_in=m_w_in, mla_q_norm=m_mla_q_norm, w_uq=m_w_uq,
             mla_kv_norm=m_mla_kv_norm, w_ukv=m_w_ukv, gla_w_g2=m_gla_w_g2, gla_b_g2=m_gla_b_g2,
             gla_norm=m_gla_norm, w_out=m_w_out, final_norm=m_final_norm)
    v = dict(norm_w=v_norm_w, ada_w=v_ada_w, ada_b=v_ada_b, w_in=v_w_in, mla_q_norm=v_mla_q_norm, w_uq=v_w_uq,
             mla_kv_norm=v_mla_kv_norm, w_ukv=v_w_ukv, gla_w_g2=v_gla_w_g2, gla_b_g2=v_gla_b_g2,
             gla_norm=v_gla_norm, w_out=v_w_out, final_norm=v_final_norm)
    B, S, D = x.shape
    me = 4 * lax.axis_index("x") + 2 * lax.axis_index("y") + lax.axis_index("c")
    ada_cols = ada_w.shape[2]
    cast = lambda a: a.astype(_MXU)

    sharded = ["w_in", "w_uq", "w_ukv", "w_out"]

    whole_cols = lambda a: jnp.transpose(a, (1, 0, 2)).reshape(a.shape[1], -1)
    whole_in = lambda blk: _arrange_w_in(whole_cols(blk))
    whole_rest = lambda blks: (_arrange_w_uq(whole_cols(blks[0])), _arrange_w_ukv(whole_cols(blks[1])),
                               blks[2].reshape(D, D))
    col_blocks = lambda a: jnp.transpose(a.reshape(a.shape[0], N_DEV, -1), (1, 0, 2)).astype(jnp.bfloat16)
    blocks_in = lambda dw_in_a: col_blocks(_unarrange_w_in(dw_in_a))
    blocks_rest = lambda dw_out, dw_uq_a, dw_ukv_a: [
        col_blocks(_unarrange_w_uq(dw_uq_a)), col_blocks(_unarrange_w_ukv(dw_ukv_a)),
        dw_out.reshape(N_DEV, D // N_DEV, D).astype(jnp.bfloat16)]

    got0 = _exchange([c, cast(w_in[0])], [True, True], "gather_first")
    c_all = got0[0].reshape(N_DEV * B, D)
    flight_r = _exchange_start([cast(w[n][0]) for n in sharded[1:]], [True] * 3, "gather_start_layer0")
    flight_w = _exchange_start([cast(w[n][1]) for n in sharded], [True] * 4, "gather_start_layer1")
    started = flight_r["token"][0, 0] + flight_w["token"][0, 0]

    ada_b_cols = lax.dynamic_slice(ada_b, (0, me * ada_cols), (DEPTH, ada_cols))
    mod_cols = _ada_fwd(c_all, ada_w, ada_b_cols)
    mod_send = jnp.transpose(mod_cols.reshape(DEPTH, N_DEV, B, ada_cols), (1, 0, 2, 3))
    (mod_recv,) = _exchange([mod_send], [False], "scatter_mod")
    mod = jnp.transpose(mod_recv, (1, 2, 0, 3)).reshape(DEPTH, B, 3 * D)

    small_w = {n: w[n] for n in SMALL}
    layer_small = lambda l: {n: a[l] for n, a in small_w.items() if n != "final_norm"}
    tabs = _rope_tables(positions.reshape(B, S, 1))
    late0 = lambda after: whole_rest(_exchange_wait(flight_r, after, me, "gather_wait_layer0"))
    x1, saved0 = _layer_fwd(x, tabs, mod[0] + started, layer_small(0), whole_in(got0[1]), late_weights=late0)
    got1 = _exchange_wait(flight_w, x1, me, "gather_wait_layer1")
    x2, saved1 = _layer_fwd(x1, tabs, mod[1], layer_small(1), whole_in(got1[0]), *whole_rest(got1[1:]))
    dx, loss, d_fw = _final_loss(x2, final_norm.reshape(1, D), loss_target)

    dx, g1 = _layer_bwd(dx, saved1, tabs)
    flight_g = _exchange_start([blocks_in(g1["w_in_a"])] + blocks_rest(g1["w_out"], g1["w_uq_a"], g1["w_ukv_a"]),
                               [False] * 4, "grads_start_layer1")
    flights = {}

    def early0(dw_out, dw_uq_a, dw_ukv_a):
        flights["rest0"] = _exchange_start(blocks_rest(dw_out, dw_uq_a, dw_ukv_a), [False] * 3, "grads_start_layer0")
        return flights["rest0"]["token"][0, 0]

    saved0 = dict(saved0, gate=saved0["gate"] + flight_g["token"][0, 0])
    grad_x, g0 = _layer_bwd(dx, saved0, tabs, early_grads=early0)
    parts1 = _exchange_wait(flight_g, grad_x, me, "grads_wait_layer1")
    rest0 = _exchange_wait(flights["rest0"], g0["w_in_a"], me, "grads_wait_layer0")

    both = lambda n: jnp.stack([g0[n], g1[n]])
    d_mod = both("d_mod")
    part = dict(norm_w=both("norm_w"), mla_q_norm=both("mla_q_norm"), mla_kv_norm=both("mla_kv_norm"),
                gla_w_g2=both("gla_w_g2"), gla_b_g2=both("gla_b_g2"), gla_norm=both("gla_norm256")[:, 0:128],
                final_norm=d_fw)
    d_mod_g, small_g, in0 = _exchange([d_mod, _pack_small(loss, part), blocks_in(g0["w_in_a"])],
                                      [True, True, False], "exchange_last")
    parts0 = [in0] + rest0

    d_mod_all = jnp.transpose(d_mod_g, (1, 0, 2, 3)).reshape(DEPTH, N_DEV * B, 3 * D)
    d_mod_cols = lax.dynamic_slice(d_mod_all, (0, 0, me * ada_cols), (DEPTH, N_DEV * B, ada_cols))
    g_ada_w = _ada_bwd(c_all, d_mod_cols)

    res = {}

    def update(name, parts2d):
        shp = w[name].shape
        two = lambda a: a.reshape(parts2d.shape[1:])
        out = _sum_adamw(parts2d, two(w[name]), two(m[name]), two(v[name]), "adamw_" + name)
        res[name] = [o.reshape(shp) for o in out]

    update("ada_w", g_ada_w.reshape(1, DEPTH * D, ada_cols))
    update("ada_b", jnp.transpose(d_mod_g, (0, 2, 1, 3)).reshape(N_DEV * B, DEPTH * 3 * D // 128, 128))
    for a, name in enumerate(sharded):
        update(name, jnp.concatenate([parts0[a], parts1[a]], axis=1))
    row = lambda a: a.reshape(1, D) if a.ndim == 1 else a
    loss_sum, small_out = _small_adamw(small_g, [row(w[n]) for n in SMALL], [row(m[n]) for n in SMALL],
                                       [row(v[n]) for n in SMALL])
    for n, outs in zip(SMALL, small_out):
        res[n] = [o.reshape(w[n].shape) for o in outs]
    loss_out = loss_sum[0, 0]
    return (loss_out, grad_x, *[res[n][0] for n in WEIGHTS], *[res[n][1] for n in WEIGHTS],
            *[res[n][2] for n in WEIGHTS], *[res[n][3] for n in WEIGHTS])
```

```python
import functools
import math

import numpy as np
import jax
import jax.numpy as jnp
from jax import lax
from jax.experimental import pallas as pl
from jax.experimental.pallas import tpu as pltpu

F32 = jnp.float32
_MXU = jnp.bfloat16

D_MODEL = 1024
DEPTH = 2
CHUNK = 64
EPS = 1e-6
ROPE_THETA = 10000.0
N_DEV = 8

MLA_SCALE = 96.0 ** -0.5
RET_KSCALE = 64.0 ** -0.5
GLA_KSCALE = 32.0 ** -0.5
GLA_TAU = 16.0

ADAM_LR = 0.001
ADAM_B1 = 0.9
ADAM_B2 = 0.999
ADAM_EPS = 1e-08
ADAM_WD = 0.01
ADAM_STEP = 10

RET_W, MLA_W, GLA_W = 1024, 1024, 896
ARR_W = RET_W + MLA_W + GLA_W
VMEM_MB = 1024 * 1024


def _cp(sem, vmem_mb=48):
    return pltpu.CompilerParams(dimension_semantics=sem, vmem_limit_bytes=vmem_mb * VMEM_MB)


def _mm(a, b):
    return jnp.dot(a.astype(_MXU), b.astype(_MXU), preferred_element_type=F32)


def _mm_nt(a, b):
    return lax.dot_general(a.astype(_MXU), b.astype(_MXU), (((1,), (1,)), ((), ())),
                           preferred_element_type=F32)


def _mm_tn(a, b):
    return lax.dot_general(a.astype(_MXU), b.astype(_MXU), (((0,), (0,)), ((), ())),
                           preferred_element_type=F32)


def _mm_f32(a, b):
    return jnp.dot(a, b, precision=lax.Precision.HIGHEST, preferred_element_type=F32)


def _sig(z):
    return 1.0 / (1.0 + jnp.exp(-z))


def _silu(z):
    return z * _sig(z)


def _dsilu(z):
    s = _sig(z)
    return s * (1.0 + z * (1.0 - s))


def _full(shape):
    nd = len(shape)
    return pl.BlockSpec(shape, lambda *_: (0,) * nd)


def _qk_perm(blk):
    r = blk.shape[0]
    return jnp.transpose(blk.reshape(r, 4, 2, 32), (0, 2, 1, 3)).reshape(r, 256)


def _qk_unperm(blk):
    r = blk.shape[0]
    return jnp.transpose(blk.reshape(r, 2, 4, 32), (0, 2, 1, 3)).reshape(r, 256)


def _arrange_w_in(w):
    z = lambda n: jnp.zeros((w.shape[0], n), w.dtype)
    ret = [_qk_perm(w[:, 0:256]), _qk_perm(w[:, 256:512]), w[:, 512:768], w[:, 768:1024]]
    mla = [w[:, 1024:1280], w[:, 1280:1408], z(64), w[:, 1408:1440], z(32), w[:, 1440:1952]]
    gla = [w[:, 1952:2080], w[:, 2080:2208], w[:, 2208:2464], w[:, 2464:2480], z(112), w[:, 2480:2736]]
    return jnp.concatenate(ret + mla + gla, axis=1)


def _unarrange_w_in(a):
    m, g = RET_W, RET_W + MLA_W
    parts = [_qk_unperm(a[:, 0:256]), _qk_unperm(a[:, 256:512]), a[:, 512:1024],
             a[:, m:m + 384], a[:, m + 448:m + 480], a[:, m + 512:m + 1024],
             a[:, g:g + 528], a[:, g + 640:g + 896]]
    return jnp.concatenate(parts, axis=1)


def _arrange_w_uq(w):
    return jnp.pad(w.reshape(256, 8, 96), ((0, 0), (0, 0), (0, 32))).reshape(256, 1024)


def _unarrange_w_uq(a):
    return a.reshape(256, 8, 128)[:, :, :96].reshape(256, 768)


def _arrange_w_ukv(w):
    r = w.reshape(128, 8, 128)
    k = jnp.pad(r[:, :, :64], ((0, 0), (0, 0), (0, 64))).reshape(128, 1024)
    return jnp.concatenate([k, r[:, :, 64:].reshape(128, 512)], axis=1)


def _unarrange_w_ukv(a):
    k = a[:, :1024].reshape(128, 8, 128)[:, :, :64]
    v = a[:, 1024:].reshape(128, 8, 64)
    return jnp.concatenate([k, v], axis=2).reshape(128, 1024)


def _rope_tables(pos3):
    B, S, _ = pos3.shape
    ts = min(S, 512)
    inv32 = (np.float32(ROPE_THETA) ** (-(np.arange(32, dtype=np.float32) / 32))).astype(np.float32)
    inv16 = (np.float32(ROPE_THETA) ** (-(np.arange(16, dtype=np.float32) / 16))).astype(np.float32)
    inv = np.zeros((1, 128), np.float32)
    inv[0, 0:32] = inv32
    inv[0, 32:48] = inv16

    def body(pos_ref, inv_ref, cr, sr, cm, sm):
        ang = pos_ref[0].astype(F32) * inv_ref[...]
        lane = lax.broadcasted_iota(jnp.int32, (1, 128), 1)

        def every_head(x):
            y = jnp.where(lane < 32, x, pltpu.roll(x, 32, 1))
            return jnp.where(lane < 64, y, pltpu.roll(y, 64, 1))

        def rotary_pair(x, fill):
            return jnp.where((lane >= 64) & (lane < 80), pltpu.roll(x, 32, 1),
                             jnp.where((lane >= 80) & (lane < 96), pltpu.roll(x, 48, 1), fill))

        c, s = jnp.cos(ang), jnp.sin(ang)
        cr[0] = every_head(c)
        sr[0] = every_head(s)
        cm[0] = rotary_pair(c, 1.0)
        sm[0] = rotary_pair(s, 0.0)

    tab = jax.ShapeDtypeStruct((B, S, 128), F32)
    blk = pl.BlockSpec((1, ts, 128), lambda b, i: (b, i, 0))
    return pl.pallas_call(
        body, name="rope_tables", grid=(B, S // ts),
        in_specs=[pl.BlockSpec((1, ts, 1), lambda b, i: (b, i, 0)), _full((1, 128))],
        out_specs=[blk, blk, blk, blk], out_shape=[tab, tab, tab, tab],
        compiler_params=_cp(("parallel", "parallel")),
    )(pos3, jnp.asarray(inv))


def _rope128(x, cos, sin):
    lane = lax.broadcasted_iota(jnp.int32, (1, 128), 1)
    rp = pltpu.roll(x, 16, 1)
    rm = pltpu.roll(x, 112, 1)
    return x * cos + jnp.where(lane < 80, -rm, rp) * sin


def _rope128_t(d, cos, sin):
    lane = lax.broadcasted_iota(jnp.int32, (1, 128), 1)
    y = d * sin
    yp = pltpu.roll(y, 16, 1)
    ym = pltpu.roll(y, 112, 1)
    return d * cos + jnp.where(lane < 64, 0.0, jnp.where(lane < 80, ym, jnp.where(lane < 96, -yp, 0.0)))


def _proj_fwd(x, shift, scale, nw, w_arr):
    B, S, D = x.shape
    tm = min(S, 512)

    def body(x_ref, sh_ref, sc_ref, nw_ref, w_ref, ret_ref, mla_ref, gla_ref, h_ref):
        xv = x_ref[0]
        rstd = lax.rsqrt(jnp.mean(xv * xv, axis=-1, keepdims=True) + EPS)
        h = (xv * rstd * nw_ref[...]) * (1.0 + sc_ref[0]) + sh_ref[0]
        hb = h.astype(_MXU)
        h_ref[0] = hb
        ret_ref[0] = jnp.dot(hb, w_ref[:, 0:RET_W], preferred_element_type=F32).astype(_MXU)
        mla_ref[0] = jnp.dot(hb, w_ref[:, RET_W:RET_W + MLA_W], preferred_element_type=F32).astype(_MXU)
        gla_ref[0] = jnp.dot(hb, w_ref[:, RET_W + MLA_W:ARR_W], preferred_element_type=F32).astype(_MXU)

    tok = lambda w: pl.BlockSpec((1, tm, w), lambda b, i: (b, i, 0))
    per_seq = pl.BlockSpec((1, 1, D), lambda b, i: (b, 0, 0))
    return pl.pallas_call(
        body, name="proj_fwd", grid=(B, S // tm),
        in_specs=[tok(D), per_seq, per_seq, _full((1, D)), _full((D, ARR_W))],
        out_specs=[tok(RET_W), tok(MLA_W), tok(GLA_W), tok(D)],
        out_shape=[jax.ShapeDtypeStruct((B, S, RET_W), _MXU), jax.ShapeDtypeStruct((B, S, MLA_W), _MXU),
                   jax.ShapeDtypeStruct((B, S, GLA_W), _MXU), jax.ShapeDtypeStruct((B, S, D), _MXU)],
        compiler_params=_cp(("parallel", "parallel")),
    )(x, shift, scale, nw, w_arr)


RET_L = 256


def _ret_consts(L):
    lg = np.log1p(-np.exp2(-5.0 - np.arange(4, dtype=np.float32))).astype(np.float32)
    i = np.arange(L)
    ci = i // CHUNK
    diff = (i[:, None] - i[None, :]).astype(np.float32)
    same = ci[:, None] == ci[None, :]
    past = ci[None, :] < ci[:, None]
    expo = np.where(same, np.abs(diff), np.where(past, diff, 0.0)).astype(np.float32)
    dec = np.where((same | past)[None], np.exp(lg[:, None, None] * expo[None]), 0.0).astype(np.float32)
    head = (np.arange(256) % 128) // 32
    qw = np.exp((i + 1.0)[:, None] * lg[head][None, :]).astype(np.float32)
    kw = np.exp((L - 1.0 - i)[:, None] * lg[head][None, :]).astype(np.float32)
    a_row = np.exp(np.float32(L) * lg[head])[None, :].astype(np.float32)
    return [jnp.asarray(t) for t in (dec.reshape(4 * L, L), qw, kw, a_row)]


def _ret_masks():
    lane = lax.broadcasted_iota(jnp.int32, (1, 256), 1)
    mh = [((lane % 128) // 32) == h for h in range(4)]
    mv = [(lane // 64) == h for h in range(4)]
    vi = lax.broadcasted_iota(jnp.int32, (256, 256), 0)
    ki = lax.broadcasted_iota(jnp.int32, (256, 256), 1)
    bd = (vi // 64) == ((ki % 128) // 32)
    return mh, mv, bd


def _ret_rope(p, cs, sn):
    q1, q2, k1, k2 = p[:, 0:128], p[:, 128:256], p[:, 256:384], p[:, 384:512]
    qr = jnp.concatenate([q1 * cs - q2 * sn, q2 * cs + q1 * sn], axis=1)
    kr = jnp.concatenate([k1 * cs - k2 * sn, k2 * cs + k1 * sn], axis=1) * RET_KSCALE
    return qr, kr


def _head_mean(x, mv, width):
    out = jnp.zeros_like(x)
    for m in mv:
        s = jnp.sum(jnp.where(m, x, 0.0), axis=-1, keepdims=True) * (1.0 / width)
        out = jnp.where(m, s, out)
    return out


def _stack_heads(x, masks):
    return jnp.concatenate([jnp.where(m, x, 0.0) for m in masks], axis=0)


def _fold_heads(xs, masks, L):
    out = jnp.where(masks[0], xs[0:L], 0.0)
    for h in range(1, 4):
        out = out + jnp.where(masks[h], xs[h * L:(h + 1) * L], 0.0)
    return out


RET_G = 2


def _ret_fwd(ret_p, cos, sin):
    B, S, _ = ret_p.shape
    L = min(RET_L, S)
    NB = S // L
    G = min(RET_G, NB)
    NG = NB // G
    consts = _ret_consts(L)

    def body(p_ref, c_ref, s_ref, ds_ref, qw_ref, kw_ref, a_ref, out_ref, raw_ref, st_ref, st_sc):
        @pl.when(pl.program_id(1) == 0)
        def _():
            st_sc[...] = jnp.zeros_like(st_sc)

        mh, mv, bd = _ret_masks()
        cs_ = range(G)
        rows = [slice(c * L, (c + 1) * L) for c in cs_]
        ps = [p_ref[0, rows[c], :].astype(F32) for c in cs_]
        qk = [_ret_rope(ps[c], c_ref[0, rows[c], :], s_ref[0, rows[c], :]) for c in cs_]
        vs = [ps[c][:, 512:768] for c in cs_]
        a_s = [_mm_nt(_stack_heads(qk[c][0], mh), qk[c][1]) for c in cs_]
        upd = [_mm_tn(vs[c], qk[c][1] * kw_ref[...]) for c in cs_]
        o_s = [_mm(a_s[c] * ds_ref[...], vs[c]) for c in cs_]
        st = st_sc[...]
        inter = []
        for c in cs_:
            st_ref[0, c] = st
            inter.append(_mm_nt(qk[c][0] * qw_ref[...], st))
            st = st * a_ref[...] + jnp.where(bd, upd[c], 0.0)
        st_sc[...] = st
        for c in cs_:
            r = _fold_heads(o_s[c], mv, L) + inter[c]
            raw_ref[0, rows[c], :] = r
            rstd = lax.rsqrt(_head_mean(r * r, mv, 64.0) + EPS)
            out_ref[0, rows[c], :] = (r * rstd * _silu(ps[c][:, 768:1024])).astype(_MXU)

    tok = lambda w: pl.BlockSpec((1, G * L, w), lambda b, n: (b, n, 0))
    return pl.pallas_call(
        body, name="ret_fwd", grid=(B, NG),
        in_specs=[tok(RET_W), tok(128), tok(128), _full((4 * L, L)), _full((L, 256)), _full((L, 256)),
                  _full((1, 256))],
        out_specs=[tok(256), tok(256), pl.BlockSpec((1, G, 256, 256), lambda b, n: (b, n, 0, 0))],
        out_shape=[jax.ShapeDtypeStruct((B, S, 256), _MXU), jax.ShapeDtypeStruct((B, S, 256), F32),
                   jax.ShapeDtypeStruct((B, NB, 256, 256), F32)],
        scratch_shapes=[pltpu.VMEM((256, 256), F32)],
        compiler_params=_cp(("parallel", "arbitrary")),
    )(ret_p, cos, sin, *consts)


def _ret_bwd(ret_p, cos, sin, raw, states, d_mix):
    B, S, _ = ret_p.shape
    L = min(RET_L, S)
    NB = S // L
    G = 1
    NG = NB // G
    consts = _ret_consts(L)

    def body(p_ref, c_ref, s_ref, raw_ref, st_ref, dm_ref, ds_ref, qw_ref, kw_ref, a_ref, dp_ref, dst_sc):
        @pl.when(pl.program_id(1) == 0)
        def _():
            dst_sc[...] = jnp.zeros_like(dst_sc)

        mh, mv, bd = _ret_masks()
        qw, kw, dec = qw_ref[...], kw_ref[...], ds_ref[...]
        cs_ = range(G)
        rows = [slice(c * L, (c + 1) * L) for c in cs_]
        ps = [p_ref[0, rows[c], :].astype(F32) for c in cs_]
        tabs = [(c_ref[0, rows[c], :], s_ref[0, rows[c], :]) for c in cs_]
        qk = [_ret_rope(ps[c], *tabs[c]) for c in cs_]
        vs = [ps[c][:, 512:768] for c in cs_]
        qs = [_stack_heads(qk[c][0], mh) for c in cs_]
        a_s = [_mm_nt(qs[c], qk[c][1]) for c in cs_]
        dr, dz = [], []
        for c in cs_:
            r = raw_ref[0, rows[c], :]
            z = ps[c][:, 768:1024]
            rstd = lax.rsqrt(_head_mean(r * r, mv, 64.0) + EPS)
            rn = r * rstd
            dm = dm_ref[0, rows[c], :]
            d_rn = dm * _silu(z)
            dz.append(dm * rn * _dsilu(z))
            dr.append(rstd * (d_rn - rn * _head_mean(d_rn * rn, mv, 64.0)))
        do_s = [_stack_heads(dr[c], mv) for c in cs_]
        da_s = [_mm_nt(do_s[c], vs[c]) for c in cs_]
        sts = [st_ref[0, c] for c in cs_]
        dq_st = [_mm(dr[c], sts[c]) for c in cs_]
        dst_in = [_mm_tn(dr[c], qk[c][0] * qw) for c in cs_]
        dv = [_mm_tn(a_s[c] * dec, do_s[c]) for c in cs_]
        dqr, dkr = [], []
        for c in cs_:
            da = da_s[c] * dec
            dqr.append(_fold_heads(_mm(da, qk[c][1]), mh, L) + dq_st[c] * qw)
            dkr.append(_mm_tn(da, qs[c]))
        dst_next = dst_sc[...]
        for c in reversed(cs_):
            g = jnp.where(bd, dst_next, 0.0)
            dv[c] = dv[c] + _mm_nt(qk[c][1] * kw, g)
            dkr[c] = dkr[c] + _mm(vs[c], g) * kw
            dst_next = dst_next * a_ref[...] + jnp.where(bd, dst_in[c], 0.0)
        dst_sc[...] = dst_next
        for c in cs_:
            cs, sn = tabs[c]
            dk = dkr[c] * RET_KSCALE
            dq1, dq2 = dqr[c][:, 0:128], dqr[c][:, 128:256]
            dk1, dk2 = dk[:, 0:128], dk[:, 128:256]
            dp_ref[0, rows[c], :] = jnp.concatenate(
                [dq1 * cs + dq2 * sn, dq2 * cs - dq1 * sn, dk1 * cs + dk2 * sn, dk2 * cs - dk1 * sn, dv[c], dz[c]],
                axis=1).astype(_MXU)

    tok = lambda w: pl.BlockSpec((1, G * L, w), lambda b, i: (b, NG - 1 - i, 0))
    return pl.pallas_call(
        body, name="ret_bwd", grid=(B, NG),
        in_specs=[tok(RET_W), tok(128), tok(128), tok(256),
                  pl.BlockSpec((1, G, 256, 256), lambda b, i: (b, NG - 1 - i, 0, 0)), tok(256),
                  _full((4 * L, L)), _full((L, 256)), _full((L, 256)), _full((1, 256))],
        out_specs=tok(RET_W), out_shape=jax.ShapeDtypeStruct((B, S, RET_W), _MXU),
        scratch_shapes=[pltpu.VMEM((256, 256), F32)],
        compiler_params=_cp(("parallel", "arbitrary")),
    )(ret_p, cos, sin, raw, states, d_mix, *consts)


def _gla_masks():
    C = CHUNK
    lk = lax.broadcasted_iota(jnp.int32, (1, 128), 1)
    lv = lax.broadcasted_iota(jnp.int32, (1, 256), 1)
    mk = [(lk // 32) == h for h in range(4)]
    mv = [(lv // 64) == h for h in range(4)]
    vi = lax.broadcasted_iota(jnp.int32, (256, 128), 0)
    ki = lax.broadcasted_iota(jnp.int32, (256, 128), 1)
    bd = (vi // 64) == (ki // 32)
    ri = lax.broadcasted_iota(jnp.int32, (4 * C, C), 0) % C
    cj = lax.broadcasted_iota(jnp.int32, (4 * C, C), 1)
    lower = ri >= cj
    ti = lax.broadcasted_iota(jnp.int32, (C, C), 0)
    tj = lax.broadcasted_iota(jnp.int32, (C, C), 1)
    ltri = jnp.where(ti >= tj, 1.0, 0.0).astype(F32)
    utri = jnp.where(ti <= tj, 1.0, 0.0).astype(F32)
    return mk, mv, bd, lower, ltri, utri


def _log_sigmoid(x):
    return jnp.minimum(x, 0.0) - jnp.log(1.0 + jnp.exp(-jnp.abs(x)))


GLA_G = 8


def _gla_fwd(gla_p, w_g2p, b_g2, gnw):
    B, S, _ = gla_p.shape
    C = CHUNK
    NC = S // C
    G = min(GLA_G, NC)
    NG = NC // G

    def body(p_ref, w_ref, b_ref, gn_ref, out_ref, raw_ref, st_ref, st_sc):
        @pl.when(pl.program_id(1) == 0)
        def _():
            st_sc[...] = jnp.zeros_like(st_sc)

        mk, mv, bd, lower, ltri, _ = _gla_masks()
        cs = range(G)
        rows = [slice(c * C, (c + 1) * C) for c in cs]
        ps = [p_ref[0, rows[c], :].astype(F32) for c in cs]
        pre = [_mm(ps[c][:, 512:640], w_ref[...]) + b_ref[...] for c in cs]
        cum = [_mm_f32(ltri, _log_sigmoid(pre[c]) * (1.0 / GLA_TAU)) for c in cs]
        past, fut, upd, q_pos, a_row = [], [], [], [], []
        for c in cs:
            q = ps[c][:, 0:128]
            k = ps[c][:, 128:256] * GLA_KSCALE
            last = cum[c][C - 1:C, :]
            e_pos = jnp.exp(cum[c])
            e_neg = jnp.exp(-cum[c])
            q_pos.append(q * e_pos)
            a_row.append(jnp.exp(last))
            past.append(_mm_nt(_stack_heads(q_pos[c], mk), k * e_neg))
            fut.append(_mm_nt(_stack_heads(q * e_neg, mk), k * e_pos))
            upd.append(_mm_tn(ps[c][:, 256:512], k * jnp.exp(last - cum[c])))
        o_s = [_mm(jnp.where(lower, past[c], fut[c]), ps[c][:, 256:512]) for c in cs]
        st = st_sc[...]
        inter = []
        for c in cs:
            st_ref[0, c] = st
            inter.append(_mm_nt(q_pos[c], st))
            st = st * a_row[c] + jnp.where(bd, upd[c], 0.0)
        st_sc[...] = st
        for c in cs:
            g = _fold_heads(o_s[c], mv, C) + inter[c]
            raw_ref[0, rows[c], :] = g
            rstd = lax.rsqrt(_head_mean(g * g, mv, 64.0) + EPS)
            out_ref[0, rows[c], :] = (g * rstd * gn_ref[...] * _silu(ps[c][:, 640:896])).astype(_MXU)

    tok = lambda w: pl.BlockSpec((1, G * C, w), lambda b, n: (b, n, 0))
    return pl.pallas_call(
        body, name="gla_fwd", grid=(B, NG),
        in_specs=[tok(GLA_W), _full((128, 128)), _full((1, 128)), _full((1, 256))],
        out_specs=[tok(256), tok(256), pl.BlockSpec((1, G, 256, 128), lambda b, n: (b, n, 0, 0))],
        out_shape=[jax.ShapeDtypeStruct((B, S, 256), _MXU), jax.ShapeDtypeStruct((B, S, 256), F32),
                   jax.ShapeDtypeStruct((B, NC, 256, 128), F32)],
        scratch_shapes=[pltpu.VMEM((256, 128), F32)],
        compiler_params=_cp(("parallel", "arbitrary")),
    )(gla_p, w_g2p, b_g2, gnw)


def _gla_bwd(gla_p, w_g2p, b_g2, gnw, raw, states, d_mix):
    B, S, _ = gla_p.shape
    C = CHUNK
    NC = S // C
    G = min(GLA_G, NC)
    NG = NC // G

    def body(p_ref, w_ref, b_ref, gn_ref, raw_ref, st_ref, dm_ref, dp_ref, dw_ref, db_ref, dgn_ref, dst_sc):
        first = (pl.program_id(0) == 0) & (pl.program_id(1) == 0)

        @pl.when(first)
        def _():
            dw_ref[...] = jnp.zeros_like(dw_ref)
            db_ref[...] = jnp.zeros_like(db_ref)
            dgn_ref[...] = jnp.zeros_like(dgn_ref)

        @pl.when(pl.program_id(1) == 0)
        def _():
            dst_sc[...] = jnp.zeros_like(dst_sc)

        mk, mv, bd, lower, ltri, utri = _gla_masks()
        gn = gn_ref[...]
        cs = range(G)
        rows = [slice(c * C, (c + 1) * C) for c in cs]
        ps = [p_ref[0, rows[c], :].astype(F32) for c in cs]
        vs = [ps[c][:, 256:512] for c in cs]
        pre = [_mm(ps[c][:, 512:640], w_ref[...]) + b_ref[...] for c in cs]
        cum = [_mm_f32(ltri, _log_sigmoid(pre[c]) * (1.0 / GLA_TAU)) for c in cs]
        dg, dz, dgn_acc = [], [], jnp.zeros((1, 256), F32)
        for c in cs:
            g = raw_ref[0, rows[c], :]
            z = ps[c][:, 640:896]
            rstd = lax.rsqrt(_head_mean(g * g, mv, 64.0) + EPS)
            gh = g * rstd
            dm = dm_ref[0, rows[c], :]
            d_gn = dm * _silu(z)
            dz.append(dm * gh * gn * _dsilu(z))
            dgn_acc = dgn_acc + jnp.sum(d_gn * gh, axis=0, keepdims=True)
            d_gh = d_gn * gn
            dg.append(rstd * (d_gh - gh * _head_mean(d_gh * gh, mv, 64.0)))
        do_s = [_stack_heads(dg[c], mv) for c in cs]
        dattn = [_mm_nt(do_s[c], vs[c]) for c in cs]
        ks, e_pos, e_neg, q_pos, q_neg, k_pos, k_neg, qp_s, qn_s, past, fut, a_row, w_dec, kd = ([] for _ in range(14))
        for c in cs:
            q = ps[c][:, 0:128]
            k = ps[c][:, 128:256] * GLA_KSCALE
            last = cum[c][C - 1:C, :]
            ep, en = jnp.exp(cum[c]), jnp.exp(-cum[c])
            ks.append(k), e_pos.append(ep), e_neg.append(en)
            q_pos.append(q * ep), q_neg.append(q * en), k_pos.append(k * ep), k_neg.append(k * en)
            qp_s.append(_stack_heads(q_pos[c], mk)), qn_s.append(_stack_heads(q_neg[c], mk))
            past.append(_mm_nt(qp_s[c], k_neg[c]))
            fut.append(_mm_nt(qn_s[c], k_pos[c]))
            a_row.append(jnp.exp(last))
            w_dec.append(jnp.exp(last - cum[c]))
            kd.append(k * w_dec[c])
        sts = [st_ref[0, c] for c in cs]
        dq_st = [_mm(dg[c], sts[c]) for c in cs]
        dst_in = [_mm_tn(dg[c], q_pos[c]) for c in cs]
        dv, dq_pos, dk_neg, dq_neg, dk_pos = [], [], [], [], []
        for c in cs:
            attn = jnp.where(lower, past[c], fut[c])
            dpast = jnp.where(lower, dattn[c], 0.0)
            dfut = jnp.where(lower, 0.0, dattn[c])
            dv.append(_mm_tn(attn, do_s[c]))
            dq_pos.append(_fold_heads(_mm(dpast, k_neg[c]), mk, C) + dq_st[c])
            dk_neg.append(_mm_tn(dpast, qp_s[c]))
            dq_neg.append(_fold_heads(_mm(dfut, k_pos[c]), mk, C))
            dk_pos.append(_mm_tn(dfut, qn_s[c]))
        dst_next = dst_sc[...]
        d_a, d_kd = [None] * G, [None] * G
        for c in reversed(cs):
            d_a[c] = jnp.sum(dst_next * sts[c], axis=0, keepdims=True)
            gmat = jnp.where(bd, dst_next, 0.0)
            d_kd[c] = _mm(vs[c], gmat)
            dv[c] = dv[c] + _mm_nt(kd[c], gmat)
            dst_next = dst_next * a_row[c] + jnp.where(bd, dst_in[c], 0.0)
        dst_sc[...] = dst_next
        row = lax.broadcasted_iota(jnp.int32, (C, 128), 0)
        d_la, dk, dq = [], [], []
        for c in cs:
            t = d_kd[c] * kd[c]
            dk.append(d_kd[c] * w_dec[c] + dk_neg[c] * e_neg[c] + dk_pos[c] * e_pos[c])
            dq.append(dq_pos[c] * e_pos[c] + dq_neg[c] * e_neg[c])
            d_last = jnp.sum(t, axis=0, keepdims=True) + d_a[c] * a_row[c]
            d_cum = (dq_pos[c] * q_pos[c] - dk_neg[c] * k_neg[c] - dq_neg[c] * q_neg[c] + dk_pos[c] * k_pos[c] - t)
            d_la.append(_mm_f32(utri, d_cum + jnp.where(row == C - 1, d_last, 0.0)))
        d_pre = [d_la[c] * _sig(-pre[c]) * (1.0 / GLA_TAU) for c in cs]
        d_gg = [_mm_nt(d_pre[c], w_ref[...]) for c in cs]
        dw_acc = _mm_tn(ps[0][:, 512:640], d_pre[0])
        db_acc = jnp.sum(d_pre[0], axis=0, keepdims=True)
        for c in cs[1:]:
            dw_acc = dw_acc + _mm_tn(ps[c][:, 512:640], d_pre[c])
            db_acc = db_acc + jnp.sum(d_pre[c], axis=0, keepdims=True)
        for c in cs:
            dp_ref[0, rows[c], :] = jnp.concatenate([dq[c], dk[c] * GLA_KSCALE, dv[c], d_gg[c], dz[c]],
                                                    axis=1).astype(_MXU)
        dw_ref[...] += dw_acc
        db_ref[...] += db_acc
        dgn_ref[...] += dgn_acc

        @pl.when((pl.program_id(0) == B - 1) & (pl.program_id(1) == NG - 1))
        def _():
            s1 = dgn_ref[...]
            s1 = s1 + pltpu.roll(s1, 128, 1)
            dgn_ref[...] = s1 + pltpu.roll(s1, 64, 1)

    tok = lambda w: pl.BlockSpec((1, G * C, w), lambda b, i: (b, NG - 1 - i, 0))
    return pl.pallas_call(
        body, name="gla_bwd", grid=(B, NG),
        in_specs=[tok(GLA_W), _full((128, 128)), _full((1, 128)), _full((1, 256)), tok(256),
                  pl.BlockSpec((1, G, 256, 128), lambda b, i: (b, NG - 1 - i, 0, 0)), tok(256)],
        out_specs=[tok(GLA_W), _full((128, 128)), _full((1, 128)), _full((1, 256))],
        out_shape=[jax.ShapeDtypeStruct((B, S, GLA_W), _MXU), jax.ShapeDtypeStruct((128, 128), F32),
                   jax.ShapeDtypeStruct((1, 128), F32), jax.ShapeDtypeStruct((1, 256), F32)],
        scratch_shapes=[pltpu.VMEM((256, 128), F32)],
        compiler_params=_cp(("arbitrary", "arbitrary")),
    )(gla_p, w_g2p, b_g2, gnw, raw, states, d_mix)


def _rms(x, w):
    rstd = lax.rsqrt(jnp.mean(x * x, axis=-1, keepdims=True) + EPS)
    xh = x * rstd
    return xh, rstd, xh * w


def _rms_bwd(dy, xh, rstd, w):
    dxh = dy * w
    return rstd * (dxh - xh * jnp.mean(dxh * xh, axis=-1, keepdims=True))


MLA_T = 256


def _mla_prep_fwd(mla_p, cos, sin, qnw, kvnw, w_uq, w_ukv):
    B, S, _ = mla_p.shape
    tm = min(S, 512)

    t = min(MLA_T, S)
    nt = tm // t

    def body(p_ref, c_ref, s_ref, qn_ref, kn_ref, wq_ref, wkv_ref, q_ref, k_ref, v_ref, kt_ref, vt_ref):
        p = p_ref[0].astype(F32)
        cs, sn = c_ref[0], s_ref[0]
        _, _, qn = _rms(p[:, 0:256], qn_ref[...])
        qpre = _mm(qn, wq_ref[...])
        _, _, kvn = _rms(p[:, 256:384], kn_ref[...])
        kv = _mm(kvn, wkv_ref[...])
        kpe = _rope128(p[:, 384:512], cs, sn)
        for h in range(8):
            sl = slice(128 * h, 128 * h + 128)
            q_ref[0, :, sl] = _rope128(qpre[:, sl], cs, sn).astype(_MXU)
            kh = kv[:, sl] + kpe
            k_ref[0, :, sl] = kh.astype(_MXU)
            kht = kh.T
            for n in range(nt):
                kt_ref[0, n, sl, :] = kht[:, n * t:(n + 1) * t].astype(_MXU)
        v_ref[0] = kv[:, 1024:1536].astype(_MXU)
        for pr in range(4):
            vht = kv[:, 1024 + 128 * pr:1152 + 128 * pr].T
            for n in range(nt):
                vt_ref[0, n, 128 * pr:128 * pr + 128, :] = vht[:, n * t:(n + 1) * t].astype(_MXU)

    tok = lambda w: pl.BlockSpec((1, tm, w), lambda b, i: (b, i, 0))
    tr = lambda w: pl.BlockSpec((1, nt, w, t), lambda b, i: (b, i, 0, 0))
    return pl.pallas_call(
        body, name="mla_prep_fwd", grid=(B, S // tm),
        in_specs=[tok(512), tok(128), tok(128), _full((1, 256)), _full((1, 128)), _full((256, 1024)),
                  _full((128, 1536))],
        out_specs=[tok(1024), tok(1024), tok(512), tr(1024), tr(512)],
        out_shape=[jax.ShapeDtypeStruct((B, S, 1024), _MXU), jax.ShapeDtypeStruct((B, S, 1024), _MXU),
                   jax.ShapeDtypeStruct((B, S, 512), _MXU), jax.ShapeDtypeStruct((B, S // t, 1024, t), _MXU),
                   jax.ShapeDtypeStruct((B, S // t, 512, t), _MXU)],
        compiler_params=_cp(("parallel", "parallel")),
    )(mla_p, cos, sin, qnw, kvnw, w_uq, w_ukv)


def _chunk_mask_t(t):
    kj = lax.broadcasted_iota(jnp.int32, (t, t), 0) // CHUNK
    qi = lax.broadcasted_iota(jnp.int32, (t, t), 1) // CHUNK
    return kj <= qi


MLA_HG = 8
MLA_HG_FWD = 8
LOG2E = 1.4426950408889634
MLA_C2 = MLA_SCALE * LOG2E


def _mla_attn_fwd(q, k, vt):
    B, S, _ = q.shape
    t = min(MLA_T, S)
    nq = S // t
    HG = MLA_HG_FWD
    NP = HG // 2

    def body(q_ref, k_ref, vt_ref, o_ref, lse_ref, sa, sb, m_sc, l_sc, acc_sc):
        i = pl.program_id(2)
        row = lax.broadcasted_iota(jnp.int32, (128, 1), 0)
        low = row < 64
        mask = _chunk_mask_t(t)
        m_sc[...] = jnp.full(m_sc.shape, -jnp.inf, F32)
        l_sc[...] = jnp.zeros_like(l_sc)
        acc_sc[...] = jnp.zeros_like(acc_sc)

        ones = jnp.ones((8, t), _MXU)

        def scores(j, buf):
            kb = k_ref[0, pl.ds(pl.multiple_of(j * t, t), t), :]
            for h in range(HG):
                cols = slice(128 * h, 128 * h + 128)
                buf[h] = (_mm_nt(kb[:, cols], q_ref[0, :, cols]) * MLA_C2).astype(_MXU)

        def absorb(j, buf, masked):
            vtb = vt_ref[0, j]
            for pr in range(NP):
                alphas, pvs = [], []
                for hh in range(2):
                    h = 2 * pr + hh
                    s = buf[h]
                    if masked:
                        s = jnp.where(mask, s, jnp.full_like(s, -jnp.inf))
                    m_old = m_sc[h]
                    m_new = jnp.maximum(m_old, jnp.max(s, axis=0, keepdims=True).astype(F32))
                    alpha = jnp.exp2(m_old - m_new)
                    p = jnp.exp2(s - m_new.astype(_MXU))
                    l_sc[h] = alpha * l_sc[h] + _mm(ones, p)[0:1, :]
                    m_sc[h] = m_new
                    vth = vtb[128 * pr:128 * pr + 128, :]
                    vth = jnp.where(low if hh == 0 else ~low, vth, jnp.zeros_like(vth))
                    pvs.append(_mm(vth, p))
                    alphas.append(alpha)
                acc_sc[pr] = acc_sc[pr] * jnp.where(low, alphas[0], alphas[1]) + pvs[0] + pvs[1]

        scores(0, sb)

        def pair(jj, carry):
            j0 = 2 * jj
            scores(j0 + 1, sa)
            absorb(j0, sb, False)
            scores(j0 + 2, sb)
            absorb(j0 + 1, sa, False)
            return carry

        lax.fori_loop(0, i // 2, pair, 0)

        @pl.when(i % 2 == 1)
        def _():
            scores(i, sa)
            absorb(i - 1, sb, False)
            absorb(i, sa, True)

        @pl.when(i % 2 == 0)
        def _():
            absorb(i, sb, True)

        for pr in range(NP):
            l_e, l_o = l_sc[2 * pr], l_sc[2 * pr + 1]
            o_ref[0, :, 128 * pr:128 * pr + 128] = (acc_sc[pr] / jnp.where(low, l_e, l_o)).T
            lse_ref[0, pr, 0, 0:1, :] = m_sc[2 * pr] + jnp.log(l_e) * LOG2E
            lse_ref[0, pr, 0, 1:2, :] = m_sc[2 * pr + 1] + jnp.log(l_o) * LOG2E

    return pl.pallas_call(
        body, name="mla_attn_fwd", grid=(B, 8 // HG, nq),
        in_specs=[pl.BlockSpec((1, t, 128 * HG), lambda b, g, i: (b, i, g)),
                  pl.BlockSpec((1, S, 128 * HG), lambda b, g, i: (b, 0, g)),
                  pl.BlockSpec((1, nq, 64 * HG, t), lambda b, g, i: (b, 0, g, 0))],
        out_specs=[pl.BlockSpec((1, t, 64 * HG), lambda b, g, i: (b, i, g)),
                   pl.BlockSpec((1, NP, 1, 2, t), lambda b, g, i: (b, g, i, 0, 0))],
        out_shape=[jax.ShapeDtypeStruct((B, S, 512), F32), jax.ShapeDtypeStruct((B, 4, nq, 2, t), F32)],
        scratch_shapes=[pltpu.VMEM((HG, t, t), _MXU), pltpu.VMEM((HG, t, t), _MXU), pltpu.VMEM((HG, 1, t), F32),
                        pltpu.VMEM((HG, 1, t), F32), pltpu.VMEM((NP, 128, t), F32)],
        compiler_params=_cp(("parallel", "parallel", "arbitrary")),
    )(q, k, vt)


def _mla_gate_bwd(d_mix, o, mla_p):
    B, S, _ = o.shape
    tm = min(S, 512)
    t = min(MLA_T, S)
    nt = tm // t

    def body(dm_ref, o_ref, z_ref, do_ref, dz_ref, dl_ref):
        dm, ov, z = dm_ref[0], o_ref[0], z_ref[0].astype(F32)
        do = dm * _silu(z)
        dz_ref[0] = (dm * ov * _dsilu(z)).astype(_MXU)
        do_ref[0] = do.astype(_MXU)
        prod = do * ov
        for pr in range(4):
            pt = prod[:, 128 * pr:128 * pr + 128].T
            se = jnp.sum(pt[0:64], axis=0, keepdims=True)
            so = jnp.sum(pt[64:128], axis=0, keepdims=True)
            for n in range(nt):
                dl_ref[0, pr, n, 0:1, :] = se[:, n * t:(n + 1) * t]
                dl_ref[0, pr, n, 1:2, :] = so[:, n * t:(n + 1) * t]

    tok = lambda c: pl.BlockSpec((1, tm, 512), lambda b, i: (b, i, c))
    return pl.pallas_call(
        body, name="mla_gate_bwd", grid=(B, S // tm),
        in_specs=[tok(0), tok(0), tok(1)],
        out_specs=[tok(0), tok(0), pl.BlockSpec((1, 4, nt, 2, t), lambda b, i: (b, 0, i, 0, 0))],
        out_shape=[jax.ShapeDtypeStruct((B, S, 512), _MXU), jax.ShapeDtypeStruct((B, S, 512), _MXU),
                   jax.ShapeDtypeStruct((B, 4, S // t, 2, t), F32)],
        compiler_params=_cp(("parallel", "parallel")),
    )(d_mix, o, mla_p)


def _mla_attn_bwd(q, k, v, kt, do, lse, dl):
    B, S, _ = q.shape
    t = min(MLA_T, S)
    nk = S // t

    HG = MLA_HG
    NP = HG // 2

    def body(q_ref, k_ref, v_ref, kt_ref, do_ref, lse_ref, dl_ref, dq_ref, dk_ref, dv_ref,
             sa, da, sb, db, dqt_sc, dk_sc, dv_sc):
        j = pl.program_id(2)

        @pl.when(j == 0)
        def _():
            dqt_sc[...] = jnp.zeros_like(dqt_sc)

        dk_sc[...] = jnp.zeros_like(dk_sc)
        dv_sc[...] = jnp.zeros_like(dv_sc)
        lane = lax.broadcasted_iota(jnp.int32, (1, 128), 1)
        low = lane < 64
        mask = _chunk_mask_t(t)

        def half(x, hh):
            return jnp.where(low if hh == 0 else ~low, x, jnp.zeros_like(x))

        def prepare(i, sbuf, dbuf):
            rows = pl.ds(pl.multiple_of(i * t, t), t)
            for h in range(HG):
                cols = slice(128 * h, 128 * h + 128)
                pc = slice(128 * (h // 2), 128 * (h // 2) + 128)
                sbuf[h] = _mm_nt(k_ref[0, :, cols], q_ref[0, rows, cols]) * MLA_C2
                dbuf[h] = _mm_nt(half(v_ref[0, :, pc], h % 2), do_ref[0, rows, pc])

        def absorb(i, sbuf, dbuf, masked):
            rows = pl.ds(pl.multiple_of(i * t, t), t)
            for h in range(HG):
                pr, hh = h // 2, h % 2
                cols = slice(128 * h, 128 * h + 128)
                pc = slice(128 * pr, 128 * pr + 128)
                p = jnp.exp2(sbuf[h] - lse_ref[0, pr, i][hh:hh + 1, :])
                if masked:
                    p = jnp.where(mask, p, 0.0)
                dv_sc[pr] += _mm(p, half(do_ref[0, rows, pc], hh))
                ds = p * (dbuf[h] - dl_ref[0, pr, i][hh:hh + 1, :])
                dqt_sc[i, cols, :] += _mm(kt_ref[0, 0, cols, :], ds)
                dk_sc[h] += _mm(ds, q_ref[0, rows, cols])

        n = nk - 1 - j
        prepare(jnp.minimum(j + 1, nk - 1), sb, db)

        def pair(jj, carry):
            i0 = j + 1 + 2 * jj
            prepare(i0 + 1, sa, da)
            absorb(i0, sb, db, False)
            prepare(jnp.where(i0 + 2 <= nk - 1, i0 + 2, j), sb, db)
            absorb(i0 + 1, sa, da, False)
            return carry

        lax.fori_loop(0, n // 2, pair, 0)

        @pl.when(n % 2 == 1)
        def _():
            prepare(j, sa, da)
            absorb(nk - 1, sb, db, False)
            absorb(j, sa, da, True)

        @pl.when(n % 2 == 0)
        def _():
            absorb(j, sb, db, True)

        for h in range(HG):
            dk_ref[0, :, 128 * h:128 * h + 128] = (dk_sc[h] * MLA_SCALE).astype(_MXU)
        for pr in range(NP):
            dv_ref[0, :, 128 * pr:128 * pr + 128] = dv_sc[pr].astype(_MXU)

        @pl.when(j == nk - 1)
        def _():
            for i in range(nk):
                dq_ref[0, i * t:(i + 1) * t, :] = (dqt_sc[i].T * MLA_SCALE).astype(_MXU)

    seq = lambda w: pl.BlockSpec((1, S, w), lambda b, g, j: (b, 0, g))
    blk = lambda w: pl.BlockSpec((1, t, w), lambda b, g, j: (b, j, g))
    stat = pl.BlockSpec((1, NP, nk, 2, t), lambda b, g, j: (b, g, 0, 0, 0))
    return pl.pallas_call(
        body, name="mla_attn_bwd", grid=(B, 8 // HG, nk),
        in_specs=[seq(128 * HG), blk(128 * HG), blk(64 * HG),
                  pl.BlockSpec((1, 1, 128 * HG, t), lambda b, g, j: (b, j, g, 0)), seq(64 * HG), stat, stat],
        out_specs=[seq(128 * HG), blk(128 * HG), blk(64 * HG)],
        out_shape=[jax.ShapeDtypeStruct((B, S, 1024), _MXU), jax.ShapeDtypeStruct((B, S, 1024), _MXU),
                   jax.ShapeDtypeStruct((B, S, 512), _MXU)],
        scratch_shapes=[pltpu.VMEM((HG, t, t), F32), pltpu.VMEM((HG, t, t), F32), pltpu.VMEM((HG, t, t), F32),
                        pltpu.VMEM((HG, t, t), F32), pltpu.VMEM((nk, 128 * HG, t), F32),
                        pltpu.VMEM((HG, t, 128), F32), pltpu.VMEM((NP, t, 128), F32)],
        compiler_params=_cp(("parallel", "parallel", "arbitrary"), 56),
    )(q, k, v, kt, do, lse, dl)


def _mla_prep_bwd(mla_p, cos, sin, qnw, kvnw, w_uq, w_ukv, dq, dk, dv):
    B, S, _ = mla_p.shape
    tm = min(S, 512)

    def body(p_ref, c_ref, s_ref, qn_ref, kn_ref, wq_ref, wkv_ref, dq_ref, dk_ref, dv_ref,
             dp_ref, dwq_ref, dwkv_ref, dqn_ref, dkn_ref):
        first = (pl.program_id(0) == 0) & (pl.program_id(1) == 0)

        @pl.when(first)
        def _():
            dwq_ref[...] = jnp.zeros_like(dwq_ref)
            dwkv_ref[...] = jnp.zeros_like(dwkv_ref)
            dqn_ref[...] = jnp.zeros_like(dqn_ref)
            dkn_ref[...] = jnp.zeros_like(dkn_ref)

        p = p_ref[0].astype(F32)
        cs, sn = c_ref[0], s_ref[0]
        lane = lax.broadcasted_iota(jnp.int32, (1, 128), 1)
        pe = (lane >= 64) & (lane < 96)
        qh, q_rstd, qn = _rms(p[:, 0:256], qn_ref[...])
        kvh, kv_rstd, kvn = _rms(p[:, 256:384], kn_ref[...])
        dqv = dq_ref[0].astype(F32)
        dkv = dk_ref[0].astype(F32)
        dqpre = jnp.concatenate(
            [_rope128_t(dqv[:, 128 * h:128 * h + 128], cs, sn) for h in range(8)], axis=1)
        dkpe = jnp.zeros((tm, 128), F32)
        for h in range(8):
            dkpe = dkpe + jnp.where(pe, dkv[:, 128 * h:128 * h + 128], 0.0)
        dkr = _rope128_t(dkpe, cs, sn)
        dkv_all = jnp.concatenate([dkv, dv_ref[0].astype(F32)], axis=1)
        d_qn = _mm_nt(dqpre, wq_ref[...])
        d_kvn = _mm_nt(dkv_all, wkv_ref[...])
        dwq_ref[...] += _mm_tn(qn, dqpre)
        dwkv_ref[...] += _mm_tn(kvn, dkv_all)
        dqn_ref[...] += jnp.sum(d_qn * qh, axis=0, keepdims=True)
        dkn_ref[...] += jnp.sum(d_kvn * kvh, axis=0, keepdims=True)
        dp_ref[0] = jnp.concatenate([_rms_bwd(d_qn, qh, q_rstd, qn_ref[...]),
                                     _rms_bwd(d_kvn, kvh, kv_rstd, kn_ref[...]), dkr], axis=1).astype(_MXU)

    tok = lambda w: pl.BlockSpec((1, tm, w), lambda b, i: (b, i, 0))
    return pl.pallas_call(
        body, name="mla_prep_bwd", grid=(B, S // tm),
        in_specs=[tok(512), tok(128), tok(128), _full((1, 256)), _full((1, 128)), _full((256, 1024)),
                  _full((128, 1536)), tok(1024), tok(1024), tok(512)],
        out_specs=[tok(512), _full((256, 1024)), _full((128, 1536)), _full((1, 256)), _full((1, 128))],
        out_shape=[jax.ShapeDtypeStruct((B, S, 512), _MXU), jax.ShapeDtypeStruct((256, 1024), F32),
                   jax.ShapeDtypeStruct((128, 1536), F32), jax.ShapeDtypeStruct((1, 256), F32),
                   jax.ShapeDtypeStruct((1, 128), F32)],
        compiler_params=_cp(("arbitrary", "arbitrary")),
    )(mla_p, cos, sin, qnw, kvnw, w_uq, w_ukv, dq, dk, dv)


def _out_fwd(x, gate, r_g, o_mla, mla_p, g_g, w_out):
    B, S, D = x.shape
    tm = min(S, 512)

    def body(x_ref, g_ref, r_ref, o_ref, z_ref, gg_ref, w_ref, xn_ref, y_ref, mm_ref):
        mm = (o_ref[0] * _silu(z_ref[0].astype(F32))).astype(_MXU)
        mm_ref[0] = mm
        y = (jnp.dot(r_ref[0], w_ref[0:256, :], preferred_element_type=F32)
             + jnp.dot(mm, w_ref[256:768, :], preferred_element_type=F32)
             + jnp.dot(gg_ref[0], w_ref[768:1024, :], preferred_element_type=F32))
        y_ref[0] = y
        xn_ref[0] = x_ref[0] + g_ref[0] * y

    tok = lambda w, c=0: pl.BlockSpec((1, tm, w), lambda b, i: (b, i, c))
    return pl.pallas_call(
        body, name="out_fwd", grid=(B, S // tm),
        in_specs=[tok(D), pl.BlockSpec((1, 1, D), lambda b, i: (b, 0, 0)), tok(256), tok(512), tok(512, 1),
                  tok(256), _full((D, D))],
        out_specs=[tok(D), tok(D), tok(512)],
        out_shape=[jax.ShapeDtypeStruct((B, S, D), F32), jax.ShapeDtypeStruct((B, S, D), F32),
                   jax.ShapeDtypeStruct((B, S, 512), _MXU)],
        compiler_params=_cp(("parallel", "parallel")),
    )(x, gate, r_g, o_mla, mla_p, g_g, w_out)


def _out_bwd(dx, y, gate, r_g, mm, g_g, w_out):
    B, S, D = dx.shape
    tm = min(S, 512)

    def body(dx_ref, y_ref, g_ref, r_ref, mm_ref, gg_ref, w_ref, dr_ref, dmm_ref, dg_ref, dw_ref, dgate_ref):
        first = (pl.program_id(0) == 0) & (pl.program_id(1) == 0)

        @pl.when(first)
        def _():
            dw_ref[...] = jnp.zeros_like(dw_ref)

        @pl.when(pl.program_id(1) == 0)
        def _():
            dgate_ref[...] = jnp.zeros_like(dgate_ref)

        dxv = dx_ref[0]
        dgate_ref[0] += jnp.sum(dxv * y_ref[0], axis=0, keepdims=True)
        dy = (dxv * g_ref[0]).astype(_MXU)
        dr_ref[0] = _mm_nt(dy, w_ref[0:256, :])
        dmm_ref[0] = _mm_nt(dy, w_ref[256:768, :])
        dg_ref[0] = _mm_nt(dy, w_ref[768:1024, :])
        dw_ref[0:256, :] += _mm_tn(r_ref[0], dy)
        dw_ref[256:768, :] += _mm_tn(mm_ref[0], dy)
        dw_ref[768:1024, :] += _mm_tn(gg_ref[0], dy)

    tok = lambda w: pl.BlockSpec((1, tm, w), lambda b, i: (b, i, 0))
    per_seq = pl.BlockSpec((1, 1, D), lambda b, i: (b, 0, 0))
    return pl.pallas_call(
        body, name="out_bwd", grid=(B, S // tm),
        in_specs=[tok(D), tok(D), per_seq, tok(256), tok(512), tok(256), _full((D, D))],
        out_specs=[tok(256), tok(512), tok(256), _full((D, D)), per_seq],
        out_shape=[jax.ShapeDtypeStruct((B, S, 256), F32), jax.ShapeDtypeStruct((B, S, 512), F32),
                   jax.ShapeDtypeStruct((B, S, 256), F32), jax.ShapeDtypeStruct((D, D), F32),
                   jax.ShapeDtypeStruct((B, 1, D), F32)],
        compiler_params=_cp(("arbitrary", "arbitrary")),
    )(dx, y, gate, r_g, mm, g_g, w_out)


def _proj_bwd_x(x, shift, scale, nw, w_arr, d_ret, d_mla, d_mz, d_gla, dx_out):
    B, S, D = x.shape
    tm = min(S, 512)

    def body(x_ref, sc_ref, nw_ref, w_ref, dr_ref, dm_ref, dz_ref, dg_ref, dxo_ref,
             dx_ref, dsh_ref, dsc_ref, dnw_ref):
        first = (pl.program_id(0) == 0) & (pl.program_id(1) == 0)

        @pl.when(first)
        def _():
            dnw_ref[...] = jnp.zeros_like(dnw_ref)

        @pl.when(pl.program_id(1) == 0)
        def _():
            dsh_ref[...] = jnp.zeros_like(dsh_ref)
            dsc_ref[...] = jnp.zeros_like(dsc_ref)

        dp = jnp.concatenate([dr_ref[0], dm_ref[0], dz_ref[0], dg_ref[0]], axis=1)
        dh = lax.dot_general(dp, w_ref[...], (((1,), (1,)), ((), ())), preferred_element_type=F32)
        xv = x_ref[0]
        rstd = lax.rsqrt(jnp.mean(xv * xv, axis=-1, keepdims=True) + EPS)
        xh = xv * rstd
        nwv = nw_ref[...]
        mod = 1.0 + sc_ref[0]
        dsh_ref[0] += jnp.sum(dh, axis=0, keepdims=True)
        dsc_ref[0] += jnp.sum(dh * xh * nwv, axis=0, keepdims=True)
        dnw_ref[...] += jnp.sum(dh * xh * mod, axis=0, keepdims=True)
        dxh = dh * nwv * mod
        dx_ref[0] = dxo_ref[0] + rstd * (dxh - xh * jnp.mean(dxh * xh, axis=-1, keepdims=True))

    tok = lambda w: pl.BlockSpec((1, tm, w), lambda b, i: (b, i, 0))
    per_seq = pl.BlockSpec((1, 1, D), lambda b, i: (b, 0, 0))
    return pl.pallas_call(
        body, name="proj_bwd_x", grid=(B, S // tm),
        in_specs=[tok(D), per_seq, _full((1, D)), _full((D, ARR_W)), tok(RET_W), tok(512), tok(512),
                  tok(GLA_W), tok(D)],
        out_specs=[tok(D), per_seq, per_seq, _full((1, D))],
        out_shape=[jax.ShapeDtypeStruct((B, S, D), F32), jax.ShapeDtypeStruct((B, 1, D), F32),
                   jax.ShapeDtypeStruct((B, 1, D), F32), jax.ShapeDtypeStruct((1, D), F32)],
        compiler_params=_cp(("arbitrary", "arbitrary")),
    )(x, scale, nw, w_arr, d_ret, d_mla, d_mz, d_gla, dx_out)


def _proj_bwd_w(h, d_ret, d_mla, d_mz, d_gla):
    B, S, D = h.shape
    tm = min(S, 512)

    def body(h_ref, dr_ref, dm_ref, dz_ref, dg_ref, dw_ref):
        first = (pl.program_id(0) == 0) & (pl.program_id(1) == 0)

        @pl.when(first)
        def _():
            dw_ref[...] = jnp.zeros_like(dw_ref)

        hv = h_ref[0]
        tn = lambda d_ref: lax.dot_general(hv, d_ref[0], (((0,), (0,)), ((), ())), preferred_element_type=F32)
        dw_ref[:, 0:RET_W] += tn(dr_ref)
        dw_ref[:, RET_W:RET_W + 512] += tn(dm_ref)
        dw_ref[:, RET_W + 512:RET_W + MLA_W] += tn(dz_ref)
        dw_ref[:, RET_W + MLA_W:ARR_W] += tn(dg_ref)

    tok = lambda w: pl.BlockSpec((1, tm, w), lambda b, i: (b, i, 0))
    return pl.pallas_call(
        body, name="proj_bwd_w", grid=(B, S // tm),
        in_specs=[tok(D), tok(RET_W), tok(512), tok(512), tok(GLA_W)],
        out_specs=_full((D, ARR_W)), out_shape=jax.ShapeDtypeStruct((D, ARR_W), F32),
        compiler_params=_cp(("arbitrary", "arbitrary"), 56),
    )(h, d_ret, d_mla, d_mz, d_gla)


def _final_loss(x, fw, target):
    B, S, D = x.shape
    tm = min(S, 512)

    def body(x_ref, fw_ref, t_ref, dx_ref, loss_ref, dfw_ref):
        first = (pl.program_id(0) == 0) & (pl.program_id(1) == 0)

        @pl.when(first)
        def _():
            loss_ref[...] = jnp.zeros_like(loss_ref)
            dfw_ref[...] = jnp.zeros_like(dfw_ref)

        xv = x_ref[0]
        fwv = fw_ref[...]
        rstd = lax.rsqrt(jnp.mean(xv * xv, axis=-1, keepdims=True) + EPS)
        xh = xv * rstd
        err = xh * fwv - t_ref[0]
        loss_ref[...] += 0.5 * jnp.sum(jnp.mean(err * err, axis=-1, keepdims=True), axis=0, keepdims=True)
        dy = err * (1.0 / D)
        dfw_ref[...] += jnp.sum(dy * xh, axis=0, keepdims=True)
        dxh = dy * fwv
        dx_ref[0] = rstd * (dxh - xh * jnp.mean(dxh * xh, axis=-1, keepdims=True))

    tok = pl.BlockSpec((1, tm, D), lambda b, i: (b, i, 0))
    return pl.pallas_call(
        body, name="final_loss", grid=(B, S // tm),
        in_specs=[tok, _full((1, D)), tok],
        out_specs=[tok, _full((1, 1)), _full((1, D))],
        out_shape=[jax.ShapeDtypeStruct((B, S, D), F32), jax.ShapeDtypeStruct((1, 1), F32),
                   jax.ShapeDtypeStruct((1, D), F32)],
        compiler_params=_cp(("arbitrary", "arbitrary")),
    )(x, fw, target)


def _local_step(x, pos3, mod, loss_target, small, w_in_a, w_uq_a, w_ukv_a, w_out_b):
    B, S, D = x.shape
    tabs = _rope_tables(pos3)
    saved = []
    for l in range(DEPTH):
        x, s = _layer_fwd(x, tabs, mod[l], {n: a[l] for n, a in small.items() if n != "final_norm"},
                          w_in_a[l], w_uq_a[l], w_ukv_a[l], w_out_b[l])
        saved.append(s)
    dx, loss, d_fw = _final_loss(x, small["final_norm"].reshape(1, D), loss_target)
    grads = dict(final_norm=d_fw.reshape(D))
    per_layer = [None] * DEPTH
    for l in reversed(range(DEPTH)):
        dx, per_layer[l] = _layer_bwd(dx, saved[l], tabs)
    for name in per_layer[0]:
        grads[name] = jnp.stack([per_layer[l][name] for l in range(DEPTH)])
    return loss, dx, grads


def _layer_fwd(x, tabs, mod_l, small_l, w_in_a, w_uq_a=None, w_ukv_a=None, w_out_b=None, late_weights=None):
    B, S, D = x.shape
    cr, sr, cm, sm = tabs
    shift = mod_l[:, 0:D].reshape(B, 1, D)
    scale = mod_l[:, D:2 * D].reshape(B, 1, D)
    gate = mod_l[:, 2 * D:3 * D].reshape(B, 1, D)
    nw = small_l["norm_w"].reshape(1, D)
    qnw = small_l["mla_q_norm"].reshape(1, 256)
    kvnw = small_l["mla_kv_norm"].reshape(1, 128)
    w_g2p = jnp.pad(small_l["gla_w_g2"], ((0, 112), (0, 0)))
    b_g2 = small_l["gla_b_g2"].reshape(1, 128)
    gnw = jnp.tile(small_l["gla_norm"], 4).reshape(1, 256)
    ret_p, mla_p, gla_p, h = _proj_fwd(x, shift, scale, nw, w_in_a)
    r_g, r_raw, r_st = _ret_fwd(ret_p, cr, sr)
    if late_weights is not None:
        w_uq_a, w_ukv_a, w_out_b = late_weights(r_raw)
    q, k, v, kt, vt = _mla_prep_fwd(mla_p, cm, sm, qnw, kvnw, w_uq_a, w_ukv_a)
    o_mla, lse = _mla_attn_fwd(q, k, vt)
    g_g, g_raw, g_st = _gla_fwd(gla_p, w_g2p, b_g2, gnw)
    x_new, y, mm = _out_fwd(x, gate, r_g, o_mla, mla_p, g_g, w_out_b)
    saved = dict(x=x, shift=shift, scale=scale, gate=gate, nw=nw, qnw=qnw, kvnw=kvnw, w_g2p=w_g2p, b_g2=b_g2,
                 gnw=gnw, ret_p=ret_p, mla_p=mla_p, gla_p=gla_p, h=h, r_g=r_g, r_raw=r_raw, r_st=r_st, q=q, k=k,
                 v=v, kt=kt, o_mla=o_mla, lse=lse, g_g=g_g, g_raw=g_raw, g_st=g_st, y=y, mm=mm,
                 w_in_a=w_in_a, w_uq_a=w_uq_a, w_ukv_a=w_ukv_a, w_out_b=w_out_b)
    return x_new, saved


def _layer_bwd(dx, s, tabs, early_grads=None):
    B, S, D = dx.shape
    cr, sr, cm, sm = tabs
    d_r, d_mm, d_g, dw_out, d_gate = _out_bwd(dx, s["y"], s["gate"], s["r_g"], s["mm"], s["g_g"], s["w_out_b"])
    d_ret = _ret_bwd(s["ret_p"], cr, sr, s["r_raw"], s["r_st"], d_r)
    do, d_mz, dl = _mla_gate_bwd(d_mm, s["o_mla"], s["mla_p"])
    dq, dk, dv = _mla_attn_bwd(s["q"], s["k"], s["v"], s["kt"], do, s["lse"], dl)
    d_mla, dw_uq, dw_ukv, d_qnw, d_kvnw = _mla_prep_bwd(
        s["mla_p"], cm, sm, s["qnw"], s["kvnw"], s["w_uq_a"], s["w_ukv_a"], dq, dk, dv)
    gnw = s["gnw"] if early_grads is None else s["gnw"] + early_grads(dw_out, dw_uq, dw_ukv)
    d_gla, dw_g2p, db_g2, d_gnw = _gla_bwd(s["gla_p"], s["w_g2p"], s["b_g2"], gnw, s["g_raw"], s["g_st"], d_g)
    dx, d_shift, d_scale, d_nw = _proj_bwd_x(s["x"], s["shift"], s["scale"], s["nw"], s["w_in_a"],
                                             d_ret, d_mla, d_mz, d_gla, dx)
    dw_in = _proj_bwd_w(s["h"], d_ret, d_mla, d_mz, d_gla)
    grads = dict(
        d_mod=jnp.concatenate([d_shift, d_scale, d_gate], axis=2).reshape(B, 3 * D),
        norm_w=d_nw.reshape(D), mla_q_norm=d_qnw.reshape(256), mla_kv_norm=d_kvnw.reshape(128),
        gla_w_g2=dw_g2p[0:16], gla_b_g2=db_g2.reshape(128), gla_norm256=d_gnw.reshape(256),
        w_in_a=dw_in, w_uq_a=dw_uq, w_ukv_a=dw_ukv, w_out=dw_out)
    return dx, grads


def _exchange(arrs, gather, name):
    n = len(arrs)
    out_shape = [jax.ShapeDtypeStruct(((N_DEV,) + a.shape) if g else a.shape, a.dtype)
                 for a, g in zip(arrs, gather)]

    def body(*refs):
        ins, outs = refs[:n], refs[n:2 * n]
        send_sems, recv_sems, local_sems = refs[2 * n:]
        ix, iy, ic = lax.axis_index("x"), lax.axis_index("y"), lax.axis_index("c")
        me = 4 * ix + 2 * iy + ic
        copies = []
        for a in range(n):
            mine = ins[a] if gather[a] else ins[a].at[me]
            loc = pltpu.make_async_copy(mine, outs[a].at[me], local_sems.at[a])
            loc.start()
            copies.append(loc)
            for d in range(1, N_DEV):
                px = 1 - ix if d & 4 else ix
                py = 1 - iy if d & 2 else iy
                pc = 1 - ic if d & 1 else ic
                src = ins[a] if gather[a] else ins[a].at[4 * px + 2 * py + pc]
                cp = pltpu.make_async_remote_copy(
                    src_ref=src, dst_ref=outs[a].at[me], send_sem=send_sems.at[a, d - 1],
                    recv_sem=recv_sems.at[a, d - 1], device_id=(px, py, pc), device_id_type=pl.DeviceIdType.MESH)
                cp.start()
                copies.append(cp)
        for cp in copies:
            cp.wait()

    any_spec = pl.BlockSpec(memory_space=pl.ANY)
    outs = pl.pallas_call(
        body, name=name, in_specs=[any_spec] * n, out_specs=[any_spec] * n, out_shape=out_shape,
        scratch_shapes=[pltpu.SemaphoreType.DMA((n, N_DEV - 1)), pltpu.SemaphoreType.DMA((n, N_DEV - 1)),
                        pltpu.SemaphoreType.DMA((n,))],
    )(*arrs)
    return list(outs)


def _peers(ix, iy, ic):
    out = []
    for d in range(1, N_DEV):
        px = 1 - ix if d & 4 else ix
        py = 1 - iy if d & 2 else iy
        pc = 1 - ic if d & 1 else ic
        out.append((d - 1, (px, py, pc), 4 * px + 2 * py + pc))
    return out


def _exchange_start(arrs, gather, name):
    n = len(arrs)
    lands = [lax.empty(((N_DEV,) + a.shape) if g else a.shape, a.dtype) for a, g in zip(arrs, gather)]

    def body(*refs):
        ins, land_refs = refs[:n], refs[n:2 * n]
        send_sems, recv_sems = refs[2 * n], refs[2 * n + 1]
        token = refs[-1]
        ix, iy, ic = lax.axis_index("x"), lax.axis_index("y"), lax.axis_index("c")
        me = 4 * ix + 2 * iy + ic
        for a in range(n):
            for k, peer, peer_idx in _peers(ix, iy, ic):
                pltpu.make_async_remote_copy(
                    src_ref=ins[a] if gather[a] else ins[a].at[peer_idx], dst_ref=land_refs[a].at[me],
                    send_sem=send_sems.at[7 * a + k], recv_sem=recv_sems.at[7 * a + k], device_id=peer,
                    device_id_type=pl.DeviceIdType.MESH).start()
        token[...] = jnp.zeros_like(token)

    hbm = pl.BlockSpec(memory_space=pltpu.HBM)
    sem = pl.BlockSpec(memory_space=pltpu.SEMAPHORE)
    held = [pltpu.with_memory_space_constraint(a, pltpu.HBM) for a in list(arrs) + lands]
    outs = pl.pallas_call(
        body, name=name,
        out_shape=(pltpu.SemaphoreType.DMA((7 * n,)), pltpu.SemaphoreType.DMA((7 * n,)),
                   *[pltpu.HBM(a.shape, a.dtype) for a in held], jax.ShapeDtypeStruct((8, 128), F32)),
        in_specs=[hbm] * (2 * n), out_specs=(sem, sem, *[hbm] * (2 * n), pl.BlockSpec(memory_space=pltpu.VMEM)),
        input_output_aliases={a: 2 + a for a in range(2 * n)},
        compiler_params=pltpu.CompilerParams(has_side_effects=pltpu.SideEffectType.DATAFLOW_SIDE_EFFECTING),
    )(*held)
    return dict(send=outs[0], recv=outs[1], srcs=list(outs[2:2 + n]), lands=list(outs[2 + n:2 + 2 * n]),
                token=outs[-1], gather=list(gather))


def _exchange_wait(flight, after, me, name):
    n = len(flight["srcs"])
    gather = flight["gather"]

    def body(*refs):
        srcs, land_refs = refs[:n], refs[n:2 * n]
        send_sems, recv_sems = refs[2 * n], refs[2 * n + 1]
        ix, iy, ic = lax.axis_index("x"), lax.axis_index("y"), lax.axis_index("c")
        mine = 4 * ix + 2 * iy + ic
        for a in range(n):
            for k, peer, peer_idx in _peers(ix, iy, ic):
                cp = pltpu.make_async_remote_copy(
                    src_ref=srcs[a] if gather[a] else srcs[a].at[peer_idx], dst_ref=land_refs[a].at[mine],
                    send_sem=send_sems.at[7 * a + k], recv_sem=recv_sems.at[7 * a + k], device_id=peer,
                    device_id_type=pl.DeviceIdType.MESH)
                cp.wait_send()
                cp.wait_recv()

    hbm = pl.BlockSpec(memory_space=pltpu.HBM)
    sem = pl.BlockSpec(memory_space=pltpu.SEMAPHORE)
    held = flight["srcs"] + flight["lands"]
    outs = pl.pallas_call(
        body, name=name, out_shape=tuple(pltpu.HBM(a.shape, a.dtype) for a in held),
        in_specs=[hbm] * (2 * n) + [sem, sem, pl.BlockSpec(memory_space=pl.ANY)], out_specs=tuple([hbm] * (2 * n)),
        input_output_aliases={a: a for a in range(2 * n)},
        compiler_params=pltpu.CompilerParams(has_side_effects=pltpu.SideEffectType.DATAFLOW_SIDE_EFFECTING),
    )(*held, flight["send"], flight["recv"], after)
    got = []
    for a in range(n):
        src, land = outs[a], outs[n + a]
        own = src if gather[a] else lax.dynamic_index_in_dim(src, me, axis=0, keepdims=False)
        got.append(lax.dynamic_update_index_in_dim(land, own, me, axis=0))
    return got


def _ada_fwd(c_all, ada_w, ada_b_cols):
    nb, D = c_all.shape
    cols = ada_w.shape[2]

    def body(c_ref, w_ref, b_ref, out_ref):
        ca = _silu(c_ref[...])
        for l in range(DEPTH):
            out_ref[l] = _mm(ca, w_ref[l]) + b_ref[l:l + 1, :]

    return pl.pallas_call(
        body, name="ada_fwd", out_shape=jax.ShapeDtypeStruct((DEPTH, nb, cols), F32),
        in_specs=[pl.BlockSpec(memory_space=pltpu.VMEM)] * 3, out_specs=pl.BlockSpec(memory_space=pltpu.VMEM),
        compiler_params=pltpu.CompilerParams(vmem_limit_bytes=32 * VMEM_MB),
    )(c_all, ada_w, ada_b_cols)


def _ada_bwd(c_all, d_mod_cols):
    nb, D = c_all.shape
    cols = d_mod_cols.shape[2]

    def body(c_ref, dm_ref, out_ref):
        ca = _silu(c_ref[...])
        for l in range(DEPTH):
            out_ref[l] = _mm_tn(ca, dm_ref[l])

    return pl.pallas_call(
        body, name="ada_bwd", out_shape=jax.ShapeDtypeStruct((DEPTH, D, cols), F32),
        in_specs=[pl.BlockSpec(memory_space=pltpu.VMEM)] * 2, out_specs=pl.BlockSpec(memory_space=pltpu.VMEM),
        compiler_params=pltpu.CompilerParams(vmem_limit_bytes=32 * VMEM_MB),
    )(c_all, d_mod_cols)


def _sum_adamw(parts, w, m, v, name):
    P, R, C = parts.shape
    tr = 256 if (R % 256 == 0 and R > 256) else R

    def body(p_ref, w_ref, m_ref, v_ref, g_ref, d_ref, nm_ref, nv_ref):
        g = p_ref[0].astype(F32)
        for k in range(1, P):
            g = g + p_ref[k].astype(F32)
        g_ref[...] = g
        nm = ADAM_B1 * m_ref[...] + (1.0 - ADAM_B1) * g
        nv = ADAM_B2 * v_ref[...] + (1.0 - ADAM_B2) * (g * g)
        nm_ref[...] = nm
        nv_ref[...] = nv
        m_hat = nm / (1.0 - ADAM_B1 ** ADAM_STEP)
        v_hat = nv / (1.0 - ADAM_B2 ** ADAM_STEP)
        d_ref[...] = -ADAM_LR * (m_hat / (jnp.sqrt(v_hat) + ADAM_EPS) + ADAM_WD * w_ref[...])

    blk = pl.BlockSpec((tr, C), lambda i: (i, 0))
    shp = jax.ShapeDtypeStruct((R, C), F32)
    return pl.pallas_call(
        body, name=name, grid=(R // tr,),
        in_specs=[pl.BlockSpec((P, tr, C), lambda i: (0, i, 0)), blk, blk, blk],
        out_specs=[blk, blk, blk, blk], out_shape=[shp, shp, shp, shp],
        compiler_params=_cp(("parallel",)),
    )(parts, w, m, v)


SMALL = ["norm_w", "mla_q_norm", "mla_kv_norm", "gla_w_g2", "gla_b_g2", "gla_norm", "final_norm"]


SMALL_ROWS = 72


def _pack_small(loss, part):
    flat = [jnp.pad(loss.reshape(1), (0, 127))] + [part[n].reshape(-1) for n in SMALL]
    used = sum(f.shape[0] for f in flat)
    flat.append(jnp.zeros((SMALL_ROWS * 128 - used,), F32))
    return jnp.concatenate(flat).reshape(SMALL_ROWS, 128)


def _small_adamw(packed_parts, w, m, v):
    n = len(w)

    def body(*refs):
        p_ref = refs[0]
        w_refs, m_refs, v_refs = refs[1:1 + n], refs[1 + n:1 + 2 * n], refs[1 + 2 * n:1 + 3 * n]
        outs, acc = refs[1 + 3 * n:-1], refs[-1]
        total = p_ref[0]
        for k in range(1, N_DEV):
            total = total + p_ref[k]
        acc[...] = total
        outs[0][...] = acc[0:1, :]
        r0 = 1
        for i in range(n):
            shp = w_refs[i].shape
            if len(shp) == 3:
                g = acc[r0:r0 + shp[0] * shp[1], :].reshape(shp)
                r0 += shp[0] * shp[1]
            elif shp[1] < 128:
                g = acc[r0:r0 + shp[0], 0:shp[1]]
                r0 += shp[0]
            else:
                k = shp[1] // 128
                g = jnp.concatenate(
                    [jnp.concatenate([acc[r0 + l * k + j:r0 + l * k + j + 1, :] for j in range(k)], axis=1)
                     for l in range(shp[0])], axis=0)
                r0 += shp[0] * k
            nm = ADAM_B1 * m_refs[i][...] + (1.0 - ADAM_B1) * g
            nv = ADAM_B2 * v_refs[i][...] + (1.0 - ADAM_B2) * (g * g)
            m_hat = nm / (1.0 - ADAM_B1 ** ADAM_STEP)
            v_hat = nv / (1.0 - ADAM_B2 ** ADAM_STEP)
            outs[1 + 4 * i][...] = g
            outs[2 + 4 * i][...] = -ADAM_LR * (m_hat / (jnp.sqrt(v_hat) + ADAM_EPS) + ADAM_WD * w_refs[i][...])
            outs[3 + 4 * i][...] = nm
            outs[4 + 4 * i][...] = nv

    vmem = pl.BlockSpec(memory_space=pltpu.VMEM)
    out_shape = [jax.ShapeDtypeStruct((1, 128), F32)]
    for a in w:
        out_shape += [jax.ShapeDtypeStruct(a.shape, F32)] * 4
    outs = pl.pallas_call(
        body, name="adamw_small", in_specs=[vmem] * (1 + 3 * n), out_specs=[vmem] * (1 + 4 * n), out_shape=out_shape,
        scratch_shapes=[pltpu.VMEM((SMALL_ROWS, 128), F32)],
    )(packed_parts, *w, *m, *v)
    return outs[0], [outs[1 + 4 * i:5 + 4 * i] for i in range(n)]


WEIGHTS = ["norm_w", "ada_w", "ada_b", "w_in", "mla_q_norm", "w_uq", "mla_kv_norm", "w_ukv", "gla_w_g2",
           "gla_b_g2", "gla_norm", "w_out", "final_norm"]


def kernel(x, c, positions, norm_w, ada_w, ada_b, w_in, mla_q_norm, w_uq, mla_kv_norm, w_ukv, gla_w_g2, gla_b_g2, gla_norm, w_out, final_norm, loss_target, m_norm_w, m_ada_w, m_ada_b, m_w_in, m_mla_q_norm, m_w_uq, m_mla_kv_norm, m_w_ukv, m_gla_w_g2, m_gla_b_g2, m_gla_norm, m_w_out, m_final_norm, v_norm_w, v_ada_w, v_ada_b, v_w_in, v_mla_q_norm, v_w_uq, v_mla_kv_norm, v_w_ukv, v_gla_w_g2, v_gla_b_g2, v_gla_norm, v_w_out, v_final_norm):
    w = dict(norm_w=norm_w, ada_w=ada_w, ada_b=ada_b, w_in=w_in, mla_q_norm=mla_q_norm, w_uq=w_uq,
             mla_kv_norm=mla_kv_norm, w_ukv=w_ukv, gla_w_g2=gla_w_g2, gla_b_g2=gla_b_g2, gla_norm=gla_norm,
             w_out=w_out, final_norm=final_norm)
    m = dict(norm_w=m_norm_w, ada_w=m_ada_w, ada_b=m_ada_b, w_in=m_w_in, mla_q_norm=m_mla_q_norm, w_uq=m_w_uq,
             mla_kv_norm=m_mla_kv_norm, w_ukv=m_w_ukv, gla_w_g2=m_gla_w_g2, gla_b_g2=m_gla_b_g2,
             gla_norm=m_gla_norm, w_out=m_w_out, final_norm=m_final_norm)
    v = dict(norm_w=v_norm_w, ada_w=v_ada_w, ada_b=v_ada_b, w_in=v_w_in, mla_q_norm=v_mla_q_norm, w_uq=v_w_uq,
             mla_kv_norm=v_mla_kv_norm, w_ukv=v_w_ukv, gla_w_g2=v_gla_w_g2, gla_b_g2=v_gla_b_g2,
             gla_norm=v_gla_norm, w_out=v_w_out, final_norm=v_final_norm)
    B, S, D = x.shape
    me = 4 * lax.axis_index("x") + 2 * lax.axis_index("y") + lax.axis_index("c")
    ada_cols = ada_w.shape[2]
    cast = lambda a: a.astype(_MXU)

    sharded = ["w_in", "w_uq", "w_ukv", "w_out"]

    whole_cols = lambda a: jnp.transpose(a, (1, 0, 2)).reshape(a.shape[1], -1)
    whole_in = lambda blk: _arrange_w_in(whole_cols(blk))
    whole_rest = lambda blks: (_arrange_w_uq(whole_cols(blks[0])), _arrange_w_ukv(whole_cols(blks[1])),
                               blks[2].reshape(D, D))
    col_blocks = lambda a: jnp.transpose(a.reshape(a.shape[0], N_DEV, -1), (1, 0, 2)).astype(jnp.bfloat16)
    blocks_in = lambda dw_in_a: col_blocks(_unarrange_w_in(dw_in_a))
    blocks_rest = lambda dw_out, dw_uq_a, dw_ukv_a: [
        col_blocks(_unarrange_w_uq(dw_uq_a)), col_blocks(_unarrange_w_ukv(dw_ukv_a)),
        dw_out.reshape(N_DEV, D // N_DEV, D).astype(jnp.bfloat16)]

    got0 = _exchange([c, cast(w_in[0])], [True, True], "gather_first")
    c_all = got0[0].reshape(N_DEV * B, D)
    flight_r = _exchange_start([cast(w[n][0]) for n in sharded[1:]], [True] * 3, "gather_start_layer0")
    flight_w = _exchange_start([cast(w[n][1]) for n in sharded], [True] * 4, "gather_start_layer1")
    started = flight_r["token"][0, 0] + flight_w["token"][0, 0]

    ada_b_cols = lax.dynamic_slice(ada_b, (0, me * ada_cols), (DEPTH, ada_cols))
    mod_cols = _ada_fwd(c_all, ada_w, ada_b_cols)
    mod_send = jnp.transpose(mod_cols.reshape(DEPTH, N_DEV, B, ada_cols), (1, 0, 2, 3))
    (mod_recv,) = _exchange([mod_send], [False], "scatter_mod")
    mod = jnp.transpose(mod_recv, (1, 2, 0, 3)).reshape(DEPTH, B, 3 * D)

    small_w = {n: w[n] for n in SMALL}
    layer_small = lambda l: {n: a[l] for n, a in small_w.items() if n != "final_norm"}
    tabs = _rope_tables(positions.reshape(B, S, 1))
    late0 = lambda after: whole_rest(_exchange_wait(flight_r, after, me, "gather_wait_layer0"))
    x1, saved0 = _layer_fwd(x, tabs, mod[0] + started, layer_small(0), whole_in(got0[1]), late_weights=late0)
    got1 = _exchange_wait(flight_w, x1, me, "gather_wait_layer1")
    x2, saved1 = _layer_fwd(x1, tabs, mod[1], layer_small(1), whole_in(got1[0]), *whole_rest(got1[1:]))
    dx, loss, d_fw = _final_loss(x2, final_norm.reshape(1, D), loss_target)

    dx, g1 = _layer_bwd(dx, saved1, tabs)
    flight_g = _exchange_start([blocks_in(g1["w_in_a"])] + blocks_rest(g1["w_out"], g1["w_uq_a"], g1["w_ukv_a"]),
                               [False] * 4, "grads_start_layer1")
    flights = {}

    def early0(dw_out, dw_uq_a, dw_ukv_a):
        flights["rest0"] = _exchange_start(blocks_rest(dw_out, dw_uq_a, dw_ukv_a), [False] * 3, "grads_start_layer0")
        return flights["rest0"]["token"][0, 0]

    saved0 = dict(saved0, gate=saved0["gate"] + flight_g["token"][0, 0])
    grad_x, g0 = _layer_bwd(dx, saved0, tabs, early_grads=early0)
    parts1 = _exchange_wait(flight_g, grad_x, me, "grads_wait_layer1")
    rest0 = _exchange_wait(flights["rest0"], g0["w_in_a"], me, "grads_wait_layer0")

    both = lambda n: jnp.stack([g0[n], g1[n]])
    d_mod = both("d_mod")
    part = dict(norm_w=both("norm_w"), mla_q_norm=both("mla_q_norm"), mla_kv_norm=both("mla_kv_norm"),
                gla_w_g2=both("gla_w_g2"), gla_b_g2=both("gla_b_g2"), gla_norm=both("gla_norm256")[:, 0:128],
                final_norm=d_fw)
    d_mod_g, small_g, in0 = _exchange([d_mod, _pack_small(loss, part), blocks_in(g0["w_in_a"])],
                                      [True, True, False], "exchange_last")
    parts0 = [in0] + rest0

    d_mod_all = jnp.transpose(d_mod_g, (1, 0, 2, 3)).reshape(DEPTH, N_DEV * B, 3 * D)
    d_mod_cols = lax.dynamic_slice(d_mod_all, (0, 0, me * ada_cols), (DEPTH, N_DEV * B, ada_cols))
    g_ada_w = _ada_bwd(c_all, d_mod_cols)

    res = {}

    def update(name, parts2d):
        shp = w[name].shape
        two = lambda a: a.reshape(parts2d.shape[1:])
        out = _sum_adamw(parts2d, two(w[name]), two(m[name]), two(v[name]), "adamw_" + name)
        res[name] = [o.reshape(shp) for o in out]

    update("ada_w", g_ada_w.reshape(1, DEPTH * D, ada_cols))
    update("ada_b", jnp.transpose(d_mod_g, (0, 2, 1, 3)).reshape(N_DEV * B, DEPTH * 3 * D // 128, 128))
    for a, name in enumerate(sharded):
        update(name, jnp.concatenate([parts0[a], parts1[a]], axis=1))
    row = lambda a: a.reshape(1, D) if a.ndim == 1 else a
    loss_sum, small_out = _small_adamw(small_g, [row(w[n]) for n in SMALL], [row(m[n]) for n in SMALL],
                                       [row(v[n]) for n in SMALL])
    for n, outs in zip(SMALL, small_out):
        res[n] = [o.reshape(w[n].shape) for o in outs]
    loss_out = loss_sum[0, 0]
    return (loss_out, grad_x, *[res[n][0] for n in WEIGHTS], *[res[n][1] for n in WEIGHTS],
            *[res[n][2] for n in WEIGHTS], *[res[n][3] for n in WEIGHTS])
```

```python
import functools
import math

import numpy as np
import jax
import jax.numpy as jnp
from jax import lax
from jax.experimental import pallas as pl
from jax.experimental.pallas import tpu as pltpu

F32 = jnp.float32
_MXU = jnp.bfloat16

D_MODEL = 1024
DEPTH = 2
CHUNK = 64
EPS = 1e-6
ROPE_THETA = 10000.0
N_DEV = 8

MLA_SCALE = 96.0 ** -0.5
RET_KSCALE = 64.0 ** -0.5
GLA_KSCALE = 32.0 ** -0.5
GLA_TAU = 16.0

ADAM_LR = 0.001
ADAM_B1 = 0.9
ADAM_B2 = 0.999
ADAM_EPS = 1e-08
ADAM_WD = 0.01
ADAM_STEP = 10

RET_W, MLA_W, GLA_W = 1024, 1024, 896
ARR_W = RET_W + MLA_W + GLA_W
VMEM_MB = 1024 * 1024


def _cp(sem, vmem_mb=48):
    return pltpu.CompilerParams(dimension_semantics=sem, vmem_limit_bytes=vmem_mb * VMEM_MB)


def _mm(a, b):
    return jnp.dot(a.astype(_MXU), b.astype(_MXU), preferred_element_type=F32)


def _mm_nt(a, b):
    return lax.dot_general(a.astype(_MXU), b.astype(_MXU), (((1,), (1,)), ((), ())),
                           preferred_element_type=F32)


def _mm_tn(a, b):
    return lax.dot_general(a.astype(_MXU), b.astype(_MXU), (((0,), (0,)), ((), ())),
                           preferred_element_type=F32)


def _mm_f32(a, b):
    return jnp.dot(a, b, precision=lax.Precision.HIGHEST, preferred_element_type=F32)


def _sig(z):
    return 1.0 / (1.0 + jnp.exp(-z))


def _silu(z):
    return z * _sig(z)


def _dsilu(z):
    s = _sig(z)
    return s * (1.0 + z * (1.0 - s))


def _full(shape):
    nd = len(shape)
    return pl.BlockSpec(shape, lambda *_: (0,) * nd)


def _qk_perm(blk):
    r = blk.shape[0]
    return jnp.transpose(blk.reshape(r, 4, 2, 32), (0, 2, 1, 3)).reshape(r, 256)


def _qk_unperm(blk):
    r = blk.shape[0]
    return jnp.transpose(blk.reshape(r, 2, 4, 32), (0, 2, 1, 3)).reshape(r, 256)


def _arrange_w_in(w):
    z = lambda n: jnp.zeros((w.shape[0], n), w.dtype)
    ret = [_qk_perm(w[:, 0:256]), _qk_perm(w[:, 256:512]), w[:, 512:768], w[:, 768:1024]]
    mla = [w[:, 1024:1280], w[:, 1280:1408], z(64), w[:, 1408:1440], z(32), w[:, 1440:1952]]
    gla = [w[:, 1952:2080], w[:, 2080:2208], w[:, 2208:2464], w[:, 2464:2480], z(112), w[:, 2480:2736]]
    return jnp.concatenate(ret + mla + gla, axis=1)


def _unarrange_w_in(a):
    m, g = RET_W, RET_W + MLA_W
    parts = [_qk_unperm(a[:, 0:256]), _qk_unperm(a[:, 256:512]), a[:, 512:1024],
             a[:, m:m + 384], a[:, m + 448:m + 480], a[:, m + 512:m + 1024],
             a[:, g:g + 528], a[:, g + 640:g + 896]]
    return jnp.concatenate(parts, axis=1)


def _arrange_w_uq(w):
    return jnp.pad(w.reshape(256, 8, 96), ((0, 0), (0, 0), (0, 32))).reshape(256, 1024)


def _unarrange_w_uq(a):
    return a.reshape(256, 8, 128)[:, :, :96].reshape(256, 768)


def _arrange_w_ukv(w):
    r = w.reshape(128, 8, 128)
    k = jnp.pad(r[:, :, :64], ((0, 0), (0, 0), (0, 64))).reshape(128, 1024)
    return jnp.concatenate([k, r[:, :, 64:].reshape(128, 512)], axis=1)


def _unarrange_w_ukv(a):
    k = a[:, :1024].reshape(128, 8, 128)[:, :, :64]
    v = a[:, 1024:].reshape(128, 8, 64)
    return jnp.concatenate([k, v], axis=2).reshape(128, 1024)


def _rope_tables(pos3):
    B, S, _ = pos3.shape
    ts = min(S, 512)
    inv32 = (np.float32(ROPE_THETA) ** (-(np.arange(32, dtype=np.float32) / 32))).astype(np.float32)
    inv16 = (np.float32(ROPE_THETA) ** (-(np.arange(16, dtype=np.float32) / 16))).astype(np.float32)
    inv = np.zeros((1, 128), np.float32)
    inv[0, 0:32] = inv32
    inv[0, 32:48] = inv16

    def body(pos_ref, inv_ref, cr, sr, cm, sm):
        ang = pos_ref[0].astype(F32) * inv_ref[...]
        lane = lax.broadcasted_iota(jnp.int32, (1, 128), 1)

        def every_head(x):
            y = jnp.where(lane < 32, x, pltpu.roll(x, 32, 1))
            return jnp.where(lane < 64, y, pltpu.roll(y, 64, 1))

        def rotary_pair(x, fill):
            return jnp.where((lane >= 64) & (lane < 80), pltpu.roll(x, 32, 1),
                             jnp.where((lane >= 80) & (lane < 96), pltpu.roll(x, 48, 1), fill))

        c, s = jnp.cos(ang), jnp.sin(ang)
        cr[0] = every_head(c)
        sr[0] = every_head(s)
        cm[0] = rotary_pair(c, 1.0)
        sm[0] = rotary_pair(s, 0.0)

    tab = jax.ShapeDtypeStruct((B, S, 128), F32)
    blk = pl.BlockSpec((1, ts, 128), lambda b, i: (b, i, 0))
    return pl.pallas_call(
        body, name="rope_tables", grid=(B, S // ts),
        in_specs=[pl.BlockSpec((1, ts, 1), lambda b, i: (b, i, 0)), _full((1, 128))],
        out_specs=[blk, blk, blk, blk], out_shape=[tab, tab, tab, tab],
        compiler_params=_cp(("parallel", "parallel")),
    )(pos3, jnp.asarray(inv))


def _rope128(x, cos, sin):
    lane = lax.broadcasted_iota(jnp.int32, (1, 128), 1)
    rp = pltpu.roll(x, 16, 1)
    rm = pltpu.roll(x, 112, 1)
    return x * cos + jnp.where(lane < 80, -rm, rp) * sin


def _rope128_t(d, cos, sin):
    lane = lax.broadcasted_iota(jnp.int32, (1, 128), 1)
    y = d * sin
    yp = pltpu.roll(y, 16, 1)
    ym = pltpu.roll(y, 112, 1)
    return d * cos + jnp.where(lane < 64, 0.0, jnp.where(lane < 80, ym, jnp.where(lane < 96, -yp, 0.0)))


def _proj_fwd(x, shift, scale, nw, w_arr):
    B, S, D = x.shape
    tm = min(S, 512)

    def body(x_ref, sh_ref, sc_ref, nw_ref, w_ref, ret_ref, mla_ref, gla_ref, h_ref):
        xv = x_ref[0]
        rstd = lax.rsqrt(jnp.mean(xv * xv, axis=-1, keepdims=True) + EPS)
        h = (xv * rstd * nw_ref[...]) * (1.0 + sc_ref[0]) + sh_ref[0]
        hb = h.astype(_MXU)
        h_ref[0] = hb
        ret_ref[0] = jnp.dot(hb, w_ref[:, 0:RET_W], preferred_element_type=F32).astype(_MXU)
        mla_ref[0] = jnp.dot(hb, w_ref[:, RET_W:RET_W + MLA_W], preferred_element_type=F32).astype(_MXU)
        gla_ref[0] = jnp.dot(hb, w_ref[:, RET_W + MLA_W:ARR_W], preferred_element_type=F32).astype(_MXU)

    tok = lambda w: pl.BlockSpec((1, tm, w), lambda b, i: (b, i, 0))
    per_seq = pl.BlockSpec((1, 1, D), lambda b, i: (b, 0, 0))
    return pl.pallas_call(
        body, name="proj_fwd", grid=(B, S // tm),
        in_specs=[tok(D), per_seq, per_seq, _full((1, D)), _full((D, ARR_W))],
        out_specs=[tok(RET_W), tok(MLA_W), tok(GLA_W), tok(D)],
        out_shape=[jax.ShapeDtypeStruct((B, S, RET_W), _MXU), jax.ShapeDtypeStruct((B, S, MLA_W), _MXU),
                   jax.ShapeDtypeStruct((B, S, GLA_W), _MXU), jax.ShapeDtypeStruct((B, S, D), _MXU)],
        compiler_params=_cp(("parallel", "parallel")),
    )(x, shift, scale, nw, w_arr)


RET_L = 256


def _ret_consts(L):
    lg = np.log1p(-np.exp2(-5.0 - np.arange(4, dtype=np.float32))).astype(np.float32)
    i = np.arange(L)
    ci = i // CHUNK
    diff = (i[:, None] - i[None, :]).astype(np.float32)
    same = ci[:, None] == ci[None, :]
    past = ci[None, :] < ci[:, None]
    expo = np.where(same, np.abs(diff), np.where(past, diff, 0.0)).astype(np.float32)
    dec = np.where((same | past)[None], np.exp(lg[:, None, None] * expo[None]), 0.0).astype(np.float32)
    head = (np.arange(256) % 128) // 32
    qw = np.exp((i + 1.0)[:, None] * lg[head][None, :]).astype(np.float32)
    kw = np.exp((L - 1.0 - i)[:, None] * lg[head][None, :]).astype(np.float32)
    a_row = np.exp(np.float32(L) * lg[head])[None, :].astype(np.float32)
    return [jnp.asarray(t) for t in (dec.reshape(4 * L, L), qw, kw, a_row)]


def _ret_masks():
    lane = lax.broadcasted_iota(jnp.int32, (1, 256), 1)
    mh = [((lane % 128) // 32) == h for h in range(4)]
    mv = [(lane // 64) == h for h in range(4)]
    vi = lax.broadcasted_iota(jnp.int32, (256, 256), 0)
    ki = lax.broadcasted_iota(jnp.int32, (256, 256), 1)
    bd = (vi // 64) == ((ki % 128) // 32)
    return mh, mv, bd


def _ret_rope(p, cs, sn):
    q1, q2, k1, k2 = p[:, 0:128], p[:, 128:256], p[:, 256:384], p[:, 384:512]
    qr = jnp.concatenate([q1 * cs - q2 * sn, q2 * cs + q1 * sn], axis=1)
    kr = jnp.concatenate([k1 * cs - k2 * sn, k2 * cs + k1 * sn], axis=1) * RET_KSCALE
    return qr, kr


def _head_mean(x, mv, width):
    out = jnp.zeros_like(x)
    for m in mv:
        s = jnp.sum(jnp.where(m, x, 0.0), axis=-1, keepdims=True) * (1.0 / width)
        out = jnp.where(m, s, out)
    return out


def _stack_heads(x, masks):
    return jnp.concatenate([jnp.where(m, x, 0.0) for m in masks], axis=0)


def _fold_heads(xs, masks, L):
    out = jnp.where(masks[0], xs[0:L], 0.0)
    for h in range(1, 4):
        out = out + jnp.where(masks[h], xs[h * L:(h + 1) * L], 0.0)
    return out


RET_G = 2


def _ret_fwd(ret_p, cos, sin):
    B, S, _ = ret_p.shape
    L = min(RET_L, S)
    NB = S // L
    G = min(RET_G, NB)
    NG = NB // G
    consts = _ret_consts(L)

    def body(p_ref, c_ref, s_ref, ds_ref, qw_ref, kw_ref, a_ref, out_ref, raw_ref, st_ref, st_sc):
        @pl.when(pl.program_id(1) == 0)
        def _():
            st_sc[...] = jnp.zeros_like(st_sc)

        mh, mv, bd = _ret_masks()
        cs_ = range(G)
        rows = [slice(c * L, (c + 1) * L) for c in cs_]
        ps = [p_ref[0, rows[c], :].astype(F32) for c in cs_]
        qk = [_ret_rope(ps[c], c_ref[0, rows[c], :], s_ref[0, rows[c], :]) for c in cs_]
        vs = [ps[c][:, 512:768] for c in cs_]
        a_s = [_mm_nt(_stack_heads(qk[c][0], mh), qk[c][1]) for c in cs_]
        upd = [_mm_tn(vs[c], qk[c][1] * kw_ref[...]) for c in cs_]
        o_s = [_mm(a_s[c] * ds_ref[...], vs[c]) for c in cs_]
        st = st_sc[...]
        inter = []
        for c in cs_:
            st_ref[0, c] = st
            inter.append(_mm_nt(qk[c][0] * qw_ref[...], st))
            st = st * a_ref[...] + jnp.where(bd, upd[c], 0.0)
        st_sc[...] = st
        for c in cs_:
            r = _fold_heads(o_s[c], mv, L) + inter[c]
            raw_ref[0, rows[c], :] = r
            rstd = lax.rsqrt(_head_mean(r * r, mv, 64.0) + EPS)
            out_ref[0, rows[c], :] = (r * rstd * _silu(ps[c][:, 768:1024])).astype(_MXU)

    tok = lambda w: pl.BlockSpec((1, G * L, w), lambda b, n: (b, n, 0))
    return pl.pallas_call(
        body, name="ret_fwd", grid=(B, NG),
        in_specs=[tok(RET_W), tok(128), tok(128), _full((4 * L, L)), _full((L, 256)), _full((L, 256)),
                  _full((1, 256))],
        out_specs=[tok(256), tok(256), pl.BlockSpec((1, G, 256, 256), lambda b, n: (b, n, 0, 0))],
        out_shape=[jax.ShapeDtypeStruct((B, S, 256), _MXU), jax.ShapeDtypeStruct((B, S, 256), F32),
                   jax.ShapeDtypeStruct((B, NB, 256, 256), F32)],
        scratch_shapes=[pltpu.VMEM((256, 256), F32)],
        compiler_params=_cp(("parallel", "arbitrary")),
    )(ret_p, cos, sin, *consts)


def _ret_bwd(ret_p, cos, sin, raw, states, d_mix):
    B, S, _ = ret_p.shape
    L = min(RET_L, S)
    NB = S // L
    G = 1
    NG = NB // G
    consts = _ret_consts(L)

    def body(p_ref, c_ref, s_ref, raw_ref, st_ref, dm_ref, ds_ref, qw_ref, kw_ref, a_ref, dp_ref, dst_sc):
        @pl.when(pl.program_id(1) == 0)
        def _():
            dst_sc[...] = jnp.zeros_like(dst_sc)

        mh, mv, bd = _ret_masks()
        qw, kw, dec = qw_ref[...], kw_ref[...], ds_ref[...]
        cs_ = range(G)
        rows = [slice(c * L, (c + 1) * L) for c in cs_]
        ps = [p_ref[0, rows[c], :].astype(F32) for c in cs_]
        tabs = [(c_ref[0, rows[c], :], s_ref[0, rows[c], :]) for c in cs_]
        qk = [_ret_rope(ps[c], *tabs[c]) for c in cs_]
        vs = [ps[c][:, 512:768] for c in cs_]
        qs = [_stack_heads(qk[c][0], mh) for c in cs_]
        a_s = [_mm_nt(qs[c], qk[c][1]) for c in cs_]
        dr, dz = [], []
        for c in cs_:
            r = raw_ref[0, rows[c], :]
            z = ps[c][:, 768:1024]
            rstd = lax.rsqrt(_head_mean(r * r, mv, 64.0) + EPS)
            rn = r * rstd
            dm = dm_ref[0, rows[c], :]
            d_rn = dm * _silu(z)
            dz.append(dm * rn * _dsilu(z))
            dr.append(rstd * (d_rn - rn * _head_mean(d_rn * rn, mv, 64.0)))
        do_s = [_stack_heads(dr[c], mv) for c in cs_]
        da_s = [_mm_nt(do_s[c], vs[c]) for c in cs_]
        sts = [st_ref[0, c] for c in cs_]
        dq_st = [_mm(dr[c], sts[c]) for c in cs_]
        dst_in = [_mm_tn(dr[c], qk[c][0] * qw) for c in cs_]
        dv = [_mm_tn(a_s[c] * dec, do_s[c]) for c in cs_]
        dqr, dkr = [], []
        for c in cs_:
            da = da_s[c] * dec
            dqr.append(_fold_heads(_mm(da, qk[c][1]), mh, L) + dq_st[c] * qw)
            dkr.append(_mm_tn(da, qs[c]))
        dst_next = dst_sc[...]
        for c in reversed(cs_):
            g = jnp.where(bd, dst_next, 0.0)
            dv[c] = dv[c] + _mm_nt(qk[c][1] * kw, g)
            dkr[c] = dkr[c] + _mm(vs[c], g) * kw
            dst_next = dst_next * a_ref[...] + jnp.where(bd, dst_in[c], 0.0)
        dst_sc[...] = dst_next
        for c in cs_:
            cs, sn = tabs[c]
            dk = dkr[c] * RET_KSCALE
            dq1, dq2 = dqr[c][:, 0:128], dqr[c][:, 128:256]
            dk1, dk2 = dk[:, 0:128], dk[:, 128:256]
            dp_ref[0, rows[c], :] = jnp.concatenate(
                [dq1 * cs + dq2 * sn, dq2 * cs - dq1 * sn, dk1 * cs + dk2 * sn, dk2 * cs - dk1 * sn, dv[c], dz[c]],
                axis=1).astype(_MXU)

    tok = lambda w: pl.BlockSpec((1, G * L, w), lambda b, i: (b, NG - 1 - i, 0))
    return pl.pallas_call(
        body, name="ret_bwd", grid=(B, NG),
        in_specs=[tok(RET_W), tok(128), tok(128), tok(256),
                  pl.BlockSpec((1, G, 256, 256), lambda b, i: (b, NG - 1 - i, 0, 0)), tok(256),
                  _full((4 * L, L)), _full((L, 256)), _full((L, 256)), _full((1, 256))],
        out_specs=tok(RET_W), out_shape=jax.ShapeDtypeStruct((B, S, RET_W), _MXU),
        scratch_shapes=[pltpu.VMEM((256, 256), F32)],
        compiler_params=_cp(("parallel", "arbitrary")),
    )(ret_p, cos, sin, raw, states, d_mix, *consts)


def _gla_masks():
    C = CHUNK
    lk = lax.broadcasted_iota(jnp.int32, (1, 128), 1)
    lv = lax.broadcasted_iota(jnp.int32, (1, 256), 1)
    mk = [(lk // 32) == h for h in range(4)]
    mv = [(lv // 64) == h for h in range(4)]
    vi = lax.broadcasted_iota(jnp.int32, (256, 128), 0)
    ki = lax.broadcasted_iota(jnp.int32, (256, 128), 1)
    bd = (vi // 64) == (ki // 32)
    ri = lax.broadcasted_iota(jnp.int32, (4 * C, C), 0) % C
    cj = lax.broadcasted_iota(jnp.int32, (4 * C, C), 1)
    lower = ri >= cj
    ti = lax.broadcasted_iota(jnp.int32, (C, C), 0)
    tj = lax.broadcasted_iota(jnp.int32, (C, C), 1)
    ltri = jnp.where(ti >= tj, 1.0, 0.0).astype(F32)
    utri = jnp.where(ti <= tj, 1.0, 0.0).astype(F32)
    return mk, mv, bd, lower, ltri, utri


def _log_sigmoid(x):
    return jnp.minimum(x, 0.0) - jnp.log(1.0 + jnp.exp(-jnp.abs(x)))


GLA_G = 8


def _gla_fwd(gla_p, w_g2p, b_g2, gnw):
    B, S, _ = gla_p.shape
    C = CHUNK
    NC = S // C
    G = min(GLA_G, NC)
    NG = NC // G

    def body(p_ref, w_ref, b_ref, gn_ref, out_ref, raw_ref, st_ref, st_sc):
        @pl.when(pl.program_id(1) == 0)
        def _():
            st_sc[...] = jnp.zeros_like(st_sc)

        mk, mv, bd, lower, ltri, _ = _gla_masks()
        cs = range(G)
        rows = [slice(c * C, (c + 1) * C) for c in cs]
        ps = [p_ref[0, rows[c], :].astype(F32) for c in cs]
        pre = [_mm(ps[c][:, 512:640], w_ref[...]) + b_ref[...] for c in cs]
        cum = [_mm_f32(ltri, _log_sigmoid(pre[c]) * (1.0 / GLA_TAU)) for c in cs]
        past, fut, upd, q_pos, a_row = [], [], [], [], []
        for c in cs:
            q = ps[c][:, 0:128]
            k = ps[c][:, 128:256] * GLA_KSCALE
            last = cum[c][C - 1:C, :]
            e_pos = jnp.exp(cum[c])
            e_neg = jnp.exp(-cum[c])
            q_pos.append(q * e_pos)
            a_row.append(jnp.exp(last))
            past.append(_mm_nt(_stack_heads(q_pos[c], mk), k * e_neg))
            fut.append(_mm_nt(_stack_heads(q * e_neg, mk), k * e_pos))
            upd.append(_mm_tn(ps[c][:, 256:512], k * jnp.exp(last - cum[c])))
        o_s = [_mm(jnp.where(lower, past[c], fut[c]), ps[c][:, 256:512]) for c in cs]
        st = st_sc[...]
        inter = []
        for c in cs:
            st_ref[0, c] = st
            inter.append(_mm_nt(q_pos[c], st))
            st = st * a_row[c] + jnp.where(bd, upd[c], 0.0)
        st_sc[...] = st
        for c in cs:
            g = _fold_heads(o_s[c], mv, C) + inter[c]
            raw_ref[0, rows[c], :] = g
            rstd = lax.rsqrt(_head_mean(g * g, mv, 64.0) + EPS)
            out_ref[0, rows[c], :] = (g * rstd * gn_ref[...] * _silu(ps[c][:, 640:896])).astype(_MXU)

    tok = lambda w: pl.BlockSpec((1, G * C, w), lambda b, n: (b, n, 0))
    return pl.pallas_call(
        body, name="gla_fwd", grid=(B, NG),
        in_specs=[tok(GLA_W), _full((128, 128)), _full((1, 128)), _full((1, 256))],
        out_specs=[tok(256), tok(256), pl.BlockSpec((1, G, 256, 128), lambda b, n: (b, n, 0, 0))],
        out_shape=[jax.ShapeDtypeStruct((B, S, 256), _MXU), jax.ShapeDtypeStruct((B, S, 256), F32),
                   jax.ShapeDtypeStruct((B, NC, 256, 128), F32)],
        scratch_shapes=[pltpu.VMEM((256, 128), F32)],
        compiler_params=_cp(("parallel", "arbitrary")),
    )(gla_p, w_g2p, b_g2, gnw)


def _gla_bwd(gla_p, w_g2p, b_g2, gnw, raw, states, d_mix):
    B, S, _ = gla_p.shape
    C = CHUNK
    NC = S // C
    G = min(GLA_G, NC)
    NG = NC // G

    def body(p_ref, w_ref, b_ref, gn_ref, raw_ref, st_ref, dm_ref, dp_ref, dw_ref, db_ref, dgn_ref, dst_sc):
        first = (pl.program_id(0) == 0) & (pl.program_id(1) == 0)

        @pl.when(first)
        def _():
            dw_ref[...] = jnp.zeros_like(dw_ref)
            db_ref[...] = jnp.zeros_like(db_ref)
            dgn_ref[...] = jnp.zeros_like(dgn_ref)

        @pl.when(pl.program_id(1) == 0)
        def _():
            dst_sc[...] = jnp.zeros_like(dst_sc)

        mk, mv, bd, lower, ltri, utri = _gla_masks()
        gn = gn_ref[...]
        cs = range(G)
        rows = [slice(c * C, (c + 1) * C) for c in cs]
        ps = [p_ref[0, rows[c], :].astype(F32) for c in cs]
        vs = [ps[c][:, 256:512] for c in cs]
        pre = [_mm(ps[c][:, 512:640], w_ref[...]) + b_ref[...] for c in cs]
        cum = [_mm_f32(ltri, _log_sigmoid(pre[c]) * (1.0 / GLA_TAU)) for c in cs]
        dg, dz, dgn_acc = [], [], jnp.zeros((1, 256), F32)
        for c in cs:
            g = raw_ref[0, rows[c], :]
            z = ps[c][:, 640:896]
            rstd = lax.rsqrt(_head_mean(g * g, mv, 64.0) + EPS)
            gh = g * rstd
            dm = dm_ref[0, rows[c], :]
            d_gn = dm * _silu(z)
            dz.append(dm * gh * gn * _dsilu(z))
            dgn_acc = dgn_acc + jnp.sum(d_gn * gh, axis=0, keepdims=True)
            d_gh = d_gn * gn
            dg.append(rstd * (d_gh - gh * _head_mean(d_gh * gh, mv, 64.0)))
        do_s = [_stack_heads(dg[c], mv) for c in cs]
        dattn = [_mm_nt(do_s[c], vs[c]) for c in cs]
        ks, e_pos, e_neg, q_pos, q_neg, k_pos, k_neg, qp_s, qn_s, past, fut, a_row, w_dec, kd = ([] for _ in range(14))
        for c in cs:
            q = ps[c][:, 0:128]
            k = ps[c][:, 128:256] * GLA_KSCALE
            last = cum[c][C - 1:C, :]
            ep, en = jnp.exp(cum[c]), jnp.exp(-cum[c])
            ks.append(k), e_pos.append(ep), e_neg.append(en)
            q_pos.append(q * ep), q_neg.append(q * en), k_pos.append(k * ep), k_neg.append(k * en)
            qp_s.append(_stack_heads(q_pos[c], mk)), qn_s.append(_stack_heads(q_neg[c], mk))
            past.append(_mm_nt(qp_s[c], k_neg[c]))
            fut.append(_mm_nt(qn_s[c], k_pos[c]))
            a_row.append(jnp.exp(last))
            w_dec.append(jnp.exp(last - cum[c]))
            kd.append(k * w_dec[c])
        sts = [st_ref[0, c] for c in cs]
        dq_st = [_mm(dg[c], sts[c]) for c in cs]
        dst_in = [_mm_tn(dg[c], q_pos[c]) for c in cs]
        dv, dq_pos, dk_neg, dq_neg, dk_pos = [], [], [], [], []
        for c in cs:
            attn = jnp.where(lower, past[c], fut[c])
            dpast = jnp.where(lower, dattn[c], 0.0)
            dfut = jnp.where(lower, 0.0, dattn[c])
            dv.append(_mm_tn(attn, do_s[c]))
            dq_pos.append(_fold_heads(_mm(dpast, k_neg[c]), mk, C) + dq_st[c])
            dk_neg.append(_mm_tn(dpast, qp_s[c]))
            dq_neg.append(_fold_heads(_mm(dfut, k_pos[c]), mk, C))
            dk_pos.append(_mm_tn(dfut, qn_s[c]))
        dst_next = dst_sc[...]
        d_a, d_kd = [None] * G, [None] * G
        for c in reversed(cs):
            d_a[c] = jnp.sum(dst_next * sts[c], axis=0, keepdims=True)
            gmat = jnp.where(bd, dst_next, 0.0)
            d_kd[c] = _mm(vs[c], gmat)
            dv[c] = dv[c] + _mm_nt(kd[c], gmat)
            dst_next = dst_next * a_row[c] + jnp.where(bd, dst_in[c], 0.0)
        dst_sc[...] = dst_next
        row = lax.broadcasted_iota(jnp.int32, (C, 128), 0)
        d_la, dk, dq = [], [], []
        for c in cs:
            t = d_kd[c] * kd[c]
            dk.append(d_kd[c] * w_dec[c] + dk_neg[c] * e_neg[c] + dk_pos[c] * e_pos[c])
            dq.append(dq_pos[c] * e_pos[c] + dq_neg[c] * e_neg[c])
            d_last = jnp.sum(t, axis=0, keepdims=True) + d_a[c] * a_row[c]
            d_cum = (dq_pos[c] * q_pos[c] - dk_neg[c] * k_neg[c] - dq_neg[c] * q_neg[c] + dk_pos[c] * k_pos[c] - t)
            d_la.append(_mm_f32(utri, d_cum + jnp.where(row == C - 1, d_last, 0.0)))
        d_pre = [d_la[c] * _sig(-pre[c]) * (1.0 / GLA_TAU) for c in cs]
        d_gg = [_mm_nt(d_pre[c], w_ref[...]) for c in cs]
        dw_acc = _mm_tn(ps[0][:, 512:640], d_pre[0])
        db_acc = jnp.sum(d_pre[0], axis=0, keepdims=True)
        for c in cs[1:]:
            dw_acc = dw_acc + _mm_tn(ps[c][:, 512:640], d_pre[c])
            db_acc = db_acc + jnp.sum(d_pre[c], axis=0, keepdims=True)
        for c in cs:
            dp_ref[0, rows[c], :] = jnp.concatenate([dq[c], dk[c] * GLA_KSCALE, dv[c], d_gg[c], dz[c]],
                                                    axis=1).astype(_MXU)
        dw_ref[...] += dw_acc
        db_ref[...] += db_acc
        dgn_ref[...] += dgn_acc

        @pl.when((pl.program_id(0) == B - 1) & (pl.program_id(1) == NG - 1))
        def _():
            s1 = dgn_ref[...]
            s1 = s1 + pltpu.roll(s1, 128, 1)
            dgn_ref[...] = s1 + pltpu.roll(s1, 64, 1)

    tok = lambda w: pl.BlockSpec((1, G * C, w), lambda b, i: (b, NG - 1 - i, 0))
    return pl.pallas_call(
        body, name="gla_bwd", grid=(B, NG),
        in_specs=[tok(GLA_W), _full((128, 128)), _full((1, 128)), _full((1, 256)), tok(256),
                  pl.BlockSpec((1, G, 256, 128), lambda b, i: (b, NG - 1 - i, 0, 0)), tok(256)],
        out_specs=[tok(GLA_W), _full((128, 128)), _full((1, 128)), _full((1, 256))],
        out_shape=[jax.ShapeDtypeStruct((B, S, GLA_W), _MXU), jax.ShapeDtypeStruct((128, 128), F32),
                   jax.ShapeDtypeStruct((1, 128), F32), jax.ShapeDtypeStruct((1, 256), F32)],
        scratch_shapes=[pltpu.VMEM((256, 128), F32)],
        compiler_params=_cp(("arbitrary", "arbitrary")),
    )(gla_p, w_g2p, b_g2, gnw, raw, states, d_mix)


def _rms(x, w):
    rstd = lax.rsqrt(jnp.mean(x * x, axis=-1, keepdims=True) + EPS)
    xh = x * rstd
    return xh, rstd, xh * w


def _rms_bwd(dy, xh, rstd, w):
    dxh = dy * w
    return rstd * (dxh - xh * jnp.mean(dxh * xh, axis=-1, keepdims=True))


MLA_T = 256


def _mla_prep_fwd(mla_p, cos, sin, qnw, kvnw, w_uq, w_ukv):
    B, S, _ = mla_p.shape
    tm = min(S, 512)

    t = min(MLA_T, S)
    nt = tm // t

    def body(p_ref, c_ref, s_ref, qn_ref, kn_ref, wq_ref, wkv_ref, q_ref, k_ref, v_ref, kt_ref, vt_ref):
        p = p_ref[0].astype(F32)
        cs, sn = c_ref[0], s_ref[0]
        _, _, qn = _rms(p[:, 0:256], qn_ref[...])
        qpre = _mm(qn, wq_ref[...])
        _, _, kvn = _rms(p[:, 256:384], kn_ref[...])
        kv = _mm(kvn, wkv_ref[...])
        kpe = _rope128(p[:, 384:512], cs, sn)
        for h in range(8):
            sl = slice(128 * h, 128 * h + 128)
            q_ref[0, :, sl] = _rope128(qpre[:, sl], cs, sn).astype(_MXU)
            kh = kv[:, sl] + kpe
            k_ref[0, :, sl] = kh.astype(_MXU)
            kht = kh.T
            for n in range(nt):
                kt_ref[0, n, sl, :] = kht[:, n * t:(n + 1) * t].astype(_MXU)
        v_ref[0] = kv[:, 1024:1536].astype(_MXU)
        for pr in range(4):
            vht = kv[:, 1024 + 128 * pr:1152 + 128 * pr].T
            for n in range(nt):
                vt_ref[0, n, 128 * pr:128 * pr + 128, :] = vht[:, n * t:(n + 1) * t].astype(_MXU)

    tok = lambda w: pl.BlockSpec((1, tm, w), lambda b, i: (b, i, 0))
    tr = lambda w: pl.BlockSpec((1, nt, w, t), lambda b, i: (b, i, 0, 0))
    return pl.pallas_call(
        body, name="mla_prep_fwd", grid=(B, S // tm),
        in_specs=[tok(512), tok(128), tok(128), _full((1, 256)), _full((1, 128)), _full((256, 1024)),
                  _full((128, 1536))],
        out_specs=[tok(1024), tok(1024), tok(512), tr(1024), tr(512)],
        out_shape=[jax.ShapeDtypeStruct((B, S, 1024), _MXU), jax.ShapeDtypeStruct((B, S, 1024), _MXU),
                   jax.ShapeDtypeStruct((B, S, 512), _MXU), jax.ShapeDtypeStruct((B, S // t, 1024, t), _MXU),
                   jax.ShapeDtypeStruct((B, S // t, 512, t), _MXU)],
        compiler_params=_cp(("parallel", "parallel")),
    )(mla_p, cos, sin, qnw, kvnw, w_uq, w_ukv)


def _chunk_mask_t(t):
    kj = lax.broadcasted_iota(jnp.int32, (t, t), 0) // CHUNK
    qi = lax.broadcasted_iota(jnp.int32, (t, t), 1) // CHUNK
    return kj <= qi


MLA_HG = 8
MLA_HG_FWD = 8
LOG2E = 1.4426950408889634
MLA_C2 = MLA_SCALE * LOG2E


def _mla_attn_fwd(q, k, vt):
    B, S, _ = q.shape
    t = min(MLA_T, S)
    nq = S // t
    HG = MLA_HG_FWD
    NP = HG // 2

    def body(q_ref, k_ref, vt_ref, o_ref, lse_ref, sa, sb, m_sc, l_sc, acc_sc):
        i = pl.program_id(2)
        row = lax.broadcasted_iota(jnp.int32, (128, 1), 0)
        low = row < 64
        mask = _chunk_mask_t(t)
        m_sc[...] = jnp.full(m_sc.shape, -jnp.inf, F32)
        l_sc[...] = jnp.zeros_like(l_sc)
        acc_sc[...] = jnp.zeros_like(acc_sc)

        ones = jnp.ones((8, t), _MXU)

        def scores(j, buf):
            kb = k_ref[0, pl.ds(pl.multiple_of(j * t, t), t), :]
            for h in range(HG):
                cols = slice(128 * h, 128 * h + 128)
                buf[h] = (_mm_nt(kb[:, cols], q_ref[0, :, cols]) * MLA_C2).astype(_MXU)

        def absorb(j, buf, masked):
            vtb = vt_ref[0, j]
            for pr in range(NP):
                alphas, pvs = [], []
                for hh in range(2):
                    h = 2 * pr + hh
                    s = buf[h]
                    if masked:
                        s = jnp.where(mask, s, jnp.full_like(s, -jnp.inf))
                    m_old = m_sc[h]
                    m_new = jnp.maximum(m_old, jnp.max(s, axis=0, keepdims=True).astype(F32))
                    alpha = jnp.exp2(m_old - m_new)
                    p = jnp.exp2(s - m_new.astype(_MXU))
                    l_sc[h] = alpha * l_sc[h] + _mm(ones, p)[0:1, :]
                    m_sc[h] = m_new
                    vth = vtb[128 * pr:128 * pr + 128, :]
                    vth = jnp.where(low if hh == 0 else ~low, vth, jnp.zeros_like(vth))
                    pvs.append(_mm(vth, p))
                    alphas.append(alpha)
                acc_sc[pr] = acc_sc[pr] * jnp.where(low, alphas[0], alphas[1]) + pvs[0] + pvs[1]

        scores(0, sb)

        def pair(jj, carry):
            j0 = 2 * jj
            scores(j0 + 1, sa)
            absorb(j0, sb, False)
            scores(j0 + 2, sb)
            absorb(j0 + 1, sa, False)
            return carry

        lax.fori_loop(0, i // 2, pair, 0)

        @pl.when(i % 2 == 1)
        def _():
            scores(i, sa)
            absorb(i - 1, sb, False)
            absorb(i, sa, True)

        @pl.when(i % 2 == 0)
        def _():
            absorb(i, sb, True)

        for pr in range(NP):
            l_e, l_o = l_sc[2 * pr], l_sc[2 * pr + 1]
            o_ref[0, :, 128 * pr:128 * pr + 128] = (acc_sc[pr] / jnp.where(low, l_e, l_o)).T
            lse_ref[0, pr, 0, 0:1, :] = m_sc[2 * pr] + jnp.log(l_e) * LOG2E
            lse_ref[0, pr, 0, 1:2, :] = m_sc[2 * pr + 1] + jnp.log(l_o) * LOG2E

    return pl.pallas_call(
        body, name="mla_attn_fwd", grid=(B, 8 // HG, nq),
        in_specs=[pl.BlockSpec((1, t, 128 * HG), lambda b, g, i: (b, i, g)),
                  pl.BlockSpec((1, S, 128 * HG), lambda b, g, i: (b, 0, g)),
                  pl.BlockSpec((1, nq, 64 * HG, t), lambda b, g, i: (b, 0, g, 0))],
        out_specs=[pl.BlockSpec((1, t, 64 * HG), lambda b, g, i: (b, i, g)),
                   pl.BlockSpec((1, NP, 1, 2, t), lambda b, g, i: (b, g, i, 0, 0))],
        out_shape=[jax.ShapeDtypeStruct((B, S, 512), F32), jax.ShapeDtypeStruct((B, 4, nq, 2, t), F32)],
        scratch_shapes=[pltpu.VMEM((HG, t, t), _MXU), pltpu.VMEM((HG, t, t), _MXU), pltpu.VMEM((HG, 1, t), F32),
                        pltpu.VMEM((HG, 1, t), F32), pltpu.VMEM((NP, 128, t), F32)],
        compiler_params=_cp(("parallel", "parallel", "arbitrary")),
    )(q, k, vt)


def _mla_gate_bwd(d_mix, o, mla_p):
    B, S, _ = o.shape
    tm = min(S, 512)
    t = min(MLA_T, S)
    nt = tm // t

    def body(dm_ref, o_ref, z_ref, do_ref, dz_ref, dl_ref):
        dm, ov, z = dm_ref[0], o_ref[0], z_ref[0].astype(F32)
        do = dm * _silu(z)
        dz_ref[0] = (dm * ov * _dsilu(z)).astype(_MXU)
        do_ref[0] = do.astype(_MXU)
        prod = do * ov
        for pr in range(4):
            pt = prod[:, 128 * pr:128 * pr + 128].T
            se = jnp.sum(pt[0:64], axis=0, keepdims=True)
            so = jnp.sum(pt[64:128], axis=0, keepdims=True)
            for n in range(nt):
                dl_ref[0, pr, n, 0:1, :] = se[:, n * t:(n + 1) * t]
                dl_ref[0, pr, n, 1:2, :] = so[:, n * t:(n + 1) * t]

    tok = lambda c: pl.BlockSpec((1, tm, 512), lambda b, i: (b, i, c))
    return pl.pallas_call(
        body, name="mla_gate_bwd", grid=(B, S // tm),
        in_specs=[tok(0), tok(0), tok(1)],
        out_specs=[tok(0), tok(0), pl.BlockSpec((1, 4, nt, 2, t), lambda b, i: (b, 0, i, 0, 0))],
        out_shape=[jax.ShapeDtypeStruct((B, S, 512), _MXU), jax.ShapeDtypeStruct((B, S, 512), _MXU),
                   jax.ShapeDtypeStruct((B, 4, S // t, 2, t), F32)],
        compiler_params=_cp(("parallel", "parallel")),
    )(d_mix, o, mla_p)


def _mla_attn_bwd(q, k, v, kt, do, lse, dl):
    B, S, _ = q.shape
    t = min(MLA_T, S)
    nk = S // t

    HG = MLA_HG
    NP = HG // 2

    def body(q_ref, k_ref, v_ref, kt_ref, do_ref, lse_ref, dl_ref, dq_ref, dk_ref, dv_ref,
             sa, da, sb, db, dqt_sc, dk_sc, dv_sc):
        j = pl.program_id(2)

        @pl.when(j == 0)
        def _():
            dqt_sc[...] = jnp.zeros_like(dqt_sc)

        dk_sc[...] = jnp.zeros_like(dk_sc)
        dv_sc[...] = jnp.zeros_like(dv_sc)
        lane = lax.broadcasted_iota(jnp.int32, (1, 128), 1)
        low = lane < 64
        mask = _chunk_mask_t(t)

        def half(x, hh):
            return jnp.where(low if hh == 0 else ~low, x, jnp.zeros_like(x))

        def prepare(i, sbuf, dbuf):
            rows = pl.ds(pl.multiple_of(i * t, t), t)
            for h in range(HG):
                cols = slice(128 * h, 128 * h + 128)
                pc = slice(128 * (h // 2), 128 * (h // 2) + 128)
                sbuf[h] = _mm_nt(k_ref[0, :, cols], q_ref[0, rows, cols]) * MLA_C2
                dbuf[h] = _mm_nt(half(v_ref[0, :, pc], h % 2), do_ref[0, rows, pc])

        def absorb(i, sbuf, dbuf, masked):
            rows = pl.ds(pl.multiple_of(i * t, t), t)
            for h in range(HG):
                pr, hh = h // 2, h % 2
                cols = slice(128 * h, 128 * h + 128)
                pc = slice(128 * pr, 128 * pr + 128)
                p = jnp.exp2(sbuf[h] - lse_ref[0, pr, i][hh:hh + 1, :])
                if masked:
                    p = jnp.where(mask, p, 0.0)
                dv_sc[pr] += _mm(p, half(do_ref[0, rows, pc], hh))
                ds = p * (dbuf[h] - dl_ref[0, pr, i][hh:hh + 1, :])
                dqt_sc[i, cols, :] += _mm(kt_ref[0, 0, cols, :], ds)
                dk_sc[h] += _mm(ds, q_ref[0, rows, cols])

        n = nk - 1 - j
        prepare(jnp.minimum(j + 1, nk - 1), sb, db)

        def pair(jj, carry):
            i0 = j + 1 + 2 * jj
            prepare(i0 + 1, sa, da)
            absorb(i0, sb, db, False)
            prepare(jnp.where(i0 + 2 <= nk - 1, i0 + 2, j), sb, db)
            absorb(i0 + 1, sa, da, False)
            return carry

        lax.fori_loop(0, n // 2, pair, 0)

        @pl.when(n % 2 == 1)
        def _():
            prepare(j, sa, da)
            absorb(nk - 1, sb, db, False)
            absorb(j, sa, da, True)

        @pl.when(n % 2 == 0)
        def _():
            absorb(j, sb, db, True)

        for h in range(HG):
            dk_ref[0, :, 128 * h:128 * h + 128] = (dk_sc[h] * MLA_SCALE).astype(_MXU)
        for pr in range(NP):
            dv_ref[0, :, 128 * pr:128 * pr + 128] = dv_sc[pr].astype(_MXU)

        @pl.when(j == nk - 1)
        def _():
            for i in range(nk):
                dq_ref[0, i * t:(i + 1) * t, :] = (dqt_sc[i].T * MLA_SCALE).astype(_MXU)

    seq = lambda w: pl.BlockSpec((1, S, w), lambda b, g, j: (b, 0, g))
    blk = lambda w: pl.BlockSpec((1, t, w), lambda b, g, j: (b, j, g))
    stat = pl.BlockSpec((1, NP, nk, 2, t), lambda b, g, j: (b, g, 0, 0, 0))
    return pl.pallas_call(
        body, name="mla_attn_bwd", grid=(B, 8 // HG, nk),
        in_specs=[seq(128 * HG), blk(128 * HG), blk(64 * HG),
                  pl.BlockSpec((1, 1, 128 * HG, t), lambda b, g, j: (b, j, g, 0)), seq(64 * HG), stat, stat],
        out_specs=[seq(128 * HG), blk(128 * HG), blk(64 * HG)],
        out_shape=[jax.ShapeDtypeStruct((B, S, 1024), _MXU), jax.ShapeDtypeStruct((B, S, 1024), _MXU),
                   jax.ShapeDtypeStruct((B, S, 512), _MXU)],
        scratch_shapes=[pltpu.VMEM((HG, t, t), F32), pltpu.VMEM((HG, t, t), F32), pltpu.VMEM((HG, t, t), F32),
                        pltpu.VMEM((HG, t, t), F32), pltpu.VMEM((nk, 128 * HG, t), F32),
                        pltpu.VMEM((HG, t, 128), F32), pltpu.VMEM((NP, t, 128), F32)],
        compiler_params=_cp(("parallel", "parallel", "arbitrary"), 56),
    )(q, k, v, kt, do, lse, dl)


def _mla_prep_bwd(mla_p, cos, sin, qnw, kvnw, w_uq, w_ukv, dq, dk, dv):
    B, S, _ = mla_p.shape
    tm = min(S, 512)

    def body(p_ref, c_ref, s_ref, qn_ref, kn_ref, wq_ref, wkv_ref, dq_ref, dk_ref, dv_ref,
             dp_ref, dwq_ref, dwkv_ref, dqn_ref, dkn_ref):
        first = (pl.program_id(0) == 0) & (pl.program_id(1) == 0)

        @pl.when(first)
        def _():
            dwq_ref[...] = jnp.zeros_like(dwq_ref)
            dwkv_ref[...] = jnp.zeros_like(dwkv_ref)
            dqn_ref[...] = jnp.zeros_like(dqn_ref)
            dkn_ref[...] = jnp.zeros_like(dkn_ref)

        p = p_ref[0].astype(F32)
        cs, sn = c_ref[0], s_ref[0]
        lane = lax.broadcasted_iota(jnp.int32, (1, 128), 1)
        pe = (lane >= 64) & (lane < 96)
        qh, q_rstd, qn = _rms(p[:, 0:256], qn_ref[...])
        kvh, kv_rstd, kvn = _rms(p[:, 256:384], kn_ref[...])
        dqv = dq_ref[0].astype(F32)
        dkv = dk_ref[0].astype(F32)
        dqpre = jnp.concatenate(
            [_rope128_t(dqv[:, 128 * h:128 * h + 128], cs, sn) for h in range(8)], axis=1)
        dkpe = jnp.zeros((tm, 128), F32)
        for h in range(8):
            dkpe = dkpe + jnp.where(pe, dkv[:, 128 * h:128 * h + 128], 0.0)
        dkr = _rope128_t(dkpe, cs, sn)
        dkv_all = jnp.concatenate([dkv, dv_ref[0].astype(F32)], axis=1)
        d_qn = _mm_nt(dqpre, wq_ref[...])
        d_kvn = _mm_nt(dkv_all, wkv_ref[...])
        dwq_ref[...] += _mm_tn(qn, dqpre)
        dwkv_ref[...] += _mm_tn(kvn, dkv_all)
        dqn_ref[...] += jnp.sum(d_qn * qh, axis=0, keepdims=True)
        dkn_ref[...] += jnp.sum(d_kvn * kvh, axis=0, keepdims=True)
        dp_ref[0] = jnp.concatenate([_rms_bwd(d_qn, qh, q_rstd, qn_ref[...]),
                                     _rms_bwd(d_kvn, kvh, kv_rstd, kn_ref[...]), dkr], axis=1).astype(_MXU)

    tok = lambda w: pl.BlockSpec((1, tm, w), lambda b, i: (b, i, 0))
    return pl.pallas_call(
        body, name="mla_prep_bwd", grid=(B, S // tm),
        in_specs=[tok(512), tok(128), tok(128), _full((1, 256)), _full((1, 128)), _full((256, 1024)),
                  _full((128, 1536)), tok(1024), tok(1024), tok(512)],
        out_specs=[tok(512), _full((256, 1024)), _full((128, 1536)), _full((1, 256)), _full((1, 128))],
        out_shape=[jax.ShapeDtypeStruct((B, S, 512), _MXU), jax.ShapeDtypeStruct((256, 1024), F32),
                   jax.ShapeDtypeStruct((128, 1536), F32), jax.ShapeDtypeStruct((1, 256), F32),
                   jax.ShapeDtypeStruct((1, 128), F32)],
        compiler_params=_cp(("arbitrary", "arbitrary")),
    )(mla_p, cos, sin, qnw, kvnw, w_uq, w_ukv, dq, dk, dv)


def _out_fwd(x, gate, r_g, o_mla, mla_p, g_g, w_out):
    B, S, D = x.shape
    tm = min(S, 512)

    def body(x_ref, g_ref, r_ref, o_ref, z_ref, gg_ref, w_ref, xn_ref, y_ref, mm_ref):
        mm = (o_ref[0] * _silu(z_ref[0].astype(F32))).astype(_MXU)
        mm_ref[0] = mm
        y = (jnp.dot(r_ref[0], w_ref[0:256, :], preferred_element_type=F32)
             + jnp.dot(mm, w_ref[256:768, :], preferred_element_type=F32)
             + jnp.dot(gg_ref[0], w_ref[768:1024, :], preferred_element_type=F32))
        y_ref[0] = y
        xn_ref[0] = x_ref[0] + g_ref[0] * y

    tok = lambda w, c=0: pl.BlockSpec((1, tm, w), lambda b, i: (b, i, c))
    return pl.pallas_call(
        body, name="out_fwd", grid=(B, S // tm),
        in_specs=[tok(D), pl.BlockSpec((1, 1, D), lambda b, i: (b, 0, 0)), tok(256), tok(512), tok(512, 1),
                  tok(256), _full((D, D))],
        out_specs=[tok(D), tok(D), tok(512)],
        out_shape=[jax.ShapeDtypeStruct((B, S, D), F32), jax.ShapeDtypeStruct((B, S, D), F32),
                   jax.ShapeDtypeStruct((B, S, 512), _MXU)],
        compiler_params=_cp(("parallel", "parallel")),
    )(x, gate, r_g, o_mla, mla_p, g_g, w_out)


def _out_bwd(dx, y, gate, r_g, mm, g_g, w_out):
    B, S, D = dx.shape
    tm = min(S, 512)

    def body(dx_ref, y_ref, g_ref, r_ref, mm_ref, gg_ref, w_ref, dr_ref, dmm_ref, dg_ref, dw_ref, dgate_ref):
        first = (pl.program_id(0) == 0) & (pl.program_id(1) == 0)

        @pl.when(first)
        def _():
            dw_ref[...] = jnp.zeros_like(dw_ref)

        @pl.when(pl.program_id(1) == 0)
        def _():
            dgate_ref[...] = jnp.zeros_like(dgate_ref)

        dxv = dx_ref[0]
        dgate_ref[0] += jnp.sum(dxv * y_ref[0], axis=0, keepdims=True)
        dy = (dxv * g_ref[0]).astype(_MXU)
        dr_ref[0] = _mm_nt(dy, w_ref[0:256, :])
        dmm_ref[0] = _mm_nt(dy, w_ref[256:768, :])
        dg_ref[0] = _mm_nt(dy, w_ref[768:1024, :])
        dw_ref[0:256, :] += _mm_tn(r_ref[0], dy)
        dw_ref[256:768, :] += _mm_tn(mm_ref[0], dy)
        dw_ref[768:1024, :] += _mm_tn(gg_ref[0], dy)

    tok = lambda w: pl.BlockSpec((1, tm, w), lambda b, i: (b, i, 0))
    per_seq = pl.BlockSpec((1, 1, D), lambda b, i: (b, 0, 0))
    return pl.pallas_call(
        body, name="out_bwd", grid=(B, S // tm),
        in_specs=[tok(D), tok(D), per_seq, tok(256), tok(512), tok(256), _full((D, D))],
        out_specs=[tok(256), tok(512), tok(256), _full((D, D)), per_seq],
        out_shape=[jax.ShapeDtypeStruct((B, S, 256), F32), jax.ShapeDtypeStruct((B, S, 512), F32),
                   jax.ShapeDtypeStruct((B, S, 256), F32), jax.ShapeDtypeStruct((D, D), F32),
                   jax.ShapeDtypeStruct((B, 1, D), F32)],
        compiler_params=_cp(("arbitrary", "arbitrary")),
    )(dx, y, gate, r_g, mm, g_g, w_out)


def _proj_bwd_x(x, shift, scale, nw, w_arr, d_ret, d_mla, d_mz, d_gla, dx_out):
    B, S, D = x.shape
    tm = min(S, 512)

    def body(x_ref, sc_ref, nw_ref, w_ref, dr_ref, dm_ref, dz_ref, dg_ref, dxo_ref,
             dx_ref, dsh_ref, dsc_ref, dnw_ref):
        first = (pl.program_id(0) == 0) & (pl.program_id(1) == 0)

        @pl.when(first)
        def _():
            dnw_ref[...] = jnp.zeros_like(dnw_ref)

        @pl.when(pl.program_id(1) == 0)
        def _():
            dsh_ref[...] = jnp.zeros_like(dsh_ref)
            dsc_ref[...] = jnp.zeros_like(dsc_ref)

        dp = jnp.concatenate([dr_ref[0], dm_ref[0], dz_ref[0], dg_ref[0]], axis=1)
        dh = lax.dot_general(dp, w_ref[...], (((1,), (1,)), ((), ())), preferred_element_type=F32)
        xv = x_ref[0]
        rstd = lax.rsqrt(jnp.mean(xv * xv, axis=-1, keepdims=True) + EPS)
        xh = xv * rstd
        nwv = nw_ref[...]
        mod = 1.0 + sc_ref[0]
        dsh_ref[0] += jnp.sum(dh, axis=0, keepdims=True)
        dsc_ref[0] += jnp.sum(dh * xh * nwv, axis=0, keepdims=True)
        dnw_ref[...] += jnp.sum(dh * xh * mod, axis=0, keepdims=True)
        dxh = dh * nwv * mod
        dx_ref[0] = dxo_ref[0] + rstd * (dxh - xh * jnp.mean(dxh * xh, axis=-1, keepdims=True))

    tok = lambda w: pl.BlockSpec((1, tm, w), lambda b, i: (b, i, 0))
    per_seq = pl.BlockSpec((1, 1, D), lambda b, i: (b, 0, 0))
    return pl.pallas_call(
        body, name="proj_bwd_x", grid=(B, S // tm),
        in_specs=[tok(D), per_seq, _full((1, D)), _full((D, ARR_W)), tok(RET_W), tok(512), tok(512),
                  tok(GLA_W), tok(D)],
        out_specs=[tok(D), per_seq, per_seq, _full((1, D))],
        out_shape=[jax.ShapeDtypeStruct((B, S, D), F32), jax.ShapeDtypeStruct((B, 1, D), F32),
                   jax.ShapeDtypeStruct((B, 1, D), F32), jax.ShapeDtypeStruct((1, D), F32)],
        compiler_params=_cp(("arbitrary", "arbitrary")),
    )(x, scale, nw, w_arr, d_ret, d_mla, d_mz, d_gla, dx_out)


def _proj_bwd_w(h, d_ret, d_mla, d_mz, d_gla):
    B, S, D = h.shape
    tm = min(S, 512)

    def body(h_ref, dr_ref, dm_ref, dz_ref, dg_ref, dw_ref):
        first = (pl.program_id(0) == 0) & (pl.program_id(1) == 0)

        @pl.when(first)
        def _():
            dw_ref[...] = jnp.zeros_like(dw_ref)

        hv = h_ref[0]
        tn = lambda d_ref: lax.dot_general(hv, d_ref[0], (((0,), (0,)), ((), ())), preferred_element_type=F32)
        dw_ref[:, 0:RET_W] += tn(dr_ref)
        dw_ref[:, RET_W:RET_W + 512] += tn(dm_ref)
        dw_ref[:, RET_W + 512:RET_W + MLA_W] += tn(dz_ref)
        dw_ref[:, RET_W + MLA_W:ARR_W] += tn(dg_ref)

    tok = lambda w: pl.BlockSpec((1, tm, w), lambda b, i: (b, i, 0))
    return pl.pallas_call(
        body, name="proj_bwd_w", grid=(B, S // tm),
        in_specs=[tok(D), tok(RET_W), tok(512), tok(512), tok(GLA_W)],
        out_specs=_full((D, ARR_W)), out_shape=jax.ShapeDtypeStruct((D, ARR_W), F32),
        compiler_params=_cp(("arbitrary", "arbitrary"), 56),
    )(h, d_ret, d_mla, d_mz, d_gla)


def _final_loss(x, fw, target):
    B, S, D = x.shape
    tm = min(S, 512)

    def body(x_ref, fw_ref, t_ref, dx_ref, loss_ref, dfw_ref):
        first = (pl.program_id(0) == 0) & (pl.program_id(1) == 0)

        @pl.when(first)
        def _():
            loss_ref[...] = jnp.zeros_like(loss_ref)
            dfw_ref[...] = jnp.zeros_like(dfw_ref)

        xv = x_ref[0]
        fwv = fw_ref[...]
        rstd = lax.rsqrt(jnp.mean(xv * xv, axis=-1, keepdims=True) + EPS)
        xh = xv * rstd
        err = xh * fwv - t_ref[0]
        loss_ref[...] += 0.5 * jnp.sum(jnp.mean(err * err, axis=-1, keepdims=True), axis=0, keepdims=True)
        dy = err * (1.0 / D)
        dfw_ref[...] += jnp.sum(dy * xh, axis=0, keepdims=True)
        dxh = dy * fwv
        dx_ref[0] = rstd * (dxh - xh * jnp.mean(dxh * xh, axis=-1, keepdims=True))

    tok = pl.BlockSpec((1, tm, D), lambda b, i: (b, i, 0))
    return pl.pallas_call(
        body, name="final_loss", grid=(B, S // tm),
        in_specs=[tok, _full((1, D)), tok],
        out_specs=[tok, _full((1, 1)), _full((1, D))],
        out_shape=[jax.ShapeDtypeStruct((B, S, D), F32), jax.ShapeDtypeStruct((1, 1), F32),
                   jax.ShapeDtypeStruct((1, D), F32)],
        compiler_params=_cp(("arbitrary", "arbitrary")),
    )(x, fw, target)


def _local_step(x, pos3, mod, loss_target, small, w_in_a, w_uq_a, w_ukv_a, w_out_b):
    B, S, D = x.shape
    tabs = _rope_tables(pos3)
    saved = []
    for l in range(DEPTH):
        x, s = _layer_fwd(x, tabs, mod[l], {n: a[l] for n, a in small.items() if n != "final_norm"},
                          w_in_a[l], w_uq_a[l], w_ukv_a[l], w_out_b[l])
        saved.append(s)
    dx, loss, d_fw = _final_loss(x, small["final_norm"].reshape(1, D), loss_target)
    grads = dict(final_norm=d_fw.reshape(D))
    per_layer = [None] * DEPTH
    for l in reversed(range(DEPTH)):
        dx, per_layer[l] = _layer_bwd(dx, saved[l], tabs)
    for name in per_layer[0]:
        grads[name] = jnp.stack([per_layer[l][name] for l in range(DEPTH)])
    return loss, dx, grads


def _layer_fwd(x, tabs, mod_l, small_l, w_in_a, w_uq_a=None, w_ukv_a=None, w_out_b=None, late_weights=None):
    B, S, D = x.shape
    cr, sr, cm, sm = tabs
    shift = mod_l[:, 0:D].reshape(B, 1, D)
    scale = mod_l[:, D:2 * D].reshape(B, 1, D)
    gate = mod_l[:, 2 * D:3 * D].reshape(B, 1, D)
    nw = small_l["norm_w"].reshape(1, D)
    qnw = small_l["mla_q_norm"].reshape(1, 256)
    kvnw = small_l["mla_kv_norm"].reshape(1, 128)
    w_g2p = jnp.pad(small_l["gla_w_g2"], ((0, 112), (0, 0)))
    b_g2 = small_l["gla_b_g2"].reshape(1, 128)
    gnw = jnp.tile(small_l["gla_norm"], 4).reshape(1, 256)
    ret_p, mla_p, gla_p, h = _proj_fwd(x, shift, scale, nw, w_in_a)
    r_g, r_raw, r_st = _ret_fwd(ret_p, cr, sr)
    if late_weights is not None:
        w_uq_a, w_ukv_a, w_out_b = late_weights(r_raw)
    q, k, v, kt, vt = _mla_prep_fwd(mla_p, cm, sm, qnw, kvnw, w_uq_a, w_ukv_a)
    o_mla, lse = _mla_attn_fwd(q, k, vt)
    g_g, g_raw, g_st = _gla_fwd(gla_p, w_g2p, b_g2, gnw)
    x_new, y, mm = _out_fwd(x, gate, r_g, o_mla, mla_p, g_g, w_out_b)
    saved = dict(x=x, shift=shift, scale=scale, gate=gate, nw=nw, qnw=qnw, kvnw=kvnw, w_g2p=w_g2p, b_g2=b_g2,
                 gnw=gnw, ret_p=ret_p, mla_p=mla_p, gla_p=gla_p, h=h, r_g=r_g, r_raw=r_raw, r_st=r_st, q=q, k=k,
                 v=v, kt=kt, o_mla=o_mla, lse=lse, g_g=g_g, g_raw=g_raw, g_st=g_st, y=y, mm=mm,
                 w_in_a=w_in_a, w_uq_a=w_uq_a, w_ukv_a=w_ukv_a, w_out_b=w_out_b)
    return x_new, saved


def _layer_bwd(dx, s, tabs, early_grads=None):
    B, S, D = dx.shape
    cr, sr, cm, sm = tabs
    d_r, d_mm, d_g, dw_out, d_gate = _out_bwd(dx, s["y"], s["gate"], s["r_g"], s["mm"], s["g_g"], s["w_out_b"])
    d_ret = _ret_bwd(s["ret_p"], cr, sr, s["r_raw"], s["r_st"], d_r)
    do, d_mz, dl = _mla_gate_bwd(d_mm, s["o_mla"], s["mla_p"])
    dq, dk, dv = _mla_attn_bwd(s["q"], s["k"], s["v"], s["kt"], do, s["lse"], dl)
    d_mla, dw_uq, dw_ukv, d_qnw, d_kvnw = _mla_prep_bwd(
        s["mla_p"], cm, sm, s["qnw"], s["kvnw"], s["w_uq_a"], s["w_ukv_a"], dq, dk, dv)
    gnw = s["gnw"] if early_grads is None else s["gnw"] + early_grads(dw_out, dw_uq, dw_ukv)
    d_gla, dw_g2p, db_g2, d_gnw = _gla_bwd(s["gla_p"], s["w_g2p"], s["b_g2"], gnw, s["g_raw"], s["g_st"], d_g)
    dx, d_shift, d_scale, d_nw = _proj_bwd_x(s["x"], s["shift"], s["scale"], s["nw"], s["w_in_a"],
                                             d_ret, d_mla, d_mz, d_gla, dx)
    dw_in = _proj_bwd_w(s["h"], d_ret, d_mla, d_mz, d_gla)
    grads = dict(
        d_mod=jnp.concatenate([d_shift, d_scale, d_gate], axis=2).reshape(B, 3 * D),
        norm_w=d_nw.reshape(D), mla_q_norm=d_qnw.reshape(256), mla_kv_norm=d_kvnw.reshape(128),
        gla_w_g2=dw_g2p[0:16], gla_b_g2=db_g2.reshape(128), gla_norm256=d_gnw.reshape(256),
        w_in_a=dw_in, w_uq_a=dw_uq, w_ukv_a=dw_ukv, w_out=dw_out)
    return dx, grads


def _exchange(arrs, gather, name):
    n = len(arrs)
    out_shape = [jax.ShapeDtypeStruct(((N_DEV,) + a.shape) if g else a.shape, a.dtype)
                 for a, g in zip(arrs, gather)]

    def body(*refs):
        ins, outs = refs[:n], refs[n:2 * n]
        send_sems, recv_sems, local_sems = refs[2 * n:]
        ix, iy, ic = lax.axis_index("x"), lax.axis_index("y"), lax.axis_index("c")
        me = 4 * ix + 2 * iy + ic
        copies = []
        for a in range(n):
            mine = ins[a] if gather[a] else ins[a].at[me]
            loc = pltpu.make_async_copy(mine, outs[a].at[me], local_sems.at[a])
            loc.start()
            copies.append(loc)
            for d in range(1, N_DEV):
                px = 1 - ix if d & 4 else ix
                py = 1 - iy if d & 2 else iy
                pc = 1 - ic if d & 1 else ic
                src = ins[a] if gather[a] else ins[a].at[4 * px + 2 * py + pc]
                cp = pltpu.make_async_remote_copy(
                    src_ref=src, dst_ref=outs[a].at[me], send_sem=send_sems.at[a, d - 1],
                    recv_sem=recv_sems.at[a, d - 1], device_id=(px, py, pc), device_id_type=pl.DeviceIdType.MESH)
                cp.start()
                copies.append(cp)
        for cp in copies:
            cp.wait()

    any_spec = pl.BlockSpec(memory_space=pl.ANY)
    outs = pl.pallas_call(
        body, name=name, in_specs=[any_spec] * n, out_specs=[any_spec] * n, out_shape=out_shape,
        scratch_shapes=[pltpu.SemaphoreType.DMA((n, N_DEV - 1)), pltpu.SemaphoreType.DMA((n, N_DEV - 1)),
                        pltpu.SemaphoreType.DMA((n,))],
    )(*arrs)
    return list(outs)


def _peers(ix, iy, ic):
    out = []
    for d in range(1, N_DEV):
        px = 1 - ix if d & 4 else ix
        py = 1 - iy if d & 2 else iy
        pc = 1 - ic if d & 1 else ic
        out.append((d - 1, (px, py, pc), 4 * px + 2 * py + pc))
    return out


def _exchange_start(arrs, gather, name):
    n = len(arrs)
    lands = [lax.empty(((N_DEV,) + a.shape) if g else a.shape, a.dtype) for a, g in zip(arrs, gather)]

    def body(*refs):
        ins, land_refs = refs[:n], refs[n:2 * n]
        send_sems, recv_sems = refs[2 * n], refs[2 * n + 1]
        token = refs[-1]
        ix, iy, ic = lax.axis_index("x"), lax.axis_index("y"), lax.axis_index("c")
        me = 4 * ix + 2 * iy + ic
        for a in range(n):
            for k, peer, peer_idx in _peers(ix, iy, ic):
                pltpu.make_async_remote_copy(
                    src_ref=ins[a] if gather[a] else ins[a].at[peer_idx], dst_ref=land_refs[a].at[me],
                    send_sem=send_sems.at[7 * a + k], recv_sem=recv_sems.at[7 * a + k], device_id=peer,
                    device_id_type=pl.DeviceIdType.MESH).start()
        token[...] = jnp.zeros_like(token)

    hbm = pl.BlockSpec(memory_space=pltpu.HBM)
    sem = pl.BlockSpec(memory_space=pltpu.SEMAPHORE)
    held = [pltpu.with_memory_space_constraint(a, pltpu.HBM) for a in list(arrs) + lands]
    outs = pl.pallas_call(
        body, name=name,
        out_shape=(pltpu.SemaphoreType.DMA((7 * n,)), pltpu.SemaphoreType.DMA((7 * n,)),
                   *[pltpu.HBM(a.shape, a.dtype) for a in held], jax.ShapeDtypeStruct((8, 128), F32)),
        in_specs=[hbm] * (2 * n), out_specs=(sem, sem, *[hbm] * (2 * n), pl.BlockSpec(memory_space=pltpu.VMEM)),
        input_output_aliases={a: 2 + a for a in range(2 * n)},
        compiler_params=pltpu.CompilerParams(has_side_effects=pltpu.SideEffectType.DATAFLOW_SIDE_EFFECTING),
    )(*held)
    return dict(send=outs[0], recv=outs[1], srcs=list(outs[2:2 + n]), lands=list(outs[2 + n:2 + 2 * n]),
                token=outs[-1], gather=list(gather))


def _exchange_wait(flight, after, me, name):
    n = len(flight["srcs"])
    gather = flight["gather"]

    def body(*refs):
        srcs, land_refs = refs[:n], refs[n:2 * n]
        send_sems, recv_sems = refs[2 * n], refs[2 * n + 1]
        ix, iy, ic = lax.axis_index("x"), lax.axis_index("y"), lax.axis_index("c")
        mine = 4 * ix + 2 * iy + ic
        for a in range(n):
            for k, peer, peer_idx in _peers(ix, iy, ic):
                cp = pltpu.make_async_remote_copy(
                    src_ref=srcs[a] if gather[a] else srcs[a].at[peer_idx], dst_ref=land_refs[a].at[mine],
                    send_sem=send_sems.at[7 * a + k], recv_sem=recv_sems.at[7 * a + k], device_id=peer,
                    device_id_type=pl.DeviceIdType.MESH)
                cp.wait_send()
                cp.wait_recv()

    hbm = pl.BlockSpec(memory_space=pltpu.HBM)
    sem = pl.BlockSpec(memory_space=pltpu.SEMAPHORE)
    held = flight["srcs"] + flight["lands"]
    outs = pl.pallas_call(
        body, name=name, out_shape=tuple(pltpu.HBM(a.shape, a.dtype) for a in held),
        in_specs=[hbm] * (2 * n) + [sem, sem, pl.BlockSpec(memory_space=pl.ANY)], out_specs=tuple([hbm] * (2 * n)),
        input_output_aliases={a: a for a in range(2 * n)},
        compiler_params=pltpu.CompilerParams(has_side_effects=pltpu.SideEffectType.DATAFLOW_SIDE_EFFECTING),
    )(*held, flight["send"], flight["recv"], after)
    got = []
    for a in range(n):
        src, land = outs[a], outs[n + a]
        own = src if gather[a] else lax.dynamic_index_in_dim(src, me, axis=0, keepdims=False)
        got.append(lax.dynamic_update_index_in_dim(land, own, me, axis=0))
    return got


def _ada_fwd(c_all, ada_w, ada_b_cols):
    nb, D = c_all.shape
    cols = ada_w.shape[2]

    def body(c_ref, w_ref, b_ref, out_ref):
        ca = _silu(c_ref[...])
        for l in range(DEPTH):
            out_ref[l] = _mm(ca, w_ref[l]) + b_ref[l:l + 1, :]

    return pl.pallas_call(
        body, name="ada_fwd", out_shape=jax.ShapeDtypeStruct((DEPTH, nb, cols), F32),
        in_specs=[pl.BlockSpec(memory_space=pltpu.VMEM)] * 3, out_specs=pl.BlockSpec(memory_space=pltpu.VMEM),
        compiler_params=pltpu.CompilerParams(vmem_limit_bytes=32 * VMEM_MB),
    )(c_all, ada_w, ada_b_cols)


def _ada_bwd(c_all, d_mod_cols):
    nb, D = c_all.shape
    cols = d_mod_cols.shape[2]

    def body(c_ref, dm_ref, out_ref):
        ca = _silu(c_ref[...])
        for l in range(DEPTH):
            out_ref[l] = _mm_tn(ca, dm_ref[l])

    return pl.pallas_call(
        body, name="ada_bwd", out_shape=jax.ShapeDtypeStruct((DEPTH, D, cols), F32),
        in_specs=[pl.BlockSpec(memory_space=pltpu.VMEM)] * 2, out_specs=pl.BlockSpec(memory_space=pltpu.VMEM),
        compiler_params=pltpu.CompilerParams(vmem_limit_bytes=32 * VMEM_MB),
    )(c_all, d_mod_cols)


def _sum_adamw(parts, w, m, v, name):
    P, R, C = parts.shape
    tr = 256 if (R % 256 == 0 and R > 256) else R

    def body(p_ref, w_ref, m_ref, v_ref, g_ref, d_ref, nm_ref, nv_ref):
        g = p_ref[0].astype(F32)
        for k in range(1, P):
            g = g + p_ref[k].astype(F32)
        g_ref[...] = g
        nm = ADAM_B1 * m_ref[...] + (1.0 - ADAM_B1) * g
        nv = ADAM_B2 * v_ref[...] + (1.0 - ADAM_B2) * (g * g)
        nm_ref[...] = nm
        nv_ref[...] = nv
        m_hat = nm / (1.0 - ADAM_B1 ** ADAM_STEP)
        v_hat = nv / (1.0 - ADAM_B2 ** ADAM_STEP)
        d_ref[...] = -ADAM_LR * (m_hat / (jnp.sqrt(v_hat) + ADAM_EPS) + ADAM_WD * w_ref[...])

    blk = pl.BlockSpec((tr, C), lambda i: (i, 0))
    shp = jax.ShapeDtypeStruct((R, C), F32)
    return pl.pallas_call(
        body, name=name, grid=(R // tr,),
        in_specs=[pl.BlockSpec((P, tr, C), lambda i: (0, i, 0)), blk, blk, blk],
        out_specs=[blk, blk, blk, blk], out_shape=[shp, shp, shp, shp],
        compiler_params=_cp(("parallel",)),
    )(parts, w, m, v)


def _sum_adamw_layer(parts, w, m, v, layer, name, prev=None):
    P, R, C = parts.shape
    tr = 256 if (R % 256 == 0 and R > 256) else R

    def body(p_ref, w_ref, m_ref, v_ref, *rest):
        g_ref, d_ref, nm_ref, nv_ref = rest[-4:]
        g = p_ref[0].astype(F32)
        for k in range(1, P):
            g = g + p_ref[k].astype(F32)
        g_ref[0] = g
        nm = ADAM_B1 * m_ref[0] + (1.0 - ADAM_B1) * g
        nv = ADAM_B2 * v_ref[0] + (1.0 - ADAM_B2) * (g * g)
        nm_ref[0] = nm
        nv_ref[0] = nv
        m_hat = nm / (1.0 - ADAM_B1 ** ADAM_STEP)
        v_hat = nv / (1.0 - ADAM_B2 ** ADAM_STEP)
        d_ref[0] = -ADAM_LR * (m_hat / (jnp.sqrt(v_hat) + ADAM_EPS) + ADAM_WD * w_ref[0])

    blk = pl.BlockSpec((1, tr, C), lambda i: (layer, i, 0))
    shp = jax.ShapeDtypeStruct(w.shape, F32)
    in_specs = [pl.BlockSpec((P, tr, C), lambda i: (0, i, 0)), blk, blk, blk]
    args = [parts, w, m, v]
    aliases = {}
    if prev is not None:
        in_specs += [pl.BlockSpec(memory_space=pl.ANY)] * 4
        args += list(prev)
        aliases = {4 + k: k for k in range(4)}
    return list(pl.pallas_call(
        body, name=name, grid=(R // tr,), in_specs=in_specs, out_specs=[blk] * 4, out_shape=[shp] * 4,
        input_output_aliases=aliases, compiler_params=_cp(("parallel",)),
    )(*args))


SMALL = ["norm_w", "mla_q_norm", "mla_kv_norm", "gla_w_g2", "gla_b_g2", "gla_norm", "final_norm"]


SMALL_ROWS = 72


def _pack_small(loss, part):
    flat = [jnp.pad(loss.reshape(1), (0, 127))] + [part[n].reshape(-1) for n in SMALL]
    used = sum(f.shape[0] for f in flat)
    flat.append(jnp.zeros((SMALL_ROWS * 128 - used,), F32))
    return jnp.concatenate(flat).reshape(SMALL_ROWS, 128)


def _small_adamw(packed_parts, w, m, v):
    n = len(w)

    def body(*refs):
        p_ref = refs[0]
        w_refs, m_refs, v_refs = refs[1:1 + n], refs[1 + n:1 + 2 * n], refs[1 + 2 * n:1 + 3 * n]
        outs, acc = refs[1 + 3 * n:-1], refs[-1]
        total = p_ref[0]
        for k in range(1, N_DEV):
            total = total + p_ref[k]
        acc[...] = total
        outs[0][...] = acc[0:1, :]
        r0 = 1
        for i in range(n):
            shp = w_refs[i].shape
            if len(shp) == 3:
                g = acc[r0:r0 + shp[0] * shp[1], :].reshape(shp)
                r0 += shp[0] * shp[1]
            elif shp[1] < 128:
                g = acc[r0:r0 + shp[0], 0:shp[1]]
                r0 += shp[0]
            else:
                k = shp[1] // 128
                g = jnp.concatenate(
                    [jnp.concatenate([acc[r0 + l * k + j:r0 + l * k + j + 1, :] for j in range(k)], axis=1)
                     for l in range(shp[0])], axis=0)
                r0 += shp[0] * k
            nm = ADAM_B1 * m_refs[i][...] + (1.0 - ADAM_B1) * g
            nv = ADAM_B2 * v_refs[i][...] + (1.0 - ADAM_B2) * (g * g)
            m_hat = nm / (1.0 - ADAM_B1 ** ADAM_STEP)
            v_hat = nv / (1.0 - ADAM_B2 ** ADAM_STEP)
            outs[1 + 4 * i][...] = g
            outs[2 + 4 * i][...] = -ADAM_LR * (m_hat / (jnp.sqrt(v_hat) + ADAM_EPS) + ADAM_WD * w_refs[i][...])
            outs[3 + 4 * i][...] = nm
            outs[4 + 4 * i][...] = nv

    vmem = pl.BlockSpec(memory_space=pltpu.VMEM)
    out_shape = [jax.ShapeDtypeStruct((1, 128), F32)]
    for a in w:
        out_shape += [jax.ShapeDtypeStruct(a.shape, F32)] * 4
    outs = pl.pallas_call(
        body, name="adamw_small", in_specs=[vmem] * (1 + 3 * n), out_specs=[vmem] * (1 + 4 * n), out_shape=out_shape,
        scratch_shapes=[pltpu.VMEM((SMALL_ROWS, 128), F32)],
    )(packed_parts, *w, *m, *v)
    return outs[0], [outs[1 + 4 * i:5 + 4 * i] for i in range(n)]


WEIGHTS = ["norm_w", "ada_w", "ada_b", "w_in", "mla_q_norm", "w_uq", "mla_kv_norm", "w_ukv", "gla_w_g2",
           "gla_b_g2", "gla_norm", "w_out", "final_norm"]


def kernel(x, c, positions, norm_w, ada_w, ada_b, w_in, mla_q_norm, w_uq, mla_kv_norm, w_ukv, gla_w_g2, gla_b_g2, gla_norm, w_out, final_norm, loss_target, m_norm_w, m_ada_w, m_ada_b, m_w_in, m_mla_q_norm, m_w_uq, m_mla_kv_norm, m_w_ukv, m_gla_w_g2, m_gla_b_g2, m_gla_norm, m_w_out, m_final_norm, v_norm_w, v_ada_w, v_ada_b, v_w_in, v_mla_q_norm, v_w_uq, v_mla_kv_norm, v_w_ukv, v_gla_w_g2, v_gla_b_g2, v_gla_norm, v_w_out, v_final_norm):
    w = dict(norm_w=norm_w, ada_w=ada_w, ada_b=ada_b, w_in=w_in, mla_q_norm=mla_q_norm, w_uq=w_uq,
             mla_kv_norm=mla_kv_norm, w_ukv=w_ukv, gla_w_g2=gla_w_g2, gla_b_g2=gla_b_g2, gla_norm=gla_norm,
             w_out=w_out, final_norm=final_norm)
    m = dict(norm_w=m_norm_w, ada_w=m_ada_w, ada_b=m_ada_b, w_in=m_w_in, mla_q_norm=m_mla_q_norm, w_uq=m_w_uq,
             mla_kv_norm=m_mla_kv_norm, w_ukv=m_w_ukv, gla_w_g2=m_gla_w_g2, gla_b_g2=m_gla_b_g2,
             gla_norm=m_gla_norm, w_out=m_w_out, final_norm=m_final_norm)
    v = dict(norm_w=v_norm_w, ada_w=v_ada_w, ada_b=v_ada_b, w_in=v_w_in, mla_q_norm=v_mla_q_norm, w_uq=v_w_uq,
             mla_kv_norm=v_mla_kv_norm, w_ukv=v_w_ukv, gla_w_g2=v_gla_w_g2, gla_b_g2=v_gla_b_g2,
             gla_norm=v_gla_norm, w_out=v_w_out, final_norm=v_final_norm)
    B, S, D = x.shape
    me = 4 * lax.axis_index("x") + 2 * lax.axis_index("y") + lax.axis_index("c")
    ada_cols = ada_w.shape[2]
    cast = lambda a: a.astype(_MXU)

    sharded = ["w_in", "w_uq", "w_ukv", "w_out"]

    whole_cols = lambda a: jnp.transpose(a, (1, 0, 2)).reshape(a.shape[1], -1)
    whole_in = lambda blk: _arrange_w_in(whole_cols(blk))
    whole_rest = lambda blks: (_arrange_w_uq(whole_cols(blks[0])), _arrange_w_ukv(whole_cols(blks[1])),
                               blks[2].reshape(D, D))
    col_blocks = lambda a: jnp.transpose(a.reshape(a.shape[0], N_DEV, -1), (1, 0, 2)).astype(jnp.bfloat16)
    blocks_in = lambda dw_in_a: col_blocks(_unarrange_w_in(dw_in_a))
    blocks_rest = lambda dw_out, dw_uq_a, dw_ukv_a: [
        col_blocks(_unarrange_w_uq(dw_uq_a)), col_blocks(_unarrange_w_ukv(dw_ukv_a)),
        dw_out.reshape(N_DEV, D // N_DEV, D).astype(jnp.bfloat16)]

    (c_g,) = _exchange([c], [True], "gather_c")
    flight_i = _exchange_start([cast(w_in[0])], [True], "gather_start_first")
    flight_r = _exchange_start([cast(w[n][0]) for n in sharded[1:]], [True] * 3, "gather_start_layer0")
    flight_w = _exchange_start([cast(w[n][1]) for n in sharded], [True] * 4, "gather_start_layer1")
    started = flight_i["token"][0, 0] + flight_r["token"][0, 0] + flight_w["token"][0, 0]
    c_all = c_g.reshape(N_DEV * B, D) + started

    ada_b_cols = lax.dynamic_slice(ada_b, (0, me * ada_cols), (DEPTH, ada_cols))
    mod_cols = _ada_fwd(c_all, ada_w, ada_b_cols)
    mod_send = jnp.transpose(mod_cols.reshape(DEPTH, N_DEV, B, ada_cols), (1, 0, 2, 3))
    (mod_recv,) = _exchange([mod_send], [False], "scatter_mod")
    mod = jnp.transpose(mod_recv, (1, 2, 0, 3)).reshape(DEPTH, B, 3 * D)

    small_w = {n: w[n] for n in SMALL}
    layer_small = lambda l: {n: a[l] for n, a in small_w.items() if n != "final_norm"}
    tabs = _rope_tables(positions.reshape(B, S, 1))
    late0 = lambda after: whole_rest(_exchange_wait(flight_r, after, me, "gather_wait_layer0"))
    (w_in0_g,) = _exchange_wait(flight_i, tabs[0], me, "gather_wait_first")
    x1, saved0 = _layer_fwd(x, tabs, mod[0], layer_small(0), whole_in(w_in0_g), late_weights=late0)
    got1 = _exchange_wait(flight_w, x1, me, "gather_wait_layer1")
    x2, saved1 = _layer_fwd(x1, tabs, mod[1], layer_small(1), whole_in(got1[0]), *whole_rest(got1[1:]))
    dx, loss, d_fw = _final_loss(x2, final_norm.reshape(1, D), loss_target)

    dx, g1 = _layer_bwd(dx, saved1, tabs)
    flight_g = _exchange_start([blocks_in(g1["w_in_a"])] + blocks_rest(g1["w_out"], g1["w_uq_a"], g1["w_ukv_a"]),
                               [False] * 4, "grads_start_layer1")
    flights = {}

    def early0(dw_out, dw_uq_a, dw_ukv_a):
        flights["rest0"] = _exchange_start(blocks_rest(dw_out, dw_uq_a, dw_ukv_a), [False] * 3, "grads_start_layer0")
        return flights["rest0"]["token"][0, 0]

    saved0 = dict(saved0, gate=saved0["gate"] + flight_g["token"][0, 0])
    grad_x, g0 = _layer_bwd(dx, saved0, tabs, early_grads=early0)
    parts1 = _exchange_wait(flight_g, grad_x, me, "grads_wait_layer1")
    rest0 = _exchange_wait(flights["rest0"], g0["w_in_a"], me, "grads_wait_layer0")

    both = lambda n: jnp.stack([g0[n], g1[n]])
    d_mod = both("d_mod")
    part = dict(norm_w=both("norm_w"), mla_q_norm=both("mla_q_norm"), mla_kv_norm=both("mla_kv_norm"),
                gla_w_g2=both("gla_w_g2"), gla_b_g2=both("gla_b_g2"), gla_norm=both("gla_norm256")[:, 0:128],
                final_norm=d_fw)
    flight_l = _exchange_start([d_mod, _pack_small(loss, part), blocks_in(g0["w_in_a"])], [True, True, False],
                               "exchange_start_last")
    res = {}
    for a, name in enumerate(sharded):
        res[name] = _sum_adamw_layer(parts1[a], w[name], m[name], v[name], 1, "adamw_%s_layer1" % name)
    for a, name in enumerate(sharded[1:]):
        res[name] = _sum_adamw_layer(rest0[a], w[name], m[name], v[name], 0, "adamw_%s_layer0" % name, prev=res[name])
    d_mod_g, small_g, in0 = _exchange_wait(flight_l, res["w_out"][0], me, "exchange_wait_last")
    res["w_in"] = _sum_adamw_layer(in0, w_in, m_w_in, v_w_in, 0, "adamw_w_in_layer0", prev=res["w_in"])

    d_mod_all = jnp.transpose(d_mod_g, (1, 0, 2, 3)).reshape(DEPTH, N_DEV * B, 3 * D)
    d_mod_cols = lax.dynamic_slice(d_mod_all, (0, 0, me * ada_cols), (DEPTH, N_DEV * B, ada_cols))
    g_ada_w = _ada_bwd(c_all, d_mod_cols)

    def update(name, parts2d):
        shp = w[name].shape
        two = lambda a: a.reshape(parts2d.shape[1:])
        out = _sum_adamw(parts2d, two(w[name]), two(m[name]), two(v[name]), "adamw_" + name)
        res[name] = [o.reshape(shp) for o in out]

    update("ada_w", g_ada_w.reshape(1, DEPTH * D, ada_cols))
    update("ada_b", jnp.transpose(d_mod_g, (0, 2, 1, 3)).reshape(N_DEV * B, DEPTH * 3 * D // 128, 128))
    row = lambda a: a.reshape(1, D) if a.ndim == 1 else a
    loss_sum, small_out = _small_adamw(small_g, [row(w[n]) for n in SMALL], [row(m[n]) for n in SMALL],
                                       [row(v[n]) for n in SMALL])
    for n, outs in zip(SMALL, small_out):
        res[n] = [o.reshape(w[n].shape) for o in outs]
    loss_out = loss_sum[0, 0]
    return (loss_out, grad_x, *[res[n][0] for n in WEIGHTS], *[res[n][1] for n in WEIGHTS],
            *[res[n][2] for n in WEIGHTS], *[res[n][3] for n in WEIGHTS])
```

```python
import functools
import math

import numpy as np
import jax
import jax.numpy as jnp
from jax import lax
from jax.experimental import pallas as pl
from jax.experimental.pallas import tpu as pltpu

F32 = jnp.float32
_MXU = jnp.bfloat16

D_MODEL = 1024
DEPTH = 2
CHUNK = 64
EPS = 1e-6
ROPE_THETA = 10000.0
N_DEV = 8

MLA_SCALE = 96.0 ** -0.5
RET_KSCALE = 64.0 ** -0.5
GLA_KSCALE = 32.0 ** -0.5
GLA_TAU = 16.0

ADAM_LR = 0.001
ADAM_B1 = 0.9
ADAM_B2 = 0.999
ADAM_EPS = 1e-08
ADAM_WD = 0.01
ADAM_STEP = 10

RET_W, MLA_W, GLA_W = 1024, 1024, 896
ARR_W = RET_W + MLA_W + GLA_W
VMEM_MB = 1024 * 1024


def _cp(sem, vmem_mb=48):
    return pltpu.CompilerParams(dimension_semantics=sem, vmem_limit_bytes=vmem_mb * VMEM_MB)


def _mm(a, b):
    return jnp.dot(a.astype(_MXU), b.astype(_MXU), preferred_element_type=F32)


def _mm_nt(a, b):
    return lax.dot_general(a.astype(_MXU), b.astype(_MXU), (((1,), (1,)), ((), ())),
                           preferred_element_type=F32)


def _mm_tn(a, b):
    return lax.dot_general(a.astype(_MXU), b.astype(_MXU), (((0,), (0,)), ((), ())),
                           preferred_element_type=F32)


def _mm_f32(a, b):
    return jnp.dot(a, b, precision=lax.Precision.HIGHEST, preferred_element_type=F32)


def _sig(z):
    return 1.0 / (1.0 + jnp.exp(-z))


def _silu(z):
    return z * _sig(z)


def _dsilu(z):
    s = _sig(z)
    return s * (1.0 + z * (1.0 - s))


def _full(shape):
    nd = len(shape)
    return pl.BlockSpec(shape, lambda *_: (0,) * nd)


def _qk_perm(blk):
    r = blk.shape[0]
    return jnp.transpose(blk.reshape(r, 4, 2, 32), (0, 2, 1, 3)).reshape(r, 256)


def _qk_unperm(blk):
    r = blk.shape[0]
    return jnp.transpose(blk.reshape(r, 2, 4, 32), (0, 2, 1, 3)).reshape(r, 256)


def _arrange_w_in(w):
    z = lambda n: jnp.zeros((w.shape[0], n), w.dtype)
    ret = [_qk_perm(w[:, 0:256]), _qk_perm(w[:, 256:512]), w[:, 512:768], w[:, 768:1024]]
    mla = [w[:, 1024:1280], w[:, 1280:1408], z(64), w[:, 1408:1440], z(32), w[:, 1440:1952]]
    gla = [w[:, 1952:2080], w[:, 2080:2208], w[:, 2208:2464], w[:, 2464:2480], z(112), w[:, 2480:2736]]
    return jnp.concatenate(ret + mla + gla, axis=1)


def _unarrange_w_in(a):
    m, g = RET_W, RET_W + MLA_W
    parts = [_qk_unperm(a[:, 0:256]), _qk_unperm(a[:, 256:512]), a[:, 512:1024],
             a[:, m:m + 384], a[:, m + 448:m + 480], a[:, m + 512:m + 1024],
             a[:, g:g + 528], a[:, g + 640:g + 896]]
    return jnp.concatenate(parts, axis=1)


def _arrange_w_uq(w):
    return jnp.pad(w.reshape(256, 8, 96), ((0, 0), (0, 0), (0, 32))).reshape(256, 1024)


def _unarrange_w_uq(a):
    return a.reshape(256, 8, 128)[:, :, :96].reshape(256, 768)


def _arrange_w_ukv(w):
    r = w.reshape(128, 8, 128)
    k = jnp.pad(r[:, :, :64], ((0, 0), (0, 0), (0, 64))).reshape(128, 1024)
    return jnp.concatenate([k, r[:, :, 64:].reshape(128, 512)], axis=1)


def _unarrange_w_ukv(a):
    k = a[:, :1024].reshape(128, 8, 128)[:, :, :64]
    v = a[:, 1024:].reshape(128, 8, 64)
    return jnp.concatenate([k, v], axis=2).reshape(128, 1024)


def _rope_tables(pos3, zero=0.0):
    B, S, _ = pos3.shape
    ts = min(S, 512)
    inv32 = (np.float32(ROPE_THETA) ** (-(np.arange(32, dtype=np.float32) / 32))).astype(np.float32)
    inv16 = (np.float32(ROPE_THETA) ** (-(np.arange(16, dtype=np.float32) / 16))).astype(np.float32)
    inv = np.zeros((1, 128), np.float32)
    inv[0, 0:32] = inv32
    inv[0, 32:48] = inv16

    def body(pos_ref, inv_ref, cr, sr, cm, sm):
        ang = pos_ref[0].astype(F32) * inv_ref[...]
        lane = lax.broadcasted_iota(jnp.int32, (1, 128), 1)

        def every_head(x):
            y = jnp.where(lane < 32, x, pltpu.roll(x, 32, 1))
            return jnp.where(lane < 64, y, pltpu.roll(y, 64, 1))

        def rotary_pair(x, fill):
            return jnp.where((lane >= 64) & (lane < 80), pltpu.roll(x, 32, 1),
                             jnp.where((lane >= 80) & (lane < 96), pltpu.roll(x, 48, 1), fill))

        c, s = jnp.cos(ang), jnp.sin(ang)
        cr[0] = every_head(c)
        sr[0] = every_head(s)
        cm[0] = rotary_pair(c, 1.0)
        sm[0] = rotary_pair(s, 0.0)

    tab = jax.ShapeDtypeStruct((B, S, 128), F32)
    blk = pl.BlockSpec((1, ts, 128), lambda b, i: (b, i, 0))
    return pl.pallas_call(
        body, name="rope_tables", grid=(B, S // ts),
        in_specs=[pl.BlockSpec((1, ts, 1), lambda b, i: (b, i, 0)), _full((1, 128))],
        out_specs=[blk, blk, blk, blk], out_shape=[tab, tab, tab, tab],
        compiler_params=_cp(("parallel", "parallel")),
    )(pos3, jnp.asarray(inv) + zero)


def _rope128(x, cos, sin):
    lane = lax.broadcasted_iota(jnp.int32, (1, 128), 1)
    rp = pltpu.roll(x, 16, 1)
    rm = pltpu.roll(x, 112, 1)
    return x * cos + jnp.where(lane < 80, -rm, rp) * sin


def _rope128_t(d, cos, sin):
    lane = lax.broadcasted_iota(jnp.int32, (1, 128), 1)
    y = d * sin
    yp = pltpu.roll(y, 16, 1)
    ym = pltpu.roll(y, 112, 1)
    return d * cos + jnp.where(lane < 64, 0.0, jnp.where(lane < 80, ym, jnp.where(lane < 96, -yp, 0.0)))


def _proj_fwd(x, shift, scale, nw, w_arr):
    B, S, D = x.shape
    tm = min(S, 512)

    def body(x_ref, sh_ref, sc_ref, nw_ref, w_ref, ret_ref, mla_ref, gla_ref, h_ref):
        xv = x_ref[0]
        rstd = lax.rsqrt(jnp.mean(xv * xv, axis=-1, keepdims=True) + EPS)
        h = (xv * rstd * nw_ref[...]) * (1.0 + sc_ref[0]) + sh_ref[0]
        hb = h.astype(_MXU)
        h_ref[0] = hb
        ret_ref[0] = jnp.dot(hb, w_ref[:, 0:RET_W], preferred_element_type=F32).astype(_MXU)
        mla_ref[0] = jnp.dot(hb, w_ref[:, RET_W:RET_W + MLA_W], preferred_element_type=F32).astype(_MXU)
        gla_ref[0] = jnp.dot(hb, w_ref[:, RET_W + MLA_W:ARR_W], preferred_element_type=F32).astype(_MXU)

    tok = lambda w: pl.BlockSpec((1, tm, w), lambda b, i: (b, i, 0))
    per_seq = pl.BlockSpec((1, 1, D), lambda b, i: (b, 0, 0))
    return pl.pallas_call(
        body, name="proj_fwd", grid=(B, S // tm),
        in_specs=[tok(D), per_seq, per_seq, _full((1, D)), _full((D, ARR_W))],
        out_specs=[tok(RET_W), tok(MLA_W), tok(GLA_W), tok(D)],
        out_shape=[jax.ShapeDtypeStruct((B, S, RET_W), _MXU), jax.ShapeDtypeStruct((B, S, MLA_W), _MXU),
                   jax.ShapeDtypeStruct((B, S, GLA_W), _MXU), jax.ShapeDtypeStruct((B, S, D), _MXU)],
        compiler_params=_cp(("parallel", "parallel")),
    )(x, shift, scale, nw, w_arr)


RET_L = 256


def _ret_consts(L):
    lg = np.log1p(-np.exp2(-5.0 - np.arange(4, dtype=np.float32))).astype(np.float32)
    i = np.arange(L)
    ci = i // CHUNK
    diff = (i[:, None] - i[None, :]).astype(np.float32)
    same = ci[:, None] == ci[None, :]
    past = ci[None, :] < ci[:, None]
    expo = np.where(same, np.abs(diff), np.where(past, diff, 0.0)).astype(np.float32)
    dec = np.where((same | past)[None], np.exp(lg[:, None, None] * expo[None]), 0.0).astype(np.float32)
    head = (np.arange(256) % 128) // 32
    qw = np.exp((i + 1.0)[:, None] * lg[head][None, :]).astype(np.float32)
    kw = np.exp((L - 1.0 - i)[:, None] * lg[head][None, :]).astype(np.float32)
    a_row = np.exp(np.float32(L) * lg[head])[None, :].astype(np.float32)
    return [jnp.asarray(t) for t in (dec.reshape(4 * L, L), qw, kw, a_row)]


def _ret_masks():
    lane = lax.broadcasted_iota(jnp.int32, (1, 256), 1)
    mh = [((lane % 128) // 32) == h for h in range(4)]
    mv = [(lane // 64) == h for h in range(4)]
    vi = lax.broadcasted_iota(jnp.int32, (256, 256), 0)
    ki = lax.broadcasted_iota(jnp.int32, (256, 256), 1)
    bd = (vi // 64) == ((ki % 128) // 32)
    return mh, mv, bd


def _ret_rope(p, cs, sn):
    q1, q2, k1, k2 = p[:, 0:128], p[:, 128:256], p[:, 256:384], p[:, 384:512]
    qr = jnp.concatenate([q1 * cs - q2 * sn, q2 * cs + q1 * sn], axis=1)
    kr = jnp.concatenate([k1 * cs - k2 * sn, k2 * cs + k1 * sn], axis=1) * RET_KSCALE
    return qr, kr


def _head_mean(x, mv, width):
    out = jnp.zeros_like(x)
    for m in mv:
        s = jnp.sum(jnp.where(m, x, 0.0), axis=-1, keepdims=True) * (1.0 / width)
        out = jnp.where(m, s, out)
    return out


def _stack_heads(x, masks):
    return jnp.concatenate([jnp.where(m, x, 0.0) for m in masks], axis=0)


def _fold_heads(xs, masks, L):
    out = jnp.where(masks[0], xs[0:L], 0.0)
    for h in range(1, 4):
        out = out + jnp.where(masks[h], xs[h * L:(h + 1) * L], 0.0)
    return out


RET_G = 2


def _ret_fwd(ret_p, cos, sin):
    B, S, _ = ret_p.shape
    L = min(RET_L, S)
    NB = S // L
    G = min(RET_G, NB)
    NG = NB // G
    consts = _ret_consts(L)

    def body(p_ref, c_ref, s_ref, ds_ref, qw_ref, kw_ref, a_ref, out_ref, raw_ref, st_ref, st_sc):
        @pl.when(pl.program_id(1) == 0)
        def _():
            st_sc[...] = jnp.zeros_like(st_sc)

        mh, mv, bd = _ret_masks()
        cs_ = range(G)
        rows = [slice(c * L, (c + 1) * L) for c in cs_]
        ps = [p_ref[0, rows[c], :].astype(F32) for c in cs_]
        qk = [_ret_rope(ps[c], c_ref[0, rows[c], :], s_ref[0, rows[c], :]) for c in cs_]
        vs = [ps[c][:, 512:768] for c in cs_]
        a_s = [_mm_nt(_stack_heads(qk[c][0], mh), qk[c][1]) for c in cs_]
        upd = [_mm_tn(vs[c], qk[c][1] * kw_ref[...]) for c in cs_]
        o_s = [_mm(a_s[c] * ds_ref[...], vs[c]) for c in cs_]
        st = st_sc[...]
        inter = []
        for c in cs_:
            st_ref[0, c] = st
            inter.append(_mm_nt(qk[c][0] * qw_ref[...], st))
            st = st * a_ref[...] + jnp.where(bd, upd[c], 0.0)
        st_sc[...] = st
        for c in cs_:
            r = _fold_heads(o_s[c], mv, L) + inter[c]
            raw_ref[0, rows[c], :] = r
            rstd = lax.rsqrt(_head_mean(r * r, mv, 64.0) + EPS)
            out_ref[0, rows[c], :] = (r * rstd * _silu(ps[c][:, 768:1024])).astype(_MXU)

    tok = lambda w: pl.BlockSpec((1, G * L, w), lambda b, n: (b, n, 0))
    return pl.pallas_call(
        body, name="ret_fwd", grid=(B, NG),
        in_specs=[tok(RET_W), tok(128), tok(128), _full((4 * L, L)), _full((L, 256)), _full((L, 256)),
                  _full((1, 256))],
        out_specs=[tok(256), tok(256), pl.BlockSpec((1, G, 256, 256), lambda b, n: (b, n, 0, 0))],
        out_shape=[jax.ShapeDtypeStruct((B, S, 256), _MXU), jax.ShapeDtypeStruct((B, S, 256), F32),
                   jax.ShapeDtypeStruct((B, NB, 256, 256), F32)],
        scratch_shapes=[pltpu.VMEM((256, 256), F32)],
        compiler_params=_cp(("parallel", "arbitrary")),
    )(ret_p, cos, sin, *consts)


def _ret_bwd(ret_p, cos, sin, raw, states, d_mix):
    B, S, _ = ret_p.shape
    L = min(RET_L, S)
    NB = S // L
    G = 1
    NG = NB // G
    consts = _ret_consts(L)

    def body(p_ref, c_ref, s_ref, raw_ref, st_ref, dm_ref, ds_ref, qw_ref, kw_ref, a_ref, dp_ref, dst_sc):
        @pl.when(pl.program_id(1) == 0)
        def _():
            dst_sc[...] = jnp.zeros_like(dst_sc)

        mh, mv, bd = _ret_masks()
        qw, kw, dec = qw_ref[...], kw_ref[...], ds_ref[...]
        cs_ = range(G)
        rows = [slice(c * L, (c + 1) * L) for c in cs_]
        ps = [p_ref[0, rows[c], :].astype(F32) for c in cs_]
        tabs = [(c_ref[0, rows[c], :], s_ref[0, rows[c], :]) for c in cs_]
        qk = [_ret_rope(ps[c], *tabs[c]) for c in cs_]
        vs = [ps[c][:, 512:768] for c in cs_]
        qs = [_stack_heads(qk[c][0], mh) for c in cs_]
        a_s = [_mm_nt(qs[c], qk[c][1]) for c in cs_]
        dr, dz = [], []
        for c in cs_:
            r = raw_ref[0, rows[c], :]
            z = ps[c][:, 768:1024]
            rstd = lax.rsqrt(_head_mean(r * r, mv, 64.0) + EPS)
            rn = r * rstd
            dm = dm_ref[0, rows[c], :]
            d_rn = dm * _silu(z)
            dz.append(dm * rn * _dsilu(z))
            dr.append(rstd * (d_rn - rn * _head_mean(d_rn * rn, mv, 64.0)))
        do_s = [_stack_heads(dr[c], mv) for c in cs_]
        da_s = [_mm_nt(do_s[c], vs[c]) for c in cs_]
        sts = [st_ref[0, c] for c in cs_]
        dq_st = [_mm(dr[c], sts[c]) for c in cs_]
        dst_in = [_mm_tn(dr[c], qk[c][0] * qw) for c in cs_]
        dv = [_mm_tn(a_s[c] * dec, do_s[c]) for c in cs_]
        dqr, dkr = [], []
        for c in cs_:
            da = da_s[c] * dec
            dqr.append(_fold_heads(_mm(da, qk[c][1]), mh, L) + dq_st[c] * qw)
            dkr.append(_mm_tn(da, qs[c]))
        dst_next = dst_sc[...]
        for c in reversed(cs_):
            g = jnp.where(bd, dst_next, 0.0)
            dv[c] = dv[c] + _mm_nt(qk[c][1] * kw, g)
            dkr[c] = dkr[c] + _mm(vs[c], g) * kw
            dst_next = dst_next * a_ref[...] + jnp.where(bd, dst_in[c], 0.0)
        dst_sc[...] = dst_next
        for c in cs_:
            cs, sn = tabs[c]
            dk = dkr[c] * RET_KSCALE
            dq1, dq2 = dqr[c][:, 0:128], dqr[c][:, 128:256]
            dk1, dk2 = dk[:, 0:128], dk[:, 128:256]
            dp_ref[0, rows[c], :] = jnp.concatenate(
                [dq1 * cs + dq2 * sn, dq2 * cs - dq1 * sn, dk1 * cs + dk2 * sn, dk2 * cs - dk1 * sn, dv[c], dz[c]],
                axis=1).astype(_MXU)

    tok = lambda w: pl.BlockSpec((1, G * L, w), lambda b, i: (b, NG - 1 - i, 0))
    return pl.pallas_call(
        body, name="ret_bwd", grid=(B, NG),
        in_specs=[tok(RET_W), tok(128), tok(128), tok(256),
                  pl.BlockSpec((1, G, 256, 256), lambda b, i: (b, NG - 1 - i, 0, 0)), tok(256),
                  _full((4 * L, L)), _full((L, 256)), _full((L, 256)), _full((1, 256))],
        out_specs=tok(RET_W), out_shape=jax.ShapeDtypeStruct((B, S, RET_W), _MXU),
        scratch_shapes=[pltpu.VMEM((256, 256), F32)],
        compiler_params=_cp(("parallel", "arbitrary")),
    )(ret_p, cos, sin, raw, states, d_mix, *consts)


def _gla_masks():
    C = CHUNK
    lk = lax.broadcasted_iota(jnp.int32, (1, 128), 1)
    lv = lax.broadcasted_iota(jnp.int32, (1, 256), 1)
    mk = [(lk // 32) == h for h in range(4)]
    mv = [(lv // 64) == h for h in range(4)]
    vi = lax.broadcasted_iota(jnp.int32, (256, 128), 0)
    ki = lax.broadcasted_iota(jnp.int32, (256, 128), 1)
    bd = (vi // 64) == (ki // 32)
    ri = lax.broadcasted_iota(jnp.int32, (4 * C, C), 0) % C
    cj = lax.broadcasted_iota(jnp.int32, (4 * C, C), 1)
    lower = ri >= cj
    ti = lax.broadcasted_iota(jnp.int32, (C, C), 0)
    tj = lax.broadcasted_iota(jnp.int32, (C, C), 1)
    ltri = jnp.where(ti >= tj, 1.0, 0.0).astype(F32)
    utri = jnp.where(ti <= tj, 1.0, 0.0).astype(F32)
    return mk, mv, bd, lower, ltri, utri


def _log_sigmoid(x):
    return jnp.minimum(x, 0.0) - jnp.log(1.0 + jnp.exp(-jnp.abs(x)))


GLA_G = 8


def _gla_fwd(gla_p, w_g2p, b_g2, gnw):
    B, S, _ = gla_p.shape
    C = CHUNK
    NC = S // C
    G = min(GLA_G, NC)
    NG = NC // G

    def body(p_ref, w_ref, b_ref, gn_ref, out_ref, raw_ref, st_ref, st_sc):
        @pl.when(pl.program_id(1) == 0)
        def _():
            st_sc[...] = jnp.zeros_like(st_sc)

        mk, mv, bd, lower, ltri, _ = _gla_masks()
        cs = range(G)
        rows = [slice(c * C, (c + 1) * C) for c in cs]
        ps = [p_ref[0, rows[c], :].astype(F32) for c in cs]
        pre = [_mm(ps[c][:, 512:640], w_ref[...]) + b_ref[...] for c in cs]
        cum = [_mm_f32(ltri, _log_sigmoid(pre[c]) * (1.0 / GLA_TAU)) for c in cs]
        past, fut, upd, q_pos, a_row = [], [], [], [], []
        for c in cs:
            q = ps[c][:, 0:128]
            k = ps[c][:, 128:256] * GLA_KSCALE
            last = cum[c][C - 1:C, :]
            e_pos = jnp.exp(cum[c])
            e_neg = jnp.exp(-cum[c])
            q_pos.append(q * e_pos)
            a_row.append(jnp.exp(last))
            past.append(_mm_nt(_stack_heads(q_pos[c], mk), k * e_neg))
            fut.append(_mm_nt(_stack_heads(q * e_neg, mk), k * e_pos))
            upd.append(_mm_tn(ps[c][:, 256:512], k * jnp.exp(last - cum[c])))
        o_s = [_mm(jnp.where(lower, past[c], fut[c]), ps[c][:, 256:512]) for c in cs]
        st = st_sc[...]
        inter = []
        for c in cs:
            st_ref[0, c] = st
            inter.append(_mm_nt(q_pos[c], st))
            st = st * a_row[c] + jnp.where(bd, upd[c], 0.0)
        st_sc[...] = st
        for c in cs:
            g = _fold_heads(o_s[c], mv, C) + inter[c]
            raw_ref[0, rows[c], :] = g
            rstd = lax.rsqrt(_head_mean(g * g, mv, 64.0) + EPS)
            out_ref[0, rows[c], :] = (g * rstd * gn_ref[...] * _silu(ps[c][:, 640:896])).astype(_MXU)

    tok = lambda w: pl.BlockSpec((1, G * C, w), lambda b, n: (b, n, 0))
    return pl.pallas_call(
        body, name="gla_fwd", grid=(B, NG),
        in_specs=[tok(GLA_W), _full((128, 128)), _full((1, 128)), _full((1, 256))],
        out_specs=[tok(256), tok(256), pl.BlockSpec((1, G, 256, 128), lambda b, n: (b, n, 0, 0))],
        out_shape=[jax.ShapeDtypeStruct((B, S, 256), _MXU), jax.ShapeDtypeStruct((B, S, 256), F32),
                   jax.ShapeDtypeStruct((B, NC, 256, 128), F32)],
        scratch_shapes=[pltpu.VMEM((256, 128), F32)],
        compiler_params=_cp(("parallel", "arbitrary")),
    )(gla_p, w_g2p, b_g2, gnw)


def _gla_bwd(gla_p, w_g2p, b_g2, gnw, raw, states, d_mix):
    B, S, _ = gla_p.shape
    C = CHUNK
    NC = S // C
    G = min(GLA_G, NC)
    NG = NC // G

    def body(p_ref, w_ref, b_ref, gn_ref, raw_ref, st_ref, dm_ref, dp_ref, dw_ref, db_ref, dgn_ref, dst_sc):
        first = (pl.program_id(0) == 0) & (pl.program_id(1) == 0)

        @pl.when(first)
        def _():
            dw_ref[...] = jnp.zeros_like(dw_ref)
            db_ref[...] = jnp.zeros_like(db_ref)
            dgn_ref[...] = jnp.zeros_like(dgn_ref)

        @pl.when(pl.program_id(1) == 0)
        def _():
            dst_sc[...] = jnp.zeros_like(dst_sc)

        mk, mv, bd, lower, ltri, utri = _gla_masks()
        gn = gn_ref[...]
        cs = range(G)
        rows = [slice(c * C, (c + 1) * C) for c in cs]
        ps = [p_ref[0, rows[c], :].astype(F32) for c in cs]
        vs = [ps[c][:, 256:512] for c in cs]
        pre = [_mm(ps[c][:, 512:640], w_ref[...]) + b_ref[...] for c in cs]
        cum = [_mm_f32(ltri, _log_sigmoid(pre[c]) * (1.0 / GLA_TAU)) for c in cs]
        dg, dz, dgn_acc = [], [], jnp.zeros((1, 256), F32)
        for c in cs:
            g = raw_ref[0, rows[c], :]
            z = ps[c][:, 640:896]
            rstd = lax.rsqrt(_head_mean(g * g, mv, 64.0) + EPS)
            gh = g * rstd
            dm = dm_ref[0, rows[c], :]
            d_gn = dm * _silu(z)
            dz.append(dm * gh * gn * _dsilu(z))
            dgn_acc = dgn_acc + jnp.sum(d_gn * gh, axis=0, keepdims=True)
            d_gh = d_gn * gn
            dg.append(rstd * (d_gh - gh * _head_mean(d_gh * gh, mv, 64.0)))
        do_s = [_stack_heads(dg[c], mv) for c in cs]
        dattn = [_mm_nt(do_s[c], vs[c]) for c in cs]
        ks, e_pos, e_neg, q_pos, q_neg, k_pos, k_neg, qp_s, qn_s, past, fut, a_row, w_dec, kd = ([] for _ in range(14))
        for c in cs:
            q = ps[c][:, 0:128]
            k = ps[c][:, 128:256] * GLA_KSCALE
            last = cum[c][C - 1:C, :]
            ep, en = jnp.exp(cum[c]), jnp.exp(-cum[c])
            ks.append(k), e_pos.append(ep), e_neg.append(en)
            q_pos.append(q * ep), q_neg.append(q * en), k_pos.append(k * ep), k_neg.append(k * en)
            qp_s.append(_stack_heads(q_pos[c], mk)), qn_s.append(_stack_heads(q_neg[c], mk))
            past.append(_mm_nt(qp_s[c], k_neg[c]))
            fut.append(_mm_nt(qn_s[c], k_pos[c]))
            a_row.append(jnp.exp(last))
            w_dec.append(jnp.exp(last - cum[c]))
            kd.append(k * w_dec[c])
        sts = [st_ref[0, c] for c in cs]
        dq_st = [_mm(dg[c], sts[c]) for c in cs]
        dst_in = [_mm_tn(dg[c], q_pos[c]) for c in cs]
        dv, dq_pos, dk_neg, dq_neg, dk_pos = [], [], [], [], []
        for c in cs:
            attn = jnp.where(lower, past[c], fut[c])
            dpast = jnp.where(lower, dattn[c], 0.0)
            dfut = jnp.where(lower, 0.0, dattn[c])
            dv.append(_mm_tn(attn, do_s[c]))
            dq_pos.append(_fold_heads(_mm(dpast, k_neg[c]), mk, C) + dq_st[c])
            dk_neg.append(_mm_tn(dpast, qp_s[c]))
            dq_neg.append(_fold_heads(_mm(dfut, k_pos[c]), mk, C))
            dk_pos.append(_mm_tn(dfut, qn_s[c]))
        dst_next = dst_sc[...]
        d_a, d_kd = [None] * G, [None] * G
        for c in reversed(cs):
            d_a[c] = jnp.sum(dst_next * sts[c], axis=0, keepdims=True)
            gmat = jnp.where(bd, dst_next, 0.0)
            d_kd[c] = _mm(vs[c], gmat)
            dv[c] = dv[c] + _mm_nt(kd[c], gmat)
            dst_next = dst_next * a_row[c] + jnp.where(bd, dst_in[c], 0.0)
        dst_sc[...] = dst_next
        row = lax.broadcasted_iota(jnp.int32, (C, 128), 0)
        d_la, dk, dq = [], [], []
        for c in cs:
            t = d_kd[c] * kd[c]
            dk.append(d_kd[c] * w_dec[c] + dk_neg[c] * e_neg[c] + dk_pos[c] * e_pos[c])
            dq.append(dq_pos[c] * e_pos[c] + dq_neg[c] * e_neg[c])
            d_last = jnp.sum(t, axis=0, keepdims=True) + d_a[c] * a_row[c]
            d_cum = (dq_pos[c] * q_pos[c] - dk_neg[c] * k_neg[c] - dq_neg[c] * q_neg[c] + dk_pos[c] * k_pos[c] - t)
            d_la.append(_mm_f32(utri, d_cum + jnp.where(row == C - 1, d_last, 0.0)))
        d_pre = [d_la[c] * _sig(-pre[c]) * (1.0 / GLA_TAU) for c in cs]
        d_gg = [_mm_nt(d_pre[c], w_ref[...]) for c in cs]
        dw_acc = _mm_tn(ps[0][:, 512:640], d_pre[0])
        db_acc = jnp.sum(d_pre[0], axis=0, keepdims=True)
        for c in cs[1:]:
            dw_acc = dw_acc + _mm_tn(ps[c][:, 512:640], d_pre[c])
            db_acc = db_acc + jnp.sum(d_pre[c], axis=0, keepdims=True)
        for c in cs:
            dp_ref[0, rows[c], :] = jnp.concatenate([dq[c], dk[c] * GLA_KSCALE, dv[c], d_gg[c], dz[c]],
                                                    axis=1).astype(_MXU)
        dw_ref[...] += dw_acc
        db_ref[...] += db_acc
        dgn_ref[...] += dgn_acc

        @pl.when((pl.program_id(0) == B - 1) & (pl.program_id(1) == NG - 1))
        def _():
            s1 = dgn_ref[...]
            s1 = s1 + pltpu.roll(s1, 128, 1)
            dgn_ref[...] = s1 + pltpu.roll(s1, 64, 1)

    tok = lambda w: pl.BlockSpec((1, G * C, w), lambda b, i: (b, NG - 1 - i, 0))
    return pl.pallas_call(
        body, name="gla_bwd", grid=(B, NG),
        in_specs=[tok(GLA_W), _full((128, 128)), _full((1, 128)), _full((1, 256)), tok(256),
                  pl.BlockSpec((1, G, 256, 128), lambda b, i: (b, NG - 1 - i, 0, 0)), tok(256)],
        out_specs=[tok(GLA_W), _full((128, 128)), _full((1, 128)), _full((1, 256))],
        out_shape=[jax.ShapeDtypeStruct((B, S, GLA_W), _MXU), jax.ShapeDtypeStruct((128, 128), F32),
                   jax.ShapeDtypeStruct((1, 128), F32), jax.ShapeDtypeStruct((1, 256), F32)],
        scratch_shapes=[pltpu.VMEM((256, 128), F32)],
        compiler_params=_cp(("arbitrary", "arbitrary")),
    )(gla_p, w_g2p, b_g2, gnw, raw, states, d_mix)


def _rms(x, w):
    rstd = lax.rsqrt(jnp.mean(x * x, axis=-1, keepdims=True) + EPS)
    xh = x * rstd
    return xh, rstd, xh * w


def _rms_bwd(dy, xh, rstd, w):
    dxh = dy * w
    return rstd * (dxh - xh * jnp.mean(dxh * xh, axis=-1, keepdims=True))


MLA_T = 256


def _mla_prep_fwd(mla_p, cos, sin, qnw, kvnw, w_uq, w_ukv):
    B, S, _ = mla_p.shape
    tm = min(S, 512)

    t = min(MLA_T, S)
    nt = tm // t

    def body(p_ref, c_ref, s_ref, qn_ref, kn_ref, wq_ref, wkv_ref, q_ref, k_ref, v_ref, kt_ref, vt_ref):
        p = p_ref[0].astype(F32)
        cs, sn = c_ref[0], s_ref[0]
        _, _, qn = _rms(p[:, 0:256], qn_ref[...])
        qpre = _mm(qn, wq_ref[...])
        _, _, kvn = _rms(p[:, 256:384], kn_ref[...])
        kv = _mm(kvn, wkv_ref[...])
        kpe = _rope128(p[:, 384:512], cs, sn)
        for h in range(8):
            sl = slice(128 * h, 128 * h + 128)
            q_ref[0, :, sl] = _rope128(qpre[:, sl], cs, sn).astype(_MXU)
            kh = kv[:, sl] + kpe
            k_ref[0, :, sl] = kh.astype(_MXU)
            kht = kh.T
            for n in range(nt):
                kt_ref[0, n, sl, :] = kht[:, n * t:(n + 1) * t].astype(_MXU)
        v_ref[0] = kv[:, 1024:1536].astype(_MXU)
        for pr in range(4):
            vht = kv[:, 1024 + 128 * pr:1152 + 128 * pr].T
            for n in range(nt):
                vt_ref[0, n, 128 * pr:128 * pr + 128, :] = vht[:, n * t:(n + 1) * t].astype(_MXU)

    tok = lambda w: pl.BlockSpec((1, tm, w), lambda b, i: (b, i, 0))
    tr = lambda w: pl.BlockSpec((1, nt, w, t), lambda b, i: (b, i, 0, 0))
    return pl.pallas_call(
        body, name="mla_prep_fwd", grid=(B, S // tm),
        in_specs=[tok(512), tok(128), tok(128), _full((1, 256)), _full((1, 128)), _full((256, 1024)),
                  _full((128, 1536))],
        out_specs=[tok(1024), tok(1024), tok(512), tr(1024), tr(512)],
        out_shape=[jax.ShapeDtypeStruct((B, S, 1024), _MXU), jax.ShapeDtypeStruct((B, S, 1024), _MXU),
                   jax.ShapeDtypeStruct((B, S, 512), _MXU), jax.ShapeDtypeStruct((B, S // t, 1024, t), _MXU),
                   jax.ShapeDtypeStruct((B, S // t, 512, t), _MXU)],
        compiler_params=_cp(("parallel", "parallel")),
    )(mla_p, cos, sin, qnw, kvnw, w_uq, w_ukv)


def _chunk_mask_t(t):
    kj = lax.broadcasted_iota(jnp.int32, (t, t), 0) // CHUNK
    qi = lax.broadcasted_iota(jnp.int32, (t, t), 1) // CHUNK
    return kj <= qi


MLA_HG = 8
MLA_HG_FWD = 8
LOG2E = 1.4426950408889634
MLA_C2 = MLA_SCALE * LOG2E


def _mla_attn_fwd(q, k, vt):
    B, S, _ = q.shape
    t = min(MLA_T, S)
    nq = S // t
    HG = MLA_HG_FWD
    NP = HG // 2

    def body(q_ref, k_ref, vt_ref, o_ref, lse_ref, sa, sb, m_sc, l_sc, acc_sc):
        i = pl.program_id(2)
        row = lax.broadcasted_iota(jnp.int32, (128, 1), 0)
        low = row < 64
        mask = _chunk_mask_t(t)
        m_sc[...] = jnp.full(m_sc.shape, -jnp.inf, F32)
        l_sc[...] = jnp.zeros_like(l_sc)
        acc_sc[...] = jnp.zeros_like(acc_sc)

        ones = jnp.ones((8, t), _MXU)

        def scores(j, buf):
            kb = k_ref[0, pl.ds(pl.multiple_of(j * t, t), t), :]
            for h in range(HG):
                cols = slice(128 * h, 128 * h + 128)
                buf[h] = (_mm_nt(kb[:, cols], q_ref[0, :, cols]) * MLA_C2).astype(_MXU)

        def absorb(j, buf, masked):
            vtb = vt_ref[0, j]
            for pr in range(NP):
                alphas, pvs = [], []
                for hh in range(2):
                    h = 2 * pr + hh
                    s = buf[h]
                    if masked:
                        s = jnp.where(mask, s, jnp.full_like(s, -jnp.inf))
                    m_old = m_sc[h]
                    m_new = jnp.maximum(m_old, jnp.max(s, axis=0, keepdims=True).astype(F32))
                    alpha = jnp.exp2(m_old - m_new)
                    p = jnp.exp2(s - m_new.astype(_MXU))
                    l_sc[h] = alpha * l_sc[h] + _mm(ones, p)[0:1, :]
                    m_sc[h] = m_new
                    vth = vtb[128 * pr:128 * pr + 128, :]
                    vth = jnp.where(low if hh == 0 else ~low, vth, jnp.zeros_like(vth))
                    pvs.append(_mm(vth, p))
                    alphas.append(alpha)
                acc_sc[pr] = acc_sc[pr] * jnp.where(low, alphas[0], alphas[1]) + pvs[0] + pvs[1]

        scores(0, sb)

        def pair(jj, carry):
            j0 = 2 * jj
            scores(j0 + 1, sa)
            absorb(j0, sb, False)
            scores(j0 + 2, sb)
            absorb(j0 + 1, sa, False)
            return carry

        lax.fori_loop(0, i // 2, pair, 0)

        @pl.when(i % 2 == 1)
        def _():
            scores(i, sa)
            absorb(i - 1, sb, False)
            absorb(i, sa, True)

        @pl.when(i % 2 == 0)
        def _():
            absorb(i, sb, True)

        for pr in range(NP):
            l_e, l_o = l_sc[2 * pr], l_sc[2 * pr + 1]
            o_ref[0, :, 128 * pr:128 * pr + 128] = (acc_sc[pr] / jnp.where(low, l_e, l_o)).T
            lse_ref[0, pr, 0, 0:1, :] = m_sc[2 * pr] + jnp.log(l_e) * LOG2E
            lse_ref[0, pr, 0, 1:2, :] = m_sc[2 * pr + 1] + jnp.log(l_o) * LOG2E

    return pl.pallas_call(
        body, name="mla_attn_fwd", grid=(B, 8 // HG, nq),
        in_specs=[pl.BlockSpec((1, t, 128 * HG), lambda b, g, i: (b, i, g)),
                  pl.BlockSpec((1, S, 128 * HG), lambda b, g, i: (b, 0, g)),
                  pl.BlockSpec((1, nq, 64 * HG, t), lambda b, g, i: (b, 0, g, 0))],
        out_specs=[pl.BlockSpec((1, t, 64 * HG), lambda b, g, i: (b, i, g)),
                   pl.BlockSpec((1, NP, 1, 2, t), lambda b, g, i: (b, g, i, 0, 0))],
        out_shape=[jax.ShapeDtypeStruct((B, S, 512), F32), jax.ShapeDtypeStruct((B, 4, nq, 2, t), F32)],
        scratch_shapes=[pltpu.VMEM((HG, t, t), _MXU), pltpu.VMEM((HG, t, t), _MXU), pltpu.VMEM((HG, 1, t), F32),
                        pltpu.VMEM((HG, 1, t), F32), pltpu.VMEM((NP, 128, t), F32)],
        compiler_params=_cp(("parallel", "parallel", "arbitrary")),
    )(q, k, vt)


def _mla_gate_bwd(d_mix, o, mla_p):
    B, S, _ = o.shape
    tm = min(S, 512)
    t = min(MLA_T, S)
    nt = tm // t

    def body(dm_ref, o_ref, z_ref, do_ref, dz_ref, dl_ref):
        dm, ov, z = dm_ref[0], o_ref[0], z_ref[0].astype(F32)
        do = dm * _silu(z)
        dz_ref[0] = (dm * ov * _dsilu(z)).astype(_MXU)
        do_ref[0] = do.astype(_MXU)
        prod = do * ov
        for pr in range(4):
            pt = prod[:, 128 * pr:128 * pr + 128].T
            se = jnp.sum(pt[0:64], axis=0, keepdims=True)
            so = jnp.sum(pt[64:128], axis=0, keepdims=True)
            for n in range(nt):
                dl_ref[0, pr, n, 0:1, :] = se[:, n * t:(n + 1) * t]
                dl_ref[0, pr, n, 1:2, :] = so[:, n * t:(n + 1) * t]

    tok = lambda c: pl.BlockSpec((1, tm, 512), lambda b, i: (b, i, c))
    return pl.pallas_call(
        body, name="mla_gate_bwd", grid=(B, S // tm),
        in_specs=[tok(0), tok(0), tok(1)],
        out_specs=[tok(0), tok(0), pl.BlockSpec((1, 4, nt, 2, t), lambda b, i: (b, 0, i, 0, 0))],
        out_shape=[jax.ShapeDtypeStruct((B, S, 512), _MXU), jax.ShapeDtypeStruct((B, S, 512), _MXU),
                   jax.ShapeDtypeStruct((B, 4, S // t, 2, t), F32)],
        compiler_params=_cp(("parallel", "parallel")),
    )(d_mix, o, mla_p)


def _mla_attn_bwd(q, k, v, kt, do, lse, dl):
    B, S, _ = q.shape
    t = min(MLA_T, S)
    nk = S // t

    HG = MLA_HG
    NP = HG // 2

    def body(q_ref, k_ref, v_ref, kt_ref, do_ref, lse_ref, dl_ref, dq_ref, dk_ref, dv_ref,
             sa, da, sb, db, dqt_sc, dk_sc, dv_sc):
        j = pl.program_id(2)

        @pl.when(j == 0)
        def _():
            dqt_sc[...] = jnp.zeros_like(dqt_sc)

        dk_sc[...] = jnp.zeros_like(dk_sc)
        dv_sc[...] = jnp.zeros_like(dv_sc)
        lane = lax.broadcasted_iota(jnp.int32, (1, 128), 1)
        low = lane < 64
        mask = _chunk_mask_t(t)

        def half(x, hh):
            return jnp.where(low if hh == 0 else ~low, x, jnp.zeros_like(x))

        def prepare(i, sbuf, dbuf):
            rows = pl.ds(pl.multiple_of(i * t, t), t)
            for h in range(HG):
                cols = slice(128 * h, 128 * h + 128)
                pc = slice(128 * (h // 2), 128 * (h // 2) + 128)
                sbuf[h] = _mm_nt(k_ref[0, :, cols], q_ref[0, rows, cols]) * MLA_C2
                dbuf[h] = _mm_nt(half(v_ref[0, :, pc], h % 2), do_ref[0, rows, pc])

        def absorb(i, sbuf, dbuf, masked):
            rows = pl.ds(pl.multiple_of(i * t, t), t)
            for h in range(HG):
                pr, hh = h // 2, h % 2
                cols = slice(128 * h, 128 * h + 128)
                pc = slice(128 * pr, 128 * pr + 128)
                p = jnp.exp2(sbuf[h] - lse_ref[0, pr, i][hh:hh + 1, :])
                if masked:
                    p = jnp.where(mask, p, 0.0)
                dv_sc[pr] += _mm(p, half(do_ref[0, rows, pc], hh))
                ds = p * (dbuf[h] - dl_ref[0, pr, i][hh:hh + 1, :])
                dqt_sc[i, cols, :] += _mm(kt_ref[0, 0, cols, :], ds)
                dk_sc[h] += _mm(ds, q_ref[0, rows, cols])

        n = nk - 1 - j
        prepare(jnp.minimum(j + 1, nk - 1), sb, db)

        def pair(jj, carry):
            i0 = j + 1 + 2 * jj
            prepare(i0 + 1, sa, da)
            absorb(i0, sb, db, False)
            prepare(jnp.where(i0 + 2 <= nk - 1, i0 + 2, j), sb, db)
            absorb(i0 + 1, sa, da, False)
            return carry

        lax.fori_loop(0, n // 2, pair, 0)

        @pl.when(n % 2 == 1)
        def _():
            prepare(j, sa, da)
            absorb(nk - 1, sb, db, False)
            absorb(j, sa, da, True)

        @pl.when(n % 2 == 0)
        def _():
            absorb(j, sb, db, True)

        for h in range(HG):
            dk_ref[0, :, 128 * h:128 * h + 128] = (dk_sc[h] * MLA_SCALE).astype(_MXU)
        for pr in range(NP):
            dv_ref[0, :, 128 * pr:128 * pr + 128] = dv_sc[pr].astype(_MXU)

        @pl.when(j == nk - 1)
        def _():
            for i in range(nk):
                dq_ref[0, i * t:(i + 1) * t, :] = (dqt_sc[i].T * MLA_SCALE).astype(_MXU)

    seq = lambda w: pl.BlockSpec((1, S, w), lambda b, g, j: (b, 0, g))
    blk = lambda w: pl.BlockSpec((1, t, w), lambda b, g, j: (b, j, g))
    stat = pl.BlockSpec((1, NP, nk, 2, t), lambda b, g, j: (b, g, 0, 0, 0))
    return pl.pallas_call(
        body, name="mla_attn_bwd", grid=(B, 8 // HG, nk),
        in_specs=[seq(128 * HG), blk(128 * HG), blk(64 * HG),
                  pl.BlockSpec((1, 1, 128 * HG, t), lambda b, g, j: (b, j, g, 0)), seq(64 * HG), stat, stat],
        out_specs=[seq(128 * HG), blk(128 * HG), blk(64 * HG)],
        out_shape=[jax.ShapeDtypeStruct((B, S, 1024), _MXU), jax.ShapeDtypeStruct((B, S, 1024), _MXU),
                   jax.ShapeDtypeStruct((B, S, 512), _MXU)],
        scratch_shapes=[pltpu.VMEM((HG, t, t), F32), pltpu.VMEM((HG, t, t), F32), pltpu.VMEM((HG, t, t), F32),
                        pltpu.VMEM((HG, t, t), F32), pltpu.VMEM((nk, 128 * HG, t), F32),
                        pltpu.VMEM((HG, t, 128), F32), pltpu.VMEM((NP, t, 128), F32)],
        compiler_params=_cp(("parallel", "parallel", "arbitrary"), 56),
    )(q, k, v, kt, do, lse, dl)


def _mla_prep_bwd(mla_p, cos, sin, qnw, kvnw, w_uq, w_ukv, dq, dk, dv):
    B, S, _ = mla_p.shape
    tm = min(S, 512)

    def body(p_ref, c_ref, s_ref, qn_ref, kn_ref, wq_ref, wkv_ref, dq_ref, dk_ref, dv_ref,
             dp_ref, dwq_ref, dwkv_ref, dqn_ref, dkn_ref):
        first = (pl.program_id(0) == 0) & (pl.program_id(1) == 0)

        @pl.when(first)
        def _():
            dwq_ref[...] = jnp.zeros_like(dwq_ref)
            dwkv_ref[...] = jnp.zeros_like(dwkv_ref)
            dqn_ref[...] = jnp.zeros_like(dqn_ref)
            dkn_ref[...] = jnp.zeros_like(dkn_ref)

        p = p_ref[0].astype(F32)
        cs, sn = c_ref[0], s_ref[0]
        lane = lax.broadcasted_iota(jnp.int32, (1, 128), 1)
        pe = (lane >= 64) & (lane < 96)
        qh, q_rstd, qn = _rms(p[:, 0:256], qn_ref[...])
        kvh, kv_rstd, kvn = _rms(p[:, 256:384], kn_ref[...])
        dqv = dq_ref[0].astype(F32)
        dkv = dk_ref[0].astype(F32)
        dqpre = jnp.concatenate(
            [_rope128_t(dqv[:, 128 * h:128 * h + 128], cs, sn) for h in range(8)], axis=1)
        dkpe = jnp.zeros((tm, 128), F32)
        for h in range(8):
            dkpe = dkpe + jnp.where(pe, dkv[:, 128 * h:128 * h + 128], 0.0)
        dkr = _rope128_t(dkpe, cs, sn)
        dkv_all = jnp.concatenate([dkv, dv_ref[0].astype(F32)], axis=1)
        d_qn = _mm_nt(dqpre, wq_ref[...])
        d_kvn = _mm_nt(dkv_all, wkv_ref[...])
        dwq_ref[...] += _mm_tn(qn, dqpre)
        dwkv_ref[...] += _mm_tn(kvn, dkv_all)
        dqn_ref[...] += jnp.sum(d_qn * qh, axis=0, keepdims=True)
        dkn_ref[...] += jnp.sum(d_kvn * kvh, axis=0, keepdims=True)
        dp_ref[0] = jnp.concatenate([_rms_bwd(d_qn, qh, q_rstd, qn_ref[...]),
                                     _rms_bwd(d_kvn, kvh, kv_rstd, kn_ref[...]), dkr], axis=1).astype(_MXU)

    tok = lambda w: pl.BlockSpec((1, tm, w), lambda b, i: (b, i, 0))
    return pl.pallas_call(
        body, name="mla_prep_bwd", grid=(B, S // tm),
        in_specs=[tok(512), tok(128), tok(128), _full((1, 256)), _full((1, 128)), _full((256, 1024)),
                  _full((128, 1536)), tok(1024), tok(1024), tok(512)],
        out_specs=[tok(512), _full((256, 1024)), _full((128, 1536)), _full((1, 256)), _full((1, 128))],
        out_shape=[jax.ShapeDtypeStruct((B, S, 512), _MXU), jax.ShapeDtypeStruct((256, 1024), F32),
                   jax.ShapeDtypeStruct((128, 1536), F32), jax.ShapeDtypeStruct((1, 256), F32),
                   jax.ShapeDtypeStruct((1, 128), F32)],
        compiler_params=_cp(("arbitrary", "arbitrary")),
    )(mla_p, cos, sin, qnw, kvnw, w_uq, w_ukv, dq, dk, dv)


def _out_fwd(x, gate, r_g, o_mla, mla_p, g_g, w_out):
    B, S, D = x.shape
    tm = min(S, 512)

    def body(x_ref, g_ref, r_ref, o_ref, z_ref, gg_ref, w_ref, xn_ref, y_ref, mm_ref):
        mm = (o_ref[0] * _silu(z_ref[0].astype(F32))).astype(_MXU)
        mm_ref[0] = mm
        y = (jnp.dot(r_ref[0], w_ref[0:256, :], preferred_element_type=F32)
             + jnp.dot(mm, w_ref[256:768, :], preferred_element_type=F32)
             + jnp.dot(gg_ref[0], w_ref[768:1024, :], preferred_element_type=F32))
        y_ref[0] = y
        xn_ref[0] = x_ref[0] + g_ref[0] * y

    tok = lambda w, c=0: pl.BlockSpec((1, tm, w), lambda b, i: (b, i, c))
    return pl.pallas_call(
        body, name="out_fwd", grid=(B, S // tm),
        in_specs=[tok(D), pl.BlockSpec((1, 1, D), lambda b, i: (b, 0, 0)), tok(256), tok(512), tok(512, 1),
                  tok(256), _full((D, D))],
        out_specs=[tok(D), tok(D), tok(512)],
        out_shape=[jax.ShapeDtypeStruct((B, S, D), F32), jax.ShapeDtypeStruct((B, S, D), F32),
                   jax.ShapeDtypeStruct((B, S, 512), _MXU)],
        compiler_params=_cp(("parallel", "parallel")),
    )(x, gate, r_g, o_mla, mla_p, g_g, w_out)


def _out_bwd(dx, y, gate, r_g, mm, g_g, w_out):
    B, S, D = dx.shape
    tm = min(S, 512)

    def body(dx_ref, y_ref, g_ref, r_ref, mm_ref, gg_ref, w_ref, dr_ref, dmm_ref, dg_ref, dw_ref, dgate_ref):
        first = (pl.program_id(0) == 0) & (pl.program_id(1) == 0)

        @pl.when(first)
        def _():
            dw_ref[...] = jnp.zeros_like(dw_ref)

        @pl.when(pl.program_id(1) == 0)
        def _():
            dgate_ref[...] = jnp.zeros_like(dgate_ref)

        dxv = dx_ref[0]
        dgate_ref[0] += jnp.sum(dxv * y_ref[0], axis=0, keepdims=True)
        dy = (dxv * g_ref[0]).astype(_MXU)
        dr_ref[0] = _mm_nt(dy, w_ref[0:256, :])
        dmm_ref[0] = _mm_nt(dy, w_ref[256:768, :])
        dg_ref[0] = _mm_nt(dy, w_ref[768:1024, :])
        dw_ref[0:256, :] += _mm_tn(r_ref[0], dy)
        dw_ref[256:768, :] += _mm_tn(mm_ref[0], dy)
        dw_ref[768:1024, :] += _mm_tn(gg_ref[0], dy)

    tok = lambda w: pl.BlockSpec((1, tm, w), lambda b, i: (b, i, 0))
    per_seq = pl.BlockSpec((1, 1, D), lambda b, i: (b, 0, 0))
    return pl.pallas_call(
        body, name="out_bwd", grid=(B, S // tm),
        in_specs=[tok(D), tok(D), per_seq, tok(256), tok(512), tok(256), _full((D, D))],
        out_specs=[tok(256), tok(512), tok(256), _full((D, D)), per_seq],
        out_shape=[jax.ShapeDtypeStruct((B, S, 256), F32), jax.ShapeDtypeStruct((B, S, 512), F32),
                   jax.ShapeDtypeStruct((B, S, 256), F32), jax.ShapeDtypeStruct((D, D), F32),
                   jax.ShapeDtypeStruct((B, 1, D), F32)],
        compiler_params=_cp(("arbitrary", "arbitrary")),
    )(dx, y, gate, r_g, mm, g_g, w_out)


def _proj_bwd_x(x, shift, scale, nw, w_arr, d_ret, d_mla, d_mz, d_gla, dx_out):
    B, S, D = x.shape
    tm = min(S, 512)

    def body(x_ref, sc_ref, nw_ref, w_ref, dr_ref, dm_ref, dz_ref, dg_ref, dxo_ref,
             dx_ref, dsh_ref, dsc_ref, dnw_ref):
        first = (pl.program_id(0) == 0) & (pl.program_id(1) == 0)

        @pl.when(first)
        def _():
            dnw_ref[...] = jnp.zeros_like(dnw_ref)

        @pl.when(pl.program_id(1) == 0)
        def _():
            dsh_ref[...] = jnp.zeros_like(dsh_ref)
            dsc_ref[...] = jnp.zeros_like(dsc_ref)

        dp = jnp.concatenate([dr_ref[0], dm_ref[0], dz_ref[0], dg_ref[0]], axis=1)
        dh = lax.dot_general(dp, w_ref[...], (((1,), (1,)), ((), ())), preferred_element_type=F32)
        xv = x_ref[0]
        rstd = lax.rsqrt(jnp.mean(xv * xv, axis=-1, keepdims=True) + EPS)
        xh = xv * rstd
        nwv = nw_ref[...]
        mod = 1.0 + sc_ref[0]
        dsh_ref[0] += jnp.sum(dh, axis=0, keepdims=True)
        dsc_ref[0] += jnp.sum(dh * xh * nwv, axis=0, keepdims=True)
        dnw_ref[...] += jnp.sum(dh * xh * mod, axis=0, keepdims=True)
        dxh = dh * nwv * mod
        dx_ref[0] = dxo_ref[0] + rstd * (dxh - xh * jnp.mean(dxh * xh, axis=-1, keepdims=True))

    tok = lambda w: pl.BlockSpec((1, tm, w), lambda b, i: (b, i, 0))
    per_seq = pl.BlockSpec((1, 1, D), lambda b, i: (b, 0, 0))
    return pl.pallas_call(
        body, name="proj_bwd_x", grid=(B, S // tm),
        in_specs=[tok(D), per_seq, _full((1, D)), _full((D, ARR_W)), tok(RET_W), tok(512), tok(512),
                  tok(GLA_W), tok(D)],
        out_specs=[tok(D), per_seq, per_seq, _full((1, D))],
        out_shape=[jax.ShapeDtypeStruct((B, S, D), F32), jax.ShapeDtypeStruct((B, 1, D), F32),
                   jax.ShapeDtypeStruct((B, 1, D), F32), jax.ShapeDtypeStruct((1, D), F32)],
        compiler_params=_cp(("arbitrary", "arbitrary")),
    )(x, scale, nw, w_arr, d_ret, d_mla, d_mz, d_gla, dx_out)


def _proj_bwd_w(h, d_ret, d_mla, d_mz, d_gla):
    B, S, D = h.shape
    tm = min(S, 512)

    def body(h_ref, dr_ref, dm_ref, dz_ref, dg_ref, dw_ref):
        first = (pl.program_id(0) == 0) & (pl.program_id(1) == 0)

        @pl.when(first)
        def _():
            dw_ref[...] = jnp.zeros_like(dw_ref)

        hv = h_ref[0]
        tn = lambda d_ref: lax.dot_general(hv, d_ref[0], (((0,), (0,)), ((), ())), preferred_element_type=F32)
        dw_ref[:, 0:RET_W] += tn(dr_ref)
        dw_ref[:, RET_W:RET_W + 512] += tn(dm_ref)
        dw_ref[:, RET_W + 512:RET_W + MLA_W] += tn(dz_ref)
        dw_ref[:, RET_W + MLA_W:ARR_W] += tn(dg_ref)

    tok = lambda w: pl.BlockSpec((1, tm, w), lambda b, i: (b, i, 0))
    return pl.pallas_call(
        body, name="proj_bwd_w", grid=(B, S // tm),
        in_specs=[tok(D), tok(RET_W), tok(512), tok(512), tok(GLA_W)],
        out_specs=_full((D, ARR_W)), out_shape=jax.ShapeDtypeStruct((D, ARR_W), F32),
        compiler_params=_cp(("arbitrary", "arbitrary"), 56),
    )(h, d_ret, d_mla, d_mz, d_gla)


def _final_loss(x, fw, target):
    B, S, D = x.shape
    tm = min(S, 512)

    def body(x_ref, fw_ref, t_ref, dx_ref, loss_ref, dfw_ref):
        first = (pl.program_id(0) == 0) & (pl.program_id(1) == 0)

        @pl.when(first)
        def _():
            loss_ref[...] = jnp.zeros_like(loss_ref)
            dfw_ref[...] = jnp.zeros_like(dfw_ref)

        xv = x_ref[0]
        fwv = fw_ref[...]
        rstd = lax.rsqrt(jnp.mean(xv * xv, axis=-1, keepdims=True) + EPS)
        xh = xv * rstd
        err = xh * fwv - t_ref[0]
        loss_ref[...] += 0.5 * jnp.sum(jnp.mean(err * err, axis=-1, keepdims=True), axis=0, keepdims=True)
        dy = err * (1.0 / D)
        dfw_ref[...] += jnp.sum(dy * xh, axis=0, keepdims=True)
        dxh = dy * fwv
        dx_ref[0] = rstd * (dxh - xh * jnp.mean(dxh * xh, axis=-1, keepdims=True))

    tok = pl.BlockSpec((1, tm, D), lambda b, i: (b, i, 0))
    return pl.pallas_call(
        body, name="final_loss", grid=(B, S // tm),
        in_specs=[tok, _full((1, D)), tok],
        out_specs=[tok, _full((1, 1)), _full((1, D))],
        out_shape=[jax.ShapeDtypeStruct((B, S, D), F32), jax.ShapeDtypeStruct((1, 1), F32),
                   jax.ShapeDtypeStruct((1, D), F32)],
        compiler_params=_cp(("arbitrary", "arbitrary")),
    )(x, fw, target)


def _local_step(x, pos3, mod, loss_target, small, w_in_a, w_uq_a, w_ukv_a, w_out_b):
    B, S, D = x.shape
    tabs = _rope_tables(pos3)
    saved = []
    for l in range(DEPTH):
        x, s = _layer_fwd(x, tabs, mod[l], {n: a[l] for n, a in small.items() if n != "final_norm"},
                          w_in_a[l], w_uq_a[l], w_ukv_a[l], w_out_b[l])
        saved.append(s)
    dx, loss, d_fw = _final_loss(x, small["final_norm"].reshape(1, D), loss_target)
    grads = dict(final_norm=d_fw.reshape(D))
    per_layer = [None] * DEPTH
    for l in reversed(range(DEPTH)):
        dx, per_layer[l] = _layer_bwd(dx, saved[l], tabs)
    for name in per_layer[0]:
        grads[name] = jnp.stack([per_layer[l][name] for l in range(DEPTH)])
    return loss, dx, grads


def _layer_fwd(x, tabs, mod_l, small_l, w_in_a, w_uq_a=None, w_ukv_a=None, w_out_b=None, late_weights=None):
    B, S, D = x.shape
    cr, sr, cm, sm = tabs
    shift = mod_l[:, 0:D].reshape(B, 1, D)
    scale = mod_l[:, D:2 * D].reshape(B, 1, D)
    gate = mod_l[:, 2 * D:3 * D].reshape(B, 1, D)
    nw = small_l["norm_w"].reshape(1, D)
    qnw = small_l["mla_q_norm"].reshape(1, 256)
    kvnw = small_l["mla_kv_norm"].reshape(1, 128)
    w_g2p = jnp.pad(small_l["gla_w_g2"], ((0, 112), (0, 0)))
    b_g2 = small_l["gla_b_g2"].reshape(1, 128)
    gnw = jnp.tile(small_l["gla_norm"], 4).reshape(1, 256)
    ret_p, mla_p, gla_p, h = _proj_fwd(x, shift, scale, nw, w_in_a)
    r_g, r_raw, r_st = _ret_fwd(ret_p, cr, sr)
    if late_weights is not None:
        w_uq_a, w_ukv_a, w_out_b = late_weights(r_raw)
    q, k, v, kt, vt = _mla_prep_fwd(mla_p, cm, sm, qnw, kvnw, w_uq_a, w_ukv_a)
    o_mla, lse = _mla_attn_fwd(q, k, vt)
    g_g, g_raw, g_st = _gla_fwd(gla_p, w_g2p, b_g2, gnw)
    x_new, y, mm = _out_fwd(x, gate, r_g, o_mla, mla_p, g_g, w_out_b)
    saved = dict(x=x, shift=shift, scale=scale, gate=gate, nw=nw, qnw=qnw, kvnw=kvnw, w_g2p=w_g2p, b_g2=b_g2,
                 gnw=gnw, ret_p=ret_p, mla_p=mla_p, gla_p=gla_p, h=h, r_g=r_g, r_raw=r_raw, r_st=r_st, q=q, k=k,
                 v=v, kt=kt, o_mla=o_mla, lse=lse, g_g=g_g, g_raw=g_raw, g_st=g_st, y=y, mm=mm,
                 w_in_a=w_in_a, w_uq_a=w_uq_a, w_ukv_a=w_ukv_a, w_out_b=w_out_b)
    return x_new, saved


def _layer_bwd(dx, s, tabs, early_grads=None):
    B, S, D = dx.shape
    cr, sr, cm, sm = tabs
    d_r, d_mm, d_g, dw_out, d_gate = _out_bwd(dx, s["y"], s["gate"], s["r_g"], s["mm"], s["g_g"], s["w_out_b"])
    d_ret = _ret_bwd(s["ret_p"], cr, sr, s["r_raw"], s["r_st"], d_r)
    do, d_mz, dl = _mla_gate_bwd(d_mm, s["o_mla"], s["mla_p"])
    dq, dk, dv = _mla_attn_bwd(s["q"], s["k"], s["v"], s["kt"], do, s["lse"], dl)
    d_mla, dw_uq, dw_ukv, d_qnw, d_kvnw = _mla_prep_bwd(
        s["mla_p"], cm, sm, s["qnw"], s["kvnw"], s["w_uq_a"], s["w_ukv_a"], dq, dk, dv)
    gnw = s["gnw"] if early_grads is None else s["gnw"] + early_grads(dw_out, dw_uq, dw_ukv)
    d_gla, dw_g2p, db_g2, d_gnw = _gla_bwd(s["gla_p"], s["w_g2p"], s["b_g2"], gnw, s["g_raw"], s["g_st"], d_g)
    dx, d_shift, d_scale, d_nw = _proj_bwd_x(s["x"], s["shift"], s["scale"], s["nw"], s["w_in_a"],
                                             d_ret, d_mla, d_mz, d_gla, dx)
    dw_in = _proj_bwd_w(s["h"], d_ret, d_mla, d_mz, d_gla)
    grads = dict(
        d_mod=jnp.concatenate([d_shift, d_scale, d_gate], axis=2).reshape(B, 3 * D),
        norm_w=d_nw.reshape(D), mla_q_norm=d_qnw.reshape(256), mla_kv_norm=d_kvnw.reshape(128),
        gla_w_g2=dw_g2p[0:16], gla_b_g2=db_g2.reshape(128), gla_norm256=d_gnw.reshape(256),
        w_in_a=dw_in, w_uq_a=dw_uq, w_ukv_a=dw_ukv, w_out=dw_out)
    return dx, grads


def _exchange(arrs, gather, name):
    n = len(arrs)
    out_shape = [jax.ShapeDtypeStruct(((N_DEV,) + a.shape) if g else a.shape, a.dtype)
                 for a, g in zip(arrs, gather)]

    def body(*refs):
        ins, outs = refs[:n], refs[n:2 * n]
        send_sems, recv_sems, local_sems = refs[2 * n:]
        ix, iy, ic = lax.axis_index("x"), lax.axis_index("y"), lax.axis_index("c")
        me = 4 * ix + 2 * iy + ic
        copies = []
        for a in range(n):
            mine = ins[a] if gather[a] else ins[a].at[me]
            loc = pltpu.make_async_copy(mine, outs[a].at[me], local_sems.at[a])
            loc.start()
            copies.append(loc)
            for d in range(1, N_DEV):
                px = 1 - ix if d & 4 else ix
                py = 1 - iy if d & 2 else iy
                pc = 1 - ic if d & 1 else ic
                src = ins[a] if gather[a] else ins[a].at[4 * px + 2 * py + pc]
                cp = pltpu.make_async_remote_copy(
                    src_ref=src, dst_ref=outs[a].at[me], send_sem=send_sems.at[a, d - 1],
                    recv_sem=recv_sems.at[a, d - 1], device_id=(px, py, pc), device_id_type=pl.DeviceIdType.MESH)
                cp.start()
                copies.append(cp)
        for cp in copies:
            cp.wait()

    any_spec = pl.BlockSpec(memory_space=pl.ANY)
    outs = pl.pallas_call(
        body, name=name, in_specs=[any_spec] * n, out_specs=[any_spec] * n, out_shape=out_shape,
        scratch_shapes=[pltpu.SemaphoreType.DMA((n, N_DEV - 1)), pltpu.SemaphoreType.DMA((n, N_DEV - 1)),
                        pltpu.SemaphoreType.DMA((n,))],
    )(*arrs)
    return list(outs)


def _peers(ix, iy, ic):
    out = []
    for d in range(1, N_DEV):
        px = 1 - ix if d & 4 else ix
        py = 1 - iy if d & 2 else iy
        pc = 1 - ic if d & 1 else ic
        out.append((d - 1, (px, py, pc), 4 * px + 2 * py + pc))
    return out


def _exchange_start(arrs, gather, name, after=None):
    n = len(arrs)
    lands = [lax.empty(((N_DEV,) + a.shape) if g else a.shape, a.dtype) for a, g in zip(arrs, gather)]
    extra = [] if after is None else [after]

    def body(*refs):
        ins, land_refs = refs[:n], refs[n:2 * n]
        send_sems, recv_sems = refs[2 * n + len(extra)], refs[2 * n + len(extra) + 1]
        token = refs[-1]
        ix, iy, ic = lax.axis_index("x"), lax.axis_index("y"), lax.axis_index("c")
        me = 4 * ix + 2 * iy + ic
        for a in range(n):
            for k, peer, peer_idx in _peers(ix, iy, ic):
                pltpu.make_async_remote_copy(
                    src_ref=ins[a] if gather[a] else ins[a].at[peer_idx], dst_ref=land_refs[a].at[me],
                    send_sem=send_sems.at[7 * a + k], recv_sem=recv_sems.at[7 * a + k], device_id=peer,
                    device_id_type=pl.DeviceIdType.MESH).start()
        token[...] = jnp.zeros_like(token)

    hbm = pl.BlockSpec(memory_space=pltpu.HBM)
    sem = pl.BlockSpec(memory_space=pltpu.SEMAPHORE)
    held = [pltpu.with_memory_space_constraint(a, pltpu.HBM) for a in list(arrs) + lands]
    outs = pl.pallas_call(
        body, name=name,
        out_shape=(pltpu.SemaphoreType.DMA((7 * n,)), pltpu.SemaphoreType.DMA((7 * n,)),
                   *[pltpu.HBM(a.shape, a.dtype) for a in held], jax.ShapeDtypeStruct((8, 128), F32)),
        in_specs=[hbm] * (2 * n) + [pl.BlockSpec(memory_space=pl.ANY)] * len(extra),
        out_specs=(sem, sem, *[hbm] * (2 * n), pl.BlockSpec(memory_space=pltpu.VMEM)),
        input_output_aliases={a: 2 + a for a in range(2 * n)},
        compiler_params=pltpu.CompilerParams(has_side_effects=pltpu.SideEffectType.DATAFLOW_SIDE_EFFECTING),
    )(*held, *extra)
    return dict(send=outs[0], recv=outs[1], srcs=list(outs[2:2 + n]), lands=list(outs[2 + n:2 + 2 * n]),
                token=outs[-1], gather=list(gather))


def _exchange_wait(flight, after, me, name):
    n = len(flight["srcs"])
    gather = flight["gather"]

    def body(*refs):
        srcs, land_refs = refs[:n], refs[n:2 * n]
        send_sems, recv_sems = refs[2 * n], refs[2 * n + 1]
        ix, iy, ic = lax.axis_index("x"), lax.axis_index("y"), lax.axis_index("c")
        mine = 4 * ix + 2 * iy + ic
        for a in range(n):
            for k, peer, peer_idx in _peers(ix, iy, ic):
                cp = pltpu.make_async_remote_copy(
                    src_ref=srcs[a] if gather[a] else srcs[a].at[peer_idx], dst_ref=land_refs[a].at[mine],
                    send_sem=send_sems.at[7 * a + k], recv_sem=recv_sems.at[7 * a + k], device_id=peer,
                    device_id_type=pl.DeviceIdType.MESH)
                cp.wait_send()
                cp.wait_recv()

    hbm = pl.BlockSpec(memory_space=pltpu.HBM)
    sem = pl.BlockSpec(memory_space=pltpu.SEMAPHORE)
    held = flight["srcs"] + flight["lands"]
    outs = pl.pallas_call(
        body, name=name, out_shape=tuple(pltpu.HBM(a.shape, a.dtype) for a in held),
        in_specs=[hbm] * (2 * n) + [sem, sem, pl.BlockSpec(memory_space=pl.ANY)], out_specs=tuple([hbm] * (2 * n)),
        input_output_aliases={a: a for a in range(2 * n)},
        compiler_params=pltpu.CompilerParams(has_side_effects=pltpu.SideEffectType.DATAFLOW_SIDE_EFFECTING),
    )(*held, flight["send"], flight["recv"], after)
    got = []
    for a in range(n):
        src, land = outs[a], outs[n + a]
        own = src if gather[a] else lax.dynamic_index_in_dim(src, me, axis=0, keepdims=False)
        got.append(lax.dynamic_update_index_in_dim(land, own, me, axis=0))
    return got


def _ada_fwd(c_all, ada_w, ada_b_cols):
    nb, D = c_all.shape
    cols = ada_w.shape[2]

    def body(c_ref, w_ref, b_ref, out_ref):
        ca = _silu(c_ref[...])
        for l in range(DEPTH):
            out_ref[l] = _mm(ca, w_ref[l]) + b_ref[l:l + 1, :]

    return pl.pallas_call(
        body, name="ada_fwd", out_shape=jax.ShapeDtypeStruct((DEPTH, nb, cols), F32),
        in_specs=[pl.BlockSpec(memory_space=pltpu.VMEM)] * 3, out_specs=pl.BlockSpec(memory_space=pltpu.VMEM),
        compiler_params=pltpu.CompilerParams(vmem_limit_bytes=32 * VMEM_MB),
    )(c_all, ada_w, ada_b_cols)


def _ada_bwd(c_all, d_mod_cols):
    nb, D = c_all.shape
    cols = d_mod_cols.shape[2]

    def body(c_ref, dm_ref, out_ref):
        ca = _silu(c_ref[...])
        for l in range(DEPTH):
            out_ref[l] = _mm_tn(ca, dm_ref[l])

    return pl.pallas_call(
        body, name="ada_bwd", out_shape=jax.ShapeDtypeStruct((DEPTH, D, cols), F32),
        in_specs=[pl.BlockSpec(memory_space=pltpu.VMEM)] * 2, out_specs=pl.BlockSpec(memory_space=pltpu.VMEM),
        compiler_params=pltpu.CompilerParams(vmem_limit_bytes=32 * VMEM_MB),
    )(c_all, d_mod_cols)


def _sum_adamw(parts, w, m, v, name):
    P, R, C = parts.shape
    tr = 256 if (R % 256 == 0 and R > 256) else R

    def body(p_ref, w_ref, m_ref, v_ref, g_ref, d_ref, nm_ref, nv_ref):
        g = p_ref[0].astype(F32)
        for k in range(1, P):
            g = g + p_ref[k].astype(F32)
        g_ref[...] = g
        nm = ADAM_B1 * m_ref[...] + (1.0 - ADAM_B1) * g
        nv = ADAM_B2 * v_ref[...] + (1.0 - ADAM_B2) * (g * g)
        nm_ref[...] = nm
        nv_ref[...] = nv
        m_hat = nm / (1.0 - ADAM_B1 ** ADAM_STEP)
        v_hat = nv / (1.0 - ADAM_B2 ** ADAM_STEP)
        d_ref[...] = -ADAM_LR * (m_hat / (jnp.sqrt(v_hat) + ADAM_EPS) + ADAM_WD * w_ref[...])

    blk = pl.BlockSpec((tr, C), lambda i: (i, 0))
    shp = jax.ShapeDtypeStruct((R, C), F32)
    return pl.pallas_call(
        body, name=name, grid=(R // tr,),
        in_specs=[pl.BlockSpec((P, tr, C), lambda i: (0, i, 0)), blk, blk, blk],
        out_specs=[blk, blk, blk, blk], out_shape=[shp, shp, shp, shp],
        compiler_params=_cp(("parallel",)),
    )(parts, w, m, v)


def _sum_adamw_layer(parts, w, m, v, layer, name, prev=None, after=None):
    P, R, C = parts.shape
    tr = 256 if (R % 256 == 0 and R > 256) else R

    def body(p_ref, w_ref, m_ref, v_ref, *rest):
        g_ref, d_ref, nm_ref, nv_ref = rest[-4:]
        g = p_ref[0].astype(F32)
        for k in range(1, P):
            g = g + p_ref[k].astype(F32)
        g_ref[0] = g
        nm = ADAM_B1 * m_ref[0] + (1.0 - ADAM_B1) * g
        nv = ADAM_B2 * v_ref[0] + (1.0 - ADAM_B2) * (g * g)
        nm_ref[0] = nm
        nv_ref[0] = nv
        m_hat = nm / (1.0 - ADAM_B1 ** ADAM_STEP)
        v_hat = nv / (1.0 - ADAM_B2 ** ADAM_STEP)
        d_ref[0] = -ADAM_LR * (m_hat / (jnp.sqrt(v_hat) + ADAM_EPS) + ADAM_WD * w_ref[0])

    blk = pl.BlockSpec((1, tr, C), lambda i: (layer, i, 0))
    shp = jax.ShapeDtypeStruct(w.shape, F32)
    in_specs = [pl.BlockSpec((P, tr, C), lambda i: (0, i, 0)), blk, blk, blk]
    args = [parts, w, m, v]
    aliases = {}
    if prev is not None:
        in_specs += [pl.BlockSpec(memory_space=pl.ANY)] * 4
        args += list(prev)
        aliases = {4 + k: k for k in range(4)}
    if after is not None:
        in_specs.append(pl.BlockSpec(memory_space=pl.ANY))
        args.append(after)
    return list(pl.pallas_call(
        body, name=name, grid=(R // tr,), in_specs=in_specs, out_specs=[blk] * 4, out_shape=[shp] * 4,
        input_output_aliases=aliases, compiler_params=_cp(("parallel",)),
    )(*args))


SMALL = ["norm_w", "mla_q_norm", "mla_kv_norm", "gla_w_g2", "gla_b_g2", "gla_norm", "final_norm"]


SMALL_ROWS = 72


def _pack_small(loss, part):
    flat = [jnp.pad(loss.reshape(1), (0, 127))] + [part[n].reshape(-1) for n in SMALL]
    used = sum(f.shape[0] for f in flat)
    flat.append(jnp.zeros((SMALL_ROWS * 128 - used,), F32))
    return jnp.concatenate(flat).reshape(SMALL_ROWS, 128)


def _small_adamw(packed_parts, w, m, v):
    n = len(w)

    def body(*refs):
        p_ref = refs[0]
        w_refs, m_refs, v_refs = refs[1:1 + n], refs[1 + n:1 + 2 * n], refs[1 + 2 * n:1 + 3 * n]
        outs, acc = refs[1 + 3 * n:-1], refs[-1]
        total = p_ref[0]
        for k in range(1, N_DEV):
            total = total + p_ref[k]
        acc[...] = total
        outs[0][...] = acc[0:1, :]
        r0 = 1
        for i in range(n):
            shp = w_refs[i].shape
            if len(shp) == 3:
                g = acc[r0:r0 + shp[0] * shp[1], :].reshape(shp)
                r0 += shp[0] * shp[1]
            elif shp[1] < 128:
                g = acc[r0:r0 + shp[0], 0:shp[1]]
                r0 += shp[0]
            else:
                k = shp[1] // 128
                g = jnp.concatenate(
                    [jnp.concatenate([acc[r0 + l * k + j:r0 + l * k + j + 1, :] for j in range(k)], axis=1)
                     for l in range(shp[0])], axis=0)
                r0 += shp[0] * k
            nm = ADAM_B1 * m_refs[i][...] + (1.0 - ADAM_B1) * g
            nv = ADAM_B2 * v_refs[i][...] + (1.0 - ADAM_B2) * (g * g)
            m_hat = nm / (1.0 - ADAM_B1 ** ADAM_STEP)
            v_hat = nv / (1.0 - ADAM_B2 ** ADAM_STEP)
            outs[1 + 4 * i][...] = g
            outs[2 + 4 * i][...] = -ADAM_LR * (m_hat / (jnp.sqrt(v_hat) + ADAM_EPS) + ADAM_WD * w_refs[i][...])
            outs[3 + 4 * i][...] = nm
            outs[4 + 4 * i][...] = nv

    vmem = pl.BlockSpec(memory_space=pltpu.VMEM)
    out_shape = [jax.ShapeDtypeStruct((1, 128), F32)]
    for a in w:
        out_shape += [jax.ShapeDtypeStruct(a.shape, F32)] * 4
    outs = pl.pallas_call(
        body, name="adamw_small", in_specs=[vmem] * (1 + 3 * n), out_specs=[vmem] * (1 + 4 * n), out_shape=out_shape,
        scratch_shapes=[pltpu.VMEM((SMALL_ROWS, 128), F32)],
    )(packed_parts, *w, *m, *v)
    return outs[0], [outs[1 + 4 * i:5 + 4 * i] for i in range(n)]


WEIGHTS = ["norm_w", "ada_w", "ada_b", "w_in", "mla_q_norm", "w_uq", "mla_kv_norm", "w_ukv", "gla_w_g2",
           "gla_b_g2", "gla_norm", "w_out", "final_norm"]


def kernel(x, c, positions, norm_w, ada_w, ada_b, w_in, mla_q_norm, w_uq, mla_kv_norm, w_ukv, gla_w_g2, gla_b_g2, gla_norm, w_out, final_norm, loss_target, m_norm_w, m_ada_w, m_ada_b, m_w_in, m_mla_q_norm, m_w_uq, m_mla_kv_norm, m_w_ukv, m_gla_w_g2, m_gla_b_g2, m_gla_norm, m_w_out, m_final_norm, v_norm_w, v_ada_w, v_ada_b, v_w_in, v_mla_q_norm, v_w_uq, v_mla_kv_norm, v_w_ukv, v_gla_w_g2, v_gla_b_g2, v_gla_norm, v_w_out, v_final_norm):
    w = dict(norm_w=norm_w, ada_w=ada_w, ada_b=ada_b, w_in=w_in, mla_q_norm=mla_q_norm, w_uq=w_uq,
             mla_kv_norm=mla_kv_norm, w_ukv=w_ukv, gla_w_g2=gla_w_g2, gla_b_g2=gla_b_g2, gla_norm=gla_norm,
             w_out=w_out, final_norm=final_norm)
    m = dict(norm_w=m_norm_w, ada_w=m_ada_w, ada_b=m_ada_b, w_in=m_w_in, mla_q_norm=m_mla_q_norm, w_uq=m_w_uq,
             mla_kv_norm=m_mla_kv_norm, w_ukv=m_w_ukv, gla_w_g2=m_gla_w_g2, gla_b_g2=m_gla_b_g2,
             gla_norm=m_gla_norm, w_out=m_w_out, final_norm=m_final_norm)
    v = dict(norm_w=v_norm_w, ada_w=v_ada_w, ada_b=v_ada_b, w_in=v_w_in, mla_q_norm=v_mla_q_norm, w_uq=v_w_uq,
             mla_kv_norm=v_mla_kv_norm, w_ukv=v_w_ukv, gla_w_g2=v_gla_w_g2, gla_b_g2=v_gla_b_g2,
             gla_norm=v_gla_norm, w_out=v_w_out, final_norm=v_final_norm)
    B, S, D = x.shape
    me = 4 * lax.axis_index("x") + 2 * lax.axis_index("y") + lax.axis_index("c")
    ada_cols = ada_w.shape[2]
    cast = lambda a: a.astype(_MXU)

    sharded = ["w_in", "w_uq", "w_ukv", "w_out"]

    whole_cols = lambda a: jnp.transpose(a, (1, 0, 2)).reshape(a.shape[1], -1)
    whole_in = lambda blk: _arrange_w_in(whole_cols(blk))
    whole_rest = lambda blks: (_arrange_w_uq(whole_cols(blks[0])), _arrange_w_ukv(whole_cols(blks[1])),
                               blks[2].reshape(D, D))
    col_blocks = lambda a: jnp.transpose(a.reshape(a.shape[0], N_DEV, -1), (1, 0, 2)).astype(jnp.bfloat16)
    blocks_in = lambda dw_in_a: col_blocks(_unarrange_w_in(dw_in_a))
    blocks_rest = lambda dw_out, dw_uq_a, dw_ukv_a: [
        col_blocks(_unarrange_w_uq(dw_uq_a)), col_blocks(_unarrange_w_ukv(dw_ukv_a)),
        dw_out.reshape(N_DEV, D // N_DEV, D).astype(jnp.bfloat16)]

    (c_g,) = _exchange([c], [True], "gather_c")
    c_all = c_g.reshape(N_DEV * B, D)

    ada_b_cols = lax.dynamic_slice(ada_b, (0, me * ada_cols), (DEPTH, ada_cols))
    mod_cols = _ada_fwd(c_all, ada_w, ada_b_cols)
    mod_send = jnp.transpose(mod_cols.reshape(DEPTH, N_DEV, B, ada_cols), (1, 0, 2, 3))
    (mod_recv,) = _exchange([mod_send], [False], "scatter_mod")
    mod = jnp.transpose(mod_recv, (1, 2, 0, 3)).reshape(DEPTH, B, 3 * D)

    flight_i = _exchange_start([cast(w_in[0])], [True], "gather_start_first", after=mod)
    flight_r = _exchange_start([cast(w[n][0]) for n in sharded[1:]], [True] * 3, "gather_start_layer0",
                               after=flight_i["token"])
    flight_w = _exchange_start([cast(w[n][1]) for n in sharded], [True] * 4, "gather_start_layer1",
                               after=flight_r["token"])
    small_w = {n: w[n] for n in SMALL}
    layer_small = lambda l: {n: a[l] for n, a in small_w.items() if n != "final_norm"}
    tabs = _rope_tables(positions.reshape(B, S, 1), flight_w["token"][0, 0])
    late0 = lambda after: whole_rest(_exchange_wait(flight_r, after, me, "gather_wait_layer0"))
    (w_in0_g,) = _exchange_wait(flight_i, tabs[0], me, "gather_wait_first")
    x1, saved0 = _layer_fwd(x, tabs, mod[0], layer_small(0), whole_in(w_in0_g), late_weights=late0)
    got1 = _exchange_wait(flight_w, x1, me, "gather_wait_layer1")
    x2, saved1 = _layer_fwd(x1, tabs, mod[1], layer_small(1), whole_in(got1[0]), *whole_rest(got1[1:]))
    dx, loss, d_fw = _final_loss(x2, final_norm.reshape(1, D), loss_target)

    dx, g1 = _layer_bwd(dx, saved1, tabs)
    flight_g = _exchange_start([blocks_in(g1["w_in_a"])] + blocks_rest(g1["w_out"], g1["w_uq_a"], g1["w_ukv_a"]),
                               [False] * 4, "grads_start_layer1")
    flights = {}

    def early0(dw_out, dw_uq_a, dw_ukv_a):
        flights["rest0"] = _exchange_start(blocks_rest(dw_out, dw_uq_a, dw_ukv_a), [False] * 3, "grads_start_layer0")
        return flights["rest0"]["token"][0, 0]

    saved0 = dict(saved0, gate=saved0["gate"] + flight_g["token"][0, 0])
    grad_x, g0 = _layer_bwd(dx, saved0, tabs, early_grads=early0)
    parts1 = _exchange_wait(flight_g, grad_x, me, "grads_wait_layer1")
    rest0 = _exchange_wait(flights["rest0"], g0["w_in_a"], me, "grads_wait_layer0")

    both = lambda n: jnp.stack([g0[n], g1[n]])
    d_mod = both("d_mod")
    part = dict(norm_w=both("norm_w"), mla_q_norm=both("mla_q_norm"), mla_kv_norm=both("mla_kv_norm"),
                gla_w_g2=both("gla_w_g2"), gla_b_g2=both("gla_b_g2"), gla_norm=both("gla_norm256")[:, 0:128],
                final_norm=d_fw)
    flight_l = _exchange_start([d_mod, _pack_small(loss, part), blocks_in(g0["w_in_a"])], [True, True, False],
                               "exchange_start_last")
    res = {}
    behind = flight_l["token"]
    for a, name in enumerate(sharded):
        res[name] = _sum_adamw_layer(parts1[a], w[name], m[name], v[name], 1, "adamw_%s_layer1" % name, after=behind)
        behind = res[name][1]
    for a, name in enumerate(sharded[1:]):
        res[name] = _sum_adamw_layer(rest0[a], w[name], m[name], v[name], 0, "adamw_%s_layer0" % name,
                                     prev=res[name], after=behind)
        behind = res[name][1]
    d_mod_g, small_g, in0 = _exchange_wait(flight_l, behind, me, "exchange_wait_last")
    res["w_in"] = _sum_adamw_layer(in0, w_in, m_w_in, v_w_in, 0, "adamw_w_in_layer0", prev=res["w_in"])

    d_mod_all = jnp.transpose(d_mod_g, (1, 0, 2, 3)).reshape(DEPTH, N_DEV * B, 3 * D)
    d_mod_cols = lax.dynamic_slice(d_mod_all, (0, 0, me * ada_cols), (DEPTH, N_DEV * B, ada_cols))
    g_ada_w = _ada_bwd(c_all, d_mod_cols)

    def update(name, parts2d):
        shp = w[name].shape
        two = lambda a: a.reshape(parts2d.shape[1:])
        out = _sum_adamw(parts2d, two(w[name]), two(m[name]), two(v[name]), "adamw_" + name)
        res[name] = [o.reshape(shp) for o in out]

    update("ada_w", g_ada_w.reshape(1, DEPTH * D, ada_cols))
    update("ada_b", jnp.transpose(d_mod_g, (0, 2, 1, 3)).reshape(N_DEV * B, DEPTH * 3 * D // 128, 128))
    row = lambda a: a.reshape(1, D) if a.ndim == 1 else a
    loss_sum, small_out = _small_adamw(small_g, [row(w[n]) for n in SMALL], [row(m[n]) for n in SMALL],
                                       [row(v[n]) for n in SMALL])
    for n, outs in zip(SMALL, small_out):
        res[n] = [o.reshape(w[n].shape) for o in outs]
    loss_out = loss_sum[0, 0]
    return (loss_out, grad_x, *[res[n][0] for n in WEIGHTS], *[res[n][1] for n in WEIGHTS],
            *[res[n][2] for n in WEIGHTS], *[res[n][3] for n in WEIGHTS])
```

```python
import functools
import math

import numpy as np
import jax
import jax.numpy as jnp
from jax import lax
from jax.experimental import pallas as pl
from jax.experimental.pallas import tpu as pltpu

F32 = jnp.float32
_MXU = jnp.bfloat16

D_MODEL = 1024
DEPTH = 2
CHUNK = 64
EPS = 1e-6
ROPE_THETA = 10000.0
N_DEV = 8

MLA_SCALE = 96.0 ** -0.5
RET_KSCALE = 64.0 ** -0.5
GLA_KSCALE = 32.0 ** -0.5
GLA_TAU = 16.0

ADAM_LR = 0.001
ADAM_B1 = 0.9
ADAM_B2 = 0.999
ADAM_EPS = 1e-08
ADAM_WD = 0.01
ADAM_STEP = 10

RET_W, MLA_W, GLA_W = 1024, 1024, 896
ARR_W = RET_W + MLA_W + GLA_W
VMEM_MB = 1024 * 1024


def _cp(sem, vmem_mb=48):
    return pltpu.CompilerParams(dimension_semantics=sem, vmem_limit_bytes=vmem_mb * VMEM_MB)


def _mm(a, b):
    return jnp.dot(a.astype(_MXU), b.astype(_MXU), preferred_element_type=F32)


def _mm_nt(a, b):
    return lax.dot_general(a.astype(_MXU), b.astype(_MXU), (((1,), (1,)), ((), ())),
                           preferred_element_type=F32)


def _mm_tn(a, b):
    return lax.dot_general(a.astype(_MXU), b.astype(_MXU), (((0,), (0,)), ((), ())),
                           preferred_element_type=F32)


def _mm_f32(a, b):
    return jnp.dot(a, b, precision=lax.Precision.HIGHEST, preferred_element_type=F32)


def _sig(z):
    return 1.0 / (1.0 + jnp.exp(-z))


def _silu(z):
    return z * _sig(z)


def _dsilu(z):
    s = _sig(z)
    return s * (1.0 + z * (1.0 - s))


def _full(shape):
    nd = len(shape)
    return pl.BlockSpec(shape, lambda *_: (0,) * nd)


def _qk_perm(blk):
    r = blk.shape[0]
    return jnp.transpose(blk.reshape(r, 4, 2, 32), (0, 2, 1, 3)).reshape(r, 256)


def _qk_unperm(blk):
    r = blk.shape[0]
    return jnp.transpose(blk.reshape(r, 2, 4, 32), (0, 2, 1, 3)).reshape(r, 256)


def _arrange_w_in(w):
    z = lambda n: jnp.zeros((w.shape[0], n), w.dtype)
    ret = [_qk_perm(w[:, 0:256]), _qk_perm(w[:, 256:512]), w[:, 512:768], w[:, 768:1024]]
    mla = [w[:, 1024:1280], w[:, 1280:1408], z(64), w[:, 1408:1440], z(32), w[:, 1440:1952]]
    gla = [w[:, 1952:2080], w[:, 2080:2208], w[:, 2208:2464], w[:, 2464:2480], z(112), w[:, 2480:2736]]
    return jnp.concatenate(ret + mla + gla, axis=1)


def _unarrange_w_in(a):
    m, g = RET_W, RET_W + MLA_W
    parts = [_qk_unperm(a[:, 0:256]), _qk_unperm(a[:, 256:512]), a[:, 512:1024],
             a[:, m:m + 384], a[:, m + 448:m + 480], a[:, m + 512:m + 1024],
             a[:, g:g + 528], a[:, g + 640:g + 896]]
    return jnp.concatenate(parts, axis=1)


def _arrange_w_uq(w):
    return jnp.pad(w.reshape(256, 8, 96), ((0, 0), (0, 0), (0, 32))).reshape(256, 1024)


def _unarrange_w_uq(a):
    return a.reshape(256, 8, 128)[:, :, :96].reshape(256, 768)


def _arrange_w_ukv(w):
    r = w.reshape(128, 8, 128)
    k = jnp.pad(r[:, :, :64], ((0, 0), (0, 0), (0, 64))).reshape(128, 1024)
    return jnp.concatenate([k, r[:, :, 64:].reshape(128, 512)], axis=1)


def _unarrange_w_ukv(a):
    k = a[:, :1024].reshape(128, 8, 128)[:, :, :64]
    v = a[:, 1024:].reshape(128, 8, 64)
    return jnp.concatenate([k, v], axis=2).reshape(128, 1024)


def _rope_tables(pos3, zero=0.0):
    B, S, _ = pos3.shape
    ts = min(S, 512)
    inv32 = (np.float32(ROPE_THETA) ** (-(np.arange(32, dtype=np.float32) / 32))).astype(np.float32)
    inv16 = (np.float32(ROPE_THETA) ** (-(np.arange(16, dtype=np.float32) / 16))).astype(np.float32)
    inv = np.zeros((1, 128), np.float32)
    inv[0, 0:32] = inv32
    inv[0, 32:48] = inv16

    def body(pos_ref, inv_ref, cr, sr, cm, sm):
        ang = pos_ref[0].astype(F32) * inv_ref[...]
        lane = lax.broadcasted_iota(jnp.int32, (1, 128), 1)

        def every_head(x):
            y = jnp.where(lane < 32, x, pltpu.roll(x, 32, 1))
            return jnp.where(lane < 64, y, pltpu.roll(y, 64, 1))

        def rotary_pair(x, fill):
            return jnp.where((lane >= 64) & (lane < 80), pltpu.roll(x, 32, 1),
                             jnp.where((lane >= 80) & (lane < 96), pltpu.roll(x, 48, 1), fill))

        c, s = jnp.cos(ang), jnp.sin(ang)
        cr[0] = every_head(c)
        sr[0] = every_head(s)
        cm[0] = rotary_pair(c, 1.0)
        sm[0] = rotary_pair(s, 0.0)

    tab = jax.ShapeDtypeStruct((B, S, 128), F32)
    blk = pl.BlockSpec((1, ts, 128), lambda b, i: (b, i, 0))
    return pl.pallas_call(
        body, name="rope_tables", grid=(B, S // ts),
        in_specs=[pl.BlockSpec((1, ts, 1), lambda b, i: (b, i, 0)), _full((1, 128))],
        out_specs=[blk, blk, blk, blk], out_shape=[tab, tab, tab, tab],
        compiler_params=_cp(("parallel", "parallel")),
    )(pos3, jnp.asarray(inv) + zero)


def _rope128(x, cos, sin):
    lane = lax.broadcasted_iota(jnp.int32, (1, 128), 1)
    rp = pltpu.roll(x, 16, 1)
    rm = pltpu.roll(x, 112, 1)
    return x * cos + jnp.where(lane < 80, -rm, rp) * sin


def _rope128_t(d, cos, sin):
    lane = lax.broadcasted_iota(jnp.int32, (1, 128), 1)
    y = d * sin
    yp = pltpu.roll(y, 16, 1)
    ym = pltpu.roll(y, 112, 1)
    return d * cos + jnp.where(lane < 64, 0.0, jnp.where(lane < 80, ym, jnp.where(lane < 96, -yp, 0.0)))


def _proj_fwd(x, shift, scale, nw, w_arr):
    B, S, D = x.shape
    tm = min(S, 512)

    def body(x_ref, sh_ref, sc_ref, nw_ref, w_ref, ret_ref, mla_ref, gla_ref, h_ref):
        xv = x_ref[0]
        rstd = lax.rsqrt(jnp.mean(xv * xv, axis=-1, keepdims=True) + EPS)
        h = (xv * rstd * nw_ref[...]) * (1.0 + sc_ref[0]) + sh_ref[0]
        hb = h.astype(_MXU)
        h_ref[0] = hb
        ret_ref[0] = jnp.dot(hb, w_ref[:, 0:RET_W], preferred_element_type=F32).astype(_MXU)
        mla_ref[0] = jnp.dot(hb, w_ref[:, RET_W:RET_W + MLA_W], preferred_element_type=F32).astype(_MXU)
        gla_ref[0] = jnp.dot(hb, w_ref[:, RET_W + MLA_W:ARR_W], preferred_element_type=F32).astype(_MXU)

    tok = lambda w: pl.BlockSpec((1, tm, w), lambda b, i: (b, i, 0))
    per_seq = pl.BlockSpec((1, 1, D), lambda b, i: (b, 0, 0))
    return pl.pallas_call(
        body, name="proj_fwd", grid=(B, S // tm),
        in_specs=[tok(D), per_seq, per_seq, _full((1, D)), _full((D, ARR_W))],
        out_specs=[tok(RET_W), tok(MLA_W), tok(GLA_W), tok(D)],
        out_shape=[jax.ShapeDtypeStruct((B, S, RET_W), _MXU), jax.ShapeDtypeStruct((B, S, MLA_W), _MXU),
                   jax.ShapeDtypeStruct((B, S, GLA_W), _MXU), jax.ShapeDtypeStruct((B, S, D), _MXU)],
        compiler_params=_cp(("parallel", "parallel")),
    )(x, shift, scale, nw, w_arr)


RET_L = 256


def _ret_consts(L):
    lg = np.log1p(-np.exp2(-5.0 - np.arange(4, dtype=np.float32))).astype(np.float32)
    i = np.arange(L)
    ci = i // CHUNK
    diff = (i[:, None] - i[None, :]).astype(np.float32)
    same = ci[:, None] == ci[None, :]
    past = ci[None, :] < ci[:, None]
    expo = np.where(same, np.abs(diff), np.where(past, diff, 0.0)).astype(np.float32)
    dec = np.where((same | past)[None], np.exp(lg[:, None, None] * expo[None]), 0.0).astype(np.float32)
    head = (np.arange(256) % 128) // 32
    qw = np.exp((i + 1.0)[:, None] * lg[head][None, :]).astype(np.float32)
    kw = np.exp((L - 1.0 - i)[:, None] * lg[head][None, :]).astype(np.float32)
    a_row = np.exp(np.float32(L) * lg[head])[None, :].astype(np.float32)
    return [jnp.asarray(t) for t in (dec.reshape(4 * L, L), qw, kw, a_row)]


def _ret_masks():
    lane = lax.broadcasted_iota(jnp.int32, (1, 256), 1)
    mh = [((lane % 128) // 32) == h for h in range(4)]
    mv = [(lane // 64) == h for h in range(4)]
    vi = lax.broadcasted_iota(jnp.int32, (256, 256), 0)
    ki = lax.broadcasted_iota(jnp.int32, (256, 256), 1)
    bd = (vi // 64) == ((ki % 128) // 32)
    return mh, mv, bd


def _ret_rope(p, cs, sn):
    q1, q2, k1, k2 = p[:, 0:128], p[:, 128:256], p[:, 256:384], p[:, 384:512]
    qr = jnp.concatenate([q1 * cs - q2 * sn, q2 * cs + q1 * sn], axis=1)
    kr = jnp.concatenate([k1 * cs - k2 * sn, k2 * cs + k1 * sn], axis=1) * RET_KSCALE
    return qr, kr


def _head_mean(x, mv, width):
    out = jnp.zeros_like(x)
    for m in mv:
        s = jnp.sum(jnp.where(m, x, 0.0), axis=-1, keepdims=True) * (1.0 / width)
        out = jnp.where(m, s, out)
    return out


def _stack_heads(x, masks):
    return jnp.concatenate([jnp.where(m, x, 0.0) for m in masks], axis=0)


def _fold_heads(xs, masks, L):
    out = jnp.where(masks[0], xs[0:L], 0.0)
    for h in range(1, 4):
        out = out + jnp.where(masks[h], xs[h * L:(h + 1) * L], 0.0)
    return out


RET_G = 2


def _ret_fwd(ret_p, cos, sin):
    B, S, _ = ret_p.shape
    L = min(RET_L, S)
    NB = S // L
    G = min(RET_G, NB)
    NG = NB // G
    consts = _ret_consts(L)

    def body(p_ref, c_ref, s_ref, ds_ref, qw_ref, kw_ref, a_ref, out_ref, raw_ref, st_ref, st_sc):
        @pl.when(pl.program_id(1) == 0)
        def _():
            st_sc[...] = jnp.zeros_like(st_sc)

        mh, mv, bd = _ret_masks()
        cs_ = range(G)
        rows = [slice(c * L, (c + 1) * L) for c in cs_]
        ps = [p_ref[0, rows[c], :].astype(F32) for c in cs_]
        qk = [_ret_rope(ps[c], c_ref[0, rows[c], :], s_ref[0, rows[c], :]) for c in cs_]
        vs = [ps[c][:, 512:768] for c in cs_]
        a_s = [_mm_nt(_stack_heads(qk[c][0], mh), qk[c][1]) for c in cs_]
        upd = [_mm_tn(vs[c], qk[c][1] * kw_ref[...]) for c in cs_]
        o_s = [_mm(a_s[c] * ds_ref[...], vs[c]) for c in cs_]
        st = st_sc[...]
        inter = []
        for c in cs_:
            st_ref[0, c] = st
            inter.append(_mm_nt(qk[c][0] * qw_ref[...], st))
            st = st * a_ref[...] + jnp.where(bd, upd[c], 0.0)
        st_sc[...] = st
        for c in cs_:
            r = _fold_heads(o_s[c], mv, L) + inter[c]
            raw_ref[0, rows[c], :] = r
            rstd = lax.rsqrt(_head_mean(r * r, mv, 64.0) + EPS)
            out_ref[0, rows[c], :] = (r * rstd * _silu(ps[c][:, 768:1024])).astype(_MXU)

    tok = lambda w: pl.BlockSpec((1, G * L, w), lambda b, n: (b, n, 0))
    return pl.pallas_call(
        body, name="ret_fwd", grid=(B, NG),
        in_specs=[tok(RET_W), tok(128), tok(128), _full((4 * L, L)), _full((L, 256)), _full((L, 256)),
                  _full((1, 256))],
        out_specs=[tok(256), tok(256), pl.BlockSpec((1, G, 256, 256), lambda b, n: (b, n, 0, 0))],
        out_shape=[jax.ShapeDtypeStruct((B, S, 256), _MXU), jax.ShapeDtypeStruct((B, S, 256), F32),
                   jax.ShapeDtypeStruct((B, NB, 256, 256), F32)],
        scratch_shapes=[pltpu.VMEM((256, 256), F32)],
        compiler_params=_cp(("parallel", "arbitrary")),
    )(ret_p, cos, sin, *consts)


def _ret_bwd(ret_p, cos, sin, raw, states, d_mix):
    B, S, _ = ret_p.shape
    L = min(RET_L, S)
    NB = S // L
    G = 1
    NG = NB // G
    consts = _ret_consts(L)

    def body(p_ref, c_ref, s_ref, raw_ref, st_ref, dm_ref, ds_ref, qw_ref, kw_ref, a_ref, dp_ref, dst_sc):
        @pl.when(pl.program_id(1) == 0)
        def _():
            dst_sc[...] = jnp.zeros_like(dst_sc)

        mh, mv, bd = _ret_masks()
        qw, kw, dec = qw_ref[...], kw_ref[...], ds_ref[...]
        cs_ = range(G)
        rows = [slice(c * L, (c + 1) * L) for c in cs_]
        ps = [p_ref[0, rows[c], :].astype(F32) for c in cs_]
        tabs = [(c_ref[0, rows[c], :], s_ref[0, rows[c], :]) for c in cs_]
        qk = [_ret_rope(ps[c], *tabs[c]) for c in cs_]
        vs = [ps[c][:, 512:768] for c in cs_]
        qs = [_stack_heads(qk[c][0], mh) for c in cs_]
        a_s = [_mm_nt(qs[c], qk[c][1]) for c in cs_]
        dr, dz = [], []
        for c in cs_:
            r = raw_ref[0, rows[c], :]
            z = ps[c][:, 768:1024]
            rstd = lax.rsqrt(_head_mean(r * r, mv, 64.0) + EPS)
            rn = r * rstd
            dm = dm_ref[0, rows[c], :]
            d_rn = dm * _silu(z)
            dz.append(dm * rn * _dsilu(z))
            dr.append(rstd * (d_rn - rn * _head_mean(d_rn * rn, mv, 64.0)))
        do_s = [_stack_heads(dr[c], mv) for c in cs_]
        da_s = [_mm_nt(do_s[c], vs[c]) for c in cs_]
        sts = [st_ref[0, c] for c in cs_]
        dq_st = [_mm(dr[c], sts[c]) for c in cs_]
        dst_in = [_mm_tn(dr[c], qk[c][0] * qw) for c in cs_]
        dv = [_mm_tn(a_s[c] * dec, do_s[c]) for c in cs_]
        dqr, dkr = [], []
        for c in cs_:
            da = da_s[c] * dec
            dqr.append(_fold_heads(_mm(da, qk[c][1]), mh, L) + dq_st[c] * qw)
            dkr.append(_mm_tn(da, qs[c]))
        dst_next = dst_sc[...]
        for c in reversed(cs_):
            g = jnp.where(bd, dst_next, 0.0)
            dv[c] = dv[c] + _mm_nt(qk[c][1] * kw, g)
            dkr[c] = dkr[c] + _mm(vs[c], g) * kw
            dst_next = dst_next * a_ref[...] + jnp.where(bd, dst_in[c], 0.0)
        dst_sc[...] = dst_next
        for c in cs_:
            cs, sn = tabs[c]
            dk = dkr[c] * RET_KSCALE
            dq1, dq2 = dqr[c][:, 0:128], dqr[c][:, 128:256]
            dk1, dk2 = dk[:, 0:128], dk[:, 128:256]
            dp_ref[0, rows[c], :] = jnp.concatenate(
                [dq1 * cs + dq2 * sn, dq2 * cs - dq1 * sn, dk1 * cs + dk2 * sn, dk2 * cs - dk1 * sn, dv[c], dz[c]],
                axis=1).astype(_MXU)

    tok = lambda w: pl.BlockSpec((1, G * L, w), lambda b, i: (b, NG - 1 - i, 0))
    return pl.pallas_call(
        body, name="ret_bwd", grid=(B, NG),
        in_specs=[tok(RET_W), tok(128), tok(128), tok(256),
                  pl.BlockSpec((1, G, 256, 256), lambda b, i: (b, NG - 1 - i, 0, 0)), tok(256),
                  _full((4 * L, L)), _full((L, 256)), _full((L, 256)), _full((1, 256))],
        out_specs=tok(RET_W), out_shape=jax.ShapeDtypeStruct((B, S, RET_W), _MXU),
        scratch_shapes=[pltpu.VMEM((256, 256), F32)],
        compiler_params=_cp(("parallel", "arbitrary")),
    )(ret_p, cos, sin, raw, states, d_mix, *consts)


def _gla_masks():
    C = CHUNK
    lk = lax.broadcasted_iota(jnp.int32, (1, 128), 1)
    lv = lax.broadcasted_iota(jnp.int32, (1, 256), 1)
    mk = [(lk // 32) == h for h in range(4)]
    mv = [(lv // 64) == h for h in range(4)]
    vi = lax.broadcasted_iota(jnp.int32, (256, 128), 0)
    ki = lax.broadcasted_iota(jnp.int32, (256, 128), 1)
    bd = (vi // 64) == (ki // 32)
    ri = lax.broadcasted_iota(jnp.int32, (4 * C, C), 0) % C
    cj = lax.broadcasted_iota(jnp.int32, (4 * C, C), 1)
    lower = ri >= cj
    ti = lax.broadcasted_iota(jnp.int32, (C, C), 0)
    tj = lax.broadcasted_iota(jnp.int32, (C, C), 1)
    ltri = jnp.where(ti >= tj, 1.0, 0.0).astype(F32)
    utri = jnp.where(ti <= tj, 1.0, 0.0).astype(F32)
    return mk, mv, bd, lower, ltri, utri


def _log_sigmoid(x):
    return jnp.minimum(x, 0.0) - jnp.log(1.0 + jnp.exp(-jnp.abs(x)))


GLA_G = 8


def _gla_fwd(gla_p, w_g2p, b_g2, gnw):
    B, S, _ = gla_p.shape
    C = CHUNK
    NC = S // C
    G = min(GLA_G, NC)
    NG = NC // G

    def body(p_ref, w_ref, b_ref, gn_ref, out_ref, raw_ref, st_ref, st_sc):
        @pl.when(pl.program_id(1) == 0)
        def _():
            st_sc[...] = jnp.zeros_like(st_sc)

        mk, mv, bd, lower, ltri, _ = _gla_masks()
        cs = range(G)
        rows = [slice(c * C, (c + 1) * C) for c in cs]
        ps = [p_ref[0, rows[c], :].astype(F32) for c in cs]
        pre = [_mm(ps[c][:, 512:640], w_ref[...]) + b_ref[...] for c in cs]
        cum = [_mm_f32(ltri, _log_sigmoid(pre[c]) * (1.0 / GLA_TAU)) for c in cs]
        past, fut, upd, q_pos, a_row = [], [], [], [], []
        for c in cs:
            q = ps[c][:, 0:128]
            k = ps[c][:, 128:256] * GLA_KSCALE
            last = cum[c][C - 1:C, :]
            e_pos = jnp.exp(cum[c])
            e_neg = jnp.exp(-cum[c])
            q_pos.append(q * e_pos)
            a_row.append(jnp.exp(last))
            past.append(_mm_nt(_stack_heads(q_pos[c], mk), k * e_neg))
            fut.append(_mm_nt(_stack_heads(q * e_neg, mk), k * e_pos))
            upd.append(_mm_tn(ps[c][:, 256:512], k * jnp.exp(last - cum[c])))
        o_s = [_mm(jnp.where(lower, past[c], fut[c]), ps[c][:, 256:512]) for c in cs]
        st = st_sc[...]
        inter = []
        for c in cs:
            st_ref[0, c] = st
            inter.append(_mm_nt(q_pos[c], st))
            st = st * a_row[c] + jnp.where(bd, upd[c], 0.0)
        st_sc[...] = st
        for c in cs:
            g = _fold_heads(o_s[c], mv, C) + inter[c]
            raw_ref[0, rows[c], :] = g
            rstd = lax.rsqrt(_head_mean(g * g, mv, 64.0) + EPS)
            out_ref[0, rows[c], :] = (g * rstd * gn_ref[...] * _silu(ps[c][:, 640:896])).astype(_MXU)

    tok = lambda w: pl.BlockSpec((1, G * C, w), lambda b, n: (b, n, 0))
    return pl.pallas_call(
        body, name="gla_fwd", grid=(B, NG),
        in_specs=[tok(GLA_W), _full((128, 128)), _full((1, 128)), _full((1, 256))],
        out_specs=[tok(256), tok(256), pl.BlockSpec((1, G, 256, 128), lambda b, n: (b, n, 0, 0))],
        out_shape=[jax.ShapeDtypeStruct((B, S, 256), _MXU), jax.ShapeDtypeStruct((B, S, 256), F32),
                   jax.ShapeDtypeStruct((B, NC, 256, 128), F32)],
        scratch_shapes=[pltpu.VMEM((256, 128), F32)],
        compiler_params=_cp(("parallel", "arbitrary")),
    )(gla_p, w_g2p, b_g2, gnw)


def _gla_bwd(gla_p, w_g2p, b_g2, gnw, raw, states, d_mix):
    B, S, _ = gla_p.shape
    C = CHUNK
    NC = S // C
    G = min(GLA_G, NC)
    NG = NC // G

    def body(p_ref, w_ref, b_ref, gn_ref, raw_ref, st_ref, dm_ref, dp_ref, dw_ref, db_ref, dgn_ref, dst_sc):
        first = (pl.program_id(0) == 0) & (pl.program_id(1) == 0)

        @pl.when(first)
        def _():
            dw_ref[...] = jnp.zeros_like(dw_ref)
            db_ref[...] = jnp.zeros_like(db_ref)
            dgn_ref[...] = jnp.zeros_like(dgn_ref)

        @pl.when(pl.program_id(1) == 0)
        def _():
            dst_sc[...] = jnp.zeros_like(dst_sc)

        mk, mv, bd, lower, ltri, utri = _gla_masks()
        gn = gn_ref[...]
        cs = range(G)
        rows = [slice(c * C, (c + 1) * C) for c in cs]
        ps = [p_ref[0, rows[c], :].astype(F32) for c in cs]
        vs = [ps[c][:, 256:512] for c in cs]
        pre = [_mm(ps[c][:, 512:640], w_ref[...]) + b_ref[...] for c in cs]
        cum = [_mm_f32(ltri, _log_sigmoid(pre[c]) * (1.0 / GLA_TAU)) for c in cs]
        dg, dz, dgn_acc = [], [], jnp.zeros((1, 256), F32)
        for c in cs:
            g = raw_ref[0, rows[c], :]
            z = ps[c][:, 640:896]
            rstd = lax.rsqrt(_head_mean(g * g, mv, 64.0) + EPS)
            gh = g * rstd
            dm = dm_ref[0, rows[c], :]
            d_gn = dm * _silu(z)
            dz.append(dm * gh * gn * _dsilu(z))
            dgn_acc = dgn_acc + jnp.sum(d_gn * gh, axis=0, keepdims=True)
            d_gh = d_gn * gn
            dg.append(rstd * (d_gh - gh * _head_mean(d_gh * gh, mv, 64.0)))
        do_s = [_stack_heads(dg[c], mv) for c in cs]
        dattn = [_mm_nt(do_s[c], vs[c]) for c in cs]
        ks, e_pos, e_neg, q_pos, q_neg, k_pos, k_neg, qp_s, qn_s, past, fut, a_row, w_dec, kd = ([] for _ in range(14))
        for c in cs:
            q = ps[c][:, 0:128]
            k = ps[c][:, 128:256] * GLA_KSCALE
            last = cum[c][C - 1:C, :]
            ep, en = jnp.exp(cum[c]), jnp.exp(-cum[c])
            ks.append(k), e_pos.append(ep), e_neg.append(en)
            q_pos.append(q * ep), q_neg.append(q * en), k_pos.append(k * ep), k_neg.append(k * en)
            qp_s.append(_stack_heads(q_pos[c], mk)), qn_s.append(_stack_heads(q_neg[c], mk))
            past.append(_mm_nt(qp_s[c], k_neg[c]))
            fut.append(_mm_nt(qn_s[c], k_pos[c]))
            a_row.append(jnp.exp(last))
            w_dec.append(jnp.exp(last - cum[c]))
            kd.append(k * w_dec[c])
        sts = [st_ref[0, c] for c in cs]
        dq_st = [_mm(dg[c], sts[c]) for c in cs]
        dst_in = [_mm_tn(dg[c], q_pos[c]) for c in cs]
        dv, dq_pos, dk_neg, dq_neg, dk_pos = [], [], [], [], []
        for c in cs:
            attn = jnp.where(lower, past[c], fut[c])
            dpast = jnp.where(lower, dattn[c], 0.0)
            dfut = jnp.where(lower, 0.0, dattn[c])
            dv.append(_mm_tn(attn, do_s[c]))
            dq_pos.append(_fold_heads(_mm(dpast, k_neg[c]), mk, C) + dq_st[c])
            dk_neg.append(_mm_tn(dpast, qp_s[c]))
            dq_neg.append(_fold_heads(_mm(dfut, k_pos[c]), mk, C))
            dk_pos.append(_mm_tn(dfut, qn_s[c]))
        dst_next = dst_sc[...]
        d_a, d_kd = [None] * G, [None] * G
        for c in reversed(cs):
            d_a[c] = jnp.sum(dst_next * sts[c], axis=0, keepdims=True)
            gmat = jnp.where(bd, dst_next, 0.0)
            d_kd[c] = _mm(vs[c], gmat)
            dv[c] = dv[c] + _mm_nt(kd[c], gmat)
            dst_next = dst_next * a_row[c] + jnp.where(bd, dst_in[c], 0.0)
        dst_sc[...] = dst_next
        row = lax.broadcasted_iota(jnp.int32, (C, 128), 0)
        d_la, dk, dq = [], [], []
        for c in cs:
            t = d_kd[c] * kd[c]
            dk.append(d_kd[c] * w_dec[c] + dk_neg[c] * e_neg[c] + dk_pos[c] * e_pos[c])
            dq.append(dq_pos[c] * e_pos[c] + dq_neg[c] * e_neg[c])
            d_last = jnp.sum(t, axis=0, keepdims=True) + d_a[c] * a_row[c]
            d_cum = (dq_pos[c] * q_pos[c] - dk_neg[c] * k_neg[c] - dq_neg[c] * q_neg[c] + dk_pos[c] * k_pos[c] - t)
            d_la.append(_mm_f32(utri, d_cum + jnp.where(row == C - 1, d_last, 0.0)))
        d_pre = [d_la[c] * _sig(-pre[c]) * (1.0 / GLA_TAU) for c in cs]
        d_gg = [_mm_nt(d_pre[c], w_ref[...]) for c in cs]
        dw_acc = _mm_tn(ps[0][:, 512:640], d_pre[0])
        db_acc = jnp.sum(d_pre[0], axis=0, keepdims=True)
        for c in cs[1:]:
            dw_acc = dw_acc + _mm_tn(ps[c][:, 512:640], d_pre[c])
            db_acc = db_acc + jnp.sum(d_pre[c], axis=0, keepdims=True)
        for c in cs:
            dp_ref[0, rows[c], :] = jnp.concatenate([dq[c], dk[c] * GLA_KSCALE, dv[c], d_gg[c], dz[c]],
                                                    axis=1).astype(_MXU)
        dw_ref[...] += dw_acc
        db_ref[...] += db_acc
        dgn_ref[...] += dgn_acc

        @pl.when((pl.program_id(0) == B - 1) & (pl.program_id(1) == NG - 1))
        def _():
            s1 = dgn_ref[...]
            s1 = s1 + pltpu.roll(s1, 128, 1)
            dgn_ref[...] = s1 + pltpu.roll(s1, 64, 1)

    tok = lambda w: pl.BlockSpec((1, G * C, w), lambda b, i: (b, NG - 1 - i, 0))
    return pl.pallas_call(
        body, name="gla_bwd", grid=(B, NG),
        in_specs=[tok(GLA_W), _full((128, 128)), _full((1, 128)), _full((1, 256)), tok(256),
                  pl.BlockSpec((1, G, 256, 128), lambda b, i: (b, NG - 1 - i, 0, 0)), tok(256)],
        out_specs=[tok(GLA_W), _full((128, 128)), _full((1, 128)), _full((1, 256))],
        out_shape=[jax.ShapeDtypeStruct((B, S, GLA_W), _MXU), jax.ShapeDtypeStruct((128, 128), F32),
                   jax.ShapeDtypeStruct((1, 128), F32), jax.ShapeDtypeStruct((1, 256), F32)],
        scratch_shapes=[pltpu.VMEM((256, 128), F32)],
        compiler_params=_cp(("arbitrary", "arbitrary")),
    )(gla_p, w_g2p, b_g2, gnw, raw, states, d_mix)


def _rms(x, w):
    rstd = lax.rsqrt(jnp.mean(x * x, axis=-1, keepdims=True) + EPS)
    xh = x * rstd
    return xh, rstd, xh * w


def _rms_bwd(dy, xh, rstd, w):
    dxh = dy * w
    return rstd * (dxh - xh * jnp.mean(dxh * xh, axis=-1, keepdims=True))


MLA_T = 256


def _mla_prep_fwd(mla_p, cos, sin, qnw, kvnw, w_uq, w_ukv):
    B, S, _ = mla_p.shape
    tm = min(S, 512)

    t = min(MLA_T, S)
    nt = tm // t

    def body(p_ref, c_ref, s_ref, qn_ref, kn_ref, wq_ref, wkv_ref, q_ref, k_ref, v_ref, kt_ref, vt_ref):
        p = p_ref[0].astype(F32)
        cs, sn = c_ref[0], s_ref[0]
        _, _, qn = _rms(p[:, 0:256], qn_ref[...])
        qpre = _mm(qn, wq_ref[...])
        _, _, kvn = _rms(p[:, 256:384], kn_ref[...])
        kv = _mm(kvn, wkv_ref[...])
        kpe = _rope128(p[:, 384:512], cs, sn)
        for h in range(8):
            sl = slice(128 * h, 128 * h + 128)
            q_ref[0, :, sl] = _rope128(qpre[:, sl], cs, sn).astype(_MXU)
            kh = kv[:, sl] + kpe
            k_ref[0, :, sl] = kh.astype(_MXU)
            kht = kh.T
            for n in range(nt):
                kt_ref[0, n, sl, :] = kht[:, n * t:(n + 1) * t].astype(_MXU)
        v_ref[0] = kv[:, 1024:1536].astype(_MXU)
        for pr in range(4):
            vht = kv[:, 1024 + 128 * pr:1152 + 128 * pr].T
            for n in range(nt):
                vt_ref[0, n, 128 * pr:128 * pr + 128, :] = vht[:, n * t:(n + 1) * t].astype(_MXU)

    tok = lambda w: pl.BlockSpec((1, tm, w), lambda b, i: (b, i, 0))
    tr = lambda w: pl.BlockSpec((1, nt, w, t), lambda b, i: (b, i, 0, 0))
    return pl.pallas_call(
        body, name="mla_prep_fwd", grid=(B, S // tm),
        in_specs=[tok(512), tok(128), tok(128), _full((1, 256)), _full((1, 128)), _full((256, 1024)),
                  _full((128, 1536))],
        out_specs=[tok(1024), tok(1024), tok(512), tr(1024), tr(512)],
        out_shape=[jax.ShapeDtypeStruct((B, S, 1024), _MXU), jax.ShapeDtypeStruct((B, S, 1024), _MXU),
                   jax.ShapeDtypeStruct((B, S, 512), _MXU), jax.ShapeDtypeStruct((B, S // t, 1024, t), _MXU),
                   jax.ShapeDtypeStruct((B, S // t, 512, t), _MXU)],
        compiler_params=_cp(("parallel", "parallel")),
    )(mla_p, cos, sin, qnw, kvnw, w_uq, w_ukv)


def _chunk_mask_t(t):
    kj = lax.broadcasted_iota(jnp.int32, (t, t), 0) // CHUNK
    qi = lax.broadcasted_iota(jnp.int32, (t, t), 1) // CHUNK
    return kj <= qi


MLA_HG = 8
MLA_HG_FWD = 8
LOG2E = 1.4426950408889634
MLA_C2 = MLA_SCALE * LOG2E


def _mla_attn_fwd(q, k, vt):
    B, S, _ = q.shape
    t = min(MLA_T, S)
    nq = S // t
    HG = MLA_HG_FWD
    NP = HG // 2

    def body(q_ref, k_ref, vt_ref, o_ref, lse_ref, sa, sb, m_sc, l_sc, acc_sc):
        i = pl.program_id(2)
        row = lax.broadcasted_iota(jnp.int32, (128, 1), 0)
        low = row < 64
        mask = _chunk_mask_t(t)
        m_sc[...] = jnp.full(m_sc.shape, -jnp.inf, F32)
        l_sc[...] = jnp.zeros_like(l_sc)
        acc_sc[...] = jnp.zeros_like(acc_sc)

        ones = jnp.ones((8, t), _MXU)

        def scores(j, buf):
            kb = k_ref[0, pl.ds(pl.multiple_of(j * t, t), t), :]
            for h in range(HG):
                cols = slice(128 * h, 128 * h + 128)
                buf[h] = (_mm_nt(kb[:, cols], q_ref[0, :, cols]) * MLA_C2).astype(_MXU)

        def absorb(j, buf, masked):
            vtb = vt_ref[0, j]
            for pr in range(NP):
                alphas, pvs = [], []
                for hh in range(2):
                    h = 2 * pr + hh
                    s = buf[h]
                    if masked:
                        s = jnp.where(mask, s, jnp.full_like(s, -jnp.inf))
                    m_old = m_sc[h]
                    m_new = jnp.maximum(m_old, jnp.max(s, axis=0, keepdims=True).astype(F32))
                    alpha = jnp.exp2(m_old - m_new)
                    p = jnp.exp2(s - m_new.astype(_MXU))
                    l_sc[h] = alpha * l_sc[h] + _mm(ones, p)[0:1, :]
                    m_sc[h] = m_new
                    vth = vtb[128 * pr:128 * pr + 128, :]
                    vth = jnp.where(low if hh == 0 else ~low, vth, jnp.zeros_like(vth))
                    pvs.append(_mm(vth, p))
                    alphas.append(alpha)
                acc_sc[pr] = acc_sc[pr] * jnp.where(low, alphas[0], alphas[1]) + pvs[0] + pvs[1]

        scores(0, sb)

        def pair(jj, carry):
            j0 = 2 * jj
            scores(j0 + 1, sa)
            absorb(j0, sb, False)
            scores(j0 + 2, sb)
            absorb(j0 + 1, sa, False)
            return carry

        lax.fori_loop(0, i // 2, pair, 0)

        @pl.when(i % 2 == 1)
        def _():
            scores(i, sa)
            absorb(i - 1, sb, False)
            absorb(i, sa, True)

        @pl.when(i % 2 == 0)
        def _():
            absorb(i, sb, True)

        for pr in range(NP):
            l_e, l_o = l_sc[2 * pr], l_sc[2 * pr + 1]
            o_ref[0, :, 128 * pr:128 * pr + 128] = (acc_sc[pr] / jnp.where(low, l_e, l_o)).T
            lse_ref[0, pr, 0, 0:1, :] = m_sc[2 * pr] + jnp.log(l_e) * LOG2E
            lse_ref[0, pr, 0, 1:2, :] = m_sc[2 * pr + 1] + jnp.log(l_o) * LOG2E

    return pl.pallas_call(
        body, name="mla_attn_fwd", grid=(B, 8 // HG, nq),
        in_specs=[pl.BlockSpec((1, t, 128 * HG), lambda b, g, i: (b, i, g)),
                  pl.BlockSpec((1, S, 128 * HG), lambda b, g, i: (b, 0, g)),
                  pl.BlockSpec((1, nq, 64 * HG, t), lambda b, g, i: (b, 0, g, 0))],
        out_specs=[pl.BlockSpec((1, t, 64 * HG), lambda b, g, i: (b, i, g)),
                   pl.BlockSpec((1, NP, 1, 2, t), lambda b, g, i: (b, g, i, 0, 0))],
        out_shape=[jax.ShapeDtypeStruct((B, S, 512), F32), jax.ShapeDtypeStruct((B, 4, nq, 2, t), F32)],
        scratch_shapes=[pltpu.VMEM((HG, t, t), _MXU), pltpu.VMEM((HG, t, t), _MXU), pltpu.VMEM((HG, 1, t), F32),
                        pltpu.VMEM((HG, 1, t), F32), pltpu.VMEM((NP, 128, t), F32)],
        compiler_params=_cp(("parallel", "parallel", "arbitrary")),
    )(q, k, vt)


def _mla_attn_bwd(q, k, v, kt, do, lse, dl):
    B, S, _ = q.shape
    t = min(MLA_T, S)
    nk = S // t

    HG = MLA_HG
    NP = HG // 2

    def body(q_ref, k_ref, v_ref, kt_ref, do_ref, lse_ref, dl_ref, dq_ref, dk_ref, dv_ref,
             sa, da, sb, db, dqt_sc, dk_sc, dv_sc):
        j = pl.program_id(2)

        @pl.when(j == 0)
        def _():
            dqt_sc[...] = jnp.zeros_like(dqt_sc)

        dk_sc[...] = jnp.zeros_like(dk_sc)
        dv_sc[...] = jnp.zeros_like(dv_sc)
        lane = lax.broadcasted_iota(jnp.int32, (1, 128), 1)
        low = lane < 64
        mask = _chunk_mask_t(t)

        def half(x, hh):
            return jnp.where(low if hh == 0 else ~low, x, jnp.zeros_like(x))

        def prepare(i, sbuf, dbuf):
            rows = pl.ds(pl.multiple_of(i * t, t), t)
            for h in range(HG):
                cols = slice(128 * h, 128 * h + 128)
                pc = slice(128 * (h // 2), 128 * (h // 2) + 128)
                sbuf[h] = _mm_nt(k_ref[0, :, cols], q_ref[0, rows, cols]) * MLA_C2
                dbuf[h] = _mm_nt(half(v_ref[0, :, pc], h % 2), do_ref[0, rows, pc])

        def absorb(i, sbuf, dbuf, masked):
            rows = pl.ds(pl.multiple_of(i * t, t), t)
            for h in range(HG):
                pr, hh = h // 2, h % 2
                cols = slice(128 * h, 128 * h + 128)
                pc = slice(128 * pr, 128 * pr + 128)
                p = jnp.exp2(sbuf[h] - lse_ref[0, pr, i][hh:hh + 1, :])
                if masked:
                    p = jnp.where(mask, p, 0.0)
                dv_sc[pr] += _mm(p, half(do_ref[0, rows, pc], hh))
                ds = p * (dbuf[h] - dl_ref[0, pr, i][hh:hh + 1, :])
                dqt_sc[i, cols, :] += _mm(kt_ref[0, 0, cols, :], ds)
                dk_sc[h] += _mm(ds, q_ref[0, rows, cols])

        n = nk - 1 - j
        prepare(jnp.minimum(j + 1, nk - 1), sb, db)

        def pair(jj, carry):
            i0 = j + 1 + 2 * jj
            prepare(i0 + 1, sa, da)
            absorb(i0, sb, db, False)
            prepare(jnp.where(i0 + 2 <= nk - 1, i0 + 2, j), sb, db)
            absorb(i0 + 1, sa, da, False)
            return carry

        lax.fori_loop(0, n // 2, pair, 0)

        @pl.when(n % 2 == 1)
        def _():
            prepare(j, sa, da)
            absorb(nk - 1, sb, db, False)
            absorb(j, sa, da, True)

        @pl.when(n % 2 == 0)
        def _():
            absorb(j, sb, db, True)

        for h in range(HG):
            dk_ref[0, :, 128 * h:128 * h + 128] = (dk_sc[h] * MLA_SCALE).astype(_MXU)
        for pr in range(NP):
            dv_ref[0, :, 128 * pr:128 * pr + 128] = dv_sc[pr].astype(_MXU)

        @pl.when(j == nk - 1)
        def _():
            for i in range(nk):
                dq_ref[0, i * t:(i + 1) * t, :] = (dqt_sc[i].T * MLA_SCALE).astype(_MXU)

    seq = lambda w: pl.BlockSpec((1, S, w), lambda b, g, j: (b, 0, g))
    blk = lambda w: pl.BlockSpec((1, t, w), lambda b, g, j: (b, j, g))
    stat = pl.BlockSpec((1, NP, nk, 2, t), lambda b, g, j: (b, g, 0, 0, 0))
    return pl.pallas_call(
        body, name="mla_attn_bwd", grid=(B, 8 // HG, nk),
        in_specs=[seq(128 * HG), blk(128 * HG), blk(64 * HG),
                  pl.BlockSpec((1, 1, 128 * HG, t), lambda b, g, j: (b, j, g, 0)), seq(64 * HG), stat, stat],
        out_specs=[seq(128 * HG), blk(128 * HG), blk(64 * HG)],
        out_shape=[jax.ShapeDtypeStruct((B, S, 1024), _MXU), jax.ShapeDtypeStruct((B, S, 1024), _MXU),
                   jax.ShapeDtypeStruct((B, S, 512), _MXU)],
        scratch_shapes=[pltpu.VMEM((HG, t, t), F32), pltpu.VMEM((HG, t, t), F32), pltpu.VMEM((HG, t, t), F32),
                        pltpu.VMEM((HG, t, t), F32), pltpu.VMEM((nk, 128 * HG, t), F32),
                        pltpu.VMEM((HG, t, 128), F32), pltpu.VMEM((NP, t, 128), F32)],
        compiler_params=_cp(("parallel", "parallel", "arbitrary"), 56),
    )(q, k, v, kt, do, lse, dl)


def _mla_prep_bwd(mla_p, cos, sin, qnw, kvnw, w_uq, w_ukv, dq, dk, dv):
    B, S, _ = mla_p.shape
    tm = min(S, 512)

    def body(p_ref, c_ref, s_ref, qn_ref, kn_ref, wq_ref, wkv_ref, dq_ref, dk_ref, dv_ref,
             dp_ref, dwq_ref, dwkv_ref, dqn_ref, dkn_ref):
        first = (pl.program_id(0) == 0) & (pl.program_id(1) == 0)

        @pl.when(first)
        def _():
            dwq_ref[...] = jnp.zeros_like(dwq_ref)
            dwkv_ref[...] = jnp.zeros_like(dwkv_ref)
            dqn_ref[...] = jnp.zeros_like(dqn_ref)
            dkn_ref[...] = jnp.zeros_like(dkn_ref)

        p = p_ref[0].astype(F32)
        cs, sn = c_ref[0], s_ref[0]
        lane = lax.broadcasted_iota(jnp.int32, (1, 128), 1)
        pe = (lane >= 64) & (lane < 96)
        qh, q_rstd, qn = _rms(p[:, 0:256], qn_ref[...])
        kvh, kv_rstd, kvn = _rms(p[:, 256:384], kn_ref[...])
        dqv = dq_ref[0].astype(F32)
        dkv = dk_ref[0].astype(F32)
        dqpre = jnp.concatenate(
            [_rope128_t(dqv[:, 128 * h:128 * h + 128], cs, sn) for h in range(8)], axis=1)
        dkpe = jnp.zeros((tm, 128), F32)
        for h in range(8):
            dkpe = dkpe + jnp.where(pe, dkv[:, 128 * h:128 * h + 128], 0.0)
        dkr = _rope128_t(dkpe, cs, sn)
        dkv_all = jnp.concatenate([dkv, dv_ref[0].astype(F32)], axis=1)
        d_qn = _mm_nt(dqpre, wq_ref[...])
        d_kvn = _mm_nt(dkv_all, wkv_ref[...])
        dwq_ref[...] += _mm_tn(qn, dqpre)
        dwkv_ref[...] += _mm_tn(kvn, dkv_all)
        dqn_ref[...] += jnp.sum(d_qn * qh, axis=0, keepdims=True)
        dkn_ref[...] += jnp.sum(d_kvn * kvh, axis=0, keepdims=True)
        dp_ref[0] = jnp.concatenate([_rms_bwd(d_qn, qh, q_rstd, qn_ref[...]),
                                     _rms_bwd(d_kvn, kvh, kv_rstd, kn_ref[...]), dkr], axis=1).astype(_MXU)

    tok = lambda w: pl.BlockSpec((1, tm, w), lambda b, i: (b, i, 0))
    return pl.pallas_call(
        body, name="mla_prep_bwd", grid=(B, S // tm),
        in_specs=[tok(512), tok(128), tok(128), _full((1, 256)), _full((1, 128)), _full((256, 1024)),
                  _full((128, 1536)), tok(1024), tok(1024), tok(512)],
        out_specs=[tok(512), _full((256, 1024)), _full((128, 1536)), _full((1, 256)), _full((1, 128))],
        out_shape=[jax.ShapeDtypeStruct((B, S, 512), _MXU), jax.ShapeDtypeStruct((256, 1024), F32),
                   jax.ShapeDtypeStruct((128, 1536), F32), jax.ShapeDtypeStruct((1, 256), F32),
                   jax.ShapeDtypeStruct((1, 128), F32)],
        compiler_params=_cp(("arbitrary", "arbitrary")),
    )(mla_p, cos, sin, qnw, kvnw, w_uq, w_ukv, dq, dk, dv)


def _out_fwd(x, gate, r_g, o_mla, mla_p, g_g, w_out):
    B, S, D = x.shape
    tm = min(S, 512)

    def body(x_ref, g_ref, r_ref, o_ref, z_ref, gg_ref, w_ref, xn_ref, y_ref, mm_ref):
        mm = (o_ref[0] * _silu(z_ref[0].astype(F32))).astype(_MXU)
        mm_ref[0] = mm
        y = (jnp.dot(r_ref[0], w_ref[0:256, :], preferred_element_type=F32)
             + jnp.dot(mm, w_ref[256:768, :], preferred_element_type=F32)
             + jnp.dot(gg_ref[0], w_ref[768:1024, :], preferred_element_type=F32))
        y_ref[0] = y.astype(_MXU)
        xn_ref[0] = x_ref[0] + g_ref[0] * y

    tok = lambda w, c=0: pl.BlockSpec((1, tm, w), lambda b, i: (b, i, c))
    return pl.pallas_call(
        body, name="out_fwd", grid=(B, S // tm),
        in_specs=[tok(D), pl.BlockSpec((1, 1, D), lambda b, i: (b, 0, 0)), tok(256), tok(512), tok(512, 1),
                  tok(256), _full((D, D))],
        out_specs=[tok(D), tok(D), tok(512)],
        out_shape=[jax.ShapeDtypeStruct((B, S, D), F32), jax.ShapeDtypeStruct((B, S, D), _MXU),
                   jax.ShapeDtypeStruct((B, S, 512), _MXU)],
        compiler_params=_cp(("parallel", "parallel")),
    )(x, gate, r_g, o_mla, mla_p, g_g, w_out)


def _out_bwd(dx, y, gate, r_g, mm, g_g, w_out, o_mla, mla_p):
    B, S, D = dx.shape
    tm = min(S, 512)
    t = min(MLA_T, S)
    nt = tm // t

    def body(dx_ref, y_ref, g_ref, r_ref, mm_ref, gg_ref, w_ref, o_ref, z_ref,
             dr_ref, do_ref, dz_ref, dl_ref, dg_ref, dw_ref, dgate_ref):
        first = (pl.program_id(0) == 0) & (pl.program_id(1) == 0)

        @pl.when(first)
        def _():
            dw_ref[...] = jnp.zeros_like(dw_ref)

        @pl.when(pl.program_id(1) == 0)
        def _():
            dgate_ref[...] = jnp.zeros_like(dgate_ref)

        dxv = dx_ref[0]
        dgate_ref[0] += jnp.sum(dxv * y_ref[0].astype(F32), axis=0, keepdims=True)
        dy = (dxv * g_ref[0]).astype(_MXU)
        dr_ref[0] = _mm_nt(dy, w_ref[0:256, :])
        dg_ref[0] = _mm_nt(dy, w_ref[768:1024, :])
        dw_ref[0:256, :] += _mm_tn(r_ref[0], dy)
        dw_ref[256:768, :] += _mm_tn(mm_ref[0], dy)
        dw_ref[768:1024, :] += _mm_tn(gg_ref[0], dy)
        dm = _mm_nt(dy, w_ref[256:768, :])
        ov, z = o_ref[0], z_ref[0].astype(F32)
        do = dm * _silu(z)
        dz_ref[0] = (dm * ov * _dsilu(z)).astype(_MXU)
        do_ref[0] = do.astype(_MXU)
        prod = do * ov
        for pr in range(4):
            pt = prod[:, 128 * pr:128 * pr + 128].T
            se = jnp.sum(pt[0:64], axis=0, keepdims=True)
            so = jnp.sum(pt[64:128], axis=0, keepdims=True)
            for n in range(nt):
                dl_ref[0, pr, n, 0:1, :] = se[:, n * t:(n + 1) * t]
                dl_ref[0, pr, n, 1:2, :] = so[:, n * t:(n + 1) * t]

    tok = lambda w, c=0: pl.BlockSpec((1, tm, w), lambda b, i: (b, i, c))
    per_seq = pl.BlockSpec((1, 1, D), lambda b, i: (b, 0, 0))
    return pl.pallas_call(
        body, name="out_bwd", grid=(B, S // tm),
        in_specs=[tok(D), tok(D), per_seq, tok(256), tok(512), tok(256), _full((D, D)), tok(512), tok(512, 1)],
        out_specs=[tok(256), tok(512), tok(512), pl.BlockSpec((1, 4, nt, 2, t), lambda b, i: (b, 0, i, 0, 0)),
                   tok(256), _full((D, D)), per_seq],
        out_shape=[jax.ShapeDtypeStruct((B, S, 256), F32), jax.ShapeDtypeStruct((B, S, 512), _MXU),
                   jax.ShapeDtypeStruct((B, S, 512), _MXU), jax.ShapeDtypeStruct((B, 4, S // t, 2, t), F32),
                   jax.ShapeDtypeStruct((B, S, 256), F32), jax.ShapeDtypeStruct((D, D), F32),
                   jax.ShapeDtypeStruct((B, 1, D), F32)],
        compiler_params=_cp(("arbitrary", "arbitrary")),
    )(dx, y, gate, r_g, mm, g_g, w_out, o_mla, mla_p)


def _proj_bwd_x(x, shift, scale, nw, w_arr, d_ret, d_mla, d_mz, d_gla, dx_out):
    B, S, D = x.shape
    tm = min(S, 512)

    def body(x_ref, sc_ref, nw_ref, w_ref, dr_ref, dm_ref, dz_ref, dg_ref, dxo_ref,
             dx_ref, dsh_ref, dsc_ref, dnw_ref):
        first = (pl.program_id(0) == 0) & (pl.program_id(1) == 0)

        @pl.when(first)
        def _():
            dnw_ref[...] = jnp.zeros_like(dnw_ref)

        @pl.when(pl.program_id(1) == 0)
        def _():
            dsh_ref[...] = jnp.zeros_like(dsh_ref)
            dsc_ref[...] = jnp.zeros_like(dsc_ref)

        dp = jnp.concatenate([dr_ref[0], dm_ref[0], dz_ref[0], dg_ref[0]], axis=1)
        dh = lax.dot_general(dp, w_ref[...], (((1,), (1,)), ((), ())), preferred_element_type=F32)
        xv = x_ref[0]
        rstd = lax.rsqrt(jnp.mean(xv * xv, axis=-1, keepdims=True) + EPS)
        xh = xv * rstd
        nwv = nw_ref[...]
        mod = 1.0 + sc_ref[0]
        dsh_ref[0] += jnp.sum(dh, axis=0, keepdims=True)
        dsc_ref[0] += jnp.sum(dh * xh * nwv, axis=0, keepdims=True)
        dnw_ref[...] += jnp.sum(dh * xh * mod, axis=0, keepdims=True)
        dxh = dh * nwv * mod
        dx_ref[0] = dxo_ref[0] + rstd * (dxh - xh * jnp.mean(dxh * xh, axis=-1, keepdims=True))

    tok = lambda w: pl.BlockSpec((1, tm, w), lambda b, i: (b, i, 0))
    per_seq = pl.BlockSpec((1, 1, D), lambda b, i: (b, 0, 0))
    return pl.pallas_call(
        body, name="proj_bwd_x", grid=(B, S // tm),
        in_specs=[tok(D), per_seq, _full((1, D)), _full((D, ARR_W)), tok(RET_W), tok(512), tok(512),
                  tok(GLA_W), tok(D)],
        out_specs=[tok(D), per_seq, per_seq, _full((1, D))],
        out_shape=[jax.ShapeDtypeStruct((B, S, D), F32), jax.ShapeDtypeStruct((B, 1, D), F32),
                   jax.ShapeDtypeStruct((B, 1, D), F32), jax.ShapeDtypeStruct((1, D), F32)],
        compiler_params=_cp(("arbitrary", "arbitrary")),
    )(x, scale, nw, w_arr, d_ret, d_mla, d_mz, d_gla, dx_out)


def _proj_bwd_w(h, d_ret, d_mla, d_mz, d_gla):
    B, S, D = h.shape
    tm = min(S, 512)

    def body(h_ref, dr_ref, dm_ref, dz_ref, dg_ref, dw_ref):
        first = (pl.program_id(0) == 0) & (pl.program_id(1) == 0)

        @pl.when(first)
        def _():
            dw_ref[...] = jnp.zeros_like(dw_ref)

        hv = h_ref[0]
        tn = lambda d_ref: lax.dot_general(hv, d_ref[0], (((0,), (0,)), ((), ())), preferred_element_type=F32)
        dw_ref[:, 0:RET_W] += tn(dr_ref)
        dw_ref[:, RET_W:RET_W + 512] += tn(dm_ref)
        dw_ref[:, RET_W + 512:RET_W + MLA_W] += tn(dz_ref)
        dw_ref[:, RET_W + MLA_W:ARR_W] += tn(dg_ref)

    tok = lambda w: pl.BlockSpec((1, tm, w), lambda b, i: (b, i, 0))
    return pl.pallas_call(
        body, name="proj_bwd_w", grid=(B, S // tm),
        in_specs=[tok(D), tok(RET_W), tok(512), tok(512), tok(GLA_W)],
        out_specs=_full((D, ARR_W)), out_shape=jax.ShapeDtypeStruct((D, ARR_W), F32),
        compiler_params=_cp(("arbitrary", "arbitrary"), 56),
    )(h, d_ret, d_mla, d_mz, d_gla)


def _out_fwd_loss(x, gate, r_g, o_mla, mla_p, g_g, w_out, fw, target):
    B, S, D = x.shape
    tm = min(S, 512)

    def body(x_ref, g_ref, r_ref, o_ref, z_ref, gg_ref, w_ref, fw_ref, t_ref, dx_ref, y_ref, mm_ref, loss_ref, dfw_ref):
        first = (pl.program_id(0) == 0) & (pl.program_id(1) == 0)

        @pl.when(first)
        def _():
            loss_ref[...] = jnp.zeros_like(loss_ref)
            dfw_ref[...] = jnp.zeros_like(dfw_ref)

        mm = (o_ref[0] * _silu(z_ref[0].astype(F32))).astype(_MXU)
        mm_ref[0] = mm
        y = (jnp.dot(r_ref[0], w_ref[0:256, :], preferred_element_type=F32)
             + jnp.dot(mm, w_ref[256:768, :], preferred_element_type=F32)
             + jnp.dot(gg_ref[0], w_ref[768:1024, :], preferred_element_type=F32))
        y_ref[0] = y.astype(_MXU)
        xv = x_ref[0] + g_ref[0] * y
        fwv = fw_ref[...]
        rstd = lax.rsqrt(jnp.mean(xv * xv, axis=-1, keepdims=True) + EPS)
        xh = xv * rstd
        err = xh * fwv - t_ref[0]
        loss_ref[...] += 0.5 * jnp.sum(jnp.mean(err * err, axis=-1, keepdims=True), axis=0, keepdims=True)
        dy = err * (1.0 / D)
        dfw_ref[...] += jnp.sum(dy * xh, axis=0, keepdims=True)
        dxh = dy * fwv
        dx_ref[0] = rstd * (dxh - xh * jnp.mean(dxh * xh, axis=-1, keepdims=True))

    tok = lambda w, c=0: pl.BlockSpec((1, tm, w), lambda b, i: (b, i, c))
    return pl.pallas_call(
        body, name="out_fwd_loss", grid=(B, S // tm),
        in_specs=[tok(D), pl.BlockSpec((1, 1, D), lambda b, i: (b, 0, 0)), tok(256), tok(512), tok(512, 1),
                  tok(256), _full((D, D)), _full((1, D)), tok(D)],
        out_specs=[tok(D), tok(D), tok(512), _full((1, 1)), _full((1, D))],
        out_shape=[jax.ShapeDtypeStruct((B, S, D), F32), jax.ShapeDtypeStruct((B, S, D), _MXU),
                   jax.ShapeDtypeStruct((B, S, 512), _MXU), jax.ShapeDtypeStruct((1, 1), F32),
                   jax.ShapeDtypeStruct((1, D), F32)],
        compiler_params=_cp(("arbitrary", "arbitrary")),
    )(x, gate, r_g, o_mla, mla_p, g_g, w_out, fw, target)


def _local_step(x, pos3, mod, loss_target, small, w_in_a, w_uq_a, w_ukv_a, w_out_b):
    B, S, D = x.shape
    tabs = _rope_tables(pos3)
    saved = []
    for l in range(DEPTH):
        last = (small["final_norm"].reshape(1, D), loss_target) if l == DEPTH - 1 else None
        x, s = _layer_fwd(x, tabs, mod[l], {n: a[l] for n, a in small.items() if n != "final_norm"},
                          w_in_a[l], w_uq_a[l], w_ukv_a[l], w_out_b[l], loss_head=last)
        saved.append(s)
    dx, loss, d_fw = x
    grads = dict(final_norm=d_fw.reshape(D))
    per_layer = [None] * DEPTH
    for l in reversed(range(DEPTH)):
        dx, per_layer[l] = _layer_bwd(dx, saved[l], tabs)
    for name in per_layer[0]:
        grads[name] = jnp.stack([per_layer[l][name] for l in range(DEPTH)])
    return loss, dx, grads


def _layer_fwd(x, tabs, mod_l, small_l, w_in_a, w_uq_a=None, w_ukv_a=None, w_out_b=None, late_weights=None,
               loss_head=None):
    B, S, D = x.shape
    cr, sr, cm, sm = tabs
    shift = mod_l[:, 0:D].reshape(B, 1, D)
    scale = mod_l[:, D:2 * D].reshape(B, 1, D)
    gate = mod_l[:, 2 * D:3 * D].reshape(B, 1, D)
    nw = small_l["norm_w"].reshape(1, D)
    qnw = small_l["mla_q_norm"].reshape(1, 256)
    kvnw = small_l["mla_kv_norm"].reshape(1, 128)
    w_g2p = jnp.pad(small_l["gla_w_g2"], ((0, 112), (0, 0)))
    b_g2 = small_l["gla_b_g2"].reshape(1, 128)
    gnw = jnp.tile(small_l["gla_norm"], 4).reshape(1, 256)
    ret_p, mla_p, gla_p, h = _proj_fwd(x, shift, scale, nw, w_in_a)
    r_g, r_raw, r_st = _ret_fwd(ret_p, cr, sr)
    if late_weights is not None:
        w_uq_a, w_ukv_a, w_out_b = late_weights(r_raw)
    q, k, v, kt, vt = _mla_prep_fwd(mla_p, cm, sm, qnw, kvnw, w_uq_a, w_ukv_a)
    o_mla, lse = _mla_attn_fwd(q, k, vt)
    g_g, g_raw, g_st = _gla_fwd(gla_p, w_g2p, b_g2, gnw)
    if loss_head is None:
        x_new, y, mm = _out_fwd(x, gate, r_g, o_mla, mla_p, g_g, w_out_b)
    else:
        dx, y, mm, loss, d_fw = _out_fwd_loss(x, gate, r_g, o_mla, mla_p, g_g, w_out_b, *loss_head)
        x_new = (dx, loss, d_fw)
    saved = dict(x=x, shift=shift, scale=scale, gate=gate, nw=nw, qnw=qnw, kvnw=kvnw, w_g2p=w_g2p, b_g2=b_g2,
                 gnw=gnw, ret_p=ret_p, mla_p=mla_p, gla_p=gla_p, h=h, r_g=r_g, r_raw=r_raw, r_st=r_st, q=q, k=k,
                 v=v, kt=kt, o_mla=o_mla, lse=lse, g_g=g_g, g_raw=g_raw, g_st=g_st, y=y, mm=mm,
                 w_in_a=w_in_a, w_uq_a=w_uq_a, w_ukv_a=w_ukv_a, w_out_b=w_out_b)
    return x_new, saved


def _layer_bwd(dx, s, tabs, early_grads=None):
    B, S, D = dx.shape
    cr, sr, cm, sm = tabs
    d_r, do, d_mz, dl, d_g, dw_out, d_gate = _out_bwd(dx, s["y"], s["gate"], s["r_g"], s["mm"], s["g_g"], s["w_out_b"],
                                                      s["o_mla"], s["mla_p"])
    d_ret = _ret_bwd(s["ret_p"], cr, sr, s["r_raw"], s["r_st"], d_r)
    dq, dk, dv = _mla_attn_bwd(s["q"], s["k"], s["v"], s["kt"], do, s["lse"], dl)
    d_mla, dw_uq, dw_ukv, d_qnw, d_kvnw = _mla_prep_bwd(
        s["mla_p"], cm, sm, s["qnw"], s["kvnw"], s["w_uq_a"], s["w_ukv_a"], dq, dk, dv)
    gnw = s["gnw"] if early_grads is None else s["gnw"] + early_grads(dw_out, dw_uq, dw_ukv)
    d_gla, dw_g2p, db_g2, d_gnw = _gla_bwd(s["gla_p"], s["w_g2p"], s["b_g2"], gnw, s["g_raw"], s["g_st"], d_g)
    dx, d_shift, d_scale, d_nw = _proj_bwd_x(s["x"], s["shift"], s["scale"], s["nw"], s["w_in_a"],
                                             d_ret, d_mla, d_mz, d_gla, dx)
    dw_in = _proj_bwd_w(s["h"], d_ret, d_mla, d_mz, d_gla)
    grads = dict(
        d_mod=jnp.concatenate([d_shift, d_scale, d_gate], axis=2).reshape(B, 3 * D),
        norm_w=d_nw.reshape(D), mla_q_norm=d_qnw.reshape(256), mla_kv_norm=d_kvnw.reshape(128),
        gla_w_g2=dw_g2p[0:16], gla_b_g2=db_g2.reshape(128), gla_norm256=d_gnw.reshape(256),
        w_in_a=dw_in, w_uq_a=dw_uq, w_ukv_a=dw_ukv, w_out=dw_out)
    return dx, grads


def _exchange(arrs, gather, name):
    n = len(arrs)
    out_shape = [jax.ShapeDtypeStruct(((N_DEV,) + a.shape) if g else a.shape, a.dtype)
                 for a, g in zip(arrs, gather)]

    def body(*refs):
        ins, outs = refs[:n], refs[n:2 * n]
        send_sems, recv_sems, local_sems = refs[2 * n:]
        ix, iy, ic = lax.axis_index("x"), lax.axis_index("y"), lax.axis_index("c")
        me = 4 * ix + 2 * iy + ic
        copies = []
        for a in range(n):
            mine = ins[a] if gather[a] else ins[a].at[me]
            loc = pltpu.make_async_copy(mine, outs[a].at[me], local_sems.at[a])
            loc.start()
            copies.append(loc)
            for d in range(1, N_DEV):
                px = 1 - ix if d & 4 else ix
                py = 1 - iy if d & 2 else iy
                pc = 1 - ic if d & 1 else ic
                src = ins[a] if gather[a] else ins[a].at[4 * px + 2 * py + pc]
                cp = pltpu.make_async_remote_copy(
                    src_ref=src, dst_ref=outs[a].at[me], send_sem=send_sems.at[a, d - 1],
                    recv_sem=recv_sems.at[a, d - 1], device_id=(px, py, pc), device_id_type=pl.DeviceIdType.MESH)
                cp.start()
                copies.append(cp)
        for cp in copies:
            cp.wait()

    any_spec = pl.BlockSpec(memory_space=pl.ANY)
    outs = pl.pallas_call(
        body, name=name, in_specs=[any_spec] * n, out_specs=[any_spec] * n, out_shape=out_shape,
        scratch_shapes=[pltpu.SemaphoreType.DMA((n, N_DEV - 1)), pltpu.SemaphoreType.DMA((n, N_DEV - 1)),
                        pltpu.SemaphoreType.DMA((n,))],
    )(*arrs)
    return list(outs)


def _peers(ix, iy, ic):
    out = []
    for d in range(1, N_DEV):
        px = 1 - ix if d & 4 else ix
        py = 1 - iy if d & 2 else iy
        pc = 1 - ic if d & 1 else ic
        out.append((d - 1, (px, py, pc), 4 * px + 2 * py + pc))
    return out


def _exchange_start(arrs, gather, name, after=None):
    n = len(arrs)
    lands = [lax.empty(((N_DEV,) + a.shape) if g else a.shape, a.dtype) for a, g in zip(arrs, gather)]
    extra = [] if after is None else [after]

    def body(*refs):
        ins, land_refs = refs[:n], refs[n:2 * n]
        send_sems, recv_sems = refs[2 * n + len(extra)], refs[2 * n + len(extra) + 1]
        token = refs[-1]
        ix, iy, ic = lax.axis_index("x"), lax.axis_index("y"), lax.axis_index("c")
        me = 4 * ix + 2 * iy + ic
        for a in range(n):
            for k, peer, peer_idx in _peers(ix, iy, ic):
                pltpu.make_async_remote_copy(
                    src_ref=ins[a] if gather[a] else ins[a].at[peer_idx], dst_ref=land_refs[a].at[me],
                    send_sem=send_sems.at[7 * a + k], recv_sem=recv_sems.at[7 * a + k], device_id=peer,
                    device_id_type=pl.DeviceIdType.MESH).start()
        token[...] = jnp.zeros_like(token)

    hbm = pl.BlockSpec(memory_space=pltpu.HBM)
    sem = pl.BlockSpec(memory_space=pltpu.SEMAPHORE)
    held = [pltpu.with_memory_space_constraint(a, pltpu.HBM) for a in list(arrs) + lands]
    outs = pl.pallas_call(
        body, name=name,
        out_shape=(pltpu.SemaphoreType.DMA((7 * n,)), pltpu.SemaphoreType.DMA((7 * n,)),
                   *[pltpu.HBM(a.shape, a.dtype) for a in held], jax.ShapeDtypeStruct((8, 128), F32)),
        in_specs=[hbm] * (2 * n) + [pl.BlockSpec(memory_space=pl.ANY)] * len(extra),
        out_specs=(sem, sem, *[hbm] * (2 * n), pl.BlockSpec(memory_space=pltpu.VMEM)),
        input_output_aliases={a: 2 + a for a in range(2 * n)},
        compiler_params=pltpu.CompilerParams(has_side_effects=pltpu.SideEffectType.DATAFLOW_SIDE_EFFECTING),
    )(*held, *extra)
    return dict(send=outs[0], recv=outs[1], srcs=list(outs[2:2 + n]), lands=list(outs[2 + n:2 + 2 * n]),
                token=outs[-1], gather=list(gather))


def _exchange_wait(flight, after, me, name):
    n = len(flight["srcs"])
    gather = flight["gather"]

    def body(*refs):
        srcs, land_refs = refs[:n], refs[n:2 * n]
        send_sems, recv_sems = refs[2 * n], refs[2 * n + 1]
        ix, iy, ic = lax.axis_index("x"), lax.axis_index("y"), lax.axis_index("c")
        mine = 4 * ix + 2 * iy + ic
        for a in range(n):
            for k, peer, peer_idx in _peers(ix, iy, ic):
                cp = pltpu.make_async_remote_copy(
                    src_ref=srcs[a] if gather[a] else srcs[a].at[peer_idx], dst_ref=land_refs[a].at[mine],
                    send_sem=send_sems.at[7 * a + k], recv_sem=recv_sems.at[7 * a + k], device_id=peer,
                    device_id_type=pl.DeviceIdType.MESH)
                cp.wait_send()
                cp.wait_recv()

    hbm = pl.BlockSpec(memory_space=pltpu.HBM)
    sem = pl.BlockSpec(memory_space=pltpu.SEMAPHORE)
    held = flight["srcs"] + flight["lands"]
    outs = pl.pallas_call(
        body, name=name, out_shape=tuple(pltpu.HBM(a.shape, a.dtype) for a in held),
        in_specs=[hbm] * (2 * n) + [sem, sem, pl.BlockSpec(memory_space=pl.ANY)], out_specs=tuple([hbm] * (2 * n)),
        input_output_aliases={a: a for a in range(2 * n)},
        compiler_params=pltpu.CompilerParams(has_side_effects=pltpu.SideEffectType.DATAFLOW_SIDE_EFFECTING),
    )(*held, flight["send"], flight["recv"], after)
    got = []
    for a in range(n):
        src, land = outs[a], outs[n + a]
        own = src if gather[a] else lax.dynamic_index_in_dim(src, me, axis=0, keepdims=False)
        got.append(lax.dynamic_update_index_in_dim(land, own, me, axis=0))
    return got


def _ada_fwd(c_all, ada_w, ada_b_cols):
    nb, D = c_all.shape
    cols = ada_w.shape[2]

    def body(c_ref, w_ref, b_ref, out_ref):
        ca = _silu(c_ref[...])
        for l in range(DEPTH):
            out_ref[l] = _mm(ca, w_ref[l]) + b_ref[l:l + 1, :]

    return pl.pallas_call(
        body, name="ada_fwd", out_shape=jax.ShapeDtypeStruct((DEPTH, nb, cols), F32),
        in_specs=[pl.BlockSpec(memory_space=pltpu.VMEM)] * 3, out_specs=pl.BlockSpec(memory_space=pltpu.VMEM),
        compiler_params=pltpu.CompilerParams(vmem_limit_bytes=32 * VMEM_MB),
    )(c_all, ada_w, ada_b_cols)


def _ada_bwd(c_all, d_mod_cols):
    nb, D = c_all.shape
    cols = d_mod_cols.shape[2]

    def body(c_ref, dm_ref, out_ref):
        ca = _silu(c_ref[...])
        for l in range(DEPTH):
            out_ref[l] = _mm_tn(ca, dm_ref[l])

    return pl.pallas_call(
        body, name="ada_bwd", out_shape=jax.ShapeDtypeStruct((DEPTH, D, cols), F32),
        in_specs=[pl.BlockSpec(memory_space=pltpu.VMEM)] * 2, out_specs=pl.BlockSpec(memory_space=pltpu.VMEM),
        compiler_params=pltpu.CompilerParams(vmem_limit_bytes=32 * VMEM_MB),
    )(c_all, d_mod_cols)


def _sum_adamw(parts, w, m, v, name):
    P, R, C = parts.shape
    tr = 256 if (R % 256 == 0 and R > 256) else R

    def body(p_ref, w_ref, m_ref, v_ref, g_ref, d_ref, nm_ref, nv_ref):
        g = p_ref[0].astype(F32)
        for k in range(1, P):
            g = g + p_ref[k].astype(F32)
        g_ref[...] = g
        nm = ADAM_B1 * m_ref[...] + (1.0 - ADAM_B1) * g
        nv = ADAM_B2 * v_ref[...] + (1.0 - ADAM_B2) * (g * g)
        nm_ref[...] = nm
        nv_ref[...] = nv
        m_hat = nm / (1.0 - ADAM_B1 ** ADAM_STEP)
        v_hat = nv / (1.0 - ADAM_B2 ** ADAM_STEP)
        d_ref[...] = -ADAM_LR * (m_hat / (jnp.sqrt(v_hat) + ADAM_EPS) + ADAM_WD * w_ref[...])

    blk = pl.BlockSpec((tr, C), lambda i: (i, 0))
    shp = jax.ShapeDtypeStruct((R, C), F32)
    return pl.pallas_call(
        body, name=name, grid=(R // tr,),
        in_specs=[pl.BlockSpec((P, tr, C), lambda i: (0, i, 0)), blk, blk, blk],
        out_specs=[blk, blk, blk, blk], out_shape=[shp, shp, shp, shp],
        compiler_params=_cp(("parallel",)),
    )(parts, w, m, v)


def _sum_adamw_layer(parts, w, m, v, layer, name, prev=None, after=None):
    P, R, C = parts.shape
    tr = 256 if (R % 256 == 0 and R > 256) else R

    def body(p_ref, w_ref, m_ref, v_ref, *rest):
        g_ref, d_ref, nm_ref, nv_ref = rest[-4:]
        g = p_ref[0].astype(F32)
        for k in range(1, P):
            g = g + p_ref[k].astype(F32)
        g_ref[0] = g
        nm = ADAM_B1 * m_ref[0] + (1.0 - ADAM_B1) * g
        nv = ADAM_B2 * v_ref[0] + (1.0 - ADAM_B2) * (g * g)
        nm_ref[0] = nm
        nv_ref[0] = nv
        m_hat = nm / (1.0 - ADAM_B1 ** ADAM_STEP)
        v_hat = nv / (1.0 - ADAM_B2 ** ADAM_STEP)
        d_ref[0] = -ADAM_LR * (m_hat / (jnp.sqrt(v_hat) + ADAM_EPS) + ADAM_WD * w_ref[0])

    blk = pl.BlockSpec((1, tr, C), lambda i: (layer, i, 0))
    shp = jax.ShapeDtypeStruct(w.shape, F32)
    in_specs = [pl.BlockSpec((P, tr, C), lambda i: (0, i, 0)), blk, blk, blk]
    args = [parts, w, m, v]
    aliases = {}
    if prev is not None:
        in_specs += [pl.BlockSpec(memory_space=pl.ANY)] * 4
        args += list(prev)
        aliases = {4 + k: k for k in range(4)}
    if after is not None:
        in_specs.append(pl.BlockSpec(memory_space=pl.ANY))
        args.append(after)
    return list(pl.pallas_call(
        body, name=name, grid=(R // tr,), in_specs=in_specs, out_specs=[blk] * 4, out_shape=[shp] * 4,
        input_output_aliases=aliases, compiler_params=_cp(("parallel",)),
    )(*args))


SMALL = ["norm_w", "mla_q_norm", "mla_kv_norm", "gla_w_g2", "gla_b_g2", "gla_norm", "final_norm"]


SMALL_ROWS = 72


def _pack_small(loss, part):
    flat = [jnp.pad(loss.reshape(1), (0, 127))] + [part[n].reshape(-1) for n in SMALL]
    used = sum(f.shape[0] for f in flat)
    flat.append(jnp.zeros((SMALL_ROWS * 128 - used,), F32))
    return jnp.concatenate(flat).reshape(SMALL_ROWS, 128)


def _small_adamw(packed_parts, w, m, v):
    n = len(w)

    def body(*refs):
        p_ref = refs[0]
        w_refs, m_refs, v_refs = refs[1:1 + n], refs[1 + n:1 + 2 * n], refs[1 + 2 * n:1 + 3 * n]
        outs, acc = refs[1 + 3 * n:-1], refs[-1]
        total = p_ref[0]
        for k in range(1, N_DEV):
            total = total + p_ref[k]
        acc[...] = total
        outs[0][...] = acc[0:1, :]
        r0 = 1
        for i in range(n):
            shp = w_refs[i].shape
            if len(shp) == 3:
                g = acc[r0:r0 + shp[0] * shp[1], :].reshape(shp)
                r0 += shp[0] * shp[1]
            elif shp[1] < 128:
                g = acc[r0:r0 + shp[0], 0:shp[1]]
                r0 += shp[0]
            else:
                k = shp[1] // 128
                g = jnp.concatenate(
                    [jnp.concatenate([acc[r0 + l * k + j:r0 + l * k + j + 1, :] for j in range(k)], axis=1)
                     for l in range(shp[0])], axis=0)
                r0 += shp[0] * k
            nm = ADAM_B1 * m_refs[i][...] + (1.0 - ADAM_B1) * g
            nv = ADAM_B2 * v_refs[i][...] + (1.0 - ADAM_B2) * (g * g)
            m_hat = nm / (1.0 - ADAM_B1 ** ADAM_STEP)
            v_hat = nv / (1.0 - ADAM_B2 ** ADAM_STEP)
            outs[1 + 4 * i][...] = g
            outs[2 + 4 * i][...] = -ADAM_LR * (m_hat / (jnp.sqrt(v_hat) + ADAM_EPS) + ADAM_WD * w_refs[i][...])
            outs[3 + 4 * i][...] = nm
            outs[4 + 4 * i][...] = nv

    vmem = pl.BlockSpec(memory_space=pltpu.VMEM)
    out_shape = [jax.ShapeDtypeStruct((1, 128), F32)]
    for a in w:
        out_shape += [jax.ShapeDtypeStruct(a.shape, F32)] * 4
    outs = pl.pallas_call(
        body, name="adamw_small", in_specs=[vmem] * (1 + 3 * n), out_specs=[vmem] * (1 + 4 * n), out_shape=out_shape,
        scratch_shapes=[pltpu.VMEM((SMALL_ROWS, 128), F32)],
    )(packed_parts, *w, *m, *v)
    return outs[0], [outs[1 + 4 * i:5 + 4 * i] for i in range(n)]


WEIGHTS = ["norm_w", "ada_w", "ada_b", "w_in", "mla_q_norm", "w_uq", "mla_kv_norm", "w_ukv", "gla_w_g2",
           "gla_b_g2", "gla_norm", "w_out", "final_norm"]


def kernel(x, c, positions, norm_w, ada_w, ada_b, w_in, mla_q_norm, w_uq, mla_kv_norm, w_ukv, gla_w_g2, gla_b_g2, gla_norm, w_out, final_norm, loss_target, m_norm_w, m_ada_w, m_ada_b, m_w_in, m_mla_q_norm, m_w_uq, m_mla_kv_norm, m_w_ukv, m_gla_w_g2, m_gla_b_g2, m_gla_norm, m_w_out, m_final_norm, v_norm_w, v_ada_w, v_ada_b, v_w_in, v_mla_q_norm, v_w_uq, v_mla_kv_norm, v_w_ukv, v_gla_w_g2, v_gla_b_g2, v_gla_norm, v_w_out, v_final_norm):
    w = dict(norm_w=norm_w, ada_w=ada_w, ada_b=ada_b, w_in=w_in, mla_q_norm=mla_q_norm, w_uq=w_uq,
             mla_kv_norm=mla_kv_norm, w_ukv=w_ukv, gla_w_g2=gla_w_g2, gla_b_g2=gla_b_g2, gla_norm=gla_norm,
             w_out=w_out, final_norm=final_norm)
    m = dict(norm_w=m_norm_w, ada_w=m_ada_w, ada_b=m_ada_b, w_in=m_w_in, mla_q_norm=m_mla_q_norm, w_uq=m_w_uq,
             mla_kv_norm=m_mla_kv_norm, w_ukv=m_w_ukv, gla_w_g2=m_gla_w_g2, gla_b_g2=m_gla_b_g2,
             gla_norm=m_gla_norm, w_out=m_w_out, final_norm=m_final_norm)
    v = dict(norm_w=v_norm_w, ada_w=v_ada_w, ada_b=v_ada_b, w_in=v_w_in, mla_q_norm=v_mla_q_norm, w_uq=v_w_uq,
             mla_kv_norm=v_mla_kv_norm, w_ukv=v_w_ukv, gla_w_g2=v_gla_w_g2, gla_b_g2=v_gla_b_g2,
             gla_norm=v_gla_norm, w_out=v_w_out, final_norm=v_final_norm)
    B, S, D = x.shape
    me = 4 * lax.axis_index("x") + 2 * lax.axis_index("y") + lax.axis_index("c")
    ada_cols = ada_w.shape[2]
    cast = lambda a: a.astype(_MXU)

    sharded = ["w_in", "w_uq", "w_ukv", "w_out"]

    whole_cols = lambda a: jnp.transpose(a, (1, 0, 2)).reshape(a.shape[1], -1)
    whole_in = lambda blk: _arrange_w_in(whole_cols(blk))
    whole_rest = lambda blks: (_arrange_w_uq(whole_cols(blks[0])), _arrange_w_ukv(whole_cols(blks[1])),
                               blks[2].reshape(D, D))
    col_blocks = lambda a: jnp.transpose(a.reshape(a.shape[0], N_DEV, -1), (1, 0, 2)).astype(jnp.bfloat16)
    blocks_in = lambda dw_in_a: col_blocks(_unarrange_w_in(dw_in_a))
    blocks_rest = lambda dw_out, dw_uq_a, dw_ukv_a: [
        col_blocks(_unarrange_w_uq(dw_uq_a)), col_blocks(_unarrange_w_ukv(dw_ukv_a)),
        dw_out.reshape(N_DEV, D // N_DEV, D).astype(jnp.bfloat16)]

    (c_g,) = _exchange([c], [True], "gather_c")
    c_all = c_g.reshape(N_DEV * B, D)

    ada_b_cols = lax.dynamic_slice(ada_b, (0, me * ada_cols), (DEPTH, ada_cols))
    mod_cols = _ada_fwd(c_all, ada_w, ada_b_cols)
    mod_send = jnp.transpose(mod_cols.reshape(DEPTH, N_DEV, B, ada_cols), (1, 0, 2, 3))
    (mod_recv,) = _exchange([mod_send], [False], "scatter_mod")
    mod = jnp.transpose(mod_recv, (1, 2, 0, 3)).reshape(DEPTH, B, 3 * D)

    flight_i = _exchange_start([cast(w_in[0])], [True], "gather_start_first", after=mod)
    flight_r = _exchange_start([cast(w[n][0]) for n in sharded[1:]], [True] * 3, "gather_start_layer0",
                               after=flight_i["token"])
    flight_w = _exchange_start([cast(w[n][1]) for n in sharded], [True] * 4, "gather_start_layer1",
                               after=flight_r["token"])
    small_w = {n: w[n] for n in SMALL}
    layer_small = lambda l: {n: a[l] for n, a in small_w.items() if n != "final_norm"}
    tabs = _rope_tables(positions.reshape(B, S, 1), flight_w["token"][0, 0])
    late0 = lambda after: whole_rest(_exchange_wait(flight_r, after, me, "gather_wait_layer0"))
    (w_in0_g,) = _exchange_wait(flight_i, tabs[0], me, "gather_wait_first")
    x1, saved0 = _layer_fwd(x, tabs, mod[0], layer_small(0), whole_in(w_in0_g), late_weights=late0)
    got1 = _exchange_wait(flight_w, x1, me, "gather_wait_layer1")
    (dx, loss, d_fw), saved1 = _layer_fwd(x1, tabs, mod[1], layer_small(1), whole_in(got1[0]), *whole_rest(got1[1:]),
                                          loss_head=(final_norm.reshape(1, D), loss_target))

    dx, g1 = _layer_bwd(dx, saved1, tabs)
    flight_g = _exchange_start([blocks_in(g1["w_in_a"])] + blocks_rest(g1["w_out"], g1["w_uq_a"], g1["w_ukv_a"]),
                               [False] * 4, "grads_start_layer1")
    flights = {}

    def early0(dw_out, dw_uq_a, dw_ukv_a):
        flights["rest0"] = _exchange_start(blocks_rest(dw_out, dw_uq_a, dw_ukv_a), [False] * 3, "grads_start_layer0")
        return flights["rest0"]["token"][0, 0]

    saved0 = dict(saved0, gate=saved0["gate"] + flight_g["token"][0, 0])
    grad_x, g0 = _layer_bwd(dx, saved0, tabs, early_grads=early0)
    parts1 = _exchange_wait(flight_g, grad_x, me, "grads_wait_layer1")
    rest0 = _exchange_wait(flights["rest0"], g0["w_in_a"], me, "grads_wait_layer0")

    both = lambda n: jnp.stack([g0[n], g1[n]])
    d_mod = both("d_mod")
    part = dict(norm_w=both("norm_w"), mla_q_norm=both("mla_q_norm"), mla_kv_norm=both("mla_kv_norm"),
                gla_w_g2=both("gla_w_g2"), gla_b_g2=both("gla_b_g2"), gla_norm=both("gla_norm256")[:, 0:128],
                final_norm=d_fw)
    flight_l = _exchange_start([d_mod, _pack_small(loss, part), blocks_in(g0["w_in_a"])], [True, True, False],
                               "exchange_start_last")
    res = {}
    behind = flight_l["token"]
    for a, name in enumerate(sharded):
        res[name] = _sum_adamw_layer(parts1[a], w[name], m[name], v[name], 1, "adamw_%s_layer1" % name, after=behind)
        behind = res[name][1]
    for a, name in enumerate(sharded[1:]):
        res[name] = _sum_adamw_layer(rest0[a], w[name], m[name], v[name], 0, "adamw_%s_layer0" % name,
                                     prev=res[name], after=behind)
        behind = res[name][1]
    d_mod_g, small_g, in0 = _exchange_wait(flight_l, behind, me, "exchange_wait_last")
    res["w_in"] = _sum_adamw_layer(in0, w_in, m_w_in, v_w_in, 0, "adamw_w_in_layer0", prev=res["w_in"])

    d_mod_all = jnp.transpose(d_mod_g, (1, 0, 2, 3)).reshape(DEPTH, N_DEV * B, 3 * D)
    d_mod_cols = lax.dynamic_slice(d_mod_all, (0, 0, me * ada_cols), (DEPTH, N_DEV * B, ada_cols))
    g_ada_w = _ada_bwd(c_all, d_mod_cols)

    def update(name, parts2d):
        shp = w[name].shape
        two = lambda a: a.reshape(parts2d.shape[1:])
        out = _sum_adamw(parts2d, two(w[name]), two(m[name]), two(v[name]), "adamw_" + name)
        res[name] = [o.reshape(shp) for o in out]

    update("ada_w", g_ada_w.reshape(1, DEPTH * D, ada_cols))
    update("ada_b", jnp.transpose(d_mod_g, (0, 2, 1, 3)).reshape(N_DEV * B, DEPTH * 3 * D // 128, 128))
    row = lambda a: a.reshape(1, D) if a.ndim == 1 else a
    loss_sum, small_out = _small_adamw(small_g, [row(w[n]) for n in SMALL], [row(m[n]) for n in SMALL],
                                       [row(v[n]) for n in SMALL])
    for n, outs in zip(SMALL, small_out):
        res[n] = [o.reshape(w[n].shape) for o in outs]
    loss_out = loss_sum[0, 0]
    return (loss_out, grad_x, *[res[n][0] for n in WEIGHTS], *[res[n][1] for n in WEIGHTS],
            *[res[n][2] for n in WEIGHTS], *[res[n][3] for n in WEIGHTS])
```

```python
import functools
import math

import numpy as np
import jax
import jax.numpy as jnp
from jax import lax
from jax.experimental import pallas as pl
from jax.experimental.pallas import tpu as pltpu

F32 = jnp.float32
_MXU = jnp.bfloat16

D_MODEL = 1024
DEPTH = 2
CHUNK = 64
EPS = 1e-6
ROPE_THETA = 10000.0
N_DEV = 8

MLA_SCALE = 96.0 ** -0.5
RET_KSCALE = 64.0 ** -0.5
GLA_KSCALE = 32.0 ** -0.5
GLA_TAU = 16.0

ADAM_LR = 0.001
ADAM_B1 = 0.9
ADAM_B2 = 0.999
ADAM_EPS = 1e-08
ADAM_WD = 0.01
ADAM_STEP = 10

RET_W, MLA_W, GLA_W = 1024, 1024, 896
ARR_W = RET_W + MLA_W + GLA_W
VMEM_MB = 1024 * 1024


def _cp(sem, vmem_mb=48):
    return pltpu.CompilerParams(dimension_semantics=sem, vmem_limit_bytes=vmem_mb * VMEM_MB)


def _mm(a, b):
    return jnp.dot(a.astype(_MXU), b.astype(_MXU), preferred_element_type=F32)


def _mm_nt(a, b):
    return lax.dot_general(a.astype(_MXU), b.astype(_MXU), (((1,), (1,)), ((), ())),
                           preferred_element_type=F32)


def _mm_tn(a, b):
    return lax.dot_general(a.astype(_MXU), b.astype(_MXU), (((0,), (0,)), ((), ())),
                           preferred_element_type=F32)


def _mm_f32(a, b):
    return jnp.dot(a, b, precision=lax.Precision.HIGHEST, preferred_element_type=F32)


def _sig(z):
    return 1.0 / (1.0 + jnp.exp(-z))


def _silu(z):
    return z * _sig(z)


def _dsilu(z):
    s = _sig(z)
    return s * (1.0 + z * (1.0 - s))


def _full(shape):
    nd = len(shape)
    return pl.BlockSpec(shape, lambda *_: (0,) * nd)


def _arrange_w_in(w):
    z = lambda n: jnp.zeros((w.shape[0], n), w.dtype)
    return jnp.concatenate([w[:, 0:1408], z(64), w[:, 1408:1440], z(32), w[:, 1440:2480], z(112), w[:, 2480:2736]],
                           axis=1)


def _unarrange_w_in(a):
    return jnp.concatenate([a[:, 0:1408], a[:, 1472:1504], a[:, 1536:2576], a[:, 2688:2944]], axis=1)


def _arrange_w_uq(w):
    return jnp.pad(w.reshape(256, 8, 96), ((0, 0), (0, 0), (0, 32))).reshape(256, 1024)


def _unarrange_w_uq(a):
    return a.reshape(256, 8, 128)[:, :, :96].reshape(256, 768)


def _arrange_w_ukv(w):
    r = w.reshape(128, 8, 128)
    k = jnp.pad(r[:, :, :64], ((0, 0), (0, 0), (0, 64))).reshape(128, 1024)
    return jnp.concatenate([k, r[:, :, 64:].reshape(128, 512)], axis=1)


def _unarrange_w_ukv(a):
    k = a[:, :1024].reshape(128, 8, 128)[:, :, :64]
    v = a[:, 1024:].reshape(128, 8, 64)
    return jnp.concatenate([k, v], axis=2).reshape(128, 1024)


def _rope_tables(pos3, zero=0.0):
    B, S, _ = pos3.shape
    ts = min(S, 512)
    inv32 = (np.float32(ROPE_THETA) ** (-(np.arange(32, dtype=np.float32) / 32))).astype(np.float32)
    inv16 = (np.float32(ROPE_THETA) ** (-(np.arange(16, dtype=np.float32) / 16))).astype(np.float32)
    inv = np.zeros((1, 128), np.float32)
    inv[0, 0:32] = inv32
    inv[0, 32:48] = inv16

    def body(pos_ref, inv_ref, cr, sr, cm, sm):
        ang = pos_ref[0].astype(F32) * inv_ref[...]
        lane = lax.broadcasted_iota(jnp.int32, (1, 128), 1)

        def every_head(x):
            y = jnp.where(lane < 32, x, pltpu.roll(x, 32, 1))
            return jnp.where(lane < 64, y, pltpu.roll(y, 64, 1))

        def rotary_pair(x, fill):
            return jnp.where((lane >= 64) & (lane < 80), pltpu.roll(x, 32, 1),
                             jnp.where((lane >= 80) & (lane < 96), pltpu.roll(x, 48, 1), fill))

        c, s = jnp.cos(ang), jnp.sin(ang)
        cr[0] = every_head(c)
        sr[0] = every_head(s)
        cm[0] = rotary_pair(c, 1.0)
        sm[0] = rotary_pair(s, 0.0)

    tab = jax.ShapeDtypeStruct((B, S, 128), F32)
    blk = pl.BlockSpec((1, ts, 128), lambda b, i: (b, i, 0))
    return pl.pallas_call(
        body, name="rope_tables", grid=(B, S // ts),
        in_specs=[pl.BlockSpec((1, ts, 1), lambda b, i: (b, i, 0)), _full((1, 128))],
        out_specs=[blk, blk, blk, blk], out_shape=[tab, tab, tab, tab],
        compiler_params=_cp(("parallel", "parallel")),
    )(pos3, jnp.asarray(inv) + zero)


def _rope128(x, cos, sin):
    lane = lax.broadcasted_iota(jnp.int32, (1, 128), 1)
    rp = pltpu.roll(x, 16, 1)
    rm = pltpu.roll(x, 112, 1)
    return x * cos + jnp.where(lane < 80, -rm, rp) * sin


def _rope128_t(d, cos, sin):
    lane = lax.broadcasted_iota(jnp.int32, (1, 128), 1)
    y = d * sin
    yp = pltpu.roll(y, 16, 1)
    ym = pltpu.roll(y, 112, 1)
    return d * cos + jnp.where(lane < 64, 0.0, jnp.where(lane < 80, ym, jnp.where(lane < 96, -yp, 0.0)))


def _proj_fwd(x, shift, scale, nw, w_arr):
    B, S, D = x.shape
    tm = min(S, 512)

    def body(x_ref, sh_ref, sc_ref, nw_ref, w_ref, ret_ref, mla_ref, gla_ref, h_ref):
        xv = x_ref[0]
        rstd = lax.rsqrt(jnp.mean(xv * xv, axis=-1, keepdims=True) + EPS)
        h = (xv * rstd * nw_ref[...]) * (1.0 + sc_ref[0]) + sh_ref[0]
        hb = h.astype(_MXU)
        h_ref[0] = hb
        ret_ref[0] = jnp.dot(hb, w_ref[:, 0:RET_W], preferred_element_type=F32).astype(_MXU)
        mla_ref[0] = jnp.dot(hb, w_ref[:, RET_W:RET_W + MLA_W], preferred_element_type=F32).astype(_MXU)
        gla_ref[0] = jnp.dot(hb, w_ref[:, RET_W + MLA_W:ARR_W], preferred_element_type=F32).astype(_MXU)

    tok = lambda w: pl.BlockSpec((1, tm, w), lambda b, i: (b, i, 0))
    per_seq = pl.BlockSpec((1, 1, D), lambda b, i: (b, 0, 0))
    return pl.pallas_call(
        body, name="proj_fwd", grid=(B, S // tm),
        in_specs=[tok(D), per_seq, per_seq, _full((1, D)), _full((D, ARR_W))],
        out_specs=[tok(RET_W), tok(MLA_W), tok(GLA_W), tok(D)],
        out_shape=[jax.ShapeDtypeStruct((B, S, RET_W), _MXU), jax.ShapeDtypeStruct((B, S, MLA_W), _MXU),
                   jax.ShapeDtypeStruct((B, S, GLA_W), _MXU), jax.ShapeDtypeStruct((B, S, D), _MXU)],
        compiler_params=_cp(("parallel", "parallel")),
    )(x, shift, scale, nw, w_arr)


RET_L = 256


def _ret_consts(L):
    lg = np.log1p(-np.exp2(-5.0 - np.arange(4, dtype=np.float32))).astype(np.float32)
    i = np.arange(L)
    ci = i // CHUNK
    diff = (i[:, None] - i[None, :]).astype(np.float32)
    same = ci[:, None] == ci[None, :]
    past = ci[None, :] < ci[:, None]
    expo = np.where(same, np.abs(diff), np.where(past, diff, 0.0)).astype(np.float32)
    dec = np.where((same | past)[None], np.exp(lg[:, None, None] * expo[None]), 0.0).astype(np.float32)
    head = np.arange(256) // 64
    qw = np.exp((i + 1.0)[:, None] * lg[head][None, :]).astype(np.float32)
    kw = np.exp((L - 1.0 - i)[:, None] * lg[head][None, :]).astype(np.float32)
    a_row = np.exp(np.float32(L) * lg[head])[None, :].astype(np.float32)
    return [jnp.asarray(t) for t in (dec.reshape(4 * L, L), qw, kw, a_row)]


def _ret_masks():
    lane = lax.broadcasted_iota(jnp.int32, (1, 256), 1)
    mh = [(lane // 64) == h for h in range(4)]
    vi = lax.broadcasted_iota(jnp.int32, (256, 256), 0)
    ki = lax.broadcasted_iota(jnp.int32, (256, 256), 1)
    bd = (vi // 64) == (ki // 64)
    return mh, mh, bd


def _ret_rot(x, transposed=False):
    lane = lax.broadcasted_iota(jnp.int32, (1, 256), 1)
    first = (lane % 64) < 32
    up, down = pltpu.roll(x, 224, 1), pltpu.roll(x, 32, 1)
    return jnp.where(first, up, -down) if transposed else jnp.where(first, -up, down)


def _ret_rope(p, cs, sn):
    c2 = jnp.concatenate([cs, cs], axis=1)
    s2 = jnp.concatenate([sn, sn], axis=1)
    q, k = p[:, 0:256], p[:, 256:512]
    return q * c2 + _ret_rot(q) * s2, (k * c2 + _ret_rot(k) * s2) * RET_KSCALE


def _ret_rope_t(d, cs, sn):
    c2 = jnp.concatenate([cs, cs], axis=1)
    s2 = jnp.concatenate([sn, sn], axis=1)
    return d * c2 + _ret_rot(d * s2, transposed=True)


def _head_mean(x, mv, width):
    out = jnp.zeros_like(x)
    for m in mv:
        s = jnp.sum(jnp.where(m, x, 0.0), axis=-1, keepdims=True) * (1.0 / width)
        out = jnp.where(m, s, out)
    return out


def _stack_heads(x, masks):
    return jnp.concatenate([jnp.where(m, x, 0.0) for m in masks], axis=0)


def _fold_heads(xs, masks, L):
    out = jnp.where(masks[0], xs[0:L], 0.0)
    for h in range(1, 4):
        out = out + jnp.where(masks[h], xs[h * L:(h + 1) * L], 0.0)
    return out


RET_G = 2


def _ret_fwd(ret_p, cos, sin):
    B, S, _ = ret_p.shape
    L = min(RET_L, S)
    NB = S // L
    G = min(RET_G, NB)
    NG = NB // G
    consts = _ret_consts(L)

    def body(p_ref, c_ref, s_ref, ds_ref, qw_ref, kw_ref, a_ref, out_ref, raw_ref, st_ref, st_sc):
        @pl.when(pl.program_id(1) == 0)
        def _():
            st_sc[...] = jnp.zeros_like(st_sc)

        mh, mv, bd = _ret_masks()
        cs_ = range(G)
        rows = [slice(c * L, (c + 1) * L) for c in cs_]
        ps = [p_ref[0, rows[c], :].astype(F32) for c in cs_]
        qk = [_ret_rope(ps[c], c_ref[0, rows[c], :], s_ref[0, rows[c], :]) for c in cs_]
        vs = [ps[c][:, 512:768] for c in cs_]
        a_s = [_mm_nt(_stack_heads(qk[c][0], mh), qk[c][1]) for c in cs_]
        upd = [_mm_tn(vs[c], qk[c][1] * kw_ref[...]) for c in cs_]
        o_s = [_mm(a_s[c] * ds_ref[...], vs[c]) for c in cs_]
        st = st_sc[...]
        inter = []
        for c in cs_:
            st_ref[0, c] = st
            inter.append(_mm_nt(qk[c][0] * qw_ref[...], st))
            st = st * a_ref[...] + jnp.where(bd, upd[c], 0.0)
        st_sc[...] = st
        for c in cs_:
            r = _fold_heads(o_s[c], mv, L) + inter[c]
            raw_ref[0, rows[c], :] = r
            rstd = lax.rsqrt(_head_mean(r * r, mv, 64.0) + EPS)
            out_ref[0, rows[c], :] = (r * rstd * _silu(ps[c][:, 768:1024])).astype(_MXU)

    tok = lambda w: pl.BlockSpec((1, G * L, w), lambda b, n: (b, n, 0))
    return pl.pallas_call(
        body, name="ret_fwd", grid=(B, NG),
        in_specs=[tok(RET_W), tok(128), tok(128), _full((4 * L, L)), _full((L, 256)), _full((L, 256)),
                  _full((1, 256))],
        out_specs=[tok(256), tok(256), pl.BlockSpec((1, G, 256, 256), lambda b, n: (b, n, 0, 0))],
        out_shape=[jax.ShapeDtypeStruct((B, S, 256), _MXU), jax.ShapeDtypeStruct((B, S, 256), F32),
                   jax.ShapeDtypeStruct((B, NB, 256, 256), F32)],
        scratch_shapes=[pltpu.VMEM((256, 256), F32)],
        compiler_params=_cp(("parallel", "arbitrary")),
    )(ret_p, cos, sin, *consts)


def _ret_bwd(ret_p, cos, sin, raw, states, d_mix):
    B, S, _ = ret_p.shape
    L = min(RET_L, S)
    NB = S // L
    G = 1
    NG = NB // G
    consts = _ret_consts(L)

    def body(p_ref, c_ref, s_ref, raw_ref, st_ref, dm_ref, ds_ref, qw_ref, kw_ref, a_ref, dp_ref, dst_sc):
        @pl.when(pl.program_id(1) == 0)
        def _():
            dst_sc[...] = jnp.zeros_like(dst_sc)

        mh, mv, bd = _ret_masks()
        qw, kw, dec = qw_ref[...], kw_ref[...], ds_ref[...]
        cs_ = range(G)
        rows = [slice(c * L, (c + 1) * L) for c in cs_]
        ps = [p_ref[0, rows[c], :].astype(F32) for c in cs_]
        tabs = [(c_ref[0, rows[c], :], s_ref[0, rows[c], :]) for c in cs_]
        qk = [_ret_rope(ps[c], *tabs[c]) for c in cs_]
        vs = [ps[c][:, 512:768] for c in cs_]
        qs = [_stack_heads(qk[c][0], mh) for c in cs_]
        a_s = [_mm_nt(qs[c], qk[c][1]) for c in cs_]
        dr, dz = [], []
        for c in cs_:
            r = raw_ref[0, rows[c], :]
            z = ps[c][:, 768:1024]
            rstd = lax.rsqrt(_head_mean(r * r, mv, 64.0) + EPS)
            rn = r * rstd
            dm = dm_ref[0, rows[c], :]
            d_rn = dm * _silu(z)
            dz.append(dm * rn * _dsilu(z))
            dr.append(rstd * (d_rn - rn * _head_mean(d_rn * rn, mv, 64.0)))
        do_s = [_stack_heads(dr[c], mv) for c in cs_]
        da_s = [_mm_nt(do_s[c], vs[c]) for c in cs_]
        sts = [st_ref[0, c] for c in cs_]
        dq_st = [_mm(dr[c], sts[c]) for c in cs_]
        dst_in = [_mm_tn(dr[c], qk[c][0] * qw) for c in cs_]
        dv = [_mm_tn(a_s[c] * dec, do_s[c]) for c in cs_]
        dqr, dkr = [], []
        for c in cs_:
            da = da_s[c] * dec
            dqr.append(_fold_heads(_mm(da, qk[c][1]), mh, L) + dq_st[c] * qw)
            dkr.append(_mm_tn(da, qs[c]))
        dst_next = dst_sc[...]
        for c in reversed(cs_):
            g = jnp.where(bd, dst_next, 0.0)
            dv[c] = dv[c] + _mm_nt(qk[c][1] * kw, g)
            dkr[c] = dkr[c] + _mm(vs[c], g) * kw
            dst_next = dst_next * a_ref[...] + jnp.where(bd, dst_in[c], 0.0)
        dst_sc[...] = dst_next
        for c in cs_:
            dp_ref[0, rows[c], :] = jnp.concatenate(
                [_ret_rope_t(dqr[c], *tabs[c]), _ret_rope_t(dkr[c] * RET_KSCALE, *tabs[c]), dv[c], dz[c]],
                axis=1).astype(_MXU)

    tok = lambda w: pl.BlockSpec((1, G * L, w), lambda b, i: (b, NG - 1 - i, 0))
    return pl.pallas_call(
        body, name="ret_bwd", grid=(B, NG),
        in_specs=[tok(RET_W), tok(128), tok(128), tok(256),
                  pl.BlockSpec((1, G, 256, 256), lambda b, i: (b, NG - 1 - i, 0, 0)), tok(256),
                  _full((4 * L, L)), _full((L, 256)), _full((L, 256)), _full((1, 256))],
        out_specs=tok(RET_W), out_shape=jax.ShapeDtypeStruct((B, S, RET_W), _MXU),
        scratch_shapes=[pltpu.VMEM((256, 256), F32)],
        compiler_params=_cp(("parallel", "arbitrary")),
    )(ret_p, cos, sin, raw, states, d_mix, *consts)


def _gla_masks():
    C = CHUNK
    lk = lax.broadcasted_iota(jnp.int32, (1, 128), 1)
    lv = lax.broadcasted_iota(jnp.int32, (1, 256), 1)
    mk = [(lk // 32) == h for h in range(4)]
    mv = [(lv // 64) == h for h in range(4)]
    vi = lax.broadcasted_iota(jnp.int32, (256, 128), 0)
    ki = lax.broadcasted_iota(jnp.int32, (256, 128), 1)
    bd = (vi // 64) == (ki // 32)
    ri = lax.broadcasted_iota(jnp.int32, (4 * C, C), 0) % C
    cj = lax.broadcasted_iota(jnp.int32, (4 * C, C), 1)
    lower = ri >= cj
    ti = lax.broadcasted_iota(jnp.int32, (C, C), 0)
    tj = lax.broadcasted_iota(jnp.int32, (C, C), 1)
    ltri = jnp.where(ti >= tj, 1.0, 0.0).astype(F32)
    utri = jnp.where(ti <= tj, 1.0, 0.0).astype(F32)
    return mk, mv, bd, lower, ltri, utri


def _log_sigmoid(x):
    return jnp.minimum(x, 0.0) - jnp.log(1.0 + jnp.exp(-jnp.abs(x)))


GLA_G = 8


def _gla_fwd(gla_p, w_g2p, b_g2, gnw):
    B, S, _ = gla_p.shape
    C = CHUNK
    NC = S // C
    G = min(GLA_G, NC)
    NG = NC // G

    def body(p_ref, w_ref, b_ref, gn_ref, out_ref, raw_ref, st_ref, st_sc):
        @pl.when(pl.program_id(1) == 0)
        def _():
            st_sc[...] = jnp.zeros_like(st_sc)

        mk, mv, bd, lower, ltri, _ = _gla_masks()
        cs = range(G)
        rows = [slice(c * C, (c + 1) * C) for c in cs]
        ps = [p_ref[0, rows[c], :].astype(F32) for c in cs]
        pre = [_mm(ps[c][:, 512:640], w_ref[...]) + b_ref[...] for c in cs]
        cum = [_mm_f32(ltri, _log_sigmoid(pre[c]) * (1.0 / GLA_TAU)) for c in cs]
        past, fut, upd, q_pos, a_row = [], [], [], [], []
        for c in cs:
            q = ps[c][:, 0:128]
            k = ps[c][:, 128:256] * GLA_KSCALE
            last = cum[c][C - 1:C, :]
            e_pos = jnp.exp(cum[c])
            e_neg = jnp.exp(-cum[c])
            q_pos.append(q * e_pos)
            a_row.append(jnp.exp(last))
            past.append(_mm_nt(_stack_heads(q_pos[c], mk), k * e_neg))
            fut.append(_mm_nt(_stack_heads(q * e_neg, mk), k * e_pos))
            upd.append(_mm_tn(ps[c][:, 256:512], k * jnp.exp(last - cum[c])))
        o_s = [_mm(jnp.where(lower, past[c], fut[c]), ps[c][:, 256:512]) for c in cs]
        st = st_sc[...]
        inter = []
        for c in cs:
            st_ref[0, c] = st
            inter.append(_mm_nt(q_pos[c], st))
            st = st * a_row[c] + jnp.where(bd, upd[c], 0.0)
        st_sc[...] = st
        for c in cs:
            g = _fold_heads(o_s[c], mv, C) + inter[c]
            raw_ref[0, rows[c], :] = g
            rstd = lax.rsqrt(_head_mean(g * g, mv, 64.0) + EPS)
            out_ref[0, rows[c], :] = (g * rstd * gn_ref[...] * _silu(ps[c][:, 640:896])).astype(_MXU)

    tok = lambda w: pl.BlockSpec((1, G * C, w), lambda b, n: (b, n, 0))
    return pl.pallas_call(
        body, name="gla_fwd", grid=(B, NG),
        in_specs=[tok(GLA_W), _full((128, 128)), _full((1, 128)), _full((1, 256))],
        out_specs=[tok(256), tok(256), pl.BlockSpec((1, G, 256, 128), lambda b, n: (b, n, 0, 0))],
        out_shape=[jax.ShapeDtypeStruct((B, S, 256), _MXU), jax.ShapeDtypeStruct((B, S, 256), F32),
                   jax.ShapeDtypeStruct((B, NC, 256, 128), F32)],
        scratch_shapes=[pltpu.VMEM((256, 128), F32)],
        compiler_params=_cp(("parallel", "arbitrary")),
    )(gla_p, w_g2p, b_g2, gnw)


def _gla_bwd(gla_p, w_g2p, b_g2, gnw, raw, states, d_mix):
    B, S, _ = gla_p.shape
    C = CHUNK
    NC = S // C
    G = min(GLA_G, NC)
    NG = NC // G

    def body(p_ref, w_ref, b_ref, gn_ref, raw_ref, st_ref, dm_ref, dp_ref, dw_ref, db_ref, dgn_ref, dst_sc):
        first = (pl.program_id(0) == 0) & (pl.program_id(1) == 0)

        @pl.when(first)
        def _():
            dw_ref[...] = jnp.zeros_like(dw_ref)
            db_ref[...] = jnp.zeros_like(db_ref)
            dgn_ref[...] = jnp.zeros_like(dgn_ref)

        @pl.when(pl.program_id(1) == 0)
        def _():
            dst_sc[...] = jnp.zeros_like(dst_sc)

        mk, mv, bd, lower, ltri, utri = _gla_masks()
        gn = gn_ref[...]
        cs = range(G)
        rows = [slice(c * C, (c + 1) * C) for c in cs]
        ps = [p_ref[0, rows[c], :].astype(F32) for c in cs]
        vs = [ps[c][:, 256:512] for c in cs]
        pre = [_mm(ps[c][:, 512:640], w_ref[...]) + b_ref[...] for c in cs]
        cum = [_mm_f32(ltri, _log_sigmoid(pre[c]) * (1.0 / GLA_TAU)) for c in cs]
        dg, dz, dgn_acc = [], [], jnp.zeros((1, 256), F32)
        for c in cs:
            g = raw_ref[0, rows[c], :]
            z = ps[c][:, 640:896]
            rstd = lax.rsqrt(_head_mean(g * g, mv, 64.0) + EPS)
            gh = g * rstd
            dm = dm_ref[0, rows[c], :]
            d_gn = dm * _silu(z)
            dz.append(dm * gh * gn * _dsilu(z))
            dgn_acc = dgn_acc + jnp.sum(d_gn * gh, axis=0, keepdims=True)
            d_gh = d_gn * gn
            dg.append(rstd * (d_gh - gh * _head_mean(d_gh * gh, mv, 64.0)))
        do_s = [_stack_heads(dg[c], mv) for c in cs]
        dattn = [_mm_nt(do_s[c], vs[c]) for c in cs]
        ks, e_pos, e_neg, q_pos, q_neg, k_pos, k_neg, qp_s, qn_s, past, fut, a_row, w_dec, kd = ([] for _ in range(14))
        for c in cs:
            q = ps[c][:, 0:128]
            k = ps[c][:, 128:256] * GLA_KSCALE
            last = cum[c][C - 1:C, :]
            ep, en = jnp.exp(cum[c]), jnp.exp(-cum[c])
            ks.append(k), e_pos.append(ep), e_neg.append(en)
            q_pos.append(q * ep), q_neg.append(q * en), k_pos.append(k * ep), k_neg.append(k * en)
            qp_s.append(_stack_heads(q_pos[c], mk)), qn_s.append(_stack_heads(q_neg[c], mk))
            past.append(_mm_nt(qp_s[c], k_neg[c]))
            fut.append(_mm_nt(qn_s[c], k_pos[c]))
            a_row.append(jnp.exp(last))
            w_dec.append(jnp.exp(last - cum[c]))
            kd.append(k * w_dec[c])
        sts = [st_ref[0, c] for c in cs]
        dq_st = [_mm(dg[c], sts[c]) for c in cs]
        dst_in = [_mm_tn(dg[c], q_pos[c]) for c in cs]
        dv, dq_pos, dk_neg, dq_neg, dk_pos = [], [], [], [], []
        for c in cs:
            attn = jnp.where(lower, past[c], fut[c])
            dpast = jnp.where(lower, dattn[c], 0.0)
            dfut = jnp.where(lower, 0.0, dattn[c])
            dv.append(_mm_tn(attn, do_s[c]))
            dq_pos.append(_fold_heads(_mm(dpast, k_neg[c]), mk, C) + dq_st[c])
            dk_neg.append(_mm_tn(dpast, qp_s[c]))
            dq_neg.append(_fold_heads(_mm(dfut, k_pos[c]), mk, C))
            dk_pos.append(_mm_tn(dfut, qn_s[c]))
        dst_next = dst_sc[...]
        d_a, d_kd = [None] * G, [None] * G
        for c in reversed(cs):
            d_a[c] = jnp.sum(dst_next * sts[c], axis=0, keepdims=True)
            gmat = jnp.where(bd, dst_next, 0.0)
            d_kd[c] = _mm(vs[c], gmat)
            dv[c] = dv[c] + _mm_nt(kd[c], gmat)
            dst_next = dst_next * a_row[c] + jnp.where(bd, dst_in[c], 0.0)
        dst_sc[...] = dst_next
        row = lax.broadcasted_iota(jnp.int32, (C, 128), 0)
        d_la, dk, dq = [], [], []
        for c in cs:
            t = d_kd[c] * kd[c]
            dk.append(d_kd[c] * w_dec[c] + dk_neg[c] * e_neg[c] + dk_pos[c] * e_pos[c])
            dq.append(dq_pos[c] * e_pos[c] + dq_neg[c] * e_neg[c])
            d_last = jnp.sum(t, axis=0, keepdims=True) + d_a[c] * a_row[c]
            d_cum = (dq_pos[c] * q_pos[c] - dk_neg[c] * k_neg[c] - dq_neg[c] * q_neg[c] + dk_pos[c] * k_pos[c] - t)
            d_la.append(_mm_f32(utri, d_cum + jnp.where(row == C - 1, d_last, 0.0)))
        d_pre = [d_la[c] * _sig(-pre[c]) * (1.0 / GLA_TAU) for c in cs]
        d_gg = [_mm_nt(d_pre[c], w_ref[...]) for c in cs]
        dw_acc = _mm_tn(ps[0][:, 512:640], d_pre[0])
        db_acc = jnp.sum(d_pre[0], axis=0, keepdims=True)
        for c in cs[1:]:
            dw_acc = dw_acc + _mm_tn(ps[c][:, 512:640], d_pre[c])
            db_acc = db_acc + jnp.sum(d_pre[c], axis=0, keepdims=True)
        for c in cs:
            dp_ref[0, rows[c], :] = jnp.concatenate([dq[c], dk[c] * GLA_KSCALE, dv[c], d_gg[c], dz[c]],
                                                    axis=1).astype(_MXU)
        dw_ref[...] += dw_acc
        db_ref[...] += db_acc
        dgn_ref[...] += dgn_acc

        @pl.when((pl.program_id(0) == B - 1) & (pl.program_id(1) == NG - 1))
        def _():
            s1 = dgn_ref[...]
            s1 = s1 + pltpu.roll(s1, 128, 1)
            dgn_ref[...] = s1 + pltpu.roll(s1, 64, 1)

    tok = lambda w: pl.BlockSpec((1, G * C, w), lambda b, i: (b, NG - 1 - i, 0))
    return pl.pallas_call(
        body, name="gla_bwd", grid=(B, NG),
        in_specs=[tok(GLA_W), _full((128, 128)), _full((1, 128)), _full((1, 256)), tok(256),
                  pl.BlockSpec((1, G, 256, 128), lambda b, i: (b, NG - 1 - i, 0, 0)), tok(256)],
        out_specs=[tok(GLA_W), _full((128, 128)), _full((1, 128)), _full((1, 256))],
        out_shape=[jax.ShapeDtypeStruct((B, S, GLA_W), _MXU), jax.ShapeDtypeStruct((128, 128), F32),
                   jax.ShapeDtypeStruct((1, 128), F32), jax.ShapeDtypeStruct((1, 256), F32)],
        scratch_shapes=[pltpu.VMEM((256, 128), F32)],
        compiler_params=_cp(("arbitrary", "arbitrary")),
    )(gla_p, w_g2p, b_g2, gnw, raw, states, d_mix)


def _rms(x, w):
    rstd = lax.rsqrt(jnp.mean(x * x, axis=-1, keepdims=True) + EPS)
    xh = x * rstd
    return xh, rstd, xh * w


def _rms_bwd(dy, xh, rstd, w):
    dxh = dy * w
    return rstd * (dxh - xh * jnp.mean(dxh * xh, axis=-1, keepdims=True))


MLA_T = 256


def _mla_prep_fwd(mla_p, cos, sin, qnw, kvnw, w_uq, w_ukv):
    B, S, _ = mla_p.shape
    tm = min(S, 512)

    t = min(MLA_T, S)
    nt = tm // t

    def body(p_ref, c_ref, s_ref, qn_ref, kn_ref, wq_ref, wkv_ref, q_ref, k_ref, v_ref, kt_ref, vt_ref):
        p = p_ref[0].astype(F32)
        cs, sn = c_ref[0], s_ref[0]
        _, _, qn = _rms(p[:, 0:256], qn_ref[...])
        qpre = _mm(qn, wq_ref[...])
        _, _, kvn = _rms(p[:, 256:384], kn_ref[...])
        kv = _mm(kvn, wkv_ref[...])
        kpe = _rope128(p[:, 384:512], cs, sn)
        for h in range(8):
            sl = slice(128 * h, 128 * h + 128)
            q_ref[0, :, sl] = _rope128(qpre[:, sl], cs, sn).astype(_MXU)
            kh = kv[:, sl] + kpe
            k_ref[0, :, sl] = kh.astype(_MXU)
            kht = kh.T
            for n in range(nt):
                kt_ref[0, n, sl, :] = kht[:, n * t:(n + 1) * t].astype(_MXU)
        v_ref[0] = kv[:, 1024:1536].astype(_MXU)
        for pr in range(4):
            vht = kv[:, 1024 + 128 * pr:1152 + 128 * pr].T
            for n in range(nt):
                vt_ref[0, n, 128 * pr:128 * pr + 128, :] = vht[:, n * t:(n + 1) * t].astype(_MXU)

    tok = lambda w: pl.BlockSpec((1, tm, w), lambda b, i: (b, i, 0))
    tr = lambda w: pl.BlockSpec((1, nt, w, t), lambda b, i: (b, i, 0, 0))
    return pl.pallas_call(
        body, name="mla_prep_fwd", grid=(B, S // tm),
        in_specs=[tok(512), tok(128), tok(128), _full((1, 256)), _full((1, 128)), _full((256, 1024)),
                  _full((128, 1536))],
        out_specs=[tok(1024), tok(1024), tok(512), tr(1024), tr(512)],
        out_shape=[jax.ShapeDtypeStruct((B, S, 1024), _MXU), jax.ShapeDtypeStruct((B, S, 1024), _MXU),
                   jax.ShapeDtypeStruct((B, S, 512), _MXU), jax.ShapeDtypeStruct((B, S // t, 1024, t), _MXU),
                   jax.ShapeDtypeStruct((B, S // t, 512, t), _MXU)],
        compiler_params=_cp(("parallel", "parallel")),
    )(mla_p, cos, sin, qnw, kvnw, w_uq, w_ukv)


def _chunk_mask_t(t):
    kj = lax.broadcasted_iota(jnp.int32, (t, t), 0) // CHUNK
    qi = lax.broadcasted_iota(jnp.int32, (t, t), 1) // CHUNK
    return kj <= qi


MLA_HG = 8
MLA_HG_FWD = 8
LOG2E = 1.4426950408889634
MLA_C2 = MLA_SCALE * LOG2E


def _mla_attn_fwd(q, k, vt):
    B, S, _ = q.shape
    t = min(MLA_T, S)
    nq = S // t
    HG = MLA_HG_FWD
    NP = HG // 2

    def body(q_ref, k_ref, vt_ref, o_ref, lse_ref, sa, sb, m_sc, l_sc, acc_sc):
        i = pl.program_id(2)
        row = lax.broadcasted_iota(jnp.int32, (128, 1), 0)
        low = row < 64
        mask = _chunk_mask_t(t)
        m_sc[...] = jnp.full(m_sc.shape, -jnp.inf, F32)
        l_sc[...] = jnp.zeros_like(l_sc)
        acc_sc[...] = jnp.zeros_like(acc_sc)

        ones = jnp.ones((8, t), _MXU)

        def scores(j, buf):
            kb = k_ref[0, pl.ds(pl.multiple_of(j * t, t), t), :]
            for h in range(HG):
                cols = slice(128 * h, 128 * h + 128)
                buf[h] = (_mm_nt(kb[:, cols], q_ref[0, :, cols]) * MLA_C2).astype(_MXU)

        def absorb(j, buf, masked):
            vtb = vt_ref[0, j]
            for pr in range(NP):
                alphas, pvs = [], []
                for hh in range(2):
                    h = 2 * pr + hh
                    s = buf[h]
                    if masked:
                        s = jnp.where(mask, s, jnp.full_like(s, -jnp.inf))
                    m_old = m_sc[h]
                    m_new = jnp.maximum(m_old, jnp.max(s, axis=0, keepdims=True).astype(F32))
                    alpha = jnp.exp2(m_old - m_new)
                    p = jnp.exp2(s - m_new.astype(_MXU))
                    l_sc[h] = alpha * l_sc[h] + _mm(ones, p)[0:1, :]
                    m_sc[h] = m_new
                    vth = vtb[128 * pr:128 * pr + 128, :]
                    vth = jnp.where(low if hh == 0 else ~low, vth, jnp.zeros_like(vth))
                    pvs.append(_mm(vth, p))
                    alphas.append(alpha)
                acc_sc[pr] = acc_sc[pr] * jnp.where(low, alphas[0], alphas[1]) + pvs[0] + pvs[1]

        scores(0, sb)

        def pair(jj, carry):
            j0 = 2 * jj
            scores(j0 + 1, sa)
            absorb(j0, sb, False)
            scores(j0 + 2, sb)
            absorb(j0 + 1, sa, False)
            return carry

        lax.fori_loop(0, i // 2, pair, 0)

        @pl.when(i % 2 == 1)
        def _():
            scores(i, sa)
            absorb(i - 1, sb, False)
            absorb(i, sa, True)

        @pl.when(i % 2 == 0)
        def _():
            absorb(i, sb, True)

        for pr in range(NP):
            l_e, l_o = l_sc[2 * pr], l_sc[2 * pr + 1]
            o_ref[0, :, 128 * pr:128 * pr + 128] = (acc_sc[pr] / jnp.where(low, l_e, l_o)).T
            lse_ref[0, pr, 0, 0:1, :] = m_sc[2 * pr] + jnp.log(l_e) * LOG2E
            lse_ref[0, pr, 0, 1:2, :] = m_sc[2 * pr + 1] + jnp.log(l_o) * LOG2E

    return pl.pallas_call(
        body, name="mla_attn_fwd", grid=(B, 8 // HG, nq),
        in_specs=[pl.BlockSpec((1, t, 128 * HG), lambda b, g, i: (b, i, g)),
                  pl.BlockSpec((1, S, 128 * HG), lambda b, g, i: (b, 0, g)),
                  pl.BlockSpec((1, nq, 64 * HG, t), lambda b, g, i: (b, 0, g, 0))],
        out_specs=[pl.BlockSpec((1, t, 64 * HG), lambda b, g, i: (b, i, g)),
                   pl.BlockSpec((1, NP, 1, 2, t), lambda b, g, i: (b, g, i, 0, 0))],
        out_shape=[jax.ShapeDtypeStruct((B, S, 512), F32), jax.ShapeDtypeStruct((B, 4, nq, 2, t), F32)],
        scratch_shapes=[pltpu.VMEM((HG, t, t), _MXU), pltpu.VMEM((HG, t, t), _MXU), pltpu.VMEM((HG, 1, t), F32),
                        pltpu.VMEM((HG, 1, t), F32), pltpu.VMEM((NP, 128, t), F32)],
        compiler_params=_cp(("parallel", "parallel", "arbitrary")),
    )(q, k, vt)


def _mla_attn_bwd(q, k, v, kt, do, lse, dl):
    B, S, _ = q.shape
    t = min(MLA_T, S)
    nk = S // t

    HG = MLA_HG
    NP = HG // 2

    def body(q_ref, k_ref, v_ref, kt_ref, do_ref, lse_ref, dl_ref, dq_ref, dk_ref, dv_ref,
             sa, da, sb, db, dqt_sc, dk_sc, dv_sc):
        j = pl.program_id(2)

        @pl.when(j == 0)
        def _():
            dqt_sc[...] = jnp.zeros_like(dqt_sc)

        dk_sc[...] = jnp.zeros_like(dk_sc)
        dv_sc[...] = jnp.zeros_like(dv_sc)
        lane = lax.broadcasted_iota(jnp.int32, (1, 128), 1)
        low = lane < 64
        mask = _chunk_mask_t(t)

        def half(x, hh):
            return jnp.where(low if hh == 0 else ~low, x, jnp.zeros_like(x))

        def prepare(i, sbuf, dbuf):
            rows = pl.ds(pl.multiple_of(i * t, t), t)
            for h in range(HG):
                cols = slice(128 * h, 128 * h + 128)
                pc = slice(128 * (h // 2), 128 * (h // 2) + 128)
                sbuf[h] = _mm_nt(k_ref[0, :, cols], q_ref[0, rows, cols]) * MLA_C2
                dbuf[h] = _mm_nt(half(v_ref[0, :, pc], h % 2), do_ref[0, rows, pc])

        def absorb(i, sbuf, dbuf, masked):
            rows = pl.ds(pl.multiple_of(i * t, t), t)
            for h in range(HG):
                pr, hh = h // 2, h % 2
                cols = slice(128 * h, 128 * h + 128)
                pc = slice(128 * pr, 128 * pr + 128)
                p = jnp.exp2(sbuf[h] - lse_ref[0, pr, i][hh:hh + 1, :])
                if masked:
                    p = jnp.where(mask, p, 0.0)
                dv_sc[pr] += _mm(p, half(do_ref[0, rows, pc], hh))
                ds = p * (dbuf[h] - dl_ref[0, pr, i][hh:hh + 1, :])
                dqt_sc[i, cols, :] += _mm(kt_ref[0, 0, cols, :], ds)
                dk_sc[h] += _mm(ds, q_ref[0, rows, cols])

        n = nk - 1 - j
        prepare(jnp.minimum(j + 1, nk - 1), sb, db)

        def pair(jj, carry):
            i0 = j + 1 + 2 * jj
            prepare(i0 + 1, sa, da)
            absorb(i0, sb, db, False)
            prepare(jnp.where(i0 + 2 <= nk - 1, i0 + 2, j), sb, db)
            absorb(i0 + 1, sa, da, False)
            return carry

        lax.fori_loop(0, n // 2, pair, 0)

        @pl.when(n % 2 == 1)
        def _():
            prepare(j, sa, da)
            absorb(nk - 1, sb, db, False)
            absorb(j, sa, da, True)

        @pl.when(n % 2 == 0)
        def _():
            absorb(j, sb, db, True)

        for h in range(HG):
            dk_ref[0, :, 128 * h:128 * h + 128] = (dk_sc[h] * MLA_SCALE).astype(_MXU)
        for pr in range(NP):
            dv_ref[0, :, 128 * pr:128 * pr + 128] = dv_sc[pr].astype(_MXU)

        @pl.when(j == nk - 1)
        def _():
            for i in range(nk):
                dq_ref[0, i * t:(i + 1) * t, :] = (dqt_sc[i].T * MLA_SCALE).astype(_MXU)

    seq = lambda w: pl.BlockSpec((1, S, w), lambda b, g, j: (b, 0, g))
    blk = lambda w: pl.BlockSpec((1, t, w), lambda b, g, j: (b, j, g))
    stat = pl.BlockSpec((1, NP, nk, 2, t), lambda b, g, j: (b, g, 0, 0, 0))
    return pl.pallas_call(
        body, name="mla_attn_bwd", grid=(B, 8 // HG, nk),
        in_specs=[seq(128 * HG), blk(128 * HG), blk(64 * HG),
                  pl.BlockSpec((1, 1, 128 * HG, t), lambda b, g, j: (b, j, g, 0)), seq(64 * HG), stat, stat],
        out_specs=[seq(128 * HG), blk(128 * HG), blk(64 * HG)],
        out_shape=[jax.ShapeDtypeStruct((B, S, 1024), _MXU), jax.ShapeDtypeStruct((B, S, 1024), _MXU),
                   jax.ShapeDtypeStruct((B, S, 512), _MXU)],
        scratch_shapes=[pltpu.VMEM((HG, t, t), F32), pltpu.VMEM((HG, t, t), F32), pltpu.VMEM((HG, t, t), F32),
                        pltpu.VMEM((HG, t, t), F32), pltpu.VMEM((nk, 128 * HG, t), F32),
                        pltpu.VMEM((HG, t, 128), F32), pltpu.VMEM((NP, t, 128), F32)],
        compiler_params=_cp(("parallel", "parallel", "arbitrary"), 56),
    )(q, k, v, kt, do, lse, dl)


def _mla_prep_bwd(mla_p, cos, sin, qnw, kvnw, w_uq, w_ukv, dq, dk, dv):
    B, S, _ = mla_p.shape
    tm = min(S, 512)

    def body(p_ref, c_ref, s_ref, qn_ref, kn_ref, wq_ref, wkv_ref, dq_ref, dk_ref, dv_ref,
             dp_ref, dwq_ref, dwkv_ref, dqn_ref, dkn_ref):
        first = (pl.program_id(0) == 0) & (pl.program_id(1) == 0)

        @pl.when(first)
        def _():
            dwq_ref[...] = jnp.zeros_like(dwq_ref)
            dwkv_ref[...] = jnp.zeros_like(dwkv_ref)
            dqn_ref[...] = jnp.zeros_like(dqn_ref)
            dkn_ref[...] = jnp.zeros_like(dkn_ref)

        p = p_ref[0].astype(F32)
        cs, sn = c_ref[0], s_ref[0]
        lane = lax.broadcasted_iota(jnp.int32, (1, 128), 1)
        pe = (lane >= 64) & (lane < 96)
        qh, q_rstd, qn = _rms(p[:, 0:256], qn_ref[...])
        kvh, kv_rstd, kvn = _rms(p[:, 256:384], kn_ref[...])
        dqv = dq_ref[0].astype(F32)
        dkv = dk_ref[0].astype(F32)
        dqpre = jnp.concatenate(
            [_rope128_t(dqv[:, 128 * h:128 * h + 128], cs, sn) for h in range(8)], axis=1)
        dkpe = jnp.zeros((tm, 128), F32)
        for h in range(8):
            dkpe = dkpe + jnp.where(pe, dkv[:, 128 * h:128 * h + 128], 0.0)
        dkr = _rope128_t(dkpe, cs, sn)
        dkv_all = jnp.concatenate([dkv, dv_ref[0].astype(F32)], axis=1)
        d_qn = _mm_nt(dqpre, wq_ref[...])
        d_kvn = _mm_nt(dkv_all, wkv_ref[...])
        dwq_ref[...] += _mm_tn(qn, dqpre)
        dwkv_ref[...] += _mm_tn(kvn, dkv_all)
        dqn_ref[...] += jnp.sum(d_qn * qh, axis=0, keepdims=True)
        dkn_ref[...] += jnp.sum(d_kvn * kvh, axis=0, keepdims=True)
        dp_ref[0] = jnp.concatenate([_rms_bwd(d_qn, qh, q_rstd, qn_ref[...]),
                                     _rms_bwd(d_kvn, kvh, kv_rstd, kn_ref[...]), dkr], axis=1).astype(_MXU)

    tok = lambda w: pl.BlockSpec((1, tm, w), lambda b, i: (b, i, 0))
    return pl.pallas_call(
        body, name="mla_prep_bwd", grid=(B, S // tm),
        in_specs=[tok(512), tok(128), tok(128), _full((1, 256)), _full((1, 128)), _full((256, 1024)),
                  _full((128, 1536)), tok(1024), tok(1024), tok(512)],
        out_specs=[tok(512), _full((256, 1024)), _full((128, 1536)), _full((1, 256)), _full((1, 128))],
        out_shape=[jax.ShapeDtypeStruct((B, S, 512), _MXU), jax.ShapeDtypeStruct((256, 1024), F32),
                   jax.ShapeDtypeStruct((128, 1536), F32), jax.ShapeDtypeStruct((1, 256), F32),
                   jax.ShapeDtypeStruct((1, 128), F32)],
        compiler_params=_cp(("arbitrary", "arbitrary")),
    )(mla_p, cos, sin, qnw, kvnw, w_uq, w_ukv, dq, dk, dv)


def _out_fwd(x, gate, r_g, o_mla, mla_p, g_g, w_out):
    B, S, D = x.shape
    tm = min(S, 512)

    def body(x_ref, g_ref, r_ref, o_ref, z_ref, gg_ref, w_ref, xn_ref, y_ref, mm_ref):
        mm = (o_ref[0] * _silu(z_ref[0].astype(F32))).astype(_MXU)
        mm_ref[0] = mm
        y = (jnp.dot(r_ref[0], w_ref[0:256, :], preferred_element_type=F32)
             + jnp.dot(mm, w_ref[256:768, :], preferred_element_type=F32)
             + jnp.dot(gg_ref[0], w_ref[768:1024, :], preferred_element_type=F32))
        y_ref[0] = y.astype(_MXU)
        xn_ref[0] = x_ref[0] + g_ref[0] * y

    tok = lambda w, c=0: pl.BlockSpec((1, tm, w), lambda b, i: (b, i, c))
    return pl.pallas_call(
        body, name="out_fwd", grid=(B, S // tm),
        in_specs=[tok(D), pl.BlockSpec((1, 1, D), lambda b, i: (b, 0, 0)), tok(256), tok(512), tok(512, 1),
                  tok(256), _full((D, D))],
        out_specs=[tok(D), tok(D), tok(512)],
        out_shape=[jax.ShapeDtypeStruct((B, S, D), F32), jax.ShapeDtypeStruct((B, S, D), _MXU),
                   jax.ShapeDtypeStruct((B, S, 512), _MXU)],
        compiler_params=_cp(("parallel", "parallel")),
    )(x, gate, r_g, o_mla, mla_p, g_g, w_out)


def _out_bwd(dx, y, gate, r_g, mm, g_g, w_out, o_mla, mla_p):
    B, S, D = dx.shape
    tm = min(S, 512)
    t = min(MLA_T, S)
    nt = tm // t

    def body(dx_ref, y_ref, g_ref, r_ref, mm_ref, gg_ref, w_ref, o_ref, z_ref,
             dr_ref, do_ref, dz_ref, dl_ref, dg_ref, dw_ref, dgate_ref):
        first = (pl.program_id(0) == 0) & (pl.program_id(1) == 0)

        @pl.when(first)
        def _():
            dw_ref[...] = jnp.zeros_like(dw_ref)

        @pl.when(pl.program_id(1) == 0)
        def _():
            dgate_ref[...] = jnp.zeros_like(dgate_ref)

        dxv = dx_ref[0]
        dgate_ref[0] += jnp.sum(dxv * y_ref[0].astype(F32), axis=0, keepdims=True)
        dy = (dxv * g_ref[0]).astype(_MXU)
        dr_ref[0] = _mm_nt(dy, w_ref[0:256, :])
        dg_ref[0] = _mm_nt(dy, w_ref[768:1024, :])
        dw_ref[0:256, :] += _mm_tn(r_ref[0], dy)
        dw_ref[256:768, :] += _mm_tn(mm_ref[0], dy)
        dw_ref[768:1024, :] += _mm_tn(gg_ref[0], dy)
        dm = _mm_nt(dy, w_ref[256:768, :])
        ov, z = o_ref[0], z_ref[0].astype(F32)
        do = dm * _silu(z)
        dz_ref[0] = (dm * ov * _dsilu(z)).astype(_MXU)
        do_ref[0] = do.astype(_MXU)
        prod = do * ov
        for pr in range(4):
            pt = prod[:, 128 * pr:128 * pr + 128].T
            se = jnp.sum(pt[0:64], axis=0, keepdims=True)
            so = jnp.sum(pt[64:128], axis=0, keepdims=True)
            for n in range(nt):
                dl_ref[0, pr, n, 0:1, :] = se[:, n * t:(n + 1) * t]
                dl_ref[0, pr, n, 1:2, :] = so[:, n * t:(n + 1) * t]

    tok = lambda w, c=0: pl.BlockSpec((1, tm, w), lambda b, i: (b, i, c))
    per_seq = pl.BlockSpec((1, 1, D), lambda b, i: (b, 0, 0))
    return pl.pallas_call(
        body, name="out_bwd", grid=(B, S // tm),
        in_specs=[tok(D), tok(D), per_seq, tok(256), tok(512), tok(256), _full((D, D)), tok(512), tok(512, 1)],
        out_specs=[tok(256), tok(512), tok(512), pl.BlockSpec((1, 4, nt, 2, t), lambda b, i: (b, 0, i, 0, 0)),
                   tok(256), _full((D, D)), per_seq],
        out_shape=[jax.ShapeDtypeStruct((B, S, 256), F32), jax.ShapeDtypeStruct((B, S, 512), _MXU),
                   jax.ShapeDtypeStruct((B, S, 512), _MXU), jax.ShapeDtypeStruct((B, 4, S // t, 2, t), F32),
                   jax.ShapeDtypeStruct((B, S, 256), F32), jax.ShapeDtypeStruct((D, D), F32),
                   jax.ShapeDtypeStruct((B, 1, D), F32)],
        compiler_params=_cp(("arbitrary", "arbitrary")),
    )(dx, y, gate, r_g, mm, g_g, w_out, o_mla, mla_p)


def _proj_bwd_x(x, shift, scale, nw, w_arr, d_ret, d_mla, d_mz, d_gla, dx_out):
    B, S, D = x.shape
    tm = min(S, 512)

    def body(x_ref, sc_ref, nw_ref, w_ref, dr_ref, dm_ref, dz_ref, dg_ref, dxo_ref,
             dx_ref, dsh_ref, dsc_ref, dnw_ref):
        first = (pl.program_id(0) == 0) & (pl.program_id(1) == 0)

        @pl.when(first)
        def _():
            dnw_ref[...] = jnp.zeros_like(dnw_ref)

        @pl.when(pl.program_id(1) == 0)
        def _():
            dsh_ref[...] = jnp.zeros_like(dsh_ref)
            dsc_ref[...] = jnp.zeros_like(dsc_ref)

        dp = jnp.concatenate([dr_ref[0], dm_ref[0], dz_ref[0], dg_ref[0]], axis=1)
        dh = lax.dot_general(dp, w_ref[...], (((1,), (1,)), ((), ())), preferred_element_type=F32)
        xv = x_ref[0]
        rstd = lax.rsqrt(jnp.mean(xv * xv, axis=-1, keepdims=True) + EPS)
        xh = xv * rstd
        nwv = nw_ref[...]
        mod = 1.0 + sc_ref[0]
        dsh_ref[0] += jnp.sum(dh, axis=0, keepdims=True)
        dsc_ref[0] += jnp.sum(dh * xh * nwv, axis=0, keepdims=True)
        dnw_ref[...] += jnp.sum(dh * xh * mod, axis=0, keepdims=True)
        dxh = dh * nwv * mod
        dx_ref[0] = dxo_ref[0] + rstd * (dxh - xh * jnp.mean(dxh * xh, axis=-1, keepdims=True))

    tok = lambda w: pl.BlockSpec((1, tm, w), lambda b, i: (b, i, 0))
    per_seq = pl.BlockSpec((1, 1, D), lambda b, i: (b, 0, 0))
    return pl.pallas_call(
        body, name="proj_bwd_x", grid=(B, S // tm),
        in_specs=[tok(D), per_seq, _full((1, D)), _full((D, ARR_W)), tok(RET_W), tok(512), tok(512),
                  tok(GLA_W), tok(D)],
        out_specs=[tok(D), per_seq, per_seq, _full((1, D))],
        out_shape=[jax.ShapeDtypeStruct((B, S, D), F32), jax.ShapeDtypeStruct((B, 1, D), F32),
                   jax.ShapeDtypeStruct((B, 1, D), F32), jax.ShapeDtypeStruct((1, D), F32)],
        compiler_params=_cp(("arbitrary", "arbitrary")),
    )(x, scale, nw, w_arr, d_ret, d_mla, d_mz, d_gla, dx_out)


def _proj_bwd_w(h, d_ret, d_mla, d_mz, d_gla):
    B, S, D = h.shape
    tm = min(S, 512)

    def body(h_ref, dr_ref, dm_ref, dz_ref, dg_ref, dw_ref):
        first = (pl.program_id(0) == 0) & (pl.program_id(1) == 0)

        @pl.when(first)
        def _():
            dw_ref[...] = jnp.zeros_like(dw_ref)

        hv = h_ref[0]
        tn = lambda d_ref: lax.dot_general(hv, d_ref[0], (((0,), (0,)), ((), ())), preferred_element_type=F32)
        dw_ref[:, 0:RET_W] += tn(dr_ref)
        dw_ref[:, RET_W:RET_W + 512] += tn(dm_ref)
        dw_ref[:, RET_W + 512:RET_W + MLA_W] += tn(dz_ref)
        dw_ref[:, RET_W + MLA_W:ARR_W] += tn(dg_ref)

    tok = lambda w: pl.BlockSpec((1, tm, w), lambda b, i: (b, i, 0))
    return pl.pallas_call(
        body, name="proj_bwd_w", grid=(B, S // tm),
        in_specs=[tok(D), tok(RET_W), tok(512), tok(512), tok(GLA_W)],
        out_specs=_full((D, ARR_W)), out_shape=jax.ShapeDtypeStruct((D, ARR_W), F32),
        compiler_params=_cp(("arbitrary", "arbitrary"), 56),
    )(h, d_ret, d_mla, d_mz, d_gla)


def _out_fwd_loss(x, gate, r_g, o_mla, mla_p, g_g, w_out, fw, target):
    B, S, D = x.shape
    tm = min(S, 512)

    def body(x_ref, g_ref, r_ref, o_ref, z_ref, gg_ref, w_ref, fw_ref, t_ref, dx_ref, y_ref, mm_ref, loss_ref, dfw_ref):
        first = (pl.program_id(0) == 0) & (pl.program_id(1) == 0)

        @pl.when(first)
        def _():
            loss_ref[...] = jnp.zeros_like(loss_ref)
            dfw_ref[...] = jnp.zeros_like(dfw_ref)

        mm = (o_ref[0] * _silu(z_ref[0].astype(F32))).astype(_MXU)
        mm_ref[0] = mm
        y = (jnp.dot(r_ref[0], w_ref[0:256, :], preferred_element_type=F32)
             + jnp.dot(mm, w_ref[256:768, :], preferred_element_type=F32)
             + jnp.dot(gg_ref[0], w_ref[768:1024, :], preferred_element_type=F32))
        y_ref[0] = y.astype(_MXU)
        xv = x_ref[0] + g_ref[0] * y
        fwv = fw_ref[...]
        rstd = lax.rsqrt(jnp.mean(xv * xv, axis=-1, keepdims=True) + EPS)
        xh = xv * rstd
        err = xh * fwv - t_ref[0]
        loss_ref[...] += 0.5 * jnp.sum(jnp.mean(err * err, axis=-1, keepdims=True), axis=0, keepdims=True)
        dy = err * (1.0 / D)
        dfw_ref[...] += jnp.sum(dy * xh, axis=0, keepdims=True)
        dxh = dy * fwv
        dx_ref[0] = rstd * (dxh - xh * jnp.mean(dxh * xh, axis=-1, keepdims=True))

    tok = lambda w, c=0: pl.BlockSpec((1, tm, w), lambda b, i: (b, i, c))
    return pl.pallas_call(
        body, name="out_fwd_loss", grid=(B, S // tm),
        in_specs=[tok(D), pl.BlockSpec((1, 1, D), lambda b, i: (b, 0, 0)), tok(256), tok(512), tok(512, 1),
                  tok(256), _full((D, D)), _full((1, D)), tok(D)],
        out_specs=[tok(D), tok(D), tok(512), _full((1, 1)), _full((1, D))],
        out_shape=[jax.ShapeDtypeStruct((B, S, D), F32), jax.ShapeDtypeStruct((B, S, D), _MXU),
                   jax.ShapeDtypeStruct((B, S, 512), _MXU), jax.ShapeDtypeStruct((1, 1), F32),
                   jax.ShapeDtypeStruct((1, D), F32)],
        compiler_params=_cp(("arbitrary", "arbitrary")),
    )(x, gate, r_g, o_mla, mla_p, g_g, w_out, fw, target)


def _local_step(x, pos3, mod, loss_target, small, w_in_a, w_uq_a, w_ukv_a, w_out_b):
    B, S, D = x.shape
    tabs = _rope_tables(pos3)
    saved = []
    for l in range(DEPTH):
        last = (small["final_norm"].reshape(1, D), loss_target) if l == DEPTH - 1 else None
        x, s = _layer_fwd(x, tabs, mod[l], {n: a[l] for n, a in small.items() if n != "final_norm"},
                          w_in_a[l], w_uq_a[l], w_ukv_a[l], w_out_b[l], loss_head=last)
        saved.append(s)
    dx, loss, d_fw = x
    grads = dict(final_norm=d_fw.reshape(D))
    per_layer = [None] * DEPTH
    for l in reversed(range(DEPTH)):
        dx, per_layer[l] = _layer_bwd(dx, saved[l], tabs)
    for name in per_layer[0]:
        grads[name] = jnp.stack([per_layer[l][name] for l in range(DEPTH)])
    return loss, dx, grads


def _layer_fwd(x, tabs, mod_l, small_l, w_in_a, w_uq_a=None, w_ukv_a=None, w_out_b=None, late_weights=None,
               loss_head=None):
    B, S, D = x.shape
    cr, sr, cm, sm = tabs
    shift = mod_l[:, 0:D].reshape(B, 1, D)
    scale = mod_l[:, D:2 * D].reshape(B, 1, D)
    gate = mod_l[:, 2 * D:3 * D].reshape(B, 1, D)
    nw = small_l["norm_w"].reshape(1, D)
    qnw = small_l["mla_q_norm"].reshape(1, 256)
    kvnw = small_l["mla_kv_norm"].reshape(1, 128)
    w_g2p = jnp.pad(small_l["gla_w_g2"], ((0, 112), (0, 0)))
    b_g2 = small_l["gla_b_g2"].reshape(1, 128)
    gnw = jnp.tile(small_l["gla_norm"], 4).reshape(1, 256)
    ret_p, mla_p, gla_p, h = _proj_fwd(x, shift, scale, nw, w_in_a)
    r_g, r_raw, r_st = _ret_fwd(ret_p, cr, sr)
    if late_weights is not None:
        w_uq_a, w_ukv_a, w_out_b = late_weights(r_raw)
    q, k, v, kt, vt = _mla_prep_fwd(mla_p, cm, sm, qnw, kvnw, w_uq_a, w_ukv_a)
    o_mla, lse = _mla_attn_fwd(q, k, vt)
    g_g, g_raw, g_st = _gla_fwd(gla_p, w_g2p, b_g2, gnw)
    if loss_head is None:
        x_new, y, mm = _out_fwd(x, gate, r_g, o_mla, mla_p, g_g, w_out_b)
    else:
        dx, y, mm, loss, d_fw = _out_fwd_loss(x, gate, r_g, o_mla, mla_p, g_g, w_out_b, *loss_head)
        x_new = (dx, loss, d_fw)
    saved = dict(x=x, shift=shift, scale=scale, gate=gate, nw=nw, qnw=qnw, kvnw=kvnw, w_g2p=w_g2p, b_g2=b_g2,
                 gnw=gnw, ret_p=ret_p, mla_p=mla_p, gla_p=gla_p, h=h, r_g=r_g, r_raw=r_raw, r_st=r_st, q=q, k=k,
                 v=v, kt=kt, o_mla=o_mla, lse=lse, g_g=g_g, g_raw=g_raw, g_st=g_st, y=y, mm=mm,
                 w_in_a=w_in_a, w_uq_a=w_uq_a, w_ukv_a=w_ukv_a, w_out_b=w_out_b)
    return x_new, saved


def _layer_bwd(dx, s, tabs, early_grads=None):
    B, S, D = dx.shape
    cr, sr, cm, sm = tabs
    d_r, do, d_mz, dl, d_g, dw_out, d_gate = _out_bwd(dx, s["y"], s["gate"], s["r_g"], s["mm"], s["g_g"], s["w_out_b"],
                                                      s["o_mla"], s["mla_p"])
    d_ret = _ret_bwd(s["ret_p"], cr, sr, s["r_raw"], s["r_st"], d_r)
    dq, dk, dv = _mla_attn_bwd(s["q"], s["k"], s["v"], s["kt"], do, s["lse"], dl)
    d_mla, dw_uq, dw_ukv, d_qnw, d_kvnw = _mla_prep_bwd(
        s["mla_p"], cm, sm, s["qnw"], s["kvnw"], s["w_uq_a"], s["w_ukv_a"], dq, dk, dv)
    gnw = s["gnw"] if early_grads is None else s["gnw"] + early_grads(dw_out, dw_uq, dw_ukv)
    d_gla, dw_g2p, db_g2, d_gnw = _gla_bwd(s["gla_p"], s["w_g2p"], s["b_g2"], gnw, s["g_raw"], s["g_st"], d_g)
    dx, d_shift, d_scale, d_nw = _proj_bwd_x(s["x"], s["shift"], s["scale"], s["nw"], s["w_in_a"],
                                             d_ret, d_mla, d_mz, d_gla, dx)
    dw_in = _proj_bwd_w(s["h"], d_ret, d_mla, d_mz, d_gla)
    grads = dict(
        d_mod=jnp.concatenate([d_shift, d_scale, d_gate], axis=2).reshape(B, 3 * D),
        norm_w=d_nw.reshape(D), mla_q_norm=d_qnw.reshape(256), mla_kv_norm=d_kvnw.reshape(128),
        gla_w_g2=dw_g2p[0:16], gla_b_g2=db_g2.reshape(128), gla_norm256=d_gnw.reshape(256),
        w_in_a=dw_in, w_uq_a=dw_uq, w_ukv_a=dw_ukv, w_out=dw_out)
    return dx, grads


def _exchange(arrs, gather, name):
    n = len(arrs)
    out_shape = [jax.ShapeDtypeStruct(((N_DEV,) + a.shape) if g else a.shape, a.dtype)
                 for a, g in zip(arrs, gather)]

    def body(*refs):
        ins, outs = refs[:n], refs[n:2 * n]
        send_sems, recv_sems, local_sems = refs[2 * n:]
        ix, iy, ic = lax.axis_index("x"), lax.axis_index("y"), lax.axis_index("c")
        me = 4 * ix + 2 * iy + ic
        copies = []
        for a in range(n):
            mine = ins[a] if gather[a] else ins[a].at[me]
            loc = pltpu.make_async_copy(mine, outs[a].at[me], local_sems.at[a])
            loc.start()
            copies.append(loc)
            for d in range(1, N_DEV):
                px = 1 - ix if d & 4 else ix
                py = 1 - iy if d & 2 else iy
                pc = 1 - ic if d & 1 else ic
                src = ins[a] if gather[a] else ins[a].at[4 * px + 2 * py + pc]
                cp = pltpu.make_async_remote_copy(
                    src_ref=src, dst_ref=outs[a].at[me], send_sem=send_sems.at[a, d - 1],
                    recv_sem=recv_sems.at[a, d - 1], device_id=(px, py, pc), device_id_type=pl.DeviceIdType.MESH)
                cp.start()
                copies.append(cp)
        for cp in copies:
            cp.wait()

    any_spec = pl.BlockSpec(memory_space=pl.ANY)
    outs = pl.pallas_call(
        body, name=name, in_specs=[any_spec] * n, out_specs=[any_spec] * n, out_shape=out_shape,
        scratch_shapes=[pltpu.SemaphoreType.DMA((n, N_DEV - 1)), pltpu.SemaphoreType.DMA((n, N_DEV - 1)),
                        pltpu.SemaphoreType.DMA((n,))],
    )(*arrs)
    return list(outs)


def _peers(ix, iy, ic):
    out = []
    for d in range(1, N_DEV):
        px = 1 - ix if d & 4 else ix
        py = 1 - iy if d & 2 else iy
        pc = 1 - ic if d & 1 else ic
        out.append((d - 1, (px, py, pc), 4 * px + 2 * py + pc))
    return out


def _exchange_start(arrs, gather, name, after=None):
    n = len(arrs)
    lands = [lax.empty(((N_DEV,) + a.shape) if g else a.shape, a.dtype) for a, g in zip(arrs, gather)]
    extra = [] if after is None else [after]

    def body(*refs):
        ins, land_refs = refs[:n], refs[n:2 * n]
        send_sems, recv_sems = refs[2 * n + len(extra)], refs[2 * n + len(extra) + 1]
        token = refs[-1]
        ix, iy, ic = lax.axis_index("x"), lax.axis_index("y"), lax.axis_index("c")
        me = 4 * ix + 2 * iy + ic
        for a in range(n):
            for k, peer, peer_idx in _peers(ix, iy, ic):
                pltpu.make_async_remote_copy(
                    src_ref=ins[a] if gather[a] else ins[a].at[peer_idx], dst_ref=land_refs[a].at[me],
                    send_sem=send_sems.at[7 * a + k], recv_sem=recv_sems.at[7 * a + k], device_id=peer,
                    device_id_type=pl.DeviceIdType.MESH).start()
        token[...] = jnp.zeros_like(token)

    hbm = pl.BlockSpec(memory_space=pltpu.HBM)
    sem = pl.BlockSpec(memory_space=pltpu.SEMAPHORE)
    held = [pltpu.with_memory_space_constraint(a, pltpu.HBM) for a in list(arrs) + lands]
    outs = pl.pallas_call(
        body, name=name,
        out_shape=(pltpu.SemaphoreType.DMA((7 * n,)), pltpu.SemaphoreType.DMA((7 * n,)),
                   *[pltpu.HBM(a.shape, a.dtype) for a in held], jax.ShapeDtypeStruct((8, 128), F32)),
        in_specs=[hbm] * (2 * n) + [pl.BlockSpec(memory_space=pl.ANY)] * len(extra),
        out_specs=(sem, sem, *[hbm] * (2 * n), pl.BlockSpec(memory_space=pltpu.VMEM)),
        input_output_aliases={a: 2 + a for a in range(2 * n)},
        compiler_params=pltpu.CompilerParams(has_side_effects=pltpu.SideEffectType.DATAFLOW_SIDE_EFFECTING),
    )(*held, *extra)
    return dict(send=outs[0], recv=outs[1], srcs=list(outs[2:2 + n]), lands=list(outs[2 + n:2 + 2 * n]),
                token=outs[-1], gather=list(gather))


def _exchange_wait(flight, after, me, name):
    n = len(flight["srcs"])
    gather = flight["gather"]

    def body(*refs):
        srcs, land_refs = refs[:n], refs[n:2 * n]
        send_sems, recv_sems = refs[2 * n], refs[2 * n + 1]
        ix, iy, ic = lax.axis_index("x"), lax.axis_index("y"), lax.axis_index("c")
        mine = 4 * ix + 2 * iy + ic
        for a in range(n):
            for k, peer, peer_idx in _peers(ix, iy, ic):
                cp = pltpu.make_async_remote_copy(
                    src_ref=srcs[a] if gather[a] else srcs[a].at[peer_idx], dst_ref=land_refs[a].at[mine],
                    send_sem=send_sems.at[7 * a + k], recv_sem=recv_sems.at[7 * a + k], device_id=peer,
                    device_id_type=pl.DeviceIdType.MESH)
                cp.wait_send()
                cp.wait_recv()

    hbm = pl.BlockSpec(memory_space=pltpu.HBM)
    sem = pl.BlockSpec(memory_space=pltpu.SEMAPHORE)
    held = flight["srcs"] + flight["lands"]
    outs = pl.pallas_call(
        body, name=name, out_shape=tuple(pltpu.HBM(a.shape, a.dtype) for a in held),
        in_specs=[hbm] * (2 * n) + [sem, sem, pl.BlockSpec(memory_space=pl.ANY)], out_specs=tuple([hbm] * (2 * n)),
        input_output_aliases={a: a for a in range(2 * n)},
        compiler_params=pltpu.CompilerParams(has_side_effects=pltpu.SideEffectType.DATAFLOW_SIDE_EFFECTING),
    )(*held, flight["send"], flight["recv"], after)
    got = []
    for a in range(n):
        src, land = outs[a], outs[n + a]
        own = src if gather[a] else lax.dynamic_index_in_dim(src, me, axis=0, keepdims=False)
        got.append(lax.dynamic_update_index_in_dim(land, own, me, axis=0))
    return got


def _ada_fwd(c_all, ada_w, ada_b_cols):
    nb, D = c_all.shape
    cols = ada_w.shape[2]

    def body(c_ref, w_ref, b_ref, out_ref):
        ca = _silu(c_ref[...])
        for l in range(DEPTH):
            out_ref[l] = _mm(ca, w_ref[l]) + b_ref[l:l + 1, :]

    return pl.pallas_call(
        body, name="ada_fwd", out_shape=jax.ShapeDtypeStruct((DEPTH, nb, cols), F32),
        in_specs=[pl.BlockSpec(memory_space=pltpu.VMEM)] * 3, out_specs=pl.BlockSpec(memory_space=pltpu.VMEM),
        compiler_params=pltpu.CompilerParams(vmem_limit_bytes=32 * VMEM_MB),
    )(c_all, ada_w, ada_b_cols)


def _ada_bwd(c_all, d_mod_cols):
    nb, D = c_all.shape
    cols = d_mod_cols.shape[2]

    def body(c_ref, dm_ref, out_ref):
        ca = _silu(c_ref[...])
        for l in range(DEPTH):
            out_ref[l] = _mm_tn(ca, dm_ref[l])

    return pl.pallas_call(
        body, name="ada_bwd", out_shape=jax.ShapeDtypeStruct((DEPTH, D, cols), F32),
        in_specs=[pl.BlockSpec(memory_space=pltpu.VMEM)] * 2, out_specs=pl.BlockSpec(memory_space=pltpu.VMEM),
        compiler_params=pltpu.CompilerParams(vmem_limit_bytes=32 * VMEM_MB),
    )(c_all, d_mod_cols)


def _sum_adamw(parts, w, m, v, name):
    P, R, C = parts.shape
    tr = 256 if (R % 256 == 0 and R > 256) else R

    def body(p_ref, w_ref, m_ref, v_ref, g_ref, d_ref, nm_ref, nv_ref):
        g = p_ref[0].astype(F32)
        for k in range(1, P):
            g = g + p_ref[k].astype(F32)
        g_ref[...] = g
        nm = ADAM_B1 * m_ref[...] + (1.0 - ADAM_B1) * g
        nv = ADAM_B2 * v_ref[...] + (1.0 - ADAM_B2) * (g * g)
        nm_ref[...] = nm
        nv_ref[...] = nv
        m_hat = nm / (1.0 - ADAM_B1 ** ADAM_STEP)
        v_hat = nv / (1.0 - ADAM_B2 ** ADAM_STEP)
        d_ref[...] = -ADAM_LR * (m_hat / (jnp.sqrt(v_hat) + ADAM_EPS) + ADAM_WD * w_ref[...])

    blk = pl.BlockSpec((tr, C), lambda i: (i, 0))
    shp = jax.ShapeDtypeStruct((R, C), F32)
    return pl.pallas_call(
        body, name=name, grid=(R // tr,),
        in_specs=[pl.BlockSpec((P, tr, C), lambda i: (0, i, 0)), blk, blk, blk],
        out_specs=[blk, blk, blk, blk], out_shape=[shp, shp, shp, shp],
        compiler_params=_cp(("parallel",)),
    )(parts, w, m, v)


def _sum_adamw_layer(parts, w, m, v, layer, name, prev=None, after=None):
    P, R, C = parts.shape
    tr = 256 if (R % 256 == 0 and R > 256) else R

    def body(p_ref, w_ref, m_ref, v_ref, *rest):
        g_ref, d_ref, nm_ref, nv_ref = rest[-4:]
        g = p_ref[0].astype(F32)
        for k in range(1, P):
            g = g + p_ref[k].astype(F32)
        g_ref[0] = g
        nm = ADAM_B1 * m_ref[0] + (1.0 - ADAM_B1) * g
        nv = ADAM_B2 * v_ref[0] + (1.0 - ADAM_B2) * (g * g)
        nm_ref[0] = nm
        nv_ref[0] = nv
        m_hat = nm / (1.0 - ADAM_B1 ** ADAM_STEP)
        v_hat = nv / (1.0 - ADAM_B2 ** ADAM_STEP)
        d_ref[0] = -ADAM_LR * (m_hat / (jnp.sqrt(v_hat) + ADAM_EPS) + ADAM_WD * w_ref[0])

    blk = pl.BlockSpec((1, tr, C), lambda i: (layer, i, 0))
    shp = jax.ShapeDtypeStruct(w.shape, F32)
    in_specs = [pl.BlockSpec((P, tr, C), lambda i: (0, i, 0)), blk, blk, blk]
    args = [parts, w, m, v]
    aliases = {}
    if prev is not None:
        in_specs += [pl.BlockSpec(memory_space=pl.ANY)] * 4
        args += list(prev)
        aliases = {4 + k: k for k in range(4)}
    if after is not None:
        in_specs.append(pl.BlockSpec(memory_space=pl.ANY))
        args.append(after)
    return list(pl.pallas_call(
        body, name=name, grid=(R // tr,), in_specs=in_specs, out_specs=[blk] * 4, out_shape=[shp] * 4,
        input_output_aliases=aliases, compiler_params=_cp(("parallel",)),
    )(*args))


SMALL = ["norm_w", "mla_q_norm", "mla_kv_norm", "gla_w_g2", "gla_b_g2", "gla_norm", "final_norm"]


SMALL_ROWS = 72


def _pack_small(loss, part):
    flat = [jnp.pad(loss.reshape(1), (0, 127))] + [part[n].reshape(-1) for n in SMALL]
    used = sum(f.shape[0] for f in flat)
    flat.append(jnp.zeros((SMALL_ROWS * 128 - used,), F32))
    return jnp.concatenate(flat).reshape(SMALL_ROWS, 128)


def _small_adamw(packed_parts, w, m, v):
    n = len(w)

    def body(*refs):
        p_ref = refs[0]
        w_refs, m_refs, v_refs = refs[1:1 + n], refs[1 + n:1 + 2 * n], refs[1 + 2 * n:1 + 3 * n]
        outs, acc = refs[1 + 3 * n:-1], refs[-1]
        total = p_ref[0]
        for k in range(1, N_DEV):
            total = total + p_ref[k]
        acc[...] = total
        outs[0][...] = acc[0:1, :]
        r0 = 1
        for i in range(n):
            shp = w_refs[i].shape
            if len(shp) == 3:
                g = acc[r0:r0 + shp[0] * shp[1], :].reshape(shp)
                r0 += shp[0] * shp[1]
            elif shp[1] < 128:
                g = acc[r0:r0 + shp[0], 0:shp[1]]
                r0 += shp[0]
            else:
                k = shp[1] // 128
                g = jnp.concatenate(
                    [jnp.concatenate([acc[r0 + l * k + j:r0 + l * k + j + 1, :] for j in range(k)], axis=1)
                     for l in range(shp[0])], axis=0)
                r0 += shp[0] * k
            nm = ADAM_B1 * m_refs[i][...] + (1.0 - ADAM_B1) * g
            nv = ADAM_B2 * v_refs[i][...] + (1.0 - ADAM_B2) * (g * g)
            m_hat = nm / (1.0 - ADAM_B1 ** ADAM_STEP)
            v_hat = nv / (1.0 - ADAM_B2 ** ADAM_STEP)
            outs[1 + 4 * i][...] = g
            outs[2 + 4 * i][...] = -ADAM_LR * (m_hat / (jnp.sqrt(v_hat) + ADAM_EPS) + ADAM_WD * w_refs[i][...])
            outs[3 + 4 * i][...] = nm
            outs[4 + 4 * i][...] = nv

    vmem = pl.BlockSpec(memory_space=pltpu.VMEM)
    out_shape = [jax.ShapeDtypeStruct((1, 128), F32)]
    for a in w:
        out_shape += [jax.ShapeDtypeStruct(a.shape, F32)] * 4
    outs = pl.pallas_call(
        body, name="adamw_small", in_specs=[vmem] * (1 + 3 * n), out_specs=[vmem] * (1 + 4 * n), out_shape=out_shape,
        scratch_shapes=[pltpu.VMEM((SMALL_ROWS, 128), F32)],
    )(packed_parts, *w, *m, *v)
    return outs[0], [outs[1 + 4 * i:5 + 4 * i] for i in range(n)]


WEIGHTS = ["norm_w", "ada_w", "ada_b", "w_in", "mla_q_norm", "w_uq", "mla_kv_norm", "w_ukv", "gla_w_g2",
           "gla_b_g2", "gla_norm", "w_out", "final_norm"]


def kernel(x, c, positions, norm_w, ada_w, ada_b, w_in, mla_q_norm, w_uq, mla_kv_norm, w_ukv, gla_w_g2, gla_b_g2, gla_norm, w_out, final_norm, loss_target, m_norm_w, m_ada_w, m_ada_b, m_w_in, m_mla_q_norm, m_w_uq, m_mla_kv_norm, m_w_ukv, m_gla_w_g2, m_gla_b_g2, m_gla_norm, m_w_out, m_final_norm, v_norm_w, v_ada_w, v_ada_b, v_w_in, v_mla_q_norm, v_w_uq, v_mla_kv_norm, v_w_ukv, v_gla_w_g2, v_gla_b_g2, v_gla_norm, v_w_out, v_final_norm):
    w = dict(norm_w=norm_w, ada_w=ada_w, ada_b=ada_b, w_in=w_in, mla_q_norm=mla_q_norm, w_uq=w_uq,
             mla_kv_norm=mla_kv_norm, w_ukv=w_ukv, gla_w_g2=gla_w_g2, gla_b_g2=gla_b_g2, gla_norm=gla_norm,
             w_out=w_out, final_norm=final_norm)
    m = dict(norm_w=m_norm_w, ada_w=m_ada_w, ada_b=m_ada_b, w_in=m_w_in, mla_q_norm=m_mla_q_norm, w_uq=m_w_uq,
             mla_kv_norm=m_mla_kv_norm, w_ukv=m_w_ukv, gla_w_g2=m_gla_w_g2, gla_b_g2=m_gla_b_g2,
             gla_norm=m_gla_norm, w_out=m_w_out, final_norm=m_final_norm)
    v = dict(norm_w=v_norm_w, ada_w=v_ada_w, ada_b=v_ada_b, w_in=v_w_in, mla_q_norm=v_mla_q_norm, w_uq=v_w_uq,
             mla_kv_norm=v_mla_kv_norm, w_ukv=v_w_ukv, gla_w_g2=v_gla_w_g2, gla_b_g2=v_gla_b_g2,
             gla_norm=v_gla_norm, w_out=v_w_out, final_norm=v_final_norm)
    B, S, D = x.shape
    me = 4 * lax.axis_index("x") + 2 * lax.axis_index("y") + lax.axis_index("c")
    ada_cols = ada_w.shape[2]
    cast = lambda a: a.astype(_MXU)

    sharded = ["w_in", "w_uq", "w_ukv", "w_out"]

    whole_cols = lambda a: jnp.transpose(a, (1, 0, 2)).reshape(a.shape[1], -1)
    whole_in = lambda blk: _arrange_w_in(whole_cols(blk))
    whole_rest = lambda blks: (_arrange_w_uq(whole_cols(blks[0])), _arrange_w_ukv(whole_cols(blks[1])),
                               blks[2].reshape(D, D))
    col_blocks = lambda a: jnp.transpose(a.reshape(a.shape[0], N_DEV, -1), (1, 0, 2)).astype(jnp.bfloat16)
    blocks_in = lambda dw_in_a: col_blocks(_unarrange_w_in(dw_in_a))
    blocks_rest = lambda dw_out, dw_uq_a, dw_ukv_a: [
        col_blocks(_unarrange_w_uq(dw_uq_a)), col_blocks(_unarrange_w_ukv(dw_ukv_a)),
        dw_out.reshape(N_DEV, D // N_DEV, D).astype(jnp.bfloat16)]

    (c_g,) = _exchange([c], [True], "gather_c")
    c_all = c_g.reshape(N_DEV * B, D)

    ada_b_cols = lax.dynamic_slice(ada_b, (0, me * ada_cols), (DEPTH, ada_cols))
    mod_cols = _ada_fwd(c_all, ada_w, ada_b_cols)
    mod_send = jnp.transpose(mod_cols.reshape(DEPTH, N_DEV, B, ada_cols), (1, 0, 2, 3))
    (mod_recv,) = _exchange([mod_send], [False], "scatter_mod")
    mod = jnp.transpose(mod_recv, (1, 2, 0, 3)).reshape(DEPTH, B, 3 * D)

    flight_i = _exchange_start([cast(w_in[0])], [True], "gather_start_first", after=mod)
    flight_r = _exchange_start([cast(w[n][0]) for n in sharded[1:]], [True] * 3, "gather_start_layer0",
                               after=flight_i["token"])
    flight_w = _exchange_start([cast(w[n][1]) for n in sharded], [True] * 4, "gather_start_layer1",
                               after=flight_r["token"])
    small_w = {n: w[n] for n in SMALL}
    layer_small = lambda l: {n: a[l] for n, a in small_w.items() if n != "final_norm"}
    tabs = _rope_tables(positions.reshape(B, S, 1), flight_w["token"][0, 0])
    late0 = lambda after: whole_rest(_exchange_wait(flight_r, after, me, "gather_wait_layer0"))
    (w_in0_g,) = _exchange_wait(flight_i, tabs[0], me, "gather_wait_first")
    x1, saved0 = _layer_fwd(x, tabs, mod[0], layer_small(0), whole_in(w_in0_g), late_weights=late0)
    got1 = _exchange_wait(flight_w, x1, me, "gather_wait_layer1")
    (dx, loss, d_fw), saved1 = _layer_fwd(x1, tabs, mod[1], layer_small(1), whole_in(got1[0]), *whole_rest(got1[1:]),
                                          loss_head=(final_norm.reshape(1, D), loss_target))

    dx, g1 = _layer_bwd(dx, saved1, tabs)
    flight_g = _exchange_start([blocks_in(g1["w_in_a"])] + blocks_rest(g1["w_out"], g1["w_uq_a"], g1["w_ukv_a"]),
                               [False] * 4, "grads_start_layer1")
    flights = {}

    def early0(dw_out, dw_uq_a, dw_ukv_a):
        flights["rest0"] = _exchange_start(blocks_rest(dw_out, dw_uq_a, dw_ukv_a), [False] * 3, "grads_start_layer0")
        return flights["rest0"]["token"][0, 0]

    saved0 = dict(saved0, gate=saved0["gate"] + flight_g["token"][0, 0])
    grad_x, g0 = _layer_bwd(dx, saved0, tabs, early_grads=early0)
    parts1 = _exchange_wait(flight_g, grad_x, me, "grads_wait_layer1")
    rest0 = _exchange_wait(flights["rest0"], g0["w_in_a"], me, "grads_wait_layer0")

    both = lambda n: jnp.stack([g0[n], g1[n]])
    d_mod = both("d_mod")
    part = dict(norm_w=both("norm_w"), mla_q_norm=both("mla_q_norm"), mla_kv_norm=both("mla_kv_norm"),
                gla_w_g2=both("gla_w_g2"), gla_b_g2=both("gla_b_g2"), gla_norm=both("gla_norm256")[:, 0:128],
                final_norm=d_fw)
    flight_l = _exchange_start([d_mod, _pack_small(loss, part), blocks_in(g0["w_in_a"])], [True, True, False],
                               "exchange_start_last")
    res = {}
    behind = flight_l["token"]
    for a, name in enumerate(sharded):
        res[name] = _sum_adamw_layer(parts1[a], w[name], m[name], v[name], 1, "adamw_%s_layer1" % name, after=behind)
        behind = res[name][1]
    for a, name in enumerate(sharded[1:]):
        res[name] = _sum_adamw_layer(rest0[a], w[name], m[name], v[name], 0, "adamw_%s_layer0" % name,
                                     prev=res[name], after=behind)
        behind = res[name][1]
    d_mod_g, small_g, in0 = _exchange_wait(flight_l, behind, me, "exchange_wait_last")
    res["w_in"] = _sum_adamw_layer(in0, w_in, m_w_in, v_w_in, 0, "adamw_w_in_layer0", prev=res["w_in"])

    d_mod_all = jnp.transpose(d_mod_g, (1, 0, 2, 3)).reshape(DEPTH, N_DEV * B, 3 * D)
    d_mod_cols = lax.dynamic_slice(d_mod_all, (0, 0, me * ada_cols), (DEPTH, N_DEV * B, ada_cols))
    g_ada_w = _ada_bwd(c_all, d_mod_cols)

    def update(name, parts2d):
        shp = w[name].shape
        two = lambda a: a.reshape(parts2d.shape[1:])
        out = _sum_adamw(parts2d, two(w[name]), two(m[name]), two(v[name]), "adamw_" + name)
        res[name] = [o.reshape(shp) for o in out]

    update("ada_w", g_ada_w.reshape(1, DEPTH * D, ada_cols))
    update("ada_b", jnp.transpose(d_mod_g, (0, 2, 1, 3)).reshape(N_DEV * B, DEPTH * 3 * D // 128, 128))
    row = lambda a: a.reshape(1, D) if a.ndim == 1 else a
    loss_sum, small_out = _small_adamw(small_g, [row(w[n]) for n in SMALL], [row(m[n]) for n in SMALL],
                                       [row(v[n]) for n in SMALL])
    for n, outs in zip(SMALL, small_out):
        res[n] = [o.reshape(w[n].shape) for o in outs]
    loss_out = loss_sum[0, 0]
    return (loss_out, grad_x, *[res[n][0] for n in WEIGHTS], *[res[n][1] for n in WEIGHTS],
            *[res[n][2] for n in WEIGHTS], *[res[n][3] for n in WEIGHTS])
```

```python
import functools
import math

import numpy as np
import jax
import jax.numpy as jnp
from jax import lax
from jax.experimental import pallas as pl
from jax.experimental.pallas import tpu as pltpu

F32 = jnp.float32
_MXU = jnp.bfloat16

D_MODEL = 1024
DEPTH = 2
CHUNK = 64
EPS = 1e-6
ROPE_THETA = 10000.0
N_DEV = 8

MLA_SCALE = 96.0 ** -0.5
RET_KSCALE = 64.0 ** -0.5
GLA_KSCALE = 32.0 ** -0.5
GLA_TAU = 16.0

ADAM_LR = 0.001
ADAM_B1 = 0.9
ADAM_B2 = 0.999
ADAM_EPS = 1e-08
ADAM_WD = 0.01
ADAM_STEP = 10

RET_W, MLA_W, GLA_W = 1024, 1024, 896
ARR_W = RET_W + MLA_W + GLA_W
VMEM_MB = 1024 * 1024


def _cp(sem, vmem_mb=48):
    return pltpu.CompilerParams(dimension_semantics=sem, vmem_limit_bytes=vmem_mb * VMEM_MB)


def _mm(a, b):
    return jnp.dot(a.astype(_MXU), b.astype(_MXU), preferred_element_type=F32)


def _mm_nt(a, b):
    return lax.dot_general(a.astype(_MXU), b.astype(_MXU), (((1,), (1,)), ((), ())),
                           preferred_element_type=F32)


def _mm_tn(a, b):
    return lax.dot_general(a.astype(_MXU), b.astype(_MXU), (((0,), (0,)), ((), ())),
                           preferred_element_type=F32)


def _mm_f32(a, b):
    return jnp.dot(a, b, precision=lax.Precision.HIGHEST, preferred_element_type=F32)


def _sig(z):
    return 1.0 / (1.0 + jnp.exp(-z))


def _silu(z):
    return z * _sig(z)


def _dsilu(z):
    s = _sig(z)
    return s * (1.0 + z * (1.0 - s))


def _full(shape):
    nd = len(shape)
    return pl.BlockSpec(shape, lambda *_: (0,) * nd)


def _qk_perm(blk):
    r = blk.shape[0]
    return jnp.transpose(blk.reshape(r, 4, 2, 32), (0, 2, 1, 3)).reshape(r, 256)


def _qk_unperm(blk):
    r = blk.shape[0]
    return jnp.transpose(blk.reshape(r, 2, 4, 32), (0, 2, 1, 3)).reshape(r, 256)


def _arrange_w_in(w):
    z = lambda n: jnp.zeros((w.shape[0], n), w.dtype)
    ret = [_qk_perm(w[:, 0:256]), _qk_perm(w[:, 256:512]), w[:, 512:768], w[:, 768:1024]]
    mla = [w[:, 1024:1280], w[:, 1280:1408], w[:, 1408:1424], z(48), w[:, 1424:1440], z(48), w[:, 1440:1952]]
    gla = [w[:, 1952:2080], w[:, 2080:2208], w[:, 2208:2464], w[:, 2464:2480], z(112), w[:, 2480:2736]]
    return jnp.concatenate(ret + mla + gla, axis=1)


def _unarrange_w_in(a):
    m, g = RET_W, RET_W + MLA_W
    parts = [_qk_unperm(a[:, 0:256]), _qk_unperm(a[:, 256:512]), a[:, 512:1024],
             a[:, m:m + 400], a[:, m + 448:m + 464], a[:, m + 512:m + 1024],
             a[:, g:g + 528], a[:, g + 640:g + 896]]
    return jnp.concatenate(parts, axis=1)


def _arrange_w_uq(w):
    r = w.reshape(256, 8, 96)
    z = jnp.zeros((256, 8, 32), w.dtype)
    return jnp.concatenate([r[:, :, 64:80], r[:, :, 0:48], r[:, :, 80:96], r[:, :, 48:64], z], axis=2).reshape(256, 1024)


def _unarrange_w_uq(a):
    r = a.reshape(256, 8, 128)
    return jnp.concatenate([r[:, :, 16:64], r[:, :, 80:96], r[:, :, 0:16], r[:, :, 64:80]], axis=2).reshape(256, 768)


def _arrange_w_ukv(w):
    r = w.reshape(128, 8, 128)
    z = lambda n: jnp.zeros((128, 8, n), w.dtype)
    k = jnp.concatenate([z(16), r[:, :, 0:48], z(16), r[:, :, 48:64], z(32)], axis=2).reshape(128, 1024)
    return jnp.concatenate([k, r[:, :, 64:].reshape(128, 512)], axis=1)


def _unarrange_w_ukv(a):
    r = a[:, :1024].reshape(128, 8, 128)
    v = a[:, 1024:].reshape(128, 8, 64)
    return jnp.concatenate([r[:, :, 16:64], r[:, :, 80:96], v], axis=2).reshape(128, 1024)


def _rope_tables(pos3, zero=0.0):
    B, S, _ = pos3.shape
    ts = min(S, 512)
    inv32 = (np.float32(ROPE_THETA) ** (-(np.arange(32, dtype=np.float32) / 32))).astype(np.float32)
    inv16 = (np.float32(ROPE_THETA) ** (-(np.arange(16, dtype=np.float32) / 16))).astype(np.float32)
    inv = np.zeros((1, 128), np.float32)
    inv[0, 0:32] = inv32
    inv[0, 32:48] = inv16

    def body(pos_ref, inv_ref, cr, sr, cm, sm):
        ang = pos_ref[0].astype(F32) * inv_ref[...]
        lane = lax.broadcasted_iota(jnp.int32, (1, 128), 1)

        def every_head(x):
            y = jnp.where(lane < 32, x, pltpu.roll(x, 32, 1))
            return jnp.where(lane < 64, y, pltpu.roll(y, 64, 1))

        def rotary_pair(x, fill):
            return jnp.where(lane < 16, pltpu.roll(x, 96, 1),
                             jnp.where((lane >= 64) & (lane < 80), pltpu.roll(x, 32, 1), fill))

        c, s = jnp.cos(ang), jnp.sin(ang)
        cr[0] = every_head(c)
        sr[0] = every_head(s)
        cm[0] = rotary_pair(c, 1.0)
        sm[0] = rotary_pair(s, 0.0)

    tab = jax.ShapeDtypeStruct((B, S, 128), F32)
    blk = pl.BlockSpec((1, ts, 128), lambda b, i: (b, i, 0))
    return pl.pallas_call(
        body, name="rope_tables", grid=(B, S // ts),
        in_specs=[pl.BlockSpec((1, ts, 1), lambda b, i: (b, i, 0)), _full((1, 128))],
        out_specs=[blk, blk, blk, blk], out_shape=[tab, tab, tab, tab],
        compiler_params=_cp(("parallel", "parallel")),
    )(pos3, jnp.asarray(inv) + zero)


def _rope128(x, cos, sin):
    lane = lax.broadcasted_iota(jnp.int32, (1, 128), 1)
    r = pltpu.roll(x, 64, 1)
    return x * cos + jnp.where(lane < 64, -r, r) * sin


def _rope128_t(d, cos, sin):
    lane = lax.broadcasted_iota(jnp.int32, (1, 128), 1)
    r = pltpu.roll(d * sin, 64, 1)
    return d * cos + jnp.where(lane < 64, r, -r)


def _proj_fwd(x, shift, scale, nw, w_arr):
    B, S, D = x.shape
    tm = min(S, 512)

    def body(x_ref, sh_ref, sc_ref, nw_ref, w_ref, ret_ref, mla_ref, gla_ref, h_ref):
        xv = x_ref[0]
        rstd = lax.rsqrt(jnp.mean(xv * xv, axis=-1, keepdims=True) + EPS)
        h = (xv * rstd * nw_ref[...]) * (1.0 + sc_ref[0]) + sh_ref[0]
        hb = h.astype(_MXU)
        h_ref[0] = hb
        ret_ref[0] = jnp.dot(hb, w_ref[:, 0:RET_W], preferred_element_type=F32).astype(_MXU)
        mla_ref[0] = jnp.dot(hb, w_ref[:, RET_W:RET_W + MLA_W], preferred_element_type=F32).astype(_MXU)
        gla_ref[0] = jnp.dot(hb, w_ref[:, RET_W + MLA_W:ARR_W], preferred_element_type=F32).astype(_MXU)

    tok = lambda w: pl.BlockSpec((1, tm, w), lambda b, i: (b, i, 0))
    per_seq = pl.BlockSpec((1, 1, D), lambda b, i: (b, 0, 0))
    return pl.pallas_call(
        body, name="proj_fwd", grid=(B, S // tm),
        in_specs=[tok(D), per_seq, per_seq, _full((1, D)), _full((D, ARR_W))],
        out_specs=[tok(RET_W), tok(MLA_W), tok(GLA_W), tok(D)],
        out_shape=[jax.ShapeDtypeStruct((B, S, RET_W), _MXU), jax.ShapeDtypeStruct((B, S, MLA_W), _MXU),
                   jax.ShapeDtypeStruct((B, S, GLA_W), _MXU), jax.ShapeDtypeStruct((B, S, D), _MXU)],
        compiler_params=_cp(("parallel", "parallel")),
    )(x, shift, scale, nw, w_arr)


RET_L = 256


def _ret_consts(L):
    lg = np.log1p(-np.exp2(-5.0 - np.arange(4, dtype=np.float32))).astype(np.float32)
    i = np.arange(L)
    ci = i // CHUNK
    diff = (i[:, None] - i[None, :]).astype(np.float32)
    same = ci[:, None] == ci[None, :]
    past = ci[None, :] < ci[:, None]
    expo = np.where(same, np.abs(diff), np.where(past, diff, 0.0)).astype(np.float32)
    dec = np.where((same | past)[None], np.exp(lg[:, None, None] * expo[None]), 0.0).astype(np.float32)
    head = (np.arange(256) % 128) // 32
    qw = np.exp((i + 1.0)[:, None] * lg[head][None, :]).astype(np.float32)
    kw = np.exp((L - 1.0 - i)[:, None] * lg[head][None, :]).astype(np.float32)
    a_row = np.exp(np.float32(L) * lg[head])[None, :].astype(np.float32)
    return [jnp.asarray(t) for t in (dec.reshape(4 * L, L), qw, kw, a_row)]


def _ret_masks():
    lane = lax.broadcasted_iota(jnp.int32, (1, 256), 1)
    mh = [((lane % 128) // 32) == h for h in range(4)]
    mv = [(lane // 64) == h for h in range(4)]
    vi = lax.broadcasted_iota(jnp.int32, (256, 256), 0)
    ki = lax.broadcasted_iota(jnp.int32, (256, 256), 1)
    bd = (vi // 64) == ((ki % 128) // 32)
    return mh, mv, bd


def _ret_rope(p, cs, sn):
    q1, q2, k1, k2 = p[:, 0:128], p[:, 128:256], p[:, 256:384], p[:, 384:512]
    qr = jnp.concatenate([q1 * cs - q2 * sn, q2 * cs + q1 * sn], axis=1)
    kr = jnp.concatenate([k1 * cs - k2 * sn, k2 * cs + k1 * sn], axis=1) * RET_KSCALE
    return qr, kr


def _head_mean(x, mv, width):
    out = jnp.zeros_like(x)
    for m in mv:
        s = jnp.sum(jnp.where(m, x, 0.0), axis=-1, keepdims=True) * (1.0 / width)
        out = jnp.where(m, s, out)
    return out


def _stack_heads(x, masks):
    return jnp.concatenate([jnp.where(m, x, 0.0) for m in masks], axis=0)


def _fold_heads(xs, masks, L):
    out = jnp.where(masks[0], xs[0:L], 0.0)
    for h in range(1, 4):
        out = out + jnp.where(masks[h], xs[h * L:(h + 1) * L], 0.0)
    return out


RET_G = 2


def _ret_fwd(ret_p, cos, sin):
    B, S, _ = ret_p.shape
    L = min(RET_L, S)
    NB = S // L
    G = min(RET_G, NB)
    NG = NB // G
    consts = _ret_consts(L)

    def body(p_ref, c_ref, s_ref, ds_ref, qw_ref, kw_ref, a_ref, out_ref, raw_ref, st_ref, st_sc):
        @pl.when(pl.program_id(1) == 0)
        def _():
            st_sc[...] = jnp.zeros_like(st_sc)

        mh, mv, bd = _ret_masks()
        cs_ = range(G)
        rows = [slice(c * L, (c + 1) * L) for c in cs_]
        ps = [p_ref[0, rows[c], :].astype(F32) for c in cs_]
        qk = [_ret_rope(ps[c], c_ref[0, rows[c], :], s_ref[0, rows[c], :]) for c in cs_]
        vs = [ps[c][:, 512:768] for c in cs_]
        a_s = [_mm_nt(_stack_heads(qk[c][0], mh), qk[c][1]) for c in cs_]
        upd = [_mm_tn(vs[c], qk[c][1] * kw_ref[...]) for c in cs_]
        o_s = [_mm(a_s[c] * ds_ref[...], vs[c]) for c in cs_]
        st = st_sc[...]
        inter = []
        for c in cs_:
            st_ref[0, c] = st
            inter.append(_mm_nt(qk[c][0] * qw_ref[...], st))
            st = st * a_ref[...] + jnp.where(bd, upd[c], 0.0)
        st_sc[...] = st
        for c in cs_:
            r = _fold_heads(o_s[c], mv, L) + inter[c]
            raw_ref[0, rows[c], :] = r
            rstd = lax.rsqrt(_head_mean(r * r, mv, 64.0) + EPS)
            out_ref[0, rows[c], :] = (r * rstd * _silu(ps[c][:, 768:1024])).astype(_MXU)

    tok = lambda w: pl.BlockSpec((1, G * L, w), lambda b, n: (b, n, 0))
    return pl.pallas_call(
        body, name="ret_fwd", grid=(B, NG),
        in_specs=[tok(RET_W), tok(128), tok(128), _full((4 * L, L)), _full((L, 256)), _full((L, 256)),
                  _full((1, 256))],
        out_specs=[tok(256), tok(256), pl.BlockSpec((1, G, 256, 256), lambda b, n: (b, n, 0, 0))],
        out_shape=[jax.ShapeDtypeStruct((B, S, 256), _MXU), jax.ShapeDtypeStruct((B, S, 256), F32),
                   jax.ShapeDtypeStruct((B, NB, 256, 256), F32)],
        scratch_shapes=[pltpu.VMEM((256, 256), F32)],
        compiler_params=_cp(("parallel", "arbitrary")),
    )(ret_p, cos, sin, *consts)


def _ret_bwd(ret_p, cos, sin, raw, states, d_mix):
    B, S, _ = ret_p.shape
    L = min(RET_L, S)
    NB = S // L
    G = 1
    NG = NB // G
    consts = _ret_consts(L)

    def body(p_ref, c_ref, s_ref, raw_ref, st_ref, dm_ref, ds_ref, qw_ref, kw_ref, a_ref, dp_ref, dst_sc):
        @pl.when(pl.program_id(1) == 0)
        def _():
            dst_sc[...] = jnp.zeros_like(dst_sc)

        mh, mv, bd = _ret_masks()
        qw, kw, dec = qw_ref[...], kw_ref[...], ds_ref[...]
        cs_ = range(G)
        rows = [slice(c * L, (c + 1) * L) for c in cs_]
        ps = [p_ref[0, rows[c], :].astype(F32) for c in cs_]
        tabs = [(c_ref[0, rows[c], :], s_ref[0, rows[c], :]) for c in cs_]
        qk = [_ret_rope(ps[c], *tabs[c]) for c in cs_]
        vs = [ps[c][:, 512:768] for c in cs_]
        qs = [_stack_heads(qk[c][0], mh) for c in cs_]
        a_s = [_mm_nt(qs[c], qk[c][1]) for c in cs_]
        dr, dz = [], []
        for c in cs_:
            r = raw_ref[0, rows[c], :]
            z = ps[c][:, 768:1024]
            rstd = lax.rsqrt(_head_mean(r * r, mv, 64.0) + EPS)
            rn = r * rstd
            dm = dm_ref[0, rows[c], :]
            d_rn = dm * _silu(z)
            dz.append(dm * rn * _dsilu(z))
            dr.append(rstd * (d_rn - rn * _head_mean(d_rn * rn, mv, 64.0)))
        do_s = [_stack_heads(dr[c], mv) for c in cs_]
        da_s = [_mm_nt(do_s[c], vs[c]) for c in cs_]
        sts = [st_ref[0, c] for c in cs_]
        dq_st = [_mm(dr[c], sts[c]) for c in cs_]
        dst_in = [_mm_tn(dr[c], qk[c][0] * qw) for c in cs_]
        dv = [_mm_tn(a_s[c] * dec, do_s[c]) for c in cs_]
        dqr, dkr = [], []
        for c in cs_:
            da = da_s[c] * dec
            dqr.append(_fold_heads(_mm(da, qk[c][1]), mh, L) + dq_st[c] * qw)
            dkr.append(_mm_tn(da, qs[c]))
        dst_next = dst_sc[...]
        for c in reversed(cs_):
            g = jnp.where(bd, dst_next, 0.0)
            dv[c] = dv[c] + _mm_nt(qk[c][1] * kw, g)
            dkr[c] = dkr[c] + _mm(vs[c], g) * kw
            dst_next = dst_next * a_ref[...] + jnp.where(bd, dst_in[c], 0.0)
        dst_sc[...] = dst_next
        for c in cs_:
            cs, sn = tabs[c]
            dk = dkr[c] * RET_KSCALE
            dq1, dq2 = dqr[c][:, 0:128], dqr[c][:, 128:256]
            dk1, dk2 = dk[:, 0:128], dk[:, 128:256]
            dp_ref[0, rows[c], :] = jnp.concatenate(
                [dq1 * cs + dq2 * sn, dq2 * cs - dq1 * sn, dk1 * cs + dk2 * sn, dk2 * cs - dk1 * sn, dv[c], dz[c]],
                axis=1).astype(_MXU)

    tok = lambda w: pl.BlockSpec((1, G * L, w), lambda b, i: (b, NG - 1 - i, 0))
    return pl.pallas_call(
        body, name="ret_bwd", grid=(B, NG),
        in_specs=[tok(RET_W), tok(128), tok(128), tok(256),
                  pl.BlockSpec((1, G, 256, 256), lambda b, i: (b, NG - 1 - i, 0, 0)), tok(256),
                  _full((4 * L, L)), _full((L, 256)), _full((L, 256)), _full((1, 256))],
        out_specs=tok(RET_W), out_shape=jax.ShapeDtypeStruct((B, S, RET_W), _MXU),
        scratch_shapes=[pltpu.VMEM((256, 256), F32)],
        compiler_params=_cp(("parallel", "arbitrary")),
    )(ret_p, cos, sin, raw, states, d_mix, *consts)


def _gla_masks():
    C = CHUNK
    lk = lax.broadcasted_iota(jnp.int32, (1, 128), 1)
    lv = lax.broadcasted_iota(jnp.int32, (1, 256), 1)
    mk = [(lk // 32) == h for h in range(4)]
    mv = [(lv // 64) == h for h in range(4)]
    vi = lax.broadcasted_iota(jnp.int32, (256, 128), 0)
    ki = lax.broadcasted_iota(jnp.int32, (256, 128), 1)
    bd = (vi // 64) == (ki // 32)
    ri = lax.broadcasted_iota(jnp.int32, (4 * C, C), 0) % C
    cj = lax.broadcasted_iota(jnp.int32, (4 * C, C), 1)
    lower = ri >= cj
    ti = lax.broadcasted_iota(jnp.int32, (C, C), 0)
    tj = lax.broadcasted_iota(jnp.int32, (C, C), 1)
    ltri = jnp.where(ti >= tj, 1.0, 0.0).astype(F32)
    utri = jnp.where(ti <= tj, 1.0, 0.0).astype(F32)
    return mk, mv, bd, lower, ltri, utri


def _log_sigmoid(x):
    return jnp.minimum(x, 0.0) - jnp.log(1.0 + jnp.exp(-jnp.abs(x)))


GLA_G = 8


def _gla_fwd(gla_p, w_g2p, b_g2, gnw):
    B, S, _ = gla_p.shape
    C = CHUNK
    NC = S // C
    G = min(GLA_G, NC)
    NG = NC // G

    def body(p_ref, w_ref, b_ref, gn_ref, out_ref, raw_ref, st_ref, st_sc):
        @pl.when(pl.program_id(1) == 0)
        def _():
            st_sc[...] = jnp.zeros_like(st_sc)

        mk, mv, bd, lower, ltri, _ = _gla_masks()
        cs = range(G)
        rows = [slice(c * C, (c + 1) * C) for c in cs]
        ps = [p_ref[0, rows[c], :].astype(F32) for c in cs]
        pre = [_mm(ps[c][:, 512:640], w_ref[...]) + b_ref[...] for c in cs]
        cum = [_mm_f32(ltri, _log_sigmoid(pre[c]) * (1.0 / GLA_TAU)) for c in cs]
        past, fut, upd, q_pos, a_row = [], [], [], [], []
        for c in cs:
            q = ps[c][:, 0:128]
            k = ps[c][:, 128:256] * GLA_KSCALE
            last = cum[c][C - 1:C, :]
            e_pos = jnp.exp(cum[c])
            e_neg = jnp.exp(-cum[c])
            q_pos.append(q * e_pos)
            a_row.append(jnp.exp(last))
            past.append(_mm_nt(_stack_heads(q_pos[c], mk), k * e_neg))
            fut.append(_mm_nt(_stack_heads(q * e_neg, mk), k * e_pos))
            upd.append(_mm_tn(ps[c][:, 256:512], k * jnp.exp(last - cum[c])))
        o_s = [_mm(jnp.where(lower, past[c], fut[c]), ps[c][:, 256:512]) for c in cs]
        st = st_sc[...]
        inter = []
        for c in cs:
            st_ref[0, c] = st
            inter.append(_mm_nt(q_pos[c], st))
            st = st * a_row[c] + jnp.where(bd, upd[c], 0.0)
        st_sc[...] = st
        for c in cs:
            g = _fold_heads(o_s[c], mv, C) + inter[c]
            raw_ref[0, rows[c], :] = g
            rstd = lax.rsqrt(_head_mean(g * g, mv, 64.0) + EPS)
            out_ref[0, rows[c], :] = (g * rstd * gn_ref[...] * _silu(ps[c][:, 640:896])).astype(_MXU)

    tok = lambda w: pl.BlockSpec((1, G * C, w), lambda b, n: (b, n, 0))
    return pl.pallas_call(
        body, name="gla_fwd", grid=(B, NG),
        in_specs=[tok(GLA_W), _full((128, 128)), _full((1, 128)), _full((1, 256))],
        out_specs=[tok(256), tok(256), pl.BlockSpec((1, G, 256, 128), lambda b, n: (b, n, 0, 0))],
        out_shape=[jax.ShapeDtypeStruct((B, S, 256), _MXU), jax.ShapeDtypeStruct((B, S, 256), F32),
                   jax.ShapeDtypeStruct((B, NC, 256, 128), F32)],
        scratch_shapes=[pltpu.VMEM((256, 128), F32)],
        compiler_params=_cp(("parallel", "arbitrary")),
    )(gla_p, w_g2p, b_g2, gnw)


def _gla_bwd(gla_p, w_g2p, b_g2, gnw, raw, states, d_mix):
    B, S, _ = gla_p.shape
    C = CHUNK
    NC = S // C
    G = min(GLA_G, NC)
    NG = NC // G

    def body(p_ref, w_ref, b_ref, gn_ref, raw_ref, st_ref, dm_ref, dp_ref, dw_ref, db_ref, dgn_ref, dst_sc):
        first = (pl.program_id(0) == 0) & (pl.program_id(1) == 0)

        @pl.when(first)
        def _():
            dw_ref[...] = jnp.zeros_like(dw_ref)
            db_ref[...] = jnp.zeros_like(db_ref)
            dgn_ref[...] = jnp.zeros_like(dgn_ref)

        @pl.when(pl.program_id(1) == 0)
        def _():
            dst_sc[...] = jnp.zeros_like(dst_sc)

        mk, mv, bd, lower, ltri, utri = _gla_masks()
        gn = gn_ref[...]
        cs = range(G)
        rows = [slice(c * C, (c + 1) * C) for c in cs]
        ps = [p_ref[0, rows[c], :].astype(F32) for c in cs]
        vs = [ps[c][:, 256:512] for c in cs]
        pre = [_mm(ps[c][:, 512:640], w_ref[...]) + b_ref[...] for c in cs]
        cum = [_mm_f32(ltri, _log_sigmoid(pre[c]) * (1.0 / GLA_TAU)) for c in cs]
        dg, dz, dgn_acc = [], [], jnp.zeros((1, 256), F32)
        for c in cs:
            g = raw_ref[0, rows[c], :]
            z = ps[c][:, 640:896]
            rstd = lax.rsqrt(_head_mean(g * g, mv, 64.0) + EPS)
            gh = g * rstd
            dm = dm_ref[0, rows[c], :]
            d_gn = dm * _silu(z)
            dz.append(dm * gh * gn * _dsilu(z))
            dgn_acc = dgn_acc + jnp.sum(d_gn * gh, axis=0, keepdims=True)
            d_gh = d_gn * gn
            dg.append(rstd * (d_gh - gh * _head_mean(d_gh * gh, mv, 64.0)))
        do_s = [_stack_heads(dg[c], mv) for c in cs]
        dattn = [_mm_nt(do_s[c], vs[c]) for c in cs]
        ks, e_pos, e_neg, q_pos, q_neg, k_pos, k_neg, qp_s, qn_s, past, fut, a_row, w_dec, kd = ([] for _ in range(14))
        for c in cs:
            q = ps[c][:, 0:128]
            k = ps[c][:, 128:256] * GLA_KSCALE
            last = cum[c][C - 1:C, :]
            ep, en = jnp.exp(cum[c]), jnp.exp(-cum[c])
            ks.append(k), e_pos.append(ep), e_neg.append(en)
            q_pos.append(q * ep), q_neg.append(q * en), k_pos.append(k * ep), k_neg.append(k * en)
            qp_s.append(_stack_heads(q_pos[c], mk)), qn_s.append(_stack_heads(q_neg[c], mk))
            past.append(_mm_nt(qp_s[c], k_neg[c]))
            fut.append(_mm_nt(qn_s[c], k_pos[c]))
            a_row.append(jnp.exp(last))
            w_dec.append(jnp.exp(last - cum[c]))
            kd.append(k * w_dec[c])
        sts = [st_ref[0, c] for c in cs]
        dq_st = [_mm(dg[c], sts[c]) for c in cs]
        dst_in = [_mm_tn(dg[c], q_pos[c]) for c in cs]
        dv, dq_pos, dk_neg, dq_neg, dk_pos = [], [], [], [], []
        for c in cs:
            attn = jnp.where(lower, past[c], fut[c])
            dpast = jnp.where(lower, dattn[c], 0.0)
            dfut = jnp.where(lower, 0.0, dattn[c])
            dv.append(_mm_tn(attn, do_s[c]))
            dq_pos.append(_fold_heads(_mm(dpast, k_neg[c]), mk, C) + dq_st[c])
            dk_neg.append(_mm_tn(dpast, qp_s[c]))
            dq_neg.append(_fold_heads(_mm(dfut, k_pos[c]), mk, C))
            dk_pos.append(_mm_tn(dfut, qn_s[c]))
        dst_next = dst_sc[...]
        d_a, d_kd = [None] * G, [None] * G
        for c in reversed(cs):
            d_a[c] = jnp.sum(dst_next * sts[c], axis=0, keepdims=True)
            gmat = jnp.where(bd, dst_next, 0.0)
            d_kd[c] = _mm(vs[c], gmat)
            dv[c] = dv[c] + _mm_nt(kd[c], gmat)
            dst_next = dst_next * a_row[c] + jnp.where(bd, dst_in[c], 0.0)
        dst_sc[...] = dst_next
        row = lax.broadcasted_iota(jnp.int32, (C, 128), 0)
        d_la, dk, dq = [], [], []
        for c in cs:
            t = d_kd[c] * kd[c]
            dk.append(d_kd[c] * w_dec[c] + dk_neg[c] * e_neg[c] + dk_pos[c] * e_pos[c])
            dq.append(dq_pos[c] * e_pos[c] + dq_neg[c] * e_neg[c])
            d_last = jnp.sum(t, axis=0, keepdims=True) + d_a[c] * a_row[c]
            d_cum = (dq_pos[c] * q_pos[c] - dk_neg[c] * k_neg[c] - dq_neg[c] * q_neg[c] + dk_pos[c] * k_pos[c] - t)
            d_la.append(_mm_f32(utri, d_cum + jnp.where(row == C - 1, d_last, 0.0)))
        d_pre = [d_la[c] * _sig(-pre[c]) * (1.0 / GLA_TAU) for c in cs]
        d_gg = [_mm_nt(d_pre[c], w_ref[...]) for c in cs]
        dw_acc = _mm_tn(ps[0][:, 512:640], d_pre[0])
        db_acc = jnp.sum(d_pre[0], axis=0, keepdims=True)
        for c in cs[1:]:
            dw_acc = dw_acc + _mm_tn(ps[c][:, 512:640], d_pre[c])
            db_acc = db_acc + jnp.sum(d_pre[c], axis=0, keepdims=True)
        for c in cs:
            dp_ref[0, rows[c], :] = jnp.concatenate([dq[c], dk[c] * GLA_KSCALE, dv[c], d_gg[c], dz[c]],
                                                    axis=1).astype(_MXU)
        dw_ref[...] += dw_acc
        db_ref[...] += db_acc
        dgn_ref[...] += dgn_acc

        @pl.when((pl.program_id(0) == B - 1) & (pl.program_id(1) == NG - 1))
        def _():
            s1 = dgn_ref[...]
            s1 = s1 + pltpu.roll(s1, 128, 1)
            dgn_ref[...] = s1 + pltpu.roll(s1, 64, 1)

    tok = lambda w: pl.BlockSpec((1, G * C, w), lambda b, i: (b, NG - 1 - i, 0))
    return pl.pallas_call(
        body, name="gla_bwd", grid=(B, NG),
        in_specs=[tok(GLA_W), _full((128, 128)), _full((1, 128)), _full((1, 256)), tok(256),
                  pl.BlockSpec((1, G, 256, 128), lambda b, i: (b, NG - 1 - i, 0, 0)), tok(256)],
        out_specs=[tok(GLA_W), _full((128, 128)), _full((1, 128)), _full((1, 256))],
        out_shape=[jax.ShapeDtypeStruct((B, S, GLA_W), _MXU), jax.ShapeDtypeStruct((128, 128), F32),
                   jax.ShapeDtypeStruct((1, 128), F32), jax.ShapeDtypeStruct((1, 256), F32)],
        scratch_shapes=[pltpu.VMEM((256, 128), F32)],
        compiler_params=_cp(("arbitrary", "arbitrary")),
    )(gla_p, w_g2p, b_g2, gnw, raw, states, d_mix)


def _rms(x, w):
    rstd = lax.rsqrt(jnp.mean(x * x, axis=-1, keepdims=True) + EPS)
    xh = x * rstd
    return xh, rstd, xh * w


def _rms_bwd(dy, xh, rstd, w):
    dxh = dy * w
    return rstd * (dxh - xh * jnp.mean(dxh * xh, axis=-1, keepdims=True))


MLA_T = 256


def _mla_prep_fwd(mla_p, cos, sin, qnw, kvnw, w_uq, w_ukv):
    B, S, _ = mla_p.shape
    tm = min(S, 512)

    t = min(MLA_T, S)
    nt = tm // t

    def body(p_ref, c_ref, s_ref, qn_ref, kn_ref, wq_ref, wkv_ref, wkvt_ref, q_ref, k_ref, v_ref, kt_ref, vt_ref):
        p = p_ref[0].astype(F32)
        cs, sn = c_ref[0], s_ref[0]
        _, _, qn = _rms(p[:, 0:256], qn_ref[...])
        qpre = _mm(qn, wq_ref[...])
        _, _, kvn = _rms(p[:, 256:384], kn_ref[...])
        kv = _mm(kvn, wkv_ref[...])
        kvt = _mm_nt(wkvt_ref[...], kvn)
        kpe = _rope128(p[:, 384:512], cs, sn)
        kpet = kpe.T
        for h in range(8):
            sl = slice(128 * h, 128 * h + 128)
            q_ref[0, :, sl] = _rope128(qpre[:, sl], cs, sn).astype(_MXU)
            k_ref[0, :, sl] = (kv[:, sl] + kpe).astype(_MXU)
            kht = kvt[sl, :] + kpet
            for n in range(nt):
                kt_ref[0, n, sl, :] = kht[:, n * t:(n + 1) * t].astype(_MXU)
        v_ref[0] = kv[:, 1024:1536].astype(_MXU)
        for n in range(nt):
            vt_ref[0, n] = kvt[1024:1536, n * t:(n + 1) * t].astype(_MXU)

    tok = lambda w: pl.BlockSpec((1, tm, w), lambda b, i: (b, i, 0))
    tr = lambda w: pl.BlockSpec((1, nt, w, t), lambda b, i: (b, i, 0, 0))
    return pl.pallas_call(
        body, name="mla_prep_fwd", grid=(B, S // tm),
        in_specs=[tok(512), tok(128), tok(128), _full((1, 256)), _full((1, 128)), _full((256, 1024)),
                  _full((128, 1536)), _full((1536, 128))],
        out_specs=[tok(1024), tok(1024), tok(512), tr(1024), tr(512)],
        out_shape=[jax.ShapeDtypeStruct((B, S, 1024), _MXU), jax.ShapeDtypeStruct((B, S, 1024), _MXU),
                   jax.ShapeDtypeStruct((B, S, 512), _MXU), jax.ShapeDtypeStruct((B, S // t, 1024, t), _MXU),
                   jax.ShapeDtypeStruct((B, S // t, 512, t), _MXU)],
        compiler_params=_cp(("parallel", "parallel")),
    )(mla_p, cos, sin, qnw, kvnw, w_uq, w_ukv, w_ukv.T)


def _chunk_mask_t(t):
    kj = lax.broadcasted_iota(jnp.int32, (t, t), 0) // CHUNK
    qi = lax.broadcasted_iota(jnp.int32, (t, t), 1) // CHUNK
    return kj <= qi


MLA_HG = 8
MLA_HG_FWD = 8
LOG2E = 1.4426950408889634
MLA_C2 = MLA_SCALE * LOG2E


def _mla_attn_fwd(q, k, vt):
    B, S, _ = q.shape
    t = min(MLA_T, S)
    nq = S // t
    HG = MLA_HG_FWD
    NP = HG // 2

    def body(q_ref, k_ref, vt_ref, o_ref, lse_ref, sa, sb, m_sc, l_sc, acc_sc):
        i = pl.program_id(2)
        row = lax.broadcasted_iota(jnp.int32, (128, 1), 0)
        low = row < 64
        mask = _chunk_mask_t(t)
        m_sc[...] = jnp.full(m_sc.shape, -jnp.inf, F32)
        l_sc[...] = jnp.zeros_like(l_sc)
        acc_sc[...] = jnp.zeros_like(acc_sc)

        ones = jnp.ones((8, t), _MXU)

        def scores(j, buf):
            kb = k_ref[0, pl.ds(pl.multiple_of(j * t, t), t), :]
            for h in range(HG):
                cols = slice(128 * h, 128 * h + 128)
                buf[h] = (_mm_nt(kb[:, cols], q_ref[0, :, cols]) * MLA_C2).astype(_MXU)

        def absorb(j, buf, masked):
            vtb = vt_ref[0, j]
            for pr in range(NP):
                alphas, pvs = [], []
                for hh in range(2):
                    h = 2 * pr + hh
                    s = buf[h]
                    if masked:
                        s = jnp.where(mask, s, jnp.full_like(s, -jnp.inf))
                    m_old = m_sc[h]
                    m_new = jnp.maximum(m_old, jnp.max(s, axis=0, keepdims=True).astype(F32))
                    alpha = jnp.exp2(m_old - m_new)
                    p = jnp.exp2(s - m_new.astype(_MXU))
                    l_sc[h] = alpha * l_sc[h] + _mm(ones, p)[0:1, :]
                    m_sc[h] = m_new
                    vth = vtb[128 * pr:128 * pr + 128, :]
                    vth = jnp.where(low if hh == 0 else ~low, vth, jnp.zeros_like(vth))
                    pvs.append(_mm(vth, p))
                    alphas.append(alpha)
                acc_sc[pr] = acc_sc[pr] * jnp.where(low, alphas[0], alphas[1]) + pvs[0] + pvs[1]

        scores(0, sb)

        def pair(jj, carry):
            j0 = 2 * jj
            scores(j0 + 1, sa)
            absorb(j0, sb, False)
            scores(j0 + 2, sb)
            absorb(j0 + 1, sa, False)
            return carry

        lax.fori_loop(0, i // 2, pair, 0)

        @pl.when(i % 2 == 1)
        def _():
            scores(i, sa)
            absorb(i - 1, sb, False)
            absorb(i, sa, True)

        @pl.when(i % 2 == 0)
        def _():
            absorb(i, sb, True)

        for pr in range(NP):
            l_e, l_o = l_sc[2 * pr], l_sc[2 * pr + 1]
            o_ref[0, :, 128 * pr:128 * pr + 128] = (acc_sc[pr] / jnp.where(low, l_e, l_o)).T
            lse_ref[0, pr, 0, 0:1, :] = m_sc[2 * pr] + jnp.log(l_e) * LOG2E
            lse_ref[0, pr, 0, 1:2, :] = m_sc[2 * pr + 1] + jnp.log(l_o) * LOG2E

    return pl.pallas_call(
        body, name="mla_attn_fwd", grid=(B, 8 // HG, nq),
        in_specs=[pl.BlockSpec((1, t, 128 * HG), lambda b, g, i: (b, i, g)),
                  pl.BlockSpec((1, S, 128 * HG), lambda b, g, i: (b, 0, g)),
                  pl.BlockSpec((1, nq, 64 * HG, t), lambda b, g, i: (b, 0, g, 0))],
        out_specs=[pl.BlockSpec((1, t, 64 * HG), lambda b, g, i: (b, i, g)),
                   pl.BlockSpec((1, NP, 1, 2, t), lambda b, g, i: (b, g, i, 0, 0))],
        out_shape=[jax.ShapeDtypeStruct((B, S, 512), F32), jax.ShapeDtypeStruct((B, 4, nq, 2, t), F32)],
        scratch_shapes=[pltpu.VMEM((HG, t, t), _MXU), pltpu.VMEM((HG, t, t), _MXU), pltpu.VMEM((HG, 1, t), F32),
                        pltpu.VMEM((HG, 1, t), F32), pltpu.VMEM((NP, 128, t), F32)],
        compiler_params=_cp(("parallel", "parallel", "arbitrary")),
    )(q, k, vt)


def _mla_attn_bwd(q, k, v, kt, do, lse, dl):
    B, S, _ = q.shape
    t = min(MLA_T, S)
    nk = S // t

    HG = MLA_HG
    NP = HG // 2

    def body(q_ref, k_ref, v_ref, kt_ref, do_ref, lse_ref, dl_ref, dq_ref, dk_ref, dv_ref,
             sa, da, sb, db, dqt_sc, dk_sc, dv_sc):
        j = pl.program_id(2)

        @pl.when(j == 0)
        def _():
            dqt_sc[...] = jnp.zeros_like(dqt_sc)

        dk_sc[...] = jnp.zeros_like(dk_sc)
        dv_sc[...] = jnp.zeros_like(dv_sc)
        lane = lax.broadcasted_iota(jnp.int32, (1, 128), 1)
        low = lane < 64
        mask = _chunk_mask_t(t)

        def half(x, hh):
            return jnp.where(low if hh == 0 else ~low, x, jnp.zeros_like(x))

        def prepare(i, sbuf, dbuf):
            rows = pl.ds(pl.multiple_of(i * t, t), t)
            for h in range(HG):
                cols = slice(128 * h, 128 * h + 128)
                pc = slice(128 * (h // 2), 128 * (h // 2) + 128)
                sbuf[h] = _mm_nt(k_ref[0, :, cols], q_ref[0, rows, cols]) * MLA_C2
                dbuf[h] = _mm_nt(half(v_ref[0, :, pc], h % 2), do_ref[0, rows, pc])

        def absorb(i, sbuf, dbuf, masked):
            rows = pl.ds(pl.multiple_of(i * t, t), t)
            for h in range(HG):
                pr, hh = h // 2, h % 2
                cols = slice(128 * h, 128 * h + 128)
                pc = slice(128 * pr, 128 * pr + 128)
                p = jnp.exp2(sbuf[h] - lse_ref[0, pr, i][hh:hh + 1, :])
                if masked:
                    p = jnp.where(mask, p, 0.0)
                dv_sc[pr] += _mm(p, half(do_ref[0, rows, pc], hh))
                ds = p * (dbuf[h] - dl_ref[0, pr, i][hh:hh + 1, :])
                dqt_sc[i, cols, :] += _mm(kt_ref[0, 0, cols, :], ds)
                dk_sc[h] += _mm(ds, q_ref[0, rows, cols])

        n = nk - 1 - j
        prepare(jnp.minimum(j + 1, nk - 1), sb, db)

        def pair(jj, carry):
            i0 = j + 1 + 2 * jj
            prepare(i0 + 1, sa, da)
            absorb(i0, sb, db, False)
            prepare(jnp.where(i0 + 2 <= nk - 1, i0 + 2, j), sb, db)
            absorb(i0 + 1, sa, da, False)
            return carry

        lax.fori_loop(0, n // 2, pair, 0)

        @pl.when(n % 2 == 1)
        def _():
            prepare(j, sa, da)
            absorb(nk - 1, sb, db, False)
            absorb(j, sa, da, True)

        @pl.when(n % 2 == 0)
        def _():
            absorb(j, sb, db, True)

        for h in range(HG):
            dk_ref[0, :, 128 * h:128 * h + 128] = (dk_sc[h] * MLA_SCALE).astype(_MXU)
        for pr in range(NP):
            dv_ref[0, :, 128 * pr:128 * pr + 128] = dv_sc[pr].astype(_MXU)

        @pl.when(j == nk - 1)
        def _():
            for i in range(nk):
                dq_ref[0, i * t:(i + 1) * t, :] = (dqt_sc[i].T * MLA_SCALE).astype(_MXU)

    seq = lambda w: pl.BlockSpec((1, S, w), lambda b, g, j: (b, 0, g))
    blk = lambda w: pl.BlockSpec((1, t, w), lambda b, g, j: (b, j, g))
    stat = pl.BlockSpec((1, NP, nk, 2, t), lambda b, g, j: (b, g, 0, 0, 0))
    return pl.pallas_call(
        body, name="mla_attn_bwd", grid=(B, 8 // HG, nk),
        in_specs=[seq(128 * HG), blk(128 * HG), blk(64 * HG),
                  pl.BlockSpec((1, 1, 128 * HG, t), lambda b, g, j: (b, j, g, 0)), seq(64 * HG), stat, stat],
        out_specs=[seq(128 * HG), blk(128 * HG), blk(64 * HG)],
        out_shape=[jax.ShapeDtypeStruct((B, S, 1024), _MXU), jax.ShapeDtypeStruct((B, S, 1024), _MXU),
                   jax.ShapeDtypeStruct((B, S, 512), _MXU)],
        scratch_shapes=[pltpu.VMEM((HG, t, t), F32), pltpu.VMEM((HG, t, t), F32), pltpu.VMEM((HG, t, t), F32),
                        pltpu.VMEM((HG, t, t), F32), pltpu.VMEM((nk, 128 * HG, t), F32),
                        pltpu.VMEM((HG, t, 128), F32), pltpu.VMEM((NP, t, 128), F32)],
        compiler_params=_cp(("parallel", "parallel", "arbitrary"), 56),
    )(q, k, v, kt, do, lse, dl)


def _mla_prep_bwd(mla_p, cos, sin, qnw, kvnw, w_uq, w_ukv, dq, dk, dv):
    B, S, _ = mla_p.shape
    tm = min(S, 512)

    def body(p_ref, c_ref, s_ref, qn_ref, kn_ref, wq_ref, wkv_ref, dq_ref, dk_ref, dv_ref,
             dp_ref, dwq_ref, dwkv_ref, dqn_ref, dkn_ref):
        first = (pl.program_id(0) == 0) & (pl.program_id(1) == 0)

        @pl.when(first)
        def _():
            dwq_ref[...] = jnp.zeros_like(dwq_ref)
            dwkv_ref[...] = jnp.zeros_like(dwkv_ref)
            dqn_ref[...] = jnp.zeros_like(dqn_ref)
            dkn_ref[...] = jnp.zeros_like(dkn_ref)

        p = p_ref[0].astype(F32)
        cs, sn = c_ref[0], s_ref[0]
        lane = lax.broadcasted_iota(jnp.int32, (1, 128), 1)
        pe = (lane < 16) | ((lane >= 64) & (lane < 80))
        qh, q_rstd, qn = _rms(p[:, 0:256], qn_ref[...])
        kvh, kv_rstd, kvn = _rms(p[:, 256:384], kn_ref[...])
        dqv = dq_ref[0].astype(F32)
        dkv = dk_ref[0].astype(F32)
        dqpre = jnp.concatenate(
            [_rope128_t(dqv[:, 128 * h:128 * h + 128], cs, sn) for h in range(8)], axis=1)
        dkpe = jnp.zeros((tm, 128), F32)
        for h in range(8):
            dkpe = dkpe + jnp.where(pe, dkv[:, 128 * h:128 * h + 128], 0.0)
        dkr = _rope128_t(dkpe, cs, sn)
        dkv_all = jnp.concatenate([dkv, dv_ref[0].astype(F32)], axis=1)
        d_qn = _mm_nt(dqpre, wq_ref[...])
        d_kvn = _mm_nt(dkv_all, wkv_ref[...])
        dwq_ref[...] += _mm_tn(qn, dqpre)
        dwkv_ref[...] += _mm_tn(kvn, dkv_all)
        dqn_ref[...] += jnp.sum(d_qn * qh, axis=0, keepdims=True)
        dkn_ref[...] += jnp.sum(d_kvn * kvh, axis=0, keepdims=True)
        dp_ref[0] = jnp.concatenate([_rms_bwd(d_qn, qh, q_rstd, qn_ref[...]),
                                     _rms_bwd(d_kvn, kvh, kv_rstd, kn_ref[...]), dkr], axis=1).astype(_MXU)

    tok = lambda w: pl.BlockSpec((1, tm, w), lambda b, i: (b, i, 0))
    return pl.pallas_call(
        body, name="mla_prep_bwd", grid=(B, S // tm),
        in_specs=[tok(512), tok(128), tok(128), _full((1, 256)), _full((1, 128)), _full((256, 1024)),
                  _full((128, 1536)), tok(1024), tok(1024), tok(512)],
        out_specs=[tok(512), _full((256, 1024)), _full((128, 1536)), _full((1, 256)), _full((1, 128))],
        out_shape=[jax.ShapeDtypeStruct((B, S, 512), _MXU), jax.ShapeDtypeStruct((256, 1024), F32),
                   jax.ShapeDtypeStruct((128, 1536), F32), jax.ShapeDtypeStruct((1, 256), F32),
                   jax.ShapeDtypeStruct((1, 128), F32)],
        compiler_params=_cp(("arbitrary", "arbitrary")),
    )(mla_p, cos, sin, qnw, kvnw, w_uq, w_ukv, dq, dk, dv)


def _out_fwd(x, gate, r_g, o_mla, mla_p, g_g, w_out):
    B, S, D = x.shape
    tm = min(S, 512)

    def body(x_ref, g_ref, r_ref, o_ref, z_ref, gg_ref, w_ref, xn_ref, y_ref, mm_ref):
        mm = (o_ref[0] * _silu(z_ref[0].astype(F32))).astype(_MXU)
        mm_ref[0] = mm
        y = (jnp.dot(r_ref[0], w_ref[0:256, :], preferred_element_type=F32)
             + jnp.dot(mm, w_ref[256:768, :], preferred_element_type=F32)
             + jnp.dot(gg_ref[0], w_ref[768:1024, :], preferred_element_type=F32))
        y_ref[0] = y.astype(_MXU)
        xn_ref[0] = x_ref[0] + g_ref[0] * y

    tok = lambda w, c=0: pl.BlockSpec((1, tm, w), lambda b, i: (b, i, c))
    return pl.pallas_call(
        body, name="out_fwd", grid=(B, S // tm),
        in_specs=[tok(D), pl.BlockSpec((1, 1, D), lambda b, i: (b, 0, 0)), tok(256), tok(512), tok(512, 1),
                  tok(256), _full((D, D))],
        out_specs=[tok(D), tok(D), tok(512)],
        out_shape=[jax.ShapeDtypeStruct((B, S, D), F32), jax.ShapeDtypeStruct((B, S, D), _MXU),
                   jax.ShapeDtypeStruct((B, S, 512), _MXU)],
        compiler_params=_cp(("parallel", "parallel")),
    )(x, gate, r_g, o_mla, mla_p, g_g, w_out)


def _out_bwd(dx, y, gate, r_g, mm, g_g, w_out, o_mla, mla_p):
    B, S, D = dx.shape
    tm = min(S, 512)
    t = min(MLA_T, S)
    nt = tm // t

    def body(dx_ref, y_ref, g_ref, r_ref, mm_ref, gg_ref, w_ref, o_ref, z_ref,
             dr_ref, do_ref, dz_ref, dl_ref, dg_ref, dw_ref, dgate_ref):
        first = (pl.program_id(0) == 0) & (pl.program_id(1) == 0)

        @pl.when(first)
        def _():
            dw_ref[...] = jnp.zeros_like(dw_ref)

        @pl.when(pl.program_id(1) == 0)
        def _():
            dgate_ref[...] = jnp.zeros_like(dgate_ref)

        dxv = dx_ref[0]
        dgate_ref[0] += jnp.sum(dxv * y_ref[0].astype(F32), axis=0, keepdims=True)
        dy = (dxv * g_ref[0]).astype(_MXU)
        dr_ref[0] = _mm_nt(dy, w_ref[0:256, :])
        dg_ref[0] = _mm_nt(dy, w_ref[768:1024, :])
        dw_ref[0:256, :] += _mm_tn(r_ref[0], dy)
        dw_ref[256:768, :] += _mm_tn(mm_ref[0], dy)
        dw_ref[768:1024, :] += _mm_tn(gg_ref[0], dy)
        dm = _mm_nt(dy, w_ref[256:768, :])
        ov, z = o_ref[0], z_ref[0].astype(F32)
        do = dm * _silu(z)
        dz_ref[0] = (dm * ov * _dsilu(z)).astype(_MXU)
        do_ref[0] = do.astype(_MXU)
        prod = do * ov
        for pr in range(4):
            pt = prod[:, 128 * pr:128 * pr + 128].T
            se = jnp.sum(pt[0:64], axis=0, keepdims=True)
            so = jnp.sum(pt[64:128], axis=0, keepdims=True)
            for n in range(nt):
                dl_ref[0, pr, n, 0:1, :] = se[:, n * t:(n + 1) * t]
                dl_ref[0, pr, n, 1:2, :] = so[:, n * t:(n + 1) * t]

    tok = lambda w, c=0: pl.BlockSpec((1, tm, w), lambda b, i: (b, i, c))
    per_seq = pl.BlockSpec((1, 1, D), lambda b, i: (b, 0, 0))
    return pl.pallas_call(
        body, name="out_bwd", grid=(B, S // tm),
        in_specs=[tok(D), tok(D), per_seq, tok(256), tok(512), tok(256), _full((D, D)), tok(512), tok(512, 1)],
        out_specs=[tok(256), tok(512), tok(512), pl.BlockSpec((1, 4, nt, 2, t), lambda b, i: (b, 0, i, 0, 0)),
                   tok(256), _full((D, D)), per_seq],
        out_shape=[jax.ShapeDtypeStruct((B, S, 256), F32), jax.ShapeDtypeStruct((B, S, 512), _MXU),
                   jax.ShapeDtypeStruct((B, S, 512), _MXU), jax.ShapeDtypeStruct((B, 4, S // t, 2, t), F32),
                   jax.ShapeDtypeStruct((B, S, 256), F32), jax.ShapeDtypeStruct((D, D), F32),
                   jax.ShapeDtypeStruct((B, 1, D), F32)],
        compiler_params=_cp(("arbitrary", "arbitrary")),
    )(dx, y, gate, r_g, mm, g_g, w_out, o_mla, mla_p)


def _proj_bwd_x(x, shift, scale, nw, w_arr, d_ret, d_mla, d_mz, d_gla, dx_out):
    B, S, D = x.shape
    tm = min(S, 512)

    def body(x_ref, sc_ref, nw_ref, w_ref, dr_ref, dm_ref, dz_ref, dg_ref, dxo_ref,
             dx_ref, dsh_ref, dsc_ref, dnw_ref):
        first = (pl.program_id(0) == 0) & (pl.program_id(1) == 0)

        @pl.when(first)
        def _():
            dnw_ref[...] = jnp.zeros_like(dnw_ref)

        @pl.when(pl.program_id(1) == 0)
        def _():
            dsh_ref[...] = jnp.zeros_like(dsh_ref)
            dsc_ref[...] = jnp.zeros_like(dsc_ref)

        dp = jnp.concatenate([dr_ref[0], dm_ref[0], dz_ref[0], dg_ref[0]], axis=1)
        dh = lax.dot_general(dp, w_ref[...], (((1,), (1,)), ((), ())), preferred_element_type=F32)
        xv = x_ref[0]
        rstd = lax.rsqrt(jnp.mean(xv * xv, axis=-1, keepdims=True) + EPS)
        xh = xv * rstd
        nwv = nw_ref[...]
        mod = 1.0 + sc_ref[0]
        dsh_ref[0] += jnp.sum(dh, axis=0, keepdims=True)
        dsc_ref[0] += jnp.sum(dh * xh * nwv, axis=0, keepdims=True)
        dnw_ref[...] += jnp.sum(dh * xh * mod, axis=0, keepdims=True)
        dxh = dh * nwv * mod
        dx_ref[0] = dxo_ref[0] + rstd * (dxh - xh * jnp.mean(dxh * xh, axis=-1, keepdims=True))

    tok = lambda w: pl.BlockSpec((1, tm, w), lambda b, i: (b, i, 0))
    per_seq = pl.BlockSpec((1, 1, D), lambda b, i: (b, 0, 0))
    return pl.pallas_call(
        body, name="proj_bwd_x", grid=(B, S // tm),
        in_specs=[tok(D), per_seq, _full((1, D)), _full((D, ARR_W)), tok(RET_W), tok(512), tok(512),
                  tok(GLA_W), tok(D)],
        out_specs=[tok(D), per_seq, per_seq, _full((1, D))],
        out_shape=[jax.ShapeDtypeStruct((B, S, D), F32), jax.ShapeDtypeStruct((B, 1, D), F32),
                   jax.ShapeDtypeStruct((B, 1, D), F32), jax.ShapeDtypeStruct((1, D), F32)],
        compiler_params=_cp(("arbitrary", "arbitrary")),
    )(x, scale, nw, w_arr, d_ret, d_mla, d_mz, d_gla, dx_out)


def _proj_bwd_w(h, d_ret, d_mla, d_mz, d_gla):
    B, S, D = h.shape
    tm = min(S, 512)

    def body(h_ref, dr_ref, dm_ref, dz_ref, dg_ref, dw_ref):
        first = (pl.program_id(0) == 0) & (pl.program_id(1) == 0)

        @pl.when(first)
        def _():
            dw_ref[...] = jnp.zeros_like(dw_ref)

        hv = h_ref[0]
        tn = lambda d_ref: lax.dot_general(hv, d_ref[0], (((0,), (0,)), ((), ())), preferred_element_type=F32)
        dw_ref[:, 0:RET_W] += tn(dr_ref)
        dw_ref[:, RET_W:RET_W + 512] += tn(dm_ref)
        dw_ref[:, RET_W + 512:RET_W + MLA_W] += tn(dz_ref)
        dw_ref[:, RET_W + MLA_W:ARR_W] += tn(dg_ref)

    tok = lambda w: pl.BlockSpec((1, tm, w), lambda b, i: (b, i, 0))
    return pl.pallas_call(
        body, name="proj_bwd_w", grid=(B, S // tm),
        in_specs=[tok(D), tok(RET_W), tok(512), tok(512), tok(GLA_W)],
        out_specs=_full((D, ARR_W)), out_shape=jax.ShapeDtypeStruct((D, ARR_W), F32),
        compiler_params=_cp(("arbitrary", "arbitrary"), 56),
    )(h, d_ret, d_mla, d_mz, d_gla)


def _out_fwd_loss(x, gate, r_g, o_mla, mla_p, g_g, w_out, fw, target):
    B, S, D = x.shape
    tm = min(S, 512)

    def body(x_ref, g_ref, r_ref, o_ref, z_ref, gg_ref, w_ref, fw_ref, t_ref, dx_ref, y_ref, mm_ref, loss_ref, dfw_ref):
        first = (pl.program_id(0) == 0) & (pl.program_id(1) == 0)

        @pl.when(first)
        def _():
            loss_ref[...] = jnp.zeros_like(loss_ref)
            dfw_ref[...] = jnp.zeros_like(dfw_ref)

        mm = (o_ref[0] * _silu(z_ref[0].astype(F32))).astype(_MXU)
        mm_ref[0] = mm
        y = (jnp.dot(r_ref[0], w_ref[0:256, :], preferred_element_type=F32)
             + jnp.dot(mm, w_ref[256:768, :], preferred_element_type=F32)
             + jnp.dot(gg_ref[0], w_ref[768:1024, :], preferred_element_type=F32))
        y_ref[0] = y.astype(_MXU)
        xv = x_ref[0] + g_ref[0] * y
        fwv = fw_ref[...]
        rstd = lax.rsqrt(jnp.mean(xv * xv, axis=-1, keepdims=True) + EPS)
        xh = xv * rstd
        err = xh * fwv - t_ref[0]
        loss_ref[...] += 0.5 * jnp.sum(jnp.mean(err * err, axis=-1, keepdims=True), axis=0, keepdims=True)
        dy = err * (1.0 / D)
        dfw_ref[...] += jnp.sum(dy * xh, axis=0, keepdims=True)
        dxh = dy * fwv
        dx_ref[0] = rstd * (dxh - xh * jnp.mean(dxh * xh, axis=-1, keepdims=True))

    tok = lambda w, c=0: pl.BlockSpec((1, tm, w), lambda b, i: (b, i, c))
    return pl.pallas_call(
        body, name="out_fwd_loss", grid=(B, S // tm),
        in_specs=[tok(D), pl.BlockSpec((1, 1, D), lambda b, i: (b, 0, 0)), tok(256), tok(512), tok(512, 1),
                  tok(256), _full((D, D)), _full((1, D)), tok(D)],
        out_specs=[tok(D), tok(D), tok(512), _full((1, 1)), _full((1, D))],
        out_shape=[jax.ShapeDtypeStruct((B, S, D), F32), jax.ShapeDtypeStruct((B, S, D), _MXU),
                   jax.ShapeDtypeStruct((B, S, 512), _MXU), jax.ShapeDtypeStruct((1, 1), F32),
                   jax.ShapeDtypeStruct((1, D), F32)],
        compiler_params=_cp(("arbitrary", "arbitrary")),
    )(x, gate, r_g, o_mla, mla_p, g_g, w_out, fw, target)


def _local_step(x, pos3, mod, loss_target, small, w_in_a, w_uq_a, w_ukv_a, w_out_b):
    B, S, D = x.shape
    tabs = _rope_tables(pos3)
    saved = []
    for l in range(DEPTH):
        last = (small["final_norm"].reshape(1, D), loss_target) if l == DEPTH - 1 else None
        x, s = _layer_fwd(x, tabs, mod[l], {n: a[l] for n, a in small.items() if n != "final_norm"},
                          w_in_a[l], w_uq_a[l], w_ukv_a[l], w_out_b[l], loss_head=last)
        saved.append(s)
    dx, loss, d_fw = x
    grads = dict(final_norm=d_fw.reshape(D))
    per_layer = [None] * DEPTH
    for l in reversed(range(DEPTH)):
        dx, per_layer[l] = _layer_bwd(dx, saved[l], tabs)
    for name in per_layer[0]:
        grads[name] = jnp.stack([per_layer[l][name] for l in range(DEPTH)])
    return loss, dx, grads


def _layer_fwd(x, tabs, mod_l, small_l, w_in_a, w_uq_a=None, w_ukv_a=None, w_out_b=None, late_weights=None,
               loss_head=None):
    B, S, D = x.shape
    cr, sr, cm, sm = tabs
    shift = mod_l[:, 0:D].reshape(B, 1, D)
    scale = mod_l[:, D:2 * D].reshape(B, 1, D)
    gate = mod_l[:, 2 * D:3 * D].reshape(B, 1, D)
    nw = small_l["norm_w"].reshape(1, D)
    qnw = small_l["mla_q_norm"].reshape(1, 256)
    kvnw = small_l["mla_kv_norm"].reshape(1, 128)
    w_g2p = jnp.pad(small_l["gla_w_g2"], ((0, 112), (0, 0)))
    b_g2 = small_l["gla_b_g2"].reshape(1, 128)
    gnw = jnp.tile(small_l["gla_norm"], 4).reshape(1, 256)
    ret_p, mla_p, gla_p, h = _proj_fwd(x, shift, scale, nw, w_in_a)
    r_g, r_raw, r_st = _ret_fwd(ret_p, cr, sr)
    if late_weights is not None:
        w_uq_a, w_ukv_a, w_out_b = late_weights(r_raw)
    q, k, v, kt, vt = _mla_prep_fwd(mla_p, cm, sm, qnw, kvnw, w_uq_a, w_ukv_a)
    o_mla, lse = _mla_attn_fwd(q, k, vt)
    g_g, g_raw, g_st = _gla_fwd(gla_p, w_g2p, b_g2, gnw)
    if loss_head is None:
        x_new, y, mm = _out_fwd(x, gate, r_g, o_mla, mla_p, g_g, w_out_b)
    else:
        dx, y, mm, loss, d_fw = _out_fwd_loss(x, gate, r_g, o_mla, mla_p, g_g, w_out_b, *loss_head)
        x_new = (dx, loss, d_fw)
    saved = dict(x=x, shift=shift, scale=scale, gate=gate, nw=nw, qnw=qnw, kvnw=kvnw, w_g2p=w_g2p, b_g2=b_g2,
                 gnw=gnw, ret_p=ret_p, mla_p=mla_p, gla_p=gla_p, h=h, r_g=r_g, r_raw=r_raw, r_st=r_st, q=q, k=k,
                 v=v, kt=kt, o_mla=o_mla, lse=lse, g_g=g_g, g_raw=g_raw, g_st=g_st, y=y, mm=mm,
                 w_in_a=w_in_a, w_uq_a=w_uq_a, w_ukv_a=w_ukv_a, w_out_b=w_out_b)
    return x_new, saved


def _layer_bwd(dx, s, tabs, early_grads=None):
    B, S, D = dx.shape
    cr, sr, cm, sm = tabs
    d_r, do, d_mz, dl, d_g, dw_out, d_gate = _out_bwd(dx, s["y"], s["gate"], s["r_g"], s["mm"], s["g_g"], s["w_out_b"],
                                                      s["o_mla"], s["mla_p"])
    d_ret = _ret_bwd(s["ret_p"], cr, sr, s["r_raw"], s["r_st"], d_r)
    dq, dk, dv = _mla_attn_bwd(s["q"], s["k"], s["v"], s["kt"], do, s["lse"], dl)
    d_mla, dw_uq, dw_ukv, d_qnw, d_kvnw = _mla_prep_bwd(
        s["mla_p"], cm, sm, s["qnw"], s["kvnw"], s["w_uq_a"], s["w_ukv_a"], dq, dk, dv)
    gnw = s["gnw"] if early_grads is None else s["gnw"] + early_grads(dw_out, dw_uq, dw_ukv)
    d_gla, dw_g2p, db_g2, d_gnw = _gla_bwd(s["gla_p"], s["w_g2p"], s["b_g2"], gnw, s["g_raw"], s["g_st"], d_g)
    dx, d_shift, d_scale, d_nw = _proj_bwd_x(s["x"], s["shift"], s["scale"], s["nw"], s["w_in_a"],
                                             d_ret, d_mla, d_mz, d_gla, dx)
    dw_in = _proj_bwd_w(s["h"], d_ret, d_mla, d_mz, d_gla)
    grads = dict(
        d_mod=jnp.concatenate([d_shift, d_scale, d_gate], axis=2).reshape(B, 3 * D),
        norm_w=d_nw.reshape(D), mla_q_norm=d_qnw.reshape(256), mla_kv_norm=d_kvnw.reshape(128),
        gla_w_g2=dw_g2p[0:16], gla_b_g2=db_g2.reshape(128), gla_norm256=d_gnw.reshape(256),
        w_in_a=dw_in, w_uq_a=dw_uq, w_ukv_a=dw_ukv, w_out=dw_out)
    return dx, grads


def _exchange(arrs, gather, name):
    n = len(arrs)
    out_shape = [jax.ShapeDtypeStruct(((N_DEV,) + a.shape) if g else a.shape, a.dtype)
                 for a, g in zip(arrs, gather)]

    def body(*refs):
        ins, outs = refs[:n], refs[n:2 * n]
        send_sems, recv_sems, local_sems = refs[2 * n:]
        ix, iy, ic = lax.axis_index("x"), lax.axis_index("y"), lax.axis_index("c")
        me = 4 * ix + 2 * iy + ic
        copies = []
        for a in range(n):
            mine = ins[a] if gather[a] else ins[a].at[me]
            loc = pltpu.make_async_copy(mine, outs[a].at[me], local_sems.at[a])
            loc.start()
            copies.append(loc)
            for d in range(1, N_DEV):
                px = 1 - ix if d & 4 else ix
                py = 1 - iy if d & 2 else iy
                pc = 1 - ic if d & 1 else ic
                src = ins[a] if gather[a] else ins[a].at[4 * px + 2 * py + pc]
                cp = pltpu.make_async_remote_copy(
                    src_ref=src, dst_ref=outs[a].at[me], send_sem=send_sems.at[a, d - 1],
                    recv_sem=recv_sems.at[a, d - 1], device_id=(px, py, pc), device_id_type=pl.DeviceIdType.MESH)
                cp.start()
                copies.append(cp)
        for cp in copies:
            cp.wait()

    any_spec = pl.BlockSpec(memory_space=pl.ANY)
    outs = pl.pallas_call(
        body, name=name, in_specs=[any_spec] * n, out_specs=[any_spec] * n, out_shape=out_shape,
        scratch_shapes=[pltpu.SemaphoreType.DMA((n, N_DEV - 1)), pltpu.SemaphoreType.DMA((n, N_DEV - 1)),
                        pltpu.SemaphoreType.DMA((n,))],
    )(*arrs)
    return list(outs)


def _peers(ix, iy, ic):
    out = []
    for d in range(1, N_DEV):
        px = 1 - ix if d & 4 else ix
        py = 1 - iy if d & 2 else iy
        pc = 1 - ic if d & 1 else ic
        out.append((d - 1, (px, py, pc), 4 * px + 2 * py + pc))
    return out


def _exchange_start(arrs, gather, name, after=None):
    n = len(arrs)
    lands = [lax.empty(((N_DEV,) + a.shape) if g else a.shape, a.dtype) for a, g in zip(arrs, gather)]
    extra = [] if after is None else [after]

    def body(*refs):
        ins, land_refs = refs[:n], refs[n:2 * n]
        send_sems, recv_sems = refs[2 * n + len(extra)], refs[2 * n + len(extra) + 1]
        token = refs[-1]
        ix, iy, ic = lax.axis_index("x"), lax.axis_index("y"), lax.axis_index("c")
        me = 4 * ix + 2 * iy + ic
        for a in range(n):
            for k, peer, peer_idx in _peers(ix, iy, ic):
                pltpu.make_async_remote_copy(
                    src_ref=ins[a] if gather[a] else ins[a].at[peer_idx], dst_ref=land_refs[a].at[me],
                    send_sem=send_sems.at[7 * a + k], recv_sem=recv_sems.at[7 * a + k], device_id=peer,
                    device_id_type=pl.DeviceIdType.MESH).start()
        token[...] = jnp.zeros_like(token)

    hbm = pl.BlockSpec(memory_space=pltpu.HBM)
    sem = pl.BlockSpec(memory_space=pltpu.SEMAPHORE)
    held = [pltpu.with_memory_space_constraint(a, pltpu.HBM) for a in list(arrs) + lands]
    outs = pl.pallas_call(
        body, name=name,
        out_shape=(pltpu.SemaphoreType.DMA((7 * n,)), pltpu.SemaphoreType.DMA((7 * n,)),
                   *[pltpu.HBM(a.shape, a.dtype) for a in held], jax.ShapeDtypeStruct((8, 128), F32)),
        in_specs=[hbm] * (2 * n) + [pl.BlockSpec(memory_space=pl.ANY)] * len(extra),
        out_specs=(sem, sem, *[hbm] * (2 * n), pl.BlockSpec(memory_space=pltpu.VMEM)),
        input_output_aliases={a: 2 + a for a in range(2 * n)},
        compiler_params=pltpu.CompilerParams(has_side_effects=pltpu.SideEffectType.DATAFLOW_SIDE_EFFECTING),
    )(*held, *extra)
    return dict(send=outs[0], recv=outs[1], srcs=list(outs[2:2 + n]), lands=list(outs[2 + n:2 + 2 * n]),
                token=outs[-1], gather=list(gather))


def _exchange_wait(flight, after, me, name):
    n = len(flight["srcs"])
    gather = flight["gather"]

    def body(*refs):
        srcs, land_refs = refs[:n], refs[n:2 * n]
        send_sems, recv_sems = refs[2 * n], refs[2 * n + 1]
        ix, iy, ic = lax.axis_index("x"), lax.axis_index("y"), lax.axis_index("c")
        mine = 4 * ix + 2 * iy + ic
        for a in range(n):
            for k, peer, peer_idx in _peers(ix, iy, ic):
                cp = pltpu.make_async_remote_copy(
                    src_ref=srcs[a] if gather[a] else srcs[a].at[peer_idx], dst_ref=land_refs[a].at[mine],
                    send_sem=send_sems.at[7 * a + k], recv_sem=recv_sems.at[7 * a + k], device_id=peer,
                    device_id_type=pl.DeviceIdType.MESH)
                cp.wait_send()
                cp.wait_recv()

    hbm = pl.BlockSpec(memory_space=pltpu.HBM)
    sem = pl.BlockSpec(memory_space=pltpu.SEMAPHORE)
    held = flight["srcs"] + flight["lands"]
    outs = pl.pallas_call(
        body, name=name, out_shape=tuple(pltpu.HBM(a.shape, a.dtype) for a in held),
        in_specs=[hbm] * (2 * n) + [sem, sem, pl.BlockSpec(memory_space=pl.ANY)], out_specs=tuple([hbm] * (2 * n)),
        input_output_aliases={a: a for a in range(2 * n)},
        compiler_params=pltpu.CompilerParams(has_side_effects=pltpu.SideEffectType.DATAFLOW_SIDE_EFFECTING),
    )(*held, flight["send"], flight["recv"], after)
    got = []
    for a in range(n):
        src, land = outs[a], outs[n + a]
        own = src if gather[a] else lax.dynamic_index_in_dim(src, me, axis=0, keepdims=False)
        got.append(lax.dynamic_update_index_in_dim(land, own, me, axis=0))
    return got


def _ada_fwd(c_all, ada_w, ada_b_cols):
    nb, D = c_all.shape
    cols = ada_w.shape[2]

    def body(c_ref, w_ref, b_ref, out_ref):
        ca = _silu(c_ref[...])
        for l in range(DEPTH):
            out_ref[l] = _mm(ca, w_ref[l]) + b_ref[l:l + 1, :]

    return pl.pallas_call(
        body, name="ada_fwd", out_shape=jax.ShapeDtypeStruct((DEPTH, nb, cols), F32),
        in_specs=[pl.BlockSpec(memory_space=pltpu.VMEM)] * 3, out_specs=pl.BlockSpec(memory_space=pltpu.VMEM),
        compiler_params=pltpu.CompilerParams(vmem_limit_bytes=32 * VMEM_MB),
    )(c_all, ada_w, ada_b_cols)


def _ada_bwd(c_all, d_mod_cols):
    nb, D = c_all.shape
    cols = d_mod_cols.shape[2]

    def body(c_ref, dm_ref, out_ref):
        ca = _silu(c_ref[...])
        for l in range(DEPTH):
            out_ref[l] = _mm_tn(ca, dm_ref[l])

    return pl.pallas_call(
        body, name="ada_bwd", out_shape=jax.ShapeDtypeStruct((DEPTH, D, cols), F32),
        in_specs=[pl.BlockSpec(memory_space=pltpu.VMEM)] * 2, out_specs=pl.BlockSpec(memory_space=pltpu.VMEM),
        compiler_params=pltpu.CompilerParams(vmem_limit_bytes=32 * VMEM_MB),
    )(c_all, d_mod_cols)


def _sum_adamw(parts, w, m, v, name):
    P, R, C = parts.shape
    tr = 256 if (R % 256 == 0 and R > 256) else R

    def body(p_ref, w_ref, m_ref, v_ref, g_ref, d_ref, nm_ref, nv_ref):
        g = p_ref[0].astype(F32)
        for k in range(1, P):
            g = g + p_ref[k].astype(F32)
        g_ref[...] = g
        nm = ADAM_B1 * m_ref[...] + (1.0 - ADAM_B1) * g
        nv = ADAM_B2 * v_ref[...] + (1.0 - ADAM_B2) * (g * g)
        nm_ref[...] = nm
        nv_ref[...] = nv
        m_hat = nm / (1.0 - ADAM_B1 ** ADAM_STEP)
        v_hat = nv / (1.0 - ADAM_B2 ** ADAM_STEP)
        d_ref[...] = -ADAM_LR * (m_hat / (jnp.sqrt(v_hat) + ADAM_EPS) + ADAM_WD * w_ref[...])

    blk = pl.BlockSpec((tr, C), lambda i: (i, 0))
    shp = jax.ShapeDtypeStruct((R, C), F32)
    return pl.pallas_call(
        body, name=name, grid=(R // tr,),
        in_specs=[pl.BlockSpec((P, tr, C), lambda i: (0, i, 0)), blk, blk, blk],
        out_specs=[blk, blk, blk, blk], out_shape=[shp, shp, shp, shp],
        compiler_params=_cp(("parallel",)),
    )(parts, w, m, v)


def _sum_adamw_layer(parts, w, m, v, layer, name, prev=None, after=None):
    P, R, C = parts.shape
    tr = 256 if (R % 256 == 0 and R > 256) else R

    def body(p_ref, w_ref, m_ref, v_ref, *rest):
        g_ref, d_ref, nm_ref, nv_ref = rest[-4:]
        g = p_ref[0].astype(F32)
        for k in range(1, P):
            g = g + p_ref[k].astype(F32)
        g_ref[0] = g
        nm = ADAM_B1 * m_ref[0] + (1.0 - ADAM_B1) * g
        nv = ADAM_B2 * v_ref[0] + (1.0 - ADAM_B2) * (g * g)
        nm_ref[0] = nm
        nv_ref[0] = nv
        m_hat = nm / (1.0 - ADAM_B1 ** ADAM_STEP)
        v_hat = nv / (1.0 - ADAM_B2 ** ADAM_STEP)
        d_ref[0] = -ADAM_LR * (m_hat / (jnp.sqrt(v_hat) + ADAM_EPS) + ADAM_WD * w_ref[0])

    blk = pl.BlockSpec((1, tr, C), lambda i: (layer, i, 0))
    shp = jax.ShapeDtypeStruct(w.shape, F32)
    in_specs = [pl.BlockSpec((P, tr, C), lambda i: (0, i, 0)), blk, blk, blk]
    args = [parts, w, m, v]
    aliases = {}
    if prev is not None:
        in_specs += [pl.BlockSpec(memory_space=pl.ANY)] * 4
        args += list(prev)
        aliases = {4 + k: k for k in range(4)}
    if after is not None:
        in_specs.append(pl.BlockSpec(memory_space=pl.ANY))
        args.append(after)
    return list(pl.pallas_call(
        body, name=name, grid=(R // tr,), in_specs=in_specs, out_specs=[blk] * 4, out_shape=[shp] * 4,
        input_output_aliases=aliases, compiler_params=_cp(("parallel",)),
    )(*args))


SMALL = ["norm_w", "mla_q_norm", "mla_kv_norm", "gla_w_g2", "gla_b_g2", "gla_norm", "final_norm"]


SMALL_ROWS = 72


def _pack_small(loss, part):
    flat = [jnp.pad(loss.reshape(1), (0, 127))] + [part[n].reshape(-1) for n in SMALL]
    used = sum(f.shape[0] for f in flat)
    flat.append(jnp.zeros((SMALL_ROWS * 128 - used,), F32))
    return jnp.concatenate(flat).reshape(SMALL_ROWS, 128)


def _small_adamw(packed_parts, w, m, v):
    n = len(w)

    def body(*refs):
        p_ref = refs[0]
        w_refs, m_refs, v_refs = refs[1:1 + n], refs[1 + n:1 + 2 * n], refs[1 + 2 * n:1 + 3 * n]
        outs, acc = refs[1 + 3 * n:-1], refs[-1]
        total = p_ref[0]
        for k in range(1, N_DEV):
            total = total + p_ref[k]
        acc[...] = total
        outs[0][...] = acc[0:1, :]
        r0 = 1
        for i in range(n):
            shp = w_refs[i].shape
            if len(shp) == 3:
                g = acc[r0:r0 + shp[0] * shp[1], :].reshape(shp)
                r0 += shp[0] * shp[1]
            elif shp[1] < 128:
                g = acc[r0:r0 + shp[0], 0:shp[1]]
                r0 += shp[0]
            else:
                k = shp[1] // 128
                g = jnp.concatenate(
                    [jnp.concatenate([acc[r0 + l * k + j:r0 + l * k + j + 1, :] for j in range(k)], axis=1)
                     for l in range(shp[0])], axis=0)
                r0 += shp[0] * k
            nm = ADAM_B1 * m_refs[i][...] + (1.0 - ADAM_B1) * g
            nv = ADAM_B2 * v_refs[i][...] + (1.0 - ADAM_B2) * (g * g)
            m_hat = nm / (1.0 - ADAM_B1 ** ADAM_STEP)
            v_hat = nv / (1.0 - ADAM_B2 ** ADAM_STEP)
            outs[1 + 4 * i][...] = g
            outs[2 + 4 * i][...] = -ADAM_LR * (m_hat / (jnp.sqrt(v_hat) + ADAM_EPS) + ADAM_WD * w_refs[i][...])
            outs[3 + 4 * i][...] = nm
            outs[4 + 4 * i][...] = nv

    vmem = pl.BlockSpec(memory_space=pltpu.VMEM)
    out_shape = [jax.ShapeDtypeStruct((1, 128), F32)]
    for a in w:
        out_shape += [jax.ShapeDtypeStruct(a.shape, F32)] * 4
    outs = pl.pallas_call(
        body, name="adamw_small", in_specs=[vmem] * (1 + 3 * n), out_specs=[vmem] * (1 + 4 * n), out_shape=out_shape,
        scratch_shapes=[pltpu.VMEM((SMALL_ROWS, 128), F32)],
    )(packed_parts, *w, *m, *v)
    return outs[0], [outs[1 + 4 * i:5 + 4 * i] for i in range(n)]


WEIGHTS = ["norm_w", "ada_w", "ada_b", "w_in", "mla_q_norm", "w_uq", "mla_kv_norm", "w_ukv", "gla_w_g2",
           "gla_b_g2", "gla_norm", "w_out", "final_norm"]


def kernel(x, c, positions, norm_w, ada_w, ada_b, w_in, mla_q_norm, w_uq, mla_kv_norm, w_ukv, gla_w_g2, gla_b_g2, gla_norm, w_out, final_norm, loss_target, m_norm_w, m_ada_w, m_ada_b, m_w_in, m_mla_q_norm, m_w_uq, m_mla_kv_norm, m_w_ukv, m_gla_w_g2, m_gla_b_g2, m_gla_norm, m_w_out, m_final_norm, v_norm_w, v_ada_w, v_ada_b, v_w_in, v_mla_q_norm, v_w_uq, v_mla_kv_norm, v_w_ukv, v_gla_w_g2, v_gla_b_g2, v_gla_norm, v_w_out, v_final_norm):
    w = dict(norm_w=norm_w, ada_w=ada_w, ada_b=ada_b, w_in=w_in, mla_q_norm=mla_q_norm, w_uq=w_uq,
             mla_kv_norm=mla_kv_norm, w_ukv=w_ukv, gla_w_g2=gla_w_g2, gla_b_g2=gla_b_g2, gla_norm=gla_norm,
             w_out=w_out, final_norm=final_norm)
    m = dict(norm_w=m_norm_w, ada_w=m_ada_w, ada_b=m_ada_b, w_in=m_w_in, mla_q_norm=m_mla_q_norm, w_uq=m_w_uq,
             mla_kv_norm=m_mla_kv_norm, w_ukv=m_w_ukv, gla_w_g2=m_gla_w_g2, gla_b_g2=m_gla_b_g2,
             gla_norm=m_gla_norm, w_out=m_w_out, final_norm=m_final_norm)
    v = dict(norm_w=v_norm_w, ada_w=v_ada_w, ada_b=v_ada_b, w_in=v_w_in, mla_q_norm=v_mla_q_norm, w_uq=v_w_uq,
             mla_kv_norm=v_mla_kv_norm, w_ukv=v_w_ukv, gla_w_g2=v_gla_w_g2, gla_b_g2=v_gla_b_g2,
             gla_norm=v_gla_norm, w_out=v_w_out, final_norm=v_final_norm)
    B, S, D = x.shape
    me = 4 * lax.axis_index("x") + 2 * lax.axis_index("y") + lax.axis_index("c")
    ada_cols = ada_w.shape[2]
    cast = lambda a: a.astype(_MXU)

    sharded = ["w_in", "w_uq", "w_ukv", "w_out"]

    whole_cols = lambda a: jnp.transpose(a, (1, 0, 2)).reshape(a.shape[1], -1)
    whole_in = lambda blk: _arrange_w_in(whole_cols(blk))
    whole_rest = lambda blks: (_arrange_w_uq(whole_cols(blks[0])), _arrange_w_ukv(whole_cols(blks[1])),
                               blks[2].reshape(D, D))
    col_blocks = lambda a: jnp.transpose(a.reshape(a.shape[0], N_DEV, -1), (1, 0, 2)).astype(jnp.bfloat16)
    blocks_in = lambda dw_in_a: col_blocks(_unarrange_w_in(dw_in_a))
    blocks_rest = lambda dw_out, dw_uq_a, dw_ukv_a: [
        col_blocks(_unarrange_w_uq(dw_uq_a)), col_blocks(_unarrange_w_ukv(dw_ukv_a)),
        dw_out.reshape(N_DEV, D // N_DEV, D).astype(jnp.bfloat16)]

    (c_g,) = _exchange([c], [True], "gather_c")
    c_all = c_g.reshape(N_DEV * B, D)

    ada_b_cols = lax.dynamic_slice(ada_b, (0, me * ada_cols), (DEPTH, ada_cols))
    mod_cols = _ada_fwd(c_all, ada_w, ada_b_cols)
    mod_send = jnp.transpose(mod_cols.reshape(DEPTH, N_DEV, B, ada_cols), (1, 0, 2, 3))
    (mod_recv,) = _exchange([mod_send], [False], "scatter_mod")
    mod = jnp.transpose(mod_recv, (1, 2, 0, 3)).reshape(DEPTH, B, 3 * D)

    flight_i = _exchange_start([cast(w_in[0])], [True], "gather_start_first", after=mod)
    flight_r = _exchange_start([cast(w[n][0]) for n in sharded[1:]], [True] * 3, "gather_start_layer0",
                               after=flight_i["token"])
    flight_w = _exchange_start([cast(w[n][1]) for n in sharded], [True] * 4, "gather_start_layer1",
                               after=flight_r["token"])
    small_w = {n: w[n] for n in SMALL}
    layer_small = lambda l: {n: a[l] for n, a in small_w.items() if n != "final_norm"}
    tabs = _rope_tables(positions.reshape(B, S, 1), flight_w["token"][0, 0])
    late0 = lambda after: whole_rest(_exchange_wait(flight_r, after, me, "gather_wait_layer0"))
    (w_in0_g,) = _exchange_wait(flight_i, tabs[0], me, "gather_wait_first")
    x1, saved0 = _layer_fwd(x, tabs, mod[0], layer_small(0), whole_in(w_in0_g), late_weights=late0)
    got1 = _exchange_wait(flight_w, x1, me, "gather_wait_layer1")
    (dx, loss, d_fw), saved1 = _layer_fwd(x1, tabs, mod[1], layer_small(1), whole_in(got1[0]), *whole_rest(got1[1:]),
                                          loss_head=(final_norm.reshape(1, D), loss_target))

    dx, g1 = _layer_bwd(dx, saved1, tabs)
    flight_g = _exchange_start([blocks_in(g1["w_in_a"])] + blocks_rest(g1["w_out"], g1["w_uq_a"], g1["w_ukv_a"]),
                               [False] * 4, "grads_start_layer1")
    flights = {}

    def early0(dw_out, dw_uq_a, dw_ukv_a):
        flights["rest0"] = _exchange_start(blocks_rest(dw_out, dw_uq_a, dw_ukv_a), [False] * 3, "grads_start_layer0")
        return flights["rest0"]["token"][0, 0]

    saved0 = dict(saved0, gate=saved0["gate"] + flight_g["token"][0, 0])
    grad_x, g0 = _layer_bwd(dx, saved0, tabs, early_grads=early0)
    parts1 = _exchange_wait(flight_g, grad_x, me, "grads_wait_layer1")
    rest0 = _exchange_wait(flights["rest0"], g0["w_in_a"], me, "grads_wait_layer0")

    both = lambda n: jnp.stack([g0[n], g1[n]])
    d_mod = both("d_mod")
    part = dict(norm_w=both("norm_w"), mla_q_norm=both("mla_q_norm"), mla_kv_norm=both("mla_kv_norm"),
                gla_w_g2=both("gla_w_g2"), gla_b_g2=both("gla_b_g2"), gla_norm=both("gla_norm256")[:, 0:128],
                final_norm=d_fw)
    flight_l = _exchange_start([d_mod, _pack_small(loss, part), blocks_in(g0["w_in_a"])], [True, True, False],
                               "exchange_start_last")
    res = {}
    behind = flight_l["token"]
    for a, name in enumerate(sharded):
        res[name] = _sum_adamw_layer(parts1[a], w[name], m[name], v[name], 1, "adamw_%s_layer1" % name, after=behind)
        behind = res[name][1]
    for a, name in enumerate(sharded[1:]):
        res[name] = _sum_adamw_layer(rest0[a], w[name], m[name], v[name], 0, "adamw_%s_layer0" % name,
                                     prev=res[name], after=behind)
        behind = res[name][1]
    d_mod_g, small_g, in0 = _exchange_wait(flight_l, behind, me, "exchange_wait_last")
    res["w_in"] = _sum_adamw_layer(in0, w_in, m_w_in, v_w_in, 0, "adamw_w_in_layer0", prev=res["w_in"])

    d_mod_all = jnp.transpose(d_mod_g, (1, 0, 2, 3)).reshape(DEPTH, N_DEV * B, 3 * D)
    d_mod_cols = lax.dynamic_slice(d_mod_all, (0, 0, me * ada_cols), (DEPTH, N_DEV * B, ada_cols))
    g_ada_w = _ada_bwd(c_all, d_mod_cols)

    def update(name, parts2d):
        shp = w[name].shape
        two = lambda a: a.reshape(parts2d.shape[1:])
        out = _sum_adamw(parts2d, two(w[name]), two(m[name]), two(v[name]), "adamw_" + name)
        res[name] = [o.reshape(shp) for o in out]

    update("ada_w", g_ada_w.reshape(1, DEPTH * D, ada_cols))
    update("ada_b", jnp.transpose(d_mod_g, (0, 2, 1, 3)).reshape(N_DEV * B, DEPTH * 3 * D // 128, 128))
    row = lambda a: a.reshape(1, D) if a.ndim == 1 else a
    loss_sum, small_out = _small_adamw(small_g, [row(w[n]) for n in SMALL], [row(m[n]) for n in SMALL],
                                       [row(v[n]) for n in SMALL])
    for n, outs in zip(SMALL, small_out):
        res[n] = [o.reshape(w[n].shape) for o in outs]
    loss_out = loss_sum[0, 0]
    return (loss_out, grad_x, *[res[n][0] for n in WEIGHTS], *[res[n][1] for n in WEIGHTS],
            *[res[n][2] for n in WEIGHTS], *[res[n][3] for n in WEIGHTS])
```

```python
import functools
import math

import numpy as np
import jax
import jax.numpy as jnp
from jax import lax
from jax.experimental import pallas as pl
from jax.experimental.pallas import tpu as pltpu

F32 = jnp.float32
_MXU = jnp.bfloat16

D_MODEL = 1024
DEPTH = 2
CHUNK = 64
EPS = 1e-6
ROPE_THETA = 10000.0
N_DEV = 8

MLA_SCALE = 96.0 ** -0.5
RET_KSCALE = 64.0 ** -0.5
GLA_KSCALE = 32.0 ** -0.5
GLA_TAU = 16.0

ADAM_LR = 0.001
ADAM_B1 = 0.9
ADAM_B2 = 0.999
ADAM_EPS = 1e-08
ADAM_WD = 0.01
ADAM_STEP = 10

RET_W, MLA_W, GLA_W = 1024, 1024, 896
ARR_W = RET_W + MLA_W + GLA_W
VMEM_MB = 1024 * 1024


def _cp(sem, vmem_mb=48):
    return pltpu.CompilerParams(dimension_semantics=sem, vmem_limit_bytes=vmem_mb * VMEM_MB)


def _mm(a, b):
    return jnp.dot(a.astype(_MXU), b.astype(_MXU), preferred_element_type=F32)


def _mm_nt(a, b):
    return lax.dot_general(a.astype(_MXU), b.astype(_MXU), (((1,), (1,)), ((), ())),
                           preferred_element_type=F32)


def _mm_tn(a, b):
    return lax.dot_general(a.astype(_MXU), b.astype(_MXU), (((0,), (0,)), ((), ())),
                           preferred_element_type=F32)


def _mm_f32(a, b):
    return jnp.dot(a, b, precision=lax.Precision.HIGHEST, preferred_element_type=F32)


def _sig(z):
    return 1.0 / (1.0 + jnp.exp(-z))


def _silu(z):
    return z * _sig(z)


def _dsilu(z):
    s = _sig(z)
    return s * (1.0 + z * (1.0 - s))


def _full(shape):
    nd = len(shape)
    return pl.BlockSpec(shape, lambda *_: (0,) * nd)


def _qk_perm(blk):
    r = blk.shape[0]
    return jnp.transpose(blk.reshape(r, 4, 2, 32), (0, 2, 1, 3)).reshape(r, 256)


def _qk_unperm(blk):
    r = blk.shape[0]
    return jnp.transpose(blk.reshape(r, 2, 4, 32), (0, 2, 1, 3)).reshape(r, 256)


def _arrange_w_in(w):
    z = lambda n: jnp.zeros((w.shape[0], n), w.dtype)
    ret = [_qk_perm(w[:, 0:256]), _qk_perm(w[:, 256:512]), w[:, 512:768], w[:, 768:1024]]
    mla = [w[:, 1024:1280], w[:, 1280:1408], z(64), w[:, 1408:1440], z(32), w[:, 1440:1952]]
    gla = [w[:, 1952:2080], w[:, 2080:2208], w[:, 2208:2464], w[:, 2464:2480], z(112), w[:, 2480:2736]]
    return jnp.concatenate(ret + mla + gla, axis=1)


def _unarrange_w_in(a):
    m, g = RET_W, RET_W + MLA_W
    parts = [_qk_unperm(a[:, 0:256]), _qk_unperm(a[:, 256:512]), a[:, 512:1024],
             a[:, m:m + 384], a[:, m + 448:m + 480], a[:, m + 512:m + 1024],
             a[:, g:g + 528], a[:, g + 640:g + 896]]
    return jnp.concatenate(parts, axis=1)


def _arrange_w_uq(w):
    return jnp.pad(w.reshape(256, 8, 96), ((0, 0), (0, 0), (0, 32))).reshape(256, 1024)


def _unarrange_w_uq(a):
    return a.reshape(256, 8, 128)[:, :, :96].reshape(256, 768)


def _arrange_w_ukv(w):
    r = w.reshape(128, 8, 128)
    k = jnp.pad(r[:, :, :64], ((0, 0), (0, 0), (0, 64))).reshape(128, 1024)
    return jnp.concatenate([k, r[:, :, 64:].reshape(128, 512)], axis=1)


def _unarrange_w_ukv(a):
    k = a[:, :1024].reshape(128, 8, 128)[:, :, :64]
    v = a[:, 1024:].reshape(128, 8, 64)
    return jnp.concatenate([k, v], axis=2).reshape(128, 1024)


def _rope_tables(pos3, zero=0.0):
    B, S, _ = pos3.shape
    ts = min(S, 512)
    inv32 = (np.float32(ROPE_THETA) ** (-(np.arange(32, dtype=np.float32) / 32))).astype(np.float32)
    inv16 = (np.float32(ROPE_THETA) ** (-(np.arange(16, dtype=np.float32) / 16))).astype(np.float32)
    inv = np.zeros((1, 128), np.float32)
    inv[0, 0:32] = inv32
    inv[0, 32:48] = inv16

    def body(pos_ref, inv_ref, cr, sr, cm, sm):
        ang = pos_ref[0].astype(F32) * inv_ref[...]
        lane = lax.broadcasted_iota(jnp.int32, (1, 128), 1)

        def every_head(x):
            y = jnp.where(lane < 32, x, pltpu.roll(x, 32, 1))
            return jnp.where(lane < 64, y, pltpu.roll(y, 64, 1))

        def rotary_pair(x, fill):
            return jnp.where((lane >= 64) & (lane < 80), pltpu.roll(x, 32, 1),
                             jnp.where((lane >= 80) & (lane < 96), pltpu.roll(x, 48, 1), fill))

        c, s = jnp.cos(ang), jnp.sin(ang)
        cr[0] = every_head(c)
        sr[0] = every_head(s)
        cm[0] = rotary_pair(c, 1.0)
        sm[0] = rotary_pair(s, 0.0)

    tab = jax.ShapeDtypeStruct((B, S, 128), F32)
    blk = pl.BlockSpec((1, ts, 128), lambda b, i: (b, i, 0))
    return pl.pallas_call(
        body, name="rope_tables", grid=(B, S // ts),
        in_specs=[pl.BlockSpec((1, ts, 1), lambda b, i: (b, i, 0)), _full((1, 128))],
        out_specs=[blk, blk, blk, blk], out_shape=[tab, tab, tab, tab],
        compiler_params=_cp(("parallel", "parallel")),
    )(pos3, jnp.asarray(inv) + zero)


def _rope128(x, cos, sin):
    lane = lax.broadcasted_iota(jnp.int32, (1, 128), 1)
    rp = pltpu.roll(x, 16, 1)
    rm = pltpu.roll(x, 112, 1)
    return x * cos + jnp.where(lane < 80, -rm, rp) * sin


def _rope128_t(d, cos, sin):
    lane = lax.broadcasted_iota(jnp.int32, (1, 128), 1)
    y = d * sin
    yp = pltpu.roll(y, 16, 1)
    ym = pltpu.roll(y, 112, 1)
    return d * cos + jnp.where(lane < 64, 0.0, jnp.where(lane < 80, ym, jnp.where(lane < 96, -yp, 0.0)))


def _proj_fwd(x, shift, scale, nw, w_arr):
    B, S, D = x.shape
    tm = min(S, 512)

    def body(x_ref, sh_ref, sc_ref, nw_ref, w_ref, ret_ref, mla_ref, gla_ref, h_ref):
        xv = x_ref[0]
        rstd = lax.rsqrt(jnp.mean(xv * xv, axis=-1, keepdims=True) + EPS)
        h = (xv * rstd * nw_ref[...]) * (1.0 + sc_ref[0]) + sh_ref[0]
        hb = h.astype(_MXU)
        h_ref[0] = hb
        ret_ref[0] = jnp.dot(hb, w_ref[:, 0:RET_W], preferred_element_type=F32).astype(_MXU)
        mla_ref[0] = jnp.dot(hb, w_ref[:, RET_W:RET_W + MLA_W], preferred_element_type=F32).astype(_MXU)
        gla_ref[0] = jnp.dot(hb, w_ref[:, RET_W + MLA_W:ARR_W], preferred_element_type=F32).astype(_MXU)

    tok = lambda w: pl.BlockSpec((1, tm, w), lambda b, i: (b, i, 0))
    per_seq = pl.BlockSpec((1, 1, D), lambda b, i: (b, 0, 0))
    return pl.pallas_call(
        body, name="proj_fwd", grid=(B, S // tm),
        in_specs=[tok(D), per_seq, per_seq, _full((1, D)), _full((D, ARR_W))],
        out_specs=[tok(RET_W), tok(MLA_W), tok(GLA_W), tok(D)],
        out_shape=[jax.ShapeDtypeStruct((B, S, RET_W), _MXU), jax.ShapeDtypeStruct((B, S, MLA_W), _MXU),
                   jax.ShapeDtypeStruct((B, S, GLA_W), _MXU), jax.ShapeDtypeStruct((B, S, D), _MXU)],
        compiler_params=_cp(("parallel", "parallel")),
    )(x, shift, scale, nw, w_arr)


RET_L = 256


def _ret_consts(L):
    lg = np.log1p(-np.exp2(-5.0 - np.arange(4, dtype=np.float32))).astype(np.float32)
    i = np.arange(L)
    ci = i // CHUNK
    diff = (i[:, None] - i[None, :]).astype(np.float32)
    same = ci[:, None] == ci[None, :]
    past = ci[None, :] < ci[:, None]
    expo = np.where(same, np.abs(diff), np.where(past, diff, 0.0)).astype(np.float32)
    dec = np.where((same | past)[None], np.exp(lg[:, None, None] * expo[None]), 0.0).astype(np.float32)
    head = (np.arange(256) % 128) // 32
    qw = np.exp((i + 1.0)[:, None] * lg[head][None, :]).astype(np.float32)
    kw = np.exp((L - 1.0 - i)[:, None] * lg[head][None, :]).astype(np.float32)
    a_row = np.exp(np.float32(L) * lg[head])[None, :].astype(np.float32)
    return [jnp.asarray(t) for t in (dec.reshape(4 * L, L), qw, kw, a_row)]


def _ret_masks():
    lane = lax.broadcasted_iota(jnp.int32, (1, 256), 1)
    mh = [((lane % 128) // 32) == h for h in range(4)]
    mv = [(lane // 64) == h for h in range(4)]
    vi = lax.broadcasted_iota(jnp.int32, (256, 256), 0)
    ki = lax.broadcasted_iota(jnp.int32, (256, 256), 1)
    bd = (vi // 64) == ((ki % 128) // 32)
    return mh, mv, bd


def _ret_rope(p, cs, sn):
    q1, q2, k1, k2 = p[:, 0:128], p[:, 128:256], p[:, 256:384], p[:, 384:512]
    qr = jnp.concatenate([q1 * cs - q2 * sn, q2 * cs + q1 * sn], axis=1)
    kr = jnp.concatenate([k1 * cs - k2 * sn, k2 * cs + k1 * sn], axis=1) * RET_KSCALE
    return qr, kr


def _head_mean(x, mv, width):
    out = jnp.zeros_like(x)
    for m in mv:
        s = jnp.sum(jnp.where(m, x, 0.0), axis=-1, keepdims=True) * (1.0 / width)
        out = jnp.where(m, s, out)
    return out


def _stack_heads(x, masks):
    return jnp.concatenate([jnp.where(m, x, 0.0) for m in masks], axis=0)


def _fold_heads(xs, masks, L):
    out = jnp.where(masks[0], xs[0:L], 0.0)
    for h in range(1, 4):
        out = out + jnp.where(masks[h], xs[h * L:(h + 1) * L], 0.0)
    return out


RET_G = 2


def _ret_fwd(ret_p, cos, sin):
    B, S, _ = ret_p.shape
    L = min(RET_L, S)
    NB = S // L
    G = min(RET_G, NB)
    NG = NB // G
    consts = _ret_consts(L)

    def body(p_ref, c_ref, s_ref, ds_ref, qw_ref, kw_ref, a_ref, out_ref, raw_ref, st_ref, st_sc):
        @pl.when(pl.program_id(1) == 0)
        def _():
            st_sc[...] = jnp.zeros_like(st_sc)

        mh, mv, bd = _ret_masks()
        cs_ = range(G)
        rows = [slice(c * L, (c + 1) * L) for c in cs_]
        ps = [p_ref[0, rows[c], :].astype(F32) for c in cs_]
        qk = [_ret_rope(ps[c], c_ref[0, rows[c], :], s_ref[0, rows[c], :]) for c in cs_]
        vs = [ps[c][:, 512:768] for c in cs_]
        a_s = [_mm_nt(_stack_heads(qk[c][0], mh), qk[c][1]) for c in cs_]
        upd = [_mm_tn(vs[c], qk[c][1] * kw_ref[...]) for c in cs_]
        o_s = [_mm(a_s[c] * ds_ref[...], vs[c]) for c in cs_]
        st = st_sc[...]
        inter = []
        for c in cs_:
            st_ref[0, c] = st
            inter.append(_mm_nt(qk[c][0] * qw_ref[...], st))
            st = st * a_ref[...] + jnp.where(bd, upd[c], 0.0)
        st_sc[...] = st
        for c in cs_:
            r = _fold_heads(o_s[c], mv, L) + inter[c]
            raw_ref[0, rows[c], :] = r
            rstd = lax.rsqrt(_head_mean(r * r, mv, 64.0) + EPS)
            out_ref[0, rows[c], :] = (r * rstd * _silu(ps[c][:, 768:1024])).astype(_MXU)

    tok = lambda w: pl.BlockSpec((1, G * L, w), lambda b, n: (b, n, 0))
    return pl.pallas_call(
        body, name="ret_fwd", grid=(B, NG),
        in_specs=[tok(RET_W), tok(128), tok(128), _full((4 * L, L)), _full((L, 256)), _full((L, 256)),
                  _full((1, 256))],
        out_specs=[tok(256), tok(256), pl.BlockSpec((1, G, 256, 256), lambda b, n: (b, n, 0, 0))],
        out_shape=[jax.ShapeDtypeStruct((B, S, 256), _MXU), jax.ShapeDtypeStruct((B, S, 256), F32),
                   jax.ShapeDtypeStruct((B, NB, 256, 256), F32)],
        scratch_shapes=[pltpu.VMEM((256, 256), F32)],
        compiler_params=_cp(("parallel", "arbitrary")),
    )(ret_p, cos, sin, *consts)


def _ret_bwd(ret_p, cos, sin, raw, states, d_mix):
    B, S, _ = ret_p.shape
    L = min(RET_L, S)
    NB = S // L
    G = 1
    NG = NB // G
    consts = _ret_consts(L)

    def body(p_ref, c_ref, s_ref, raw_ref, st_ref, dm_ref, ds_ref, qw_ref, kw_ref, a_ref, dp_ref, dst_sc):
        @pl.when(pl.program_id(1) == 0)
        def _():
            dst_sc[...] = jnp.zeros_like(dst_sc)

        mh, mv, bd = _ret_masks()
        qw, kw, dec = qw_ref[...], kw_ref[...], ds_ref[...]
        cs_ = range(G)
        rows = [slice(c * L, (c + 1) * L) for c in cs_]
        ps = [p_ref[0, rows[c], :].astype(F32) for c in cs_]
        tabs = [(c_ref[0, rows[c], :], s_ref[0, rows[c], :]) for c in cs_]
        qk = [_ret_rope(ps[c], *tabs[c]) for c in cs_]
        vs = [ps[c][:, 512:768] for c in cs_]
        qs = [_stack_heads(qk[c][0], mh) for c in cs_]
        a_s = [_mm_nt(qs[c], qk[c][1]) for c in cs_]
        dr, dz = [], []
        for c in cs_:
            r = raw_ref[0, rows[c], :]
            z = ps[c][:, 768:1024]
            rstd = lax.rsqrt(_head_mean(r * r, mv, 64.0) + EPS)
            rn = r * rstd
            dm = dm_ref[0, rows[c], :]
            d_rn = dm * _silu(z)
            dz.append(dm * rn * _dsilu(z))
            dr.append(rstd * (d_rn - rn * _head_mean(d_rn * rn, mv, 64.0)))
        do_s = [_stack_heads(dr[c], mv) for c in cs_]
        da_s = [_mm_nt(do_s[c], vs[c]) for c in cs_]
        sts = [st_ref[0, c] for c in cs_]
        dq_st = [_mm(dr[c], sts[c]) for c in cs_]
        dst_in = [_mm_tn(dr[c], qk[c][0] * qw) for c in cs_]
        dv = [_mm_tn(a_s[c] * dec, do_s[c]) for c in cs_]
        dqr, dkr = [], []
        for c in cs_:
            da = da_s[c] * dec
            dqr.append(_fold_heads(_mm(da, qk[c][1]), mh, L) + dq_st[c] * qw)
            dkr.append(_mm_tn(da, qs[c]))
        dst_next = dst_sc[...]
        for c in reversed(cs_):
            g = jnp.where(bd, dst_next, 0.0)
            dv[c] = dv[c] + _mm_nt(qk[c][1] * kw, g)
            dkr[c] = dkr[c] + _mm(vs[c], g) * kw
            dst_next = dst_next * a_ref[...] + jnp.where(bd, dst_in[c], 0.0)
        dst_sc[...] = dst_next
        for c in cs_:
            cs, sn = tabs[c]
            dk = dkr[c] * RET_KSCALE
            dq1, dq2 = dqr[c][:, 0:128], dqr[c][:, 128:256]
            dk1, dk2 = dk[:, 0:128], dk[:, 128:256]
            dp_ref[0, rows[c], :] = jnp.concatenate(
                [dq1 * cs + dq2 * sn, dq2 * cs - dq1 * sn, dk1 * cs + dk2 * sn, dk2 * cs - dk1 * sn, dv[c], dz[c]],
                axis=1).astype(_MXU)

    tok = lambda w: pl.BlockSpec((1, G * L, w), lambda b, i: (b, NG - 1 - i, 0))
    return pl.pallas_call(
        body, name="ret_bwd", grid=(B, NG),
        in_specs=[tok(RET_W), tok(128), tok(128), tok(256),
                  pl.BlockSpec((1, G, 256, 256), lambda b, i: (b, NG - 1 - i, 0, 0)), tok(256),
                  _full((4 * L, L)), _full((L, 256)), _full((L, 256)), _full((1, 256))],
        out_specs=tok(RET_W), out_shape=jax.ShapeDtypeStruct((B, S, RET_W), _MXU),
        scratch_shapes=[pltpu.VMEM((256, 256), F32)],
        compiler_params=_cp(("parallel", "arbitrary")),
    )(ret_p, cos, sin, raw, states, d_mix, *consts)


def _gla_masks():
    C = CHUNK
    lk = lax.broadcasted_iota(jnp.int32, (1, 128), 1)
    lv = lax.broadcasted_iota(jnp.int32, (1, 256), 1)
    mk = [(lk // 32) == h for h in range(4)]
    mv = [(lv // 64) == h for h in range(4)]
    vi = lax.broadcasted_iota(jnp.int32, (256, 128), 0)
    ki = lax.broadcasted_iota(jnp.int32, (256, 128), 1)
    bd = (vi // 64) == (ki // 32)
    ri = lax.broadcasted_iota(jnp.int32, (4 * C, C), 0) % C
    cj = lax.broadcasted_iota(jnp.int32, (4 * C, C), 1)
    lower = ri >= cj
    ti = lax.broadcasted_iota(jnp.int32, (C, C), 0)
    tj = lax.broadcasted_iota(jnp.int32, (C, C), 1)
    ltri = jnp.where(ti >= tj, 1.0, 0.0).astype(F32)
    utri = jnp.where(ti <= tj, 1.0, 0.0).astype(F32)
    return mk, mv, bd, lower, ltri, utri


def _log_sigmoid(x):
    return jnp.minimum(x, 0.0) - jnp.log(1.0 + jnp.exp(-jnp.abs(x)))


GLA_G = 8


def _gla_fwd(gla_p, w_g2p, b_g2, gnw):
    B, S, _ = gla_p.shape
    C = CHUNK
    NC = S // C
    G = min(GLA_G, NC)
    NG = NC // G

    def body(p_ref, w_ref, b_ref, gn_ref, out_ref, raw_ref, st_ref, st_sc):
        @pl.when(pl.program_id(1) == 0)
        def _():
            st_sc[...] = jnp.zeros_like(st_sc)

        mk, mv, bd, lower, ltri, _ = _gla_masks()
        cs = range(G)
        rows = [slice(c * C, (c + 1) * C) for c in cs]
        ps = [p_ref[0, rows[c], :].astype(F32) for c in cs]
        pre = [_mm(ps[c][:, 512:640], w_ref[...]) + b_ref[...] for c in cs]
        cum = [_mm_f32(ltri, _log_sigmoid(pre[c]) * (1.0 / GLA_TAU)) for c in cs]
        past, fut, upd, q_pos, a_row = [], [], [], [], []
        for c in cs:
            q = ps[c][:, 0:128]
            k = ps[c][:, 128:256] * GLA_KSCALE
            last = cum[c][C - 1:C, :]
            e_pos = jnp.exp(cum[c])
            e_neg = jnp.exp(-cum[c])
            q_pos.append(q * e_pos)
            a_row.append(jnp.exp(last))
            past.append(_mm_nt(_stack_heads(q_pos[c], mk), k * e_neg))
            fut.append(_mm_nt(_stack_heads(q * e_neg, mk), k * e_pos))
            upd.append(_mm_tn(ps[c][:, 256:512], k * jnp.exp(last - cum[c])))
        o_s = [_mm(jnp.where(lower, past[c], fut[c]), ps[c][:, 256:512]) for c in cs]
        st = st_sc[...]
        inter = []
        for c in cs:
            st_ref[0, c] = st
            inter.append(_mm_nt(q_pos[c], st))
            st = st * a_row[c] + jnp.where(bd, upd[c], 0.0)
        st_sc[...] = st
        for c in cs:
            g = _fold_heads(o_s[c], mv, C) + inter[c]
            raw_ref[0, rows[c], :] = g
            rstd = lax.rsqrt(_head_mean(g * g, mv, 64.0) + EPS)
            out_ref[0, rows[c], :] = (g * rstd * gn_ref[...] * _silu(ps[c][:, 640:896])).astype(_MXU)

    tok = lambda w: pl.BlockSpec((1, G * C, w), lambda b, n: (b, n, 0))
    return pl.pallas_call(
        body, name="gla_fwd", grid=(B, NG),
        in_specs=[tok(GLA_W), _full((128, 128)), _full((1, 128)), _full((1, 256))],
        out_specs=[tok(256), tok(256), pl.BlockSpec((1, G, 256, 128), lambda b, n: (b, n, 0, 0))],
        out_shape=[jax.ShapeDtypeStruct((B, S, 256), _MXU), jax.ShapeDtypeStruct((B, S, 256), F32),
                   jax.ShapeDtypeStruct((B, NC, 256, 128), F32)],
        scratch_shapes=[pltpu.VMEM((256, 128), F32)],
        compiler_params=_cp(("parallel", "arbitrary")),
    )(gla_p, w_g2p, b_g2, gnw)


def _gla_bwd(gla_p, w_g2p, b_g2, gnw, raw, states, d_mix):
    B, S, _ = gla_p.shape
    C = CHUNK
    NC = S // C
    G = min(GLA_G, NC)
    NG = NC // G

    def body(p_ref, w_ref, b_ref, gn_ref, raw_ref, st_ref, dm_ref, dp_ref, dw_ref, db_ref, dgn_ref, dst_sc):
        first = (pl.program_id(0) == 0) & (pl.program_id(1) == 0)

        @pl.when(first)
        def _():
            dw_ref[...] = jnp.zeros_like(dw_ref)
            db_ref[...] = jnp.zeros_like(db_ref)
            dgn_ref[...] = jnp.zeros_like(dgn_ref)

        @pl.when(pl.program_id(1) == 0)
        def _():
            dst_sc[...] = jnp.zeros_like(dst_sc)

        mk, mv, bd, lower, ltri, utri = _gla_masks()
        gn = gn_ref[...]
        cs = range(G)
        rows = [slice(c * C, (c + 1) * C) for c in cs]
        ps = [p_ref[0, rows[c], :].astype(F32) for c in cs]
        vs = [ps[c][:, 256:512] for c in cs]
        pre = [_mm(ps[c][:, 512:640], w_ref[...]) + b_ref[...] for c in cs]
        cum = [_mm_f32(ltri, _log_sigmoid(pre[c]) * (1.0 / GLA_TAU)) for c in cs]
        dg, dz, dgn_acc = [], [], jnp.zeros((1, 256), F32)
        for c in cs:
            g = raw_ref[0, rows[c], :]
            z = ps[c][:, 640:896]
            rstd = lax.rsqrt(_head_mean(g * g, mv, 64.0) + EPS)
            gh = g * rstd
            dm = dm_ref[0, rows[c], :]
            d_gn = dm * _silu(z)
            dz.append(dm * gh * gn * _dsilu(z))
            dgn_acc = dgn_acc + jnp.sum(d_gn * gh, axis=0, keepdims=True)
            d_gh = d_gn * gn
            dg.append(rstd * (d_gh - gh * _head_mean(d_gh * gh, mv, 64.0)))
        do_s = [_stack_heads(dg[c], mv) for c in cs]
        dattn = [_mm_nt(do_s[c], vs[c]) for c in cs]
        ks, e_pos, e_neg, q_pos, q_neg, k_pos, k_neg, qp_s, qn_s, past, fut, a_row, w_dec, kd = ([] for _ in range(14))
        for c in cs:
            q = ps[c][:, 0:128]
            k = ps[c][:, 128:256] * GLA_KSCALE
            last = cum[c][C - 1:C, :]
            ep, en = jnp.exp(cum[c]), jnp.exp(-cum[c])
            ks.append(k), e_pos.append(ep), e_neg.append(en)
            q_pos.append(q * ep), q_neg.append(q * en), k_pos.append(k * ep), k_neg.append(k * en)
            qp_s.append(_stack_heads(q_pos[c], mk)), qn_s.append(_stack_heads(q_neg[c], mk))
            past.append(_mm_nt(qp_s[c], k_neg[c]))
            fut.append(_mm_nt(qn_s[c], k_pos[c]))
            a_row.append(jnp.exp(last))
            w_dec.append(jnp.exp(last - cum[c]))
            kd.append(k * w_dec[c])
        sts = [st_ref[0, c] for c in cs]
        dq_st = [_mm(dg[c], sts[c]) for c in cs]
        dst_in = [_mm_tn(dg[c], q_pos[c]) for c in cs]
        dv, dq_pos, dk_neg, dq_neg, dk_pos = [], [], [], [], []
        for c in cs:
            attn = jnp.where(lower, past[c], fut[c])
            dpast = jnp.where(lower, dattn[c], 0.0)
            dfut = jnp.where(lower, 0.0, dattn[c])
            dv.append(_mm_tn(attn, do_s[c]))
            dq_pos.append(_fold_heads(_mm(dpast, k_neg[c]), mk, C) + dq_st[c])
            dk_neg.append(_mm_tn(dpast, qp_s[c]))
            dq_neg.append(_fold_heads(_mm(dfut, k_pos[c]), mk, C))
            dk_pos.append(_mm_tn(dfut, qn_s[c]))
        dst_next = dst_sc[...]
        d_a, d_kd = [None] * G, [None] * G
        for c in reversed(cs):
            d_a[c] = jnp.sum(dst_next * sts[c], axis=0, keepdims=True)
            gmat = jnp.where(bd, dst_next, 0.0)
            d_kd[c] = _mm(vs[c], gmat)
            dv[c] = dv[c] + _mm_nt(kd[c], gmat)
            dst_next = dst_next * a_row[c] + jnp.where(bd, dst_in[c], 0.0)
        dst_sc[...] = dst_next
        row = lax.broadcasted_iota(jnp.int32, (C, 128), 0)
        d_la, dk, dq = [], [], []
        for c in cs:
            t = d_kd[c] * kd[c]
            dk.append(d_kd[c] * w_dec[c] + dk_neg[c] * e_neg[c] + dk_pos[c] * e_pos[c])
            dq.append(dq_pos[c] * e_pos[c] + dq_neg[c] * e_neg[c])
            d_last = jnp.sum(t, axis=0, keepdims=True) + d_a[c] * a_row[c]
            d_cum = (dq_pos[c] * q_pos[c] - dk_neg[c] * k_neg[c] - dq_neg[c] * q_neg[c] + dk_pos[c] * k_pos[c] - t)
            d_la.append(_mm_f32(utri, d_cum + jnp.where(row == C - 1, d_last, 0.0)))
        d_pre = [d_la[c] * _sig(-pre[c]) * (1.0 / GLA_TAU) for c in cs]
        d_gg = [_mm_nt(d_pre[c], w_ref[...]) for c in cs]
        dw_acc = _mm_tn(ps[0][:, 512:640], d_pre[0])
        db_acc = jnp.sum(d_pre[0], axis=0, keepdims=True)
        for c in cs[1:]:
            dw_acc = dw_acc + _mm_tn(ps[c][:, 512:640], d_pre[c])
            db_acc = db_acc + jnp.sum(d_pre[c], axis=0, keepdims=True)
        for c in cs:
            dp_ref[0, rows[c], :] = jnp.concatenate([dq[c], dk[c] * GLA_KSCALE, dv[c], d_gg[c], dz[c]],
                                                    axis=1).astype(_MXU)
        dw_ref[...] += dw_acc
        db_ref[...] += db_acc
        dgn_ref[...] += dgn_acc

        @pl.when((pl.program_id(0) == B - 1) & (pl.program_id(1) == NG - 1))
        def _():
            s1 = dgn_ref[...]
            s1 = s1 + pltpu.roll(s1, 128, 1)
            dgn_ref[...] = s1 + pltpu.roll(s1, 64, 1)

    tok = lambda w: pl.BlockSpec((1, G * C, w), lambda b, i: (b, NG - 1 - i, 0))
    return pl.pallas_call(
        body, name="gla_bwd", grid=(B, NG),
        in_specs=[tok(GLA_W), _full((128, 128)), _full((1, 128)), _full((1, 256)), tok(256),
                  pl.BlockSpec((1, G, 256, 128), lambda b, i: (b, NG - 1 - i, 0, 0)), tok(256)],
        out_specs=[tok(GLA_W), _full((128, 128)), _full((1, 128)), _full((1, 256))],
        out_shape=[jax.ShapeDtypeStruct((B, S, GLA_W), _MXU), jax.ShapeDtypeStruct((128, 128), F32),
                   jax.ShapeDtypeStruct((1, 128), F32), jax.ShapeDtypeStruct((1, 256), F32)],
        scratch_shapes=[pltpu.VMEM((256, 128), F32)],
        compiler_params=_cp(("arbitrary", "arbitrary")),
    )(gla_p, w_g2p, b_g2, gnw, raw, states, d_mix)


def _rms(x, w):
    rstd = lax.rsqrt(jnp.mean(x * x, axis=-1, keepdims=True) + EPS)
    xh = x * rstd
    return xh, rstd, xh * w


def _rms_bwd(dy, xh, rstd, w):
    dxh = dy * w
    return rstd * (dxh - xh * jnp.mean(dxh * xh, axis=-1, keepdims=True))


MLA_T = 256


def _mla_prep_fwd(mla_p, cos, sin, qnw, kvnw, w_uq, w_ukv):
    B, S, _ = mla_p.shape
    tm = min(S, 512)

    t = min(MLA_T, S)
    nt = tm // t

    def body(p_ref, c_ref, s_ref, qn_ref, kn_ref, wq_ref, wkv_ref, wkvt_ref, q_ref, k_ref, v_ref, kt_ref, vt_ref):
        p = p_ref[0].astype(F32)
        cs, sn = c_ref[0], s_ref[0]
        _, _, qn = _rms(p[:, 0:256], qn_ref[...])
        qpre = _mm(qn, wq_ref[...])
        _, _, kvn = _rms(p[:, 256:384], kn_ref[...])
        kv = _mm(kvn, wkv_ref[...])
        kvt = _mm_nt(wkvt_ref[...], kvn)
        kpe = _rope128(p[:, 384:512], cs, sn)
        kpet = kpe.T
        for h in range(8):
            sl = slice(128 * h, 128 * h + 128)
            q_ref[0, :, sl] = _rope128(qpre[:, sl], cs, sn).astype(_MXU)
            k_ref[0, :, sl] = (kv[:, sl] + kpe).astype(_MXU)
            kht = kvt[sl, :] + kpet
            for n in range(nt):
                kt_ref[0, n, sl, :] = kht[:, n * t:(n + 1) * t].astype(_MXU)
        v_ref[0] = kv[:, 1024:1536].astype(_MXU)
        for n in range(nt):
            vt_ref[0, n] = kvt[1024:1536, n * t:(n + 1) * t].astype(_MXU)

    tok = lambda w: pl.BlockSpec((1, tm, w), lambda b, i: (b, i, 0))
    tr = lambda w: pl.BlockSpec((1, nt, w, t), lambda b, i: (b, i, 0, 0))
    return pl.pallas_call(
        body, name="mla_prep_fwd", grid=(B, S // tm),
        in_specs=[tok(512), tok(128), tok(128), _full((1, 256)), _full((1, 128)), _full((256, 1024)),
                  _full((128, 1536)), _full((1536, 128))],
        out_specs=[tok(1024), tok(1024), tok(512), tr(1024), tr(512)],
        out_shape=[jax.ShapeDtypeStruct((B, S, 1024), _MXU), jax.ShapeDtypeStruct((B, S, 1024), _MXU),
                   jax.ShapeDtypeStruct((B, S, 512), _MXU), jax.ShapeDtypeStruct((B, S // t, 1024, t), _MXU),
                   jax.ShapeDtypeStruct((B, S // t, 512, t), _MXU)],
        compiler_params=_cp(("parallel", "parallel")),
    )(mla_p, cos, sin, qnw, kvnw, w_uq, w_ukv, w_ukv.T)


def _chunk_mask_t(t):
    kj = lax.broadcasted_iota(jnp.int32, (t, t), 0) // CHUNK
    qi = lax.broadcasted_iota(jnp.int32, (t, t), 1) // CHUNK
    return kj <= qi


MLA_HG = 8
MLA_HG_FWD = 8
LOG2E = 1.4426950408889634
MLA_C2 = MLA_SCALE * LOG2E


def _mla_attn_fwd(q, k, vt):
    B, S, _ = q.shape
    t = min(MLA_T, S)
    nq = S // t
    HG = MLA_HG_FWD
    NP = HG // 2

    def body(q_ref, k_ref, vt_ref, o_ref, lse_ref, sa, sb, m_sc, l_sc, acc_sc):
        i = pl.program_id(2)
        row = lax.broadcasted_iota(jnp.int32, (128, 1), 0)
        low = row < 64
        mask = _chunk_mask_t(t)
        m_sc[...] = jnp.full(m_sc.shape, -jnp.inf, F32)
        l_sc[...] = jnp.zeros_like(l_sc)
        acc_sc[...] = jnp.zeros_like(acc_sc)

        ones = jnp.ones((8, t), _MXU)

        def scores(j, buf):
            kb = k_ref[0, pl.ds(pl.multiple_of(j * t, t), t), :]
            for h in range(HG):
                cols = slice(128 * h, 128 * h + 128)
                buf[h] = (_mm_nt(kb[:, cols], q_ref[0, :, cols]) * MLA_C2).astype(_MXU)

        def absorb(j, buf, masked):
            vtb = vt_ref[0, j]
            for pr in range(NP):
                alphas, pvs = [], []
                for hh in range(2):
                    h = 2 * pr + hh
                    s = buf[h]
                    if masked:
                        s = jnp.where(mask, s, jnp.full_like(s, -jnp.inf))
                    m_old = m_sc[h]
                    m_new = jnp.maximum(m_old, jnp.max(s, axis=0, keepdims=True).astype(F32))
                    alpha = jnp.exp2(m_old - m_new)
                    p = jnp.exp2(s - m_new.astype(_MXU))
                    l_sc[h] = alpha * l_sc[h] + _mm(ones, p)[0:1, :]
                    m_sc[h] = m_new
                    vth = vtb[128 * pr:128 * pr + 128, :]
                    vth = jnp.where(low if hh == 0 else ~low, vth, jnp.zeros_like(vth))
                    pvs.append(_mm(vth, p))
                    alphas.append(alpha)
                acc_sc[pr] = acc_sc[pr] * jnp.where(low, alphas[0], alphas[1]) + pvs[0] + pvs[1]

        scores(0, sb)

        def pair(jj, carry):
            j0 = 2 * jj
            scores(j0 + 1, sa)
            absorb(j0, sb, False)
            scores(j0 + 2, sb)
            absorb(j0 + 1, sa, False)
            return carry

        lax.fori_loop(0, i // 2, pair, 0)

        @pl.when(i % 2 == 1)
        def _():
            scores(i, sa)
            absorb(i - 1, sb, False)
            absorb(i, sa, True)

        @pl.when(i % 2 == 0)
        def _():
            absorb(i, sb, True)

        for pr in range(NP):
            l_e, l_o = l_sc[2 * pr], l_sc[2 * pr + 1]
            o_ref[0, :, 128 * pr:128 * pr + 128] = (acc_sc[pr] / jnp.where(low, l_e, l_o)).T
            lse_ref[0, pr, 0, 0:1, :] = m_sc[2 * pr] + jnp.log(l_e) * LOG2E
            lse_ref[0, pr, 0, 1:2, :] = m_sc[2 * pr + 1] + jnp.log(l_o) * LOG2E

    return pl.pallas_call(
        body, name="mla_attn_fwd", grid=(B, 8 // HG, nq),
        in_specs=[pl.BlockSpec((1, t, 128 * HG), lambda b, g, i: (b, i, g)),
                  pl.BlockSpec((1, S, 128 * HG), lambda b, g, i: (b, 0, g)),
                  pl.BlockSpec((1, nq, 64 * HG, t), lambda b, g, i: (b, 0, g, 0))],
        out_specs=[pl.BlockSpec((1, t, 64 * HG), lambda b, g, i: (b, i, g)),
                   pl.BlockSpec((1, NP, 1, 2, t), lambda b, g, i: (b, g, i, 0, 0))],
        out_shape=[jax.ShapeDtypeStruct((B, S, 512), F32), jax.ShapeDtypeStruct((B, 4, nq, 2, t), F32)],
        scratch_shapes=[pltpu.VMEM((HG, t, t), _MXU), pltpu.VMEM((HG, t, t), _MXU), pltpu.VMEM((HG, 1, t), F32),
                        pltpu.VMEM((HG, 1, t), F32), pltpu.VMEM((NP, 128, t), F32)],
        compiler_params=_cp(("parallel", "parallel", "arbitrary")),
    )(q, k, vt)


def _mla_attn_bwd(q, k, v, kt, do, lse, dl):
    B, S, _ = q.shape
    t = min(MLA_T, S)
    nk = S // t

    HG = MLA_HG
    NP = HG // 2

    def body(q_ref, k_ref, v_ref, kt_ref, do_ref, lse_ref, dl_ref, dq_ref, dk_ref, dv_ref,
             sa, da, sb, db, dqt_sc, dk_sc, dv_sc):
        j = pl.program_id(2)

        @pl.when(j == 0)
        def _():
            dqt_sc[...] = jnp.zeros_like(dqt_sc)

        dk_sc[...] = jnp.zeros_like(dk_sc)
        dv_sc[...] = jnp.zeros_like(dv_sc)
        lane = lax.broadcasted_iota(jnp.int32, (1, 128), 1)
        low = lane < 64
        mask = _chunk_mask_t(t)

        def half(x, hh):
            return jnp.where(low if hh == 0 else ~low, x, jnp.zeros_like(x))

        def prepare(i, sbuf, dbuf):
            rows = pl.ds(pl.multiple_of(i * t, t), t)
            for h in range(HG):
                cols = slice(128 * h, 128 * h + 128)
                pc = slice(128 * (h // 2), 128 * (h // 2) + 128)
                sbuf[h] = _mm_nt(k_ref[0, :, cols], q_ref[0, rows, cols]) * MLA_C2
                dbuf[h] = _mm_nt(half(v_ref[0, :, pc], h % 2), do_ref[0, rows, pc])

        def absorb(i, sbuf, dbuf, masked):
            rows = pl.ds(pl.multiple_of(i * t, t), t)
            for h in range(HG):
                pr, hh = h // 2, h % 2
                cols = slice(128 * h, 128 * h + 128)
                pc = slice(128 * pr, 128 * pr + 128)
                p = jnp.exp2(sbuf[h] - lse_ref[0, pr, i][hh:hh + 1, :])
                if masked:
                    p = jnp.where(mask, p, 0.0)
                dv_sc[pr] += _mm(p, half(do_ref[0, rows, pc], hh))
                ds = p * (dbuf[h] - dl_ref[0, pr, i][hh:hh + 1, :])
                dqt_sc[i, cols, :] += _mm(kt_ref[0, 0, cols, :], ds)
                dk_sc[h] += _mm(ds, q_ref[0, rows, cols])

        n = nk - 1 - j
        prepare(jnp.minimum(j + 1, nk - 1), sb, db)

        def pair(jj, carry):
            i0 = j + 1 + 2 * jj
            prepare(i0 + 1, sa, da)
            absorb(i0, sb, db, False)
            prepare(jnp.where(i0 + 2 <= nk - 1, i0 + 2, j), sb, db)
            absorb(i0 + 1, sa, da, False)
            return carry

        lax.fori_loop(0, n // 2, pair, 0)

        @pl.when(n % 2 == 1)
        def _():
            prepare(j, sa, da)
            absorb(nk - 1, sb, db, False)
            absorb(j, sa, da, True)

        @pl.when(n % 2 == 0)
        def _():
            absorb(j, sb, db, True)

        for h in range(HG):
            dk_ref[0, :, 128 * h:128 * h + 128] = (dk_sc[h] * MLA_SCALE).astype(_MXU)
        for pr in range(NP):
            dv_ref[0, :, 128 * pr:128 * pr + 128] = dv_sc[pr].astype(_MXU)

        @pl.when(j == nk - 1)
        def _():
            for i in range(nk):
                dq_ref[0, i * t:(i + 1) * t, :] = (dqt_sc[i].T * MLA_SCALE).astype(_MXU)

    seq = lambda w: pl.BlockSpec((1, S, w), lambda b, g, j: (b, 0, g))
    blk = lambda w: pl.BlockSpec((1, t, w), lambda b, g, j: (b, j, g))
    stat = pl.BlockSpec((1, NP, nk, 2, t), lambda b, g, j: (b, g, 0, 0, 0))
    return pl.pallas_call(
        body, name="mla_attn_bwd", grid=(B, 8 // HG, nk),
        in_specs=[seq(128 * HG), blk(128 * HG), blk(64 * HG),
                  pl.BlockSpec((1, 1, 128 * HG, t), lambda b, g, j: (b, j, g, 0)), seq(64 * HG), stat, stat],
        out_specs=[seq(128 * HG), blk(128 * HG), blk(64 * HG)],
        out_shape=[jax.ShapeDtypeStruct((B, S, 1024), _MXU), jax.ShapeDtypeStruct((B, S, 1024), _MXU),
                   jax.ShapeDtypeStruct((B, S, 512), _MXU)],
        scratch_shapes=[pltpu.VMEM((HG, t, t), F32), pltpu.VMEM((HG, t, t), F32), pltpu.VMEM((HG, t, t), F32),
                        pltpu.VMEM((HG, t, t), F32), pltpu.VMEM((nk, 128 * HG, t), F32),
                        pltpu.VMEM((HG, t, 128), F32), pltpu.VMEM((NP, t, 128), F32)],
        compiler_params=_cp(("parallel", "parallel", "arbitrary"), 56),
    )(q, k, v, kt, do, lse, dl)


def _mla_prep_bwd(mla_p, cos, sin, qnw, kvnw, w_uq, w_ukv, dq, dk, dv):
    B, S, _ = mla_p.shape
    tm = min(S, 512)

    def body(p_ref, c_ref, s_ref, qn_ref, kn_ref, wq_ref, wkv_ref, dq_ref, dk_ref, dv_ref,
             dp_ref, dwq_ref, dwkv_ref, dqn_ref, dkn_ref):
        first = (pl.program_id(0) == 0) & (pl.program_id(1) == 0)

        @pl.when(first)
        def _():
            dwq_ref[...] = jnp.zeros_like(dwq_ref)
            dwkv_ref[...] = jnp.zeros_like(dwkv_ref)
            dqn_ref[...] = jnp.zeros_like(dqn_ref)
            dkn_ref[...] = jnp.zeros_like(dkn_ref)

        p = p_ref[0].astype(F32)
        cs, sn = c_ref[0], s_ref[0]
        lane = lax.broadcasted_iota(jnp.int32, (1, 128), 1)
        pe = (lane >= 64) & (lane < 96)
        qh, q_rstd, qn = _rms(p[:, 0:256], qn_ref[...])
        kvh, kv_rstd, kvn = _rms(p[:, 256:384], kn_ref[...])
        dqv = dq_ref[0].astype(F32)
        dkv = dk_ref[0].astype(F32)
        dqpre = jnp.concatenate(
            [_rope128_t(dqv[:, 128 * h:128 * h + 128], cs, sn) for h in range(8)], axis=1)
        dkpe = jnp.zeros((tm, 128), F32)
        for h in range(8):
            dkpe = dkpe + jnp.where(pe, dkv[:, 128 * h:128 * h + 128], 0.0)
        dkr = _rope128_t(dkpe, cs, sn)
        dkv_all = jnp.concatenate([dkv, dv_ref[0].astype(F32)], axis=1)
        d_qn = _mm_nt(dqpre, wq_ref[...])
        d_kvn = _mm_nt(dkv_all, wkv_ref[...])
        dwq_ref[...] += _mm_tn(qn, dqpre)
        dwkv_ref[...] += _mm_tn(kvn, dkv_all)
        dqn_ref[...] += jnp.sum(d_qn * qh, axis=0, keepdims=True)
        dkn_ref[...] += jnp.sum(d_kvn * kvh, axis=0, keepdims=True)
        dp_ref[0] = jnp.concatenate([_rms_bwd(d_qn, qh, q_rstd, qn_ref[...]),
                                     _rms_bwd(d_kvn, kvh, kv_rstd, kn_ref[...]), dkr], axis=1).astype(_MXU)

    tok = lambda w: pl.BlockSpec((1, tm, w), lambda b, i: (b, i, 0))
    return pl.pallas_call(
        body, name="mla_prep_bwd", grid=(B, S // tm),
        in_specs=[tok(512), tok(128), tok(128), _full((1, 256)), _full((1, 128)), _full((256, 1024)),
                  _full((128, 1536)), tok(1024), tok(1024), tok(512)],
        out_specs=[tok(512), _full((256, 1024)), _full((128, 1536)), _full((1, 256)), _full((1, 128))],
        out_shape=[jax.ShapeDtypeStruct((B, S, 512), _MXU), jax.ShapeDtypeStruct((256, 1024), F32),
                   jax.ShapeDtypeStruct((128, 1536), F32), jax.ShapeDtypeStruct((1, 256), F32),
                   jax.ShapeDtypeStruct((1, 128), F32)],
        compiler_params=_cp(("arbitrary", "arbitrary")),
    )(mla_p, cos, sin, qnw, kvnw, w_uq, w_ukv, dq, dk, dv)


def _out_fwd(x, gate, r_g, o_mla, mla_p, g_g, w_out):
    B, S, D = x.shape
    tm = min(S, 512)

    def body(x_ref, g_ref, r_ref, o_ref, z_ref, gg_ref, w_ref, xn_ref, y_ref, mm_ref):
        mm = (o_ref[0] * _silu(z_ref[0].astype(F32))).astype(_MXU)
        mm_ref[0] = mm
        y = (jnp.dot(r_ref[0], w_ref[0:256, :], preferred_element_type=F32)
             + jnp.dot(mm, w_ref[256:768, :], preferred_element_type=F32)
             + jnp.dot(gg_ref[0], w_ref[768:1024, :], preferred_element_type=F32))
        y_ref[0] = y.astype(_MXU)
        xn_ref[0] = x_ref[0] + g_ref[0] * y

    tok = lambda w, c=0: pl.BlockSpec((1, tm, w), lambda b, i: (b, i, c))
    return pl.pallas_call(
        body, name="out_fwd", grid=(B, S // tm),
        in_specs=[tok(D), pl.BlockSpec((1, 1, D), lambda b, i: (b, 0, 0)), tok(256), tok(512), tok(512, 1),
                  tok(256), _full((D, D))],
        out_specs=[tok(D), tok(D), tok(512)],
        out_shape=[jax.ShapeDtypeStruct((B, S, D), F32), jax.ShapeDtypeStruct((B, S, D), _MXU),
                   jax.ShapeDtypeStruct((B, S, 512), _MXU)],
        compiler_params=_cp(("parallel", "parallel")),
    )(x, gate, r_g, o_mla, mla_p, g_g, w_out)


def _out_bwd(dx, y, gate, r_g, mm, g_g, w_out, o_mla, mla_p):
    B, S, D = dx.shape
    tm = min(S, 512)
    t = min(MLA_T, S)
    nt = tm // t

    def body(dx_ref, y_ref, g_ref, r_ref, mm_ref, gg_ref, w_ref, o_ref, z_ref,
             dr_ref, do_ref, dz_ref, dl_ref, dg_ref, dw_ref, dgate_ref, acc):
        first = (pl.program_id(0) == 0) & (pl.program_id(1) == 0)

        @pl.when(first)
        def _():
            acc[...] = jnp.zeros_like(acc)

        @pl.when(pl.program_id(1) == 0)
        def _():
            dgate_ref[...] = jnp.zeros_like(dgate_ref)

        dxv = dx_ref[0]
        dgate_ref[0] += jnp.sum(dxv * y_ref[0].astype(F32), axis=0, keepdims=True)
        dy = (dxv * g_ref[0]).astype(_MXU)
        dr_ref[0] = _mm_nt(dy, w_ref[0:256, :])
        dg_ref[0] = _mm_nt(dy, w_ref[768:1024, :])
        acc[0:256, :] += _mm_tn(r_ref[0], dy)
        acc[256:768, :] += _mm_tn(mm_ref[0], dy)
        acc[768:1024, :] += _mm_tn(gg_ref[0], dy)

        @pl.when((pl.program_id(0) == B - 1) & (pl.program_id(1) == S // tm - 1))
        def _():
            dw_ref[...] = acc[...].astype(_MXU)

        dm = _mm_nt(dy, w_ref[256:768, :])
        ov, z = o_ref[0], z_ref[0].astype(F32)
        do = dm * _silu(z)
        dz_ref[0] = (dm * ov * _dsilu(z)).astype(_MXU)
        do_ref[0] = do.astype(_MXU)
        prod = do * ov
        for pr in range(4):
            pt = prod[:, 128 * pr:128 * pr + 128].T
            se = jnp.sum(pt[0:64], axis=0, keepdims=True)
            so = jnp.sum(pt[64:128], axis=0, keepdims=True)
            for n in range(nt):
                dl_ref[0, pr, n, 0:1, :] = se[:, n * t:(n + 1) * t]
                dl_ref[0, pr, n, 1:2, :] = so[:, n * t:(n + 1) * t]

    tok = lambda w, c=0: pl.BlockSpec((1, tm, w), lambda b, i: (b, i, c))
    per_seq = pl.BlockSpec((1, 1, D), lambda b, i: (b, 0, 0))
    return pl.pallas_call(
        body, name="out_bwd", grid=(B, S // tm),
        in_specs=[tok(D), tok(D), per_seq, tok(256), tok(512), tok(256), _full((D, D)), tok(512), tok(512, 1)],
        out_specs=[tok(256), tok(512), tok(512), pl.BlockSpec((1, 4, nt, 2, t), lambda b, i: (b, 0, i, 0, 0)),
                   tok(256), _full((D, D)), per_seq],
        out_shape=[jax.ShapeDtypeStruct((B, S, 256), F32), jax.ShapeDtypeStruct((B, S, 512), _MXU),
                   jax.ShapeDtypeStruct((B, S, 512), _MXU), jax.ShapeDtypeStruct((B, 4, S // t, 2, t), F32),
                   jax.ShapeDtypeStruct((B, S, 256), F32), jax.ShapeDtypeStruct((D, D), _MXU),
                   jax.ShapeDtypeStruct((B, 1, D), F32)],
        scratch_shapes=[pltpu.VMEM((D, D), F32)],
        compiler_params=_cp(("arbitrary", "arbitrary")),
    )(dx, y, gate, r_g, mm, g_g, w_out, o_mla, mla_p)


def _proj_bwd_x(x, shift, scale, nw, w_arr, d_ret, d_mla, d_mz, d_gla, dx_out):
    B, S, D = x.shape
    tm = min(S, 512)

    def body(x_ref, sc_ref, nw_ref, w_ref, dr_ref, dm_ref, dz_ref, dg_ref, dxo_ref,
             dx_ref, dsh_ref, dsc_ref, dnw_ref):
        first = (pl.program_id(0) == 0) & (pl.program_id(1) == 0)

        @pl.when(first)
        def _():
            dnw_ref[...] = jnp.zeros_like(dnw_ref)

        @pl.when(pl.program_id(1) == 0)
        def _():
            dsh_ref[...] = jnp.zeros_like(dsh_ref)
            dsc_ref[...] = jnp.zeros_like(dsc_ref)

        dp = jnp.concatenate([dr_ref[0], dm_ref[0], dz_ref[0], dg_ref[0]], axis=1)
        dh = lax.dot_general(dp, w_ref[...], (((1,), (1,)), ((), ())), preferred_element_type=F32)
        xv = x_ref[0]
        rstd = lax.rsqrt(jnp.mean(xv * xv, axis=-1, keepdims=True) + EPS)
        xh = xv * rstd
        nwv = nw_ref[...]
        mod = 1.0 + sc_ref[0]
        dsh_ref[0] += jnp.sum(dh, axis=0, keepdims=True)
        dsc_ref[0] += jnp.sum(dh * xh * nwv, axis=0, keepdims=True)
        dnw_ref[...] += jnp.sum(dh * xh * mod, axis=0, keepdims=True)
        dxh = dh * nwv * mod
        dx_ref[0] = dxo_ref[0] + rstd * (dxh - xh * jnp.mean(dxh * xh, axis=-1, keepdims=True))

    tok = lambda w: pl.BlockSpec((1, tm, w), lambda b, i: (b, i, 0))
    per_seq = pl.BlockSpec((1, 1, D), lambda b, i: (b, 0, 0))
    return pl.pallas_call(
        body, name="proj_bwd_x", grid=(B, S // tm),
        in_specs=[tok(D), per_seq, _full((1, D)), _full((D, ARR_W)), tok(RET_W), tok(512), tok(512),
                  tok(GLA_W), tok(D)],
        out_specs=[tok(D), per_seq, per_seq, _full((1, D))],
        out_shape=[jax.ShapeDtypeStruct((B, S, D), F32), jax.ShapeDtypeStruct((B, 1, D), F32),
                   jax.ShapeDtypeStruct((B, 1, D), F32), jax.ShapeDtypeStruct((1, D), F32)],
        compiler_params=_cp(("arbitrary", "arbitrary")),
    )(x, scale, nw, w_arr, d_ret, d_mla, d_mz, d_gla, dx_out)


def _proj_bwd_w(h, d_ret, d_mla, d_mz, d_gla):
    B, S, D = h.shape
    tm = min(S, 512)

    def body(h_ref, dr_ref, dm_ref, dz_ref, dg_ref, dw_ref, acc):
        first = (pl.program_id(0) == 0) & (pl.program_id(1) == 0)

        @pl.when(first)
        def _():
            acc[...] = jnp.zeros_like(acc)

        hv = h_ref[0]
        tn = lambda d_ref: lax.dot_general(hv, d_ref[0], (((0,), (0,)), ((), ())), preferred_element_type=F32)
        acc[:, 0:RET_W] += tn(dr_ref)
        acc[:, RET_W:RET_W + 512] += tn(dm_ref)
        acc[:, RET_W + 512:RET_W + MLA_W] += tn(dz_ref)
        acc[:, RET_W + MLA_W:ARR_W] += tn(dg_ref)

        @pl.when((pl.program_id(0) == B - 1) & (pl.program_id(1) == S // tm - 1))
        def _():
            dw_ref[...] = acc[...].astype(_MXU)

    tok = lambda w: pl.BlockSpec((1, tm, w), lambda b, i: (b, i, 0))
    return pl.pallas_call(
        body, name="proj_bwd_w", grid=(B, S // tm),
        in_specs=[tok(D), tok(RET_W), tok(512), tok(512), tok(GLA_W)],
        out_specs=_full((D, ARR_W)), out_shape=jax.ShapeDtypeStruct((D, ARR_W), _MXU),
        scratch_shapes=[pltpu.VMEM((D, ARR_W), F32)],
        compiler_params=_cp(("arbitrary", "arbitrary"), 56),
    )(h, d_ret, d_mla, d_mz, d_gla)


def _out_fwd_loss(x, gate, r_g, o_mla, mla_p, g_g, w_out, fw, target):
    B, S, D = x.shape
    tm = min(S, 512)

    def body(x_ref, g_ref, r_ref, o_ref, z_ref, gg_ref, w_ref, fw_ref, t_ref, dx_ref, y_ref, mm_ref, loss_ref, dfw_ref):
        first = (pl.program_id(0) == 0) & (pl.program_id(1) == 0)

        @pl.when(first)
        def _():
            loss_ref[...] = jnp.zeros_like(loss_ref)
            dfw_ref[...] = jnp.zeros_like(dfw_ref)

        mm = (o_ref[0] * _silu(z_ref[0].astype(F32))).astype(_MXU)
        mm_ref[0] = mm
        y = (jnp.dot(r_ref[0], w_ref[0:256, :], preferred_element_type=F32)
             + jnp.dot(mm, w_ref[256:768, :], preferred_element_type=F32)
             + jnp.dot(gg_ref[0], w_ref[768:1024, :], preferred_element_type=F32))
        y_ref[0] = y.astype(_MXU)
        xv = x_ref[0] + g_ref[0] * y
        fwv = fw_ref[...]
        rstd = lax.rsqrt(jnp.mean(xv * xv, axis=-1, keepdims=True) + EPS)
        xh = xv * rstd
        err = xh * fwv - t_ref[0]
        loss_ref[...] += 0.5 * jnp.sum(jnp.mean(err * err, axis=-1, keepdims=True), axis=0, keepdims=True)
        dy = err * (1.0 / D)
        dfw_ref[...] += jnp.sum(dy * xh, axis=0, keepdims=True)
        dxh = dy * fwv
        dx_ref[0] = rstd * (dxh - xh * jnp.mean(dxh * xh, axis=-1, keepdims=True))

    tok = lambda w, c=0: pl.BlockSpec((1, tm, w), lambda b, i: (b, i, c))
    return pl.pallas_call(
        body, name="out_fwd_loss", grid=(B, S // tm),
        in_specs=[tok(D), pl.BlockSpec((1, 1, D), lambda b, i: (b, 0, 0)), tok(256), tok(512), tok(512, 1),
                  tok(256), _full((D, D)), _full((1, D)), tok(D)],
        out_specs=[tok(D), tok(D), tok(512), _full((1, 1)), _full((1, D))],
        out_shape=[jax.ShapeDtypeStruct((B, S, D), F32), jax.ShapeDtypeStruct((B, S, D), _MXU),
                   jax.ShapeDtypeStruct((B, S, 512), _MXU), jax.ShapeDtypeStruct((1, 1), F32),
                   jax.ShapeDtypeStruct((1, D), F32)],
        compiler_params=_cp(("arbitrary", "arbitrary")),
    )(x, gate, r_g, o_mla, mla_p, g_g, w_out, fw, target)


def _local_step(x, pos3, mod, loss_target, small, w_in_a, w_uq_a, w_ukv_a, w_out_b):
    B, S, D = x.shape
    tabs = _rope_tables(pos3)
    saved = []
    for l in range(DEPTH):
        last = (small["final_norm"].reshape(1, D), loss_target) if l == DEPTH - 1 else None
        x, s = _layer_fwd(x, tabs, mod[l], {n: a[l] for n, a in small.items() if n != "final_norm"},
                          w_in_a[l], w_uq_a[l], w_ukv_a[l], w_out_b[l], loss_head=last)
        saved.append(s)
    dx, loss, d_fw = x
    grads = dict(final_norm=d_fw.reshape(D))
    per_layer = [None] * DEPTH
    for l in reversed(range(DEPTH)):
        dx, per_layer[l] = _layer_bwd(dx, saved[l], tabs)
    for name in per_layer[0]:
        grads[name] = jnp.stack([per_layer[l][name] for l in range(DEPTH)])
    return loss, dx, grads


def _layer_fwd(x, tabs, mod_l, small_l, w_in_a, w_uq_a=None, w_ukv_a=None, w_out_b=None, late_weights=None,
               loss_head=None):
    B, S, D = x.shape
    cr, sr, cm, sm = tabs
    shift = mod_l[:, 0:D].reshape(B, 1, D)
    scale = mod_l[:, D:2 * D].reshape(B, 1, D)
    gate = mod_l[:, 2 * D:3 * D].reshape(B, 1, D)
    nw = small_l["norm_w"].reshape(1, D)
    qnw = small_l["mla_q_norm"].reshape(1, 256)
    kvnw = small_l["mla_kv_norm"].reshape(1, 128)
    w_g2p = jnp.pad(small_l["gla_w_g2"], ((0, 112), (0, 0)))
    b_g2 = small_l["gla_b_g2"].reshape(1, 128)
    gnw = jnp.tile(small_l["gla_norm"], 4).reshape(1, 256)
    ret_p, mla_p, gla_p, h = _proj_fwd(x, shift, scale, nw, w_in_a)
    r_g, r_raw, r_st = _ret_fwd(ret_p, cr, sr)
    if late_weights is not None:
        w_uq_a, w_ukv_a, w_out_b = late_weights(r_raw)
    q, k, v, kt, vt = _mla_prep_fwd(mla_p, cm, sm, qnw, kvnw, w_uq_a, w_ukv_a)
    o_mla, lse = _mla_attn_fwd(q, k, vt)
    g_g, g_raw, g_st = _gla_fwd(gla_p, w_g2p, b_g2, gnw)
    if loss_head is None:
        x_new, y, mm = _out_fwd(x, gate, r_g, o_mla, mla_p, g_g, w_out_b)
    else:
        dx, y, mm, loss, d_fw = _out_fwd_loss(x, gate, r_g, o_mla, mla_p, g_g, w_out_b, *loss_head)
        x_new = (dx, loss, d_fw)
    saved = dict(x=x, shift=shift, scale=scale, gate=gate, nw=nw, qnw=qnw, kvnw=kvnw, w_g2p=w_g2p, b_g2=b_g2,
                 gnw=gnw, ret_p=ret_p, mla_p=mla_p, gla_p=gla_p, h=h, r_g=r_g, r_raw=r_raw, r_st=r_st, q=q, k=k,
                 v=v, kt=kt, o_mla=o_mla, lse=lse, g_g=g_g, g_raw=g_raw, g_st=g_st, y=y, mm=mm,
                 w_in_a=w_in_a, w_uq_a=w_uq_a, w_ukv_a=w_ukv_a, w_out_b=w_out_b)
    return x_new, saved


def _layer_bwd(dx, s, tabs, early_grads=None):
    B, S, D = dx.shape
    cr, sr, cm, sm = tabs
    d_r, do, d_mz, dl, d_g, dw_out, d_gate = _out_bwd(dx, s["y"], s["gate"], s["r_g"], s["mm"], s["g_g"], s["w_out_b"],
                                                      s["o_mla"], s["mla_p"])
    d_ret = _ret_bwd(s["ret_p"], cr, sr, s["r_raw"], s["r_st"], d_r)
    dq, dk, dv = _mla_attn_bwd(s["q"], s["k"], s["v"], s["kt"], do, s["lse"], dl)
    d_mla, dw_uq, dw_ukv, d_qnw, d_kvnw = _mla_prep_bwd(
        s["mla_p"], cm, sm, s["qnw"], s["kvnw"], s["w_uq_a"], s["w_ukv_a"], dq, dk, dv)
    gnw = s["gnw"] if early_grads is None else s["gnw"] + early_grads(dw_out, dw_uq, dw_ukv)
    d_gla, dw_g2p, db_g2, d_gnw = _gla_bwd(s["gla_p"], s["w_g2p"], s["b_g2"], gnw, s["g_raw"], s["g_st"], d_g)
    dx, d_shift, d_scale, d_nw = _proj_bwd_x(s["x"], s["shift"], s["scale"], s["nw"], s["w_in_a"],
                                             d_ret, d_mla, d_mz, d_gla, dx)
    dw_in = _proj_bwd_w(s["h"], d_ret, d_mla, d_mz, d_gla)
    grads = dict(
        d_mod=jnp.concatenate([d_shift, d_scale, d_gate], axis=2).reshape(B, 3 * D),
        norm_w=d_nw.reshape(D), mla_q_norm=d_qnw.reshape(256), mla_kv_norm=d_kvnw.reshape(128),
        gla_w_g2=dw_g2p[0:16], gla_b_g2=db_g2.reshape(128), gla_norm256=d_gnw.reshape(256),
        w_in_a=dw_in, w_uq_a=dw_uq, w_ukv_a=dw_ukv, w_out=dw_out)
    return dx, grads


def _exchange(arrs, gather, name):
    n = len(arrs)
    out_shape = [jax.ShapeDtypeStruct(((N_DEV,) + a.shape) if g else a.shape, a.dtype)
                 for a, g in zip(arrs, gather)]

    def body(*refs):
        ins, outs = refs[:n], refs[n:2 * n]
        send_sems, recv_sems, local_sems = refs[2 * n:]
        ix, iy, ic = lax.axis_index("x"), lax.axis_index("y"), lax.axis_index("c")
        me = 4 * ix + 2 * iy + ic
        copies = []
        for a in range(n):
            mine = ins[a] if gather[a] else ins[a].at[me]
            loc = pltpu.make_async_copy(mine, outs[a].at[me], local_sems.at[a])
            loc.start()
            copies.append(loc)
            for d in range(1, N_DEV):
                px = 1 - ix if d & 4 else ix
                py = 1 - iy if d & 2 else iy
                pc = 1 - ic if d & 1 else ic
                src = ins[a] if gather[a] else ins[a].at[4 * px + 2 * py + pc]
                cp = pltpu.make_async_remote_copy(
                    src_ref=src, dst_ref=outs[a].at[me], send_sem=send_sems.at[a, d - 1],
                    recv_sem=recv_sems.at[a, d - 1], device_id=(px, py, pc), device_id_type=pl.DeviceIdType.MESH)
                cp.start()
                copies.append(cp)
        for cp in copies:
            cp.wait()

    any_spec = pl.BlockSpec(memory_space=pl.ANY)
    outs = pl.pallas_call(
        body, name=name, in_specs=[any_spec] * n, out_specs=[any_spec] * n, out_shape=out_shape,
        scratch_shapes=[pltpu.SemaphoreType.DMA((n, N_DEV - 1)), pltpu.SemaphoreType.DMA((n, N_DEV - 1)),
                        pltpu.SemaphoreType.DMA((n,))],
    )(*arrs)
    return list(outs)


def _peers(ix, iy, ic):
    out = []
    for d in range(1, N_DEV):
        px = 1 - ix if d & 4 else ix
        py = 1 - iy if d & 2 else iy
        pc = 1 - ic if d & 1 else ic
        out.append((d - 1, (px, py, pc), 4 * px + 2 * py + pc))
    return out


def _exchange_start(arrs, gather, name, after=None):
    n = len(arrs)
    lands = [lax.empty(((N_DEV,) + a.shape) if g else a.shape, a.dtype) for a, g in zip(arrs, gather)]
    extra = [] if after is None else [after]

    def body(*refs):
        ins, land_refs = refs[:n], refs[n:2 * n]
        send_sems, recv_sems = refs[2 * n + len(extra)], refs[2 * n + len(extra) + 1]
        token = refs[-1]
        ix, iy, ic = lax.axis_index("x"), lax.axis_index("y"), lax.axis_index("c")
        me = 4 * ix + 2 * iy + ic
        for a in range(n):
            for k, peer, peer_idx in _peers(ix, iy, ic):
                pltpu.make_async_remote_copy(
                    src_ref=ins[a] if gather[a] else ins[a].at[peer_idx], dst_ref=land_refs[a].at[me],
                    send_sem=send_sems.at[7 * a + k], recv_sem=recv_sems.at[7 * a + k], device_id=peer,
                    device_id_type=pl.DeviceIdType.MESH).start()
        token[...] = jnp.zeros_like(token)

    hbm = pl.BlockSpec(memory_space=pltpu.HBM)
    sem = pl.BlockSpec(memory_space=pltpu.SEMAPHORE)
    held = [pltpu.with_memory_space_constraint(a, pltpu.HBM) for a in list(arrs) + lands]
    outs = pl.pallas_call(
        body, name=name,
        out_shape=(pltpu.SemaphoreType.DMA((7 * n,)), pltpu.SemaphoreType.DMA((7 * n,)),
                   *[pltpu.HBM(a.shape, a.dtype) for a in held], jax.ShapeDtypeStruct((8, 128), F32)),
        in_specs=[hbm] * (2 * n) + [pl.BlockSpec(memory_space=pl.ANY)] * len(extra),
        out_specs=(sem, sem, *[hbm] * (2 * n), pl.BlockSpec(memory_space=pltpu.VMEM)),
        input_output_aliases={a: 2 + a for a in range(2 * n)},
        compiler_params=pltpu.CompilerParams(has_side_effects=pltpu.SideEffectType.DATAFLOW_SIDE_EFFECTING),
    )(*held, *extra)
    return dict(send=outs[0], recv=outs[1], srcs=list(outs[2:2 + n]), lands=list(outs[2 + n:2 + 2 * n]),
                token=outs[-1], gather=list(gather))


def _exchange_wait(flight, after, me, name):
    n = len(flight["srcs"])
    gather = flight["gather"]

    def body(*refs):
        srcs, land_refs = refs[:n], refs[n:2 * n]
        send_sems, recv_sems = refs[2 * n], refs[2 * n + 1]
        ix, iy, ic = lax.axis_index("x"), lax.axis_index("y"), lax.axis_index("c")
        mine = 4 * ix + 2 * iy + ic
        for a in range(n):
            for k, peer, peer_idx in _peers(ix, iy, ic):
                cp = pltpu.make_async_remote_copy(
                    src_ref=srcs[a] if gather[a] else srcs[a].at[peer_idx], dst_ref=land_refs[a].at[mine],
                    send_sem=send_sems.at[7 * a + k], recv_sem=recv_sems.at[7 * a + k], device_id=peer,
                    device_id_type=pl.DeviceIdType.MESH)
                cp.wait_send()
                cp.wait_recv()

    hbm = pl.BlockSpec(memory_space=pltpu.HBM)
    sem = pl.BlockSpec(memory_space=pltpu.SEMAPHORE)
    held = flight["srcs"] + flight["lands"]
    outs = pl.pallas_call(
        body, name=name, out_shape=tuple(pltpu.HBM(a.shape, a.dtype) for a in held),
        in_specs=[hbm] * (2 * n) + [sem, sem, pl.BlockSpec(memory_space=pl.ANY)], out_specs=tuple([hbm] * (2 * n)),
        input_output_aliases={a: a for a in range(2 * n)},
        compiler_params=pltpu.CompilerParams(has_side_effects=pltpu.SideEffectType.DATAFLOW_SIDE_EFFECTING),
    )(*held, flight["send"], flight["recv"], after)
    got = []
    for a in range(n):
        src, land = outs[a], outs[n + a]
        own = src if gather[a] else lax.dynamic_index_in_dim(src, me, axis=0, keepdims=False)
        got.append(lax.dynamic_update_index_in_dim(land, own, me, axis=0))
    return got


def _ada_fwd(c_all, ada_w, ada_b_cols):
    nb, D = c_all.shape
    cols = ada_w.shape[2]

    def body(c_ref, w_ref, b_ref, out_ref):
        ca = _silu(c_ref[...])
        for l in range(DEPTH):
            out_ref[l] = _mm(ca, w_ref[l]) + b_ref[l:l + 1, :]

    return pl.pallas_call(
        body, name="ada_fwd", out_shape=jax.ShapeDtypeStruct((DEPTH, nb, cols), F32),
        in_specs=[pl.BlockSpec(memory_space=pltpu.VMEM)] * 3, out_specs=pl.BlockSpec(memory_space=pltpu.VMEM),
        compiler_params=pltpu.CompilerParams(vmem_limit_bytes=32 * VMEM_MB),
    )(c_all, ada_w, ada_b_cols)


def _ada_bwd(c_all, d_mod_cols):
    nb, D = c_all.shape
    cols = d_mod_cols.shape[2]

    def body(c_ref, dm_ref, out_ref):
        ca = _silu(c_ref[...])
        for l in range(DEPTH):
            out_ref[l] = _mm_tn(ca, dm_ref[l])

    return pl.pallas_call(
        body, name="ada_bwd", out_shape=jax.ShapeDtypeStruct((DEPTH, D, cols), F32),
        in_specs=[pl.BlockSpec(memory_space=pltpu.VMEM)] * 2, out_specs=pl.BlockSpec(memory_space=pltpu.VMEM),
        compiler_params=pltpu.CompilerParams(vmem_limit_bytes=32 * VMEM_MB),
    )(c_all, d_mod_cols)


def _sum_adamw(parts, w, m, v, name):
    P, R, C = parts.shape
    tr = 256 if (R % 256 == 0 and R > 256) else R

    def body(p_ref, w_ref, m_ref, v_ref, g_ref, d_ref, nm_ref, nv_ref):
        g = p_ref[0].astype(F32)
        for k in range(1, P):
            g = g + p_ref[k].astype(F32)
        g_ref[...] = g
        nm = ADAM_B1 * m_ref[...] + (1.0 - ADAM_B1) * g
        nv = ADAM_B2 * v_ref[...] + (1.0 - ADAM_B2) * (g * g)
        nm_ref[...] = nm
        nv_ref[...] = nv
        m_hat = nm / (1.0 - ADAM_B1 ** ADAM_STEP)
        v_hat = nv / (1.0 - ADAM_B2 ** ADAM_STEP)
        d_ref[...] = -ADAM_LR * (m_hat / (jnp.sqrt(v_hat) + ADAM_EPS) + ADAM_WD * w_ref[...])

    blk = pl.BlockSpec((tr, C), lambda i: (i, 0))
    shp = jax.ShapeDtypeStruct((R, C), F32)
    return pl.pallas_call(
        body, name=name, grid=(R // tr,),
        in_specs=[pl.BlockSpec((P, tr, C), lambda i: (0, i, 0)), blk, blk, blk],
        out_specs=[blk, blk, blk, blk], out_shape=[shp, shp, shp, shp],
        compiler_params=_cp(("parallel",)),
    )(parts, w, m, v)


def _sum_adamw_layer(parts, w, m, v, layer, name, prev=None, after=None):
    P, R, C = parts.shape
    tr = 256 if (R % 256 == 0 and R > 256) else R

    def body(p_ref, w_ref, m_ref, v_ref, *rest):
        g_ref, d_ref, nm_ref, nv_ref = rest[-4:]
        g = p_ref[0].astype(F32)
        for k in range(1, P):
            g = g + p_ref[k].astype(F32)
        g_ref[0] = g
        nm = ADAM_B1 * m_ref[0] + (1.0 - ADAM_B1) * g
        nv = ADAM_B2 * v_ref[0] + (1.0 - ADAM_B2) * (g * g)
        nm_ref[0] = nm
        nv_ref[0] = nv
        m_hat = nm / (1.0 - ADAM_B1 ** ADAM_STEP)
        v_hat = nv / (1.0 - ADAM_B2 ** ADAM_STEP)
        d_ref[0] = -ADAM_LR * (m_hat / (jnp.sqrt(v_hat) + ADAM_EPS) + ADAM_WD * w_ref[0])

    blk = pl.BlockSpec((1, tr, C), lambda i: (layer, i, 0))
    shp = jax.ShapeDtypeStruct(w.shape, F32)
    in_specs = [pl.BlockSpec((P, tr, C), lambda i: (0, i, 0)), blk, blk, blk]
    args = [parts, w, m, v]
    aliases = {}
    if prev is not None:
        in_specs += [pl.BlockSpec(memory_space=pl.ANY)] * 4
        args += list(prev)
        aliases = {4 + k: k for k in range(4)}
    if after is not None:
        in_specs.append(pl.BlockSpec(memory_space=pl.ANY))
        args.append(after)
    return list(pl.pallas_call(
        body, name=name, grid=(R // tr,), in_specs=in_specs, out_specs=[blk] * 4, out_shape=[shp] * 4,
        input_output_aliases=aliases, compiler_params=_cp(("parallel",)),
    )(*args))


SMALL = ["norm_w", "mla_q_norm", "mla_kv_norm", "gla_w_g2", "gla_b_g2", "gla_norm", "final_norm"]


SMALL_ROWS = 72


def _pack_small(loss, part):
    flat = [jnp.pad(loss.reshape(1), (0, 127))] + [part[n].reshape(-1) for n in SMALL]
    used = sum(f.shape[0] for f in flat)
    flat.append(jnp.zeros((SMALL_ROWS * 128 - used,), F32))
    return jnp.concatenate(flat).reshape(SMALL_ROWS, 128)


def _small_adamw(packed_parts, w, m, v):
    n = len(w)

    def body(*refs):
        p_ref = refs[0]
        w_refs, m_refs, v_refs = refs[1:1 + n], refs[1 + n:1 + 2 * n], refs[1 + 2 * n:1 + 3 * n]
        outs, acc = refs[1 + 3 * n:-1], refs[-1]
        total = p_ref[0]
        for k in range(1, N_DEV):
            total = total + p_ref[k]
        acc[...] = total
        outs[0][...] = acc[0:1, :]
        r0 = 1
        for i in range(n):
            shp = w_refs[i].shape
            if len(shp) == 3:
                g = acc[r0:r0 + shp[0] * shp[1], :].reshape(shp)
                r0 += shp[0] * shp[1]
            elif shp[1] < 128:
                g = acc[r0:r0 + shp[0], 0:shp[1]]
                r0 += shp[0]
            else:
                k = shp[1] // 128
                g = jnp.concatenate(
                    [jnp.concatenate([acc[r0 + l * k + j:r0 + l * k + j + 1, :] for j in range(k)], axis=1)
                     for l in range(shp[0])], axis=0)
                r0 += shp[0] * k
            nm = ADAM_B1 * m_refs[i][...] + (1.0 - ADAM_B1) * g
            nv = ADAM_B2 * v_refs[i][...] + (1.0 - ADAM_B2) * (g * g)
            m_hat = nm / (1.0 - ADAM_B1 ** ADAM_STEP)
            v_hat = nv / (1.0 - ADAM_B2 ** ADAM_STEP)
            outs[1 + 4 * i][...] = g
            outs[2 + 4 * i][...] = -ADAM_LR * (m_hat / (jnp.sqrt(v_hat) + ADAM_EPS) + ADAM_WD * w_refs[i][...])
            outs[3 + 4 * i][...] = nm
            outs[4 + 4 * i][...] = nv

    vmem = pl.BlockSpec(memory_space=pltpu.VMEM)
    out_shape = [jax.ShapeDtypeStruct((1, 128), F32)]
    for a in w:
        out_shape += [jax.ShapeDtypeStruct(a.shape, F32)] * 4
    outs = pl.pallas_call(
        body, name="adamw_small", in_specs=[vmem] * (1 + 3 * n), out_specs=[vmem] * (1 + 4 * n), out_shape=out_shape,
        scratch_shapes=[pltpu.VMEM((SMALL_ROWS, 128), F32)],
    )(packed_parts, *w, *m, *v)
    return outs[0], [outs[1 + 4 * i:5 + 4 * i] for i in range(n)]


WEIGHTS = ["norm_w", "ada_w", "ada_b", "w_in", "mla_q_norm", "w_uq", "mla_kv_norm", "w_ukv", "gla_w_g2",
           "gla_b_g2", "gla_norm", "w_out", "final_norm"]


def kernel(x, c, positions, norm_w, ada_w, ada_b, w_in, mla_q_norm, w_uq, mla_kv_norm, w_ukv, gla_w_g2, gla_b_g2, gla_norm, w_out, final_norm, loss_target, m_norm_w, m_ada_w, m_ada_b, m_w_in, m_mla_q_norm, m_w_uq, m_mla_kv_norm, m_w_ukv, m_gla_w_g2, m_gla_b_g2, m_gla_norm, m_w_out, m_final_norm, v_norm_w, v_ada_w, v_ada_b, v_w_in, v_mla_q_norm, v_w_uq, v_mla_kv_norm, v_w_ukv, v_gla_w_g2, v_gla_b_g2, v_gla_norm, v_w_out, v_final_norm):
    w = dict(norm_w=norm_w, ada_w=ada_w, ada_b=ada_b, w_in=w_in, mla_q_norm=mla_q_norm, w_uq=w_uq,
             mla_kv_norm=mla_kv_norm, w_ukv=w_ukv, gla_w_g2=gla_w_g2, gla_b_g2=gla_b_g2, gla_norm=gla_norm,
             w_out=w_out, final_norm=final_norm)
    m = dict(norm_w=m_norm_w, ada_w=m_ada_w, ada_b=m_ada_b, w_in=m_w_in, mla_q_norm=m_mla_q_norm, w_uq=m_w_uq,
             mla_kv_norm=m_mla_kv_norm, w_ukv=m_w_ukv, gla_w_g2=m_gla_w_g2, gla_b_g2=m_gla_b_g2,
             gla_norm=m_gla_norm, w_out=m_w_out, final_norm=m_final_norm)
    v = dict(norm_w=v_norm_w, ada_w=v_ada_w, ada_b=v_ada_b, w_in=v_w_in, mla_q_norm=v_mla_q_norm, w_uq=v_w_uq,
             mla_kv_norm=v_mla_kv_norm, w_ukv=v_w_ukv, gla_w_g2=v_gla_w_g2, gla_b_g2=v_gla_b_g2,
             gla_norm=v_gla_norm, w_out=v_w_out, final_norm=v_final_norm)
    B, S, D = x.shape
    me = 4 * lax.axis_index("x") + 2 * lax.axis_index("y") + lax.axis_index("c")
    ada_cols = ada_w.shape[2]
    cast = lambda a: a.astype(_MXU)

    sharded = ["w_in", "w_uq", "w_ukv", "w_out"]

    whole_cols = lambda a: jnp.transpose(a, (1, 0, 2)).reshape(a.shape[1], -1)
    whole_in = lambda blk: _arrange_w_in(whole_cols(blk))
    whole_rest = lambda blks: (_arrange_w_uq(whole_cols(blks[0])), _arrange_w_ukv(whole_cols(blks[1])),
                               blks[2].reshape(D, D))
    col_blocks = lambda a: jnp.transpose(a.reshape(a.shape[0], N_DEV, -1), (1, 0, 2)).astype(jnp.bfloat16)
    blocks_in = lambda dw_in_a: col_blocks(_unarrange_w_in(dw_in_a))
    blocks_rest = lambda dw_out, dw_uq_a, dw_ukv_a: [
        col_blocks(_unarrange_w_uq(dw_uq_a)), col_blocks(_unarrange_w_ukv(dw_ukv_a)),
        dw_out.reshape(N_DEV, D // N_DEV, D).astype(jnp.bfloat16)]

    (c_g,) = _exchange([c], [True], "gather_c")
    c_all = c_g.reshape(N_DEV * B, D)

    ada_b_cols = lax.dynamic_slice(ada_b, (0, me * ada_cols), (DEPTH, ada_cols))
    mod_cols = _ada_fwd(c_all, ada_w, ada_b_cols)
    mod_send = jnp.transpose(mod_cols.reshape(DEPTH, N_DEV, B, ada_cols), (1, 0, 2, 3))
    (mod_recv,) = _exchange([mod_send], [False], "scatter_mod")
    mod = jnp.transpose(mod_recv, (1, 2, 0, 3)).reshape(DEPTH, B, 3 * D)

    flight_i = _exchange_start([cast(w_in[0])], [True], "gather_start_first", after=mod)
    flight_r = _exchange_start([cast(w[n][0]) for n in sharded[1:]], [True] * 3, "gather_start_layer0",
                               after=flight_i["token"])
    flight_w = _exchange_start([cast(w[n][1]) for n in sharded], [True] * 4, "gather_start_layer1",
                               after=flight_r["token"])
    small_w = {n: w[n] for n in SMALL}
    layer_small = lambda l: {n: a[l] for n, a in small_w.items() if n != "final_norm"}
    tabs = _rope_tables(positions.reshape(B, S, 1), flight_w["token"][0, 0])
    late0 = lambda after: whole_rest(_exchange_wait(flight_r, after, me, "gather_wait_layer0"))
    (w_in0_g,) = _exchange_wait(flight_i, tabs[0], me, "gather_wait_first")
    x1, saved0 = _layer_fwd(x, tabs, mod[0], layer_small(0), whole_in(w_in0_g), late_weights=late0)
    got1 = _exchange_wait(flight_w, x1, me, "gather_wait_layer1")
    (dx, loss, d_fw), saved1 = _layer_fwd(x1, tabs, mod[1], layer_small(1), whole_in(got1[0]), *whole_rest(got1[1:]),
                                          loss_head=(final_norm.reshape(1, D), loss_target))

    dx, g1 = _layer_bwd(dx, saved1, tabs)
    flight_g = _exchange_start([blocks_in(g1["w_in_a"])] + blocks_rest(g1["w_out"], g1["w_uq_a"], g1["w_ukv_a"]),
                               [False] * 4, "grads_start_layer1")
    flights = {}

    def early0(dw_out, dw_uq_a, dw_ukv_a):
        flights["rest0"] = _exchange_start(blocks_rest(dw_out, dw_uq_a, dw_ukv_a), [False] * 3, "grads_start_layer0")
        return flights["rest0"]["token"][0, 0]

    saved0 = dict(saved0, gate=saved0["gate"] + flight_g["token"][0, 0])
    grad_x, g0 = _layer_bwd(dx, saved0, tabs, early_grads=early0)
    parts1 = _exchange_wait(flight_g, grad_x, me, "grads_wait_layer1")
    rest0 = _exchange_wait(flights["rest0"], g0["w_in_a"], me, "grads_wait_layer0")

    both = lambda n: jnp.stack([g0[n], g1[n]])
    d_mod = both("d_mod")
    part = dict(norm_w=both("norm_w"), mla_q_norm=both("mla_q_norm"), mla_kv_norm=both("mla_kv_norm"),
                gla_w_g2=both("gla_w_g2"), gla_b_g2=both("gla_b_g2"), gla_norm=both("gla_norm256")[:, 0:128],
                final_norm=d_fw)
    flight_l = _exchange_start([d_mod, _pack_small(loss, part), blocks_in(g0["w_in_a"])], [True, True, False],
                               "exchange_start_last")
    res = {}
    behind = flight_l["token"]
    for a, name in enumerate(sharded):
        res[name] = _sum_adamw_layer(parts1[a], w[name], m[name], v[name], 1, "adamw_%s_layer1" % name, after=behind)
        behind = res[name][1]
    for a, name in enumerate(sharded[1:]):
        res[name] = _sum_adamw_layer(rest0[a], w[name], m[name], v[name], 0, "adamw_%s_layer0" % name,
                                     prev=res[name], after=behind)
        behind = res[name][1]
    d_mod_g, small_g, in0 = _exchange_wait(flight_l, behind, me, "exchange_wait_last")
    res["w_in"] = _sum_adamw_layer(in0, w_in, m_w_in, v_w_in, 0, "adamw_w_in_layer0", prev=res["w_in"])

    d_mod_all = jnp.transpose(d_mod_g, (1, 0, 2, 3)).reshape(DEPTH, N_DEV * B, 3 * D)
    d_mod_cols = lax.dynamic_slice(d_mod_all, (0, 0, me * ada_cols), (DEPTH, N_DEV * B, ada_cols))
    g_ada_w = _ada_bwd(c_all, d_mod_cols)

    def update(name, parts2d):
        shp = w[name].shape
        two = lambda a: a.reshape(parts2d.shape[1:])
        out = _sum_adamw(parts2d, two(w[name]), two(m[name]), two(v[name]), "adamw_" + name)
        res[name] = [o.reshape(shp) for o in out]

    update("ada_w", g_ada_w.reshape(1, DEPTH * D, ada_cols))
    update("ada_b", jnp.transpose(d_mod_g, (0, 2, 1, 3)).reshape(N_DEV * B, DEPTH * 3 * D // 128, 128))
    row = lambda a: a.reshape(1, D) if a.ndim == 1 else a
    loss_sum, small_out = _small_adamw(small_g, [row(w[n]) for n in SMALL], [row(m[n]) for n in SMALL],
                                       [row(v[n]) for n in SMALL])
    for n, outs in zip(SMALL, small_out):
        res[n] = [o.reshape(w[n].shape) for o in outs]
    loss_out = loss_sum[0, 0]
    return (loss_out, grad_x, *[res[n][0] for n in WEIGHTS], *[res[n][1] for n in WEIGHTS],
            *[res[n][2] for n in WEIGHTS], *[res[n][3] for n in WEIGHTS])
```

```python
import functools
import math

import numpy as np
import jax
import jax.numpy as jnp
from jax import lax
from jax.experimental import pallas as pl
from jax.experimental.pallas import tpu as pltpu

F32 = jnp.float32
_MXU = jnp.bfloat16

D_MODEL = 1024
DEPTH = 2
CHUNK = 64
EPS = 1e-6
ROPE_THETA = 10000.0
N_DEV = 8

MLA_SCALE = 96.0 ** -0.5
RET_KSCALE = 64.0 ** -0.5
GLA_KSCALE = 32.0 ** -0.5
GLA_TAU = 16.0

ADAM_LR = 0.001
ADAM_B1 = 0.9
ADAM_B2 = 0.999
ADAM_EPS = 1e-08
ADAM_WD = 0.01
ADAM_STEP = 10

RET_W, MLA_W, GLA_W = 1024, 1024, 896
ARR_W = RET_W + MLA_W + GLA_W
VMEM_MB = 1024 * 1024


def _cp(sem, vmem_mb=48):
    return pltpu.CompilerParams(dimension_semantics=sem, vmem_limit_bytes=vmem_mb * VMEM_MB)


def _mm(a, b):
    return jnp.dot(a.astype(_MXU), b.astype(_MXU), preferred_element_type=F32)


def _mm_nt(a, b):
    return lax.dot_general(a.astype(_MXU), b.astype(_MXU), (((1,), (1,)), ((), ())),
                           preferred_element_type=F32)


def _mm_tn(a, b):
    return lax.dot_general(a.astype(_MXU), b.astype(_MXU), (((0,), (0,)), ((), ())),
                           preferred_element_type=F32)


def _mm_f32(a, b):
    return jnp.dot(a, b, precision=lax.Precision.HIGHEST, preferred_element_type=F32)


def _sig(z):
    return 1.0 / (1.0 + jnp.exp(-z))


def _silu(z):
    return z * _sig(z)


def _dsilu(z):
    s = _sig(z)
    return s * (1.0 + z * (1.0 - s))


def _full(shape):
    nd = len(shape)
    return pl.BlockSpec(shape, lambda *_: (0,) * nd)


def _qk_perm(blk):
    r = blk.shape[0]
    return jnp.transpose(blk.reshape(r, 4, 2, 32), (0, 2, 1, 3)).reshape(r, 256)


def _qk_unperm(blk):
    r = blk.shape[0]
    return jnp.transpose(blk.reshape(r, 2, 4, 32), (0, 2, 1, 3)).reshape(r, 256)


def _arrange_w_in(w):
    z = lambda n: jnp.zeros((w.shape[0], n), w.dtype)
    ret = [_qk_perm(w[:, 0:256]), _qk_perm(w[:, 256:512]), w[:, 512:768], w[:, 768:1024]]
    mla = [w[:, 1024:1280], w[:, 1280:1408], z(64), w[:, 1408:1440], z(32), w[:, 1440:1952]]
    gla = [w[:, 1952:2080], w[:, 2080:2208], w[:, 2208:2464], w[:, 2464:2480], z(112), w[:, 2480:2736]]
    return jnp.concatenate(ret + mla + gla, axis=1)


def _unarrange_w_in(a):
    m, g = RET_W, RET_W + MLA_W
    parts = [_qk_unperm(a[:, 0:256]), _qk_unperm(a[:, 256:512]), a[:, 512:1024],
             a[:, m:m + 384], a[:, m + 448:m + 480], a[:, m + 512:m + 1024],
             a[:, g:g + 528], a[:, g + 640:g + 896]]
    return jnp.concatenate(parts, axis=1)


def _arrange_w_uq(w):
    return jnp.pad(w.reshape(256, 8, 96), ((0, 0), (0, 0), (0, 32))).reshape(256, 1024)


def _unarrange_w_uq(a):
    return a.reshape(256, 8, 128)[:, :, :96].reshape(256, 768)


def _arrange_w_ukv(w):
    r = w.reshape(128, 8, 128)
    k = jnp.pad(r[:, :, :64], ((0, 0), (0, 0), (0, 64))).reshape(128, 1024)
    return jnp.concatenate([k, r[:, :, 64:].reshape(128, 512)], axis=1)


def _unarrange_w_ukv(a):
    k = a[:, :1024].reshape(128, 8, 128)[:, :, :64]
    v = a[:, 1024:].reshape(128, 8, 64)
    return jnp.concatenate([k, v], axis=2).reshape(128, 1024)


def _rope_tables(pos3, zero=0.0):
    B, S, _ = pos3.shape
    ts = min(S, 512)
    inv32 = (np.float32(ROPE_THETA) ** (-(np.arange(32, dtype=np.float32) / 32))).astype(np.float32)
    inv16 = (np.float32(ROPE_THETA) ** (-(np.arange(16, dtype=np.float32) / 16))).astype(np.float32)
    inv = np.zeros((1, 128), np.float32)
    inv[0, 0:32] = inv32
    inv[0, 32:48] = inv16

    def body(pos_ref, inv_ref, cr, sr, cm, sm):
        ang = pos_ref[0].astype(F32) * inv_ref[...]
        lane = lax.broadcasted_iota(jnp.int32, (1, 128), 1)

        def every_head(x):
            y = jnp.where(lane < 32, x, pltpu.roll(x, 32, 1))
            return jnp.where(lane < 64, y, pltpu.roll(y, 64, 1))

        def rotary_pair(x, fill):
            return jnp.where((lane >= 64) & (lane < 80), pltpu.roll(x, 32, 1),
                             jnp.where((lane >= 80) & (lane < 96), pltpu.roll(x, 48, 1), fill))

        c, s = jnp.cos(ang), jnp.sin(ang)
        cr[0] = every_head(c)
        sr[0] = every_head(s)
        cm[0] = rotary_pair(c, 1.0)
        sm[0] = rotary_pair(s, 0.0)

    tab = jax.ShapeDtypeStruct((B, S, 128), F32)
    blk = pl.BlockSpec((1, ts, 128), lambda b, i: (b, i, 0))
    return pl.pallas_call(
        body, name="rope_tables", grid=(B, S // ts),
        in_specs=[pl.BlockSpec((1, ts, 1), lambda b, i: (b, i, 0)), _full((1, 128))],
        out_specs=[blk, blk, blk, blk], out_shape=[tab, tab, tab, tab],
        compiler_params=_cp(("parallel", "parallel")),
    )(pos3, jnp.asarray(inv) + zero)


def _rope128(x, cos, sin):
    lane = lax.broadcasted_iota(jnp.int32, (1, 128), 1)
    rp = pltpu.roll(x, 16, 1)
    rm = pltpu.roll(x, 112, 1)
    return x * cos + jnp.where(lane < 80, -rm, rp) * sin


def _rope128_t(d, cos, sin):
    lane = lax.broadcasted_iota(jnp.int32, (1, 128), 1)
    y = d * sin
    yp = pltpu.roll(y, 16, 1)
    ym = pltpu.roll(y, 112, 1)
    return d * cos + jnp.where(lane < 64, 0.0, jnp.where(lane < 80, ym, jnp.where(lane < 96, -yp, 0.0)))


def _proj_fwd(x, shift, scale, nw, w_arr):
    B, S, D = x.shape
    tm = min(S, 512)

    def body(x_ref, sh_ref, sc_ref, nw_ref, w_ref, ret_ref, mla_ref, gla_ref, h_ref):
        xv = x_ref[0]
        rstd = lax.rsqrt(jnp.mean(xv * xv, axis=-1, keepdims=True) + EPS)
        h = (xv * rstd * nw_ref[...]) * (1.0 + sc_ref[0]) + sh_ref[0]
        hb = h.astype(_MXU)
        h_ref[0] = hb
        ret_ref[0] = jnp.dot(hb, w_ref[:, 0:RET_W], preferred_element_type=F32).astype(_MXU)
        mla_ref[0] = jnp.dot(hb, w_ref[:, RET_W:RET_W + MLA_W], preferred_element_type=F32).astype(_MXU)
        gla_ref[0] = jnp.dot(hb, w_ref[:, RET_W + MLA_W:ARR_W], preferred_element_type=F32).astype(_MXU)

    tok = lambda w: pl.BlockSpec((1, tm, w), lambda b, i: (b, i, 0))
    per_seq = pl.BlockSpec((1, 1, D), lambda b, i: (b, 0, 0))
    return pl.pallas_call(
        body, name="proj_fwd", grid=(B, S // tm),
        in_specs=[tok(D), per_seq, per_seq, _full((1, D)), _full((D, ARR_W))],
        out_specs=[tok(RET_W), tok(MLA_W), tok(GLA_W), tok(D)],
        out_shape=[jax.ShapeDtypeStruct((B, S, RET_W), _MXU), jax.ShapeDtypeStruct((B, S, MLA_W), _MXU),
                   jax.ShapeDtypeStruct((B, S, GLA_W), _MXU), jax.ShapeDtypeStruct((B, S, D), _MXU)],
        compiler_params=_cp(("parallel", "parallel")),
    )(x, shift, scale, nw, w_arr)


RET_L = 256


def _ret_consts(L):
    lg = np.log1p(-np.exp2(-5.0 - np.arange(4, dtype=np.float32))).astype(np.float32)
    i = np.arange(L)
    ci = i // CHUNK
    diff = (i[:, None] - i[None, :]).astype(np.float32)
    same = ci[:, None] == ci[None, :]
    past = ci[None, :] < ci[:, None]
    expo = np.where(same, np.abs(diff), np.where(past, diff, 0.0)).astype(np.float32)
    dec = np.where((same | past)[None], np.exp(lg[:, None, None] * expo[None]), 0.0).astype(np.float32)
    head = (np.arange(256) % 128) // 32
    qw = np.exp((i + 1.0)[:, None] * lg[head][None, :]).astype(np.float32)
    kw = np.exp((L - 1.0 - i)[:, None] * lg[head][None, :]).astype(np.float32)
    a_row = np.exp(np.float32(L) * lg[head])[None, :].astype(np.float32)
    return [jnp.asarray(t) for t in (dec.reshape(4 * L, L), qw, kw, a_row)]


def _ret_masks():
    lane = lax.broadcasted_iota(jnp.int32, (1, 256), 1)
    mh = [((lane % 128) // 32) == h for h in range(4)]
    mv = [(lane // 64) == h for h in range(4)]
    vi = lax.broadcasted_iota(jnp.int32, (256, 256), 0)
    ki = lax.broadcasted_iota(jnp.int32, (256, 256), 1)
    bd = (vi // 64) == ((ki % 128) // 32)
    return mh, mv, bd


def _ret_rope(p, cs, sn):
    q1, q2, k1, k2 = p[:, 0:128], p[:, 128:256], p[:, 256:384], p[:, 384:512]
    qr = jnp.concatenate([q1 * cs - q2 * sn, q2 * cs + q1 * sn], axis=1)
    kr = jnp.concatenate([k1 * cs - k2 * sn, k2 * cs + k1 * sn], axis=1) * RET_KSCALE
    return qr, kr


def _head_mean(x, mv, width):
    out = jnp.zeros_like(x)
    for m in mv:
        s = jnp.sum(jnp.where(m, x, 0.0), axis=-1, keepdims=True) * (1.0 / width)
        out = jnp.where(m, s, out)
    return out


def _stack_heads(x, masks):
    return jnp.concatenate([jnp.where(m, x, 0.0) for m in masks], axis=0)


def _fold_heads(xs, masks, L):
    out = jnp.where(masks[0], xs[0:L], 0.0)
    for h in range(1, 4):
        out = out + jnp.where(masks[h], xs[h * L:(h + 1) * L], 0.0)
    return out


RET_G = 2


def _ret_fwd(ret_p, cos, sin):
    B, S, _ = ret_p.shape
    L = min(RET_L, S)
    NB = S // L
    G = min(RET_G, NB)
    NG = NB // G
    consts = _ret_consts(L)

    def body(p_ref, c_ref, s_ref, ds_ref, qw_ref, kw_ref, a_ref, out_ref, raw_ref, st_ref, st_sc):
        @pl.when(pl.program_id(1) == 0)
        def _():
            st_sc[...] = jnp.zeros_like(st_sc)

        mh, mv, bd = _ret_masks()
        cs_ = range(G)
        rows = [slice(c * L, (c + 1) * L) for c in cs_]
        ps = [p_ref[0, rows[c], :].astype(F32) for c in cs_]
        qk = [_ret_rope(ps[c], c_ref[0, rows[c], :], s_ref[0, rows[c], :]) for c in cs_]
        vs = [ps[c][:, 512:768] for c in cs_]
        a_s = [_mm_nt(_stack_heads(qk[c][0], mh), qk[c][1]) for c in cs_]
        upd = [_mm_tn(vs[c], qk[c][1] * kw_ref[...]) for c in cs_]
        o_s = [_mm(a_s[c] * ds_ref[...], vs[c]) for c in cs_]
        st = st_sc[...]
        inter = []
        for c in cs_:
            st_ref[0, c] = st
            inter.append(_mm_nt(qk[c][0] * qw_ref[...], st))
            st = st * a_ref[...] + jnp.where(bd, upd[c], 0.0)
        st_sc[...] = st
        for c in cs_:
            r = _fold_heads(o_s[c], mv, L) + inter[c]
            raw_ref[0, rows[c], :] = r
            rstd = lax.rsqrt(_head_mean(r * r, mv, 64.0) + EPS)
            out_ref[0, rows[c], :] = (r * rstd * _silu(ps[c][:, 768:1024])).astype(_MXU)

    tok = lambda w: pl.BlockSpec((1, G * L, w), lambda b, n: (b, n, 0))
    return pl.pallas_call(
        body, name="ret_fwd", grid=(B, NG),
        in_specs=[tok(RET_W), tok(128), tok(128), _full((4 * L, L)), _full((L, 256)), _full((L, 256)),
                  _full((1, 256))],
        out_specs=[tok(256), tok(256), pl.BlockSpec((1, G, 256, 256), lambda b, n: (b, n, 0, 0))],
        out_shape=[jax.ShapeDtypeStruct((B, S, 256), _MXU), jax.ShapeDtypeStruct((B, S, 256), F32),
                   jax.ShapeDtypeStruct((B, NB, 256, 256), F32)],
        scratch_shapes=[pltpu.VMEM((256, 256), F32)],
        compiler_params=_cp(("parallel", "arbitrary")),
    )(ret_p, cos, sin, *consts)


def _ret_bwd(ret_p, cos, sin, raw, states, d_mix):
    B, S, _ = ret_p.shape
    L = min(RET_L, S)
    NB = S // L
    G = 1
    NG = NB // G
    consts = _ret_consts(L)

    def body(p_ref, c_ref, s_ref, raw_ref, st_ref, dm_ref, ds_ref, qw_ref, kw_ref, a_ref, dp_ref, dst_sc):
        @pl.when(pl.program_id(1) == 0)
        def _():
            dst_sc[...] = jnp.zeros_like(dst_sc)

        mh, mv, bd = _ret_masks()
        qw, kw, dec = qw_ref[...], kw_ref[...], ds_ref[...]
        cs_ = range(G)
        rows = [slice(c * L, (c + 1) * L) for c in cs_]
        ps = [p_ref[0, rows[c], :].astype(F32) for c in cs_]
        tabs = [(c_ref[0, rows[c], :], s_ref[0, rows[c], :]) for c in cs_]
        qk = [_ret_rope(ps[c], *tabs[c]) for c in cs_]
        vs = [ps[c][:, 512:768] for c in cs_]
        qs = [_stack_heads(qk[c][0], mh) for c in cs_]
        a_s = [_mm_nt(qs[c], qk[c][1]) for c in cs_]
        dr, dz = [], []
        for c in cs_:
            r = raw_ref[0, rows[c], :]
            z = ps[c][:, 768:1024]
            rstd = lax.rsqrt(_head_mean(r * r, mv, 64.0) + EPS)
            rn = r * rstd
            dm = dm_ref[0, rows[c], :]
            d_rn = dm * _silu(z)
            dz.append(dm * rn * _dsilu(z))
            dr.append(rstd * (d_rn - rn * _head_mean(d_rn * rn, mv, 64.0)))
        do_s = [_stack_heads(dr[c], mv) for c in cs_]
        da_s = [_mm_nt(do_s[c], vs[c]) for c in cs_]
        sts = [st_ref[0, c] for c in cs_]
        dq_st = [_mm(dr[c], sts[c]) for c in cs_]
        dst_in = [_mm_tn(dr[c], qk[c][0] * qw) for c in cs_]
        dv = [_mm_tn(a_s[c] * dec, do_s[c]) for c in cs_]
        dqr, dkr = [], []
        for c in cs_:
            da = da_s[c] * dec
            dqr.append(_fold_heads(_mm(da, qk[c][1]), mh, L) + dq_st[c] * qw)
            dkr.append(_mm_tn(da, qs[c]))
        dst_next = dst_sc[...]
        for c in reversed(cs_):
            g = jnp.where(bd, dst_next, 0.0)
            dv[c] = dv[c] + _mm_nt(qk[c][1] * kw, g)
            dkr[c] = dkr[c] + _mm(vs[c], g) * kw
            dst_next = dst_next * a_ref[...] + jnp.where(bd, dst_in[c], 0.0)
        dst_sc[...] = dst_next
        for c in cs_:
            cs, sn = tabs[c]
            dk = dkr[c] * RET_KSCALE
            dq1, dq2 = dqr[c][:, 0:128], dqr[c][:, 128:256]
            dk1, dk2 = dk[:, 0:128], dk[:, 128:256]
            dp_ref[0, rows[c], :] = jnp.concatenate(
                [dq1 * cs + dq2 * sn, dq2 * cs - dq1 * sn, dk1 * cs + dk2 * sn, dk2 * cs - dk1 * sn, dv[c], dz[c]],
                axis=1).astype(_MXU)

    tok = lambda w: pl.BlockSpec((1, G * L, w), lambda b, i: (b, NG - 1 - i, 0))
    return pl.pallas_call(
        body, name="ret_bwd", grid=(B, NG),
        in_specs=[tok(RET_W), tok(128), tok(128), tok(256),
                  pl.BlockSpec((1, G, 256, 256), lambda b, i: (b, NG - 1 - i, 0, 0)), tok(256),
                  _full((4 * L, L)), _full((L, 256)), _full((L, 256)), _full((1, 256))],
        out_specs=tok(RET_W), out_shape=jax.ShapeDtypeStruct((B, S, RET_W), _MXU),
        scratch_shapes=[pltpu.VMEM((256, 256), F32)],
        compiler_params=_cp(("parallel", "arbitrary")),
    )(ret_p, cos, sin, raw, states, d_mix, *consts)


def _gla_masks():
    C = CHUNK
    lk = lax.broadcasted_iota(jnp.int32, (1, 128), 1)
    lv = lax.broadcasted_iota(jnp.int32, (1, 256), 1)
    mk = [(lk // 32) == h for h in range(4)]
    mv = [(lv // 64) == h for h in range(4)]
    vi = lax.broadcasted_iota(jnp.int32, (256, 128), 0)
    ki = lax.broadcasted_iota(jnp.int32, (256, 128), 1)
    bd = (vi // 64) == (ki // 32)
    ri = lax.broadcasted_iota(jnp.int32, (4 * C, C), 0) % C
    cj = lax.broadcasted_iota(jnp.int32, (4 * C, C), 1)
    lower = ri >= cj
    ti = lax.broadcasted_iota(jnp.int32, (C, C), 0)
    tj = lax.broadcasted_iota(jnp.int32, (C, C), 1)
    ltri = jnp.where(ti >= tj, 1.0, 0.0).astype(F32)
    utri = jnp.where(ti <= tj, 1.0, 0.0).astype(F32)
    return mk, mv, bd, lower, ltri, utri


def _log_sigmoid(x):
    return jnp.minimum(x, 0.0) - jnp.log(1.0 + jnp.exp(-jnp.abs(x)))


GLA_G = 8


def _gla_fwd(gla_p, w_g2p, b_g2, gnw):
    B, S, _ = gla_p.shape
    C = CHUNK
    NC = S // C
    G = min(GLA_G, NC)
    NG = NC // G

    def body(p_ref, w_ref, b_ref, gn_ref, out_ref, raw_ref, st_ref, st_sc):
        @pl.when(pl.program_id(1) == 0)
        def _():
            st_sc[...] = jnp.zeros_like(st_sc)

        mk, mv, bd, lower, ltri, _ = _gla_masks()
        cs = range(G)
        rows = [slice(c * C, (c + 1) * C) for c in cs]
        ps = [p_ref[0, rows[c], :].astype(F32) for c in cs]
        pre = [_mm(ps[c][:, 512:640], w_ref[...]) + b_ref[...] for c in cs]
        cum = [_mm_f32(ltri, _log_sigmoid(pre[c]) * (1.0 / GLA_TAU)) for c in cs]
        past, fut, upd, q_pos, a_row = [], [], [], [], []
        for c in cs:
            q = ps[c][:, 0:128]
            k = ps[c][:, 128:256] * GLA_KSCALE
            last = cum[c][C - 1:C, :]
            e_pos = jnp.exp(cum[c])
            e_neg = jnp.exp(-cum[c])
            q_pos.append(q * e_pos)
            a_row.append(jnp.exp(last))
            past.append(_mm_nt(_stack_heads(q_pos[c], mk), k * e_neg))
            fut.append(_mm_nt(_stack_heads(q * e_neg, mk), k * e_pos))
            upd.append(_mm_tn(ps[c][:, 256:512], k * jnp.exp(last - cum[c])))
        o_s = [_mm(jnp.where(lower, past[c], fut[c]), ps[c][:, 256:512]) for c in cs]
        st = st_sc[...]
        inter = []
        for c in cs:
            st_ref[0, c] = st
            inter.append(_mm_nt(q_pos[c], st))
            st = st * a_row[c] + jnp.where(bd, upd[c], 0.0)
        st_sc[...] = st
        for c in cs:
            g = _fold_heads(o_s[c], mv, C) + inter[c]
            raw_ref[0, rows[c], :] = g
            rstd = lax.rsqrt(_head_mean(g * g, mv, 64.0) + EPS)
            out_ref[0, rows[c], :] = (g * rstd * gn_ref[...] * _silu(ps[c][:, 640:896])).astype(_MXU)

    tok = lambda w: pl.BlockSpec((1, G * C, w), lambda b, n: (b, n, 0))
    return pl.pallas_call(
        body, name="gla_fwd", grid=(B, NG),
        in_specs=[tok(GLA_W), _full((128, 128)), _full((1, 128)), _full((1, 256))],
        out_specs=[tok(256), tok(256), pl.BlockSpec((1, G, 256, 128), lambda b, n: (b, n, 0, 0))],
        out_shape=[jax.ShapeDtypeStruct((B, S, 256), _MXU), jax.ShapeDtypeStruct((B, S, 256), F32),
                   jax.ShapeDtypeStruct((B, NC, 256, 128), F32)],
        scratch_shapes=[pltpu.VMEM((256, 128), F32)],
        compiler_params=_cp(("parallel", "arbitrary")),
    )(gla_p, w_g2p, b_g2, gnw)


def _gla_bwd(gla_p, w_g2p, b_g2, gnw, raw, states, d_mix):
    B, S, _ = gla_p.shape
    C = CHUNK
    NC = S // C
    G = min(GLA_G, NC)
    NG = NC // G

    def body(p_ref, w_ref, b_ref, gn_ref, raw_ref, st_ref, dm_ref, dp_ref, dw_ref, db_ref, dgn_ref, dst_sc):
        first = (pl.program_id(0) == 0) & (pl.program_id(1) == 0)

        @pl.when(first)
        def _():
            dw_ref[...] = jnp.zeros_like(dw_ref)
            db_ref[...] = jnp.zeros_like(db_ref)
            dgn_ref[...] = jnp.zeros_like(dgn_ref)

        @pl.when(pl.program_id(1) == 0)
        def _():
            dst_sc[...] = jnp.zeros_like(dst_sc)

        mk, mv, bd, lower, ltri, utri = _gla_masks()
        gn = gn_ref[...]
        cs = range(G)
        rows = [slice(c * C, (c + 1) * C) for c in cs]
        ps = [p_ref[0, rows[c], :].astype(F32) for c in cs]
        vs = [ps[c][:, 256:512] for c in cs]
        pre = [_mm(ps[c][:, 512:640], w_ref[...]) + b_ref[...] for c in cs]
        cum = [_mm_f32(ltri, _log_sigmoid(pre[c]) * (1.0 / GLA_TAU)) for c in cs]
        dg, dz, dgn_acc = [], [], jnp.zeros((1, 256), F32)
        for c in cs:
            g = raw_ref[0, rows[c], :]
            z = ps[c][:, 640:896]
            rstd = lax.rsqrt(_head_mean(g * g, mv, 64.0) + EPS)
            gh = g * rstd
            dm = dm_ref[0, rows[c], :]
            d_gn = dm * _silu(z)
            dz.append(dm * gh * gn * _dsilu(z))
            dgn_acc = dgn_acc + jnp.sum(d_gn * gh, axis=0, keepdims=True)
            d_gh = d_gn * gn
            dg.append(rstd * (d_gh - gh * _head_mean(d_gh * gh, mv, 64.0)))
        do_s = [_stack_heads(dg[c], mv) for c in cs]
        dattn = [_mm_nt(do_s[c], vs[c]) for c in cs]
        ks, e_pos, e_neg, q_pos, q_neg, k_pos, k_neg, qp_s, qn_s, past, fut, a_row, w_dec, kd = ([] for _ in range(14))
        for c in cs:
            q = ps[c][:, 0:128]
            k = ps[c][:, 128:256] * GLA_KSCALE
            last = cum[c][C - 1:C, :]
            ep, en = jnp.exp(cum[c]), jnp.exp(-cum[c])
            ks.append(k), e_pos.append(ep), e_neg.append(en)
            q_pos.append(q * ep), q_neg.append(q * en), k_pos.append(k * ep), k_neg.append(k * en)
            qp_s.append(_stack_heads(q_pos[c], mk)), qn_s.append(_stack_heads(q_neg[c], mk))
            past.append(_mm_nt(qp_s[c], k_neg[c]))
            fut.append(_mm_nt(qn_s[c], k_pos[c]))
            a_row.append(jnp.exp(last))
            w_dec.append(jnp.exp(last - cum[c]))
            kd.append(k * w_dec[c])
        sts = [st_ref[0, c] for c in cs]
        dq_st = [_mm(dg[c], sts[c]) for c in cs]
        dst_in = [_mm_tn(dg[c], q_pos[c]) for c in cs]
        dv, dq_pos, dk_neg, dq_neg, dk_pos = [], [], [], [], []
        for c in cs:
            attn = jnp.where(lower, past[c], fut[c])
            dpast = jnp.where(lower, dattn[c], 0.0)
            dfut = jnp.where(lower, 0.0, dattn[c])
            dv.append(_mm_tn(attn, do_s[c]))
            dq_pos.append(_fold_heads(_mm(dpast, k_neg[c]), mk, C) + dq_st[c])
            dk_neg.append(_mm_tn(dpast, qp_s[c]))
            dq_neg.append(_fold_heads(_mm(dfut, k_pos[c]), mk, C))
            dk_pos.append(_mm_tn(dfut, qn_s[c]))
        dst_next = dst_sc[...]
        d_a, d_kd = [None] * G, [None] * G
        for c in reversed(cs):
            d_a[c] = jnp.sum(dst_next * sts[c], axis=0, keepdims=True)
            gmat = jnp.where(bd, dst_next, 0.0)
            d_kd[c] = _mm(vs[c], gmat)
            dv[c] = dv[c] + _mm_nt(kd[c], gmat)
            dst_next = dst_next * a_row[c] + jnp.where(bd, dst_in[c], 0.0)
        dst_sc[...] = dst_next
        row = lax.broadcasted_iota(jnp.int32, (C, 128), 0)
        d_la, dk, dq = [], [], []
        for c in cs:
            t = d_kd[c] * kd[c]
            dk.append(d_kd[c] * w_dec[c] + dk_neg[c] * e_neg[c] + dk_pos[c] * e_pos[c])
            dq.append(dq_pos[c] * e_pos[c] + dq_neg[c] * e_neg[c])
            d_last = jnp.sum(t, axis=0, keepdims=True) + d_a[c] * a_row[c]
            d_cum = (dq_pos[c] * q_pos[c] - dk_neg[c] * k_neg[c] - dq_neg[c] * q_neg[c] + dk_pos[c] * k_pos[c] - t)
            d_la.append(_mm_f32(utri, d_cum + jnp.where(row == C - 1, d_last, 0.0)))
        d_pre = [d_la[c] * _sig(-pre[c]) * (1.0 / GLA_TAU) for c in cs]
        d_gg = [_mm_nt(d_pre[c], w_ref[...]) for c in cs]
        dw_acc = _mm_tn(ps[0][:, 512:640], d_pre[0])
        db_acc = jnp.sum(d_pre[0], axis=0, keepdims=True)
        for c in cs[1:]:
            dw_acc = dw_acc + _mm_tn(ps[c][:, 512:640], d_pre[c])
            db_acc = db_acc + jnp.sum(d_pre[c], axis=0, keepdims=True)
        for c in cs:
            dp_ref[0, rows[c], :] = jnp.concatenate([dq[c], dk[c] * GLA_KSCALE, dv[c], d_gg[c], dz[c]],
                                                    axis=1).astype(_MXU)
        dw_ref[...] += dw_acc
        db_ref[...] += db_acc
        dgn_ref[...] += dgn_acc

        @pl.when((pl.program_id(0) == B - 1) & (pl.program_id(1) == NG - 1))
        def _():
            s1 = dgn_ref[...]
            s1 = s1 + pltpu.roll(s1, 128, 1)
            dgn_ref[...] = s1 + pltpu.roll(s1, 64, 1)

    tok = lambda w: pl.BlockSpec((1, G * C, w), lambda b, i: (b, NG - 1 - i, 0))
    return pl.pallas_call(
        body, name="gla_bwd", grid=(B, NG),
        in_specs=[tok(GLA_W), _full((128, 128)), _full((1, 128)), _full((1, 256)), tok(256),
                  pl.BlockSpec((1, G, 256, 128), lambda b, i: (b, NG - 1 - i, 0, 0)), tok(256)],
        out_specs=[tok(GLA_W), _full((128, 128)), _full((1, 128)), _full((1, 256))],
        out_shape=[jax.ShapeDtypeStruct((B, S, GLA_W), _MXU), jax.ShapeDtypeStruct((128, 128), F32),
                   jax.ShapeDtypeStruct((1, 128), F32), jax.ShapeDtypeStruct((1, 256), F32)],
        scratch_shapes=[pltpu.VMEM((256, 128), F32)],
        compiler_params=_cp(("arbitrary", "arbitrary")),
    )(gla_p, w_g2p, b_g2, gnw, raw, states, d_mix)


def _rms(x, w):
    rstd = lax.rsqrt(jnp.mean(x * x, axis=-1, keepdims=True) + EPS)
    xh = x * rstd
    return xh, rstd, xh * w


def _rms_bwd(dy, xh, rstd, w):
    dxh = dy * w
    return rstd * (dxh - xh * jnp.mean(dxh * xh, axis=-1, keepdims=True))


MLA_T = 256


def _mla_prep_fwd(mla_p, cos, sin, qnw, kvnw, w_uq, w_ukv):
    B, S, _ = mla_p.shape
    tm = min(S, 512)

    t = min(MLA_T, S)
    nt = tm // t

    def body(p_ref, c_ref, s_ref, qn_ref, kn_ref, wq_ref, wkv_ref, wkvt_ref, q_ref, k_ref, v_ref, kt_ref, vt_ref):
        p = p_ref[0].astype(F32)
        cs, sn = c_ref[0], s_ref[0]
        _, _, qn = _rms(p[:, 0:256], qn_ref[...])
        qpre = _mm(qn, wq_ref[...])
        _, _, kvn = _rms(p[:, 256:384], kn_ref[...])
        kv = _mm(kvn, wkv_ref[...])
        kvt = _mm_nt(wkvt_ref[...], kvn)
        kpe = _rope128(p[:, 384:512], cs, sn)
        kpet = kpe.T
        for h in range(8):
            sl = slice(128 * h, 128 * h + 128)
            q_ref[0, :, sl] = _rope128(qpre[:, sl], cs, sn).astype(_MXU)
            k_ref[0, :, sl] = (kv[:, sl] + kpe).astype(_MXU)
            kht = kvt[sl, :] + kpet
            for n in range(nt):
                kt_ref[0, n, sl, :] = kht[:, n * t:(n + 1) * t].astype(_MXU)
        v_ref[0] = kv[:, 1024:1536].astype(_MXU)
        for n in range(nt):
            vt_ref[0, n] = kvt[1024:1536, n * t:(n + 1) * t].astype(_MXU)

    tok = lambda w: pl.BlockSpec((1, tm, w), lambda b, i: (b, i, 0))
    tr = lambda w: pl.BlockSpec((1, nt, w, t), lambda b, i: (b, i, 0, 0))
    return pl.pallas_call(
        body, name="mla_prep_fwd", grid=(B, S // tm),
        in_specs=[tok(512), tok(128), tok(128), _full((1, 256)), _full((1, 128)), _full((256, 1024)),
                  _full((128, 1536)), _full((1536, 128))],
        out_specs=[tok(1024), tok(1024), tok(512), tr(1024), tr(512)],
        out_shape=[jax.ShapeDtypeStruct((B, S, 1024), _MXU), jax.ShapeDtypeStruct((B, S, 1024), _MXU),
                   jax.ShapeDtypeStruct((B, S, 512), _MXU), jax.ShapeDtypeStruct((B, S // t, 1024, t), _MXU),
                   jax.ShapeDtypeStruct((B, S // t, 512, t), _MXU)],
        compiler_params=_cp(("parallel", "parallel")),
    )(mla_p, cos, sin, qnw, kvnw, w_uq, w_ukv, w_ukv.T)


def _chunk_mask_t(t):
    kj = lax.broadcasted_iota(jnp.int32, (t, t), 0) // CHUNK
    qi = lax.broadcasted_iota(jnp.int32, (t, t), 1) // CHUNK
    return kj <= qi


MLA_HG = 8
MLA_HG_FWD = 8
LOG2E = 1.4426950408889634
MLA_C2 = MLA_SCALE * LOG2E


def _mla_attn_fwd(q, k, vt):
    B, S, _ = q.shape
    t = min(MLA_T, S)
    nq = S // t
    HG = MLA_HG_FWD
    NP = HG // 2

    def body(q_ref, k_ref, vt_ref, o_ref, lse_ref, sa, sb, m_sc, l_sc, acc_sc):
        i = pl.program_id(2)
        row = lax.broadcasted_iota(jnp.int32, (128, 1), 0)
        low = row < 64
        mask = _chunk_mask_t(t)
        m_sc[...] = jnp.full(m_sc.shape, -jnp.inf, F32)
        l_sc[...] = jnp.zeros_like(l_sc)
        acc_sc[...] = jnp.zeros_like(acc_sc)

        ones = jnp.ones((8, t), _MXU)

        def scores(j, buf):
            kb = k_ref[0, pl.ds(pl.multiple_of(j * t, t), t), :]
            for h in range(HG):
                cols = slice(128 * h, 128 * h + 128)
                buf[h] = (_mm_nt(kb[:, cols], q_ref[0, :, cols]) * MLA_C2).astype(_MXU)

        def absorb(j, buf, masked):
            vtb = vt_ref[0, j]
            for pr in range(NP):
                alphas, pvs = [], []
                for hh in range(2):
                    h = 2 * pr + hh
                    s = buf[h]
                    if masked:
                        s = jnp.where(mask, s, jnp.full_like(s, -jnp.inf))
                    m_old = m_sc[h]
                    m_new = jnp.maximum(m_old, jnp.max(s, axis=0, keepdims=True).astype(F32))
                    alpha = jnp.exp2(m_old - m_new)
                    p = jnp.exp2(s - m_new.astype(_MXU))
                    l_sc[h] = alpha * l_sc[h] + _mm(ones, p)[0:1, :]
                    m_sc[h] = m_new
                    vth = vtb[128 * pr:128 * pr + 128, :]
                    vth = jnp.where(low if hh == 0 else ~low, vth, jnp.zeros_like(vth))
                    pvs.append(_mm(vth, p))
                    alphas.append(alpha)
                acc_sc[pr] = acc_sc[pr] * jnp.where(low, alphas[0], alphas[1]) + pvs[0] + pvs[1]

        scores(0, sb)

        def pair(jj, carry):
            j0 = 2 * jj
            scores(j0 + 1, sa)
            absorb(j0, sb, False)
            scores(j0 + 2, sb)
            absorb(j0 + 1, sa, False)
            return carry

        lax.fori_loop(0, i // 2, pair, 0)

        @pl.when(i % 2 == 1)
        def _():
            scores(i, sa)
            absorb(i - 1, sb, False)
            absorb(i, sa, True)

        @pl.when(i % 2 == 0)
        def _():
            absorb(i, sb, True)

        for pr in range(NP):
            l_e, l_o = l_sc[2 * pr], l_sc[2 * pr + 1]
            o_ref[0, :, 128 * pr:128 * pr + 128] = (acc_sc[pr] / jnp.where(low, l_e, l_o)).T
            lse_ref[0, pr, 0, 0:1, :] = m_sc[2 * pr] + jnp.log(l_e) * LOG2E
            lse_ref[0, pr, 0, 1:2, :] = m_sc[2 * pr + 1] + jnp.log(l_o) * LOG2E

    return pl.pallas_call(
        body, name="mla_attn_fwd", grid=(B, 8 // HG, nq),
        in_specs=[pl.BlockSpec((1, t, 128 * HG), lambda b, g, i: (b, i, g)),
                  pl.BlockSpec((1, S, 128 * HG), lambda b, g, i: (b, 0, g)),
                  pl.BlockSpec((1, nq, 64 * HG, t), lambda b, g, i: (b, 0, g, 0))],
        out_specs=[pl.BlockSpec((1, t, 64 * HG), lambda b, g, i: (b, i, g)),
                   pl.BlockSpec((1, NP, 1, 2, t), lambda b, g, i: (b, g, i, 0, 0))],
        out_shape=[jax.ShapeDtypeStruct((B, S, 512), F32), jax.ShapeDtypeStruct((B, 4, nq, 2, t), F32)],
        scratch_shapes=[pltpu.VMEM((HG, t, t), _MXU), pltpu.VMEM((HG, t, t), _MXU), pltpu.VMEM((HG, 1, t), F32),
                        pltpu.VMEM((HG, 1, t), F32), pltpu.VMEM((NP, 128, t), F32)],
        compiler_params=_cp(("parallel", "parallel", "arbitrary")),
    )(q, k, vt)


def _mla_attn_bwd(q, k, v, kt, do, lse, dl):
    B, S, _ = q.shape
    t = min(MLA_T, S)
    nk = S // t

    HG = MLA_HG
    NP = HG // 2

    def body(q_ref, k_ref, v_ref, kt_ref, do_ref, lse_ref, dl_ref, dq_ref, dk_ref, dv_ref,
             sa, da, sb, db, dqt_sc, dk_sc, dv_sc):
        j = pl.program_id(2)

        @pl.when(j == 0)
        def _():
            dqt_sc[...] = jnp.zeros_like(dqt_sc)

        dk_sc[...] = jnp.zeros_like(dk_sc)
        dv_sc[...] = jnp.zeros_like(dv_sc)
        lane = lax.broadcasted_iota(jnp.int32, (1, 128), 1)
        low = lane < 64
        mask = _chunk_mask_t(t)

        def half(x, hh):
            return jnp.where(low if hh == 0 else ~low, x, jnp.zeros_like(x))

        def prepare(i, sbuf, dbuf):
            rows = pl.ds(pl.multiple_of(i * t, t), t)
            for h in range(HG):
                cols = slice(128 * h, 128 * h + 128)
                pc = slice(128 * (h // 2), 128 * (h // 2) + 128)
                sbuf[h] = _mm_nt(k_ref[0, :, cols], q_ref[0, rows, cols]) * MLA_C2
                dbuf[h] = _mm_nt(half(v_ref[0, :, pc], h % 2), do_ref[0, rows, pc])

        def absorb(i, sbuf, dbuf, masked):
            rows = pl.ds(pl.multiple_of(i * t, t), t)
            for h in range(HG):
                pr, hh = h // 2, h % 2
                cols = slice(128 * h, 128 * h + 128)
                pc = slice(128 * pr, 128 * pr + 128)
                p = jnp.exp2(sbuf[h] - lse_ref[0, pr, i][hh:hh + 1, :])
                if masked:
                    p = jnp.where(mask, p, 0.0)
                dv_sc[pr] += _mm(p, half(do_ref[0, rows, pc], hh))
                ds = p * (dbuf[h] - dl_ref[0, pr, i][hh:hh + 1, :])
                dqt_sc[i, cols, :] += _mm(kt_ref[0, 0, cols, :], ds)
                dk_sc[h] += _mm(ds, q_ref[0, rows, cols])

        n = nk - 1 - j
        prepare(jnp.minimum(j + 1, nk - 1), sb, db)

        def pair(jj, carry):
            i0 = j + 1 + 2 * jj
            prepare(i0 + 1, sa, da)
            absorb(i0, sb, db, False)
            prepare(jnp.where(i0 + 2 <= nk - 1, i0 + 2, j), sb, db)
            absorb(i0 + 1, sa, da, False)
            return carry

        lax.fori_loop(0, n // 2, pair, 0)

        @pl.when(n % 2 == 1)
        def _():
            prepare(j, sa, da)
            absorb(nk - 1, sb, db, False)
            absorb(j, sa, da, True)

        @pl.when(n % 2 == 0)
        def _():
            absorb(j, sb, db, True)

        for h in range(HG):
            dk_ref[0, :, 128 * h:128 * h + 128] = (dk_sc[h] * MLA_SCALE).astype(_MXU)
        for pr in range(NP):
            dv_ref[0, :, 128 * pr:128 * pr + 128] = dv_sc[pr].astype(_MXU)

        @pl.when(j == nk - 1)
        def _():
            for i in range(nk):
                dq_ref[0, i * t:(i + 1) * t, :] = (dqt_sc[i].T * MLA_SCALE).astype(_MXU)

    seq = lambda w: pl.BlockSpec((1, S, w), lambda b, g, j: (b, 0, g))
    blk = lambda w: pl.BlockSpec((1, t, w), lambda b, g, j: (b, j, g))
    stat = pl.BlockSpec((1, NP, nk, 2, t), lambda b, g, j: (b, g, 0, 0, 0))
    return pl.pallas_call(
        body, name="mla_attn_bwd", grid=(B, 8 // HG, nk),
        in_specs=[seq(128 * HG), blk(128 * HG), blk(64 * HG),
                  pl.BlockSpec((1, 1, 128 * HG, t), lambda b, g, j: (b, j, g, 0)), seq(64 * HG), stat, stat],
        out_specs=[seq(128 * HG), blk(128 * HG), blk(64 * HG)],
        out_shape=[jax.ShapeDtypeStruct((B, S, 1024), _MXU), jax.ShapeDtypeStruct((B, S, 1024), _MXU),
                   jax.ShapeDtypeStruct((B, S, 512), _MXU)],
        scratch_shapes=[pltpu.VMEM((HG, t, t), F32), pltpu.VMEM((HG, t, t), F32), pltpu.VMEM((HG, t, t), F32),
                        pltpu.VMEM((HG, t, t), F32), pltpu.VMEM((nk, 128 * HG, t), F32),
                        pltpu.VMEM((HG, t, 128), F32), pltpu.VMEM((NP, t, 128), F32)],
        compiler_params=_cp(("parallel", "parallel", "arbitrary"), 56),
    )(q, k, v, kt, do, lse, dl)


def _mla_prep_bwd(mla_p, cos, sin, qnw, kvnw, w_uq, w_ukv, dq, dk, dv):
    B, S, _ = mla_p.shape
    tm = min(S, 512)

    def body(p_ref, c_ref, s_ref, qn_ref, kn_ref, wq_ref, wkv_ref, dq_ref, dk_ref, dv_ref,
             dp_ref, dwq_ref, dwkv_ref, dqn_ref, dkn_ref):
        first = (pl.program_id(0) == 0) & (pl.program_id(1) == 0)

        @pl.when(first)
        def _():
            dwq_ref[...] = jnp.zeros_like(dwq_ref)
            dwkv_ref[...] = jnp.zeros_like(dwkv_ref)
            dqn_ref[...] = jnp.zeros_like(dqn_ref)
            dkn_ref[...] = jnp.zeros_like(dkn_ref)

        p = p_ref[0].astype(F32)
        cs, sn = c_ref[0], s_ref[0]
        lane = lax.broadcasted_iota(jnp.int32, (1, 128), 1)
        pe = (lane >= 64) & (lane < 96)
        qh, q_rstd, qn = _rms(p[:, 0:256], qn_ref[...])
        kvh, kv_rstd, kvn = _rms(p[:, 256:384], kn_ref[...])
        dqv = dq_ref[0].astype(F32)
        dkv = dk_ref[0].astype(F32)
        dqpre = jnp.concatenate(
            [_rope128_t(dqv[:, 128 * h:128 * h + 128], cs, sn) for h in range(8)], axis=1)
        dkpe = jnp.zeros((tm, 128), F32)
        for h in range(8):
            dkpe = dkpe + jnp.where(pe, dkv[:, 128 * h:128 * h + 128], 0.0)
        dkr = _rope128_t(dkpe, cs, sn)
        dkv_all = jnp.concatenate([dkv, dv_ref[0].astype(F32)], axis=1)
        d_qn = _mm_nt(dqpre, wq_ref[...])
        d_kvn = _mm_nt(dkv_all, wkv_ref[...])
        dwq_ref[...] += _mm_tn(qn, dqpre)
        dwkv_ref[...] += _mm_tn(kvn, dkv_all)
        dqn_ref[...] += jnp.sum(d_qn * qh, axis=0, keepdims=True)
        dkn_ref[...] += jnp.sum(d_kvn * kvh, axis=0, keepdims=True)
        dp_ref[0] = jnp.concatenate([_rms_bwd(d_qn, qh, q_rstd, qn_ref[...]),
                                     _rms_bwd(d_kvn, kvh, kv_rstd, kn_ref[...]), dkr], axis=1).astype(_MXU)

    tok = lambda w: pl.BlockSpec((1, tm, w), lambda b, i: (b, i, 0))
    return pl.pallas_call(
        body, name="mla_prep_bwd", grid=(B, S // tm),
        in_specs=[tok(512), tok(128), tok(128), _full((1, 256)), _full((1, 128)), _full((256, 1024)),
                  _full((128, 1536)), tok(1024), tok(1024), tok(512)],
        out_specs=[tok(512), _full((256, 1024)), _full((128, 1536)), _full((1, 256)), _full((1, 128))],
        out_shape=[jax.ShapeDtypeStruct((B, S, 512), _MXU), jax.ShapeDtypeStruct((256, 1024), F32),
                   jax.ShapeDtypeStruct((128, 1536), F32), jax.ShapeDtypeStruct((1, 256), F32),
                   jax.ShapeDtypeStruct((1, 128), F32)],
        compiler_params=_cp(("arbitrary", "arbitrary")),
    )(mla_p, cos, sin, qnw, kvnw, w_uq, w_ukv, dq, dk, dv)


def _out_fwd(x, gate, r_g, o_mla, mla_p, g_g, w_out):
    B, S, D = x.shape
    tm = min(S, 512)

    def body(x_ref, g_ref, r_ref, o_ref, z_ref, gg_ref, w_ref, xn_ref, y_ref, mm_ref):
        mm = (o_ref[0] * _silu(z_ref[0].astype(F32))).astype(_MXU)
        mm_ref[0] = mm
        y = (jnp.dot(r_ref[0], w_ref[0:256, :], preferred_element_type=F32)
             + jnp.dot(mm, w_ref[256:768, :], preferred_element_type=F32)
             + jnp.dot(gg_ref[0], w_ref[768:1024, :], preferred_element_type=F32))
        y_ref[0] = y.astype(_MXU)
        xn_ref[0] = x_ref[0] + g_ref[0] * y

    tok = lambda w, c=0: pl.BlockSpec((1, tm, w), lambda b, i: (b, i, c))
    return pl.pallas_call(
        body, name="out_fwd", grid=(B, S // tm),
        in_specs=[tok(D), pl.BlockSpec((1, 1, D), lambda b, i: (b, 0, 0)), tok(256), tok(512), tok(512, 1),
                  tok(256), _full((D, D))],
        out_specs=[tok(D), tok(D), tok(512)],
        out_shape=[jax.ShapeDtypeStruct((B, S, D), F32), jax.ShapeDtypeStruct((B, S, D), _MXU),
                   jax.ShapeDtypeStruct((B, S, 512), _MXU)],
        compiler_params=_cp(("parallel", "parallel")),
    )(x, gate, r_g, o_mla, mla_p, g_g, w_out)


def _out_bwd(dx, y, gate, r_g, mm, g_g, w_out, o_mla, mla_p):
    B, S, D = dx.shape
    tm = min(S, 512)
    t = min(MLA_T, S)
    nt = tm // t

    def body(dx_ref, y_ref, g_ref, r_ref, mm_ref, gg_ref, w_ref, o_ref, z_ref,
             dr_ref, do_ref, dz_ref, dl_ref, dg_ref, dw_ref, dgate_ref, acc):
        first = (pl.program_id(0) == 0) & (pl.program_id(1) == 0)

        @pl.when(first)
        def _():
            acc[...] = jnp.zeros_like(acc)

        @pl.when(pl.program_id(1) == 0)
        def _():
            dgate_ref[...] = jnp.zeros_like(dgate_ref)

        dxv = dx_ref[0]
        dgate_ref[0] += jnp.sum(dxv * y_ref[0].astype(F32), axis=0, keepdims=True)
        dy = (dxv * g_ref[0]).astype(_MXU)
        dr_ref[0] = _mm_nt(dy, w_ref[0:256, :])
        dg_ref[0] = _mm_nt(dy, w_ref[768:1024, :])
        acc[0:256, :] += _mm_tn(r_ref[0], dy)
        acc[256:768, :] += _mm_tn(mm_ref[0], dy)
        acc[768:1024, :] += _mm_tn(gg_ref[0], dy)

        @pl.when((pl.program_id(0) == B - 1) & (pl.program_id(1) == S // tm - 1))
        def _():
            dw_ref[...] = acc[...].astype(_MXU)

        dm = _mm_nt(dy, w_ref[256:768, :])
        ov, z = o_ref[0], z_ref[0].astype(F32)
        do = dm * _silu(z)
        dz_ref[0] = (dm * ov * _dsilu(z)).astype(_MXU)
        do_ref[0] = do.astype(_MXU)
        prod = do * ov
        for pr in range(4):
            pt = prod[:, 128 * pr:128 * pr + 128].T
            se = jnp.sum(pt[0:64], axis=0, keepdims=True)
            so = jnp.sum(pt[64:128], axis=0, keepdims=True)
            for n in range(nt):
                dl_ref[0, pr, n, 0:1, :] = se[:, n * t:(n + 1) * t]
                dl_ref[0, pr, n, 1:2, :] = so[:, n * t:(n + 1) * t]

    tok = lambda w, c=0: pl.BlockSpec((1, tm, w), lambda b, i: (b, i, c))
    per_seq = pl.BlockSpec((1, 1, D), lambda b, i: (b, 0, 0))
    return pl.pallas_call(
        body, name="out_bwd", grid=(B, S // tm),
        in_specs=[tok(D), tok(D), per_seq, tok(256), tok(512), tok(256), _full((D, D)), tok(512), tok(512, 1)],
        out_specs=[tok(256), tok(512), tok(512), pl.BlockSpec((1, 4, nt, 2, t), lambda b, i: (b, 0, i, 0, 0)),
                   tok(256), _full((D, D)), per_seq],
        out_shape=[jax.ShapeDtypeStruct((B, S, 256), F32), jax.ShapeDtypeStruct((B, S, 512), _MXU),
                   jax.ShapeDtypeStruct((B, S, 512), _MXU), jax.ShapeDtypeStruct((B, 4, S // t, 2, t), F32),
                   jax.ShapeDtypeStruct((B, S, 256), F32), jax.ShapeDtypeStruct((D, D), _MXU),
                   jax.ShapeDtypeStruct((B, 1, D), F32)],
        scratch_shapes=[pltpu.VMEM((D, D), F32)],
        compiler_params=_cp(("arbitrary", "arbitrary")),
    )(dx, y, gate, r_g, mm, g_g, w_out, o_mla, mla_p)


def _proj_bwd_x(x, shift, scale, nw, w_arr, d_ret, d_mla, d_mz, d_gla, dx_out):
    B, S, D = x.shape
    tm = min(S, 512)

    def body(x_ref, sc_ref, nw_ref, w_ref, dr_ref, dm_ref, dz_ref, dg_ref, dxo_ref,
             dx_ref, dsh_ref, dsc_ref, dnw_ref):
        first = (pl.program_id(0) == 0) & (pl.program_id(1) == 0)

        @pl.when(first)
        def _():
            dnw_ref[...] = jnp.zeros_like(dnw_ref)

        @pl.when(pl.program_id(1) == 0)
        def _():
            dsh_ref[...] = jnp.zeros_like(dsh_ref)
            dsc_ref[...] = jnp.zeros_like(dsc_ref)

        dp = jnp.concatenate([dr_ref[0], dm_ref[0], dz_ref[0], dg_ref[0]], axis=1)
        dh = lax.dot_general(dp, w_ref[...], (((1,), (1,)), ((), ())), preferred_element_type=F32)
        xv = x_ref[0]
        rstd = lax.rsqrt(jnp.mean(xv * xv, axis=-1, keepdims=True) + EPS)
        xh = xv * rstd
        nwv = nw_ref[...]
        mod = 1.0 + sc_ref[0]
        dsh_ref[0] += jnp.sum(dh, axis=0, keepdims=True)
        dsc_ref[0] += jnp.sum(dh * xh * nwv, axis=0, keepdims=True)
        dnw_ref[...] += jnp.sum(dh * xh * mod, axis=0, keepdims=True)
        dxh = dh * nwv * mod
        dx_ref[0] = dxo_ref[0] + rstd * (dxh - xh * jnp.mean(dxh * xh, axis=-1, keepdims=True))

    tok = lambda w: pl.BlockSpec((1, tm, w), lambda b, i: (b, i, 0))
    per_seq = pl.BlockSpec((1, 1, D), lambda b, i: (b, 0, 0))
    return pl.pallas_call(
        body, name="proj_bwd_x", grid=(B, S // tm),
        in_specs=[tok(D), per_seq, _full((1, D)), _full((D, ARR_W)), tok(RET_W), tok(512), tok(512),
                  tok(GLA_W), tok(D)],
        out_specs=[tok(D), per_seq, per_seq, _full((1, D))],
        out_shape=[jax.ShapeDtypeStruct((B, S, D), F32), jax.ShapeDtypeStruct((B, 1, D), F32),
                   jax.ShapeDtypeStruct((B, 1, D), F32), jax.ShapeDtypeStruct((1, D), F32)],
        compiler_params=_cp(("arbitrary", "arbitrary")),
    )(x, scale, nw, w_arr, d_ret, d_mla, d_mz, d_gla, dx_out)


def _proj_bwd_w(h, d_ret, d_mla, d_mz, d_gla):
    B, S, D = h.shape
    tm = min(S, 512)

    def body(h_ref, dr_ref, dm_ref, dz_ref, dg_ref, dw_ref, acc):
        first = (pl.program_id(0) == 0) & (pl.program_id(1) == 0)

        @pl.when(first)
        def _():
            acc[...] = jnp.zeros_like(acc)

        hv = h_ref[0]
        tn = lambda d_ref: lax.dot_general(hv, d_ref[0], (((0,), (0,)), ((), ())), preferred_element_type=F32)
        acc[:, 0:RET_W] += tn(dr_ref)
        acc[:, RET_W:RET_W + 512] += tn(dm_ref)
        acc[:, RET_W + 512:RET_W + MLA_W] += tn(dz_ref)
        acc[:, RET_W + MLA_W:ARR_W] += tn(dg_ref)

        @pl.when((pl.program_id(0) == B - 1) & (pl.program_id(1) == S // tm - 1))
        def _():
            dw_ref[...] = acc[...].astype(_MXU)

    tok = lambda w: pl.BlockSpec((1, tm, w), lambda b, i: (b, i, 0))
    return pl.pallas_call(
        body, name="proj_bwd_w", grid=(B, S // tm),
        in_specs=[tok(D), tok(RET_W), tok(512), tok(512), tok(GLA_W)],
        out_specs=_full((D, ARR_W)), out_shape=jax.ShapeDtypeStruct((D, ARR_W), _MXU),
        scratch_shapes=[pltpu.VMEM((D, ARR_W), F32)],
        compiler_params=_cp(("arbitrary", "arbitrary"), 56),
    )(h, d_ret, d_mla, d_mz, d_gla)


def _out_fwd_loss(x, gate, r_g, o_mla, mla_p, g_g, w_out, fw, target):
    B, S, D = x.shape
    tm = min(S, 512)

    def body(x_ref, g_ref, r_ref, o_ref, z_ref, gg_ref, w_ref, fw_ref, t_ref, dx_ref, y_ref, mm_ref, loss_ref, dfw_ref):
        first = (pl.program_id(0) == 0) & (pl.program_id(1) == 0)

        @pl.when(first)
        def _():
            loss_ref[...] = jnp.zeros_like(loss_ref)
            dfw_ref[...] = jnp.zeros_like(dfw_ref)

        mm = (o_ref[0] * _silu(z_ref[0].astype(F32))).astype(_MXU)
        mm_ref[0] = mm
        y = (jnp.dot(r_ref[0], w_ref[0:256, :], preferred_element_type=F32)
             + jnp.dot(mm, w_ref[256:768, :], preferred_element_type=F32)
             + jnp.dot(gg_ref[0], w_ref[768:1024, :], preferred_element_type=F32))
        y_ref[0] = y.astype(_MXU)
        xv = x_ref[0] + g_ref[0] * y
        fwv = fw_ref[...]
        rstd = lax.rsqrt(jnp.mean(xv * xv, axis=-1, keepdims=True) + EPS)
        xh = xv * rstd
        err = xh * fwv - t_ref[0]
        loss_ref[...] += 0.5 * jnp.sum(jnp.mean(err * err, axis=-1, keepdims=True), axis=0, keepdims=True)
        dy = err * (1.0 / D)
        dfw_ref[...] += jnp.sum(dy * xh, axis=0, keepdims=True)
        dxh = dy * fwv
        dx_ref[0] = rstd * (dxh - xh * jnp.mean(dxh * xh, axis=-1, keepdims=True))

    tok = lambda w, c=0: pl.BlockSpec((1, tm, w), lambda b, i: (b, i, c))
    return pl.pallas_call(
        body, name="out_fwd_loss", grid=(B, S // tm),
        in_specs=[tok(D), pl.BlockSpec((1, 1, D), lambda b, i: (b, 0, 0)), tok(256), tok(512), tok(512, 1),
                  tok(256), _full((D, D)), _full((1, D)), tok(D)],
        out_specs=[tok(D), tok(D), tok(512), _full((1, 1)), _full((1, D))],
        out_shape=[jax.ShapeDtypeStruct((B, S, D), F32), jax.ShapeDtypeStruct((B, S, D), _MXU),
                   jax.ShapeDtypeStruct((B, S, 512), _MXU), jax.ShapeDtypeStruct((1, 1), F32),
                   jax.ShapeDtypeStruct((1, D), F32)],
        compiler_params=_cp(("arbitrary", "arbitrary")),
    )(x, gate, r_g, o_mla, mla_p, g_g, w_out, fw, target)


def _local_step(x, pos3, mod, loss_target, small, w_in_a, w_uq_a, w_ukv_a, w_out_b):
    B, S, D = x.shape
    tabs = _rope_tables(pos3)
    saved = []
    for l in range(DEPTH):
        last = (small["final_norm"].reshape(1, D), loss_target) if l == DEPTH - 1 else None
        x, s = _layer_fwd(x, tabs, mod[l], {n: a[l] for n, a in small.items() if n != "final_norm"},
                          w_in_a[l], w_uq_a[l], w_ukv_a[l], w_out_b[l], loss_head=last)
        saved.append(s)
    dx, loss, d_fw = x
    grads = dict(final_norm=d_fw.reshape(D))
    per_layer = [None] * DEPTH
    for l in reversed(range(DEPTH)):
        dx, per_layer[l] = _layer_bwd(dx, saved[l], tabs)
    for name in per_layer[0]:
        grads[name] = jnp.stack([per_layer[l][name] for l in range(DEPTH)])
    return loss, dx, grads


def _layer_fwd(x, tabs, mod_l, small_l, w_in_a, w_uq_a=None, w_ukv_a=None, w_out_b=None, late_weights=None,
               loss_head=None):
    B, S, D = x.shape
    cr, sr, cm, sm = tabs
    shift = mod_l[:, 0:D].reshape(B, 1, D)
    scale = mod_l[:, D:2 * D].reshape(B, 1, D)
    gate = mod_l[:, 2 * D:3 * D].reshape(B, 1, D)
    nw = small_l["norm_w"].reshape(1, D)
    qnw = small_l["mla_q_norm"].reshape(1, 256)
    kvnw = small_l["mla_kv_norm"].reshape(1, 128)
    w_g2p = jnp.pad(small_l["gla_w_g2"], ((0, 112), (0, 0)))
    b_g2 = small_l["gla_b_g2"].reshape(1, 128)
    gnw = jnp.tile(small_l["gla_norm"], 4).reshape(1, 256)
    ret_p, mla_p, gla_p, h = _proj_fwd(x, shift, scale, nw, w_in_a)
    r_g, r_raw, r_st = _ret_fwd(ret_p, cr, sr)
    if late_weights is not None:
        w_uq_a, w_ukv_a, w_out_b = late_weights(r_raw)
    q, k, v, kt, vt = _mla_prep_fwd(mla_p, cm, sm, qnw, kvnw, w_uq_a, w_ukv_a)
    o_mla, lse = _mla_attn_fwd(q, k, vt)
    g_g, g_raw, g_st = _gla_fwd(gla_p, w_g2p, b_g2, gnw)
    if loss_head is None:
        x_new, y, mm = _out_fwd(x, gate, r_g, o_mla, mla_p, g_g, w_out_b)
    else:
        dx, y, mm, loss, d_fw = _out_fwd_loss(x, gate, r_g, o_mla, mla_p, g_g, w_out_b, *loss_head)
        x_new = (dx, loss, d_fw)
    saved = dict(x=x, shift=shift, scale=scale, gate=gate, nw=nw, qnw=qnw, kvnw=kvnw, w_g2p=w_g2p, b_g2=b_g2,
                 gnw=gnw, ret_p=ret_p, mla_p=mla_p, gla_p=gla_p, h=h, r_g=r_g, r_raw=r_raw, r_st=r_st, q=q, k=k,
                 v=v, kt=kt, o_mla=o_mla, lse=lse, g_g=g_g, g_raw=g_raw, g_st=g_st, y=y, mm=mm,
                 w_in_a=w_in_a, w_uq_a=w_uq_a, w_ukv_a=w_ukv_a, w_out_b=w_out_b)
    return x_new, saved


def _layer_bwd(dx, s, tabs, early_grads=None):
    B, S, D = dx.shape
    cr, sr, cm, sm = tabs
    d_r, do, d_mz, dl, d_g, dw_out, d_gate = _out_bwd(dx, s["y"], s["gate"], s["r_g"], s["mm"], s["g_g"], s["w_out_b"],
                                                      s["o_mla"], s["mla_p"])
    d_ret = _ret_bwd(s["ret_p"], cr, sr, s["r_raw"], s["r_st"], d_r)
    dq, dk, dv = _mla_attn_bwd(s["q"], s["k"], s["v"], s["kt"], do, s["lse"], dl)
    d_mla, dw_uq, dw_ukv, d_qnw, d_kvnw = _mla_prep_bwd(
        s["mla_p"], cm, sm, s["qnw"], s["kvnw"], s["w_uq_a"], s["w_ukv_a"], dq, dk, dv)
    gnw = s["gnw"] if early_grads is None else s["gnw"] + early_grads(dw_out, dw_uq, dw_ukv)
    d_gla, dw_g2p, db_g2, d_gnw = _gla_bwd(s["gla_p"], s["w_g2p"], s["b_g2"], gnw, s["g_raw"], s["g_st"], d_g)
    dx, d_shift, d_scale, d_nw = _proj_bwd_x(s["x"], s["shift"], s["scale"], s["nw"], s["w_in_a"],
                                             d_ret, d_mla, d_mz, d_gla, dx)
    dw_in = _proj_bwd_w(s["h"], d_ret, d_mla, d_mz, d_gla)
    grads = dict(
        d_mod=jnp.concatenate([d_shift, d_scale, d_gate], axis=2).reshape(B, 3 * D),
        norm_w=d_nw.reshape(D), mla_q_norm=d_qnw.reshape(256), mla_kv_norm=d_kvnw.reshape(128),
        gla_w_g2=dw_g2p[0:16], gla_b_g2=db_g2.reshape(128), gla_norm256=d_gnw.reshape(256),
        w_in_a=dw_in, w_uq_a=dw_uq, w_ukv_a=dw_ukv, w_out=dw_out)
    return dx, grads


def _exchange(arrs, gather, name):
    n = len(arrs)
    out_shape = [jax.ShapeDtypeStruct(((N_DEV,) + a.shape) if g else a.shape, a.dtype)
                 for a, g in zip(arrs, gather)]

    def body(*refs):
        ins, outs = refs[:n], refs[n:2 * n]
        send_sems, recv_sems, local_sems = refs[2 * n:]
        ix, iy, ic = lax.axis_index("x"), lax.axis_index("y"), lax.axis_index("c")
        me = 4 * ix + 2 * iy + ic
        copies = []
        for a in range(n):
            mine = ins[a] if gather[a] else ins[a].at[me]
            loc = pltpu.make_async_copy(mine, outs[a].at[me], local_sems.at[a])
            loc.start()
            copies.append(loc)
            for d in range(1, N_DEV):
                px = 1 - ix if d & 4 else ix
                py = 1 - iy if d & 2 else iy
                pc = 1 - ic if d & 1 else ic
                src = ins[a] if gather[a] else ins[a].at[4 * px + 2 * py + pc]
                cp = pltpu.make_async_remote_copy(
                    src_ref=src, dst_ref=outs[a].at[me], send_sem=send_sems.at[a, d - 1],
                    recv_sem=recv_sems.at[a, d - 1], device_id=(px, py, pc), device_id_type=pl.DeviceIdType.MESH)
                cp.start()
                copies.append(cp)
        for cp in copies:
            cp.wait()

    any_spec = pl.BlockSpec(memory_space=pl.ANY)
    outs = pl.pallas_call(
        body, name=name, in_specs=[any_spec] * n, out_specs=[any_spec] * n, out_shape=out_shape,
        scratch_shapes=[pltpu.SemaphoreType.DMA((n, N_DEV - 1)), pltpu.SemaphoreType.DMA((n, N_DEV - 1)),
                        pltpu.SemaphoreType.DMA((n,))],
    )(*arrs)
    return list(outs)


def _peers(ix, iy, ic):
    out = []
    for d in range(1, N_DEV):
        px = 1 - ix if d & 4 else ix
        py = 1 - iy if d & 2 else iy
        pc = 1 - ic if d & 1 else ic
        out.append((d - 1, (px, py, pc), 4 * px + 2 * py + pc))
    return out


def _exchange_start(arrs, gather, name, after=None):
    n = len(arrs)
    lands = [lax.empty(((N_DEV,) + a.shape) if g else a.shape, a.dtype) for a, g in zip(arrs, gather)]
    extra = [] if after is None else [after]

    def body(*refs):
        ins, land_refs = refs[:n], refs[n:2 * n]
        send_sems, recv_sems = refs[2 * n + len(extra)], refs[2 * n + len(extra) + 1]
        token = refs[-1]
        ix, iy, ic = lax.axis_index("x"), lax.axis_index("y"), lax.axis_index("c")
        me = 4 * ix + 2 * iy + ic
        for a in range(n):
            for k, peer, peer_idx in _peers(ix, iy, ic):
                pltpu.make_async_remote_copy(
                    src_ref=ins[a] if gather[a] else ins[a].at[peer_idx], dst_ref=land_refs[a].at[me],
                    send_sem=send_sems.at[7 * a + k], recv_sem=recv_sems.at[7 * a + k], device_id=peer,
                    device_id_type=pl.DeviceIdType.MESH).start()
        token[...] = jnp.zeros_like(token)

    hbm = pl.BlockSpec(memory_space=pltpu.HBM)
    sem = pl.BlockSpec(memory_space=pltpu.SEMAPHORE)
    held = [pltpu.with_memory_space_constraint(a, pltpu.HBM) for a in list(arrs) + lands]
    outs = pl.pallas_call(
        body, name=name,
        out_shape=(pltpu.SemaphoreType.DMA((7 * n,)), pltpu.SemaphoreType.DMA((7 * n,)),
                   *[pltpu.HBM(a.shape, a.dtype) for a in held], jax.ShapeDtypeStruct((8, 128), F32)),
        in_specs=[hbm] * (2 * n) + [pl.BlockSpec(memory_space=pl.ANY)] * len(extra),
        out_specs=(sem, sem, *[hbm] * (2 * n), pl.BlockSpec(memory_space=pltpu.VMEM)),
        input_output_aliases={a: 2 + a for a in range(2 * n)},
        compiler_params=pltpu.CompilerParams(has_side_effects=pltpu.SideEffectType.DATAFLOW_SIDE_EFFECTING),
    )(*held, *extra)
    return dict(send=outs[0], recv=outs[1], srcs=list(outs[2:2 + n]), lands=list(outs[2 + n:2 + 2 * n]),
                token=outs[-1], gather=list(gather))


def _exchange_wait(flight, after, me, name):
    n = len(flight["srcs"])
    gather = flight["gather"]

    def body(*refs):
        srcs, land_refs = refs[:n], refs[n:2 * n]
        send_sems, recv_sems = refs[2 * n], refs[2 * n + 1]
        ix, iy, ic = lax.axis_index("x"), lax.axis_index("y"), lax.axis_index("c")
        mine = 4 * ix + 2 * iy + ic
        for a in range(n):
            for k, peer, peer_idx in _peers(ix, iy, ic):
                cp = pltpu.make_async_remote_copy(
                    src_ref=srcs[a] if gather[a] else srcs[a].at[peer_idx], dst_ref=land_refs[a].at[mine],
                    send_sem=send_sems.at[7 * a + k], recv_sem=recv_sems.at[7 * a + k], device_id=peer,
                    device_id_type=pl.DeviceIdType.MESH)
                cp.wait_send()
                cp.wait_recv()

    hbm = pl.BlockSpec(memory_space=pltpu.HBM)
    sem = pl.BlockSpec(memory_space=pltpu.SEMAPHORE)
    held = flight["srcs"] + flight["lands"]
    outs = pl.pallas_call(
        body, name=name, out_shape=tuple(pltpu.HBM(a.shape, a.dtype) for a in held),
        in_specs=[hbm] * (2 * n) + [sem, sem, pl.BlockSpec(memory_space=pl.ANY)], out_specs=tuple([hbm] * (2 * n)),
        input_output_aliases={a: a for a in range(2 * n)},
        compiler_params=pltpu.CompilerParams(has_side_effects=pltpu.SideEffectType.DATAFLOW_SIDE_EFFECTING),
    )(*held, flight["send"], flight["recv"], after)
    got = []
    for a in range(n):
        src, land = outs[a], outs[n + a]
        own = src if gather[a] else lax.dynamic_index_in_dim(src, me, axis=0, keepdims=False)
        got.append(lax.dynamic_update_index_in_dim(land, own, me, axis=0))
    return got


def _ada_fwd(c_all, ada_w, ada_b_cols):
    nb, D = c_all.shape
    cols = ada_w.shape[2]

    def body(c_ref, w_ref, b_ref, out_ref):
        ca = _silu(c_ref[...])
        for l in range(DEPTH):
            out_ref[l] = _mm(ca, w_ref[l]) + b_ref[l:l + 1, :]

    return pl.pallas_call(
        body, name="ada_fwd", out_shape=jax.ShapeDtypeStruct((DEPTH, nb, cols), F32),
        in_specs=[pl.BlockSpec(memory_space=pltpu.VMEM)] * 3, out_specs=pl.BlockSpec(memory_space=pltpu.VMEM),
        compiler_params=pltpu.CompilerParams(vmem_limit_bytes=32 * VMEM_MB),
    )(c_all, ada_w, ada_b_cols)


def _ada_bwd(c_all, d_mod_cols):
    nb, D = c_all.shape
    cols = d_mod_cols.shape[2]

    def body(c_ref, dm_ref, out_ref):
        ca = _silu(c_ref[...])
        for l in range(DEPTH):
            out_ref[l] = _mm_tn(ca, dm_ref[l])

    return pl.pallas_call(
        body, name="ada_bwd", out_shape=jax.ShapeDtypeStruct((DEPTH, D, cols), F32),
        in_specs=[pl.BlockSpec(memory_space=pltpu.VMEM)] * 2, out_specs=pl.BlockSpec(memory_space=pltpu.VMEM),
        compiler_params=pltpu.CompilerParams(vmem_limit_bytes=32 * VMEM_MB),
    )(c_all, d_mod_cols)


def _sum_adamw(parts, w, m, v, name):
    P, R, C = parts.shape
    tr = 256 if (R % 256 == 0 and R > 256) else R

    def body(p_ref, w_ref, m_ref, v_ref, g_ref, d_ref, nm_ref, nv_ref):
        g = p_ref[0].astype(F32)
        for k in range(1, P):
            g = g + p_ref[k].astype(F32)
        g_ref[...] = g
        nm = ADAM_B1 * m_ref[...] + (1.0 - ADAM_B1) * g
        nv = ADAM_B2 * v_ref[...] + (1.0 - ADAM_B2) * (g * g)
        nm_ref[...] = nm
        nv_ref[...] = nv
        m_hat = nm / (1.0 - ADAM_B1 ** ADAM_STEP)
        v_hat = nv / (1.0 - ADAM_B2 ** ADAM_STEP)
        d_ref[...] = -ADAM_LR * (m_hat / (jnp.sqrt(v_hat) + ADAM_EPS) + ADAM_WD * w_ref[...])

    blk = pl.BlockSpec((tr, C), lambda i: (i, 0))
    shp = jax.ShapeDtypeStruct((R, C), F32)
    return pl.pallas_call(
        body, name=name, grid=(R // tr,),
        in_specs=[pl.BlockSpec((P, tr, C), lambda i: (0, i, 0)), blk, blk, blk],
        out_specs=[blk, blk, blk, blk], out_shape=[shp, shp, shp, shp],
        compiler_params=_cp(("parallel",)),
    )(parts, w, m, v)


def _sum_adamw_layer(parts, w, m, v, layer, name, prev=None, after=None):
    P, R, C = parts.shape
    tr = 256 if (R % 256 == 0 and R > 256) else R

    def body(p_ref, w_ref, m_ref, v_ref, *rest):
        g_ref, d_ref, nm_ref, nv_ref = rest[-4:]
        g = p_ref[0].astype(F32)
        for k in range(1, P):
            g = g + p_ref[k].astype(F32)
        g_ref[0] = g
        nm = ADAM_B1 * m_ref[0] + (1.0 - ADAM_B1) * g
        nv = ADAM_B2 * v_ref[0] + (1.0 - ADAM_B2) * (g * g)
        nm_ref[0] = nm
        nv_ref[0] = nv
        m_hat = nm / (1.0 - ADAM_B1 ** ADAM_STEP)
        v_hat = nv / (1.0 - ADAM_B2 ** ADAM_STEP)
        d_ref[0] = -ADAM_LR * (m_hat / (jnp.sqrt(v_hat) + ADAM_EPS) + ADAM_WD * w_ref[0])

    blk = pl.BlockSpec((1, tr, C), lambda i: (layer, i, 0))
    shp = jax.ShapeDtypeStruct(w.shape, F32)
    in_specs = [pl.BlockSpec((P, tr, C), lambda i: (0, i, 0)), blk, blk, blk]
    args = [parts, w, m, v]
    aliases = {}
    if prev is not None:
        in_specs += [pl.BlockSpec(memory_space=pl.ANY)] * 4
        args += list(prev)
        aliases = {4 + k: k for k in range(4)}
    if after is not None:
        in_specs.append(pl.BlockSpec(memory_space=pl.ANY))
        args.append(after)
    return list(pl.pallas_call(
        body, name=name, grid=(R // tr,), in_specs=in_specs, out_specs=[blk] * 4, out_shape=[shp] * 4,
        input_output_aliases=aliases, compiler_params=_cp(("parallel",)),
    )(*args))


def _sum_adamw_w_in(parts0, parts1, w, m, v):
    wt, mt, vt = [jnp.transpose(a, (2, 0, 1)) for a in (w, m, v)]
    C, L2, R = wt.shape
    P = parts0.shape[0]
    tc = 128

    def body(p0_ref, p1_ref, w_ref, m_ref, v_ref, g_ref, d_ref, nm_ref, nv_ref):
        for l, p_ref in enumerate((p0_ref, p1_ref)):
            g = p_ref[0].astype(F32)
            for k in range(1, P):
                g = g + p_ref[k].astype(F32)
            nm = ADAM_B1 * m_ref[:, l, :] + (1.0 - ADAM_B1) * g
            nv = ADAM_B2 * v_ref[:, l, :] + (1.0 - ADAM_B2) * (g * g)
            m_hat = nm / (1.0 - ADAM_B1 ** ADAM_STEP)
            v_hat = nv / (1.0 - ADAM_B2 ** ADAM_STEP)
            g_ref[:, l, :] = g
            nm_ref[:, l, :] = nm
            nv_ref[:, l, :] = nv
            d_ref[:, l, :] = -ADAM_LR * (m_hat / (jnp.sqrt(v_hat) + ADAM_EPS) + ADAM_WD * w_ref[:, l, :])

    pb = pl.BlockSpec((P, C, tc), lambda j: (0, 0, j))
    wb = pl.BlockSpec((C, L2, tc), lambda j: (0, 0, j))
    shp = jax.ShapeDtypeStruct(wt.shape, F32)
    outs = pl.pallas_call(
        body, name="adamw_w_in", grid=(R // tc,), in_specs=[pb, pb, wb, wb, wb], out_specs=[wb] * 4,
        out_shape=[shp] * 4, compiler_params=_cp(("parallel",)),
    )(parts0, parts1, wt, mt, vt)
    return [jnp.transpose(o, (1, 2, 0)) for o in outs]


SMALL = ["norm_w", "mla_q_norm", "mla_kv_norm", "gla_w_g2", "gla_b_g2", "gla_norm", "final_norm"]


SMALL_ROWS = 72


def _pack_small(loss, part):
    flat = [jnp.pad(loss.reshape(1), (0, 127))] + [part[n].reshape(-1) for n in SMALL]
    used = sum(f.shape[0] for f in flat)
    flat.append(jnp.zeros((SMALL_ROWS * 128 - used,), F32))
    return jnp.concatenate(flat).reshape(SMALL_ROWS, 128)


def _small_adamw(packed_parts, w, m, v):
    n = len(w)

    def body(*refs):
        p_ref = refs[0]
        w_refs, m_refs, v_refs = refs[1:1 + n], refs[1 + n:1 + 2 * n], refs[1 + 2 * n:1 + 3 * n]
        outs, acc = refs[1 + 3 * n:-1], refs[-1]
        total = p_ref[0]
        for k in range(1, N_DEV):
            total = total + p_ref[k]
        acc[...] = total
        outs[0][...] = acc[0:1, :]
        r0 = 1
        for i in range(n):
            shp = w_refs[i].shape
            if len(shp) == 3:
                g = acc[r0:r0 + shp[0] * shp[1], :].reshape(shp)
                r0 += shp[0] * shp[1]
            elif shp[1] < 128:
                g = acc[r0:r0 + shp[0], 0:shp[1]]
                r0 += shp[0]
            else:
                k = shp[1] // 128
                g = jnp.concatenate(
                    [jnp.concatenate([acc[r0 + l * k + j:r0 + l * k + j + 1, :] for j in range(k)], axis=1)
                     for l in range(shp[0])], axis=0)
                r0 += shp[0] * k
            nm = ADAM_B1 * m_refs[i][...] + (1.0 - ADAM_B1) * g
            nv = ADAM_B2 * v_refs[i][...] + (1.0 - ADAM_B2) * (g * g)
            m_hat = nm / (1.0 - ADAM_B1 ** ADAM_STEP)
            v_hat = nv / (1.0 - ADAM_B2 ** ADAM_STEP)
            outs[1 + 4 * i][...] = g
            outs[2 + 4 * i][...] = -ADAM_LR * (m_hat / (jnp.sqrt(v_hat) + ADAM_EPS) + ADAM_WD * w_refs[i][...])
            outs[3 + 4 * i][...] = nm
            outs[4 + 4 * i][...] = nv

    vmem = pl.BlockSpec(memory_space=pltpu.VMEM)
    out_shape = [jax.ShapeDtypeStruct((1, 128), F32)]
    for a in w:
        out_shape += [jax.ShapeDtypeStruct(a.shape, F32)] * 4
    outs = pl.pallas_call(
        body, name="adamw_small", in_specs=[vmem] * (1 + 3 * n), out_specs=[vmem] * (1 + 4 * n), out_shape=out_shape,
        scratch_shapes=[pltpu.VMEM((SMALL_ROWS, 128), F32)],
    )(packed_parts, *w, *m, *v)
    return outs[0], [outs[1 + 4 * i:5 + 4 * i] for i in range(n)]


WEIGHTS = ["norm_w", "ada_w", "ada_b", "w_in", "mla_q_norm", "w_uq", "mla_kv_norm", "w_ukv", "gla_w_g2",
           "gla_b_g2", "gla_norm", "w_out", "final_norm"]


def kernel(x, c, positions, norm_w, ada_w, ada_b, w_in, mla_q_norm, w_uq, mla_kv_norm, w_ukv, gla_w_g2, gla_b_g2, gla_norm, w_out, final_norm, loss_target, m_norm_w, m_ada_w, m_ada_b, m_w_in, m_mla_q_norm, m_w_uq, m_mla_kv_norm, m_w_ukv, m_gla_w_g2, m_gla_b_g2, m_gla_norm, m_w_out, m_final_norm, v_norm_w, v_ada_w, v_ada_b, v_w_in, v_mla_q_norm, v_w_uq, v_mla_kv_norm, v_w_ukv, v_gla_w_g2, v_gla_b_g2, v_gla_norm, v_w_out, v_final_norm):
    w = dict(norm_w=norm_w, ada_w=ada_w, ada_b=ada_b, w_in=w_in, mla_q_norm=mla_q_norm, w_uq=w_uq,
             mla_kv_norm=mla_kv_norm, w_ukv=w_ukv, gla_w_g2=gla_w_g2, gla_b_g2=gla_b_g2, gla_norm=gla_norm,
             w_out=w_out, final_norm=final_norm)
    m = dict(norm_w=m_norm_w, ada_w=m_ada_w, ada_b=m_ada_b, w_in=m_w_in, mla_q_norm=m_mla_q_norm, w_uq=m_w_uq,
             mla_kv_norm=m_mla_kv_norm, w_ukv=m_w_ukv, gla_w_g2=m_gla_w_g2, gla_b_g2=m_gla_b_g2,
             gla_norm=m_gla_norm, w_out=m_w_out, final_norm=m_final_norm)
    v = dict(norm_w=v_norm_w, ada_w=v_ada_w, ada_b=v_ada_b, w_in=v_w_in, mla_q_norm=v_mla_q_norm, w_uq=v_w_uq,
             mla_kv_norm=v_mla_kv_norm, w_ukv=v_w_ukv, gla_w_g2=v_gla_w_g2, gla_b_g2=v_gla_b_g2,
             gla_norm=v_gla_norm, w_out=v_w_out, final_norm=v_final_norm)
    B, S, D = x.shape
    me = 4 * lax.axis_index("x") + 2 * lax.axis_index("y") + lax.axis_index("c")
    ada_cols = ada_w.shape[2]
    cast = lambda a: a.astype(_MXU)

    sharded = ["w_in", "w_uq", "w_ukv", "w_out"]

    whole_cols = lambda a: jnp.transpose(a, (1, 0, 2)).reshape(a.shape[1], -1)
    whole_in = lambda blk: _arrange_w_in(whole_cols(blk))
    whole_rest = lambda blks: (_arrange_w_uq(whole_cols(blks[0])), _arrange_w_ukv(whole_cols(blks[1])),
                               blks[2].reshape(D, D))
    col_blocks = lambda a: jnp.transpose(a.reshape(a.shape[0], N_DEV, -1), (1, 0, 2)).astype(jnp.bfloat16)
    blocks_in = lambda dw_in_a: jnp.transpose(_unarrange_w_in(dw_in_a).reshape(D, N_DEV, -1),
                                              (1, 2, 0)).astype(jnp.bfloat16)
    blocks_rest = lambda dw_out, dw_uq_a, dw_ukv_a: [
        col_blocks(_unarrange_w_uq(dw_uq_a)), col_blocks(_unarrange_w_ukv(dw_ukv_a)),
        dw_out.reshape(N_DEV, D // N_DEV, D).astype(jnp.bfloat16)]

    (c_g,) = _exchange([c], [True], "gather_c")
    c_all = c_g.reshape(N_DEV * B, D)

    ada_b_cols = lax.dynamic_slice(ada_b, (0, me * ada_cols), (DEPTH, ada_cols))
    mod_cols = _ada_fwd(c_all, ada_w, ada_b_cols)
    mod_send = jnp.transpose(mod_cols.reshape(DEPTH, N_DEV, B, ada_cols), (1, 0, 2, 3))
    (mod_recv,) = _exchange([mod_send], [False], "scatter_mod")
    mod = jnp.transpose(mod_recv, (1, 2, 0, 3)).reshape(DEPTH, B, 3 * D)

    flight_i = _exchange_start([cast(w_in[0])], [True], "gather_start_first", after=mod)
    flight_r = _exchange_start([cast(w[n][0]) for n in sharded[1:]], [True] * 3, "gather_start_layer0",
                               after=flight_i["token"])
    flight_w = _exchange_start([cast(w[n][1]) for n in sharded], [True] * 4, "gather_start_layer1",
                               after=flight_r["token"])
    small_w = {n: w[n] for n in SMALL}
    layer_small = lambda l: {n: a[l] for n, a in small_w.items() if n != "final_norm"}
    tabs = _rope_tables(positions.reshape(B, S, 1), flight_w["token"][0, 0])
    late0 = lambda after: whole_rest(_exchange_wait(flight_r, after, me, "gather_wait_layer0"))
    (w_in0_g,) = _exchange_wait(flight_i, tabs[0], me, "gather_wait_first")
    x1, saved0 = _layer_fwd(x, tabs, mod[0], layer_small(0), whole_in(w_in0_g), late_weights=late0)
    got1 = _exchange_wait(flight_w, x1, me, "gather_wait_layer1")
    (dx, loss, d_fw), saved1 = _layer_fwd(x1, tabs, mod[1], layer_small(1), whole_in(got1[0]), *whole_rest(got1[1:]),
                                          loss_head=(final_norm.reshape(1, D), loss_target))

    dx, g1 = _layer_bwd(dx, saved1, tabs)
    flight_g = _exchange_start([blocks_in(g1["w_in_a"])] + blocks_rest(g1["w_out"], g1["w_uq_a"], g1["w_ukv_a"]),
                               [False] * 4, "grads_start_layer1")
    flights = {}

    def early0(dw_out, dw_uq_a, dw_ukv_a):
        flights["rest0"] = _exchange_start(blocks_rest(dw_out, dw_uq_a, dw_ukv_a), [False] * 3, "grads_start_layer0")
        return flights["rest0"]["token"][0, 0]

    saved0 = dict(saved0, gate=saved0["gate"] + flight_g["token"][0, 0])
    grad_x, g0 = _layer_bwd(dx, saved0, tabs, early_grads=early0)
    parts1 = _exchange_wait(flight_g, grad_x, me, "grads_wait_layer1")
    rest0 = _exchange_wait(flights["rest0"], g0["w_in_a"], me, "grads_wait_layer0")

    both = lambda n: jnp.stack([g0[n], g1[n]])
    d_mod = both("d_mod")
    part = dict(norm_w=both("norm_w"), mla_q_norm=both("mla_q_norm"), mla_kv_norm=both("mla_kv_norm"),
                gla_w_g2=both("gla_w_g2"), gla_b_g2=both("gla_b_g2"), gla_norm=both("gla_norm256")[:, 0:128],
                final_norm=d_fw)
    flight_l = _exchange_start([d_mod, _pack_small(loss, part), blocks_in(g0["w_in_a"])], [True, True, False],
                               "exchange_start_last")
    res = {}
    behind = flight_l["token"]
    for a, name in enumerate(sharded[1:]):
        res[name] = _sum_adamw_layer(parts1[1 + a], w[name], m[name], v[name], 1, "adamw_%s_layer1" % name,
                                     after=behind)
        behind = res[name][1]
    for a, name in enumerate(sharded[1:]):
        res[name] = _sum_adamw_layer(rest0[a], w[name], m[name], v[name], 0, "adamw_%s_layer0" % name,
                                     prev=res[name], after=behind)
        behind = res[name][1]
    d_mod_g, small_g, in0 = _exchange_wait(flight_l, behind, me, "exchange_wait_last")
    res["w_in"] = _sum_adamw_w_in(in0, parts1[0], w_in, m_w_in, v_w_in)

    d_mod_all = jnp.transpose(d_mod_g, (1, 0, 2, 3)).reshape(DEPTH, N_DEV * B, 3 * D)
    d_mod_cols = lax.dynamic_slice(d_mod_all, (0, 0, me * ada_cols), (DEPTH, N_DEV * B, ada_cols))
    g_ada_w = _ada_bwd(c_all, d_mod_cols)

    def update(name, parts2d):
        shp = w[name].shape
        two = lambda a: a.reshape(parts2d.shape[1:])
        out = _sum_adamw(parts2d, two(w[name]), two(m[name]), two(v[name]), "adamw_" + name)
        res[name] = [o.reshape(shp) for o in out]

    update("ada_w", g_ada_w.reshape(1, DEPTH * D, ada_cols))
    update("ada_b", jnp.transpose(d_mod_g, (0, 2, 1, 3)).reshape(N_DEV * B, DEPTH * 3 * D // 128, 128))
    row = lambda a: a.reshape(1, D) if a.ndim == 1 else a
    loss_sum, small_out = _small_adamw(small_g, [row(w[n]) for n in SMALL], [row(m[n]) for n in SMALL],
                                       [row(v[n]) for n in SMALL])
    for n, outs in zip(SMALL, small_out):
        res[n] = [o.reshape(w[n].shape) for o in outs]
    loss_out = loss_sum[0, 0]
    return (loss_out, grad_x, *[res[n][0] for n in WEIGHTS], *[res[n][1] for n in WEIGHTS],
            *[res[n][2] for n in WEIGHTS], *[res[n][3] for n in WEIGHTS])
```

```python
import functools
import math

import numpy as np
import jax
import jax.numpy as jnp
from jax import lax
from jax.experimental import pallas as pl
from jax.experimental.pallas import tpu as pltpu

F32 = jnp.float32
_MXU = jnp.bfloat16

D_MODEL = 1024
DEPTH = 2
CHUNK = 64
EPS = 1e-6
ROPE_THETA = 10000.0
N_DEV = 8

MLA_SCALE = 96.0 ** -0.5
RET_KSCALE = 64.0 ** -0.5
GLA_KSCALE = 32.0 ** -0.5
GLA_TAU = 16.0

ADAM_LR = 0.001
ADAM_B1 = 0.9
ADAM_B2 = 0.999
ADAM_EPS = 1e-08
ADAM_WD = 0.01
ADAM_STEP = 10

RET_W, MLA_W, GLA_W = 1024, 1024, 896
ARR_W = RET_W + MLA_W + GLA_W
VMEM_MB = 1024 * 1024


def _cp(sem, vmem_mb=48):
    return pltpu.CompilerParams(dimension_semantics=sem, vmem_limit_bytes=vmem_mb * VMEM_MB)


def _mm(a, b):
    return jnp.dot(a.astype(_MXU), b.astype(_MXU), preferred_element_type=F32)


def _mm_nt(a, b):
    return lax.dot_general(a.astype(_MXU), b.astype(_MXU), (((1,), (1,)), ((), ())),
                           preferred_element_type=F32)


def _mm_tn(a, b):
    return lax.dot_general(a.astype(_MXU), b.astype(_MXU), (((0,), (0,)), ((), ())),
                           preferred_element_type=F32)


def _mm_f32(a, b):
    return jnp.dot(a, b, precision=lax.Precision.HIGHEST, preferred_element_type=F32)


def _sig(z):
    return 1.0 / (1.0 + jnp.exp(-z))


def _silu(z):
    return z * _sig(z)


def _dsilu(z):
    s = _sig(z)
    return s * (1.0 + z * (1.0 - s))


def _full(shape):
    nd = len(shape)
    return pl.BlockSpec(shape, lambda *_: (0,) * nd)


def _qk_perm(blk):
    r = blk.shape[0]
    return jnp.transpose(blk.reshape(r, 4, 2, 32), (0, 2, 1, 3)).reshape(r, 256)


def _qk_unperm(blk):
    r = blk.shape[0]
    return jnp.transpose(blk.reshape(r, 2, 4, 32), (0, 2, 1, 3)).reshape(r, 256)


def _arrange_w_in(w):
    z = lambda n: jnp.zeros((w.shape[0], n), w.dtype)
    ret = [_qk_perm(w[:, 0:256]), _qk_perm(w[:, 256:512]), w[:, 512:768], w[:, 768:1024]]
    mla = [w[:, 1024:1280], w[:, 1280:1408], z(64), w[:, 1408:1440], z(32), w[:, 1440:1952]]
    gla = [w[:, 1952:2080], w[:, 2080:2208], w[:, 2208:2464], w[:, 2464:2480], z(112), w[:, 2480:2736]]
    return jnp.concatenate(ret + mla + gla, axis=1)


def _unarrange_w_in(a):
    m, g = RET_W, RET_W + MLA_W
    parts = [_qk_unperm(a[:, 0:256]), _qk_unperm(a[:, 256:512]), a[:, 512:1024],
             a[:, m:m + 384], a[:, m + 448:m + 480], a[:, m + 512:m + 1024],
             a[:, g:g + 528], a[:, g + 640:g + 896]]
    return jnp.concatenate(parts, axis=1)


def _arrange_w_uq(w):
    return jnp.pad(w.reshape(256, 8, 96), ((0, 0), (0, 0), (0, 32))).reshape(256, 1024)


def _unarrange_w_uq(a):
    return a.reshape(256, 8, 128)[:, :, :96].reshape(256, 768)


def _arrange_w_ukv(w):
    r = w.reshape(128, 8, 128)
    k = jnp.pad(r[:, :, :64], ((0, 0), (0, 0), (0, 64))).reshape(128, 1024)
    return jnp.concatenate([k, r[:, :, 64:].reshape(128, 512)], axis=1)


def _unarrange_w_ukv(a):
    k = a[:, :1024].reshape(128, 8, 128)[:, :, :64]
    v = a[:, 1024:].reshape(128, 8, 64)
    return jnp.concatenate([k, v], axis=2).reshape(128, 1024)


def _rope_tables(pos3, zero=0.0):
    B, S, _ = pos3.shape
    ts = min(S, 512)
    inv32 = (np.float32(ROPE_THETA) ** (-(np.arange(32, dtype=np.float32) / 32))).astype(np.float32)
    inv16 = (np.float32(ROPE_THETA) ** (-(np.arange(16, dtype=np.float32) / 16))).astype(np.float32)
    inv = np.zeros((1, 128), np.float32)
    inv[0, 0:32] = inv32
    inv[0, 32:48] = inv16

    def body(pos_ref, inv_ref, cr, sr, cm, sm):
        ang = pos_ref[0].astype(F32) * inv_ref[...]
        lane = lax.broadcasted_iota(jnp.int32, (1, 128), 1)

        def every_head(x):
            y = jnp.where(lane < 32, x, pltpu.roll(x, 32, 1))
            return jnp.where(lane < 64, y, pltpu.roll(y, 64, 1))

        def rotary_pair(x, fill):
            return jnp.where((lane >= 64) & (lane < 80), pltpu.roll(x, 32, 1),
                             jnp.where((lane >= 80) & (lane < 96), pltpu.roll(x, 48, 1), fill))

        c, s = jnp.cos(ang), jnp.sin(ang)
        cr[0] = every_head(c)
        sr[0] = every_head(s)
        cm[0] = rotary_pair(c, 1.0)
        sm[0] = rotary_pair(s, 0.0)

    tab = jax.ShapeDtypeStruct((B, S, 128), F32)
    blk = pl.BlockSpec((1, ts, 128), lambda b, i: (b, i, 0))
    return pl.pallas_call(
        body, name="rope_tables", grid=(B, S // ts),
        in_specs=[pl.BlockSpec((1, ts, 1), lambda b, i: (b, i, 0)), _full((1, 128))],
        out_specs=[blk, blk, blk, blk], out_shape=[tab, tab, tab, tab],
        compiler_params=_cp(("parallel", "parallel")),
    )(pos3, jnp.asarray(inv) + zero)


def _rope128(x, cos, sin):
    lane = lax.broadcasted_iota(jnp.int32, (1, 128), 1)
    rp = pltpu.roll(x, 16, 1)
    rm = pltpu.roll(x, 112, 1)
    return x * cos + jnp.where(lane < 80, -rm, rp) * sin


def _rope128_t(d, cos, sin):
    lane = lax.broadcasted_iota(jnp.int32, (1, 128), 1)
    y = d * sin
    yp = pltpu.roll(y, 16, 1)
    ym = pltpu.roll(y, 112, 1)
    return d * cos + jnp.where(lane < 64, 0.0, jnp.where(lane < 80, ym, jnp.where(lane < 96, -yp, 0.0)))


def _proj_fwd(x, shift, scale, nw, w_arr):
    B, S, D = x.shape
    tm = min(S, 512)

    def body(x_ref, sh_ref, sc_ref, nw_ref, w_ref, ret_ref, mla_ref, gla_ref, h_ref):
        xv = x_ref[0]
        rstd = lax.rsqrt(jnp.mean(xv * xv, axis=-1, keepdims=True) + EPS)
        h = (xv * rstd * nw_ref[...]) * (1.0 + sc_ref[0]) + sh_ref[0]
        hb = h.astype(_MXU)
        h_ref[0] = hb
        ret_ref[0] = jnp.dot(hb, w_ref[:, 0:RET_W], preferred_element_type=F32).astype(_MXU)
        mla_ref[0] = jnp.dot(hb, w_ref[:, RET_W:RET_W + MLA_W], preferred_element_type=F32).astype(_MXU)
        gla_ref[0] = jnp.dot(hb, w_ref[:, RET_W + MLA_W:ARR_W], preferred_element_type=F32).astype(_MXU)

    tok = lambda w: pl.BlockSpec((1, tm, w), lambda b, i: (b, i, 0))
    per_seq = pl.BlockSpec((1, 1, D), lambda b, i: (b, 0, 0))
    return pl.pallas_call(
        body, name="proj_fwd", grid=(B, S // tm),
        in_specs=[tok(D), per_seq, per_seq, _full((1, D)), _full((D, ARR_W))],
        out_specs=[tok(RET_W), tok(MLA_W), tok(GLA_W), tok(D)],
        out_shape=[jax.ShapeDtypeStruct((B, S, RET_W), _MXU), jax.ShapeDtypeStruct((B, S, MLA_W), _MXU),
                   jax.ShapeDtypeStruct((B, S, GLA_W), _MXU), jax.ShapeDtypeStruct((B, S, D), _MXU)],
        compiler_params=_cp(("parallel", "parallel")),
    )(x, shift, scale, nw, w_arr)


RET_L = 256


def _ret_consts(L):
    lg = np.log1p(-np.exp2(-5.0 - np.arange(4, dtype=np.float32))).astype(np.float32)
    i = np.arange(L)
    ci = i // CHUNK
    diff = (i[:, None] - i[None, :]).astype(np.float32)
    same = ci[:, None] == ci[None, :]
    past = ci[None, :] < ci[:, None]
    expo = np.where(same, np.abs(diff), np.where(past, diff, 0.0)).astype(np.float32)
    dec = np.where((same | past)[None], np.exp(lg[:, None, None] * expo[None]), 0.0).astype(np.float32)
    head = (np.arange(256) % 128) // 32
    qw = np.exp((i + 1.0)[:, None] * lg[head][None, :]).astype(np.float32)
    kw = np.exp((L - 1.0 - i)[:, None] * lg[head][None, :]).astype(np.float32)
    a_row = np.exp(np.float32(L) * lg[head])[None, :].astype(np.float32)
    return [jnp.asarray(t) for t in (dec.reshape(4 * L, L), qw, kw, a_row)]


def _ret_masks():
    lane = lax.broadcasted_iota(jnp.int32, (1, 256), 1)
    mh = [((lane % 128) // 32) == h for h in range(4)]
    mv = [(lane // 64) == h for h in range(4)]
    vi = lax.broadcasted_iota(jnp.int32, (256, 256), 0)
    ki = lax.broadcasted_iota(jnp.int32, (256, 256), 1)
    bd = (vi // 64) == ((ki % 128) // 32)
    return mh, mv, bd


def _ret_rope(p, cs, sn):
    q1, q2, k1, k2 = p[:, 0:128], p[:, 128:256], p[:, 256:384], p[:, 384:512]
    qr = jnp.concatenate([q1 * cs - q2 * sn, q2 * cs + q1 * sn], axis=1)
    kr = jnp.concatenate([k1 * cs - k2 * sn, k2 * cs + k1 * sn], axis=1) * RET_KSCALE
    return qr, kr


def _head_mean(x, mv, width):
    out = jnp.zeros_like(x)
    for m in mv:
        s = jnp.sum(jnp.where(m, x, 0.0), axis=-1, keepdims=True) * (1.0 / width)
        out = jnp.where(m, s, out)
    return out


def _stack_heads(x, masks):
    return jnp.concatenate([jnp.where(m, x, 0.0) for m in masks], axis=0)


def _fold_heads(xs, masks, L):
    out = jnp.where(masks[0], xs[0:L], 0.0)
    for h in range(1, 4):
        out = out + jnp.where(masks[h], xs[h * L:(h + 1) * L], 0.0)
    return out


RET_G = 2


def _ret_fwd(ret_p, cos, sin):
    B, S, _ = ret_p.shape
    L = min(RET_L, S)
    NB = S // L
    G = min(RET_G, NB)
    NG = NB // G
    consts = _ret_consts(L)

    def body(p_ref, c_ref, s_ref, ds_ref, qw_ref, kw_ref, a_ref, out_ref, raw_ref, st_ref, st_sc):
        @pl.when(pl.program_id(1) == 0)
        def _():
            st_sc[...] = jnp.zeros_like(st_sc)

        mh, mv, bd = _ret_masks()
        cs_ = range(G)
        rows = [slice(c * L, (c + 1) * L) for c in cs_]
        ps = [p_ref[0, rows[c], :].astype(F32) for c in cs_]
        qk = [_ret_rope(ps[c], c_ref[0, rows[c], :], s_ref[0, rows[c], :]) for c in cs_]
        vs = [ps[c][:, 512:768] for c in cs_]
        a_s = [_mm_nt(_stack_heads(qk[c][0], mh), qk[c][1]) for c in cs_]
        upd = [_mm_tn(vs[c], qk[c][1] * kw_ref[...]) for c in cs_]
        o_s = [_mm(a_s[c] * ds_ref[...], vs[c]) for c in cs_]
        st = st_sc[...]
        inter = []
        for c in cs_:
            st_ref[0, c] = st
            inter.append(_mm_nt(qk[c][0] * qw_ref[...], st))
            st = st * a_ref[...] + jnp.where(bd, upd[c], 0.0)
        st_sc[...] = st
        for c in cs_:
            r = _fold_heads(o_s[c], mv, L) + inter[c]
            raw_ref[0, rows[c], :] = r
            rstd = lax.rsqrt(_head_mean(r * r, mv, 64.0) + EPS)
            out_ref[0, rows[c], :] = (r * rstd * _silu(ps[c][:, 768:1024])).astype(_MXU)

    tok = lambda w: pl.BlockSpec((1, G * L, w), lambda b, n: (b, n, 0))
    return pl.pallas_call(
        body, name="ret_fwd", grid=(B, NG),
        in_specs=[tok(RET_W), tok(128), tok(128), _full((4 * L, L)), _full((L, 256)), _full((L, 256)),
                  _full((1, 256))],
        out_specs=[tok(256), tok(256), pl.BlockSpec((1, G, 256, 256), lambda b, n: (b, n, 0, 0))],
        out_shape=[jax.ShapeDtypeStruct((B, S, 256), _MXU), jax.ShapeDtypeStruct((B, S, 256), F32),
                   jax.ShapeDtypeStruct((B, NB, 256, 256), F32)],
        scratch_shapes=[pltpu.VMEM((256, 256), F32)],
        compiler_params=_cp(("parallel", "arbitrary")),
    )(ret_p, cos, sin, *consts)


def _ret_bwd(ret_p, cos, sin, raw, states, d_mix):
    B, S, _ = ret_p.shape
    L = min(RET_L, S)
    NB = S // L
    G = 1
    NG = NB // G
    consts = _ret_consts(L)

    def body(p_ref, c_ref, s_ref, raw_ref, st_ref, dm_ref, ds_ref, qw_ref, kw_ref, a_ref, dp_ref, dst_sc):
        @pl.when(pl.program_id(1) == 0)
        def _():
            dst_sc[...] = jnp.zeros_like(dst_sc)

        mh, mv, bd = _ret_masks()
        qw, kw, dec = qw_ref[...], kw_ref[...], ds_ref[...]
        cs_ = range(G)
        rows = [slice(c * L, (c + 1) * L) for c in cs_]
        ps = [p_ref[0, rows[c], :].astype(F32) for c in cs_]
        tabs = [(c_ref[0, rows[c], :], s_ref[0, rows[c], :]) for c in cs_]
        qk = [_ret_rope(ps[c], *tabs[c]) for c in cs_]
        vs = [ps[c][:, 512:768] for c in cs_]
        qs = [_stack_heads(qk[c][0], mh) for c in cs_]
        a_s = [_mm_nt(qs[c], qk[c][1]) for c in cs_]
        dr, dz = [], []
        for c in cs_:
            r = raw_ref[0, rows[c], :]
            z = ps[c][:, 768:1024]
            rstd = lax.rsqrt(_head_mean(r * r, mv, 64.0) + EPS)
            rn = r * rstd
            dm = dm_ref[0, rows[c], :]
            d_rn = dm * _silu(z)
            dz.append(dm * rn * _dsilu(z))
            dr.append(rstd * (d_rn - rn * _head_mean(d_rn * rn, mv, 64.0)))
        do_s = [_stack_heads(dr[c], mv) for c in cs_]
        da_s = [_mm_nt(do_s[c], vs[c]) for c in cs_]
        sts = [st_ref[0, c] for c in cs_]
        dq_st = [_mm(dr[c], sts[c]) for c in cs_]
        dst_in = [_mm_tn(dr[c], qk[c][0] * qw) for c in cs_]
        dv = [_mm_tn(a_s[c] * dec, do_s[c]) for c in cs_]
        dqr, dkr = [], []
        for c in cs_:
            da = da_s[c] * dec
            dqr.append(_fold_heads(_mm(da, qk[c][1]), mh, L) + dq_st[c] * qw)
            dkr.append(_mm_tn(da, qs[c]))
        dst_next = dst_sc[...]
        for c in reversed(cs_):
            g = jnp.where(bd, dst_next, 0.0)
            dv[c] = dv[c] + _mm_nt(qk[c][1] * kw, g)
            dkr[c] = dkr[c] + _mm(vs[c], g) * kw
            dst_next = dst_next * a_ref[...] + jnp.where(bd, dst_in[c], 0.0)
        dst_sc[...] = dst_next
        for c in cs_:
            cs, sn = tabs[c]
            dk = dkr[c] * RET_KSCALE
            dq1, dq2 = dqr[c][:, 0:128], dqr[c][:, 128:256]
            dk1, dk2 = dk[:, 0:128], dk[:, 128:256]
            dp_ref[0, rows[c], :] = jnp.concatenate(
                [dq1 * cs + dq2 * sn, dq2 * cs - dq1 * sn, dk1 * cs + dk2 * sn, dk2 * cs - dk1 * sn, dv[c], dz[c]],
                axis=1).astype(_MXU)

    tok = lambda w: pl.BlockSpec((1, G * L, w), lambda b, i: (b, NG - 1 - i, 0))
    return pl.pallas_call(
        body, name="ret_bwd", grid=(B, NG),
        in_specs=[tok(RET_W), tok(128), tok(128), tok(256),
                  pl.BlockSpec((1, G, 256, 256), lambda b, i: (b, NG - 1 - i, 0, 0)), tok(256),
                  _full((4 * L, L)), _full((L, 256)), _full((L, 256)), _full((1, 256))],
        out_specs=tok(RET_W), out_shape=jax.ShapeDtypeStruct((B, S, RET_W), _MXU),
        scratch_shapes=[pltpu.VMEM((256, 256), F32)],
        compiler_params=_cp(("parallel", "arbitrary")),
    )(ret_p, cos, sin, raw, states, d_mix, *consts)


def _gla_masks():
    C = CHUNK
    lk = lax.broadcasted_iota(jnp.int32, (1, 128), 1)
    lv = lax.broadcasted_iota(jnp.int32, (1, 256), 1)
    mk = [(lk // 32) == h for h in range(4)]
    mv = [(lv // 64) == h for h in range(4)]
    vi = lax.broadcasted_iota(jnp.int32, (256, 128), 0)
    ki = lax.broadcasted_iota(jnp.int32, (256, 128), 1)
    bd = (vi // 64) == (ki // 32)
    ri = lax.broadcasted_iota(jnp.int32, (4 * C, C), 0) % C
    cj = lax.broadcasted_iota(jnp.int32, (4 * C, C), 1)
    lower = ri >= cj
    ti = lax.broadcasted_iota(jnp.int32, (C, C), 0)
    tj = lax.broadcasted_iota(jnp.int32, (C, C), 1)
    ltri = jnp.where(ti >= tj, 1.0, 0.0).astype(F32)
    utri = jnp.where(ti <= tj, 1.0, 0.0).astype(F32)
    return mk, mv, bd, lower, ltri, utri


def _log_sigmoid(x):
    return jnp.minimum(x, 0.0) - jnp.log(1.0 + jnp.exp(-jnp.abs(x)))


GLA_G = 8


def _gla_fwd(gla_p, w_g2p, b_g2, gnw):
    B, S, _ = gla_p.shape
    C = CHUNK
    NC = S // C
    G = min(GLA_G, NC)
    NG = NC // G

    def body(p_ref, w_ref, b_ref, gn_ref, out_ref, raw_ref, st_ref, st_sc):
        @pl.when(pl.program_id(1) == 0)
        def _():
            st_sc[...] = jnp.zeros_like(st_sc)

        mk, mv, bd, lower, ltri, _ = _gla_masks()
        cs = range(G)
        rows = [slice(c * C, (c + 1) * C) for c in cs]
        ps = [p_ref[0, rows[c], :].astype(F32) for c in cs]
        pre = [_mm(ps[c][:, 512:640], w_ref[...]) + b_ref[...] for c in cs]
        cum = [_mm_f32(ltri, _log_sigmoid(pre[c]) * (1.0 / GLA_TAU)) for c in cs]
        past, fut, upd, q_pos, a_row = [], [], [], [], []
        for c in cs:
            q = ps[c][:, 0:128]
            k = ps[c][:, 128:256] * GLA_KSCALE
            last = cum[c][C - 1:C, :]
            e_pos = jnp.exp(cum[c])
            e_neg = jnp.exp(-cum[c])
            q_pos.append(q * e_pos)
            a_row.append(jnp.exp(last))
            past.append(_mm_nt(_stack_heads(q_pos[c], mk), k * e_neg))
            fut.append(_mm_nt(_stack_heads(q * e_neg, mk), k * e_pos))
            upd.append(_mm_tn(ps[c][:, 256:512], k * jnp.exp(last - cum[c])))
        o_s = [_mm(jnp.where(lower, past[c], fut[c]), ps[c][:, 256:512]) for c in cs]
        st = st_sc[...]
        inter = []
        for c in cs:
            st_ref[0, c] = st
            inter.append(_mm_nt(q_pos[c], st))
            st = st * a_row[c] + jnp.where(bd, upd[c], 0.0)
        st_sc[...] = st
        for c in cs:
            g = _fold_heads(o_s[c], mv, C) + inter[c]
            raw_ref[0, rows[c], :] = g
            rstd = lax.rsqrt(_head_mean(g * g, mv, 64.0) + EPS)
            out_ref[0, rows[c], :] = (g * rstd * gn_ref[...] * _silu(ps[c][:, 640:896])).astype(_MXU)

    tok = lambda w: pl.BlockSpec((1, G * C, w), lambda b, n: (b, n, 0))
    return pl.pallas_call(
        body, name="gla_fwd", grid=(B, NG),
        in_specs=[tok(GLA_W), _full((128, 128)), _full((1, 128)), _full((1, 256))],
        out_specs=[tok(256), tok(256), pl.BlockSpec((1, G, 256, 128), lambda b, n: (b, n, 0, 0))],
        out_shape=[jax.ShapeDtypeStruct((B, S, 256), _MXU), jax.ShapeDtypeStruct((B, S, 256), F32),
                   jax.ShapeDtypeStruct((B, NC, 256, 128), F32)],
        scratch_shapes=[pltpu.VMEM((256, 128), F32)],
        compiler_params=_cp(("parallel", "arbitrary")),
    )(gla_p, w_g2p, b_g2, gnw)


def _gla_bwd(gla_p, w_g2p, b_g2, gnw, raw, states, d_mix):
    B, S, _ = gla_p.shape
    C = CHUNK
    NC = S // C
    G = min(GLA_G, NC)
    NG = NC // G

    def body(p_ref, w_ref, b_ref, gn_ref, raw_ref, st_ref, dm_ref, dp_ref, dw_ref, db_ref, dgn_ref, dst_sc):
        first = (pl.program_id(0) == 0) & (pl.program_id(1) == 0)

        @pl.when(first)
        def _():
            dw_ref[...] = jnp.zeros_like(dw_ref)
            db_ref[...] = jnp.zeros_like(db_ref)
            dgn_ref[...] = jnp.zeros_like(dgn_ref)

        @pl.when(pl.program_id(1) == 0)
        def _():
            dst_sc[...] = jnp.zeros_like(dst_sc)

        mk, mv, bd, lower, ltri, utri = _gla_masks()
        gn = gn_ref[...]
        cs = range(G)
        rows = [slice(c * C, (c + 1) * C) for c in cs]
        ps = [p_ref[0, rows[c], :].astype(F32) for c in cs]
        vs = [ps[c][:, 256:512] for c in cs]
        pre = [_mm(ps[c][:, 512:640], w_ref[...]) + b_ref[...] for c in cs]
        cum = [_mm_f32(ltri, _log_sigmoid(pre[c]) * (1.0 / GLA_TAU)) for c in cs]
        dg, dz, dgn_acc = [], [], jnp.zeros((1, 256), F32)
        for c in cs:
            g = raw_ref[0, rows[c], :]
            z = ps[c][:, 640:896]
            rstd = lax.rsqrt(_head_mean(g * g, mv, 64.0) + EPS)
            gh = g * rstd
            dm = dm_ref[0, rows[c], :]
            d_gn = dm * _silu(z)
            dz.append(dm * gh * gn * _dsilu(z))
            dgn_acc = dgn_acc + jnp.sum(d_gn * gh, axis=0, keepdims=True)
            d_gh = d_gn * gn
            dg.append(rstd * (d_gh - gh * _head_mean(d_gh * gh, mv, 64.0)))
        do_s = [_stack_heads(dg[c], mv) for c in cs]
        dattn = [_mm_nt(do_s[c], vs[c]) for c in cs]
        ks, e_pos, e_neg, q_pos, q_neg, k_pos, k_neg, qp_s, qn_s, past, fut, a_row, w_dec, kd = ([] for _ in range(14))
        for c in cs:
            q = ps[c][:, 0:128]
            k = ps[c][:, 128:256] * GLA_KSCALE
            last = cum[c][C - 1:C, :]
            ep, en = jnp.exp(cum[c]), jnp.exp(-cum[c])
            ks.append(k), e_pos.append(ep), e_neg.append(en)
            q_pos.append(q * ep), q_neg.append(q * en), k_pos.append(k * ep), k_neg.append(k * en)
            qp_s.append(_stack_heads(q_pos[c], mk)), qn_s.append(_stack_heads(q_neg[c], mk))
            past.append(_mm_nt(qp_s[c], k_neg[c]))
            fut.append(_mm_nt(qn_s[c], k_pos[c]))
            a_row.append(jnp.exp(last))
            w_dec.append(jnp.exp(last - cum[c]))
            kd.append(k * w_dec[c])
        sts = [st_ref[0, c] for c in cs]
        dq_st = [_mm(dg[c], sts[c]) for c in cs]
        dst_in = [_mm_tn(dg[c], q_pos[c]) for c in cs]
        dv, dq_pos, dk_neg, dq_neg, dk_pos = [], [], [], [], []
        for c in cs:
            attn = jnp.where(lower, past[c], fut[c])
            dpast = jnp.where(lower, dattn[c], 0.0)
            dfut = jnp.where(lower, 0.0, dattn[c])
            dv.append(_mm_tn(attn, do_s[c]))
            dq_pos.append(_fold_heads(_mm(dpast, k_neg[c]), mk, C) + dq_st[c])
            dk_neg.append(_mm_tn(dpast, qp_s[c]))
            dq_neg.append(_fold_heads(_mm(dfut, k_pos[c]), mk, C))
            dk_pos.append(_mm_tn(dfut, qn_s[c]))
        dst_next = dst_sc[...]
        d_a, d_kd = [None] * G, [None] * G
        for c in reversed(cs):
            d_a[c] = jnp.sum(dst_next * sts[c], axis=0, keepdims=True)
            gmat = jnp.where(bd, dst_next, 0.0)
            d_kd[c] = _mm(vs[c], gmat)
            dv[c] = dv[c] + _mm_nt(kd[c], gmat)
            dst_next = dst_next * a_row[c] + jnp.where(bd, dst_in[c], 0.0)
        dst_sc[...] = dst_next
        row = lax.broadcasted_iota(jnp.int32, (C, 128), 0)
        d_la, dk, dq = [], [], []
        for c in cs:
            t = d_kd[c] * kd[c]
            dk.append(d_kd[c] * w_dec[c] + dk_neg[c] * e_neg[c] + dk_pos[c] * e_pos[c])
            dq.append(dq_pos[c] * e_pos[c] + dq_neg[c] * e_neg[c])
            d_last = jnp.sum(t, axis=0, keepdims=True) + d_a[c] * a_row[c]
            d_cum = (dq_pos[c] * q_pos[c] - dk_neg[c] * k_neg[c] - dq_neg[c] * q_neg[c] + dk_pos[c] * k_pos[c] - t)
            d_la.append(_mm_f32(utri, d_cum + jnp.where(row == C - 1, d_last, 0.0)))
        d_pre = [d_la[c] * _sig(-pre[c]) * (1.0 / GLA_TAU) for c in cs]
        d_gg = [_mm_nt(d_pre[c], w_ref[...]) for c in cs]
        dw_acc = _mm_tn(ps[0][:, 512:640], d_pre[0])
        db_acc = jnp.sum(d_pre[0], axis=0, keepdims=True)
        for c in cs[1:]:
            dw_acc = dw_acc + _mm_tn(ps[c][:, 512:640], d_pre[c])
            db_acc = db_acc + jnp.sum(d_pre[c], axis=0, keepdims=True)
        for c in cs:
            dp_ref[0, rows[c], :] = jnp.concatenate([dq[c], dk[c] * GLA_KSCALE, dv[c], d_gg[c], dz[c]],
                                                    axis=1).astype(_MXU)
        dw_ref[...] += dw_acc
        db_ref[...] += db_acc
        dgn_ref[...] += dgn_acc

        @pl.when((pl.program_id(0) == B - 1) & (pl.program_id(1) == NG - 1))
        def _():
            s1 = dgn_ref[...]
            s1 = s1 + pltpu.roll(s1, 128, 1)
            dgn_ref[...] = s1 + pltpu.roll(s1, 64, 1)

    tok = lambda w: pl.BlockSpec((1, G * C, w), lambda b, i: (b, NG - 1 - i, 0))
    return pl.pallas_call(
        body, name="gla_bwd", grid=(B, NG),
        in_specs=[tok(GLA_W), _full((128, 128)), _full((1, 128)), _full((1, 256)), tok(256),
                  pl.BlockSpec((1, G, 256, 128), lambda b, i: (b, NG - 1 - i, 0, 0)), tok(256)],
        out_specs=[tok(GLA_W), _full((128, 128)), _full((1, 128)), _full((1, 256))],
        out_shape=[jax.ShapeDtypeStruct((B, S, GLA_W), _MXU), jax.ShapeDtypeStruct((128, 128), F32),
                   jax.ShapeDtypeStruct((1, 128), F32), jax.ShapeDtypeStruct((1, 256), F32)],
        scratch_shapes=[pltpu.VMEM((256, 128), F32)],
        compiler_params=_cp(("arbitrary", "arbitrary")),
    )(gla_p, w_g2p, b_g2, gnw, raw, states, d_mix)


def _rms(x, w):
    rstd = lax.rsqrt(jnp.mean(x * x, axis=-1, keepdims=True) + EPS)
    xh = x * rstd
    return xh, rstd, xh * w


def _rms_bwd(dy, xh, rstd, w):
    dxh = dy * w
    return rstd * (dxh - xh * jnp.mean(dxh * xh, axis=-1, keepdims=True))


MLA_T = 256


def _mla_prep_fwd(mla_p, cos, sin, qnw, kvnw, w_uq, w_ukv):
    B, S, _ = mla_p.shape
    tm = min(S, 512)

    t = min(MLA_T, S)
    nt = tm // t

    def body(p_ref, c_ref, s_ref, qn_ref, kn_ref, wq_ref, wkv_ref, wkvt_ref, q_ref, k_ref, v_ref, kt_ref, vt_ref):
        p = p_ref[0].astype(F32)
        cs, sn = c_ref[0], s_ref[0]
        _, _, qn = _rms(p[:, 0:256], qn_ref[...])
        qpre = _mm(qn, wq_ref[...])
        _, _, kvn = _rms(p[:, 256:384], kn_ref[...])
        kv = _mm(kvn, wkv_ref[...])
        kvt = _mm_nt(wkvt_ref[...], kvn)
        kpe = _rope128(p[:, 384:512], cs, sn)
        kpet = kpe.T
        for h in range(8):
            sl = slice(128 * h, 128 * h + 128)
            q_ref[0, :, sl] = _rope128(qpre[:, sl], cs, sn).astype(_MXU)
            k_ref[0, :, sl] = (kv[:, sl] + kpe).astype(_MXU)
            kht = kvt[sl, :] + kpet
            for n in range(nt):
                kt_ref[0, n, sl, :] = kht[:, n * t:(n + 1) * t].astype(_MXU)
        v_ref[0] = kv[:, 1024:1536].astype(_MXU)
        for n in range(nt):
            vt_ref[0, n] = kvt[1024:1536, n * t:(n + 1) * t].astype(_MXU)

    tok = lambda w: pl.BlockSpec((1, tm, w), lambda b, i: (b, i, 0))
    tr = lambda w: pl.BlockSpec((1, nt, w, t), lambda b, i: (b, i, 0, 0))
    return pl.pallas_call(
        body, name="mla_prep_fwd", grid=(B, S // tm),
        in_specs=[tok(512), tok(128), tok(128), _full((1, 256)), _full((1, 128)), _full((256, 1024)),
                  _full((128, 1536)), _full((1536, 128))],
        out_specs=[tok(1024), tok(1024), tok(512), tr(1024), tr(512)],
        out_shape=[jax.ShapeDtypeStruct((B, S, 1024), _MXU), jax.ShapeDtypeStruct((B, S, 1024), _MXU),
                   jax.ShapeDtypeStruct((B, S, 512), _MXU), jax.ShapeDtypeStruct((B, S // t, 1024, t), _MXU),
                   jax.ShapeDtypeStruct((B, S // t, 512, t), _MXU)],
        compiler_params=_cp(("parallel", "parallel")),
    )(mla_p, cos, sin, qnw, kvnw, w_uq, w_ukv, w_ukv.T)


def _chunk_mask_t(t):
    kj = lax.broadcasted_iota(jnp.int32, (t, t), 0) // CHUNK
    qi = lax.broadcasted_iota(jnp.int32, (t, t), 1) // CHUNK
    return kj <= qi


MLA_HG = 8
MLA_HG_FWD = 8
LOG2E = 1.4426950408889634
MLA_C2 = MLA_SCALE * LOG2E


def _mla_attn_fwd(q, k, vt):
    B, S, _ = q.shape
    t = min(MLA_T, S)
    nq = S // t
    HG = MLA_HG_FWD
    NP = HG // 2

    def body(q_ref, k_ref, vt_ref, o_ref, lse_ref, sa, sb, m_sc, l_sc, acc_sc):
        i = pl.program_id(2)
        row = lax.broadcasted_iota(jnp.int32, (128, 1), 0)
        low = row < 64
        mask = _chunk_mask_t(t)
        m_sc[...] = jnp.full(m_sc.shape, -jnp.inf, F32)
        l_sc[...] = jnp.zeros_like(l_sc)
        acc_sc[...] = jnp.zeros_like(acc_sc)

        ones = jnp.ones((8, t), _MXU)

        def scores(j, buf):
            kb = k_ref[0, pl.ds(pl.multiple_of(j * t, t), t), :]
            for h in range(HG):
                cols = slice(128 * h, 128 * h + 128)
                buf[h] = (_mm_nt(kb[:, cols], q_ref[0, :, cols]) * MLA_C2).astype(_MXU)

        def absorb(j, buf, masked):
            vtb = vt_ref[0, j]
            for pr in range(NP):
                alphas, pvs = [], []
                for hh in range(2):
                    h = 2 * pr + hh
                    s = buf[h]
                    if masked:
                        s = jnp.where(mask, s, jnp.full_like(s, -jnp.inf))
                    m_old = m_sc[h]
                    m_new = jnp.maximum(m_old, jnp.max(s, axis=0, keepdims=True).astype(F32))
                    alpha = jnp.exp2(m_old - m_new)
                    p = jnp.exp2(s - m_new.astype(_MXU))
                    l_sc[h] = alpha * l_sc[h] + _mm(ones, p)[0:1, :]
                    m_sc[h] = m_new
                    vth = vtb[128 * pr:128 * pr + 128, :]
                    vth = jnp.where(low if hh == 0 else ~low, vth, jnp.zeros_like(vth))
                    pvs.append(_mm(vth, p))
                    alphas.append(alpha)
                acc_sc[pr] = acc_sc[pr] * jnp.where(low, alphas[0], alphas[1]) + pvs[0] + pvs[1]

        scores(0, sb)

        def pair(jj, carry):
            j0 = 2 * jj
            scores(j0 + 1, sa)
            absorb(j0, sb, False)
            scores(j0 + 2, sb)
            absorb(j0 + 1, sa, False)
            return carry

        lax.fori_loop(0, i // 2, pair, 0)

        @pl.when(i % 2 == 1)
        def _():
            scores(i, sa)
            absorb(i - 1, sb, False)
            absorb(i, sa, True)

        @pl.when(i % 2 == 0)
        def _():
            absorb(i, sb, True)

        for pr in range(NP):
            l_e, l_o = l_sc[2 * pr], l_sc[2 * pr + 1]
            o_ref[0, :, 128 * pr:128 * pr + 128] = (acc_sc[pr] / jnp.where(low, l_e, l_o)).T
            lse_ref[0, pr, 0, 0:1, :] = m_sc[2 * pr] + jnp.log(l_e) * LOG2E
            lse_ref[0, pr, 0, 1:2, :] = m_sc[2 * pr + 1] + jnp.log(l_o) * LOG2E

    return pl.pallas_call(
        body, name="mla_attn_fwd", grid=(B, 8 // HG, nq),
        in_specs=[pl.BlockSpec((1, t, 128 * HG), lambda b, g, i: (b, i, g)),
                  pl.BlockSpec((1, S, 128 * HG), lambda b, g, i: (b, 0, g)),
                  pl.BlockSpec((1, nq, 64 * HG, t), lambda b, g, i: (b, 0, g, 0))],
        out_specs=[pl.BlockSpec((1, t, 64 * HG), lambda b, g, i: (b, i, g)),
                   pl.BlockSpec((1, NP, 1, 2, t), lambda b, g, i: (b, g, i, 0, 0))],
        out_shape=[jax.ShapeDtypeStruct((B, S, 512), F32), jax.ShapeDtypeStruct((B, 4, nq, 2, t), F32)],
        scratch_shapes=[pltpu.VMEM((HG, t, t), _MXU), pltpu.VMEM((HG, t, t), _MXU), pltpu.VMEM((HG, 1, t), F32),
                        pltpu.VMEM((HG, 1, t), F32), pltpu.VMEM((NP, 128, t), F32)],
        compiler_params=_cp(("parallel", "parallel", "arbitrary")),
    )(q, k, vt)


def _mla_attn_bwd(q, k, v, kt, do, lse, dl):
    B, S, _ = q.shape
    t = min(MLA_T, S)
    nk = S // t

    HG = MLA_HG
    NP = HG // 2

    def body(q_ref, k_ref, v_ref, kt_ref, do_ref, lse_ref, dl_ref, dq_ref, dk_ref, dv_ref,
             sa, da, sb, db, dqt_sc, dk_sc, dv_sc):
        j = pl.program_id(2)

        @pl.when(j == 0)
        def _():
            dqt_sc[...] = jnp.zeros_like(dqt_sc)

        dk_sc[...] = jnp.zeros_like(dk_sc)
        dv_sc[...] = jnp.zeros_like(dv_sc)
        lane = lax.broadcasted_iota(jnp.int32, (1, 128), 1)
        low = lane < 64
        mask = _chunk_mask_t(t)

        def half(x, hh):
            return jnp.where(low if hh == 0 else ~low, x, jnp.zeros_like(x))

        def prepare(i, sbuf, dbuf):
            rows = pl.ds(pl.multiple_of(i * t, t), t)
            for h in range(HG):
                cols = slice(128 * h, 128 * h + 128)
                pc = slice(128 * (h // 2), 128 * (h // 2) + 128)
                sbuf[h] = _mm_nt(k_ref[0, :, cols], q_ref[0, rows, cols]) * MLA_C2
                dbuf[h] = _mm_nt(half(v_ref[0, :, pc], h % 2), do_ref[0, rows, pc])

        def absorb(i, sbuf, dbuf, masked):
            rows = pl.ds(pl.multiple_of(i * t, t), t)
            for h in range(HG):
                pr, hh = h // 2, h % 2
                cols = slice(128 * h, 128 * h + 128)
                pc = slice(128 * pr, 128 * pr + 128)
                p = jnp.exp2(sbuf[h] - lse_ref[0, pr, i][hh:hh + 1, :])
                if masked:
                    p = jnp.where(mask, p, 0.0)
                dv_sc[pr] += _mm(p, half(do_ref[0, rows, pc], hh))
                ds = p * (dbuf[h] - dl_ref[0, pr, i][hh:hh + 1, :])
                dqt_sc[i, cols, :] += _mm(kt_ref[0, 0, cols, :], ds)
                dk_sc[h] += _mm(ds, q_ref[0, rows, cols])

        n = nk - 1 - j
        prepare(jnp.minimum(j + 1, nk - 1), sb, db)

        def pair(jj, carry):
            i0 = j + 1 + 2 * jj
            prepare(i0 + 1, sa, da)
            absorb(i0, sb, db, False)
            prepare(jnp.where(i0 + 2 <= nk - 1, i0 + 2, j), sb, db)
            absorb(i0 + 1, sa, da, False)
            return carry

        lax.fori_loop(0, n // 2, pair, 0)

        @pl.when(n % 2 == 1)
        def _():
            prepare(j, sa, da)
            absorb(nk - 1, sb, db, False)
            absorb(j, sa, da, True)

        @pl.when(n % 2 == 0)
        def _():
            absorb(j, sb, db, True)

        for h in range(HG):
            dk_ref[0, :, 128 * h:128 * h + 128] = (dk_sc[h] * MLA_SCALE).astype(_MXU)
        for pr in range(NP):
            dv_ref[0, :, 128 * pr:128 * pr + 128] = dv_sc[pr].astype(_MXU)

        @pl.when(j == nk - 1)
        def _():
            for i in range(nk):
                dq_ref[0, i * t:(i + 1) * t, :] = (dqt_sc[i].T * MLA_SCALE).astype(_MXU)

    seq = lambda w: pl.BlockSpec((1, S, w), lambda b, g, j: (b, 0, g))
    blk = lambda w: pl.BlockSpec((1, t, w), lambda b, g, j: (b, j, g))
    stat = pl.BlockSpec((1, NP, nk, 2, t), lambda b, g, j: (b, g, 0, 0, 0))
    return pl.pallas_call(
        body, name="mla_attn_bwd", grid=(B, 8 // HG, nk),
        in_specs=[seq(128 * HG), blk(128 * HG), blk(64 * HG),
                  pl.BlockSpec((1, 1, 128 * HG, t), lambda b, g, j: (b, j, g, 0)), seq(64 * HG), stat, stat],
        out_specs=[seq(128 * HG), blk(128 * HG), blk(64 * HG)],
        out_shape=[jax.ShapeDtypeStruct((B, S, 1024), _MXU), jax.ShapeDtypeStruct((B, S, 1024), _MXU),
                   jax.ShapeDtypeStruct((B, S, 512), _MXU)],
        scratch_shapes=[pltpu.VMEM((HG, t, t), F32), pltpu.VMEM((HG, t, t), F32), pltpu.VMEM((HG, t, t), F32),
                        pltpu.VMEM((HG, t, t), F32), pltpu.VMEM((nk, 128 * HG, t), F32),
                        pltpu.VMEM((HG, t, 128), F32), pltpu.VMEM((NP, t, 128), F32)],
        compiler_params=_cp(("parallel", "parallel", "arbitrary"), 56),
    )(q, k, v, kt, do, lse, dl)


def _mla_prep_bwd(mla_p, cos, sin, qnw, kvnw, w_uq, w_ukv, dq, dk, dv):
    B, S, _ = mla_p.shape
    tm = min(S, 512)

    def body(p_ref, c_ref, s_ref, qn_ref, kn_ref, wq_ref, wkv_ref, dq_ref, dk_ref, dv_ref,
             dp_ref, dwq_ref, dwkv_ref, dqn_ref, dkn_ref):
        first = (pl.program_id(0) == 0) & (pl.program_id(1) == 0)

        @pl.when(first)
        def _():
            dwq_ref[...] = jnp.zeros_like(dwq_ref)
            dwkv_ref[...] = jnp.zeros_like(dwkv_ref)
            dqn_ref[...] = jnp.zeros_like(dqn_ref)
            dkn_ref[...] = jnp.zeros_like(dkn_ref)

        p = p_ref[0].astype(F32)
        cs, sn = c_ref[0], s_ref[0]
        lane = lax.broadcasted_iota(jnp.int32, (1, 128), 1)
        pe = (lane >= 64) & (lane < 96)
        qh, q_rstd, qn = _rms(p[:, 0:256], qn_ref[...])
        kvh, kv_rstd, kvn = _rms(p[:, 256:384], kn_ref[...])
        dqv = dq_ref[0].astype(F32)
        dkv = dk_ref[0].astype(F32)
        dqpre = jnp.concatenate(
            [_rope128_t(dqv[:, 128 * h:128 * h + 128], cs, sn) for h in range(8)], axis=1)
        dkpe = jnp.zeros((tm, 128), F32)
        for h in range(8):
            dkpe = dkpe + jnp.where(pe, dkv[:, 128 * h:128 * h + 128], 0.0)
        dkr = _rope128_t(dkpe, cs, sn)
        dkv_all = jnp.concatenate([dkv, dv_ref[0].astype(F32)], axis=1)
        d_qn = _mm_nt(dqpre, wq_ref[...])
        d_kvn = _mm_nt(dkv_all, wkv_ref[...])
        dwq_ref[...] += _mm_tn(qn, dqpre)
        dwkv_ref[...] += _mm_tn(kvn, dkv_all)
        dqn_ref[...] += jnp.sum(d_qn * qh, axis=0, keepdims=True)
        dkn_ref[...] += jnp.sum(d_kvn * kvh, axis=0, keepdims=True)
        dp_ref[0] = jnp.concatenate([_rms_bwd(d_qn, qh, q_rstd, qn_ref[...]),
                                     _rms_bwd(d_kvn, kvh, kv_rstd, kn_ref[...]), dkr], axis=1).astype(_MXU)

    tok = lambda w: pl.BlockSpec((1, tm, w), lambda b, i: (b, i, 0))
    return pl.pallas_call(
        body, name="mla_prep_bwd", grid=(B, S // tm),
        in_specs=[tok(512), tok(128), tok(128), _full((1, 256)), _full((1, 128)), _full((256, 1024)),
                  _full((128, 1536)), tok(1024), tok(1024), tok(512)],
        out_specs=[tok(512), _full((256, 1024)), _full((128, 1536)), _full((1, 256)), _full((1, 128))],
        out_shape=[jax.ShapeDtypeStruct((B, S, 512), _MXU), jax.ShapeDtypeStruct((256, 1024), F32),
                   jax.ShapeDtypeStruct((128, 1536), F32), jax.ShapeDtypeStruct((1, 256), F32),
                   jax.ShapeDtypeStruct((1, 128), F32)],
        compiler_params=_cp(("arbitrary", "arbitrary")),
    )(mla_p, cos, sin, qnw, kvnw, w_uq, w_ukv, dq, dk, dv)


def _out_fwd(x, gate, r_g, o_mla, mla_p, g_g, w_out):
    B, S, D = x.shape
    tm = min(S, 512)

    def body(x_ref, g_ref, r_ref, o_ref, z_ref, gg_ref, w_ref, xn_ref, y_ref):
        mm = (o_ref[0] * _silu(z_ref[0].astype(F32))).astype(_MXU)
        y = (jnp.dot(r_ref[0], w_ref[0:256, :], preferred_element_type=F32)
             + jnp.dot(mm, w_ref[256:768, :], preferred_element_type=F32)
             + jnp.dot(gg_ref[0], w_ref[768:1024, :], preferred_element_type=F32))
        y_ref[0] = y.astype(_MXU)
        xn_ref[0] = x_ref[0] + g_ref[0] * y

    tok = lambda w, c=0: pl.BlockSpec((1, tm, w), lambda b, i: (b, i, c))
    return pl.pallas_call(
        body, name="out_fwd", grid=(B, S // tm),
        in_specs=[tok(D), pl.BlockSpec((1, 1, D), lambda b, i: (b, 0, 0)), tok(256), tok(512), tok(512, 1),
                  tok(256), _full((D, D))],
        out_specs=[tok(D), tok(D)],
        out_shape=[jax.ShapeDtypeStruct((B, S, D), F32), jax.ShapeDtypeStruct((B, S, D), _MXU)],
        compiler_params=_cp(("parallel", "parallel")),
    )(x, gate, r_g, o_mla, mla_p, g_g, w_out)


def _out_bwd(dx, y, gate, r_g, g_g, w_out, o_mla, mla_p):
    B, S, D = dx.shape
    tm = min(S, 512)
    t = min(MLA_T, S)
    nt = tm // t

    def body(dx_ref, y_ref, g_ref, r_ref, gg_ref, w_ref, o_ref, z_ref,
             dr_ref, do_ref, dz_ref, dl_ref, dg_ref, dw_ref, dgate_ref, acc):
        first = (pl.program_id(0) == 0) & (pl.program_id(1) == 0)

        @pl.when(first)
        def _():
            acc[...] = jnp.zeros_like(acc)

        @pl.when(pl.program_id(1) == 0)
        def _():
            dgate_ref[...] = jnp.zeros_like(dgate_ref)

        dxv = dx_ref[0]
        dgate_ref[0] += jnp.sum(dxv * y_ref[0].astype(F32), axis=0, keepdims=True)
        dy = (dxv * g_ref[0]).astype(_MXU)
        dr_ref[0] = _mm_nt(dy, w_ref[0:256, :])
        dg_ref[0] = _mm_nt(dy, w_ref[768:1024, :])
        ov, z = o_ref[0], z_ref[0].astype(F32)
        acc[0:256, :] += _mm_tn(r_ref[0], dy)
        acc[256:768, :] += _mm_tn((ov * _silu(z)).astype(_MXU), dy)
        acc[768:1024, :] += _mm_tn(gg_ref[0], dy)

        @pl.when((pl.program_id(0) == B - 1) & (pl.program_id(1) == S // tm - 1))
        def _():
            dw_ref[...] = acc[...].astype(_MXU)

        dm = _mm_nt(dy, w_ref[256:768, :])
        do = dm * _silu(z)
        dz_ref[0] = (dm * ov * _dsilu(z)).astype(_MXU)
        do_ref[0] = do.astype(_MXU)
        prod = do * ov
        for pr in range(4):
            pt = prod[:, 128 * pr:128 * pr + 128].T
            se = jnp.sum(pt[0:64], axis=0, keepdims=True)
            so = jnp.sum(pt[64:128], axis=0, keepdims=True)
            for n in range(nt):
                dl_ref[0, pr, n, 0:1, :] = se[:, n * t:(n + 1) * t]
                dl_ref[0, pr, n, 1:2, :] = so[:, n * t:(n + 1) * t]

    tok = lambda w, c=0: pl.BlockSpec((1, tm, w), lambda b, i: (b, i, c))
    per_seq = pl.BlockSpec((1, 1, D), lambda b, i: (b, 0, 0))
    return pl.pallas_call(
        body, name="out_bwd", grid=(B, S // tm),
        in_specs=[tok(D), tok(D), per_seq, tok(256), tok(256), _full((D, D)), tok(512), tok(512, 1)],
        out_specs=[tok(256), tok(512), tok(512), pl.BlockSpec((1, 4, nt, 2, t), lambda b, i: (b, 0, i, 0, 0)),
                   tok(256), _full((D, D)), per_seq],
        out_shape=[jax.ShapeDtypeStruct((B, S, 256), F32), jax.ShapeDtypeStruct((B, S, 512), _MXU),
                   jax.ShapeDtypeStruct((B, S, 512), _MXU), jax.ShapeDtypeStruct((B, 4, S // t, 2, t), F32),
                   jax.ShapeDtypeStruct((B, S, 256), F32), jax.ShapeDtypeStruct((D, D), _MXU),
                   jax.ShapeDtypeStruct((B, 1, D), F32)],
        scratch_shapes=[pltpu.VMEM((D, D), F32)],
        compiler_params=_cp(("arbitrary", "arbitrary")),
    )(dx, y, gate, r_g, g_g, w_out, o_mla, mla_p)


def _proj_bwd_x(x, shift, scale, nw, w_arr, d_ret, d_mla, d_mz, d_gla, dx_out):
    B, S, D = x.shape
    tm = min(S, 512)

    def body(x_ref, sc_ref, nw_ref, w_ref, dr_ref, dm_ref, dz_ref, dg_ref, dxo_ref,
             dx_ref, dsh_ref, dsc_ref, dnw_ref):
        first = (pl.program_id(0) == 0) & (pl.program_id(1) == 0)

        @pl.when(first)
        def _():
            dnw_ref[...] = jnp.zeros_like(dnw_ref)

        @pl.when(pl.program_id(1) == 0)
        def _():
            dsh_ref[...] = jnp.zeros_like(dsh_ref)
            dsc_ref[...] = jnp.zeros_like(dsc_ref)

        dp = jnp.concatenate([dr_ref[0], dm_ref[0], dz_ref[0], dg_ref[0]], axis=1)
        dh = lax.dot_general(dp, w_ref[...], (((1,), (1,)), ((), ())), preferred_element_type=F32)
        xv = x_ref[0]
        rstd = lax.rsqrt(jnp.mean(xv * xv, axis=-1, keepdims=True) + EPS)
        xh = xv * rstd
        nwv = nw_ref[...]
        mod = 1.0 + sc_ref[0]
        dsh_ref[0] += jnp.sum(dh, axis=0, keepdims=True)
        dsc_ref[0] += jnp.sum(dh * xh * nwv, axis=0, keepdims=True)
        dnw_ref[...] += jnp.sum(dh * xh * mod, axis=0, keepdims=True)
        dxh = dh * nwv * mod
        dx_ref[0] = dxo_ref[0] + rstd * (dxh - xh * jnp.mean(dxh * xh, axis=-1, keepdims=True))

    tok = lambda w: pl.BlockSpec((1, tm, w), lambda b, i: (b, i, 0))
    per_seq = pl.BlockSpec((1, 1, D), lambda b, i: (b, 0, 0))
    return pl.pallas_call(
        body, name="proj_bwd_x", grid=(B, S // tm),
        in_specs=[tok(D), per_seq, _full((1, D)), _full((D, ARR_W)), tok(RET_W), tok(512), tok(512),
                  tok(GLA_W), tok(D)],
        out_specs=[tok(D), per_seq, per_seq, _full((1, D))],
        out_shape=[jax.ShapeDtypeStruct((B, S, D), F32), jax.ShapeDtypeStruct((B, 1, D), F32),
                   jax.ShapeDtypeStruct((B, 1, D), F32), jax.ShapeDtypeStruct((1, D), F32)],
        compiler_params=_cp(("arbitrary", "arbitrary")),
    )(x, scale, nw, w_arr, d_ret, d_mla, d_mz, d_gla, dx_out)


def _proj_bwd_w(h, d_ret, d_mla, d_mz, d_gla):
    B, S, D = h.shape
    tm = min(S, 512)

    def body(h_ref, dr_ref, dm_ref, dz_ref, dg_ref, dw_ref, acc):
        first = (pl.program_id(0) == 0) & (pl.program_id(1) == 0)

        @pl.when(first)
        def _():
            acc[...] = jnp.zeros_like(acc)

        hv = h_ref[0]
        tn = lambda d_ref: lax.dot_general(hv, d_ref[0], (((0,), (0,)), ((), ())), preferred_element_type=F32)
        acc[:, 0:RET_W] += tn(dr_ref)
        acc[:, RET_W:RET_W + 512] += tn(dm_ref)
        acc[:, RET_W + 512:RET_W + MLA_W] += tn(dz_ref)
        acc[:, RET_W + MLA_W:ARR_W] += tn(dg_ref)

        @pl.when((pl.program_id(0) == B - 1) & (pl.program_id(1) == S // tm - 1))
        def _():
            dw_ref[...] = acc[...].astype(_MXU)

    tok = lambda w: pl.BlockSpec((1, tm, w), lambda b, i: (b, i, 0))
    return pl.pallas_call(
        body, name="proj_bwd_w", grid=(B, S // tm),
        in_specs=[tok(D), tok(RET_W), tok(512), tok(512), tok(GLA_W)],
        out_specs=_full((D, ARR_W)), out_shape=jax.ShapeDtypeStruct((D, ARR_W), _MXU),
        scratch_shapes=[pltpu.VMEM((D, ARR_W), F32)],
        compiler_params=_cp(("arbitrary", "arbitrary"), 56),
    )(h, d_ret, d_mla, d_mz, d_gla)


def _out_fwd_loss(x, gate, r_g, o_mla, mla_p, g_g, w_out, fw, target):
    B, S, D = x.shape
    tm = min(S, 512)

    def body(x_ref, g_ref, r_ref, o_ref, z_ref, gg_ref, w_ref, fw_ref, t_ref, dx_ref, y_ref, loss_ref, dfw_ref):
        first = (pl.program_id(0) == 0) & (pl.program_id(1) == 0)

        @pl.when(first)
        def _():
            loss_ref[...] = jnp.zeros_like(loss_ref)
            dfw_ref[...] = jnp.zeros_like(dfw_ref)

        mm = (o_ref[0] * _silu(z_ref[0].astype(F32))).astype(_MXU)
        y = (jnp.dot(r_ref[0], w_ref[0:256, :], preferred_element_type=F32)
             + jnp.dot(mm, w_ref[256:768, :], preferred_element_type=F32)
             + jnp.dot(gg_ref[0], w_ref[768:1024, :], preferred_element_type=F32))
        y_ref[0] = y.astype(_MXU)
        xv = x_ref[0] + g_ref[0] * y
        fwv = fw_ref[...]
        rstd = lax.rsqrt(jnp.mean(xv * xv, axis=-1, keepdims=True) + EPS)
        xh = xv * rstd
        err = xh * fwv - t_ref[0]
        loss_ref[...] += 0.5 * jnp.sum(jnp.mean(err * err, axis=-1, keepdims=True), axis=0, keepdims=True)
        dy = err * (1.0 / D)
        dfw_ref[...] += jnp.sum(dy * xh, axis=0, keepdims=True)
        dxh = dy * fwv
        dx_ref[0] = rstd * (dxh - xh * jnp.mean(dxh * xh, axis=-1, keepdims=True))

    tok = lambda w, c=0: pl.BlockSpec((1, tm, w), lambda b, i: (b, i, c))
    return pl.pallas_call(
        body, name="out_fwd_loss", grid=(B, S // tm),
        in_specs=[tok(D), pl.BlockSpec((1, 1, D), lambda b, i: (b, 0, 0)), tok(256), tok(512), tok(512, 1),
                  tok(256), _full((D, D)), _full((1, D)), tok(D)],
        out_specs=[tok(D), tok(D), _full((1, 1)), _full((1, D))],
        out_shape=[jax.ShapeDtypeStruct((B, S, D), F32), jax.ShapeDtypeStruct((B, S, D), _MXU),
                   jax.ShapeDtypeStruct((1, 1), F32), jax.ShapeDtypeStruct((1, D), F32)],
        compiler_params=_cp(("arbitrary", "arbitrary")),
    )(x, gate, r_g, o_mla, mla_p, g_g, w_out, fw, target)


def _local_step(x, pos3, mod, loss_target, small, w_in_a, w_uq_a, w_ukv_a, w_out_b):
    B, S, D = x.shape
    tabs = _rope_tables(pos3)
    saved = []
    for l in range(DEPTH):
        last = (small["final_norm"].reshape(1, D), loss_target) if l == DEPTH - 1 else None
        x, s = _layer_fwd(x, tabs, mod[l], {n: a[l] for n, a in small.items() if n != "final_norm"},
                          w_in_a[l], w_uq_a[l], w_ukv_a[l], w_out_b[l], loss_head=last)
        saved.append(s)
    dx, loss, d_fw = x
    grads = dict(final_norm=d_fw.reshape(D))
    per_layer = [None] * DEPTH
    for l in reversed(range(DEPTH)):
        dx, per_layer[l] = _layer_bwd(dx, saved[l], tabs)
    for name in per_layer[0]:
        grads[name] = jnp.stack([per_layer[l][name] for l in range(DEPTH)])
    return loss, dx, grads


def _layer_fwd(x, tabs, mod_l, small_l, w_in_a, w_uq_a=None, w_ukv_a=None, w_out_b=None, late_weights=None,
               loss_head=None):
    B, S, D = x.shape
    cr, sr, cm, sm = tabs
    shift = mod_l[:, 0:D].reshape(B, 1, D)
    scale = mod_l[:, D:2 * D].reshape(B, 1, D)
    gate = mod_l[:, 2 * D:3 * D].reshape(B, 1, D)
    nw = small_l["norm_w"].reshape(1, D)
    qnw = small_l["mla_q_norm"].reshape(1, 256)
    kvnw = small_l["mla_kv_norm"].reshape(1, 128)
    w_g2p = jnp.pad(small_l["gla_w_g2"], ((0, 112), (0, 0)))
    b_g2 = small_l["gla_b_g2"].reshape(1, 128)
    gnw = jnp.tile(small_l["gla_norm"], 4).reshape(1, 256)
    ret_p, mla_p, gla_p, h = _proj_fwd(x, shift, scale, nw, w_in_a)
    r_g, r_raw, r_st = _ret_fwd(ret_p, cr, sr)
    if late_weights is not None:
        w_uq_a, w_ukv_a, w_out_b = late_weights(r_raw)
    q, k, v, kt, vt = _mla_prep_fwd(mla_p, cm, sm, qnw, kvnw, w_uq_a, w_ukv_a)
    o_mla, lse = _mla_attn_fwd(q, k, vt)
    g_g, g_raw, g_st = _gla_fwd(gla_p, w_g2p, b_g2, gnw)
    if loss_head is None:
        x_new, y = _out_fwd(x, gate, r_g, o_mla, mla_p, g_g, w_out_b)
    else:
        dx, y, loss, d_fw = _out_fwd_loss(x, gate, r_g, o_mla, mla_p, g_g, w_out_b, *loss_head)
        x_new = (dx, loss, d_fw)
    saved = dict(x=x, shift=shift, scale=scale, gate=gate, nw=nw, qnw=qnw, kvnw=kvnw, w_g2p=w_g2p, b_g2=b_g2,
                 gnw=gnw, ret_p=ret_p, mla_p=mla_p, gla_p=gla_p, h=h, r_g=r_g, r_raw=r_raw, r_st=r_st, q=q, k=k,
                 v=v, kt=kt, o_mla=o_mla, lse=lse, g_g=g_g, g_raw=g_raw, g_st=g_st, y=y,
                 w_in_a=w_in_a, w_uq_a=w_uq_a, w_ukv_a=w_ukv_a, w_out_b=w_out_b)
    return x_new, saved


def _layer_bwd(dx, s, tabs, early_grads=None):
    B, S, D = dx.shape
    cr, sr, cm, sm = tabs
    d_r, do, d_mz, dl, d_g, dw_out, d_gate = _out_bwd(dx, s["y"], s["gate"], s["r_g"], s["g_g"], s["w_out_b"],
                                                      s["o_mla"], s["mla_p"])
    d_ret = _ret_bwd(s["ret_p"], cr, sr, s["r_raw"], s["r_st"], d_r)
    dq, dk, dv = _mla_attn_bwd(s["q"], s["k"], s["v"], s["kt"], do, s["lse"], dl)
    d_mla, dw_uq, dw_ukv, d_qnw, d_kvnw = _mla_prep_bwd(
        s["mla_p"], cm, sm, s["qnw"], s["kvnw"], s["w_uq_a"], s["w_ukv_a"], dq, dk, dv)
    gnw = s["gnw"] if early_grads is None else s["gnw"] + early_grads(dw_out, dw_uq, dw_ukv)
    d_gla, dw_g2p, db_g2, d_gnw = _gla_bwd(s["gla_p"], s["w_g2p"], s["b_g2"], gnw, s["g_raw"], s["g_st"], d_g)
    dx, d_shift, d_scale, d_nw = _proj_bwd_x(s["x"], s["shift"], s["scale"], s["nw"], s["w_in_a"],
                                             d_ret, d_mla, d_mz, d_gla, dx)
    dw_in = _proj_bwd_w(s["h"], d_ret, d_mla, d_mz, d_gla)
    grads = dict(
        d_mod=jnp.concatenate([d_shift, d_scale, d_gate], axis=2).reshape(B, 3 * D),
        norm_w=d_nw.reshape(D), mla_q_norm=d_qnw.reshape(256), mla_kv_norm=d_kvnw.reshape(128),
        gla_w_g2=dw_g2p[0:16], gla_b_g2=db_g2.reshape(128), gla_norm256=d_gnw.reshape(256),
        w_in_a=dw_in, w_uq_a=dw_uq, w_ukv_a=dw_ukv, w_out=dw_out)
    return dx, grads


def _exchange(arrs, gather, name):
    n = len(arrs)
    out_shape = [jax.ShapeDtypeStruct(((N_DEV,) + a.shape) if g else a.shape, a.dtype)
                 for a, g in zip(arrs, gather)]

    def body(*refs):
        ins, outs = refs[:n], refs[n:2 * n]
        send_sems, recv_sems, local_sems = refs[2 * n:]
        ix, iy, ic = lax.axis_index("x"), lax.axis_index("y"), lax.axis_index("c")
        me = 4 * ix + 2 * iy + ic
        copies = []
        for a in range(n):
            mine = ins[a] if gather[a] else ins[a].at[me]
            loc = pltpu.make_async_copy(mine, outs[a].at[me], local_sems.at[a])
            loc.start()
            copies.append(loc)
            for d in range(1, N_DEV):
                px = 1 - ix if d & 4 else ix
                py = 1 - iy if d & 2 else iy
                pc = 1 - ic if d & 1 else ic
                src = ins[a] if gather[a] else ins[a].at[4 * px + 2 * py + pc]
                cp = pltpu.make_async_remote_copy(
                    src_ref=src, dst_ref=outs[a].at[me], send_sem=send_sems.at[a, d - 1],
                    recv_sem=recv_sems.at[a, d - 1], device_id=(px, py, pc), device_id_type=pl.DeviceIdType.MESH)
                cp.start()
                copies.append(cp)
        for cp in copies:
            cp.wait()

    any_spec = pl.BlockSpec(memory_space=pl.ANY)
    outs = pl.pallas_call(
        body, name=name, in_specs=[any_spec] * n, out_specs=[any_spec] * n, out_shape=out_shape,
        scratch_shapes=[pltpu.SemaphoreType.DMA((n, N_DEV - 1)), pltpu.SemaphoreType.DMA((n, N_DEV - 1)),
                        pltpu.SemaphoreType.DMA((n,))],
    )(*arrs)
    return list(outs)


def _peers(ix, iy, ic):
    out = []
    for d in range(1, N_DEV):
        px = 1 - ix if d & 4 else ix
        py = 1 - iy if d & 2 else iy
        pc = 1 - ic if d & 1 else ic
        out.append((d - 1, (px, py, pc), 4 * px + 2 * py + pc))
    return out


def _exchange_start(arrs, gather, name, after=None):
    n = len(arrs)
    lands = [lax.empty(((N_DEV,) + a.shape) if g else a.shape, a.dtype) for a, g in zip(arrs, gather)]
    extra = [] if after is None else [after]

    def body(*refs):
        ins, land_refs = refs[:n], refs[n:2 * n]
        send_sems, recv_sems = refs[2 * n + len(extra)], refs[2 * n + len(extra) + 1]
        token = refs[-1]
        ix, iy, ic = lax.axis_index("x"), lax.axis_index("y"), lax.axis_index("c")
        me = 4 * ix + 2 * iy + ic
        for a in range(n):
            for k, peer, peer_idx in _peers(ix, iy, ic):
                pltpu.make_async_remote_copy(
                    src_ref=ins[a] if gather[a] else ins[a].at[peer_idx], dst_ref=land_refs[a].at[me],
                    send_sem=send_sems.at[7 * a + k], recv_sem=recv_sems.at[7 * a + k], device_id=peer,
                    device_id_type=pl.DeviceIdType.MESH).start()
        token[...] = jnp.zeros_like(token)

    hbm = pl.BlockSpec(memory_space=pltpu.HBM)
    sem = pl.BlockSpec(memory_space=pltpu.SEMAPHORE)
    held = [pltpu.with_memory_space_constraint(a, pltpu.HBM) for a in list(arrs) + lands]
    outs = pl.pallas_call(
        body, name=name,
        out_shape=(pltpu.SemaphoreType.DMA((7 * n,)), pltpu.SemaphoreType.DMA((7 * n,)),
                   *[pltpu.HBM(a.shape, a.dtype) for a in held], jax.ShapeDtypeStruct((8, 128), F32)),
        in_specs=[hbm] * (2 * n) + [pl.BlockSpec(memory_space=pl.ANY)] * len(extra),
        out_specs=(sem, sem, *[hbm] * (2 * n), pl.BlockSpec(memory_space=pltpu.VMEM)),
        input_output_aliases={a: 2 + a for a in range(2 * n)},
        compiler_params=pltpu.CompilerParams(has_side_effects=pltpu.SideEffectType.DATAFLOW_SIDE_EFFECTING),
    )(*held, *extra)
    return dict(send=outs[0], recv=outs[1], srcs=list(outs[2:2 + n]), lands=list(outs[2 + n:2 + 2 * n]),
                token=outs[-1], gather=list(gather))


def _exchange_wait(flight, after, me, name):
    n = len(flight["srcs"])
    gather = flight["gather"]

    def body(*refs):
        srcs, land_refs = refs[:n], refs[n:2 * n]
        send_sems, recv_sems = refs[2 * n], refs[2 * n + 1]
        ix, iy, ic = lax.axis_index("x"), lax.axis_index("y"), lax.axis_index("c")
        mine = 4 * ix + 2 * iy + ic
        for a in range(n):
            for k, peer, peer_idx in _peers(ix, iy, ic):
                cp = pltpu.make_async_remote_copy(
                    src_ref=srcs[a] if gather[a] else srcs[a].at[peer_idx], dst_ref=land_refs[a].at[mine],
                    send_sem=send_sems.at[7 * a + k], recv_sem=recv_sems.at[7 * a + k], device_id=peer,
                    device_id_type=pl.DeviceIdType.MESH)
                cp.wait_send()
                cp.wait_recv()

    hbm = pl.BlockSpec(memory_space=pltpu.HBM)
    sem = pl.BlockSpec(memory_space=pltpu.SEMAPHORE)
    held = flight["srcs"] + flight["lands"]
    outs = pl.pallas_call(
        body, name=name, out_shape=tuple(pltpu.HBM(a.shape, a.dtype) for a in held),
        in_specs=[hbm] * (2 * n) + [sem, sem, pl.BlockSpec(memory_space=pl.ANY)], out_specs=tuple([hbm] * (2 * n)),
        input_output_aliases={a: a for a in range(2 * n)},
        compiler_params=pltpu.CompilerParams(has_side_effects=pltpu.SideEffectType.DATAFLOW_SIDE_EFFECTING),
    )(*held, flight["send"], flight["recv"], after)
    got = []
    for a in range(n):
        src, land = outs[a], outs[n + a]
        own = src if gather[a] else lax.dynamic_index_in_dim(src, me, axis=0, keepdims=False)
        got.append(lax.dynamic_update_index_in_dim(land, own, me, axis=0))
    return got


def _ada_fwd(c_all, ada_w, ada_b_cols):
    nb, D = c_all.shape
    cols = ada_w.shape[2]

    def body(c_ref, w_ref, b_ref, out_ref):
        ca = _silu(c_ref[...])
        for l in range(DEPTH):
            out_ref[l] = _mm(ca, w_ref[l]) + b_ref[l:l + 1, :]

    return pl.pallas_call(
        body, name="ada_fwd", out_shape=jax.ShapeDtypeStruct((DEPTH, nb, cols), F32),
        in_specs=[pl.BlockSpec(memory_space=pltpu.VMEM)] * 3, out_specs=pl.BlockSpec(memory_space=pltpu.VMEM),
        compiler_params=pltpu.CompilerParams(vmem_limit_bytes=32 * VMEM_MB),
    )(c_all, ada_w, ada_b_cols)


def _ada_bwd(c_all, d_mod_cols):
    nb, D = c_all.shape
    cols = d_mod_cols.shape[2]

    def body(c_ref, dm_ref, out_ref):
        ca = _silu(c_ref[...])
        for l in range(DEPTH):
            out_ref[l] = _mm_tn(ca, dm_ref[l])

    return pl.pallas_call(
        body, name="ada_bwd", out_shape=jax.ShapeDtypeStruct((DEPTH, D, cols), F32),
        in_specs=[pl.BlockSpec(memory_space=pltpu.VMEM)] * 2, out_specs=pl.BlockSpec(memory_space=pltpu.VMEM),
        compiler_params=pltpu.CompilerParams(vmem_limit_bytes=32 * VMEM_MB),
    )(c_all, d_mod_cols)


def _sum_adamw(parts, w, m, v, name, after=None):
    P, R, C = parts.shape
    tr = 256 if (R % 256 == 0 and R > 256) else R
    extra = [] if after is None else [after]

    def body(p_ref, w_ref, m_ref, v_ref, *rest):
        g_ref, d_ref, nm_ref, nv_ref = rest[-4:]
        g = p_ref[0].astype(F32)
        for k in range(1, P):
            g = g + p_ref[k].astype(F32)
        g_ref[...] = g
        nm = ADAM_B1 * m_ref[...] + (1.0 - ADAM_B1) * g
        nv = ADAM_B2 * v_ref[...] + (1.0 - ADAM_B2) * (g * g)
        nm_ref[...] = nm
        nv_ref[...] = nv
        m_hat = nm / (1.0 - ADAM_B1 ** ADAM_STEP)
        v_hat = nv / (1.0 - ADAM_B2 ** ADAM_STEP)
        d_ref[...] = -ADAM_LR * (m_hat / (jnp.sqrt(v_hat) + ADAM_EPS) + ADAM_WD * w_ref[...])

    blk = pl.BlockSpec((tr, C), lambda i: (i, 0))
    shp = jax.ShapeDtypeStruct((R, C), F32)
    return pl.pallas_call(
        body, name=name, grid=(R // tr,),
        in_specs=[pl.BlockSpec((P, tr, C), lambda i: (0, i, 0)), blk, blk, blk]
        + [pl.BlockSpec(memory_space=pl.ANY)] * len(extra),
        out_specs=[blk, blk, blk, blk], out_shape=[shp, shp, shp, shp],
        compiler_params=_cp(("parallel",)),
    )(parts, w, m, v, *extra)


def _sum_adamw_layer(parts, w, m, v, layer, name, prev=None, after=None):
    P, R, C = parts.shape
    tr = 256 if (R % 256 == 0 and R > 256) else R

    def body(p_ref, w_ref, m_ref, v_ref, *rest):
        g_ref, d_ref, nm_ref, nv_ref = rest[-4:]
        g = p_ref[0].astype(F32)
        for k in range(1, P):
            g = g + p_ref[k].astype(F32)
        g_ref[0] = g
        nm = ADAM_B1 * m_ref[0] + (1.0 - ADAM_B1) * g
        nv = ADAM_B2 * v_ref[0] + (1.0 - ADAM_B2) * (g * g)
        nm_ref[0] = nm
        nv_ref[0] = nv
        m_hat = nm / (1.0 - ADAM_B1 ** ADAM_STEP)
        v_hat = nv / (1.0 - ADAM_B2 ** ADAM_STEP)
        d_ref[0] = -ADAM_LR * (m_hat / (jnp.sqrt(v_hat) + ADAM_EPS) + ADAM_WD * w_ref[0])

    blk = pl.BlockSpec((1, tr, C), lambda i: (layer, i, 0))
    shp = jax.ShapeDtypeStruct(w.shape, F32)
    in_specs = [pl.BlockSpec((P, tr, C), lambda i: (0, i, 0)), blk, blk, blk]
    args = [parts, w, m, v]
    aliases = {}
    if prev is not None:
        in_specs += [pl.BlockSpec(memory_space=pl.ANY)] * 4
        args += list(prev)
        aliases = {4 + k: k for k in range(4)}
    if after is not None:
        in_specs.append(pl.BlockSpec(memory_space=pl.ANY))
        args.append(after)
    return list(pl.pallas_call(
        body, name=name, grid=(R // tr,), in_specs=in_specs, out_specs=[blk] * 4, out_shape=[shp] * 4,
        input_output_aliases=aliases, compiler_params=_cp(("parallel",)),
    )(*args))


SMALL = ["norm_w", "mla_q_norm", "mla_kv_norm", "gla_w_g2", "gla_b_g2", "gla_norm", "final_norm"]


SMALL_ROWS = 72


def _pack_small(loss, part):
    flat = [jnp.pad(loss.reshape(1), (0, 127))] + [part[n].reshape(-1) for n in SMALL]
    used = sum(f.shape[0] for f in flat)
    flat.append(jnp.zeros((SMALL_ROWS * 128 - used,), F32))
    return jnp.concatenate(flat).reshape(SMALL_ROWS, 128)


def _small_adamw(packed_parts, w, m, v, after=None):
    n = len(w)
    extra = [] if after is None else [after]

    def body(*refs):
        p_ref = refs[0]
        w_refs, m_refs, v_refs = refs[1:1 + n], refs[1 + n:1 + 2 * n], refs[1 + 2 * n:1 + 3 * n]
        outs, acc = refs[1 + 3 * n + len(extra):-1], refs[-1]
        total = p_ref[0]
        for k in range(1, N_DEV):
            total = total + p_ref[k]
        acc[...] = total
        outs[0][...] = acc[0:1, :]
        r0 = 1
        for i in range(n):
            shp = w_refs[i].shape
            if len(shp) == 3:
                g = acc[r0:r0 + shp[0] * shp[1], :].reshape(shp)
                r0 += shp[0] * shp[1]
            elif shp[1] < 128:
                g = acc[r0:r0 + shp[0], 0:shp[1]]
                r0 += shp[0]
            else:
                k = shp[1] // 128
                g = jnp.concatenate(
                    [jnp.concatenate([acc[r0 + l * k + j:r0 + l * k + j + 1, :] for j in range(k)], axis=1)
                     for l in range(shp[0])], axis=0)
                r0 += shp[0] * k
            nm = ADAM_B1 * m_refs[i][...] + (1.0 - ADAM_B1) * g
            nv = ADAM_B2 * v_refs[i][...] + (1.0 - ADAM_B2) * (g * g)
            m_hat = nm / (1.0 - ADAM_B1 ** ADAM_STEP)
            v_hat = nv / (1.0 - ADAM_B2 ** ADAM_STEP)
            outs[1 + 4 * i][...] = g
            outs[2 + 4 * i][...] = -ADAM_LR * (m_hat / (jnp.sqrt(v_hat) + ADAM_EPS) + ADAM_WD * w_refs[i][...])
            outs[3 + 4 * i][...] = nm
            outs[4 + 4 * i][...] = nv

    vmem = pl.BlockSpec(memory_space=pltpu.VMEM)
    out_shape = [jax.ShapeDtypeStruct((1, 128), F32)]
    for a in w:
        out_shape += [jax.ShapeDtypeStruct(a.shape, F32)] * 4
    outs = pl.pallas_call(
        body, name="adamw_small", in_specs=[vmem] * (1 + 3 * n) + [pl.BlockSpec(memory_space=pl.ANY)] * len(extra),
        out_specs=[vmem] * (1 + 4 * n), out_shape=out_shape, scratch_shapes=[pltpu.VMEM((SMALL_ROWS, 128), F32)],
    )(packed_parts, *w, *m, *v, *extra)
    return outs[0], [outs[1 + 4 * i:5 + 4 * i] for i in range(n)]


WEIGHTS = ["norm_w", "ada_w", "ada_b", "w_in", "mla_q_norm", "w_uq", "mla_kv_norm", "w_ukv", "gla_w_g2",
           "gla_b_g2", "gla_norm", "w_out", "final_norm"]


def kernel(x, c, positions, norm_w, ada_w, ada_b, w_in, mla_q_norm, w_uq, mla_kv_norm, w_ukv, gla_w_g2, gla_b_g2, gla_norm, w_out, final_norm, loss_target, m_norm_w, m_ada_w, m_ada_b, m_w_in, m_mla_q_norm, m_w_uq, m_mla_kv_norm, m_w_ukv, m_gla_w_g2, m_gla_b_g2, m_gla_norm, m_w_out, m_final_norm, v_norm_w, v_ada_w, v_ada_b, v_w_in, v_mla_q_norm, v_w_uq, v_mla_kv_norm, v_w_ukv, v_gla_w_g2, v_gla_b_g2, v_gla_norm, v_w_out, v_final_norm):
    w = dict(norm_w=norm_w, ada_w=ada_w, ada_b=ada_b, w_in=w_in, mla_q_norm=mla_q_norm, w_uq=w_uq,
             mla_kv_norm=mla_kv_norm, w_ukv=w_ukv, gla_w_g2=gla_w_g2, gla_b_g2=gla_b_g2, gla_norm=gla_norm,
             w_out=w_out, final_norm=final_norm)
    m = dict(norm_w=m_norm_w, ada_w=m_ada_w, ada_b=m_ada_b, w_in=m_w_in, mla_q_norm=m_mla_q_norm, w_uq=m_w_uq,
             mla_kv_norm=m_mla_kv_norm, w_ukv=m_w_ukv, gla_w_g2=m_gla_w_g2, gla_b_g2=m_gla_b_g2,
             gla_norm=m_gla_norm, w_out=m_w_out, final_norm=m_final_norm)
    v = dict(norm_w=v_norm_w, ada_w=v_ada_w, ada_b=v_ada_b, w_in=v_w_in, mla_q_norm=v_mla_q_norm, w_uq=v_w_uq,
             mla_kv_norm=v_mla_kv_norm, w_ukv=v_w_ukv, gla_w_g2=v_gla_w_g2, gla_b_g2=v_gla_b_g2,
             gla_norm=v_gla_norm, w_out=v_w_out, final_norm=v_final_norm)
    B, S, D = x.shape
    me = 4 * lax.axis_index("x") + 2 * lax.axis_index("y") + lax.axis_index("c")
    ada_cols = ada_w.shape[2]
    cast = lambda a: a.astype(_MXU)

    sharded = ["w_in", "w_uq", "w_ukv", "w_out"]

    whole_cols = lambda a: jnp.transpose(a, (1, 0, 2)).reshape(a.shape[1], -1)
    whole_in = lambda blk: _arrange_w_in(whole_cols(blk))
    whole_rest = lambda blks: (_arrange_w_uq(whole_cols(blks[0])), _arrange_w_ukv(whole_cols(blks[1])),
                               blks[2].reshape(D, D))
    col_blocks = lambda a: jnp.transpose(a.reshape(a.shape[0], N_DEV, -1), (1, 0, 2)).astype(jnp.bfloat16)
    blocks_in = lambda dw_in_a: col_blocks(_unarrange_w_in(dw_in_a))
    blocks_rest = lambda dw_out, dw_uq_a, dw_ukv_a: [
        col_blocks(_unarrange_w_uq(dw_uq_a)), col_blocks(_unarrange_w_ukv(dw_ukv_a)),
        dw_out.reshape(N_DEV, D // N_DEV, D).astype(jnp.bfloat16)]

    (c_g,) = _exchange([c], [True], "gather_c")
    c_all = c_g.reshape(N_DEV * B, D)

    ada_b_cols = lax.dynamic_slice(ada_b, (0, me * ada_cols), (DEPTH, ada_cols))
    mod_cols = _ada_fwd(c_all, ada_w, ada_b_cols)
    mod_send = jnp.transpose(mod_cols.reshape(DEPTH, N_DEV, B, ada_cols), (1, 0, 2, 3))
    (mod_recv,) = _exchange([mod_send], [False], "scatter_mod")
    mod = jnp.transpose(mod_recv, (1, 2, 0, 3)).reshape(DEPTH, B, 3 * D)

    flight_i = _exchange_start([cast(w_in[0])], [True], "gather_start_first", after=mod)
    flight_r = _exchange_start([cast(w[n][0]) for n in sharded[1:]], [True] * 3, "gather_start_layer0",
                               after=flight_i["token"])
    flight_w = _exchange_start([cast(w[n][1]) for n in sharded], [True] * 4, "gather_start_layer1",
                               after=flight_r["token"])
    small_w = {n: w[n] for n in SMALL}
    layer_small = lambda l: {n: a[l] for n, a in small_w.items() if n != "final_norm"}
    tabs = _rope_tables(positions.reshape(B, S, 1), flight_w["token"][0, 0])
    late0 = lambda after: whole_rest(_exchange_wait(flight_r, after, me, "gather_wait_layer0"))
    (w_in0_g,) = _exchange_wait(flight_i, tabs[0], me, "gather_wait_first")
    x1, saved0 = _layer_fwd(x, tabs, mod[0], layer_small(0), whole_in(w_in0_g), late_weights=late0)
    got1 = _exchange_wait(flight_w, x1, me, "gather_wait_layer1")
    (dx, loss, d_fw), saved1 = _layer_fwd(x1, tabs, mod[1], layer_small(1), whole_in(got1[0]), *whole_rest(got1[1:]),
                                          loss_head=(final_norm.reshape(1, D), loss_target))

    dx, g1 = _layer_bwd(dx, saved1, tabs)
    flight_g = _exchange_start([blocks_in(g1["w_in_a"])] + blocks_rest(g1["w_out"], g1["w_uq_a"], g1["w_ukv_a"]),
                               [False] * 4, "grads_start_layer1")
    flights = {}

    def early0(dw_out, dw_uq_a, dw_ukv_a):
        flights["rest0"] = _exchange_start(blocks_rest(dw_out, dw_uq_a, dw_ukv_a), [False] * 3, "grads_start_layer0")
        return flights["rest0"]["token"][0, 0]

    saved0 = dict(saved0, gate=saved0["gate"] + flight_g["token"][0, 0])
    grad_x, g0 = _layer_bwd(dx, saved0, tabs, early_grads=early0)
    parts1 = _exchange_wait(flight_g, grad_x, me, "grads_wait_layer1")
    rest0 = _exchange_wait(flights["rest0"], g0["w_in_a"], me, "grads_wait_layer0")

    both = lambda n: jnp.stack([g0[n], g1[n]])
    d_mod = both("d_mod")
    part = dict(norm_w=both("norm_w"), mla_q_norm=both("mla_q_norm"), mla_kv_norm=both("mla_kv_norm"),
                gla_w_g2=both("gla_w_g2"), gla_b_g2=both("gla_b_g2"), gla_norm=both("gla_norm256")[:, 0:128],
                final_norm=d_fw)
    d_mod_g, small_g = _exchange([d_mod, _pack_small(loss, part)], [True, True], "gather_small")
    flight_l = _exchange_start([blocks_in(g0["w_in_a"])], [False], "exchange_start_last", after=small_g)
    res = {}
    behind = flight_l["token"]
    for a, name in enumerate(sharded):
        res[name] = _sum_adamw_layer(parts1[a], w[name], m[name], v[name], 1, "adamw_%s_layer1" % name, after=behind)
        behind = res[name][1]
    for a, name in enumerate(sharded[1:]):
        res[name] = _sum_adamw_layer(rest0[a], w[name], m[name], v[name], 0, "adamw_%s_layer0" % name,
                                     prev=res[name], after=behind)
        behind = res[name][1]

    d_mod_all = jnp.transpose(d_mod_g, (1, 0, 2, 3)).reshape(DEPTH, N_DEV * B, 3 * D)
    d_mod_cols = lax.dynamic_slice(d_mod_all, (0, 0, me * ada_cols), (DEPTH, N_DEV * B, ada_cols))
    g_ada_w = _ada_bwd(c_all, d_mod_cols)

    def update(name, parts2d, after):
        shp = w[name].shape
        two = lambda a: a.reshape(parts2d.shape[1:])
        out = _sum_adamw(parts2d, two(w[name]), two(m[name]), two(v[name]), "adamw_" + name, after=after)
        res[name] = [o.reshape(shp) for o in out]
        return out[1]

    behind = update("ada_w", g_ada_w.reshape(1, DEPTH * D, ada_cols), behind)
    behind = update("ada_b", jnp.transpose(d_mod_g, (0, 2, 1, 3)).reshape(N_DEV * B, DEPTH * 3 * D // 128, 128), behind)
    row = lambda a: a.reshape(1, D) if a.ndim == 1 else a
    loss_sum, small_out = _small_adamw(small_g, [row(w[n]) for n in SMALL], [row(m[n]) for n in SMALL],
                                       [row(v[n]) for n in SMALL], after=behind)
    for n, outs in zip(SMALL, small_out):
        res[n] = [o.reshape(w[n].shape) for o in outs]
    loss_out = loss_sum[0, 0]
    (in0,) = _exchange_wait(flight_l, loss_sum, me, "exchange_wait_last")
    res["w_in"] = _sum_adamw_layer(in0, w_in, m_w_in, v_w_in, 0, "adamw_w_in_layer0", prev=res["w_in"])
    return (loss_out, grad_x, *[res[n][0] for n in WEIGHTS], *[res[n][1] for n in WEIGHTS],
            *[res[n][2] for n in WEIGHTS], *[res[n][3] for n in WEIGHTS])
```

```python
import functools
import math

import numpy as np
import jax
import jax.numpy as jnp
from jax import lax
from jax.experimental import pallas as pl
from jax.experimental.pallas import tpu as pltpu

F32 = jnp.float32
_MXU = jnp.bfloat16

D_MODEL = 1024
DEPTH = 2
CHUNK = 64
EPS = 1e-6
ROPE_THETA = 10000.0
N_DEV = 8

MLA_SCALE = 96.0 ** -0.5
RET_KSCALE = 64.0 ** -0.5
GLA_KSCALE = 32.0 ** -0.5
GLA_TAU = 16.0

ADAM_LR = 0.001
ADAM_B1 = 0.9
ADAM_B2 = 0.999
ADAM_EPS = 1e-08
ADAM_WD = 0.01
ADAM_STEP = 10

RET_W, MLA_W, GLA_W = 1024, 1024, 896
ARR_W = RET_W + MLA_W + GLA_W
VMEM_MB = 1024 * 1024


def _cp(sem, vmem_mb=48):
    return pltpu.CompilerParams(dimension_semantics=sem, vmem_limit_bytes=vmem_mb * VMEM_MB)


def _mm(a, b):
    return jnp.dot(a.astype(_MXU), b.astype(_MXU), preferred_element_type=F32)


def _mm_nt(a, b):
    return lax.dot_general(a.astype(_MXU), b.astype(_MXU), (((1,), (1,)), ((), ())),
                           preferred_element_type=F32)


def _mm_tn(a, b):
    return lax.dot_general(a.astype(_MXU), b.astype(_MXU), (((0,), (0,)), ((), ())),
                           preferred_element_type=F32)


def _mm_f32(a, b):
    return jnp.dot(a, b, precision=lax.Precision.HIGHEST, preferred_element_type=F32)


def _sig(z):
    return 1.0 / (1.0 + jnp.exp(-z))


def _silu(z):
    return z * _sig(z)


def _dsilu(z):
    s = _sig(z)
    return s * (1.0 + z * (1.0 - s))


def _full(shape):
    nd = len(shape)
    return pl.BlockSpec(shape, lambda *_: (0,) * nd)


def _qk_perm(blk):
    r = blk.shape[0]
    return jnp.transpose(blk.reshape(r, 4, 2, 32), (0, 2, 1, 3)).reshape(r, 256)


def _qk_unperm(blk):
    r = blk.shape[0]
    return jnp.transpose(blk.reshape(r, 2, 4, 32), (0, 2, 1, 3)).reshape(r, 256)


def _arrange_w_in(w):
    z = lambda n: jnp.zeros((w.shape[0], n), w.dtype)
    ret = [_qk_perm(w[:, 0:256]), _qk_perm(w[:, 256:512]), w[:, 512:768], w[:, 768:1024]]
    mla = [w[:, 1024:1280], w[:, 1280:1408], z(64), w[:, 1408:1440], z(32), w[:, 1440:1952]]
    gla = [w[:, 1952:2080], w[:, 2080:2208], w[:, 2208:2464], w[:, 2464:2480], z(112), w[:, 2480:2736]]
    return jnp.concatenate(ret + mla + gla, axis=1)


def _unarrange_w_in(a):
    m, g = RET_W, RET_W + MLA_W
    parts = [_qk_unperm(a[:, 0:256]), _qk_unperm(a[:, 256:512]), a[:, 512:1024],
             a[:, m:m + 384], a[:, m + 448:m + 480], a[:, m + 512:m + 1024],
             a[:, g:g + 528], a[:, g + 640:g + 896]]
    return jnp.concatenate(parts, axis=1)


def _arrange_w_uq(w):
    return jnp.pad(w.reshape(256, 8, 96), ((0, 0), (0, 0), (0, 32))).reshape(256, 1024)


def _unarrange_w_uq(a):
    return a.reshape(256, 8, 128)[:, :, :96].reshape(256, 768)


def _arrange_w_ukv(w):
    r = w.reshape(128, 8, 128)
    k = jnp.pad(r[:, :, :64], ((0, 0), (0, 0), (0, 64))).reshape(128, 1024)
    return jnp.concatenate([k, r[:, :, 64:].reshape(128, 512)], axis=1)


def _unarrange_w_ukv(a):
    k = a[:, :1024].reshape(128, 8, 128)[:, :, :64]
    v = a[:, 1024:].reshape(128, 8, 64)
    return jnp.concatenate([k, v], axis=2).reshape(128, 1024)


def _rope_tables(pos3, zero=0.0):
    B, S, _ = pos3.shape
    ts = min(S, 512)
    inv32 = (np.float32(ROPE_THETA) ** (-(np.arange(32, dtype=np.float32) / 32))).astype(np.float32)
    inv16 = (np.float32(ROPE_THETA) ** (-(np.arange(16, dtype=np.float32) / 16))).astype(np.float32)
    inv = np.zeros((1, 128), np.float32)
    inv[0, 0:32] = inv32
    inv[0, 32:48] = inv16

    def body(pos_ref, inv_ref, cr, sr, cm, sm):
        ang = pos_ref[0].astype(F32) * inv_ref[...]
        lane = lax.broadcasted_iota(jnp.int32, (1, 128), 1)

        def every_head(x):
            y = jnp.where(lane < 32, x, pltpu.roll(x, 32, 1))
            return jnp.where(lane < 64, y, pltpu.roll(y, 64, 1))

        def rotary_pair(x, fill):
            return jnp.where((lane >= 64) & (lane < 80), pltpu.roll(x, 32, 1),
                             jnp.where((lane >= 80) & (lane < 96), pltpu.roll(x, 48, 1), fill))

        c, s = jnp.cos(ang), jnp.sin(ang)
        cr[0] = every_head(c)
        sr[0] = every_head(s)
        cm[0] = rotary_pair(c, 1.0)
        sm[0] = rotary_pair(s, 0.0)

    tab = jax.ShapeDtypeStruct((B, S, 128), F32)
    blk = pl.BlockSpec((1, ts, 128), lambda b, i: (b, i, 0))
    return pl.pallas_call(
        body, name="rope_tables", grid=(B, S // ts),
        in_specs=[pl.BlockSpec((1, ts, 1), lambda b, i: (b, i, 0)), _full((1, 128))],
        out_specs=[blk, blk, blk, blk], out_shape=[tab, tab, tab, tab],
        compiler_params=_cp(("parallel", "parallel")),
    )(pos3, jnp.asarray(inv) + zero)


def _rope128(x, cos, sin):
    lane = lax.broadcasted_iota(jnp.int32, (1, 128), 1)
    rp = pltpu.roll(x, 16, 1)
    rm = pltpu.roll(x, 112, 1)
    return x * cos + jnp.where(lane < 80, -rm, rp) * sin


def _rope128_t(d, cos, sin):
    lane = lax.broadcasted_iota(jnp.int32, (1, 128), 1)
    y = d * sin
    yp = pltpu.roll(y, 16, 1)
    ym = pltpu.roll(y, 112, 1)
    return d * cos + jnp.where(lane < 64, 0.0, jnp.where(lane < 80, ym, jnp.where(lane < 96, -yp, 0.0)))


def _proj_fwd(x, shift, scale, nw, w_arr):
    B, S, D = x.shape
    tm = min(S, 512)

    def body(x_ref, sh_ref, sc_ref, nw_ref, w_ref, ret_ref, mla_ref, gla_ref, h_ref):
        xv = x_ref[0]
        rstd = lax.rsqrt(jnp.mean(xv * xv, axis=-1, keepdims=True) + EPS)
        h = (xv * rstd * nw_ref[...]) * (1.0 + sc_ref[0]) + sh_ref[0]
        hb = h.astype(_MXU)
        h_ref[0] = hb
        ret_ref[0] = jnp.dot(hb, w_ref[:, 0:RET_W], preferred_element_type=F32).astype(_MXU)
        mla_ref[0] = jnp.dot(hb, w_ref[:, RET_W:RET_W + MLA_W], preferred_element_type=F32).astype(_MXU)
        gla_ref[0] = jnp.dot(hb, w_ref[:, RET_W + MLA_W:ARR_W], preferred_element_type=F32).astype(_MXU)

    tok = lambda w: pl.BlockSpec((1, tm, w), lambda b, i: (b, i, 0))
    per_seq = pl.BlockSpec((1, 1, D), lambda b, i: (b, 0, 0))
    return pl.pallas_call(
        body, name="proj_fwd", grid=(B, S // tm),
        in_specs=[tok(D), per_seq, per_seq, _full((1, D)), _full((D, ARR_W))],
        out_specs=[tok(RET_W), tok(MLA_W), tok(GLA_W), tok(D)],
        out_shape=[jax.ShapeDtypeStruct((B, S, RET_W), _MXU), jax.ShapeDtypeStruct((B, S, MLA_W), _MXU),
                   jax.ShapeDtypeStruct((B, S, GLA_W), _MXU), jax.ShapeDtypeStruct((B, S, D), _MXU)],
        compiler_params=_cp(("parallel", "parallel")),
    )(x, shift, scale, nw, w_arr)


RET_L = 256


def _ret_consts(L):
    lg = np.log1p(-np.exp2(-5.0 - np.arange(4, dtype=np.float32))).astype(np.float32)
    i = np.arange(L)
    ci = i // CHUNK
    diff = (i[:, None] - i[None, :]).astype(np.float32)
    same = ci[:, None] == ci[None, :]
    past = ci[None, :] < ci[:, None]
    expo = np.where(same, np.abs(diff), np.where(past, diff, 0.0)).astype(np.float32)
    dec = np.where((same | past)[None], np.exp(lg[:, None, None] * expo[None]), 0.0).astype(np.float32)
    head = (np.arange(256) % 128) // 32
    qw = np.exp((i + 1.0)[:, None] * lg[head][None, :]).astype(np.float32)
    kw = np.exp((L - 1.0 - i)[:, None] * lg[head][None, :]).astype(np.float32)
    a_row = np.exp(np.float32(L) * lg[head])[None, :].astype(np.float32)
    return [jnp.asarray(t) for t in (dec.reshape(4 * L, L), qw, kw, a_row)]


def _ret_masks():
    lane = lax.broadcasted_iota(jnp.int32, (1, 256), 1)
    mh = [((lane % 128) // 32) == h for h in range(4)]
    mv = [(lane // 64) == h for h in range(4)]
    vi = lax.broadcasted_iota(jnp.int32, (256, 256), 0)
    ki = lax.broadcasted_iota(jnp.int32, (256, 256), 1)
    bd = (vi // 64) == ((ki % 128) // 32)
    return mh, mv, bd


def _ret_rope(p, cs, sn):
    q1, q2, k1, k2 = p[:, 0:128], p[:, 128:256], p[:, 256:384], p[:, 384:512]
    qr = jnp.concatenate([q1 * cs - q2 * sn, q2 * cs + q1 * sn], axis=1)
    kr = jnp.concatenate([k1 * cs - k2 * sn, k2 * cs + k1 * sn], axis=1) * RET_KSCALE
    return qr, kr


def _head_mean(x, mv, width):
    out = jnp.zeros_like(x)
    for m in mv:
        s = jnp.sum(jnp.where(m, x, 0.0), axis=-1, keepdims=True) * (1.0 / width)
        out = jnp.where(m, s, out)
    return out


def _stack_heads(x, masks):
    return jnp.concatenate([jnp.where(m, x, 0.0) for m in masks], axis=0)


def _fold_heads(xs, masks, L):
    out = jnp.where(masks[0], xs[0:L], 0.0)
    for h in range(1, 4):
        out = out + jnp.where(masks[h], xs[h * L:(h + 1) * L], 0.0)
    return out


RET_G = 2


def _ret_fwd(ret_p, cos, sin):
    B, S, _ = ret_p.shape
    L = min(RET_L, S)
    NB = S // L
    G = min(RET_G, NB)
    NG = NB // G
    consts = _ret_consts(L)

    def body(p_ref, c_ref, s_ref, ds_ref, qw_ref, kw_ref, a_ref, out_ref, raw_ref, st_ref, st_sc):
        @pl.when(pl.program_id(1) == 0)
        def _():
            st_sc[...] = jnp.zeros_like(st_sc)

        mh, mv, bd = _ret_masks()
        cs_ = range(G)
        rows = [slice(c * L, (c + 1) * L) for c in cs_]
        ps = [p_ref[0, rows[c], :].astype(F32) for c in cs_]
        qk = [_ret_rope(ps[c], c_ref[0, rows[c], :], s_ref[0, rows[c], :]) for c in cs_]
        vs = [ps[c][:, 512:768] for c in cs_]
        a_s = [_mm_nt(_stack_heads(qk[c][0], mh), qk[c][1]) for c in cs_]
        upd = [_mm_tn(vs[c], qk[c][1] * kw_ref[...]) for c in cs_]
        o_s = [_mm(a_s[c] * ds_ref[...], vs[c]) for c in cs_]
        st = st_sc[...]
        inter = []
        for c in cs_:
            st_ref[0, c] = st
            inter.append(_mm_nt(qk[c][0] * qw_ref[...], st))
            st = st * a_ref[...] + jnp.where(bd, upd[c], 0.0)
        st_sc[...] = st
        for c in cs_:
            r = _fold_heads(o_s[c], mv, L) + inter[c]
            raw_ref[0, rows[c], :] = r
            rstd = lax.rsqrt(_head_mean(r * r, mv, 64.0) + EPS)
            out_ref[0, rows[c], :] = (r * rstd * _silu(ps[c][:, 768:1024])).astype(_MXU)

    tok = lambda w: pl.BlockSpec((1, G * L, w), lambda b, n: (b, n, 0))
    return pl.pallas_call(
        body, name="ret_fwd", grid=(B, NG),
        in_specs=[tok(RET_W), tok(128), tok(128), _full((4 * L, L)), _full((L, 256)), _full((L, 256)),
                  _full((1, 256))],
        out_specs=[tok(256), tok(256), pl.BlockSpec((1, G, 256, 256), lambda b, n: (b, n, 0, 0))],
        out_shape=[jax.ShapeDtypeStruct((B, S, 256), _MXU), jax.ShapeDtypeStruct((B, S, 256), F32),
                   jax.ShapeDtypeStruct((B, NB, 256, 256), F32)],
        scratch_shapes=[pltpu.VMEM((256, 256), F32)],
        compiler_params=_cp(("parallel", "arbitrary")),
    )(ret_p, cos, sin, *consts)


def _ret_bwd(ret_p, cos, sin, raw, states, d_mix):
    B, S, _ = ret_p.shape
    L = min(RET_L, S)
    NB = S // L
    G = 1
    NG = NB // G
    consts = _ret_consts(L)

    def body(p_ref, c_ref, s_ref, raw_ref, st_ref, dm_ref, ds_ref, qw_ref, kw_ref, a_ref, dp_ref, dst_sc):
        @pl.when(pl.program_id(1) == 0)
        def _():
            dst_sc[...] = jnp.zeros_like(dst_sc)

        mh, mv, bd = _ret_masks()
        qw, kw, dec = qw_ref[...], kw_ref[...], ds_ref[...]
        cs_ = range(G)
        rows = [slice(c * L, (c + 1) * L) for c in cs_]
        ps = [p_ref[0, rows[c], :].astype(F32) for c in cs_]
        tabs = [(c_ref[0, rows[c], :], s_ref[0, rows[c], :]) for c in cs_]
        qk = [_ret_rope(ps[c], *tabs[c]) for c in cs_]
        vs = [ps[c][:, 512:768] for c in cs_]
        qs = [_stack_heads(qk[c][0], mh) for c in cs_]
        a_s = [_mm_nt(qs[c], qk[c][1]) for c in cs_]
        dr, dz = [], []
        for c in cs_:
            r = raw_ref[0, rows[c], :]
            z = ps[c][:, 768:1024]
            rstd = lax.rsqrt(_head_mean(r * r, mv, 64.0) + EPS)
            rn = r * rstd
            dm = dm_ref[0, rows[c], :]
            d_rn = dm * _silu(z)
            dz.append(dm * rn * _dsilu(z))
            dr.append(rstd * (d_rn - rn * _head_mean(d_rn * rn, mv, 64.0)))
        do_s = [_stack_heads(dr[c], mv) for c in cs_]
        da_s = [_mm_nt(do_s[c], vs[c]) for c in cs_]
        sts = [st_ref[0, c] for c in cs_]
        dq_st = [_mm(dr[c], sts[c]) for c in cs_]
        dst_in = [_mm_tn(dr[c], qk[c][0] * qw) for c in cs_]
        dv = [_mm_tn(a_s[c] * dec, do_s[c]) for c in cs_]
        dqr, dkr = [], []
        for c in cs_:
            da = da_s[c] * dec
            dqr.append(_fold_heads(_mm(da, qk[c][1]), mh, L) + dq_st[c] * qw)
            dkr.append(_mm_tn(da, qs[c]))
        dst_next = dst_sc[...]
        for c in reversed(cs_):
            g = jnp.where(bd, dst_next, 0.0)
            dv[c] = dv[c] + _mm_nt(qk[c][1] * kw, g)
            dkr[c] = dkr[c] + _mm(vs[c], g) * kw
            dst_next = dst_next * a_ref[...] + jnp.where(bd, dst_in[c], 0.0)
        dst_sc[...] = dst_next
        for c in cs_:
            cs, sn = tabs[c]
            dk = dkr[c] * RET_KSCALE
            dq1, dq2 = dqr[c][:, 0:128], dqr[c][:, 128:256]
            dk1, dk2 = dk[:, 0:128], dk[:, 128:256]
            dp_ref[0, rows[c], :] = jnp.concatenate(
                [dq1 * cs + dq2 * sn, dq2 * cs - dq1 * sn, dk1 * cs + dk2 * sn, dk2 * cs - dk1 * sn, dv[c], dz[c]],
                axis=1).astype(_MXU)

    tok = lambda w: pl.BlockSpec((1, G * L, w), lambda b, i: (b, NG - 1 - i, 0))
    return pl.pallas_call(
        body, name="ret_bwd", grid=(B, NG),
        in_specs=[tok(RET_W), tok(128), tok(128), tok(256),
                  pl.BlockSpec((1, G, 256, 256), lambda b, i: (b, NG - 1 - i, 0, 0)), tok(256),
                  _full((4 * L, L)), _full((L, 256)), _full((L, 256)), _full((1, 256))],
        out_specs=tok(RET_W), out_shape=jax.ShapeDtypeStruct((B, S, RET_W), _MXU),
        scratch_shapes=[pltpu.VMEM((256, 256), F32)],
        compiler_params=_cp(("parallel", "arbitrary")),
    )(ret_p, cos, sin, raw, states, d_mix, *consts)


def _gla_masks():
    C = CHUNK
    lk = lax.broadcasted_iota(jnp.int32, (1, 128), 1)
    lv = lax.broadcasted_iota(jnp.int32, (1, 256), 1)
    mk = [(lk // 32) == h for h in range(4)]
    mv = [(lv // 64) == h for h in range(4)]
    vi = lax.broadcasted_iota(jnp.int32, (256, 128), 0)
    ki = lax.broadcasted_iota(jnp.int32, (256, 128), 1)
    bd = (vi // 64) == (ki // 32)
    ri = lax.broadcasted_iota(jnp.int32, (4 * C, C), 0) % C
    cj = lax.broadcasted_iota(jnp.int32, (4 * C, C), 1)
    lower = ri >= cj
    ti = lax.broadcasted_iota(jnp.int32, (C, C), 0)
    tj = lax.broadcasted_iota(jnp.int32, (C, C), 1)
    ltri = jnp.where(ti >= tj, 1.0, 0.0).astype(F32)
    utri = jnp.where(ti <= tj, 1.0, 0.0).astype(F32)
    return mk, mv, bd, lower, ltri, utri


def _log_sigmoid(x):
    return jnp.minimum(x, 0.0) - jnp.log(1.0 + jnp.exp(-jnp.abs(x)))


GLA_G = 8


def _gla_fwd(gla_p, w_g2p, b_g2, gnw):
    B, S, _ = gla_p.shape
    C = CHUNK
    NC = S // C
    G = min(GLA_G, NC)
    NG = NC // G

    def body(p_ref, w_ref, b_ref, gn_ref, out_ref, raw_ref, st_ref, st_sc):
        @pl.when(pl.program_id(1) == 0)
        def _():
            st_sc[...] = jnp.zeros_like(st_sc)

        mk, mv, bd, lower, ltri, _ = _gla_masks()
        cs = range(G)
        rows = [slice(c * C, (c + 1) * C) for c in cs]
        ps = [p_ref[0, rows[c], :].astype(F32) for c in cs]
        pre = [_mm(ps[c][:, 512:640], w_ref[...]) + b_ref[...] for c in cs]
        cum = [_mm_f32(ltri, _log_sigmoid(pre[c]) * (1.0 / GLA_TAU)) for c in cs]
        past, fut, upd, q_pos, a_row = [], [], [], [], []
        for c in cs:
            q = ps[c][:, 0:128]
            k = ps[c][:, 128:256] * GLA_KSCALE
            last = cum[c][C - 1:C, :]
            e_pos = jnp.exp(cum[c])
            e_neg = jnp.exp(-cum[c])
            q_pos.append(q * e_pos)
            a_row.append(jnp.exp(last))
            past.append(_mm_nt(_stack_heads(q_pos[c], mk), k * e_neg))
            fut.append(_mm_nt(_stack_heads(q * e_neg, mk), k * e_pos))
            upd.append(_mm_tn(ps[c][:, 256:512], k * jnp.exp(last - cum[c])))
        o_s = [_mm(jnp.where(lower, past[c], fut[c]), ps[c][:, 256:512]) for c in cs]
        st = st_sc[...]
        inter = []
        for c in cs:
            st_ref[0, c] = st
            inter.append(_mm_nt(q_pos[c], st))
            st = st * a_row[c] + jnp.where(bd, upd[c], 0.0)
        st_sc[...] = st
        for c in cs:
            g = _fold_heads(o_s[c], mv, C) + inter[c]
            raw_ref[0, rows[c], :] = g
            rstd = lax.rsqrt(_head_mean(g * g, mv, 64.0) + EPS)
            out_ref[0, rows[c], :] = (g * rstd * gn_ref[...] * _silu(ps[c][:, 640:896])).astype(_MXU)

    tok = lambda w: pl.BlockSpec((1, G * C, w), lambda b, n: (b, n, 0))
    return pl.pallas_call(
        body, name="gla_fwd", grid=(B, NG),
        in_specs=[tok(GLA_W), _full((128, 128)), _full((1, 128)), _full((1, 256))],
        out_specs=[tok(256), tok(256), pl.BlockSpec((1, G, 256, 128), lambda b, n: (b, n, 0, 0))],
        out_shape=[jax.ShapeDtypeStruct((B, S, 256), _MXU), jax.ShapeDtypeStruct((B, S, 256), F32),
                   jax.ShapeDtypeStruct((B, NC, 256, 128), F32)],
        scratch_shapes=[pltpu.VMEM((256, 128), F32)],
        compiler_params=_cp(("parallel", "arbitrary")),
    )(gla_p, w_g2p, b_g2, gnw)


def _gla_bwd(gla_p, w_g2p, b_g2, gnw, raw, states, d_mix):
    B, S, _ = gla_p.shape
    C = CHUNK
    NC = S // C
    G = min(GLA_G, NC)
    NG = NC // G

    def body(p_ref, w_ref, b_ref, gn_ref, raw_ref, st_ref, dm_ref, dp_ref, dw_ref, db_ref, dgn_ref, dst_sc):
        first = (pl.program_id(0) == 0) & (pl.program_id(1) == 0)

        @pl.when(first)
        def _():
            dw_ref[...] = jnp.zeros_like(dw_ref)
            db_ref[...] = jnp.zeros_like(db_ref)
            dgn_ref[...] = jnp.zeros_like(dgn_ref)

        @pl.when(pl.program_id(1) == 0)
        def _():
            dst_sc[...] = jnp.zeros_like(dst_sc)

        mk, mv, bd, lower, ltri, utri = _gla_masks()
        gn = gn_ref[...]
        cs = range(G)
        rows = [slice(c * C, (c + 1) * C) for c in cs]
        ps = [p_ref[0, rows[c], :].astype(F32) for c in cs]
        vs = [ps[c][:, 256:512] for c in cs]
        pre = [_mm(ps[c][:, 512:640], w_ref[...]) + b_ref[...] for c in cs]
        cum = [_mm_f32(ltri, _log_sigmoid(pre[c]) * (1.0 / GLA_TAU)) for c in cs]
        dg, dz, dgn_acc = [], [], jnp.zeros((1, 256), F32)
        for c in cs:
            g = raw_ref[0, rows[c], :]
            z = ps[c][:, 640:896]
            rstd = lax.rsqrt(_head_mean(g * g, mv, 64.0) + EPS)
            gh = g * rstd
            dm = dm_ref[0, rows[c], :]
            d_gn = dm * _silu(z)
            dz.append(dm * gh * gn * _dsilu(z))
            dgn_acc = dgn_acc + jnp.sum(d_gn * gh, axis=0, keepdims=True)
            d_gh = d_gn * gn
            dg.append(rstd * (d_gh - gh * _head_mean(d_gh * gh, mv, 64.0)))
        do_s = [_stack_heads(dg[c], mv) for c in cs]
        dattn = [_mm_nt(do_s[c], vs[c]) for c in cs]
        ks, e_pos, e_neg, q_pos, q_neg, k_pos, k_neg, qp_s, qn_s, past, fut, a_row, w_dec, kd = ([] for _ in range(14))
        for c in cs:
            q = ps[c][:, 0:128]
            k = ps[c][:, 128:256] * GLA_KSCALE
            last = cum[c][C - 1:C, :]
            ep, en = jnp.exp(cum[c]), jnp.exp(-cum[c])
            ks.append(k), e_pos.append(ep), e_neg.append(en)
            q_pos.append(q * ep), q_neg.append(q * en), k_pos.append(k * ep), k_neg.append(k * en)
            qp_s.append(_stack_heads(q_pos[c], mk)), qn_s.append(_stack_heads(q_neg[c], mk))
            past.append(_mm_nt(qp_s[c], k_neg[c]))
            fut.append(_mm_nt(qn_s[c], k_pos[c]))
            a_row.append(jnp.exp(last))
            w_dec.append(jnp.exp(last - cum[c]))
            kd.append(k * w_dec[c])
        sts = [st_ref[0, c] for c in cs]
        dq_st = [_mm(dg[c], sts[c]) for c in cs]
        dst_in = [_mm_tn(dg[c], q_pos[c]) for c in cs]
        dv, dq_pos, dk_neg, dq_neg, dk_pos = [], [], [], [], []
        for c in cs:
            attn = jnp.where(lower, past[c], fut[c])
            dpast = jnp.where(lower, dattn[c], 0.0)
            dfut = jnp.where(lower, 0.0, dattn[c])
            dv.append(_mm_tn(attn, do_s[c]))
            dq_pos.append(_fold_heads(_mm(dpast, k_neg[c]), mk, C) + dq_st[c])
            dk_neg.append(_mm_tn(dpast, qp_s[c]))
            dq_neg.append(_fold_heads(_mm(dfut, k_pos[c]), mk, C))
            dk_pos.append(_mm_tn(dfut, qn_s[c]))
        dst_next = dst_sc[...]
        d_a, d_kd = [None] * G, [None] * G
        for c in reversed(cs):
            d_a[c] = jnp.sum(dst_next * sts[c], axis=0, keepdims=True)
            gmat = jnp.where(bd, dst_next, 0.0)
            d_kd[c] = _mm(vs[c], gmat)
            dv[c] = dv[c] + _mm_nt(kd[c], gmat)
            dst_next = dst_next * a_row[c] + jnp.where(bd, dst_in[c], 0.0)
        dst_sc[...] = dst_next
        row = lax.broadcasted_iota(jnp.int32, (C, 128), 0)
        d_la, dk, dq = [], [], []
        for c in cs:
            t = d_kd[c] * kd[c]
            dk.append(d_kd[c] * w_dec[c] + dk_neg[c] * e_neg[c] + dk_pos[c] * e_pos[c])
            dq.append(dq_pos[c] * e_pos[c] + dq_neg[c] * e_neg[c])
            d_last = jnp.sum(t, axis=0, keepdims=True) + d_a[c] * a_row[c]
            d_cum = (dq_pos[c] * q_pos[c] - dk_neg[c] * k_neg[c] - dq_neg[c] * q_neg[c] + dk_pos[c] * k_pos[c] - t)
            d_la.append(_mm_f32(utri, d_cum + jnp.where(row == C - 1, d_last, 0.0)))
        d_pre = [d_la[c] * _sig(-pre[c]) * (1.0 / GLA_TAU) for c in cs]
        d_gg = [_mm_nt(d_pre[c], w_ref[...]) for c in cs]
        dw_acc = _mm_tn(ps[0][:, 512:640], d_pre[0])
        db_acc = jnp.sum(d_pre[0], axis=0, keepdims=True)
        for c in cs[1:]:
            dw_acc = dw_acc + _mm_tn(ps[c][:, 512:640], d_pre[c])
            db_acc = db_acc + jnp.sum(d_pre[c], axis=0, keepdims=True)
        for c in cs:
            dp_ref[0, rows[c], :] = jnp.concatenate([dq[c], dk[c] * GLA_KSCALE, dv[c], d_gg[c], dz[c]],
                                                    axis=1).astype(_MXU)
        dw_ref[...] += dw_acc
        db_ref[...] += db_acc
        dgn_ref[...] += dgn_acc

        @pl.when((pl.program_id(0) == B - 1) & (pl.program_id(1) == NG - 1))
        def _():
            s1 = dgn_ref[...]
            s1 = s1 + pltpu.roll(s1, 128, 1)
            dgn_ref[...] = s1 + pltpu.roll(s1, 64, 1)

    tok = lambda w: pl.BlockSpec((1, G * C, w), lambda b, i: (b, NG - 1 - i, 0))
    return pl.pallas_call(
        body, name="gla_bwd", grid=(B, NG),
        in_specs=[tok(GLA_W), _full((128, 128)), _full((1, 128)), _full((1, 256)), tok(256),
                  pl.BlockSpec((1, G, 256, 128), lambda b, i: (b, NG - 1 - i, 0, 0)), tok(256)],
        out_specs=[tok(GLA_W), _full((128, 128)), _full((1, 128)), _full((1, 256))],
        out_shape=[jax.ShapeDtypeStruct((B, S, GLA_W), _MXU), jax.ShapeDtypeStruct((128, 128), F32),
                   jax.ShapeDtypeStruct((1, 128), F32), jax.ShapeDtypeStruct((1, 256), F32)],
        scratch_shapes=[pltpu.VMEM((256, 128), F32)],
        compiler_params=_cp(("arbitrary", "arbitrary")),
    )(gla_p, w_g2p, b_g2, gnw, raw, states, d_mix)


def _rms(x, w):
    rstd = lax.rsqrt(jnp.mean(x * x, axis=-1, keepdims=True) + EPS)
    xh = x * rstd
    return xh, rstd, xh * w


def _rms_bwd(dy, xh, rstd, w):
    dxh = dy * w
    return rstd * (dxh - xh * jnp.mean(dxh * xh, axis=-1, keepdims=True))


MLA_T = 256


def _mla_prep_fwd(mla_p, cos, sin, qnw, kvnw, w_uq, w_ukv):
    B, S, _ = mla_p.shape
    tm = min(S, 512)

    t = min(MLA_T, S)
    nt = tm // t

    def body(p_ref, c_ref, s_ref, qn_ref, kn_ref, wq_ref, wkv_ref, wkvt_ref, q_ref, k_ref, v_ref, kt_ref, vt_ref):
        p = p_ref[0].astype(F32)
        cs, sn = c_ref[0], s_ref[0]
        _, _, qn = _rms(p[:, 0:256], qn_ref[...])
        qpre = _mm(qn, wq_ref[...])
        _, _, kvn = _rms(p[:, 256:384], kn_ref[...])
        kv = _mm(kvn, wkv_ref[...])
        kvt = _mm_nt(wkvt_ref[...], kvn)
        kpe = _rope128(p[:, 384:512], cs, sn)
        kpet = kpe.T
        for h in range(8):
            sl = slice(128 * h, 128 * h + 128)
            q_ref[0, :, sl] = _rope128(qpre[:, sl], cs, sn).astype(_MXU)
            k_ref[0, :, sl] = (kv[:, sl] + kpe).astype(_MXU)
            kht = kvt[sl, :] + kpet
            for n in range(nt):
                kt_ref[0, n, sl, :] = kht[:, n * t:(n + 1) * t].astype(_MXU)
        v_ref[0] = kv[:, 1024:1536].astype(_MXU)
        for n in range(nt):
            vt_ref[0, n] = kvt[1024:1536, n * t:(n + 1) * t].astype(_MXU)

    tok = lambda w: pl.BlockSpec((1, tm, w), lambda b, i: (b, i, 0))
    tr = lambda w: pl.BlockSpec((1, nt, w, t), lambda b, i: (b, i, 0, 0))
    return pl.pallas_call(
        body, name="mla_prep_fwd", grid=(B, S // tm),
        in_specs=[tok(512), tok(128), tok(128), _full((1, 256)), _full((1, 128)), _full((256, 1024)),
                  _full((128, 1536)), _full((1536, 128))],
        out_specs=[tok(1024), tok(1024), tok(512), tr(1024), tr(512)],
        out_shape=[jax.ShapeDtypeStruct((B, S, 1024), _MXU), jax.ShapeDtypeStruct((B, S, 1024), _MXU),
                   jax.ShapeDtypeStruct((B, S, 512), _MXU), jax.ShapeDtypeStruct((B, S // t, 1024, t), _MXU),
                   jax.ShapeDtypeStruct((B, S // t, 512, t), _MXU)],
        compiler_params=_cp(("parallel", "parallel")),
    )(mla_p, cos, sin, qnw, kvnw, w_uq, w_ukv, w_ukv.T)


def _chunk_mask_t(t):
    kj = lax.broadcasted_iota(jnp.int32, (t, t), 0) // CHUNK
    qi = lax.broadcasted_iota(jnp.int32, (t, t), 1) // CHUNK
    return kj <= qi


MLA_HG = 8
MLA_HG_FWD = 8
LOG2E = 1.4426950408889634
MLA_C2 = MLA_SCALE * LOG2E


def _mla_attn_fwd(q, k, vt):
    B, S, _ = q.shape
    t = min(MLA_T, S)
    nq = S // t
    HG = MLA_HG_FWD
    NP = HG // 2

    def body(q_ref, k_ref, vt_ref, o_ref, lse_ref, sa, sb, sc, m_sc, l_sc, acc_sc):
        i = pl.program_id(2)
        row = lax.broadcasted_iota(jnp.int32, (128, 1), 0)
        low = row < 64
        mask = _chunk_mask_t(t)
        m_sc[...] = jnp.full(m_sc.shape, -jnp.inf, F32)
        l_sc[...] = jnp.zeros_like(l_sc)
        acc_sc[...] = jnp.zeros_like(acc_sc)

        ones = jnp.ones((8, t), _MXU)

        def scores(j, buf, qi=None):
            kb = k_ref[0, pl.ds(pl.multiple_of(j * t, t), t), :]
            qb = q_ref[0, pl.ds(pl.multiple_of((i if qi is None else qi) * t, t), t), :]
            for h in range(HG):
                cols = slice(128 * h, 128 * h + 128)
                buf[h] = (_mm_nt(kb[:, cols], qb[:, cols]) * MLA_C2).astype(_MXU)

        def absorb(j, buf, masked):
            vtb = vt_ref[0, j]
            for pr in range(NP):
                alphas, pvs = [], []
                for hh in range(2):
                    h = 2 * pr + hh
                    s = buf[h]
                    if masked:
                        s = jnp.where(mask, s, jnp.full_like(s, -jnp.inf))
                    m_old = m_sc[h]
                    m_new = jnp.maximum(m_old, jnp.max(s, axis=0, keepdims=True).astype(F32))
                    alpha = jnp.exp2(m_old - m_new)
                    p = jnp.exp2(s - m_new.astype(_MXU))
                    l_sc[h] = alpha * l_sc[h] + _mm(ones, p)[0:1, :]
                    m_sc[h] = m_new
                    vth = vtb[128 * pr:128 * pr + 128, :]
                    vth = jnp.where(low if hh == 0 else ~low, vth, jnp.zeros_like(vth))
                    pvs.append(_mm(vth, p))
                    alphas.append(alpha)
                acc_sc[pr] = acc_sc[pr] * jnp.where(low, alphas[0], alphas[1]) + pvs[0] + pvs[1]

        nxt = jnp.minimum(i + 1, nq - 1)

        @pl.when(i == 0)
        def _():
            scores(0, sc)

        scores(0, sb)
        absorb(i, sc, True)

        def pair(jj, carry):
            j0 = 2 * jj
            scores(j0 + 1, sa)
            absorb(j0, sb, False)
            scores(j0 + 2, sb)
            absorb(j0 + 1, sa, False)
            return carry

        lax.fori_loop(0, (i - 1) // 2, pair, 0)

        @pl.when(i == 0)
        def _():
            scores(nxt, sc, nxt)

        @pl.when(i % 2 == 1)
        def _():
            scores(nxt, sc, nxt)
            absorb(i - 1, sb, False)

        @pl.when((i % 2 == 0) & (i > 0))
        def _():
            scores(i - 1, sa)
            absorb(i - 2, sb, False)
            scores(nxt, sc, nxt)
            absorb(i - 1, sa, False)

        for pr in range(NP):
            l_e, l_o = l_sc[2 * pr], l_sc[2 * pr + 1]
            o_ref[0, :, 128 * pr:128 * pr + 128] = (acc_sc[pr] / jnp.where(low, l_e, l_o)).T
            lse_ref[0, pr, 0, 0:1, :] = m_sc[2 * pr] + jnp.log(l_e) * LOG2E
            lse_ref[0, pr, 0, 1:2, :] = m_sc[2 * pr + 1] + jnp.log(l_o) * LOG2E

    return pl.pallas_call(
        body, name="mla_attn_fwd", grid=(B, 8 // HG, nq),
        in_specs=[pl.BlockSpec((1, S, 128 * HG), lambda b, g, i: (b, 0, g)),
                  pl.BlockSpec((1, S, 128 * HG), lambda b, g, i: (b, 0, g)),
                  pl.BlockSpec((1, nq, 64 * HG, t), lambda b, g, i: (b, 0, g, 0))],
        out_specs=[pl.BlockSpec((1, t, 64 * HG), lambda b, g, i: (b, i, g)),
                   pl.BlockSpec((1, NP, 1, 2, t), lambda b, g, i: (b, g, i, 0, 0))],
        out_shape=[jax.ShapeDtypeStruct((B, S, 512), F32), jax.ShapeDtypeStruct((B, 4, nq, 2, t), F32)],
        scratch_shapes=[pltpu.VMEM((HG, t, t), _MXU), pltpu.VMEM((HG, t, t), _MXU), pltpu.VMEM((HG, t, t), _MXU),
                        pltpu.VMEM((HG, 1, t), F32),
                        pltpu.VMEM((HG, 1, t), F32), pltpu.VMEM((NP, 128, t), F32)],
        compiler_params=_cp(("parallel", "parallel", "arbitrary")),
    )(q, k, vt)


def _mla_attn_bwd(q, k, v, kt, do, lse, dl):
    B, S, _ = q.shape
    t = min(MLA_T, S)
    nk = S // t

    HG = MLA_HG
    NP = HG // 2

    def body(q_ref, k_ref, v_ref, kt_ref, do_ref, lse_ref, dl_ref, dq_ref, dk_ref, dv_ref,
             sa, da, sb, db, dqt_sc, dk_sc, dv_sc):
        j = pl.program_id(2)

        @pl.when(j == 0)
        def _():
            dqt_sc[...] = jnp.zeros_like(dqt_sc)

        dk_sc[...] = jnp.zeros_like(dk_sc)
        dv_sc[...] = jnp.zeros_like(dv_sc)
        lane = lax.broadcasted_iota(jnp.int32, (1, 128), 1)
        low = lane < 64
        mask = _chunk_mask_t(t)

        def half(x, hh):
            return jnp.where(low if hh == 0 else ~low, x, jnp.zeros_like(x))

        def prepare(i, sbuf, dbuf):
            rows = pl.ds(pl.multiple_of(i * t, t), t)
            for h in range(HG):
                cols = slice(128 * h, 128 * h + 128)
                pc = slice(128 * (h // 2), 128 * (h // 2) + 128)
                sbuf[h] = _mm_nt(k_ref[0, :, cols], q_ref[0, rows, cols]) * MLA_C2
                dbuf[h] = _mm_nt(half(v_ref[0, :, pc], h % 2), do_ref[0, rows, pc])

        def absorb(i, sbuf, dbuf, masked):
            rows = pl.ds(pl.multiple_of(i * t, t), t)
            for h in range(HG):
                pr, hh = h // 2, h % 2
                cols = slice(128 * h, 128 * h + 128)
                pc = slice(128 * pr, 128 * pr + 128)
                p = jnp.exp2(sbuf[h] - lse_ref[0, pr, i][hh:hh + 1, :])
                if masked:
                    p = jnp.where(mask, p, 0.0)
                dv_sc[pr] += _mm(p, half(do_ref[0, rows, pc], hh))
                ds = p * (dbuf[h] - dl_ref[0, pr, i][hh:hh + 1, :])
                dqt_sc[i, cols, :] += _mm(kt_ref[0, 0, cols, :], ds)
                dk_sc[h] += _mm(ds, q_ref[0, rows, cols])

        n = nk - 1 - j
        prepare(jnp.minimum(j + 1, nk - 1), sb, db)

        def pair(jj, carry):
            i0 = j + 1 + 2 * jj
            prepare(i0 + 1, sa, da)
            absorb(i0, sb, db, False)
            prepare(jnp.where(i0 + 2 <= nk - 1, i0 + 2, j), sb, db)
            absorb(i0 + 1, sa, da, False)
            return carry

        lax.fori_loop(0, n // 2, pair, 0)

        @pl.when(n % 2 == 1)
        def _():
            prepare(j, sa, da)
            absorb(nk - 1, sb, db, False)
            absorb(j, sa, da, True)

        @pl.when(n % 2 == 0)
        def _():
            absorb(j, sb, db, True)

        for h in range(HG):
            dk_ref[0, :, 128 * h:128 * h + 128] = (dk_sc[h] * MLA_SCALE).astype(_MXU)
        for pr in range(NP):
            dv_ref[0, :, 128 * pr:128 * pr + 128] = dv_sc[pr].astype(_MXU)

        @pl.when(j == nk - 1)
        def _():
            for i in range(nk):
                dq_ref[0, i * t:(i + 1) * t, :] = (dqt_sc[i].T * MLA_SCALE).astype(_MXU)

    seq = lambda w: pl.BlockSpec((1, S, w), lambda b, g, j: (b, 0, g))
    blk = lambda w: pl.BlockSpec((1, t, w), lambda b, g, j: (b, j, g))
    stat = pl.BlockSpec((1, NP, nk, 2, t), lambda b, g, j: (b, g, 0, 0, 0))
    return pl.pallas_call(
        body, name="mla_attn_bwd", grid=(B, 8 // HG, nk),
        in_specs=[seq(128 * HG), blk(128 * HG), blk(64 * HG),
                  pl.BlockSpec((1, 1, 128 * HG, t), lambda b, g, j: (b, j, g, 0)), seq(64 * HG), stat, stat],
        out_specs=[seq(128 * HG), blk(128 * HG), blk(64 * HG)],
        out_shape=[jax.ShapeDtypeStruct((B, S, 1024), _MXU), jax.ShapeDtypeStruct((B, S, 1024), _MXU),
                   jax.ShapeDtypeStruct((B, S, 512), _MXU)],
        scratch_shapes=[pltpu.VMEM((HG, t, t), F32), pltpu.VMEM((HG, t, t), F32), pltpu.VMEM((HG, t, t), F32),
                        pltpu.VMEM((HG, t, t), F32), pltpu.VMEM((nk, 128 * HG, t), F32),
                        pltpu.VMEM((HG, t, 128), F32), pltpu.VMEM((NP, t, 128), F32)],
        compiler_params=_cp(("parallel", "parallel", "arbitrary"), 56),
    )(q, k, v, kt, do, lse, dl)


def _mla_prep_bwd(mla_p, cos, sin, qnw, kvnw, w_uq, w_ukv, dq, dk, dv):
    B, S, _ = mla_p.shape
    tm = min(S, 512)

    def body(p_ref, c_ref, s_ref, qn_ref, kn_ref, wq_ref, wkv_ref, dq_ref, dk_ref, dv_ref,
             dp_ref, dwq_ref, dwkv_ref, dqn_ref, dkn_ref):
        first = (pl.program_id(0) == 0) & (pl.program_id(1) == 0)

        @pl.when(first)
        def _():
            dwq_ref[...] = jnp.zeros_like(dwq_ref)
            dwkv_ref[...] = jnp.zeros_like(dwkv_ref)
            dqn_ref[...] = jnp.zeros_like(dqn_ref)
            dkn_ref[...] = jnp.zeros_like(dkn_ref)

        p = p_ref[0].astype(F32)
        cs, sn = c_ref[0], s_ref[0]
        lane = lax.broadcasted_iota(jnp.int32, (1, 128), 1)
        pe = (lane >= 64) & (lane < 96)
        qh, q_rstd, qn = _rms(p[:, 0:256], qn_ref[...])
        kvh, kv_rstd, kvn = _rms(p[:, 256:384], kn_ref[...])
        dqv = dq_ref[0].astype(F32)
        dkv = dk_ref[0].astype(F32)
        dqpre = jnp.concatenate(
            [_rope128_t(dqv[:, 128 * h:128 * h + 128], cs, sn) for h in range(8)], axis=1)
        dkpe = jnp.zeros((tm, 128), F32)
        for h in range(8):
            dkpe = dkpe + jnp.where(pe, dkv[:, 128 * h:128 * h + 128], 0.0)
        dkr = _rope128_t(dkpe, cs, sn)
        dkv_all = jnp.concatenate([dkv, dv_ref[0].astype(F32)], axis=1)
        d_qn = _mm_nt(dqpre, wq_ref[...])
        d_kvn = _mm_nt(dkv_all, wkv_ref[...])
        dwq_ref[...] += _mm_tn(qn, dqpre)
        dwkv_ref[...] += _mm_tn(kvn, dkv_all)
        dqn_ref[...] += jnp.sum(d_qn * qh, axis=0, keepdims=True)
        dkn_ref[...] += jnp.sum(d_kvn * kvh, axis=0, keepdims=True)
        dp_ref[0] = jnp.concatenate([_rms_bwd(d_qn, qh, q_rstd, qn_ref[...]),
                                     _rms_bwd(d_kvn, kvh, kv_rstd, kn_ref[...]), dkr], axis=1).astype(_MXU)

    tok = lambda w: pl.BlockSpec((1, tm, w), lambda b, i: (b, i, 0))
    return pl.pallas_call(
        body, name="mla_prep_bwd", grid=(B, S // tm),
        in_specs=[tok(512), tok(128), tok(128), _full((1, 256)), _full((1, 128)), _full((256, 1024)),
                  _full((128, 1536)), tok(1024), tok(1024), tok(512)],
        out_specs=[tok(512), _full((256, 1024)), _full((128, 1536)), _full((1, 256)), _full((1, 128))],
        out_shape=[jax.ShapeDtypeStruct((B, S, 512), _MXU), jax.ShapeDtypeStruct((256, 1024), F32),
                   jax.ShapeDtypeStruct((128, 1536), F32), jax.ShapeDtypeStruct((1, 256), F32),
                   jax.ShapeDtypeStruct((1, 128), F32)],
        compiler_params=_cp(("arbitrary", "arbitrary")),
    )(mla_p, cos, sin, qnw, kvnw, w_uq, w_ukv, dq, dk, dv)


def _out_fwd(x, gate, r_g, o_mla, mla_p, g_g, w_out):
    B, S, D = x.shape
    tm = min(S, 512)

    def body(x_ref, g_ref, r_ref, o_ref, z_ref, gg_ref, w_ref, xn_ref, y_ref):
        mm = (o_ref[0] * _silu(z_ref[0].astype(F32))).astype(_MXU)
        y = (jnp.dot(r_ref[0], w_ref[0:256, :], preferred_element_type=F32)
             + jnp.dot(mm, w_ref[256:768, :], preferred_element_type=F32)
             + jnp.dot(gg_ref[0], w_ref[768:1024, :], preferred_element_type=F32))
        y_ref[0] = y.astype(_MXU)
        xn_ref[0] = x_ref[0] + g_ref[0] * y

    tok = lambda w, c=0: pl.BlockSpec((1, tm, w), lambda b, i: (b, i, c))
    return pl.pallas_call(
        body, name="out_fwd", grid=(B, S // tm),
        in_specs=[tok(D), pl.BlockSpec((1, 1, D), lambda b, i: (b, 0, 0)), tok(256), tok(512), tok(512, 1),
                  tok(256), _full((D, D))],
        out_specs=[tok(D), tok(D)],
        out_shape=[jax.ShapeDtypeStruct((B, S, D), F32), jax.ShapeDtypeStruct((B, S, D), _MXU)],
        compiler_params=_cp(("parallel", "parallel")),
    )(x, gate, r_g, o_mla, mla_p, g_g, w_out)


def _out_bwd(dx, y, gate, r_g, g_g, w_out, o_mla, mla_p):
    B, S, D = dx.shape
    tm = min(S, 512)
    t = min(MLA_T, S)
    nt = tm // t

    def body(dx_ref, y_ref, g_ref, r_ref, gg_ref, w_ref, o_ref, z_ref,
             dr_ref, do_ref, dz_ref, dl_ref, dg_ref, dw_ref, dgate_ref, acc):
        first = (pl.program_id(0) == 0) & (pl.program_id(1) == 0)

        @pl.when(first)
        def _():
            acc[...] = jnp.zeros_like(acc)

        @pl.when(pl.program_id(1) == 0)
        def _():
            dgate_ref[...] = jnp.zeros_like(dgate_ref)

        dxv = dx_ref[0]
        dgate_ref[0] += jnp.sum(dxv * y_ref[0].astype(F32), axis=0, keepdims=True)
        dy = (dxv * g_ref[0]).astype(_MXU)
        dr_ref[0] = _mm_nt(dy, w_ref[0:256, :])
        dg_ref[0] = _mm_nt(dy, w_ref[768:1024, :])
        ov, z = o_ref[0], z_ref[0].astype(F32)
        acc[0:256, :] += _mm_tn(r_ref[0], dy)
        acc[256:768, :] += _mm_tn((ov * _silu(z)).astype(_MXU), dy)
        acc[768:1024, :] += _mm_tn(gg_ref[0], dy)

        @pl.when((pl.program_id(0) == B - 1) & (pl.program_id(1) == S // tm - 1))
        def _():
            dw_ref[...] = acc[...].astype(_MXU)

        dm = _mm_nt(dy, w_ref[256:768, :])
        do = dm * _silu(z)
        dz_ref[0] = (dm * ov * _dsilu(z)).astype(_MXU)
        do_ref[0] = do.astype(_MXU)
        prod = do * ov
        for pr in range(4):
            pt = prod[:, 128 * pr:128 * pr + 128].T
            se = jnp.sum(pt[0:64], axis=0, keepdims=True)
            so = jnp.sum(pt[64:128], axis=0, keepdims=True)
            for n in range(nt):
                dl_ref[0, pr, n, 0:1, :] = se[:, n * t:(n + 1) * t]
                dl_ref[0, pr, n, 1:2, :] = so[:, n * t:(n + 1) * t]

    tok = lambda w, c=0: pl.BlockSpec((1, tm, w), lambda b, i: (b, i, c))
    per_seq = pl.BlockSpec((1, 1, D), lambda b, i: (b, 0, 0))
    return pl.pallas_call(
        body, name="out_bwd", grid=(B, S // tm),
        in_specs=[tok(D), tok(D), per_seq, tok(256), tok(256), _full((D, D)), tok(512), tok(512, 1)],
        out_specs=[tok(256), tok(512), tok(512), pl.BlockSpec((1, 4, nt, 2, t), lambda b, i: (b, 0, i, 0, 0)),
                   tok(256), _full((D, D)), per_seq],
        out_shape=[jax.ShapeDtypeStruct((B, S, 256), F32), jax.ShapeDtypeStruct((B, S, 512), _MXU),
                   jax.ShapeDtypeStruct((B, S, 512), _MXU), jax.ShapeDtypeStruct((B, 4, S // t, 2, t), F32),
                   jax.ShapeDtypeStruct((B, S, 256), F32), jax.ShapeDtypeStruct((D, D), _MXU),
                   jax.ShapeDtypeStruct((B, 1, D), F32)],
        scratch_shapes=[pltpu.VMEM((D, D), F32)],
        compiler_params=_cp(("arbitrary", "arbitrary")),
    )(dx, y, gate, r_g, g_g, w_out, o_mla, mla_p)


def _proj_bwd_x(x, shift, scale, nw, w_arr, d_ret, d_mla, d_mz, d_gla, dx_out):
    B, S, D = x.shape
    tm = min(S, 512)

    def body(x_ref, sc_ref, nw_ref, w_ref, dr_ref, dm_ref, dz_ref, dg_ref, dxo_ref,
             dx_ref, dsh_ref, dsc_ref, dnw_ref):
        first = (pl.program_id(0) == 0) & (pl.program_id(1) == 0)

        @pl.when(first)
        def _():
            dnw_ref[...] = jnp.zeros_like(dnw_ref)

        @pl.when(pl.program_id(1) == 0)
        def _():
            dsh_ref[...] = jnp.zeros_like(dsh_ref)
            dsc_ref[...] = jnp.zeros_like(dsc_ref)

        dp = jnp.concatenate([dr_ref[0], dm_ref[0], dz_ref[0], dg_ref[0]], axis=1)
        dh = lax.dot_general(dp, w_ref[...], (((1,), (1,)), ((), ())), preferred_element_type=F32)
        xv = x_ref[0]
        rstd = lax.rsqrt(jnp.mean(xv * xv, axis=-1, keepdims=True) + EPS)
        xh = xv * rstd
        nwv = nw_ref[...]
        mod = 1.0 + sc_ref[0]
        dsh_ref[0] += jnp.sum(dh, axis=0, keepdims=True)
        dsc_ref[0] += jnp.sum(dh * xh * nwv, axis=0, keepdims=True)
        dnw_ref[...] += jnp.sum(dh * xh * mod, axis=0, keepdims=True)
        dxh = dh * nwv * mod
        dx_ref[0] = dxo_ref[0] + rstd * (dxh - xh * jnp.mean(dxh * xh, axis=-1, keepdims=True))

    tok = lambda w: pl.BlockSpec((1, tm, w), lambda b, i: (b, i, 0))
    per_seq = pl.BlockSpec((1, 1, D), lambda b, i: (b, 0, 0))
    return pl.pallas_call(
        body, name="proj_bwd_x", grid=(B, S // tm),
        in_specs=[tok(D), per_seq, _full((1, D)), _full((D, ARR_W)), tok(RET_W), tok(512), tok(512),
                  tok(GLA_W), tok(D)],
        out_specs=[tok(D), per_seq, per_seq, _full((1, D))],
        out_shape=[jax.ShapeDtypeStruct((B, S, D), F32), jax.ShapeDtypeStruct((B, 1, D), F32),
                   jax.ShapeDtypeStruct((B, 1, D), F32), jax.ShapeDtypeStruct((1, D), F32)],
        compiler_params=_cp(("arbitrary", "arbitrary")),
    )(x, scale, nw, w_arr, d_ret, d_mla, d_mz, d_gla, dx_out)


def _proj_bwd_w(h, d_ret, d_mla, d_mz, d_gla):
    B, S, D = h.shape
    tm = min(S, 512)

    def body(h_ref, dr_ref, dm_ref, dz_ref, dg_ref, dw_ref, acc):
        first = (pl.program_id(0) == 0) & (pl.program_id(1) == 0)

        @pl.when(first)
        def _():
            acc[...] = jnp.zeros_like(acc)

        hv = h_ref[0]
        tn = lambda d_ref: lax.dot_general(hv, d_ref[0], (((0,), (0,)), ((), ())), preferred_element_type=F32)
        acc[:, 0:RET_W] += tn(dr_ref)
        acc[:, RET_W:RET_W + 512] += tn(dm_ref)
        acc[:, RET_W + 512:RET_W + MLA_W] += tn(dz_ref)
        acc[:, RET_W + MLA_W:ARR_W] += tn(dg_ref)

        @pl.when((pl.program_id(0) == B - 1) & (pl.program_id(1) == S // tm - 1))
        def _():
            dw_ref[...] = acc[...].astype(_MXU)

    tok = lambda w: pl.BlockSpec((1, tm, w), lambda b, i: (b, i, 0))
    return pl.pallas_call(
        body, name="proj_bwd_w", grid=(B, S // tm),
        in_specs=[tok(D), tok(RET_W), tok(512), tok(512), tok(GLA_W)],
        out_specs=_full((D, ARR_W)), out_shape=jax.ShapeDtypeStruct((D, ARR_W), _MXU),
        scratch_shapes=[pltpu.VMEM((D, ARR_W), F32)],
        compiler_params=_cp(("arbitrary", "arbitrary"), 56),
    )(h, d_ret, d_mla, d_mz, d_gla)


def _out_fwd_loss(x, gate, r_g, o_mla, mla_p, g_g, w_out, fw, target):
    B, S, D = x.shape
    tm = min(S, 512)

    def body(x_ref, g_ref, r_ref, o_ref, z_ref, gg_ref, w_ref, fw_ref, t_ref, dx_ref, y_ref, loss_ref, dfw_ref):
        first = (pl.program_id(0) == 0) & (pl.program_id(1) == 0)

        @pl.when(first)
        def _():
            loss_ref[...] = jnp.zeros_like(loss_ref)
            dfw_ref[...] = jnp.zeros_like(dfw_ref)

        mm = (o_ref[0] * _silu(z_ref[0].astype(F32))).astype(_MXU)
        y = (jnp.dot(r_ref[0], w_ref[0:256, :], preferred_element_type=F32)
             + jnp.dot(mm, w_ref[256:768, :], preferred_element_type=F32)
             + jnp.dot(gg_ref[0], w_ref[768:1024, :], preferred_element_type=F32))
        y_ref[0] = y.astype(_MXU)
        xv = x_ref[0] + g_ref[0] * y
        fwv = fw_ref[...]
        rstd = lax.rsqrt(jnp.mean(xv * xv, axis=-1, keepdims=True) + EPS)
        xh = xv * rstd
        err = xh * fwv - t_ref[0]
        loss_ref[...] += 0.5 * jnp.sum(jnp.mean(err * err, axis=-1, keepdims=True), axis=0, keepdims=True)
        dy = err * (1.0 / D)
        dfw_ref[...] += jnp.sum(dy * xh, axis=0, keepdims=True)
        dxh = dy * fwv
        dx_ref[0] = rstd * (dxh - xh * jnp.mean(dxh * xh, axis=-1, keepdims=True))

    tok = lambda w, c=0: pl.BlockSpec((1, tm, w), lambda b, i: (b, i, c))
    return pl.pallas_call(
        body, name="out_fwd_loss", grid=(B, S // tm),
        in_specs=[tok(D), pl.BlockSpec((1, 1, D), lambda b, i: (b, 0, 0)), tok(256), tok(512), tok(512, 1),
                  tok(256), _full((D, D)), _full((1, D)), tok(D)],
        out_specs=[tok(D), tok(D), _full((1, 1)), _full((1, D))],
        out_shape=[jax.ShapeDtypeStruct((B, S, D), F32), jax.ShapeDtypeStruct((B, S, D), _MXU),
                   jax.ShapeDtypeStruct((1, 1), F32), jax.ShapeDtypeStruct((1, D), F32)],
        compiler_params=_cp(("arbitrary", "arbitrary")),
    )(x, gate, r_g, o_mla, mla_p, g_g, w_out, fw, target)


def _local_step(x, pos3, mod, loss_target, small, w_in_a, w_uq_a, w_ukv_a, w_out_b):
    B, S, D = x.shape
    tabs = _rope_tables(pos3)
    saved = []
    for l in range(DEPTH):
        last = (small["final_norm"].reshape(1, D), loss_target) if l == DEPTH - 1 else None
        x, s = _layer_fwd(x, tabs, mod[l], {n: a[l] for n, a in small.items() if n != "final_norm"},
                          w_in_a[l], w_uq_a[l], w_ukv_a[l], w_out_b[l], loss_head=last)
        saved.append(s)
    dx, loss, d_fw = x
    grads = dict(final_norm=d_fw.reshape(D))
    per_layer = [None] * DEPTH
    for l in reversed(range(DEPTH)):
        dx, per_layer[l] = _layer_bwd(dx, saved[l], tabs)
    for name in per_layer[0]:
        grads[name] = jnp.stack([per_layer[l][name] for l in range(DEPTH)])
    return loss, dx, grads


def _layer_fwd(x, tabs, mod_l, small_l, w_in_a, w_uq_a=None, w_ukv_a=None, w_out_b=None, late_weights=None,
               loss_head=None):
    B, S, D = x.shape
    cr, sr, cm, sm = tabs
    shift = mod_l[:, 0:D].reshape(B, 1, D)
    scale = mod_l[:, D:2 * D].reshape(B, 1, D)
    gate = mod_l[:, 2 * D:3 * D].reshape(B, 1, D)
    nw = small_l["norm_w"].reshape(1, D)
    qnw = small_l["mla_q_norm"].reshape(1, 256)
    kvnw = small_l["mla_kv_norm"].reshape(1, 128)
    w_g2p = jnp.pad(small_l["gla_w_g2"], ((0, 112), (0, 0)))
    b_g2 = small_l["gla_b_g2"].reshape(1, 128)
    gnw = jnp.tile(small_l["gla_norm"], 4).reshape(1, 256)
    ret_p, mla_p, gla_p, h = _proj_fwd(x, shift, scale, nw, w_in_a)
    r_g, r_raw, r_st = _ret_fwd(ret_p, cr, sr)
    if late_weights is not None:
        w_uq_a, w_ukv_a, w_out_b = late_weights(r_raw)
    q, k, v, kt, vt = _mla_prep_fwd(mla_p, cm, sm, qnw, kvnw, w_uq_a, w_ukv_a)
    o_mla, lse = _mla_attn_fwd(q, k, vt)
    g_g, g_raw, g_st = _gla_fwd(gla_p, w_g2p, b_g2, gnw)
    if loss_head is None:
        x_new, y = _out_fwd(x, gate, r_g, o_mla, mla_p, g_g, w_out_b)
    else:
        dx, y, loss, d_fw = _out_fwd_loss(x, gate, r_g, o_mla, mla_p, g_g, w_out_b, *loss_head)
        x_new = (dx, loss, d_fw)
    saved = dict(x=x, shift=shift, scale=scale, gate=gate, nw=nw, qnw=qnw, kvnw=kvnw, w_g2p=w_g2p, b_g2=b_g2,
                 gnw=gnw, ret_p=ret_p, mla_p=mla_p, gla_p=gla_p, h=h, r_g=r_g, r_raw=r_raw, r_st=r_st, q=q, k=k,
                 v=v, kt=kt, o_mla=o_mla, lse=lse, g_g=g_g, g_raw=g_raw, g_st=g_st, y=y,
                 w_in_a=w_in_a, w_uq_a=w_uq_a, w_ukv_a=w_ukv_a, w_out_b=w_out_b)
    return x_new, saved


def _layer_bwd(dx, s, tabs, early_grads=None):
    B, S, D = dx.shape
    cr, sr, cm, sm = tabs
    d_r, do, d_mz, dl, d_g, dw_out, d_gate = _out_bwd(dx, s["y"], s["gate"], s["r_g"], s["g_g"], s["w_out_b"],
                                                      s["o_mla"], s["mla_p"])
    d_ret = _ret_bwd(s["ret_p"], cr, sr, s["r_raw"], s["r_st"], d_r)
    dq, dk, dv = _mla_attn_bwd(s["q"], s["k"], s["v"], s["kt"], do, s["lse"], dl)
    d_mla, dw_uq, dw_ukv, d_qnw, d_kvnw = _mla_prep_bwd(
        s["mla_p"], cm, sm, s["qnw"], s["kvnw"], s["w_uq_a"], s["w_ukv_a"], dq, dk, dv)
    gnw = s["gnw"] if early_grads is None else s["gnw"] + early_grads(dw_out, dw_uq, dw_ukv)
    d_gla, dw_g2p, db_g2, d_gnw = _gla_bwd(s["gla_p"], s["w_g2p"], s["b_g2"], gnw, s["g_raw"], s["g_st"], d_g)
    dx, d_shift, d_scale, d_nw = _proj_bwd_x(s["x"], s["shift"], s["scale"], s["nw"], s["w_in_a"],
                                             d_ret, d_mla, d_mz, d_gla, dx)
    dw_in = _proj_bwd_w(s["h"], d_ret, d_mla, d_mz, d_gla)
    grads = dict(
        d_mod=jnp.concatenate([d_shift, d_scale, d_gate], axis=2).reshape(B, 3 * D),
        norm_w=d_nw.reshape(D), mla_q_norm=d_qnw.reshape(256), mla_kv_norm=d_kvnw.reshape(128),
        gla_w_g2=dw_g2p[0:16], gla_b_g2=db_g2.reshape(128), gla_norm256=d_gnw.reshape(256),
        w_in_a=dw_in, w_uq_a=dw_uq, w_ukv_a=dw_ukv, w_out=dw_out)
    return dx, grads


def _exchange(arrs, gather, name):
    n = len(arrs)
    out_shape = [jax.ShapeDtypeStruct(((N_DEV,) + a.shape) if g else a.shape, a.dtype)
                 for a, g in zip(arrs, gather)]

    def body(*refs):
        ins, outs = refs[:n], refs[n:2 * n]
        send_sems, recv_sems, local_sems = refs[2 * n:]
        ix, iy, ic = lax.axis_index("x"), lax.axis_index("y"), lax.axis_index("c")
        me = 4 * ix + 2 * iy + ic
        copies = []
        for a in range(n):
            mine = ins[a] if gather[a] else ins[a].at[me]
            loc = pltpu.make_async_copy(mine, outs[a].at[me], local_sems.at[a])
            loc.start()
            copies.append(loc)
            for d in range(1, N_DEV):
                px = 1 - ix if d & 4 else ix
                py = 1 - iy if d & 2 else iy
                pc = 1 - ic if d & 1 else ic
                src = ins[a] if gather[a] else ins[a].at[4 * px + 2 * py + pc]
                cp = pltpu.make_async_remote_copy(
                    src_ref=src, dst_ref=outs[a].at[me], send_sem=send_sems.at[a, d - 1],
                    recv_sem=recv_sems.at[a, d - 1], device_id=(px, py, pc), device_id_type=pl.DeviceIdType.MESH)
                cp.start()
                copies.append(cp)
        for cp in copies:
            cp.wait()

    any_spec = pl.BlockSpec(memory_space=pl.ANY)
    outs = pl.pallas_call(
        body, name=name, in_specs=[any_spec] * n, out_specs=[any_spec] * n, out_shape=out_shape,
        scratch_shapes=[pltpu.SemaphoreType.DMA((n, N_DEV - 1)), pltpu.SemaphoreType.DMA((n, N_DEV - 1)),
                        pltpu.SemaphoreType.DMA((n,))],
    )(*arrs)
    return list(outs)


def _peers(ix, iy, ic):
    out = []
    for d in range(1, N_DEV):
        px = 1 - ix if d & 4 else ix
        py = 1 - iy if d & 2 else iy
        pc = 1 - ic if d & 1 else ic
        out.append((d - 1, (px, py, pc), 4 * px + 2 * py + pc))
    return out


def _exchange_start(arrs, gather, name, after=None):
    n = len(arrs)
    lands = [lax.empty(((N_DEV,) + a.shape) if g else a.shape, a.dtype) for a, g in zip(arrs, gather)]
    extra = [] if after is None else [after]

    def body(*refs):
        ins, land_refs = refs[:n], refs[n:2 * n]
        send_sems, recv_sems = refs[2 * n + len(extra)], refs[2 * n + len(extra) + 1]
        token = refs[-1]
        ix, iy, ic = lax.axis_index("x"), lax.axis_index("y"), lax.axis_index("c")
        me = 4 * ix + 2 * iy + ic
        for a in range(n):
            for k, peer, peer_idx in _peers(ix, iy, ic):
                pltpu.make_async_remote_copy(
                    src_ref=ins[a] if gather[a] else ins[a].at[peer_idx], dst_ref=land_refs[a].at[me],
                    send_sem=send_sems.at[7 * a + k], recv_sem=recv_sems.at[7 * a + k], device_id=peer,
                    device_id_type=pl.DeviceIdType.MESH).start()
        token[...] = jnp.zeros_like(token)

    hbm = pl.BlockSpec(memory_space=pltpu.HBM)
    sem = pl.BlockSpec(memory_space=pltpu.SEMAPHORE)
    held = [pltpu.with_memory_space_constraint(a, pltpu.HBM) for a in list(arrs) + lands]
    outs = pl.pallas_call(
        body, name=name,
        out_shape=(pltpu.SemaphoreType.DMA((7 * n,)), pltpu.SemaphoreType.DMA((7 * n,)),
                   *[pltpu.HBM(a.shape, a.dtype) for a in held], jax.ShapeDtypeStruct((8, 128), F32)),
        in_specs=[hbm] * (2 * n) + [pl.BlockSpec(memory_space=pl.ANY)] * len(extra),
        out_specs=(sem, sem, *[hbm] * (2 * n), pl.BlockSpec(memory_space=pltpu.VMEM)),
        input_output_aliases={a: 2 + a for a in range(2 * n)},
        compiler_params=pltpu.CompilerParams(has_side_effects=pltpu.SideEffectType.DATAFLOW_SIDE_EFFECTING),
    )(*held, *extra)
    return dict(send=outs[0], recv=outs[1], srcs=list(outs[2:2 + n]), lands=list(outs[2 + n:2 + 2 * n]),
                token=outs[-1], gather=list(gather))


def _exchange_wait(flight, after, me, name):
    n = len(flight["srcs"])
    gather = flight["gather"]

    def body(*refs):
        srcs, land_refs = refs[:n], refs[n:2 * n]
        send_sems, recv_sems = refs[2 * n], refs[2 * n + 1]
        ix, iy, ic = lax.axis_index("x"), lax.axis_index("y"), lax.axis_index("c")
        mine = 4 * ix + 2 * iy + ic
        for a in range(n):
            for k, peer, peer_idx in _peers(ix, iy, ic):
                cp = pltpu.make_async_remote_copy(
                    src_ref=srcs[a] if gather[a] else srcs[a].at[peer_idx], dst_ref=land_refs[a].at[mine],
                    send_sem=send_sems.at[7 * a + k], recv_sem=recv_sems.at[7 * a + k], device_id=peer,
                    device_id_type=pl.DeviceIdType.MESH)
                cp.wait_send()
                cp.wait_recv()

    hbm = pl.BlockSpec(memory_space=pltpu.HBM)
    sem = pl.BlockSpec(memory_space=pltpu.SEMAPHORE)
    held = flight["srcs"] + flight["lands"]
    outs = pl.pallas_call(
        body, name=name, out_shape=tuple(pltpu.HBM(a.shape, a.dtype) for a in held),
        in_specs=[hbm] * (2 * n) + [sem, sem, pl.BlockSpec(memory_space=pl.ANY)], out_specs=tuple([hbm] * (2 * n)),
        input_output_aliases={a: a for a in range(2 * n)},
        compiler_params=pltpu.CompilerParams(has_side_effects=pltpu.SideEffectType.DATAFLOW_SIDE_EFFECTING),
    )(*held, flight["send"], flight["recv"], after)
    got = []
    for a in range(n):
        src, land = outs[a], outs[n + a]
        own = src if gather[a] else lax.dynamic_index_in_dim(src, me, axis=0, keepdims=False)
        got.append(lax.dynamic_update_index_in_dim(land, own, me, axis=0))
    return got


def _ada_fwd(c_all, ada_w, ada_b_cols):
    nb, D = c_all.shape
    cols = ada_w.shape[2]

    def body(c_ref, w_ref, b_ref, out_ref):
        ca = _silu(c_ref[...])
        for l in range(DEPTH):
            out_ref[l] = _mm(ca, w_ref[l]) + b_ref[l:l + 1, :]

    return pl.pallas_call(
        body, name="ada_fwd", out_shape=jax.ShapeDtypeStruct((DEPTH, nb, cols), F32),
        in_specs=[pl.BlockSpec(memory_space=pltpu.VMEM)] * 3, out_specs=pl.BlockSpec(memory_space=pltpu.VMEM),
        compiler_params=pltpu.CompilerParams(vmem_limit_bytes=32 * VMEM_MB),
    )(c_all, ada_w, ada_b_cols)


def _ada_bwd(c_all, d_mod_cols):
    nb, D = c_all.shape
    cols = d_mod_cols.shape[2]

    def body(c_ref, dm_ref, out_ref):
        ca = _silu(c_ref[...])
        for l in range(DEPTH):
            out_ref[l] = _mm_tn(ca, dm_ref[l])

    return pl.pallas_call(
        body, name="ada_bwd", out_shape=jax.ShapeDtypeStruct((DEPTH, D, cols), F32),
        in_specs=[pl.BlockSpec(memory_space=pltpu.VMEM)] * 2, out_specs=pl.BlockSpec(memory_space=pltpu.VMEM),
        compiler_params=pltpu.CompilerParams(vmem_limit_bytes=32 * VMEM_MB),
    )(c_all, d_mod_cols)


def _sum_adamw(parts, w, m, v, name, after=None):
    P, R, C = parts.shape
    tr = 256 if (R % 256 == 0 and R > 256) else R
    extra = [] if after is None else [after]

    def body(p_ref, w_ref, m_ref, v_ref, *rest):
        g_ref, d_ref, nm_ref, nv_ref = rest[-4:]
        g = p_ref[0].astype(F32)
        for k in range(1, P):
            g = g + p_ref[k].astype(F32)
        g_ref[...] = g
        nm = ADAM_B1 * m_ref[...] + (1.0 - ADAM_B1) * g
        nv = ADAM_B2 * v_ref[...] + (1.0 - ADAM_B2) * (g * g)
        nm_ref[...] = nm
        nv_ref[...] = nv
        m_hat = nm / (1.0 - ADAM_B1 ** ADAM_STEP)
        v_hat = nv / (1.0 - ADAM_B2 ** ADAM_STEP)
        d_ref[...] = -ADAM_LR * (m_hat / (jnp.sqrt(v_hat) + ADAM_EPS) + ADAM_WD * w_ref[...])

    blk = pl.BlockSpec((tr, C), lambda i: (i, 0))
    shp = jax.ShapeDtypeStruct((R, C), F32)
    return pl.pallas_call(
        body, name=name, grid=(R // tr,),
        in_specs=[pl.BlockSpec((P, tr, C), lambda i: (0, i, 0)), blk, blk, blk]
        + [pl.BlockSpec(memory_space=pl.ANY)] * len(extra),
        out_specs=[blk, blk, blk, blk], out_shape=[shp, shp, shp, shp],
        compiler_params=_cp(("parallel",)),
    )(parts, w, m, v, *extra)


def _sum_adamw_layer(parts, w, m, v, layer, name, prev=None, after=None):
    P, R, C = parts.shape
    tr = 256 if (R % 256 == 0 and R > 256) else R

    def body(p_ref, w_ref, m_ref, v_ref, *rest):
        g_ref, d_ref, nm_ref, nv_ref = rest[-4:]
        g = p_ref[0].astype(F32)
        for k in range(1, P):
            g = g + p_ref[k].astype(F32)
        g_ref[0] = g
        nm = ADAM_B1 * m_ref[0] + (1.0 - ADAM_B1) * g
        nv = ADAM_B2 * v_ref[0] + (1.0 - ADAM_B2) * (g * g)
        nm_ref[0] = nm
        nv_ref[0] = nv
        m_hat = nm / (1.0 - ADAM_B1 ** ADAM_STEP)
        v_hat = nv / (1.0 - ADAM_B2 ** ADAM_STEP)
        d_ref[0] = -ADAM_LR * (m_hat / (jnp.sqrt(v_hat) + ADAM_EPS) + ADAM_WD * w_ref[0])

    blk = pl.BlockSpec((1, tr, C), lambda i: (layer, i, 0))
    shp = jax.ShapeDtypeStruct(w.shape, F32)
    in_specs = [pl.BlockSpec((P, tr, C), lambda i: (0, i, 0)), blk, blk, blk]
    args = [parts, w, m, v]
    aliases = {}
    if prev is not None:
        in_specs += [pl.BlockSpec(memory_space=pl.ANY)] * 4
        args += list(prev)
        aliases = {4 + k: k for k in range(4)}
    if after is not None:
        in_specs.append(pl.BlockSpec(memory_space=pl.ANY))
        args.append(after)
    return list(pl.pallas_call(
        body, name=name, grid=(R // tr,), in_specs=in_specs, out_specs=[blk] * 4, out_shape=[shp] * 4,
        input_output_aliases=aliases, compiler_params=_cp(("parallel",)),
    )(*args))


SMALL = ["norm_w", "mla_q_norm", "mla_kv_norm", "gla_w_g2", "gla_b_g2", "gla_norm", "final_norm"]


SMALL_ROWS = 72


def _pack_small(loss, part):
    flat = [jnp.pad(loss.reshape(1), (0, 127))] + [part[n].reshape(-1) for n in SMALL]
    used = sum(f.shape[0] for f in flat)
    flat.append(jnp.zeros((SMALL_ROWS * 128 - used,), F32))
    return jnp.concatenate(flat).reshape(SMALL_ROWS, 128)


def _small_adamw(packed_parts, w, m, v, after=None):
    n = len(w)
    extra = [] if after is None else [after]

    def body(*refs):
        p_ref = refs[0]
        w_refs, m_refs, v_refs = refs[1:1 + n], refs[1 + n:1 + 2 * n], refs[1 + 2 * n:1 + 3 * n]
        outs, acc = refs[1 + 3 * n + len(extra):-1], refs[-1]
        total = p_ref[0]
        for k in range(1, N_DEV):
            total = total + p_ref[k]
        acc[...] = total
        outs[0][...] = acc[0:1, :]
        r0 = 1
        for i in range(n):
            shp = w_refs[i].shape
            if len(shp) == 3:
                g = acc[r0:r0 + shp[0] * shp[1], :].reshape(shp)
                r0 += shp[0] * shp[1]
            elif shp[1] < 128:
                g = acc[r0:r0 + shp[0], 0:shp[1]]
                r0 += shp[0]
            else:
                k = shp[1] // 128
                g = jnp.concatenate(
                    [jnp.concatenate([acc[r0 + l * k + j:r0 + l * k + j + 1, :] for j in range(k)], axis=1)
                     for l in range(shp[0])], axis=0)
                r0 += shp[0] * k
            nm = ADAM_B1 * m_refs[i][...] + (1.0 - ADAM_B1) * g
            nv = ADAM_B2 * v_refs[i][...] + (1.0 - ADAM_B2) * (g * g)
            m_hat = nm / (1.0 - ADAM_B1 ** ADAM_STEP)
            v_hat = nv / (1.0 - ADAM_B2 ** ADAM_STEP)
            outs[1 + 4 * i][...] = g
            outs[2 + 4 * i][...] = -ADAM_LR * (m_hat / (jnp.sqrt(v_hat) + ADAM_EPS) + ADAM_WD * w_refs[i][...])
            outs[3 + 4 * i][...] = nm
            outs[4 + 4 * i][...] = nv

    vmem = pl.BlockSpec(memory_space=pltpu.VMEM)
    out_shape = [jax.ShapeDtypeStruct((1, 128), F32)]
    for a in w:
        out_shape += [jax.ShapeDtypeStruct(a.shape, F32)] * 4
    outs = pl.pallas_call(
        body, name="adamw_small", in_specs=[vmem] * (1 + 3 * n) + [pl.BlockSpec(memory_space=pl.ANY)] * len(extra),
        out_specs=[vmem] * (1 + 4 * n), out_shape=out_shape, scratch_shapes=[pltpu.VMEM((SMALL_ROWS, 128), F32)],
    )(packed_parts, *w, *m, *v, *extra)
    return outs[0], [outs[1 + 4 * i:5 + 4 * i] for i in range(n)]


WEIGHTS = ["norm_w", "ada_w", "ada_b", "w_in", "mla_q_norm", "w_uq", "mla_kv_norm", "w_ukv", "gla_w_g2",
           "gla_b_g2", "gla_norm", "w_out", "final_norm"]


def kernel(x, c, positions, norm_w, ada_w, ada_b, w_in, mla_q_norm, w_uq, mla_kv_norm, w_ukv, gla_w_g2, gla_b_g2, gla_norm, w_out, final_norm, loss_target, m_norm_w, m_ada_w, m_ada_b, m_w_in, m_mla_q_norm, m_w_uq, m_mla_kv_norm, m_w_ukv, m_gla_w_g2, m_gla_b_g2, m_gla_norm, m_w_out, m_final_norm, v_norm_w, v_ada_w, v_ada_b, v_w_in, v_mla_q_norm, v_w_uq, v_mla_kv_norm, v_w_ukv, v_gla_w_g2, v_gla_b_g2, v_gla_norm, v_w_out, v_final_norm):
    w = dict(norm_w=norm_w, ada_w=ada_w, ada_b=ada_b, w_in=w_in, mla_q_norm=mla_q_norm, w_uq=w_uq,
             mla_kv_norm=mla_kv_norm, w_ukv=w_ukv, gla_w_g2=gla_w_g2, gla_b_g2=gla_b_g2, gla_norm=gla_norm,
             w_out=w_out, final_norm=final_norm)
    m = dict(norm_w=m_norm_w, ada_w=m_ada_w, ada_b=m_ada_b, w_in=m_w_in, mla_q_norm=m_mla_q_norm, w_uq=m_w_uq,
             mla_kv_norm=m_mla_kv_norm, w_ukv=m_w_ukv, gla_w_g2=m_gla_w_g2, gla_b_g2=m_gla_b_g2,
             gla_norm=m_gla_norm, w_out=m_w_out, final_norm=m_final_norm)
    v = dict(norm_w=v_norm_w, ada_w=v_ada_w, ada_b=v_ada_b, w_in=v_w_in, mla_q_norm=v_mla_q_norm, w_uq=v_w_uq,
             mla_kv_norm=v_mla_kv_norm, w_ukv=v_w_ukv, gla_w_g2=v_gla_w_g2, gla_b_g2=v_gla_b_g2,
             gla_norm=v_gla_norm, w_out=v_w_out, final_norm=v_final_norm)
    B, S, D = x.shape
    me = 4 * lax.axis_index("x") + 2 * lax.axis_index("y") + lax.axis_index("c")
    ada_cols = ada_w.shape[2]
    cast = lambda a: a.astype(_MXU)

    sharded = ["w_in", "w_uq", "w_ukv", "w_out"]

    whole_cols = lambda a: jnp.transpose(a, (1, 0, 2)).reshape(a.shape[1], -1)
    whole_in = lambda blk: _arrange_w_in(whole_cols(blk))
    whole_rest = lambda blks: (_arrange_w_uq(whole_cols(blks[0])), _arrange_w_ukv(whole_cols(blks[1])),
                               blks[2].reshape(D, D))
    col_blocks = lambda a: jnp.transpose(a.reshape(a.shape[0], N_DEV, -1), (1, 0, 2)).astype(jnp.bfloat16)
    blocks_in = lambda dw_in_a: col_blocks(_unarrange_w_in(dw_in_a))
    blocks_rest = lambda dw_out, dw_uq_a, dw_ukv_a: [
        col_blocks(_unarrange_w_uq(dw_uq_a)), col_blocks(_unarrange_w_ukv(dw_ukv_a)),
        dw_out.reshape(N_DEV, D // N_DEV, D).astype(jnp.bfloat16)]

    (c_g,) = _exchange([c], [True], "gather_c")
    c_all = c_g.reshape(N_DEV * B, D)

    ada_b_cols = lax.dynamic_slice(ada_b, (0, me * ada_cols), (DEPTH, ada_cols))
    mod_cols = _ada_fwd(c_all, ada_w, ada_b_cols)
    mod_send = jnp.transpose(mod_cols.reshape(DEPTH, N_DEV, B, ada_cols), (1, 0, 2, 3))
    (mod_recv,) = _exchange([mod_send], [False], "scatter_mod")
    mod = jnp.transpose(mod_recv, (1, 2, 0, 3)).reshape(DEPTH, B, 3 * D)

    flight_i = _exchange_start([cast(w_in[0])], [True], "gather_start_first", after=mod)
    flight_r = _exchange_start([cast(w[n][0]) for n in sharded[1:]], [True] * 3, "gather_start_layer0",
                               after=flight_i["token"])
    flight_w = _exchange_start([cast(w[n][1]) for n in sharded], [True] * 4, "gather_start_layer1",
                               after=flight_r["token"])
    small_w = {n: w[n] for n in SMALL}
    layer_small = lambda l: {n: a[l] for n, a in small_w.items() if n != "final_norm"}
    tabs = _rope_tables(positions.reshape(B, S, 1), flight_w["token"][0, 0])
    late0 = lambda after: whole_rest(_exchange_wait(flight_r, after, me, "gather_wait_layer0"))
    (w_in0_g,) = _exchange_wait(flight_i, tabs[0], me, "gather_wait_first")
    x1, saved0 = _layer_fwd(x, tabs, mod[0], layer_small(0), whole_in(w_in0_g), late_weights=late0)
    got1 = _exchange_wait(flight_w, x1, me, "gather_wait_layer1")
    (dx, loss, d_fw), saved1 = _layer_fwd(x1, tabs, mod[1], layer_small(1), whole_in(got1[0]), *whole_rest(got1[1:]),
                                          loss_head=(final_norm.reshape(1, D), loss_target))

    dx, g1 = _layer_bwd(dx, saved1, tabs)
    flight_g = _exchange_start([blocks_in(g1["w_in_a"])] + blocks_rest(g1["w_out"], g1["w_uq_a"], g1["w_ukv_a"]),
                               [False] * 4, "grads_start_layer1")
    flights = {}

    def early0(dw_out, dw_uq_a, dw_ukv_a):
        flights["rest0"] = _exchange_start(blocks_rest(dw_out, dw_uq_a, dw_ukv_a), [False] * 3, "grads_start_layer0")
        return flights["rest0"]["token"][0, 0]

    saved0 = dict(saved0, gate=saved0["gate"] + flight_g["token"][0, 0])
    grad_x, g0 = _layer_bwd(dx, saved0, tabs, early_grads=early0)
    parts1 = _exchange_wait(flight_g, grad_x, me, "grads_wait_layer1")
    rest0 = _exchange_wait(flights["rest0"], g0["w_in_a"], me, "grads_wait_layer0")

    both = lambda n: jnp.stack([g0[n], g1[n]])
    d_mod = both("d_mod")
    part = dict(norm_w=both("norm_w"), mla_q_norm=both("mla_q_norm"), mla_kv_norm=both("mla_kv_norm"),
                gla_w_g2=both("gla_w_g2"), gla_b_g2=both("gla_b_g2"), gla_norm=both("gla_norm256")[:, 0:128],
                final_norm=d_fw)
    d_mod_g, small_g = _exchange([d_mod, _pack_small(loss, part)], [True, True], "gather_small")
    flight_l = _exchange_start([blocks_in(g0["w_in_a"])], [False], "exchange_start_last", after=small_g)
    res = {}
    behind = flight_l["token"]
    for a, name in enumerate(sharded):
        res[name] = _sum_adamw_layer(parts1[a], w[name], m[name], v[name], 1, "adamw_%s_layer1" % name, after=behind)
        behind = res[name][1]
    for a, name in enumerate(sharded[1:]):
        res[name] = _sum_adamw_layer(rest0[a], w[name], m[name], v[name], 0, "adamw_%s_layer0" % name,
                                     prev=res[name], after=behind)
        behind = res[name][1]

    d_mod_all = jnp.transpose(d_mod_g, (1, 0, 2, 3)).reshape(DEPTH, N_DEV * B, 3 * D)
    d_mod_cols = lax.dynamic_slice(d_mod_all, (0, 0, me * ada_cols), (DEPTH, N_DEV * B, ada_cols))
    g_ada_w = _ada_bwd(c_all, d_mod_cols)

    def update(name, parts2d, after):
        shp = w[name].shape
        two = lambda a: a.reshape(parts2d.shape[1:])
        out = _sum_adamw(parts2d, two(w[name]), two(m[name]), two(v[name]), "adamw_" + name, after=after)
        res[name] = [o.reshape(shp) for o in out]
        return out[1]

    behind = update("ada_w", g_ada_w.reshape(1, DEPTH * D, ada_cols), behind)
    behind = update("ada_b", jnp.transpose(d_mod_g, (0, 2, 1, 3)).reshape(N_DEV * B, DEPTH * 3 * D // 128, 128), behind)
    row = lambda a: a.reshape(1, D) if a.ndim == 1 else a
    loss_sum, small_out = _small_adamw(small_g, [row(w[n]) for n in SMALL], [row(m[n]) for n in SMALL],
                                       [row(v[n]) for n in SMALL], after=behind)
    for n, outs in zip(SMALL, small_out):
        res[n] = [o.reshape(w[n].shape) for o in outs]
    loss_out = loss_sum[0, 0]
    (in0,) = _exchange_wait(flight_l, loss_sum, me, "exchange_wait_last")
    res["w_in"] = _sum_adamw_layer(in0, w_in, m_w_in, v_w_in, 0, "adamw_w_in_layer0", prev=res["w_in"])
    return (loss_out, grad_x, *[res[n][0] for n in WEIGHTS], *[res[n][1] for n in WEIGHTS],
            *[res[n][2] for n in WEIGHTS], *[res[n][3] for n in WEIGHTS])
```

```python
import functools
import math

import numpy as np
import jax
import jax.numpy as jnp
from jax import lax
from jax.experimental import pallas as pl
from jax.experimental.pallas import tpu as pltpu

F32 = jnp.float32
_MXU = jnp.bfloat16

D_MODEL = 1024
DEPTH = 2
CHUNK = 64
EPS = 1e-6
ROPE_THETA = 10000.0
N_DEV = 8

MLA_SCALE = 96.0 ** -0.5
RET_KSCALE = 64.0 ** -0.5
GLA_KSCALE = 32.0 ** -0.5
GLA_TAU = 16.0

ADAM_LR = 0.001
ADAM_B1 = 0.9
ADAM_B2 = 0.999
ADAM_EPS = 1e-08
ADAM_WD = 0.01
ADAM_STEP = 10

RET_W, MLA_W, GLA_W = 1024, 1024, 896
ARR_W = RET_W + MLA_W + GLA_W
VMEM_MB = 1024 * 1024


def _cp(sem, vmem_mb=48):
    return pltpu.CompilerParams(dimension_semantics=sem, vmem_limit_bytes=vmem_mb * VMEM_MB)


def _mm(a, b):
    return jnp.dot(a.astype(_MXU), b.astype(_MXU), preferred_element_type=F32)


def _mm_nt(a, b):
    return lax.dot_general(a.astype(_MXU), b.astype(_MXU), (((1,), (1,)), ((), ())),
                           preferred_element_type=F32)


def _mm_tn(a, b):
    return lax.dot_general(a.astype(_MXU), b.astype(_MXU), (((0,), (0,)), ((), ())),
                           preferred_element_type=F32)


def _mm_f32(a, b):
    return jnp.dot(a, b, precision=lax.Precision.HIGHEST, preferred_element_type=F32)


def _sig(z):
    return 1.0 / (1.0 + jnp.exp(-z))


def _silu(z):
    return z * _sig(z)


def _dsilu(z):
    s = _sig(z)
    return s * (1.0 + z * (1.0 - s))


def _full(shape):
    nd = len(shape)
    return pl.BlockSpec(shape, lambda *_: (0,) * nd)


def _qk_perm(blk):
    r = blk.shape[0]
    return jnp.transpose(blk.reshape(r, 4, 2, 32), (0, 2, 1, 3)).reshape(r, 256)


def _qk_unperm(blk):
    r = blk.shape[0]
    return jnp.transpose(blk.reshape(r, 2, 4, 32), (0, 2, 1, 3)).reshape(r, 256)


def _arrange_w_in(w):
    z = lambda n: jnp.zeros((w.shape[0], n), w.dtype)
    ret = [_qk_perm(w[:, 0:256]), _qk_perm(w[:, 256:512]), w[:, 512:768], w[:, 768:1024]]
    mla = [w[:, 1024:1280], w[:, 1280:1408], z(64), w[:, 1408:1440], z(32), w[:, 1440:1952]]
    gla = [w[:, 1952:2080], w[:, 2080:2208], w[:, 2208:2464], w[:, 2464:2480], z(112), w[:, 2480:2736]]
    return jnp.concatenate(ret + mla + gla, axis=1)


def _unarrange_w_in(a):
    m, g = RET_W, RET_W + MLA_W
    parts = [_qk_unperm(a[:, 0:256]), _qk_unperm(a[:, 256:512]), a[:, 512:1024],
             a[:, m:m + 384], a[:, m + 448:m + 480], a[:, m + 512:m + 1024],
             a[:, g:g + 528], a[:, g + 640:g + 896]]
    return jnp.concatenate(parts, axis=1)


def _arrange_w_uq(w):
    return jnp.pad(w.reshape(256, 8, 96), ((0, 0), (0, 0), (0, 32))).reshape(256, 1024)


def _unarrange_w_uq(a):
    return a.reshape(256, 8, 128)[:, :, :96].reshape(256, 768)


def _arrange_w_ukv(w):
    r = w.reshape(128, 8, 128)
    k = jnp.pad(r[:, :, :64], ((0, 0), (0, 0), (0, 64))).reshape(128, 1024)
    return jnp.concatenate([k, r[:, :, 64:].reshape(128, 512)], axis=1)


def _unarrange_w_ukv(a):
    k = a[:, :1024].reshape(128, 8, 128)[:, :, :64]
    v = a[:, 1024:].reshape(128, 8, 64)
    return jnp.concatenate([k, v], axis=2).reshape(128, 1024)


def _rope_tables(pos3, zero=0.0):
    B, S, _ = pos3.shape
    ts = min(S, 512)
    inv32 = (np.float32(ROPE_THETA) ** (-(np.arange(32, dtype=np.float32) / 32))).astype(np.float32)
    inv16 = (np.float32(ROPE_THETA) ** (-(np.arange(16, dtype=np.float32) / 16))).astype(np.float32)
    inv = np.zeros((1, 128), np.float32)
    inv[0, 0:32] = inv32
    inv[0, 32:48] = inv16

    def body(pos_ref, inv_ref, cr, sr, cm, sm):
        ang = pos_ref[0].astype(F32) * inv_ref[...]
        lane = lax.broadcasted_iota(jnp.int32, (1, 128), 1)

        def every_head(x):
            y = jnp.where(lane < 32, x, pltpu.roll(x, 32, 1))
            return jnp.where(lane < 64, y, pltpu.roll(y, 64, 1))

        def rotary_pair(x, fill):
            return jnp.where((lane >= 64) & (lane < 80), pltpu.roll(x, 32, 1),
                             jnp.where((lane >= 80) & (lane < 96), pltpu.roll(x, 48, 1), fill))

        c, s = jnp.cos(ang), jnp.sin(ang)
        cr[0] = every_head(c)
        sr[0] = every_head(s)
        cm[0] = rotary_pair(c, 1.0)
        sm[0] = rotary_pair(s, 0.0)

    tab = jax.ShapeDtypeStruct((B, S, 128), F32)
    blk = pl.BlockSpec((1, ts, 128), lambda b, i: (b, i, 0))
    return pl.pallas_call(
        body, name="rope_tables", grid=(B, S // ts),
        in_specs=[pl.BlockSpec((1, ts, 1), lambda b, i: (b, i, 0)), _full((1, 128))],
        out_specs=[blk, blk, blk, blk], out_shape=[tab, tab, tab, tab],
        compiler_params=_cp(("parallel", "parallel")),
    )(pos3, jnp.asarray(inv) + zero)


def _rope128(x, cos, sin):
    lane = lax.broadcasted_iota(jnp.int32, (1, 128), 1)
    rp = pltpu.roll(x, 16, 1)
    rm = pltpu.roll(x, 112, 1)
    return x * cos + jnp.where(lane < 80, -rm, rp) * sin


def _rope128_t(d, cos, sin):
    lane = lax.broadcasted_iota(jnp.int32, (1, 128), 1)
    y = d * sin
    yp = pltpu.roll(y, 16, 1)
    ym = pltpu.roll(y, 112, 1)
    return d * cos + jnp.where(lane < 64, 0.0, jnp.where(lane < 80, ym, jnp.where(lane < 96, -yp, 0.0)))


def _proj_fwd(x, shift, scale, nw, w_arr):
    B, S, D = x.shape
    tm = min(S, 512)

    def body(x_ref, sh_ref, sc_ref, nw_ref, w_ref, ret_ref, mla_ref, gla_ref, h_ref):
        xv = x_ref[0]
        rstd = lax.rsqrt(jnp.mean(xv * xv, axis=-1, keepdims=True) + EPS)
        h = (xv * rstd * nw_ref[...]) * (1.0 + sc_ref[0]) + sh_ref[0]
        hb = h.astype(_MXU)
        h_ref[0] = hb
        ret_ref[0] = jnp.dot(hb, w_ref[:, 0:RET_W], preferred_element_type=F32).astype(_MXU)
        mla_ref[0] = jnp.dot(hb, w_ref[:, RET_W:RET_W + MLA_W], preferred_element_type=F32).astype(_MXU)
        gla_ref[0] = jnp.dot(hb, w_ref[:, RET_W + MLA_W:ARR_W], preferred_element_type=F32).astype(_MXU)

    tok = lambda w: pl.BlockSpec((1, tm, w), lambda b, i: (b, i, 0))
    per_seq = pl.BlockSpec((1, 1, D), lambda b, i: (b, 0, 0))
    return pl.pallas_call(
        body, name="proj_fwd", grid=(B, S // tm),
        in_specs=[tok(D), per_seq, per_seq, _full((1, D)), _full((D, ARR_W))],
        out_specs=[tok(RET_W), tok(MLA_W), tok(GLA_W), tok(D)],
        out_shape=[jax.ShapeDtypeStruct((B, S, RET_W), _MXU), jax.ShapeDtypeStruct((B, S, MLA_W), _MXU),
                   jax.ShapeDtypeStruct((B, S, GLA_W), _MXU), jax.ShapeDtypeStruct((B, S, D), _MXU)],
        compiler_params=_cp(("parallel", "parallel")),
    )(x, shift, scale, nw, w_arr)


RET_L = 256


def _ret_consts(L):
    lg = np.log1p(-np.exp2(-5.0 - np.arange(4, dtype=np.float32))).astype(np.float32)
    i = np.arange(L)
    ci = i // CHUNK
    diff = (i[:, None] - i[None, :]).astype(np.float32)
    same = ci[:, None] == ci[None, :]
    past = ci[None, :] < ci[:, None]
    expo = np.where(same, np.abs(diff), np.where(past, diff, 0.0)).astype(np.float32)
    dec = np.where((same | past)[None], np.exp(lg[:, None, None] * expo[None]), 0.0).astype(np.float32)
    head = (np.arange(256) % 128) // 32
    qw = np.exp((i + 1.0)[:, None] * lg[head][None, :]).astype(np.float32)
    kw = np.exp((L - 1.0 - i)[:, None] * lg[head][None, :]).astype(np.float32)
    a_row = np.exp(np.float32(L) * lg[head])[None, :].astype(np.float32)
    return [jnp.asarray(t) for t in (dec.reshape(4 * L, L), qw, kw, a_row)]


def _ret_masks():
    lane = lax.broadcasted_iota(jnp.int32, (1, 256), 1)
    mh = [((lane % 128) // 32) == h for h in range(4)]
    mv = [(lane // 64) == h for h in range(4)]
    vi = lax.broadcasted_iota(jnp.int32, (256, 256), 0)
    ki = lax.broadcasted_iota(jnp.int32, (256, 256), 1)
    bd = (vi // 64) == ((ki % 128) // 32)
    return mh, mv, bd


def _ret_rope(p, cs, sn):
    q1, q2, k1, k2 = p[:, 0:128], p[:, 128:256], p[:, 256:384], p[:, 384:512]
    qr = jnp.concatenate([q1 * cs - q2 * sn, q2 * cs + q1 * sn], axis=1)
    kr = jnp.concatenate([k1 * cs - k2 * sn, k2 * cs + k1 * sn], axis=1) * RET_KSCALE
    return qr, kr


def _head_mean(x, mv, width):
    out = jnp.zeros_like(x)
    for m in mv:
        s = jnp.sum(jnp.where(m, x, 0.0), axis=-1, keepdims=True) * (1.0 / width)
        out = jnp.where(m, s, out)
    return out


def _stack_heads(x, masks):
    return jnp.concatenate([jnp.where(m, x, 0.0) for m in masks], axis=0)


def _fold_heads(xs, masks, L):
    out = jnp.where(masks[0], xs[0:L], 0.0)
    for h in range(1, 4):
        out = out + jnp.where(masks[h], xs[h * L:(h + 1) * L], 0.0)
    return out


RET_G = 2


def _ret_fwd(ret_p, cos, sin):
    B, S, _ = ret_p.shape
    L = min(RET_L, S)
    NB = S // L
    G = min(RET_G, NB)
    NG = NB // G
    consts = _ret_consts(L)

    def body(p_ref, c_ref, s_ref, ds_ref, qw_ref, kw_ref, a_ref, out_ref, raw_ref, st_ref, st_sc):
        @pl.when(pl.program_id(1) == 0)
        def _():
            st_sc[...] = jnp.zeros_like(st_sc)

        mh, mv, bd = _ret_masks()
        cs_ = range(G)
        rows = [slice(c * L, (c + 1) * L) for c in cs_]
        ps = [p_ref[0, rows[c], :].astype(F32) for c in cs_]
        qk = [_ret_rope(ps[c], c_ref[0, rows[c], :], s_ref[0, rows[c], :]) for c in cs_]
        vs = [ps[c][:, 512:768] for c in cs_]
        a_s = [_mm_nt(_stack_heads(qk[c][0], mh), qk[c][1]) for c in cs_]
        upd = [_mm_tn(vs[c], qk[c][1] * kw_ref[...]) for c in cs_]
        o_s = [_mm(a_s[c] * ds_ref[...], vs[c]) for c in cs_]
        st = st_sc[...]
        inter = []
        for c in cs_:
            st_ref[0, c] = st
            inter.append(_mm_nt(qk[c][0] * qw_ref[...], st))
            st = st * a_ref[...] + jnp.where(bd, upd[c], 0.0)
        st_sc[...] = st
        for c in cs_:
            r = _fold_heads(o_s[c], mv, L) + inter[c]
            raw_ref[0, rows[c], :] = r
            rstd = lax.rsqrt(_head_mean(r * r, mv, 64.0) + EPS)
            out_ref[0, rows[c], :] = (r * rstd * _silu(ps[c][:, 768:1024])).astype(_MXU)

    tok = lambda w: pl.BlockSpec((1, G * L, w), lambda b, n: (b, n, 0))
    return pl.pallas_call(
        body, name="ret_fwd", grid=(B, NG),
        in_specs=[tok(RET_W), tok(128), tok(128), _full((4 * L, L)), _full((L, 256)), _full((L, 256)),
                  _full((1, 256))],
        out_specs=[tok(256), tok(256), pl.BlockSpec((1, G, 256, 256), lambda b, n: (b, n, 0, 0))],
        out_shape=[jax.ShapeDtypeStruct((B, S, 256), _MXU), jax.ShapeDtypeStruct((B, S, 256), F32),
                   jax.ShapeDtypeStruct((B, NB, 256, 256), F32)],
        scratch_shapes=[pltpu.VMEM((256, 256), F32)],
        compiler_params=_cp(("parallel", "arbitrary")),
    )(ret_p, cos, sin, *consts)


def _ret_bwd(ret_p, cos, sin, raw, states, d_mix):
    B, S, _ = ret_p.shape
    L = min(RET_L, S)
    NB = S // L
    G = 1
    NG = NB // G
    consts = _ret_consts(L)

    def body(p_ref, c_ref, s_ref, raw_ref, st_ref, dm_ref, ds_ref, qw_ref, kw_ref, a_ref, dp_ref, dst_sc):
        @pl.when(pl.program_id(1) == 0)
        def _():
            dst_sc[...] = jnp.zeros_like(dst_sc)

        mh, mv, bd = _ret_masks()
        qw, kw, dec = qw_ref[...], kw_ref[...], ds_ref[...]
        cs_ = range(G)
        rows = [slice(c * L, (c + 1) * L) for c in cs_]
        ps = [p_ref[0, rows[c], :].astype(F32) for c in cs_]
        tabs = [(c_ref[0, rows[c], :], s_ref[0, rows[c], :]) for c in cs_]
        qk = [_ret_rope(ps[c], *tabs[c]) for c in cs_]
        vs = [ps[c][:, 512:768] for c in cs_]
        qs = [_stack_heads(qk[c][0], mh) for c in cs_]
        a_s = [_mm_nt(qs[c], qk[c][1]) for c in cs_]
        dr, dz = [], []
        for c in cs_:
            r = raw_ref[0, rows[c], :]
            z = ps[c][:, 768:1024]
            rstd = lax.rsqrt(_head_mean(r * r, mv, 64.0) + EPS)
            rn = r * rstd
            dm = dm_ref[0, rows[c], :]
            d_rn = dm * _silu(z)
            dz.append(dm * rn * _dsilu(z))
            dr.append(rstd * (d_rn - rn * _head_mean(d_rn * rn, mv, 64.0)))
        do_s = [_stack_heads(dr[c], mv) for c in cs_]
        da_s = [_mm_nt(do_s[c], vs[c]) for c in cs_]
        sts = [st_ref[0, c] for c in cs_]
        dq_st = [_mm(dr[c], sts[c]) for c in cs_]
        dst_in = [_mm_tn(dr[c], qk[c][0] * qw) for c in cs_]
        dv = [_mm_tn(a_s[c] * dec, do_s[c]) for c in cs_]
        dqr, dkr = [], []
        for c in cs_:
            da = da_s[c] * dec
            dqr.append(_fold_heads(_mm(da, qk[c][1]), mh, L) + dq_st[c] * qw)
            dkr.append(_mm_tn(da, qs[c]))
        dst_next = dst_sc[...]
        for c in reversed(cs_):
            g = jnp.where(bd, dst_next, 0.0)
            dv[c] = dv[c] + _mm_nt(qk[c][1] * kw, g)
            dkr[c] = dkr[c] + _mm(vs[c], g) * kw
            dst_next = dst_next * a_ref[...] + jnp.where(bd, dst_in[c], 0.0)
        dst_sc[...] = dst_next
        for c in cs_:
            cs, sn = tabs[c]
            dk = dkr[c] * RET_KSCALE
            dq1, dq2 = dqr[c][:, 0:128], dqr[c][:, 128:256]
            dk1, dk2 = dk[:, 0:128], dk[:, 128:256]
            dp_ref[0, rows[c], :] = jnp.concatenate(
                [dq1 * cs + dq2 * sn, dq2 * cs - dq1 * sn, dk1 * cs + dk2 * sn, dk2 * cs - dk1 * sn, dv[c], dz[c]],
                axis=1).astype(_MXU)

    tok = lambda w: pl.BlockSpec((1, G * L, w), lambda b, i: (b, NG - 1 - i, 0))
    return pl.pallas_call(
        body, name="ret_bwd", grid=(B, NG),
        in_specs=[tok(RET_W), tok(128), tok(128), tok(256),
                  pl.BlockSpec((1, G, 256, 256), lambda b, i: (b, NG - 1 - i, 0, 0)), tok(256),
                  _full((4 * L, L)), _full((L, 256)), _full((L, 256)), _full((1, 256))],
        out_specs=tok(RET_W), out_shape=jax.ShapeDtypeStruct((B, S, RET_W), _MXU),
        scratch_shapes=[pltpu.VMEM((256, 256), F32)],
        compiler_params=_cp(("parallel", "arbitrary")),
    )(ret_p, cos, sin, raw, states, d_mix, *consts)


def _gla_masks():
    C = CHUNK
    lk = lax.broadcasted_iota(jnp.int32, (1, 128), 1)
    lv = lax.broadcasted_iota(jnp.int32, (1, 256), 1)
    mk = [(lk // 32) == h for h in range(4)]
    mv = [(lv // 64) == h for h in range(4)]
    vi = lax.broadcasted_iota(jnp.int32, (256, 128), 0)
    ki = lax.broadcasted_iota(jnp.int32, (256, 128), 1)
    bd = (vi // 64) == (ki // 32)
    ri = lax.broadcasted_iota(jnp.int32, (4 * C, C), 0) % C
    cj = lax.broadcasted_iota(jnp.int32, (4 * C, C), 1)
    lower = ri >= cj
    ti = lax.broadcasted_iota(jnp.int32, (C, C), 0)
    tj = lax.broadcasted_iota(jnp.int32, (C, C), 1)
    ltri = jnp.where(ti >= tj, 1.0, 0.0).astype(F32)
    utri = jnp.where(ti <= tj, 1.0, 0.0).astype(F32)
    return mk, mv, bd, lower, ltri, utri


def _log_sigmoid(x):
    return jnp.minimum(x, 0.0) - jnp.log(1.0 + jnp.exp(-jnp.abs(x)))


GLA_G = 8


def _gla_fwd(gla_p, w_g2p, b_g2, gnw):
    B, S, _ = gla_p.shape
    C = CHUNK
    NC = S // C
    G = min(GLA_G, NC)
    NG = NC // G

    def body(p_ref, w_ref, b_ref, gn_ref, out_ref, raw_ref, st_ref, st_sc):
        @pl.when(pl.program_id(1) == 0)
        def _():
            st_sc[...] = jnp.zeros_like(st_sc)

        mk, mv, bd, lower, ltri, _ = _gla_masks()
        cs = range(G)
        rows = [slice(c * C, (c + 1) * C) for c in cs]
        ps = [p_ref[0, rows[c], :].astype(F32) for c in cs]
        pre = [_mm(ps[c][:, 512:640], w_ref[...]) + b_ref[...] for c in cs]
        cum = [_mm_f32(ltri, _log_sigmoid(pre[c]) * (1.0 / GLA_TAU)) for c in cs]
        past, fut, upd, q_pos, a_row = [], [], [], [], []
        for c in cs:
            q = ps[c][:, 0:128]
            k = ps[c][:, 128:256] * GLA_KSCALE
            last = cum[c][C - 1:C, :]
            e_pos = jnp.exp(cum[c])
            e_neg = jnp.exp(-cum[c])
            q_pos.append(q * e_pos)
            a_row.append(jnp.exp(last))
            past.append(_mm_nt(_stack_heads(q_pos[c], mk), k * e_neg))
            fut.append(_mm_nt(_stack_heads(q * e_neg, mk), k * e_pos))
            upd.append(_mm_tn(ps[c][:, 256:512], k * jnp.exp(last - cum[c])))
        o_s = [_mm(jnp.where(lower, past[c], fut[c]), ps[c][:, 256:512]) for c in cs]
        st = st_sc[...]
        inter = []
        for c in cs:
            st_ref[0, c] = st
            inter.append(_mm_nt(q_pos[c], st))
            st = st * a_row[c] + jnp.where(bd, upd[c], 0.0)
        st_sc[...] = st
        for c in cs:
            g = _fold_heads(o_s[c], mv, C) + inter[c]
            raw_ref[0, rows[c], :] = g
            rstd = lax.rsqrt(_head_mean(g * g, mv, 64.0) + EPS)
            out_ref[0, rows[c], :] = (g * rstd * gn_ref[...] * _silu(ps[c][:, 640:896])).astype(_MXU)

    tok = lambda w: pl.BlockSpec((1, G * C, w), lambda b, n: (b, n, 0))
    return pl.pallas_call(
        body, name="gla_fwd", grid=(B, NG),
        in_specs=[tok(GLA_W), _full((128, 128)), _full((1, 128)), _full((1, 256))],
        out_specs=[tok(256), tok(256), pl.BlockSpec((1, G, 256, 128), lambda b, n: (b, n, 0, 0))],
        out_shape=[jax.ShapeDtypeStruct((B, S, 256), _MXU), jax.ShapeDtypeStruct((B, S, 256), F32),
                   jax.ShapeDtypeStruct((B, NC, 256, 128), F32)],
        scratch_shapes=[pltpu.VMEM((256, 128), F32)],
        compiler_params=_cp(("parallel", "arbitrary")),
    )(gla_p, w_g2p, b_g2, gnw)


def _gla_bwd(gla_p, w_g2p, b_g2, gnw, raw, states, d_mix):
    B, S, _ = gla_p.shape
    C = CHUNK
    NC = S // C
    G = min(GLA_G, NC)
    NG = NC // G

    def body(p_ref, w_ref, b_ref, gn_ref, raw_ref, st_ref, dm_ref, dp_ref, dw_ref, db_ref, dgn_ref, dst_sc):
        first = (pl.program_id(0) == 0) & (pl.program_id(1) == 0)

        @pl.when(first)
        def _():
            dw_ref[...] = jnp.zeros_like(dw_ref)
            db_ref[...] = jnp.zeros_like(db_ref)
            dgn_ref[...] = jnp.zeros_like(dgn_ref)

        @pl.when(pl.program_id(1) == 0)
        def _():
            dst_sc[...] = jnp.zeros_like(dst_sc)

        mk, mv, bd, lower, ltri, utri = _gla_masks()
        gn = gn_ref[...]
        cs = range(G)
        rows = [slice(c * C, (c + 1) * C) for c in cs]
        ps = [p_ref[0, rows[c], :].astype(F32) for c in cs]
        vs = [ps[c][:, 256:512] for c in cs]
        pre = [_mm(ps[c][:, 512:640], w_ref[...]) + b_ref[...] for c in cs]
        cum = [_mm_f32(ltri, _log_sigmoid(pre[c]) * (1.0 / GLA_TAU)) for c in cs]
        dg, dz, dgn_acc = [], [], jnp.zeros((1, 256), F32)
        for c in cs:
            g = raw_ref[0, rows[c], :]
            z = ps[c][:, 640:896]
            rstd = lax.rsqrt(_head_mean(g * g, mv, 64.0) + EPS)
            gh = g * rstd
            dm = dm_ref[0, rows[c], :]
            d_gn = dm * _silu(z)
            dz.append(dm * gh * gn * _dsilu(z))
            dgn_acc = dgn_acc + jnp.sum(d_gn * gh, axis=0, keepdims=True)
            d_gh = d_gn * gn
            dg.append(rstd * (d_gh - gh * _head_mean(d_gh * gh, mv, 64.0)))
        do_s = [_stack_heads(dg[c], mv) for c in cs]
        dattn = [_mm_nt(do_s[c], vs[c]) for c in cs]
        ks, e_pos, e_neg, q_pos, q_neg, k_pos, k_neg, qp_s, qn_s, past, fut, a_row, w_dec, kd = ([] for _ in range(14))
        for c in cs:
            q = ps[c][:, 0:128]
            k = ps[c][:, 128:256] * GLA_KSCALE
            last = cum[c][C - 1:C, :]
            ep, en = jnp.exp(cum[c]), jnp.exp(-cum[c])
            ks.append(k), e_pos.append(ep), e_neg.append(en)
            q_pos.append(q * ep), q_neg.append(q * en), k_pos.append(k * ep), k_neg.append(k * en)
            qp_s.append(_stack_heads(q_pos[c], mk)), qn_s.append(_stack_heads(q_neg[c], mk))
            past.append(_mm_nt(qp_s[c], k_neg[c]))
            fut.append(_mm_nt(qn_s[c], k_pos[c]))
            a_row.append(jnp.exp(last))
            w_dec.append(jnp.exp(last - cum[c]))
            kd.append(k * w_dec[c])
        sts = [st_ref[0, c] for c in cs]
        dq_st = [_mm(dg[c], sts[c]) for c in cs]
        dst_in = [_mm_tn(dg[c], q_pos[c]) for c in cs]
        dv, dq_pos, dk_neg, dq_neg, dk_pos = [], [], [], [], []
        for c in cs:
            attn = jnp.where(lower, past[c], fut[c])
            dpast = jnp.where(lower, dattn[c], 0.0)
            dfut = jnp.where(lower, 0.0, dattn[c])
            dv.append(_mm_tn(attn, do_s[c]))
            dq_pos.append(_fold_heads(_mm(dpast, k_neg[c]), mk, C) + dq_st[c])
            dk_neg.append(_mm_tn(dpast, qp_s[c]))
            dq_neg.append(_fold_heads(_mm(dfut, k_pos[c]), mk, C))
            dk_pos.append(_mm_tn(dfut, qn_s[c]))
        dst_next = dst_sc[...]
        d_a, d_kd = [None] * G, [None] * G
        for c in reversed(cs):
            d_a[c] = jnp.sum(dst_next * sts[c], axis=0, keepdims=True)
            gmat = jnp.where(bd, dst_next, 0.0)
            d_kd[c] = _mm(vs[c], gmat)
            dv[c] = dv[c] + _mm_nt(kd[c], gmat)
            dst_next = dst_next * a_row[c] + jnp.where(bd, dst_in[c], 0.0)
        dst_sc[...] = dst_next
        row = lax.broadcasted_iota(jnp.int32, (C, 128), 0)
        d_la, dk, dq = [], [], []
        for c in cs:
            t = d_kd[c] * kd[c]
            dk.append(d_kd[c] * w_dec[c] + dk_neg[c] * e_neg[c] + dk_pos[c] * e_pos[c])
            dq.append(dq_pos[c] * e_pos[c] + dq_neg[c] * e_neg[c])
            d_last = jnp.sum(t, axis=0, keepdims=True) + d_a[c] * a_row[c]
            d_cum = (dq_pos[c] * q_pos[c] - dk_neg[c] * k_neg[c] - dq_neg[c] * q_neg[c] + dk_pos[c] * k_pos[c] - t)
            d_la.append(_mm_f32(utri, d_cum + jnp.where(row == C - 1, d_last, 0.0)))
        d_pre = [d_la[c] * _sig(-pre[c]) * (1.0 / GLA_TAU) for c in cs]
        d_gg = [_mm_nt(d_pre[c], w_ref[...]) for c in cs]
        dw_acc = _mm_tn(ps[0][:, 512:640], d_pre[0])
        db_acc = jnp.sum(d_pre[0], axis=0, keepdims=True)
        for c in cs[1:]:
            dw_acc = dw_acc + _mm_tn(ps[c][:, 512:640], d_pre[c])
            db_acc = db_acc + jnp.sum(d_pre[c], axis=0, keepdims=True)
        for c in cs:
            dp_ref[0, rows[c], :] = jnp.concatenate([dq[c], dk[c] * GLA_KSCALE, dv[c], d_gg[c], dz[c]],
                                                    axis=1).astype(_MXU)
        dw_ref[...] += dw_acc
        db_ref[...] += db_acc
        dgn_ref[...] += dgn_acc

        @pl.when((pl.program_id(0) == B - 1) & (pl.program_id(1) == NG - 1))
        def _():
            s1 = dgn_ref[...]
            s1 = s1 + pltpu.roll(s1, 128, 1)
            dgn_ref[...] = s1 + pltpu.roll(s1, 64, 1)

    tok = lambda w: pl.BlockSpec((1, G * C, w), lambda b, i: (b, NG - 1 - i, 0))
    return pl.pallas_call(
        body, name="gla_bwd", grid=(B, NG),
        in_specs=[tok(GLA_W), _full((128, 128)), _full((1, 128)), _full((1, 256)), tok(256),
                  pl.BlockSpec((1, G, 256, 128), lambda b, i: (b, NG - 1 - i, 0, 0)), tok(256)],
        out_specs=[tok(GLA_W), _full((128, 128)), _full((1, 128)), _full((1, 256))],
        out_shape=[jax.ShapeDtypeStruct((B, S, GLA_W), _MXU), jax.ShapeDtypeStruct((128, 128), F32),
                   jax.ShapeDtypeStruct((1, 128), F32), jax.ShapeDtypeStruct((1, 256), F32)],
        scratch_shapes=[pltpu.VMEM((256, 128), F32)],
        compiler_params=_cp(("arbitrary", "arbitrary")),
    )(gla_p, w_g2p, b_g2, gnw, raw, states, d_mix)


def _rms(x, w):
    rstd = lax.rsqrt(jnp.mean(x * x, axis=-1, keepdims=True) + EPS)
    xh = x * rstd
    return xh, rstd, xh * w


def _rms_bwd(dy, xh, rstd, w):
    dxh = dy * w
    return rstd * (dxh - xh * jnp.mean(dxh * xh, axis=-1, keepdims=True))


MLA_T = 256


def _mla_prep_fwd(mla_p, cos, sin, qnw, kvnw, w_uq, w_ukv):
    B, S, _ = mla_p.shape
    tm = min(S, 512)

    t = min(MLA_T, S)
    nt = tm // t

    def body(p_ref, c_ref, s_ref, qn_ref, kn_ref, wq_ref, wkv_ref, wkvt_ref, q_ref, k_ref, v_ref, kt_ref, vt_ref):
        p = p_ref[0].astype(F32)
        cs, sn = c_ref[0], s_ref[0]
        _, _, qn = _rms(p[:, 0:256], qn_ref[...])
        qpre = _mm(qn, wq_ref[...])
        _, _, kvn = _rms(p[:, 256:384], kn_ref[...])
        kv = _mm(kvn, wkv_ref[...])
        kvt = _mm_nt(wkvt_ref[...], kvn)
        kpe = _rope128(p[:, 384:512], cs, sn)
        kpet = kpe.T
        for h in range(8):
            sl = slice(128 * h, 128 * h + 128)
            q_ref[0, :, sl] = _rope128(qpre[:, sl], cs, sn).astype(_MXU)
            k_ref[0, :, sl] = (kv[:, sl] + kpe).astype(_MXU)
            kht = kvt[sl, :] + kpet
            for n in range(nt):
                kt_ref[0, n, sl, :] = kht[:, n * t:(n + 1) * t].astype(_MXU)
        v_ref[0] = kv[:, 1024:1536].astype(_MXU)
        for n in range(nt):
            vt_ref[0, n] = kvt[1024:1536, n * t:(n + 1) * t].astype(_MXU)

    tok = lambda w: pl.BlockSpec((1, tm, w), lambda b, i: (b, i, 0))
    tr = lambda w: pl.BlockSpec((1, nt, w, t), lambda b, i: (b, i, 0, 0))
    return pl.pallas_call(
        body, name="mla_prep_fwd", grid=(B, S // tm),
        in_specs=[tok(512), tok(128), tok(128), _full((1, 256)), _full((1, 128)), _full((256, 1024)),
                  _full((128, 1536)), _full((1536, 128))],
        out_specs=[tok(1024), tok(1024), tok(512), tr(1024), tr(512)],
        out_shape=[jax.ShapeDtypeStruct((B, S, 1024), _MXU), jax.ShapeDtypeStruct((B, S, 1024), _MXU),
                   jax.ShapeDtypeStruct((B, S, 512), _MXU), jax.ShapeDtypeStruct((B, S // t, 1024, t), _MXU),
                   jax.ShapeDtypeStruct((B, S // t, 512, t), _MXU)],
        compiler_params=_cp(("parallel", "parallel")),
    )(mla_p, cos, sin, qnw, kvnw, w_uq, w_ukv, w_ukv.T)


def _chunk_mask_t(t):
    kj = lax.broadcasted_iota(jnp.int32, (t, t), 0) // CHUNK
    qi = lax.broadcasted_iota(jnp.int32, (t, t), 1) // CHUNK
    return kj <= qi


MLA_HG = 8
MLA_HG_FWD = 8
LOG2E = 1.4426950408889634
MLA_C2 = MLA_SCALE * LOG2E


def _mla_attn_fwd(q, k, vt):
    B, S, _ = q.shape
    t = min(MLA_T, S)
    nq = S // t
    HG = MLA_HG_FWD
    NP = HG // 2

    def body(q_ref, k_ref, vt_ref, o_ref, lse_ref, sa, sb, m_sc, l_sc, acc_sc):
        i = pl.program_id(2)
        row = lax.broadcasted_iota(jnp.int32, (128, 1), 0)
        low = row < 64
        mask = _chunk_mask_t(t)
        m_sc[...] = jnp.full(m_sc.shape, -jnp.inf, F32)
        l_sc[...] = jnp.zeros_like(l_sc)
        acc_sc[...] = jnp.zeros_like(acc_sc)

        ones = jnp.ones((8, t), _MXU)

        def scores(j, buf):
            kb = k_ref[0, pl.ds(pl.multiple_of(j * t, t), t), :]
            for h in range(HG):
                cols = slice(128 * h, 128 * h + 128)
                buf[h] = (_mm_nt(kb[:, cols], q_ref[0, :, cols]) * MLA_C2).astype(_MXU)

        def absorb(j, buf, masked):
            vtb = vt_ref[0, j]
            for pr in range(NP):
                alphas, pvs = [], []
                for hh in range(2):
                    h = 2 * pr + hh
                    s = buf[h]
                    if masked:
                        s = jnp.where(mask, s, jnp.full_like(s, -jnp.inf))
                    m_old = m_sc[h]
                    m_new = jnp.maximum(m_old, jnp.max(s, axis=0, keepdims=True).astype(F32))
                    alpha = jnp.exp2(m_old - m_new)
                    p = jnp.exp2(s - m_new.astype(_MXU))
                    l_sc[h] = alpha * l_sc[h] + _mm(ones, p)[0:1, :]
                    m_sc[h] = m_new
                    vth = vtb[128 * pr:128 * pr + 128, :]
                    vth = jnp.where(low if hh == 0 else ~low, vth, jnp.zeros_like(vth))
                    pvs.append(_mm(vth, p))
                    alphas.append(alpha)
                acc_sc[pr] = acc_sc[pr] * jnp.where(low, alphas[0], alphas[1]) + pvs[0] + pvs[1]

        scores(0, sb)

        def pair(jj, carry):
            j0 = 2 * jj
            scores(j0 + 1, sa)
            absorb(j0, sb, False)
            scores(j0 + 2, sb)
            absorb(j0 + 1, sa, False)
            return carry

        lax.fori_loop(0, i // 2, pair, 0)

        @pl.when(i % 2 == 1)
        def _():
            scores(i, sa)
            absorb(i - 1, sb, False)
            absorb(i, sa, True)

        @pl.when(i % 2 == 0)
        def _():
            absorb(i, sb, True)

        for pr in range(NP):
            l_e, l_o = l_sc[2 * pr], l_sc[2 * pr + 1]
            o_ref[0, :, 128 * pr:128 * pr + 128] = (acc_sc[pr] / jnp.where(low, l_e, l_o)).T
            lse_ref[0, pr, 0, 0:1, :] = m_sc[2 * pr] + jnp.log(l_e) * LOG2E
            lse_ref[0, pr, 0, 1:2, :] = m_sc[2 * pr + 1] + jnp.log(l_o) * LOG2E

    return pl.pallas_call(
        body, name="mla_attn_fwd", grid=(B, 8 // HG, nq),
        in_specs=[pl.BlockSpec((1, t, 128 * HG), lambda b, g, i: (b, i, g)),
                  pl.BlockSpec((1, S, 128 * HG), lambda b, g, i: (b, 0, g)),
                  pl.BlockSpec((1, nq, 64 * HG, t), lambda b, g, i: (b, 0, g, 0))],
        out_specs=[pl.BlockSpec((1, t, 64 * HG), lambda b, g, i: (b, i, g)),
                   pl.BlockSpec((1, NP, 1, 2, t), lambda b, g, i: (b, g, i, 0, 0))],
        out_shape=[jax.ShapeDtypeStruct((B, S, 512), F32), jax.ShapeDtypeStruct((B, 4, nq, 2, t), F32)],
        scratch_shapes=[pltpu.VMEM((HG, t, t), _MXU), pltpu.VMEM((HG, t, t), _MXU), pltpu.VMEM((HG, 1, t), F32),
                        pltpu.VMEM((HG, 1, t), F32), pltpu.VMEM((NP, 128, t), F32)],
        compiler_params=_cp(("parallel", "parallel", "arbitrary")),
    )(q, k, vt)


def _mla_attn_bwd(q, k, v, kt, do, lse, dl):
    B, S, _ = q.shape
    t = min(MLA_T, S)
    nk = S // t

    HG = MLA_HG
    NP = HG // 2

    def body(q_ref, k_ref, v_ref, kt_ref, do_ref, lse_ref, dl_ref, dq_ref, dk_ref, dv_ref,
             sa, da, sb, db, dqt_sc, dk_sc, dv_sc):
        j = pl.program_id(2)

        @pl.when(j == 0)
        def _():
            dqt_sc[...] = jnp.zeros_like(dqt_sc)

        dk_sc[...] = jnp.zeros_like(dk_sc)
        dv_sc[...] = jnp.zeros_like(dv_sc)
        lane = lax.broadcasted_iota(jnp.int32, (1, 128), 1)
        low = lane < 64
        mask = _chunk_mask_t(t)

        def half(x, hh):
            return jnp.where(low if hh == 0 else ~low, x, jnp.zeros_like(x))

        def prepare(i, sbuf, dbuf):
            rows = pl.ds(pl.multiple_of(i * t, t), t)
            for h in range(HG):
                cols = slice(128 * h, 128 * h + 128)
                pc = slice(128 * (h // 2), 128 * (h // 2) + 128)
                sbuf[h] = _mm_nt(k_ref[0, :, cols], q_ref[0, rows, cols]) * MLA_C2
                dbuf[h] = _mm_nt(half(v_ref[0, :, pc], h % 2), do_ref[0, rows, pc])

        def absorb(i, sbuf, dbuf, masked):
            rows = pl.ds(pl.multiple_of(i * t, t), t)
            for h in range(HG):
                pr, hh = h // 2, h % 2
                cols = slice(128 * h, 128 * h + 128)
                pc = slice(128 * pr, 128 * pr + 128)
                p = jnp.exp2(sbuf[h] - lse_ref[0, pr, i][hh:hh + 1, :])
                if masked:
                    p = jnp.where(mask, p, 0.0)
                dv_sc[pr] += _mm(p, half(do_ref[0, rows, pc], hh))
                ds = p * (dbuf[h] - dl_ref[0, pr, i][hh:hh + 1, :])
                dqt_sc[i, cols, :] += _mm(kt_ref[0, 0, cols, :], ds)
                dk_sc[h] += _mm(ds, q_ref[0, rows, cols])

        n = nk - 1 - j
        prepare(jnp.minimum(j + 1, nk - 1), sb, db)

        def pair(jj, carry):
            i0 = j + 1 + 2 * jj
            prepare(i0 + 1, sa, da)
            absorb(i0, sb, db, False)
            prepare(jnp.where(i0 + 2 <= nk - 1, i0 + 2, j), sb, db)
            absorb(i0 + 1, sa, da, False)
            return carry

        lax.fori_loop(0, n // 2, pair, 0)

        @pl.when(n % 2 == 1)
        def _():
            prepare(j, sa, da)
            absorb(nk - 1, sb, db, False)
            absorb(j, sa, da, True)

        @pl.when(n % 2 == 0)
        def _():
            absorb(j, sb, db, True)

        for h in range(HG):
            dk_ref[0, :, 128 * h:128 * h + 128] = (dk_sc[h] * MLA_SCALE).astype(_MXU)
        for pr in range(NP):
            dv_ref[0, :, 128 * pr:128 * pr + 128] = dv_sc[pr].astype(_MXU)

        @pl.when(j == nk - 1)
        def _():
            for i in range(nk):
                dq_ref[0, i * t:(i + 1) * t, :] = (dqt_sc[i].T * MLA_SCALE).astype(_MXU)

    seq = lambda w: pl.BlockSpec((1, S, w), lambda b, g, j: (b, 0, g))
    blk = lambda w: pl.BlockSpec((1, t, w), lambda b, g, j: (b, j, g))
    stat = pl.BlockSpec((1, NP, nk, 2, t), lambda b, g, j: (b, g, 0, 0, 0))
    return pl.pallas_call(
        body, name="mla_attn_bwd", grid=(B, 8 // HG, nk),
        in_specs=[seq(128 * HG), blk(128 * HG), blk(64 * HG),
                  pl.BlockSpec((1, 1, 128 * HG, t), lambda b, g, j: (b, j, g, 0)), seq(64 * HG), stat, stat],
        out_specs=[seq(128 * HG), blk(128 * HG), blk(64 * HG)],
        out_shape=[jax.ShapeDtypeStruct((B, S, 1024), _MXU), jax.ShapeDtypeStruct((B, S, 1024), _MXU),
                   jax.ShapeDtypeStruct((B, S, 512), _MXU)],
        scratch_shapes=[pltpu.VMEM((HG, t, t), F32), pltpu.VMEM((HG, t, t), F32), pltpu.VMEM((HG, t, t), F32),
                        pltpu.VMEM((HG, t, t), F32), pltpu.VMEM((nk, 128 * HG, t), F32),
                        pltpu.VMEM((HG, t, 128), F32), pltpu.VMEM((NP, t, 128), F32)],
        compiler_params=_cp(("parallel", "parallel", "arbitrary"), 56),
    )(q, k, v, kt, do, lse, dl)


def _mla_prep_bwd(mla_p, cos, sin, qnw, kvnw, w_uq, w_ukv, dq, dk, dv):
    B, S, _ = mla_p.shape
    tm = min(S, 512)

    def body(p_ref, c_ref, s_ref, qn_ref, kn_ref, wq_ref, wkv_ref, dq_ref, dk_ref, dv_ref,
             dp_ref, dwq_ref, dwkv_ref, dqn_ref, dkn_ref):
        first = (pl.program_id(0) == 0) & (pl.program_id(1) == 0)

        @pl.when(first)
        def _():
            dwq_ref[...] = jnp.zeros_like(dwq_ref)
            dwkv_ref[...] = jnp.zeros_like(dwkv_ref)
            dqn_ref[...] = jnp.zeros_like(dqn_ref)
            dkn_ref[...] = jnp.zeros_like(dkn_ref)

        p = p_ref[0].astype(F32)
        cs, sn = c_ref[0], s_ref[0]
        lane = lax.broadcasted_iota(jnp.int32, (1, 128), 1)
        pe = (lane >= 64) & (lane < 96)
        qh, q_rstd, qn = _rms(p[:, 0:256], qn_ref[...])
        kvh, kv_rstd, kvn = _rms(p[:, 256:384], kn_ref[...])
        dqv = dq_ref[0].astype(F32)
        dkv = dk_ref[0].astype(F32)
        dqpre = jnp.concatenate(
            [_rope128_t(dqv[:, 128 * h:128 * h + 128], cs, sn) for h in range(8)], axis=1)
        dkpe = jnp.zeros((tm, 128), F32)
        for h in range(8):
            dkpe = dkpe + jnp.where(pe, dkv[:, 128 * h:128 * h + 128], 0.0)
        dkr = _rope128_t(dkpe, cs, sn)
        dkv_all = jnp.concatenate([dkv, dv_ref[0].astype(F32)], axis=1)
        d_qn = _mm_nt(dqpre, wq_ref[...])
        d_kvn = _mm_nt(dkv_all, wkv_ref[...])
        dwq_ref[...] += _mm_tn(qn, dqpre)
        dwkv_ref[...] += _mm_tn(kvn, dkv_all)
        dqn_ref[...] += jnp.sum(d_qn * qh, axis=0, keepdims=True)
        dkn_ref[...] += jnp.sum(d_kvn * kvh, axis=0, keepdims=True)
        dp_ref[0] = jnp.concatenate([_rms_bwd(d_qn, qh, q_rstd, qn_ref[...]),
                                     _rms_bwd(d_kvn, kvh, kv_rstd, kn_ref[...]), dkr], axis=1).astype(_MXU)

    tok = lambda w: pl.BlockSpec((1, tm, w), lambda b, i: (b, i, 0))
    return pl.pallas_call(
        body, name="mla_prep_bwd", grid=(B, S // tm),
        in_specs=[tok(512), tok(128), tok(128), _full((1, 256)), _full((1, 128)), _full((256, 1024)),
                  _full((128, 1536)), tok(1024), tok(1024), tok(512)],
        out_specs=[tok(512), _full((256, 1024)), _full((128, 1536)), _full((1, 256)), _full((1, 128))],
        out_shape=[jax.ShapeDtypeStruct((B, S, 512), _MXU), jax.ShapeDtypeStruct((256, 1024), F32),
                   jax.ShapeDtypeStruct((128, 1536), F32), jax.ShapeDtypeStruct((1, 256), F32),
                   jax.ShapeDtypeStruct((1, 128), F32)],
        compiler_params=_cp(("arbitrary", "arbitrary")),
    )(mla_p, cos, sin, qnw, kvnw, w_uq, w_ukv, dq, dk, dv)


def _out_fwd(x, gate, r_g, o_mla, mla_p, g_g, w_out):
    B, S, D = x.shape
    tm = min(S, 512)

    def body(x_ref, g_ref, r_ref, o_ref, z_ref, gg_ref, w_ref, xn_ref, y_ref):
        mm = (o_ref[0] * _silu(z_ref[0].astype(F32))).astype(_MXU)
        y = (jnp.dot(r_ref[0], w_ref[0:256, :], preferred_element_type=F32)
             + jnp.dot(mm, w_ref[256:768, :], preferred_element_type=F32)
             + jnp.dot(gg_ref[0], w_ref[768:1024, :], preferred_element_type=F32))
        y_ref[0] = y.astype(_MXU)
        xn_ref[0] = x_ref[0] + g_ref[0] * y

    tok = lambda w, c=0: pl.BlockSpec((1, tm, w), lambda b, i: (b, i, c))
    return pl.pallas_call(
        body, name="out_fwd", grid=(B, S // tm),
        in_specs=[tok(D), pl.BlockSpec((1, 1, D), lambda b, i: (b, 0, 0)), tok(256), tok(512), tok(512, 1),
                  tok(256), _full((D, D))],
        out_specs=[tok(D), tok(D)],
        out_shape=[jax.ShapeDtypeStruct((B, S, D), F32), jax.ShapeDtypeStruct((B, S, D), _MXU)],
        compiler_params=_cp(("parallel", "parallel")),
    )(x, gate, r_g, o_mla, mla_p, g_g, w_out)


def _out_bwd(dx, y, gate, r_g, g_g, w_out, o_mla, mla_p):
    B, S, D = dx.shape
    tm = min(S, 512)
    t = min(MLA_T, S)
    nt = tm // t

    def body(dx_ref, y_ref, g_ref, r_ref, gg_ref, w_ref, o_ref, z_ref,
             dr_ref, do_ref, dz_ref, dl_ref, dg_ref, dw_ref, dgate_ref, acc):
        first = (pl.program_id(0) == 0) & (pl.program_id(1) == 0)

        @pl.when(first)
        def _():
            acc[...] = jnp.zeros_like(acc)

        @pl.when(pl.program_id(1) == 0)
        def _():
            dgate_ref[...] = jnp.zeros_like(dgate_ref)

        dxv = dx_ref[0]
        dgate_ref[0] += jnp.sum(dxv * y_ref[0].astype(F32), axis=0, keepdims=True)
        dy = (dxv * g_ref[0]).astype(_MXU)
        dr_ref[0] = _mm_nt(dy, w_ref[0:256, :])
        dg_ref[0] = _mm_nt(dy, w_ref[768:1024, :])
        ov, z = o_ref[0], z_ref[0].astype(F32)
        acc[0:256, :] += _mm_tn(r_ref[0], dy)
        acc[256:768, :] += _mm_tn((ov * _silu(z)).astype(_MXU), dy)
        acc[768:1024, :] += _mm_tn(gg_ref[0], dy)

        @pl.when((pl.program_id(0) == B - 1) & (pl.program_id(1) == S // tm - 1))
        def _():
            dw_ref[...] = acc[...].astype(_MXU)

        dm = _mm_nt(dy, w_ref[256:768, :])
        do = dm * _silu(z)
        dz_ref[0] = (dm * ov * _dsilu(z)).astype(_MXU)
        do_ref[0] = do.astype(_MXU)
        prod = do * ov
        for pr in range(4):
            pt = prod[:, 128 * pr:128 * pr + 128].T
            se = jnp.sum(pt[0:64], axis=0, keepdims=True)
            so = jnp.sum(pt[64:128], axis=0, keepdims=True)
            for n in range(nt):
                dl_ref[0, pr, n, 0:1, :] = se[:, n * t:(n + 1) * t]
                dl_ref[0, pr, n, 1:2, :] = so[:, n * t:(n + 1) * t]

    tok = lambda w, c=0: pl.BlockSpec((1, tm, w), lambda b, i: (b, i, c))
    per_seq = pl.BlockSpec((1, 1, D), lambda b, i: (b, 0, 0))
    return pl.pallas_call(
        body, name="out_bwd", grid=(B, S // tm),
        in_specs=[tok(D), tok(D), per_seq, tok(256), tok(256), _full((D, D)), tok(512), tok(512, 1)],
        out_specs=[tok(256), tok(512), tok(512), pl.BlockSpec((1, 4, nt, 2, t), lambda b, i: (b, 0, i, 0, 0)),
                   tok(256), _full((D, D)), per_seq],
        out_shape=[jax.ShapeDtypeStruct((B, S, 256), F32), jax.ShapeDtypeStruct((B, S, 512), _MXU),
                   jax.ShapeDtypeStruct((B, S, 512), _MXU), jax.ShapeDtypeStruct((B, 4, S // t, 2, t), F32),
                   jax.ShapeDtypeStruct((B, S, 256), F32), jax.ShapeDtypeStruct((D, D), _MXU),
                   jax.ShapeDtypeStruct((B, 1, D), F32)],
        scratch_shapes=[pltpu.VMEM((D, D), F32)],
        compiler_params=_cp(("arbitrary", "arbitrary")),
    )(dx, y, gate, r_g, g_g, w_out, o_mla, mla_p)


def _proj_bwd_x(x, shift, scale, nw, w_arr, d_ret, d_mla, d_mz, d_gla, dx_out):
    B, S, D = x.shape
    tm = min(S, 512)

    def body(x_ref, sc_ref, nw_ref, w_ref, dr_ref, dm_ref, dz_ref, dg_ref, dxo_ref,
             dx_ref, dsh_ref, dsc_ref, dnw_ref):
        first = (pl.program_id(0) == 0) & (pl.program_id(1) == 0)

        @pl.when(first)
        def _():
            dnw_ref[...] = jnp.zeros_like(dnw_ref)

        @pl.when(pl.program_id(1) == 0)
        def _():
            dsh_ref[...] = jnp.zeros_like(dsh_ref)
            dsc_ref[...] = jnp.zeros_like(dsc_ref)

        dp = jnp.concatenate([dr_ref[0], dm_ref[0], dz_ref[0], dg_ref[0]], axis=1)
        dh = lax.dot_general(dp, w_ref[...], (((1,), (1,)), ((), ())), preferred_element_type=F32)
        xv = x_ref[0]
        rstd = lax.rsqrt(jnp.mean(xv * xv, axis=-1, keepdims=True) + EPS)
        xh = xv * rstd
        nwv = nw_ref[...]
        mod = 1.0 + sc_ref[0]
        dsh_ref[0] += jnp.sum(dh, axis=0, keepdims=True)
        dsc_ref[0] += jnp.sum(dh * xh * nwv, axis=0, keepdims=True)
        dnw_ref[...] += jnp.sum(dh * xh * mod, axis=0, keepdims=True)
        dxh = dh * nwv * mod
        dx_ref[0] = dxo_ref[0] + rstd * (dxh - xh * jnp.mean(dxh * xh, axis=-1, keepdims=True))

    tok = lambda w: pl.BlockSpec((1, tm, w), lambda b, i: (b, i, 0))
    per_seq = pl.BlockSpec((1, 1, D), lambda b, i: (b, 0, 0))
    return pl.pallas_call(
        body, name="proj_bwd_x", grid=(B, S // tm),
        in_specs=[tok(D), per_seq, _full((1, D)), _full((D, ARR_W)), tok(RET_W), tok(512), tok(512),
                  tok(GLA_W), tok(D)],
        out_specs=[tok(D), per_seq, per_seq, _full((1, D))],
        out_shape=[jax.ShapeDtypeStruct((B, S, D), F32), jax.ShapeDtypeStruct((B, 1, D), F32),
                   jax.ShapeDtypeStruct((B, 1, D), F32), jax.ShapeDtypeStruct((1, D), F32)],
        compiler_params=_cp(("arbitrary", "arbitrary")),
    )(x, scale, nw, w_arr, d_ret, d_mla, d_mz, d_gla, dx_out)


def _proj_bwd_w(h, d_ret, d_mla, d_mz, d_gla):
    B, S, D = h.shape
    tm = min(S, 512)

    def body(h_ref, dr_ref, dm_ref, dz_ref, dg_ref, dw_ref, acc):
        first = (pl.program_id(0) == 0) & (pl.program_id(1) == 0)

        @pl.when(first)
        def _():
            acc[...] = jnp.zeros_like(acc)

        hv = h_ref[0]
        tn = lambda d_ref: lax.dot_general(hv, d_ref[0], (((0,), (0,)), ((), ())), preferred_element_type=F32)
        acc[:, 0:RET_W] += tn(dr_ref)
        acc[:, RET_W:RET_W + 512] += tn(dm_ref)
        acc[:, RET_W + 512:RET_W + MLA_W] += tn(dz_ref)
        acc[:, RET_W + MLA_W:ARR_W] += tn(dg_ref)

        @pl.when((pl.program_id(0) == B - 1) & (pl.program_id(1) == S // tm - 1))
        def _():
            dw_ref[...] = acc[...].astype(_MXU)

    tok = lambda w: pl.BlockSpec((1, tm, w), lambda b, i: (b, i, 0))
    return pl.pallas_call(
        body, name="proj_bwd_w", grid=(B, S // tm),
        in_specs=[tok(D), tok(RET_W), tok(512), tok(512), tok(GLA_W)],
        out_specs=_full((D, ARR_W)), out_shape=jax.ShapeDtypeStruct((D, ARR_W), _MXU),
        scratch_shapes=[pltpu.VMEM((D, ARR_W), F32)],
        compiler_params=_cp(("arbitrary", "arbitrary"), 56),
    )(h, d_ret, d_mla, d_mz, d_gla)


def _out_fwd_loss(x, gate, r_g, o_mla, mla_p, g_g, w_out, fw, target):
    B, S, D = x.shape
    tm = min(S, 512)

    def body(x_ref, g_ref, r_ref, o_ref, z_ref, gg_ref, w_ref, fw_ref, t_ref, dx_ref, y_ref, loss_ref, dfw_ref):
        first = (pl.program_id(0) == 0) & (pl.program_id(1) == 0)

        @pl.when(first)
        def _():
            loss_ref[...] = jnp.zeros_like(loss_ref)
            dfw_ref[...] = jnp.zeros_like(dfw_ref)

        mm = (o_ref[0] * _silu(z_ref[0].astype(F32))).astype(_MXU)
        y = (jnp.dot(r_ref[0], w_ref[0:256, :], preferred_element_type=F32)
             + jnp.dot(mm, w_ref[256:768, :], preferred_element_type=F32)
             + jnp.dot(gg_ref[0], w_ref[768:1024, :], preferred_element_type=F32))
        y_ref[0] = y.astype(_MXU)
        xv = x_ref[0] + g_ref[0] * y
        fwv = fw_ref[...]
        rstd = lax.rsqrt(jnp.mean(xv * xv, axis=-1, keepdims=True) + EPS)
        xh = xv * rstd
        err = xh * fwv - t_ref[0]
        loss_ref[...] += 0.5 * jnp.sum(jnp.mean(err * err, axis=-1, keepdims=True), axis=0, keepdims=True)
        dy = err * (1.0 / D)
        dfw_ref[...] += jnp.sum(dy * xh, axis=0, keepdims=True)
        dxh = dy * fwv
        dx_ref[0] = rstd * (dxh - xh * jnp.mean(dxh * xh, axis=-1, keepdims=True))

    tok = lambda w, c=0: pl.BlockSpec((1, tm, w), lambda b, i: (b, i, c))
    return pl.pallas_call(
        body, name="out_fwd_loss", grid=(B, S // tm),
        in_specs=[tok(D), pl.BlockSpec((1, 1, D), lambda b, i: (b, 0, 0)), tok(256), tok(512), tok(512, 1),
                  tok(256), _full((D, D)), _full((1, D)), tok(D)],
        out_specs=[tok(D), tok(D), _full((1, 1)), _full((1, D))],
        out_shape=[jax.ShapeDtypeStruct((B, S, D), F32), jax.ShapeDtypeStruct((B, S, D), _MXU),
                   jax.ShapeDtypeStruct((1, 1), F32), jax.ShapeDtypeStruct((1, D), F32)],
        compiler_params=_cp(("arbitrary", "arbitrary")),
    )(x, gate, r_g, o_mla, mla_p, g_g, w_out, fw, target)


def _local_step(x, pos3, mod, loss_target, small, w_in_a, w_uq_a, w_ukv_a, w_out_b):
    B, S, D = x.shape
    tabs = _rope_tables(pos3)
    saved = []
    for l in range(DEPTH):
        last = (small["final_norm"].reshape(1, D), loss_target) if l == DEPTH - 1 else None
        x, s = _layer_fwd(x, tabs, mod[l], {n: a[l] for n, a in small.items() if n != "final_norm"},
                          w_in_a[l], w_uq_a[l], w_ukv_a[l], w_out_b[l], loss_head=last)
        saved.append(s)
    dx, loss, d_fw = x
    grads = dict(final_norm=d_fw.reshape(D))
    per_layer = [None] * DEPTH
    for l in reversed(range(DEPTH)):
        dx, per_layer[l] = _layer_bwd(dx, saved[l], tabs)
    for name in per_layer[0]:
        grads[name] = jnp.stack([per_layer[l][name] for l in range(DEPTH)])
    return loss, dx, grads


def _layer_fwd(x, tabs, mod_l, small_l, w_in_a, w_uq_a=None, w_ukv_a=None, w_out_b=None, late_weights=None,
               loss_head=None):
    B, S, D = x.shape
    cr, sr, cm, sm = tabs
    shift = mod_l[:, 0:D].reshape(B, 1, D)
    scale = mod_l[:, D:2 * D].reshape(B, 1, D)
    gate = mod_l[:, 2 * D:3 * D].reshape(B, 1, D)
    nw = small_l["norm_w"].reshape(1, D)
    qnw = small_l["mla_q_norm"].reshape(1, 256)
    kvnw = small_l["mla_kv_norm"].reshape(1, 128)
    w_g2p = jnp.pad(small_l["gla_w_g2"], ((0, 112), (0, 0)))
    b_g2 = small_l["gla_b_g2"].reshape(1, 128)
    gnw = jnp.tile(small_l["gla_norm"], 4).reshape(1, 256)
    ret_p, mla_p, gla_p, h = _proj_fwd(x, shift, scale, nw, w_in_a)
    r_g, r_raw, r_st = _ret_fwd(ret_p, cr, sr)
    if late_weights is not None:
        w_uq_a, w_ukv_a, w_out_b = late_weights(r_raw)
    q, k, v, kt, vt = _mla_prep_fwd(mla_p, cm, sm, qnw, kvnw, w_uq_a, w_ukv_a)
    o_mla, lse = _mla_attn_fwd(q, k, vt)
    g_g, g_raw, g_st = _gla_fwd(gla_p, w_g2p, b_g2, gnw)
    if loss_head is None:
        x_new, y = _out_fwd(x, gate, r_g, o_mla, mla_p, g_g, w_out_b)
    else:
        dx, y, loss, d_fw = _out_fwd_loss(x, gate, r_g, o_mla, mla_p, g_g, w_out_b, *loss_head)
        x_new = (dx, loss, d_fw)
    saved = dict(x=x, shift=shift, scale=scale, gate=gate, nw=nw, qnw=qnw, kvnw=kvnw, w_g2p=w_g2p, b_g2=b_g2,
                 gnw=gnw, ret_p=ret_p, mla_p=mla_p, gla_p=gla_p, h=h, r_g=r_g, r_raw=r_raw, r_st=r_st, q=q, k=k,
                 v=v, kt=kt, o_mla=o_mla, lse=lse, g_g=g_g, g_raw=g_raw, g_st=g_st, y=y,
                 w_in_a=w_in_a, w_uq_a=w_uq_a, w_ukv_a=w_ukv_a, w_out_b=w_out_b)
    return x_new, saved


def _layer_bwd(dx, s, tabs, early_grads=None):
    B, S, D = dx.shape
    cr, sr, cm, sm = tabs
    d_r, do, d_mz, dl, d_g, dw_out, d_gate = _out_bwd(dx, s["y"], s["gate"], s["r_g"], s["g_g"], s["w_out_b"],
                                                      s["o_mla"], s["mla_p"])
    d_ret = _ret_bwd(s["ret_p"], cr, sr, s["r_raw"], s["r_st"], d_r)
    dq, dk, dv = _mla_attn_bwd(s["q"], s["k"], s["v"], s["kt"], do, s["lse"], dl)
    d_mla, dw_uq, dw_ukv, d_qnw, d_kvnw = _mla_prep_bwd(
        s["mla_p"], cm, sm, s["qnw"], s["kvnw"], s["w_uq_a"], s["w_ukv_a"], dq, dk, dv)
    gnw = s["gnw"] if early_grads is None else s["gnw"] + early_grads(dw_out, dw_uq, dw_ukv)
    d_gla, dw_g2p, db_g2, d_gnw = _gla_bwd(s["gla_p"], s["w_g2p"], s["b_g2"], gnw, s["g_raw"], s["g_st"], d_g)
    dx, d_shift, d_scale, d_nw = _proj_bwd_x(s["x"], s["shift"], s["scale"], s["nw"], s["w_in_a"],
                                             d_ret, d_mla, d_mz, d_gla, dx)
    dw_in = _proj_bwd_w(s["h"], d_ret, d_mla, d_mz, d_gla)
    grads = dict(
        d_mod=jnp.concatenate([d_shift, d_scale, d_gate], axis=2).reshape(B, 3 * D),
        norm_w=d_nw.reshape(D), mla_q_norm=d_qnw.reshape(256), mla_kv_norm=d_kvnw.reshape(128),
        gla_w_g2=dw_g2p[0:16], gla_b_g2=db_g2.reshape(128), gla_norm256=d_gnw.reshape(256),
        w_in_a=dw_in, w_uq_a=dw_uq, w_ukv_a=dw_ukv, w_out=dw_out)
    return dx, grads


def _exchange(arrs, gather, name):
    n = len(arrs)
    out_shape = [jax.ShapeDtypeStruct(((N_DEV,) + a.shape) if g else a.shape, a.dtype)
                 for a, g in zip(arrs, gather)]

    def body(*refs):
        ins, outs = refs[:n], refs[n:2 * n]
        send_sems, recv_sems, local_sems = refs[2 * n:]
        ix, iy, ic = lax.axis_index("x"), lax.axis_index("y"), lax.axis_index("c")
        me = 4 * ix + 2 * iy + ic
        copies = []
        for a in range(n):
            mine = ins[a] if gather[a] else ins[a].at[me]
            loc = pltpu.make_async_copy(mine, outs[a].at[me], local_sems.at[a])
            loc.start()
            copies.append(loc)
            for d in range(1, N_DEV):
                px = 1 - ix if d & 4 else ix
                py = 1 - iy if d & 2 else iy
                pc = 1 - ic if d & 1 else ic
                src = ins[a] if gather[a] else ins[a].at[4 * px + 2 * py + pc]
                cp = pltpu.make_async_remote_copy(
                    src_ref=src, dst_ref=outs[a].at[me], send_sem=send_sems.at[a, d - 1],
                    recv_sem=recv_sems.at[a, d - 1], device_id=(px, py, pc), device_id_type=pl.DeviceIdType.MESH)
                cp.start()
                copies.append(cp)
        for cp in copies:
            cp.wait()

    any_spec = pl.BlockSpec(memory_space=pl.ANY)
    outs = pl.pallas_call(
        body, name=name, in_specs=[any_spec] * n, out_specs=[any_spec] * n, out_shape=out_shape,
        scratch_shapes=[pltpu.SemaphoreType.DMA((n, N_DEV - 1)), pltpu.SemaphoreType.DMA((n, N_DEV - 1)),
                        pltpu.SemaphoreType.DMA((n,))],
    )(*arrs)
    return list(outs)


def _peers(ix, iy, ic):
    out = []
    for d in range(1, N_DEV):
        px = 1 - ix if d & 4 else ix
        py = 1 - iy if d & 2 else iy
        pc = 1 - ic if d & 1 else ic
        out.append((d - 1, (px, py, pc), 4 * px + 2 * py + pc))
    return out


def _exchange_start(arrs, gather, name, after=None):
    n = len(arrs)
    lands = [lax.empty(((N_DEV,) + a.shape) if g else a.shape, a.dtype) for a, g in zip(arrs, gather)]
    extra = [] if after is None else [after]

    def body(*refs):
        ins, land_refs = refs[:n], refs[n:2 * n]
        send_sems, recv_sems = refs[2 * n + len(extra)], refs[2 * n + len(extra) + 1]
        token = refs[-1]
        ix, iy, ic = lax.axis_index("x"), lax.axis_index("y"), lax.axis_index("c")
        me = 4 * ix + 2 * iy + ic
        for a in range(n):
            for k, peer, peer_idx in _peers(ix, iy, ic):
                pltpu.make_async_remote_copy(
                    src_ref=ins[a] if gather[a] else ins[a].at[peer_idx], dst_ref=land_refs[a].at[me],
                    send_sem=send_sems.at[7 * a + k], recv_sem=recv_sems.at[7 * a + k], device_id=peer,
                    device_id_type=pl.DeviceIdType.MESH).start()
        token[...] = jnp.zeros_like(token)

    hbm = pl.BlockSpec(memory_space=pltpu.HBM)
    sem = pl.BlockSpec(memory_space=pltpu.SEMAPHORE)
    held = [pltpu.with_memory_space_constraint(a, pltpu.HBM) for a in list(arrs) + lands]
    outs = pl.pallas_call(
        body, name=name,
        out_shape=(pltpu.SemaphoreType.DMA((7 * n,)), pltpu.SemaphoreType.DMA((7 * n,)),
                   *[pltpu.HBM(a.shape, a.dtype) for a in held], jax.ShapeDtypeStruct((8, 128), F32)),
        in_specs=[hbm] * (2 * n) + [pl.BlockSpec(memory_space=pl.ANY)] * len(extra),
        out_specs=(sem, sem, *[hbm] * (2 * n), pl.BlockSpec(memory_space=pltpu.VMEM)),
        input_output_aliases={a: 2 + a for a in range(2 * n)},
        compiler_params=pltpu.CompilerParams(has_side_effects=pltpu.SideEffectType.DATAFLOW_SIDE_EFFECTING),
    )(*held, *extra)
    return dict(send=outs[0], recv=outs[1], srcs=list(outs[2:2 + n]), lands=list(outs[2 + n:2 + 2 * n]),
                token=outs[-1], gather=list(gather))


def _exchange_wait(flight, after, me, name):
    n = len(flight["srcs"])
    gather = flight["gather"]

    def body(*refs):
        srcs, land_refs = refs[:n], refs[n:2 * n]
        send_sems, recv_sems = refs[2 * n], refs[2 * n + 1]
        ix, iy, ic = lax.axis_index("x"), lax.axis_index("y"), lax.axis_index("c")
        mine = 4 * ix + 2 * iy + ic
        for a in range(n):
            for k, peer, peer_idx in _peers(ix, iy, ic):
                cp = pltpu.make_async_remote_copy(
                    src_ref=srcs[a] if gather[a] else srcs[a].at[peer_idx], dst_ref=land_refs[a].at[mine],
                    send_sem=send_sems.at[7 * a + k], recv_sem=recv_sems.at[7 * a + k], device_id=peer,
                    device_id_type=pl.DeviceIdType.MESH)
                cp.wait_send()
                cp.wait_recv()

    hbm = pl.BlockSpec(memory_space=pltpu.HBM)
    sem = pl.BlockSpec(memory_space=pltpu.SEMAPHORE)
    held = flight["srcs"] + flight["lands"]
    outs = pl.pallas_call(
        body, name=name, out_shape=tuple(pltpu.HBM(a.shape, a.dtype) for a in held),
        in_specs=[hbm] * (2 * n) + [sem, sem, pl.BlockSpec(memory_space=pl.ANY)], out_specs=tuple([hbm] * (2 * n)),
        input_output_aliases={a: a for a in range(2 * n)},
        compiler_params=pltpu.CompilerParams(has_side_effects=pltpu.SideEffectType.DATAFLOW_SIDE_EFFECTING),
    )(*held, flight["send"], flight["recv"], after)
    got = []
    for a in range(n):
        src, land = outs[a], outs[n + a]
        own = src if gather[a] else lax.dynamic_index_in_dim(src, me, axis=0, keepdims=False)
        got.append(lax.dynamic_update_index_in_dim(land, own, me, axis=0))
    return got


def _ada_fwd(c_all, ada_w, ada_b_cols):
    nb, D = c_all.shape
    cols = ada_w.shape[2]

    def body(c_ref, w_ref, b_ref, out_ref):
        ca = _silu(c_ref[...])
        for l in range(DEPTH):
            out_ref[l] = _mm(ca, w_ref[l]) + b_ref[l:l + 1, :]

    return pl.pallas_call(
        body, name="ada_fwd", out_shape=jax.ShapeDtypeStruct((DEPTH, nb, cols), F32),
        in_specs=[pl.BlockSpec(memory_space=pltpu.VMEM)] * 3, out_specs=pl.BlockSpec(memory_space=pltpu.VMEM),
        compiler_params=pltpu.CompilerParams(vmem_limit_bytes=32 * VMEM_MB),
    )(c_all, ada_w, ada_b_cols)


def _ada_bwd(c_all, d_mod_cols):
    nb, D = c_all.shape
    cols = d_mod_cols.shape[2]

    def body(c_ref, dm_ref, out_ref):
        ca = _silu(c_ref[...])
        for l in range(DEPTH):
            out_ref[l] = _mm_tn(ca, dm_ref[l])

    return pl.pallas_call(
        body, name="ada_bwd", out_shape=jax.ShapeDtypeStruct((DEPTH, D, cols), F32),
        in_specs=[pl.BlockSpec(memory_space=pltpu.VMEM)] * 2, out_specs=pl.BlockSpec(memory_space=pltpu.VMEM),
        compiler_params=pltpu.CompilerParams(vmem_limit_bytes=32 * VMEM_MB),
    )(c_all, d_mod_cols)


def _sum_adamw(parts, w, m, v, name, after=None):
    P, R, C = parts.shape
    tr = 256 if (R % 256 == 0 and R > 256) else R
    extra = [] if after is None else [after]

    def body(p_ref, w_ref, m_ref, v_ref, *rest):
        g_ref, d_ref, nm_ref, nv_ref = rest[-4:]
        g = p_ref[0].astype(F32)
        for k in range(1, P):
            g = g + p_ref[k].astype(F32)
        g_ref[...] = g
        nm = ADAM_B1 * m_ref[...] + (1.0 - ADAM_B1) * g
        nv = ADAM_B2 * v_ref[...] + (1.0 - ADAM_B2) * (g * g)
        nm_ref[...] = nm
        nv_ref[...] = nv
        m_hat = nm / (1.0 - ADAM_B1 ** ADAM_STEP)
        v_hat = nv / (1.0 - ADAM_B2 ** ADAM_STEP)
        d_ref[...] = -ADAM_LR * (m_hat / (jnp.sqrt(v_hat) + ADAM_EPS) + ADAM_WD * w_ref[...])

    blk = pl.BlockSpec((tr, C), lambda i: (i, 0))
    shp = jax.ShapeDtypeStruct((R, C), F32)
    return pl.pallas_call(
        body, name=name, grid=(R // tr,),
        in_specs=[pl.BlockSpec((P, tr, C), lambda i: (0, i, 0)), blk, blk, blk]
        + [pl.BlockSpec(memory_space=pl.ANY)] * len(extra),
        out_specs=[blk, blk, blk, blk], out_shape=[shp, shp, shp, shp],
        compiler_params=_cp(("parallel",)),
    )(parts, w, m, v, *extra)


def _sum_adamw_layer(parts, w, m, v, layer, name, prev=None, after=None):
    P, R, C = parts.shape
    tr = 256 if (R % 256 == 0 and R > 256) else R

    def body(p_ref, w_ref, m_ref, v_ref, *rest):
        g_ref, d_ref, nm_ref, nv_ref = rest[-4:]
        g = p_ref[0].astype(F32)
        for k in range(1, P):
            g = g + p_ref[k].astype(F32)
        g_ref[0] = g
        nm = ADAM_B1 * m_ref[0] + (1.0 - ADAM_B1) * g
        nv = ADAM_B2 * v_ref[0] + (1.0 - ADAM_B2) * (g * g)
        nm_ref[0] = nm
        nv_ref[0] = nv
        m_hat = nm / (1.0 - ADAM_B1 ** ADAM_STEP)
        v_hat = nv / (1.0 - ADAM_B2 ** ADAM_STEP)
        d_ref[0] = -ADAM_LR * (m_hat / (jnp.sqrt(v_hat) + ADAM_EPS) + ADAM_WD * w_ref[0])

    blk = pl.BlockSpec((1, tr, C), lambda i: (layer, i, 0))
    shp = jax.ShapeDtypeStruct(w.shape, F32)
    in_specs = [pl.BlockSpec((P, tr, C), lambda i: (0, i, 0)), blk, blk, blk]
    args = [parts, w, m, v]
    aliases = {}
    if prev is not None:
        in_specs += [pl.BlockSpec(memory_space=pl.ANY)] * 4
        args += list(prev)
        aliases = {4 + k: k for k in range(4)}
    if after is not None:
        in_specs.append(pl.BlockSpec(memory_space=pl.ANY))
        args.append(after)
    return list(pl.pallas_call(
        body, name=name, grid=(R // tr,), in_specs=in_specs, out_specs=[blk] * 4, out_shape=[shp] * 4,
        input_output_aliases=aliases, compiler_params=_cp(("parallel",)),
    )(*args))


SMALL = ["norm_w", "mla_q_norm", "mla_kv_norm", "gla_w_g2", "gla_b_g2", "gla_norm", "final_norm"]


SMALL_ROWS = 72


def _pack_small(loss, part):
    flat = [jnp.pad(loss.reshape(1), (0, 127))] + [part[n].reshape(-1) for n in SMALL]
    used = sum(f.shape[0] for f in flat)
    flat.append(jnp.zeros((SMALL_ROWS * 128 - used,), F32))
    return jnp.concatenate(flat).reshape(SMALL_ROWS, 128)


def _small_adamw(packed_parts, w, m, v, after=None):
    n = len(w)
    extra = [] if after is None else [after]

    def body(*refs):
        p_ref = refs[0]
        w_refs, m_refs, v_refs = refs[1:1 + n], refs[1 + n:1 + 2 * n], refs[1 + 2 * n:1 + 3 * n]
        outs, acc = refs[1 + 3 * n + len(extra):-1], refs[-1]
        total = p_ref[0]
        for k in range(1, N_DEV):
            total = total + p_ref[k]
        acc[...] = total
        outs[0][...] = acc[0:1, :]
        r0 = 1
        for i in range(n):
            shp = w_refs[i].shape
            if len(shp) == 3:
                g = acc[r0:r0 + shp[0] * shp[1], :].reshape(shp)
                r0 += shp[0] * shp[1]
            elif shp[1] < 128:
                g = acc[r0:r0 + shp[0], 0:shp[1]]
                r0 += shp[0]
            else:
                k = shp[1] // 128
                g = jnp.concatenate(
                    [jnp.concatenate([acc[r0 + l * k + j:r0 + l * k + j + 1, :] for j in range(k)], axis=1)
                     for l in range(shp[0])], axis=0)
                r0 += shp[0] * k
            nm = ADAM_B1 * m_refs[i][...] + (1.0 - ADAM_B1) * g
            nv = ADAM_B2 * v_refs[i][...] + (1.0 - ADAM_B2) * (g * g)
            m_hat = nm / (1.0 - ADAM_B1 ** ADAM_STEP)
            v_hat = nv / (1.0 - ADAM_B2 ** ADAM_STEP)
            outs[1 + 4 * i][...] = g
            outs[2 + 4 * i][...] = -ADAM_LR * (m_hat / (jnp.sqrt(v_hat) + ADAM_EPS) + ADAM_WD * w_refs[i][...])
            outs[3 + 4 * i][...] = nm
            outs[4 + 4 * i][...] = nv

    vmem = pl.BlockSpec(memory_space=pltpu.VMEM)
    out_shape = [jax.ShapeDtypeStruct((1, 128), F32)]
    for a in w:
        out_shape += [jax.ShapeDtypeStruct(a.shape, F32)] * 4
    outs = pl.pallas_call(
        body, name="adamw_small", in_specs=[vmem] * (1 + 3 * n) + [pl.BlockSpec(memory_space=pl.ANY)] * len(extra),
        out_specs=[vmem] * (1 + 4 * n), out_shape=out_shape, scratch_shapes=[pltpu.VMEM((SMALL_ROWS, 128), F32)],
    )(packed_parts, *w, *m, *v, *extra)
    return outs[0], [outs[1 + 4 * i:5 + 4 * i] for i in range(n)]


WEIGHTS = ["norm_w", "ada_w", "ada_b", "w_in", "mla_q_norm", "w_uq", "mla_kv_norm", "w_ukv", "gla_w_g2",
           "gla_b_g2", "gla_norm", "w_out", "final_norm"]


def kernel(x, c, positions, norm_w, ada_w, ada_b, w_in, mla_q_norm, w_uq, mla_kv_norm, w_ukv, gla_w_g2, gla_b_g2, gla_norm, w_out, final_norm, loss_target, m_norm_w, m_ada_w, m_ada_b, m_w_in, m_mla_q_norm, m_w_uq, m_mla_kv_norm, m_w_ukv, m_gla_w_g2, m_gla_b_g2, m_gla_norm, m_w_out, m_final_norm, v_norm_w, v_ada_w, v_ada_b, v_w_in, v_mla_q_norm, v_w_uq, v_mla_kv_norm, v_w_ukv, v_gla_w_g2, v_gla_b_g2, v_gla_norm, v_w_out, v_final_norm):
    w = dict(norm_w=norm_w, ada_w=ada_w, ada_b=ada_b, w_in=w_in, mla_q_norm=mla_q_norm, w_uq=w_uq,
             mla_kv_norm=mla_kv_norm, w_ukv=w_ukv, gla_w_g2=gla_w_g2, gla_b_g2=gla_b_g2, gla_norm=gla_norm,
             w_out=w_out, final_norm=final_norm)
    m = dict(norm_w=m_norm_w, ada_w=m_ada_w, ada_b=m_ada_b, w_in=m_w_in, mla_q_norm=m_mla_q_norm, w_uq=m_w_uq,
             mla_kv_norm=m_mla_kv_norm, w_ukv=m_w_ukv, gla_w_g2=m_gla_w_g2, gla_b_g2=m_gla_b_g2,
             gla_norm=m_gla_norm, w_out=m_w_out, final_norm=m_final_norm)
    v = dict(norm_w=v_norm_w, ada_w=v_ada_w, ada_b=v_ada_b, w_in=v_w_in, mla_q_norm=v_mla_q_norm, w_uq=v_w_uq,
             mla_kv_norm=v_mla_kv_norm, w_ukv=v_w_ukv, gla_w_g2=v_gla_w_g2, gla_b_g2=v_gla_b_g2,
             gla_norm=v_gla_norm, w_out=v_w_out, final_norm=v_final_norm)
    B, S, D = x.shape
    me = 4 * lax.axis_index("x") + 2 * lax.axis_index("y") + lax.axis_index("c")
    ada_cols = ada_w.shape[2]
    cast = lambda a: a.astype(_MXU)

    sharded = ["w_in", "w_uq", "w_ukv", "w_out"]

    whole_cols = lambda a: jnp.transpose(a, (1, 0, 2)).reshape(a.shape[1], -1)
    whole_in = lambda blk: _arrange_w_in(whole_cols(blk))
    whole_rest = lambda blks: (_arrange_w_uq(whole_cols(blks[0])), _arrange_w_ukv(whole_cols(blks[1])),
                               blks[2].reshape(D, D))
    col_blocks = lambda a: jnp.transpose(a.reshape(a.shape[0], N_DEV, -1), (1, 0, 2)).astype(jnp.bfloat16)
    blocks_in = lambda dw_in_a: col_blocks(_unarrange_w_in(dw_in_a))
    blocks_rest = lambda dw_out, dw_uq_a, dw_ukv_a: [
        col_blocks(_unarrange_w_uq(dw_uq_a)), col_blocks(_unarrange_w_ukv(dw_ukv_a)),
        dw_out.reshape(N_DEV, D // N_DEV, D).astype(jnp.bfloat16)]

    (c_g,) = _exchange([c], [True], "gather_c")
    flight_i = _exchange_start([cast(w_in[0])], [True], "gather_start_first", after=c_g)
    c_all = c_g.reshape(N_DEV * B, D) + flight_i["token"][0, 0]

    ada_b_cols = lax.dynamic_slice(ada_b, (0, me * ada_cols), (DEPTH, ada_cols))
    mod_cols = _ada_fwd(c_all, ada_w, ada_b_cols)
    mod_send = jnp.transpose(mod_cols.reshape(DEPTH, N_DEV, B, ada_cols), (1, 0, 2, 3))
    (mod_recv,) = _exchange([mod_send], [False], "scatter_mod")
    mod = jnp.transpose(mod_recv, (1, 2, 0, 3)).reshape(DEPTH, B, 3 * D)

    flight_r = _exchange_start([cast(w[n][0]) for n in sharded[1:]], [True] * 3, "gather_start_layer0", after=mod)
    flight_w = _exchange_start([cast(w[n][1]) for n in sharded], [True] * 4, "gather_start_layer1",
                               after=flight_r["token"])
    small_w = {n: w[n] for n in SMALL}
    layer_small = lambda l: {n: a[l] for n, a in small_w.items() if n != "final_norm"}
    tabs = _rope_tables(positions.reshape(B, S, 1), flight_w["token"][0, 0])
    late0 = lambda after: whole_rest(_exchange_wait(flight_r, after, me, "gather_wait_layer0"))
    (w_in0_g,) = _exchange_wait(flight_i, tabs[0], me, "gather_wait_first")
    x1, saved0 = _layer_fwd(x, tabs, mod[0], layer_small(0), whole_in(w_in0_g), late_weights=late0)
    got1 = _exchange_wait(flight_w, x1, me, "gather_wait_layer1")
    (dx, loss, d_fw), saved1 = _layer_fwd(x1, tabs, mod[1], layer_small(1), whole_in(got1[0]), *whole_rest(got1[1:]),
                                          loss_head=(final_norm.reshape(1, D), loss_target))

    dx, g1 = _layer_bwd(dx, saved1, tabs)
    flight_g = _exchange_start([blocks_in(g1["w_in_a"])] + blocks_rest(g1["w_out"], g1["w_uq_a"], g1["w_ukv_a"]),
                               [False] * 4, "grads_start_layer1")
    flights = {}

    def early0(dw_out, dw_uq_a, dw_ukv_a):
        flights["rest0"] = _exchange_start(blocks_rest(dw_out, dw_uq_a, dw_ukv_a), [False] * 3, "grads_start_layer0")
        return flights["rest0"]["token"][0, 0]

    saved0 = dict(saved0, gate=saved0["gate"] + flight_g["token"][0, 0])
    grad_x, g0 = _layer_bwd(dx, saved0, tabs, early_grads=early0)
    parts1 = _exchange_wait(flight_g, grad_x, me, "grads_wait_layer1")
    rest0 = _exchange_wait(flights["rest0"], g0["w_in_a"], me, "grads_wait_layer0")

    both = lambda n: jnp.stack([g0[n], g1[n]])
    d_mod = both("d_mod")
    part = dict(norm_w=both("norm_w"), mla_q_norm=both("mla_q_norm"), mla_kv_norm=both("mla_kv_norm"),
                gla_w_g2=both("gla_w_g2"), gla_b_g2=both("gla_b_g2"), gla_norm=both("gla_norm256")[:, 0:128],
                final_norm=d_fw)
    d_mod_g, small_g = _exchange([d_mod, _pack_small(loss, part)], [True, True], "gather_small")
    flight_l = _exchange_start([blocks_in(g0["w_in_a"])], [False], "exchange_start_last", after=small_g)
    res = {}
    behind = flight_l["token"]
    for a, name in enumerate(sharded):
        res[name] = _sum_adamw_layer(parts1[a], w[name], m[name], v[name], 1, "adamw_%s_layer1" % name, after=behind)
        behind = res[name][1]
    for a, name in enumerate(sharded[1:]):
        res[name] = _sum_adamw_layer(rest0[a], w[name], m[name], v[name], 0, "adamw_%s_layer0" % name,
                                     prev=res[name], after=behind)
        behind = res[name][1]

    d_mod_all = jnp.transpose(d_mod_g, (1, 0, 2, 3)).reshape(DEPTH, N_DEV * B, 3 * D)
    d_mod_cols = lax.dynamic_slice(d_mod_all, (0, 0, me * ada_cols), (DEPTH, N_DEV * B, ada_cols))
    g_ada_w = _ada_bwd(c_all, d_mod_cols)

    def update(name, parts2d, after):
        shp = w[name].shape
        two = lambda a: a.reshape(parts2d.shape[1:])
        out = _sum_adamw(parts2d, two(w[name]), two(m[name]), two(v[name]), "adamw_" + name, after=after)
        res[name] = [o.reshape(shp) for o in out]
        return out[1]

    behind = update("ada_w", g_ada_w.reshape(1, DEPTH * D, ada_cols), behind)
    behind = update("ada_b", jnp.transpose(d_mod_g, (0, 2, 1, 3)).reshape(N_DEV * B, DEPTH * 3 * D // 128, 128), behind)
    row = lambda a: a.reshape(1, D) if a.ndim == 1 else a
    loss_sum, small_out = _small_adamw(small_g, [row(w[n]) for n in SMALL], [row(m[n]) for n in SMALL],
                                       [row(v[n]) for n in SMALL], after=behind)
    for n, outs in zip(SMALL, small_out):
        res[n] = [o.reshape(w[n].shape) for o in outs]
    loss_out = loss_sum[0, 0]
    (in0,) = _exchange_wait(flight_l, loss_sum, me, "exchange_wait_last")
    res["w_in"] = _sum_adamw_layer(in0, w_in, m_w_in, v_w_in, 0, "adamw_w_in_layer0", prev=res["w_in"])
    return (loss_out, grad_x, *[res[n][0] for n in WEIGHTS], *[res[n][1] for n in WEIGHTS],
            *[res[n][2] for n in WEIGHTS], *[res[n][3] for n in WEIGHTS])
```

```python
import functools
import math

import numpy as np
import jax
import jax.numpy as jnp
from jax import lax
from jax.experimental import pallas as pl
from jax.experimental.pallas import tpu as pltpu

F32 = jnp.float32
_MXU = jnp.bfloat16

D_MODEL = 1024
DEPTH = 2
CHUNK = 64
EPS = 1e-6
ROPE_THETA = 10000.0
N_DEV = 8

MLA_SCALE = 96.0 ** -0.5
RET_KSCALE = 64.0 ** -0.5
GLA_KSCALE = 32.0 ** -0.5
GLA_TAU = 16.0

ADAM_LR = 0.001
ADAM_B1 = 0.9
ADAM_B2 = 0.999
ADAM_EPS = 1e-08
ADAM_WD = 0.01
ADAM_STEP = 10

RET_W, MLA_W, GLA_W = 1024, 1024, 896
ARR_W = RET_W + MLA_W + GLA_W
VMEM_MB = 1024 * 1024


def _cp(sem, vmem_mb=48):
    return pltpu.CompilerParams(dimension_semantics=sem, vmem_limit_bytes=vmem_mb * VMEM_MB)


def _mm(a, b):
    return jnp.dot(a.astype(_MXU), b.astype(_MXU), preferred_element_type=F32)


def _mm_nt(a, b):
    return lax.dot_general(a.astype(_MXU), b.astype(_MXU), (((1,), (1,)), ((), ())),
                           preferred_element_type=F32)


def _mm_tn(a, b):
    return lax.dot_general(a.astype(_MXU), b.astype(_MXU), (((0,), (0,)), ((), ())),
                           preferred_element_type=F32)


def _mm_f32(a, b):
    return jnp.dot(a, b, precision=lax.Precision.HIGHEST, preferred_element_type=F32)


def _sig(z):
    return 1.0 / (1.0 + jnp.exp(-z))


def _silu(z):
    return z * _sig(z)


def _dsilu(z):
    s = _sig(z)
    return s * (1.0 + z * (1.0 - s))


def _full(shape):
    nd = len(shape)
    return pl.BlockSpec(shape, lambda *_: (0,) * nd)


def _w_in_runs(block_cols):
    m, g = RET_W, RET_W + MLA_W
    whole = [(base + 64 * h + 32 * t, 32, base + 128 * t + 32 * h)
             for base in (0, 256) for t in range(2) for h in range(4)]
    whole += [(512, 512, 512), (1024, 384, m), (1408, 32, m + 448), (1440, 512, m + 512),
              (1952, 528, g), (2480, 256, g + 640)]
    zeros = [(m + 384, 64), (m + 480, 32), (g + 528, 112)]
    runs = []
    for src, n, dst in whole:
        while n:
            blk, off = divmod(src, block_cols)
            k = min(n, block_cols - off)
            runs.append((blk, off, k, dst))
            src, n, dst = src + k, n - k, dst + k
    return runs, zeros


def _arrange_w_in(blocks, tm=256):
    n, rows, cols = blocks.shape
    runs, zeros = _w_in_runs(cols)

    def arrange_w_in_kernel(b_ref, a_ref):
        for dst, k in zeros:
            a_ref[:, dst:dst + k] = jnp.zeros((tm, k), a_ref.dtype)
        for blk, off, k, dst in runs:
            a_ref[:, dst:dst + k] = b_ref[blk, :, off:off + k]

    return pl.pallas_call(
        arrange_w_in_kernel, grid=(rows // tm,),
        in_specs=[pl.BlockSpec((n, tm, cols), lambda i: (0, i, 0))],
        out_specs=pl.BlockSpec((tm, ARR_W), lambda i: (i, 0)),
        out_shape=jax.ShapeDtypeStruct((rows, ARR_W), blocks.dtype),
        compiler_params=_cp(("parallel",)), name="arrange_w_in")(blocks)


def _unarrange_w_in(a, n, cols, tm=256):
    rows = a.shape[0]
    runs, _ = _w_in_runs(cols)

    def unarrange_w_in_kernel(a_ref, b_ref):
        for blk, off, k, dst in runs:
            b_ref[blk, :, off:off + k] = a_ref[:, dst:dst + k].astype(b_ref.dtype)

    return pl.pallas_call(
        unarrange_w_in_kernel, grid=(rows // tm,),
        in_specs=[pl.BlockSpec((tm, ARR_W), lambda i: (i, 0))],
        out_specs=pl.BlockSpec((n, tm, cols), lambda i: (0, i, 0)),
        out_shape=jax.ShapeDtypeStruct((n, rows, cols), jnp.bfloat16),
        compiler_params=_cp(("parallel",)), name="unarrange_w_in")(a)


def _arrange_w_uq(w):
    return jnp.pad(w.reshape(256, 8, 96), ((0, 0), (0, 0), (0, 32))).reshape(256, 1024)


def _unarrange_w_uq(a):
    return a.reshape(256, 8, 128)[:, :, :96].reshape(256, 768)


def _arrange_w_ukv(w):
    r = w.reshape(128, 8, 128)
    k = jnp.pad(r[:, :, :64], ((0, 0), (0, 0), (0, 64))).reshape(128, 1024)
    return jnp.concatenate([k, r[:, :, 64:].reshape(128, 512)], axis=1)


def _unarrange_w_ukv(a):
    k = a[:, :1024].reshape(128, 8, 128)[:, :, :64]
    v = a[:, 1024:].reshape(128, 8, 64)
    return jnp.concatenate([k, v], axis=2).reshape(128, 1024)


def _rope_tables(pos3, zero=0.0):
    B, S, _ = pos3.shape
    ts = min(S, 512)
    inv32 = (np.float32(ROPE_THETA) ** (-(np.arange(32, dtype=np.float32) / 32))).astype(np.float32)
    inv16 = (np.float32(ROPE_THETA) ** (-(np.arange(16, dtype=np.float32) / 16))).astype(np.float32)
    inv = np.zeros((1, 128), np.float32)
    inv[0, 0:32] = inv32
    inv[0, 32:48] = inv16

    def body(pos_ref, inv_ref, cr, sr, cm, sm):
        ang = pos_ref[0].astype(F32) * inv_ref[...]
        lane = lax.broadcasted_iota(jnp.int32, (1, 128), 1)

        def every_head(x):
            y = jnp.where(lane < 32, x, pltpu.roll(x, 32, 1))
            return jnp.where(lane < 64, y, pltpu.roll(y, 64, 1))

        def rotary_pair(x, fill):
            return jnp.where((lane >= 64) & (lane < 80), pltpu.roll(x, 32, 1),
                             jnp.where((lane >= 80) & (lane < 96), pltpu.roll(x, 48, 1), fill))

        c, s = jnp.cos(ang), jnp.sin(ang)
        cr[0] = every_head(c)
        sr[0] = every_head(s)
        cm[0] = rotary_pair(c, 1.0)
        sm[0] = rotary_pair(s, 0.0)

    tab = jax.ShapeDtypeStruct((B, S, 128), F32)
    blk = pl.BlockSpec((1, ts, 128), lambda b, i: (b, i, 0))
    return pl.pallas_call(
        body, name="rope_tables", grid=(B, S // ts),
        in_specs=[pl.BlockSpec((1, ts, 1), lambda b, i: (b, i, 0)), _full((1, 128))],
        out_specs=[blk, blk, blk, blk], out_shape=[tab, tab, tab, tab],
        compiler_params=_cp(("parallel", "parallel")),
    )(pos3, jnp.asarray(inv) + zero)


def _rope128(x, cos, sin):
    lane = lax.broadcasted_iota(jnp.int32, (1, 128), 1)
    rp = pltpu.roll(x, 16, 1)
    rm = pltpu.roll(x, 112, 1)
    return x * cos + jnp.where(lane < 80, -rm, rp) * sin


def _rope128_t(d, cos, sin):
    lane = lax.broadcasted_iota(jnp.int32, (1, 128), 1)
    y = d * sin
    yp = pltpu.roll(y, 16, 1)
    ym = pltpu.roll(y, 112, 1)
    return d * cos + jnp.where(lane < 64, 0.0, jnp.where(lane < 80, ym, jnp.where(lane < 96, -yp, 0.0)))


def _proj_fwd(x, shift, scale, nw, w_arr):
    B, S, D = x.shape
    tm = min(S, 512)

    def body(x_ref, sh_ref, sc_ref, nw_ref, w_ref, ret_ref, mla_ref, gla_ref, h_ref):
        xv = x_ref[0]
        rstd = lax.rsqrt(jnp.mean(xv * xv, axis=-1, keepdims=True) + EPS)
        h = (xv * rstd * nw_ref[...]) * (1.0 + sc_ref[0]) + sh_ref[0]
        hb = h.astype(_MXU)
        h_ref[0] = hb
        ret_ref[0] = jnp.dot(hb, w_ref[:, 0:RET_W], preferred_element_type=F32).astype(_MXU)
        mla_ref[0] = jnp.dot(hb, w_ref[:, RET_W:RET_W + MLA_W], preferred_element_type=F32).astype(_MXU)
        gla_ref[0] = jnp.dot(hb, w_ref[:, RET_W + MLA_W:ARR_W], preferred_element_type=F32).astype(_MXU)

    tok = lambda w: pl.BlockSpec((1, tm, w), lambda b, i: (b, i, 0))
    per_seq = pl.BlockSpec((1, 1, D), lambda b, i: (b, 0, 0))
    return pl.pallas_call(
        body, name="proj_fwd", grid=(B, S // tm),
        in_specs=[tok(D), per_seq, per_seq, _full((1, D)), _full((D, ARR_W))],
        out_specs=[tok(RET_W), tok(MLA_W), tok(GLA_W), tok(D)],
        out_shape=[jax.ShapeDtypeStruct((B, S, RET_W), _MXU), jax.ShapeDtypeStruct((B, S, MLA_W), _MXU),
                   jax.ShapeDtypeStruct((B, S, GLA_W), _MXU), jax.ShapeDtypeStruct((B, S, D), _MXU)],
        compiler_params=_cp(("parallel", "parallel")),
    )(x, shift, scale, nw, w_arr)


RET_L = 256


def _ret_consts(L):
    lg = np.log1p(-np.exp2(-5.0 - np.arange(4, dtype=np.float32))).astype(np.float32)
    i = np.arange(L)
    ci = i // CHUNK
    diff = (i[:, None] - i[None, :]).astype(np.float32)
    same = ci[:, None] == ci[None, :]
    past = ci[None, :] < ci[:, None]
    expo = np.where(same, np.abs(diff), np.where(past, diff, 0.0)).astype(np.float32)
    dec = np.where((same | past)[None], np.exp(lg[:, None, None] * expo[None]), 0.0).astype(np.float32)
    head = (np.arange(256) % 128) // 32
    qw = np.exp((i + 1.0)[:, None] * lg[head][None, :]).astype(np.float32)
    kw = np.exp((L - 1.0 - i)[:, None] * lg[head][None, :]).astype(np.float32)
    a_row = np.exp(np.float32(L) * lg[head])[None, :].astype(np.float32)
    return [jnp.asarray(t) for t in (dec.reshape(4 * L, L), qw, kw, a_row)]


def _ret_masks():
    lane = lax.broadcasted_iota(jnp.int32, (1, 256), 1)
    mh = [((lane % 128) // 32) == h for h in range(4)]
    mv = [(lane // 64) == h for h in range(4)]
    vi = lax.broadcasted_iota(jnp.int32, (256, 256), 0)
    ki = lax.broadcasted_iota(jnp.int32, (256, 256), 1)
    bd = (vi // 64) == ((ki % 128) // 32)
    return mh, mv, bd


def _ret_rope(p, cs, sn):
    q1, q2, k1, k2 = p[:, 0:128], p[:, 128:256], p[:, 256:384], p[:, 384:512]
    qr = jnp.concatenate([q1 * cs - q2 * sn, q2 * cs + q1 * sn], axis=1)
    kr = jnp.concatenate([k1 * cs - k2 * sn, k2 * cs + k1 * sn], axis=1) * RET_KSCALE
    return qr, kr


def _head_mean(x, mv, width):
    out = jnp.zeros_like(x)
    for m in mv:
        s = jnp.sum(jnp.where(m, x, 0.0), axis=-1, keepdims=True) * (1.0 / width)
        out = jnp.where(m, s, out)
    return out


def _stack_heads(x, masks):
    return jnp.concatenate([jnp.where(m, x, 0.0) for m in masks], axis=0)


def _fold_heads(xs, masks, L):
    out = jnp.where(masks[0], xs[0:L], 0.0)
    for h in range(1, 4):
        out = out + jnp.where(masks[h], xs[h * L:(h + 1) * L], 0.0)
    return out


RET_G = 2


def _ret_fwd(ret_p, cos, sin):
    B, S, _ = ret_p.shape
    L = min(RET_L, S)
    NB = S // L
    G = min(RET_G, NB)
    NG = NB // G
    consts = _ret_consts(L)

    def body(p_ref, c_ref, s_ref, ds_ref, qw_ref, kw_ref, a_ref, out_ref, raw_ref, st_ref, st_sc):
        @pl.when(pl.program_id(1) == 0)
        def _():
            st_sc[...] = jnp.zeros_like(st_sc)

        mh, mv, bd = _ret_masks()
        cs_ = range(G)
        rows = [slice(c * L, (c + 1) * L) for c in cs_]
        ps = [p_ref[0, rows[c], :].astype(F32) for c in cs_]
        qk = [_ret_rope(ps[c], c_ref[0, rows[c], :], s_ref[0, rows[c], :]) for c in cs_]
        vs = [ps[c][:, 512:768] for c in cs_]
        a_s = [_mm_nt(_stack_heads(qk[c][0], mh), qk[c][1]) for c in cs_]
        upd = [_mm_tn(vs[c], qk[c][1] * kw_ref[...]) for c in cs_]
        o_s = [_mm(a_s[c] * ds_ref[...], vs[c]) for c in cs_]
        st = st_sc[...]
        inter = []
        for c in cs_:
            st_ref[0, c] = st
            inter.append(_mm_nt(qk[c][0] * qw_ref[...], st))
            st = st * a_ref[...] + jnp.where(bd, upd[c], 0.0)
        st_sc[...] = st
        for c in cs_:
            r = _fold_heads(o_s[c], mv, L) + inter[c]
            raw_ref[0, rows[c], :] = r
            rstd = lax.rsqrt(_head_mean(r * r, mv, 64.0) + EPS)
            out_ref[0, rows[c], :] = (r * rstd * _silu(ps[c][:, 768:1024])).astype(_MXU)

    tok = lambda w: pl.BlockSpec((1, G * L, w), lambda b, n: (b, n, 0))
    return pl.pallas_call(
        body, name="ret_fwd", grid=(B, NG),
        in_specs=[tok(RET_W), tok(128), tok(128), _full((4 * L, L)), _full((L, 256)), _full((L, 256)),
                  _full((1, 256))],
        out_specs=[tok(256), tok(256), pl.BlockSpec((1, G, 256, 256), lambda b, n: (b, n, 0, 0))],
        out_shape=[jax.ShapeDtypeStruct((B, S, 256), _MXU), jax.ShapeDtypeStruct((B, S, 256), F32),
                   jax.ShapeDtypeStruct((B, NB, 256, 256), F32)],
        scratch_shapes=[pltpu.VMEM((256, 256), F32)],
        compiler_params=_cp(("parallel", "arbitrary")),
    )(ret_p, cos, sin, *consts)


def _ret_bwd(ret_p, cos, sin, raw, states, d_mix):
    B, S, _ = ret_p.shape
    L = min(RET_L, S)
    NB = S // L
    G = 1
    NG = NB // G
    consts = _ret_consts(L)

    def body(p_ref, c_ref, s_ref, raw_ref, st_ref, dm_ref, ds_ref, qw_ref, kw_ref, a_ref, dp_ref, dst_sc):
        @pl.when(pl.program_id(1) == 0)
        def _():
            dst_sc[...] = jnp.zeros_like(dst_sc)

        mh, mv, bd = _ret_masks()
        qw, kw, dec = qw_ref[...], kw_ref[...], ds_ref[...]
        cs_ = range(G)
        rows = [slice(c * L, (c + 1) * L) for c in cs_]
        ps = [p_ref[0, rows[c], :].astype(F32) for c in cs_]
        tabs = [(c_ref[0, rows[c], :], s_ref[0, rows[c], :]) for c in cs_]
        qk = [_ret_rope(ps[c], *tabs[c]) for c in cs_]
        vs = [ps[c][:, 512:768] for c in cs_]
        qs = [_stack_heads(qk[c][0], mh) for c in cs_]
        a_s = [_mm_nt(qs[c], qk[c][1]) for c in cs_]
        dr, dz = [], []
        for c in cs_:
            r = raw_ref[0, rows[c], :]
            z = ps[c][:, 768:1024]
            rstd = lax.rsqrt(_head_mean(r * r, mv, 64.0) + EPS)
            rn = r * rstd
            dm = dm_ref[0, rows[c], :]
            d_rn = dm * _silu(z)
            dz.append(dm * rn * _dsilu(z))
            dr.append(rstd * (d_rn - rn * _head_mean(d_rn * rn, mv, 64.0)))
        do_s = [_stack_heads(dr[c], mv) for c in cs_]
        da_s = [_mm_nt(do_s[c], vs[c]) for c in cs_]
        sts = [st_ref[0, c] for c in cs_]
        dq_st = [_mm(dr[c], sts[c]) for c in cs_]
        dst_in = [_mm_tn(dr[c], qk[c][0] * qw) for c in cs_]
        dv = [_mm_tn(a_s[c] * dec, do_s[c]) for c in cs_]
        dqr, dkr = [], []
        for c in cs_:
            da = da_s[c] * dec
            dqr.append(_fold_heads(_mm(da, qk[c][1]), mh, L) + dq_st[c] * qw)
            dkr.append(_mm_tn(da, qs[c]))
        dst_next = dst_sc[...]
        for c in reversed(cs_):
            g = jnp.where(bd, dst_next, 0.0)
            dv[c] = dv[c] + _mm_nt(qk[c][1] * kw, g)
            dkr[c] = dkr[c] + _mm(vs[c], g) * kw
            dst_next = dst_next * a_ref[...] + jnp.where(bd, dst_in[c], 0.0)
        dst_sc[...] = dst_next
        for c in cs_:
            cs, sn = tabs[c]
            dk = dkr[c] * RET_KSCALE
            dq1, dq2 = dqr[c][:, 0:128], dqr[c][:, 128:256]
            dk1, dk2 = dk[:, 0:128], dk[:, 128:256]
            dp_ref[0, rows[c], :] = jnp.concatenate(
                [dq1 * cs + dq2 * sn, dq2 * cs - dq1 * sn, dk1 * cs + dk2 * sn, dk2 * cs - dk1 * sn, dv[c], dz[c]],
                axis=1).astype(_MXU)

    tok = lambda w: pl.BlockSpec((1, G * L, w), lambda b, i: (b, NG - 1 - i, 0))
    return pl.pallas_call(
        body, name="ret_bwd", grid=(B, NG),
        in_specs=[tok(RET_W), tok(128), tok(128), tok(256),
                  pl.BlockSpec((1, G, 256, 256), lambda b, i: (b, NG - 1 - i, 0, 0)), tok(256),
                  _full((4 * L, L)), _full((L, 256)), _full((L, 256)), _full((1, 256))],
        out_specs=tok(RET_W), out_shape=jax.ShapeDtypeStruct((B, S, RET_W), _MXU),
        scratch_shapes=[pltpu.VMEM((256, 256), F32)],
        compiler_params=_cp(("parallel", "arbitrary")),
    )(ret_p, cos, sin, raw, states, d_mix, *consts)


def _gla_masks():
    C = CHUNK
    lk = lax.broadcasted_iota(jnp.int32, (1, 128), 1)
    lv = lax.broadcasted_iota(jnp.int32, (1, 256), 1)
    mk = [(lk // 32) == h for h in range(4)]
    mv = [(lv // 64) == h for h in range(4)]
    vi = lax.broadcasted_iota(jnp.int32, (256, 128), 0)
    ki = lax.broadcasted_iota(jnp.int32, (256, 128), 1)
    bd = (vi // 64) == (ki // 32)
    ri = lax.broadcasted_iota(jnp.int32, (4 * C, C), 0) % C
    cj = lax.broadcasted_iota(jnp.int32, (4 * C, C), 1)
    lower = ri >= cj
    ti = lax.broadcasted_iota(jnp.int32, (C, C), 0)
    tj = lax.broadcasted_iota(jnp.int32, (C, C), 1)
    ltri = jnp.where(ti >= tj, 1.0, 0.0).astype(F32)
    utri = jnp.where(ti <= tj, 1.0, 0.0).astype(F32)
    return mk, mv, bd, lower, ltri, utri


def _log_sigmoid(x):
    return jnp.minimum(x, 0.0) - jnp.log(1.0 + jnp.exp(-jnp.abs(x)))


GLA_G = 8


def _gla_fwd(gla_p, w_g2p, b_g2, gnw):
    B, S, _ = gla_p.shape
    C = CHUNK
    NC = S // C
    G = min(GLA_G, NC)
    NG = NC // G

    def body(p_ref, w_ref, b_ref, gn_ref, out_ref, raw_ref, st_ref, st_sc):
        @pl.when(pl.program_id(1) == 0)
        def _():
            st_sc[...] = jnp.zeros_like(st_sc)

        mk, mv, bd, lower, ltri, _ = _gla_masks()
        cs = range(G)
        rows = [slice(c * C, (c + 1) * C) for c in cs]
        ps = [p_ref[0, rows[c], :].astype(F32) for c in cs]
        pre = [_mm(ps[c][:, 512:640], w_ref[...]) + b_ref[...] for c in cs]
        cum = [_mm_f32(ltri, _log_sigmoid(pre[c]) * (1.0 / GLA_TAU)) for c in cs]
        past, fut, upd, q_pos, a_row = [], [], [], [], []
        for c in cs:
            q = ps[c][:, 0:128]
            k = ps[c][:, 128:256] * GLA_KSCALE
            last = cum[c][C - 1:C, :]
            e_pos = jnp.exp(cum[c])
            e_neg = jnp.exp(-cum[c])
            q_pos.append(q * e_pos)
            a_row.append(jnp.exp(last))
            past.append(_mm_nt(_stack_heads(q_pos[c], mk), k * e_neg))
            fut.append(_mm_nt(_stack_heads(q * e_neg, mk), k * e_pos))
            upd.append(_mm_tn(ps[c][:, 256:512], k * jnp.exp(last - cum[c])))
        o_s = [_mm(jnp.where(lower, past[c], fut[c]), ps[c][:, 256:512]) for c in cs]
        st = st_sc[...]
        inter = []
        for c in cs:
            st_ref[0, c] = st
            inter.append(_mm_nt(q_pos[c], st))
            st = st * a_row[c] + jnp.where(bd, upd[c], 0.0)
        st_sc[...] = st
        for c in cs:
            g = _fold_heads(o_s[c], mv, C) + inter[c]
            raw_ref[0, rows[c], :] = g
            rstd = lax.rsqrt(_head_mean(g * g, mv, 64.0) + EPS)
            out_ref[0, rows[c], :] = (g * rstd * gn_ref[...] * _silu(ps[c][:, 640:896])).astype(_MXU)

    tok = lambda w: pl.BlockSpec((1, G * C, w), lambda b, n: (b, n, 0))
    return pl.pallas_call(
        body, name="gla_fwd", grid=(B, NG),
        in_specs=[tok(GLA_W), _full((128, 128)), _full((1, 128)), _full((1, 256))],
        out_specs=[tok(256), tok(256), pl.BlockSpec((1, G, 256, 128), lambda b, n: (b, n, 0, 0))],
        out_shape=[jax.ShapeDtypeStruct((B, S, 256), _MXU), jax.ShapeDtypeStruct((B, S, 256), F32),
                   jax.ShapeDtypeStruct((B, NC, 256, 128), F32)],
        scratch_shapes=[pltpu.VMEM((256, 128), F32)],
        compiler_params=_cp(("parallel", "arbitrary")),
    )(gla_p, w_g2p, b_g2, gnw)


def _gla_bwd(gla_p, w_g2p, b_g2, gnw, raw, states, d_mix):
    B, S, _ = gla_p.shape
    C = CHUNK
    NC = S // C
    G = min(GLA_G, NC)
    NG = NC // G

    def body(p_ref, w_ref, b_ref, gn_ref, raw_ref, st_ref, dm_ref, dp_ref, dw_ref, db_ref, dgn_ref, dst_sc):
        first = (pl.program_id(0) == 0) & (pl.program_id(1) == 0)

        @pl.when(first)
        def _():
            dw_ref[...] = jnp.zeros_like(dw_ref)
            db_ref[...] = jnp.zeros_like(db_ref)
            dgn_ref[...] = jnp.zeros_like(dgn_ref)

        @pl.when(pl.program_id(1) == 0)
        def _():
            dst_sc[...] = jnp.zeros_like(dst_sc)

        mk, mv, bd, lower, ltri, utri = _gla_masks()
        gn = gn_ref[...]
        cs = range(G)
        rows = [slice(c * C, (c + 1) * C) for c in cs]
        ps = [p_ref[0, rows[c], :].astype(F32) for c in cs]
        vs = [ps[c][:, 256:512] for c in cs]
        pre = [_mm(ps[c][:, 512:640], w_ref[...]) + b_ref[...] for c in cs]
        cum = [_mm_f32(ltri, _log_sigmoid(pre[c]) * (1.0 / GLA_TAU)) for c in cs]
        dg, dz, dgn_acc = [], [], jnp.zeros((1, 256), F32)
        for c in cs:
            g = raw_ref[0, rows[c], :]
            z = ps[c][:, 640:896]
            rstd = lax.rsqrt(_head_mean(g * g, mv, 64.0) + EPS)
            gh = g * rstd
            dm = dm_ref[0, rows[c], :]
            d_gn = dm * _silu(z)
            dz.append(dm * gh * gn * _dsilu(z))
            dgn_acc = dgn_acc + jnp.sum(d_gn * gh, axis=0, keepdims=True)
            d_gh = d_gn * gn
            dg.append(rstd * (d_gh - gh * _head_mean(d_gh * gh, mv, 64.0)))
        do_s = [_stack_heads(dg[c], mv) for c in cs]
        dattn = [_mm_nt(do_s[c], vs[c]) for c in cs]
        ks, e_pos, e_neg, q_pos, q_neg, k_pos, k_neg, qp_s, qn_s, past, fut, a_row, w_dec, kd = ([] for _ in range(14))
        for c in cs:
            q = ps[c][:, 0:128]
            k = ps[c][:, 128:256] * GLA_KSCALE
            last = cum[c][C - 1:C, :]
            ep, en = jnp.exp(cum[c]), jnp.exp(-cum[c])
            ks.append(k), e_pos.append(ep), e_neg.append(en)
            q_pos.append(q * ep), q_neg.append(q * en), k_pos.append(k * ep), k_neg.append(k * en)
            qp_s.append(_stack_heads(q_pos[c], mk)), qn_s.append(_stack_heads(q_neg[c], mk))
            past.append(_mm_nt(qp_s[c], k_neg[c]))
            fut.append(_mm_nt(qn_s[c], k_pos[c]))
            a_row.append(jnp.exp(last))
            w_dec.append(jnp.exp(last - cum[c]))
            kd.append(k * w_dec[c])
        sts = [st_ref[0, c] for c in cs]
        dq_st = [_mm(dg[c], sts[c]) for c in cs]
        dst_in = [_mm_tn(dg[c], q_pos[c]) for c in cs]
        dv, dq_pos, dk_neg, dq_neg, dk_pos = [], [], [], [], []
        for c in cs:
            attn = jnp.where(lower, past[c], fut[c])
            dpast = jnp.where(lower, dattn[c], 0.0)
            dfut = jnp.where(lower, 0.0, dattn[c])
            dv.append(_mm_tn(attn, do_s[c]))
            dq_pos.append(_fold_heads(_mm(dpast, k_neg[c]), mk, C) + dq_st[c])
            dk_neg.append(_mm_tn(dpast, qp_s[c]))
            dq_neg.append(_fold_heads(_mm(dfut, k_pos[c]), mk, C))
            dk_pos.append(_mm_tn(dfut, qn_s[c]))
        dst_next = dst_sc[...]
        d_a, d_kd = [None] * G, [None] * G
        for c in reversed(cs):
            d_a[c] = jnp.sum(dst_next * sts[c], axis=0, keepdims=True)
            gmat = jnp.where(bd, dst_next, 0.0)
            d_kd[c] = _mm(vs[c], gmat)
            dv[c] = dv[c] + _mm_nt(kd[c], gmat)
            dst_next = dst_next * a_row[c] + jnp.where(bd, dst_in[c], 0.0)
        dst_sc[...] = dst_next
        row = lax.broadcasted_iota(jnp.int32, (C, 128), 0)
        d_la, dk, dq = [], [], []
        for c in cs:
            t = d_kd[c] * kd[c]
            dk.append(d_kd[c] * w_dec[c] + dk_neg[c] * e_neg[c] + dk_pos[c] * e_pos[c])
            dq.append(dq_pos[c] * e_pos[c] + dq_neg[c] * e_neg[c])
            d_last = jnp.sum(t, axis=0, keepdims=True) + d_a[c] * a_row[c]
            d_cum = (dq_pos[c] * q_pos[c] - dk_neg[c] * k_neg[c] - dq_neg[c] * q_neg[c] + dk_pos[c] * k_pos[c] - t)
            d_la.append(_mm_f32(utri, d_cum + jnp.where(row == C - 1, d_last, 0.0)))
        d_pre = [d_la[c] * _sig(-pre[c]) * (1.0 / GLA_TAU) for c in cs]
        d_gg = [_mm_nt(d_pre[c], w_ref[...]) for c in cs]
        dw_acc = _mm_tn(ps[0][:, 512:640], d_pre[0])
        db_acc = jnp.sum(d_pre[0], axis=0, keepdims=True)
        for c in cs[1:]:
            dw_acc = dw_acc + _mm_tn(ps[c][:, 512:640], d_pre[c])
            db_acc = db_acc + jnp.sum(d_pre[c], axis=0, keepdims=True)
        for c in cs:
            dp_ref[0, rows[c], :] = jnp.concatenate([dq[c], dk[c] * GLA_KSCALE, dv[c], d_gg[c], dz[c]],
                                                    axis=1).astype(_MXU)
        dw_ref[...] += dw_acc
        db_ref[...] += db_acc
        dgn_ref[...] += dgn_acc

        @pl.when((pl.program_id(0) == B - 1) & (pl.program_id(1) == NG - 1))
        def _():
            s1 = dgn_ref[...]
            s1 = s1 + pltpu.roll(s1, 128, 1)
            dgn_ref[...] = s1 + pltpu.roll(s1, 64, 1)

    tok = lambda w: pl.BlockSpec((1, G * C, w), lambda b, i: (b, NG - 1 - i, 0))
    return pl.pallas_call(
        body, name="gla_bwd", grid=(B, NG),
        in_specs=[tok(GLA_W), _full((128, 128)), _full((1, 128)), _full((1, 256)), tok(256),
                  pl.BlockSpec((1, G, 256, 128), lambda b, i: (b, NG - 1 - i, 0, 0)), tok(256)],
        out_specs=[tok(GLA_W), _full((128, 128)), _full((1, 128)), _full((1, 256))],
        out_shape=[jax.ShapeDtypeStruct((B, S, GLA_W), _MXU), jax.ShapeDtypeStruct((128, 128), F32),
                   jax.ShapeDtypeStruct((1, 128), F32), jax.ShapeDtypeStruct((1, 256), F32)],
        scratch_shapes=[pltpu.VMEM((256, 128), F32)],
        compiler_params=_cp(("arbitrary", "arbitrary")),
    )(gla_p, w_g2p, b_g2, gnw, raw, states, d_mix)


def _rms(x, w):
    rstd = lax.rsqrt(jnp.mean(x * x, axis=-1, keepdims=True) + EPS)
    xh = x * rstd
    return xh, rstd, xh * w


def _rms_bwd(dy, xh, rstd, w):
    dxh = dy * w
    return rstd * (dxh - xh * jnp.mean(dxh * xh, axis=-1, keepdims=True))


MLA_T = 256


def _mla_prep_fwd(mla_p, cos, sin, qnw, kvnw, w_uq, w_ukv):
    B, S, _ = mla_p.shape
    tm = min(S, 512)

    t = min(MLA_T, S)
    nt = tm // t

    def body(p_ref, c_ref, s_ref, qn_ref, kn_ref, wq_ref, wkv_ref, wkvt_ref, q_ref, k_ref, v_ref, kt_ref, vt_ref):
        p = p_ref[0].astype(F32)
        cs, sn = c_ref[0], s_ref[0]
        _, _, qn = _rms(p[:, 0:256], qn_ref[...])
        qpre = _mm(qn, wq_ref[...])
        _, _, kvn = _rms(p[:, 256:384], kn_ref[...])
        kv = _mm(kvn, wkv_ref[...])
        kvt = _mm_nt(wkvt_ref[...], kvn)
        kpe = _rope128(p[:, 384:512], cs, sn)
        kpet = kpe.T
        for h in range(8):
            sl = slice(128 * h, 128 * h + 128)
            q_ref[0, :, sl] = _rope128(qpre[:, sl], cs, sn).astype(_MXU)
            k_ref[0, :, sl] = (kv[:, sl] + kpe).astype(_MXU)
            kht = kvt[sl, :] + kpet
            for n in range(nt):
                kt_ref[0, n, sl, :] = kht[:, n * t:(n + 1) * t].astype(_MXU)
        v_ref[0] = kv[:, 1024:1536].astype(_MXU)
        for n in range(nt):
            vt_ref[0, n] = kvt[1024:1536, n * t:(n + 1) * t].astype(_MXU)

    tok = lambda w: pl.BlockSpec((1, tm, w), lambda b, i: (b, i, 0))
    tr = lambda w: pl.BlockSpec((1, nt, w, t), lambda b, i: (b, i, 0, 0))
    return pl.pallas_call(
        body, name="mla_prep_fwd", grid=(B, S // tm),
        in_specs=[tok(512), tok(128), tok(128), _full((1, 256)), _full((1, 128)), _full((256, 1024)),
                  _full((128, 1536)), _full((1536, 128))],
        out_specs=[tok(1024), tok(1024), tok(512), tr(1024), tr(512)],
        out_shape=[jax.ShapeDtypeStruct((B, S, 1024), _MXU), jax.ShapeDtypeStruct((B, S, 1024), _MXU),
                   jax.ShapeDtypeStruct((B, S, 512), _MXU), jax.ShapeDtypeStruct((B, S // t, 1024, t), _MXU),
                   jax.ShapeDtypeStruct((B, S // t, 512, t), _MXU)],
        compiler_params=_cp(("parallel", "parallel")),
    )(mla_p, cos, sin, qnw, kvnw, w_uq, w_ukv, w_ukv.T)


def _chunk_mask_t(t):
    kj = lax.broadcasted_iota(jnp.int32, (t, t), 0) // CHUNK
    qi = lax.broadcasted_iota(jnp.int32, (t, t), 1) // CHUNK
    return kj <= qi


MLA_HG = 8
MLA_HG_FWD = 8
LOG2E = 1.4426950408889634
MLA_C2 = MLA_SCALE * LOG2E


def _mla_attn_fwd(q, k, vt):
    B, S, _ = q.shape
    t = min(MLA_T, S)
    nq = S // t
    HG = MLA_HG_FWD
    NP = HG // 2

    def body(q_ref, k_ref, vt_ref, o_ref, lse_ref, sa, sb, m_sc, l_sc, acc_sc):
        i = pl.program_id(2)
        row = lax.broadcasted_iota(jnp.int32, (128, 1), 0)
        low = row < 64
        mask = _chunk_mask_t(t)
        m_sc[...] = jnp.full(m_sc.shape, -jnp.inf, F32)
        l_sc[...] = jnp.zeros_like(l_sc)
        acc_sc[...] = jnp.zeros_like(acc_sc)

        ones = jnp.ones((8, t), _MXU)

        def scores(j, buf):
            kb = k_ref[0, pl.ds(pl.multiple_of(j * t, t), t), :]
            for h in range(HG):
                cols = slice(128 * h, 128 * h + 128)
                buf[h] = (_mm_nt(kb[:, cols], q_ref[0, :, cols]) * MLA_C2).astype(_MXU)

        def absorb(j, buf, masked):
            vtb = vt_ref[0, j]
            for pr in range(NP):
                alphas, pvs = [], []
                for hh in range(2):
                    h = 2 * pr + hh
                    s = buf[h]
                    if masked:
                        s = jnp.where(mask, s, jnp.full_like(s, -jnp.inf))
                    m_old = m_sc[h]
                    m_new = jnp.maximum(m_old, jnp.max(s, axis=0, keepdims=True).astype(F32))
                    alpha = jnp.exp2(m_old - m_new)
                    p = jnp.exp2(s - m_new.astype(_MXU))
                    l_sc[h] = alpha * l_sc[h] + _mm(ones, p)[0:1, :]
                    m_sc[h] = m_new
                    vth = vtb[128 * pr:128 * pr + 128, :]
                    vth = jnp.where(low if hh == 0 else ~low, vth, jnp.zeros_like(vth))
                    pvs.append(_mm(vth, p))
                    alphas.append(alpha)
                acc_sc[pr] = acc_sc[pr] * jnp.where(low, alphas[0], alphas[1]) + pvs[0] + pvs[1]

        scores(0, sb)

        def pair(jj, carry):
            j0 = 2 * jj
            scores(j0 + 1, sa)
            absorb(j0, sb, False)
            scores(j0 + 2, sb)
            absorb(j0 + 1, sa, False)
            return carry

        lax.fori_loop(0, i // 2, pair, 0)

        @pl.when(i % 2 == 1)
        def _():
            scores(i, sa)
            absorb(i - 1, sb, False)
            absorb(i, sa, True)

        @pl.when(i % 2 == 0)
        def _():
            absorb(i, sb, True)

        for pr in range(NP):
            l_e, l_o = l_sc[2 * pr], l_sc[2 * pr + 1]
            o_ref[0, :, 128 * pr:128 * pr + 128] = (acc_sc[pr] / jnp.where(low, l_e, l_o)).T
            lse_ref[0, pr, 0, 0:1, :] = m_sc[2 * pr] + jnp.log(l_e) * LOG2E
            lse_ref[0, pr, 0, 1:2, :] = m_sc[2 * pr + 1] + jnp.log(l_o) * LOG2E

    return pl.pallas_call(
        body, name="mla_attn_fwd", grid=(B, 8 // HG, nq),
        in_specs=[pl.BlockSpec((1, t, 128 * HG), lambda b, g, i: (b, i, g)),
                  pl.BlockSpec((1, S, 128 * HG), lambda b, g, i: (b, 0, g)),
                  pl.BlockSpec((1, nq, 64 * HG, t), lambda b, g, i: (b, 0, g, 0))],
        out_specs=[pl.BlockSpec((1, t, 64 * HG), lambda b, g, i: (b, i, g)),
                   pl.BlockSpec((1, NP, 1, 2, t), lambda b, g, i: (b, g, i, 0, 0))],
        out_shape=[jax.ShapeDtypeStruct((B, S, 512), F32), jax.ShapeDtypeStruct((B, 4, nq, 2, t), F32)],
        scratch_shapes=[pltpu.VMEM((HG, t, t), _MXU), pltpu.VMEM((HG, t, t), _MXU), pltpu.VMEM((HG, 1, t), F32),
                        pltpu.VMEM((HG, 1, t), F32), pltpu.VMEM((NP, 128, t), F32)],
        compiler_params=_cp(("parallel", "parallel", "arbitrary")),
    )(q, k, vt)


def _mla_attn_bwd(q, k, v, kt, do, lse, dl):
    B, S, _ = q.shape
    t = min(MLA_T, S)
    nk = S // t

    HG = MLA_HG
    NP = HG // 2

    def body(q_ref, k_ref, v_ref, kt_ref, do_ref, lse_ref, dl_ref, dq_ref, dk_ref, dv_ref,
             sa, da, sb, db, dqt_sc, dk_sc, dv_sc):
        j = pl.program_id(2)

        @pl.when(j == 0)
        def _():
            dqt_sc[...] = jnp.zeros_like(dqt_sc)

        dk_sc[...] = jnp.zeros_like(dk_sc)
        dv_sc[...] = jnp.zeros_like(dv_sc)
        lane = lax.broadcasted_iota(jnp.int32, (1, 128), 1)
        low = lane < 64
        mask = _chunk_mask_t(t)

        def half(x, hh):
            return jnp.where(low if hh == 0 else ~low, x, jnp.zeros_like(x))

        def prepare(i, sbuf, dbuf):
            rows = pl.ds(pl.multiple_of(i * t, t), t)
            for h in range(HG):
                cols = slice(128 * h, 128 * h + 128)
                pc = slice(128 * (h // 2), 128 * (h // 2) + 128)
                sbuf[h] = _mm_nt(k_ref[0, :, cols], q_ref[0, rows, cols]) * MLA_C2
                dbuf[h] = _mm_nt(half(v_ref[0, :, pc], h % 2), do_ref[0, rows, pc])

        def absorb(i, sbuf, dbuf, masked):
            rows = pl.ds(pl.multiple_of(i * t, t), t)
            for h in range(HG):
                pr, hh = h // 2, h % 2
                cols = slice(128 * h, 128 * h + 128)
                pc = slice(128 * pr, 128 * pr + 128)
                p = jnp.exp2(sbuf[h] - lse_ref[0, pr, i][hh:hh + 1, :])
                if masked:
                    p = jnp.where(mask, p, 0.0)
                dv_sc[pr] += _mm(p, half(do_ref[0, rows, pc], hh))
                ds = p * (dbuf[h] - dl_ref[0, pr, i][hh:hh + 1, :])
                dqt_sc[i, cols, :] += _mm(kt_ref[0, 0, cols, :], ds)
                dk_sc[h] += _mm(ds, q_ref[0, rows, cols])

        n = nk - 1 - j
        prepare(jnp.minimum(j + 1, nk - 1), sb, db)

        def pair(jj, carry):
            i0 = j + 1 + 2 * jj
            prepare(i0 + 1, sa, da)
            absorb(i0, sb, db, False)
            prepare(jnp.where(i0 + 2 <= nk - 1, i0 + 2, j), sb, db)
            absorb(i0 + 1, sa, da, False)
            return carry

        lax.fori_loop(0, n // 2, pair, 0)

        @pl.when(n % 2 == 1)
        def _():
            prepare(j, sa, da)
            absorb(nk - 1, sb, db, False)
            absorb(j, sa, da, True)

        @pl.when(n % 2 == 0)
        def _():
            absorb(j, sb, db, True)

        for h in range(HG):
            dk_ref[0, :, 128 * h:128 * h + 128] = (dk_sc[h] * MLA_SCALE).astype(_MXU)
        for pr in range(NP):
            dv_ref[0, :, 128 * pr:128 * pr + 128] = dv_sc[pr].astype(_MXU)

        @pl.when(j == nk - 1)
        def _():
            for i in range(nk):
                dq_ref[0, i * t:(i + 1) * t, :] = (dqt_sc[i].T * MLA_SCALE).astype(_MXU)

    seq = lambda w: pl.BlockSpec((1, S, w), lambda b, g, j: (b, 0, g))
    blk = lambda w: pl.BlockSpec((1, t, w), lambda b, g, j: (b, j, g))
    stat = pl.BlockSpec((1, NP, nk, 2, t), lambda b, g, j: (b, g, 0, 0, 0))
    return pl.pallas_call(
        body, name="mla_attn_bwd", grid=(B, 8 // HG, nk),
        in_specs=[seq(128 * HG), blk(128 * HG), blk(64 * HG),
                  pl.BlockSpec((1, 1, 128 * HG, t), lambda b, g, j: (b, j, g, 0)), seq(64 * HG), stat, stat],
        out_specs=[seq(128 * HG), blk(128 * HG), blk(64 * HG)],
        out_shape=[jax.ShapeDtypeStruct((B, S, 1024), _MXU), jax.ShapeDtypeStruct((B, S, 1024), _MXU),
                   jax.ShapeDtypeStruct((B, S, 512), _MXU)],
        scratch_shapes=[pltpu.VMEM((HG, t, t), F32), pltpu.VMEM((HG, t, t), F32), pltpu.VMEM((HG, t, t), F32),
                        pltpu.VMEM((HG, t, t), F32), pltpu.VMEM((nk, 128 * HG, t), F32),
                        pltpu.VMEM((HG, t, 128), F32), pltpu.VMEM((NP, t, 128), F32)],
        compiler_params=_cp(("parallel", "parallel", "arbitrary"), 56),
    )(q, k, v, kt, do, lse, dl)


def _mla_prep_bwd(mla_p, cos, sin, qnw, kvnw, w_uq, w_ukv, dq, dk, dv):
    B, S, _ = mla_p.shape
    tm = min(S, 512)

    def body(p_ref, c_ref, s_ref, qn_ref, kn_ref, wq_ref, wkv_ref, dq_ref, dk_ref, dv_ref,
             dp_ref, dwq_ref, dwkv_ref, dqn_ref, dkn_ref):
        first = (pl.program_id(0) == 0) & (pl.program_id(1) == 0)

        @pl.when(first)
        def _():
            dwq_ref[...] = jnp.zeros_like(dwq_ref)
            dwkv_ref[...] = jnp.zeros_like(dwkv_ref)
            dqn_ref[...] = jnp.zeros_like(dqn_ref)
            dkn_ref[...] = jnp.zeros_like(dkn_ref)

        p = p_ref[0].astype(F32)
        cs, sn = c_ref[0], s_ref[0]
        lane = lax.broadcasted_iota(jnp.int32, (1, 128), 1)
        pe = (lane >= 64) & (lane < 96)
        qh, q_rstd, qn = _rms(p[:, 0:256], qn_ref[...])
        kvh, kv_rstd, kvn = _rms(p[:, 256:384], kn_ref[...])
        dqv = dq_ref[0].astype(F32)
        dkv = dk_ref[0].astype(F32)
        dqpre = jnp.concatenate(
            [_rope128_t(dqv[:, 128 * h:128 * h + 128], cs, sn) for h in range(8)], axis=1)
        dkpe = jnp.zeros((tm, 128), F32)
        for h in range(8):
            dkpe = dkpe + jnp.where(pe, dkv[:, 128 * h:128 * h + 128], 0.0)
        dkr = _rope128_t(dkpe, cs, sn)
        dkv_all = jnp.concatenate([dkv, dv_ref[0].astype(F32)], axis=1)
        d_qn = _mm_nt(dqpre, wq_ref[...])
        d_kvn = _mm_nt(dkv_all, wkv_ref[...])
        dwq_ref[...] += _mm_tn(qn, dqpre)
        dwkv_ref[...] += _mm_tn(kvn, dkv_all)
        dqn_ref[...] += jnp.sum(d_qn * qh, axis=0, keepdims=True)
        dkn_ref[...] += jnp.sum(d_kvn * kvh, axis=0, keepdims=True)
        dp_ref[0] = jnp.concatenate([_rms_bwd(d_qn, qh, q_rstd, qn_ref[...]),
                                     _rms_bwd(d_kvn, kvh, kv_rstd, kn_ref[...]), dkr], axis=1).astype(_MXU)

    tok = lambda w: pl.BlockSpec((1, tm, w), lambda b, i: (b, i, 0))
    return pl.pallas_call(
        body, name="mla_prep_bwd", grid=(B, S // tm),
        in_specs=[tok(512), tok(128), tok(128), _full((1, 256)), _full((1, 128)), _full((256, 1024)),
                  _full((128, 1536)), tok(1024), tok(1024), tok(512)],
        out_specs=[tok(512), _full((256, 1024)), _full((128, 1536)), _full((1, 256)), _full((1, 128))],
        out_shape=[jax.ShapeDtypeStruct((B, S, 512), _MXU), jax.ShapeDtypeStruct((256, 1024), F32),
                   jax.ShapeDtypeStruct((128, 1536), F32), jax.ShapeDtypeStruct((1, 256), F32),
                   jax.ShapeDtypeStruct((1, 128), F32)],
        compiler_params=_cp(("arbitrary", "arbitrary")),
    )(mla_p, cos, sin, qnw, kvnw, w_uq, w_ukv, dq, dk, dv)


def _out_fwd(x, gate, r_g, o_mla, mla_p, g_g, w_out):
    B, S, D = x.shape
    tm = min(S, 512)

    def body(x_ref, g_ref, r_ref, o_ref, z_ref, gg_ref, w_ref, xn_ref, y_ref):
        mm = (o_ref[0] * _silu(z_ref[0].astype(F32))).astype(_MXU)
        y = (jnp.dot(r_ref[0], w_ref[0:256, :], preferred_element_type=F32)
             + jnp.dot(mm, w_ref[256:768, :], preferred_element_type=F32)
             + jnp.dot(gg_ref[0], w_ref[768:1024, :], preferred_element_type=F32))
        y_ref[0] = y.astype(_MXU)
        xn_ref[0] = x_ref[0] + g_ref[0] * y

    tok = lambda w, c=0: pl.BlockSpec((1, tm, w), lambda b, i: (b, i, c))
    return pl.pallas_call(
        body, name="out_fwd", grid=(B, S // tm),
        in_specs=[tok(D), pl.BlockSpec((1, 1, D), lambda b, i: (b, 0, 0)), tok(256), tok(512), tok(512, 1),
                  tok(256), _full((D, D))],
        out_specs=[tok(D), tok(D)],
        out_shape=[jax.ShapeDtypeStruct((B, S, D), F32), jax.ShapeDtypeStruct((B, S, D), _MXU)],
        compiler_params=_cp(("parallel", "parallel")),
    )(x, gate, r_g, o_mla, mla_p, g_g, w_out)


def _out_bwd(dx, y, gate, r_g, g_g, w_out, o_mla, mla_p):
    B, S, D = dx.shape
    tm = min(S, 512)
    t = min(MLA_T, S)
    nt = tm // t

    def body(dx_ref, y_ref, g_ref, r_ref, gg_ref, w_ref, o_ref, z_ref,
             dr_ref, do_ref, dz_ref, dl_ref, dg_ref, dw_ref, dgate_ref, acc):
        first = (pl.program_id(0) == 0) & (pl.program_id(1) == 0)

        @pl.when(first)
        def _():
            acc[...] = jnp.zeros_like(acc)

        @pl.when(pl.program_id(1) == 0)
        def _():
            dgate_ref[...] = jnp.zeros_like(dgate_ref)

        dxv = dx_ref[0]
        dgate_ref[0] += jnp.sum(dxv * y_ref[0].astype(F32), axis=0, keepdims=True)
        dy = (dxv * g_ref[0]).astype(_MXU)
        dr_ref[0] = _mm_nt(dy, w_ref[0:256, :])
        dg_ref[0] = _mm_nt(dy, w_ref[768:1024, :])
        ov, z = o_ref[0], z_ref[0].astype(F32)
        acc[0:256, :] += _mm_tn(r_ref[0], dy)
        acc[256:768, :] += _mm_tn((ov * _silu(z)).astype(_MXU), dy)
        acc[768:1024, :] += _mm_tn(gg_ref[0], dy)

        @pl.when((pl.program_id(0) == B - 1) & (pl.program_id(1) == S // tm - 1))
        def _():
            dw_ref[...] = acc[...].astype(_MXU)

        dm = _mm_nt(dy, w_ref[256:768, :])
        do = dm * _silu(z)
        dz_ref[0] = (dm * ov * _dsilu(z)).astype(_MXU)
        do_ref[0] = do.astype(_MXU)
        prod = do * ov
        for pr in range(4):
            pt = prod[:, 128 * pr:128 * pr + 128].T
            se = jnp.sum(pt[0:64], axis=0, keepdims=True)
            so = jnp.sum(pt[64:128], axis=0, keepdims=True)
            for n in range(nt):
                dl_ref[0, pr, n, 0:1, :] = se[:, n * t:(n + 1) * t]
                dl_ref[0, pr, n, 1:2, :] = so[:, n * t:(n + 1) * t]

    tok = lambda w, c=0: pl.BlockSpec((1, tm, w), lambda b, i: (b, i, c))
    per_seq = pl.BlockSpec((1, 1, D), lambda b, i: (b, 0, 0))
    return pl.pallas_call(
        body, name="out_bwd", grid=(B, S // tm),
        in_specs=[tok(D), tok(D), per_seq, tok(256), tok(256), _full((D, D)), tok(512), tok(512, 1)],
        out_specs=[tok(256), tok(512), tok(512), pl.BlockSpec((1, 4, nt, 2, t), lambda b, i: (b, 0, i, 0, 0)),
                   tok(256), _full((D, D)), per_seq],
        out_shape=[jax.ShapeDtypeStruct((B, S, 256), F32), jax.ShapeDtypeStruct((B, S, 512), _MXU),
                   jax.ShapeDtypeStruct((B, S, 512), _MXU), jax.ShapeDtypeStruct((B, 4, S // t, 2, t), F32),
                   jax.ShapeDtypeStruct((B, S, 256), F32), jax.ShapeDtypeStruct((D, D), _MXU),
                   jax.ShapeDtypeStruct((B, 1, D), F32)],
        scratch_shapes=[pltpu.VMEM((D, D), F32)],
        compiler_params=_cp(("arbitrary", "arbitrary")),
    )(dx, y, gate, r_g, g_g, w_out, o_mla, mla_p)


def _proj_bwd_x(x, shift, scale, nw, w_arr, d_ret, d_mla, d_mz, d_gla, dx_out):
    B, S, D = x.shape
    tm = min(S, 512)

    def body(x_ref, sc_ref, nw_ref, w_ref, dr_ref, dm_ref, dz_ref, dg_ref, dxo_ref,
             dx_ref, dsh_ref, dsc_ref, dnw_ref):
        first = (pl.program_id(0) == 0) & (pl.program_id(1) == 0)

        @pl.when(first)
        def _():
            dnw_ref[...] = jnp.zeros_like(dnw_ref)

        @pl.when(pl.program_id(1) == 0)
        def _():
            dsh_ref[...] = jnp.zeros_like(dsh_ref)
            dsc_ref[...] = jnp.zeros_like(dsc_ref)

        dp = jnp.concatenate([dr_ref[0], dm_ref[0], dz_ref[0], dg_ref[0]], axis=1)
        dh = lax.dot_general(dp, w_ref[...], (((1,), (1,)), ((), ())), preferred_element_type=F32)
        xv = x_ref[0]
        rstd = lax.rsqrt(jnp.mean(xv * xv, axis=-1, keepdims=True) + EPS)
        xh = xv * rstd
        nwv = nw_ref[...]
        mod = 1.0 + sc_ref[0]
        dsh_ref[0] += jnp.sum(dh, axis=0, keepdims=True)
        dsc_ref[0] += jnp.sum(dh * xh * nwv, axis=0, keepdims=True)
        dnw_ref[...] += jnp.sum(dh * xh * mod, axis=0, keepdims=True)
        dxh = dh * nwv * mod
        dx_ref[0] = dxo_ref[0] + rstd * (dxh - xh * jnp.mean(dxh * xh, axis=-1, keepdims=True))

    tok = lambda w: pl.BlockSpec((1, tm, w), lambda b, i: (b, i, 0))
    per_seq = pl.BlockSpec((1, 1, D), lambda b, i: (b, 0, 0))
    return pl.pallas_call(
        body, name="proj_bwd_x", grid=(B, S // tm),
        in_specs=[tok(D), per_seq, _full((1, D)), _full((D, ARR_W)), tok(RET_W), tok(512), tok(512),
                  tok(GLA_W), tok(D)],
        out_specs=[tok(D), per_seq, per_seq, _full((1, D))],
        out_shape=[jax.ShapeDtypeStruct((B, S, D), F32), jax.ShapeDtypeStruct((B, 1, D), F32),
                   jax.ShapeDtypeStruct((B, 1, D), F32), jax.ShapeDtypeStruct((1, D), F32)],
        compiler_params=_cp(("arbitrary", "arbitrary")),
    )(x, scale, nw, w_arr, d_ret, d_mla, d_mz, d_gla, dx_out)


def _proj_bwd_w(h, d_ret, d_mla, d_mz, d_gla):
    B, S, D = h.shape
    tm = min(S, 512)

    def body(h_ref, dr_ref, dm_ref, dz_ref, dg_ref, dw_ref, acc):
        first = (pl.program_id(0) == 0) & (pl.program_id(1) == 0)

        @pl.when(first)
        def _():
            acc[...] = jnp.zeros_like(acc)

        hv = h_ref[0]
        tn = lambda d_ref: lax.dot_general(hv, d_ref[0], (((0,), (0,)), ((), ())), preferred_element_type=F32)
        acc[:, 0:RET_W] += tn(dr_ref)
        acc[:, RET_W:RET_W + 512] += tn(dm_ref)
        acc[:, RET_W + 512:RET_W + MLA_W] += tn(dz_ref)
        acc[:, RET_W + MLA_W:ARR_W] += tn(dg_ref)

        @pl.when((pl.program_id(0) == B - 1) & (pl.program_id(1) == S // tm - 1))
        def _():
            dw_ref[...] = acc[...].astype(_MXU)

    tok = lambda w: pl.BlockSpec((1, tm, w), lambda b, i: (b, i, 0))
    return pl.pallas_call(
        body, name="proj_bwd_w", grid=(B, S // tm),
        in_specs=[tok(D), tok(RET_W), tok(512), tok(512), tok(GLA_W)],
        out_specs=_full((D, ARR_W)), out_shape=jax.ShapeDtypeStruct((D, ARR_W), _MXU),
        scratch_shapes=[pltpu.VMEM((D, ARR_W), F32)],
        compiler_params=_cp(("arbitrary", "arbitrary"), 56),
    )(h, d_ret, d_mla, d_mz, d_gla)


def _out_fwd_loss(x, gate, r_g, o_mla, mla_p, g_g, w_out, fw, target):
    B, S, D = x.shape
    tm = min(S, 512)

    def body(x_ref, g_ref, r_ref, o_ref, z_ref, gg_ref, w_ref, fw_ref, t_ref, dx_ref, y_ref, loss_ref, dfw_ref):
        first = (pl.program_id(0) == 0) & (pl.program_id(1) == 0)

        @pl.when(first)
        def _():
            loss_ref[...] = jnp.zeros_like(loss_ref)
            dfw_ref[...] = jnp.zeros_like(dfw_ref)

        mm = (o_ref[0] * _silu(z_ref[0].astype(F32))).astype(_MXU)
        y = (jnp.dot(r_ref[0], w_ref[0:256, :], preferred_element_type=F32)
             + jnp.dot(mm, w_ref[256:768, :], preferred_element_type=F32)
             + jnp.dot(gg_ref[0], w_ref[768:1024, :], preferred_element_type=F32))
        y_ref[0] = y.astype(_MXU)
        xv = x_ref[0] + g_ref[0] * y
        fwv = fw_ref[...]
        rstd = lax.rsqrt(jnp.mean(xv * xv, axis=-1, keepdims=True) + EPS)
        xh = xv * rstd
        err = xh * fwv - t_ref[0]
        loss_ref[...] += 0.5 * jnp.sum(jnp.mean(err * err, axis=-1, keepdims=True), axis=0, keepdims=True)
        dy = err * (1.0 / D)
        dfw_ref[...] += jnp.sum(dy * xh, axis=0, keepdims=True)
        dxh = dy * fwv
        dx_ref[0] = rstd * (dxh - xh * jnp.mean(dxh * xh, axis=-1, keepdims=True))

    tok = lambda w, c=0: pl.BlockSpec((1, tm, w), lambda b, i: (b, i, c))
    return pl.pallas_call(
        body, name="out_fwd_loss", grid=(B, S // tm),
        in_specs=[tok(D), pl.BlockSpec((1, 1, D), lambda b, i: (b, 0, 0)), tok(256), tok(512), tok(512, 1),
                  tok(256), _full((D, D)), _full((1, D)), tok(D)],
        out_specs=[tok(D), tok(D), _full((1, 1)), _full((1, D))],
        out_shape=[jax.ShapeDtypeStruct((B, S, D), F32), jax.ShapeDtypeStruct((B, S, D), _MXU),
                   jax.ShapeDtypeStruct((1, 1), F32), jax.ShapeDtypeStruct((1, D), F32)],
        compiler_params=_cp(("arbitrary", "arbitrary")),
    )(x, gate, r_g, o_mla, mla_p, g_g, w_out, fw, target)


def _local_step(x, pos3, mod, loss_target, small, w_in_a, w_uq_a, w_ukv_a, w_out_b):
    B, S, D = x.shape
    tabs = _rope_tables(pos3)
    saved = []
    for l in range(DEPTH):
        last = (small["final_norm"].reshape(1, D), loss_target) if l == DEPTH - 1 else None
        x, s = _layer_fwd(x, tabs, mod[l], {n: a[l] for n, a in small.items() if n != "final_norm"},
                          w_in_a[l], w_uq_a[l], w_ukv_a[l], w_out_b[l], loss_head=last)
        saved.append(s)
    dx, loss, d_fw = x
    grads = dict(final_norm=d_fw.reshape(D))
    per_layer = [None] * DEPTH
    for l in reversed(range(DEPTH)):
        dx, per_layer[l] = _layer_bwd(dx, saved[l], tabs)
    for name in per_layer[0]:
        grads[name] = jnp.stack([per_layer[l][name] for l in range(DEPTH)])
    return loss, dx, grads


def _layer_fwd(x, tabs, mod_l, small_l, w_in_a, w_uq_a=None, w_ukv_a=None, w_out_b=None, late_weights=None,
               loss_head=None):
    B, S, D = x.shape
    cr, sr, cm, sm = tabs
    shift = mod_l[:, 0:D].reshape(B, 1, D)
    scale = mod_l[:, D:2 * D].reshape(B, 1, D)
    gate = mod_l[:, 2 * D:3 * D].reshape(B, 1, D)
    nw = small_l["norm_w"].reshape(1, D)
    qnw = small_l["mla_q_norm"].reshape(1, 256)
    kvnw = small_l["mla_kv_norm"].reshape(1, 128)
    w_g2p = jnp.pad(small_l["gla_w_g2"], ((0, 112), (0, 0)))
    b_g2 = small_l["gla_b_g2"].reshape(1, 128)
    gnw = jnp.tile(small_l["gla_norm"], 4).reshape(1, 256)
    ret_p, mla_p, gla_p, h = _proj_fwd(x, shift, scale, nw, w_in_a)
    r_g, r_raw, r_st = _ret_fwd(ret_p, cr, sr)
    if late_weights is not None:
        w_uq_a, w_ukv_a, w_out_b = late_weights(r_raw)
    q, k, v, kt, vt = _mla_prep_fwd(mla_p, cm, sm, qnw, kvnw, w_uq_a, w_ukv_a)
    o_mla, lse = _mla_attn_fwd(q, k, vt)
    g_g, g_raw, g_st = _gla_fwd(gla_p, w_g2p, b_g2, gnw)
    if loss_head is None:
        x_new, y = _out_fwd(x, gate, r_g, o_mla, mla_p, g_g, w_out_b)
    else:
        dx, y, loss, d_fw = _out_fwd_loss(x, gate, r_g, o_mla, mla_p, g_g, w_out_b, *loss_head)
        x_new = (dx, loss, d_fw)
    saved = dict(x=x, shift=shift, scale=scale, gate=gate, nw=nw, qnw=qnw, kvnw=kvnw, w_g2p=w_g2p, b_g2=b_g2,
                 gnw=gnw, ret_p=ret_p, mla_p=mla_p, gla_p=gla_p, h=h, r_g=r_g, r_raw=r_raw, r_st=r_st, q=q, k=k,
                 v=v, kt=kt, o_mla=o_mla, lse=lse, g_g=g_g, g_raw=g_raw, g_st=g_st, y=y,
                 w_in_a=w_in_a, w_uq_a=w_uq_a, w_ukv_a=w_ukv_a, w_out_b=w_out_b)
    return x_new, saved


def _layer_bwd(dx, s, tabs, early_grads=None):
    B, S, D = dx.shape
    cr, sr, cm, sm = tabs
    d_r, do, d_mz, dl, d_g, dw_out, d_gate = _out_bwd(dx, s["y"], s["gate"], s["r_g"], s["g_g"], s["w_out_b"],
                                                      s["o_mla"], s["mla_p"])
    d_ret = _ret_bwd(s["ret_p"], cr, sr, s["r_raw"], s["r_st"], d_r)
    dq, dk, dv = _mla_attn_bwd(s["q"], s["k"], s["v"], s["kt"], do, s["lse"], dl)
    d_mla, dw_uq, dw_ukv, d_qnw, d_kvnw = _mla_prep_bwd(
        s["mla_p"], cm, sm, s["qnw"], s["kvnw"], s["w_uq_a"], s["w_ukv_a"], dq, dk, dv)
    gnw = s["gnw"] if early_grads is None else s["gnw"] + early_grads(dw_out, dw_uq, dw_ukv)
    d_gla, dw_g2p, db_g2, d_gnw = _gla_bwd(s["gla_p"], s["w_g2p"], s["b_g2"], gnw, s["g_raw"], s["g_st"], d_g)
    dx, d_shift, d_scale, d_nw = _proj_bwd_x(s["x"], s["shift"], s["scale"], s["nw"], s["w_in_a"],
                                             d_ret, d_mla, d_mz, d_gla, dx)
    dw_in = _proj_bwd_w(s["h"], d_ret, d_mla, d_mz, d_gla)
    grads = dict(
        d_mod=jnp.concatenate([d_shift, d_scale, d_gate], axis=2).reshape(B, 3 * D),
        norm_w=d_nw.reshape(D), mla_q_norm=d_qnw.reshape(256), mla_kv_norm=d_kvnw.reshape(128),
        gla_w_g2=dw_g2p[0:16], gla_b_g2=db_g2.reshape(128), gla_norm256=d_gnw.reshape(256),
        w_in_a=dw_in, w_uq_a=dw_uq, w_ukv_a=dw_ukv, w_out=dw_out)
    return dx, grads


def _exchange(arrs, gather, name):
    n = len(arrs)
    out_shape = [jax.ShapeDtypeStruct(((N_DEV,) + a.shape) if g else a.shape, a.dtype)
                 for a, g in zip(arrs, gather)]

    def body(*refs):
        ins, outs = refs[:n], refs[n:2 * n]
        send_sems, recv_sems, local_sems = refs[2 * n:]
        ix, iy, ic = lax.axis_index("x"), lax.axis_index("y"), lax.axis_index("c")
        me = 4 * ix + 2 * iy + ic
        copies = []
        for a in range(n):
            mine = ins[a] if gather[a] else ins[a].at[me]
            loc = pltpu.make_async_copy(mine, outs[a].at[me], local_sems.at[a])
            loc.start()
            copies.append(loc)
            for d in range(1, N_DEV):
                px = 1 - ix if d & 4 else ix
                py = 1 - iy if d & 2 else iy
                pc = 1 - ic if d & 1 else ic
                src = ins[a] if gather[a] else ins[a].at[4 * px + 2 * py + pc]
                cp = pltpu.make_async_remote_copy(
                    src_ref=src, dst_ref=outs[a].at[me], send_sem=send_sems.at[a, d - 1],
                    recv_sem=recv_sems.at[a, d - 1], device_id=(px, py, pc), device_id_type=pl.DeviceIdType.MESH)
                cp.start()
                copies.append(cp)
        for cp in copies:
            cp.wait()

    any_spec = pl.BlockSpec(memory_space=pl.ANY)
    outs = pl.pallas_call(
        body, name=name, in_specs=[any_spec] * n, out_specs=[any_spec] * n, out_shape=out_shape,
        scratch_shapes=[pltpu.SemaphoreType.DMA((n, N_DEV - 1)), pltpu.SemaphoreType.DMA((n, N_DEV - 1)),
                        pltpu.SemaphoreType.DMA((n,))],
    )(*arrs)
    return list(outs)


def _peers(ix, iy, ic):
    out = []
    for d in range(1, N_DEV):
        px = 1 - ix if d & 4 else ix
        py = 1 - iy if d & 2 else iy
        pc = 1 - ic if d & 1 else ic
        out.append((d - 1, (px, py, pc), 4 * px + 2 * py + pc))
    return out


def _exchange_start(arrs, gather, name, after=None):
    n = len(arrs)
    lands = [lax.empty(((N_DEV,) + a.shape) if g else a.shape, a.dtype) for a, g in zip(arrs, gather)]
    extra = [] if after is None else [after]

    def body(*refs):
        ins, land_refs = refs[:n], refs[n:2 * n]
        send_sems, recv_sems = refs[2 * n + len(extra)], refs[2 * n + len(extra) + 1]
        token = refs[-1]
        ix, iy, ic = lax.axis_index("x"), lax.axis_index("y"), lax.axis_index("c")
        me = 4 * ix + 2 * iy + ic
        for a in range(n):
            for k, peer, peer_idx in _peers(ix, iy, ic):
                pltpu.make_async_remote_copy(
                    src_ref=ins[a] if gather[a] else ins[a].at[peer_idx], dst_ref=land_refs[a].at[me],
                    send_sem=send_sems.at[7 * a + k], recv_sem=recv_sems.at[7 * a + k], device_id=peer,
                    device_id_type=pl.DeviceIdType.MESH).start()
        token[...] = jnp.zeros_like(token)

    hbm = pl.BlockSpec(memory_space=pltpu.HBM)
    sem = pl.BlockSpec(memory_space=pltpu.SEMAPHORE)
    held = [pltpu.with_memory_space_constraint(a, pltpu.HBM) for a in list(arrs) + lands]
    outs = pl.pallas_call(
        body, name=name,
        out_shape=(pltpu.SemaphoreType.DMA((7 * n,)), pltpu.SemaphoreType.DMA((7 * n,)),
                   *[pltpu.HBM(a.shape, a.dtype) for a in held], jax.ShapeDtypeStruct((8, 128), F32)),
        in_specs=[hbm] * (2 * n) + [pl.BlockSpec(memory_space=pl.ANY)] * len(extra),
        out_specs=(sem, sem, *[hbm] * (2 * n), pl.BlockSpec(memory_space=pltpu.VMEM)),
        input_output_aliases={a: 2 + a for a in range(2 * n)},
        compiler_params=pltpu.CompilerParams(has_side_effects=pltpu.SideEffectType.DATAFLOW_SIDE_EFFECTING),
    )(*held, *extra)
    return dict(send=outs[0], recv=outs[1], srcs=list(outs[2:2 + n]), lands=list(outs[2 + n:2 + 2 * n]),
                token=outs[-1], gather=list(gather))


def _exchange_wait(flight, after, me, name):
    n = len(flight["srcs"])
    gather = flight["gather"]

    def body(*refs):
        srcs, land_refs = refs[:n], refs[n:2 * n]
        send_sems, recv_sems = refs[2 * n], refs[2 * n + 1]
        ix, iy, ic = lax.axis_index("x"), lax.axis_index("y"), lax.axis_index("c")
        mine = 4 * ix + 2 * iy + ic
        for a in range(n):
            for k, peer, peer_idx in _peers(ix, iy, ic):
                cp = pltpu.make_async_remote_copy(
                    src_ref=srcs[a] if gather[a] else srcs[a].at[peer_idx], dst_ref=land_refs[a].at[mine],
                    send_sem=send_sems.at[7 * a + k], recv_sem=recv_sems.at[7 * a + k], device_id=peer,
                    device_id_type=pl.DeviceIdType.MESH)
                cp.wait_send()
                cp.wait_recv()

    hbm = pl.BlockSpec(memory_space=pltpu.HBM)
    sem = pl.BlockSpec(memory_space=pltpu.SEMAPHORE)
    held = flight["srcs"] + flight["lands"]
    outs = pl.pallas_call(
        body, name=name, out_shape=tuple(pltpu.HBM(a.shape, a.dtype) for a in held),
        in_specs=[hbm] * (2 * n) + [sem, sem, pl.BlockSpec(memory_space=pl.ANY)], out_specs=tuple([hbm] * (2 * n)),
        input_output_aliases={a: a for a in range(2 * n)},
        compiler_params=pltpu.CompilerParams(has_side_effects=pltpu.SideEffectType.DATAFLOW_SIDE_EFFECTING),
    )(*held, flight["send"], flight["recv"], after)
    got = []
    for a in range(n):
        src, land = outs[a], outs[n + a]
        own = src if gather[a] else lax.dynamic_index_in_dim(src, me, axis=0, keepdims=False)
        got.append(lax.dynamic_update_index_in_dim(land, own, me, axis=0))
    return got


def _ada_fwd(c_all, ada_w, ada_b_cols):
    nb, D = c_all.shape
    cols = ada_w.shape[2]

    def body(c_ref, w_ref, b_ref, out_ref):
        ca = _silu(c_ref[...])
        for l in range(DEPTH):
            out_ref[l] = _mm(ca, w_ref[l]) + b_ref[l:l + 1, :]

    return pl.pallas_call(
        body, name="ada_fwd", out_shape=jax.ShapeDtypeStruct((DEPTH, nb, cols), F32),
        in_specs=[pl.BlockSpec(memory_space=pltpu.VMEM)] * 3, out_specs=pl.BlockSpec(memory_space=pltpu.VMEM),
        compiler_params=pltpu.CompilerParams(vmem_limit_bytes=32 * VMEM_MB),
    )(c_all, ada_w, ada_b_cols)


def _ada_bwd(c_all, d_mod_cols):
    nb, D = c_all.shape
    cols = d_mod_cols.shape[2]

    def body(c_ref, dm_ref, out_ref):
        ca = _silu(c_ref[...])
        for l in range(DEPTH):
            out_ref[l] = _mm_tn(ca, dm_ref[l])

    return pl.pallas_call(
        body, name="ada_bwd", out_shape=jax.ShapeDtypeStruct((DEPTH, D, cols), F32),
        in_specs=[pl.BlockSpec(memory_space=pltpu.VMEM)] * 2, out_specs=pl.BlockSpec(memory_space=pltpu.VMEM),
        compiler_params=pltpu.CompilerParams(vmem_limit_bytes=32 * VMEM_MB),
    )(c_all, d_mod_cols)


def _sum_adamw(parts, w, m, v, name, after=None):
    P, R, C = parts.shape
    tr = 256 if (R % 256 == 0 and R > 256) else R
    extra = [] if after is None else [after]

    def body(p_ref, w_ref, m_ref, v_ref, *rest):
        g_ref, d_ref, nm_ref, nv_ref = rest[-4:]
        g = p_ref[0].astype(F32)
        for k in range(1, P):
            g = g + p_ref[k].astype(F32)
        g_ref[...] = g
        nm = ADAM_B1 * m_ref[...] + (1.0 - ADAM_B1) * g
        nv = ADAM_B2 * v_ref[...] + (1.0 - ADAM_B2) * (g * g)
        nm_ref[...] = nm
        nv_ref[...] = nv
        m_hat = nm / (1.0 - ADAM_B1 ** ADAM_STEP)
        v_hat = nv / (1.0 - ADAM_B2 ** ADAM_STEP)
        d_ref[...] = -ADAM_LR * (m_hat / (jnp.sqrt(v_hat) + ADAM_EPS) + ADAM_WD * w_ref[...])

    blk = pl.BlockSpec((tr, C), lambda i: (i, 0))
    shp = jax.ShapeDtypeStruct((R, C), F32)
    return pl.pallas_call(
        body, name=name, grid=(R // tr,),
        in_specs=[pl.BlockSpec((P, tr, C), lambda i: (0, i, 0)), blk, blk, blk]
        + [pl.BlockSpec(memory_space=pl.ANY)] * len(extra),
        out_specs=[blk, blk, blk, blk], out_shape=[shp, shp, shp, shp],
        compiler_params=_cp(("parallel",)),
    )(parts, w, m, v, *extra)


def _sum_adamw_layer(parts, w, m, v, layer, name, prev=None, after=None):
    P, R, C = parts.shape
    tr = 256 if (R % 256 == 0 and R > 256) else R

    def body(p_ref, w_ref, m_ref, v_ref, *rest):
        g_ref, d_ref, nm_ref, nv_ref = rest[-4:]
        g = p_ref[0].astype(F32)
        for k in range(1, P):
            g = g + p_ref[k].astype(F32)
        g_ref[0] = g
        nm = ADAM_B1 * m_ref[0] + (1.0 - ADAM_B1) * g
        nv = ADAM_B2 * v_ref[0] + (1.0 - ADAM_B2) * (g * g)
        nm_ref[0] = nm
        nv_ref[0] = nv
        m_hat = nm / (1.0 - ADAM_B1 ** ADAM_STEP)
        v_hat = nv / (1.0 - ADAM_B2 ** ADAM_STEP)
        d_ref[0] = -ADAM_LR * (m_hat / (jnp.sqrt(v_hat) + ADAM_EPS) + ADAM_WD * w_ref[0])

    blk = pl.BlockSpec((1, tr, C), lambda i: (layer, i, 0))
    shp = jax.ShapeDtypeStruct(w.shape, F32)
    in_specs = [pl.BlockSpec((P, tr, C), lambda i: (0, i, 0)), blk, blk, blk]
    args = [parts, w, m, v]
    aliases = {}
    if prev is not None:
        in_specs += [pl.BlockSpec(memory_space=pl.ANY)] * 4
        args += list(prev)
        aliases = {4 + k: k for k in range(4)}
    if after is not None:
        in_specs.append(pl.BlockSpec(memory_space=pl.ANY))
        args.append(after)
    return list(pl.pallas_call(
        body, name=name, grid=(R // tr,), in_specs=in_specs, out_specs=[blk] * 4, out_shape=[shp] * 4,
        input_output_aliases=aliases, compiler_params=_cp(("parallel",)),
    )(*args))


SMALL = ["norm_w", "mla_q_norm", "mla_kv_norm", "gla_w_g2", "gla_b_g2", "gla_norm", "final_norm"]


SMALL_ROWS = 72


def _pack_small(loss, part):
    flat = [jnp.pad(loss.reshape(1), (0, 127))] + [part[n].reshape(-1) for n in SMALL]
    used = sum(f.shape[0] for f in flat)
    flat.append(jnp.zeros((SMALL_ROWS * 128 - used,), F32))
    return jnp.concatenate(flat).reshape(SMALL_ROWS, 128)


def _small_adamw(packed_parts, w, m, v, after=None):
    n = len(w)
    extra = [] if after is None else [after]

    def body(*refs):
        p_ref = refs[0]
        w_refs, m_refs, v_refs = refs[1:1 + n], refs[1 + n:1 + 2 * n], refs[1 + 2 * n:1 + 3 * n]
        outs, acc = refs[1 + 3 * n + len(extra):-1], refs[-1]
        total = p_ref[0]
        for k in range(1, N_DEV):
            total = total + p_ref[k]
        acc[...] = total
        outs[0][...] = acc[0:1, :]
        r0 = 1
        for i in range(n):
            shp = w_refs[i].shape
            if len(shp) == 3:
                g = acc[r0:r0 + shp[0] * shp[1], :].reshape(shp)
                r0 += shp[0] * shp[1]
            elif shp[1] < 128:
                g = acc[r0:r0 + shp[0], 0:shp[1]]
                r0 += shp[0]
            else:
                k = shp[1] // 128
                g = jnp.concatenate(
                    [jnp.concatenate([acc[r0 + l * k + j:r0 + l * k + j + 1, :] for j in range(k)], axis=1)
                     for l in range(shp[0])], axis=0)
                r0 += shp[0] * k
            nm = ADAM_B1 * m_refs[i][...] + (1.0 - ADAM_B1) * g
            nv = ADAM_B2 * v_refs[i][...] + (1.0 - ADAM_B2) * (g * g)
            m_hat = nm / (1.0 - ADAM_B1 ** ADAM_STEP)
            v_hat = nv / (1.0 - ADAM_B2 ** ADAM_STEP)
            outs[1 + 4 * i][...] = g
            outs[2 + 4 * i][...] = -ADAM_LR * (m_hat / (jnp.sqrt(v_hat) + ADAM_EPS) + ADAM_WD * w_refs[i][...])
            outs[3 + 4 * i][...] = nm
            outs[4 + 4 * i][...] = nv

    vmem = pl.BlockSpec(memory_space=pltpu.VMEM)
    out_shape = [jax.ShapeDtypeStruct((1, 128), F32)]
    for a in w:
        out_shape += [jax.ShapeDtypeStruct(a.shape, F32)] * 4
    outs = pl.pallas_call(
        body, name="adamw_small", in_specs=[vmem] * (1 + 3 * n) + [pl.BlockSpec(memory_space=pl.ANY)] * len(extra),
        out_specs=[vmem] * (1 + 4 * n), out_shape=out_shape, scratch_shapes=[pltpu.VMEM((SMALL_ROWS, 128), F32)],
    )(packed_parts, *w, *m, *v, *extra)
    return outs[0], [outs[1 + 4 * i:5 + 4 * i] for i in range(n)]


WEIGHTS = ["norm_w", "ada_w", "ada_b", "w_in", "mla_q_norm", "w_uq", "mla_kv_norm", "w_ukv", "gla_w_g2",
           "gla_b_g2", "gla_norm", "w_out", "final_norm"]


def kernel(x, c, positions, norm_w, ada_w, ada_b, w_in, mla_q_norm, w_uq, mla_kv_norm, w_ukv, gla_w_g2, gla_b_g2, gla_norm, w_out, final_norm, loss_target, m_norm_w, m_ada_w, m_ada_b, m_w_in, m_mla_q_norm, m_w_uq, m_mla_kv_norm, m_w_ukv, m_gla_w_g2, m_gla_b_g2, m_gla_norm, m_w_out, m_final_norm, v_norm_w, v_ada_w, v_ada_b, v_w_in, v_mla_q_norm, v_w_uq, v_mla_kv_norm, v_w_ukv, v_gla_w_g2, v_gla_b_g2, v_gla_norm, v_w_out, v_final_norm):
    w = dict(norm_w=norm_w, ada_w=ada_w, ada_b=ada_b, w_in=w_in, mla_q_norm=mla_q_norm, w_uq=w_uq,
             mla_kv_norm=mla_kv_norm, w_ukv=w_ukv, gla_w_g2=gla_w_g2, gla_b_g2=gla_b_g2, gla_norm=gla_norm,
             w_out=w_out, final_norm=final_norm)
    m = dict(norm_w=m_norm_w, ada_w=m_ada_w, ada_b=m_ada_b, w_in=m_w_in, mla_q_norm=m_mla_q_norm, w_uq=m_w_uq,
             mla_kv_norm=m_mla_kv_norm, w_ukv=m_w_ukv, gla_w_g2=m_gla_w_g2, gla_b_g2=m_gla_b_g2,
             gla_norm=m_gla_norm, w_out=m_w_out, final_norm=m_final_norm)
    v = dict(norm_w=v_norm_w, ada_w=v_ada_w, ada_b=v_ada_b, w_in=v_w_in, mla_q_norm=v_mla_q_norm, w_uq=v_w_uq,
             mla_kv_norm=v_mla_kv_norm, w_ukv=v_w_ukv, gla_w_g2=v_gla_w_g2, gla_b_g2=v_gla_b_g2,
             gla_norm=v_gla_norm, w_out=v_w_out, final_norm=v_final_norm)
    B, S, D = x.shape
    me = 4 * lax.axis_index("x") + 2 * lax.axis_index("y") + lax.axis_index("c")
    ada_cols = ada_w.shape[2]
    cast = lambda a: a.astype(_MXU)

    sharded = ["w_in", "w_uq", "w_ukv", "w_out"]

    whole_cols = lambda a: jnp.transpose(a, (1, 0, 2)).reshape(a.shape[1], -1)
    whole_in = _arrange_w_in
    whole_rest = lambda blks: (_arrange_w_uq(whole_cols(blks[0])), _arrange_w_ukv(whole_cols(blks[1])),
                               blks[2].reshape(D, D))
    col_blocks = lambda a: jnp.transpose(a.reshape(a.shape[0], N_DEV, -1), (1, 0, 2)).astype(jnp.bfloat16)
    blocks_in = lambda dw_in_a: _unarrange_w_in(dw_in_a, N_DEV, w_in.shape[2])
    blocks_rest = lambda dw_out, dw_uq_a, dw_ukv_a: [
        col_blocks(_unarrange_w_uq(dw_uq_a)), col_blocks(_unarrange_w_ukv(dw_ukv_a)),
        dw_out.reshape(N_DEV, D // N_DEV, D).astype(jnp.bfloat16)]

    (c_g,) = _exchange([c], [True], "gather_c")
    c_all = c_g.reshape(N_DEV * B, D)

    ada_b_cols = lax.dynamic_slice(ada_b, (0, me * ada_cols), (DEPTH, ada_cols))
    mod_cols = _ada_fwd(c_all, ada_w, ada_b_cols)
    mod_send = jnp.transpose(mod_cols.reshape(DEPTH, N_DEV, B, ada_cols), (1, 0, 2, 3))
    (mod_recv,) = _exchange([mod_send], [False], "scatter_mod")
    mod = jnp.transpose(mod_recv, (1, 2, 0, 3)).reshape(DEPTH, B, 3 * D)

    flight_i = _exchange_start([cast(w_in[0])], [True], "gather_start_first", after=mod)
    flight_r = _exchange_start([cast(w[n][0]) for n in sharded[1:]], [True] * 3, "gather_start_layer0",
                               after=flight_i["token"])
    flight_w = _exchange_start([cast(w[n][1]) for n in sharded], [True] * 4, "gather_start_layer1",
                               after=flight_r["token"])
    small_w = {n: w[n] for n in SMALL}
    layer_small = lambda l: {n: a[l] for n, a in small_w.items() if n != "final_norm"}
    tabs = _rope_tables(positions.reshape(B, S, 1), flight_w["token"][0, 0])
    late0 = lambda after: whole_rest(_exchange_wait(flight_r, after, me, "gather_wait_layer0"))
    (w_in0_g,) = _exchange_wait(flight_i, tabs[0], me, "gather_wait_first")
    x1, saved0 = _layer_fwd(x, tabs, mod[0], layer_small(0), whole_in(w_in0_g), late_weights=late0)
    got1 = _exchange_wait(flight_w, x1, me, "gather_wait_layer1")
    (dx, loss, d_fw), saved1 = _layer_fwd(x1, tabs, mod[1], layer_small(1), whole_in(got1[0]), *whole_rest(got1[1:]),
                                          loss_head=(final_norm.reshape(1, D), loss_target))

    dx, g1 = _layer_bwd(dx, saved1, tabs)
    flight_g = _exchange_start([blocks_in(g1["w_in_a"])] + blocks_rest(g1["w_out"], g1["w_uq_a"], g1["w_ukv_a"]),
                               [False] * 4, "grads_start_layer1")
    flights = {}

    def early0(dw_out, dw_uq_a, dw_ukv_a):
        flights["rest0"] = _exchange_start(blocks_rest(dw_out, dw_uq_a, dw_ukv_a), [False] * 3, "grads_start_layer0")
        return flights["rest0"]["token"][0, 0]

    saved0 = dict(saved0, gate=saved0["gate"] + flight_g["token"][0, 0])
    grad_x, g0 = _layer_bwd(dx, saved0, tabs, early_grads=early0)
    parts1 = _exchange_wait(flight_g, grad_x, me, "grads_wait_layer1")
    rest0 = _exchange_wait(flights["rest0"], g0["w_in_a"], me, "grads_wait_layer0")

    both = lambda n: jnp.stack([g0[n], g1[n]])
    d_mod = both("d_mod")
    part = dict(norm_w=both("norm_w"), mla_q_norm=both("mla_q_norm"), mla_kv_norm=both("mla_kv_norm"),
                gla_w_g2=both("gla_w_g2"), gla_b_g2=both("gla_b_g2"), gla_norm=both("gla_norm256")[:, 0:128],
                final_norm=d_fw)
    d_mod_g, small_g = _exchange([d_mod, _pack_small(loss, part)], [True, True], "gather_small")
    flight_l = _exchange_start([blocks_in(g0["w_in_a"])], [False], "exchange_start_last", after=small_g)
    res = {}
    behind = flight_l["token"]
    for a, name in enumerate(sharded):
        res[name] = _sum_adamw_layer(parts1[a], w[name], m[name], v[name], 1, "adamw_%s_layer1" % name, after=behind)
        behind = res[name][1]
    for a, name in enumerate(sharded[1:]):
        res[name] = _sum_adamw_layer(rest0[a], w[name], m[name], v[name], 0, "adamw_%s_layer0" % name,
                                     prev=res[name], after=behind)
        behind = res[name][1]

    d_mod_all = jnp.transpose(d_mod_g, (1, 0, 2, 3)).reshape(DEPTH, N_DEV * B, 3 * D)
    d_mod_cols = lax.dynamic_slice(d_mod_all, (0, 0, me * ada_cols), (DEPTH, N_DEV * B, ada_cols))
    g_ada_w = _ada_bwd(c_all, d_mod_cols)

    def update(name, parts2d, after):
        shp = w[name].shape
        two = lambda a: a.reshape(parts2d.shape[1:])
        out = _sum_adamw(parts2d, two(w[name]), two(m[name]), two(v[name]), "adamw_" + name, after=after)
        res[name] = [o.reshape(shp) for o in out]
        return out[1]

    behind = update("ada_w", g_ada_w.reshape(1, DEPTH * D, ada_cols), behind)
    behind = update("ada_b", jnp.transpose(d_mod_g, (0, 2, 1, 3)).reshape(N_DEV * B, DEPTH * 3 * D // 128, 128), behind)
    row = lambda a: a.reshape(1, D) if a.ndim == 1 else a
    loss_sum, small_out = _small_adamw(small_g, [row(w[n]) for n in SMALL], [row(m[n]) for n in SMALL],
                                       [row(v[n]) for n in SMALL], after=behind)
    for n, outs in zip(SMALL, small_out):
        res[n] = [o.reshape(w[n].shape) for o in outs]
    loss_out = loss_sum[0, 0]
    (in0,) = _exchange_wait(flight_l, loss_sum, me, "exchange_wait_last")
    res["w_in"] = _sum_adamw_layer(in0, w_in, m_w_in, v_w_in, 0, "adamw_w_in_layer0", prev=res["w_in"])
    return (loss_out, grad_x, *[res[n][0] for n in WEIGHTS], *[res[n][1] for n in WEIGHTS],
            *[res[n][2] for n in WEIGHTS], *[res[n][3] for n in WEIGHTS])
```

```python
import functools
import math

import numpy as np
import jax
import jax.numpy as jnp
from jax import lax
from jax.experimental import pallas as pl
from jax.experimental.pallas import tpu as pltpu

F32 = jnp.float32
_MXU = jnp.bfloat16

D_MODEL = 1024
DEPTH = 2
CHUNK = 64
EPS = 1e-6
ROPE_THETA = 10000.0
N_DEV = 8

MLA_SCALE = 96.0 ** -0.5
RET_KSCALE = 64.0 ** -0.5
GLA_KSCALE = 32.0 ** -0.5
GLA_TAU = 16.0

ADAM_LR = 0.001
ADAM_B1 = 0.9
ADAM_B2 = 0.999
ADAM_EPS = 1e-08
ADAM_WD = 0.01
ADAM_STEP = 10

RET_W, MLA_W, GLA_W = 1024, 1024, 896
ARR_W = RET_W + MLA_W + GLA_W
VMEM_MB = 1024 * 1024


def _cp(sem, vmem_mb=48):
    return pltpu.CompilerParams(dimension_semantics=sem, vmem_limit_bytes=vmem_mb * VMEM_MB)


def _mm(a, b):
    return jnp.dot(a.astype(_MXU), b.astype(_MXU), preferred_element_type=F32)


def _mm_nt(a, b):
    return lax.dot_general(a.astype(_MXU), b.astype(_MXU), (((1,), (1,)), ((), ())),
                           preferred_element_type=F32)


def _mm_tn(a, b):
    return lax.dot_general(a.astype(_MXU), b.astype(_MXU), (((0,), (0,)), ((), ())),
                           preferred_element_type=F32)


def _mm_f32(a, b):
    return jnp.dot(a, b, precision=lax.Precision.HIGHEST, preferred_element_type=F32)


def _sig(z):
    return 1.0 / (1.0 + jnp.exp(-z))


def _silu(z):
    return z * _sig(z)


def _dsilu(z):
    s = _sig(z)
    return s * (1.0 + z * (1.0 - s))


def _full(shape):
    nd = len(shape)
    return pl.BlockSpec(shape, lambda *_: (0,) * nd)


def _w_in_runs(block_cols):
    m, g = RET_W, RET_W + MLA_W
    whole = [(base + 64 * h + 32 * t, 32, base + 128 * t + 32 * h)
             for base in (0, 256) for t in range(2) for h in range(4)]
    whole += [(512, 512, 512), (1024, 384, m), (1408, 32, m + 448), (1440, 512, m + 512),
              (1952, 528, g), (2480, 256, g + 640)]
    zeros = [(m + 384, 64), (m + 480, 32), (g + 528, 112)]
    runs = []
    for src, n, dst in whole:
        while n:
            blk, off = divmod(src, block_cols)
            k = min(n, block_cols - off)
            runs.append((blk, off, k, dst))
            src, n, dst = src + k, n - k, dst + k
    return runs, zeros


def _arrange_w_in(blocks, tm=256):
    n, rows, cols = blocks.shape
    runs, zeros = _w_in_runs(cols)

    def arrange_w_in_kernel(b_ref, a_ref):
        for dst, k in zeros:
            a_ref[:, dst:dst + k] = jnp.zeros((tm, k), a_ref.dtype)
        for blk, off, k, dst in runs:
            a_ref[:, dst:dst + k] = b_ref[blk, :, off:off + k]

    return pl.pallas_call(
        arrange_w_in_kernel, grid=(rows // tm,),
        in_specs=[pl.BlockSpec((n, tm, cols), lambda i: (0, i, 0))],
        out_specs=pl.BlockSpec((tm, ARR_W), lambda i: (i, 0)),
        out_shape=jax.ShapeDtypeStruct((rows, ARR_W), blocks.dtype),
        compiler_params=_cp(("parallel",)), name="arrange_w_in")(blocks)


def _unarrange_w_in(a, n, cols, tm=256):
    rows = a.shape[0]
    runs, _ = _w_in_runs(cols)

    def unarrange_w_in_kernel(a_ref, b_ref):
        for blk, off, k, dst in runs:
            b_ref[blk, :, off:off + k] = a_ref[:, dst:dst + k].astype(b_ref.dtype)

    return pl.pallas_call(
        unarrange_w_in_kernel, grid=(rows // tm,),
        in_specs=[pl.BlockSpec((tm, ARR_W), lambda i: (i, 0))],
        out_specs=pl.BlockSpec((n, tm, cols), lambda i: (0, i, 0)),
        out_shape=jax.ShapeDtypeStruct((n, rows, cols), jnp.bfloat16),
        compiler_params=_cp(("parallel",)), name="unarrange_w_in")(a)


def _arrange_mla_weights(uq_heads, ukv_heads):
    nh = uq_heads.shape[0]
    dt = uq_heads.dtype

    def arrange_mla_weights_kernel(uq_ref, ukv_ref, q_ref, kv_ref, kvt_ref):
        q_ref[...] = jnp.zeros(q_ref.shape, dt)
        kv_ref[...] = jnp.zeros(kv_ref.shape, dt)
        kvt_ref[...] = jnp.zeros(kvt_ref.shape, dt)
        for h in range(nh):
            q_ref[:, 128 * h:128 * h + 96] = uq_ref[h]
            blk = ukv_ref[h]
            kv_ref[:, 128 * h:128 * h + 64] = blk[:, 0:64]
            kv_ref[:, 128 * nh + 64 * h:128 * nh + 64 * h + 64] = blk[:, 64:128]
            blk_t = blk.astype(F32).T.astype(dt)
            kvt_ref[128 * h:128 * h + 64, :] = blk_t[0:64]
            kvt_ref[128 * nh + 64 * h:128 * nh + 64 * h + 64, :] = blk_t[64:128]

    return pl.pallas_call(
        arrange_mla_weights_kernel, name="arrange_mla_weights",
        out_shape=[jax.ShapeDtypeStruct((256, 128 * nh), dt), jax.ShapeDtypeStruct((128, 192 * nh), dt),
                   jax.ShapeDtypeStruct((192 * nh, 128), dt)])(uq_heads, ukv_heads)


def _unarrange_mla_weights(dw_uq_a, dw_ukv_a):
    nh = dw_uq_a.shape[1] // 128

    def unarrange_mla_weights_kernel(q_ref, kv_ref, uq_ref, ukv_ref):
        for h in range(nh):
            uq_ref[h] = q_ref[:, 128 * h:128 * h + 96].astype(uq_ref.dtype)
            ukv_ref[h, :, 0:64] = kv_ref[:, 128 * h:128 * h + 64].astype(ukv_ref.dtype)
            ukv_ref[h, :, 64:128] = kv_ref[:, 128 * nh + 64 * h:128 * nh + 64 * h + 64].astype(ukv_ref.dtype)

    return pl.pallas_call(
        unarrange_mla_weights_kernel, name="unarrange_mla_weights",
        out_shape=[jax.ShapeDtypeStruct((nh, 256, 96), jnp.bfloat16),
                   jax.ShapeDtypeStruct((nh, 128, 128), jnp.bfloat16)])(dw_uq_a, dw_ukv_a)


def _rope_tables(pos3, zero=0.0):
    B, S, _ = pos3.shape
    ts = min(S, 512)
    inv32 = (np.float32(ROPE_THETA) ** (-(np.arange(32, dtype=np.float32) / 32))).astype(np.float32)
    inv16 = (np.float32(ROPE_THETA) ** (-(np.arange(16, dtype=np.float32) / 16))).astype(np.float32)
    inv = np.zeros((1, 128), np.float32)
    inv[0, 0:32] = inv32
    inv[0, 32:48] = inv16

    def body(pos_ref, inv_ref, cr, sr, cm, sm):
        ang = pos_ref[0].astype(F32) * inv_ref[...]
        lane = lax.broadcasted_iota(jnp.int32, (1, 128), 1)

        def every_head(x):
            y = jnp.where(lane < 32, x, pltpu.roll(x, 32, 1))
            return jnp.where(lane < 64, y, pltpu.roll(y, 64, 1))

        def rotary_pair(x, fill):
            return jnp.where((lane >= 64) & (lane < 80), pltpu.roll(x, 32, 1),
                             jnp.where((lane >= 80) & (lane < 96), pltpu.roll(x, 48, 1), fill))

        c, s = jnp.cos(ang), jnp.sin(ang)
        cr[0] = every_head(c)
        sr[0] = every_head(s)
        cm[0] = rotary_pair(c, 1.0)
        sm[0] = rotary_pair(s, 0.0)

    tab = jax.ShapeDtypeStruct((B, S, 128), F32)
    blk = pl.BlockSpec((1, ts, 128), lambda b, i: (b, i, 0))
    return pl.pallas_call(
        body, name="rope_tables", grid=(B, S // ts),
        in_specs=[pl.BlockSpec((1, ts, 1), lambda b, i: (b, i, 0)), _full((1, 128))],
        out_specs=[blk, blk, blk, blk], out_shape=[tab, tab, tab, tab],
        compiler_params=_cp(("parallel", "parallel")),
    )(pos3, jnp.asarray(inv) + zero)


def _rope128(x, cos, sin):
    lane = lax.broadcasted_iota(jnp.int32, (1, 128), 1)
    rp = pltpu.roll(x, 16, 1)
    rm = pltpu.roll(x, 112, 1)
    return x * cos + jnp.where(lane < 80, -rm, rp) * sin


def _rope128_t(d, cos, sin):
    lane = lax.broadcasted_iota(jnp.int32, (1, 128), 1)
    y = d * sin
    yp = pltpu.roll(y, 16, 1)
    ym = pltpu.roll(y, 112, 1)
    return d * cos + jnp.where(lane < 64, 0.0, jnp.where(lane < 80, ym, jnp.where(lane < 96, -yp, 0.0)))


def _proj_fwd(x, shift, scale, nw, w_arr):
    B, S, D = x.shape
    tm = min(S, 512)

    def body(x_ref, sh_ref, sc_ref, nw_ref, w_ref, ret_ref, mla_ref, gla_ref, h_ref):
        xv = x_ref[0]
        rstd = lax.rsqrt(jnp.mean(xv * xv, axis=-1, keepdims=True) + EPS)
        h = (xv * rstd * nw_ref[...]) * (1.0 + sc_ref[0]) + sh_ref[0]
        hb = h.astype(_MXU)
        h_ref[0] = hb
        ret_ref[0] = jnp.dot(hb, w_ref[:, 0:RET_W], preferred_element_type=F32).astype(_MXU)
        mla_ref[0] = jnp.dot(hb, w_ref[:, RET_W:RET_W + MLA_W], preferred_element_type=F32).astype(_MXU)
        gla_ref[0] = jnp.dot(hb, w_ref[:, RET_W + MLA_W:ARR_W], preferred_element_type=F32).astype(_MXU)

    tok = lambda w: pl.BlockSpec((1, tm, w), lambda b, i: (b, i, 0))
    per_seq = pl.BlockSpec((1, 1, D), lambda b, i: (b, 0, 0))
    return pl.pallas_call(
        body, name="proj_fwd", grid=(B, S // tm),
        in_specs=[tok(D), per_seq, per_seq, _full((1, D)), _full((D, ARR_W))],
        out_specs=[tok(RET_W), tok(MLA_W), tok(GLA_W), tok(D)],
        out_shape=[jax.ShapeDtypeStruct((B, S, RET_W), _MXU), jax.ShapeDtypeStruct((B, S, MLA_W), _MXU),
                   jax.ShapeDtypeStruct((B, S, GLA_W), _MXU), jax.ShapeDtypeStruct((B, S, D), _MXU)],
        compiler_params=_cp(("parallel", "parallel")),
    )(x, shift, scale, nw, w_arr)


RET_L = 256


def _ret_consts(L):
    lg = np.log1p(-np.exp2(-5.0 - np.arange(4, dtype=np.float32))).astype(np.float32)
    i = np.arange(L)
    ci = i // CHUNK
    diff = (i[:, None] - i[None, :]).astype(np.float32)
    same = ci[:, None] == ci[None, :]
    past = ci[None, :] < ci[:, None]
    expo = np.where(same, np.abs(diff), np.where(past, diff, 0.0)).astype(np.float32)
    dec = np.where((same | past)[None], np.exp(lg[:, None, None] * expo[None]), 0.0).astype(np.float32)
    head = (np.arange(256) % 128) // 32
    qw = np.exp((i + 1.0)[:, None] * lg[head][None, :]).astype(np.float32)
    kw = np.exp((L - 1.0 - i)[:, None] * lg[head][None, :]).astype(np.float32)
    a_row = np.exp(np.float32(L) * lg[head])[None, :].astype(np.float32)
    return [jnp.asarray(t) for t in (dec.reshape(4 * L, L), qw, kw, a_row)]


def _ret_masks():
    lane = lax.broadcasted_iota(jnp.int32, (1, 256), 1)
    mh = [((lane % 128) // 32) == h for h in range(4)]
    mv = [(lane // 64) == h for h in range(4)]
    vi = lax.broadcasted_iota(jnp.int32, (256, 256), 0)
    ki = lax.broadcasted_iota(jnp.int32, (256, 256), 1)
    bd = (vi // 64) == ((ki % 128) // 32)
    return mh, mv, bd


def _ret_rope(p, cs, sn):
    q1, q2, k1, k2 = p[:, 0:128], p[:, 128:256], p[:, 256:384], p[:, 384:512]
    qr = jnp.concatenate([q1 * cs - q2 * sn, q2 * cs + q1 * sn], axis=1)
    kr = jnp.concatenate([k1 * cs - k2 * sn, k2 * cs + k1 * sn], axis=1) * RET_KSCALE
    return qr, kr


def _head_mean(x, mv, width):
    out = jnp.zeros_like(x)
    for m in mv:
        s = jnp.sum(jnp.where(m, x, 0.0), axis=-1, keepdims=True) * (1.0 / width)
        out = jnp.where(m, s, out)
    return out


def _stack_heads(x, masks):
    return jnp.concatenate([jnp.where(m, x, 0.0) for m in masks], axis=0)


def _fold_heads(xs, masks, L):
    out = jnp.where(masks[0], xs[0:L], 0.0)
    for h in range(1, 4):
        out = out + jnp.where(masks[h], xs[h * L:(h + 1) * L], 0.0)
    return out


RET_G = 2


def _ret_fwd(ret_p, cos, sin):
    B, S, _ = ret_p.shape
    L = min(RET_L, S)
    NB = S // L
    G = min(RET_G, NB)
    NG = NB // G
    consts = _ret_consts(L)

    def body(p_ref, c_ref, s_ref, ds_ref, qw_ref, kw_ref, a_ref, out_ref, raw_ref, st_ref, st_sc):
        @pl.when(pl.program_id(1) == 0)
        def _():
            st_sc[...] = jnp.zeros_like(st_sc)

        mh, mv, bd = _ret_masks()
        cs_ = range(G)
        rows = [slice(c * L, (c + 1) * L) for c in cs_]
        ps = [p_ref[0, rows[c], :].astype(F32) for c in cs_]
        qk = [_ret_rope(ps[c], c_ref[0, rows[c], :], s_ref[0, rows[c], :]) for c in cs_]
        vs = [ps[c][:, 512:768] for c in cs_]
        a_s = [_mm_nt(_stack_heads(qk[c][0], mh), qk[c][1]) for c in cs_]
        upd = [_mm_tn(vs[c], qk[c][1] * kw_ref[...]) for c in cs_]
        o_s = [_mm(a_s[c] * ds_ref[...], vs[c]) for c in cs_]
        st = st_sc[...]
        inter = []
        for c in cs_:
            st_ref[0, c] = st
            inter.append(_mm_nt(qk[c][0] * qw_ref[...], st))
            st = st * a_ref[...] + jnp.where(bd, upd[c], 0.0)
        st_sc[...] = st
        for c in cs_:
            r = _fold_heads(o_s[c], mv, L) + inter[c]
            raw_ref[0, rows[c], :] = r
            rstd = lax.rsqrt(_head_mean(r * r, mv, 64.0) + EPS)
            out_ref[0, rows[c], :] = (r * rstd * _silu(ps[c][:, 768:1024])).astype(_MXU)

    tok = lambda w: pl.BlockSpec((1, G * L, w), lambda b, n: (b, n, 0))
    return pl.pallas_call(
        body, name="ret_fwd", grid=(B, NG),
        in_specs=[tok(RET_W), tok(128), tok(128), _full((4 * L, L)), _full((L, 256)), _full((L, 256)),
                  _full((1, 256))],
        out_specs=[tok(256), tok(256), pl.BlockSpec((1, G, 256, 256), lambda b, n: (b, n, 0, 0))],
        out_shape=[jax.ShapeDtypeStruct((B, S, 256), _MXU), jax.ShapeDtypeStruct((B, S, 256), F32),
                   jax.ShapeDtypeStruct((B, NB, 256, 256), F32)],
        scratch_shapes=[pltpu.VMEM((256, 256), F32)],
        compiler_params=_cp(("parallel", "arbitrary")),
    )(ret_p, cos, sin, *consts)


def _ret_bwd(ret_p, cos, sin, raw, states, d_mix):
    B, S, _ = ret_p.shape
    L = min(RET_L, S)
    NB = S // L
    G = 1
    NG = NB // G
    consts = _ret_consts(L)

    def body(p_ref, c_ref, s_ref, raw_ref, st_ref, dm_ref, ds_ref, qw_ref, kw_ref, a_ref, dp_ref, dst_sc):
        @pl.when(pl.program_id(1) == 0)
        def _():
            dst_sc[...] = jnp.zeros_like(dst_sc)

        mh, mv, bd = _ret_masks()
        qw, kw, dec = qw_ref[...], kw_ref[...], ds_ref[...]
        cs_ = range(G)
        rows = [slice(c * L, (c + 1) * L) for c in cs_]
        ps = [p_ref[0, rows[c], :].astype(F32) for c in cs_]
        tabs = [(c_ref[0, rows[c], :], s_ref[0, rows[c], :]) for c in cs_]
        qk = [_ret_rope(ps[c], *tabs[c]) for c in cs_]
        vs = [ps[c][:, 512:768] for c in cs_]
        qs = [_stack_heads(qk[c][0], mh) for c in cs_]
        a_s = [_mm_nt(qs[c], qk[c][1]) for c in cs_]
        dr, dz = [], []
        for c in cs_:
            r = raw_ref[0, rows[c], :]
            z = ps[c][:, 768:1024]
            rstd = lax.rsqrt(_head_mean(r * r, mv, 64.0) + EPS)
            rn = r * rstd
            dm = dm_ref[0, rows[c], :]
            d_rn = dm * _silu(z)
            dz.append(dm * rn * _dsilu(z))
            dr.append(rstd * (d_rn - rn * _head_mean(d_rn * rn, mv, 64.0)))
        do_s = [_stack_heads(dr[c], mv) for c in cs_]
        da_s = [_mm_nt(do_s[c], vs[c]) for c in cs_]
        sts = [st_ref[0, c] for c in cs_]
        dq_st = [_mm(dr[c], sts[c]) for c in cs_]
        dst_in = [_mm_tn(dr[c], qk[c][0] * qw) for c in cs_]
        dv = [_mm_tn(a_s[c] * dec, do_s[c]) for c in cs_]
        dqr, dkr = [], []
        for c in cs_:
            da = da_s[c] * dec
            dqr.append(_fold_heads(_mm(da, qk[c][1]), mh, L) + dq_st[c] * qw)
            dkr.append(_mm_tn(da, qs[c]))
        dst_next = dst_sc[...]
        for c in reversed(cs_):
            g = jnp.where(bd, dst_next, 0.0)
            dv[c] = dv[c] + _mm_nt(qk[c][1] * kw, g)
            dkr[c] = dkr[c] + _mm(vs[c], g) * kw
            dst_next = dst_next * a_ref[...] + jnp.where(bd, dst_in[c], 0.0)
        dst_sc[...] = dst_next
        for c in cs_:
            cs, sn = tabs[c]
            dk = dkr[c] * RET_KSCALE
            dq1, dq2 = dqr[c][:, 0:128], dqr[c][:, 128:256]
            dk1, dk2 = dk[:, 0:128], dk[:, 128:256]
            dp_ref[0, rows[c], :] = jnp.concatenate(
                [dq1 * cs + dq2 * sn, dq2 * cs - dq1 * sn, dk1 * cs + dk2 * sn, dk2 * cs - dk1 * sn, dv[c], dz[c]],
                axis=1).astype(_MXU)

    tok = lambda w: pl.BlockSpec((1, G * L, w), lambda b, i: (b, NG - 1 - i, 0))
    return pl.pallas_call(
        body, name="ret_bwd", grid=(B, NG),
        in_specs=[tok(RET_W), tok(128), tok(128), tok(256),
                  pl.BlockSpec((1, G, 256, 256), lambda b, i: (b, NG - 1 - i, 0, 0)), tok(256),
                  _full((4 * L, L)), _full((L, 256)), _full((L, 256)), _full((1, 256))],
        out_specs=tok(RET_W), out_shape=jax.ShapeDtypeStruct((B, S, RET_W), _MXU),
        scratch_shapes=[pltpu.VMEM((256, 256), F32)],
        compiler_params=_cp(("parallel", "arbitrary")),
    )(ret_p, cos, sin, raw, states, d_mix, *consts)


def _gla_masks():
    C = CHUNK
    lk = lax.broadcasted_iota(jnp.int32, (1, 128), 1)
    lv = lax.broadcasted_iota(jnp.int32, (1, 256), 1)
    mk = [(lk // 32) == h for h in range(4)]
    mv = [(lv // 64) == h for h in range(4)]
    vi = lax.broadcasted_iota(jnp.int32, (256, 128), 0)
    ki = lax.broadcasted_iota(jnp.int32, (256, 128), 1)
    bd = (vi // 64) == (ki // 32)
    ri = lax.broadcasted_iota(jnp.int32, (4 * C, C), 0) % C
    cj = lax.broadcasted_iota(jnp.int32, (4 * C, C), 1)
    lower = ri >= cj
    ti = lax.broadcasted_iota(jnp.int32, (C, C), 0)
    tj = lax.broadcasted_iota(jnp.int32, (C, C), 1)
    ltri = jnp.where(ti >= tj, 1.0, 0.0).astype(F32)
    utri = jnp.where(ti <= tj, 1.0, 0.0).astype(F32)
    return mk, mv, bd, lower, ltri, utri


def _log_sigmoid(x):
    return jnp.minimum(x, 0.0) - jnp.log(1.0 + jnp.exp(-jnp.abs(x)))


GLA_G = 8


def _gla_fwd(gla_p, w_g2p, b_g2, gnw):
    B, S, _ = gla_p.shape
    C = CHUNK
    NC = S // C
    G = min(GLA_G, NC)
    NG = NC // G

    def body(p_ref, w_ref, b_ref, gn_ref, out_ref, raw_ref, st_ref, st_sc):
        @pl.when(pl.program_id(1) == 0)
        def _():
            st_sc[...] = jnp.zeros_like(st_sc)

        mk, mv, bd, lower, ltri, _ = _gla_masks()
        cs = range(G)
        rows = [slice(c * C, (c + 1) * C) for c in cs]
        ps = [p_ref[0, rows[c], :].astype(F32) for c in cs]
        pre = [_mm(ps[c][:, 512:640], w_ref[...]) + b_ref[...] for c in cs]
        cum = [_mm_f32(ltri, _log_sigmoid(pre[c]) * (1.0 / GLA_TAU)) for c in cs]
        past, fut, upd, q_pos, a_row = [], [], [], [], []
        for c in cs:
            q = ps[c][:, 0:128]
            k = ps[c][:, 128:256] * GLA_KSCALE
            last = cum[c][C - 1:C, :]
            e_pos = jnp.exp(cum[c])
            e_neg = jnp.exp(-cum[c])
            q_pos.append(q * e_pos)
            a_row.append(jnp.exp(last))
            past.append(_mm_nt(_stack_heads(q_pos[c], mk), k * e_neg))
            fut.append(_mm_nt(_stack_heads(q * e_neg, mk), k * e_pos))
            upd.append(_mm_tn(ps[c][:, 256:512], k * jnp.exp(last - cum[c])))
        o_s = [_mm(jnp.where(lower, past[c], fut[c]), ps[c][:, 256:512]) for c in cs]
        st = st_sc[...]
        inter = []
        for c in cs:
            st_ref[0, c] = st
            inter.append(_mm_nt(q_pos[c], st))
            st = st * a_row[c] + jnp.where(bd, upd[c], 0.0)
        st_sc[...] = st
        for c in cs:
            g = _fold_heads(o_s[c], mv, C) + inter[c]
            raw_ref[0, rows[c], :] = g
            rstd = lax.rsqrt(_head_mean(g * g, mv, 64.0) + EPS)
            out_ref[0, rows[c], :] = (g * rstd * gn_ref[...] * _silu(ps[c][:, 640:896])).astype(_MXU)

    tok = lambda w: pl.BlockSpec((1, G * C, w), lambda b, n: (b, n, 0))
    return pl.pallas_call(
        body, name="gla_fwd", grid=(B, NG),
        in_specs=[tok(GLA_W), _full((128, 128)), _full((1, 128)), _full((1, 256))],
        out_specs=[tok(256), tok(256), pl.BlockSpec((1, G, 256, 128), lambda b, n: (b, n, 0, 0))],
        out_shape=[jax.ShapeDtypeStruct((B, S, 256), _MXU), jax.ShapeDtypeStruct((B, S, 256), F32),
                   jax.ShapeDtypeStruct((B, NC, 256, 128), F32)],
        scratch_shapes=[pltpu.VMEM((256, 128), F32)],
        compiler_params=_cp(("parallel", "arbitrary")),
    )(gla_p, w_g2p, b_g2, gnw)


def _gla_bwd(gla_p, w_g2p, b_g2, gnw, raw, states, d_mix):
    B, S, _ = gla_p.shape
    C = CHUNK
    NC = S // C
    G = min(GLA_G, NC)
    NG = NC // G

    def body(p_ref, w_ref, b_ref, gn_ref, raw_ref, st_ref, dm_ref, dp_ref, dw_ref, db_ref, dgn_ref, dst_sc):
        first = (pl.program_id(0) == 0) & (pl.program_id(1) == 0)

        @pl.when(first)
        def _():
            dw_ref[...] = jnp.zeros_like(dw_ref)
            db_ref[...] = jnp.zeros_like(db_ref)
            dgn_ref[...] = jnp.zeros_like(dgn_ref)

        @pl.when(pl.program_id(1) == 0)
        def _():
            dst_sc[...] = jnp.zeros_like(dst_sc)

        mk, mv, bd, lower, ltri, utri = _gla_masks()
        gn = gn_ref[...]
        cs = range(G)
        rows = [slice(c * C, (c + 1) * C) for c in cs]
        ps = [p_ref[0, rows[c], :].astype(F32) for c in cs]
        vs = [ps[c][:, 256:512] for c in cs]
        pre = [_mm(ps[c][:, 512:640], w_ref[...]) + b_ref[...] for c in cs]
        cum = [_mm_f32(ltri, _log_sigmoid(pre[c]) * (1.0 / GLA_TAU)) for c in cs]
        dg, dz, dgn_acc = [], [], jnp.zeros((1, 256), F32)
        for c in cs:
            g = raw_ref[0, rows[c], :]
            z = ps[c][:, 640:896]
            rstd = lax.rsqrt(_head_mean(g * g, mv, 64.0) + EPS)
            gh = g * rstd
            dm = dm_ref[0, rows[c], :]
            d_gn = dm * _silu(z)
            dz.append(dm * gh * gn * _dsilu(z))
            dgn_acc = dgn_acc + jnp.sum(d_gn * gh, axis=0, keepdims=True)
            d_gh = d_gn * gn
            dg.append(rstd * (d_gh - gh * _head_mean(d_gh * gh, mv, 64.0)))
        do_s = [_stack_heads(dg[c], mv) for c in cs]
        dattn = [_mm_nt(do_s[c], vs[c]) for c in cs]
        ks, e_pos, e_neg, q_pos, q_neg, k_pos, k_neg, qp_s, qn_s, past, fut, a_row, w_dec, kd = ([] for _ in range(14))
        for c in cs:
            q = ps[c][:, 0:128]
            k = ps[c][:, 128:256] * GLA_KSCALE
            last = cum[c][C - 1:C, :]
            ep, en = jnp.exp(cum[c]), jnp.exp(-cum[c])
            ks.append(k), e_pos.append(ep), e_neg.append(en)
            q_pos.append(q * ep), q_neg.append(q * en), k_pos.append(k * ep), k_neg.append(k * en)
            qp_s.append(_stack_heads(q_pos[c], mk)), qn_s.append(_stack_heads(q_neg[c], mk))
            past.append(_mm_nt(qp_s[c], k_neg[c]))
            fut.append(_mm_nt(qn_s[c], k_pos[c]))
            a_row.append(jnp.exp(last))
            w_dec.append(jnp.exp(last - cum[c]))
            kd.append(k * w_dec[c])
        sts = [st_ref[0, c] for c in cs]
        dq_st = [_mm(dg[c], sts[c]) for c in cs]
        dst_in = [_mm_tn(dg[c], q_pos[c]) for c in cs]
        dv, dq_pos, dk_neg, dq_neg, dk_pos = [], [], [], [], []
        for c in cs:
            attn = jnp.where(lower, past[c], fut[c])
            dpast = jnp.where(lower, dattn[c], 0.0)
            dfut = jnp.where(lower, 0.0, dattn[c])
            dv.append(_mm_tn(attn, do_s[c]))
            dq_pos.append(_fold_heads(_mm(dpast, k_neg[c]), mk, C) + dq_st[c])
            dk_neg.append(_mm_tn(dpast, qp_s[c]))
            dq_neg.append(_fold_heads(_mm(dfut, k_pos[c]), mk, C))
            dk_pos.append(_mm_tn(dfut, qn_s[c]))
        dst_next = dst_sc[...]
        d_a, d_kd = [None] * G, [None] * G
        for c in reversed(cs):
            d_a[c] = jnp.sum(dst_next * sts[c], axis=0, keepdims=True)
            gmat = jnp.where(bd, dst_next, 0.0)
            d_kd[c] = _mm(vs[c], gmat)
            dv[c] = dv[c] + _mm_nt(kd[c], gmat)
            dst_next = dst_next * a_row[c] + jnp.where(bd, dst_in[c], 0.0)
        dst_sc[...] = dst_next
        row = lax.broadcasted_iota(jnp.int32, (C, 128), 0)
        d_la, dk, dq = [], [], []
        for c in cs:
            t = d_kd[c] * kd[c]
            dk.append(d_kd[c] * w_dec[c] + dk_neg[c] * e_neg[c] + dk_pos[c] * e_pos[c])
            dq.append(dq_pos[c] * e_pos[c] + dq_neg[c] * e_neg[c])
            d_last = jnp.sum(t, axis=0, keepdims=True) + d_a[c] * a_row[c]
            d_cum = (dq_pos[c] * q_pos[c] - dk_neg[c] * k_neg[c] - dq_neg[c] * q_neg[c] + dk_pos[c] * k_pos[c] - t)
            d_la.append(_mm_f32(utri, d_cum + jnp.where(row == C - 1, d_last, 0.0)))
        d_pre = [d_la[c] * _sig(-pre[c]) * (1.0 / GLA_TAU) for c in cs]
        d_gg = [_mm_nt(d_pre[c], w_ref[...]) for c in cs]
        dw_acc = _mm_tn(ps[0][:, 512:640], d_pre[0])
        db_acc = jnp.sum(d_pre[0], axis=0, keepdims=True)
        for c in cs[1:]:
            dw_acc = dw_acc + _mm_tn(ps[c][:, 512:640], d_pre[c])
            db_acc = db_acc + jnp.sum(d_pre[c], axis=0, keepdims=True)
        for c in cs:
            dp_ref[0, rows[c], :] = jnp.concatenate([dq[c], dk[c] * GLA_KSCALE, dv[c], d_gg[c], dz[c]],
                                                    axis=1).astype(_MXU)
        dw_ref[...] += dw_acc
        db_ref[...] += db_acc
        dgn_ref[...] += dgn_acc

        @pl.when((pl.program_id(0) == B - 1) & (pl.program_id(1) == NG - 1))
        def _():
            s1 = dgn_ref[...]
            s1 = s1 + pltpu.roll(s1, 128, 1)
            dgn_ref[...] = s1 + pltpu.roll(s1, 64, 1)

    tok = lambda w: pl.BlockSpec((1, G * C, w), lambda b, i: (b, NG - 1 - i, 0))
    return pl.pallas_call(
        body, name="gla_bwd", grid=(B, NG),
        in_specs=[tok(GLA_W), _full((128, 128)), _full((1, 128)), _full((1, 256)), tok(256),
                  pl.BlockSpec((1, G, 256, 128), lambda b, i: (b, NG - 1 - i, 0, 0)), tok(256)],
        out_specs=[tok(GLA_W), _full((128, 128)), _full((1, 128)), _full((1, 256))],
        out_shape=[jax.ShapeDtypeStruct((B, S, GLA_W), _MXU), jax.ShapeDtypeStruct((128, 128), F32),
                   jax.ShapeDtypeStruct((1, 128), F32), jax.ShapeDtypeStruct((1, 256), F32)],
        scratch_shapes=[pltpu.VMEM((256, 128), F32)],
        compiler_params=_cp(("arbitrary", "arbitrary")),
    )(gla_p, w_g2p, b_g2, gnw, raw, states, d_mix)


def _rms(x, w):
    rstd = lax.rsqrt(jnp.mean(x * x, axis=-1, keepdims=True) + EPS)
    xh = x * rstd
    return xh, rstd, xh * w


def _rms_bwd(dy, xh, rstd, w):
    dxh = dy * w
    return rstd * (dxh - xh * jnp.mean(dxh * xh, axis=-1, keepdims=True))


MLA_T = 256


def _mla_prep_fwd(mla_p, cos, sin, qnw, kvnw, w_uq, w_ukv, w_ukv_t):
    B, S, _ = mla_p.shape
    tm = min(S, 512)

    t = min(MLA_T, S)
    nt = tm // t

    def body(p_ref, c_ref, s_ref, qn_ref, kn_ref, wq_ref, wkv_ref, wkvt_ref, q_ref, k_ref, v_ref, kt_ref, vt_ref):
        p = p_ref[0].astype(F32)
        cs, sn = c_ref[0], s_ref[0]
        _, _, qn = _rms(p[:, 0:256], qn_ref[...])
        qpre = _mm(qn, wq_ref[...])
        _, _, kvn = _rms(p[:, 256:384], kn_ref[...])
        kv = _mm(kvn, wkv_ref[...])
        kvt = _mm_nt(wkvt_ref[...], kvn)
        kpe = _rope128(p[:, 384:512], cs, sn)
        kpet = kpe.T
        for h in range(8):
            sl = slice(128 * h, 128 * h + 128)
            q_ref[0, :, sl] = _rope128(qpre[:, sl], cs, sn).astype(_MXU)
            k_ref[0, :, sl] = (kv[:, sl] + kpe).astype(_MXU)
            kht = kvt[sl, :] + kpet
            for n in range(nt):
                kt_ref[0, n, sl, :] = kht[:, n * t:(n + 1) * t].astype(_MXU)
        v_ref[0] = kv[:, 1024:1536].astype(_MXU)
        for n in range(nt):
            vt_ref[0, n] = kvt[1024:1536, n * t:(n + 1) * t].astype(_MXU)

    tok = lambda w: pl.BlockSpec((1, tm, w), lambda b, i: (b, i, 0))
    tr = lambda w: pl.BlockSpec((1, nt, w, t), lambda b, i: (b, i, 0, 0))
    return pl.pallas_call(
        body, name="mla_prep_fwd", grid=(B, S // tm),
        in_specs=[tok(512), tok(128), tok(128), _full((1, 256)), _full((1, 128)), _full((256, 1024)),
                  _full((128, 1536)), _full((1536, 128))],
        out_specs=[tok(1024), tok(1024), tok(512), tr(1024), tr(512)],
        out_shape=[jax.ShapeDtypeStruct((B, S, 1024), _MXU), jax.ShapeDtypeStruct((B, S, 1024), _MXU),
                   jax.ShapeDtypeStruct((B, S, 512), _MXU), jax.ShapeDtypeStruct((B, S // t, 1024, t), _MXU),
                   jax.ShapeDtypeStruct((B, S // t, 512, t), _MXU)],
        compiler_params=_cp(("parallel", "parallel")),
    )(mla_p, cos, sin, qnw, kvnw, w_uq, w_ukv, w_ukv_t)


def _chunk_mask_t(t):
    kj = lax.broadcasted_iota(jnp.int32, (t, t), 0) // CHUNK
    qi = lax.broadcasted_iota(jnp.int32, (t, t), 1) // CHUNK
    return kj <= qi


MLA_HG = 8
MLA_HG_FWD = 8
LOG2E = 1.4426950408889634
MLA_C2 = MLA_SCALE * LOG2E


def _mla_attn_fwd(q, k, vt):
    B, S, _ = q.shape
    t = min(MLA_T, S)
    nq = S // t
    HG = MLA_HG_FWD
    NP = HG // 2

    def body(q_ref, k_ref, vt_ref, o_ref, lse_ref, sa, sb, m_sc, l_sc, acc_sc):
        i = pl.program_id(2)
        row = lax.broadcasted_iota(jnp.int32, (128, 1), 0)
        low = row < 64
        mask = _chunk_mask_t(t)
        m_sc[...] = jnp.full(m_sc.shape, -jnp.inf, F32)
        l_sc[...] = jnp.zeros_like(l_sc)
        acc_sc[...] = jnp.zeros_like(acc_sc)

        ones = jnp.ones((8, t), _MXU)

        def scores(j, buf):
            kb = k_ref[0, pl.ds(pl.multiple_of(j * t, t), t), :]
            for h in range(HG):
                cols = slice(128 * h, 128 * h + 128)
                buf[h] = (_mm_nt(kb[:, cols], q_ref[0, :, cols]) * MLA_C2).astype(_MXU)

        def absorb(j, buf, masked):
            vtb = vt_ref[0, j]
            for pr in range(NP):
                alphas, pvs = [], []
                for hh in range(2):
                    h = 2 * pr + hh
                    s = buf[h]
                    if masked:
                        s = jnp.where(mask, s, jnp.full_like(s, -jnp.inf))
                    m_old = m_sc[h]
                    m_new = jnp.maximum(m_old, jnp.max(s, axis=0, keepdims=True).astype(F32))
                    alpha = jnp.exp2(m_old - m_new)
                    p = jnp.exp2(s - m_new.astype(_MXU))
                    l_sc[h] = alpha * l_sc[h] + _mm(ones, p)[0:1, :]
                    m_sc[h] = m_new
                    vth = vtb[128 * pr:128 * pr + 128, :]
                    vth = jnp.where(low if hh == 0 else ~low, vth, jnp.zeros_like(vth))
                    pvs.append(_mm(vth, p))
                    alphas.append(alpha)
                acc_sc[pr] = acc_sc[pr] * jnp.where(low, alphas[0], alphas[1]) + pvs[0] + pvs[1]

        scores(0, sb)

        def pair(jj, carry):
            j0 = 2 * jj
            scores(j0 + 1, sa)
            absorb(j0, sb, False)
            scores(j0 + 2, sb)
            absorb(j0 + 1, sa, False)
            return carry

        lax.fori_loop(0, i // 2, pair, 0)

        @pl.when(i % 2 == 1)
        def _():
            scores(i, sa)
            absorb(i - 1, sb, False)
            absorb(i, sa, True)

        @pl.when(i % 2 == 0)
        def _():
            absorb(i, sb, True)

        for pr in range(NP):
            l_e, l_o = l_sc[2 * pr], l_sc[2 * pr + 1]
            o_ref[0, :, 128 * pr:128 * pr + 128] = (acc_sc[pr] / jnp.where(low, l_e, l_o)).T
            lse_ref[0, pr, 0, 0:1, :] = m_sc[2 * pr] + jnp.log(l_e) * LOG2E
            lse_ref[0, pr, 0, 1:2, :] = m_sc[2 * pr + 1] + jnp.log(l_o) * LOG2E

    return pl.pallas_call(
        body, name="mla_attn_fwd", grid=(B, 8 // HG, nq),
        in_specs=[pl.BlockSpec((1, t, 128 * HG), lambda b, g, i: (b, i, g)),
                  pl.BlockSpec((1, S, 128 * HG), lambda b, g, i: (b, 0, g)),
                  pl.BlockSpec((1, nq, 64 * HG, t), lambda b, g, i: (b, 0, g, 0))],
        out_specs=[pl.BlockSpec((1, t, 64 * HG), lambda b, g, i: (b, i, g)),
                   pl.BlockSpec((1, NP, 1, 2, t), lambda b, g, i: (b, g, i, 0, 0))],
        out_shape=[jax.ShapeDtypeStruct((B, S, 512), F32), jax.ShapeDtypeStruct((B, 4, nq, 2, t), F32)],
        scratch_shapes=[pltpu.VMEM((HG, t, t), _MXU), pltpu.VMEM((HG, t, t), _MXU), pltpu.VMEM((HG, 1, t), F32),
                        pltpu.VMEM((HG, 1, t), F32), pltpu.VMEM((NP, 128, t), F32)],
        compiler_params=_cp(("parallel", "parallel", "arbitrary")),
    )(q, k, vt)


def _mla_attn_bwd(q, k, v, kt, do, lse, dl):
    B, S, _ = q.shape
    t = min(MLA_T, S)
    nk = S // t

    HG = MLA_HG
    NP = HG // 2

    def body(q_ref, k_ref, v_ref, kt_ref, do_ref, lse_ref, dl_ref, dq_ref, dk_ref, dv_ref,
             sa, da, sb, db, dqt_sc, dk_sc, dv_sc):
        j = pl.program_id(2)

        @pl.when(j == 0)
        def _():
            dqt_sc[...] = jnp.zeros_like(dqt_sc)

        dk_sc[...] = jnp.zeros_like(dk_sc)
        dv_sc[...] = jnp.zeros_like(dv_sc)
        lane = lax.broadcasted_iota(jnp.int32, (1, 128), 1)
        low = lane < 64
        mask = _chunk_mask_t(t)

        def half(x, hh):
            return jnp.where(low if hh == 0 else ~low, x, jnp.zeros_like(x))

        def prepare(i, sbuf, dbuf):
            rows = pl.ds(pl.multiple_of(i * t, t), t)
            for h in range(HG):
                cols = slice(128 * h, 128 * h + 128)
                pc = slice(128 * (h // 2), 128 * (h // 2) + 128)
                sbuf[h] = _mm_nt(k_ref[0, :, cols], q_ref[0, rows, cols]) * MLA_C2
                dbuf[h] = _mm_nt(half(v_ref[0, :, pc], h % 2), do_ref[0, rows, pc])

        def absorb(i, sbuf, dbuf, masked):
            rows = pl.ds(pl.multiple_of(i * t, t), t)
            for h in range(HG):
                pr, hh = h // 2, h % 2
                cols = slice(128 * h, 128 * h + 128)
                pc = slice(128 * pr, 128 * pr + 128)
                p = jnp.exp2(sbuf[h] - lse_ref[0, pr, i][hh:hh + 1, :])
                if masked:
                    p = jnp.where(mask, p, 0.0)
                dv_sc[pr] += _mm(p, half(do_ref[0, rows, pc], hh))
                ds = p * (dbuf[h] - dl_ref[0, pr, i][hh:hh + 1, :])
                dqt_sc[i, cols, :] += _mm(kt_ref[0, 0, cols, :], ds)
                dk_sc[h] += _mm(ds, q_ref[0, rows, cols])

        n = nk - 1 - j
        prepare(jnp.minimum(j + 1, nk - 1), sb, db)

        def pair(jj, carry):
            i0 = j + 1 + 2 * jj
            prepare(i0 + 1, sa, da)
            absorb(i0, sb, db, False)
            prepare(jnp.where(i0 + 2 <= nk - 1, i0 + 2, j), sb, db)
            absorb(i0 + 1, sa, da, False)
            return carry

        lax.fori_loop(0, n // 2, pair, 0)

        @pl.when(n % 2 == 1)
        def _():
            prepare(j, sa, da)
            absorb(nk - 1, sb, db, False)
            absorb(j, sa, da, True)

        @pl.when(n % 2 == 0)
        def _():
            absorb(j, sb, db, True)

        for h in range(HG):
            dk_ref[0, :, 128 * h:128 * h + 128] = (dk_sc[h] * MLA_SCALE).astype(_MXU)
        for pr in range(NP):
            dv_ref[0, :, 128 * pr:128 * pr + 128] = dv_sc[pr].astype(_MXU)

        @pl.when(j == nk - 1)
        def _():
            for i in range(nk):
                dq_ref[0, i * t:(i + 1) * t, :] = (dqt_sc[i].T * MLA_SCALE).astype(_MXU)

    seq = lambda w: pl.BlockSpec((1, S, w), lambda b, g, j: (b, 0, g))
    blk = lambda w: pl.BlockSpec((1, t, w), lambda b, g, j: (b, j, g))
    stat = pl.BlockSpec((1, NP, nk, 2, t), lambda b, g, j: (b, g, 0, 0, 0))
    return pl.pallas_call(
        body, name="mla_attn_bwd", grid=(B, 8 // HG, nk),
        in_specs=[seq(128 * HG), blk(128 * HG), blk(64 * HG),
                  pl.BlockSpec((1, 1, 128 * HG, t), lambda b, g, j: (b, j, g, 0)), seq(64 * HG), stat, stat],
        out_specs=[seq(128 * HG), blk(128 * HG), blk(64 * HG)],
        out_shape=[jax.ShapeDtypeStruct((B, S, 1024), _MXU), jax.ShapeDtypeStruct((B, S, 1024), _MXU),
                   jax.ShapeDtypeStruct((B, S, 512), _MXU)],
        scratch_shapes=[pltpu.VMEM((HG, t, t), F32), pltpu.VMEM((HG, t, t), F32), pltpu.VMEM((HG, t, t), F32),
                        pltpu.VMEM((HG, t, t), F32), pltpu.VMEM((nk, 128 * HG, t), F32),
                        pltpu.VMEM((HG, t, 128), F32), pltpu.VMEM((NP, t, 128), F32)],
        compiler_params=_cp(("parallel", "parallel", "arbitrary"), 56),
    )(q, k, v, kt, do, lse, dl)


def _mla_prep_bwd(mla_p, cos, sin, qnw, kvnw, w_uq, w_ukv, dq, dk, dv):
    B, S, _ = mla_p.shape
    tm = min(S, 512)

    def body(p_ref, c_ref, s_ref, qn_ref, kn_ref, wq_ref, wkv_ref, dq_ref, dk_ref, dv_ref,
             dp_ref, dwq_ref, dwkv_ref, dqn_ref, dkn_ref):
        first = (pl.program_id(0) == 0) & (pl.program_id(1) == 0)

        @pl.when(first)
        def _():
            dwq_ref[...] = jnp.zeros_like(dwq_ref)
            dwkv_ref[...] = jnp.zeros_like(dwkv_ref)
            dqn_ref[...] = jnp.zeros_like(dqn_ref)
            dkn_ref[...] = jnp.zeros_like(dkn_ref)

        p = p_ref[0].astype(F32)
        cs, sn = c_ref[0], s_ref[0]
        lane = lax.broadcasted_iota(jnp.int32, (1, 128), 1)
        pe = (lane >= 64) & (lane < 96)
        qh, q_rstd, qn = _rms(p[:, 0:256], qn_ref[...])
        kvh, kv_rstd, kvn = _rms(p[:, 256:384], kn_ref[...])
        dqv = dq_ref[0].astype(F32)
        dkv = dk_ref[0].astype(F32)
        dqpre = jnp.concatenate(
            [_rope128_t(dqv[:, 128 * h:128 * h + 128], cs, sn) for h in range(8)], axis=1)
        dkpe = jnp.zeros((tm, 128), F32)
        for h in range(8):
            dkpe = dkpe + jnp.where(pe, dkv[:, 128 * h:128 * h + 128], 0.0)
        dkr = _rope128_t(dkpe, cs, sn)
        dkv_all = jnp.concatenate([dkv, dv_ref[0].astype(F32)], axis=1)
        d_qn = _mm_nt(dqpre, wq_ref[...])
        d_kvn = _mm_nt(dkv_all, wkv_ref[...])
        dwq_ref[...] += _mm_tn(qn, dqpre)
        dwkv_ref[...] += _mm_tn(kvn, dkv_all)
        dqn_ref[...] += jnp.sum(d_qn * qh, axis=0, keepdims=True)
        dkn_ref[...] += jnp.sum(d_kvn * kvh, axis=0, keepdims=True)
        dp_ref[0] = jnp.concatenate([_rms_bwd(d_qn, qh, q_rstd, qn_ref[...]),
                                     _rms_bwd(d_kvn, kvh, kv_rstd, kn_ref[...]), dkr], axis=1).astype(_MXU)

    tok = lambda w: pl.BlockSpec((1, tm, w), lambda b, i: (b, i, 0))
    return pl.pallas_call(
        body, name="mla_prep_bwd", grid=(B, S // tm),
        in_specs=[tok(512), tok(128), tok(128), _full((1, 256)), _full((1, 128)), _full((256, 1024)),
                  _full((128, 1536)), tok(1024), tok(1024), tok(512)],
        out_specs=[tok(512), _full((256, 1024)), _full((128, 1536)), _full((1, 256)), _full((1, 128))],
        out_shape=[jax.ShapeDtypeStruct((B, S, 512), _MXU), jax.ShapeDtypeStruct((256, 1024), F32),
                   jax.ShapeDtypeStruct((128, 1536), F32), jax.ShapeDtypeStruct((1, 256), F32),
                   jax.ShapeDtypeStruct((1, 128), F32)],
        compiler_params=_cp(("arbitrary", "arbitrary")),
    )(mla_p, cos, sin, qnw, kvnw, w_uq, w_ukv, dq, dk, dv)


def _out_fwd(x, gate, r_g, o_mla, mla_p, g_g, w_out):
    B, S, D = x.shape
    tm = min(S, 512)

    def body(x_ref, g_ref, r_ref, o_ref, z_ref, gg_ref, w_ref, xn_ref, y_ref):
        mm = (o_ref[0] * _silu(z_ref[0].astype(F32))).astype(_MXU)
        y = (jnp.dot(r_ref[0], w_ref[0:256, :], preferred_element_type=F32)
             + jnp.dot(mm, w_ref[256:768, :], preferred_element_type=F32)
             + jnp.dot(gg_ref[0], w_ref[768:1024, :], preferred_element_type=F32))
        y_ref[0] = y.astype(_MXU)
        xn_ref[0] = x_ref[0] + g_ref[0] * y

    tok = lambda w, c=0: pl.BlockSpec((1, tm, w), lambda b, i: (b, i, c))
    return pl.pallas_call(
        body, name="out_fwd", grid=(B, S // tm),
        in_specs=[tok(D), pl.BlockSpec((1, 1, D), lambda b, i: (b, 0, 0)), tok(256), tok(512), tok(512, 1),
                  tok(256), _full((D, D))],
        out_specs=[tok(D), tok(D)],
        out_shape=[jax.ShapeDtypeStruct((B, S, D), F32), jax.ShapeDtypeStruct((B, S, D), _MXU)],
        compiler_params=_cp(("parallel", "parallel")),
    )(x, gate, r_g, o_mla, mla_p, g_g, w_out)


def _out_bwd(dx, y, gate, r_g, g_g, w_out, o_mla, mla_p):
    B, S, D = dx.shape
    tm = min(S, 512)
    t = min(MLA_T, S)
    nt = tm // t

    def body(dx_ref, y_ref, g_ref, r_ref, gg_ref, w_ref, o_ref, z_ref,
             dr_ref, do_ref, dz_ref, dl_ref, dg_ref, dw_ref, dgate_ref, acc):
        first = (pl.program_id(0) == 0) & (pl.program_id(1) == 0)

        @pl.when(first)
        def _():
            acc[...] = jnp.zeros_like(acc)

        @pl.when(pl.program_id(1) == 0)
        def _():
            dgate_ref[...] = jnp.zeros_like(dgate_ref)

        dxv = dx_ref[0]
        dgate_ref[0] += jnp.sum(dxv * y_ref[0].astype(F32), axis=0, keepdims=True)
        dy = (dxv * g_ref[0]).astype(_MXU)
        dr_ref[0] = _mm_nt(dy, w_ref[0:256, :])
        dg_ref[0] = _mm_nt(dy, w_ref[768:1024, :])
        ov, z = o_ref[0], z_ref[0].astype(F32)
        acc[0:256, :] += _mm_tn(r_ref[0], dy)
        acc[256:768, :] += _mm_tn((ov * _silu(z)).astype(_MXU), dy)
        acc[768:1024, :] += _mm_tn(gg_ref[0], dy)

        @pl.when((pl.program_id(0) == B - 1) & (pl.program_id(1) == S // tm - 1))
        def _():
            dw_ref[...] = acc[...].astype(_MXU)

        dm = _mm_nt(dy, w_ref[256:768, :])
        do = dm * _silu(z)
        dz_ref[0] = (dm * ov * _dsilu(z)).astype(_MXU)
        do_ref[0] = do.astype(_MXU)
        prod = do * ov
        for pr in range(4):
            pt = prod[:, 128 * pr:128 * pr + 128].T
            se = jnp.sum(pt[0:64], axis=0, keepdims=True)
            so = jnp.sum(pt[64:128], axis=0, keepdims=True)
            for n in range(nt):
                dl_ref[0, pr, n, 0:1, :] = se[:, n * t:(n + 1) * t]
                dl_ref[0, pr, n, 1:2, :] = so[:, n * t:(n + 1) * t]

    tok = lambda w, c=0: pl.BlockSpec((1, tm, w), lambda b, i: (b, i, c))
    per_seq = pl.BlockSpec((1, 1, D), lambda b, i: (b, 0, 0))
    return pl.pallas_call(
        body, name="out_bwd", grid=(B, S // tm),
        in_specs=[tok(D), tok(D), per_seq, tok(256), tok(256), _full((D, D)), tok(512), tok(512, 1)],
        out_specs=[tok(256), tok(512), tok(512), pl.BlockSpec((1, 4, nt, 2, t), lambda b, i: (b, 0, i, 0, 0)),
                   tok(256), _full((D, D)), per_seq],
        out_shape=[jax.ShapeDtypeStruct((B, S, 256), F32), jax.ShapeDtypeStruct((B, S, 512), _MXU),
                   jax.ShapeDtypeStruct((B, S, 512), _MXU), jax.ShapeDtypeStruct((B, 4, S // t, 2, t), F32),
                   jax.ShapeDtypeStruct((B, S, 256), F32), jax.ShapeDtypeStruct((D, D), _MXU),
                   jax.ShapeDtypeStruct((B, 1, D), F32)],
        scratch_shapes=[pltpu.VMEM((D, D), F32)],
        compiler_params=_cp(("arbitrary", "arbitrary")),
    )(dx, y, gate, r_g, g_g, w_out, o_mla, mla_p)


def _proj_bwd_x(x, shift, scale, nw, w_arr, d_ret, d_mla, d_mz, d_gla, dx_out):
    B, S, D = x.shape
    tm = min(S, 512)

    def body(x_ref, sc_ref, nw_ref, w_ref, dr_ref, dm_ref, dz_ref, dg_ref, dxo_ref,
             dx_ref, dsh_ref, dsc_ref, dnw_ref):
        first = (pl.program_id(0) == 0) & (pl.program_id(1) == 0)

        @pl.when(first)
        def _():
            dnw_ref[...] = jnp.zeros_like(dnw_ref)

        @pl.when(pl.program_id(1) == 0)
        def _():
            dsh_ref[...] = jnp.zeros_like(dsh_ref)
            dsc_ref[...] = jnp.zeros_like(dsc_ref)

        dp = jnp.concatenate([dr_ref[0], dm_ref[0], dz_ref[0], dg_ref[0]], axis=1)
        dh = lax.dot_general(dp, w_ref[...], (((1,), (1,)), ((), ())), preferred_element_type=F32)
        xv = x_ref[0]
        rstd = lax.rsqrt(jnp.mean(xv * xv, axis=-1, keepdims=True) + EPS)
        xh = xv * rstd
        nwv = nw_ref[...]
        mod = 1.0 + sc_ref[0]
        dsh_ref[0] += jnp.sum(dh, axis=0, keepdims=True)
        dsc_ref[0] += jnp.sum(dh * xh * nwv, axis=0, keepdims=True)
        dnw_ref[...] += jnp.sum(dh * xh * mod, axis=0, keepdims=True)
        dxh = dh * nwv * mod
        dx_ref[0] = dxo_ref[0] + rstd * (dxh - xh * jnp.mean(dxh * xh, axis=-1, keepdims=True))

    tok = lambda w: pl.BlockSpec((1, tm, w), lambda b, i: (b, i, 0))
    per_seq = pl.BlockSpec((1, 1, D), lambda b, i: (b, 0, 0))
    return pl.pallas_call(
        body, name="proj_bwd_x", grid=(B, S // tm),
        in_specs=[tok(D), per_seq, _full((1, D)), _full((D, ARR_W)), tok(RET_W), tok(512), tok(512),
                  tok(GLA_W), tok(D)],
        out_specs=[tok(D), per_seq, per_seq, _full((1, D))],
        out_shape=[jax.ShapeDtypeStruct((B, S, D), F32), jax.ShapeDtypeStruct((B, 1, D), F32),
                   jax.ShapeDtypeStruct((B, 1, D), F32), jax.ShapeDtypeStruct((1, D), F32)],
        compiler_params=_cp(("arbitrary", "arbitrary")),
    )(x, scale, nw, w_arr, d_ret, d_mla, d_mz, d_gla, dx_out)


def _proj_bwd_w(h, d_ret, d_mla, d_mz, d_gla):
    B, S, D = h.shape
    tm = min(S, 512)

    def body(h_ref, dr_ref, dm_ref, dz_ref, dg_ref, dw_ref, acc):
        first = (pl.program_id(0) == 0) & (pl.program_id(1) == 0)

        @pl.when(first)
        def _():
            acc[...] = jnp.zeros_like(acc)

        hv = h_ref[0]
        tn = lambda d_ref: lax.dot_general(hv, d_ref[0], (((0,), (0,)), ((), ())), preferred_element_type=F32)
        acc[:, 0:RET_W] += tn(dr_ref)
        acc[:, RET_W:RET_W + 512] += tn(dm_ref)
        acc[:, RET_W + 512:RET_W + MLA_W] += tn(dz_ref)
        acc[:, RET_W + MLA_W:ARR_W] += tn(dg_ref)

        @pl.when((pl.program_id(0) == B - 1) & (pl.program_id(1) == S // tm - 1))
        def _():
            dw_ref[...] = acc[...].astype(_MXU)

    tok = lambda w: pl.BlockSpec((1, tm, w), lambda b, i: (b, i, 0))
    return pl.pallas_call(
        body, name="proj_bwd_w", grid=(B, S // tm),
        in_specs=[tok(D), tok(RET_W), tok(512), tok(512), tok(GLA_W)],
        out_specs=_full((D, ARR_W)), out_shape=jax.ShapeDtypeStruct((D, ARR_W), _MXU),
        scratch_shapes=[pltpu.VMEM((D, ARR_W), F32)],
        compiler_params=_cp(("arbitrary", "arbitrary"), 56),
    )(h, d_ret, d_mla, d_mz, d_gla)


def _out_fwd_loss(x, gate, r_g, o_mla, mla_p, g_g, w_out, fw, target):
    B, S, D = x.shape
    tm = min(S, 512)

    def body(x_ref, g_ref, r_ref, o_ref, z_ref, gg_ref, w_ref, fw_ref, t_ref, dx_ref, y_ref, loss_ref, dfw_ref):
        first = (pl.program_id(0) == 0) & (pl.program_id(1) == 0)

        @pl.when(first)
        def _():
            loss_ref[...] = jnp.zeros_like(loss_ref)
            dfw_ref[...] = jnp.zeros_like(dfw_ref)

        mm = (o_ref[0] * _silu(z_ref[0].astype(F32))).astype(_MXU)
        y = (jnp.dot(r_ref[0], w_ref[0:256, :], preferred_element_type=F32)
             + jnp.dot(mm, w_ref[256:768, :], preferred_element_type=F32)
             + jnp.dot(gg_ref[0], w_ref[768:1024, :], preferred_element_type=F32))
        y_ref[0] = y.astype(_MXU)
        xv = x_ref[0] + g_ref[0] * y
        fwv = fw_ref[...]
        rstd = lax.rsqrt(jnp.mean(xv * xv, axis=-1, keepdims=True) + EPS)
        xh = xv * rstd
        err = xh * fwv - t_ref[0]
        loss_ref[...] += 0.5 * jnp.sum(jnp.mean(err * err, axis=-1, keepdims=True), axis=0, keepdims=True)
        dy = err * (1.0 / D)
        dfw_ref[...] += jnp.sum(dy * xh, axis=0, keepdims=True)
        dxh = dy * fwv
        dx_ref[0] = rstd * (dxh - xh * jnp.mean(dxh * xh, axis=-1, keepdims=True))

    tok = lambda w, c=0: pl.BlockSpec((1, tm, w), lambda b, i: (b, i, c))
    return pl.pallas_call(
        body, name="out_fwd_loss", grid=(B, S // tm),
        in_specs=[tok(D), pl.BlockSpec((1, 1, D), lambda b, i: (b, 0, 0)), tok(256), tok(512), tok(512, 1),
                  tok(256), _full((D, D)), _full((1, D)), tok(D)],
        out_specs=[tok(D), tok(D), _full((1, 1)), _full((1, D))],
        out_shape=[jax.ShapeDtypeStruct((B, S, D), F32), jax.ShapeDtypeStruct((B, S, D), _MXU),
                   jax.ShapeDtypeStruct((1, 1), F32), jax.ShapeDtypeStruct((1, D), F32)],
        compiler_params=_cp(("arbitrary", "arbitrary")),
    )(x, gate, r_g, o_mla, mla_p, g_g, w_out, fw, target)


def _local_step(x, pos3, mod, loss_target, small, w_in_a, w_uq_a, w_ukv_a, w_out_b):
    B, S, D = x.shape
    tabs = _rope_tables(pos3)
    saved = []
    for l in range(DEPTH):
        last = (small["final_norm"].reshape(1, D), loss_target) if l == DEPTH - 1 else None
        x, s = _layer_fwd(x, tabs, mod[l], {n: a[l] for n, a in small.items() if n != "final_norm"},
                          w_in_a[l], w_uq_a[l], w_ukv_a[l], w_ukv_a[l].T, w_out_b[l], loss_head=last)
        saved.append(s)
    dx, loss, d_fw = x
    grads = dict(final_norm=d_fw.reshape(D))
    per_layer = [None] * DEPTH
    for l in reversed(range(DEPTH)):
        dx, per_layer[l] = _layer_bwd(dx, saved[l], tabs)
    for name in per_layer[0]:
        grads[name] = jnp.stack([per_layer[l][name] for l in range(DEPTH)])
    return loss, dx, grads


def _layer_fwd(x, tabs, mod_l, small_l, w_in_a, w_uq_a=None, w_ukv_a=None, w_ukv_t=None, w_out_b=None, late_weights=None,
               loss_head=None):
    B, S, D = x.shape
    cr, sr, cm, sm = tabs
    shift = mod_l[:, 0:D].reshape(B, 1, D)
    scale = mod_l[:, D:2 * D].reshape(B, 1, D)
    gate = mod_l[:, 2 * D:3 * D].reshape(B, 1, D)
    nw = small_l["norm_w"].reshape(1, D)
    qnw = small_l["mla_q_norm"].reshape(1, 256)
    kvnw = small_l["mla_kv_norm"].reshape(1, 128)
    w_g2p = jnp.pad(small_l["gla_w_g2"], ((0, 112), (0, 0)))
    b_g2 = small_l["gla_b_g2"].reshape(1, 128)
    gnw = jnp.tile(small_l["gla_norm"], 4).reshape(1, 256)
    ret_p, mla_p, gla_p, h = _proj_fwd(x, shift, scale, nw, w_in_a)
    r_g, r_raw, r_st = _ret_fwd(ret_p, cr, sr)
    if late_weights is not None:
        w_uq_a, w_ukv_a, w_ukv_t, w_out_b = late_weights(r_raw)
    q, k, v, kt, vt = _mla_prep_fwd(mla_p, cm, sm, qnw, kvnw, w_uq_a, w_ukv_a, w_ukv_t)
    o_mla, lse = _mla_attn_fwd(q, k, vt)
    g_g, g_raw, g_st = _gla_fwd(gla_p, w_g2p, b_g2, gnw)
    if loss_head is None:
        x_new, y = _out_fwd(x, gate, r_g, o_mla, mla_p, g_g, w_out_b)
    else:
        dx, y, loss, d_fw = _out_fwd_loss(x, gate, r_g, o_mla, mla_p, g_g, w_out_b, *loss_head)
        x_new = (dx, loss, d_fw)
    saved = dict(x=x, shift=shift, scale=scale, gate=gate, nw=nw, qnw=qnw, kvnw=kvnw, w_g2p=w_g2p, b_g2=b_g2,
                 gnw=gnw, ret_p=ret_p, mla_p=mla_p, gla_p=gla_p, h=h, r_g=r_g, r_raw=r_raw, r_st=r_st, q=q, k=k,
                 v=v, kt=kt, o_mla=o_mla, lse=lse, g_g=g_g, g_raw=g_raw, g_st=g_st, y=y,
                 w_in_a=w_in_a, w_uq_a=w_uq_a, w_ukv_a=w_ukv_a, w_out_b=w_out_b)
    return x_new, saved


def _layer_bwd(dx, s, tabs, early_grads=None):
    B, S, D = dx.shape
    cr, sr, cm, sm = tabs
    d_r, do, d_mz, dl, d_g, dw_out, d_gate = _out_bwd(dx, s["y"], s["gate"], s["r_g"], s["g_g"], s["w_out_b"],
                                                      s["o_mla"], s["mla_p"])
    d_ret = _ret_bwd(s["ret_p"], cr, sr, s["r_raw"], s["r_st"], d_r)
    dq, dk, dv = _mla_attn_bwd(s["q"], s["k"], s["v"], s["kt"], do, s["lse"], dl)
    d_mla, dw_uq, dw_ukv, d_qnw, d_kvnw = _mla_prep_bwd(
        s["mla_p"], cm, sm, s["qnw"], s["kvnw"], s["w_uq_a"], s["w_ukv_a"], dq, dk, dv)
    gnw = s["gnw"] if early_grads is None else s["gnw"] + early_grads(dw_out, dw_uq, dw_ukv)
    d_gla, dw_g2p, db_g2, d_gnw = _gla_bwd(s["gla_p"], s["w_g2p"], s["b_g2"], gnw, s["g_raw"], s["g_st"], d_g)
    dx, d_shift, d_scale, d_nw = _proj_bwd_x(s["x"], s["shift"], s["scale"], s["nw"], s["w_in_a"],
                                             d_ret, d_mla, d_mz, d_gla, dx)
    dw_in = _proj_bwd_w(s["h"], d_ret, d_mla, d_mz, d_gla)
    grads = dict(
        d_mod=jnp.concatenate([d_shift, d_scale, d_gate], axis=2).reshape(B, 3 * D),
        norm_w=d_nw.reshape(D), mla_q_norm=d_qnw.reshape(256), mla_kv_norm=d_kvnw.reshape(128),
        gla_w_g2=dw_g2p[0:16], gla_b_g2=db_g2.reshape(128), gla_norm256=d_gnw.reshape(256),
        w_in_a=dw_in, w_uq_a=dw_uq, w_ukv_a=dw_ukv, w_out=dw_out)
    return dx, grads


def _exchange(arrs, gather, name):
    n = len(arrs)
    out_shape = [jax.ShapeDtypeStruct(((N_DEV,) + a.shape) if g else a.shape, a.dtype)
                 for a, g in zip(arrs, gather)]

    def body(*refs):
        ins, outs = refs[:n], refs[n:2 * n]
        send_sems, recv_sems, local_sems = refs[2 * n:]
        ix, iy, ic = lax.axis_index("x"), lax.axis_index("y"), lax.axis_index("c")
        me = 4 * ix + 2 * iy + ic
        copies = []
        for a in range(n):
            mine = ins[a] if gather[a] else ins[a].at[me]
            loc = pltpu.make_async_copy(mine, outs[a].at[me], local_sems.at[a])
            loc.start()
            copies.append(loc)
            for d in range(1, N_DEV):
                px = 1 - ix if d & 4 else ix
                py = 1 - iy if d & 2 else iy
                pc = 1 - ic if d & 1 else ic
                src = ins[a] if gather[a] else ins[a].at[4 * px + 2 * py + pc]
                cp = pltpu.make_async_remote_copy(
                    src_ref=src, dst_ref=outs[a].at[me], send_sem=send_sems.at[a, d - 1],
                    recv_sem=recv_sems.at[a, d - 1], device_id=(px, py, pc), device_id_type=pl.DeviceIdType.MESH)
                cp.start()
                copies.append(cp)
        for cp in copies:
            cp.wait()

    any_spec = pl.BlockSpec(memory_space=pl.ANY)
    outs = pl.pallas_call(
        body, name=name, in_specs=[any_spec] * n, out_specs=[any_spec] * n, out_shape=out_shape,
        scratch_shapes=[pltpu.SemaphoreType.DMA((n, N_DEV - 1)), pltpu.SemaphoreType.DMA((n, N_DEV - 1)),
                        pltpu.SemaphoreType.DMA((n,))],
    )(*arrs)
    return list(outs)


def _peers(ix, iy, ic):
    out = []
    for d in range(1, N_DEV):
        px = 1 - ix if d & 4 else ix
        py = 1 - iy if d & 2 else iy
        pc = 1 - ic if d & 1 else ic
        out.append((d - 1, (px, py, pc), 4 * px + 2 * py + pc))
    return out


def _exchange_start(arrs, gather, name, after=None):
    n = len(arrs)
    lands = [lax.empty(((N_DEV,) + a.shape) if g else a.shape, a.dtype) for a, g in zip(arrs, gather)]
    extra = [] if after is None else [after]

    def body(*refs):
        ins, land_refs = refs[:n], refs[n:2 * n]
        send_sems, recv_sems = refs[2 * n + len(extra)], refs[2 * n + len(extra) + 1]
        token = refs[-1]
        ix, iy, ic = lax.axis_index("x"), lax.axis_index("y"), lax.axis_index("c")
        me = 4 * ix + 2 * iy + ic
        for a in range(n):
            for k, peer, peer_idx in _peers(ix, iy, ic):
                pltpu.make_async_remote_copy(
                    src_ref=ins[a] if gather[a] else ins[a].at[peer_idx], dst_ref=land_refs[a].at[me],
                    send_sem=send_sems.at[7 * a + k], recv_sem=recv_sems.at[7 * a + k], device_id=peer,
                    device_id_type=pl.DeviceIdType.MESH).start()
        token[...] = jnp.zeros_like(token)

    hbm = pl.BlockSpec(memory_space=pltpu.HBM)
    sem = pl.BlockSpec(memory_space=pltpu.SEMAPHORE)
    held = [pltpu.with_memory_space_constraint(a, pltpu.HBM) for a in list(arrs) + lands]
    outs = pl.pallas_call(
        body, name=name,
        out_shape=(pltpu.SemaphoreType.DMA((7 * n,)), pltpu.SemaphoreType.DMA((7 * n,)),
                   *[pltpu.HBM(a.shape, a.dtype) for a in held], jax.ShapeDtypeStruct((8, 128), F32)),
        in_specs=[hbm] * (2 * n) + [pl.BlockSpec(memory_space=pl.ANY)] * len(extra),
        out_specs=(sem, sem, *[hbm] * (2 * n), pl.BlockSpec(memory_space=pltpu.VMEM)),
        input_output_aliases={a: 2 + a for a in range(2 * n)},
        compiler_params=pltpu.CompilerParams(has_side_effects=pltpu.SideEffectType.DATAFLOW_SIDE_EFFECTING),
    )(*held, *extra)
    return dict(send=outs[0], recv=outs[1], srcs=list(outs[2:2 + n]), lands=list(outs[2 + n:2 + 2 * n]),
                token=outs[-1], gather=list(gather))


def _exchange_wait(flight, after, me, name):
    n = len(flight["srcs"])
    gather = flight["gather"]

    def body(*refs):
        srcs, land_refs = refs[:n], refs[n:2 * n]
        send_sems, recv_sems = refs[2 * n], refs[2 * n + 1]
        ix, iy, ic = lax.axis_index("x"), lax.axis_index("y"), lax.axis_index("c")
        mine = 4 * ix + 2 * iy + ic
        for a in range(n):
            for k, peer, peer_idx in _peers(ix, iy, ic):
                cp = pltpu.make_async_remote_copy(
                    src_ref=srcs[a] if gather[a] else srcs[a].at[peer_idx], dst_ref=land_refs[a].at[mine],
                    send_sem=send_sems.at[7 * a + k], recv_sem=recv_sems.at[7 * a + k], device_id=peer,
                    device_id_type=pl.DeviceIdType.MESH)
                cp.wait_send()
                cp.wait_recv()

    hbm = pl.BlockSpec(memory_space=pltpu.HBM)
    sem = pl.BlockSpec(memory_space=pltpu.SEMAPHORE)
    held = flight["srcs"] + flight["lands"]
    outs = pl.pallas_call(
        body, name=name, out_shape=tuple(pltpu.HBM(a.shape, a.dtype) for a in held),
        in_specs=[hbm] * (2 * n) + [sem, sem, pl.BlockSpec(memory_space=pl.ANY)], out_specs=tuple([hbm] * (2 * n)),
        input_output_aliases={a: a for a in range(2 * n)},
        compiler_params=pltpu.CompilerParams(has_side_effects=pltpu.SideEffectType.DATAFLOW_SIDE_EFFECTING),
    )(*held, flight["send"], flight["recv"], after)
    got = []
    for a in range(n):
        src, land = outs[a], outs[n + a]
        own = src if gather[a] else lax.dynamic_index_in_dim(src, me, axis=0, keepdims=False)
        got.append(lax.dynamic_update_index_in_dim(land, own, me, axis=0))
    return got


def _ada_fwd(c_all, ada_w, ada_b_cols):
    nb, D = c_all.shape
    cols = ada_w.shape[2]

    def body(c_ref, w_ref, b_ref, out_ref):
        ca = _silu(c_ref[...])
        for l in range(DEPTH):
            out_ref[l] = _mm(ca, w_ref[l]) + b_ref[l:l + 1, :]

    return pl.pallas_call(
        body, name="ada_fwd", out_shape=jax.ShapeDtypeStruct((DEPTH, nb, cols), F32),
        in_specs=[pl.BlockSpec(memory_space=pltpu.VMEM)] * 3, out_specs=pl.BlockSpec(memory_space=pltpu.VMEM),
        compiler_params=pltpu.CompilerParams(vmem_limit_bytes=32 * VMEM_MB),
    )(c_all, ada_w, ada_b_cols)


def _ada_bwd(c_all, d_mod_cols):
    nb, D = c_all.shape
    cols = d_mod_cols.shape[2]

    def body(c_ref, dm_ref, out_ref):
        ca = _silu(c_ref[...])
        for l in range(DEPTH):
            out_ref[l] = _mm_tn(ca, dm_ref[l])

    return pl.pallas_call(
        body, name="ada_bwd", out_shape=jax.ShapeDtypeStruct((DEPTH, D, cols), F32),
        in_specs=[pl.BlockSpec(memory_space=pltpu.VMEM)] * 2, out_specs=pl.BlockSpec(memory_space=pltpu.VMEM),
        compiler_params=pltpu.CompilerParams(vmem_limit_bytes=32 * VMEM_MB),
    )(c_all, d_mod_cols)


def _sum_adamw(parts, w, m, v, name, after=None):
    P, R, C = parts.shape
    tr = 256 if (R % 256 == 0 and R > 256) else R
    extra = [] if after is None else [after]

    def body(p_ref, w_ref, m_ref, v_ref, *rest):
        g_ref, d_ref, nm_ref, nv_ref = rest[-4:]
        g = p_ref[0].astype(F32)
        for k in range(1, P):
            g = g + p_ref[k].astype(F32)
        g_ref[...] = g
        nm = ADAM_B1 * m_ref[...] + (1.0 - ADAM_B1) * g
        nv = ADAM_B2 * v_ref[...] + (1.0 - ADAM_B2) * (g * g)
        nm_ref[...] = nm
        nv_ref[...] = nv
        m_hat = nm / (1.0 - ADAM_B1 ** ADAM_STEP)
        v_hat = nv / (1.0 - ADAM_B2 ** ADAM_STEP)
        d_ref[...] = -ADAM_LR * (m_hat / (jnp.sqrt(v_hat) + ADAM_EPS) + ADAM_WD * w_ref[...])

    blk = pl.BlockSpec((tr, C), lambda i: (i, 0))
    shp = jax.ShapeDtypeStruct((R, C), F32)
    return pl.pallas_call(
        body, name=name, grid=(R // tr,),
        in_specs=[pl.BlockSpec((P, tr, C), lambda i: (0, i, 0)), blk, blk, blk]
        + [pl.BlockSpec(memory_space=pl.ANY)] * len(extra),
        out_specs=[blk, blk, blk, blk], out_shape=[shp, shp, shp, shp],
        compiler_params=_cp(("parallel",)),
    )(parts, w, m, v, *extra)


def _sum_adamw_layer(parts, w, m, v, layer, name, prev=None, after=None):
    P, R, C = parts.shape
    tr = 256 if (R % 256 == 0 and R > 256) else R

    def body(p_ref, w_ref, m_ref, v_ref, *rest):
        g_ref, d_ref, nm_ref, nv_ref = rest[-4:]
        g = p_ref[0].astype(F32)
        for k in range(1, P):
            g = g + p_ref[k].astype(F32)
        g_ref[0] = g
        nm = ADAM_B1 * m_ref[0] + (1.0 - ADAM_B1) * g
        nv = ADAM_B2 * v_ref[0] + (1.0 - ADAM_B2) * (g * g)
        nm_ref[0] = nm
        nv_ref[0] = nv
        m_hat = nm / (1.0 - ADAM_B1 ** ADAM_STEP)
        v_hat = nv / (1.0 - ADAM_B2 ** ADAM_STEP)
        d_ref[0] = -ADAM_LR * (m_hat / (jnp.sqrt(v_hat) + ADAM_EPS) + ADAM_WD * w_ref[0])

    blk = pl.BlockSpec((1, tr, C), lambda i: (layer, i, 0))
    shp = jax.ShapeDtypeStruct(w.shape, F32)
    in_specs = [pl.BlockSpec((P, tr, C), lambda i: (0, i, 0)), blk, blk, blk]
    args = [parts, w, m, v]
    aliases = {}
    if prev is not None:
        in_specs += [pl.BlockSpec(memory_space=pl.ANY)] * 4
        args += list(prev)
        aliases = {4 + k: k for k in range(4)}
    if after is not None:
        in_specs.append(pl.BlockSpec(memory_space=pl.ANY))
        args.append(after)
    return list(pl.pallas_call(
        body, name=name, grid=(R // tr,), in_specs=in_specs, out_specs=[blk] * 4, out_shape=[shp] * 4,
        input_output_aliases=aliases, compiler_params=_cp(("parallel",)),
    )(*args))


SMALL = ["norm_w", "mla_q_norm", "mla_kv_norm", "gla_w_g2", "gla_b_g2", "gla_norm", "final_norm"]


SMALL_ROWS = 72


def _pack_small(loss, part):
    flat = [jnp.pad(loss.reshape(1), (0, 127))] + [part[n].reshape(-1) for n in SMALL]
    used = sum(f.shape[0] for f in flat)
    flat.append(jnp.zeros((SMALL_ROWS * 128 - used,), F32))
    return jnp.concatenate(flat).reshape(SMALL_ROWS, 128)


def _small_adamw(packed_parts, w, m, v, after=None):
    n = len(w)
    extra = [] if after is None else [after]

    def body(*refs):
        p_ref = refs[0]
        w_refs, m_refs, v_refs = refs[1:1 + n], refs[1 + n:1 + 2 * n], refs[1 + 2 * n:1 + 3 * n]
        outs, acc = refs[1 + 3 * n + len(extra):-1], refs[-1]
        total = p_ref[0]
        for k in range(1, N_DEV):
            total = total + p_ref[k]
        acc[...] = total
        outs[0][...] = acc[0:1, :]
        r0 = 1
        for i in range(n):
            shp = w_refs[i].shape
            if len(shp) == 3:
                g = acc[r0:r0 + shp[0] * shp[1], :].reshape(shp)
                r0 += shp[0] * shp[1]
            elif shp[1] < 128:
                g = acc[r0:r0 + shp[0], 0:shp[1]]
                r0 += shp[0]
            else:
                k = shp[1] // 128
                g = jnp.concatenate(
                    [jnp.concatenate([acc[r0 + l * k + j:r0 + l * k + j + 1, :] for j in range(k)], axis=1)
                     for l in range(shp[0])], axis=0)
                r0 += shp[0] * k
            nm = ADAM_B1 * m_refs[i][...] + (1.0 - ADAM_B1) * g
            nv = ADAM_B2 * v_refs[i][...] + (1.0 - ADAM_B2) * (g * g)
            m_hat = nm / (1.0 - ADAM_B1 ** ADAM_STEP)
            v_hat = nv / (1.0 - ADAM_B2 ** ADAM_STEP)
            outs[1 + 4 * i][...] = g
            outs[2 + 4 * i][...] = -ADAM_LR * (m_hat / (jnp.sqrt(v_hat) + ADAM_EPS) + ADAM_WD * w_refs[i][...])
            outs[3 + 4 * i][...] = nm
            outs[4 + 4 * i][...] = nv

    vmem = pl.BlockSpec(memory_space=pltpu.VMEM)
    out_shape = [jax.ShapeDtypeStruct((1, 128), F32)]
    for a in w:
        out_shape += [jax.ShapeDtypeStruct(a.shape, F32)] * 4
    outs = pl.pallas_call(
        body, name="adamw_small", in_specs=[vmem] * (1 + 3 * n) + [pl.BlockSpec(memory_space=pl.ANY)] * len(extra),
        out_specs=[vmem] * (1 + 4 * n), out_shape=out_shape, scratch_shapes=[pltpu.VMEM((SMALL_ROWS, 128), F32)],
    )(packed_parts, *w, *m, *v, *extra)
    return outs[0], [outs[1 + 4 * i:5 + 4 * i] for i in range(n)]


WEIGHTS = ["norm_w", "ada_w", "ada_b", "w_in", "mla_q_norm", "w_uq", "mla_kv_norm", "w_ukv", "gla_w_g2",
           "gla_b_g2", "gla_norm", "w_out", "final_norm"]


def kernel(x, c, positions, norm_w, ada_w, ada_b, w_in, mla_q_norm, w_uq, mla_kv_norm, w_ukv, gla_w_g2, gla_b_g2, gla_norm, w_out, final_norm, loss_target, m_norm_w, m_ada_w, m_ada_b, m_w_in, m_mla_q_norm, m_w_uq, m_mla_kv_norm, m_w_ukv, m_gla_w_g2, m_gla_b_g2, m_gla_norm, m_w_out, m_final_norm, v_norm_w, v_ada_w, v_ada_b, v_w_in, v_mla_q_norm, v_w_uq, v_mla_kv_norm, v_w_ukv, v_gla_w_g2, v_gla_b_g2, v_gla_norm, v_w_out, v_final_norm):
    w = dict(norm_w=norm_w, ada_w=ada_w, ada_b=ada_b, w_in=w_in, mla_q_norm=mla_q_norm, w_uq=w_uq,
             mla_kv_norm=mla_kv_norm, w_ukv=w_ukv, gla_w_g2=gla_w_g2, gla_b_g2=gla_b_g2, gla_norm=gla_norm,
             w_out=w_out, final_norm=final_norm)
    m = dict(norm_w=m_norm_w, ada_w=m_ada_w, ada_b=m_ada_b, w_in=m_w_in, mla_q_norm=m_mla_q_norm, w_uq=m_w_uq,
             mla_kv_norm=m_mla_kv_norm, w_ukv=m_w_ukv, gla_w_g2=m_gla_w_g2, gla_b_g2=m_gla_b_g2,
             gla_norm=m_gla_norm, w_out=m_w_out, final_norm=m_final_norm)
    v = dict(norm_w=v_norm_w, ada_w=v_ada_w, ada_b=v_ada_b, w_in=v_w_in, mla_q_norm=v_mla_q_norm, w_uq=v_w_uq,
             mla_kv_norm=v_mla_kv_norm, w_ukv=v_w_ukv, gla_w_g2=v_gla_w_g2, gla_b_g2=v_gla_b_g2,
             gla_norm=v_gla_norm, w_out=v_w_out, final_norm=v_final_norm)
    B, S, D = x.shape
    me = 4 * lax.axis_index("x") + 2 * lax.axis_index("y") + lax.axis_index("c")
    ada_cols = ada_w.shape[2]
    cast = lambda a: a.astype(_MXU)

    sharded = ["w_in", "w_uq", "w_ukv", "w_out"]

    whole_in = _arrange_w_in
    whole_rest = lambda blks: (*_arrange_mla_weights(blks[0], blks[1]), blks[2].reshape(D, D))
    blocks_in = lambda dw_in_a: _unarrange_w_in(dw_in_a, N_DEV, w_in.shape[2])
    blocks_rest = lambda dw_out, dw_uq_a, dw_ukv_a: [
        *_unarrange_mla_weights(dw_uq_a, dw_ukv_a), dw_out.reshape(N_DEV, D // N_DEV, D).astype(jnp.bfloat16)]

    (c_g,) = _exchange([c], [True], "gather_c")
    c_all = c_g.reshape(N_DEV * B, D)

    ada_b_cols = lax.dynamic_slice(ada_b, (0, me * ada_cols), (DEPTH, ada_cols))
    mod_cols = _ada_fwd(c_all, ada_w, ada_b_cols)
    mod_send = jnp.transpose(mod_cols.reshape(DEPTH, N_DEV, B, ada_cols), (1, 0, 2, 3))
    (mod_recv,) = _exchange([mod_send], [False], "scatter_mod")
    mod = jnp.transpose(mod_recv, (1, 2, 0, 3)).reshape(DEPTH, B, 3 * D)

    flight_i = _exchange_start([cast(w_in[0])], [True], "gather_start_first", after=mod)
    flight_r = _exchange_start([cast(w[n][0]) for n in sharded[1:]], [True] * 3, "gather_start_layer0",
                               after=flight_i["token"])
    flight_w = _exchange_start([cast(w[n][1]) for n in sharded], [True] * 4, "gather_start_layer1",
                               after=flight_r["token"])
    small_w = {n: w[n] for n in SMALL}
    layer_small = lambda l: {n: a[l] for n, a in small_w.items() if n != "final_norm"}
    tabs = _rope_tables(positions.reshape(B, S, 1), flight_w["token"][0, 0])
    late0 = lambda after: whole_rest(_exchange_wait(flight_r, after, me, "gather_wait_layer0"))
    (w_in0_g,) = _exchange_wait(flight_i, tabs[0], me, "gather_wait_first")
    x1, saved0 = _layer_fwd(x, tabs, mod[0], layer_small(0), whole_in(w_in0_g), late_weights=late0)
    got1 = _exchange_wait(flight_w, x1, me, "gather_wait_layer1")
    (dx, loss, d_fw), saved1 = _layer_fwd(x1, tabs, mod[1], layer_small(1), whole_in(got1[0]), *whole_rest(got1[1:]),
                                          loss_head=(final_norm.reshape(1, D), loss_target))

    dx, g1 = _layer_bwd(dx, saved1, tabs)
    flight_g = _exchange_start([blocks_in(g1["w_in_a"])] + blocks_rest(g1["w_out"], g1["w_uq_a"], g1["w_ukv_a"]),
                               [False] * 4, "grads_start_layer1")
    flights = {}

    def early0(dw_out, dw_uq_a, dw_ukv_a):
        flights["rest0"] = _exchange_start(blocks_rest(dw_out, dw_uq_a, dw_ukv_a), [False] * 3, "grads_start_layer0")
        return flights["rest0"]["token"][0, 0]

    saved0 = dict(saved0, gate=saved0["gate"] + flight_g["token"][0, 0])
    grad_x, g0 = _layer_bwd(dx, saved0, tabs, early_grads=early0)
    parts1 = _exchange_wait(flight_g, grad_x, me, "grads_wait_layer1")
    rest0 = _exchange_wait(flights["rest0"], g0["w_in_a"], me, "grads_wait_layer0")

    both = lambda n: jnp.stack([g0[n], g1[n]])
    d_mod = both("d_mod")
    part = dict(norm_w=both("norm_w"), mla_q_norm=both("mla_q_norm"), mla_kv_norm=both("mla_kv_norm"),
                gla_w_g2=both("gla_w_g2"), gla_b_g2=both("gla_b_g2"), gla_norm=both("gla_norm256")[:, 0:128],
                final_norm=d_fw)
    d_mod_g, small_g = _exchange([d_mod, _pack_small(loss, part)], [True, True], "gather_small")
    flight_l = _exchange_start([blocks_in(g0["w_in_a"])], [False], "exchange_start_last", after=small_g)
    res = {}
    behind = flight_l["token"]
    for a, name in enumerate(sharded):
        res[name] = _sum_adamw_layer(parts1[a], w[name], m[name], v[name], 1, "adamw_%s_layer1" % name, after=behind)
        behind = res[name][1]
    for a, name in enumerate(sharded[1:]):
        res[name] = _sum_adamw_layer(rest0[a], w[name], m[name], v[name], 0, "adamw_%s_layer0" % name,
                                     prev=res[name], after=behind)
        behind = res[name][1]

    d_mod_all = jnp.transpose(d_mod_g, (1, 0, 2, 3)).reshape(DEPTH, N_DEV * B, 3 * D)
    d_mod_cols = lax.dynamic_slice(d_mod_all, (0, 0, me * ada_cols), (DEPTH, N_DEV * B, ada_cols))
    g_ada_w = _ada_bwd(c_all, d_mod_cols)

    def update(name, parts2d, after):
        shp = w[name].shape
        two = lambda a: a.reshape(parts2d.shape[1:])
        out = _sum_adamw(parts2d, two(w[name]), two(m[name]), two(v[name]), "adamw_" + name, after=after)
        res[name] = [o.reshape(shp) for o in out]
        return out[1]

    behind = update("ada_w", g_ada_w.reshape(1, DEPTH * D, ada_cols), behind)
    behind = update("ada_b", jnp.transpose(d_mod_g, (0, 2, 1, 3)).reshape(N_DEV * B, DEPTH * 3 * D // 128, 128), behind)
    row = lambda a: a.reshape(1, D) if a.ndim == 1 else a
    loss_sum, small_out = _small_adamw(small_g, [row(w[n]) for n in SMALL], [row(m[n]) for n in SMALL],
                                       [row(v[n]) for n in SMALL], after=behind)
    for n, outs in zip(SMALL, small_out):
        res[n] = [o.reshape(w[n].shape) for o in outs]
    loss_out = loss_sum[0, 0]
    (in0,) = _exchange_wait(flight_l, loss_sum, me, "exchange_wait_last")
    res["w_in"] = _sum_adamw_layer(in0, w_in, m_w_in, v_w_in, 0, "adamw_w_in_layer0", prev=res["w_in"])
    return (loss_out, grad_x, *[res[n][0] for n in WEIGHTS], *[res[n][1] for n in WEIGHTS],
            *[res[n][2] for n in WEIGHTS], *[res[n][3] for n in WEIGHTS])
```

```python
import functools
import math

import numpy as np
import jax
import jax.numpy as jnp
from jax import lax
from jax.experimental import pallas as pl
from jax.experimental.pallas import tpu as pltpu

F32 = jnp.float32
_MXU = jnp.bfloat16

D_MODEL = 1024
DEPTH = 2
CHUNK = 64
EPS = 1e-6
ROPE_THETA = 10000.0
N_DEV = 8

MLA_SCALE = 96.0 ** -0.5
RET_KSCALE = 64.0 ** -0.5
GLA_KSCALE = 32.0 ** -0.5
GLA_TAU = 16.0

ADAM_LR = 0.001
ADAM_B1 = 0.9
ADAM_B2 = 0.999
ADAM_EPS = 1e-08
ADAM_WD = 0.01
ADAM_STEP = 10

RET_W, MLA_W, GLA_W = 1024, 1024, 896
ARR_W = RET_W + MLA_W + GLA_W
VMEM_MB = 1024 * 1024


def _cp(sem, vmem_mb=48):
    return pltpu.CompilerParams(dimension_semantics=sem, vmem_limit_bytes=vmem_mb * VMEM_MB)


def _mm(a, b):
    return jnp.dot(a.astype(_MXU), b.astype(_MXU), preferred_element_type=F32)


def _mm_nt(a, b):
    return lax.dot_general(a.astype(_MXU), b.astype(_MXU), (((1,), (1,)), ((), ())),
                           preferred_element_type=F32)


def _mm_tn(a, b):
    return lax.dot_general(a.astype(_MXU), b.astype(_MXU), (((0,), (0,)), ((), ())),
                           preferred_element_type=F32)


def _mm_f32(a, b):
    return jnp.dot(a, b, precision=lax.Precision.HIGHEST, preferred_element_type=F32)


def _sig(z):
    return 1.0 / (1.0 + jnp.exp(-z))


def _silu(z):
    return z * _sig(z)


def _dsilu(z):
    s = _sig(z)
    return s * (1.0 + z * (1.0 - s))


def _full(shape):
    nd = len(shape)
    return pl.BlockSpec(shape, lambda *_: (0,) * nd)


def _w_in_runs(block_cols):
    m, g = RET_W, RET_W + MLA_W
    whole = [(base + 64 * h + 32 * t, 32, base + 128 * t + 32 * h)
             for base in (0, 256) for t in range(2) for h in range(4)]
    whole += [(512, 512, 512), (1024, 384, m), (1408, 32, m + 448), (1440, 512, m + 512),
              (1952, 528, g), (2480, 256, g + 640)]
    zeros = [(m + 384, 64), (m + 480, 32), (g + 528, 112)]
    runs = []
    for src, n, dst in whole:
        while n:
            blk, off = divmod(src, block_cols)
            k = min(n, block_cols - off)
            runs.append((blk, off, k, dst))
            src, n, dst = src + k, n - k, dst + k
    return runs, zeros


def _arrange_w_in(blocks, tm=256):
    n, rows, cols = blocks.shape
    runs, zeros = _w_in_runs(cols)

    def arrange_w_in_kernel(b_ref, a_ref):
        for dst, k in zeros:
            a_ref[:, dst:dst + k] = jnp.zeros((tm, k), a_ref.dtype)
        for blk, off, k, dst in runs:
            a_ref[:, dst:dst + k] = b_ref[blk, :, off:off + k]

    return pl.pallas_call(
        arrange_w_in_kernel, grid=(rows // tm,),
        in_specs=[pl.BlockSpec((n, tm, cols), lambda i: (0, i, 0))],
        out_specs=pl.BlockSpec((tm, ARR_W), lambda i: (i, 0)),
        out_shape=jax.ShapeDtypeStruct((rows, ARR_W), blocks.dtype),
        compiler_params=_cp(("parallel",)), name="arrange_w_in")(blocks)


def _unarrange_w_in(a, n, cols, tm=256):
    rows = a.shape[0]
    runs, _ = _w_in_runs(cols)

    def unarrange_w_in_kernel(a_ref, b_ref):
        for blk, off, k, dst in runs:
            b_ref[blk, :, off:off + k] = a_ref[:, dst:dst + k].astype(b_ref.dtype)

    return pl.pallas_call(
        unarrange_w_in_kernel, grid=(rows // tm,),
        in_specs=[pl.BlockSpec((tm, ARR_W), lambda i: (i, 0))],
        out_specs=pl.BlockSpec((n, tm, cols), lambda i: (0, i, 0)),
        out_shape=jax.ShapeDtypeStruct((n, rows, cols), jnp.bfloat16),
        compiler_params=_cp(("parallel",)), name="unarrange_w_in")(a)


def _arrange_mla_weights(uq_heads, ukv_heads):
    nh = uq_heads.shape[0]
    dt = uq_heads.dtype

    def arrange_mla_weights_kernel(uq_ref, ukv_ref, q_ref, kv_ref, kvt_ref):
        q_ref[...] = jnp.zeros(q_ref.shape, dt)
        kv_ref[...] = jnp.zeros(kv_ref.shape, dt)
        kvt_ref[...] = jnp.zeros(kvt_ref.shape, dt)
        for h in range(nh):
            q_ref[:, 128 * h:128 * h + 96] = uq_ref[h]
            blk = ukv_ref[h]
            kv_ref[:, 128 * h:128 * h + 64] = blk[:, 0:64]
            kv_ref[:, 128 * nh + 64 * h:128 * nh + 64 * h + 64] = blk[:, 64:128]
            blk_t = blk.astype(F32).T.astype(dt)
            kvt_ref[128 * h:128 * h + 64, :] = blk_t[0:64]
            kvt_ref[128 * nh + 64 * h:128 * nh + 64 * h + 64, :] = blk_t[64:128]

    return pl.pallas_call(
        arrange_mla_weights_kernel, name="arrange_mla_weights",
        out_shape=[jax.ShapeDtypeStruct((256, 128 * nh), dt), jax.ShapeDtypeStruct((128, 192 * nh), dt),
                   jax.ShapeDtypeStruct((192 * nh, 128), dt)])(uq_heads, ukv_heads)


def _unarrange_mla_weights(dw_uq_a, dw_ukv_a):
    nh = dw_uq_a.shape[1] // 128

    def unarrange_mla_weights_kernel(q_ref, kv_ref, uq_ref, ukv_ref):
        for h in range(nh):
            uq_ref[h] = q_ref[:, 128 * h:128 * h + 96].astype(uq_ref.dtype)
            ukv_ref[h, :, 0:64] = kv_ref[:, 128 * h:128 * h + 64].astype(ukv_ref.dtype)
            ukv_ref[h, :, 64:128] = kv_ref[:, 128 * nh + 64 * h:128 * nh + 64 * h + 64].astype(ukv_ref.dtype)

    return pl.pallas_call(
        unarrange_mla_weights_kernel, name="unarrange_mla_weights",
        out_shape=[jax.ShapeDtypeStruct((nh, 256, 96), jnp.bfloat16),
                   jax.ShapeDtypeStruct((nh, 128, 128), jnp.bfloat16)])(dw_uq_a, dw_ukv_a)


def _rope_tables(pos3, zero=0.0):
    B, S, _ = pos3.shape
    ts = min(S, 512)
    inv32 = (np.float32(ROPE_THETA) ** (-(np.arange(32, dtype=np.float32) / 32))).astype(np.float32)
    inv16 = (np.float32(ROPE_THETA) ** (-(np.arange(16, dtype=np.float32) / 16))).astype(np.float32)
    inv = np.zeros((1, 128), np.float32)
    inv[0, 0:32] = inv32
    inv[0, 32:48] = inv16

    def body(pos_ref, inv_ref, cr, sr, cm, sm):
        ang = pos_ref[0].astype(F32) * inv_ref[...]
        lane = lax.broadcasted_iota(jnp.int32, (1, 128), 1)

        def every_head(x):
            y = jnp.where(lane < 32, x, pltpu.roll(x, 32, 1))
            return jnp.where(lane < 64, y, pltpu.roll(y, 64, 1))

        def rotary_pair(x, fill):
            return jnp.where((lane >= 64) & (lane < 80), pltpu.roll(x, 32, 1),
                             jnp.where((lane >= 80) & (lane < 96), pltpu.roll(x, 48, 1), fill))

        c, s = jnp.cos(ang), jnp.sin(ang)
        cr[0] = every_head(c)
        sr[0] = every_head(s)
        cm[0] = rotary_pair(c, 1.0)
        sm[0] = rotary_pair(s, 0.0)

    tab = jax.ShapeDtypeStruct((B, S, 128), F32)
    blk = pl.BlockSpec((1, ts, 128), lambda b, i: (b, i, 0))
    return pl.pallas_call(
        body, name="rope_tables", grid=(B, S // ts),
        in_specs=[pl.BlockSpec((1, ts, 1), lambda b, i: (b, i, 0)), _full((1, 128))],
        out_specs=[blk, blk, blk, blk], out_shape=[tab, tab, tab, tab],
        compiler_params=_cp(("parallel", "parallel")),
    )(pos3, jnp.asarray(inv) + zero)


def _rope128(x, cos, sin):
    lane = lax.broadcasted_iota(jnp.int32, (1, 128), 1)
    rp = pltpu.roll(x, 16, 1)
    rm = pltpu.roll(x, 112, 1)
    return x * cos + jnp.where(lane < 80, -rm, rp) * sin


def _rope128_t(d, cos, sin):
    lane = lax.broadcasted_iota(jnp.int32, (1, 128), 1)
    y = d * sin
    yp = pltpu.roll(y, 16, 1)
    ym = pltpu.roll(y, 112, 1)
    return d * cos + jnp.where(lane < 64, 0.0, jnp.where(lane < 80, ym, jnp.where(lane < 96, -yp, 0.0)))


def _proj_fwd(x, shift, scale, nw, w_arr):
    B, S, D = x.shape
    tm = min(S, 512)

    def body(x_ref, sh_ref, sc_ref, nw_ref, w_ref, ret_ref, mla_ref, gla_ref, h_ref):
        xv = x_ref[0]
        rstd = lax.rsqrt(jnp.mean(xv * xv, axis=-1, keepdims=True) + EPS)
        h = (xv * rstd * nw_ref[...]) * (1.0 + sc_ref[0]) + sh_ref[0]
        hb = h.astype(_MXU)
        h_ref[0] = hb
        ret_ref[0] = jnp.dot(hb, w_ref[:, 0:RET_W], preferred_element_type=F32).astype(_MXU)
        mla_ref[0] = jnp.dot(hb, w_ref[:, RET_W:RET_W + MLA_W], preferred_element_type=F32).astype(_MXU)
        gla_ref[0] = jnp.dot(hb, w_ref[:, RET_W + MLA_W:ARR_W], preferred_element_type=F32).astype(_MXU)

    tok = lambda w: pl.BlockSpec((1, tm, w), lambda b, i: (b, i, 0))
    per_seq = pl.BlockSpec((1, 1, D), lambda b, i: (b, 0, 0))
    return pl.pallas_call(
        body, name="proj_fwd", grid=(B, S // tm),
        in_specs=[tok(D), per_seq, per_seq, _full((1, D)), _full((D, ARR_W))],
        out_specs=[tok(RET_W), tok(MLA_W), tok(GLA_W), tok(D)],
        out_shape=[jax.ShapeDtypeStruct((B, S, RET_W), _MXU), jax.ShapeDtypeStruct((B, S, MLA_W), _MXU),
                   jax.ShapeDtypeStruct((B, S, GLA_W), _MXU), jax.ShapeDtypeStruct((B, S, D), _MXU)],
        compiler_params=_cp(("parallel", "parallel")),
    )(x, shift, scale, nw, w_arr)


RET_L = 256


def _ret_consts(L):
    lg = np.log1p(-np.exp2(-5.0 - np.arange(4, dtype=np.float32))).astype(np.float32)
    i = np.arange(L)
    ci = i // CHUNK
    diff = (i[:, None] - i[None, :]).astype(np.float32)
    same = ci[:, None] == ci[None, :]
    past = ci[None, :] < ci[:, None]
    expo = np.where(same, np.abs(diff), np.where(past, diff, 0.0)).astype(np.float32)
    dec = np.where((same | past)[None], np.exp(lg[:, None, None] * expo[None]), 0.0).astype(np.float32)
    head = (np.arange(256) % 128) // 32
    qw = np.exp((i + 1.0)[:, None] * lg[head][None, :]).astype(np.float32)
    kw = np.exp((L - 1.0 - i)[:, None] * lg[head][None, :]).astype(np.float32)
    a_row = np.exp(np.float32(L) * lg[head])[None, :].astype(np.float32)
    return [jnp.asarray(t) for t in (dec.reshape(4 * L, L), qw, kw, a_row)]


def _ret_masks():
    lane = lax.broadcasted_iota(jnp.int32, (1, 256), 1)
    mh = [((lane % 128) // 32) == h for h in range(4)]
    mv = [(lane // 64) == h for h in range(4)]
    vi = lax.broadcasted_iota(jnp.int32, (256, 256), 0)
    ki = lax.broadcasted_iota(jnp.int32, (256, 256), 1)
    bd = (vi // 64) == ((ki % 128) // 32)
    return mh, mv, bd


def _ret_rope(p, cs, sn):
    q1, q2, k1, k2 = p[:, 0:128], p[:, 128:256], p[:, 256:384], p[:, 384:512]
    qr = jnp.concatenate([q1 * cs - q2 * sn, q2 * cs + q1 * sn], axis=1)
    kr = jnp.concatenate([k1 * cs - k2 * sn, k2 * cs + k1 * sn], axis=1) * RET_KSCALE
    return qr, kr


def _head_mean(x, mv, width):
    out = jnp.zeros_like(x)
    for m in mv:
        s = jnp.sum(jnp.where(m, x, 0.0), axis=-1, keepdims=True) * (1.0 / width)
        out = jnp.where(m, s, out)
    return out


def _stack_heads(x, masks):
    return jnp.concatenate([jnp.where(m, x, 0.0) for m in masks], axis=0)


def _fold_heads(xs, masks, L):
    out = jnp.where(masks[0], xs[0:L], 0.0)
    for h in range(1, 4):
        out = out + jnp.where(masks[h], xs[h * L:(h + 1) * L], 0.0)
    return out


RET_G = 2


def _ret_fwd(ret_p, cos, sin):
    B, S, _ = ret_p.shape
    L = min(RET_L, S)
    NB = S // L
    G = min(RET_G, NB)
    NG = NB // G
    consts = _ret_consts(L)

    def body(p_ref, c_ref, s_ref, ds_ref, qw_ref, kw_ref, a_ref, out_ref, raw_ref, st_ref, st_sc):
        @pl.when(pl.program_id(1) == 0)
        def _():
            st_sc[...] = jnp.zeros_like(st_sc)

        mh, mv, bd = _ret_masks()
        cs_ = range(G)
        rows = [slice(c * L, (c + 1) * L) for c in cs_]
        ps = [p_ref[0, rows[c], :].astype(F32) for c in cs_]
        qk = [_ret_rope(ps[c], c_ref[0, rows[c], :], s_ref[0, rows[c], :]) for c in cs_]
        vs = [ps[c][:, 512:768] for c in cs_]
        a_s = [_mm_nt(_stack_heads(qk[c][0], mh), qk[c][1]) for c in cs_]
        upd = [_mm_tn(vs[c], qk[c][1] * kw_ref[...]) for c in cs_]
        o_s = [_mm(a_s[c] * ds_ref[...], vs[c]) for c in cs_]
        st = st_sc[...]
        inter = []
        for c in cs_:
            st_ref[0, c] = st
            inter.append(_mm_nt(qk[c][0] * qw_ref[...], st))
            st = st * a_ref[...] + jnp.where(bd, upd[c], 0.0)
        st_sc[...] = st
        for c in cs_:
            r = _fold_heads(o_s[c], mv, L) + inter[c]
            raw_ref[0, rows[c], :] = r
            rstd = lax.rsqrt(_head_mean(r * r, mv, 64.0) + EPS)
            out_ref[0, rows[c], :] = (r * rstd * _silu(ps[c][:, 768:1024])).astype(_MXU)

    tok = lambda w: pl.BlockSpec((1, G * L, w), lambda b, n: (b, n, 0))
    return pl.pallas_call(
        body, name="ret_fwd", grid=(B, NG),
        in_specs=[tok(RET_W), tok(128), tok(128), _full((4 * L, L)), _full((L, 256)), _full((L, 256)),
                  _full((1, 256))],
        out_specs=[tok(256), tok(256), pl.BlockSpec((1, G, 256, 256), lambda b, n: (b, n, 0, 0))],
        out_shape=[jax.ShapeDtypeStruct((B, S, 256), _MXU), jax.ShapeDtypeStruct((B, S, 256), F32),
                   jax.ShapeDtypeStruct((B, NB, 256, 256), F32)],
        scratch_shapes=[pltpu.VMEM((256, 256), F32)],
        compiler_params=_cp(("parallel", "arbitrary")),
    )(ret_p, cos, sin, *consts)


def _ret_bwd(ret_p, cos, sin, raw, states, d_mix):
    B, S, _ = ret_p.shape
    L = min(RET_L, S)
    NB = S // L
    G = 1
    NG = NB // G
    consts = _ret_consts(L)

    def body(p_ref, c_ref, s_ref, raw_ref, st_ref, dm_ref, ds_ref, qw_ref, kw_ref, a_ref, dp_ref, dst_sc):
        @pl.when(pl.program_id(1) == 0)
        def _():
            dst_sc[...] = jnp.zeros_like(dst_sc)

        mh, mv, bd = _ret_masks()
        qw, kw, dec = qw_ref[...], kw_ref[...], ds_ref[...]
        cs_ = range(G)
        rows = [slice(c * L, (c + 1) * L) for c in cs_]
        ps = [p_ref[0, rows[c], :].astype(F32) for c in cs_]
        tabs = [(c_ref[0, rows[c], :], s_ref[0, rows[c], :]) for c in cs_]
        qk = [_ret_rope(ps[c], *tabs[c]) for c in cs_]
        vs = [ps[c][:, 512:768] for c in cs_]
        qs = [_stack_heads(qk[c][0], mh) for c in cs_]
        a_s = [_mm_nt(qs[c], qk[c][1]) for c in cs_]
        dr, dz = [], []
        for c in cs_:
            r = raw_ref[0, rows[c], :]
            z = ps[c][:, 768:1024]
            rstd = lax.rsqrt(_head_mean(r * r, mv, 64.0) + EPS)
            rn = r * rstd
            dm = dm_ref[0, rows[c], :]
            d_rn = dm * _silu(z)
            dz.append(dm * rn * _dsilu(z))
            dr.append(rstd * (d_rn - rn * _head_mean(d_rn * rn, mv, 64.0)))
        do_s = [_stack_heads(dr[c], mv) for c in cs_]
        da_s = [_mm_nt(do_s[c], vs[c]) for c in cs_]
        sts = [st_ref[0, c] for c in cs_]
        dq_st = [_mm(dr[c], sts[c]) for c in cs_]
        dst_in = [_mm_tn(dr[c], qk[c][0] * qw) for c in cs_]
        dv = [_mm_tn(a_s[c] * dec, do_s[c]) for c in cs_]
        dqr, dkr = [], []
        for c in cs_:
            da = da_s[c] * dec
            dqr.append(_fold_heads(_mm(da, qk[c][1]), mh, L) + dq_st[c] * qw)
            dkr.append(_mm_tn(da, qs[c]))
        dst_next = dst_sc[...]
        for c in reversed(cs_):
            g = jnp.where(bd, dst_next, 0.0)
            dv[c] = dv[c] + _mm_nt(qk[c][1] * kw, g)
            dkr[c] = dkr[c] + _mm(vs[c], g) * kw
            dst_next = dst_next * a_ref[...] + jnp.where(bd, dst_in[c], 0.0)
        dst_sc[...] = dst_next
        for c in cs_:
            cs, sn = tabs[c]
            dk = dkr[c] * RET_KSCALE
            dq1, dq2 = dqr[c][:, 0:128], dqr[c][:, 128:256]
            dk1, dk2 = dk[:, 0:128], dk[:, 128:256]
            dp_ref[0, rows[c], :] = jnp.concatenate(
                [dq1 * cs + dq2 * sn, dq2 * cs - dq1 * sn, dk1 * cs + dk2 * sn, dk2 * cs - dk1 * sn, dv[c], dz[c]],
                axis=1).astype(_MXU)

    tok = lambda w: pl.BlockSpec((1, G * L, w), lambda b, i: (b, NG - 1 - i, 0))
    return pl.pallas_call(
        body, name="ret_bwd", grid=(B, NG),
        in_specs=[tok(RET_W), tok(128), tok(128), tok(256),
                  pl.BlockSpec((1, G, 256, 256), lambda b, i: (b, NG - 1 - i, 0, 0)), tok(256),
                  _full((4 * L, L)), _full((L, 256)), _full((L, 256)), _full((1, 256))],
        out_specs=tok(RET_W), out_shape=jax.ShapeDtypeStruct((B, S, RET_W), _MXU),
        scratch_shapes=[pltpu.VMEM((256, 256), F32)],
        compiler_params=_cp(("parallel", "arbitrary")),
    )(ret_p, cos, sin, raw, states, d_mix, *consts)


def _gla_masks():
    C = CHUNK
    lk = lax.broadcasted_iota(jnp.int32, (1, 128), 1)
    lv = lax.broadcasted_iota(jnp.int32, (1, 256), 1)
    mk = [(lk // 32) == h for h in range(4)]
    mv = [(lv // 64) == h for h in range(4)]
    vi = lax.broadcasted_iota(jnp.int32, (256, 128), 0)
    ki = lax.broadcasted_iota(jnp.int32, (256, 128), 1)
    bd = (vi // 64) == (ki // 32)
    ri = lax.broadcasted_iota(jnp.int32, (4 * C, C), 0) % C
    cj = lax.broadcasted_iota(jnp.int32, (4 * C, C), 1)
    lower = ri >= cj
    ti = lax.broadcasted_iota(jnp.int32, (C, C), 0)
    tj = lax.broadcasted_iota(jnp.int32, (C, C), 1)
    ltri = jnp.where(ti >= tj, 1.0, 0.0).astype(F32)
    utri = jnp.where(ti <= tj, 1.0, 0.0).astype(F32)
    return mk, mv, bd, lower, ltri, utri


def _log_sigmoid(x):
    return jnp.minimum(x, 0.0) - jnp.log(1.0 + jnp.exp(-jnp.abs(x)))


GLA_G = 8


def _gla_fwd(gla_p, w_g2p, b_g2, gnw):
    B, S, _ = gla_p.shape
    C = CHUNK
    NC = S // C
    G = min(GLA_G, NC)
    NG = NC // G

    def body(p_ref, w_ref, b_ref, gn_ref, out_ref, raw_ref, st_ref, st_sc):
        @pl.when(pl.program_id(1) == 0)
        def _():
            st_sc[...] = jnp.zeros_like(st_sc)

        mk, mv, bd, lower, ltri, _ = _gla_masks()
        cs = range(G)
        rows = [slice(c * C, (c + 1) * C) for c in cs]
        ps = [p_ref[0, rows[c], :].astype(F32) for c in cs]
        pre = [_mm(ps[c][:, 512:640], w_ref[...]) + b_ref[...] for c in cs]
        cum = [_mm_f32(ltri, _log_sigmoid(pre[c]) * (1.0 / GLA_TAU)) for c in cs]
        past, fut, upd, q_pos, a_row = [], [], [], [], []
        for c in cs:
            q = ps[c][:, 0:128]
            k = ps[c][:, 128:256] * GLA_KSCALE
            last = cum[c][C - 1:C, :]
            e_pos = jnp.exp(cum[c])
            e_neg = jnp.exp(-cum[c])
            q_pos.append(q * e_pos)
            a_row.append(jnp.exp(last))
            past.append(_mm_nt(_stack_heads(q_pos[c], mk), k * e_neg))
            fut.append(_mm_nt(_stack_heads(q * e_neg, mk), k * e_pos))
            upd.append(_mm_tn(ps[c][:, 256:512], k * jnp.exp(last - cum[c])))
        o_s = [_mm(jnp.where(lower, past[c], fut[c]), ps[c][:, 256:512]) for c in cs]
        st = st_sc[...]
        inter = []
        for c in cs:
            st_ref[0, c] = st
            inter.append(_mm_nt(q_pos[c], st))
            st = st * a_row[c] + jnp.where(bd, upd[c], 0.0)
        st_sc[...] = st
        for c in cs:
            g = _fold_heads(o_s[c], mv, C) + inter[c]
            raw_ref[0, rows[c], :] = g
            rstd = lax.rsqrt(_head_mean(g * g, mv, 64.0) + EPS)
            out_ref[0, rows[c], :] = (g * rstd * gn_ref[...] * _silu(ps[c][:, 640:896])).astype(_MXU)

    tok = lambda w: pl.BlockSpec((1, G * C, w), lambda b, n: (b, n, 0))
    return pl.pallas_call(
        body, name="gla_fwd", grid=(B, NG),
        in_specs=[tok(GLA_W), _full((128, 128)), _full((1, 128)), _full((1, 256))],
        out_specs=[tok(256), tok(256), pl.BlockSpec((1, G, 256, 128), lambda b, n: (b, n, 0, 0))],
        out_shape=[jax.ShapeDtypeStruct((B, S, 256), _MXU), jax.ShapeDtypeStruct((B, S, 256), F32),
                   jax.ShapeDtypeStruct((B, NC, 256, 128), F32)],
        scratch_shapes=[pltpu.VMEM((256, 128), F32)],
        compiler_params=_cp(("parallel", "arbitrary")),
    )(gla_p, w_g2p, b_g2, gnw)


def _gla_bwd(gla_p, w_g2p, b_g2, gnw, raw, states, d_mix):
    B, S, _ = gla_p.shape
    C = CHUNK
    NC = S // C
    G = min(GLA_G, NC)
    NG = NC // G

    def body(p_ref, w_ref, b_ref, gn_ref, raw_ref, st_ref, dm_ref, dp_ref, dw_ref, db_ref, dgn_ref, dst_sc):
        first = (pl.program_id(0) == 0) & (pl.program_id(1) == 0)

        @pl.when(first)
        def _():
            dw_ref[...] = jnp.zeros_like(dw_ref)
            db_ref[...] = jnp.zeros_like(db_ref)
            dgn_ref[...] = jnp.zeros_like(dgn_ref)

        @pl.when(pl.program_id(1) == 0)
        def _():
            dst_sc[...] = jnp.zeros_like(dst_sc)

        mk, mv, bd, lower, ltri, utri = _gla_masks()
        gn = gn_ref[...]
        cs = range(G)
        rows = [slice(c * C, (c + 1) * C) for c in cs]
        ps = [p_ref[0, rows[c], :].astype(F32) for c in cs]
        vs = [ps[c][:, 256:512] for c in cs]
        pre = [_mm(ps[c][:, 512:640], w_ref[...]) + b_ref[...] for c in cs]
        cum = [_mm_f32(ltri, _log_sigmoid(pre[c]) * (1.0 / GLA_TAU)) for c in cs]
        dg, dz, dgn_acc = [], [], jnp.zeros((1, 256), F32)
        for c in cs:
            g = raw_ref[0, rows[c], :]
            z = ps[c][:, 640:896]
            rstd = lax.rsqrt(_head_mean(g * g, mv, 64.0) + EPS)
            gh = g * rstd
            dm = dm_ref[0, rows[c], :]
            d_gn = dm * _silu(z)
            dz.append(dm * gh * gn * _dsilu(z))
            dgn_acc = dgn_acc + jnp.sum(d_gn * gh, axis=0, keepdims=True)
            d_gh = d_gn * gn
            dg.append(rstd * (d_gh - gh * _head_mean(d_gh * gh, mv, 64.0)))
        do_s = [_stack_heads(dg[c], mv) for c in cs]
        dattn = [_mm_nt(do_s[c], vs[c]) for c in cs]
        ks, e_pos, e_neg, q_pos, q_neg, k_pos, k_neg, qp_s, qn_s, past, fut, a_row, w_dec, kd = ([] for _ in range(14))
        for c in cs:
            q = ps[c][:, 0:128]
            k = ps[c][:, 128:256] * GLA_KSCALE
            last = cum[c][C - 1:C, :]
            ep, en = jnp.exp(cum[c]), jnp.exp(-cum[c])
            ks.append(k), e_pos.append(ep), e_neg.append(en)
            q_pos.append(q * ep), q_neg.append(q * en), k_pos.append(k * ep), k_neg.append(k * en)
            qp_s.append(_stack_heads(q_pos[c], mk)), qn_s.append(_stack_heads(q_neg[c], mk))
            past.append(_mm_nt(qp_s[c], k_neg[c]))
            fut.append(_mm_nt(qn_s[c], k_pos[c]))
            a_row.append(jnp.exp(last))
            w_dec.append(jnp.exp(last - cum[c]))
            kd.append(k * w_dec[c])
        sts = [st_ref[0, c] for c in cs]
        dq_st = [_mm(dg[c], sts[c]) for c in cs]
        dst_in = [_mm_tn(dg[c], q_pos[c]) for c in cs]
        dv, dq_pos, dk_neg, dq_neg, dk_pos = [], [], [], [], []
        for c in cs:
            attn = jnp.where(lower, past[c], fut[c])
            dpast = jnp.where(lower, dattn[c], 0.0)
            dfut = jnp.where(lower, 0.0, dattn[c])
            dv.append(_mm_tn(attn, do_s[c]))
            dq_pos.append(_fold_heads(_mm(dpast, k_neg[c]), mk, C) + dq_st[c])
            dk_neg.append(_mm_tn(dpast, qp_s[c]))
            dq_neg.append(_fold_heads(_mm(dfut, k_pos[c]), mk, C))
            dk_pos.append(_mm_tn(dfut, qn_s[c]))
        dst_next = dst_sc[...]
        d_a, d_kd = [None] * G, [None] * G
        for c in reversed(cs):
            d_a[c] = jnp.sum(dst_next * sts[c], axis=0, keepdims=True)
            gmat = jnp.where(bd, dst_next, 0.0)
            d_kd[c] = _mm(vs[c], gmat)
            dv[c] = dv[c] + _mm_nt(kd[c], gmat)
            dst_next = dst_next * a_row[c] + jnp.where(bd, dst_in[c], 0.0)
        dst_sc[...] = dst_next
        row = lax.broadcasted_iota(jnp.int32, (C, 128), 0)
        d_la, dk, dq = [], [], []
        for c in cs:
            t = d_kd[c] * kd[c]
            dk.append(d_kd[c] * w_dec[c] + dk_neg[c] * e_neg[c] + dk_pos[c] * e_pos[c])
            dq.append(dq_pos[c] * e_pos[c] + dq_neg[c] * e_neg[c])
            d_last = jnp.sum(t, axis=0, keepdims=True) + d_a[c] * a_row[c]
            d_cum = (dq_pos[c] * q_pos[c] - dk_neg[c] * k_neg[c] - dq_neg[c] * q_neg[c] + dk_pos[c] * k_pos[c] - t)
            d_la.append(_mm_f32(utri, d_cum + jnp.where(row == C - 1, d_last, 0.0)))
        d_pre = [d_la[c] * _sig(-pre[c]) * (1.0 / GLA_TAU) for c in cs]
        d_gg = [_mm_nt(d_pre[c], w_ref[...]) for c in cs]
        dw_acc = _mm_tn(ps[0][:, 512:640], d_pre[0])
        db_acc = jnp.sum(d_pre[0], axis=0, keepdims=True)
        for c in cs[1:]:
            dw_acc = dw_acc + _mm_tn(ps[c][:, 512:640], d_pre[c])
            db_acc = db_acc + jnp.sum(d_pre[c], axis=0, keepdims=True)
        for c in cs:
            dp_ref[0, rows[c], :] = jnp.concatenate([dq[c], dk[c] * GLA_KSCALE, dv[c], d_gg[c], dz[c]],
                                                    axis=1).astype(_MXU)
        dw_ref[...] += dw_acc
        db_ref[...] += db_acc
        dgn_ref[...] += dgn_acc

        @pl.when((pl.program_id(0) == B - 1) & (pl.program_id(1) == NG - 1))
        def _():
            s1 = dgn_ref[...]
            s1 = s1 + pltpu.roll(s1, 128, 1)
            dgn_ref[...] = s1 + pltpu.roll(s1, 64, 1)

    tok = lambda w: pl.BlockSpec((1, G * C, w), lambda b, i: (b, NG - 1 - i, 0))
    return pl.pallas_call(
        body, name="gla_bwd", grid=(B, NG),
        in_specs=[tok(GLA_W), _full((128, 128)), _full((1, 128)), _full((1, 256)), tok(256),
                  pl.BlockSpec((1, G, 256, 128), lambda b, i: (b, NG - 1 - i, 0, 0)), tok(256)],
        out_specs=[tok(GLA_W), _full((128, 128)), _full((1, 128)), _full((1, 256))],
        out_shape=[jax.ShapeDtypeStruct((B, S, GLA_W), _MXU), jax.ShapeDtypeStruct((128, 128), F32),
                   jax.ShapeDtypeStruct((1, 128), F32), jax.ShapeDtypeStruct((1, 256), F32)],
        scratch_shapes=[pltpu.VMEM((256, 128), F32)],
        compiler_params=_cp(("arbitrary", "arbitrary")),
    )(gla_p, w_g2p, b_g2, gnw, raw, states, d_mix)


def _rms(x, w):
    rstd = lax.rsqrt(jnp.mean(x * x, axis=-1, keepdims=True) + EPS)
    xh = x * rstd
    return xh, rstd, xh * w


def _rms_bwd(dy, xh, rstd, w):
    dxh = dy * w
    return rstd * (dxh - xh * jnp.mean(dxh * xh, axis=-1, keepdims=True))


MLA_T = 256


def _mla_prep_fwd(mla_p, cos, sin, qnw, kvnw, w_uq, w_ukv, w_ukv_t):
    B, S, _ = mla_p.shape
    tm = min(S, 512)

    t = min(MLA_T, S)
    nt = tm // t

    def body(p_ref, c_ref, s_ref, qn_ref, kn_ref, wq_ref, wkv_ref, wkvt_ref, q_ref, k_ref, v_ref, kt_ref, vt_ref):
        p = p_ref[0].astype(F32)
        cs, sn = c_ref[0], s_ref[0]
        _, _, qn = _rms(p[:, 0:256], qn_ref[...])
        qpre = _mm(qn, wq_ref[...])
        _, _, kvn = _rms(p[:, 256:384], kn_ref[...])
        kv = _mm(kvn, wkv_ref[...])
        kvt = _mm_nt(wkvt_ref[...], kvn)
        kpe = _rope128(p[:, 384:512], cs, sn)
        kpet = kpe.T
        for h in range(8):
            sl = slice(128 * h, 128 * h + 128)
            q_ref[0, :, sl] = _rope128(qpre[:, sl], cs, sn).astype(_MXU)
            k_ref[0, :, sl] = (kv[:, sl] + kpe).astype(_MXU)
            kht = kvt[sl, :] + kpet
            for n in range(nt):
                kt_ref[0, n, sl, :] = kht[:, n * t:(n + 1) * t].astype(_MXU)
        v_ref[0] = kv[:, 1024:1536].astype(_MXU)
        for n in range(nt):
            vt_ref[0, n] = kvt[1024:1536, n * t:(n + 1) * t].astype(_MXU)

    tok = lambda w: pl.BlockSpec((1, tm, w), lambda b, i: (b, i, 0))
    tr = lambda w: pl.BlockSpec((1, nt, w, t), lambda b, i: (b, i, 0, 0))
    return pl.pallas_call(
        body, name="mla_prep_fwd", grid=(B, S // tm),
        in_specs=[tok(512), tok(128), tok(128), _full((1, 256)), _full((1, 128)), _full((256, 1024)),
                  _full((128, 1536)), _full((1536, 128))],
        out_specs=[tok(1024), tok(1024), tok(512), tr(1024), tr(512)],
        out_shape=[jax.ShapeDtypeStruct((B, S, 1024), _MXU), jax.ShapeDtypeStruct((B, S, 1024), _MXU),
                   jax.ShapeDtypeStruct((B, S, 512), _MXU), jax.ShapeDtypeStruct((B, S // t, 1024, t), _MXU),
                   jax.ShapeDtypeStruct((B, S // t, 512, t), _MXU)],
        compiler_params=_cp(("parallel", "parallel")),
    )(mla_p, cos, sin, qnw, kvnw, w_uq, w_ukv, w_ukv_t)


def _chunk_mask_t(t):
    kj = lax.broadcasted_iota(jnp.int32, (t, t), 0) // CHUNK
    qi = lax.broadcasted_iota(jnp.int32, (t, t), 1) // CHUNK
    return kj <= qi


MLA_HG = 8
MLA_HG_FWD = 8
LOG2E = 1.4426950408889634
MLA_C2 = MLA_SCALE * LOG2E


def _mla_attn_fwd(q, k, vt):
    B, S, _ = q.shape
    t = min(MLA_T, S)
    nq = S // t
    HG = MLA_HG_FWD
    NP = HG // 2

    def body(q_ref, k_ref, vt_ref, o_ref, lse_ref, sa, sb, m_sc, l_sc, acc_sc):
        i = pl.program_id(2)
        row = lax.broadcasted_iota(jnp.int32, (128, 1), 0)
        low = row < 64
        mask = _chunk_mask_t(t)
        m_sc[...] = jnp.full(m_sc.shape, -jnp.inf, F32)
        l_sc[...] = jnp.zeros_like(l_sc)
        acc_sc[...] = jnp.zeros_like(acc_sc)

        ones = jnp.ones((8, t), _MXU)

        def scores(j, buf):
            kb = k_ref[0, pl.ds(pl.multiple_of(j * t, t), t), :]
            for h in range(HG):
                cols = slice(128 * h, 128 * h + 128)
                buf[h] = (_mm_nt(kb[:, cols], q_ref[0, :, cols]) * MLA_C2).astype(_MXU)

        def absorb(j, buf, masked):
            vtb = vt_ref[0, j]
            for pr in range(NP):
                alphas, pvs = [], []
                for hh in range(2):
                    h = 2 * pr + hh
                    s = buf[h]
                    if masked:
                        s = jnp.where(mask, s, jnp.full_like(s, -jnp.inf))
                    m_old = m_sc[h]
                    m_new = jnp.maximum(m_old, jnp.max(s, axis=0, keepdims=True).astype(F32))
                    alpha = jnp.exp2(m_old - m_new)
                    p = jnp.exp2(s - m_new.astype(_MXU))
                    l_sc[h] = alpha * l_sc[h] + _mm(ones, p)[0:1, :]
                    m_sc[h] = m_new
                    vth = vtb[128 * pr:128 * pr + 128, :]
                    vth = jnp.where(low if hh == 0 else ~low, vth, jnp.zeros_like(vth))
                    pvs.append(_mm(vth, p))
                    alphas.append(alpha)
                acc_sc[pr] = acc_sc[pr] * jnp.where(low, alphas[0], alphas[1]) + pvs[0] + pvs[1]

        scores(0, sb)

        def pair(jj, carry):
            j0 = 2 * jj
            scores(j0 + 1, sa)
            absorb(j0, sb, False)
            scores(j0 + 2, sb)
            absorb(j0 + 1, sa, False)
            return carry

        lax.fori_loop(0, i // 2, pair, 0)

        @pl.when(i % 2 == 1)
        def _():
            scores(i, sa)
            absorb(i - 1, sb, False)
            absorb(i, sa, True)

        @pl.when(i % 2 == 0)
        def _():
            absorb(i, sb, True)

        for pr in range(NP):
            l_e, l_o = l_sc[2 * pr], l_sc[2 * pr + 1]
            o_ref[0, :, 128 * pr:128 * pr + 128] = (acc_sc[pr] / jnp.where(low, l_e, l_o)).T
            lse_ref[0, pr, 0, 0:1, :] = m_sc[2 * pr] + jnp.log(l_e) * LOG2E
            lse_ref[0, pr, 0, 1:2, :] = m_sc[2 * pr + 1] + jnp.log(l_o) * LOG2E

    return pl.pallas_call(
        body, name="mla_attn_fwd", grid=(B, 8 // HG, nq),
        in_specs=[pl.BlockSpec((1, t, 128 * HG), lambda b, g, i: (b, i, g)),
                  pl.BlockSpec((1, S, 128 * HG), lambda b, g, i: (b, 0, g)),
                  pl.BlockSpec((1, nq, 64 * HG, t), lambda b, g, i: (b, 0, g, 0))],
        out_specs=[pl.BlockSpec((1, t, 64 * HG), lambda b, g, i: (b, i, g)),
                   pl.BlockSpec((1, NP, 1, 2, t), lambda b, g, i: (b, g, i, 0, 0))],
        out_shape=[jax.ShapeDtypeStruct((B, S, 512), F32), jax.ShapeDtypeStruct((B, 4, nq, 2, t), F32)],
        scratch_shapes=[pltpu.VMEM((HG, t, t), _MXU), pltpu.VMEM((HG, t, t), _MXU), pltpu.VMEM((HG, 1, t), F32),
                        pltpu.VMEM((HG, 1, t), F32), pltpu.VMEM((NP, 128, t), F32)],
        compiler_params=_cp(("parallel", "parallel", "arbitrary")),
    )(q, k, vt)


def _mla_attn_bwd(q, k, v, kt, do, lse, dl):
    B, S, _ = q.shape
    t = min(MLA_T, S)
    nk = S // t

    HG = MLA_HG
    NP = HG // 2

    def body(q_ref, k_ref, v_ref, kt_ref, do_ref, lse_ref, dl_ref, dq_ref, dk_ref, dv_ref,
             sa, da, sb, db, dqt_sc, dk_sc, dv_sc):
        j = pl.program_id(2)

        @pl.when(j == 0)
        def _():
            dqt_sc[...] = jnp.zeros_like(dqt_sc)

        dk_sc[...] = jnp.zeros_like(dk_sc)
        dv_sc[...] = jnp.zeros_like(dv_sc)
        lane = lax.broadcasted_iota(jnp.int32, (1, 128), 1)
        low = lane < 64
        mask = _chunk_mask_t(t)

        def half(x, hh):
            return jnp.where(low if hh == 0 else ~low, x, jnp.zeros_like(x))

        def prepare(i, sbuf, dbuf):
            rows = pl.ds(pl.multiple_of(i * t, t), t)
            for h in range(HG):
                cols = slice(128 * h, 128 * h + 128)
                pc = slice(128 * (h // 2), 128 * (h // 2) + 128)
                sbuf[h] = _mm_nt(k_ref[0, :, cols], q_ref[0, rows, cols]) * MLA_C2
                dbuf[h] = _mm_nt(half(v_ref[0, :, pc], h % 2), do_ref[0, rows, pc])

        def absorb(i, sbuf, dbuf, masked):
            rows = pl.ds(pl.multiple_of(i * t, t), t)
            for h in range(HG):
                pr, hh = h // 2, h % 2
                cols = slice(128 * h, 128 * h + 128)
                pc = slice(128 * pr, 128 * pr + 128)
                p = jnp.exp2(sbuf[h] - lse_ref[0, pr, i][hh:hh + 1, :])
                if masked:
                    p = jnp.where(mask, p, 0.0)
                dv_sc[pr] += _mm(p, half(do_ref[0, rows, pc], hh))
                ds = p * (dbuf[h] - dl_ref[0, pr, i][hh:hh + 1, :])
                dqt_sc[i, cols, :] += _mm(kt_ref[0, 0, cols, :], ds)
                dk_sc[h] += _mm(ds, q_ref[0, rows, cols])

        n = nk - 1 - j
        prepare(jnp.minimum(j + 1, nk - 1), sb, db)

        def pair(jj, carry):
            i0 = j + 1 + 2 * jj
            prepare(i0 + 1, sa, da)
            absorb(i0, sb, db, False)
            prepare(jnp.where(i0 + 2 <= nk - 1, i0 + 2, j), sb, db)
            absorb(i0 + 1, sa, da, False)
            return carry

        lax.fori_loop(0, n // 2, pair, 0)

        @pl.when(n % 2 == 1)
        def _():
            prepare(j, sa, da)
            absorb(nk - 1, sb, db, False)
            absorb(j, sa, da, True)

        @pl.when(n % 2 == 0)
        def _():
            absorb(j, sb, db, True)

        for h in range(HG):
            dk_ref[0, :, 128 * h:128 * h + 128] = (dk_sc[h] * MLA_SCALE).astype(_MXU)
        for pr in range(NP):
            dv_ref[0, :, 128 * pr:128 * pr + 128] = dv_sc[pr].astype(_MXU)

        @pl.when(j == nk - 1)
        def _():
            for i in range(nk):
                dq_ref[0, i * t:(i + 1) * t, :] = (dqt_sc[i].T * MLA_SCALE).astype(_MXU)

    seq = lambda w: pl.BlockSpec((1, S, w), lambda b, g, j: (b, 0, g))
    blk = lambda w: pl.BlockSpec((1, t, w), lambda b, g, j: (b, j, g))
    stat = pl.BlockSpec((1, NP, nk, 2, t), lambda b, g, j: (b, g, 0, 0, 0))
    return pl.pallas_call(
        body, name="mla_attn_bwd", grid=(B, 8 // HG, nk),
        in_specs=[seq(128 * HG), blk(128 * HG), blk(64 * HG),
                  pl.BlockSpec((1, 1, 128 * HG, t), lambda b, g, j: (b, j, g, 0)), seq(64 * HG), stat, stat],
        out_specs=[seq(128 * HG), blk(128 * HG), blk(64 * HG)],
        out_shape=[jax.ShapeDtypeStruct((B, S, 1024), _MXU), jax.ShapeDtypeStruct((B, S, 1024), _MXU),
                   jax.ShapeDtypeStruct((B, S, 512), _MXU)],
        scratch_shapes=[pltpu.VMEM((HG, t, t), F32), pltpu.VMEM((HG, t, t), F32), pltpu.VMEM((HG, t, t), F32),
                        pltpu.VMEM((HG, t, t), F32), pltpu.VMEM((nk, 128 * HG, t), F32),
                        pltpu.VMEM((HG, t, 128), F32), pltpu.VMEM((NP, t, 128), F32)],
        compiler_params=_cp(("parallel", "parallel", "arbitrary"), 56),
    )(q, k, v, kt, do, lse, dl)


def _mla_prep_bwd(mla_p, cos, sin, qnw, kvnw, w_uq, w_ukv, dq, dk, dv):
    B, S, _ = mla_p.shape
    tm = min(S, 512)

    def body(p_ref, c_ref, s_ref, qn_ref, kn_ref, wq_ref, wkv_ref, dq_ref, dk_ref, dv_ref,
             dp_ref, dwq_ref, dwkv_ref, dqn_ref, dkn_ref):
        first = (pl.program_id(0) == 0) & (pl.program_id(1) == 0)

        @pl.when(first)
        def _():
            dwq_ref[...] = jnp.zeros_like(dwq_ref)
            dwkv_ref[...] = jnp.zeros_like(dwkv_ref)
            dqn_ref[...] = jnp.zeros_like(dqn_ref)
            dkn_ref[...] = jnp.zeros_like(dkn_ref)

        p = p_ref[0].astype(F32)
        cs, sn = c_ref[0], s_ref[0]
        lane = lax.broadcasted_iota(jnp.int32, (1, 128), 1)
        pe = (lane >= 64) & (lane < 96)
        qh, q_rstd, qn = _rms(p[:, 0:256], qn_ref[...])
        kvh, kv_rstd, kvn = _rms(p[:, 256:384], kn_ref[...])
        dqv = dq_ref[0].astype(F32)
        dkv = dk_ref[0].astype(F32)
        dqpre = jnp.concatenate(
            [_rope128_t(dqv[:, 128 * h:128 * h + 128], cs, sn) for h in range(8)], axis=1)
        dkpe = jnp.zeros((tm, 128), F32)
        for h in range(8):
            dkpe = dkpe + jnp.where(pe, dkv[:, 128 * h:128 * h + 128], 0.0)
        dkr = _rope128_t(dkpe, cs, sn)
        dkv_all = jnp.concatenate([dkv, dv_ref[0].astype(F32)], axis=1)
        d_qn = _mm_nt(dqpre, wq_ref[...])
        d_kvn = _mm_nt(dkv_all, wkv_ref[...])
        dwq_ref[...] += _mm_tn(qn, dqpre)
        dwkv_ref[...] += _mm_tn(kvn, dkv_all)
        dqn_ref[...] += jnp.sum(d_qn * qh, axis=0, keepdims=True)
        dkn_ref[...] += jnp.sum(d_kvn * kvh, axis=0, keepdims=True)
        dp_ref[0] = jnp.concatenate([_rms_bwd(d_qn, qh, q_rstd, qn_ref[...]),
                                     _rms_bwd(d_kvn, kvh, kv_rstd, kn_ref[...]), dkr], axis=1).astype(_MXU)

    tok = lambda w: pl.BlockSpec((1, tm, w), lambda b, i: (b, i, 0))
    return pl.pallas_call(
        body, name="mla_prep_bwd", grid=(B, S // tm),
        in_specs=[tok(512), tok(128), tok(128), _full((1, 256)), _full((1, 128)), _full((256, 1024)),
                  _full((128, 1536)), tok(1024), tok(1024), tok(512)],
        out_specs=[tok(512), _full((256, 1024)), _full((128, 1536)), _full((1, 256)), _full((1, 128))],
        out_shape=[jax.ShapeDtypeStruct((B, S, 512), _MXU), jax.ShapeDtypeStruct((256, 1024), F32),
                   jax.ShapeDtypeStruct((128, 1536), F32), jax.ShapeDtypeStruct((1, 256), F32),
                   jax.ShapeDtypeStruct((1, 128), F32)],
        compiler_params=_cp(("arbitrary", "arbitrary")),
    )(mla_p, cos, sin, qnw, kvnw, w_uq, w_ukv, dq, dk, dv)


def _out_fwd(x, gate, r_g, o_mla, mla_p, g_g, w_out):
    B, S, D = x.shape
    tm = min(S, 512)

    def body(x_ref, g_ref, r_ref, o_ref, z_ref, gg_ref, w_ref, xn_ref, y_ref):
        mm = (o_ref[0] * _silu(z_ref[0].astype(F32))).astype(_MXU)
        y = (jnp.dot(r_ref[0], w_ref[0:256, :], preferred_element_type=F32)
             + jnp.dot(mm, w_ref[256:768, :], preferred_element_type=F32)
             + jnp.dot(gg_ref[0], w_ref[768:1024, :], preferred_element_type=F32))
        y_ref[0] = y.astype(_MXU)
        xn_ref[0] = x_ref[0] + g_ref[0] * y

    tok = lambda w, c=0: pl.BlockSpec((1, tm, w), lambda b, i: (b, i, c))
    return pl.pallas_call(
        body, name="out_fwd", grid=(B, S // tm),
        in_specs=[tok(D), pl.BlockSpec((1, 1, D), lambda b, i: (b, 0, 0)), tok(256), tok(512), tok(512, 1),
                  tok(256), _full((D, D))],
        out_specs=[tok(D), tok(D)],
        out_shape=[jax.ShapeDtypeStruct((B, S, D), F32), jax.ShapeDtypeStruct((B, S, D), _MXU)],
        compiler_params=_cp(("parallel", "parallel")),
    )(x, gate, r_g, o_mla, mla_p, g_g, w_out)


def _out_bwd(dx, y, gate, r_g, g_g, w_out, o_mla, mla_p):
    B, S, D = dx.shape
    tm = min(S, 512)
    t = min(MLA_T, S)
    nt = tm // t

    def body(dx_ref, y_ref, g_ref, r_ref, gg_ref, w_ref, o_ref, z_ref,
             dr_ref, do_ref, dz_ref, dl_ref, dg_ref, dw_ref, dgate_ref, acc):
        first = (pl.program_id(0) == 0) & (pl.program_id(1) == 0)

        @pl.when(first)
        def _():
            acc[...] = jnp.zeros_like(acc)

        @pl.when(pl.program_id(1) == 0)
        def _():
            dgate_ref[...] = jnp.zeros_like(dgate_ref)

        dxv = dx_ref[0]
        dgate_ref[0] += jnp.sum(dxv * y_ref[0].astype(F32), axis=0, keepdims=True)
        dy = (dxv * g_ref[0]).astype(_MXU)
        dr_ref[0] = _mm_nt(dy, w_ref[0:256, :])
        dg_ref[0] = _mm_nt(dy, w_ref[768:1024, :])
        ov, z = o_ref[0], z_ref[0].astype(F32)
        acc[0:256, :] += _mm_tn(r_ref[0], dy)
        acc[256:768, :] += _mm_tn((ov * _silu(z)).astype(_MXU), dy)
        acc[768:1024, :] += _mm_tn(gg_ref[0], dy)

        @pl.when((pl.program_id(0) == B - 1) & (pl.program_id(1) == S // tm - 1))
        def _():
            dw_ref[...] = acc[...].astype(_MXU)

        dm = _mm_nt(dy, w_ref[256:768, :])
        do = dm * _silu(z)
        dz_ref[0] = (dm * ov * _dsilu(z)).astype(_MXU)
        do_ref[0] = do.astype(_MXU)
        prod = do * ov
        for pr in range(4):
            pt = prod[:, 128 * pr:128 * pr + 128].T
            se = jnp.sum(pt[0:64], axis=0, keepdims=True)
            so = jnp.sum(pt[64:128], axis=0, keepdims=True)
            for n in range(nt):
                dl_ref[0, pr, n, 0:1, :] = se[:, n * t:(n + 1) * t]
                dl_ref[0, pr, n, 1:2, :] = so[:, n * t:(n + 1) * t]

    tok = lambda w, c=0: pl.BlockSpec((1, tm, w), lambda b, i: (b, i, c))
    per_seq = pl.BlockSpec((1, 1, D), lambda b, i: (b, 0, 0))
    return pl.pallas_call(
        body, name="out_bwd", grid=(B, S // tm),
        in_specs=[tok(D), tok(D), per_seq, tok(256), tok(256), _full((D, D)), tok(512), tok(512, 1)],
        out_specs=[tok(256), tok(512), tok(512), pl.BlockSpec((1, 4, nt, 2, t), lambda b, i: (b, 0, i, 0, 0)),
                   tok(256), _full((D, D)), per_seq],
        out_shape=[jax.ShapeDtypeStruct((B, S, 256), F32), jax.ShapeDtypeStruct((B, S, 512), _MXU),
                   jax.ShapeDtypeStruct((B, S, 512), _MXU), jax.ShapeDtypeStruct((B, 4, S // t, 2, t), F32),
                   jax.ShapeDtypeStruct((B, S, 256), F32), jax.ShapeDtypeStruct((D, D), _MXU),
                   jax.ShapeDtypeStruct((B, 1, D), F32)],
        scratch_shapes=[pltpu.VMEM((D, D), F32)],
        compiler_params=_cp(("arbitrary", "arbitrary")),
    )(dx, y, gate, r_g, g_g, w_out, o_mla, mla_p)


def _proj_bwd_x(x, shift, scale, nw, w_arr, d_ret, d_mla, d_mz, d_gla, dx_out):
    B, S, D = x.shape
    tm = min(S, 512)

    def body(x_ref, sc_ref, nw_ref, w_ref, dr_ref, dm_ref, dz_ref, dg_ref, dxo_ref,
             dx_ref, dsh_ref, dsc_ref, dnw_ref):
        first = (pl.program_id(0) == 0) & (pl.program_id(1) == 0)

        @pl.when(first)
        def _():
            dnw_ref[...] = jnp.zeros_like(dnw_ref)

        @pl.when(pl.program_id(1) == 0)
        def _():
            dsh_ref[...] = jnp.zeros_like(dsh_ref)
            dsc_ref[...] = jnp.zeros_like(dsc_ref)

        dp = jnp.concatenate([dr_ref[0], dm_ref[0], dz_ref[0], dg_ref[0]], axis=1)
        dh = lax.dot_general(dp, w_ref[...], (((1,), (1,)), ((), ())), preferred_element_type=F32)
        xv = x_ref[0]
        rstd = lax.rsqrt(jnp.mean(xv * xv, axis=-1, keepdims=True) + EPS)
        xh = xv * rstd
        nwv = nw_ref[...]
        mod = 1.0 + sc_ref[0]
        dsh_ref[0] += jnp.sum(dh, axis=0, keepdims=True)
        dsc_ref[0] += jnp.sum(dh * xh * nwv, axis=0, keepdims=True)
        dnw_ref[...] += jnp.sum(dh * xh * mod, axis=0, keepdims=True)
        dxh = dh * nwv * mod
        dx_ref[0] = dxo_ref[0] + rstd * (dxh - xh * jnp.mean(dxh * xh, axis=-1, keepdims=True))

    tok = lambda w: pl.BlockSpec((1, tm, w), lambda b, i: (b, i, 0))
    per_seq = pl.BlockSpec((1, 1, D), lambda b, i: (b, 0, 0))
    return pl.pallas_call(
        body, name="proj_bwd_x", grid=(B, S // tm),
        in_specs=[tok(D), per_seq, _full((1, D)), _full((D, ARR_W)), tok(RET_W), tok(512), tok(512),
                  tok(GLA_W), tok(D)],
        out_specs=[tok(D), per_seq, per_seq, _full((1, D))],
        out_shape=[jax.ShapeDtypeStruct((B, S, D), F32), jax.ShapeDtypeStruct((B, 1, D), F32),
                   jax.ShapeDtypeStruct((B, 1, D), F32), jax.ShapeDtypeStruct((1, D), F32)],
        compiler_params=_cp(("arbitrary", "arbitrary")),
    )(x, scale, nw, w_arr, d_ret, d_mla, d_mz, d_gla, dx_out)


def _proj_bwd_w(h, d_ret, d_mla, d_mz, d_gla):
    B, S, D = h.shape
    tm = min(S, 512)

    def body(h_ref, dr_ref, dm_ref, dz_ref, dg_ref, dw_ref, acc):
        first = (pl.program_id(0) == 0) & (pl.program_id(1) == 0)

        @pl.when(first)
        def _():
            acc[...] = jnp.zeros_like(acc)

        hv = h_ref[0]
        tn = lambda d_ref: lax.dot_general(hv, d_ref[0], (((0,), (0,)), ((), ())), preferred_element_type=F32)
        acc[:, 0:RET_W] += tn(dr_ref)
        acc[:, RET_W:RET_W + 512] += tn(dm_ref)
        acc[:, RET_W + 512:RET_W + MLA_W] += tn(dz_ref)
        acc[:, RET_W + MLA_W:ARR_W] += tn(dg_ref)

        @pl.when((pl.program_id(0) == B - 1) & (pl.program_id(1) == S // tm - 1))
        def _():
            dw_ref[...] = acc[...].astype(_MXU)

    tok = lambda w: pl.BlockSpec((1, tm, w), lambda b, i: (b, i, 0))
    return pl.pallas_call(
        body, name="proj_bwd_w", grid=(B, S // tm),
        in_specs=[tok(D), tok(RET_W), tok(512), tok(512), tok(GLA_W)],
        out_specs=_full((D, ARR_W)), out_shape=jax.ShapeDtypeStruct((D, ARR_W), _MXU),
        scratch_shapes=[pltpu.VMEM((D, ARR_W), F32)],
        compiler_params=_cp(("arbitrary", "arbitrary"), 56),
    )(h, d_ret, d_mla, d_mz, d_gla)


def _out_fwd_loss(x, gate, r_g, o_mla, mla_p, g_g, w_out, fw, target):
    B, S, D = x.shape
    tm = min(S, 512)

    def body(x_ref, g_ref, r_ref, o_ref, z_ref, gg_ref, w_ref, fw_ref, t_ref, dx_ref, y_ref, loss_ref, dfw_ref):
        first = (pl.program_id(0) == 0) & (pl.program_id(1) == 0)

        @pl.when(first)
        def _():
            loss_ref[...] = jnp.zeros_like(loss_ref)
            dfw_ref[...] = jnp.zeros_like(dfw_ref)

        mm = (o_ref[0] * _silu(z_ref[0].astype(F32))).astype(_MXU)
        y = (jnp.dot(r_ref[0], w_ref[0:256, :], preferred_element_type=F32)
             + jnp.dot(mm, w_ref[256:768, :], preferred_element_type=F32)
             + jnp.dot(gg_ref[0], w_ref[768:1024, :], preferred_element_type=F32))
        y_ref[0] = y.astype(_MXU)
        xv = x_ref[0] + g_ref[0] * y
        fwv = fw_ref[...]
        rstd = lax.rsqrt(jnp.mean(xv * xv, axis=-1, keepdims=True) + EPS)
        xh = xv * rstd
        err = xh * fwv - t_ref[0]
        loss_ref[...] += 0.5 * jnp.sum(jnp.mean(err * err, axis=-1, keepdims=True), axis=0, keepdims=True)
        dy = err * (1.0 / D)
        dfw_ref[...] += jnp.sum(dy * xh, axis=0, keepdims=True)
        dxh = dy * fwv
        dx_ref[0] = rstd * (dxh - xh * jnp.mean(dxh * xh, axis=-1, keepdims=True))

    tok = lambda w, c=0: pl.BlockSpec((1, tm, w), lambda b, i: (b, i, c))
    return pl.pallas_call(
        body, name="out_fwd_loss", grid=(B, S // tm),
        in_specs=[tok(D), pl.BlockSpec((1, 1, D), lambda b, i: (b, 0, 0)), tok(256), tok(512), tok(512, 1),
                  tok(256), _full((D, D)), _full((1, D)), tok(D)],
        out_specs=[tok(D), tok(D), _full((1, 1)), _full((1, D))],
        out_shape=[jax.ShapeDtypeStruct((B, S, D), F32), jax.ShapeDtypeStruct((B, S, D), _MXU),
                   jax.ShapeDtypeStruct((1, 1), F32), jax.ShapeDtypeStruct((1, D), F32)],
        compiler_params=_cp(("arbitrary", "arbitrary")),
    )(x, gate, r_g, o_mla, mla_p, g_g, w_out, fw, target)


def _local_step(x, pos3, mod, loss_target, small, w_in_a, w_uq_a, w_ukv_a, w_out_b):
    B, S, D = x.shape
    tabs = _rope_tables(pos3)
    saved = []
    for l in range(DEPTH):
        last = (small["final_norm"].reshape(1, D), loss_target) if l == DEPTH - 1 else None
        x, s = _layer_fwd(x, tabs, mod[l], {n: a[l] for n, a in small.items() if n != "final_norm"},
                          w_in_a[l], w_uq_a[l], w_ukv_a[l], w_ukv_a[l].T, w_out_b[l], loss_head=last)
        saved.append(s)
    dx, loss, d_fw = x
    grads = dict(final_norm=d_fw.reshape(D))
    per_layer = [None] * DEPTH
    for l in reversed(range(DEPTH)):
        dx, per_layer[l] = _layer_bwd(dx, saved[l], tabs)
    for name in per_layer[0]:
        grads[name] = jnp.stack([per_layer[l][name] for l in range(DEPTH)])
    return loss, dx, grads


def _layer_fwd(x, tabs, mod_l, small_l, w_in_a, w_uq_a=None, w_ukv_a=None, w_ukv_t=None, w_out_b=None, late_weights=None,
               loss_head=None):
    B, S, D = x.shape
    cr, sr, cm, sm = tabs
    shift = mod_l[:, 0:D].reshape(B, 1, D)
    scale = mod_l[:, D:2 * D].reshape(B, 1, D)
    gate = mod_l[:, 2 * D:3 * D].reshape(B, 1, D)
    nw = small_l["norm_w"].reshape(1, D)
    qnw = small_l["mla_q_norm"].reshape(1, 256)
    kvnw = small_l["mla_kv_norm"].reshape(1, 128)
    w_g2p = jnp.pad(small_l["gla_w_g2"], ((0, 112), (0, 0)))
    b_g2 = small_l["gla_b_g2"].reshape(1, 128)
    gnw = jnp.tile(small_l["gla_norm"], 4).reshape(1, 256)
    ret_p, mla_p, gla_p, h = _proj_fwd(x, shift, scale, nw, w_in_a)
    r_g, r_raw, r_st = _ret_fwd(ret_p, cr, sr)
    if late_weights is not None:
        w_uq_a, w_ukv_a, w_ukv_t, w_out_b = late_weights(r_raw)
    q, k, v, kt, vt = _mla_prep_fwd(mla_p, cm, sm, qnw, kvnw, w_uq_a, w_ukv_a, w_ukv_t)
    o_mla, lse = _mla_attn_fwd(q, k, vt)
    g_g, g_raw, g_st = _gla_fwd(gla_p, w_g2p, b_g2, gnw)
    if loss_head is None:
        x_new, y = _out_fwd(x, gate, r_g, o_mla, mla_p, g_g, w_out_b)
    else:
        dx, y, loss, d_fw = _out_fwd_loss(x, gate, r_g, o_mla, mla_p, g_g, w_out_b, *loss_head)
        x_new = (dx, loss, d_fw)
    saved = dict(x=x, shift=shift, scale=scale, gate=gate, nw=nw, qnw=qnw, kvnw=kvnw, w_g2p=w_g2p, b_g2=b_g2,
                 gnw=gnw, ret_p=ret_p, mla_p=mla_p, gla_p=gla_p, h=h, r_g=r_g, r_raw=r_raw, r_st=r_st, q=q, k=k,
                 v=v, kt=kt, o_mla=o_mla, lse=lse, g_g=g_g, g_raw=g_raw, g_st=g_st, y=y,
                 w_in_a=w_in_a, w_uq_a=w_uq_a, w_ukv_a=w_ukv_a, w_out_b=w_out_b)
    return x_new, saved


def _layer_bwd(dx, s, tabs, early_grads=None):
    B, S, D = dx.shape
    cr, sr, cm, sm = tabs
    d_r, do, d_mz, dl, d_g, dw_out, d_gate = _out_bwd(dx, s["y"], s["gate"], s["r_g"], s["g_g"], s["w_out_b"],
                                                      s["o_mla"], s["mla_p"])
    d_ret = _ret_bwd(s["ret_p"], cr, sr, s["r_raw"], s["r_st"], d_r)
    dq, dk, dv = _mla_attn_bwd(s["q"], s["k"], s["v"], s["kt"], do, s["lse"], dl)
    d_mla, dw_uq, dw_ukv, d_qnw, d_kvnw = _mla_prep_bwd(
        s["mla_p"], cm, sm, s["qnw"], s["kvnw"], s["w_uq_a"], s["w_ukv_a"], dq, dk, dv)
    gnw = s["gnw"] if early_grads is None else s["gnw"] + early_grads(dw_out, dw_uq, dw_ukv)
    d_gla, dw_g2p, db_g2, d_gnw = _gla_bwd(s["gla_p"], s["w_g2p"], s["b_g2"], gnw, s["g_raw"], s["g_st"], d_g)
    dx, d_shift, d_scale, d_nw = _proj_bwd_x(s["x"], s["shift"], s["scale"], s["nw"], s["w_in_a"],
                                             d_ret, d_mla, d_mz, d_gla, dx)
    dw_in = _proj_bwd_w(s["h"], d_ret, d_mla, d_mz, d_gla)
    grads = dict(
        d_mod=jnp.concatenate([d_shift, d_scale, d_gate], axis=2).reshape(B, 3 * D),
        norm_w=d_nw.reshape(D), mla_q_norm=d_qnw.reshape(256), mla_kv_norm=d_kvnw.reshape(128),
        gla_w_g2=dw_g2p[0:16], gla_b_g2=db_g2.reshape(128), gla_norm256=d_gnw.reshape(256),
        w_in_a=dw_in, w_uq_a=dw_uq, w_ukv_a=dw_ukv, w_out=dw_out)
    return dx, grads


def _exchange(arrs, gather, name):
    n = len(arrs)
    out_shape = [jax.ShapeDtypeStruct(((N_DEV,) + a.shape) if g else a.shape, a.dtype)
                 for a, g in zip(arrs, gather)]

    def body(*refs):
        ins, outs = refs[:n], refs[n:2 * n]
        send_sems, recv_sems, local_sems = refs[2 * n:]
        ix, iy, ic = lax.axis_index("x"), lax.axis_index("y"), lax.axis_index("c")
        me = 4 * ix + 2 * iy + ic
        copies = []
        for a in range(n):
            mine = ins[a] if gather[a] else ins[a].at[me]
            loc = pltpu.make_async_copy(mine, outs[a].at[me], local_sems.at[a])
            loc.start()
            copies.append(loc)
            for d in range(1, N_DEV):
                px = 1 - ix if d & 4 else ix
                py = 1 - iy if d & 2 else iy
                pc = 1 - ic if d & 1 else ic
                src = ins[a] if gather[a] else ins[a].at[4 * px + 2 * py + pc]
                cp = pltpu.make_async_remote_copy(
                    src_ref=src, dst_ref=outs[a].at[me], send_sem=send_sems.at[a, d - 1],
                    recv_sem=recv_sems.at[a, d - 1], device_id=(px, py, pc), device_id_type=pl.DeviceIdType.MESH)
                cp.start()
                copies.append(cp)
        for cp in copies:
            cp.wait()

    any_spec = pl.BlockSpec(memory_space=pl.ANY)
    outs = pl.pallas_call(
        body, name=name, in_specs=[any_spec] * n, out_specs=[any_spec] * n, out_shape=out_shape,
        scratch_shapes=[pltpu.SemaphoreType.DMA((n, N_DEV - 1)), pltpu.SemaphoreType.DMA((n, N_DEV - 1)),
                        pltpu.SemaphoreType.DMA((n,))],
    )(*arrs)
    return list(outs)


def _peers(ix, iy, ic):
    out = []
    for d in range(1, N_DEV):
        px = 1 - ix if d & 4 else ix
        py = 1 - iy if d & 2 else iy
        pc = 1 - ic if d & 1 else ic
        out.append((d - 1, (px, py, pc), 4 * px + 2 * py + pc))
    return out


def _exchange_start(arrs, gather, name, after=None):
    n = len(arrs)
    lands = [lax.empty(((N_DEV,) + a.shape) if g else a.shape, a.dtype) for a, g in zip(arrs, gather)]
    extra = [] if after is None else [after]

    def body(*refs):
        ins, land_refs = refs[:n], refs[n:2 * n]
        send_sems, recv_sems = refs[2 * n + len(extra)], refs[2 * n + len(extra) + 1]
        token = refs[-1]
        ix, iy, ic = lax.axis_index("x"), lax.axis_index("y"), lax.axis_index("c")
        me = 4 * ix + 2 * iy + ic
        for a in range(n):
            for k, peer, peer_idx in _peers(ix, iy, ic):
                pltpu.make_async_remote_copy(
                    src_ref=ins[a] if gather[a] else ins[a].at[peer_idx], dst_ref=land_refs[a].at[me],
                    send_sem=send_sems.at[7 * a + k], recv_sem=recv_sems.at[7 * a + k], device_id=peer,
                    device_id_type=pl.DeviceIdType.MESH).start()
        token[...] = jnp.zeros_like(token)

    hbm = pl.BlockSpec(memory_space=pltpu.HBM)
    sem = pl.BlockSpec(memory_space=pltpu.SEMAPHORE)
    held = [pltpu.with_memory_space_constraint(a, pltpu.HBM) for a in list(arrs) + lands]
    outs = pl.pallas_call(
        body, name=name,
        out_shape=(pltpu.SemaphoreType.DMA((7 * n,)), pltpu.SemaphoreType.DMA((7 * n,)),
                   *[pltpu.HBM(a.shape, a.dtype) for a in held], jax.ShapeDtypeStruct((8, 128), F32)),
        in_specs=[hbm] * (2 * n) + [pl.BlockSpec(memory_space=pl.ANY)] * len(extra),
        out_specs=(sem, sem, *[hbm] * (2 * n), pl.BlockSpec(memory_space=pltpu.VMEM)),
        input_output_aliases={a: 2 + a for a in range(2 * n)},
        compiler_params=pltpu.CompilerParams(has_side_effects=pltpu.SideEffectType.DATAFLOW_SIDE_EFFECTING),
    )(*held, *extra)
    return dict(send=outs[0], recv=outs[1], srcs=list(outs[2:2 + n]), lands=list(outs[2 + n:2 + 2 * n]),
                token=outs[-1], gather=list(gather))


def _exchange_wait(flight, after, name):
    n = len(flight["srcs"])
    gather = flight["gather"]

    def body(*refs):
        srcs, land_refs = refs[:n], refs[n:2 * n]
        send_sems, recv_sems = refs[2 * n], refs[2 * n + 1]
        land_outs, own_sems = refs[3 * n + 3:4 * n + 3], refs[-1]
        ix, iy, ic = lax.axis_index("x"), lax.axis_index("y"), lax.axis_index("c")
        mine = 4 * ix + 2 * iy + ic
        own = [pltpu.make_async_copy(srcs[a] if gather[a] else srcs[a].at[mine], land_outs[a].at[mine], own_sems.at[a])
               for a in range(n)]
        for cp in own:
            cp.start()
        for a in range(n):
            for k, peer, peer_idx in _peers(ix, iy, ic):
                cp = pltpu.make_async_remote_copy(
                    src_ref=srcs[a] if gather[a] else srcs[a].at[peer_idx], dst_ref=land_refs[a].at[mine],
                    send_sem=send_sems.at[7 * a + k], recv_sem=recv_sems.at[7 * a + k], device_id=peer,
                    device_id_type=pl.DeviceIdType.MESH)
                cp.wait_send()
                cp.wait_recv()
        for cp in own:
            cp.wait()

    hbm = pl.BlockSpec(memory_space=pltpu.HBM)
    sem = pl.BlockSpec(memory_space=pltpu.SEMAPHORE)
    held = flight["srcs"] + flight["lands"]
    outs = pl.pallas_call(
        body, name=name, out_shape=tuple(pltpu.HBM(a.shape, a.dtype) for a in held),
        in_specs=[hbm] * (2 * n) + [sem, sem, pl.BlockSpec(memory_space=pl.ANY)], out_specs=tuple([hbm] * (2 * n)),
        scratch_shapes=[pltpu.SemaphoreType.DMA((n,))],
        input_output_aliases={a: a for a in range(2 * n)},
        compiler_params=pltpu.CompilerParams(has_side_effects=pltpu.SideEffectType.DATAFLOW_SIDE_EFFECTING),
    )(*held, flight["send"], flight["recv"], after)
    return list(outs[n:2 * n])


def _ada_fwd(c_all, ada_w, ada_b_cols):
    nb, D = c_all.shape
    cols = ada_w.shape[2]

    def body(c_ref, w_ref, b_ref, out_ref):
        ca = _silu(c_ref[...])
        for l in range(DEPTH):
            out_ref[l] = _mm(ca, w_ref[l]) + b_ref[l:l + 1, :]

    return pl.pallas_call(
        body, name="ada_fwd", out_shape=jax.ShapeDtypeStruct((DEPTH, nb, cols), F32),
        in_specs=[pl.BlockSpec(memory_space=pltpu.VMEM)] * 3, out_specs=pl.BlockSpec(memory_space=pltpu.VMEM),
        compiler_params=pltpu.CompilerParams(vmem_limit_bytes=32 * VMEM_MB),
    )(c_all, ada_w, ada_b_cols)


def _ada_bwd(c_all, d_mod_cols):
    nb, D = c_all.shape
    cols = d_mod_cols.shape[2]

    def body(c_ref, dm_ref, out_ref):
        ca = _silu(c_ref[...])
        for l in range(DEPTH):
            out_ref[l] = _mm_tn(ca, dm_ref[l])

    return pl.pallas_call(
        body, name="ada_bwd", out_shape=jax.ShapeDtypeStruct((DEPTH, D, cols), F32),
        in_specs=[pl.BlockSpec(memory_space=pltpu.VMEM)] * 2, out_specs=pl.BlockSpec(memory_space=pltpu.VMEM),
        compiler_params=pltpu.CompilerParams(vmem_limit_bytes=32 * VMEM_MB),
    )(c_all, d_mod_cols)


def _sum_adamw(parts, w, m, v, name, after=None):
    P, R, C = parts.shape
    tr = 256 if (R % 256 == 0 and R > 256) else R
    extra = [] if after is None else [after]

    def body(p_ref, w_ref, m_ref, v_ref, *rest):
        g_ref, d_ref, nm_ref, nv_ref = rest[-4:]
        g = p_ref[0].astype(F32)
        for k in range(1, P):
            g = g + p_ref[k].astype(F32)
        g_ref[...] = g
        nm = ADAM_B1 * m_ref[...] + (1.0 - ADAM_B1) * g
        nv = ADAM_B2 * v_ref[...] + (1.0 - ADAM_B2) * (g * g)
        nm_ref[...] = nm
        nv_ref[...] = nv
        m_hat = nm / (1.0 - ADAM_B1 ** ADAM_STEP)
        v_hat = nv / (1.0 - ADAM_B2 ** ADAM_STEP)
        d_ref[...] = -ADAM_LR * (m_hat / (jnp.sqrt(v_hat) + ADAM_EPS) + ADAM_WD * w_ref[...])

    blk = pl.BlockSpec((tr, C), lambda i: (i, 0))
    shp = jax.ShapeDtypeStruct((R, C), F32)
    return pl.pallas_call(
        body, name=name, grid=(R // tr,),
        in_specs=[pl.BlockSpec((P, tr, C), lambda i: (0, i, 0)), blk, blk, blk]
        + [pl.BlockSpec(memory_space=pl.ANY)] * len(extra),
        out_specs=[blk, blk, blk, blk], out_shape=[shp, shp, shp, shp],
        compiler_params=_cp(("parallel",)),
    )(parts, w, m, v, *extra)


def _sum_adamw_layer(parts, w, m, v, layer, name, prev=None, after=None):
    P, R, C = parts.shape
    tr = 256 if (R % 256 == 0 and R > 256) else R

    def body(p_ref, w_ref, m_ref, v_ref, *rest):
        g_ref, d_ref, nm_ref, nv_ref = rest[-4:]
        g = p_ref[0].astype(F32)
        for k in range(1, P):
            g = g + p_ref[k].astype(F32)
        g_ref[0] = g
        nm = ADAM_B1 * m_ref[0] + (1.0 - ADAM_B1) * g
        nv = ADAM_B2 * v_ref[0] + (1.0 - ADAM_B2) * (g * g)
        nm_ref[0] = nm
        nv_ref[0] = nv
        m_hat = nm / (1.0 - ADAM_B1 ** ADAM_STEP)
        v_hat = nv / (1.0 - ADAM_B2 ** ADAM_STEP)
        d_ref[0] = -ADAM_LR * (m_hat / (jnp.sqrt(v_hat) + ADAM_EPS) + ADAM_WD * w_ref[0])

    blk = pl.BlockSpec((1, tr, C), lambda i: (layer, i, 0))
    shp = jax.ShapeDtypeStruct(w.shape, F32)
    in_specs = [pl.BlockSpec((P, tr, C), lambda i: (0, i, 0)), blk, blk, blk]
    args = [parts, w, m, v]
    aliases = {}
    if prev is not None:
        in_specs += [pl.BlockSpec(memory_space=pl.ANY)] * 4
        args += list(prev)
        aliases = {4 + k: k for k in range(4)}
    if after is not None:
        in_specs.append(pl.BlockSpec(memory_space=pl.ANY))
        args.append(after)
    return list(pl.pallas_call(
        body, name=name, grid=(R // tr,), in_specs=in_specs, out_specs=[blk] * 4, out_shape=[shp] * 4,
        input_output_aliases=aliases, compiler_params=_cp(("parallel",)),
    )(*args))


SMALL = ["norm_w", "mla_q_norm", "mla_kv_norm", "gla_w_g2", "gla_b_g2", "gla_norm", "final_norm"]


SMALL_ROWS = 72


def _pack_small(loss, part):
    flat = [jnp.pad(loss.reshape(1), (0, 127))] + [part[n].reshape(-1) for n in SMALL]
    used = sum(f.shape[0] for f in flat)
    flat.append(jnp.zeros((SMALL_ROWS * 128 - used,), F32))
    return jnp.concatenate(flat).reshape(SMALL_ROWS, 128)


def _small_adamw(packed_parts, w, m, v, after=None):
    n = len(w)
    extra = [] if after is None else [after]

    def body(*refs):
        p_ref = refs[0]
        w_refs, m_refs, v_refs = refs[1:1 + n], refs[1 + n:1 + 2 * n], refs[1 + 2 * n:1 + 3 * n]
        outs, acc = refs[1 + 3 * n + len(extra):-1], refs[-1]
        total = p_ref[0]
        for k in range(1, N_DEV):
            total = total + p_ref[k]
        acc[...] = total
        outs[0][...] = acc[0:1, :]
        r0 = 1
        for i in range(n):
            shp = w_refs[i].shape
            if len(shp) == 3:
                g = acc[r0:r0 + shp[0] * shp[1], :].reshape(shp)
                r0 += shp[0] * shp[1]
            elif shp[1] < 128:
                g = acc[r0:r0 + shp[0], 0:shp[1]]
                r0 += shp[0]
            else:
                k = shp[1] // 128
                g = jnp.concatenate(
                    [jnp.concatenate([acc[r0 + l * k + j:r0 + l * k + j + 1, :] for j in range(k)], axis=1)
                     for l in range(shp[0])], axis=0)
                r0 += shp[0] * k
            nm = ADAM_B1 * m_refs[i][...] + (1.0 - ADAM_B1) * g
            nv = ADAM_B2 * v_refs[i][...] + (1.0 - ADAM_B2) * (g * g)
            m_hat = nm / (1.0 - ADAM_B1 ** ADAM_STEP)
            v_hat = nv / (1.0 - ADAM_B2 ** ADAM_STEP)
            outs[1 + 4 * i][...] = g
            outs[2 + 4 * i][...] = -ADAM_LR * (m_hat / (jnp.sqrt(v_hat) + ADAM_EPS) + ADAM_WD * w_refs[i][...])
            outs[3 + 4 * i][...] = nm
            outs[4 + 4 * i][...] = nv

    vmem = pl.BlockSpec(memory_space=pltpu.VMEM)
    out_shape = [jax.ShapeDtypeStruct((1, 128), F32)]
    for a in w:
        out_shape += [jax.ShapeDtypeStruct(a.shape, F32)] * 4
    outs = pl.pallas_call(
        body, name="adamw_small", in_specs=[vmem] * (1 + 3 * n) + [pl.BlockSpec(memory_space=pl.ANY)] * len(extra),
        out_specs=[vmem] * (1 + 4 * n), out_shape=out_shape, scratch_shapes=[pltpu.VMEM((SMALL_ROWS, 128), F32)],
    )(packed_parts, *w, *m, *v, *extra)
    return outs[0], [outs[1 + 4 * i:5 + 4 * i] for i in range(n)]


WEIGHTS = ["norm_w", "ada_w", "ada_b", "w_in", "mla_q_norm", "w_uq", "mla_kv_norm", "w_ukv", "gla_w_g2",
           "gla_b_g2", "gla_norm", "w_out", "final_norm"]


def kernel(x, c, positions, norm_w, ada_w, ada_b, w_in, mla_q_norm, w_uq, mla_kv_norm, w_ukv, gla_w_g2, gla_b_g2, gla_norm, w_out, final_norm, loss_target, m_norm_w, m_ada_w, m_ada_b, m_w_in, m_mla_q_norm, m_w_uq, m_mla_kv_norm, m_w_ukv, m_gla_w_g2, m_gla_b_g2, m_gla_norm, m_w_out, m_final_norm, v_norm_w, v_ada_w, v_ada_b, v_w_in, v_mla_q_norm, v_w_uq, v_mla_kv_norm, v_w_ukv, v_gla_w_g2, v_gla_b_g2, v_gla_norm, v_w_out, v_final_norm):
    w = dict(norm_w=norm_w, ada_w=ada_w, ada_b=ada_b, w_in=w_in, mla_q_norm=mla_q_norm, w_uq=w_uq,
             mla_kv_norm=mla_kv_norm, w_ukv=w_ukv, gla_w_g2=gla_w_g2, gla_b_g2=gla_b_g2, gla_norm=gla_norm,
             w_out=w_out, final_norm=final_norm)
    m = dict(norm_w=m_norm_w, ada_w=m_ada_w, ada_b=m_ada_b, w_in=m_w_in, mla_q_norm=m_mla_q_norm, w_uq=m_w_uq,
             mla_kv_norm=m_mla_kv_norm, w_ukv=m_w_ukv, gla_w_g2=m_gla_w_g2, gla_b_g2=m_gla_b_g2,
             gla_norm=m_gla_norm, w_out=m_w_out, final_norm=m_final_norm)
    v = dict(norm_w=v_norm_w, ada_w=v_ada_w, ada_b=v_ada_b, w_in=v_w_in, mla_q_norm=v_mla_q_norm, w_uq=v_w_uq,
             mla_kv_norm=v_mla_kv_norm, w_ukv=v_w_ukv, gla_w_g2=v_gla_w_g2, gla_b_g2=v_gla_b_g2,
             gla_norm=v_gla_norm, w_out=v_w_out, final_norm=v_final_norm)
    B, S, D = x.shape
    me = 4 * lax.axis_index("x") + 2 * lax.axis_index("y") + lax.axis_index("c")
    ada_cols = ada_w.shape[2]
    cast = lambda a: a.astype(_MXU)

    sharded = ["w_in", "w_uq", "w_ukv", "w_out"]

    whole_in = _arrange_w_in
    whole_rest = lambda blks: (*_arrange_mla_weights(blks[0], blks[1]), blks[2].reshape(D, D))
    blocks_in = lambda dw_in_a: _unarrange_w_in(dw_in_a, N_DEV, w_in.shape[2])
    blocks_rest = lambda dw_out, dw_uq_a, dw_ukv_a: [
        *_unarrange_mla_weights(dw_uq_a, dw_ukv_a), dw_out.reshape(N_DEV, D // N_DEV, D).astype(jnp.bfloat16)]

    (c_g,) = _exchange([c], [True], "gather_c")
    c_all = c_g.reshape(N_DEV * B, D)

    ada_b_cols = lax.dynamic_slice(ada_b, (0, me * ada_cols), (DEPTH, ada_cols))
    mod_cols = _ada_fwd(c_all, ada_w, ada_b_cols)
    mod_send = jnp.transpose(mod_cols.reshape(DEPTH, N_DEV, B, ada_cols), (1, 0, 2, 3))
    (mod_recv,) = _exchange([mod_send], [False], "scatter_mod")
    mod = jnp.transpose(mod_recv, (1, 2, 0, 3)).reshape(DEPTH, B, 3 * D)

    flight_i = _exchange_start([cast(w_in[0])], [True], "gather_start_first", after=mod)
    flight_r = _exchange_start([cast(w[n][0]) for n in sharded[1:]], [True] * 3, "gather_start_layer0",
                               after=flight_i["token"])
    flight_w = _exchange_start([cast(w[n][1]) for n in sharded], [True] * 4, "gather_start_layer1",
                               after=flight_r["token"])
    small_w = {n: w[n] for n in SMALL}
    layer_small = lambda l: {n: a[l] for n, a in small_w.items() if n != "final_norm"}
    tabs = _rope_tables(positions.reshape(B, S, 1), flight_w["token"][0, 0])
    late0 = lambda after: whole_rest(_exchange_wait(flight_r, after, "gather_wait_layer0"))
    (w_in0_g,) = _exchange_wait(flight_i, tabs[0], "gather_wait_first")
    x1, saved0 = _layer_fwd(x, tabs, mod[0], layer_small(0), whole_in(w_in0_g), late_weights=late0)
    got1 = _exchange_wait(flight_w, x1, "gather_wait_layer1")
    (dx, loss, d_fw), saved1 = _layer_fwd(x1, tabs, mod[1], layer_small(1), whole_in(got1[0]), *whole_rest(got1[1:]),
                                          loss_head=(final_norm.reshape(1, D), loss_target))

    dx, g1 = _layer_bwd(dx, saved1, tabs)
    flight_g = _exchange_start([blocks_in(g1["w_in_a"])] + blocks_rest(g1["w_out"], g1["w_uq_a"], g1["w_ukv_a"]),
                               [False] * 4, "grads_start_layer1")
    flights = {}

    def early0(dw_out, dw_uq_a, dw_ukv_a):
        flights["rest0"] = _exchange_start(blocks_rest(dw_out, dw_uq_a, dw_ukv_a), [False] * 3, "grads_start_layer0")
        return flights["rest0"]["token"][0, 0]

    saved0 = dict(saved0, gate=saved0["gate"] + flight_g["token"][0, 0])
    grad_x, g0 = _layer_bwd(dx, saved0, tabs, early_grads=early0)
    parts1 = _exchange_wait(flight_g, grad_x, "grads_wait_layer1")
    rest0 = _exchange_wait(flights["rest0"], g0["w_in_a"], "grads_wait_layer0")

    both = lambda n: jnp.stack([g0[n], g1[n]])
    d_mod = both("d_mod")
    part = dict(norm_w=both("norm_w"), mla_q_norm=both("mla_q_norm"), mla_kv_norm=both("mla_kv_norm"),
                gla_w_g2=both("gla_w_g2"), gla_b_g2=both("gla_b_g2"), gla_norm=both("gla_norm256")[:, 0:128],
                final_norm=d_fw)
    d_mod_g, small_g = _exchange([d_mod, _pack_small(loss, part)], [True, True], "gather_small")
    flight_l = _exchange_start([blocks_in(g0["w_in_a"])], [False], "exchange_start_last", after=small_g)
    res = {}
    behind = flight_l["token"]
    for a, name in enumerate(sharded):
        res[name] = _sum_adamw_layer(parts1[a], w[name], m[name], v[name], 1, "adamw_%s_layer1" % name, after=behind)
        behind = res[name][1]
    for a, name in enumerate(sharded[1:]):
        res[name] = _sum_adamw_layer(rest0[a], w[name], m[name], v[name], 0, "adamw_%s_layer0" % name,
                                     prev=res[name], after=behind)
        behind = res[name][1]

    d_mod_all = jnp.transpose(d_mod_g, (1, 0, 2, 3)).reshape(DEPTH, N_DEV * B, 3 * D)
    d_mod_cols = lax.dynamic_slice(d_mod_all, (0, 0, me * ada_cols), (DEPTH, N_DEV * B, ada_cols))
    g_ada_w = _ada_bwd(c_all, d_mod_cols)

    def update(name, parts2d, after):
        shp = w[name].shape
        two = lambda a: a.reshape(parts2d.shape[1:])
        out = _sum_adamw(parts2d, two(w[name]), two(m[name]), two(v[name]), "adamw_" + name, after=after)
        res[name] = [o.reshape(shp) for o in out]
        return out[1]

    behind = update("ada_w", g_ada_w.reshape(1, DEPTH * D, ada_cols), behind)
    behind = update("ada_b", jnp.transpose(d_mod_g, (0, 2, 1, 3)).reshape(N_DEV * B, DEPTH * 3 * D // 128, 128), behind)
    row = lambda a: a.reshape(1, D) if a.ndim == 1 else a
    loss_sum, small_out = _small_adamw(small_g, [row(w[n]) for n in SMALL], [row(m[n]) for n in SMALL],
                                       [row(v[n]) for n in SMALL], after=behind)
    for n, outs in zip(SMALL, small_out):
        res[n] = [o.reshape(w[n].shape) for o in outs]
    loss_out = loss_sum[0, 0]
    (in0,) = _exchange_wait(flight_l, loss_sum, "exchange_wait_last")
    res["w_in"] = _sum_adamw_layer(in0, w_in, m_w_in, v_w_in, 0, "adamw_w_in_layer0", prev=res["w_in"])
    return (loss_out, grad_x, *[res[n][0] for n in WEIGHTS], *[res[n][1] for n in WEIGHTS],
            *[res[n][2] for n in WEIGHTS], *[res[n][3] for n in WEIGHTS])
```

```python
import functools
import math

import numpy as np
import jax
import jax.numpy as jnp
from jax import lax
from jax.experimental import pallas as pl
from jax.experimental.pallas import tpu as pltpu

F32 = jnp.float32
_MXU = jnp.bfloat16

D_MODEL = 1024
DEPTH = 2
CHUNK = 64
EPS = 1e-6
ROPE_THETA = 10000.0
N_DEV = 8

MLA_SCALE = 96.0 ** -0.5
RET_KSCALE = 64.0 ** -0.5
GLA_KSCALE = 32.0 ** -0.5
GLA_TAU = 16.0

ADAM_LR = 0.001
ADAM_B1 = 0.9
ADAM_B2 = 0.999
ADAM_EPS = 1e-08
ADAM_WD = 0.01
ADAM_STEP = 10

RET_W, MLA_W, GLA_W = 1024, 1024, 896
ARR_W = RET_W + MLA_W + GLA_W
VMEM_MB = 1024 * 1024


def _cp(sem, vmem_mb=48):
    return pltpu.CompilerParams(dimension_semantics=sem, vmem_limit_bytes=vmem_mb * VMEM_MB)


def _mm(a, b):
    return jnp.dot(a.astype(_MXU), b.astype(_MXU), preferred_element_type=F32)


def _mm_nt(a, b):
    return lax.dot_general(a.astype(_MXU), b.astype(_MXU), (((1,), (1,)), ((), ())),
                           preferred_element_type=F32)


def _mm_tn(a, b):
    return lax.dot_general(a.astype(_MXU), b.astype(_MXU), (((0,), (0,)), ((), ())),
                           preferred_element_type=F32)


def _mm_f32(a, b):
    return jnp.dot(a, b, precision=lax.Precision.HIGHEST, preferred_element_type=F32)


def _sig(z):
    return 1.0 / (1.0 + jnp.exp(-z))


def _silu(z):
    return z * _sig(z)


def _dsilu(z):
    s = _sig(z)
    return s * (1.0 + z * (1.0 - s))


def _full(shape):
    nd = len(shape)
    return pl.BlockSpec(shape, lambda *_: (0,) * nd)


def _w_in_runs(block_cols):
    m, g = RET_W, RET_W + MLA_W
    whole = [(base + 64 * h + 32 * t, 32, base + 128 * t + 32 * h)
             for base in (0, 256) for t in range(2) for h in range(4)]
    whole += [(512, 512, 512), (1024, 384, m), (1408, 32, m + 448), (1440, 512, m + 512),
              (1952, 528, g), (2480, 256, g + 640)]
    zeros = [(m + 384, 64), (m + 480, 32), (g + 528, 112)]
    runs = []
    for src, n, dst in whole:
        while n:
            blk, off = divmod(src, block_cols)
            k = min(n, block_cols - off)
            runs.append((blk, off, k, dst))
            src, n, dst = src + k, n - k, dst + k
    return runs, zeros


def _arrange_w_in(blocks, tm=256):
    n, rows, cols = blocks.shape
    runs, zeros = _w_in_runs(cols)

    def arrange_w_in_kernel(b_ref, a_ref):
        for dst, k in zeros:
            a_ref[:, dst:dst + k] = jnp.zeros((tm, k), a_ref.dtype)
        for blk, off, k, dst in runs:
            a_ref[:, dst:dst + k] = b_ref[blk, :, off:off + k]

    return pl.pallas_call(
        arrange_w_in_kernel, grid=(rows // tm,),
        in_specs=[pl.BlockSpec((n, tm, cols), lambda i: (0, i, 0))],
        out_specs=pl.BlockSpec((tm, ARR_W), lambda i: (i, 0)),
        out_shape=jax.ShapeDtypeStruct((rows, ARR_W), blocks.dtype),
        compiler_params=_cp(("parallel",)), name="arrange_w_in")(blocks)


def _unarrange_w_in(a, n, cols, tm=256):
    rows = a.shape[0]
    runs, _ = _w_in_runs(cols)

    def unarrange_w_in_kernel(a_ref, b_ref):
        for blk, off, k, dst in runs:
            b_ref[blk, :, off:off + k] = a_ref[:, dst:dst + k].astype(b_ref.dtype)

    return pl.pallas_call(
        unarrange_w_in_kernel, grid=(rows // tm,),
        in_specs=[pl.BlockSpec((tm, ARR_W), lambda i: (i, 0))],
        out_specs=pl.BlockSpec((n, tm, cols), lambda i: (0, i, 0)),
        out_shape=jax.ShapeDtypeStruct((n, rows, cols), jnp.bfloat16),
        compiler_params=_cp(("parallel",)), name="unarrange_w_in")(a)


def _arrange_mla_weights(uq_heads, ukv_heads):
    nh = uq_heads.shape[0]
    dt = uq_heads.dtype

    def arrange_mla_weights_kernel(uq_ref, ukv_ref, q_ref, kv_ref, kvt_ref):
        q_ref[...] = jnp.zeros(q_ref.shape, dt)
        kv_ref[...] = jnp.zeros(kv_ref.shape, dt)
        kvt_ref[...] = jnp.zeros(kvt_ref.shape, dt)
        for h in range(nh):
            q_ref[:, 128 * h:128 * h + 96] = uq_ref[h]
            blk = ukv_ref[h]
            kv_ref[:, 128 * h:128 * h + 64] = blk[:, 0:64]
            kv_ref[:, 128 * nh + 64 * h:128 * nh + 64 * h + 64] = blk[:, 64:128]
            blk_t = blk.astype(F32).T.astype(dt)
            kvt_ref[128 * h:128 * h + 64, :] = blk_t[0:64]
            kvt_ref[128 * nh + 64 * h:128 * nh + 64 * h + 64, :] = blk_t[64:128]

    return pl.pallas_call(
        arrange_mla_weights_kernel, name="arrange_mla_weights",
        out_shape=[jax.ShapeDtypeStruct((256, 128 * nh), dt), jax.ShapeDtypeStruct((128, 192 * nh), dt),
                   jax.ShapeDtypeStruct((192 * nh, 128), dt)])(uq_heads, ukv_heads)


def _unarrange_mla_weights(dw_uq_a, dw_ukv_a):
    nh = dw_uq_a.shape[1] // 128

    def unarrange_mla_weights_kernel(q_ref, kv_ref, uq_ref, ukv_ref):
        for h in range(nh):
            uq_ref[h] = q_ref[:, 128 * h:128 * h + 96].astype(uq_ref.dtype)
            ukv_ref[h, :, 0:64] = kv_ref[:, 128 * h:128 * h + 64].astype(ukv_ref.dtype)
            ukv_ref[h, :, 64:128] = kv_ref[:, 128 * nh + 64 * h:128 * nh + 64 * h + 64].astype(ukv_ref.dtype)

    return pl.pallas_call(
        unarrange_mla_weights_kernel, name="unarrange_mla_weights",
        out_shape=[jax.ShapeDtypeStruct((nh, 256, 96), jnp.bfloat16),
                   jax.ShapeDtypeStruct((nh, 128, 128), jnp.bfloat16)])(dw_uq_a, dw_ukv_a)


def _rope_tables(pos3, zero=0.0):
    B, S, _ = pos3.shape
    ts = min(S, 512)
    inv32 = (np.float32(ROPE_THETA) ** (-(np.arange(32, dtype=np.float32) / 32))).astype(np.float32)
    inv16 = (np.float32(ROPE_THETA) ** (-(np.arange(16, dtype=np.float32) / 16))).astype(np.float32)
    inv = np.zeros((1, 128), np.float32)
    inv[0, 0:32] = inv32
    inv[0, 32:48] = inv16

    def body(pos_ref, inv_ref, cr, sr, cm, sm):
        ang = pos_ref[0].astype(F32) * inv_ref[...]
        lane = lax.broadcasted_iota(jnp.int32, (1, 128), 1)

        def every_head(x):
            y = jnp.where(lane < 32, x, pltpu.roll(x, 32, 1))
            return jnp.where(lane < 64, y, pltpu.roll(y, 64, 1))

        def rotary_pair(x, fill):
            return jnp.where((lane >= 64) & (lane < 80), pltpu.roll(x, 32, 1),
                             jnp.where((lane >= 80) & (lane < 96), pltpu.roll(x, 48, 1), fill))

        c, s = jnp.cos(ang), jnp.sin(ang)
        cr[0] = every_head(c)
        sr[0] = every_head(s)
        cm[0] = rotary_pair(c, 1.0)
        sm[0] = rotary_pair(s, 0.0)

    tab = jax.ShapeDtypeStruct((B, S, 128), F32)
    blk = pl.BlockSpec((1, ts, 128), lambda b, i: (b, i, 0))
    return pl.pallas_call(
        body, name="rope_tables", grid=(B, S // ts),
        in_specs=[pl.BlockSpec((1, ts, 1), lambda b, i: (b, i, 0)), _full((1, 128))],
        out_specs=[blk, blk, blk, blk], out_shape=[tab, tab, tab, tab],
        compiler_params=_cp(("parallel", "parallel")),
    )(pos3, jnp.asarray(inv) + zero)


def _rope128(x, cos, sin):
    lane = lax.broadcasted_iota(jnp.int32, (1, 128), 1)
    rp = pltpu.roll(x, 16, 1)
    rm = pltpu.roll(x, 112, 1)
    return x * cos + jnp.where(lane < 80, -rm, rp) * sin


def _rope128_t(d, cos, sin):
    lane = lax.broadcasted_iota(jnp.int32, (1, 128), 1)
    y = d * sin
    yp = pltpu.roll(y, 16, 1)
    ym = pltpu.roll(y, 112, 1)
    return d * cos + jnp.where(lane < 64, 0.0, jnp.where(lane < 80, ym, jnp.where(lane < 96, -yp, 0.0)))


def _proj_fwd(x, shift, scale, nw, w_arr):
    B, S, D = x.shape
    tm = min(S, 512)

    def body(x_ref, sh_ref, sc_ref, nw_ref, w_ref, ret_ref, mla_ref, gla_ref, h_ref):
        xv = x_ref[0]
        rstd = lax.rsqrt(jnp.mean(xv * xv, axis=-1, keepdims=True) + EPS)
        h = (xv * rstd * nw_ref[...]) * (1.0 + sc_ref[0]) + sh_ref[0]
        hb = h.astype(_MXU)
        h_ref[0] = hb
        ret_ref[0] = jnp.dot(hb, w_ref[:, 0:RET_W], preferred_element_type=F32).astype(_MXU)
        mla_ref[0] = jnp.dot(hb, w_ref[:, RET_W:RET_W + MLA_W], preferred_element_type=F32).astype(_MXU)
        gla_ref[0] = jnp.dot(hb, w_ref[:, RET_W + MLA_W:ARR_W], preferred_element_type=F32).astype(_MXU)

    tok = lambda w: pl.BlockSpec((1, tm, w), lambda b, i: (b, i, 0))
    per_seq = pl.BlockSpec((1, 1, D), lambda b, i: (b, 0, 0))
    return pl.pallas_call(
        body, name="proj_fwd", grid=(B, S // tm),
        in_specs=[tok(D), per_seq, per_seq, _full((1, D)), _full((D, ARR_W))],
        out_specs=[tok(RET_W), tok(MLA_W), tok(GLA_W), tok(D)],
        out_shape=[jax.ShapeDtypeStruct((B, S, RET_W), _MXU), jax.ShapeDtypeStruct((B, S, MLA_W), _MXU),
                   jax.ShapeDtypeStruct((B, S, GLA_W), _MXU), jax.ShapeDtypeStruct((B, S, D), _MXU)],
        compiler_params=_cp(("parallel", "parallel")),
    )(x, shift, scale, nw, w_arr)


RET_L = 256


def _ret_consts(L):
    lg = np.log1p(-np.exp2(-5.0 - np.arange(4, dtype=np.float32))).astype(np.float32)
    i = np.arange(L)
    ci = i // CHUNK
    diff = (i[:, None] - i[None, :]).astype(np.float32)
    same = ci[:, None] == ci[None, :]
    past = ci[None, :] < ci[:, None]
    expo = np.where(same, np.abs(diff), np.where(past, diff, 0.0)).astype(np.float32)
    dec = np.where((same | past)[None], np.exp(lg[:, None, None] * expo[None]), 0.0).astype(np.float32)
    head = (np.arange(256) % 128) // 32
    qw = np.exp((i + 1.0)[:, None] * lg[head][None, :]).astype(np.float32)
    kw = np.exp((L - 1.0 - i)[:, None] * lg[head][None, :]).astype(np.float32)
    a_row = np.exp(np.float32(L) * lg[head])[None, :].astype(np.float32)
    return [jnp.asarray(t) for t in (dec.reshape(4 * L, L), qw, kw, a_row)]


def _ret_masks():
    lane = lax.broadcasted_iota(jnp.int32, (1, 256), 1)
    mh = [((lane % 128) // 32) == h for h in range(4)]
    mv = [(lane // 64) == h for h in range(4)]
    vi = lax.broadcasted_iota(jnp.int32, (256, 256), 0)
    ki = lax.broadcasted_iota(jnp.int32, (256, 256), 1)
    bd = (vi // 64) == ((ki % 128) // 32)
    return mh, mv, bd


def _ret_rope(p, cs, sn):
    q1, q2, k1, k2 = p[:, 0:128], p[:, 128:256], p[:, 256:384], p[:, 384:512]
    qr = jnp.concatenate([q1 * cs - q2 * sn, q2 * cs + q1 * sn], axis=1)
    kr = jnp.concatenate([k1 * cs - k2 * sn, k2 * cs + k1 * sn], axis=1) * RET_KSCALE
    return qr, kr


def _head_mean(x, mv, width):
    out = jnp.zeros_like(x)
    for m in mv:
        s = jnp.sum(jnp.where(m, x, 0.0), axis=-1, keepdims=True) * (1.0 / width)
        out = jnp.where(m, s, out)
    return out


def _stack_heads(x, masks):
    return jnp.concatenate([jnp.where(m, x, 0.0) for m in masks], axis=0)


def _fold_heads(xs, masks, L):
    out = jnp.where(masks[0], xs[0:L], 0.0)
    for h in range(1, 4):
        out = out + jnp.where(masks[h], xs[h * L:(h + 1) * L], 0.0)
    return out


RET_G = 2


def _ret_fwd(ret_p, cos, sin):
    B, S, _ = ret_p.shape
    L = min(RET_L, S)
    NB = S // L
    G = min(RET_G, NB)
    NG = NB // G
    consts = _ret_consts(L)

    def body(p_ref, c_ref, s_ref, ds_ref, qw_ref, kw_ref, a_ref, out_ref, raw_ref, st_ref, st_sc):
        @pl.when(pl.program_id(1) == 0)
        def _():
            st_sc[...] = jnp.zeros_like(st_sc)

        mh, mv, bd = _ret_masks()
        cs_ = range(G)
        rows = [slice(c * L, (c + 1) * L) for c in cs_]
        ps = [p_ref[0, rows[c], :].astype(F32) for c in cs_]
        qk = [_ret_rope(ps[c], c_ref[0, rows[c], :], s_ref[0, rows[c], :]) for c in cs_]
        vs = [ps[c][:, 512:768] for c in cs_]
        a_s = [_mm_nt(_stack_heads(qk[c][0], mh), qk[c][1]) for c in cs_]
        upd = [_mm_tn(vs[c], qk[c][1] * kw_ref[...]) for c in cs_]
        o_s = [_mm(a_s[c] * ds_ref[...], vs[c]) for c in cs_]
        st = st_sc[...]
        inter = []
        for c in cs_:
            st_ref[0, c] = st
            inter.append(_mm_nt(qk[c][0] * qw_ref[...], st))
            st = st * a_ref[...] + jnp.where(bd, upd[c], 0.0)
        st_sc[...] = st
        for c in cs_:
            r = _fold_heads(o_s[c], mv, L) + inter[c]
            raw_ref[0, rows[c], :] = r
            rstd = lax.rsqrt(_head_mean(r * r, mv, 64.0) + EPS)
            out_ref[0, rows[c], :] = (r * rstd * _silu(ps[c][:, 768:1024])).astype(_MXU)

    tok = lambda w: pl.BlockSpec((1, G * L, w), lambda b, n: (b, n, 0))
    return pl.pallas_call(
        body, name="ret_fwd", grid=(B, NG),
        in_specs=[tok(RET_W), tok(128), tok(128), _full((4 * L, L)), _full((L, 256)), _full((L, 256)),
                  _full((1, 256))],
        out_specs=[tok(256), tok(256), pl.BlockSpec((1, G, 256, 256), lambda b, n: (b, n, 0, 0))],
        out_shape=[jax.ShapeDtypeStruct((B, S, 256), _MXU), jax.ShapeDtypeStruct((B, S, 256), F32),
                   jax.ShapeDtypeStruct((B, NB, 256, 256), F32)],
        scratch_shapes=[pltpu.VMEM((256, 256), F32)],
        compiler_params=_cp(("parallel", "arbitrary")),
    )(ret_p, cos, sin, *consts)


def _ret_bwd(ret_p, cos, sin, raw, states, d_mix):
    B, S, _ = ret_p.shape
    L = min(RET_L, S)
    NB = S // L
    G = 1
    NG = NB // G
    consts = _ret_consts(L)

    def body(p_ref, c_ref, s_ref, raw_ref, st_ref, dm_ref, ds_ref, qw_ref, kw_ref, a_ref, dp_ref, dst_sc):
        @pl.when(pl.program_id(1) == 0)
        def _():
            dst_sc[...] = jnp.zeros_like(dst_sc)

        mh, mv, bd = _ret_masks()
        qw, kw, dec = qw_ref[...], kw_ref[...], ds_ref[...]
        cs_ = range(G)
        rows = [slice(c * L, (c + 1) * L) for c in cs_]
        ps = [p_ref[0, rows[c], :].astype(F32) for c in cs_]
        tabs = [(c_ref[0, rows[c], :], s_ref[0, rows[c], :]) for c in cs_]
        qk = [_ret_rope(ps[c], *tabs[c]) for c in cs_]
        vs = [ps[c][:, 512:768] for c in cs_]
        qs = [_stack_heads(qk[c][0], mh) for c in cs_]
        a_s = [_mm_nt(qs[c], qk[c][1]) for c in cs_]
        dr, dz = [], []
        for c in cs_:
            r = raw_ref[0, rows[c], :]
            z = ps[c][:, 768:1024]
            rstd = lax.rsqrt(_head_mean(r * r, mv, 64.0) + EPS)
            rn = r * rstd
            dm = dm_ref[0, rows[c], :]
            d_rn = dm * _silu(z)
            dz.append(dm * rn * _dsilu(z))
            dr.append(rstd * (d_rn - rn * _head_mean(d_rn * rn, mv, 64.0)))
        do_s = [_stack_heads(dr[c], mv) for c in cs_]
        da_s = [_mm_nt(do_s[c], vs[c]) for c in cs_]
        sts = [st_ref[0, c] for c in cs_]
        dq_st = [_mm(dr[c], sts[c]) for c in cs_]
        dst_in = [_mm_tn(dr[c], qk[c][0] * qw) for c in cs_]
        dv = [_mm_tn(a_s[c] * dec, do_s[c]) for c in cs_]
        dqr, dkr = [], []
        for c in cs_:
            da = da_s[c] * dec
            dqr.append(_fold_heads(_mm(da, qk[c][1]), mh, L) + dq_st[c] * qw)
            dkr.append(_mm_tn(da, qs[c]))
        dst_next = dst_sc[...]
        for c in reversed(cs_):
            g = jnp.where(bd, dst_next, 0.0)
            dv[c] = dv[c] + _mm_nt(qk[c][1] * kw, g)
            dkr[c] = dkr[c] + _mm(vs[c], g) * kw
            dst_next = dst_next * a_ref[...] + jnp.where(bd, dst_in[c], 0.0)
        dst_sc[...] = dst_next
        for c in cs_:
            cs, sn = tabs[c]
            dk = dkr[c] * RET_KSCALE
            dq1, dq2 = dqr[c][:, 0:128], dqr[c][:, 128:256]
            dk1, dk2 = dk[:, 0:128], dk[:, 128:256]
            dp_ref[0, rows[c], :] = jnp.concatenate(
                [dq1 * cs + dq2 * sn, dq2 * cs - dq1 * sn, dk1 * cs + dk2 * sn, dk2 * cs - dk1 * sn, dv[c], dz[c]],
                axis=1).astype(_MXU)

    tok = lambda w: pl.BlockSpec((1, G * L, w), lambda b, i: (b, NG - 1 - i, 0))
    return pl.pallas_call(
        body, name="ret_bwd", grid=(B, NG),
        in_specs=[tok(RET_W), tok(128), tok(128), tok(256),
                  pl.BlockSpec((1, G, 256, 256), lambda b, i: (b, NG - 1 - i, 0, 0)), tok(256),
                  _full((4 * L, L)), _full((L, 256)), _full((L, 256)), _full((1, 256))],
        out_specs=tok(RET_W), out_shape=jax.ShapeDtypeStruct((B, S, RET_W), _MXU),
        scratch_shapes=[pltpu.VMEM((256, 256), F32)],
        compiler_params=_cp(("parallel", "arbitrary")),
    )(ret_p, cos, sin, raw, states, d_mix, *consts)


def _gla_masks():
    C = CHUNK
    lk = lax.broadcasted_iota(jnp.int32, (1, 128), 1)
    lv = lax.broadcasted_iota(jnp.int32, (1, 256), 1)
    mk = [(lk // 32) == h for h in range(4)]
    mv = [(lv // 64) == h for h in range(4)]
    vi = lax.broadcasted_iota(jnp.int32, (256, 128), 0)
    ki = lax.broadcasted_iota(jnp.int32, (256, 128), 1)
    bd = (vi // 64) == (ki // 32)
    ri = lax.broadcasted_iota(jnp.int32, (4 * C, C), 0) % C
    cj = lax.broadcasted_iota(jnp.int32, (4 * C, C), 1)
    lower = ri >= cj
    ti = lax.broadcasted_iota(jnp.int32, (C, C), 0)
    tj = lax.broadcasted_iota(jnp.int32, (C, C), 1)
    ltri = jnp.where(ti >= tj, 1.0, 0.0).astype(F32)
    utri = jnp.where(ti <= tj, 1.0, 0.0).astype(F32)
    return mk, mv, bd, lower, ltri, utri


def _log_sigmoid(x):
    return jnp.minimum(x, 0.0) - jnp.log(1.0 + jnp.exp(-jnp.abs(x)))


GLA_G = 8


def _gla_fwd(gla_p, w_g2p, b_g2, gnw):
    B, S, _ = gla_p.shape
    C = CHUNK
    NC = S // C
    G = min(GLA_G, NC)
    NG = NC // G

    def body(p_ref, w_ref, b_ref, gn_ref, out_ref, raw_ref, st_ref, st_sc):
        @pl.when(pl.program_id(1) == 0)
        def _():
            st_sc[...] = jnp.zeros_like(st_sc)

        mk, mv, bd, lower, ltri, _ = _gla_masks()
        cs = range(G)
        rows = [slice(c * C, (c + 1) * C) for c in cs]
        ps = [p_ref[0, rows[c], :].astype(F32) for c in cs]
        pre = [_mm(ps[c][:, 512:640], w_ref[...]) + b_ref[...] for c in cs]
        cum = [_mm_f32(ltri, _log_sigmoid(pre[c]) * (1.0 / GLA_TAU)) for c in cs]
        past, fut, upd, q_pos, a_row = [], [], [], [], []
        for c in cs:
            q = ps[c][:, 0:128]
            k = ps[c][:, 128:256] * GLA_KSCALE
            last = cum[c][C - 1:C, :]
            e_pos = jnp.exp(cum[c])
            e_neg = jnp.exp(-cum[c])
            q_pos.append(q * e_pos)
            a_row.append(jnp.exp(last))
            past.append(_mm_nt(_stack_heads(q_pos[c], mk), k * e_neg))
            fut.append(_mm_nt(_stack_heads(q * e_neg, mk), k * e_pos))
            upd.append(_mm_tn(ps[c][:, 256:512], k * jnp.exp(last - cum[c])))
        o_s = [_mm(jnp.where(lower, past[c], fut[c]), ps[c][:, 256:512]) for c in cs]
        st = st_sc[...]
        inter = []
        for c in cs:
            st_ref[0, c] = st
            inter.append(_mm_nt(q_pos[c], st))
            st = st * a_row[c] + jnp.where(bd, upd[c], 0.0)
        st_sc[...] = st
        for c in cs:
            g = _fold_heads(o_s[c], mv, C) + inter[c]
            raw_ref[0, rows[c], :] = g
            rstd = lax.rsqrt(_head_mean(g * g, mv, 64.0) + EPS)
            out_ref[0, rows[c], :] = (g * rstd * gn_ref[...] * _silu(ps[c][:, 640:896])).astype(_MXU)

    tok = lambda w: pl.BlockSpec((1, G * C, w), lambda b, n: (b, n, 0))
    return pl.pallas_call(
        body, name="gla_fwd", grid=(B, NG),
        in_specs=[tok(GLA_W), _full((128, 128)), _full((1, 128)), _full((1, 256))],
        out_specs=[tok(256), tok(256), pl.BlockSpec((1, G, 256, 128), lambda b, n: (b, n, 0, 0))],
        out_shape=[jax.ShapeDtypeStruct((B, S, 256), _MXU), jax.ShapeDtypeStruct((B, S, 256), F32),
                   jax.ShapeDtypeStruct((B, NC, 256, 128), F32)],
        scratch_shapes=[pltpu.VMEM((256, 128), F32)],
        compiler_params=_cp(("parallel", "arbitrary")),
    )(gla_p, w_g2p, b_g2, gnw)


def _gla_bwd(gla_p, w_g2p, b_g2, gnw, raw, states, d_mix):
    B, S, _ = gla_p.shape
    C = CHUNK
    NC = S // C
    G = min(GLA_G, NC)
    NG = NC // G

    def body(p_ref, w_ref, b_ref, gn_ref, raw_ref, st_ref, dm_ref, dp_ref, dw_ref, db_ref, dgn_ref, dst_sc):
        first = (pl.program_id(0) == 0) & (pl.program_id(1) == 0)

        @pl.when(first)
        def _():
            dw_ref[...] = jnp.zeros_like(dw_ref)
            db_ref[...] = jnp.zeros_like(db_ref)
            dgn_ref[...] = jnp.zeros_like(dgn_ref)

        @pl.when(pl.program_id(1) == 0)
        def _():
            dst_sc[...] = jnp.zeros_like(dst_sc)

        mk, mv, bd, lower, ltri, utri = _gla_masks()
        gn = gn_ref[...]
        cs = range(G)
        rows = [slice(c * C, (c + 1) * C) for c in cs]
        ps = [p_ref[0, rows[c], :].astype(F32) for c in cs]
        vs = [ps[c][:, 256:512] for c in cs]
        pre = [_mm(ps[c][:, 512:640], w_ref[...]) + b_ref[...] for c in cs]
        cum = [_mm_f32(ltri, _log_sigmoid(pre[c]) * (1.0 / GLA_TAU)) for c in cs]
        dg, dz, dgn_acc = [], [], jnp.zeros((1, 256), F32)
        for c in cs:
            g = raw_ref[0, rows[c], :]
            z = ps[c][:, 640:896]
            rstd = lax.rsqrt(_head_mean(g * g, mv, 64.0) + EPS)
            gh = g * rstd
            dm = dm_ref[0, rows[c], :]
            d_gn = dm * _silu(z)
            dz.append(dm * gh * gn * _dsilu(z))
            dgn_acc = dgn_acc + jnp.sum(d_gn * gh, axis=0, keepdims=True)
            d_gh = d_gn * gn
            dg.append(rstd * (d_gh - gh * _head_mean(d_gh * gh, mv, 64.0)))
        do_s = [_stack_heads(dg[c], mv) for c in cs]
        dattn = [_mm_nt(do_s[c], vs[c]) for c in cs]
        ks, e_pos, e_neg, q_pos, q_neg, k_pos, k_neg, qp_s, qn_s, past, fut, a_row, w_dec, kd = ([] for _ in range(14))
        for c in cs:
            q = ps[c][:, 0:128]
            k = ps[c][:, 128:256] * GLA_KSCALE
            last = cum[c][C - 1:C, :]
            ep, en = jnp.exp(cum[c]), jnp.exp(-cum[c])
            ks.append(k), e_pos.append(ep), e_neg.append(en)
            q_pos.append(q * ep), q_neg.append(q * en), k_pos.append(k * ep), k_neg.append(k * en)
            qp_s.append(_stack_heads(q_pos[c], mk)), qn_s.append(_stack_heads(q_neg[c], mk))
            past.append(_mm_nt(qp_s[c], k_neg[c]))
            fut.append(_mm_nt(qn_s[c], k_pos[c]))
            a_row.append(jnp.exp(last))
            w_dec.append(jnp.exp(last - cum[c]))
            kd.append(k * w_dec[c])
        sts = [st_ref[0, c] for c in cs]
        dq_st = [_mm(dg[c], sts[c]) for c in cs]
        dst_in = [_mm_tn(dg[c], q_pos[c]) for c in cs]
        dv, dq_pos, dk_neg, dq_neg, dk_pos = [], [], [], [], []
        for c in cs:
            attn = jnp.where(lower, past[c], fut[c])
            dpast = jnp.where(lower, dattn[c], 0.0)
            dfut = jnp.where(lower, 0.0, dattn[c])
            dv.append(_mm_tn(attn, do_s[c]))
            dq_pos.append(_fold_heads(_mm(dpast, k_neg[c]), mk, C) + dq_st[c])
            dk_neg.append(_mm_tn(dpast, qp_s[c]))
            dq_neg.append(_fold_heads(_mm(dfut, k_pos[c]), mk, C))
            dk_pos.append(_mm_tn(dfut, qn_s[c]))
        dst_next = dst_sc[...]
        d_a, d_kd = [None] * G, [None] * G
        for c in reversed(cs):
            d_a[c] = jnp.sum(dst_next * sts[c], axis=0, keepdims=True)
            gmat = jnp.where(bd, dst_next, 0.0)
            d_kd[c] = _mm(vs[c], gmat)
            dv[c] = dv[c] + _mm_nt(kd[c], gmat)
            dst_next = dst_next * a_row[c] + jnp.where(bd, dst_in[c], 0.0)
        dst_sc[...] = dst_next
        row = lax.broadcasted_iota(jnp.int32, (C, 128), 0)
        d_la, dk, dq = [], [], []
        for c in cs:
            t = d_kd[c] * kd[c]
            dk.append(d_kd[c] * w_dec[c] + dk_neg[c] * e_neg[c] + dk_pos[c] * e_pos[c])
            dq.append(dq_pos[c] * e_pos[c] + dq_neg[c] * e_neg[c])
            d_last = jnp.sum(t, axis=0, keepdims=True) + d_a[c] * a_row[c]
            d_cum = (dq_pos[c] * q_pos[c] - dk_neg[c] * k_neg[c] - dq_neg[c] * q_neg[c] + dk_pos[c] * k_pos[c] - t)
            d_la.append(_mm_f32(utri, d_cum + jnp.where(row == C - 1, d_last, 0.0)))
        d_pre = [d_la[c] * _sig(-pre[c]) * (1.0 / GLA_TAU) for c in cs]
        d_gg = [_mm_nt(d_pre[c], w_ref[...]) for c in cs]
        dw_acc = _mm_tn(ps[0][:, 512:640], d_pre[0])
        db_acc = jnp.sum(d_pre[0], axis=0, keepdims=True)
        for c in cs[1:]:
            dw_acc = dw_acc + _mm_tn(ps[c][:, 512:640], d_pre[c])
            db_acc = db_acc + jnp.sum(d_pre[c], axis=0, keepdims=True)
        for c in cs:
            dp_ref[0, rows[c], :] = jnp.concatenate([dq[c], dk[c] * GLA_KSCALE, dv[c], d_gg[c], dz[c]],
                                                    axis=1).astype(_MXU)
        dw_ref[...] += dw_acc
        db_ref[...] += db_acc
        dgn_ref[...] += dgn_acc

        @pl.when((pl.program_id(0) == B - 1) & (pl.program_id(1) == NG - 1))
        def _():
            s1 = dgn_ref[...]
            s1 = s1 + pltpu.roll(s1, 128, 1)
            dgn_ref[...] = s1 + pltpu.roll(s1, 64, 1)

    tok = lambda w: pl.BlockSpec((1, G * C, w), lambda b, i: (b, NG - 1 - i, 0))
    return pl.pallas_call(
        body, name="gla_bwd", grid=(B, NG),
        in_specs=[tok(GLA_W), _full((128, 128)), _full((1, 128)), _full((1, 256)), tok(256),
                  pl.BlockSpec((1, G, 256, 128), lambda b, i: (b, NG - 1 - i, 0, 0)), tok(256)],
        out_specs=[tok(GLA_W), _full((128, 128)), _full((1, 128)), _full((1, 256))],
        out_shape=[jax.ShapeDtypeStruct((B, S, GLA_W), _MXU), jax.ShapeDtypeStruct((128, 128), F32),
                   jax.ShapeDtypeStruct((1, 128), F32), jax.ShapeDtypeStruct((1, 256), F32)],
        scratch_shapes=[pltpu.VMEM((256, 128), F32)],
        compiler_params=_cp(("arbitrary", "arbitrary")),
    )(gla_p, w_g2p, b_g2, gnw, raw, states, d_mix)


def _rms(x, w):
    rstd = lax.rsqrt(jnp.mean(x * x, axis=-1, keepdims=True) + EPS)
    xh = x * rstd
    return xh, rstd, xh * w


def _rms_bwd(dy, xh, rstd, w):
    dxh = dy * w
    return rstd * (dxh - xh * jnp.mean(dxh * xh, axis=-1, keepdims=True))


MLA_T = 256


def _mla_prep_fwd(mla_p, cos, sin, qnw, kvnw, w_uq, w_ukv, w_ukv_t):
    B, S, _ = mla_p.shape
    tm = min(S, 512)

    t = min(MLA_T, S)
    nt = tm // t

    def body(p_ref, c_ref, s_ref, qn_ref, kn_ref, wq_ref, wkv_ref, wkvt_ref, q_ref, k_ref, v_ref, kt_ref, vt_ref):
        p = p_ref[0].astype(F32)
        cs, sn = c_ref[0], s_ref[0]
        _, _, qn = _rms(p[:, 0:256], qn_ref[...])
        qpre = _mm(qn, wq_ref[...])
        _, _, kvn = _rms(p[:, 256:384], kn_ref[...])
        kv = _mm(kvn, wkv_ref[...])
        kvt = _mm_nt(wkvt_ref[...], kvn)
        kpe = _rope128(p[:, 384:512], cs, sn)
        kpet = kpe.T
        for h in range(8):
            sl = slice(128 * h, 128 * h + 128)
            q_ref[0, :, sl] = _rope128(qpre[:, sl], cs, sn).astype(_MXU)
            k_ref[0, :, sl] = (kv[:, sl] + kpe).astype(_MXU)
            kht = kvt[sl, :] + kpet
            for n in range(nt):
                kt_ref[0, n, sl, :] = kht[:, n * t:(n + 1) * t].astype(_MXU)
        v_ref[0] = kv[:, 1024:1536].astype(_MXU)
        for n in range(nt):
            vt_ref[0, n] = kvt[1024:1536, n * t:(n + 1) * t].astype(_MXU)

    tok = lambda w: pl.BlockSpec((1, tm, w), lambda b, i: (b, i, 0))
    tr = lambda w: pl.BlockSpec((1, nt, w, t), lambda b, i: (b, i, 0, 0))
    return pl.pallas_call(
        body, name="mla_prep_fwd", grid=(B, S // tm),
        in_specs=[tok(512), tok(128), tok(128), _full((1, 256)), _full((1, 128)), _full((256, 1024)),
                  _full((128, 1536)), _full((1536, 128))],
        out_specs=[tok(1024), tok(1024), tok(512), tr(1024), tr(512)],
        out_shape=[jax.ShapeDtypeStruct((B, S, 1024), _MXU), jax.ShapeDtypeStruct((B, S, 1024), _MXU),
                   jax.ShapeDtypeStruct((B, S, 512), _MXU), jax.ShapeDtypeStruct((B, S // t, 1024, t), _MXU),
                   jax.ShapeDtypeStruct((B, S // t, 512, t), _MXU)],
        compiler_params=_cp(("parallel", "parallel")),
    )(mla_p, cos, sin, qnw, kvnw, w_uq, w_ukv, w_ukv_t)


def _chunk_mask_t(t):
    kj = lax.broadcasted_iota(jnp.int32, (t, t), 0) // CHUNK
    qi = lax.broadcasted_iota(jnp.int32, (t, t), 1) // CHUNK
    return kj <= qi


MLA_HG = 8
MLA_HG_FWD = 8
LOG2E = 1.4426950408889634
MLA_C2 = MLA_SCALE * LOG2E


def _mla_attn_fwd(q, k, vt):
    B, S, _ = q.shape
    t = min(MLA_T, S)
    nq = S // t
    HG = MLA_HG_FWD
    NP = HG // 2

    def body(q_ref, k_ref, vt_ref, o_ref, lse_ref, sa, sb, m_sc, l_sc, acc_sc):
        i = pl.program_id(2)
        row = lax.broadcasted_iota(jnp.int32, (128, 1), 0)
        low = row < 64
        mask = _chunk_mask_t(t)
        m_sc[...] = jnp.full(m_sc.shape, -jnp.inf, F32)
        l_sc[...] = jnp.zeros_like(l_sc)
        acc_sc[...] = jnp.zeros_like(acc_sc)

        ones = jnp.ones((8, t), _MXU)

        def scores(j, buf):
            kb = k_ref[0, pl.ds(pl.multiple_of(j * t, t), t), :]
            for h in range(HG):
                cols = slice(128 * h, 128 * h + 128)
                buf[h] = (_mm_nt(kb[:, cols], q_ref[0, :, cols]) * MLA_C2).astype(_MXU)

        def absorb(j, buf, masked):
            vtb = vt_ref[0, j]
            for pr in range(NP):
                alphas, pvs = [], []
                for hh in range(2):
                    h = 2 * pr + hh
                    s = buf[h]
                    if masked:
                        s = jnp.where(mask, s, jnp.full_like(s, -jnp.inf))
                    m_old = m_sc[h]
                    m_new = jnp.maximum(m_old, jnp.max(s, axis=0, keepdims=True).astype(F32))
                    alpha = jnp.exp2(m_old - m_new)
                    p = jnp.exp2(s - m_new.astype(_MXU))
                    l_sc[h] = alpha * l_sc[h] + _mm(ones, p)[0:1, :]
                    m_sc[h] = m_new
                    vth = vtb[128 * pr:128 * pr + 128, :]
                    vth = jnp.where(low if hh == 0 else ~low, vth, jnp.zeros_like(vth))
                    pvs.append(_mm(vth, p))
                    alphas.append(alpha)
                acc_sc[pr] = acc_sc[pr] * jnp.where(low, alphas[0], alphas[1]) + pvs[0] + pvs[1]

        scores(0, sb)

        def pair(jj, carry):
            j0 = 2 * jj
            scores(j0 + 1, sa)
            absorb(j0, sb, False)
            scores(j0 + 2, sb)
            absorb(j0 + 1, sa, False)
            return carry

        lax.fori_loop(0, i // 2, pair, 0)

        @pl.when(i % 2 == 1)
        def _():
            scores(i, sa)
            absorb(i - 1, sb, False)
            absorb(i, sa, True)

        @pl.when(i % 2 == 0)
        def _():
            absorb(i, sb, True)

        for pr in range(NP):
            l_e, l_o = l_sc[2 * pr], l_sc[2 * pr + 1]
            o_ref[0, :, 128 * pr:128 * pr + 128] = (acc_sc[pr] / jnp.where(low, l_e, l_o)).T
            lse_ref[0, pr, 0, 0:1, :] = m_sc[2 * pr] + jnp.log(l_e) * LOG2E
            lse_ref[0, pr, 0, 1:2, :] = m_sc[2 * pr + 1] + jnp.log(l_o) * LOG2E

    return pl.pallas_call(
        body, name="mla_attn_fwd", grid=(B, 8 // HG, nq),
        in_specs=[pl.BlockSpec((1, t, 128 * HG), lambda b, g, i: (b, i, g)),
                  pl.BlockSpec((1, S, 128 * HG), lambda b, g, i: (b, 0, g)),
                  pl.BlockSpec((1, nq, 64 * HG, t), lambda b, g, i: (b, 0, g, 0))],
        out_specs=[pl.BlockSpec((1, t, 64 * HG), lambda b, g, i: (b, i, g)),
                   pl.BlockSpec((1, NP, 1, 2, t), lambda b, g, i: (b, g, i, 0, 0))],
        out_shape=[jax.ShapeDtypeStruct((B, S, 512), F32), jax.ShapeDtypeStruct((B, 4, nq, 2, t), F32)],
        scratch_shapes=[pltpu.VMEM((HG, t, t), _MXU), pltpu.VMEM((HG, t, t), _MXU), pltpu.VMEM((HG, 1, t), F32),
                        pltpu.VMEM((HG, 1, t), F32), pltpu.VMEM((NP, 128, t), F32)],
        compiler_params=_cp(("parallel", "parallel", "arbitrary")),
    )(q, k, vt)


def _mla_attn_bwd(q, k, v, kt, do, lse, dl):
    B, S, _ = q.shape
    t = min(MLA_T, S)
    nk = S // t

    HG = MLA_HG
    NP = HG // 2

    def body(q_ref, k_ref, v_ref, kt_ref, do_ref, lse_ref, dl_ref, dq_ref, dk_ref, dv_ref,
             sa, da, sb, db, dqt_sc, dk_sc, dv_sc):
        j = pl.program_id(2)

        @pl.when(j == 0)
        def _():
            dqt_sc[...] = jnp.zeros_like(dqt_sc)

        dk_sc[...] = jnp.zeros_like(dk_sc)
        dv_sc[...] = jnp.zeros_like(dv_sc)
        lane = lax.broadcasted_iota(jnp.int32, (1, 128), 1)
        low = lane < 64
        mask = _chunk_mask_t(t)

        def half(x, hh):
            return jnp.where(low if hh == 0 else ~low, x, jnp.zeros_like(x))

        def prepare(i, sbuf, dbuf):
            rows = pl.ds(pl.multiple_of(i * t, t), t)
            for h in range(HG):
                cols = slice(128 * h, 128 * h + 128)
                pc = slice(128 * (h // 2), 128 * (h // 2) + 128)
                sbuf[h] = _mm_nt(k_ref[0, :, cols], q_ref[0, rows, cols]) * MLA_C2
                dbuf[h] = _mm_nt(half(v_ref[0, :, pc], h % 2), do_ref[0, rows, pc])

        def absorb(i, sbuf, dbuf, masked):
            rows = pl.ds(pl.multiple_of(i * t, t), t)
            for h in range(HG):
                pr, hh = h // 2, h % 2
                cols = slice(128 * h, 128 * h + 128)
                pc = slice(128 * pr, 128 * pr + 128)
                p = jnp.exp2(sbuf[h] - lse_ref[0, pr, i][hh:hh + 1, :])
                if masked:
                    p = jnp.where(mask, p, 0.0)
                dv_sc[pr] += _mm(p, half(do_ref[0, rows, pc], hh))
                ds = p * (dbuf[h] - dl_ref[0, pr, i][hh:hh + 1, :])
                dqt_sc[i, cols, :] += _mm(kt_ref[0, 0, cols, :], ds)
                dk_sc[h] += _mm(ds, q_ref[0, rows, cols])

        n = nk - 1 - j
        prepare(jnp.minimum(j + 1, nk - 1), sb, db)

        def pair(jj, carry):
            i0 = j + 1 + 2 * jj
            prepare(i0 + 1, sa, da)
            absorb(i0, sb, db, False)
            prepare(jnp.where(i0 + 2 <= nk - 1, i0 + 2, j), sb, db)
            absorb(i0 + 1, sa, da, False)
            return carry

        lax.fori_loop(0, n // 2, pair, 0)

        @pl.when(n % 2 == 1)
        def _():
            prepare(j, sa, da)
            absorb(nk - 1, sb, db, False)
            absorb(j, sa, da, True)

        @pl.when(n % 2 == 0)
        def _():
            absorb(j, sb, db, True)

        for h in range(HG):
            dk_ref[0, :, 128 * h:128 * h + 128] = (dk_sc[h] * MLA_SCALE).astype(_MXU)
        for pr in range(NP):
            dv_ref[0, :, 128 * pr:128 * pr + 128] = dv_sc[pr].astype(_MXU)

        @pl.when(j == nk - 1)
        def _():
            for i in range(nk):
                dq_ref[0, i * t:(i + 1) * t, :] = (dqt_sc[i].T * MLA_SCALE).astype(_MXU)

    seq = lambda w: pl.BlockSpec((1, S, w), lambda b, g, j: (b, 0, g))
    blk = lambda w: pl.BlockSpec((1, t, w), lambda b, g, j: (b, j, g))
    stat = pl.BlockSpec((1, NP, nk, 2, t), lambda b, g, j: (b, g, 0, 0, 0))
    return pl.pallas_call(
        body, name="mla_attn_bwd", grid=(B, 8 // HG, nk),
        in_specs=[seq(128 * HG), blk(128 * HG), blk(64 * HG),
                  pl.BlockSpec((1, 1, 128 * HG, t), lambda b, g, j: (b, j, g, 0)), seq(64 * HG), stat, stat],
        out_specs=[seq(128 * HG), blk(128 * HG), blk(64 * HG)],
        out_shape=[jax.ShapeDtypeStruct((B, S, 1024), _MXU), jax.ShapeDtypeStruct((B, S, 1024), _MXU),
                   jax.ShapeDtypeStruct((B, S, 512), _MXU)],
        scratch_shapes=[pltpu.VMEM((HG, t, t), F32), pltpu.VMEM((HG, t, t), F32), pltpu.VMEM((HG, t, t), F32),
                        pltpu.VMEM((HG, t, t), F32), pltpu.VMEM((nk, 128 * HG, t), F32),
                        pltpu.VMEM((HG, t, 128), F32), pltpu.VMEM((NP, t, 128), F32)],
        compiler_params=_cp(("parallel", "parallel", "arbitrary"), 56),
    )(q, k, v, kt, do, lse, dl)


def _mla_prep_bwd(mla_p, cos, sin, qnw, kvnw, w_uq, w_ukv, dq, dk, dv):
    B, S, _ = mla_p.shape
    tm = min(S, 512)

    def body(p_ref, c_ref, s_ref, qn_ref, kn_ref, wq_ref, wkv_ref, dq_ref, dk_ref, dv_ref,
             dp_ref, dwq_ref, dwkv_ref, dqn_ref, dkn_ref):
        first = (pl.program_id(0) == 0) & (pl.program_id(1) == 0)

        @pl.when(first)
        def _():
            dwq_ref[...] = jnp.zeros_like(dwq_ref)
            dwkv_ref[...] = jnp.zeros_like(dwkv_ref)
            dqn_ref[...] = jnp.zeros_like(dqn_ref)
            dkn_ref[...] = jnp.zeros_like(dkn_ref)

        p = p_ref[0].astype(F32)
        cs, sn = c_ref[0], s_ref[0]
        lane = lax.broadcasted_iota(jnp.int32, (1, 128), 1)
        pe = (lane >= 64) & (lane < 96)
        qh, q_rstd, qn = _rms(p[:, 0:256], qn_ref[...])
        kvh, kv_rstd, kvn = _rms(p[:, 256:384], kn_ref[...])
        dqv = dq_ref[0].astype(F32)
        dkv = dk_ref[0].astype(F32)
        dqpre = jnp.concatenate(
            [_rope128_t(dqv[:, 128 * h:128 * h + 128], cs, sn) for h in range(8)], axis=1)
        dkpe = jnp.zeros((tm, 128), F32)
        for h in range(8):
            dkpe = dkpe + jnp.where(pe, dkv[:, 128 * h:128 * h + 128], 0.0)
        dkr = _rope128_t(dkpe, cs, sn)
        dkv_all = jnp.concatenate([dkv, dv_ref[0].astype(F32)], axis=1)
        d_qn = _mm_nt(dqpre, wq_ref[...])
        d_kvn = _mm_nt(dkv_all, wkv_ref[...])
        dwq_ref[...] += _mm_tn(qn, dqpre)
        dwkv_ref[...] += _mm_tn(kvn, dkv_all)
        dqn_ref[...] += jnp.sum(d_qn * qh, axis=0, keepdims=True)
        dkn_ref[...] += jnp.sum(d_kvn * kvh, axis=0, keepdims=True)
        dp_ref[0] = jnp.concatenate([_rms_bwd(d_qn, qh, q_rstd, qn_ref[...]),
                                     _rms_bwd(d_kvn, kvh, kv_rstd, kn_ref[...]), dkr], axis=1).astype(_MXU)

    tok = lambda w: pl.BlockSpec((1, tm, w), lambda b, i: (b, i, 0))
    return pl.pallas_call(
        body, name="mla_prep_bwd", grid=(B, S // tm),
        in_specs=[tok(512), tok(128), tok(128), _full((1, 256)), _full((1, 128)), _full((256, 1024)),
                  _full((128, 1536)), tok(1024), tok(1024), tok(512)],
        out_specs=[tok(512), _full((256, 1024)), _full((128, 1536)), _full((1, 256)), _full((1, 128))],
        out_shape=[jax.ShapeDtypeStruct((B, S, 512), _MXU), jax.ShapeDtypeStruct((256, 1024), F32),
                   jax.ShapeDtypeStruct((128, 1536), F32), jax.ShapeDtypeStruct((1, 256), F32),
                   jax.ShapeDtypeStruct((1, 128), F32)],
        compiler_params=_cp(("arbitrary", "arbitrary")),
    )(mla_p, cos, sin, qnw, kvnw, w_uq, w_ukv, dq, dk, dv)


def _out_fwd(x, gate, r_g, o_mla, mla_p, g_g, w_out):
    B, S, D = x.shape
    tm = min(S, 512)

    def body(x_ref, g_ref, r_ref, o_ref, z_ref, gg_ref, w_ref, xn_ref, y_ref):
        mm = (o_ref[0] * _silu(z_ref[0].astype(F32))).astype(_MXU)
        y = (jnp.dot(r_ref[0], w_ref[0:256, :], preferred_element_type=F32)
             + jnp.dot(mm, w_ref[256:768, :], preferred_element_type=F32)
             + jnp.dot(gg_ref[0], w_ref[768:1024, :], preferred_element_type=F32))
        y_ref[0] = y.astype(_MXU)
        xn_ref[0] = x_ref[0] + g_ref[0] * y

    tok = lambda w, c=0: pl.BlockSpec((1, tm, w), lambda b, i: (b, i, c))
    return pl.pallas_call(
        body, name="out_fwd", grid=(B, S // tm),
        in_specs=[tok(D), pl.BlockSpec((1, 1, D), lambda b, i: (b, 0, 0)), tok(256), tok(512), tok(512, 1),
                  tok(256), _full((D, D))],
        out_specs=[tok(D), tok(D)],
        out_shape=[jax.ShapeDtypeStruct((B, S, D), F32), jax.ShapeDtypeStruct((B, S, D), _MXU)],
        compiler_params=_cp(("parallel", "parallel")),
    )(x, gate, r_g, o_mla, mla_p, g_g, w_out)


def _out_bwd(dx, y, gate, r_g, g_g, w_out, o_mla, mla_p):
    B, S, D = dx.shape
    tm = min(S, 512)
    t = min(MLA_T, S)
    nt = tm // t

    def body(dx_ref, y_ref, g_ref, r_ref, gg_ref, w_ref, o_ref, z_ref,
             dr_ref, do_ref, dz_ref, dl_ref, dg_ref, dw_ref, dgate_ref, acc):
        first = (pl.program_id(0) == 0) & (pl.program_id(1) == 0)

        @pl.when(first)
        def _():
            acc[...] = jnp.zeros_like(acc)

        @pl.when(pl.program_id(1) == 0)
        def _():
            dgate_ref[...] = jnp.zeros_like(dgate_ref)

        dxv = dx_ref[0]
        dgate_ref[0] += jnp.sum(dxv * y_ref[0].astype(F32), axis=0, keepdims=True)
        dy = (dxv * g_ref[0]).astype(_MXU)
        dr_ref[0] = _mm_nt(dy, w_ref[0:256, :])
        dg_ref[0] = _mm_nt(dy, w_ref[768:1024, :])
        ov, z = o_ref[0], z_ref[0].astype(F32)
        acc[0:256, :] += _mm_tn(r_ref[0], dy)
        acc[256:768, :] += _mm_tn((ov * _silu(z)).astype(_MXU), dy)
        acc[768:1024, :] += _mm_tn(gg_ref[0], dy)

        @pl.when((pl.program_id(0) == B - 1) & (pl.program_id(1) == S // tm - 1))
        def _():
            dw_ref[...] = acc[...].astype(_MXU)

        dm = _mm_nt(dy, w_ref[256:768, :])
        do = dm * _silu(z)
        dz_ref[0] = (dm * ov * _dsilu(z)).astype(_MXU)
        do_ref[0] = do.astype(_MXU)
        prod = do * ov
        for pr in range(4):
            pt = prod[:, 128 * pr:128 * pr + 128].T
            se = jnp.sum(pt[0:64], axis=0, keepdims=True)
            so = jnp.sum(pt[64:128], axis=0, keepdims=True)
            for n in range(nt):
                dl_ref[0, pr, n, 0:1, :] = se[:, n * t:(n + 1) * t]
                dl_ref[0, pr, n, 1:2, :] = so[:, n * t:(n + 1) * t]

    tok = lambda w, c=0: pl.BlockSpec((1, tm, w), lambda b, i: (b, i, c))
    per_seq = pl.BlockSpec((1, 1, D), lambda b, i: (b, 0, 0))
    return pl.pallas_call(
        body, name="out_bwd", grid=(B, S // tm),
        in_specs=[tok(D), tok(D), per_seq, tok(256), tok(256), _full((D, D)), tok(512), tok(512, 1)],
        out_specs=[tok(256), tok(512), tok(512), pl.BlockSpec((1, 4, nt, 2, t), lambda b, i: (b, 0, i, 0, 0)),
                   tok(256), _full((D, D)), per_seq],
        out_shape=[jax.ShapeDtypeStruct((B, S, 256), F32), jax.ShapeDtypeStruct((B, S, 512), _MXU),
                   jax.ShapeDtypeStruct((B, S, 512), _MXU), jax.ShapeDtypeStruct((B, 4, S // t, 2, t), F32),
                   jax.ShapeDtypeStruct((B, S, 256), F32), jax.ShapeDtypeStruct((D, D), _MXU),
                   jax.ShapeDtypeStruct((B, 1, D), F32)],
        scratch_shapes=[pltpu.VMEM((D, D), F32)],
        compiler_params=_cp(("arbitrary", "arbitrary")),
    )(dx, y, gate, r_g, g_g, w_out, o_mla, mla_p)


def _proj_bwd_x(x, shift, scale, nw, w_arr, d_ret, d_mla, d_mz, d_gla, dx_out):
    B, S, D = x.shape
    tm = min(S, 512)

    def body(x_ref, sc_ref, nw_ref, w_ref, dr_ref, dm_ref, dz_ref, dg_ref, dxo_ref,
             dx_ref, dsh_ref, dsc_ref, dnw_ref):
        first = (pl.program_id(0) == 0) & (pl.program_id(1) == 0)

        @pl.when(first)
        def _():
            dnw_ref[...] = jnp.zeros_like(dnw_ref)

        @pl.when(pl.program_id(1) == 0)
        def _():
            dsh_ref[...] = jnp.zeros_like(dsh_ref)
            dsc_ref[...] = jnp.zeros_like(dsc_ref)

        dp = jnp.concatenate([dr_ref[0], dm_ref[0], dz_ref[0], dg_ref[0]], axis=1)
        dh = lax.dot_general(dp, w_ref[...], (((1,), (1,)), ((), ())), preferred_element_type=F32)
        xv = x_ref[0]
        rstd = lax.rsqrt(jnp.mean(xv * xv, axis=-1, keepdims=True) + EPS)
        xh = xv * rstd
        nwv = nw_ref[...]
        mod = 1.0 + sc_ref[0]
        dsh_ref[0] += jnp.sum(dh, axis=0, keepdims=True)
        dsc_ref[0] += jnp.sum(dh * xh * nwv, axis=0, keepdims=True)
        dnw_ref[...] += jnp.sum(dh * xh * mod, axis=0, keepdims=True)
        dxh = dh * nwv * mod
        dx_ref[0] = dxo_ref[0] + rstd * (dxh - xh * jnp.mean(dxh * xh, axis=-1, keepdims=True))

    tok = lambda w: pl.BlockSpec((1, tm, w), lambda b, i: (b, i, 0))
    per_seq = pl.BlockSpec((1, 1, D), lambda b, i: (b, 0, 0))
    return pl.pallas_call(
        body, name="proj_bwd_x", grid=(B, S // tm),
        in_specs=[tok(D), per_seq, _full((1, D)), _full((D, ARR_W)), tok(RET_W), tok(512), tok(512),
                  tok(GLA_W), tok(D)],
        out_specs=[tok(D), per_seq, per_seq, _full((1, D))],
        out_shape=[jax.ShapeDtypeStruct((B, S, D), F32), jax.ShapeDtypeStruct((B, 1, D), F32),
                   jax.ShapeDtypeStruct((B, 1, D), F32), jax.ShapeDtypeStruct((1, D), F32)],
        compiler_params=_cp(("arbitrary", "arbitrary")),
    )(x, scale, nw, w_arr, d_ret, d_mla, d_mz, d_gla, dx_out)


def _proj_bwd_w(h, d_ret, d_mla, d_mz, d_gla):
    B, S, D = h.shape
    tm = min(S, 512)

    def body(h_ref, dr_ref, dm_ref, dz_ref, dg_ref, dw_ref, acc):
        first = (pl.program_id(0) == 0) & (pl.program_id(1) == 0)

        @pl.when(first)
        def _():
            acc[...] = jnp.zeros_like(acc)

        hv = h_ref[0]
        tn = lambda d_ref: lax.dot_general(hv, d_ref[0], (((0,), (0,)), ((), ())), preferred_element_type=F32)
        acc[:, 0:RET_W] += tn(dr_ref)
        acc[:, RET_W:RET_W + 512] += tn(dm_ref)
        acc[:, RET_W + 512:RET_W + MLA_W] += tn(dz_ref)
        acc[:, RET_W + MLA_W:ARR_W] += tn(dg_ref)

        @pl.when((pl.program_id(0) == B - 1) & (pl.program_id(1) == S // tm - 1))
        def _():
            dw_ref[...] = acc[...].astype(_MXU)

    tok = lambda w: pl.BlockSpec((1, tm, w), lambda b, i: (b, i, 0))
    return pl.pallas_call(
        body, name="proj_bwd_w", grid=(B, S // tm),
        in_specs=[tok(D), tok(RET_W), tok(512), tok(512), tok(GLA_W)],
        out_specs=_full((D, ARR_W)), out_shape=jax.ShapeDtypeStruct((D, ARR_W), _MXU),
        scratch_shapes=[pltpu.VMEM((D, ARR_W), F32)],
        compiler_params=_cp(("arbitrary", "arbitrary"), 56),
    )(h, d_ret, d_mla, d_mz, d_gla)


def _out_fwd_loss(x, gate, r_g, o_mla, mla_p, g_g, w_out, fw, target):
    B, S, D = x.shape
    tm = min(S, 512)

    def body(x_ref, g_ref, r_ref, o_ref, z_ref, gg_ref, w_ref, fw_ref, t_ref, dx_ref, y_ref, loss_ref, dfw_ref):
        first = (pl.program_id(0) == 0) & (pl.program_id(1) == 0)

        @pl.when(first)
        def _():
            loss_ref[...] = jnp.zeros_like(loss_ref)
            dfw_ref[...] = jnp.zeros_like(dfw_ref)

        mm = (o_ref[0] * _silu(z_ref[0].astype(F32))).astype(_MXU)
        y = (jnp.dot(r_ref[0], w_ref[0:256, :], preferred_element_type=F32)
             + jnp.dot(mm, w_ref[256:768, :], preferred_element_type=F32)
             + jnp.dot(gg_ref[0], w_ref[768:1024, :], preferred_element_type=F32))
        y_ref[0] = y.astype(_MXU)
        xv = x_ref[0] + g_ref[0] * y
        fwv = fw_ref[...]
        rstd = lax.rsqrt(jnp.mean(xv * xv, axis=-1, keepdims=True) + EPS)
        xh = xv * rstd
        err = xh * fwv - t_ref[0]
        loss_ref[...] += 0.5 * jnp.sum(jnp.mean(err * err, axis=-1, keepdims=True), axis=0, keepdims=True)
        dy = err * (1.0 / D)
        dfw_ref[...] += jnp.sum(dy * xh, axis=0, keepdims=True)
        dxh = dy * fwv
        dx_ref[0] = rstd * (dxh - xh * jnp.mean(dxh * xh, axis=-1, keepdims=True))

    tok = lambda w, c=0: pl.BlockSpec((1, tm, w), lambda b, i: (b, i, c))
    return pl.pallas_call(
        body, name="out_fwd_loss", grid=(B, S // tm),
        in_specs=[tok(D), pl.BlockSpec((1, 1, D), lambda b, i: (b, 0, 0)), tok(256), tok(512), tok(512, 1),
                  tok(256), _full((D, D)), _full((1, D)), tok(D)],
        out_specs=[tok(D), tok(D), _full((1, 1)), _full((1, D))],
        out_shape=[jax.ShapeDtypeStruct((B, S, D), F32), jax.ShapeDtypeStruct((B, S, D), _MXU),
                   jax.ShapeDtypeStruct((1, 1), F32), jax.ShapeDtypeStruct((1, D), F32)],
        compiler_params=_cp(("arbitrary", "arbitrary")),
    )(x, gate, r_g, o_mla, mla_p, g_g, w_out, fw, target)


def _local_step(x, pos3, mod, loss_target, small, w_in_a, w_uq_a, w_ukv_a, w_out_b):
    B, S, D = x.shape
    tabs = _rope_tables(pos3)
    saved = []
    for l in range(DEPTH):
        last = (small["final_norm"].reshape(1, D), loss_target) if l == DEPTH - 1 else None
        x, s = _layer_fwd(x, tabs, mod[l], {n: a[l] for n, a in small.items() if n != "final_norm"},
                          w_in_a[l], w_uq_a[l], w_ukv_a[l], w_ukv_a[l].T, w_out_b[l], loss_head=last)
        saved.append(s)
    dx, loss, d_fw = x
    grads = dict(final_norm=d_fw.reshape(D))
    per_layer = [None] * DEPTH
    for l in reversed(range(DEPTH)):
        dx, per_layer[l] = _layer_bwd(dx, saved[l], tabs)
    for name in per_layer[0]:
        grads[name] = jnp.stack([per_layer[l][name] for l in range(DEPTH)])
    return loss, dx, grads


def _layer_fwd(x, tabs, mod_l, small_l, w_in_a, w_uq_a=None, w_ukv_a=None, w_ukv_t=None, w_out_b=None, late_weights=None,
               loss_head=None):
    B, S, D = x.shape
    cr, sr, cm, sm = tabs
    shift = mod_l[:, 0:D].reshape(B, 1, D)
    scale = mod_l[:, D:2 * D].reshape(B, 1, D)
    gate = mod_l[:, 2 * D:3 * D].reshape(B, 1, D)
    nw = small_l["norm_w"].reshape(1, D)
    qnw = small_l["mla_q_norm"].reshape(1, 256)
    kvnw = small_l["mla_kv_norm"].reshape(1, 128)
    w_g2p = jnp.pad(small_l["gla_w_g2"], ((0, 112), (0, 0)))
    b_g2 = small_l["gla_b_g2"].reshape(1, 128)
    gnw = jnp.tile(small_l["gla_norm"], 4).reshape(1, 256)
    ret_p, mla_p, gla_p, h = _proj_fwd(x, shift, scale, nw, w_in_a)
    r_g, r_raw, r_st = _ret_fwd(ret_p, cr, sr)
    if late_weights is not None:
        w_uq_a, w_ukv_a, w_ukv_t, w_out_b = late_weights(r_raw)
    q, k, v, kt, vt = _mla_prep_fwd(mla_p, cm, sm, qnw, kvnw, w_uq_a, w_ukv_a, w_ukv_t)
    o_mla, lse = _mla_attn_fwd(q, k, vt)
    g_g, g_raw, g_st = _gla_fwd(gla_p, w_g2p, b_g2, gnw)
    if loss_head is None:
        x_new, y = _out_fwd(x, gate, r_g, o_mla, mla_p, g_g, w_out_b)
    else:
        dx, y, loss, d_fw = _out_fwd_loss(x, gate, r_g, o_mla, mla_p, g_g, w_out_b, *loss_head)
        x_new = (dx, loss, d_fw)
    saved = dict(x=x, shift=shift, scale=scale, gate=gate, nw=nw, qnw=qnw, kvnw=kvnw, w_g2p=w_g2p, b_g2=b_g2,
                 gnw=gnw, ret_p=ret_p, mla_p=mla_p, gla_p=gla_p, h=h, r_g=r_g, r_raw=r_raw, r_st=r_st, q=q, k=k,
                 v=v, kt=kt, o_mla=o_mla, lse=lse, g_g=g_g, g_raw=g_raw, g_st=g_st, y=y,
                 w_in_a=w_in_a, w_uq_a=w_uq_a, w_ukv_a=w_ukv_a, w_out_b=w_out_b)
    return x_new, saved


def _layer_bwd(dx, s, tabs, early_grads=None, early_w_in=None):
    B, S, D = dx.shape
    cr, sr, cm, sm = tabs
    d_r, do, d_mz, dl, d_g, dw_out, d_gate = _out_bwd(dx, s["y"], s["gate"], s["r_g"], s["g_g"], s["w_out_b"],
                                                      s["o_mla"], s["mla_p"])
    d_ret = _ret_bwd(s["ret_p"], cr, sr, s["r_raw"], s["r_st"], d_r)
    dq, dk, dv = _mla_attn_bwd(s["q"], s["k"], s["v"], s["kt"], do, s["lse"], dl)
    d_mla, dw_uq, dw_ukv, d_qnw, d_kvnw = _mla_prep_bwd(
        s["mla_p"], cm, sm, s["qnw"], s["kvnw"], s["w_uq_a"], s["w_ukv_a"], dq, dk, dv)
    gnw = s["gnw"] if early_grads is None else s["gnw"] + early_grads(dw_out, dw_uq, dw_ukv)
    d_gla, dw_g2p, db_g2, d_gnw = _gla_bwd(s["gla_p"], s["w_g2p"], s["b_g2"], gnw, s["g_raw"], s["g_st"], d_g)
    dw_in = _proj_bwd_w(s["h"], d_ret, d_mla, d_mz, d_gla)
    nw = s["nw"] if early_w_in is None else s["nw"] + early_w_in(dw_in)
    dx, d_shift, d_scale, d_nw = _proj_bwd_x(s["x"], s["shift"], s["scale"], nw, s["w_in_a"],
                                             d_ret, d_mla, d_mz, d_gla, dx)
    grads = dict(
        d_mod=jnp.concatenate([d_shift, d_scale, d_gate], axis=2).reshape(B, 3 * D),
        norm_w=d_nw.reshape(D), mla_q_norm=d_qnw.reshape(256), mla_kv_norm=d_kvnw.reshape(128),
        gla_w_g2=dw_g2p[0:16], gla_b_g2=db_g2.reshape(128), gla_norm256=d_gnw.reshape(256),
        w_in_a=dw_in, w_uq_a=dw_uq, w_ukv_a=dw_ukv, w_out=dw_out)
    return dx, grads


def _exchange(arrs, gather, name):
    n = len(arrs)
    out_shape = [jax.ShapeDtypeStruct(((N_DEV,) + a.shape) if g else a.shape, a.dtype)
                 for a, g in zip(arrs, gather)]

    def body(*refs):
        ins, outs = refs[:n], refs[n:2 * n]
        send_sems, recv_sems, local_sems = refs[2 * n:]
        ix, iy, ic = lax.axis_index("x"), lax.axis_index("y"), lax.axis_index("c")
        me = 4 * ix + 2 * iy + ic
        copies = []
        for a in range(n):
            mine = ins[a] if gather[a] else ins[a].at[me]
            loc = pltpu.make_async_copy(mine, outs[a].at[me], local_sems.at[a])
            loc.start()
            copies.append(loc)
            for d in range(1, N_DEV):
                px = 1 - ix if d & 4 else ix
                py = 1 - iy if d & 2 else iy
                pc = 1 - ic if d & 1 else ic
                src = ins[a] if gather[a] else ins[a].at[4 * px + 2 * py + pc]
                cp = pltpu.make_async_remote_copy(
                    src_ref=src, dst_ref=outs[a].at[me], send_sem=send_sems.at[a, d - 1],
                    recv_sem=recv_sems.at[a, d - 1], device_id=(px, py, pc), device_id_type=pl.DeviceIdType.MESH)
                cp.start()
                copies.append(cp)
        for cp in copies:
            cp.wait()

    any_spec = pl.BlockSpec(memory_space=pl.ANY)
    outs = pl.pallas_call(
        body, name=name, in_specs=[any_spec] * n, out_specs=[any_spec] * n, out_shape=out_shape,
        scratch_shapes=[pltpu.SemaphoreType.DMA((n, N_DEV - 1)), pltpu.SemaphoreType.DMA((n, N_DEV - 1)),
                        pltpu.SemaphoreType.DMA((n,))],
    )(*arrs)
    return list(outs)


def _peers(ix, iy, ic):
    out = []
    for d in range(1, N_DEV):
        px = 1 - ix if d & 4 else ix
        py = 1 - iy if d & 2 else iy
        pc = 1 - ic if d & 1 else ic
        out.append((d - 1, (px, py, pc), 4 * px + 2 * py + pc))
    return out


def _exchange_start(arrs, gather, name, after=None):
    n = len(arrs)
    lands = [lax.empty(((N_DEV,) + a.shape) if g else a.shape, a.dtype) for a, g in zip(arrs, gather)]
    extra = [] if after is None else [after]

    def body(*refs):
        ins, land_refs = refs[:n], refs[n:2 * n]
        send_sems, recv_sems = refs[2 * n + len(extra)], refs[2 * n + len(extra) + 1]
        token = refs[-1]
        ix, iy, ic = lax.axis_index("x"), lax.axis_index("y"), lax.axis_index("c")
        me = 4 * ix + 2 * iy + ic
        for a in range(n):
            for k, peer, peer_idx in _peers(ix, iy, ic):
                pltpu.make_async_remote_copy(
                    src_ref=ins[a] if gather[a] else ins[a].at[peer_idx], dst_ref=land_refs[a].at[me],
                    send_sem=send_sems.at[7 * a + k], recv_sem=recv_sems.at[7 * a + k], device_id=peer,
                    device_id_type=pl.DeviceIdType.MESH).start()
        token[...] = jnp.zeros_like(token)

    hbm = pl.BlockSpec(memory_space=pltpu.HBM)
    sem = pl.BlockSpec(memory_space=pltpu.SEMAPHORE)
    held = [pltpu.with_memory_space_constraint(a, pltpu.HBM) for a in list(arrs) + lands]
    outs = pl.pallas_call(
        body, name=name,
        out_shape=(pltpu.SemaphoreType.DMA((7 * n,)), pltpu.SemaphoreType.DMA((7 * n,)),
                   *[pltpu.HBM(a.shape, a.dtype) for a in held], jax.ShapeDtypeStruct((8, 128), F32)),
        in_specs=[hbm] * (2 * n) + [pl.BlockSpec(memory_space=pl.ANY)] * len(extra),
        out_specs=(sem, sem, *[hbm] * (2 * n), pl.BlockSpec(memory_space=pltpu.VMEM)),
        input_output_aliases={a: 2 + a for a in range(2 * n)},
        compiler_params=pltpu.CompilerParams(has_side_effects=pltpu.SideEffectType.DATAFLOW_SIDE_EFFECTING),
    )(*held, *extra)
    return dict(send=outs[0], recv=outs[1], srcs=list(outs[2:2 + n]), lands=list(outs[2 + n:2 + 2 * n]),
                token=outs[-1], gather=list(gather))


def _exchange_wait(flight, after, me, name):
    n = len(flight["srcs"])
    gather = flight["gather"]

    def body(*refs):
        srcs, land_refs = refs[:n], refs[n:2 * n]
        send_sems, recv_sems = refs[2 * n], refs[2 * n + 1]
        ix, iy, ic = lax.axis_index("x"), lax.axis_index("y"), lax.axis_index("c")
        mine = 4 * ix + 2 * iy + ic
        for a in range(n):
            for k, peer, peer_idx in _peers(ix, iy, ic):
                cp = pltpu.make_async_remote_copy(
                    src_ref=srcs[a] if gather[a] else srcs[a].at[peer_idx], dst_ref=land_refs[a].at[mine],
                    send_sem=send_sems.at[7 * a + k], recv_sem=recv_sems.at[7 * a + k], device_id=peer,
                    device_id_type=pl.DeviceIdType.MESH)
                cp.wait_send()
                cp.wait_recv()

    hbm = pl.BlockSpec(memory_space=pltpu.HBM)
    sem = pl.BlockSpec(memory_space=pltpu.SEMAPHORE)
    held = flight["srcs"] + flight["lands"]
    outs = pl.pallas_call(
        body, name=name, out_shape=tuple(pltpu.HBM(a.shape, a.dtype) for a in held),
        in_specs=[hbm] * (2 * n) + [sem, sem, pl.BlockSpec(memory_space=pl.ANY)], out_specs=tuple([hbm] * (2 * n)),
        input_output_aliases={a: a for a in range(2 * n)},
        compiler_params=pltpu.CompilerParams(has_side_effects=pltpu.SideEffectType.DATAFLOW_SIDE_EFFECTING),
    )(*held, flight["send"], flight["recv"], after)
    got = []
    for a in range(n):
        src, land = outs[a], outs[n + a]
        own = src if gather[a] else lax.dynamic_index_in_dim(src, me, axis=0, keepdims=False)
        got.append(lax.dynamic_update_index_in_dim(land, own, me, axis=0))
    return got


def _ada_fwd(c_all, ada_w, ada_b_cols):
    nb, D = c_all.shape
    cols = ada_w.shape[2]

    def body(c_ref, w_ref, b_ref, out_ref):
        ca = _silu(c_ref[...])
        for l in range(DEPTH):
            out_ref[l] = _mm(ca, w_ref[l]) + b_ref[l:l + 1, :]

    return pl.pallas_call(
        body, name="ada_fwd", out_shape=jax.ShapeDtypeStruct((DEPTH, nb, cols), F32),
        in_specs=[pl.BlockSpec(memory_space=pltpu.VMEM)] * 3, out_specs=pl.BlockSpec(memory_space=pltpu.VMEM),
        compiler_params=pltpu.CompilerParams(vmem_limit_bytes=32 * VMEM_MB),
    )(c_all, ada_w, ada_b_cols)


def _ada_bwd(c_all, d_mod_cols):
    nb, D = c_all.shape
    cols = d_mod_cols.shape[2]

    def body(c_ref, dm_ref, out_ref):
        ca = _silu(c_ref[...])
        for l in range(DEPTH):
            out_ref[l] = _mm_tn(ca, dm_ref[l])

    return pl.pallas_call(
        body, name="ada_bwd", out_shape=jax.ShapeDtypeStruct((DEPTH, D, cols), F32),
        in_specs=[pl.BlockSpec(memory_space=pltpu.VMEM)] * 2, out_specs=pl.BlockSpec(memory_space=pltpu.VMEM),
        compiler_params=pltpu.CompilerParams(vmem_limit_bytes=32 * VMEM_MB),
    )(c_all, d_mod_cols)


def _sum_adamw(parts, w, m, v, name, after=None):
    P, R, C = parts.shape
    tr = 256 if (R % 256 == 0 and R > 256) else R
    extra = [] if after is None else [after]

    def body(p_ref, w_ref, m_ref, v_ref, *rest):
        g_ref, d_ref, nm_ref, nv_ref = rest[-4:]
        g = p_ref[0].astype(F32)
        for k in range(1, P):
            g = g + p_ref[k].astype(F32)
        g_ref[...] = g
        nm = ADAM_B1 * m_ref[...] + (1.0 - ADAM_B1) * g
        nv = ADAM_B2 * v_ref[...] + (1.0 - ADAM_B2) * (g * g)
        nm_ref[...] = nm
        nv_ref[...] = nv
        m_hat = nm / (1.0 - ADAM_B1 ** ADAM_STEP)
        v_hat = nv / (1.0 - ADAM_B2 ** ADAM_STEP)
        d_ref[...] = -ADAM_LR * (m_hat / (jnp.sqrt(v_hat) + ADAM_EPS) + ADAM_WD * w_ref[...])

    blk = pl.BlockSpec((tr, C), lambda i: (i, 0))
    shp = jax.ShapeDtypeStruct((R, C), F32)
    return pl.pallas_call(
        body, name=name, grid=(R // tr,),
        in_specs=[pl.BlockSpec((P, tr, C), lambda i: (0, i, 0)), blk, blk, blk]
        + [pl.BlockSpec(memory_space=pl.ANY)] * len(extra),
        out_specs=[blk, blk, blk, blk], out_shape=[shp, shp, shp, shp],
        compiler_params=_cp(("parallel",)),
    )(parts, w, m, v, *extra)


def _sum_adamw_layer(parts, w, m, v, layer, name, prev=None, after=None):
    P, R, C = parts.shape
    tr = 256 if (R % 256 == 0 and R > 256) else R

    def body(p_ref, w_ref, m_ref, v_ref, *rest):
        g_ref, d_ref, nm_ref, nv_ref = rest[-4:]
        g = p_ref[0].astype(F32)
        for k in range(1, P):
            g = g + p_ref[k].astype(F32)
        g_ref[0] = g
        nm = ADAM_B1 * m_ref[0] + (1.0 - ADAM_B1) * g
        nv = ADAM_B2 * v_ref[0] + (1.0 - ADAM_B2) * (g * g)
        nm_ref[0] = nm
        nv_ref[0] = nv
        m_hat = nm / (1.0 - ADAM_B1 ** ADAM_STEP)
        v_hat = nv / (1.0 - ADAM_B2 ** ADAM_STEP)
        d_ref[0] = -ADAM_LR * (m_hat / (jnp.sqrt(v_hat) + ADAM_EPS) + ADAM_WD * w_ref[0])

    blk = pl.BlockSpec((1, tr, C), lambda i: (layer, i, 0))
    shp = jax.ShapeDtypeStruct(w.shape, F32)
    in_specs = [pl.BlockSpec((P, tr, C), lambda i: (0, i, 0)), blk, blk, blk]
    args = [parts, w, m, v]
    aliases = {}
    if prev is not None:
        in_specs += [pl.BlockSpec(memory_space=pl.ANY)] * 4
        args += list(prev)
        aliases = {4 + k: k for k in range(4)}
    if after is not None:
        in_specs.append(pl.BlockSpec(memory_space=pl.ANY))
        args.append(after)
    return list(pl.pallas_call(
        body, name=name, grid=(R // tr,), in_specs=in_specs, out_specs=[blk] * 4, out_shape=[shp] * 4,
        input_output_aliases=aliases, compiler_params=_cp(("parallel",)),
    )(*args))


SMALL = ["norm_w", "mla_q_norm", "mla_kv_norm", "gla_w_g2", "gla_b_g2", "gla_norm", "final_norm"]


SMALL_ROWS = 72


def _pack_small(loss, part):
    flat = [jnp.pad(loss.reshape(1), (0, 127))] + [part[n].reshape(-1) for n in SMALL]
    used = sum(f.shape[0] for f in flat)
    flat.append(jnp.zeros((SMALL_ROWS * 128 - used,), F32))
    return jnp.concatenate(flat).reshape(SMALL_ROWS, 128)


def _small_adamw(packed_parts, w, m, v, after=None):
    n = len(w)
    extra = [] if after is None else [after]

    def body(*refs):
        p_ref = refs[0]
        w_refs, m_refs, v_refs = refs[1:1 + n], refs[1 + n:1 + 2 * n], refs[1 + 2 * n:1 + 3 * n]
        outs, acc = refs[1 + 3 * n + len(extra):-1], refs[-1]
        total = p_ref[0]
        for k in range(1, N_DEV):
            total = total + p_ref[k]
        acc[...] = total
        outs[0][...] = acc[0:1, :]
        r0 = 1
        for i in range(n):
            shp = w_refs[i].shape
            if len(shp) == 3:
                g = acc[r0:r0 + shp[0] * shp[1], :].reshape(shp)
                r0 += shp[0] * shp[1]
            elif shp[1] < 128:
                g = acc[r0:r0 + shp[0], 0:shp[1]]
                r0 += shp[0]
            else:
                k = shp[1] // 128
                g = jnp.concatenate(
                    [jnp.concatenate([acc[r0 + l * k + j:r0 + l * k + j + 1, :] for j in range(k)], axis=1)
                     for l in range(shp[0])], axis=0)
                r0 += shp[0] * k
            nm = ADAM_B1 * m_refs[i][...] + (1.0 - ADAM_B1) * g
            nv = ADAM_B2 * v_refs[i][...] + (1.0 - ADAM_B2) * (g * g)
            m_hat = nm / (1.0 - ADAM_B1 ** ADAM_STEP)
            v_hat = nv / (1.0 - ADAM_B2 ** ADAM_STEP)
            outs[1 + 4 * i][...] = g
            outs[2 + 4 * i][...] = -ADAM_LR * (m_hat / (jnp.sqrt(v_hat) + ADAM_EPS) + ADAM_WD * w_refs[i][...])
            outs[3 + 4 * i][...] = nm
            outs[4 + 4 * i][...] = nv

    vmem = pl.BlockSpec(memory_space=pltpu.VMEM)
    out_shape = [jax.ShapeDtypeStruct((1, 128), F32)]
    for a in w:
        out_shape += [jax.ShapeDtypeStruct(a.shape, F32)] * 4
    outs = pl.pallas_call(
        body, name="adamw_small", in_specs=[vmem] * (1 + 3 * n) + [pl.BlockSpec(memory_space=pl.ANY)] * len(extra),
        out_specs=[vmem] * (1 + 4 * n), out_shape=out_shape, scratch_shapes=[pltpu.VMEM((SMALL_ROWS, 128), F32)],
    )(packed_parts, *w, *m, *v, *extra)
    return outs[0], [outs[1 + 4 * i:5 + 4 * i] for i in range(n)]


WEIGHTS = ["norm_w", "ada_w", "ada_b", "w_in", "mla_q_norm", "w_uq", "mla_kv_norm", "w_ukv", "gla_w_g2",
           "gla_b_g2", "gla_norm", "w_out", "final_norm"]


def kernel(x, c, positions, norm_w, ada_w, ada_b, w_in, mla_q_norm, w_uq, mla_kv_norm, w_ukv, gla_w_g2, gla_b_g2, gla_norm, w_out, final_norm, loss_target, m_norm_w, m_ada_w, m_ada_b, m_w_in, m_mla_q_norm, m_w_uq, m_mla_kv_norm, m_w_ukv, m_gla_w_g2, m_gla_b_g2, m_gla_norm, m_w_out, m_final_norm, v_norm_w, v_ada_w, v_ada_b, v_w_in, v_mla_q_norm, v_w_uq, v_mla_kv_norm, v_w_ukv, v_gla_w_g2, v_gla_b_g2, v_gla_norm, v_w_out, v_final_norm):
    w = dict(norm_w=norm_w, ada_w=ada_w, ada_b=ada_b, w_in=w_in, mla_q_norm=mla_q_norm, w_uq=w_uq,
             mla_kv_norm=mla_kv_norm, w_ukv=w_ukv, gla_w_g2=gla_w_g2, gla_b_g2=gla_b_g2, gla_norm=gla_norm,
             w_out=w_out, final_norm=final_norm)
    m = dict(norm_w=m_norm_w, ada_w=m_ada_w, ada_b=m_ada_b, w_in=m_w_in, mla_q_norm=m_mla_q_norm, w_uq=m_w_uq,
             mla_kv_norm=m_mla_kv_norm, w_ukv=m_w_ukv, gla_w_g2=m_gla_w_g2, gla_b_g2=m_gla_b_g2,
             gla_norm=m_gla_norm, w_out=m_w_out, final_norm=m_final_norm)
    v = dict(norm_w=v_norm_w, ada_w=v_ada_w, ada_b=v_ada_b, w_in=v_w_in, mla_q_norm=v_mla_q_norm, w_uq=v_w_uq,
             mla_kv_norm=v_mla_kv_norm, w_ukv=v_w_ukv, gla_w_g2=v_gla_w_g2, gla_b_g2=v_gla_b_g2,
             gla_norm=v_gla_norm, w_out=v_w_out, final_norm=v_final_norm)
    B, S, D = x.shape
    me = 4 * lax.axis_index("x") + 2 * lax.axis_index("y") + lax.axis_index("c")
    ada_cols = ada_w.shape[2]
    cast = lambda a: a.astype(_MXU)

    sharded = ["w_in", "w_uq", "w_ukv", "w_out"]

    whole_in = _arrange_w_in
    whole_rest = lambda blks: (*_arrange_mla_weights(blks[0], blks[1]), blks[2].reshape(D, D))
    blocks_in = lambda dw_in_a: _unarrange_w_in(dw_in_a, N_DEV, w_in.shape[2])
    blocks_rest = lambda dw_out, dw_uq_a, dw_ukv_a: [
        *_unarrange_mla_weights(dw_uq_a, dw_ukv_a), dw_out.reshape(N_DEV, D // N_DEV, D).astype(jnp.bfloat16)]

    (c_g,) = _exchange([c], [True], "gather_c")
    c_all = c_g.reshape(N_DEV * B, D)

    ada_b_cols = lax.dynamic_slice(ada_b, (0, me * ada_cols), (DEPTH, ada_cols))
    mod_cols = _ada_fwd(c_all, ada_w, ada_b_cols)
    mod_send = jnp.transpose(mod_cols.reshape(DEPTH, N_DEV, B, ada_cols), (1, 0, 2, 3))
    (mod_recv,) = _exchange([mod_send], [False], "scatter_mod")
    mod = jnp.transpose(mod_recv, (1, 2, 0, 3)).reshape(DEPTH, B, 3 * D)

    flight_i = _exchange_start([cast(w_in[0])], [True], "gather_start_first", after=mod)
    flight_r = _exchange_start([cast(w[n][0]) for n in sharded[1:]], [True] * 3, "gather_start_layer0",
                               after=flight_i["token"])
    flight_w = _exchange_start([cast(w[n][1]) for n in sharded], [True] * 4, "gather_start_layer1",
                               after=flight_r["token"])
    small_w = {n: w[n] for n in SMALL}
    layer_small = lambda l: {n: a[l] for n, a in small_w.items() if n != "final_norm"}
    tabs = _rope_tables(positions.reshape(B, S, 1), flight_w["token"][0, 0])
    late0 = lambda after: whole_rest(_exchange_wait(flight_r, after, me, "gather_wait_layer0"))
    (w_in0_g,) = _exchange_wait(flight_i, tabs[0], me, "gather_wait_first")
    x1, saved0 = _layer_fwd(x, tabs, mod[0], layer_small(0), whole_in(w_in0_g), late_weights=late0)
    got1 = _exchange_wait(flight_w, x1, me, "gather_wait_layer1")
    (dx, loss, d_fw), saved1 = _layer_fwd(x1, tabs, mod[1], layer_small(1), whole_in(got1[0]), *whole_rest(got1[1:]),
                                          loss_head=(final_norm.reshape(1, D), loss_target))

    dx, g1 = _layer_bwd(dx, saved1, tabs)
    flight_g = _exchange_start([blocks_in(g1["w_in_a"])] + blocks_rest(g1["w_out"], g1["w_uq_a"], g1["w_ukv_a"]),
                               [False] * 4, "grads_start_layer1")
    flights = {}

    def early0(dw_out, dw_uq_a, dw_ukv_a):
        flights["rest0"] = _exchange_start(blocks_rest(dw_out, dw_uq_a, dw_ukv_a), [False] * 3, "grads_start_layer0")
        return flights["rest0"]["token"][0, 0]

    def early_in0(dw_in_a):
        flights["in0"] = _exchange_start([blocks_in(dw_in_a)], [False], "exchange_start_last")
        return flights["in0"]["token"][0, 0]

    saved0 = dict(saved0, gate=saved0["gate"] + flight_g["token"][0, 0])
    grad_x, g0 = _layer_bwd(dx, saved0, tabs, early_grads=early0, early_w_in=early_in0)
    parts1 = _exchange_wait(flight_g, grad_x, me, "grads_wait_layer1")
    rest0 = _exchange_wait(flights["rest0"], g0["w_in_a"], me, "grads_wait_layer0")

    both = lambda n: jnp.stack([g0[n], g1[n]])
    d_mod = both("d_mod")
    part = dict(norm_w=both("norm_w"), mla_q_norm=both("mla_q_norm"), mla_kv_norm=both("mla_kv_norm"),
                gla_w_g2=both("gla_w_g2"), gla_b_g2=both("gla_b_g2"), gla_norm=both("gla_norm256")[:, 0:128],
                final_norm=d_fw)
    d_mod_g, small_g = _exchange([d_mod, _pack_small(loss, part)], [True, True], "gather_small")
    flight_l = flights["in0"]
    res = {}
    behind = flight_l["token"]
    for a, name in enumerate(sharded):
        res[name] = _sum_adamw_layer(parts1[a], w[name], m[name], v[name], 1, "adamw_%s_layer1" % name, after=behind)
        behind = res[name][1]
    for a, name in enumerate(sharded[1:]):
        res[name] = _sum_adamw_layer(rest0[a], w[name], m[name], v[name], 0, "adamw_%s_layer0" % name,
                                     prev=res[name], after=behind)
        behind = res[name][1]

    d_mod_all = jnp.transpose(d_mod_g, (1, 0, 2, 3)).reshape(DEPTH, N_DEV * B, 3 * D)
    d_mod_cols = lax.dynamic_slice(d_mod_all, (0, 0, me * ada_cols), (DEPTH, N_DEV * B, ada_cols))
    g_ada_w = _ada_bwd(c_all, d_mod_cols)

    def update(name, parts2d, after):
        shp = w[name].shape
        two = lambda a: a.reshape(parts2d.shape[1:])
        out = _sum_adamw(parts2d, two(w[name]), two(m[name]), two(v[name]), "adamw_" + name, after=after)
        res[name] = [o.reshape(shp) for o in out]
        return out[1]

    behind = update("ada_w", g_ada_w.reshape(1, DEPTH * D, ada_cols), behind)
    behind = update("ada_b", jnp.transpose(d_mod_g, (0, 2, 1, 3)).reshape(N_DEV * B, DEPTH * 3 * D // 128, 128), behind)
    row = lambda a: a.reshape(1, D) if a.ndim == 1 else a
    loss_sum, small_out = _small_adamw(small_g, [row(w[n]) for n in SMALL], [row(m[n]) for n in SMALL],
                                       [row(v[n]) for n in SMALL], after=behind)
    for n, outs in zip(SMALL, small_out):
        res[n] = [o.reshape(w[n].shape) for o in outs]
    loss_out = loss_sum[0, 0]
    (in0,) = _exchange_wait(flight_l, loss_sum, me, "exchange_wait_last")
    res["w_in"] = _sum_adamw_layer(in0, w_in, m_w_in, v_w_in, 0, "adamw_w_in_layer0", prev=res["w_in"])
    return (loss_out, grad_x, *[res[n][0] for n in WEIGHTS], *[res[n][1] for n in WEIGHTS],
            *[res[n][2] for n in WEIGHTS], *[res[n][3] for n in WEIGHTS])
```

```python
import functools
import math

import numpy as np
import jax
import jax.numpy as jnp
from jax import lax
from jax.experimental import pallas as pl
from jax.experimental.pallas import tpu as pltpu

F32 = jnp.float32
_MXU = jnp.bfloat16

D_MODEL = 1024
DEPTH = 2
CHUNK = 64
EPS = 1e-6
ROPE_THETA = 10000.0
N_DEV = 8

MLA_SCALE = 96.0 ** -0.5
RET_KSCALE = 64.0 ** -0.5
GLA_KSCALE = 32.0 ** -0.5
GLA_TAU = 16.0

ADAM_LR = 0.001
ADAM_B1 = 0.9
ADAM_B2 = 0.999
ADAM_EPS = 1e-08
ADAM_WD = 0.01
ADAM_STEP = 10

RET_W, MLA_W, GLA_W = 1024, 1024, 896
ARR_W = RET_W + MLA_W + GLA_W
VMEM_MB = 1024 * 1024


def _cp(sem, vmem_mb=48):
    return pltpu.CompilerParams(dimension_semantics=sem, vmem_limit_bytes=vmem_mb * VMEM_MB)


def _mm(a, b):
    return jnp.dot(a.astype(_MXU), b.astype(_MXU), preferred_element_type=F32)


def _mm_nt(a, b):
    return lax.dot_general(a.astype(_MXU), b.astype(_MXU), (((1,), (1,)), ((), ())),
                           preferred_element_type=F32)


def _mm_tn(a, b):
    return lax.dot_general(a.astype(_MXU), b.astype(_MXU), (((0,), (0,)), ((), ())),
                           preferred_element_type=F32)


def _mm_f32(a, b):
    return jnp.dot(a, b, precision=lax.Precision.HIGHEST, preferred_element_type=F32)


def _sig(z):
    return 1.0 / (1.0 + jnp.exp(-z))


def _silu(z):
    return z * _sig(z)


def _dsilu(z):
    s = _sig(z)
    return s * (1.0 + z * (1.0 - s))


def _full(shape):
    nd = len(shape)
    return pl.BlockSpec(shape, lambda *_: (0,) * nd)


def _w_in_runs(block_cols):
    m, g = RET_W, RET_W + MLA_W
    whole = [(base + 64 * h + 32 * t, 32, base + 128 * t + 32 * h)
             for base in (0, 256) for t in range(2) for h in range(4)]
    whole += [(512, 512, 512), (1024, 384, m), (1408, 32, m + 448), (1440, 512, m + 512),
              (1952, 528, g), (2480, 256, g + 640)]
    zeros = [(m + 384, 64), (m + 480, 32), (g + 528, 112)]
    runs = []
    for src, n, dst in whole:
        while n:
            blk, off = divmod(src, block_cols)
            k = min(n, block_cols - off)
            runs.append((blk, off, k, dst))
            src, n, dst = src + k, n - k, dst + k
    return runs, zeros


def _arrange_w_in(blocks, tm=256):
    n, rows, cols = blocks.shape
    runs, zeros = _w_in_runs(cols)

    def arrange_w_in_kernel(b_ref, a_ref):
        for dst, k in zeros:
            a_ref[:, dst:dst + k] = jnp.zeros((tm, k), a_ref.dtype)
        for blk, off, k, dst in runs:
            a_ref[:, dst:dst + k] = b_ref[blk, :, off:off + k]

    return pl.pallas_call(
        arrange_w_in_kernel, grid=(rows // tm,),
        in_specs=[pl.BlockSpec((n, tm, cols), lambda i: (0, i, 0))],
        out_specs=pl.BlockSpec((tm, ARR_W), lambda i: (i, 0)),
        out_shape=jax.ShapeDtypeStruct((rows, ARR_W), blocks.dtype),
        compiler_params=_cp(("parallel",)), name="arrange_w_in")(blocks)


def _unarrange_w_in(a, n, cols, tm=256):
    rows = a.shape[0]
    runs, _ = _w_in_runs(cols)

    def unarrange_w_in_kernel(a_ref, b_ref):
        for blk, off, k, dst in runs:
            b_ref[blk, :, off:off + k] = a_ref[:, dst:dst + k].astype(b_ref.dtype)

    return pl.pallas_call(
        unarrange_w_in_kernel, grid=(rows // tm,),
        in_specs=[pl.BlockSpec((tm, ARR_W), lambda i: (i, 0))],
        out_specs=pl.BlockSpec((n, tm, cols), lambda i: (0, i, 0)),
        out_shape=jax.ShapeDtypeStruct((n, rows, cols), jnp.bfloat16),
        compiler_params=_cp(("parallel",)), name="unarrange_w_in")(a)


def _arrange_mla_weights(uq_heads, ukv_heads):
    nh = uq_heads.shape[0]
    dt = uq_heads.dtype

    def arrange_mla_weights_kernel(uq_ref, ukv_ref, q_ref, kv_ref, kvt_ref):
        q_ref[...] = jnp.zeros(q_ref.shape, dt)
        kv_ref[...] = jnp.zeros(kv_ref.shape, dt)
        kvt_ref[...] = jnp.zeros(kvt_ref.shape, dt)
        for h in range(nh):
            q_ref[:, 128 * h:128 * h + 96] = uq_ref[h]
            blk = ukv_ref[h]
            kv_ref[:, 128 * h:128 * h + 64] = blk[:, 0:64]
            kv_ref[:, 128 * nh + 64 * h:128 * nh + 64 * h + 64] = blk[:, 64:128]
            blk_t = blk.astype(F32).T.astype(dt)
            kvt_ref[128 * h:128 * h + 64, :] = blk_t[0:64]
            kvt_ref[128 * nh + 64 * h:128 * nh + 64 * h + 64, :] = blk_t[64:128]

    return pl.pallas_call(
        arrange_mla_weights_kernel, name="arrange_mla_weights",
        out_shape=[jax.ShapeDtypeStruct((256, 128 * nh), dt), jax.ShapeDtypeStruct((128, 192 * nh), dt),
                   jax.ShapeDtypeStruct((192 * nh, 128), dt)])(uq_heads, ukv_heads)


def _unarrange_mla_weights(dw_uq_a, dw_ukv_a):
    nh = dw_uq_a.shape[1] // 128

    def unarrange_mla_weights_kernel(q_ref, kv_ref, uq_ref, ukv_ref):
        for h in range(nh):
            uq_ref[h] = q_ref[:, 128 * h:128 * h + 96].astype(uq_ref.dtype)
            ukv_ref[h, :, 0:64] = kv_ref[:, 128 * h:128 * h + 64].astype(ukv_ref.dtype)
            ukv_ref[h, :, 64:128] = kv_ref[:, 128 * nh + 64 * h:128 * nh + 64 * h + 64].astype(ukv_ref.dtype)

    return pl.pallas_call(
        unarrange_mla_weights_kernel, name="unarrange_mla_weights",
        out_shape=[jax.ShapeDtypeStruct((nh, 256, 96), jnp.bfloat16),
                   jax.ShapeDtypeStruct((nh, 128, 128), jnp.bfloat16)])(dw_uq_a, dw_ukv_a)


def _rope_tables(pos3, zero=0.0):
    B, S, _ = pos3.shape
    ts = min(S, 512)
    inv32 = (np.float32(ROPE_THETA) ** (-(np.arange(32, dtype=np.float32) / 32))).astype(np.float32)
    inv16 = (np.float32(ROPE_THETA) ** (-(np.arange(16, dtype=np.float32) / 16))).astype(np.float32)
    inv = np.zeros((1, 128), np.float32)
    inv[0, 0:32] = inv32
    inv[0, 32:48] = inv16

    def body(pos_ref, inv_ref, cr, sr, cm, sm):
        ang = pos_ref[0].astype(F32) * inv_ref[...]
        lane = lax.broadcasted_iota(jnp.int32, (1, 128), 1)

        def every_head(x):
            y = jnp.where(lane < 32, x, pltpu.roll(x, 32, 1))
            return jnp.where(lane < 64, y, pltpu.roll(y, 64, 1))

        def rotary_pair(x, fill):
            return jnp.where((lane >= 64) & (lane < 80), pltpu.roll(x, 32, 1),
                             jnp.where((lane >= 80) & (lane < 96), pltpu.roll(x, 48, 1), fill))

        c, s = jnp.cos(ang), jnp.sin(ang)
        cr[0] = every_head(c)
        sr[0] = every_head(s)
        cm[0] = rotary_pair(c, 1.0)
        sm[0] = rotary_pair(s, 0.0)

    tab = jax.ShapeDtypeStruct((B, S, 128), F32)
    blk = pl.BlockSpec((1, ts, 128), lambda b, i: (b, i, 0))
    return pl.pallas_call(
        body, name="rope_tables", grid=(B, S // ts),
        in_specs=[pl.BlockSpec((1, ts, 1), lambda b, i: (b, i, 0)), _full((1, 128))],
        out_specs=[blk, blk, blk, blk], out_shape=[tab, tab, tab, tab],
        compiler_params=_cp(("parallel", "parallel")),
    )(pos3, jnp.asarray(inv) + zero)


def _rope128(x, cos, sin):
    lane = lax.broadcasted_iota(jnp.int32, (1, 128), 1)
    rp = pltpu.roll(x, 16, 1)
    rm = pltpu.roll(x, 112, 1)
    return x * cos + jnp.where(lane < 80, -rm, rp) * sin


def _rope128_t(d, cos, sin):
    lane = lax.broadcasted_iota(jnp.int32, (1, 128), 1)
    y = d * sin
    yp = pltpu.roll(y, 16, 1)
    ym = pltpu.roll(y, 112, 1)
    return d * cos + jnp.where(lane < 64, 0.0, jnp.where(lane < 80, ym, jnp.where(lane < 96, -yp, 0.0)))


def _proj_fwd(x, shift, scale, nw, w_arr):
    B, S, D = x.shape
    tm = min(S, 512)

    def body(x_ref, sh_ref, sc_ref, nw_ref, w_ref, ret_ref, mla_ref, gla_ref, h_ref):
        xv = x_ref[0]
        rstd = lax.rsqrt(jnp.mean(xv * xv, axis=-1, keepdims=True) + EPS)
        h = (xv * rstd * nw_ref[...]) * (1.0 + sc_ref[0]) + sh_ref[0]
        hb = h.astype(_MXU)
        h_ref[0] = hb
        ret_ref[0] = jnp.dot(hb, w_ref[:, 0:RET_W], preferred_element_type=F32).astype(_MXU)
        mla_ref[0] = jnp.dot(hb, w_ref[:, RET_W:RET_W + MLA_W], preferred_element_type=F32).astype(_MXU)
        gla_ref[0] = jnp.dot(hb, w_ref[:, RET_W + MLA_W:ARR_W], preferred_element_type=F32).astype(_MXU)

    tok = lambda w: pl.BlockSpec((1, tm, w), lambda b, i: (b, i, 0))
    per_seq = pl.BlockSpec((1, 1, D), lambda b, i: (b, 0, 0))
    return pl.pallas_call(
        body, name="proj_fwd", grid=(B, S // tm),
        in_specs=[tok(D), per_seq, per_seq, _full((1, D)), _full((D, ARR_W))],
        out_specs=[tok(RET_W), tok(MLA_W), tok(GLA_W), tok(D)],
        out_shape=[jax.ShapeDtypeStruct((B, S, RET_W), _MXU), jax.ShapeDtypeStruct((B, S, MLA_W), _MXU),
                   jax.ShapeDtypeStruct((B, S, GLA_W), _MXU), jax.ShapeDtypeStruct((B, S, D), _MXU)],
        compiler_params=_cp(("parallel", "parallel")),
    )(x, shift, scale, nw, w_arr)


RET_L = 256


def _ret_consts(L):
    lg = np.log1p(-np.exp2(-5.0 - np.arange(4, dtype=np.float32))).astype(np.float32)
    i = np.arange(L)
    ci = i // CHUNK
    diff = (i[:, None] - i[None, :]).astype(np.float32)
    same = ci[:, None] == ci[None, :]
    past = ci[None, :] < ci[:, None]
    expo = np.where(same, np.abs(diff), np.where(past, diff, 0.0)).astype(np.float32)
    dec = np.where((same | past)[None], np.exp(lg[:, None, None] * expo[None]), 0.0).astype(np.float32)
    head = (np.arange(256) % 128) // 32
    qw = np.exp((i + 1.0)[:, None] * lg[head][None, :]).astype(np.float32)
    kw = np.exp((L - 1.0 - i)[:, None] * lg[head][None, :]).astype(np.float32)
    a_row = np.exp(np.float32(L) * lg[head])[None, :].astype(np.float32)
    return [jnp.asarray(t) for t in (dec.reshape(4 * L, L), qw, kw, a_row)]


def _ret_masks():
    lane = lax.broadcasted_iota(jnp.int32, (1, 256), 1)
    mh = [((lane % 128) // 32) == h for h in range(4)]
    mv = [(lane // 64) == h for h in range(4)]
    vi = lax.broadcasted_iota(jnp.int32, (256, 256), 0)
    ki = lax.broadcasted_iota(jnp.int32, (256, 256), 1)
    bd = (vi // 64) == ((ki % 128) // 32)
    return mh, mv, bd


def _ret_rope(p, cs, sn):
    q1, q2, k1, k2 = p[:, 0:128], p[:, 128:256], p[:, 256:384], p[:, 384:512]
    qr = jnp.concatenate([q1 * cs - q2 * sn, q2 * cs + q1 * sn], axis=1)
    kr = jnp.concatenate([k1 * cs - k2 * sn, k2 * cs + k1 * sn], axis=1) * RET_KSCALE
    return qr, kr


def _head_mean(x, mv, width):
    out = jnp.zeros_like(x)
    for m in mv:
        s = jnp.sum(jnp.where(m, x, 0.0), axis=-1, keepdims=True) * (1.0 / width)
        out = jnp.where(m, s, out)
    return out


def _stack_heads(x, masks):
    return jnp.concatenate([jnp.where(m, x, 0.0) for m in masks], axis=0)


def _fold_heads(xs, masks, L):
    out = jnp.where(masks[0], xs[0:L], 0.0)
    for h in range(1, 4):
        out = out + jnp.where(masks[h], xs[h * L:(h + 1) * L], 0.0)
    return out


RET_G = 2


def _ret_fwd(ret_p, cos, sin):
    B, S, _ = ret_p.shape
    L = min(RET_L, S)
    NB = S // L
    G = min(RET_G, NB)
    NG = NB // G
    consts = _ret_consts(L)

    def body(p_ref, c_ref, s_ref, ds_ref, qw_ref, kw_ref, a_ref, out_ref, raw_ref, st_ref, st_sc):
        @pl.when(pl.program_id(1) == 0)
        def _():
            st_sc[...] = jnp.zeros_like(st_sc)

        mh, mv, bd = _ret_masks()
        cs_ = range(G)
        rows = [slice(c * L, (c + 1) * L) for c in cs_]
        ps = [p_ref[0, rows[c], :].astype(F32) for c in cs_]
        qk = [_ret_rope(ps[c], c_ref[0, rows[c], :], s_ref[0, rows[c], :]) for c in cs_]
        vs = [ps[c][:, 512:768] for c in cs_]
        a_s = [_mm_nt(_stack_heads(qk[c][0], mh), qk[c][1]) for c in cs_]
        upd = [_mm_tn(vs[c], qk[c][1] * kw_ref[...]) for c in cs_]
        o_s = [_mm(a_s[c] * ds_ref[...], vs[c]) for c in cs_]
        st = st_sc[...]
        inter = []
        for c in cs_:
            st_ref[0, c] = st
            inter.append(_mm_nt(qk[c][0] * qw_ref[...], st))
            st = st * a_ref[...] + jnp.where(bd, upd[c], 0.0)
        st_sc[...] = st
        for c in cs_:
            r = _fold_heads(o_s[c], mv, L) + inter[c]
            raw_ref[0, rows[c], :] = r
            rstd = lax.rsqrt(_head_mean(r * r, mv, 64.0) + EPS)
            out_ref[0, rows[c], :] = (r * rstd * _silu(ps[c][:, 768:1024])).astype(_MXU)

    tok = lambda w: pl.BlockSpec((1, G * L, w), lambda b, n: (b, n, 0))
    return pl.pallas_call(
        body, name="ret_fwd", grid=(B, NG),
        in_specs=[tok(RET_W), tok(128), tok(128), _full((4 * L, L)), _full((L, 256)), _full((L, 256)),
                  _full((1, 256))],
        out_specs=[tok(256), tok(256), pl.BlockSpec((1, G, 256, 256), lambda b, n: (b, n, 0, 0))],
        out_shape=[jax.ShapeDtypeStruct((B, S, 256), _MXU), jax.ShapeDtypeStruct((B, S, 256), F32),
                   jax.ShapeDtypeStruct((B, NB, 256, 256), F32)],
        scratch_shapes=[pltpu.VMEM((256, 256), F32)],
        compiler_params=_cp(("parallel", "arbitrary")),
    )(ret_p, cos, sin, *consts)


def _ret_bwd(ret_p, cos, sin, raw, states, d_mix):
    B, S, _ = ret_p.shape
    L = min(RET_L, S)
    NB = S // L
    G = 1
    NG = NB // G
    consts = _ret_consts(L)

    def body(p_ref, c_ref, s_ref, raw_ref, st_ref, dm_ref, ds_ref, qw_ref, kw_ref, a_ref, dp_ref, dst_sc):
        @pl.when(pl.program_id(1) == 0)
        def _():
            dst_sc[...] = jnp.zeros_like(dst_sc)

        mh, mv, bd = _ret_masks()
        qw, kw, dec = qw_ref[...], kw_ref[...], ds_ref[...]
        cs_ = range(G)
        rows = [slice(c * L, (c + 1) * L) for c in cs_]
        ps = [p_ref[0, rows[c], :].astype(F32) for c in cs_]
        tabs = [(c_ref[0, rows[c], :], s_ref[0, rows[c], :]) for c in cs_]
        qk = [_ret_rope(ps[c], *tabs[c]) for c in cs_]
        vs = [ps[c][:, 512:768] for c in cs_]
        qs = [_stack_heads(qk[c][0], mh) for c in cs_]
        a_s = [_mm_nt(qs[c], qk[c][1]) for c in cs_]
        dr, dz = [], []
        for c in cs_:
            r = raw_ref[0, rows[c], :]
            z = ps[c][:, 768:1024]
            rstd = lax.rsqrt(_head_mean(r * r, mv, 64.0) + EPS)
            rn = r * rstd
            dm = dm_ref[0, rows[c], :]
            d_rn = dm * _silu(z)
            dz.append(dm * rn * _dsilu(z))
            dr.append(rstd * (d_rn - rn * _head_mean(d_rn * rn, mv, 64.0)))
        do_s = [_stack_heads(dr[c], mv) for c in cs_]
        da_s = [_mm_nt(do_s[c], vs[c]) for c in cs_]
        sts = [st_ref[0, c] for c in cs_]
        dq_st = [_mm(dr[c], sts[c]) for c in cs_]
        dst_in = [_mm_tn(dr[c], qk[c][0] * qw) for c in cs_]
        dv = [_mm_tn(a_s[c] * dec, do_s[c]) for c in cs_]
        dqr, dkr = [], []
        for c in cs_:
            da = da_s[c] * dec
            dqr.append(_fold_heads(_mm(da, qk[c][1]), mh, L) + dq_st[c] * qw)
            dkr.append(_mm_tn(da, qs[c]))
        dst_next = dst_sc[...]
        for c in reversed(cs_):
            g = jnp.where(bd, dst_next, 0.0)
            dv[c] = dv[c] + _mm_nt(qk[c][1] * kw, g)
            dkr[c] = dkr[c] + _mm(vs[c], g) * kw
            dst_next = dst_next * a_ref[...] + jnp.where(bd, dst_in[c], 0.0)
        dst_sc[...] = dst_next
        for c in cs_:
            cs, sn = tabs[c]
            dk = dkr[c] * RET_KSCALE
            dq1, dq2 = dqr[c][:, 0:128], dqr[c][:, 128:256]
            dk1, dk2 = dk[:, 0:128], dk[:, 128:256]
            dp_ref[0, rows[c], :] = jnp.concatenate(
                [dq1 * cs + dq2 * sn, dq2 * cs - dq1 * sn, dk1 * cs + dk2 * sn, dk2 * cs - dk1 * sn, dv[c], dz[c]],
                axis=1).astype(_MXU)

    tok = lambda w: pl.BlockSpec((1, G * L, w), lambda b, i: (b, NG - 1 - i, 0))
    return pl.pallas_call(
        body, name="ret_bwd", grid=(B, NG),
        in_specs=[tok(RET_W), tok(128), tok(128), tok(256),
                  pl.BlockSpec((1, G, 256, 256), lambda b, i: (b, NG - 1 - i, 0, 0)), tok(256),
                  _full((4 * L, L)), _full((L, 256)), _full((L, 256)), _full((1, 256))],
        out_specs=tok(RET_W), out_shape=jax.ShapeDtypeStruct((B, S, RET_W), _MXU),
        scratch_shapes=[pltpu.VMEM((256, 256), F32)],
        compiler_params=_cp(("parallel", "arbitrary")),
    )(ret_p, cos, sin, raw, states, d_mix, *consts)


def _gla_masks():
    C = CHUNK
    lk = lax.broadcasted_iota(jnp.int32, (1, 128), 1)
    lv = lax.broadcasted_iota(jnp.int32, (1, 256), 1)
    mk = [(lk // 32) == h for h in range(4)]
    mv = [(lv // 64) == h for h in range(4)]
    vi = lax.broadcasted_iota(jnp.int32, (256, 128), 0)
    ki = lax.broadcasted_iota(jnp.int32, (256, 128), 1)
    bd = (vi // 64) == (ki // 32)
    ri = lax.broadcasted_iota(jnp.int32, (4 * C, C), 0) % C
    cj = lax.broadcasted_iota(jnp.int32, (4 * C, C), 1)
    lower = ri >= cj
    ti = lax.broadcasted_iota(jnp.int32, (C, C), 0)
    tj = lax.broadcasted_iota(jnp.int32, (C, C), 1)
    ltri = jnp.where(ti >= tj, 1.0, 0.0).astype(F32)
    utri = jnp.where(ti <= tj, 1.0, 0.0).astype(F32)
    return mk, mv, bd, lower, ltri, utri


def _log_sigmoid(x):
    return jnp.minimum(x, 0.0) - jnp.log(1.0 + jnp.exp(-jnp.abs(x)))


GLA_G = 8


def _gla_fwd(gla_p, w_g2p, b_g2, gnw):
    B, S, _ = gla_p.shape
    C = CHUNK
    NC = S // C
    G = min(GLA_G, NC)
    NG = NC // G

    def body(p_ref, w_ref, b_ref, gn_ref, out_ref, raw_ref, st_ref, st_sc):
        @pl.when(pl.program_id(1) == 0)
        def _():
            st_sc[...] = jnp.zeros_like(st_sc)

        mk, mv, bd, lower, ltri, _ = _gla_masks()
        cs = range(G)
        rows = [slice(c * C, (c + 1) * C) for c in cs]
        ps = [p_ref[0, rows[c], :].astype(F32) for c in cs]
        pre = [_mm(ps[c][:, 512:640], w_ref[...]) + b_ref[...] for c in cs]
        cum = [_mm_f32(ltri, _log_sigmoid(pre[c]) * (1.0 / GLA_TAU)) for c in cs]
        past, fut, upd, q_pos, a_row = [], [], [], [], []
        for c in cs:
            q = ps[c][:, 0:128]
            k = ps[c][:, 128:256] * GLA_KSCALE
            last = cum[c][C - 1:C, :]
            e_pos = jnp.exp(cum[c])
            e_neg = jnp.exp(-cum[c])
            q_pos.append(q * e_pos)
            a_row.append(jnp.exp(last))
            past.append(_mm_nt(_stack_heads(q_pos[c], mk), k * e_neg))
            fut.append(_mm_nt(_stack_heads(q * e_neg, mk), k * e_pos))
            upd.append(_mm_tn(ps[c][:, 256:512], k * jnp.exp(last - cum[c])))
        o_s = [_mm(jnp.where(lower, past[c], fut[c]), ps[c][:, 256:512]) for c in cs]
        st = st_sc[...]
        inter = []
        for c in cs:
            st_ref[0, c] = st
            inter.append(_mm_nt(q_pos[c], st))
            st = st * a_row[c] + jnp.where(bd, upd[c], 0.0)
        st_sc[...] = st
        for c in cs:
            g = _fold_heads(o_s[c], mv, C) + inter[c]
            raw_ref[0, rows[c], :] = g
            rstd = lax.rsqrt(_head_mean(g * g, mv, 64.0) + EPS)
            out_ref[0, rows[c], :] = (g * rstd * gn_ref[...] * _silu(ps[c][:, 640:896])).astype(_MXU)

    tok = lambda w: pl.BlockSpec((1, G * C, w), lambda b, n: (b, n, 0))
    return pl.pallas_call(
        body, name="gla_fwd", grid=(B, NG),
        in_specs=[tok(GLA_W), _full((128, 128)), _full((1, 128)), _full((1, 256))],
        out_specs=[tok(256), tok(256), pl.BlockSpec((1, G, 256, 128), lambda b, n: (b, n, 0, 0))],
        out_shape=[jax.ShapeDtypeStruct((B, S, 256), _MXU), jax.ShapeDtypeStruct((B, S, 256), F32),
                   jax.ShapeDtypeStruct((B, NC, 256, 128), F32)],
        scratch_shapes=[pltpu.VMEM((256, 128), F32)],
        compiler_params=_cp(("parallel", "arbitrary")),
    )(gla_p, w_g2p, b_g2, gnw)


def _gla_bwd(gla_p, w_g2p, b_g2, gnw, raw, states, d_mix):
    B, S, _ = gla_p.shape
    C = CHUNK
    NC = S // C
    G = min(GLA_G, NC)
    NG = NC // G

    def body(p_ref, w_ref, b_ref, gn_ref, raw_ref, st_ref, dm_ref, dp_ref, dw_ref, db_ref, dgn_ref, dst_sc):
        first = (pl.program_id(0) == 0) & (pl.program_id(1) == 0)

        @pl.when(first)
        def _():
            dw_ref[...] = jnp.zeros_like(dw_ref)
            db_ref[...] = jnp.zeros_like(db_ref)
            dgn_ref[...] = jnp.zeros_like(dgn_ref)

        @pl.when(pl.program_id(1) == 0)
        def _():
            dst_sc[...] = jnp.zeros_like(dst_sc)

        mk, mv, bd, lower, ltri, utri = _gla_masks()
        gn = gn_ref[...]
        cs = range(G)
        rows = [slice(c * C, (c + 1) * C) for c in cs]
        ps = [p_ref[0, rows[c], :].astype(F32) for c in cs]
        vs = [ps[c][:, 256:512] for c in cs]
        pre = [_mm(ps[c][:, 512:640], w_ref[...]) + b_ref[...] for c in cs]
        cum = [_mm_f32(ltri, _log_sigmoid(pre[c]) * (1.0 / GLA_TAU)) for c in cs]
        dg, dz, dgn_acc = [], [], jnp.zeros((1, 256), F32)
        for c in cs:
            g = raw_ref[0, rows[c], :]
            z = ps[c][:, 640:896]
            rstd = lax.rsqrt(_head_mean(g * g, mv, 64.0) + EPS)
            gh = g * rstd
            dm = dm_ref[0, rows[c], :]
            d_gn = dm * _silu(z)
            dz.append(dm * gh * gn * _dsilu(z))
            dgn_acc = dgn_acc + jnp.sum(d_gn * gh, axis=0, keepdims=True)
            d_gh = d_gn * gn
            dg.append(rstd * (d_gh - gh * _head_mean(d_gh * gh, mv, 64.0)))
        do_s = [_stack_heads(dg[c], mv) for c in cs]
        dattn = [_mm_nt(do_s[c], vs[c]) for c in cs]
        ks, e_pos, e_neg, q_pos, q_neg, k_pos, k_neg, qp_s, qn_s, past, fut, a_row, w_dec, kd = ([] for _ in range(14))
        for c in cs:
            q = ps[c][:, 0:128]
            k = ps[c][:, 128:256] * GLA_KSCALE
            last = cum[c][C - 1:C, :]
            ep, en = jnp.exp(cum[c]), jnp.exp(-cum[c])
            ks.append(k), e_pos.append(ep), e_neg.append(en)
            q_pos.append(q * ep), q_neg.append(q * en), k_pos.append(k * ep), k_neg.append(k * en)
            qp_s.append(_stack_heads(q_pos[c], mk)), qn_s.append(_stack_heads(q_neg[c], mk))
            past.append(_mm_nt(qp_s[c], k_neg[c]))
            fut.append(_mm_nt(qn_s[c], k_pos[c]))
            a_row.append(jnp.exp(last))
            w_dec.append(jnp.exp(last - cum[c]))
            kd.append(k * w_dec[c])
        sts = [st_ref[0, c] for c in cs]
        dq_st = [_mm(dg[c], sts[c]) for c in cs]
        dst_in = [_mm_tn(dg[c], q_pos[c]) for c in cs]
        dv, dq_pos, dk_neg, dq_neg, dk_pos = [], [], [], [], []
        for c in cs:
            attn = jnp.where(lower, past[c], fut[c])
            dpast = jnp.where(lower, dattn[c], 0.0)
            dfut = jnp.where(lower, 0.0, dattn[c])
            dv.append(_mm_tn(attn, do_s[c]))
            dq_pos.append(_fold_heads(_mm(dpast, k_neg[c]), mk, C) + dq_st[c])
            dk_neg.append(_mm_tn(dpast, qp_s[c]))
            dq_neg.append(_fold_heads(_mm(dfut, k_pos[c]), mk, C))
            dk_pos.append(_mm_tn(dfut, qn_s[c]))
        dst_next = dst_sc[...]
        d_a, d_kd = [None] * G, [None] * G
        for c in reversed(cs):
            d_a[c] = jnp.sum(dst_next * sts[c], axis=0, keepdims=True)
            gmat = jnp.where(bd, dst_next, 0.0)
            d_kd[c] = _mm(vs[c], gmat)
            dv[c] = dv[c] + _mm_nt(kd[c], gmat)
            dst_next = dst_next * a_row[c] + jnp.where(bd, dst_in[c], 0.0)
        dst_sc[...] = dst_next
        row = lax.broadcasted_iota(jnp.int32, (C, 128), 0)
        d_la, dk, dq = [], [], []
        for c in cs:
            t = d_kd[c] * kd[c]
            dk.append(d_kd[c] * w_dec[c] + dk_neg[c] * e_neg[c] + dk_pos[c] * e_pos[c])
            dq.append(dq_pos[c] * e_pos[c] + dq_neg[c] * e_neg[c])
            d_last = jnp.sum(t, axis=0, keepdims=True) + d_a[c] * a_row[c]
            d_cum = (dq_pos[c] * q_pos[c] - dk_neg[c] * k_neg[c] - dq_neg[c] * q_neg[c] + dk_pos[c] * k_pos[c] - t)
            d_la.append(_mm_f32(utri, d_cum + jnp.where(row == C - 1, d_last, 0.0)))
        d_pre = [d_la[c] * _sig(-pre[c]) * (1.0 / GLA_TAU) for c in cs]
        d_gg = [_mm_nt(d_pre[c], w_ref[...]) for c in cs]
        dw_acc = _mm_tn(ps[0][:, 512:640], d_pre[0])
        db_acc = jnp.sum(d_pre[0], axis=0, keepdims=True)
        for c in cs[1:]:
            dw_acc = dw_acc + _mm_tn(ps[c][:, 512:640], d_pre[c])
            db_acc = db_acc + jnp.sum(d_pre[c], axis=0, keepdims=True)
        for c in cs:
            dp_ref[0, rows[c], :] = jnp.concatenate([dq[c], dk[c] * GLA_KSCALE, dv[c], d_gg[c], dz[c]],
                                                    axis=1).astype(_MXU)
        dw_ref[...] += dw_acc
        db_ref[...] += db_acc
        dgn_ref[...] += dgn_acc

        @pl.when((pl.program_id(0) == B - 1) & (pl.program_id(1) == NG - 1))
        def _():
            s1 = dgn_ref[...]
            s1 = s1 + pltpu.roll(s1, 128, 1)
            dgn_ref[...] = s1 + pltpu.roll(s1, 64, 1)

    tok = lambda w: pl.BlockSpec((1, G * C, w), lambda b, i: (b, NG - 1 - i, 0))
    return pl.pallas_call(
        body, name="gla_bwd", grid=(B, NG),
        in_specs=[tok(GLA_W), _full((128, 128)), _full((1, 128)), _full((1, 256)), tok(256),
                  pl.BlockSpec((1, G, 256, 128), lambda b, i: (b, NG - 1 - i, 0, 0)), tok(256)],
        out_specs=[tok(GLA_W), _full((128, 128)), _full((1, 128)), _full((1, 256))],
        out_shape=[jax.ShapeDtypeStruct((B, S, GLA_W), _MXU), jax.ShapeDtypeStruct((128, 128), F32),
                   jax.ShapeDtypeStruct((1, 128), F32), jax.ShapeDtypeStruct((1, 256), F32)],
        scratch_shapes=[pltpu.VMEM((256, 128), F32)],
        compiler_params=_cp(("arbitrary", "arbitrary")),
    )(gla_p, w_g2p, b_g2, gnw, raw, states, d_mix)


def _rms(x, w):
    rstd = lax.rsqrt(jnp.mean(x * x, axis=-1, keepdims=True) + EPS)
    xh = x * rstd
    return xh, rstd, xh * w


def _rms_bwd(dy, xh, rstd, w):
    dxh = dy * w
    return rstd * (dxh - xh * jnp.mean(dxh * xh, axis=-1, keepdims=True))


MLA_T = 256


def _mla_prep_fwd(mla_p, cos, sin, qnw, kvnw, w_uq, w_ukv, w_ukv_t):
    B, S, _ = mla_p.shape
    tm = min(S, 512)

    t = min(MLA_T, S)
    nt = tm // t

    def body(p_ref, c_ref, s_ref, qn_ref, kn_ref, wq_ref, wkv_ref, wkvt_ref, q_ref, k_ref, v_ref, kt_ref, vt_ref):
        p = p_ref[0].astype(F32)
        cs, sn = c_ref[0], s_ref[0]
        _, _, qn = _rms(p[:, 0:256], qn_ref[...])
        qpre = _mm(qn, wq_ref[...])
        _, _, kvn = _rms(p[:, 256:384], kn_ref[...])
        kv = _mm(kvn, wkv_ref[...])
        kvt = _mm_nt(wkvt_ref[...], kvn)
        kpe = _rope128(p[:, 384:512], cs, sn)
        kpet = kpe.T
        for h in range(8):
            sl = slice(128 * h, 128 * h + 128)
            q_ref[0, :, sl] = _rope128(qpre[:, sl], cs, sn).astype(_MXU)
            k_ref[0, :, sl] = (kv[:, sl] + kpe).astype(_MXU)
            kht = kvt[sl, :] + kpet
            for n in range(nt):
                kt_ref[0, n, sl, :] = kht[:, n * t:(n + 1) * t].astype(_MXU)
        v_ref[0] = kv[:, 1024:1536].astype(_MXU)
        for n in range(nt):
            vt_ref[0, n] = kvt[1024:1536, n * t:(n + 1) * t].astype(_MXU)

    tok = lambda w: pl.BlockSpec((1, tm, w), lambda b, i: (b, i, 0))
    tr = lambda w: pl.BlockSpec((1, nt, w, t), lambda b, i: (b, i, 0, 0))
    return pl.pallas_call(
        body, name="mla_prep_fwd", grid=(B, S // tm),
        in_specs=[tok(512), tok(128), tok(128), _full((1, 256)), _full((1, 128)), _full((256, 1024)),
                  _full((128, 1536)), _full((1536, 128))],
        out_specs=[tok(1024), tok(1024), tok(512), tr(1024), tr(512)],
        out_shape=[jax.ShapeDtypeStruct((B, S, 1024), _MXU), jax.ShapeDtypeStruct((B, S, 1024), _MXU),
                   jax.ShapeDtypeStruct((B, S, 512), _MXU), jax.ShapeDtypeStruct((B, S // t, 1024, t), _MXU),
                   jax.ShapeDtypeStruct((B, S // t, 512, t), _MXU)],
        compiler_params=_cp(("parallel", "parallel")),
    )(mla_p, cos, sin, qnw, kvnw, w_uq, w_ukv, w_ukv_t)


def _chunk_mask_t(t):
    kj = lax.broadcasted_iota(jnp.int32, (t, t), 0) // CHUNK
    qi = lax.broadcasted_iota(jnp.int32, (t, t), 1) // CHUNK
    return kj <= qi


MLA_HG = 8
MLA_HG_FWD = 8
LOG2E = 1.4426950408889634
MLA_C2 = MLA_SCALE * LOG2E


def _mla_attn_fwd(q, k, vt):
    B, S, _ = q.shape
    t = min(MLA_T, S)
    nq = S // t
    HG = MLA_HG_FWD
    NP = HG // 2

    def body(q_ref, k_ref, vt_ref, o_ref, lse_ref, sa, sb, m_sc, l_sc, acc_sc):
        i = pl.program_id(2)
        row = lax.broadcasted_iota(jnp.int32, (128, 1), 0)
        low = row < 64
        mask = _chunk_mask_t(t)
        m_sc[...] = jnp.full(m_sc.shape, -jnp.inf, F32)
        l_sc[...] = jnp.zeros_like(l_sc)
        acc_sc[...] = jnp.zeros_like(acc_sc)

        ones = jnp.ones((8, t), _MXU)

        def scores(j, buf):
            kb = k_ref[0, pl.ds(pl.multiple_of(j * t, t), t), :]
            for h in range(HG):
                cols = slice(128 * h, 128 * h + 128)
                buf[h] = (_mm_nt(kb[:, cols], q_ref[0, :, cols]) * MLA_C2).astype(_MXU)

        def absorb(j, buf, masked):
            vtb = vt_ref[0, j]
            for pr in range(NP):
                alphas, pvs = [], []
                for hh in range(2):
                    h = 2 * pr + hh
                    s = buf[h]
                    if masked:
                        s = jnp.where(mask, s, jnp.full_like(s, -jnp.inf))
                    m_old = m_sc[h]
                    m_new = jnp.maximum(m_old, jnp.max(s, axis=0, keepdims=True).astype(F32))
                    alpha = jnp.exp2(m_old - m_new)
                    p = jnp.exp2(s - m_new.astype(_MXU))
                    l_sc[h] = alpha * l_sc[h] + _mm(ones, p)[0:1, :]
                    m_sc[h] = m_new
                    vth = vtb[128 * pr:128 * pr + 128, :]
                    vth = jnp.where(low if hh == 0 else ~low, vth, jnp.zeros_like(vth))
                    pvs.append(_mm(vth, p))
                    alphas.append(alpha)
                acc_sc[pr] = acc_sc[pr] * jnp.where(low, alphas[0], alphas[1]) + pvs[0] + pvs[1]

        scores(0, sb)

        def pair(jj, carry):
            j0 = 2 * jj
            scores(j0 + 1, sa)
            absorb(j0, sb, False)
            scores(j0 + 2, sb)
            absorb(j0 + 1, sa, False)
            return carry

        lax.fori_loop(0, i // 2, pair, 0)

        @pl.when(i % 2 == 1)
        def _():
            scores(i, sa)
            absorb(i - 1, sb, False)
            absorb(i, sa, True)

        @pl.when(i % 2 == 0)
        def _():
            absorb(i, sb, True)

        for pr in range(NP):
            l_e, l_o = l_sc[2 * pr], l_sc[2 * pr + 1]
            o_ref[0, :, 128 * pr:128 * pr + 128] = (acc_sc[pr] / jnp.where(low, l_e, l_o)).T
            lse_ref[0, pr, 0, 0:1, :] = m_sc[2 * pr] + jnp.log(l_e) * LOG2E
            lse_ref[0, pr, 0, 1:2, :] = m_sc[2 * pr + 1] + jnp.log(l_o) * LOG2E

    return pl.pallas_call(
        body, name="mla_attn_fwd", grid=(B, 8 // HG, nq),
        in_specs=[pl.BlockSpec((1, t, 128 * HG), lambda b, g, i: (b, i, g)),
                  pl.BlockSpec((1, S, 128 * HG), lambda b, g, i: (b, 0, g)),
                  pl.BlockSpec((1, nq, 64 * HG, t), lambda b, g, i: (b, 0, g, 0))],
        out_specs=[pl.BlockSpec((1, t, 64 * HG), lambda b, g, i: (b, i, g)),
                   pl.BlockSpec((1, NP, 1, 2, t), lambda b, g, i: (b, g, i, 0, 0))],
        out_shape=[jax.ShapeDtypeStruct((B, S, 512), F32), jax.ShapeDtypeStruct((B, 4, nq, 2, t), F32)],
        scratch_shapes=[pltpu.VMEM((HG, t, t), _MXU), pltpu.VMEM((HG, t, t), _MXU), pltpu.VMEM((HG, 1, t), F32),
                        pltpu.VMEM((HG, 1, t), F32), pltpu.VMEM((NP, 128, t), F32)],
        compiler_params=_cp(("parallel", "parallel", "arbitrary")),
    )(q, k, vt)


def _mla_attn_bwd(q, k, v, kt, do, lse, dl):
    B, S, _ = q.shape
    t = min(MLA_T, S)
    nk = S // t

    HG = MLA_HG
    NP = HG // 2

    def body(q_ref, k_ref, v_ref, kt_ref, do_ref, lse_ref, dl_ref, dq_ref, dk_ref, dv_ref,
             sa, da, sb, db, dqt_sc, dk_sc, dv_sc):
        j = pl.program_id(2)

        @pl.when(j == 0)
        def _():
            dqt_sc[...] = jnp.zeros_like(dqt_sc)

        dk_sc[...] = jnp.zeros_like(dk_sc)
        dv_sc[...] = jnp.zeros_like(dv_sc)
        lane = lax.broadcasted_iota(jnp.int32, (1, 128), 1)
        low = lane < 64
        mask = _chunk_mask_t(t)

        def half(x, hh):
            return jnp.where(low if hh == 0 else ~low, x, jnp.zeros_like(x))

        def prepare(i, sbuf, dbuf):
            rows = pl.ds(pl.multiple_of(i * t, t), t)
            for h in range(HG):
                cols = slice(128 * h, 128 * h + 128)
                pc = slice(128 * (h // 2), 128 * (h // 2) + 128)
                sbuf[h] = _mm_nt(k_ref[0, :, cols], q_ref[0, rows, cols]) * MLA_C2
                dbuf[h] = _mm_nt(half(v_ref[0, :, pc], h % 2), do_ref[0, rows, pc])

        def absorb(i, sbuf, dbuf, masked):
            rows = pl.ds(pl.multiple_of(i * t, t), t)
            for h in range(HG):
                pr, hh = h // 2, h % 2
                cols = slice(128 * h, 128 * h + 128)
                pc = slice(128 * pr, 128 * pr + 128)
                p = jnp.exp2(sbuf[h] - lse_ref[0, pr, i][hh:hh + 1, :])
                if masked:
                    p = jnp.where(mask, p, 0.0)
                dv_sc[pr] += _mm(p, half(do_ref[0, rows, pc], hh))
                ds = p * (dbuf[h] - dl_ref[0, pr, i][hh:hh + 1, :])
                dqt_sc[i, cols, :] += _mm(kt_ref[0, 0, cols, :], ds)
                dk_sc[h] += _mm(ds, q_ref[0, rows, cols])

        n = nk - 1 - j
        prepare(jnp.minimum(j + 1, nk - 1), sb, db)

        def pair(jj, carry):
            i0 = j + 1 + 2 * jj
            prepare(i0 + 1, sa, da)
            absorb(i0, sb, db, False)
            prepare(jnp.where(i0 + 2 <= nk - 1, i0 + 2, j), sb, db)
            absorb(i0 + 1, sa, da, False)
            return carry

        lax.fori_loop(0, n // 2, pair, 0)

        @pl.when(n % 2 == 1)
        def _():
            prepare(j, sa, da)
            absorb(nk - 1, sb, db, False)
            absorb(j, sa, da, True)

        @pl.when(n % 2 == 0)
        def _():
            absorb(j, sb, db, True)

        for h in range(HG):
            dk_ref[0, :, 128 * h:128 * h + 128] = (dk_sc[h] * MLA_SCALE).astype(_MXU)
        for pr in range(NP):
            dv_ref[0, :, 128 * pr:128 * pr + 128] = dv_sc[pr].astype(_MXU)

        @pl.when(j == nk - 1)
        def _():
            for i in range(nk):
                dq_ref[0, i * t:(i + 1) * t, :] = (dqt_sc[i].T * MLA_SCALE).astype(_MXU)

    seq = lambda w: pl.BlockSpec((1, S, w), lambda b, g, j: (b, 0, g))
    blk = lambda w: pl.BlockSpec((1, t, w), lambda b, g, j: (b, j, g))
    stat = pl.BlockSpec((1, NP, nk, 2, t), lambda b, g, j: (b, g, 0, 0, 0))
    return pl.pallas_call(
        body, name="mla_attn_bwd", grid=(B, 8 // HG, nk),
        in_specs=[seq(128 * HG), blk(128 * HG), blk(64 * HG),
                  pl.BlockSpec((1, 1, 128 * HG, t), lambda b, g, j: (b, j, g, 0)), seq(64 * HG), stat, stat],
        out_specs=[seq(128 * HG), blk(128 * HG), blk(64 * HG)],
        out_shape=[jax.ShapeDtypeStruct((B, S, 1024), _MXU), jax.ShapeDtypeStruct((B, S, 1024), _MXU),
                   jax.ShapeDtypeStruct((B, S, 512), _MXU)],
        scratch_shapes=[pltpu.VMEM((HG, t, t), F32), pltpu.VMEM((HG, t, t), F32), pltpu.VMEM((HG, t, t), F32),
                        pltpu.VMEM((HG, t, t), F32), pltpu.VMEM((nk, 128 * HG, t), F32),
                        pltpu.VMEM((HG, t, 128), F32), pltpu.VMEM((NP, t, 128), F32)],
        compiler_params=_cp(("parallel", "parallel", "arbitrary"), 56),
    )(q, k, v, kt, do, lse, dl)


def _mla_prep_bwd(mla_p, cos, sin, qnw, kvnw, w_uq, w_ukv, dq, dk, dv):
    B, S, _ = mla_p.shape
    tm = min(S, 512)

    def body(p_ref, c_ref, s_ref, qn_ref, kn_ref, wq_ref, wkv_ref, dq_ref, dk_ref, dv_ref,
             dp_ref, dwq_ref, dwkv_ref, dqn_ref, dkn_ref):
        first = (pl.program_id(0) == 0) & (pl.program_id(1) == 0)

        @pl.when(first)
        def _():
            dwq_ref[...] = jnp.zeros_like(dwq_ref)
            dwkv_ref[...] = jnp.zeros_like(dwkv_ref)
            dqn_ref[...] = jnp.zeros_like(dqn_ref)
            dkn_ref[...] = jnp.zeros_like(dkn_ref)

        p = p_ref[0].astype(F32)
        cs, sn = c_ref[0], s_ref[0]
        lane = lax.broadcasted_iota(jnp.int32, (1, 128), 1)
        pe = (lane >= 64) & (lane < 96)
        qh, q_rstd, qn = _rms(p[:, 0:256], qn_ref[...])
        kvh, kv_rstd, kvn = _rms(p[:, 256:384], kn_ref[...])
        dqv = dq_ref[0].astype(F32)
        dkv = dk_ref[0].astype(F32)
        dqpre = jnp.concatenate(
            [_rope128_t(dqv[:, 128 * h:128 * h + 128], cs, sn) for h in range(8)], axis=1)
        dkpe = jnp.zeros((tm, 128), F32)
        for h in range(8):
            dkpe = dkpe + jnp.where(pe, dkv[:, 128 * h:128 * h + 128], 0.0)
        dkr = _rope128_t(dkpe, cs, sn)
        dkv_all = jnp.concatenate([dkv, dv_ref[0].astype(F32)], axis=1)
        d_qn = _mm_nt(dqpre, wq_ref[...])
        d_kvn = _mm_nt(dkv_all, wkv_ref[...])
        dwq_ref[...] += _mm_tn(qn, dqpre)
        dwkv_ref[...] += _mm_tn(kvn, dkv_all)
        dqn_ref[...] += jnp.sum(d_qn * qh, axis=0, keepdims=True)
        dkn_ref[...] += jnp.sum(d_kvn * kvh, axis=0, keepdims=True)
        dp_ref[0] = jnp.concatenate([_rms_bwd(d_qn, qh, q_rstd, qn_ref[...]),
                                     _rms_bwd(d_kvn, kvh, kv_rstd, kn_ref[...]), dkr], axis=1).astype(_MXU)

    tok = lambda w: pl.BlockSpec((1, tm, w), lambda b, i: (b, i, 0))
    return pl.pallas_call(
        body, name="mla_prep_bwd", grid=(B, S // tm),
        in_specs=[tok(512), tok(128), tok(128), _full((1, 256)), _full((1, 128)), _full((256, 1024)),
                  _full((128, 1536)), tok(1024), tok(1024), tok(512)],
        out_specs=[tok(512), _full((256, 1024)), _full((128, 1536)), _full((1, 256)), _full((1, 128))],
        out_shape=[jax.ShapeDtypeStruct((B, S, 512), _MXU), jax.ShapeDtypeStruct((256, 1024), F32),
                   jax.ShapeDtypeStruct((128, 1536), F32), jax.ShapeDtypeStruct((1, 256), F32),
                   jax.ShapeDtypeStruct((1, 128), F32)],
        compiler_params=_cp(("arbitrary", "arbitrary")),
    )(mla_p, cos, sin, qnw, kvnw, w_uq, w_ukv, dq, dk, dv)


def _out_fwd(x, gate, r_g, o_mla, mla_p, g_g, w_out):
    B, S, D = x.shape
    tm = min(S, 512)

    def body(x_ref, g_ref, r_ref, o_ref, z_ref, gg_ref, w_ref, xn_ref, y_ref):
        mm = (o_ref[0] * _silu(z_ref[0].astype(F32))).astype(_MXU)
        y = (jnp.dot(r_ref[0], w_ref[0:256, :], preferred_element_type=F32)
             + jnp.dot(mm, w_ref[256:768, :], preferred_element_type=F32)
             + jnp.dot(gg_ref[0], w_ref[768:1024, :], preferred_element_type=F32))
        y_ref[0] = y.astype(_MXU)
        xn_ref[0] = x_ref[0] + g_ref[0] * y

    tok = lambda w, c=0: pl.BlockSpec((1, tm, w), lambda b, i: (b, i, c))
    return pl.pallas_call(
        body, name="out_fwd", grid=(B, S // tm),
        in_specs=[tok(D), pl.BlockSpec((1, 1, D), lambda b, i: (b, 0, 0)), tok(256), tok(512), tok(512, 1),
                  tok(256), _full((D, D))],
        out_specs=[tok(D), tok(D)],
        out_shape=[jax.ShapeDtypeStruct((B, S, D), F32), jax.ShapeDtypeStruct((B, S, D), _MXU)],
        compiler_params=_cp(("parallel", "parallel")),
    )(x, gate, r_g, o_mla, mla_p, g_g, w_out)


def _out_bwd(dx, y, gate, r_g, g_g, w_out, o_mla, mla_p):
    B, S, D = dx.shape
    tm = min(S, 512)
    t = min(MLA_T, S)
    nt = tm // t

    def body(dx_ref, y_ref, g_ref, r_ref, gg_ref, w_ref, o_ref, z_ref,
             dr_ref, do_ref, dz_ref, dl_ref, dg_ref, dw_ref, dgate_ref, acc):
        first = (pl.program_id(0) == 0) & (pl.program_id(1) == 0)

        @pl.when(first)
        def _():
            acc[...] = jnp.zeros_like(acc)

        @pl.when(pl.program_id(1) == 0)
        def _():
            dgate_ref[...] = jnp.zeros_like(dgate_ref)

        dxv = dx_ref[0]
        dgate_ref[0] += jnp.sum(dxv * y_ref[0].astype(F32), axis=0, keepdims=True)
        dy = (dxv * g_ref[0]).astype(_MXU)
        dr_ref[0] = _mm_nt(dy, w_ref[0:256, :])
        dg_ref[0] = _mm_nt(dy, w_ref[768:1024, :])
        ov, z = o_ref[0], z_ref[0].astype(F32)
        acc[0:256, :] += _mm_tn(r_ref[0], dy)
        acc[256:768, :] += _mm_tn((ov * _silu(z)).astype(_MXU), dy)
        acc[768:1024, :] += _mm_tn(gg_ref[0], dy)

        @pl.when((pl.program_id(0) == B - 1) & (pl.program_id(1) == S // tm - 1))
        def _():
            dw_ref[...] = acc[...].astype(_MXU)

        dm = _mm_nt(dy, w_ref[256:768, :])
        do = dm * _silu(z)
        dz_ref[0] = (dm * ov * _dsilu(z)).astype(_MXU)
        do_ref[0] = do.astype(_MXU)
        prod = do * ov
        for pr in range(4):
            pt = prod[:, 128 * pr:128 * pr + 128].T
            se = jnp.sum(pt[0:64], axis=0, keepdims=True)
            so = jnp.sum(pt[64:128], axis=0, keepdims=True)
            for n in range(nt):
                dl_ref[0, pr, n, 0:1, :] = se[:, n * t:(n + 1) * t]
                dl_ref[0, pr, n, 1:2, :] = so[:, n * t:(n + 1) * t]

    tok = lambda w, c=0: pl.BlockSpec((1, tm, w), lambda b, i: (b, i, c))
    per_seq = pl.BlockSpec((1, 1, D), lambda b, i: (b, 0, 0))
    return pl.pallas_call(
        body, name="out_bwd", grid=(B, S // tm),
        in_specs=[tok(D), tok(D), per_seq, tok(256), tok(256), _full((D, D)), tok(512), tok(512, 1)],
        out_specs=[tok(256), tok(512), tok(512), pl.BlockSpec((1, 4, nt, 2, t), lambda b, i: (b, 0, i, 0, 0)),
                   tok(256), _full((D, D)), per_seq],
        out_shape=[jax.ShapeDtypeStruct((B, S, 256), F32), jax.ShapeDtypeStruct((B, S, 512), _MXU),
                   jax.ShapeDtypeStruct((B, S, 512), _MXU), jax.ShapeDtypeStruct((B, 4, S // t, 2, t), F32),
                   jax.ShapeDtypeStruct((B, S, 256), F32), jax.ShapeDtypeStruct((D, D), _MXU),
                   jax.ShapeDtypeStruct((B, 1, D), F32)],
        scratch_shapes=[pltpu.VMEM((D, D), F32)],
        compiler_params=_cp(("arbitrary", "arbitrary")),
    )(dx, y, gate, r_g, g_g, w_out, o_mla, mla_p)


def _proj_bwd_x(x, shift, scale, nw, w_arr, d_ret, d_mla, d_mz, d_gla, dx_out):
    B, S, D = x.shape
    tm = min(S, 512)

    def body(x_ref, sc_ref, nw_ref, w_ref, dr_ref, dm_ref, dz_ref, dg_ref, dxo_ref,
             dx_ref, dsh_ref, dsc_ref, dnw_ref):
        first = (pl.program_id(0) == 0) & (pl.program_id(1) == 0)

        @pl.when(first)
        def _():
            dnw_ref[...] = jnp.zeros_like(dnw_ref)

        @pl.when(pl.program_id(1) == 0)
        def _():
            dsh_ref[...] = jnp.zeros_like(dsh_ref)
            dsc_ref[...] = jnp.zeros_like(dsc_ref)

        dp = jnp.concatenate([dr_ref[0], dm_ref[0], dz_ref[0], dg_ref[0]], axis=1)
        dh = lax.dot_general(dp, w_ref[...], (((1,), (1,)), ((), ())), preferred_element_type=F32)
        xv = x_ref[0]
        rstd = lax.rsqrt(jnp.mean(xv * xv, axis=-1, keepdims=True) + EPS)
        xh = xv * rstd
        nwv = nw_ref[...]
        mod = 1.0 + sc_ref[0]
        dsh_ref[0] += jnp.sum(dh, axis=0, keepdims=True)
        dsc_ref[0] += jnp.sum(dh * xh * nwv, axis=0, keepdims=True)
        dnw_ref[...] += jnp.sum(dh * xh * mod, axis=0, keepdims=True)
        dxh = dh * nwv * mod
        dx_ref[0] = dxo_ref[0] + rstd * (dxh - xh * jnp.mean(dxh * xh, axis=-1, keepdims=True))

    tok = lambda w: pl.BlockSpec((1, tm, w), lambda b, i: (b, i, 0))
    per_seq = pl.BlockSpec((1, 1, D), lambda b, i: (b, 0, 0))
    return pl.pallas_call(
        body, name="proj_bwd_x", grid=(B, S // tm),
        in_specs=[tok(D), per_seq, _full((1, D)), _full((D, ARR_W)), tok(RET_W), tok(512), tok(512),
                  tok(GLA_W), tok(D)],
        out_specs=[tok(D), per_seq, per_seq, _full((1, D))],
        out_shape=[jax.ShapeDtypeStruct((B, S, D), F32), jax.ShapeDtypeStruct((B, 1, D), F32),
                   jax.ShapeDtypeStruct((B, 1, D), F32), jax.ShapeDtypeStruct((1, D), F32)],
        compiler_params=_cp(("arbitrary", "arbitrary")),
    )(x, scale, nw, w_arr, d_ret, d_mla, d_mz, d_gla, dx_out)


def _proj_bwd_w(h, d_ret, d_mla, d_mz, d_gla):
    B, S, D = h.shape
    tm = min(S, 512)

    def body(h_ref, dr_ref, dm_ref, dz_ref, dg_ref, dw_ref, acc):
        first = (pl.program_id(0) == 0) & (pl.program_id(1) == 0)

        @pl.when(first)
        def _():
            acc[...] = jnp.zeros_like(acc)

        hv = h_ref[0]
        tn = lambda d_ref: lax.dot_general(hv, d_ref[0], (((0,), (0,)), ((), ())), preferred_element_type=F32)
        acc[:, 0:RET_W] += tn(dr_ref)
        acc[:, RET_W:RET_W + 512] += tn(dm_ref)
        acc[:, RET_W + 512:RET_W + MLA_W] += tn(dz_ref)
        acc[:, RET_W + MLA_W:ARR_W] += tn(dg_ref)

        @pl.when((pl.program_id(0) == B - 1) & (pl.program_id(1) == S // tm - 1))
        def _():
            dw_ref[...] = acc[...].astype(_MXU)

    tok = lambda w: pl.BlockSpec((1, tm, w), lambda b, i: (b, i, 0))
    return pl.pallas_call(
        body, name="proj_bwd_w", grid=(B, S // tm),
        in_specs=[tok(D), tok(RET_W), tok(512), tok(512), tok(GLA_W)],
        out_specs=_full((D, ARR_W)), out_shape=jax.ShapeDtypeStruct((D, ARR_W), _MXU),
        scratch_shapes=[pltpu.VMEM((D, ARR_W), F32)],
        compiler_params=_cp(("arbitrary", "arbitrary"), 56),
    )(h, d_ret, d_mla, d_mz, d_gla)


def _out_fwd_loss(x, gate, r_g, o_mla, mla_p, g_g, w_out, fw, target):
    B, S, D = x.shape
    tm = min(S, 512)

    def body(x_ref, g_ref, r_ref, o_ref, z_ref, gg_ref, w_ref, fw_ref, t_ref, dx_ref, y_ref, loss_ref, dfw_ref):
        first = (pl.program_id(0) == 0) & (pl.program_id(1) == 0)

        @pl.when(first)
        def _():
            loss_ref[...] = jnp.zeros_like(loss_ref)
            dfw_ref[...] = jnp.zeros_like(dfw_ref)

        mm = (o_ref[0] * _silu(z_ref[0].astype(F32))).astype(_MXU)
        y = (jnp.dot(r_ref[0], w_ref[0:256, :], preferred_element_type=F32)
             + jnp.dot(mm, w_ref[256:768, :], preferred_element_type=F32)
             + jnp.dot(gg_ref[0], w_ref[768:1024, :], preferred_element_type=F32))
        y_ref[0] = y.astype(_MXU)
        xv = x_ref[0] + g_ref[0] * y
        fwv = fw_ref[...]
        rstd = lax.rsqrt(jnp.mean(xv * xv, axis=-1, keepdims=True) + EPS)
        xh = xv * rstd
        err = xh * fwv - t_ref[0]
        loss_ref[...] += 0.5 * jnp.sum(jnp.mean(err * err, axis=-1, keepdims=True), axis=0, keepdims=True)
        dy = err * (1.0 / D)
        dfw_ref[...] += jnp.sum(dy * xh, axis=0, keepdims=True)
        dxh = dy * fwv
        dx_ref[0] = rstd * (dxh - xh * jnp.mean(dxh * xh, axis=-1, keepdims=True))

    tok = lambda w, c=0: pl.BlockSpec((1, tm, w), lambda b, i: (b, i, c))
    return pl.pallas_call(
        body, name="out_fwd_loss", grid=(B, S // tm),
        in_specs=[tok(D), pl.BlockSpec((1, 1, D), lambda b, i: (b, 0, 0)), tok(256), tok(512), tok(512, 1),
                  tok(256), _full((D, D)), _full((1, D)), tok(D)],
        out_specs=[tok(D), tok(D), _full((1, 1)), _full((1, D))],
        out_shape=[jax.ShapeDtypeStruct((B, S, D), F32), jax.ShapeDtypeStruct((B, S, D), _MXU),
                   jax.ShapeDtypeStruct((1, 1), F32), jax.ShapeDtypeStruct((1, D), F32)],
        compiler_params=_cp(("arbitrary", "arbitrary")),
    )(x, gate, r_g, o_mla, mla_p, g_g, w_out, fw, target)


def _local_step(x, pos3, mod, loss_target, small, w_in_a, w_uq_a, w_ukv_a, w_out_b):
    B, S, D = x.shape
    tabs = _rope_tables(pos3)
    saved = []
    for l in range(DEPTH):
        last = (small["final_norm"].reshape(1, D), loss_target) if l == DEPTH - 1 else None
        x, s = _layer_fwd(x, tabs, mod[l], {n: a[l] for n, a in small.items() if n != "final_norm"},
                          w_in_a[l], w_uq_a[l], w_ukv_a[l], w_ukv_a[l].T, w_out_b[l], loss_head=last)
        saved.append(s)
    dx, loss, d_fw = x
    grads = dict(final_norm=d_fw.reshape(D))
    per_layer = [None] * DEPTH
    for l in reversed(range(DEPTH)):
        dx, per_layer[l] = _layer_bwd(dx, saved[l], tabs)
    for name in per_layer[0]:
        grads[name] = jnp.stack([per_layer[l][name] for l in range(DEPTH)])
    return loss, dx, grads


def _layer_fwd(x, tabs, mod_l, small_l, w_in_a, w_uq_a=None, w_ukv_a=None, w_ukv_t=None, w_out_b=None, late_weights=None,
               loss_head=None):
    B, S, D = x.shape
    cr, sr, cm, sm = tabs
    shift = mod_l[:, 0:D].reshape(B, 1, D)
    scale = mod_l[:, D:2 * D].reshape(B, 1, D)
    gate = mod_l[:, 2 * D:3 * D].reshape(B, 1, D)
    nw = small_l["norm_w"].reshape(1, D)
    qnw = small_l["mla_q_norm"].reshape(1, 256)
    kvnw = small_l["mla_kv_norm"].reshape(1, 128)
    w_g2p = jnp.pad(small_l["gla_w_g2"], ((0, 112), (0, 0)))
    b_g2 = small_l["gla_b_g2"].reshape(1, 128)
    gnw = jnp.tile(small_l["gla_norm"], 4).reshape(1, 256)
    ret_p, mla_p, gla_p, h = _proj_fwd(x, shift, scale, nw, w_in_a)
    r_g, r_raw, r_st = _ret_fwd(ret_p, cr, sr)
    if late_weights is not None:
        w_uq_a, w_ukv_a, w_ukv_t, w_out_b = late_weights(r_raw)
    q, k, v, kt, vt = _mla_prep_fwd(mla_p, cm, sm, qnw, kvnw, w_uq_a, w_ukv_a, w_ukv_t)
    o_mla, lse = _mla_attn_fwd(q, k, vt)
    g_g, g_raw, g_st = _gla_fwd(gla_p, w_g2p, b_g2, gnw)
    if loss_head is None:
        x_new, y = _out_fwd(x, gate, r_g, o_mla, mla_p, g_g, w_out_b)
    else:
        dx, y, loss, d_fw = _out_fwd_loss(x, gate, r_g, o_mla, mla_p, g_g, w_out_b, *loss_head)
        x_new = (dx, loss, d_fw)
    saved = dict(x=x, shift=shift, scale=scale, gate=gate, nw=nw, qnw=qnw, kvnw=kvnw, w_g2p=w_g2p, b_g2=b_g2,
                 gnw=gnw, ret_p=ret_p, mla_p=mla_p, gla_p=gla_p, h=h, r_g=r_g, r_raw=r_raw, r_st=r_st, q=q, k=k,
                 v=v, kt=kt, o_mla=o_mla, lse=lse, g_g=g_g, g_raw=g_raw, g_st=g_st, y=y,
                 w_in_a=w_in_a, w_uq_a=w_uq_a, w_ukv_a=w_ukv_a, w_out_b=w_out_b)
    return x_new, saved


def _layer_bwd(dx, s, tabs, early_grads=None, early_w_in=None):
    B, S, D = dx.shape
    cr, sr, cm, sm = tabs
    d_r, do, d_mz, dl, d_g, dw_out, d_gate = _out_bwd(dx, s["y"], s["gate"], s["r_g"], s["g_g"], s["w_out_b"],
                                                      s["o_mla"], s["mla_p"])
    d_ret = _ret_bwd(s["ret_p"], cr, sr, s["r_raw"], s["r_st"], d_r)
    dq, dk, dv = _mla_attn_bwd(s["q"], s["k"], s["v"], s["kt"], do, s["lse"], dl)
    d_mla, dw_uq, dw_ukv, d_qnw, d_kvnw = _mla_prep_bwd(
        s["mla_p"], cm, sm, s["qnw"], s["kvnw"], s["w_uq_a"], s["w_ukv_a"], dq, dk, dv)
    gnw = s["gnw"] if early_grads is None else s["gnw"] + early_grads(dw_out, dw_uq, dw_ukv)
    d_gla, dw_g2p, db_g2, d_gnw = _gla_bwd(s["gla_p"], s["w_g2p"], s["b_g2"], gnw, s["g_raw"], s["g_st"], d_g)
    dw_in = _proj_bwd_w(s["h"], d_ret, d_mla, d_mz, d_gla)
    nw = s["nw"] if early_w_in is None else s["nw"] + early_w_in(dw_in)
    dx, d_shift, d_scale, d_nw = _proj_bwd_x(s["x"], s["shift"], s["scale"], nw, s["w_in_a"],
                                             d_ret, d_mla, d_mz, d_gla, dx)
    grads = dict(
        d_mod=jnp.concatenate([d_shift, d_scale, d_gate], axis=2).reshape(B, 3 * D),
        norm_w=d_nw.reshape(D), mla_q_norm=d_qnw.reshape(256), mla_kv_norm=d_kvnw.reshape(128),
        gla_w_g2=dw_g2p[0:16], gla_b_g2=db_g2.reshape(128), gla_norm256=d_gnw.reshape(256),
        w_in_a=dw_in, w_uq_a=dw_uq, w_ukv_a=dw_ukv, w_out=dw_out)
    return dx, grads


def _exchange(arrs, gather, name):
    n = len(arrs)
    out_shape = [jax.ShapeDtypeStruct(((N_DEV,) + a.shape) if g else a.shape, a.dtype)
                 for a, g in zip(arrs, gather)]

    def body(*refs):
        ins, outs = refs[:n], refs[n:2 * n]
        send_sems, recv_sems, local_sems = refs[2 * n:]
        ix, iy, ic = lax.axis_index("x"), lax.axis_index("y"), lax.axis_index("c")
        me = 4 * ix + 2 * iy + ic
        copies = []
        for a in range(n):
            mine = ins[a] if gather[a] else ins[a].at[me]
            loc = pltpu.make_async_copy(mine, outs[a].at[me], local_sems.at[a])
            loc.start()
            copies.append(loc)
            for d in range(1, N_DEV):
                px = 1 - ix if d & 4 else ix
                py = 1 - iy if d & 2 else iy
                pc = 1 - ic if d & 1 else ic
                src = ins[a] if gather[a] else ins[a].at[4 * px + 2 * py + pc]
                cp = pltpu.make_async_remote_copy(
                    src_ref=src, dst_ref=outs[a].at[me], send_sem=send_sems.at[a, d - 1],
                    recv_sem=recv_sems.at[a, d - 1], device_id=(px, py, pc), device_id_type=pl.DeviceIdType.MESH)
                cp.start()
                copies.append(cp)
        for cp in copies:
            cp.wait()

    any_spec = pl.BlockSpec(memory_space=pl.ANY)
    outs = pl.pallas_call(
        body, name=name, in_specs=[any_spec] * n, out_specs=[any_spec] * n, out_shape=out_shape,
        scratch_shapes=[pltpu.SemaphoreType.DMA((n, N_DEV - 1)), pltpu.SemaphoreType.DMA((n, N_DEV - 1)),
                        pltpu.SemaphoreType.DMA((n,))],
    )(*arrs)
    return list(outs)


def _peers(ix, iy, ic):
    out = []
    for d in range(1, N_DEV):
        px = 1 - ix if d & 4 else ix
        py = 1 - iy if d & 2 else iy
        pc = 1 - ic if d & 1 else ic
        out.append((d - 1, (px, py, pc), 4 * px + 2 * py + pc))
    return out


def _exchange_start(arrs, gather, name, after=None):
    n = len(arrs)
    lands = [lax.empty(((N_DEV,) + a.shape) if g else a.shape, a.dtype) for a, g in zip(arrs, gather)]
    extra = [] if after is None else [after]

    def body(*refs):
        ins, land_refs = refs[:n], refs[n:2 * n]
        send_sems, recv_sems = refs[2 * n + len(extra)], refs[2 * n + len(extra) + 1]
        token = refs[-1]
        ix, iy, ic = lax.axis_index("x"), lax.axis_index("y"), lax.axis_index("c")
        me = 4 * ix + 2 * iy + ic
        for a in range(n):
            for k, peer, peer_idx in _peers(ix, iy, ic):
                pltpu.make_async_remote_copy(
                    src_ref=ins[a] if gather[a] else ins[a].at[peer_idx], dst_ref=land_refs[a].at[me],
                    send_sem=send_sems.at[7 * a + k], recv_sem=recv_sems.at[7 * a + k], device_id=peer,
                    device_id_type=pl.DeviceIdType.MESH).start()
        token[...] = jnp.zeros_like(token)

    hbm = pl.BlockSpec(memory_space=pltpu.HBM)
    sem = pl.BlockSpec(memory_space=pltpu.SEMAPHORE)
    held = [pltpu.with_memory_space_constraint(a, pltpu.HBM) for a in list(arrs) + lands]
    outs = pl.pallas_call(
        body, name=name,
        out_shape=(pltpu.SemaphoreType.DMA((7 * n,)), pltpu.SemaphoreType.DMA((7 * n,)),
                   *[pltpu.HBM(a.shape, a.dtype) for a in held], jax.ShapeDtypeStruct((8, 128), F32)),
        in_specs=[hbm] * (2 * n) + [pl.BlockSpec(memory_space=pl.ANY)] * len(extra),
        out_specs=(sem, sem, *[hbm] * (2 * n), pl.BlockSpec(memory_space=pltpu.VMEM)),
        input_output_aliases={a: 2 + a for a in range(2 * n)},
        compiler_params=pltpu.CompilerParams(has_side_effects=pltpu.SideEffectType.DATAFLOW_SIDE_EFFECTING),
    )(*held, *extra)
    return dict(send=outs[0], recv=outs[1], srcs=list(outs[2:2 + n]), lands=list(outs[2 + n:2 + 2 * n]),
                token=outs[-1], gather=list(gather))


def _exchange_wait(flight, after, me, name):
    n = len(flight["srcs"])
    gather = flight["gather"]

    def body(*refs):
        srcs, land_refs = refs[:n], refs[n:2 * n]
        send_sems, recv_sems = refs[2 * n], refs[2 * n + 1]
        ix, iy, ic = lax.axis_index("x"), lax.axis_index("y"), lax.axis_index("c")
        mine = 4 * ix + 2 * iy + ic
        for a in range(n):
            for k, peer, peer_idx in _peers(ix, iy, ic):
                cp = pltpu.make_async_remote_copy(
                    src_ref=srcs[a] if gather[a] else srcs[a].at[peer_idx], dst_ref=land_refs[a].at[mine],
                    send_sem=send_sems.at[7 * a + k], recv_sem=recv_sems.at[7 * a + k], device_id=peer,
                    device_id_type=pl.DeviceIdType.MESH)
                cp.wait_send()
                cp.wait_recv()

    hbm = pl.BlockSpec(memory_space=pltpu.HBM)
    sem = pl.BlockSpec(memory_space=pltpu.SEMAPHORE)
    held = flight["srcs"] + flight["lands"]
    outs = pl.pallas_call(
        body, name=name, out_shape=tuple(pltpu.HBM(a.shape, a.dtype) for a in held),
        in_specs=[hbm] * (2 * n) + [sem, sem, pl.BlockSpec(memory_space=pl.ANY)], out_specs=tuple([hbm] * (2 * n)),
        input_output_aliases={a: a for a in range(2 * n)},
        compiler_params=pltpu.CompilerParams(has_side_effects=pltpu.SideEffectType.DATAFLOW_SIDE_EFFECTING),
    )(*held, flight["send"], flight["recv"], after)
    got = []
    for a in range(n):
        src, land = outs[a], outs[n + a]
        own = src if gather[a] else lax.dynamic_index_in_dim(src, me, axis=0, keepdims=False)
        got.append(lax.dynamic_update_index_in_dim(land, own, me, axis=0))
    return got


def _ada_fwd(c_all, ada_w, ada_b_cols):
    nb, D = c_all.shape
    cols = ada_w.shape[2]

    def body(c_ref, w_ref, b_ref, out_ref):
        ca = _silu(c_ref[...])
        for l in range(DEPTH):
            out_ref[l] = _mm(ca, w_ref[l]) + b_ref[l:l + 1, :]

    return pl.pallas_call(
        body, name="ada_fwd", out_shape=jax.ShapeDtypeStruct((DEPTH, nb, cols), F32),
        in_specs=[pl.BlockSpec(memory_space=pltpu.VMEM)] * 3, out_specs=pl.BlockSpec(memory_space=pltpu.VMEM),
        compiler_params=pltpu.CompilerParams(vmem_limit_bytes=32 * VMEM_MB),
    )(c_all, ada_w, ada_b_cols)


def _ada_bwd(c_all, d_mod_cols):
    nb, D = c_all.shape
    cols = d_mod_cols.shape[2]

    def body(c_ref, dm_ref, out_ref):
        ca = _silu(c_ref[...])
        for l in range(DEPTH):
            out_ref[l] = _mm_tn(ca, dm_ref[l])

    return pl.pallas_call(
        body, name="ada_bwd", out_shape=jax.ShapeDtypeStruct((DEPTH, D, cols), F32),
        in_specs=[pl.BlockSpec(memory_space=pltpu.VMEM)] * 2, out_specs=pl.BlockSpec(memory_space=pltpu.VMEM),
        compiler_params=pltpu.CompilerParams(vmem_limit_bytes=32 * VMEM_MB),
    )(c_all, d_mod_cols)


def _sum_adamw(parts, w, m, v, name, after=None):
    P, R, C = parts.shape
    tr = 256 if (R % 256 == 0 and R > 256) else R
    extra = [] if after is None else [after]

    def body(p_ref, w_ref, m_ref, v_ref, *rest):
        g_ref, d_ref, nm_ref, nv_ref = rest[-4:]
        g = p_ref[0].astype(F32)
        for k in range(1, P):
            g = g + p_ref[k].astype(F32)
        g_ref[...] = g
        nm = ADAM_B1 * m_ref[...] + (1.0 - ADAM_B1) * g
        nv = ADAM_B2 * v_ref[...] + (1.0 - ADAM_B2) * (g * g)
        nm_ref[...] = nm
        nv_ref[...] = nv
        m_hat = nm / (1.0 - ADAM_B1 ** ADAM_STEP)
        v_hat = nv / (1.0 - ADAM_B2 ** ADAM_STEP)
        d_ref[...] = -ADAM_LR * (m_hat / (jnp.sqrt(v_hat) + ADAM_EPS) + ADAM_WD * w_ref[...])

    blk = pl.BlockSpec((tr, C), lambda i: (i, 0))
    shp = jax.ShapeDtypeStruct((R, C), F32)
    return pl.pallas_call(
        body, name=name, grid=(R // tr,),
        in_specs=[pl.BlockSpec((P, tr, C), lambda i: (0, i, 0)), blk, blk, blk]
        + [pl.BlockSpec(memory_space=pl.ANY)] * len(extra),
        out_specs=[blk, blk, blk, blk], out_shape=[shp, shp, shp, shp],
        compiler_params=_cp(("parallel",)),
    )(parts, w, m, v, *extra)


def _sum_adamw_layer(parts, w, m, v, layer, name, prev=None, after=None):
    P, R, C = parts.shape
    tr = 256 if (R % 256 == 0 and R > 256) else R

    def body(p_ref, w_ref, m_ref, v_ref, *rest):
        g_ref, d_ref, nm_ref, nv_ref = rest[-4:]
        g = p_ref[0].astype(F32)
        for k in range(1, P):
            g = g + p_ref[k].astype(F32)
        g_ref[0] = g
        nm = ADAM_B1 * m_ref[0] + (1.0 - ADAM_B1) * g
        nv = ADAM_B2 * v_ref[0] + (1.0 - ADAM_B2) * (g * g)
        nm_ref[0] = nm
        nv_ref[0] = nv
        m_hat = nm / (1.0 - ADAM_B1 ** ADAM_STEP)
        v_hat = nv / (1.0 - ADAM_B2 ** ADAM_STEP)
        d_ref[0] = -ADAM_LR * (m_hat / (jnp.sqrt(v_hat) + ADAM_EPS) + ADAM_WD * w_ref[0])

    blk = pl.BlockSpec((1, tr, C), lambda i: (layer, i, 0))
    shp = jax.ShapeDtypeStruct(w.shape, F32)
    in_specs = [pl.BlockSpec((P, tr, C), lambda i: (0, i, 0)), blk, blk, blk]
    args = [parts, w, m, v]
    aliases = {}
    if prev is not None:
        in_specs += [pl.BlockSpec(memory_space=pl.ANY)] * 4
        args += list(prev)
        aliases = {4 + k: k for k in range(4)}
    if after is not None:
        in_specs.append(pl.BlockSpec(memory_space=pl.ANY))
        args.append(after)
    return list(pl.pallas_call(
        body, name=name, grid=(R // tr,), in_specs=in_specs, out_specs=[blk] * 4, out_shape=[shp] * 4,
        input_output_aliases=aliases, compiler_params=_cp(("parallel",)),
    )(*args))


SMALL = ["norm_w", "mla_q_norm", "mla_kv_norm", "gla_w_g2", "gla_b_g2", "gla_norm", "final_norm"]


SMALL_ROWS = 72


def _pack_small(loss, part):
    flat = [jnp.pad(loss.reshape(1), (0, 127))] + [part[n].reshape(-1) for n in SMALL]
    used = sum(f.shape[0] for f in flat)
    flat.append(jnp.zeros((SMALL_ROWS * 128 - used,), F32))
    return jnp.concatenate(flat).reshape(SMALL_ROWS, 128)


def _small_adamw(packed_parts, w, m, v, after=None):
    n = len(w)
    extra = [] if after is None else [after]

    def body(*refs):
        p_ref = refs[0]
        w_refs, m_refs, v_refs = refs[1:1 + n], refs[1 + n:1 + 2 * n], refs[1 + 2 * n:1 + 3 * n]
        outs, acc = refs[1 + 3 * n + len(extra):-1], refs[-1]
        total = p_ref[0]
        for k in range(1, N_DEV):
            total = total + p_ref[k]
        acc[...] = total
        outs[0][...] = acc[0:1, :]
        r0 = 1
        for i in range(n):
            shp = w_refs[i].shape
            if len(shp) == 3:
                g = acc[r0:r0 + shp[0] * shp[1], :].reshape(shp)
                r0 += shp[0] * shp[1]
            elif shp[1] < 128:
                g = acc[r0:r0 + shp[0], 0:shp[1]]
                r0 += shp[0]
            else:
                k = shp[1] // 128
                g = jnp.concatenate(
                    [jnp.concatenate([acc[r0 + l * k + j:r0 + l * k + j + 1, :] for j in range(k)], axis=1)
                     for l in range(shp[0])], axis=0)
                r0 += shp[0] * k
            nm = ADAM_B1 * m_refs[i][...] + (1.0 - ADAM_B1) * g
            nv = ADAM_B2 * v_refs[i][...] + (1.0 - ADAM_B2) * (g * g)
            m_hat = nm / (1.0 - ADAM_B1 ** ADAM_STEP)
            v_hat = nv / (1.0 - ADAM_B2 ** ADAM_STEP)
            outs[1 + 4 * i][...] = g
            outs[2 + 4 * i][...] = -ADAM_LR * (m_hat / (jnp.sqrt(v_hat) + ADAM_EPS) + ADAM_WD * w_refs[i][...])
            outs[3 + 4 * i][...] = nm
            outs[4 + 4 * i][...] = nv

    vmem = pl.BlockSpec(memory_space=pltpu.VMEM)
    out_shape = [jax.ShapeDtypeStruct((1, 128), F32)]
    for a in w:
        out_shape += [jax.ShapeDtypeStruct(a.shape, F32)] * 4
    outs = pl.pallas_call(
        body, name="adamw_small", in_specs=[vmem] * (1 + 3 * n) + [pl.BlockSpec(memory_space=pl.ANY)] * len(extra),
        out_specs=[vmem] * (1 + 4 * n), out_shape=out_shape, scratch_shapes=[pltpu.VMEM((SMALL_ROWS, 128), F32)],
    )(packed_parts, *w, *m, *v, *extra)
    return outs[0], [outs[1 + 4 * i:5 + 4 * i] for i in range(n)]


WEIGHTS = ["norm_w", "ada_w", "ada_b", "w_in", "mla_q_norm", "w_uq", "mla_kv_norm", "w_ukv", "gla_w_g2",
           "gla_b_g2", "gla_norm", "w_out", "final_norm"]


def kernel(x, c, positions, norm_w, ada_w, ada_b, w_in, mla_q_norm, w_uq, mla_kv_norm, w_ukv, gla_w_g2, gla_b_g2, gla_norm, w_out, final_norm, loss_target, m_norm_w, m_ada_w, m_ada_b, m_w_in, m_mla_q_norm, m_w_uq, m_mla_kv_norm, m_w_ukv, m_gla_w_g2, m_gla_b_g2, m_gla_norm, m_w_out, m_final_norm, v_norm_w, v_ada_w, v_ada_b, v_w_in, v_mla_q_norm, v_w_uq, v_mla_kv_norm, v_w_ukv, v_gla_w_g2, v_gla_b_g2, v_gla_norm, v_w_out, v_final_norm):
    w = dict(norm_w=norm_w, ada_w=ada_w, ada_b=ada_b, w_in=w_in, mla_q_norm=mla_q_norm, w_uq=w_uq,
             mla_kv_norm=mla_kv_norm, w_ukv=w_ukv, gla_w_g2=gla_w_g2, gla_b_g2=gla_b_g2, gla_norm=gla_norm,
             w_out=w_out, final_norm=final_norm)
    m = dict(norm_w=m_norm_w, ada_w=m_ada_w, ada_b=m_ada_b, w_in=m_w_in, mla_q_norm=m_mla_q_norm, w_uq=m_w_uq,
             mla_kv_norm=m_mla_kv_norm, w_ukv=m_w_ukv, gla_w_g2=m_gla_w_g2, gla_b_g2=m_gla_b_g2,
             gla_norm=m_gla_norm, w_out=m_w_out, final_norm=m_final_norm)
    v = dict(norm_w=v_norm_w, ada_w=v_ada_w, ada_b=v_ada_b, w_in=v_w_in, mla_q_norm=v_mla_q_norm, w_uq=v_w_uq,
             mla_kv_norm=v_mla_kv_norm, w_ukv=v_w_ukv, gla_w_g2=v_gla_w_g2, gla_b_g2=v_gla_b_g2,
             gla_norm=v_gla_norm, w_out=v_w_out, final_norm=v_final_norm)
    B, S, D = x.shape
    me = 4 * lax.axis_index("x") + 2 * lax.axis_index("y") + lax.axis_index("c")
    ada_cols = ada_w.shape[2]
    cast = lambda a: a.astype(_MXU)

    sharded = ["w_in", "w_uq", "w_ukv", "w_out"]

    whole_in = _arrange_w_in
    whole_rest = lambda blks: (*_arrange_mla_weights(blks[0], blks[1]), blks[2].reshape(D, D))
    blocks_in = lambda dw_in_a: _unarrange_w_in(dw_in_a, N_DEV, w_in.shape[2])
    blocks_rest = lambda dw_out, dw_uq_a, dw_ukv_a: [
        *_unarrange_mla_weights(dw_uq_a, dw_ukv_a), dw_out.reshape(N_DEV, D // N_DEV, D).astype(jnp.bfloat16)]

    (c_g,) = _exchange([c], [True], "gather_c")
    c_all = c_g.reshape(N_DEV * B, D)

    ada_b_cols = lax.dynamic_slice(ada_b, (0, me * ada_cols), (DEPTH, ada_cols))
    mod_cols = _ada_fwd(c_all, ada_w, ada_b_cols)
    mod_send = jnp.transpose(mod_cols.reshape(DEPTH, N_DEV, B, ada_cols), (1, 0, 2, 3))
    (mod_recv,) = _exchange([mod_send], [False], "scatter_mod")
    mod = jnp.transpose(mod_recv, (1, 2, 0, 3)).reshape(DEPTH, B, 3 * D)

    flight_i = _exchange_start([cast(w_in[0])], [True], "gather_start_first", after=mod)
    flight_r = _exchange_start([cast(w[n][0]) for n in sharded[1:]], [True] * 3, "gather_start_layer0",
                               after=flight_i["token"])
    flight_w = _exchange_start([cast(w[n][1]) for n in sharded], [True] * 4, "gather_start_layer1",
                               after=flight_r["token"])
    small_w = {n: w[n] for n in SMALL}
    layer_small = lambda l: {n: a[l] for n, a in small_w.items() if n != "final_norm"}
    tabs = _rope_tables(positions.reshape(B, S, 1), flight_w["token"][0, 0])
    late0 = lambda after: whole_rest(_exchange_wait(flight_r, after, me, "gather_wait_layer0"))
    (w_in0_g,) = _exchange_wait(flight_i, tabs[0], me, "gather_wait_first")
    x1, saved0 = _layer_fwd(x, tabs, mod[0], layer_small(0), whole_in(w_in0_g), late_weights=late0)
    got1 = _exchange_wait(flight_w, x1, me, "gather_wait_layer1")
    (dx, loss, d_fw), saved1 = _layer_fwd(x1, tabs, mod[1], layer_small(1), whole_in(got1[0]), *whole_rest(got1[1:]),
                                          loss_head=(final_norm.reshape(1, D), loss_target))

    dx, g1 = _layer_bwd(dx, saved1, tabs)
    flight_g = _exchange_start([blocks_in(g1["w_in_a"])] + blocks_rest(g1["w_out"], g1["w_uq_a"], g1["w_ukv_a"]),
                               [False] * 4, "grads_start_layer1")
    flights = {}

    def early0(dw_out, dw_uq_a, dw_ukv_a):
        flights["rest0"] = _exchange_start(blocks_rest(dw_out, dw_uq_a, dw_ukv_a), [False] * 3, "grads_start_layer0")
        return flights["rest0"]["token"][0, 0]

    def early_in0(dw_in_a):
        flights["in0"] = _exchange_start([blocks_in(dw_in_a)], [False], "exchange_start_last")
        return flights["in0"]["token"][0, 0]

    saved0 = dict(saved0, gate=saved0["gate"] + flight_g["token"][0, 0])
    grad_x, g0 = _layer_bwd(dx, saved0, tabs, early_grads=early0, early_w_in=early_in0)
    parts1 = _exchange_wait(flight_g, grad_x, me, "grads_wait_layer1")
    rest0 = _exchange_wait(flights["rest0"], g0["w_in_a"], me, "grads_wait_layer0")

    both = lambda n: jnp.stack([g0[n], g1[n]])
    d_mod = both("d_mod")
    part = dict(norm_w=both("norm_w"), mla_q_norm=both("mla_q_norm"), mla_kv_norm=both("mla_kv_norm"),
                gla_w_g2=both("gla_w_g2"), gla_b_g2=both("gla_b_g2"), gla_norm=both("gla_norm256")[:, 0:128],
                final_norm=d_fw)
    flight_s = _exchange_start([d_mod, _pack_small(loss, part)], [True, True], "gather_small_start")
    flight_l = flights["in0"]
    res = {}
    behind = flight_s["token"]
    for a, name in enumerate(sharded):
        res[name] = _sum_adamw_layer(parts1[a], w[name], m[name], v[name], 1, "adamw_%s_layer1" % name, after=behind)
        behind = res[name][1]
    for a, name in enumerate(sharded[1:]):
        res[name] = _sum_adamw_layer(rest0[a], w[name], m[name], v[name], 0, "adamw_%s_layer0" % name,
                                     prev=res[name], after=behind)
        behind = res[name][1]
    (in0,) = _exchange_wait(flight_l, behind, me, "exchange_wait_last")
    res["w_in"] = _sum_adamw_layer(in0, w_in, m_w_in, v_w_in, 0, "adamw_w_in_layer0", prev=res["w_in"])
    behind = res["w_in"][1]

    d_mod_g, small_g = _exchange_wait(flight_s, behind, me, "gather_small_wait")
    d_mod_all = jnp.transpose(d_mod_g, (1, 0, 2, 3)).reshape(DEPTH, N_DEV * B, 3 * D)
    d_mod_cols = lax.dynamic_slice(d_mod_all, (0, 0, me * ada_cols), (DEPTH, N_DEV * B, ada_cols))
    g_ada_w = _ada_bwd(c_all, d_mod_cols)

    def update(name, parts2d, after):
        shp = w[name].shape
        two = lambda a: a.reshape(parts2d.shape[1:])
        out = _sum_adamw(parts2d, two(w[name]), two(m[name]), two(v[name]), "adamw_" + name, after=after)
        res[name] = [o.reshape(shp) for o in out]
        return out[1]

    behind = update("ada_w", g_ada_w.reshape(1, DEPTH * D, ada_cols), behind)
    behind = update("ada_b", jnp.transpose(d_mod_g, (0, 2, 1, 3)).reshape(N_DEV * B, DEPTH * 3 * D // 128, 128), behind)
    row = lambda a: a.reshape(1, D) if a.ndim == 1 else a
    loss_sum, small_out = _small_adamw(small_g, [row(w[n]) for n in SMALL], [row(m[n]) for n in SMALL],
                                       [row(v[n]) for n in SMALL], after=behind)
    for n, outs in zip(SMALL, small_out):
        res[n] = [o.reshape(w[n].shape) for o in outs]
    loss_out = loss_sum[0, 0]
    return (loss_out, grad_x, *[res[n][0] for n in WEIGHTS], *[res[n][1] for n in WEIGHTS],
            *[res[n][2] for n in WEIGHTS], *[res[n][3] for n in WEIGHTS])
```

```python
import functools
import math

import numpy as np
import jax
import jax.numpy as jnp
from jax import lax
from jax.experimental import pallas as pl
from jax.experimental.pallas import tpu as pltpu

F32 = jnp.float32
_MXU = jnp.bfloat16

D_MODEL = 1024
DEPTH = 2
CHUNK = 64
EPS = 1e-6
ROPE_THETA = 10000.0
N_DEV = 8

MLA_SCALE = 96.0 ** -0.5
RET_KSCALE = 64.0 ** -0.5
GLA_KSCALE = 32.0 ** -0.5
GLA_TAU = 16.0

ADAM_LR = 0.001
ADAM_B1 = 0.9
ADAM_B2 = 0.999
ADAM_EPS = 1e-08
ADAM_WD = 0.01
ADAM_STEP = 10

RET_W, MLA_W, GLA_W = 1024, 1024, 896
ARR_W = RET_W + MLA_W + GLA_W
VMEM_MB = 1024 * 1024


def _cp(sem, vmem_mb=48):
    return pltpu.CompilerParams(dimension_semantics=sem, vmem_limit_bytes=vmem_mb * VMEM_MB)


def _mm(a, b):
    return jnp.dot(a.astype(_MXU), b.astype(_MXU), preferred_element_type=F32)


def _mm_nt(a, b):
    return lax.dot_general(a.astype(_MXU), b.astype(_MXU), (((1,), (1,)), ((), ())),
                           preferred_element_type=F32)


def _mm_tn(a, b):
    return lax.dot_general(a.astype(_MXU), b.astype(_MXU), (((0,), (0,)), ((), ())),
                           preferred_element_type=F32)


def _mm_f32(a, b):
    return jnp.dot(a, b, precision=lax.Precision.HIGHEST, preferred_element_type=F32)


def _sig(z):
    return 1.0 / (1.0 + jnp.exp(-z))


def _silu(z):
    return z * _sig(z)


def _dsilu(z):
    s = _sig(z)
    return s * (1.0 + z * (1.0 - s))


def _full(shape):
    nd = len(shape)
    return pl.BlockSpec(shape, lambda *_: (0,) * nd)


def _behind(body, n_in, after):
    if after is None:
        return body, [], []

    def body_behind(*refs):
        body(*refs[:n_in], *refs[n_in + 1:])

    return body_behind, [after], [pl.BlockSpec(memory_space=pl.ANY)]


def _w_in_runs(block_cols):
    m, g = RET_W, RET_W + MLA_W
    whole = [(base + 64 * h + 32 * t, 32, base + 128 * t + 32 * h)
             for base in (0, 256) for t in range(2) for h in range(4)]
    whole += [(512, 512, 512), (1024, 384, m), (1408, 32, m + 448), (1440, 512, m + 512),
              (1952, 528, g), (2480, 256, g + 640)]
    zeros = [(m + 384, 64), (m + 480, 32), (g + 528, 112)]
    runs = []
    for src, n, dst in whole:
        while n:
            blk, off = divmod(src, block_cols)
            k = min(n, block_cols - off)
            runs.append((blk, off, k, dst))
            src, n, dst = src + k, n - k, dst + k
    return runs, zeros


def _arrange_w_in(blocks, tm=256):
    n, rows, cols = blocks.shape
    runs, zeros = _w_in_runs(cols)

    def arrange_w_in_kernel(b_ref, a_ref):
        for dst, k in zeros:
            a_ref[:, dst:dst + k] = jnp.zeros((tm, k), a_ref.dtype)
        for blk, off, k, dst in runs:
            a_ref[:, dst:dst + k] = b_ref[blk, :, off:off + k]

    return pl.pallas_call(
        arrange_w_in_kernel, grid=(rows // tm,),
        in_specs=[pl.BlockSpec((n, tm, cols), lambda i: (0, i, 0))],
        out_specs=pl.BlockSpec((tm, ARR_W), lambda i: (i, 0)),
        out_shape=jax.ShapeDtypeStruct((rows, ARR_W), blocks.dtype),
        compiler_params=_cp(("parallel",)), name="arrange_w_in")(blocks)


def _unarrange_w_in(a, n, cols, tm=256):
    rows = a.shape[0]
    runs, _ = _w_in_runs(cols)

    def unarrange_w_in_kernel(a_ref, b_ref):
        for blk, off, k, dst in runs:
            b_ref[blk, :, off:off + k] = a_ref[:, dst:dst + k].astype(b_ref.dtype)

    return pl.pallas_call(
        unarrange_w_in_kernel, grid=(rows // tm,),
        in_specs=[pl.BlockSpec((tm, ARR_W), lambda i: (i, 0))],
        out_specs=pl.BlockSpec((n, tm, cols), lambda i: (0, i, 0)),
        out_shape=jax.ShapeDtypeStruct((n, rows, cols), jnp.bfloat16),
        compiler_params=_cp(("parallel",)), name="unarrange_w_in")(a)


def _arrange_mla_weights(uq_heads, ukv_heads):
    nh = uq_heads.shape[0]
    dt = uq_heads.dtype

    def arrange_mla_weights_kernel(uq_ref, ukv_ref, q_ref, kv_ref, kvt_ref):
        q_ref[...] = jnp.zeros(q_ref.shape, dt)
        kv_ref[...] = jnp.zeros(kv_ref.shape, dt)
        kvt_ref[...] = jnp.zeros(kvt_ref.shape, dt)
        for h in range(nh):
            q_ref[:, 128 * h:128 * h + 96] = uq_ref[h]
            blk = ukv_ref[h]
            kv_ref[:, 128 * h:128 * h + 64] = blk[:, 0:64]
            kv_ref[:, 128 * nh + 64 * h:128 * nh + 64 * h + 64] = blk[:, 64:128]
            blk_t = blk.astype(F32).T.astype(dt)
            kvt_ref[128 * h:128 * h + 64, :] = blk_t[0:64]
            kvt_ref[128 * nh + 64 * h:128 * nh + 64 * h + 64, :] = blk_t[64:128]

    return pl.pallas_call(
        arrange_mla_weights_kernel, name="arrange_mla_weights",
        out_shape=[jax.ShapeDtypeStruct((256, 128 * nh), dt), jax.ShapeDtypeStruct((128, 192 * nh), dt),
                   jax.ShapeDtypeStruct((192 * nh, 128), dt)])(uq_heads, ukv_heads)


def _unarrange_mla_weights(dw_uq_a, dw_ukv_a):
    nh = dw_uq_a.shape[1] // 128

    def unarrange_mla_weights_kernel(q_ref, kv_ref, uq_ref, ukv_ref):
        for h in range(nh):
            uq_ref[h] = q_ref[:, 128 * h:128 * h + 96].astype(uq_ref.dtype)
            ukv_ref[h, :, 0:64] = kv_ref[:, 128 * h:128 * h + 64].astype(ukv_ref.dtype)
            ukv_ref[h, :, 64:128] = kv_ref[:, 128 * nh + 64 * h:128 * nh + 64 * h + 64].astype(ukv_ref.dtype)

    return pl.pallas_call(
        unarrange_mla_weights_kernel, name="unarrange_mla_weights",
        out_shape=[jax.ShapeDtypeStruct((nh, 256, 96), jnp.bfloat16),
                   jax.ShapeDtypeStruct((nh, 128, 128), jnp.bfloat16)])(dw_uq_a, dw_ukv_a)


def _rope_tables(pos3, after=None):
    B, S, _ = pos3.shape
    ts = min(S, 512)
    inv32 = (np.float32(ROPE_THETA) ** (-(np.arange(32, dtype=np.float32) / 32))).astype(np.float32)
    inv16 = (np.float32(ROPE_THETA) ** (-(np.arange(16, dtype=np.float32) / 16))).astype(np.float32)
    inv = np.zeros((1, 128), np.float32)
    inv[0, 0:32] = inv32
    inv[0, 32:48] = inv16

    def body(pos_ref, inv_ref, cr, sr, cm, sm):
        ang = pos_ref[0].astype(F32) * inv_ref[...]
        lane = lax.broadcasted_iota(jnp.int32, (1, 128), 1)

        def every_head(x):
            y = jnp.where(lane < 32, x, pltpu.roll(x, 32, 1))
            return jnp.where(lane < 64, y, pltpu.roll(y, 64, 1))

        def rotary_pair(x, fill):
            return jnp.where((lane >= 64) & (lane < 80), pltpu.roll(x, 32, 1),
                             jnp.where((lane >= 80) & (lane < 96), pltpu.roll(x, 48, 1), fill))

        c, s = jnp.cos(ang), jnp.sin(ang)
        cr[0] = every_head(c)
        sr[0] = every_head(s)
        cm[0] = rotary_pair(c, 1.0)
        sm[0] = rotary_pair(s, 0.0)

    tab = jax.ShapeDtypeStruct((B, S, 128), F32)
    blk = pl.BlockSpec((1, ts, 128), lambda b, i: (b, i, 0))
    body, extra, extra_specs = _behind(body, 2, after)
    return pl.pallas_call(
        body, name="rope_tables", grid=(B, S // ts),
        in_specs=[pl.BlockSpec((1, ts, 1), lambda b, i: (b, i, 0)), _full((1, 128))] + extra_specs,
        out_specs=[blk, blk, blk, blk], out_shape=[tab, tab, tab, tab],
        compiler_params=_cp(("parallel", "parallel")),
    )(pos3, jnp.asarray(inv), *extra)


def _rope128(x, cos, sin):
    lane = lax.broadcasted_iota(jnp.int32, (1, 128), 1)
    rp = pltpu.roll(x, 16, 1)
    rm = pltpu.roll(x, 112, 1)
    return x * cos + jnp.where(lane < 80, -rm, rp) * sin


def _rope128_t(d, cos, sin):
    lane = lax.broadcasted_iota(jnp.int32, (1, 128), 1)
    y = d * sin
    yp = pltpu.roll(y, 16, 1)
    ym = pltpu.roll(y, 112, 1)
    return d * cos + jnp.where(lane < 64, 0.0, jnp.where(lane < 80, ym, jnp.where(lane < 96, -yp, 0.0)))


def _proj_fwd(x, shift, scale, nw, w_arr):
    B, S, D = x.shape
    tm = min(S, 512)

    def body(x_ref, sh_ref, sc_ref, nw_ref, w_ref, ret_ref, mla_ref, gla_ref, h_ref):
        xv = x_ref[0]
        rstd = lax.rsqrt(jnp.mean(xv * xv, axis=-1, keepdims=True) + EPS)
        h = (xv * rstd * nw_ref[...]) * (1.0 + sc_ref[0]) + sh_ref[0]
        hb = h.astype(_MXU)
        h_ref[0] = hb
        ret_ref[0] = jnp.dot(hb, w_ref[:, 0:RET_W], preferred_element_type=F32).astype(_MXU)
        mla_ref[0] = jnp.dot(hb, w_ref[:, RET_W:RET_W + MLA_W], preferred_element_type=F32).astype(_MXU)
        gla_ref[0] = jnp.dot(hb, w_ref[:, RET_W + MLA_W:ARR_W], preferred_element_type=F32).astype(_MXU)

    tok = lambda w: pl.BlockSpec((1, tm, w), lambda b, i: (b, i, 0))
    per_seq = pl.BlockSpec((1, 1, D), lambda b, i: (b, 0, 0))
    return pl.pallas_call(
        body, name="proj_fwd", grid=(B, S // tm),
        in_specs=[tok(D), per_seq, per_seq, _full((1, D)), _full((D, ARR_W))],
        out_specs=[tok(RET_W), tok(MLA_W), tok(GLA_W), tok(D)],
        out_shape=[jax.ShapeDtypeStruct((B, S, RET_W), _MXU), jax.ShapeDtypeStruct((B, S, MLA_W), _MXU),
                   jax.ShapeDtypeStruct((B, S, GLA_W), _MXU), jax.ShapeDtypeStruct((B, S, D), _MXU)],
        compiler_params=_cp(("parallel", "parallel")),
    )(x, shift, scale, nw, w_arr)


RET_L = 256


def _ret_consts(L):
    lg = np.log1p(-np.exp2(-5.0 - np.arange(4, dtype=np.float32))).astype(np.float32)
    i = np.arange(L)
    ci = i // CHUNK
    diff = (i[:, None] - i[None, :]).astype(np.float32)
    same = ci[:, None] == ci[None, :]
    past = ci[None, :] < ci[:, None]
    expo = np.where(same, np.abs(diff), np.where(past, diff, 0.0)).astype(np.float32)
    dec = np.where((same | past)[None], np.exp(lg[:, None, None] * expo[None]), 0.0).astype(np.float32)
    head = (np.arange(256) % 128) // 32
    qw = np.exp((i + 1.0)[:, None] * lg[head][None, :]).astype(np.float32)
    kw = np.exp((L - 1.0 - i)[:, None] * lg[head][None, :]).astype(np.float32)
    a_row = np.exp(np.float32(L) * lg[head])[None, :].astype(np.float32)
    return [jnp.asarray(t) for t in (dec.reshape(4 * L, L), qw, kw, a_row)]


def _ret_masks():
    lane = lax.broadcasted_iota(jnp.int32, (1, 256), 1)
    mh = [((lane % 128) // 32) == h for h in range(4)]
    mv = [(lane // 64) == h for h in range(4)]
    vi = lax.broadcasted_iota(jnp.int32, (256, 256), 0)
    ki = lax.broadcasted_iota(jnp.int32, (256, 256), 1)
    bd = (vi // 64) == ((ki % 128) // 32)
    return mh, mv, bd


def _ret_rope(p, cs, sn):
    q1, q2, k1, k2 = p[:, 0:128], p[:, 128:256], p[:, 256:384], p[:, 384:512]
    qr = jnp.concatenate([q1 * cs - q2 * sn, q2 * cs + q1 * sn], axis=1)
    kr = jnp.concatenate([k1 * cs - k2 * sn, k2 * cs + k1 * sn], axis=1) * RET_KSCALE
    return qr, kr


def _head_mean(x, mv, width):
    out = jnp.zeros_like(x)
    for m in mv:
        s = jnp.sum(jnp.where(m, x, 0.0), axis=-1, keepdims=True) * (1.0 / width)
        out = jnp.where(m, s, out)
    return out


def _stack_heads(x, masks):
    return jnp.concatenate([jnp.where(m, x, 0.0) for m in masks], axis=0)


def _fold_heads(xs, masks, L):
    out = jnp.where(masks[0], xs[0:L], 0.0)
    for h in range(1, 4):
        out = out + jnp.where(masks[h], xs[h * L:(h + 1) * L], 0.0)
    return out


RET_G = 2


def _ret_fwd(ret_p, cos, sin):
    B, S, _ = ret_p.shape
    L = min(RET_L, S)
    NB = S // L
    G = min(RET_G, NB)
    NG = NB // G
    consts = _ret_consts(L)

    def body(p_ref, c_ref, s_ref, ds_ref, qw_ref, kw_ref, a_ref, out_ref, raw_ref, st_ref, st_sc):
        @pl.when(pl.program_id(1) == 0)
        def _():
            st_sc[...] = jnp.zeros_like(st_sc)

        mh, mv, bd = _ret_masks()
        cs_ = range(G)
        rows = [slice(c * L, (c + 1) * L) for c in cs_]
        ps = [p_ref[0, rows[c], :].astype(F32) for c in cs_]
        qk = [_ret_rope(ps[c], c_ref[0, rows[c], :], s_ref[0, rows[c], :]) for c in cs_]
        vs = [ps[c][:, 512:768] for c in cs_]
        a_s = [_mm_nt(_stack_heads(qk[c][0], mh), qk[c][1]) for c in cs_]
        upd = [_mm_tn(vs[c], qk[c][1] * kw_ref[...]) for c in cs_]
        o_s = [_mm(a_s[c] * ds_ref[...], vs[c]) for c in cs_]
        st = st_sc[...]
        inter = []
        for c in cs_:
            st_ref[0, c] = st
            inter.append(_mm_nt(qk[c][0] * qw_ref[...], st))
            st = st * a_ref[...] + jnp.where(bd, upd[c], 0.0)
        st_sc[...] = st
        for c in cs_:
            r = _fold_heads(o_s[c], mv, L) + inter[c]
            raw_ref[0, rows[c], :] = r
            rstd = lax.rsqrt(_head_mean(r * r, mv, 64.0) + EPS)
            out_ref[0, rows[c], :] = (r * rstd * _silu(ps[c][:, 768:1024])).astype(_MXU)

    tok = lambda w: pl.BlockSpec((1, G * L, w), lambda b, n: (b, n, 0))
    return pl.pallas_call(
        body, name="ret_fwd", grid=(B, NG),
        in_specs=[tok(RET_W), tok(128), tok(128), _full((4 * L, L)), _full((L, 256)), _full((L, 256)),
                  _full((1, 256))],
        out_specs=[tok(256), tok(256), pl.BlockSpec((1, G, 256, 256), lambda b, n: (b, n, 0, 0))],
        out_shape=[jax.ShapeDtypeStruct((B, S, 256), _MXU), jax.ShapeDtypeStruct((B, S, 256), F32),
                   jax.ShapeDtypeStruct((B, NB, 256, 256), F32)],
        scratch_shapes=[pltpu.VMEM((256, 256), F32)],
        compiler_params=_cp(("parallel", "arbitrary")),
    )(ret_p, cos, sin, *consts)


def _ret_bwd(ret_p, cos, sin, raw, states, d_mix):
    B, S, _ = ret_p.shape
    L = min(RET_L, S)
    NB = S // L
    G = 1
    NG = NB // G
    consts = _ret_consts(L)

    def body(p_ref, c_ref, s_ref, raw_ref, st_ref, dm_ref, ds_ref, qw_ref, kw_ref, a_ref, dp_ref, dst_sc):
        @pl.when(pl.program_id(1) == 0)
        def _():
            dst_sc[...] = jnp.zeros_like(dst_sc)

        mh, mv, bd = _ret_masks()
        qw, kw, dec = qw_ref[...], kw_ref[...], ds_ref[...]
        cs_ = range(G)
        rows = [slice(c * L, (c + 1) * L) for c in cs_]
        ps = [p_ref[0, rows[c], :].astype(F32) for c in cs_]
        tabs = [(c_ref[0, rows[c], :], s_ref[0, rows[c], :]) for c in cs_]
        qk = [_ret_rope(ps[c], *tabs[c]) for c in cs_]
        vs = [ps[c][:, 512:768] for c in cs_]
        qs = [_stack_heads(qk[c][0], mh) for c in cs_]
        a_s = [_mm_nt(qs[c], qk[c][1]) for c in cs_]
        dr, dz = [], []
        for c in cs_:
            r = raw_ref[0, rows[c], :]
            z = ps[c][:, 768:1024]
            rstd = lax.rsqrt(_head_mean(r * r, mv, 64.0) + EPS)
            rn = r * rstd
            dm = dm_ref[0, rows[c], :]
            d_rn = dm * _silu(z)
            dz.append(dm * rn * _dsilu(z))
            dr.append(rstd * (d_rn - rn * _head_mean(d_rn * rn, mv, 64.0)))
        do_s = [_stack_heads(dr[c], mv) for c in cs_]
        da_s = [_mm_nt(do_s[c], vs[c]) for c in cs_]
        sts = [st_ref[0, c] for c in cs_]
        dq_st = [_mm(dr[c], sts[c]) for c in cs_]
        dst_in = [_mm_tn(dr[c], qk[c][0] * qw) for c in cs_]
        dv = [_mm_tn(a_s[c] * dec, do_s[c]) for c in cs_]
        dqr, dkr = [], []
        for c in cs_:
            da = da_s[c] * dec
            dqr.append(_fold_heads(_mm(da, qk[c][1]), mh, L) + dq_st[c] * qw)
            dkr.append(_mm_tn(da, qs[c]))
        dst_next = dst_sc[...]
        for c in reversed(cs_):
            g = jnp.where(bd, dst_next, 0.0)
            dv[c] = dv[c] + _mm_nt(qk[c][1] * kw, g)
            dkr[c] = dkr[c] + _mm(vs[c], g) * kw
            dst_next = dst_next * a_ref[...] + jnp.where(bd, dst_in[c], 0.0)
        dst_sc[...] = dst_next
        for c in cs_:
            cs, sn = tabs[c]
            dk = dkr[c] * RET_KSCALE
            dq1, dq2 = dqr[c][:, 0:128], dqr[c][:, 128:256]
            dk1, dk2 = dk[:, 0:128], dk[:, 128:256]
            dp_ref[0, rows[c], :] = jnp.concatenate(
                [dq1 * cs + dq2 * sn, dq2 * cs - dq1 * sn, dk1 * cs + dk2 * sn, dk2 * cs - dk1 * sn, dv[c], dz[c]],
                axis=1).astype(_MXU)

    tok = lambda w: pl.BlockSpec((1, G * L, w), lambda b, i: (b, NG - 1 - i, 0))
    return pl.pallas_call(
        body, name="ret_bwd", grid=(B, NG),
        in_specs=[tok(RET_W), tok(128), tok(128), tok(256),
                  pl.BlockSpec((1, G, 256, 256), lambda b, i: (b, NG - 1 - i, 0, 0)), tok(256),
                  _full((4 * L, L)), _full((L, 256)), _full((L, 256)), _full((1, 256))],
        out_specs=tok(RET_W), out_shape=jax.ShapeDtypeStruct((B, S, RET_W), _MXU),
        scratch_shapes=[pltpu.VMEM((256, 256), F32)],
        compiler_params=_cp(("parallel", "arbitrary")),
    )(ret_p, cos, sin, raw, states, d_mix, *consts)


def _gla_masks():
    C = CHUNK
    lk = lax.broadcasted_iota(jnp.int32, (1, 128), 1)
    lv = lax.broadcasted_iota(jnp.int32, (1, 256), 1)
    mk = [(lk // 32) == h for h in range(4)]
    mv = [(lv // 64) == h for h in range(4)]
    vi = lax.broadcasted_iota(jnp.int32, (256, 128), 0)
    ki = lax.broadcasted_iota(jnp.int32, (256, 128), 1)
    bd = (vi // 64) == (ki // 32)
    ri = lax.broadcasted_iota(jnp.int32, (4 * C, C), 0) % C
    cj = lax.broadcasted_iota(jnp.int32, (4 * C, C), 1)
    lower = ri >= cj
    ti = lax.broadcasted_iota(jnp.int32, (C, C), 0)
    tj = lax.broadcasted_iota(jnp.int32, (C, C), 1)
    ltri = jnp.where(ti >= tj, 1.0, 0.0).astype(F32)
    utri = jnp.where(ti <= tj, 1.0, 0.0).astype(F32)
    return mk, mv, bd, lower, ltri, utri


def _log_sigmoid(x):
    return jnp.minimum(x, 0.0) - jnp.log(1.0 + jnp.exp(-jnp.abs(x)))


GLA_G = 8


def _gla_fwd(gla_p, w_g2p, b_g2, gnw):
    B, S, _ = gla_p.shape
    C = CHUNK
    NC = S // C
    G = min(GLA_G, NC)
    NG = NC // G

    def body(p_ref, w_ref, b_ref, gn_ref, out_ref, raw_ref, st_ref, st_sc):
        @pl.when(pl.program_id(1) == 0)
        def _():
            st_sc[...] = jnp.zeros_like(st_sc)

        mk, mv, bd, lower, ltri, _ = _gla_masks()
        cs = range(G)
        rows = [slice(c * C, (c + 1) * C) for c in cs]
        ps = [p_ref[0, rows[c], :].astype(F32) for c in cs]
        pre = [_mm(ps[c][:, 512:640], w_ref[...]) + b_ref[...] for c in cs]
        cum = [_mm_f32(ltri, _log_sigmoid(pre[c]) * (1.0 / GLA_TAU)) for c in cs]
        past, fut, upd, q_pos, a_row = [], [], [], [], []
        for c in cs:
            q = ps[c][:, 0:128]
            k = ps[c][:, 128:256] * GLA_KSCALE
            last = cum[c][C - 1:C, :]
            e_pos = jnp.exp(cum[c])
            e_neg = jnp.exp(-cum[c])
            q_pos.append(q * e_pos)
            a_row.append(jnp.exp(last))
            past.append(_mm_nt(_stack_heads(q_pos[c], mk), k * e_neg))
            fut.append(_mm_nt(_stack_heads(q * e_neg, mk), k * e_pos))
            upd.append(_mm_tn(ps[c][:, 256:512], k * jnp.exp(last - cum[c])))
        o_s = [_mm(jnp.where(lower, past[c], fut[c]), ps[c][:, 256:512]) for c in cs]
        st = st_sc[...]
        inter = []
        for c in cs:
            st_ref[0, c] = st
            inter.append(_mm_nt(q_pos[c], st))
            st = st * a_row[c] + jnp.where(bd, upd[c], 0.0)
        st_sc[...] = st
        for c in cs:
            g = _fold_heads(o_s[c], mv, C) + inter[c]
            raw_ref[0, rows[c], :] = g
            rstd = lax.rsqrt(_head_mean(g * g, mv, 64.0) + EPS)
            out_ref[0, rows[c], :] = (g * rstd * gn_ref[...] * _silu(ps[c][:, 640:896])).astype(_MXU)

    tok = lambda w: pl.BlockSpec((1, G * C, w), lambda b, n: (b, n, 0))
    return pl.pallas_call(
        body, name="gla_fwd", grid=(B, NG),
        in_specs=[tok(GLA_W), _full((128, 128)), _full((1, 128)), _full((1, 256))],
        out_specs=[tok(256), tok(256), pl.BlockSpec((1, G, 256, 128), lambda b, n: (b, n, 0, 0))],
        out_shape=[jax.ShapeDtypeStruct((B, S, 256), _MXU), jax.ShapeDtypeStruct((B, S, 256), F32),
                   jax.ShapeDtypeStruct((B, NC, 256, 128), F32)],
        scratch_shapes=[pltpu.VMEM((256, 128), F32)],
        compiler_params=_cp(("parallel", "arbitrary")),
    )(gla_p, w_g2p, b_g2, gnw)


def _gla_bwd(gla_p, w_g2p, b_g2, gnw, raw, states, d_mix, after=None):
    B, S, _ = gla_p.shape
    C = CHUNK
    NC = S // C
    G = min(GLA_G, NC)
    NG = NC // G

    def body(p_ref, w_ref, b_ref, gn_ref, raw_ref, st_ref, dm_ref, dp_ref, dw_ref, db_ref, dgn_ref, dst_sc):
        first = (pl.program_id(0) == 0) & (pl.program_id(1) == 0)

        @pl.when(first)
        def _():
            dw_ref[...] = jnp.zeros_like(dw_ref)
            db_ref[...] = jnp.zeros_like(db_ref)
            dgn_ref[...] = jnp.zeros_like(dgn_ref)

        @pl.when(pl.program_id(1) == 0)
        def _():
            dst_sc[...] = jnp.zeros_like(dst_sc)

        mk, mv, bd, lower, ltri, utri = _gla_masks()
        gn = gn_ref[...]
        cs = range(G)
        rows = [slice(c * C, (c + 1) * C) for c in cs]
        ps = [p_ref[0, rows[c], :].astype(F32) for c in cs]
        vs = [ps[c][:, 256:512] for c in cs]
        pre = [_mm(ps[c][:, 512:640], w_ref[...]) + b_ref[...] for c in cs]
        cum = [_mm_f32(ltri, _log_sigmoid(pre[c]) * (1.0 / GLA_TAU)) for c in cs]
        dg, dz, dgn_acc = [], [], jnp.zeros((1, 256), F32)
        for c in cs:
            g = raw_ref[0, rows[c], :]
            z = ps[c][:, 640:896]
            rstd = lax.rsqrt(_head_mean(g * g, mv, 64.0) + EPS)
            gh = g * rstd
            dm = dm_ref[0, rows[c], :]
            d_gn = dm * _silu(z)
            dz.append(dm * gh * gn * _dsilu(z))
            dgn_acc = dgn_acc + jnp.sum(d_gn * gh, axis=0, keepdims=True)
            d_gh = d_gn * gn
            dg.append(rstd * (d_gh - gh * _head_mean(d_gh * gh, mv, 64.0)))
        do_s = [_stack_heads(dg[c], mv) for c in cs]
        dattn = [_mm_nt(do_s[c], vs[c]) for c in cs]
        ks, e_pos, e_neg, q_pos, q_neg, k_pos, k_neg, qp_s, qn_s, past, fut, a_row, w_dec, kd = ([] for _ in range(14))
        for c in cs:
            q = ps[c][:, 0:128]
            k = ps[c][:, 128:256] * GLA_KSCALE
            last = cum[c][C - 1:C, :]
            ep, en = jnp.exp(cum[c]), jnp.exp(-cum[c])
            ks.append(k), e_pos.append(ep), e_neg.append(en)
            q_pos.append(q * ep), q_neg.append(q * en), k_pos.append(k * ep), k_neg.append(k * en)
            qp_s.append(_stack_heads(q_pos[c], mk)), qn_s.append(_stack_heads(q_neg[c], mk))
            past.append(_mm_nt(qp_s[c], k_neg[c]))
            fut.append(_mm_nt(qn_s[c], k_pos[c]))
            a_row.append(jnp.exp(last))
            w_dec.append(jnp.exp(last - cum[c]))
            kd.append(k * w_dec[c])
        sts = [st_ref[0, c] for c in cs]
        dq_st = [_mm(dg[c], sts[c]) for c in cs]
        dst_in = [_mm_tn(dg[c], q_pos[c]) for c in cs]
        dv, dq_pos, dk_neg, dq_neg, dk_pos = [], [], [], [], []
        for c in cs:
            attn = jnp.where(lower, past[c], fut[c])
            dpast = jnp.where(lower, dattn[c], 0.0)
            dfut = jnp.where(lower, 0.0, dattn[c])
            dv.append(_mm_tn(attn, do_s[c]))
            dq_pos.append(_fold_heads(_mm(dpast, k_neg[c]), mk, C) + dq_st[c])
            dk_neg.append(_mm_tn(dpast, qp_s[c]))
            dq_neg.append(_fold_heads(_mm(dfut, k_pos[c]), mk, C))
            dk_pos.append(_mm_tn(dfut, qn_s[c]))
        dst_next = dst_sc[...]
        d_a, d_kd = [None] * G, [None] * G
        for c in reversed(cs):
            d_a[c] = jnp.sum(dst_next * sts[c], axis=0, keepdims=True)
            gmat = jnp.where(bd, dst_next, 0.0)
            d_kd[c] = _mm(vs[c], gmat)
            dv[c] = dv[c] + _mm_nt(kd[c], gmat)
            dst_next = dst_next * a_row[c] + jnp.where(bd, dst_in[c], 0.0)
        dst_sc[...] = dst_next
        row = lax.broadcasted_iota(jnp.int32, (C, 128), 0)
        d_la, dk, dq = [], [], []
        for c in cs:
            t = d_kd[c] * kd[c]
            dk.append(d_kd[c] * w_dec[c] + dk_neg[c] * e_neg[c] + dk_pos[c] * e_pos[c])
            dq.append(dq_pos[c] * e_pos[c] + dq_neg[c] * e_neg[c])
            d_last = jnp.sum(t, axis=0, keepdims=True) + d_a[c] * a_row[c]
            d_cum = (dq_pos[c] * q_pos[c] - dk_neg[c] * k_neg[c] - dq_neg[c] * q_neg[c] + dk_pos[c] * k_pos[c] - t)
            d_la.append(_mm_f32(utri, d_cum + jnp.where(row == C - 1, d_last, 0.0)))
        d_pre = [d_la[c] * _sig(-pre[c]) * (1.0 / GLA_TAU) for c in cs]
        d_gg = [_mm_nt(d_pre[c], w_ref[...]) for c in cs]
        dw_acc = _mm_tn(ps[0][:, 512:640], d_pre[0])
        db_acc = jnp.sum(d_pre[0], axis=0, keepdims=True)
        for c in cs[1:]:
            dw_acc = dw_acc + _mm_tn(ps[c][:, 512:640], d_pre[c])
            db_acc = db_acc + jnp.sum(d_pre[c], axis=0, keepdims=True)
        for c in cs:
            dp_ref[0, rows[c], :] = jnp.concatenate([dq[c], dk[c] * GLA_KSCALE, dv[c], d_gg[c], dz[c]],
                                                    axis=1).astype(_MXU)
        dw_ref[...] += dw_acc
        db_ref[...] += db_acc
        dgn_ref[...] += dgn_acc

        @pl.when((pl.program_id(0) == B - 1) & (pl.program_id(1) == NG - 1))
        def _():
            s1 = dgn_ref[...]
            s1 = s1 + pltpu.roll(s1, 128, 1)
            dgn_ref[...] = s1 + pltpu.roll(s1, 64, 1)

    tok = lambda w: pl.BlockSpec((1, G * C, w), lambda b, i: (b, NG - 1 - i, 0))
    body, extra, extra_specs = _behind(body, 7, after)
    return pl.pallas_call(
        body, name="gla_bwd", grid=(B, NG),
        in_specs=[tok(GLA_W), _full((128, 128)), _full((1, 128)), _full((1, 256)), tok(256),
                  pl.BlockSpec((1, G, 256, 128), lambda b, i: (b, NG - 1 - i, 0, 0)), tok(256)] + extra_specs,
        out_specs=[tok(GLA_W), _full((128, 128)), _full((1, 128)), _full((1, 256))],
        out_shape=[jax.ShapeDtypeStruct((B, S, GLA_W), _MXU), jax.ShapeDtypeStruct((128, 128), F32),
                   jax.ShapeDtypeStruct((1, 128), F32), jax.ShapeDtypeStruct((1, 256), F32)],
        scratch_shapes=[pltpu.VMEM((256, 128), F32)],
        compiler_params=_cp(("arbitrary", "arbitrary")),
    )(gla_p, w_g2p, b_g2, gnw, raw, states, d_mix, *extra)


def _rms(x, w):
    rstd = lax.rsqrt(jnp.mean(x * x, axis=-1, keepdims=True) + EPS)
    xh = x * rstd
    return xh, rstd, xh * w


def _rms_bwd(dy, xh, rstd, w):
    dxh = dy * w
    return rstd * (dxh - xh * jnp.mean(dxh * xh, axis=-1, keepdims=True))


MLA_T = 256


def _mla_prep_fwd(mla_p, cos, sin, qnw, kvnw, w_uq, w_ukv, w_ukv_t):
    B, S, _ = mla_p.shape
    tm = min(S, 512)

    t = min(MLA_T, S)
    nt = tm // t

    def body(p_ref, c_ref, s_ref, qn_ref, kn_ref, wq_ref, wkv_ref, wkvt_ref, q_ref, k_ref, v_ref, kt_ref, vt_ref):
        p = p_ref[0].astype(F32)
        cs, sn = c_ref[0], s_ref[0]
        _, _, qn = _rms(p[:, 0:256], qn_ref[...])
        qpre = _mm(qn, wq_ref[...])
        _, _, kvn = _rms(p[:, 256:384], kn_ref[...])
        kv = _mm(kvn, wkv_ref[...])
        kvt = _mm_nt(wkvt_ref[...], kvn)
        kpe = _rope128(p[:, 384:512], cs, sn)
        kpet = kpe.T
        for h in range(8):
            sl = slice(128 * h, 128 * h + 128)
            q_ref[0, :, sl] = _rope128(qpre[:, sl], cs, sn).astype(_MXU)
            k_ref[0, :, sl] = (kv[:, sl] + kpe).astype(_MXU)
            kht = kvt[sl, :] + kpet
            for n in range(nt):
                kt_ref[0, n, sl, :] = kht[:, n * t:(n + 1) * t].astype(_MXU)
        v_ref[0] = kv[:, 1024:1536].astype(_MXU)
        for n in range(nt):
            vt_ref[0, n] = kvt[1024:1536, n * t:(n + 1) * t].astype(_MXU)

    tok = lambda w: pl.BlockSpec((1, tm, w), lambda b, i: (b, i, 0))
    tr = lambda w: pl.BlockSpec((1, nt, w, t), lambda b, i: (b, i, 0, 0))
    return pl.pallas_call(
        body, name="mla_prep_fwd", grid=(B, S // tm),
        in_specs=[tok(512), tok(128), tok(128), _full((1, 256)), _full((1, 128)), _full((256, 1024)),
                  _full((128, 1536)), _full((1536, 128))],
        out_specs=[tok(1024), tok(1024), tok(512), tr(1024), tr(512)],
        out_shape=[jax.ShapeDtypeStruct((B, S, 1024), _MXU), jax.ShapeDtypeStruct((B, S, 1024), _MXU),
                   jax.ShapeDtypeStruct((B, S, 512), _MXU), jax.ShapeDtypeStruct((B, S // t, 1024, t), _MXU),
                   jax.ShapeDtypeStruct((B, S // t, 512, t), _MXU)],
        compiler_params=_cp(("parallel", "parallel")),
    )(mla_p, cos, sin, qnw, kvnw, w_uq, w_ukv, w_ukv_t)


def _chunk_mask_t(t):
    kj = lax.broadcasted_iota(jnp.int32, (t, t), 0) // CHUNK
    qi = lax.broadcasted_iota(jnp.int32, (t, t), 1) // CHUNK
    return kj <= qi


MLA_HG = 8
MLA_HG_FWD = 8
LOG2E = 1.4426950408889634
MLA_C2 = MLA_SCALE * LOG2E


def _mla_attn_fwd(q, k, vt):
    B, S, _ = q.shape
    t = min(MLA_T, S)
    nq = S // t
    HG = MLA_HG_FWD
    NP = HG // 2

    def body(q_ref, k_ref, vt_ref, o_ref, lse_ref, sa, sb, m_sc, l_sc, acc_sc):
        i = pl.program_id(2)
        row = lax.broadcasted_iota(jnp.int32, (128, 1), 0)
        low = row < 64
        mask = _chunk_mask_t(t)
        m_sc[...] = jnp.full(m_sc.shape, -jnp.inf, F32)
        l_sc[...] = jnp.zeros_like(l_sc)
        acc_sc[...] = jnp.zeros_like(acc_sc)

        ones = jnp.ones((8, t), _MXU)

        def scores(j, buf):
            kb = k_ref[0, pl.ds(pl.multiple_of(j * t, t), t), :]
            for h in range(HG):
                cols = slice(128 * h, 128 * h + 128)
                buf[h] = (_mm_nt(kb[:, cols], q_ref[0, :, cols]) * MLA_C2).astype(_MXU)

        def absorb(j, buf, masked):
            vtb = vt_ref[0, j]
            for pr in range(NP):
                alphas, pvs = [], []
                for hh in range(2):
                    h = 2 * pr + hh
                    s = buf[h]
                    if masked:
                        s = jnp.where(mask, s, jnp.full_like(s, -jnp.inf))
                    m_old = m_sc[h]
                    m_new = jnp.maximum(m_old, jnp.max(s, axis=0, keepdims=True).astype(F32))
                    alpha = jnp.exp2(m_old - m_new)
                    p = jnp.exp2(s - m_new.astype(_MXU))
                    l_sc[h] = alpha * l_sc[h] + _mm(ones, p)[0:1, :]
                    m_sc[h] = m_new
                    vth = vtb[128 * pr:128 * pr + 128, :]
                    vth = jnp.where(low if hh == 0 else ~low, vth, jnp.zeros_like(vth))
                    pvs.append(_mm(vth, p))
                    alphas.append(alpha)
                acc_sc[pr] = acc_sc[pr] * jnp.where(low, alphas[0], alphas[1]) + pvs[0] + pvs[1]

        scores(0, sb)

        def pair(jj, carry):
            j0 = 2 * jj
            scores(j0 + 1, sa)
            absorb(j0, sb, False)
            scores(j0 + 2, sb)
            absorb(j0 + 1, sa, False)
            return carry

        lax.fori_loop(0, i // 2, pair, 0)

        @pl.when(i % 2 == 1)
        def _():
            scores(i, sa)
            absorb(i - 1, sb, False)
            absorb(i, sa, True)

        @pl.when(i % 2 == 0)
        def _():
            absorb(i, sb, True)

        for pr in range(NP):
            l_e, l_o = l_sc[2 * pr], l_sc[2 * pr + 1]
            o_ref[0, :, 128 * pr:128 * pr + 128] = (acc_sc[pr] / jnp.where(low, l_e, l_o)).T
            lse_ref[0, pr, 0, 0:1, :] = m_sc[2 * pr] + jnp.log(l_e) * LOG2E
            lse_ref[0, pr, 0, 1:2, :] = m_sc[2 * pr + 1] + jnp.log(l_o) * LOG2E

    return pl.pallas_call(
        body, name="mla_attn_fwd", grid=(B, 8 // HG, nq),
        in_specs=[pl.BlockSpec((1, t, 128 * HG), lambda b, g, i: (b, i, g)),
                  pl.BlockSpec((1, S, 128 * HG), lambda b, g, i: (b, 0, g)),
                  pl.BlockSpec((1, nq, 64 * HG, t), lambda b, g, i: (b, 0, g, 0))],
        out_specs=[pl.BlockSpec((1, t, 64 * HG), lambda b, g, i: (b, i, g)),
                   pl.BlockSpec((1, NP, 1, 2, t), lambda b, g, i: (b, g, i, 0, 0))],
        out_shape=[jax.ShapeDtypeStruct((B, S, 512), F32), jax.ShapeDtypeStruct((B, 4, nq, 2, t), F32)],
        scratch_shapes=[pltpu.VMEM((HG, t, t), _MXU), pltpu.VMEM((HG, t, t), _MXU), pltpu.VMEM((HG, 1, t), F32),
                        pltpu.VMEM((HG, 1, t), F32), pltpu.VMEM((NP, 128, t), F32)],
        compiler_params=_cp(("parallel", "parallel", "arbitrary")),
    )(q, k, vt)


def _mla_attn_bwd(q, k, v, kt, do, lse, dl):
    B, S, _ = q.shape
    t = min(MLA_T, S)
    nk = S // t

    HG = MLA_HG
    NP = HG // 2

    def body(q_ref, k_ref, v_ref, kt_ref, do_ref, lse_ref, dl_ref, dq_ref, dk_ref, dv_ref,
             sa, da, sb, db, dqt_sc, dk_sc, dv_sc):
        j = pl.program_id(2)

        @pl.when(j == 0)
        def _():
            dqt_sc[...] = jnp.zeros_like(dqt_sc)

        dk_sc[...] = jnp.zeros_like(dk_sc)
        dv_sc[...] = jnp.zeros_like(dv_sc)
        lane = lax.broadcasted_iota(jnp.int32, (1, 128), 1)
        low = lane < 64
        mask = _chunk_mask_t(t)

        def half(x, hh):
            return jnp.where(low if hh == 0 else ~low, x, jnp.zeros_like(x))

        def prepare(i, sbuf, dbuf):
            rows = pl.ds(pl.multiple_of(i * t, t), t)
            for h in range(HG):
                cols = slice(128 * h, 128 * h + 128)
                pc = slice(128 * (h // 2), 128 * (h // 2) + 128)
                sbuf[h] = _mm_nt(k_ref[0, :, cols], q_ref[0, rows, cols]) * MLA_C2
                dbuf[h] = _mm_nt(half(v_ref[0, :, pc], h % 2), do_ref[0, rows, pc])

        def absorb(i, sbuf, dbuf, masked):
            rows = pl.ds(pl.multiple_of(i * t, t), t)
            for h in range(HG):
                pr, hh = h // 2, h % 2
                cols = slice(128 * h, 128 * h + 128)
                pc = slice(128 * pr, 128 * pr + 128)
                p = jnp.exp2(sbuf[h] - lse_ref[0, pr, i][hh:hh + 1, :])
                if masked:
                    p = jnp.where(mask, p, 0.0)
                dv_sc[pr] += _mm(p, half(do_ref[0, rows, pc], hh))
                ds = p * (dbuf[h] - dl_ref[0, pr, i][hh:hh + 1, :])
                dqt_sc[i, cols, :] += _mm(kt_ref[0, 0, cols, :], ds)
                dk_sc[h] += _mm(ds, q_ref[0, rows, cols])

        n = nk - 1 - j
        prepare(jnp.minimum(j + 1, nk - 1), sb, db)

        def pair(jj, carry):
            i0 = j + 1 + 2 * jj
            prepare(i0 + 1, sa, da)
            absorb(i0, sb, db, False)
            prepare(jnp.where(i0 + 2 <= nk - 1, i0 + 2, j), sb, db)
            absorb(i0 + 1, sa, da, False)
            return carry

        lax.fori_loop(0, n // 2, pair, 0)

        @pl.when(n % 2 == 1)
        def _():
            prepare(j, sa, da)
            absorb(nk - 1, sb, db, False)
            absorb(j, sa, da, True)

        @pl.when(n % 2 == 0)
        def _():
            absorb(j, sb, db, True)

        for h in range(HG):
            dk_ref[0, :, 128 * h:128 * h + 128] = (dk_sc[h] * MLA_SCALE).astype(_MXU)
        for pr in range(NP):
            dv_ref[0, :, 128 * pr:128 * pr + 128] = dv_sc[pr].astype(_MXU)

        @pl.when(j == nk - 1)
        def _():
            for i in range(nk):
                dq_ref[0, i * t:(i + 1) * t, :] = (dqt_sc[i].T * MLA_SCALE).astype(_MXU)

    seq = lambda w: pl.BlockSpec((1, S, w), lambda b, g, j: (b, 0, g))
    blk = lambda w: pl.BlockSpec((1, t, w), lambda b, g, j: (b, j, g))
    stat = pl.BlockSpec((1, NP, nk, 2, t), lambda b, g, j: (b, g, 0, 0, 0))
    return pl.pallas_call(
        body, name="mla_attn_bwd", grid=(B, 8 // HG, nk),
        in_specs=[seq(128 * HG), blk(128 * HG), blk(64 * HG),
                  pl.BlockSpec((1, 1, 128 * HG, t), lambda b, g, j: (b, j, g, 0)), seq(64 * HG), stat, stat],
        out_specs=[seq(128 * HG), blk(128 * HG), blk(64 * HG)],
        out_shape=[jax.ShapeDtypeStruct((B, S, 1024), _MXU), jax.ShapeDtypeStruct((B, S, 1024), _MXU),
                   jax.ShapeDtypeStruct((B, S, 512), _MXU)],
        scratch_shapes=[pltpu.VMEM((HG, t, t), F32), pltpu.VMEM((HG, t, t), F32), pltpu.VMEM((HG, t, t), F32),
                        pltpu.VMEM((HG, t, t), F32), pltpu.VMEM((nk, 128 * HG, t), F32),
                        pltpu.VMEM((HG, t, 128), F32), pltpu.VMEM((NP, t, 128), F32)],
        compiler_params=_cp(("parallel", "parallel", "arbitrary"), 56),
    )(q, k, v, kt, do, lse, dl)


def _mla_prep_bwd(mla_p, cos, sin, qnw, kvnw, w_uq, w_ukv, dq, dk, dv):
    B, S, _ = mla_p.shape
    tm = min(S, 512)

    def body(p_ref, c_ref, s_ref, qn_ref, kn_ref, wq_ref, wkv_ref, dq_ref, dk_ref, dv_ref,
             dp_ref, dwq_ref, dwkv_ref, dqn_ref, dkn_ref):
        first = (pl.program_id(0) == 0) & (pl.program_id(1) == 0)

        @pl.when(first)
        def _():
            dwq_ref[...] = jnp.zeros_like(dwq_ref)
            dwkv_ref[...] = jnp.zeros_like(dwkv_ref)
            dqn_ref[...] = jnp.zeros_like(dqn_ref)
            dkn_ref[...] = jnp.zeros_like(dkn_ref)

        p = p_ref[0].astype(F32)
        cs, sn = c_ref[0], s_ref[0]
        lane = lax.broadcasted_iota(jnp.int32, (1, 128), 1)
        pe = (lane >= 64) & (lane < 96)
        qh, q_rstd, qn = _rms(p[:, 0:256], qn_ref[...])
        kvh, kv_rstd, kvn = _rms(p[:, 256:384], kn_ref[...])
        dqv = dq_ref[0].astype(F32)
        dkv = dk_ref[0].astype(F32)
        dqpre = jnp.concatenate(
            [_rope128_t(dqv[:, 128 * h:128 * h + 128], cs, sn) for h in range(8)], axis=1)
        dkpe = jnp.zeros((tm, 128), F32)
        for h in range(8):
            dkpe = dkpe + jnp.where(pe, dkv[:, 128 * h:128 * h + 128], 0.0)
        dkr = _rope128_t(dkpe, cs, sn)
        dkv_all = jnp.concatenate([dkv, dv_ref[0].astype(F32)], axis=1)
        d_qn = _mm_nt(dqpre, wq_ref[...])
        d_kvn = _mm_nt(dkv_all, wkv_ref[...])
        dwq_ref[...] += _mm_tn(qn, dqpre)
        dwkv_ref[...] += _mm_tn(kvn, dkv_all)
        dqn_ref[...] += jnp.sum(d_qn * qh, axis=0, keepdims=True)
        dkn_ref[...] += jnp.sum(d_kvn * kvh, axis=0, keepdims=True)
        dp_ref[0] = jnp.concatenate([_rms_bwd(d_qn, qh, q_rstd, qn_ref[...]),
                                     _rms_bwd(d_kvn, kvh, kv_rstd, kn_ref[...]), dkr], axis=1).astype(_MXU)

    tok = lambda w: pl.BlockSpec((1, tm, w), lambda b, i: (b, i, 0))
    return pl.pallas_call(
        body, name="mla_prep_bwd", grid=(B, S // tm),
        in_specs=[tok(512), tok(128), tok(128), _full((1, 256)), _full((1, 128)), _full((256, 1024)),
                  _full((128, 1536)), tok(1024), tok(1024), tok(512)],
        out_specs=[tok(512), _full((256, 1024)), _full((128, 1536)), _full((1, 256)), _full((1, 128))],
        out_shape=[jax.ShapeDtypeStruct((B, S, 512), _MXU), jax.ShapeDtypeStruct((256, 1024), F32),
                   jax.ShapeDtypeStruct((128, 1536), F32), jax.ShapeDtypeStruct((1, 256), F32),
                   jax.ShapeDtypeStruct((1, 128), F32)],
        compiler_params=_cp(("arbitrary", "arbitrary")),
    )(mla_p, cos, sin, qnw, kvnw, w_uq, w_ukv, dq, dk, dv)


def _out_fwd(x, gate, r_g, o_mla, mla_p, g_g, w_out):
    B, S, D = x.shape
    tm = min(S, 512)

    def body(x_ref, g_ref, r_ref, o_ref, z_ref, gg_ref, w_ref, xn_ref, y_ref):
        mm = (o_ref[0] * _silu(z_ref[0].astype(F32))).astype(_MXU)
        y = (jnp.dot(r_ref[0], w_ref[0:256, :], preferred_element_type=F32)
             + jnp.dot(mm, w_ref[256:768, :], preferred_element_type=F32)
             + jnp.dot(gg_ref[0], w_ref[768:1024, :], preferred_element_type=F32))
        y_ref[0] = y.astype(_MXU)
        xn_ref[0] = x_ref[0] + g_ref[0] * y

    tok = lambda w, c=0: pl.BlockSpec((1, tm, w), lambda b, i: (b, i, c))
    return pl.pallas_call(
        body, name="out_fwd", grid=(B, S // tm),
        in_specs=[tok(D), pl.BlockSpec((1, 1, D), lambda b, i: (b, 0, 0)), tok(256), tok(512), tok(512, 1),
                  tok(256), _full((D, D))],
        out_specs=[tok(D), tok(D)],
        out_shape=[jax.ShapeDtypeStruct((B, S, D), F32), jax.ShapeDtypeStruct((B, S, D), _MXU)],
        compiler_params=_cp(("parallel", "parallel")),
    )(x, gate, r_g, o_mla, mla_p, g_g, w_out)


def _out_bwd(dx, y, gate, r_g, g_g, w_out, o_mla, mla_p, after=None):
    B, S, D = dx.shape
    tm = min(S, 512)
    t = min(MLA_T, S)
    nt = tm // t

    def body(dx_ref, y_ref, g_ref, r_ref, gg_ref, w_ref, o_ref, z_ref,
             dr_ref, do_ref, dz_ref, dl_ref, dg_ref, dw_ref, dgate_ref, acc):
        first = (pl.program_id(0) == 0) & (pl.program_id(1) == 0)

        @pl.when(first)
        def _():
            acc[...] = jnp.zeros_like(acc)

        @pl.when(pl.program_id(1) == 0)
        def _():
            dgate_ref[...] = jnp.zeros_like(dgate_ref)

        dxv = dx_ref[0]
        dgate_ref[0] += jnp.sum(dxv * y_ref[0].astype(F32), axis=0, keepdims=True)
        dy = (dxv * g_ref[0]).astype(_MXU)
        dr_ref[0] = _mm_nt(dy, w_ref[0:256, :])
        dg_ref[0] = _mm_nt(dy, w_ref[768:1024, :])
        ov, z = o_ref[0], z_ref[0].astype(F32)
        acc[0:256, :] += _mm_tn(r_ref[0], dy)
        acc[256:768, :] += _mm_tn((ov * _silu(z)).astype(_MXU), dy)
        acc[768:1024, :] += _mm_tn(gg_ref[0], dy)

        @pl.when((pl.program_id(0) == B - 1) & (pl.program_id(1) == S // tm - 1))
        def _():
            dw_ref[...] = acc[...].astype(_MXU)

        dm = _mm_nt(dy, w_ref[256:768, :])
        do = dm * _silu(z)
        dz_ref[0] = (dm * ov * _dsilu(z)).astype(_MXU)
        do_ref[0] = do.astype(_MXU)
        prod = do * ov
        for pr in range(4):
            pt = prod[:, 128 * pr:128 * pr + 128].T
            se = jnp.sum(pt[0:64], axis=0, keepdims=True)
            so = jnp.sum(pt[64:128], axis=0, keepdims=True)
            for n in range(nt):
                dl_ref[0, pr, n, 0:1, :] = se[:, n * t:(n + 1) * t]
                dl_ref[0, pr, n, 1:2, :] = so[:, n * t:(n + 1) * t]

    tok = lambda w, c=0: pl.BlockSpec((1, tm, w), lambda b, i: (b, i, c))
    per_seq = pl.BlockSpec((1, 1, D), lambda b, i: (b, 0, 0))
    body, extra, extra_specs = _behind(body, 8, after)
    return pl.pallas_call(
        body, name="out_bwd", grid=(B, S // tm),
        in_specs=[tok(D), tok(D), per_seq, tok(256), tok(256), _full((D, D)), tok(512), tok(512, 1)] + extra_specs,
        out_specs=[tok(256), tok(512), tok(512), pl.BlockSpec((1, 4, nt, 2, t), lambda b, i: (b, 0, i, 0, 0)),
                   tok(256), _full((D, D)), per_seq],
        out_shape=[jax.ShapeDtypeStruct((B, S, 256), F32), jax.ShapeDtypeStruct((B, S, 512), _MXU),
                   jax.ShapeDtypeStruct((B, S, 512), _MXU), jax.ShapeDtypeStruct((B, 4, S // t, 2, t), F32),
                   jax.ShapeDtypeStruct((B, S, 256), F32), jax.ShapeDtypeStruct((D, D), _MXU),
                   jax.ShapeDtypeStruct((B, 1, D), F32)],
        scratch_shapes=[pltpu.VMEM((D, D), F32)],
        compiler_params=_cp(("arbitrary", "arbitrary")),
    )(dx, y, gate, r_g, g_g, w_out, o_mla, mla_p, *extra)


def _proj_bwd_x(x, shift, scale, nw, w_arr, d_ret, d_mla, d_mz, d_gla, dx_out, after=None):
    B, S, D = x.shape
    tm = min(S, 512)

    def body(x_ref, sc_ref, nw_ref, w_ref, dr_ref, dm_ref, dz_ref, dg_ref, dxo_ref,
             dx_ref, dsh_ref, dsc_ref, dnw_ref):
        first = (pl.program_id(0) == 0) & (pl.program_id(1) == 0)

        @pl.when(first)
        def _():
            dnw_ref[...] = jnp.zeros_like(dnw_ref)

        @pl.when(pl.program_id(1) == 0)
        def _():
            dsh_ref[...] = jnp.zeros_like(dsh_ref)
            dsc_ref[...] = jnp.zeros_like(dsc_ref)

        dp = jnp.concatenate([dr_ref[0], dm_ref[0], dz_ref[0], dg_ref[0]], axis=1)
        dh = lax.dot_general(dp, w_ref[...], (((1,), (1,)), ((), ())), preferred_element_type=F32)
        xv = x_ref[0]
        rstd = lax.rsqrt(jnp.mean(xv * xv, axis=-1, keepdims=True) + EPS)
        xh = xv * rstd
        nwv = nw_ref[...]
        mod = 1.0 + sc_ref[0]
        dsh_ref[0] += jnp.sum(dh, axis=0, keepdims=True)
        dsc_ref[0] += jnp.sum(dh * xh * nwv, axis=0, keepdims=True)
        dnw_ref[...] += jnp.sum(dh * xh * mod, axis=0, keepdims=True)
        dxh = dh * nwv * mod
        dx_ref[0] = dxo_ref[0] + rstd * (dxh - xh * jnp.mean(dxh * xh, axis=-1, keepdims=True))

    tok = lambda w: pl.BlockSpec((1, tm, w), lambda b, i: (b, i, 0))
    per_seq = pl.BlockSpec((1, 1, D), lambda b, i: (b, 0, 0))
    body, extra, extra_specs = _behind(body, 9, after)
    return pl.pallas_call(
        body, name="proj_bwd_x", grid=(B, S // tm),
        in_specs=[tok(D), per_seq, _full((1, D)), _full((D, ARR_W)), tok(RET_W), tok(512), tok(512),
                  tok(GLA_W), tok(D)] + extra_specs,
        out_specs=[tok(D), per_seq, per_seq, _full((1, D))],
        out_shape=[jax.ShapeDtypeStruct((B, S, D), F32), jax.ShapeDtypeStruct((B, 1, D), F32),
                   jax.ShapeDtypeStruct((B, 1, D), F32), jax.ShapeDtypeStruct((1, D), F32)],
        compiler_params=_cp(("arbitrary", "arbitrary")),
    )(x, scale, nw, w_arr, d_ret, d_mla, d_mz, d_gla, dx_out, *extra)


def _proj_bwd_w(h, d_ret, d_mla, d_mz, d_gla):
    B, S, D = h.shape
    tm = min(S, 512)

    def body(h_ref, dr_ref, dm_ref, dz_ref, dg_ref, dw_ref, acc):
        first = (pl.program_id(0) == 0) & (pl.program_id(1) == 0)

        @pl.when(first)
        def _():
            acc[...] = jnp.zeros_like(acc)

        hv = h_ref[0]
        tn = lambda d_ref: lax.dot_general(hv, d_ref[0], (((0,), (0,)), ((), ())), preferred_element_type=F32)
        acc[:, 0:RET_W] += tn(dr_ref)
        acc[:, RET_W:RET_W + 512] += tn(dm_ref)
        acc[:, RET_W + 512:RET_W + MLA_W] += tn(dz_ref)
        acc[:, RET_W + MLA_W:ARR_W] += tn(dg_ref)

        @pl.when((pl.program_id(0) == B - 1) & (pl.program_id(1) == S // tm - 1))
        def _():
            dw_ref[...] = acc[...].astype(_MXU)

    tok = lambda w: pl.BlockSpec((1, tm, w), lambda b, i: (b, i, 0))
    return pl.pallas_call(
        body, name="proj_bwd_w", grid=(B, S // tm),
        in_specs=[tok(D), tok(RET_W), tok(512), tok(512), tok(GLA_W)],
        out_specs=_full((D, ARR_W)), out_shape=jax.ShapeDtypeStruct((D, ARR_W), _MXU),
        scratch_shapes=[pltpu.VMEM((D, ARR_W), F32)],
        compiler_params=_cp(("arbitrary", "arbitrary"), 56),
    )(h, d_ret, d_mla, d_mz, d_gla)


def _out_fwd_loss(x, gate, r_g, o_mla, mla_p, g_g, w_out, fw, target):
    B, S, D = x.shape
    tm = min(S, 512)

    def body(x_ref, g_ref, r_ref, o_ref, z_ref, gg_ref, w_ref, fw_ref, t_ref, dx_ref, y_ref, loss_ref, dfw_ref):
        first = (pl.program_id(0) == 0) & (pl.program_id(1) == 0)

        @pl.when(first)
        def _():
            loss_ref[...] = jnp.zeros_like(loss_ref)
            dfw_ref[...] = jnp.zeros_like(dfw_ref)

        mm = (o_ref[0] * _silu(z_ref[0].astype(F32))).astype(_MXU)
        y = (jnp.dot(r_ref[0], w_ref[0:256, :], preferred_element_type=F32)
             + jnp.dot(mm, w_ref[256:768, :], preferred_element_type=F32)
             + jnp.dot(gg_ref[0], w_ref[768:1024, :], preferred_element_type=F32))
        y_ref[0] = y.astype(_MXU)
        xv = x_ref[0] + g_ref[0] * y
        fwv = fw_ref[...]
        rstd = lax.rsqrt(jnp.mean(xv * xv, axis=-1, keepdims=True) + EPS)
        xh = xv * rstd
        err = xh * fwv - t_ref[0]
        loss_ref[...] += 0.5 * jnp.sum(jnp.mean(err * err, axis=-1, keepdims=True), axis=0, keepdims=True)
        dy = err * (1.0 / D)
        dfw_ref[...] += jnp.sum(dy * xh, axis=0, keepdims=True)
        dxh = dy * fwv
        dx_ref[0] = rstd * (dxh - xh * jnp.mean(dxh * xh, axis=-1, keepdims=True))

    tok = lambda w, c=0: pl.BlockSpec((1, tm, w), lambda b, i: (b, i, c))
    return pl.pallas_call(
        body, name="out_fwd_loss", grid=(B, S // tm),
        in_specs=[tok(D), pl.BlockSpec((1, 1, D), lambda b, i: (b, 0, 0)), tok(256), tok(512), tok(512, 1),
                  tok(256), _full((D, D)), _full((1, D)), tok(D)],
        out_specs=[tok(D), tok(D), _full((1, 1)), _full((1, D))],
        out_shape=[jax.ShapeDtypeStruct((B, S, D), F32), jax.ShapeDtypeStruct((B, S, D), _MXU),
                   jax.ShapeDtypeStruct((1, 1), F32), jax.ShapeDtypeStruct((1, D), F32)],
        compiler_params=_cp(("arbitrary", "arbitrary")),
    )(x, gate, r_g, o_mla, mla_p, g_g, w_out, fw, target)


def _local_step(x, pos3, mod, loss_target, small, w_in_a, w_uq_a, w_ukv_a, w_out_b):
    B, S, D = x.shape
    tabs = _rope_tables(pos3)
    saved = []
    for l in range(DEPTH):
        last = (small["final_norm"].reshape(1, D), loss_target) if l == DEPTH - 1 else None
        x, s = _layer_fwd(x, tabs, mod[l], {n: a[l] for n, a in small.items() if n != "final_norm"},
                          w_in_a[l], w_uq_a[l], w_ukv_a[l], w_ukv_a[l].T, w_out_b[l], loss_head=last)
        saved.append(s)
    dx, loss, d_fw = x
    grads = dict(final_norm=d_fw.reshape(D))
    per_layer = [None] * DEPTH
    for l in reversed(range(DEPTH)):
        dx, per_layer[l] = _layer_bwd(dx, saved[l], tabs)
    for name in per_layer[0]:
        grads[name] = jnp.stack([per_layer[l][name] for l in range(DEPTH)])
    return loss, dx, grads


def _layer_fwd(x, tabs, mod_l, small_l, w_in_a, w_uq_a=None, w_ukv_a=None, w_ukv_t=None, w_out_b=None, late_weights=None,
               loss_head=None):
    B, S, D = x.shape
    cr, sr, cm, sm = tabs
    shift = mod_l[:, 0:D].reshape(B, 1, D)
    scale = mod_l[:, D:2 * D].reshape(B, 1, D)
    gate = mod_l[:, 2 * D:3 * D].reshape(B, 1, D)
    nw = small_l["norm_w"].reshape(1, D)
    qnw = small_l["mla_q_norm"].reshape(1, 256)
    kvnw = small_l["mla_kv_norm"].reshape(1, 128)
    w_g2p = jnp.pad(small_l["gla_w_g2"], ((0, 112), (0, 0)))
    b_g2 = small_l["gla_b_g2"].reshape(1, 128)
    gnw = jnp.tile(small_l["gla_norm"], 4).reshape(1, 256)
    ret_p, mla_p, gla_p, h = _proj_fwd(x, shift, scale, nw, w_in_a)
    r_g, r_raw, r_st = _ret_fwd(ret_p, cr, sr)
    if late_weights is not None:
        w_uq_a, w_ukv_a, w_ukv_t, w_out_b = late_weights(r_raw)
    q, k, v, kt, vt = _mla_prep_fwd(mla_p, cm, sm, qnw, kvnw, w_uq_a, w_ukv_a, w_ukv_t)
    o_mla, lse = _mla_attn_fwd(q, k, vt)
    g_g, g_raw, g_st = _gla_fwd(gla_p, w_g2p, b_g2, gnw)
    if loss_head is None:
        x_new, y = _out_fwd(x, gate, r_g, o_mla, mla_p, g_g, w_out_b)
    else:
        dx, y, loss, d_fw = _out_fwd_loss(x, gate, r_g, o_mla, mla_p, g_g, w_out_b, *loss_head)
        x_new = (dx, loss, d_fw)
    saved = dict(x=x, shift=shift, scale=scale, gate=gate, nw=nw, qnw=qnw, kvnw=kvnw, w_g2p=w_g2p, b_g2=b_g2,
                 gnw=gnw, ret_p=ret_p, mla_p=mla_p, gla_p=gla_p, h=h, r_g=r_g, r_raw=r_raw, r_st=r_st, q=q, k=k,
                 v=v, kt=kt, o_mla=o_mla, lse=lse, g_g=g_g, g_raw=g_raw, g_st=g_st, y=y,
                 w_in_a=w_in_a, w_uq_a=w_uq_a, w_ukv_a=w_ukv_a, w_out_b=w_out_b)
    return x_new, saved


def _layer_bwd(dx, s, tabs, after=None, early_grads=None, early_w_in=None):
    B, S, D = dx.shape
    cr, sr, cm, sm = tabs
    d_r, do, d_mz, dl, d_g, dw_out, d_gate = _out_bwd(dx, s["y"], s["gate"], s["r_g"], s["g_g"], s["w_out_b"],
                                                      s["o_mla"], s["mla_p"], after=after)
    d_ret = _ret_bwd(s["ret_p"], cr, sr, s["r_raw"], s["r_st"], d_r)
    dq, dk, dv = _mla_attn_bwd(s["q"], s["k"], s["v"], s["kt"], do, s["lse"], dl)
    d_mla, dw_uq, dw_ukv, d_qnw, d_kvnw = _mla_prep_bwd(
        s["mla_p"], cm, sm, s["qnw"], s["kvnw"], s["w_uq_a"], s["w_ukv_a"], dq, dk, dv)
    sent = None if early_grads is None else early_grads(dw_out, dw_uq, dw_ukv)
    d_gla, dw_g2p, db_g2, d_gnw = _gla_bwd(s["gla_p"], s["w_g2p"], s["b_g2"], s["gnw"], s["g_raw"], s["g_st"], d_g,
                                           after=sent)
    dw_in = _proj_bwd_w(s["h"], d_ret, d_mla, d_mz, d_gla)
    sent = None if early_w_in is None else early_w_in(dw_in)
    dx, d_shift, d_scale, d_nw = _proj_bwd_x(s["x"], s["shift"], s["scale"], s["nw"], s["w_in_a"],
                                             d_ret, d_mla, d_mz, d_gla, dx, after=sent)
    grads = dict(
        d_mod=jnp.concatenate([d_shift, d_scale, d_gate], axis=2).reshape(B, 3 * D),
        norm_w=d_nw.reshape(D), mla_q_norm=d_qnw.reshape(256), mla_kv_norm=d_kvnw.reshape(128),
        gla_w_g2=dw_g2p[0:16], gla_b_g2=db_g2.reshape(128), gla_norm256=d_gnw.reshape(256),
        w_in_a=dw_in, w_uq_a=dw_uq, w_ukv_a=dw_ukv, w_out=dw_out)
    return dx, grads


def _exchange(arrs, gather, name):
    n = len(arrs)
    out_shape = [jax.ShapeDtypeStruct(((N_DEV,) + a.shape) if g else a.shape, a.dtype)
                 for a, g in zip(arrs, gather)]

    def body(*refs):
        ins, outs = refs[:n], refs[n:2 * n]
        send_sems, recv_sems, local_sems = refs[2 * n:]
        ix, iy, ic = lax.axis_index("x"), lax.axis_index("y"), lax.axis_index("c")
        me = 4 * ix + 2 * iy + ic
        copies = []
        for a in range(n):
            mine = ins[a] if gather[a] else ins[a].at[me]
            loc = pltpu.make_async_copy(mine, outs[a].at[me], local_sems.at[a])
            loc.start()
            copies.append(loc)
            for d in range(1, N_DEV):
                px = 1 - ix if d & 4 else ix
                py = 1 - iy if d & 2 else iy
                pc = 1 - ic if d & 1 else ic
                src = ins[a] if gather[a] else ins[a].at[4 * px + 2 * py + pc]
                cp = pltpu.make_async_remote_copy(
                    src_ref=src, dst_ref=outs[a].at[me], send_sem=send_sems.at[a, d - 1],
                    recv_sem=recv_sems.at[a, d - 1], device_id=(px, py, pc), device_id_type=pl.DeviceIdType.MESH)
                cp.start()
                copies.append(cp)
        for cp in copies:
            cp.wait()

    any_spec = pl.BlockSpec(memory_space=pl.ANY)
    outs = pl.pallas_call(
        body, name=name, in_specs=[any_spec] * n, out_specs=[any_spec] * n, out_shape=out_shape,
        scratch_shapes=[pltpu.SemaphoreType.DMA((n, N_DEV - 1)), pltpu.SemaphoreType.DMA((n, N_DEV - 1)),
                        pltpu.SemaphoreType.DMA((n,))],
    )(*arrs)
    return list(outs)


def _peers(ix, iy, ic):
    out = []
    for d in range(1, N_DEV):
        px = 1 - ix if d & 4 else ix
        py = 1 - iy if d & 2 else iy
        pc = 1 - ic if d & 1 else ic
        out.append((d - 1, (px, py, pc), 4 * px + 2 * py + pc))
    return out


def _exchange_start(arrs, gather, name, after=None):
    n = len(arrs)
    lands = [lax.empty(((N_DEV,) + a.shape) if g else a.shape, a.dtype) for a, g in zip(arrs, gather)]
    extra = [] if after is None else [after]

    def body(*refs):
        ins, land_refs = refs[:n], refs[n:2 * n]
        send_sems, recv_sems = refs[2 * n + len(extra)], refs[2 * n + len(extra) + 1]
        token = refs[-1]
        ix, iy, ic = lax.axis_index("x"), lax.axis_index("y"), lax.axis_index("c")
        me = 4 * ix + 2 * iy + ic
        for a in range(n):
            for k, peer, peer_idx in _peers(ix, iy, ic):
                pltpu.make_async_remote_copy(
                    src_ref=ins[a] if gather[a] else ins[a].at[peer_idx], dst_ref=land_refs[a].at[me],
                    send_sem=send_sems.at[7 * a + k], recv_sem=recv_sems.at[7 * a + k], device_id=peer,
                    device_id_type=pl.DeviceIdType.MESH).start()
        token[...] = jnp.zeros_like(token)

    hbm = pl.BlockSpec(memory_space=pltpu.HBM)
    sem = pl.BlockSpec(memory_space=pltpu.SEMAPHORE)
    held = [pltpu.with_memory_space_constraint(a, pltpu.HBM) for a in list(arrs) + lands]
    outs = pl.pallas_call(
        body, name=name,
        out_shape=(pltpu.SemaphoreType.DMA((7 * n,)), pltpu.SemaphoreType.DMA((7 * n,)),
                   *[pltpu.HBM(a.shape, a.dtype) for a in held], jax.ShapeDtypeStruct((8, 128), F32)),
        in_specs=[hbm] * (2 * n) + [pl.BlockSpec(memory_space=pl.ANY)] * len(extra),
        out_specs=(sem, sem, *[hbm] * (2 * n), pl.BlockSpec(memory_space=pltpu.VMEM)),
        input_output_aliases={a: 2 + a for a in range(2 * n)},
        compiler_params=pltpu.CompilerParams(has_side_effects=pltpu.SideEffectType.DATAFLOW_SIDE_EFFECTING),
    )(*held, *extra)
    return dict(send=outs[0], recv=outs[1], srcs=list(outs[2:2 + n]), lands=list(outs[2 + n:2 + 2 * n]),
                token=outs[-1], gather=list(gather))


def _exchange_wait(flight, after, me, name):
    n = len(flight["srcs"])
    gather = flight["gather"]

    def body(*refs):
        srcs, land_refs = refs[:n], refs[n:2 * n]
        send_sems, recv_sems = refs[2 * n], refs[2 * n + 1]
        ix, iy, ic = lax.axis_index("x"), lax.axis_index("y"), lax.axis_index("c")
        mine = 4 * ix + 2 * iy + ic
        for a in range(n):
            for k, peer, peer_idx in _peers(ix, iy, ic):
                cp = pltpu.make_async_remote_copy(
                    src_ref=srcs[a] if gather[a] else srcs[a].at[peer_idx], dst_ref=land_refs[a].at[mine],
                    send_sem=send_sems.at[7 * a + k], recv_sem=recv_sems.at[7 * a + k], device_id=peer,
                    device_id_type=pl.DeviceIdType.MESH)
                cp.wait_send()
                cp.wait_recv()

    hbm = pl.BlockSpec(memory_space=pltpu.HBM)
    sem = pl.BlockSpec(memory_space=pltpu.SEMAPHORE)
    held = flight["srcs"] + flight["lands"]
    outs = pl.pallas_call(
        body, name=name, out_shape=tuple(pltpu.HBM(a.shape, a.dtype) for a in held),
        in_specs=[hbm] * (2 * n) + [sem, sem, pl.BlockSpec(memory_space=pl.ANY)], out_specs=tuple([hbm] * (2 * n)),
        input_output_aliases={a: a for a in range(2 * n)},
        compiler_params=pltpu.CompilerParams(has_side_effects=pltpu.SideEffectType.DATAFLOW_SIDE_EFFECTING),
    )(*held, flight["send"], flight["recv"], after)
    got = []
    for a in range(n):
        src, land = outs[a], outs[n + a]
        own = src if gather[a] else lax.dynamic_index_in_dim(src, me, axis=0, keepdims=False)
        got.append(lax.dynamic_update_index_in_dim(land, own, me, axis=0))
    return got


def _ada_fwd(c_all, ada_w, ada_b_cols):
    nb, D = c_all.shape
    cols = ada_w.shape[2]

    def body(c_ref, w_ref, b_ref, out_ref):
        ca = _silu(c_ref[...])
        for l in range(DEPTH):
            out_ref[l] = _mm(ca, w_ref[l]) + b_ref[l:l + 1, :]

    return pl.pallas_call(
        body, name="ada_fwd", out_shape=jax.ShapeDtypeStruct((DEPTH, nb, cols), F32),
        in_specs=[pl.BlockSpec(memory_space=pltpu.VMEM)] * 3, out_specs=pl.BlockSpec(memory_space=pltpu.VMEM),
        compiler_params=pltpu.CompilerParams(vmem_limit_bytes=32 * VMEM_MB),
    )(c_all, ada_w, ada_b_cols)


def _ada_bwd(c_all, d_mod_cols):
    nb, D = c_all.shape
    cols = d_mod_cols.shape[2]

    def body(c_ref, dm_ref, out_ref):
        ca = _silu(c_ref[...])
        for l in range(DEPTH):
            out_ref[l] = _mm_tn(ca, dm_ref[l])

    return pl.pallas_call(
        body, name="ada_bwd", out_shape=jax.ShapeDtypeStruct((DEPTH, D, cols), F32),
        in_specs=[pl.BlockSpec(memory_space=pltpu.VMEM)] * 2, out_specs=pl.BlockSpec(memory_space=pltpu.VMEM),
        compiler_params=pltpu.CompilerParams(vmem_limit_bytes=32 * VMEM_MB),
    )(c_all, d_mod_cols)


def _sum_adamw(parts, w, m, v, name, after=None):
    P, R, C = parts.shape
    tr = 256 if (R % 256 == 0 and R > 256) else R
    extra = [] if after is None else [after]

    def body(p_ref, w_ref, m_ref, v_ref, *rest):
        g_ref, d_ref, nm_ref, nv_ref = rest[-4:]
        g = p_ref[0].astype(F32)
        for k in range(1, P):
            g = g + p_ref[k].astype(F32)
        g_ref[...] = g
        nm = ADAM_B1 * m_ref[...] + (1.0 - ADAM_B1) * g
        nv = ADAM_B2 * v_ref[...] + (1.0 - ADAM_B2) * (g * g)
        nm_ref[...] = nm
        nv_ref[...] = nv
        m_hat = nm / (1.0 - ADAM_B1 ** ADAM_STEP)
        v_hat = nv / (1.0 - ADAM_B2 ** ADAM_STEP)
        d_ref[...] = -ADAM_LR * (m_hat / (jnp.sqrt(v_hat) + ADAM_EPS) + ADAM_WD * w_ref[...])

    blk = pl.BlockSpec((tr, C), lambda i: (i, 0))
    shp = jax.ShapeDtypeStruct((R, C), F32)
    return pl.pallas_call(
        body, name=name, grid=(R // tr,),
        in_specs=[pl.BlockSpec((P, tr, C), lambda i: (0, i, 0)), blk, blk, blk]
        + [pl.BlockSpec(memory_space=pl.ANY)] * len(extra),
        out_specs=[blk, blk, blk, blk], out_shape=[shp, shp, shp, shp],
        compiler_params=_cp(("parallel",)),
    )(parts, w, m, v, *extra)


def _sum_adamw_layer(parts, w, m, v, layer, name, prev=None, after=None):
    P, R, C = parts.shape
    tr = 256 if (R % 256 == 0 and R > 256) else R

    def body(p_ref, w_ref, m_ref, v_ref, *rest):
        g_ref, d_ref, nm_ref, nv_ref = rest[-4:]
        g = p_ref[0].astype(F32)
        for k in range(1, P):
            g = g + p_ref[k].astype(F32)
        g_ref[0] = g
        nm = ADAM_B1 * m_ref[0] + (1.0 - ADAM_B1) * g
        nv = ADAM_B2 * v_ref[0] + (1.0 - ADAM_B2) * (g * g)
        nm_ref[0] = nm
        nv_ref[0] = nv
        m_hat = nm / (1.0 - ADAM_B1 ** ADAM_STEP)
        v_hat = nv / (1.0 - ADAM_B2 ** ADAM_STEP)
        d_ref[0] = -ADAM_LR * (m_hat / (jnp.sqrt(v_hat) + ADAM_EPS) + ADAM_WD * w_ref[0])

    blk = pl.BlockSpec((1, tr, C), lambda i: (layer, i, 0))
    shp = jax.ShapeDtypeStruct(w.shape, F32)
    in_specs = [pl.BlockSpec((P, tr, C), lambda i: (0, i, 0)), blk, blk, blk]
    args = [parts, w, m, v]
    aliases = {}
    if prev is not None:
        in_specs += [pl.BlockSpec(memory_space=pl.ANY)] * 4
        args += list(prev)
        aliases = {4 + k: k for k in range(4)}
    if after is not None:
        in_specs.append(pl.BlockSpec(memory_space=pl.ANY))
        args.append(after)
    return list(pl.pallas_call(
        body, name=name, grid=(R // tr,), in_specs=in_specs, out_specs=[blk] * 4, out_shape=[shp] * 4,
        input_output_aliases=aliases, compiler_params=_cp(("parallel",)),
    )(*args))


SMALL = ["norm_w", "mla_q_norm", "mla_kv_norm", "gla_w_g2", "gla_b_g2", "gla_norm", "final_norm"]


SMALL_ROWS = 72


def _pack_small(loss, part):
    flat = [jnp.pad(loss.reshape(1), (0, 127))] + [part[n].reshape(-1) for n in SMALL]
    used = sum(f.shape[0] for f in flat)
    flat.append(jnp.zeros((SMALL_ROWS * 128 - used,), F32))
    return jnp.concatenate(flat).reshape(SMALL_ROWS, 128)


def _small_adamw(packed_parts, w, m, v, after=None):
    n = len(w)
    extra = [] if after is None else [after]

    def body(*refs):
        p_ref = refs[0]
        w_refs, m_refs, v_refs = refs[1:1 + n], refs[1 + n:1 + 2 * n], refs[1 + 2 * n:1 + 3 * n]
        outs, acc = refs[1 + 3 * n + len(extra):-1], refs[-1]
        total = p_ref[0]
        for k in range(1, N_DEV):
            total = total + p_ref[k]
        acc[...] = total
        outs[0][...] = acc[0:1, :]
        r0 = 1
        for i in range(n):
            shp = w_refs[i].shape
            if len(shp) == 3:
                g = acc[r0:r0 + shp[0] * shp[1], :].reshape(shp)
                r0 += shp[0] * shp[1]
            elif shp[1] < 128:
                g = acc[r0:r0 + shp[0], 0:shp[1]]
                r0 += shp[0]
            else:
                k = shp[1] // 128
                g = jnp.concatenate(
                    [jnp.concatenate([acc[r0 + l * k + j:r0 + l * k + j + 1, :] for j in range(k)], axis=1)
                     for l in range(shp[0])], axis=0)
                r0 += shp[0] * k
            nm = ADAM_B1 * m_refs[i][...] + (1.0 - ADAM_B1) * g
            nv = ADAM_B2 * v_refs[i][...] + (1.0 - ADAM_B2) * (g * g)
            m_hat = nm / (1.0 - ADAM_B1 ** ADAM_STEP)
            v_hat = nv / (1.0 - ADAM_B2 ** ADAM_STEP)
            outs[1 + 4 * i][...] = g
            outs[2 + 4 * i][...] = -ADAM_LR * (m_hat / (jnp.sqrt(v_hat) + ADAM_EPS) + ADAM_WD * w_refs[i][...])
            outs[3 + 4 * i][...] = nm
            outs[4 + 4 * i][...] = nv

    vmem = pl.BlockSpec(memory_space=pltpu.VMEM)
    out_shape = [jax.ShapeDtypeStruct((1, 128), F32)]
    for a in w:
        out_shape += [jax.ShapeDtypeStruct(a.shape, F32)] * 4
    outs = pl.pallas_call(
        body, name="adamw_small", in_specs=[vmem] * (1 + 3 * n) + [pl.BlockSpec(memory_space=pl.ANY)] * len(extra),
        out_specs=[vmem] * (1 + 4 * n), out_shape=out_shape, scratch_shapes=[pltpu.VMEM((SMALL_ROWS, 128), F32)],
    )(packed_parts, *w, *m, *v, *extra)
    return outs[0], [outs[1 + 4 * i:5 + 4 * i] for i in range(n)]


WEIGHTS = ["norm_w", "ada_w", "ada_b", "w_in", "mla_q_norm", "w_uq", "mla_kv_norm", "w_ukv", "gla_w_g2",
           "gla_b_g2", "gla_norm", "w_out", "final_norm"]


def kernel(x, c, positions, norm_w, ada_w, ada_b, w_in, mla_q_norm, w_uq, mla_kv_norm, w_ukv, gla_w_g2, gla_b_g2, gla_norm, w_out, final_norm, loss_target, m_norm_w, m_ada_w, m_ada_b, m_w_in, m_mla_q_norm, m_w_uq, m_mla_kv_norm, m_w_ukv, m_gla_w_g2, m_gla_b_g2, m_gla_norm, m_w_out, m_final_norm, v_norm_w, v_ada_w, v_ada_b, v_w_in, v_mla_q_norm, v_w_uq, v_mla_kv_norm, v_w_ukv, v_gla_w_g2, v_gla_b_g2, v_gla_norm, v_w_out, v_final_norm):
    w = dict(norm_w=norm_w, ada_w=ada_w, ada_b=ada_b, w_in=w_in, mla_q_norm=mla_q_norm, w_uq=w_uq,
             mla_kv_norm=mla_kv_norm, w_ukv=w_ukv, gla_w_g2=gla_w_g2, gla_b_g2=gla_b_g2, gla_norm=gla_norm,
             w_out=w_out, final_norm=final_norm)
    m = dict(norm_w=m_norm_w, ada_w=m_ada_w, ada_b=m_ada_b, w_in=m_w_in, mla_q_norm=m_mla_q_norm, w_uq=m_w_uq,
             mla_kv_norm=m_mla_kv_norm, w_ukv=m_w_ukv, gla_w_g2=m_gla_w_g2, gla_b_g2=m_gla_b_g2,
             gla_norm=m_gla_norm, w_out=m_w_out, final_norm=m_final_norm)
    v = dict(norm_w=v_norm_w, ada_w=v_ada_w, ada_b=v_ada_b, w_in=v_w_in, mla_q_norm=v_mla_q_norm, w_uq=v_w_uq,
             mla_kv_norm=v_mla_kv_norm, w_ukv=v_w_ukv, gla_w_g2=v_gla_w_g2, gla_b_g2=v_gla_b_g2,
             gla_norm=v_gla_norm, w_out=v_w_out, final_norm=v_final_norm)
    B, S, D = x.shape
    me = 4 * lax.axis_index("x") + 2 * lax.axis_index("y") + lax.axis_index("c")
    ada_cols = ada_w.shape[2]
    cast = lambda a: a.astype(_MXU)

    sharded = ["w_in", "w_uq", "w_ukv", "w_out"]

    whole_in = _arrange_w_in
    whole_rest = lambda blks: (*_arrange_mla_weights(blks[0], blks[1]), blks[2].reshape(D, D))
    blocks_in = lambda dw_in_a: _unarrange_w_in(dw_in_a, N_DEV, w_in.shape[2])
    blocks_rest = lambda dw_out, dw_uq_a, dw_ukv_a: [
        *_unarrange_mla_weights(dw_uq_a, dw_ukv_a), dw_out.reshape(N_DEV, D // N_DEV, D).astype(jnp.bfloat16)]

    (c_g,) = _exchange([c], [True], "gather_c")
    c_all = c_g.reshape(N_DEV * B, D)

    ada_b_cols = lax.dynamic_slice(ada_b, (0, me * ada_cols), (DEPTH, ada_cols))
    mod_cols = _ada_fwd(c_all, ada_w, ada_b_cols)
    mod_send = jnp.transpose(mod_cols.reshape(DEPTH, N_DEV, B, ada_cols), (1, 0, 2, 3))
    (mod_recv,) = _exchange([mod_send], [False], "scatter_mod")
    mod = jnp.transpose(mod_recv, (1, 2, 0, 3)).reshape(DEPTH, B, 3 * D)

    flight_i = _exchange_start([cast(w_in[0])], [True], "gather_start_first", after=mod)
    flight_r = _exchange_start([cast(w[n][0]) for n in sharded[1:]], [True] * 3, "gather_start_layer0",
                               after=flight_i["token"])
    flight_w = _exchange_start([cast(w[n][1]) for n in sharded], [True] * 4, "gather_start_layer1",
                               after=flight_r["token"])
    small_w = {n: w[n] for n in SMALL}
    layer_small = lambda l: {n: a[l] for n, a in small_w.items() if n != "final_norm"}
    tabs = _rope_tables(positions.reshape(B, S, 1), flight_w["token"])
    late0 = lambda after: whole_rest(_exchange_wait(flight_r, after, me, "gather_wait_layer0"))
    (w_in0_g,) = _exchange_wait(flight_i, tabs[0], me, "gather_wait_first")
    x1, saved0 = _layer_fwd(x, tabs, mod[0], layer_small(0), whole_in(w_in0_g), late_weights=late0)
    got1 = _exchange_wait(flight_w, x1, me, "gather_wait_layer1")
    (dx, loss, d_fw), saved1 = _layer_fwd(x1, tabs, mod[1], layer_small(1), whole_in(got1[0]), *whole_rest(got1[1:]),
                                          loss_head=(final_norm.reshape(1, D), loss_target))

    dx, g1 = _layer_bwd(dx, saved1, tabs)
    flight_g = _exchange_start([blocks_in(g1["w_in_a"])] + blocks_rest(g1["w_out"], g1["w_uq_a"], g1["w_ukv_a"]),
                               [False] * 4, "grads_start_layer1")
    flights = {}

    def early0(dw_out, dw_uq_a, dw_ukv_a):
        flights["rest0"] = _exchange_start(blocks_rest(dw_out, dw_uq_a, dw_ukv_a), [False] * 3, "grads_start_layer0")
        return flights["rest0"]["token"]

    def early_in0(dw_in_a):
        flights["in0"] = _exchange_start([blocks_in(dw_in_a)], [False], "exchange_start_last")
        return flights["in0"]["token"]

    grad_x, g0 = _layer_bwd(dx, saved0, tabs, after=flight_g["token"], early_grads=early0, early_w_in=early_in0)
    parts1 = _exchange_wait(flight_g, grad_x, me, "grads_wait_layer1")
    rest0 = _exchange_wait(flights["rest0"], g0["w_in_a"], me, "grads_wait_layer0")

    both = lambda n: jnp.stack([g0[n], g1[n]])
    d_mod = both("d_mod")
    part = dict(norm_w=both("norm_w"), mla_q_norm=both("mla_q_norm"), mla_kv_norm=both("mla_kv_norm"),
                gla_w_g2=both("gla_w_g2"), gla_b_g2=both("gla_b_g2"), gla_norm=both("gla_norm256")[:, 0:128],
                final_norm=d_fw)
    flight_s = _exchange_start([d_mod, _pack_small(loss, part)], [True, True], "gather_small_start")
    flight_l = flights["in0"]
    res = {}
    behind = flight_s["token"]
    for a, name in enumerate(sharded):
        res[name] = _sum_adamw_layer(parts1[a], w[name], m[name], v[name], 1, "adamw_%s_layer1" % name, after=behind)
        behind = res[name][1]
    for a, name in enumerate(sharded[1:]):
        res[name] = _sum_adamw_layer(rest0[a], w[name], m[name], v[name], 0, "adamw_%s_layer0" % name,
                                     prev=res[name], after=behind)
        behind = res[name][1]
    (in0,) = _exchange_wait(flight_l, behind, me, "exchange_wait_last")
    res["w_in"] = _sum_adamw_layer(in0, w_in, m_w_in, v_w_in, 0, "adamw_w_in_layer0", prev=res["w_in"])
    behind = res["w_in"][1]

    d_mod_g, small_g = _exchange_wait(flight_s, behind, me, "gather_small_wait")
    d_mod_all = jnp.transpose(d_mod_g, (1, 0, 2, 3)).reshape(DEPTH, N_DEV * B, 3 * D)
    d_mod_cols = lax.dynamic_slice(d_mod_all, (0, 0, me * ada_cols), (DEPTH, N_DEV * B, ada_cols))
    g_ada_w = _ada_bwd(c_all, d_mod_cols)

    def update(name, parts2d, after):
        shp = w[name].shape
        two = lambda a: a.reshape(parts2d.shape[1:])
        out = _sum_adamw(parts2d, two(w[name]), two(m[name]), two(v[name]), "adamw_" + name, after=after)
        res[name] = [o.reshape(shp) for o in out]
        return out[1]

    behind = update("ada_w", g_ada_w.reshape(1, DEPTH * D, ada_cols), behind)
    behind = update("ada_b", jnp.transpose(d_mod_g, (0, 2, 1, 3)).reshape(N_DEV * B, DEPTH * 3 * D // 128, 128), behind)
    row = lambda a: a.reshape(1, D) if a.ndim == 1 else a
    loss_sum, small_out = _small_adamw(small_g, [row(w[n]) for n in SMALL], [row(m[n]) for n in SMALL],
                                       [row(v[n]) for n in SMALL], after=behind)
    for n, outs in zip(SMALL, small_out):
        res[n] = [o.reshape(w[n].shape) for o in outs]
    loss_out = loss_sum[0, 0]
    return (loss_out, grad_x, *[res[n][0] for n in WEIGHTS], *[res[n][1] for n in WEIGHTS],
            *[res[n][2] for n in WEIGHTS], *[res[n][3] for n in WEIGHTS])
```

```python
import functools
import math

import numpy as np
import jax
import jax.numpy as jnp
from jax import lax
from jax.experimental import pallas as pl
from jax.experimental.pallas import tpu as pltpu

F32 = jnp.float32
_MXU = jnp.bfloat16

D_MODEL = 1024
DEPTH = 2
CHUNK = 64
EPS = 1e-6
ROPE_THETA = 10000.0
N_DEV = 8

MLA_SCALE = 96.0 ** -0.5
RET_KSCALE = 64.0 ** -0.5
GLA_KSCALE = 32.0 ** -0.5
GLA_TAU = 16.0

ADAM_LR = 0.001
ADAM_B1 = 0.9
ADAM_B2 = 0.999
ADAM_EPS = 1e-08
ADAM_WD = 0.01
ADAM_STEP = 10

RET_W, MLA_W, GLA_W = 1024, 1024, 896
ARR_W = RET_W + MLA_W + GLA_W
VMEM_MB = 1024 * 1024


def _cp(sem, vmem_mb=48):
    return pltpu.CompilerParams(dimension_semantics=sem, vmem_limit_bytes=vmem_mb * VMEM_MB)


def _mm(a, b):
    return jnp.dot(a.astype(_MXU), b.astype(_MXU), preferred_element_type=F32)


def _mm_nt(a, b):
    return lax.dot_general(a.astype(_MXU), b.astype(_MXU), (((1,), (1,)), ((), ())),
                           preferred_element_type=F32)


def _mm_tn(a, b):
    return lax.dot_general(a.astype(_MXU), b.astype(_MXU), (((0,), (0,)), ((), ())),
                           preferred_element_type=F32)


def _mm_f32(a, b):
    return jnp.dot(a, b, precision=lax.Precision.HIGHEST, preferred_element_type=F32)


def _sig(z):
    return 1.0 / (1.0 + jnp.exp(-z))


def _silu(z):
    return z * _sig(z)


def _dsilu(z):
    s = _sig(z)
    return s * (1.0 + z * (1.0 - s))


def _full(shape):
    nd = len(shape)
    return pl.BlockSpec(shape, lambda *_: (0,) * nd)


def _behind(body, n_in, after):
    if after is None:
        return body, [], []

    def body_behind(*refs):
        body(*refs[:n_in], *refs[n_in + 1:])

    return body_behind, [after], [pl.BlockSpec(memory_space=pl.ANY)]


def _w_in_runs(block_cols):
    m, g = RET_W, RET_W + MLA_W
    whole = [(base + 64 * h + 32 * t, 32, base + 128 * t + 32 * h)
             for base in (0, 256) for t in range(2) for h in range(4)]
    whole += [(512, 512, 512), (1024, 384, m), (1408, 32, m + 448), (1440, 512, m + 512),
              (1952, 528, g), (2480, 256, g + 640)]
    zeros = [(m + 384, 64), (m + 480, 32), (g + 528, 112)]
    runs = []
    for src, n, dst in whole:
        while n:
            blk, off = divmod(src, block_cols)
            k = min(n, block_cols - off)
            runs.append((blk, off, k, dst))
            src, n, dst = src + k, n - k, dst + k
    return runs, zeros


def _arrange_w_in(parts, tm=256):
    n, _, cols = parts[0].shape
    tiles = [p.shape[1] // tm for p in parts]
    first = [sum(tiles[:k]) for k in range(len(parts))]
    runs, zeros = _w_in_runs(cols)

    def arrange_w_in_kernel(*refs):
        a_ref = refs[-1]
        i = pl.program_id(0)
        for dst, k in zeros:
            a_ref[:, dst:dst + k] = jnp.zeros((tm, k), a_ref.dtype)
        for b_ref, f, t in zip(refs[:-1], first, tiles):
            @pl.when((i >= f) & (i < f + t))
            def _():
                for blk, off, k, dst in runs:
                    a_ref[:, dst:dst + k] = b_ref[blk, :, off:off + k]

    return pl.pallas_call(
        arrange_w_in_kernel, grid=(sum(tiles),),
        in_specs=[pl.BlockSpec((n, tm, cols), lambda i, f=f, t=t: (0, jnp.clip(i - f, 0, t - 1), 0))
                  for f, t in zip(first, tiles)],
        out_specs=pl.BlockSpec((tm, ARR_W), lambda i: (i, 0)),
        out_shape=jax.ShapeDtypeStruct((sum(tiles) * tm, ARR_W), parts[0].dtype),
        compiler_params=_cp(("parallel",)), name="arrange_w_in")(*parts)


def _unarrange_w_in(a, n, cols, tm=256):
    rows = a.shape[0]
    runs, _ = _w_in_runs(cols)

    def unarrange_w_in_kernel(a_ref, b_ref):
        for blk, off, k, dst in runs:
            b_ref[blk, :, off:off + k] = a_ref[:, dst:dst + k].astype(b_ref.dtype)

    return pl.pallas_call(
        unarrange_w_in_kernel, grid=(rows // tm,),
        in_specs=[pl.BlockSpec((tm, ARR_W), lambda i: (i, 0))],
        out_specs=pl.BlockSpec((n, tm, cols), lambda i: (0, i, 0)),
        out_shape=jax.ShapeDtypeStruct((n, rows, cols), jnp.bfloat16),
        compiler_params=_cp(("parallel",)), name="unarrange_w_in")(a)


def _arrange_mla_weights(uq_heads, ukv_heads):
    nh = uq_heads.shape[0]
    dt = uq_heads.dtype

    def arrange_mla_weights_kernel(uq_ref, ukv_ref, q_ref, kv_ref, kvt_ref):
        q_ref[...] = jnp.zeros(q_ref.shape, dt)
        kv_ref[...] = jnp.zeros(kv_ref.shape, dt)
        kvt_ref[...] = jnp.zeros(kvt_ref.shape, dt)
        for h in range(nh):
            q_ref[:, 128 * h:128 * h + 96] = uq_ref[h]
            blk = ukv_ref[h]
            kv_ref[:, 128 * h:128 * h + 64] = blk[:, 0:64]
            kv_ref[:, 128 * nh + 64 * h:128 * nh + 64 * h + 64] = blk[:, 64:128]
            blk_t = blk.astype(F32).T.astype(dt)
            kvt_ref[128 * h:128 * h + 64, :] = blk_t[0:64]
            kvt_ref[128 * nh + 64 * h:128 * nh + 64 * h + 64, :] = blk_t[64:128]

    return pl.pallas_call(
        arrange_mla_weights_kernel, name="arrange_mla_weights",
        out_shape=[jax.ShapeDtypeStruct((256, 128 * nh), dt), jax.ShapeDtypeStruct((128, 192 * nh), dt),
                   jax.ShapeDtypeStruct((192 * nh, 128), dt)])(uq_heads, ukv_heads)


def _unarrange_mla_weights(dw_uq_a, dw_ukv_a):
    nh = dw_uq_a.shape[1] // 128

    def unarrange_mla_weights_kernel(q_ref, kv_ref, uq_ref, ukv_ref):
        for h in range(nh):
            uq_ref[h] = q_ref[:, 128 * h:128 * h + 96].astype(uq_ref.dtype)
            ukv_ref[h, :, 0:64] = kv_ref[:, 128 * h:128 * h + 64].astype(ukv_ref.dtype)
            ukv_ref[h, :, 64:128] = kv_ref[:, 128 * nh + 64 * h:128 * nh + 64 * h + 64].astype(ukv_ref.dtype)

    return pl.pallas_call(
        unarrange_mla_weights_kernel, name="unarrange_mla_weights",
        out_shape=[jax.ShapeDtypeStruct((nh, 256, 96), jnp.bfloat16),
                   jax.ShapeDtypeStruct((nh, 128, 128), jnp.bfloat16)])(dw_uq_a, dw_ukv_a)


def _rope_tables(pos3, after=None):
    B, S, _ = pos3.shape
    ts = min(S, 512)
    inv32 = (np.float32(ROPE_THETA) ** (-(np.arange(32, dtype=np.float32) / 32))).astype(np.float32)
    inv16 = (np.float32(ROPE_THETA) ** (-(np.arange(16, dtype=np.float32) / 16))).astype(np.float32)
    inv = np.zeros((1, 128), np.float32)
    inv[0, 0:32] = inv32
    inv[0, 32:48] = inv16

    def body(pos_ref, inv_ref, cr, sr, cm, sm):
        ang = pos_ref[0].astype(F32) * inv_ref[...]
        lane = lax.broadcasted_iota(jnp.int32, (1, 128), 1)

        def every_head(x):
            y = jnp.where(lane < 32, x, pltpu.roll(x, 32, 1))
            return jnp.where(lane < 64, y, pltpu.roll(y, 64, 1))

        def rotary_pair(x, fill):
            return jnp.where((lane >= 64) & (lane < 80), pltpu.roll(x, 32, 1),
                             jnp.where((lane >= 80) & (lane < 96), pltpu.roll(x, 48, 1), fill))

        c, s = jnp.cos(ang), jnp.sin(ang)
        cr[0] = every_head(c)
        sr[0] = every_head(s)
        cm[0] = rotary_pair(c, 1.0)
        sm[0] = rotary_pair(s, 0.0)

    tab = jax.ShapeDtypeStruct((B, S, 128), F32)
    blk = pl.BlockSpec((1, ts, 128), lambda b, i: (b, i, 0))
    body, extra, extra_specs = _behind(body, 2, after)
    return pl.pallas_call(
        body, name="rope_tables", grid=(B, S // ts),
        in_specs=[pl.BlockSpec((1, ts, 1), lambda b, i: (b, i, 0)), _full((1, 128))] + extra_specs,
        out_specs=[blk, blk, blk, blk], out_shape=[tab, tab, tab, tab],
        compiler_params=_cp(("parallel", "parallel")),
    )(pos3, jnp.asarray(inv), *extra)


def _rope128(x, cos, sin):
    lane = lax.broadcasted_iota(jnp.int32, (1, 128), 1)
    rp = pltpu.roll(x, 16, 1)
    rm = pltpu.roll(x, 112, 1)
    return x * cos + jnp.where(lane < 80, -rm, rp) * sin


def _rope128_t(d, cos, sin):
    lane = lax.broadcasted_iota(jnp.int32, (1, 128), 1)
    y = d * sin
    yp = pltpu.roll(y, 16, 1)
    ym = pltpu.roll(y, 112, 1)
    return d * cos + jnp.where(lane < 64, 0.0, jnp.where(lane < 80, ym, jnp.where(lane < 96, -yp, 0.0)))


def _proj_fwd(x, shift, scale, nw, w_arr):
    B, S, D = x.shape
    tm = min(S, 512)

    def body(x_ref, sh_ref, sc_ref, nw_ref, w_ref, ret_ref, mla_ref, gla_ref, h_ref):
        xv = x_ref[0]
        rstd = lax.rsqrt(jnp.mean(xv * xv, axis=-1, keepdims=True) + EPS)
        h = (xv * rstd * nw_ref[...]) * (1.0 + sc_ref[0]) + sh_ref[0]
        hb = h.astype(_MXU)
        h_ref[0] = hb
        ret_ref[0] = jnp.dot(hb, w_ref[:, 0:RET_W], preferred_element_type=F32).astype(_MXU)
        mla_ref[0] = jnp.dot(hb, w_ref[:, RET_W:RET_W + MLA_W], preferred_element_type=F32).astype(_MXU)
        gla_ref[0] = jnp.dot(hb, w_ref[:, RET_W + MLA_W:ARR_W], preferred_element_type=F32).astype(_MXU)

    tok = lambda w: pl.BlockSpec((1, tm, w), lambda b, i: (b, i, 0))
    per_seq = pl.BlockSpec((1, 1, D), lambda b, i: (b, 0, 0))
    return pl.pallas_call(
        body, name="proj_fwd", grid=(B, S // tm),
        in_specs=[tok(D), per_seq, per_seq, _full((1, D)), _full((D, ARR_W))],
        out_specs=[tok(RET_W), tok(MLA_W), tok(GLA_W), tok(D)],
        out_shape=[jax.ShapeDtypeStruct((B, S, RET_W), _MXU), jax.ShapeDtypeStruct((B, S, MLA_W), _MXU),
                   jax.ShapeDtypeStruct((B, S, GLA_W), _MXU), jax.ShapeDtypeStruct((B, S, D), _MXU)],
        compiler_params=_cp(("parallel", "parallel")),
    )(x, shift, scale, nw, w_arr)


RET_L = 256


def _ret_consts(L):
    lg = np.log1p(-np.exp2(-5.0 - np.arange(4, dtype=np.float32))).astype(np.float32)
    i = np.arange(L)
    ci = i // CHUNK
    diff = (i[:, None] - i[None, :]).astype(np.float32)
    same = ci[:, None] == ci[None, :]
    past = ci[None, :] < ci[:, None]
    expo = np.where(same, np.abs(diff), np.where(past, diff, 0.0)).astype(np.float32)
    dec = np.where((same | past)[None], np.exp(lg[:, None, None] * expo[None]), 0.0).astype(np.float32)
    head = (np.arange(256) % 128) // 32
    qw = np.exp((i + 1.0)[:, None] * lg[head][None, :]).astype(np.float32)
    kw = np.exp((L - 1.0 - i)[:, None] * lg[head][None, :]).astype(np.float32)
    a_row = np.exp(np.float32(L) * lg[head])[None, :].astype(np.float32)
    return [jnp.asarray(t) for t in (dec.reshape(4 * L, L), qw, kw, a_row)]


def _ret_masks():
    lane = lax.broadcasted_iota(jnp.int32, (1, 256), 1)
    mh = [((lane % 128) // 32) == h for h in range(4)]
    mv = [(lane // 64) == h for h in range(4)]
    vi = lax.broadcasted_iota(jnp.int32, (256, 256), 0)
    ki = lax.broadcasted_iota(jnp.int32, (256, 256), 1)
    bd = (vi // 64) == ((ki % 128) // 32)
    return mh, mv, bd


def _ret_rope(p, cs, sn):
    q1, q2, k1, k2 = p[:, 0:128], p[:, 128:256], p[:, 256:384], p[:, 384:512]
    qr = jnp.concatenate([q1 * cs - q2 * sn, q2 * cs + q1 * sn], axis=1)
    kr = jnp.concatenate([k1 * cs - k2 * sn, k2 * cs + k1 * sn], axis=1) * RET_KSCALE
    return qr, kr


def _head_mean(x, mv, width):
    out = jnp.zeros_like(x)
    for m in mv:
        s = jnp.sum(jnp.where(m, x, 0.0), axis=-1, keepdims=True) * (1.0 / width)
        out = jnp.where(m, s, out)
    return out


def _stack_heads(x, masks):
    return jnp.concatenate([jnp.where(m, x, 0.0) for m in masks], axis=0)


def _fold_heads(xs, masks, L):
    out = jnp.where(masks[0], xs[0:L], 0.0)
    for h in range(1, 4):
        out = out + jnp.where(masks[h], xs[h * L:(h + 1) * L], 0.0)
    return out


RET_G = 2


def _ret_fwd(ret_p, cos, sin):
    B, S, _ = ret_p.shape
    L = min(RET_L, S)
    NB = S // L
    G = min(RET_G, NB)
    NG = NB // G
    consts = _ret_consts(L)

    def body(p_ref, c_ref, s_ref, ds_ref, qw_ref, kw_ref, a_ref, out_ref, raw_ref, st_ref, st_sc):
        @pl.when(pl.program_id(1) == 0)
        def _():
            st_sc[...] = jnp.zeros_like(st_sc)

        mh, mv, bd = _ret_masks()
        cs_ = range(G)
        rows = [slice(c * L, (c + 1) * L) for c in cs_]
        ps = [p_ref[0, rows[c], :].astype(F32) for c in cs_]
        qk = [_ret_rope(ps[c], c_ref[0, rows[c], :], s_ref[0, rows[c], :]) for c in cs_]
        vs = [ps[c][:, 512:768] for c in cs_]
        a_s = [_mm_nt(_stack_heads(qk[c][0], mh), qk[c][1]) for c in cs_]
        upd = [_mm_tn(vs[c], qk[c][1] * kw_ref[...]) for c in cs_]
        o_s = [_mm(a_s[c] * ds_ref[...], vs[c]) for c in cs_]
        st = st_sc[...]
        inter = []
        for c in cs_:
            st_ref[0, c] = st
            inter.append(_mm_nt(qk[c][0] * qw_ref[...], st))
            st = st * a_ref[...] + jnp.where(bd, upd[c], 0.0)
        st_sc[...] = st
        for c in cs_:
            r = _fold_heads(o_s[c], mv, L) + inter[c]
            raw_ref[0, rows[c], :] = r
            rstd = lax.rsqrt(_head_mean(r * r, mv, 64.0) + EPS)
            out_ref[0, rows[c], :] = (r * rstd * _silu(ps[c][:, 768:1024])).astype(_MXU)

    tok = lambda w: pl.BlockSpec((1, G * L, w), lambda b, n: (b, n, 0))
    return pl.pallas_call(
        body, name="ret_fwd", grid=(B, NG),
        in_specs=[tok(RET_W), tok(128), tok(128), _full((4 * L, L)), _full((L, 256)), _full((L, 256)),
                  _full((1, 256))],
        out_specs=[tok(256), tok(256), pl.BlockSpec((1, G, 256, 256), lambda b, n: (b, n, 0, 0))],
        out_shape=[jax.ShapeDtypeStruct((B, S, 256), _MXU), jax.ShapeDtypeStruct((B, S, 256), F32),
                   jax.ShapeDtypeStruct((B, NB, 256, 256), F32)],
        scratch_shapes=[pltpu.VMEM((256, 256), F32)],
        compiler_params=_cp(("parallel", "arbitrary")),
    )(ret_p, cos, sin, *consts)


def _ret_bwd(ret_p, cos, sin, raw, states, d_mix):
    B, S, _ = ret_p.shape
    L = min(RET_L, S)
    NB = S // L
    G = 1
    NG = NB // G
    consts = _ret_consts(L)

    def body(p_ref, c_ref, s_ref, raw_ref, st_ref, dm_ref, ds_ref, qw_ref, kw_ref, a_ref, dp_ref, dst_sc):
        @pl.when(pl.program_id(1) == 0)
        def _():
            dst_sc[...] = jnp.zeros_like(dst_sc)

        mh, mv, bd = _ret_masks()
        qw, kw, dec = qw_ref[...], kw_ref[...], ds_ref[...]
        cs_ = range(G)
        rows = [slice(c * L, (c + 1) * L) for c in cs_]
        ps = [p_ref[0, rows[c], :].astype(F32) for c in cs_]
        tabs = [(c_ref[0, rows[c], :], s_ref[0, rows[c], :]) for c in cs_]
        qk = [_ret_rope(ps[c], *tabs[c]) for c in cs_]
        vs = [ps[c][:, 512:768] for c in cs_]
        qs = [_stack_heads(qk[c][0], mh) for c in cs_]
        a_s = [_mm_nt(qs[c], qk[c][1]) for c in cs_]
        dr, dz = [], []
        for c in cs_:
            r = raw_ref[0, rows[c], :]
            z = ps[c][:, 768:1024]
            rstd = lax.rsqrt(_head_mean(r * r, mv, 64.0) + EPS)
            rn = r * rstd
            dm = dm_ref[0, rows[c], :]
            d_rn = dm * _silu(z)
            dz.append(dm * rn * _dsilu(z))
            dr.append(rstd * (d_rn - rn * _head_mean(d_rn * rn, mv, 64.0)))
        do_s = [_stack_heads(dr[c], mv) for c in cs_]
        da_s = [_mm_nt(do_s[c], vs[c]) for c in cs_]
        sts = [st_ref[0, c] for c in cs_]
        dq_st = [_mm(dr[c], sts[c]) for c in cs_]
        dst_in = [_mm_tn(dr[c], qk[c][0] * qw) for c in cs_]
        dv = [_mm_tn(a_s[c] * dec, do_s[c]) for c in cs_]
        dqr, dkr = [], []
        for c in cs_:
            da = da_s[c] * dec
            dqr.append(_fold_heads(_mm(da, qk[c][1]), mh, L) + dq_st[c] * qw)
            dkr.append(_mm_tn(da, qs[c]))
        dst_next = dst_sc[...]
        for c in reversed(cs_):
            g = jnp.where(bd, dst_next, 0.0)
            dv[c] = dv[c] + _mm_nt(qk[c][1] * kw, g)
            dkr[c] = dkr[c] + _mm(vs[c], g) * kw
            dst_next = dst_next * a_ref[...] + jnp.where(bd, dst_in[c], 0.0)
        dst_sc[...] = dst_next
        for c in cs_:
            cs, sn = tabs[c]
            dk = dkr[c] * RET_KSCALE
            dq1, dq2 = dqr[c][:, 0:128], dqr[c][:, 128:256]
            dk1, dk2 = dk[:, 0:128], dk[:, 128:256]
            dp_ref[0, rows[c], :] = jnp.concatenate(
                [dq1 * cs + dq2 * sn, dq2 * cs - dq1 * sn, dk1 * cs + dk2 * sn, dk2 * cs - dk1 * sn, dv[c], dz[c]],
                axis=1).astype(_MXU)

    tok = lambda w: pl.BlockSpec((1, G * L, w), lambda b, i: (b, NG - 1 - i, 0))
    return pl.pallas_call(
        body, name="ret_bwd", grid=(B, NG),
        in_specs=[tok(RET_W), tok(128), tok(128), tok(256),
                  pl.BlockSpec((1, G, 256, 256), lambda b, i: (b, NG - 1 - i, 0, 0)), tok(256),
                  _full((4 * L, L)), _full((L, 256)), _full((L, 256)), _full((1, 256))],
        out_specs=tok(RET_W), out_shape=jax.ShapeDtypeStruct((B, S, RET_W), _MXU),
        scratch_shapes=[pltpu.VMEM((256, 256), F32)],
        compiler_params=_cp(("parallel", "arbitrary")),
    )(ret_p, cos, sin, raw, states, d_mix, *consts)


def _gla_masks():
    C = CHUNK
    lk = lax.broadcasted_iota(jnp.int32, (1, 128), 1)
    lv = lax.broadcasted_iota(jnp.int32, (1, 256), 1)
    mk = [(lk // 32) == h for h in range(4)]
    mv = [(lv // 64) == h for h in range(4)]
    vi = lax.broadcasted_iota(jnp.int32, (256, 128), 0)
    ki = lax.broadcasted_iota(jnp.int32, (256, 128), 1)
    bd = (vi // 64) == (ki // 32)
    ri = lax.broadcasted_iota(jnp.int32, (4 * C, C), 0) % C
    cj = lax.broadcasted_iota(jnp.int32, (4 * C, C), 1)
    lower = ri >= cj
    ti = lax.broadcasted_iota(jnp.int32, (C, C), 0)
    tj = lax.broadcasted_iota(jnp.int32, (C, C), 1)
    ltri = jnp.where(ti >= tj, 1.0, 0.0).astype(F32)
    utri = jnp.where(ti <= tj, 1.0, 0.0).astype(F32)
    return mk, mv, bd, lower, ltri, utri


def _log_sigmoid(x):
    return jnp.minimum(x, 0.0) - jnp.log(1.0 + jnp.exp(-jnp.abs(x)))


GLA_G = 8


def _gla_fwd(gla_p, w_g2p, b_g2, gnw):
    B, S, _ = gla_p.shape
    C = CHUNK
    NC = S // C
    G = min(GLA_G, NC)
    NG = NC // G

    def body(p_ref, w_ref, b_ref, gn_ref, out_ref, raw_ref, st_ref, st_sc):
        @pl.when(pl.program_id(1) == 0)
        def _():
            st_sc[...] = jnp.zeros_like(st_sc)

        mk, mv, bd, lower, ltri, _ = _gla_masks()
        cs = range(G)
        rows = [slice(c * C, (c + 1) * C) for c in cs]
        ps = [p_ref[0, rows[c], :].astype(F32) for c in cs]
        pre = [_mm(ps[c][:, 512:640], w_ref[...]) + b_ref[...] for c in cs]
        cum = [_mm_f32(ltri, _log_sigmoid(pre[c]) * (1.0 / GLA_TAU)) for c in cs]
        past, fut, upd, q_pos, a_row = [], [], [], [], []
        for c in cs:
            q = ps[c][:, 0:128]
            k = ps[c][:, 128:256] * GLA_KSCALE
            last = cum[c][C - 1:C, :]
            e_pos = jnp.exp(cum[c])
            e_neg = jnp.exp(-cum[c])
            q_pos.append(q * e_pos)
            a_row.append(jnp.exp(last))
            past.append(_mm_nt(_stack_heads(q_pos[c], mk), k * e_neg))
            fut.append(_mm_nt(_stack_heads(q * e_neg, mk), k * e_pos))
            upd.append(_mm_tn(ps[c][:, 256:512], k * jnp.exp(last - cum[c])))
        o_s = [_mm(jnp.where(lower, past[c], fut[c]), ps[c][:, 256:512]) for c in cs]
        st = st_sc[...]
        inter = []
        for c in cs:
            st_ref[0, c] = st
            inter.append(_mm_nt(q_pos[c], st))
            st = st * a_row[c] + jnp.where(bd, upd[c], 0.0)
        st_sc[...] = st
        for c in cs:
            g = _fold_heads(o_s[c], mv, C) + inter[c]
            raw_ref[0, rows[c], :] = g
            rstd = lax.rsqrt(_head_mean(g * g, mv, 64.0) + EPS)
            out_ref[0, rows[c], :] = (g * rstd * gn_ref[...] * _silu(ps[c][:, 640:896])).astype(_MXU)

    tok = lambda w: pl.BlockSpec((1, G * C, w), lambda b, n: (b, n, 0))
    return pl.pallas_call(
        body, name="gla_fwd", grid=(B, NG),
        in_specs=[tok(GLA_W), _full((128, 128)), _full((1, 128)), _full((1, 256))],
        out_specs=[tok(256), tok(256), pl.BlockSpec((1, G, 256, 128), lambda b, n: (b, n, 0, 0))],
        out_shape=[jax.ShapeDtypeStruct((B, S, 256), _MXU), jax.ShapeDtypeStruct((B, S, 256), F32),
                   jax.ShapeDtypeStruct((B, NC, 256, 128), F32)],
        scratch_shapes=[pltpu.VMEM((256, 128), F32)],
        compiler_params=_cp(("parallel", "arbitrary")),
    )(gla_p, w_g2p, b_g2, gnw)


def _gla_bwd(gla_p, w_g2p, b_g2, gnw, raw, states, d_mix, after=None):
    B, S, _ = gla_p.shape
    C = CHUNK
    NC = S // C
    G = min(GLA_G, NC)
    NG = NC // G

    def body(p_ref, w_ref, b_ref, gn_ref, raw_ref, st_ref, dm_ref, dp_ref, dw_ref, db_ref, dgn_ref, dst_sc):
        first = (pl.program_id(0) == 0) & (pl.program_id(1) == 0)

        @pl.when(first)
        def _():
            dw_ref[...] = jnp.zeros_like(dw_ref)
            db_ref[...] = jnp.zeros_like(db_ref)
            dgn_ref[...] = jnp.zeros_like(dgn_ref)

        @pl.when(pl.program_id(1) == 0)
        def _():
            dst_sc[...] = jnp.zeros_like(dst_sc)

        mk, mv, bd, lower, ltri, utri = _gla_masks()
        gn = gn_ref[...]
        cs = range(G)
        rows = [slice(c * C, (c + 1) * C) for c in cs]
        ps = [p_ref[0, rows[c], :].astype(F32) for c in cs]
        vs = [ps[c][:, 256:512] for c in cs]
        pre = [_mm(ps[c][:, 512:640], w_ref[...]) + b_ref[...] for c in cs]
        cum = [_mm_f32(ltri, _log_sigmoid(pre[c]) * (1.0 / GLA_TAU)) for c in cs]
        dg, dz, dgn_acc = [], [], jnp.zeros((1, 256), F32)
        for c in cs:
            g = raw_ref[0, rows[c], :]
            z = ps[c][:, 640:896]
            rstd = lax.rsqrt(_head_mean(g * g, mv, 64.0) + EPS)
            gh = g * rstd
            dm = dm_ref[0, rows[c], :]
            d_gn = dm * _silu(z)
            dz.append(dm * gh * gn * _dsilu(z))
            dgn_acc = dgn_acc + jnp.sum(d_gn * gh, axis=0, keepdims=True)
            d_gh = d_gn * gn
            dg.append(rstd * (d_gh - gh * _head_mean(d_gh * gh, mv, 64.0)))
        do_s = [_stack_heads(dg[c], mv) for c in cs]
        dattn = [_mm_nt(do_s[c], vs[c]) for c in cs]
        ks, e_pos, e_neg, q_pos, q_neg, k_pos, k_neg, qp_s, qn_s, past, fut, a_row, w_dec, kd = ([] for _ in range(14))
        for c in cs:
            q = ps[c][:, 0:128]
            k = ps[c][:, 128:256] * GLA_KSCALE
            last = cum[c][C - 1:C, :]
            ep, en = jnp.exp(cum[c]), jnp.exp(-cum[c])
            ks.append(k), e_pos.append(ep), e_neg.append(en)
            q_pos.append(q * ep), q_neg.append(q * en), k_pos.append(k * ep), k_neg.append(k * en)
            qp_s.append(_stack_heads(q_pos[c], mk)), qn_s.append(_stack_heads(q_neg[c], mk))
            past.append(_mm_nt(qp_s[c], k_neg[c]))
            fut.append(_mm_nt(qn_s[c], k_pos[c]))
            a_row.append(jnp.exp(last))
            w_dec.append(jnp.exp(last - cum[c]))
            kd.append(k * w_dec[c])
        sts = [st_ref[0, c] for c in cs]
        dq_st = [_mm(dg[c], sts[c]) for c in cs]
        dst_in = [_mm_tn(dg[c], q_pos[c]) for c in cs]
        dv, dq_pos, dk_neg, dq_neg, dk_pos = [], [], [], [], []
        for c in cs:
            attn = jnp.where(lower, past[c], fut[c])
            dpast = jnp.where(lower, dattn[c], 0.0)
            dfut = jnp.where(lower, 0.0, dattn[c])
            dv.append(_mm_tn(attn, do_s[c]))
            dq_pos.append(_fold_heads(_mm(dpast, k_neg[c]), mk, C) + dq_st[c])
            dk_neg.append(_mm_tn(dpast, qp_s[c]))
            dq_neg.append(_fold_heads(_mm(dfut, k_pos[c]), mk, C))
            dk_pos.append(_mm_tn(dfut, qn_s[c]))
        dst_next = dst_sc[...]
        d_a, d_kd = [None] * G, [None] * G
        for c in reversed(cs):
            d_a[c] = jnp.sum(dst_next * sts[c], axis=0, keepdims=True)
            gmat = jnp.where(bd, dst_next, 0.0)
            d_kd[c] = _mm(vs[c], gmat)
            dv[c] = dv[c] + _mm_nt(kd[c], gmat)
            dst_next = dst_next * a_row[c] + jnp.where(bd, dst_in[c], 0.0)
        dst_sc[...] = dst_next
        row = lax.broadcasted_iota(jnp.int32, (C, 128), 0)
        d_la, dk, dq = [], [], []
        for c in cs:
            t = d_kd[c] * kd[c]
            dk.append(d_kd[c] * w_dec[c] + dk_neg[c] * e_neg[c] + dk_pos[c] * e_pos[c])
            dq.append(dq_pos[c] * e_pos[c] + dq_neg[c] * e_neg[c])
            d_last = jnp.sum(t, axis=0, keepdims=True) + d_a[c] * a_row[c]
            d_cum = (dq_pos[c] * q_pos[c] - dk_neg[c] * k_neg[c] - dq_neg[c] * q_neg[c] + dk_pos[c] * k_pos[c] - t)
            d_la.append(_mm_f32(utri, d_cum + jnp.where(row == C - 1, d_last, 0.0)))
        d_pre = [d_la[c] * _sig(-pre[c]) * (1.0 / GLA_TAU) for c in cs]
        d_gg = [_mm_nt(d_pre[c], w_ref[...]) for c in cs]
        dw_acc = _mm_tn(ps[0][:, 512:640], d_pre[0])
        db_acc = jnp.sum(d_pre[0], axis=0, keepdims=True)
        for c in cs[1:]:
            dw_acc = dw_acc + _mm_tn(ps[c][:, 512:640], d_pre[c])
            db_acc = db_acc + jnp.sum(d_pre[c], axis=0, keepdims=True)
        for c in cs:
            dp_ref[0, rows[c], :] = jnp.concatenate([dq[c], dk[c] * GLA_KSCALE, dv[c], d_gg[c], dz[c]],
                                                    axis=1).astype(_MXU)
        dw_ref[...] += dw_acc
        db_ref[...] += db_acc
        dgn_ref[...] += dgn_acc

        @pl.when((pl.program_id(0) == B - 1) & (pl.program_id(1) == NG - 1))
        def _():
            s1 = dgn_ref[...]
            s1 = s1 + pltpu.roll(s1, 128, 1)
            dgn_ref[...] = s1 + pltpu.roll(s1, 64, 1)

    tok = lambda w: pl.BlockSpec((1, G * C, w), lambda b, i: (b, NG - 1 - i, 0))
    body, extra, extra_specs = _behind(body, 7, after)
    return pl.pallas_call(
        body, name="gla_bwd", grid=(B, NG),
        in_specs=[tok(GLA_W), _full((128, 128)), _full((1, 128)), _full((1, 256)), tok(256),
                  pl.BlockSpec((1, G, 256, 128), lambda b, i: (b, NG - 1 - i, 0, 0)), tok(256)] + extra_specs,
        out_specs=[tok(GLA_W), _full((128, 128)), _full((1, 128)), _full((1, 256))],
        out_shape=[jax.ShapeDtypeStruct((B, S, GLA_W), _MXU), jax.ShapeDtypeStruct((128, 128), F32),
                   jax.ShapeDtypeStruct((1, 128), F32), jax.ShapeDtypeStruct((1, 256), F32)],
        scratch_shapes=[pltpu.VMEM((256, 128), F32)],
        compiler_params=_cp(("arbitrary", "arbitrary")),
    )(gla_p, w_g2p, b_g2, gnw, raw, states, d_mix, *extra)


def _rms(x, w):
    rstd = lax.rsqrt(jnp.mean(x * x, axis=-1, keepdims=True) + EPS)
    xh = x * rstd
    return xh, rstd, xh * w


def _rms_bwd(dy, xh, rstd, w):
    dxh = dy * w
    return rstd * (dxh - xh * jnp.mean(dxh * xh, axis=-1, keepdims=True))


MLA_T = 256


def _mla_prep_fwd(mla_p, cos, sin, qnw, kvnw, w_uq, w_ukv, w_ukv_t):
    B, S, _ = mla_p.shape
    tm = min(S, 512)

    t = min(MLA_T, S)
    nt = tm // t

    def body(p_ref, c_ref, s_ref, qn_ref, kn_ref, wq_ref, wkv_ref, wkvt_ref, q_ref, k_ref, v_ref, kt_ref, vt_ref):
        p = p_ref[0].astype(F32)
        cs, sn = c_ref[0], s_ref[0]
        _, _, qn = _rms(p[:, 0:256], qn_ref[...])
        qpre = _mm(qn, wq_ref[...])
        _, _, kvn = _rms(p[:, 256:384], kn_ref[...])
        kv = _mm(kvn, wkv_ref[...])
        kvt = _mm_nt(wkvt_ref[...], kvn)
        kpe = _rope128(p[:, 384:512], cs, sn)
        kpet = kpe.T
        for h in range(8):
            sl = slice(128 * h, 128 * h + 128)
            q_ref[0, :, sl] = _rope128(qpre[:, sl], cs, sn).astype(_MXU)
            k_ref[0, :, sl] = (kv[:, sl] + kpe).astype(_MXU)
            kht = kvt[sl, :] + kpet
            for n in range(nt):
                kt_ref[0, n, sl, :] = kht[:, n * t:(n + 1) * t].astype(_MXU)
        v_ref[0] = kv[:, 1024:1536].astype(_MXU)
        for n in range(nt):
            vt_ref[0, n] = kvt[1024:1536, n * t:(n + 1) * t].astype(_MXU)

    tok = lambda w: pl.BlockSpec((1, tm, w), lambda b, i: (b, i, 0))
    tr = lambda w: pl.BlockSpec((1, nt, w, t), lambda b, i: (b, i, 0, 0))
    return pl.pallas_call(
        body, name="mla_prep_fwd", grid=(B, S // tm),
        in_specs=[tok(512), tok(128), tok(128), _full((1, 256)), _full((1, 128)), _full((256, 1024)),
                  _full((128, 1536)), _full((1536, 128))],
        out_specs=[tok(1024), tok(1024), tok(512), tr(1024), tr(512)],
        out_shape=[jax.ShapeDtypeStruct((B, S, 1024), _MXU), jax.ShapeDtypeStruct((B, S, 1024), _MXU),
                   jax.ShapeDtypeStruct((B, S, 512), _MXU), jax.ShapeDtypeStruct((B, S // t, 1024, t), _MXU),
                   jax.ShapeDtypeStruct((B, S // t, 512, t), _MXU)],
        compiler_params=_cp(("parallel", "parallel")),
    )(mla_p, cos, sin, qnw, kvnw, w_uq, w_ukv, w_ukv_t)


def _chunk_mask_t(t):
    kj = lax.broadcasted_iota(jnp.int32, (t, t), 0) // CHUNK
    qi = lax.broadcasted_iota(jnp.int32, (t, t), 1) // CHUNK
    return kj <= qi


MLA_HG = 8
MLA_HG_FWD = 8
LOG2E = 1.4426950408889634
MLA_C2 = MLA_SCALE * LOG2E


def _mla_attn_fwd(q, k, vt):
    B, S, _ = q.shape
    t = min(MLA_T, S)
    nq = S // t
    HG = MLA_HG_FWD
    NP = HG // 2

    def body(q_ref, k_ref, vt_ref, o_ref, lse_ref, sa, sb, m_sc, l_sc, acc_sc):
        i = pl.program_id(2)
        row = lax.broadcasted_iota(jnp.int32, (128, 1), 0)
        low = row < 64
        mask = _chunk_mask_t(t)
        m_sc[...] = jnp.full(m_sc.shape, -jnp.inf, F32)
        l_sc[...] = jnp.zeros_like(l_sc)
        acc_sc[...] = jnp.zeros_like(acc_sc)

        ones = jnp.ones((8, t), _MXU)

        def scores(j, buf):
            kb = k_ref[0, pl.ds(pl.multiple_of(j * t, t), t), :]
            for h in range(HG):
                cols = slice(128 * h, 128 * h + 128)
                buf[h] = (_mm_nt(kb[:, cols], q_ref[0, :, cols]) * MLA_C2).astype(_MXU)

        def absorb(j, buf, masked):
            vtb = vt_ref[0, j]
            for pr in range(NP):
                alphas, pvs = [], []
                for hh in range(2):
                    h = 2 * pr + hh
                    s = buf[h]
                    if masked:
                        s = jnp.where(mask, s, jnp.full_like(s, -jnp.inf))
                    m_old = m_sc[h]
                    m_new = jnp.maximum(m_old, jnp.max(s, axis=0, keepdims=True).astype(F32))
                    alpha = jnp.exp2(m_old - m_new)
                    p = jnp.exp2(s - m_new.astype(_MXU))
                    l_sc[h] = alpha * l_sc[h] + _mm(ones, p)[0:1, :]
                    m_sc[h] = m_new
                    vth = vtb[128 * pr:128 * pr + 128, :]
                    vth = jnp.where(low if hh == 0 else ~low, vth, jnp.zeros_like(vth))
                    pvs.append(_mm(vth, p))
                    alphas.append(alpha)
                acc_sc[pr] = acc_sc[pr] * jnp.where(low, alphas[0], alphas[1]) + pvs[0] + pvs[1]

        scores(0, sb)

        def pair(jj, carry):
            j0 = 2 * jj
            scores(j0 + 1, sa)
            absorb(j0, sb, False)
            scores(j0 + 2, sb)
            absorb(j0 + 1, sa, False)
            return carry

        lax.fori_loop(0, i // 2, pair, 0)

        @pl.when(i % 2 == 1)
        def _():
            scores(i, sa)
            absorb(i - 1, sb, False)
            absorb(i, sa, True)

        @pl.when(i % 2 == 0)
        def _():
            absorb(i, sb, True)

        for pr in range(NP):
            l_e, l_o = l_sc[2 * pr], l_sc[2 * pr + 1]
            o_ref[0, :, 128 * pr:128 * pr + 128] = (acc_sc[pr] / jnp.where(low, l_e, l_o)).T
            lse_ref[0, pr, 0, 0:1, :] = m_sc[2 * pr] + jnp.log(l_e) * LOG2E
            lse_ref[0, pr, 0, 1:2, :] = m_sc[2 * pr + 1] + jnp.log(l_o) * LOG2E

    return pl.pallas_call(
        body, name="mla_attn_fwd", grid=(B, 8 // HG, nq),
        in_specs=[pl.BlockSpec((1, t, 128 * HG), lambda b, g, i: (b, i, g)),
                  pl.BlockSpec((1, S, 128 * HG), lambda b, g, i: (b, 0, g)),
                  pl.BlockSpec((1, nq, 64 * HG, t), lambda b, g, i: (b, 0, g, 0))],
        out_specs=[pl.BlockSpec((1, t, 64 * HG), lambda b, g, i: (b, i, g)),
                   pl.BlockSpec((1, NP, 1, 2, t), lambda b, g, i: (b, g, i, 0, 0))],
        out_shape=[jax.ShapeDtypeStruct((B, S, 512), F32), jax.ShapeDtypeStruct((B, 4, nq, 2, t), F32)],
        scratch_shapes=[pltpu.VMEM((HG, t, t), _MXU), pltpu.VMEM((HG, t, t), _MXU), pltpu.VMEM((HG, 1, t), F32),
                        pltpu.VMEM((HG, 1, t), F32), pltpu.VMEM((NP, 128, t), F32)],
        compiler_params=_cp(("parallel", "parallel", "arbitrary")),
    )(q, k, vt)


def _mla_attn_bwd(q, k, v, kt, do, lse, dl):
    B, S, _ = q.shape
    t = min(MLA_T, S)
    nk = S // t

    HG = MLA_HG
    NP = HG // 2

    def body(q_ref, k_ref, v_ref, kt_ref, do_ref, lse_ref, dl_ref, dq_ref, dk_ref, dv_ref,
             sa, da, sb, db, dqt_sc, dk_sc, dv_sc):
        j = pl.program_id(2)

        @pl.when(j == 0)
        def _():
            dqt_sc[...] = jnp.zeros_like(dqt_sc)

        dk_sc[...] = jnp.zeros_like(dk_sc)
        dv_sc[...] = jnp.zeros_like(dv_sc)
        lane = lax.broadcasted_iota(jnp.int32, (1, 128), 1)
        low = lane < 64
        mask = _chunk_mask_t(t)

        def half(x, hh):
            return jnp.where(low if hh == 0 else ~low, x, jnp.zeros_like(x))

        def prepare(i, sbuf, dbuf):
            rows = pl.ds(pl.multiple_of(i * t, t), t)
            for h in range(HG):
                cols = slice(128 * h, 128 * h + 128)
                pc = slice(128 * (h // 2), 128 * (h // 2) + 128)
                sbuf[h] = _mm_nt(k_ref[0, :, cols], q_ref[0, rows, cols]) * MLA_C2
                dbuf[h] = _mm_nt(half(v_ref[0, :, pc], h % 2), do_ref[0, rows, pc])

        def absorb(i, sbuf, dbuf, masked):
            rows = pl.ds(pl.multiple_of(i * t, t), t)
            for h in range(HG):
                pr, hh = h // 2, h % 2
                cols = slice(128 * h, 128 * h + 128)
                pc = slice(128 * pr, 128 * pr + 128)
                p = jnp.exp2(sbuf[h] - lse_ref[0, pr, i][hh:hh + 1, :])
                if masked:
                    p = jnp.where(mask, p, 0.0)
                dv_sc[pr] += _mm(p, half(do_ref[0, rows, pc], hh))
                ds = p * (dbuf[h] - dl_ref[0, pr, i][hh:hh + 1, :])
                dqt_sc[i, cols, :] += _mm(kt_ref[0, 0, cols, :], ds)
                dk_sc[h] += _mm(ds, q_ref[0, rows, cols])

        n = nk - 1 - j
        prepare(jnp.minimum(j + 1, nk - 1), sb, db)

        def pair(jj, carry):
            i0 = j + 1 + 2 * jj
            prepare(i0 + 1, sa, da)
            absorb(i0, sb, db, False)
            prepare(jnp.where(i0 + 2 <= nk - 1, i0 + 2, j), sb, db)
            absorb(i0 + 1, sa, da, False)
            return carry

        lax.fori_loop(0, n // 2, pair, 0)

        @pl.when(n % 2 == 1)
        def _():
            prepare(j, sa, da)
            absorb(nk - 1, sb, db, False)
            absorb(j, sa, da, True)

        @pl.when(n % 2 == 0)
        def _():
            absorb(j, sb, db, True)

        for h in range(HG):
            dk_ref[0, :, 128 * h:128 * h + 128] = (dk_sc[h] * MLA_SCALE).astype(_MXU)
        for pr in range(NP):
            dv_ref[0, :, 128 * pr:128 * pr + 128] = dv_sc[pr].astype(_MXU)

        @pl.when(j == nk - 1)
        def _():
            for i in range(nk):
                dq_ref[0, i * t:(i + 1) * t, :] = (dqt_sc[i].T * MLA_SCALE).astype(_MXU)

    seq = lambda w: pl.BlockSpec((1, S, w), lambda b, g, j: (b, 0, g))
    blk = lambda w: pl.BlockSpec((1, t, w), lambda b, g, j: (b, j, g))
    stat = pl.BlockSpec((1, NP, nk, 2, t), lambda b, g, j: (b, g, 0, 0, 0))
    return pl.pallas_call(
        body, name="mla_attn_bwd", grid=(B, 8 // HG, nk),
        in_specs=[seq(128 * HG), blk(128 * HG), blk(64 * HG),
                  pl.BlockSpec((1, 1, 128 * HG, t), lambda b, g, j: (b, j, g, 0)), seq(64 * HG), stat, stat],
        out_specs=[seq(128 * HG), blk(128 * HG), blk(64 * HG)],
        out_shape=[jax.ShapeDtypeStruct((B, S, 1024), _MXU), jax.ShapeDtypeStruct((B, S, 1024), _MXU),
                   jax.ShapeDtypeStruct((B, S, 512), _MXU)],
        scratch_shapes=[pltpu.VMEM((HG, t, t), F32), pltpu.VMEM((HG, t, t), F32), pltpu.VMEM((HG, t, t), F32),
                        pltpu.VMEM((HG, t, t), F32), pltpu.VMEM((nk, 128 * HG, t), F32),
                        pltpu.VMEM((HG, t, 128), F32), pltpu.VMEM((NP, t, 128), F32)],
        compiler_params=_cp(("parallel", "parallel", "arbitrary"), 56),
    )(q, k, v, kt, do, lse, dl)


def _mla_prep_bwd(mla_p, cos, sin, qnw, kvnw, w_uq, w_ukv, dq, dk, dv):
    B, S, _ = mla_p.shape
    tm = min(S, 512)

    def body(p_ref, c_ref, s_ref, qn_ref, kn_ref, wq_ref, wkv_ref, dq_ref, dk_ref, dv_ref,
             dp_ref, dwq_ref, dwkv_ref, dqn_ref, dkn_ref):
        first = (pl.program_id(0) == 0) & (pl.program_id(1) == 0)

        @pl.when(first)
        def _():
            dwq_ref[...] = jnp.zeros_like(dwq_ref)
            dwkv_ref[...] = jnp.zeros_like(dwkv_ref)
            dqn_ref[...] = jnp.zeros_like(dqn_ref)
            dkn_ref[...] = jnp.zeros_like(dkn_ref)

        p = p_ref[0].astype(F32)
        cs, sn = c_ref[0], s_ref[0]
        lane = lax.broadcasted_iota(jnp.int32, (1, 128), 1)
        pe = (lane >= 64) & (lane < 96)
        qh, q_rstd, qn = _rms(p[:, 0:256], qn_ref[...])
        kvh, kv_rstd, kvn = _rms(p[:, 256:384], kn_ref[...])
        dqv = dq_ref[0].astype(F32)
        dkv = dk_ref[0].astype(F32)
        dqpre = jnp.concatenate(
            [_rope128_t(dqv[:, 128 * h:128 * h + 128], cs, sn) for h in range(8)], axis=1)
        dkpe = jnp.zeros((tm, 128), F32)
        for h in range(8):
            dkpe = dkpe + jnp.where(pe, dkv[:, 128 * h:128 * h + 128], 0.0)
        dkr = _rope128_t(dkpe, cs, sn)
        dkv_all = jnp.concatenate([dkv, dv_ref[0].astype(F32)], axis=1)
        d_qn = _mm_nt(dqpre, wq_ref[...])
        d_kvn = _mm_nt(dkv_all, wkv_ref[...])
        dwq_ref[...] += _mm_tn(qn, dqpre)
        dwkv_ref[...] += _mm_tn(kvn, dkv_all)
        dqn_ref[...] += jnp.sum(d_qn * qh, axis=0, keepdims=True)
        dkn_ref[...] += jnp.sum(d_kvn * kvh, axis=0, keepdims=True)
        dp_ref[0] = jnp.concatenate([_rms_bwd(d_qn, qh, q_rstd, qn_ref[...]),
                                     _rms_bwd(d_kvn, kvh, kv_rstd, kn_ref[...]), dkr], axis=1).astype(_MXU)

    tok = lambda w: pl.BlockSpec((1, tm, w), lambda b, i: (b, i, 0))
    return pl.pallas_call(
        body, name="mla_prep_bwd", grid=(B, S // tm),
        in_specs=[tok(512), tok(128), tok(128), _full((1, 256)), _full((1, 128)), _full((256, 1024)),
                  _full((128, 1536)), tok(1024), tok(1024), tok(512)],
        out_specs=[tok(512), _full((256, 1024)), _full((128, 1536)), _full((1, 256)), _full((1, 128))],
        out_shape=[jax.ShapeDtypeStruct((B, S, 512), _MXU), jax.ShapeDtypeStruct((256, 1024), F32),
                   jax.ShapeDtypeStruct((128, 1536), F32), jax.ShapeDtypeStruct((1, 256), F32),
                   jax.ShapeDtypeStruct((1, 128), F32)],
        compiler_params=_cp(("arbitrary", "arbitrary")),
    )(mla_p, cos, sin, qnw, kvnw, w_uq, w_ukv, dq, dk, dv)


def _out_fwd(x, gate, r_g, o_mla, mla_p, g_g, w_out):
    B, S, D = x.shape
    tm = min(S, 512)

    def body(x_ref, g_ref, r_ref, o_ref, z_ref, gg_ref, w_ref, xn_ref, y_ref):
        mm = (o_ref[0] * _silu(z_ref[0].astype(F32))).astype(_MXU)
        y = (jnp.dot(r_ref[0], w_ref[0:256, :], preferred_element_type=F32)
             + jnp.dot(mm, w_ref[256:768, :], preferred_element_type=F32)
             + jnp.dot(gg_ref[0], w_ref[768:1024, :], preferred_element_type=F32))
        y_ref[0] = y.astype(_MXU)
        xn_ref[0] = x_ref[0] + g_ref[0] * y

    tok = lambda w, c=0: pl.BlockSpec((1, tm, w), lambda b, i: (b, i, c))
    return pl.pallas_call(
        body, name="out_fwd", grid=(B, S // tm),
        in_specs=[tok(D), pl.BlockSpec((1, 1, D), lambda b, i: (b, 0, 0)), tok(256), tok(512), tok(512, 1),
                  tok(256), _full((D, D))],
        out_specs=[tok(D), tok(D)],
        out_shape=[jax.ShapeDtypeStruct((B, S, D), F32), jax.ShapeDtypeStruct((B, S, D), _MXU)],
        compiler_params=_cp(("parallel", "parallel")),
    )(x, gate, r_g, o_mla, mla_p, g_g, w_out)


def _out_bwd(dx, y, gate, r_g, g_g, w_out, o_mla, mla_p, after=None):
    B, S, D = dx.shape
    tm = min(S, 512)
    t = min(MLA_T, S)
    nt = tm // t

    def body(dx_ref, y_ref, g_ref, r_ref, gg_ref, w_ref, o_ref, z_ref,
             dr_ref, do_ref, dz_ref, dl_ref, dg_ref, dw_ref, dgate_ref, acc):
        first = (pl.program_id(0) == 0) & (pl.program_id(1) == 0)

        @pl.when(first)
        def _():
            acc[...] = jnp.zeros_like(acc)

        @pl.when(pl.program_id(1) == 0)
        def _():
            dgate_ref[...] = jnp.zeros_like(dgate_ref)

        dxv = dx_ref[0]
        dgate_ref[0] += jnp.sum(dxv * y_ref[0].astype(F32), axis=0, keepdims=True)
        dy = (dxv * g_ref[0]).astype(_MXU)
        dr_ref[0] = _mm_nt(dy, w_ref[0:256, :])
        dg_ref[0] = _mm_nt(dy, w_ref[768:1024, :])
        ov, z = o_ref[0], z_ref[0].astype(F32)
        acc[0:256, :] += _mm_tn(r_ref[0], dy)
        acc[256:768, :] += _mm_tn((ov * _silu(z)).astype(_MXU), dy)
        acc[768:1024, :] += _mm_tn(gg_ref[0], dy)

        @pl.when((pl.program_id(0) == B - 1) & (pl.program_id(1) == S // tm - 1))
        def _():
            dw_ref[...] = acc[...].astype(_MXU)

        dm = _mm_nt(dy, w_ref[256:768, :])
        do = dm * _silu(z)
        dz_ref[0] = (dm * ov * _dsilu(z)).astype(_MXU)
        do_ref[0] = do.astype(_MXU)
        prod = do * ov
        for pr in range(4):
            pt = prod[:, 128 * pr:128 * pr + 128].T
            se = jnp.sum(pt[0:64], axis=0, keepdims=True)
            so = jnp.sum(pt[64:128], axis=0, keepdims=True)
            for n in range(nt):
                dl_ref[0, pr, n, 0:1, :] = se[:, n * t:(n + 1) * t]
                dl_ref[0, pr, n, 1:2, :] = so[:, n * t:(n + 1) * t]

    tok = lambda w, c=0: pl.BlockSpec((1, tm, w), lambda b, i: (b, i, c))
    per_seq = pl.BlockSpec((1, 1, D), lambda b, i: (b, 0, 0))
    body, extra, extra_specs = _behind(body, 8, after)
    return pl.pallas_call(
        body, name="out_bwd", grid=(B, S // tm),
        in_specs=[tok(D), tok(D), per_seq, tok(256), tok(256), _full((D, D)), tok(512), tok(512, 1)] + extra_specs,
        out_specs=[tok(256), tok(512), tok(512), pl.BlockSpec((1, 4, nt, 2, t), lambda b, i: (b, 0, i, 0, 0)),
                   tok(256), _full((D, D)), per_seq],
        out_shape=[jax.ShapeDtypeStruct((B, S, 256), F32), jax.ShapeDtypeStruct((B, S, 512), _MXU),
                   jax.ShapeDtypeStruct((B, S, 512), _MXU), jax.ShapeDtypeStruct((B, 4, S // t, 2, t), F32),
                   jax.ShapeDtypeStruct((B, S, 256), F32), jax.ShapeDtypeStruct((D, D), _MXU),
                   jax.ShapeDtypeStruct((B, 1, D), F32)],
        scratch_shapes=[pltpu.VMEM((D, D), F32)],
        compiler_params=_cp(("arbitrary", "arbitrary")),
    )(dx, y, gate, r_g, g_g, w_out, o_mla, mla_p, *extra)


def _proj_bwd_x(x, shift, scale, nw, w_arr, d_ret, d_mla, d_mz, d_gla, dx_out, after=None):
    B, S, D = x.shape
    tm = min(S, 512)

    def body(x_ref, sc_ref, nw_ref, w_ref, dr_ref, dm_ref, dz_ref, dg_ref, dxo_ref,
             dx_ref, dsh_ref, dsc_ref, dnw_ref):
        first = (pl.program_id(0) == 0) & (pl.program_id(1) == 0)

        @pl.when(first)
        def _():
            dnw_ref[...] = jnp.zeros_like(dnw_ref)

        @pl.when(pl.program_id(1) == 0)
        def _():
            dsh_ref[...] = jnp.zeros_like(dsh_ref)
            dsc_ref[...] = jnp.zeros_like(dsc_ref)

        dp = jnp.concatenate([dr_ref[0], dm_ref[0], dz_ref[0], dg_ref[0]], axis=1)
        dh = lax.dot_general(dp, w_ref[...], (((1,), (1,)), ((), ())), preferred_element_type=F32)
        xv = x_ref[0]
        rstd = lax.rsqrt(jnp.mean(xv * xv, axis=-1, keepdims=True) + EPS)
        xh = xv * rstd
        nwv = nw_ref[...]
        mod = 1.0 + sc_ref[0]
        dsh_ref[0] += jnp.sum(dh, axis=0, keepdims=True)
        dsc_ref[0] += jnp.sum(dh * xh * nwv, axis=0, keepdims=True)
        dnw_ref[...] += jnp.sum(dh * xh * mod, axis=0, keepdims=True)
        dxh = dh * nwv * mod
        dx_ref[0] = dxo_ref[0] + rstd * (dxh - xh * jnp.mean(dxh * xh, axis=-1, keepdims=True))

    tok = lambda w: pl.BlockSpec((1, tm, w), lambda b, i: (b, i, 0))
    per_seq = pl.BlockSpec((1, 1, D), lambda b, i: (b, 0, 0))
    body, extra, extra_specs = _behind(body, 9, after)
    return pl.pallas_call(
        body, name="proj_bwd_x", grid=(B, S // tm),
        in_specs=[tok(D), per_seq, _full((1, D)), _full((D, ARR_W)), tok(RET_W), tok(512), tok(512),
                  tok(GLA_W), tok(D)] + extra_specs,
        out_specs=[tok(D), per_seq, per_seq, _full((1, D))],
        out_shape=[jax.ShapeDtypeStruct((B, S, D), F32), jax.ShapeDtypeStruct((B, 1, D), F32),
                   jax.ShapeDtypeStruct((B, 1, D), F32), jax.ShapeDtypeStruct((1, D), F32)],
        compiler_params=_cp(("arbitrary", "arbitrary")),
    )(x, scale, nw, w_arr, d_ret, d_mla, d_mz, d_gla, dx_out, *extra)


def _proj_bwd_w(h, d_ret, d_mla, d_mz, d_gla):
    B, S, D = h.shape
    tm = min(S, 512)

    def body(h_ref, dr_ref, dm_ref, dz_ref, dg_ref, dw_ref, acc):
        first = (pl.program_id(0) == 0) & (pl.program_id(1) == 0)

        @pl.when(first)
        def _():
            acc[...] = jnp.zeros_like(acc)

        hv = h_ref[0]
        tn = lambda d_ref: lax.dot_general(hv, d_ref[0], (((0,), (0,)), ((), ())), preferred_element_type=F32)
        acc[:, 0:RET_W] += tn(dr_ref)
        acc[:, RET_W:RET_W + 512] += tn(dm_ref)
        acc[:, RET_W + 512:RET_W + MLA_W] += tn(dz_ref)
        acc[:, RET_W + MLA_W:ARR_W] += tn(dg_ref)

        @pl.when((pl.program_id(0) == B - 1) & (pl.program_id(1) == S // tm - 1))
        def _():
            dw_ref[...] = acc[...].astype(_MXU)

    tok = lambda w: pl.BlockSpec((1, tm, w), lambda b, i: (b, i, 0))
    return pl.pallas_call(
        body, name="proj_bwd_w", grid=(B, S // tm),
        in_specs=[tok(D), tok(RET_W), tok(512), tok(512), tok(GLA_W)],
        out_specs=_full((D, ARR_W)), out_shape=jax.ShapeDtypeStruct((D, ARR_W), _MXU),
        scratch_shapes=[pltpu.VMEM((D, ARR_W), F32)],
        compiler_params=_cp(("arbitrary", "arbitrary"), 56),
    )(h, d_ret, d_mla, d_mz, d_gla)


def _out_fwd_loss(x, gate, r_g, o_mla, mla_p, g_g, w_out, fw, target):
    B, S, D = x.shape
    tm = min(S, 512)

    def body(x_ref, g_ref, r_ref, o_ref, z_ref, gg_ref, w_ref, fw_ref, t_ref, dx_ref, y_ref, loss_ref, dfw_ref):
        first = (pl.program_id(0) == 0) & (pl.program_id(1) == 0)

        @pl.when(first)
        def _():
            loss_ref[...] = jnp.zeros_like(loss_ref)
            dfw_ref[...] = jnp.zeros_like(dfw_ref)

        mm = (o_ref[0] * _silu(z_ref[0].astype(F32))).astype(_MXU)
        y = (jnp.dot(r_ref[0], w_ref[0:256, :], preferred_element_type=F32)
             + jnp.dot(mm, w_ref[256:768, :], preferred_element_type=F32)
             + jnp.dot(gg_ref[0], w_ref[768:1024, :], preferred_element_type=F32))
        y_ref[0] = y.astype(_MXU)
        xv = x_ref[0] + g_ref[0] * y
        fwv = fw_ref[...]
        rstd = lax.rsqrt(jnp.mean(xv * xv, axis=-1, keepdims=True) + EPS)
        xh = xv * rstd
        err = xh * fwv - t_ref[0]
        loss_ref[...] += 0.5 * jnp.sum(jnp.mean(err * err, axis=-1, keepdims=True), axis=0, keepdims=True)
        dy = err * (1.0 / D)
        dfw_ref[...] += jnp.sum(dy * xh, axis=0, keepdims=True)
        dxh = dy * fwv
        dx_ref[0] = rstd * (dxh - xh * jnp.mean(dxh * xh, axis=-1, keepdims=True))

    tok = lambda w, c=0: pl.BlockSpec((1, tm, w), lambda b, i: (b, i, c))
    return pl.pallas_call(
        body, name="out_fwd_loss", grid=(B, S // tm),
        in_specs=[tok(D), pl.BlockSpec((1, 1, D), lambda b, i: (b, 0, 0)), tok(256), tok(512), tok(512, 1),
                  tok(256), _full((D, D)), _full((1, D)), tok(D)],
        out_specs=[tok(D), tok(D), _full((1, 1)), _full((1, D))],
        out_shape=[jax.ShapeDtypeStruct((B, S, D), F32), jax.ShapeDtypeStruct((B, S, D), _MXU),
                   jax.ShapeDtypeStruct((1, 1), F32), jax.ShapeDtypeStruct((1, D), F32)],
        compiler_params=_cp(("arbitrary", "arbitrary")),
    )(x, gate, r_g, o_mla, mla_p, g_g, w_out, fw, target)


def _local_step(x, pos3, mod, loss_target, small, w_in_a, w_uq_a, w_ukv_a, w_out_b):
    B, S, D = x.shape
    tabs = _rope_tables(pos3)
    saved = []
    for l in range(DEPTH):
        last = (small["final_norm"].reshape(1, D), loss_target) if l == DEPTH - 1 else None
        x, s = _layer_fwd(x, tabs, mod[l], {n: a[l] for n, a in small.items() if n != "final_norm"},
                          w_in_a[l], w_uq_a[l], w_ukv_a[l], w_ukv_a[l].T, w_out_b[l], loss_head=last)
        saved.append(s)
    dx, loss, d_fw = x
    grads = dict(final_norm=d_fw.reshape(D))
    per_layer = [None] * DEPTH
    for l in reversed(range(DEPTH)):
        dx, per_layer[l] = _layer_bwd(dx, saved[l], tabs)
    for name in per_layer[0]:
        grads[name] = jnp.stack([per_layer[l][name] for l in range(DEPTH)])
    return loss, dx, grads


def _layer_fwd(x, tabs, mod_l, small_l, w_in_a, w_uq_a=None, w_ukv_a=None, w_ukv_t=None, w_out_b=None, late_weights=None,
               loss_head=None):
    B, S, D = x.shape
    cr, sr, cm, sm = tabs
    shift = mod_l[:, 0:D].reshape(B, 1, D)
    scale = mod_l[:, D:2 * D].reshape(B, 1, D)
    gate = mod_l[:, 2 * D:3 * D].reshape(B, 1, D)
    nw = small_l["norm_w"].reshape(1, D)
    qnw = small_l["mla_q_norm"].reshape(1, 256)
    kvnw = small_l["mla_kv_norm"].reshape(1, 128)
    w_g2p = jnp.pad(small_l["gla_w_g2"], ((0, 112), (0, 0)))
    b_g2 = small_l["gla_b_g2"].reshape(1, 128)
    gnw = jnp.tile(small_l["gla_norm"], 4).reshape(1, 256)
    ret_p, mla_p, gla_p, h = _proj_fwd(x, shift, scale, nw, w_in_a)
    r_g, r_raw, r_st = _ret_fwd(ret_p, cr, sr)
    if late_weights is not None:
        w_uq_a, w_ukv_a, w_ukv_t, w_out_b = late_weights(r_raw)
    q, k, v, kt, vt = _mla_prep_fwd(mla_p, cm, sm, qnw, kvnw, w_uq_a, w_ukv_a, w_ukv_t)
    o_mla, lse = _mla_attn_fwd(q, k, vt)
    g_g, g_raw, g_st = _gla_fwd(gla_p, w_g2p, b_g2, gnw)
    if loss_head is None:
        x_new, y = _out_fwd(x, gate, r_g, o_mla, mla_p, g_g, w_out_b)
    else:
        dx, y, loss, d_fw = _out_fwd_loss(x, gate, r_g, o_mla, mla_p, g_g, w_out_b, *loss_head)
        x_new = (dx, loss, d_fw)
    saved = dict(x=x, shift=shift, scale=scale, gate=gate, nw=nw, qnw=qnw, kvnw=kvnw, w_g2p=w_g2p, b_g2=b_g2,
                 gnw=gnw, ret_p=ret_p, mla_p=mla_p, gla_p=gla_p, h=h, r_g=r_g, r_raw=r_raw, r_st=r_st, q=q, k=k,
                 v=v, kt=kt, o_mla=o_mla, lse=lse, g_g=g_g, g_raw=g_raw, g_st=g_st, y=y,
                 w_in_a=w_in_a, w_uq_a=w_uq_a, w_ukv_a=w_ukv_a, w_out_b=w_out_b)
    return x_new, saved


def _layer_bwd(dx, s, tabs, after=None, early_grads=None, early_w_in=None):
    B, S, D = dx.shape
    cr, sr, cm, sm = tabs
    d_r, do, d_mz, dl, d_g, dw_out, d_gate = _out_bwd(dx, s["y"], s["gate"], s["r_g"], s["g_g"], s["w_out_b"],
                                                      s["o_mla"], s["mla_p"], after=after)
    d_ret = _ret_bwd(s["ret_p"], cr, sr, s["r_raw"], s["r_st"], d_r)
    dq, dk, dv = _mla_attn_bwd(s["q"], s["k"], s["v"], s["kt"], do, s["lse"], dl)
    d_mla, dw_uq, dw_ukv, d_qnw, d_kvnw = _mla_prep_bwd(
        s["mla_p"], cm, sm, s["qnw"], s["kvnw"], s["w_uq_a"], s["w_ukv_a"], dq, dk, dv)
    sent = None if early_grads is None else early_grads(dw_out, dw_uq, dw_ukv)
    d_gla, dw_g2p, db_g2, d_gnw = _gla_bwd(s["gla_p"], s["w_g2p"], s["b_g2"], s["gnw"], s["g_raw"], s["g_st"], d_g,
                                           after=sent)
    dw_in = _proj_bwd_w(s["h"], d_ret, d_mla, d_mz, d_gla)
    sent = None if early_w_in is None else early_w_in(dw_in)
    dx, d_shift, d_scale, d_nw = _proj_bwd_x(s["x"], s["shift"], s["scale"], s["nw"], s["w_in_a"],
                                             d_ret, d_mla, d_mz, d_gla, dx, after=sent)
    grads = dict(
        d_mod=jnp.concatenate([d_shift, d_scale, d_gate], axis=2).reshape(B, 3 * D),
        norm_w=d_nw.reshape(D), mla_q_norm=d_qnw.reshape(256), mla_kv_norm=d_kvnw.reshape(128),
        gla_w_g2=dw_g2p[0:16], gla_b_g2=db_g2.reshape(128), gla_norm256=d_gnw.reshape(256),
        w_in_a=dw_in, w_uq_a=dw_uq, w_ukv_a=dw_ukv, w_out=dw_out)
    return dx, grads


def _exchange(arrs, gather, name, after=None):
    n = len(arrs)
    out_shape = [jax.ShapeDtypeStruct(((N_DEV,) + a.shape) if g else a.shape, a.dtype)
                 for a, g in zip(arrs, gather)]

    def body(*refs):
        ins, outs = refs[:n], refs[n:2 * n]
        send_sems, recv_sems, local_sems = refs[2 * n:]
        ix, iy, ic = lax.axis_index("x"), lax.axis_index("y"), lax.axis_index("c")
        me = 4 * ix + 2 * iy + ic
        copies = []
        for a in range(n):
            mine = ins[a] if gather[a] else ins[a].at[me]
            loc = pltpu.make_async_copy(mine, outs[a].at[me], local_sems.at[a])
            loc.start()
            copies.append(loc)
            for d in range(1, N_DEV):
                px = 1 - ix if d & 4 else ix
                py = 1 - iy if d & 2 else iy
                pc = 1 - ic if d & 1 else ic
                src = ins[a] if gather[a] else ins[a].at[4 * px + 2 * py + pc]
                cp = pltpu.make_async_remote_copy(
                    src_ref=src, dst_ref=outs[a].at[me], send_sem=send_sems.at[a, d - 1],
                    recv_sem=recv_sems.at[a, d - 1], device_id=(px, py, pc), device_id_type=pl.DeviceIdType.MESH)
                cp.start()
                copies.append(cp)
        for cp in copies:
            cp.wait()

    any_spec = pl.BlockSpec(memory_space=pl.ANY)
    body, extra, extra_specs = _behind(body, n, after)
    outs = pl.pallas_call(
        body, name=name, in_specs=[any_spec] * n + extra_specs, out_specs=[any_spec] * n, out_shape=out_shape,
        scratch_shapes=[pltpu.SemaphoreType.DMA((n, N_DEV - 1)), pltpu.SemaphoreType.DMA((n, N_DEV - 1)),
                        pltpu.SemaphoreType.DMA((n,))],
    )(*arrs, *extra)
    return list(outs)


def _peers(ix, iy, ic):
    out = []
    for d in range(1, N_DEV):
        px = 1 - ix if d & 4 else ix
        py = 1 - iy if d & 2 else iy
        pc = 1 - ic if d & 1 else ic
        out.append((d - 1, (px, py, pc), 4 * px + 2 * py + pc))
    return out


def _exchange_start(arrs, gather, name, after=None):
    n = len(arrs)
    lands = [lax.empty(((N_DEV,) + a.shape) if g else a.shape, a.dtype) for a, g in zip(arrs, gather)]
    extra = [] if after is None else [after]

    def body(*refs):
        ins, land_refs = refs[:n], refs[n:2 * n]
        send_sems, recv_sems = refs[2 * n + len(extra)], refs[2 * n + len(extra) + 1]
        token = refs[-1]
        ix, iy, ic = lax.axis_index("x"), lax.axis_index("y"), lax.axis_index("c")
        me = 4 * ix + 2 * iy + ic
        for a in range(n):
            for k, peer, peer_idx in _peers(ix, iy, ic):
                pltpu.make_async_remote_copy(
                    src_ref=ins[a] if gather[a] else ins[a].at[peer_idx], dst_ref=land_refs[a].at[me],
                    send_sem=send_sems.at[7 * a + k], recv_sem=recv_sems.at[7 * a + k], device_id=peer,
                    device_id_type=pl.DeviceIdType.MESH).start()
        token[...] = jnp.zeros_like(token)

    hbm = pl.BlockSpec(memory_space=pltpu.HBM)
    sem = pl.BlockSpec(memory_space=pltpu.SEMAPHORE)
    held = [pltpu.with_memory_space_constraint(a, pltpu.HBM) for a in list(arrs) + lands]
    outs = pl.pallas_call(
        body, name=name,
        out_shape=(pltpu.SemaphoreType.DMA((7 * n,)), pltpu.SemaphoreType.DMA((7 * n,)),
                   *[pltpu.HBM(a.shape, a.dtype) for a in held], jax.ShapeDtypeStruct((8, 128), F32)),
        in_specs=[hbm] * (2 * n) + [pl.BlockSpec(memory_space=pl.ANY)] * len(extra),
        out_specs=(sem, sem, *[hbm] * (2 * n), pl.BlockSpec(memory_space=pltpu.VMEM)),
        input_output_aliases={a: 2 + a for a in range(2 * n)},
        compiler_params=pltpu.CompilerParams(has_side_effects=pltpu.SideEffectType.DATAFLOW_SIDE_EFFECTING),
    )(*held, *extra)
    return dict(send=outs[0], recv=outs[1], srcs=list(outs[2:2 + n]), lands=list(outs[2 + n:2 + 2 * n]),
                token=outs[-1], gather=list(gather))


def _exchange_wait(flight, after, me, name):
    n = len(flight["srcs"])
    gather = flight["gather"]

    def body(*refs):
        srcs, land_refs = refs[:n], refs[n:2 * n]
        send_sems, recv_sems = refs[2 * n], refs[2 * n + 1]
        ix, iy, ic = lax.axis_index("x"), lax.axis_index("y"), lax.axis_index("c")
        mine = 4 * ix + 2 * iy + ic
        for a in range(n):
            for k, peer, peer_idx in _peers(ix, iy, ic):
                cp = pltpu.make_async_remote_copy(
                    src_ref=srcs[a] if gather[a] else srcs[a].at[peer_idx], dst_ref=land_refs[a].at[mine],
                    send_sem=send_sems.at[7 * a + k], recv_sem=recv_sems.at[7 * a + k], device_id=peer,
                    device_id_type=pl.DeviceIdType.MESH)
                cp.wait_send()
                cp.wait_recv()

    hbm = pl.BlockSpec(memory_space=pltpu.HBM)
    sem = pl.BlockSpec(memory_space=pltpu.SEMAPHORE)
    held = flight["srcs"] + flight["lands"]
    outs = pl.pallas_call(
        body, name=name, out_shape=tuple(pltpu.HBM(a.shape, a.dtype) for a in held),
        in_specs=[hbm] * (2 * n) + [sem, sem, pl.BlockSpec(memory_space=pl.ANY)], out_specs=tuple([hbm] * (2 * n)),
        input_output_aliases={a: a for a in range(2 * n)},
        compiler_params=pltpu.CompilerParams(has_side_effects=pltpu.SideEffectType.DATAFLOW_SIDE_EFFECTING),
    )(*held, flight["send"], flight["recv"], after)
    got = []
    for a in range(n):
        src, land = outs[a], outs[n + a]
        own = src if gather[a] else lax.dynamic_index_in_dim(src, me, axis=0, keepdims=False)
        got.append(lax.dynamic_update_index_in_dim(land, own, me, axis=0))
    return got


def _ada_fwd(c_all, ada_w, ada_b_cols):
    nb, D = c_all.shape
    cols = ada_w.shape[2]

    def body(c_ref, w_ref, b_ref, out_ref):
        ca = _silu(c_ref[...])
        for l in range(DEPTH):
            out_ref[l] = _mm(ca, w_ref[l]) + b_ref[l:l + 1, :]

    return pl.pallas_call(
        body, name="ada_fwd", out_shape=jax.ShapeDtypeStruct((DEPTH, nb, cols), F32),
        in_specs=[pl.BlockSpec(memory_space=pltpu.VMEM)] * 3, out_specs=pl.BlockSpec(memory_space=pltpu.VMEM),
        compiler_params=pltpu.CompilerParams(vmem_limit_bytes=32 * VMEM_MB),
    )(c_all, ada_w, ada_b_cols)


def _ada_bwd(c_all, d_mod_cols):
    nb, D = c_all.shape
    cols = d_mod_cols.shape[2]

    def body(c_ref, dm_ref, out_ref):
        ca = _silu(c_ref[...])
        for l in range(DEPTH):
            out_ref[l] = _mm_tn(ca, dm_ref[l])

    return pl.pallas_call(
        body, name="ada_bwd", out_shape=jax.ShapeDtypeStruct((DEPTH, D, cols), F32),
        in_specs=[pl.BlockSpec(memory_space=pltpu.VMEM)] * 2, out_specs=pl.BlockSpec(memory_space=pltpu.VMEM),
        compiler_params=pltpu.CompilerParams(vmem_limit_bytes=32 * VMEM_MB),
    )(c_all, d_mod_cols)


def _sum_adamw(parts, w, m, v, name, after=None):
    P, R, C = parts.shape
    tr = 256 if (R % 256 == 0 and R > 256) else R
    extra = [] if after is None else [after]

    def body(p_ref, w_ref, m_ref, v_ref, *rest):
        g_ref, d_ref, nm_ref, nv_ref = rest[-4:]
        g = p_ref[0].astype(F32)
        for k in range(1, P):
            g = g + p_ref[k].astype(F32)
        g_ref[...] = g
        nm = ADAM_B1 * m_ref[...] + (1.0 - ADAM_B1) * g
        nv = ADAM_B2 * v_ref[...] + (1.0 - ADAM_B2) * (g * g)
        nm_ref[...] = nm
        nv_ref[...] = nv
        m_hat = nm / (1.0 - ADAM_B1 ** ADAM_STEP)
        v_hat = nv / (1.0 - ADAM_B2 ** ADAM_STEP)
        d_ref[...] = -ADAM_LR * (m_hat / (jnp.sqrt(v_hat) + ADAM_EPS) + ADAM_WD * w_ref[...])

    blk = pl.BlockSpec((tr, C), lambda i: (i, 0))
    shp = jax.ShapeDtypeStruct((R, C), F32)
    return pl.pallas_call(
        body, name=name, grid=(R // tr,),
        in_specs=[pl.BlockSpec((P, tr, C), lambda i: (0, i, 0)), blk, blk, blk]
        + [pl.BlockSpec(memory_space=pl.ANY)] * len(extra),
        out_specs=[blk, blk, blk, blk], out_shape=[shp, shp, shp, shp],
        compiler_params=_cp(("parallel",)),
    )(parts, w, m, v, *extra)


def _sum_adamw_layer(parts, w, m, v, layer, name, prev=None, after=None):
    P, R, C = parts.shape
    tr = 256 if (R % 256 == 0 and R > 256) else R

    def body(p_ref, w_ref, m_ref, v_ref, *rest):
        g_ref, d_ref, nm_ref, nv_ref = rest[-4:]
        g = p_ref[0].astype(F32)
        for k in range(1, P):
            g = g + p_ref[k].astype(F32)
        g_ref[0] = g
        nm = ADAM_B1 * m_ref[0] + (1.0 - ADAM_B1) * g
        nv = ADAM_B2 * v_ref[0] + (1.0 - ADAM_B2) * (g * g)
        nm_ref[0] = nm
        nv_ref[0] = nv
        m_hat = nm / (1.0 - ADAM_B1 ** ADAM_STEP)
        v_hat = nv / (1.0 - ADAM_B2 ** ADAM_STEP)
        d_ref[0] = -ADAM_LR * (m_hat / (jnp.sqrt(v_hat) + ADAM_EPS) + ADAM_WD * w_ref[0])

    blk = pl.BlockSpec((1, tr, C), lambda i: (layer, i, 0))
    shp = jax.ShapeDtypeStruct(w.shape, F32)
    in_specs = [pl.BlockSpec((P, tr, C), lambda i: (0, i, 0)), blk, blk, blk]
    args = [parts, w, m, v]
    aliases = {}
    if prev is not None:
        in_specs += [pl.BlockSpec(memory_space=pl.ANY)] * 4
        args += list(prev)
        aliases = {4 + k: k for k in range(4)}
    if after is not None:
        in_specs.append(pl.BlockSpec(memory_space=pl.ANY))
        args.append(after)
    return list(pl.pallas_call(
        body, name=name, grid=(R // tr,), in_specs=in_specs, out_specs=[blk] * 4, out_shape=[shp] * 4,
        input_output_aliases=aliases, compiler_params=_cp(("parallel",)),
    )(*args))


SMALL = ["norm_w", "mla_q_norm", "mla_kv_norm", "gla_w_g2", "gla_b_g2", "gla_norm", "final_norm"]


SMALL_ROWS = 72


def _pack_small(loss, part):
    flat = [jnp.pad(loss.reshape(1), (0, 127))] + [part[n].reshape(-1) for n in SMALL]
    used = sum(f.shape[0] for f in flat)
    flat.append(jnp.zeros((SMALL_ROWS * 128 - used,), F32))
    return jnp.concatenate(flat).reshape(SMALL_ROWS, 128)


def _small_adamw(packed_parts, w, m, v, after=None):
    n = len(w)
    extra = [] if after is None else [after]

    def body(*refs):
        p_ref = refs[0]
        w_refs, m_refs, v_refs = refs[1:1 + n], refs[1 + n:1 + 2 * n], refs[1 + 2 * n:1 + 3 * n]
        outs, acc = refs[1 + 3 * n + len(extra):-1], refs[-1]
        total = p_ref[0]
        for k in range(1, N_DEV):
            total = total + p_ref[k]
        acc[...] = total
        outs[0][...] = acc[0:1, :]
        r0 = 1
        for i in range(n):
            shp = w_refs[i].shape
            if len(shp) == 3:
                g = acc[r0:r0 + shp[0] * shp[1], :].reshape(shp)
                r0 += shp[0] * shp[1]
            elif shp[1] < 128:
                g = acc[r0:r0 + shp[0], 0:shp[1]]
                r0 += shp[0]
            else:
                k = shp[1] // 128
                g = jnp.concatenate(
                    [jnp.concatenate([acc[r0 + l * k + j:r0 + l * k + j + 1, :] for j in range(k)], axis=1)
                     for l in range(shp[0])], axis=0)
                r0 += shp[0] * k
            nm = ADAM_B1 * m_refs[i][...] + (1.0 - ADAM_B1) * g
            nv = ADAM_B2 * v_refs[i][...] + (1.0 - ADAM_B2) * (g * g)
            m_hat = nm / (1.0 - ADAM_B1 ** ADAM_STEP)
            v_hat = nv / (1.0 - ADAM_B2 ** ADAM_STEP)
            outs[1 + 4 * i][...] = g
            outs[2 + 4 * i][...] = -ADAM_LR * (m_hat / (jnp.sqrt(v_hat) + ADAM_EPS) + ADAM_WD * w_refs[i][...])
            outs[3 + 4 * i][...] = nm
            outs[4 + 4 * i][...] = nv

    vmem = pl.BlockSpec(memory_space=pltpu.VMEM)
    out_shape = [jax.ShapeDtypeStruct((1, 128), F32)]
    for a in w:
        out_shape += [jax.ShapeDtypeStruct(a.shape, F32)] * 4
    outs = pl.pallas_call(
        body, name="adamw_small", in_specs=[vmem] * (1 + 3 * n) + [pl.BlockSpec(memory_space=pl.ANY)] * len(extra),
        out_specs=[vmem] * (1 + 4 * n), out_shape=out_shape, scratch_shapes=[pltpu.VMEM((SMALL_ROWS, 128), F32)],
    )(packed_parts, *w, *m, *v, *extra)
    return outs[0], [outs[1 + 4 * i:5 + 4 * i] for i in range(n)]


WEIGHTS = ["norm_w", "ada_w", "ada_b", "w_in", "mla_q_norm", "w_uq", "mla_kv_norm", "w_ukv", "gla_w_g2",
           "gla_b_g2", "gla_norm", "w_out", "final_norm"]


def kernel(x, c, positions, norm_w, ada_w, ada_b, w_in, mla_q_norm, w_uq, mla_kv_norm, w_ukv, gla_w_g2, gla_b_g2, gla_norm, w_out, final_norm, loss_target, m_norm_w, m_ada_w, m_ada_b, m_w_in, m_mla_q_norm, m_w_uq, m_mla_kv_norm, m_w_ukv, m_gla_w_g2, m_gla_b_g2, m_gla_norm, m_w_out, m_final_norm, v_norm_w, v_ada_w, v_ada_b, v_w_in, v_mla_q_norm, v_w_uq, v_mla_kv_norm, v_w_ukv, v_gla_w_g2, v_gla_b_g2, v_gla_norm, v_w_out, v_final_norm):
    w = dict(norm_w=norm_w, ada_w=ada_w, ada_b=ada_b, w_in=w_in, mla_q_norm=mla_q_norm, w_uq=w_uq,
             mla_kv_norm=mla_kv_norm, w_ukv=w_ukv, gla_w_g2=gla_w_g2, gla_b_g2=gla_b_g2, gla_norm=gla_norm,
             w_out=w_out, final_norm=final_norm)
    m = dict(norm_w=m_norm_w, ada_w=m_ada_w, ada_b=m_ada_b, w_in=m_w_in, mla_q_norm=m_mla_q_norm, w_uq=m_w_uq,
             mla_kv_norm=m_mla_kv_norm, w_ukv=m_w_ukv, gla_w_g2=m_gla_w_g2, gla_b_g2=m_gla_b_g2,
             gla_norm=m_gla_norm, w_out=m_w_out, final_norm=m_final_norm)
    v = dict(norm_w=v_norm_w, ada_w=v_ada_w, ada_b=v_ada_b, w_in=v_w_in, mla_q_norm=v_mla_q_norm, w_uq=v_w_uq,
             mla_kv_norm=v_mla_kv_norm, w_ukv=v_w_ukv, gla_w_g2=v_gla_w_g2, gla_b_g2=v_gla_b_g2,
             gla_norm=v_gla_norm, w_out=v_w_out, final_norm=v_final_norm)
    B, S, D = x.shape
    me = 4 * lax.axis_index("x") + 2 * lax.axis_index("y") + lax.axis_index("c")
    ada_cols = ada_w.shape[2]
    cast = lambda a: a.astype(_MXU)

    sharded = ["w_in", "w_uq", "w_ukv", "w_out"]

    whole_in = _arrange_w_in
    whole_rest = lambda blks: (*_arrange_mla_weights(blks[0], blks[1]), blks[2].reshape(D, D))
    blocks_in = lambda dw_in_a: _unarrange_w_in(dw_in_a, N_DEV, w_in.shape[2])
    blocks_rest = lambda dw_out, dw_uq_a, dw_ukv_a: [
        *_unarrange_mla_weights(dw_uq_a, dw_ukv_a), dw_out.reshape(N_DEV, D // N_DEV, D).astype(jnp.bfloat16)]

    w_in0 = cast(w_in[0])
    upper = w_in0.shape[0] // 2
    flight_u = _exchange_start([w_in0[:upper]], [True], "gather_start_first")
    tabs = _rope_tables(positions.reshape(B, S, 1), flight_u["token"])
    (c_g,) = _exchange([c], [True], "gather_c", after=tabs[0])
    c_all = c_g.reshape(N_DEV * B, D)

    ada_b_cols = lax.dynamic_slice(ada_b, (0, me * ada_cols), (DEPTH, ada_cols))
    mod_cols = _ada_fwd(c_all, ada_w, ada_b_cols)
    mod_send = jnp.transpose(mod_cols.reshape(DEPTH, N_DEV, B, ada_cols), (1, 0, 2, 3))
    (mod_recv,) = _exchange([mod_send], [False], "scatter_mod")
    mod = jnp.transpose(mod_recv, (1, 2, 0, 3)).reshape(DEPTH, B, 3 * D)

    flight_i = _exchange_start([w_in0[upper:]], [True], "gather_start_second", after=mod)
    flight_r = _exchange_start([cast(w[n][0]) for n in sharded[1:]], [True] * 3, "gather_start_layer0",
                               after=flight_i["token"])
    flight_w = _exchange_start([cast(w[n][1]) for n in sharded], [True] * 4, "gather_start_layer1",
                               after=flight_r["token"])
    small_w = {n: w[n] for n in SMALL}
    layer_small = lambda l: {n: a[l] for n, a in small_w.items() if n != "final_norm"}
    late0 = lambda after: whole_rest(_exchange_wait(flight_r, after, me, "gather_wait_layer0"))
    (upper_g,) = _exchange_wait(flight_u, flight_w["token"], me, "gather_wait_first")
    (lower_g,) = _exchange_wait(flight_i, upper_g, me, "gather_wait_second")
    x1, saved0 = _layer_fwd(x, tabs, mod[0], layer_small(0), whole_in([upper_g, lower_g]), late_weights=late0)
    got1 = _exchange_wait(flight_w, x1, me, "gather_wait_layer1")
    (dx, loss, d_fw), saved1 = _layer_fwd(x1, tabs, mod[1], layer_small(1), whole_in([got1[0]]), *whole_rest(got1[1:]),
                                          loss_head=(final_norm.reshape(1, D), loss_target))

    dx, g1 = _layer_bwd(dx, saved1, tabs)
    flight_g = _exchange_start([blocks_in(g1["w_in_a"])] + blocks_rest(g1["w_out"], g1["w_uq_a"], g1["w_ukv_a"]),
                               [False] * 4, "grads_start_layer1")
    flights = {}

    def early0(dw_out, dw_uq_a, dw_ukv_a):
        flights["rest0"] = _exchange_start(blocks_rest(dw_out, dw_uq_a, dw_ukv_a), [False] * 3, "grads_start_layer0")
        return flights["rest0"]["token"]

    def early_in0(dw_in_a):
        flights["in0"] = _exchange_start([blocks_in(dw_in_a)], [False], "exchange_start_last")
        return flights["in0"]["token"]

    grad_x, g0 = _layer_bwd(dx, saved0, tabs, after=flight_g["token"], early_grads=early0, early_w_in=early_in0)
    parts1 = _exchange_wait(flight_g, grad_x, me, "grads_wait_layer1")
    rest0 = _exchange_wait(flights["rest0"], g0["w_in_a"], me, "grads_wait_layer0")

    both = lambda n: jnp.stack([g0[n], g1[n]])
    d_mod = both("d_mod")
    part = dict(norm_w=both("norm_w"), mla_q_norm=both("mla_q_norm"), mla_kv_norm=both("mla_kv_norm"),
                gla_w_g2=both("gla_w_g2"), gla_b_g2=both("gla_b_g2"), gla_norm=both("gla_norm256")[:, 0:128],
                final_norm=d_fw)
    flight_s = _exchange_start([d_mod, _pack_small(loss, part)], [True, True], "gather_small_start")
    flight_l = flights["in0"]
    res = {}
    behind = flight_s["token"]
    for a, name in enumerate(sharded):
        res[name] = _sum_adamw_layer(parts1[a], w[name], m[name], v[name], 1, "adamw_%s_layer1" % name, after=behind)
        behind = res[name][1]
    for a, name in enumerate(sharded[1:]):
        res[name] = _sum_adamw_layer(rest0[a], w[name], m[name], v[name], 0, "adamw_%s_layer0" % name,
                                     prev=res[name], after=behind)
        behind = res[name][1]
    (in0,) = _exchange_wait(flight_l, behind, me, "exchange_wait_last")
    res["w_in"] = _sum_adamw_layer(in0, w_in, m_w_in, v_w_in, 0, "adamw_w_in_layer0", prev=res["w_in"])
    behind = res["w_in"][1]

    d_mod_g, small_g = _exchange_wait(flight_s, behind, me, "gather_small_wait")
    d_mod_all = jnp.transpose(d_mod_g, (1, 0, 2, 3)).reshape(DEPTH, N_DEV * B, 3 * D)
    d_mod_cols = lax.dynamic_slice(d_mod_all, (0, 0, me * ada_cols), (DEPTH, N_DEV * B, ada_cols))
    g_ada_w = _ada_bwd(c_all, d_mod_cols)

    def update(name, parts2d, after):
        shp = w[name].shape
        two = lambda a: a.reshape(parts2d.shape[1:])
        out = _sum_adamw(parts2d, two(w[name]), two(m[name]), two(v[name]), "adamw_" + name, after=after)
        res[name] = [o.reshape(shp) for o in out]
        return out[1]

    behind = update("ada_w", g_ada_w.reshape(1, DEPTH * D, ada_cols), behind)
    behind = update("ada_b", jnp.transpose(d_mod_g, (0, 2, 1, 3)).reshape(N_DEV * B, DEPTH * 3 * D // 128, 128), behind)
    row = lambda a: a.reshape(1, D) if a.ndim == 1 else a
    loss_sum, small_out = _small_adamw(small_g, [row(w[n]) for n in SMALL], [row(m[n]) for n in SMALL],
                                       [row(v[n]) for n in SMALL], after=behind)
    for n, outs in zip(SMALL, small_out):
        res[n] = [o.reshape(w[n].shape) for o in outs]
    loss_out = loss_sum[0, 0]
    return (loss_out, grad_x, *[res[n][0] for n in WEIGHTS], *[res[n][1] for n in WEIGHTS],
            *[res[n][2] for n in WEIGHTS], *[res[n][3] for n in WEIGHTS])
```

```python
import functools
import math

import numpy as np
import jax
import jax.numpy as jnp
from jax import lax
from jax.experimental import pallas as pl
from jax.experimental.pallas import tpu as pltpu

F32 = jnp.float32
_MXU = jnp.bfloat16

D_MODEL = 1024
DEPTH = 2
CHUNK = 64
EPS = 1e-6
ROPE_THETA = 10000.0
N_DEV = 8

MLA_SCALE = 96.0 ** -0.5
RET_KSCALE = 64.0 ** -0.5
GLA_KSCALE = 32.0 ** -0.5
GLA_TAU = 16.0

ADAM_LR = 0.001
ADAM_B1 = 0.9
ADAM_B2 = 0.999
ADAM_EPS = 1e-08
ADAM_WD = 0.01
ADAM_STEP = 10

RET_W, MLA_W, GLA_W = 1024, 1024, 896
ARR_W = RET_W + MLA_W + GLA_W
VMEM_MB = 1024 * 1024


def _cp(sem, vmem_mb=48):
    return pltpu.CompilerParams(dimension_semantics=sem, vmem_limit_bytes=vmem_mb * VMEM_MB)


def _mm(a, b):
    return jnp.dot(a.astype(_MXU), b.astype(_MXU), preferred_element_type=F32)


def _mm_nt(a, b):
    return lax.dot_general(a.astype(_MXU), b.astype(_MXU), (((1,), (1,)), ((), ())),
                           preferred_element_type=F32)


def _mm_tn(a, b):
    return lax.dot_general(a.astype(_MXU), b.astype(_MXU), (((0,), (0,)), ((), ())),
                           preferred_element_type=F32)


def _mm_f32(a, b):
    return jnp.dot(a, b, precision=lax.Precision.HIGHEST, preferred_element_type=F32)


def _sig(z):
    return 1.0 / (1.0 + jnp.exp(-z))


def _silu(z):
    return z * _sig(z)


def _dsilu(z):
    s = _sig(z)
    return s * (1.0 + z * (1.0 - s))


def _full(shape):
    nd = len(shape)
    return pl.BlockSpec(shape, lambda *_: (0,) * nd)


def _behind(body, n_in, after):
    if after is None:
        return body, [], []

    def body_behind(*refs):
        body(*refs[:n_in], *refs[n_in + 1:])

    return body_behind, [after], [pl.BlockSpec(memory_space=pl.ANY)]


def _w_in_runs(block_cols):
    m, g = RET_W, RET_W + MLA_W
    whole = [(base + 64 * h + 32 * t, 32, base + 128 * t + 32 * h)
             for base in (0, 256) for t in range(2) for h in range(4)]
    whole += [(512, 512, 512), (1024, 384, m), (1408, 32, m + 448), (1440, 512, m + 512),
              (1952, 528, g), (2480, 256, g + 640)]
    zeros = [(m + 384, 64), (m + 480, 32), (g + 528, 112)]
    runs = []
    for src, n, dst in whole:
        while n:
            blk, off = divmod(src, block_cols)
            k = min(n, block_cols - off)
            runs.append((blk, off, k, dst))
            src, n, dst = src + k, n - k, dst + k
    return runs, zeros


def _arrange_w_in(blocks, tm=256):
    n, rows, cols = blocks.shape
    runs, zeros = _w_in_runs(cols)

    def arrange_w_in_kernel(b_ref, a_ref):
        for dst, k in zeros:
            a_ref[:, dst:dst + k] = jnp.zeros((tm, k), a_ref.dtype)
        for blk, off, k, dst in runs:
            a_ref[:, dst:dst + k] = b_ref[blk, :, off:off + k]

    return pl.pallas_call(
        arrange_w_in_kernel, grid=(rows // tm,),
        in_specs=[pl.BlockSpec((n, tm, cols), lambda i: (0, i, 0))],
        out_specs=pl.BlockSpec((tm, ARR_W), lambda i: (i, 0)),
        out_shape=jax.ShapeDtypeStruct((rows, ARR_W), blocks.dtype),
        compiler_params=_cp(("parallel",)), name="arrange_w_in")(blocks)


def _unarrange_w_in(a, n, cols, tm=256):
    rows = a.shape[0]
    runs, _ = _w_in_runs(cols)

    def unarrange_w_in_kernel(a_ref, b_ref):
        for blk, off, k, dst in runs:
            b_ref[blk, :, off:off + k] = a_ref[:, dst:dst + k].astype(b_ref.dtype)

    return pl.pallas_call(
        unarrange_w_in_kernel, grid=(rows // tm,),
        in_specs=[pl.BlockSpec((tm, ARR_W), lambda i: (i, 0))],
        out_specs=pl.BlockSpec((n, tm, cols), lambda i: (0, i, 0)),
        out_shape=jax.ShapeDtypeStruct((n, rows, cols), jnp.bfloat16),
        compiler_params=_cp(("parallel",)), name="unarrange_w_in")(a)


def _arrange_mla_weights(uq_heads, ukv_heads):
    nh = uq_heads.shape[0]
    dt = uq_heads.dtype

    def arrange_mla_weights_kernel(uq_ref, ukv_ref, q_ref, kv_ref, kvt_ref):
        q_ref[...] = jnp.zeros(q_ref.shape, dt)
        kv_ref[...] = jnp.zeros(kv_ref.shape, dt)
        kvt_ref[...] = jnp.zeros(kvt_ref.shape, dt)
        for h in range(nh):
            q_ref[:, 128 * h:128 * h + 96] = uq_ref[h]
            blk = ukv_ref[h]
            kv_ref[:, 128 * h:128 * h + 64] = blk[:, 0:64]
            kv_ref[:, 128 * nh + 64 * h:128 * nh + 64 * h + 64] = blk[:, 64:128]
            blk_t = blk.astype(F32).T.astype(dt)
            kvt_ref[128 * h:128 * h + 64, :] = blk_t[0:64]
            kvt_ref[128 * nh + 64 * h:128 * nh + 64 * h + 64, :] = blk_t[64:128]

    return pl.pallas_call(
        arrange_mla_weights_kernel, name="arrange_mla_weights",
        out_shape=[jax.ShapeDtypeStruct((256, 128 * nh), dt), jax.ShapeDtypeStruct((128, 192 * nh), dt),
                   jax.ShapeDtypeStruct((192 * nh, 128), dt)])(uq_heads, ukv_heads)


def _unarrange_mla_weights(dw_uq_a, dw_ukv_a):
    nh = dw_uq_a.shape[1] // 128

    def unarrange_mla_weights_kernel(q_ref, kv_ref, uq_ref, ukv_ref):
        for h in range(nh):
            uq_ref[h] = q_ref[:, 128 * h:128 * h + 96].astype(uq_ref.dtype)
            ukv_ref[h, :, 0:64] = kv_ref[:, 128 * h:128 * h + 64].astype(ukv_ref.dtype)
            ukv_ref[h, :, 64:128] = kv_ref[:, 128 * nh + 64 * h:128 * nh + 64 * h + 64].astype(ukv_ref.dtype)

    return pl.pallas_call(
        unarrange_mla_weights_kernel, name="unarrange_mla_weights",
        out_shape=[jax.ShapeDtypeStruct((nh, 256, 96), jnp.bfloat16),
                   jax.ShapeDtypeStruct((nh, 128, 128), jnp.bfloat16)])(dw_uq_a, dw_ukv_a)


def _rope_tables(pos3, after=None):
    B, S, _ = pos3.shape
    ts = min(S, 512)
    inv32 = (np.float32(ROPE_THETA) ** (-(np.arange(32, dtype=np.float32) / 32))).astype(np.float32)
    inv16 = (np.float32(ROPE_THETA) ** (-(np.arange(16, dtype=np.float32) / 16))).astype(np.float32)
    inv = np.zeros((1, 128), np.float32)
    inv[0, 0:32] = inv32
    inv[0, 32:48] = inv16

    def body(pos_ref, inv_ref, cr, sr, cm, sm):
        ang = pos_ref[0].astype(F32) * inv_ref[...]
        lane = lax.broadcasted_iota(jnp.int32, (1, 128), 1)

        def every_head(x):
            y = jnp.where(lane < 32, x, pltpu.roll(x, 32, 1))
            return jnp.where(lane < 64, y, pltpu.roll(y, 64, 1))

        def rotary_pair(x, fill):
            return jnp.where((lane >= 64) & (lane < 80), pltpu.roll(x, 32, 1),
                             jnp.where((lane >= 80) & (lane < 96), pltpu.roll(x, 48, 1), fill))

        c, s = jnp.cos(ang), jnp.sin(ang)
        cr[0] = every_head(c)
        sr[0] = every_head(s)
        cm[0] = rotary_pair(c, 1.0)
        sm[0] = rotary_pair(s, 0.0)

    tab = jax.ShapeDtypeStruct((B, S, 128), F32)
    blk = pl.BlockSpec((1, ts, 128), lambda b, i: (b, i, 0))
    body, extra, extra_specs = _behind(body, 2, after)
    return pl.pallas_call(
        body, name="rope_tables", grid=(B, S // ts),
        in_specs=[pl.BlockSpec((1, ts, 1), lambda b, i: (b, i, 0)), _full((1, 128))] + extra_specs,
        out_specs=[blk, blk, blk, blk], out_shape=[tab, tab, tab, tab],
        compiler_params=_cp(("parallel", "parallel")),
    )(pos3, jnp.asarray(inv), *extra)


def _rope128(x, cos, sin):
    lane = lax.broadcasted_iota(jnp.int32, (1, 128), 1)
    rp = pltpu.roll(x, 16, 1)
    rm = pltpu.roll(x, 112, 1)
    return x * cos + jnp.where(lane < 80, -rm, rp) * sin


def _rope128_t(d, cos, sin):
    lane = lax.broadcasted_iota(jnp.int32, (1, 128), 1)
    y = d * sin
    yp = pltpu.roll(y, 16, 1)
    ym = pltpu.roll(y, 112, 1)
    return d * cos + jnp.where(lane < 64, 0.0, jnp.where(lane < 80, ym, jnp.where(lane < 96, -yp, 0.0)))


def _proj_fwd(x, shift, scale, nw, w_arr):
    B, S, D = x.shape
    tm = min(S, 512)

    def body(x_ref, sh_ref, sc_ref, nw_ref, w_ref, ret_ref, mla_ref, gla_ref, h_ref):
        xv = x_ref[0]
        rstd = lax.rsqrt(jnp.mean(xv * xv, axis=-1, keepdims=True) + EPS)
        h = (xv * rstd * nw_ref[...]) * (1.0 + sc_ref[0]) + sh_ref[0]
        hb = h.astype(_MXU)
        h_ref[0] = hb
        ret_ref[0] = jnp.dot(hb, w_ref[:, 0:RET_W], preferred_element_type=F32).astype(_MXU)
        mla_ref[0] = jnp.dot(hb, w_ref[:, RET_W:RET_W + MLA_W], preferred_element_type=F32).astype(_MXU)
        gla_ref[0] = jnp.dot(hb, w_ref[:, RET_W + MLA_W:ARR_W], preferred_element_type=F32).astype(_MXU)

    tok = lambda w: pl.BlockSpec((1, tm, w), lambda b, i: (b, i, 0))
    per_seq = pl.BlockSpec((1, 1, D), lambda b, i: (b, 0, 0))
    return pl.pallas_call(
        body, name="proj_fwd", grid=(B, S // tm),
        in_specs=[tok(D), per_seq, per_seq, _full((1, D)), _full((D, ARR_W))],
        out_specs=[tok(RET_W), tok(MLA_W), tok(GLA_W), tok(D)],
        out_shape=[jax.ShapeDtypeStruct((B, S, RET_W), _MXU), jax.ShapeDtypeStruct((B, S, MLA_W), _MXU),
                   jax.ShapeDtypeStruct((B, S, GLA_W), _MXU), jax.ShapeDtypeStruct((B, S, D), _MXU)],
        compiler_params=_cp(("parallel", "parallel")),
    )(x, shift, scale, nw, w_arr)


RET_L = 256


def _ret_consts(L):
    lg = np.log1p(-np.exp2(-5.0 - np.arange(4, dtype=np.float32))).astype(np.float32)
    i = np.arange(L)
    ci = i // CHUNK
    diff = (i[:, None] - i[None, :]).astype(np.float32)
    same = ci[:, None] == ci[None, :]
    past = ci[None, :] < ci[:, None]
    expo = np.where(same, np.abs(diff), np.where(past, diff, 0.0)).astype(np.float32)
    dec = np.where((same | past)[None], np.exp(lg[:, None, None] * expo[None]), 0.0).astype(np.float32)
    head = (np.arange(256) % 128) // 32
    qw = np.exp((i + 1.0)[:, None] * lg[head][None, :]).astype(np.float32)
    kw = np.exp((L - 1.0 - i)[:, None] * lg[head][None, :]).astype(np.float32)
    a_row = np.exp(np.float32(L) * lg[head])[None, :].astype(np.float32)
    return [jnp.asarray(t) for t in (dec.reshape(4 * L, L), qw, kw, a_row)]


def _ret_masks():
    lane = lax.broadcasted_iota(jnp.int32, (1, 256), 1)
    mh = [((lane % 128) // 32) == h for h in range(4)]
    mv = [(lane // 64) == h for h in range(4)]
    vi = lax.broadcasted_iota(jnp.int32, (256, 256), 0)
    ki = lax.broadcasted_iota(jnp.int32, (256, 256), 1)
    bd = (vi // 64) == ((ki % 128) // 32)
    return mh, mv, bd


def _ret_rope(p, cs, sn):
    q1, q2, k1, k2 = p[:, 0:128], p[:, 128:256], p[:, 256:384], p[:, 384:512]
    qr = jnp.concatenate([q1 * cs - q2 * sn, q2 * cs + q1 * sn], axis=1)
    kr = jnp.concatenate([k1 * cs - k2 * sn, k2 * cs + k1 * sn], axis=1) * RET_KSCALE
    return qr, kr


def _head_mean(x, mv, width):
    out = jnp.zeros_like(x)
    for m in mv:
        s = jnp.sum(jnp.where(m, x, 0.0), axis=-1, keepdims=True) * (1.0 / width)
        out = jnp.where(m, s, out)
    return out


def _stack_heads(x, masks):
    return jnp.concatenate([jnp.where(m, x, 0.0) for m in masks], axis=0)


def _fold_heads(xs, masks, L):
    out = jnp.where(masks[0], xs[0:L], 0.0)
    for h in range(1, 4):
        out = out + jnp.where(masks[h], xs[h * L:(h + 1) * L], 0.0)
    return out


RET_G = 2


def _ret_fwd(ret_p, cos, sin):
    B, S, _ = ret_p.shape
    L = min(RET_L, S)
    NB = S // L
    G = min(RET_G, NB)
    NG = NB // G
    consts = _ret_consts(L)

    def body(p_ref, c_ref, s_ref, ds_ref, qw_ref, kw_ref, a_ref, out_ref, raw_ref, st_ref, st_sc):
        @pl.when(pl.program_id(1) == 0)
        def _():
            st_sc[...] = jnp.zeros_like(st_sc)

        mh, mv, bd = _ret_masks()
        cs_ = range(G)
        rows = [slice(c * L, (c + 1) * L) for c in cs_]
        ps = [p_ref[0, rows[c], :].astype(F32) for c in cs_]
        qk = [_ret_rope(ps[c], c_ref[0, rows[c], :], s_ref[0, rows[c], :]) for c in cs_]
        vs = [ps[c][:, 512:768] for c in cs_]
        a_s = [_mm_nt(_stack_heads(qk[c][0], mh), qk[c][1]) for c in cs_]
        upd = [_mm_tn(vs[c], qk[c][1] * kw_ref[...]) for c in cs_]
        o_s = [_mm(a_s[c] * ds_ref[...], vs[c]) for c in cs_]
        st = st_sc[...]
        inter = []
        for c in cs_:
            st_ref[0, c] = st
            inter.append(_mm_nt(qk[c][0] * qw_ref[...], st))
            st = st * a_ref[...] + jnp.where(bd, upd[c], 0.0)
        st_sc[...] = st
        for c in cs_:
            r = _fold_heads(o_s[c], mv, L) + inter[c]
            raw_ref[0, rows[c], :] = r
            rstd = lax.rsqrt(_head_mean(r * r, mv, 64.0) + EPS)
            out_ref[0, rows[c], :] = (r * rstd * _silu(ps[c][:, 768:1024])).astype(_MXU)

    tok = lambda w: pl.BlockSpec((1, G * L, w), lambda b, n: (b, n, 0))
    return pl.pallas_call(
        body, name="ret_fwd", grid=(B, NG),
        in_specs=[tok(RET_W), tok(128), tok(128), _full((4 * L, L)), _full((L, 256)), _full((L, 256)),
                  _full((1, 256))],
        out_specs=[tok(256), tok(256), pl.BlockSpec((1, G, 256, 256), lambda b, n: (b, n, 0, 0))],
        out_shape=[jax.ShapeDtypeStruct((B, S, 256), _MXU), jax.ShapeDtypeStruct((B, S, 256), F32),
                   jax.ShapeDtypeStruct((B, NB, 256, 256), F32)],
        scratch_shapes=[pltpu.VMEM((256, 256), F32)],
        compiler_params=_cp(("parallel", "arbitrary")),
    )(ret_p, cos, sin, *consts)


def _ret_bwd(ret_p, cos, sin, raw, states, d_mix):
    B, S, _ = ret_p.shape
    L = min(RET_L, S)
    NB = S // L
    G = 1
    NG = NB // G
    consts = _ret_consts(L)

    def body(p_ref, c_ref, s_ref, raw_ref, st_ref, dm_ref, ds_ref, qw_ref, kw_ref, a_ref, dp_ref, dst_sc):
        @pl.when(pl.program_id(1) == 0)
        def _():
            dst_sc[...] = jnp.zeros_like(dst_sc)

        mh, mv, bd = _ret_masks()
        qw, kw, dec = qw_ref[...], kw_ref[...], ds_ref[...]
        cs_ = range(G)
        rows = [slice(c * L, (c + 1) * L) for c in cs_]
        ps = [p_ref[0, rows[c], :].astype(F32) for c in cs_]
        tabs = [(c_ref[0, rows[c], :], s_ref[0, rows[c], :]) for c in cs_]
        qk = [_ret_rope(ps[c], *tabs[c]) for c in cs_]
        vs = [ps[c][:, 512:768] for c in cs_]
        qs = [_stack_heads(qk[c][0], mh) for c in cs_]
        a_s = [_mm_nt(qs[c], qk[c][1]) for c in cs_]
        dr, dz = [], []
        for c in cs_:
            r = raw_ref[0, rows[c], :]
            z = ps[c][:, 768:1024]
            rstd = lax.rsqrt(_head_mean(r * r, mv, 64.0) + EPS)
            rn = r * rstd
            dm = dm_ref[0, rows[c], :]
            d_rn = dm * _silu(z)
            dz.append(dm * rn * _dsilu(z))
            dr.append(rstd * (d_rn - rn * _head_mean(d_rn * rn, mv, 64.0)))
        do_s = [_stack_heads(dr[c], mv) for c in cs_]
        da_s = [_mm_nt(do_s[c], vs[c]) for c in cs_]
        sts = [st_ref[0, c] for c in cs_]
        dq_st = [_mm(dr[c], sts[c]) for c in cs_]
        dst_in = [_mm_tn(dr[c], qk[c][0] * qw) for c in cs_]
        dv = [_mm_tn(a_s[c] * dec, do_s[c]) for c in cs_]
        dqr, dkr = [], []
        for c in cs_:
            da = da_s[c] * dec
            dqr.append(_fold_heads(_mm(da, qk[c][1]), mh, L) + dq_st[c] * qw)
            dkr.append(_mm_tn(da, qs[c]))
        dst_next = dst_sc[...]
        for c in reversed(cs_):
            g = jnp.where(bd, dst_next, 0.0)
            dv[c] = dv[c] + _mm_nt(qk[c][1] * kw, g)
            dkr[c] = dkr[c] + _mm(vs[c], g) * kw
            dst_next = dst_next * a_ref[...] + jnp.where(bd, dst_in[c], 0.0)
        dst_sc[...] = dst_next
        for c in cs_:
            cs, sn = tabs[c]
            dk = dkr[c] * RET_KSCALE
            dq1, dq2 = dqr[c][:, 0:128], dqr[c][:, 128:256]
            dk1, dk2 = dk[:, 0:128], dk[:, 128:256]
            dp_ref[0, rows[c], :] = jnp.concatenate(
                [dq1 * cs + dq2 * sn, dq2 * cs - dq1 * sn, dk1 * cs + dk2 * sn, dk2 * cs - dk1 * sn, dv[c], dz[c]],
                axis=1).astype(_MXU)

    tok = lambda w: pl.BlockSpec((1, G * L, w), lambda b, i: (b, NG - 1 - i, 0))
    return pl.pallas_call(
        body, name="ret_bwd", grid=(B, NG),
        in_specs=[tok(RET_W), tok(128), tok(128), tok(256),
                  pl.BlockSpec((1, G, 256, 256), lambda b, i: (b, NG - 1 - i, 0, 0)), tok(256),
                  _full((4 * L, L)), _full((L, 256)), _full((L, 256)), _full((1, 256))],
        out_specs=tok(RET_W), out_shape=jax.ShapeDtypeStruct((B, S, RET_W), _MXU),
        scratch_shapes=[pltpu.VMEM((256, 256), F32)],
        compiler_params=_cp(("parallel", "arbitrary")),
    )(ret_p, cos, sin, raw, states, d_mix, *consts)


def _gla_masks():
    C = CHUNK
    lk = lax.broadcasted_iota(jnp.int32, (1, 128), 1)
    lv = lax.broadcasted_iota(jnp.int32, (1, 256), 1)
    mk = [(lk // 32) == h for h in range(4)]
    mv = [(lv // 64) == h for h in range(4)]
    vi = lax.broadcasted_iota(jnp.int32, (256, 128), 0)
    ki = lax.broadcasted_iota(jnp.int32, (256, 128), 1)
    bd = (vi // 64) == (ki // 32)
    ri = lax.broadcasted_iota(jnp.int32, (4 * C, C), 0) % C
    cj = lax.broadcasted_iota(jnp.int32, (4 * C, C), 1)
    lower = ri >= cj
    ti = lax.broadcasted_iota(jnp.int32, (C, C), 0)
    tj = lax.broadcasted_iota(jnp.int32, (C, C), 1)
    ltri = jnp.where(ti >= tj, 1.0, 0.0).astype(F32)
    utri = jnp.where(ti <= tj, 1.0, 0.0).astype(F32)
    return mk, mv, bd, lower, ltri, utri


def _log_sigmoid(x):
    return jnp.minimum(x, 0.0) - jnp.log(1.0 + jnp.exp(-jnp.abs(x)))


GLA_G = 8


def _gla_fwd(gla_p, w_g2p, b_g2, gnw):
    B, S, _ = gla_p.shape
    C = CHUNK
    NC = S // C
    G = min(GLA_G, NC)
    NG = NC // G

    def body(p_ref, w_ref, b_ref, gn_ref, out_ref, raw_ref, st_ref, st_sc):
        @pl.when(pl.program_id(1) == 0)
        def _():
            st_sc[...] = jnp.zeros_like(st_sc)

        mk, mv, bd, lower, ltri, _ = _gla_masks()
        cs = range(G)
        rows = [slice(c * C, (c + 1) * C) for c in cs]
        ps = [p_ref[0, rows[c], :].astype(F32) for c in cs]
        pre = [_mm(ps[c][:, 512:640], w_ref[...]) + b_ref[...] for c in cs]
        cum = [_mm_f32(ltri, _log_sigmoid(pre[c]) * (1.0 / GLA_TAU)) for c in cs]
        past, fut, upd, q_pos, a_row = [], [], [], [], []
        for c in cs:
            q = ps[c][:, 0:128]
            k = ps[c][:, 128:256] * GLA_KSCALE
            last = cum[c][C - 1:C, :]
            e_pos = jnp.exp(cum[c])
            e_neg = jnp.exp(-cum[c])
            q_pos.append(q * e_pos)
            a_row.append(jnp.exp(last))
            past.append(_mm_nt(_stack_heads(q_pos[c], mk), k * e_neg))
            fut.append(_mm_nt(_stack_heads(q * e_neg, mk), k * e_pos))
            upd.append(_mm_tn(ps[c][:, 256:512], k * jnp.exp(last - cum[c])))
        o_s = [_mm(jnp.where(lower, past[c], fut[c]), ps[c][:, 256:512]) for c in cs]
        st = st_sc[...]
        inter = []
        for c in cs:
            st_ref[0, c] = st
            inter.append(_mm_nt(q_pos[c], st))
            st = st * a_row[c] + jnp.where(bd, upd[c], 0.0)
        st_sc[...] = st
        for c in cs:
            g = _fold_heads(o_s[c], mv, C) + inter[c]
            raw_ref[0, rows[c], :] = g
            rstd = lax.rsqrt(_head_mean(g * g, mv, 64.0) + EPS)
            out_ref[0, rows[c], :] = (g * rstd * gn_ref[...] * _silu(ps[c][:, 640:896])).astype(_MXU)

    tok = lambda w: pl.BlockSpec((1, G * C, w), lambda b, n: (b, n, 0))
    return pl.pallas_call(
        body, name="gla_fwd", grid=(B, NG),
        in_specs=[tok(GLA_W), _full((128, 128)), _full((1, 128)), _full((1, 256))],
        out_specs=[tok(256), tok(256), pl.BlockSpec((1, G, 256, 128), lambda b, n: (b, n, 0, 0))],
        out_shape=[jax.ShapeDtypeStruct((B, S, 256), _MXU), jax.ShapeDtypeStruct((B, S, 256), F32),
                   jax.ShapeDtypeStruct((B, NC, 256, 128), F32)],
        scratch_shapes=[pltpu.VMEM((256, 128), F32)],
        compiler_params=_cp(("parallel", "arbitrary")),
    )(gla_p, w_g2p, b_g2, gnw)


def _gla_bwd(gla_p, w_g2p, b_g2, gnw, raw, states, d_mix, after=None):
    B, S, _ = gla_p.shape
    C = CHUNK
    NC = S // C
    G = min(GLA_G, NC)
    NG = NC // G

    def body(p_ref, w_ref, b_ref, gn_ref, raw_ref, st_ref, dm_ref, dp_ref, dw_ref, db_ref, dgn_ref, dst_sc):
        first = (pl.program_id(0) == 0) & (pl.program_id(1) == 0)

        @pl.when(first)
        def _():
            dw_ref[...] = jnp.zeros_like(dw_ref)
            db_ref[...] = jnp.zeros_like(db_ref)
            dgn_ref[...] = jnp.zeros_like(dgn_ref)

        @pl.when(pl.program_id(1) == 0)
        def _():
            dst_sc[...] = jnp.zeros_like(dst_sc)

        mk, mv, bd, lower, ltri, utri = _gla_masks()
        gn = gn_ref[...]
        cs = range(G)
        rows = [slice(c * C, (c + 1) * C) for c in cs]
        ps = [p_ref[0, rows[c], :].astype(F32) for c in cs]
        vs = [ps[c][:, 256:512] for c in cs]
        pre = [_mm(ps[c][:, 512:640], w_ref[...]) + b_ref[...] for c in cs]
        cum = [_mm_f32(ltri, _log_sigmoid(pre[c]) * (1.0 / GLA_TAU)) for c in cs]
        dg, dz, dgn_acc = [], [], jnp.zeros((1, 256), F32)
        for c in cs:
            g = raw_ref[0, rows[c], :]
            z = ps[c][:, 640:896]
            rstd = lax.rsqrt(_head_mean(g * g, mv, 64.0) + EPS)
            gh = g * rstd
            dm = dm_ref[0, rows[c], :]
            d_gn = dm * _silu(z)
            dz.append(dm * gh * gn * _dsilu(z))
            dgn_acc = dgn_acc + jnp.sum(d_gn * gh, axis=0, keepdims=True)
            d_gh = d_gn * gn
            dg.append(rstd * (d_gh - gh * _head_mean(d_gh * gh, mv, 64.0)))
        do_s = [_stack_heads(dg[c], mv) for c in cs]
        dattn = [_mm_nt(do_s[c], vs[c]) for c in cs]
        ks, e_pos, e_neg, q_pos, q_neg, k_pos, k_neg, qp_s, qn_s, past, fut, a_row, w_dec, kd = ([] for _ in range(14))
        for c in cs:
            q = ps[c][:, 0:128]
            k = ps[c][:, 128:256] * GLA_KSCALE
            last = cum[c][C - 1:C, :]
            ep, en = jnp.exp(cum[c]), jnp.exp(-cum[c])
            ks.append(k), e_pos.append(ep), e_neg.append(en)
            q_pos.append(q * ep), q_neg.append(q * en), k_pos.append(k * ep), k_neg.append(k * en)
            qp_s.append(_stack_heads(q_pos[c], mk)), qn_s.append(_stack_heads(q_neg[c], mk))
            past.append(_mm_nt(qp_s[c], k_neg[c]))
            fut.append(_mm_nt(qn_s[c], k_pos[c]))
            a_row.append(jnp.exp(last))
            w_dec.append(jnp.exp(last - cum[c]))
            kd.append(k * w_dec[c])
        sts = [st_ref[0, c] for c in cs]
        dq_st = [_mm(dg[c], sts[c]) for c in cs]
        dst_in = [_mm_tn(dg[c], q_pos[c]) for c in cs]
        dv, dq_pos, dk_neg, dq_neg, dk_pos = [], [], [], [], []
        for c in cs:
            attn = jnp.where(lower, past[c], fut[c])
            dpast = jnp.where(lower, dattn[c], 0.0)
            dfut = jnp.where(lower, 0.0, dattn[c])
            dv.append(_mm_tn(attn, do_s[c]))
            dq_pos.append(_fold_heads(_mm(dpast, k_neg[c]), mk, C) + dq_st[c])
            dk_neg.append(_mm_tn(dpast, qp_s[c]))
            dq_neg.append(_fold_heads(_mm(dfut, k_pos[c]), mk, C))
            dk_pos.append(_mm_tn(dfut, qn_s[c]))
        dst_next = dst_sc[...]
        d_a, d_kd = [None] * G, [None] * G
        for c in reversed(cs):
            d_a[c] = jnp.sum(dst_next * sts[c], axis=0, keepdims=True)
            gmat = jnp.where(bd, dst_next, 0.0)
            d_kd[c] = _mm(vs[c], gmat)
            dv[c] = dv[c] + _mm_nt(kd[c], gmat)
            dst_next = dst_next * a_row[c] + jnp.where(bd, dst_in[c], 0.0)
        dst_sc[...] = dst_next
        row = lax.broadcasted_iota(jnp.int32, (C, 128), 0)
        d_la, dk, dq = [], [], []
        for c in cs:
            t = d_kd[c] * kd[c]
            dk.append(d_kd[c] * w_dec[c] + dk_neg[c] * e_neg[c] + dk_pos[c] * e_pos[c])
            dq.append(dq_pos[c] * e_pos[c] + dq_neg[c] * e_neg[c])
            d_last = jnp.sum(t, axis=0, keepdims=True) + d_a[c] * a_row[c]
            d_cum = (dq_pos[c] * q_pos[c] - dk_neg[c] * k_neg[c] - dq_neg[c] * q_neg[c] + dk_pos[c] * k_pos[c] - t)
            d_la.append(_mm_f32(utri, d_cum + jnp.where(row == C - 1, d_last, 0.0)))
        d_pre = [d_la[c] * _sig(-pre[c]) * (1.0 / GLA_TAU) for c in cs]
        d_gg = [_mm_nt(d_pre[c], w_ref[...]) for c in cs]
        dw_acc = _mm_tn(ps[0][:, 512:640], d_pre[0])
        db_acc = jnp.sum(d_pre[0], axis=0, keepdims=True)
        for c in cs[1:]:
            dw_acc = dw_acc + _mm_tn(ps[c][:, 512:640], d_pre[c])
            db_acc = db_acc + jnp.sum(d_pre[c], axis=0, keepdims=True)
        for c in cs:
            dp_ref[0, rows[c], :] = jnp.concatenate([dq[c], dk[c] * GLA_KSCALE, dv[c], d_gg[c], dz[c]],
                                                    axis=1).astype(_MXU)
        dw_ref[...] += dw_acc
        db_ref[...] += db_acc
        dgn_ref[...] += dgn_acc

        @pl.when((pl.program_id(0) == B - 1) & (pl.program_id(1) == NG - 1))
        def _():
            s1 = dgn_ref[...]
            s1 = s1 + pltpu.roll(s1, 128, 1)
            dgn_ref[...] = s1 + pltpu.roll(s1, 64, 1)

    tok = lambda w: pl.BlockSpec((1, G * C, w), lambda b, i: (b, NG - 1 - i, 0))
    body, extra, extra_specs = _behind(body, 7, after)
    return pl.pallas_call(
        body, name="gla_bwd", grid=(B, NG),
        in_specs=[tok(GLA_W), _full((128, 128)), _full((1, 128)), _full((1, 256)), tok(256),
                  pl.BlockSpec((1, G, 256, 128), lambda b, i: (b, NG - 1 - i, 0, 0)), tok(256)] + extra_specs,
        out_specs=[tok(GLA_W), _full((128, 128)), _full((1, 128)), _full((1, 256))],
        out_shape=[jax.ShapeDtypeStruct((B, S, GLA_W), _MXU), jax.ShapeDtypeStruct((128, 128), F32),
                   jax.ShapeDtypeStruct((1, 128), F32), jax.ShapeDtypeStruct((1, 256), F32)],
        scratch_shapes=[pltpu.VMEM((256, 128), F32)],
        compiler_params=_cp(("arbitrary", "arbitrary")),
    )(gla_p, w_g2p, b_g2, gnw, raw, states, d_mix, *extra)


def _rms(x, w):
    rstd = lax.rsqrt(jnp.mean(x * x, axis=-1, keepdims=True) + EPS)
    xh = x * rstd
    return xh, rstd, xh * w


def _rms_bwd(dy, xh, rstd, w):
    dxh = dy * w
    return rstd * (dxh - xh * jnp.mean(dxh * xh, axis=-1, keepdims=True))


MLA_T = 256


def _mla_prep_fwd(mla_p, cos, sin, qnw, kvnw, w_uq, w_ukv, w_ukv_t):
    B, S, _ = mla_p.shape
    tm = min(S, 512)

    t = min(MLA_T, S)
    nt = tm // t

    def body(p_ref, c_ref, s_ref, qn_ref, kn_ref, wq_ref, wkv_ref, wkvt_ref, q_ref, k_ref, v_ref, kt_ref, vt_ref):
        p = p_ref[0].astype(F32)
        cs, sn = c_ref[0], s_ref[0]
        _, _, qn = _rms(p[:, 0:256], qn_ref[...])
        qpre = _mm(qn, wq_ref[...])
        _, _, kvn = _rms(p[:, 256:384], kn_ref[...])
        kv = _mm(kvn, wkv_ref[...])
        kvt = _mm_nt(wkvt_ref[...], kvn)
        kpe = _rope128(p[:, 384:512], cs, sn)
        kpet = kpe.T
        for h in range(8):
            sl = slice(128 * h, 128 * h + 128)
            q_ref[0, :, sl] = _rope128(qpre[:, sl], cs, sn).astype(_MXU)
            k_ref[0, :, sl] = (kv[:, sl] + kpe).astype(_MXU)
            kht = kvt[sl, :] + kpet
            for n in range(nt):
                kt_ref[0, n, sl, :] = kht[:, n * t:(n + 1) * t].astype(_MXU)
        v_ref[0] = kv[:, 1024:1536].astype(_MXU)
        for n in range(nt):
            vt_ref[0, n] = kvt[1024:1536, n * t:(n + 1) * t].astype(_MXU)

    tok = lambda w: pl.BlockSpec((1, tm, w), lambda b, i: (b, i, 0))
    tr = lambda w: pl.BlockSpec((1, nt, w, t), lambda b, i: (b, i, 0, 0))
    return pl.pallas_call(
        body, name="mla_prep_fwd", grid=(B, S // tm),
        in_specs=[tok(512), tok(128), tok(128), _full((1, 256)), _full((1, 128)), _full((256, 1024)),
                  _full((128, 1536)), _full((1536, 128))],
        out_specs=[tok(1024), tok(1024), tok(512), tr(1024), tr(512)],
        out_shape=[jax.ShapeDtypeStruct((B, S, 1024), _MXU), jax.ShapeDtypeStruct((B, S, 1024), _MXU),
                   jax.ShapeDtypeStruct((B, S, 512), _MXU), jax.ShapeDtypeStruct((B, S // t, 1024, t), _MXU),
                   jax.ShapeDtypeStruct((B, S // t, 512, t), _MXU)],
        compiler_params=_cp(("parallel", "parallel")),
    )(mla_p, cos, sin, qnw, kvnw, w_uq, w_ukv, w_ukv_t)


def _chunk_mask_t(t):
    kj = lax.broadcasted_iota(jnp.int32, (t, t), 0) // CHUNK
    qi = lax.broadcasted_iota(jnp.int32, (t, t), 1) // CHUNK
    return kj <= qi


MLA_HG = 8
MLA_HG_FWD = 8
LOG2E = 1.4426950408889634
MLA_C2 = MLA_SCALE * LOG2E


def _mla_attn_fwd(q, k, vt):
    B, S, _ = q.shape
    t = min(MLA_T, S)
    nq = S // t
    HG = MLA_HG_FWD
    NP = HG // 2

    def body(q_ref, k_ref, vt_ref, o_ref, lse_ref, sa, sb, m_sc, l_sc, acc_sc):
        i = pl.program_id(2)
        row = lax.broadcasted_iota(jnp.int32, (128, 1), 0)
        low = row < 64
        mask = _chunk_mask_t(t)
        m_sc[...] = jnp.full(m_sc.shape, -jnp.inf, F32)
        l_sc[...] = jnp.zeros_like(l_sc)
        acc_sc[...] = jnp.zeros_like(acc_sc)

        ones = jnp.ones((8, t), _MXU)

        def scores(j, buf):
            kb = k_ref[0, pl.ds(pl.multiple_of(j * t, t), t), :]
            for h in range(HG):
                cols = slice(128 * h, 128 * h + 128)
                buf[h] = (_mm_nt(kb[:, cols], q_ref[0, :, cols]) * MLA_C2).astype(_MXU)

        def absorb(j, buf, masked):
            vtb = vt_ref[0, j]
            for pr in range(NP):
                alphas, pvs = [], []
                for hh in range(2):
                    h = 2 * pr + hh
                    s = buf[h]
                    if masked:
                        s = jnp.where(mask, s, jnp.full_like(s, -jnp.inf))
                    m_old = m_sc[h]
                    m_new = jnp.maximum(m_old, jnp.max(s, axis=0, keepdims=True).astype(F32))
                    alpha = jnp.exp2(m_old - m_new)
                    p = jnp.exp2(s - m_new.astype(_MXU))
                    l_sc[h] = alpha * l_sc[h] + _mm(ones, p)[0:1, :]
                    m_sc[h] = m_new
                    vth = vtb[128 * pr:128 * pr + 128, :]
                    vth = jnp.where(low if hh == 0 else ~low, vth, jnp.zeros_like(vth))
                    pvs.append(_mm(vth, p))
                    alphas.append(alpha)
                acc_sc[pr] = acc_sc[pr] * jnp.where(low, alphas[0], alphas[1]) + pvs[0] + pvs[1]

        scores(0, sb)

        def pair(jj, carry):
            j0 = 2 * jj
            scores(j0 + 1, sa)
            absorb(j0, sb, False)
            scores(j0 + 2, sb)
            absorb(j0 + 1, sa, False)
            return carry

        lax.fori_loop(0, i // 2, pair, 0)

        @pl.when(i % 2 == 1)
        def _():
            scores(i, sa)
            absorb(i - 1, sb, False)
            absorb(i, sa, True)

        @pl.when(i % 2 == 0)
        def _():
            absorb(i, sb, True)

        for pr in range(NP):
            l_e, l_o = l_sc[2 * pr], l_sc[2 * pr + 1]
            o_ref[0, :, 128 * pr:128 * pr + 128] = (acc_sc[pr] / jnp.where(low, l_e, l_o)).T
            lse_ref[0, pr, 0, 0:1, :] = m_sc[2 * pr] + jnp.log(l_e) * LOG2E
            lse_ref[0, pr, 0, 1:2, :] = m_sc[2 * pr + 1] + jnp.log(l_o) * LOG2E

    return pl.pallas_call(
        body, name="mla_attn_fwd", grid=(B, 8 // HG, nq),
        in_specs=[pl.BlockSpec((1, t, 128 * HG), lambda b, g, i: (b, i, g)),
                  pl.BlockSpec((1, S, 128 * HG), lambda b, g, i: (b, 0, g)),
                  pl.BlockSpec((1, nq, 64 * HG, t), lambda b, g, i: (b, 0, g, 0))],
        out_specs=[pl.BlockSpec((1, t, 64 * HG), lambda b, g, i: (b, i, g)),
                   pl.BlockSpec((1, NP, 1, 2, t), lambda b, g, i: (b, g, i, 0, 0))],
        out_shape=[jax.ShapeDtypeStruct((B, S, 512), F32), jax.ShapeDtypeStruct((B, 4, nq, 2, t), F32)],
        scratch_shapes=[pltpu.VMEM((HG, t, t), _MXU), pltpu.VMEM((HG, t, t), _MXU), pltpu.VMEM((HG, 1, t), F32),
                        pltpu.VMEM((HG, 1, t), F32), pltpu.VMEM((NP, 128, t), F32)],
        compiler_params=_cp(("parallel", "parallel", "arbitrary")),
    )(q, k, vt)


def _mla_attn_bwd(q, k, v, kt, do, lse, dl):
    B, S, _ = q.shape
    t = min(MLA_T, S)
    nk = S // t

    HG = MLA_HG
    NP = HG // 2

    def body(q_ref, k_ref, v_ref, kt_ref, do_ref, lse_ref, dl_ref, dq_ref, dk_ref, dv_ref,
             sa, da, sb, db, dqt_sc, dk_sc, dv_sc):
        j = pl.program_id(2)

        @pl.when(j == 0)
        def _():
            dqt_sc[...] = jnp.zeros_like(dqt_sc)

        dk_sc[...] = jnp.zeros_like(dk_sc)
        dv_sc[...] = jnp.zeros_like(dv_sc)
        lane = lax.broadcasted_iota(jnp.int32, (1, 128), 1)
        low = lane < 64
        mask = _chunk_mask_t(t)

        def half(x, hh):
            return jnp.where(low if hh == 0 else ~low, x, jnp.zeros_like(x))

        def prepare(i, sbuf, dbuf):
            rows = pl.ds(pl.multiple_of(i * t, t), t)
            for h in range(HG):
                cols = slice(128 * h, 128 * h + 128)
                pc = slice(128 * (h // 2), 128 * (h // 2) + 128)
                sbuf[h] = _mm_nt(k_ref[0, :, cols], q_ref[0, rows, cols]) * MLA_C2
                dbuf[h] = _mm_nt(half(v_ref[0, :, pc], h % 2), do_ref[0, rows, pc])

        def absorb(i, sbuf, dbuf, masked):
            rows = pl.ds(pl.multiple_of(i * t, t), t)
            for h in range(HG):
                pr, hh = h // 2, h % 2
                cols = slice(128 * h, 128 * h + 128)
                pc = slice(128 * pr, 128 * pr + 128)
                p = jnp.exp2(sbuf[h] - lse_ref[0, pr, i][hh:hh + 1, :])
                if masked:
                    p = jnp.where(mask, p, 0.0)
                dv_sc[pr] += _mm(p, half(do_ref[0, rows, pc], hh))
                ds = p * (dbuf[h] - dl_ref[0, pr, i][hh:hh + 1, :])
                dqt_sc[i, cols, :] += _mm(kt_ref[0, 0, cols, :], ds)
                dk_sc[h] += _mm(ds, q_ref[0, rows, cols])

        n = nk - 1 - j
        prepare(jnp.minimum(j + 1, nk - 1), sb, db)

        def pair(jj, carry):
            i0 = j + 1 + 2 * jj
            prepare(i0 + 1, sa, da)
            absorb(i0, sb, db, False)
            prepare(jnp.where(i0 + 2 <= nk - 1, i0 + 2, j), sb, db)
            absorb(i0 + 1, sa, da, False)
            return carry

        lax.fori_loop(0, n // 2, pair, 0)

        @pl.when(n % 2 == 1)
        def _():
            prepare(j, sa, da)
            absorb(nk - 1, sb, db, False)
            absorb(j, sa, da, True)

        @pl.when(n % 2 == 0)
        def _():
            absorb(j, sb, db, True)

        for h in range(HG):
            dk_ref[0, :, 128 * h:128 * h + 128] = (dk_sc[h] * MLA_SCALE).astype(_MXU)
        for pr in range(NP):
            dv_ref[0, :, 128 * pr:128 * pr + 128] = dv_sc[pr].astype(_MXU)

        @pl.when(j == nk - 1)
        def _():
            for i in range(nk):
                dq_ref[0, i * t:(i + 1) * t, :] = (dqt_sc[i].T * MLA_SCALE).astype(_MXU)

    seq = lambda w: pl.BlockSpec((1, S, w), lambda b, g, j: (b, 0, g))
    blk = lambda w: pl.BlockSpec((1, t, w), lambda b, g, j: (b, j, g))
    stat = pl.BlockSpec((1, NP, nk, 2, t), lambda b, g, j: (b, g, 0, 0, 0))
    return pl.pallas_call(
        body, name="mla_attn_bwd", grid=(B, 8 // HG, nk),
        in_specs=[seq(128 * HG), blk(128 * HG), blk(64 * HG),
                  pl.BlockSpec((1, 1, 128 * HG, t), lambda b, g, j: (b, j, g, 0)), seq(64 * HG), stat, stat],
        out_specs=[seq(128 * HG), blk(128 * HG), blk(64 * HG)],
        out_shape=[jax.ShapeDtypeStruct((B, S, 1024), _MXU), jax.ShapeDtypeStruct((B, S, 1024), _MXU),
                   jax.ShapeDtypeStruct((B, S, 512), _MXU)],
        scratch_shapes=[pltpu.VMEM((HG, t, t), F32), pltpu.VMEM((HG, t, t), F32), pltpu.VMEM((HG, t, t), F32),
                        pltpu.VMEM((HG, t, t), F32), pltpu.VMEM((nk, 128 * HG, t), F32),
                        pltpu.VMEM((HG, t, 128), F32), pltpu.VMEM((NP, t, 128), F32)],
        compiler_params=_cp(("parallel", "parallel", "arbitrary"), 56),
    )(q, k, v, kt, do, lse, dl)


def _mla_prep_bwd(mla_p, cos, sin, qnw, kvnw, w_uq, w_ukv, dq, dk, dv):
    B, S, _ = mla_p.shape
    tm = min(S, 512)

    def body(p_ref, c_ref, s_ref, qn_ref, kn_ref, wq_ref, wkv_ref, dq_ref, dk_ref, dv_ref,
             dp_ref, dwq_ref, dwkv_ref, dqn_ref, dkn_ref):
        first = (pl.program_id(0) == 0) & (pl.program_id(1) == 0)

        @pl.when(first)
        def _():
            dwq_ref[...] = jnp.zeros_like(dwq_ref)
            dwkv_ref[...] = jnp.zeros_like(dwkv_ref)
            dqn_ref[...] = jnp.zeros_like(dqn_ref)
            dkn_ref[...] = jnp.zeros_like(dkn_ref)

        p = p_ref[0].astype(F32)
        cs, sn = c_ref[0], s_ref[0]
        lane = lax.broadcasted_iota(jnp.int32, (1, 128), 1)
        pe = (lane >= 64) & (lane < 96)
        qh, q_rstd, qn = _rms(p[:, 0:256], qn_ref[...])
        kvh, kv_rstd, kvn = _rms(p[:, 256:384], kn_ref[...])
        dqv = dq_ref[0].astype(F32)
        dkv = dk_ref[0].astype(F32)
        dqpre = jnp.concatenate(
            [_rope128_t(dqv[:, 128 * h:128 * h + 128], cs, sn) for h in range(8)], axis=1)
        dkpe = jnp.zeros((tm, 128), F32)
        for h in range(8):
            dkpe = dkpe + jnp.where(pe, dkv[:, 128 * h:128 * h + 128], 0.0)
        dkr = _rope128_t(dkpe, cs, sn)
        dkv_all = jnp.concatenate([dkv, dv_ref[0].astype(F32)], axis=1)
        d_qn = _mm_nt(dqpre, wq_ref[...])
        d_kvn = _mm_nt(dkv_all, wkv_ref[...])
        dwq_ref[...] += _mm_tn(qn, dqpre)
        dwkv_ref[...] += _mm_tn(kvn, dkv_all)
        dqn_ref[...] += jnp.sum(d_qn * qh, axis=0, keepdims=True)
        dkn_ref[...] += jnp.sum(d_kvn * kvh, axis=0, keepdims=True)
        dp_ref[0] = jnp.concatenate([_rms_bwd(d_qn, qh, q_rstd, qn_ref[...]),
                                     _rms_bwd(d_kvn, kvh, kv_rstd, kn_ref[...]), dkr], axis=1).astype(_MXU)

    tok = lambda w: pl.BlockSpec((1, tm, w), lambda b, i: (b, i, 0))
    return pl.pallas_call(
        body, name="mla_prep_bwd", grid=(B, S // tm),
        in_specs=[tok(512), tok(128), tok(128), _full((1, 256)), _full((1, 128)), _full((256, 1024)),
                  _full((128, 1536)), tok(1024), tok(1024), tok(512)],
        out_specs=[tok(512), _full((256, 1024)), _full((128, 1536)), _full((1, 256)), _full((1, 128))],
        out_shape=[jax.ShapeDtypeStruct((B, S, 512), _MXU), jax.ShapeDtypeStruct((256, 1024), F32),
                   jax.ShapeDtypeStruct((128, 1536), F32), jax.ShapeDtypeStruct((1, 256), F32),
                   jax.ShapeDtypeStruct((1, 128), F32)],
        compiler_params=_cp(("arbitrary", "arbitrary")),
    )(mla_p, cos, sin, qnw, kvnw, w_uq, w_ukv, dq, dk, dv)


def _out_fwd(x, gate, r_g, o_mla, mla_p, g_g, w_out):
    B, S, D = x.shape
    tm = min(S, 512)

    def body(x_ref, g_ref, r_ref, o_ref, z_ref, gg_ref, w_ref, xn_ref, y_ref):
        mm = (o_ref[0] * _silu(z_ref[0].astype(F32))).astype(_MXU)
        y = (jnp.dot(r_ref[0], w_ref[0:256, :], preferred_element_type=F32)
             + jnp.dot(mm, w_ref[256:768, :], preferred_element_type=F32)
             + jnp.dot(gg_ref[0], w_ref[768:1024, :], preferred_element_type=F32))
        y_ref[0] = y.astype(_MXU)
        xn_ref[0] = x_ref[0] + g_ref[0] * y

    tok = lambda w, c=0: pl.BlockSpec((1, tm, w), lambda b, i: (b, i, c))
    return pl.pallas_call(
        body, name="out_fwd", grid=(B, S // tm),
        in_specs=[tok(D), pl.BlockSpec((1, 1, D), lambda b, i: (b, 0, 0)), tok(256), tok(512), tok(512, 1),
                  tok(256), _full((D, D))],
        out_specs=[tok(D), tok(D)],
        out_shape=[jax.ShapeDtypeStruct((B, S, D), F32), jax.ShapeDtypeStruct((B, S, D), _MXU)],
        compiler_params=_cp(("parallel", "parallel")),
    )(x, gate, r_g, o_mla, mla_p, g_g, w_out)


def _out_bwd(dx, y, gate, r_g, g_g, w_out, o_mla, mla_p, after=None):
    B, S, D = dx.shape
    tm = min(S, 512)
    t = min(MLA_T, S)
    nt = tm // t

    def body(dx_ref, y_ref, g_ref, r_ref, gg_ref, w_ref, o_ref, z_ref,
             dr_ref, do_ref, dz_ref, dl_ref, dg_ref, dw_ref, dgate_ref, acc):
        first = (pl.program_id(0) == 0) & (pl.program_id(1) == 0)

        @pl.when(first)
        def _():
            acc[...] = jnp.zeros_like(acc)

        @pl.when(pl.program_id(1) == 0)
        def _():
            dgate_ref[...] = jnp.zeros_like(dgate_ref)

        dxv = dx_ref[0]
        dgate_ref[0] += jnp.sum(dxv * y_ref[0].astype(F32), axis=0, keepdims=True)
        dy = (dxv * g_ref[0]).astype(_MXU)
        dr_ref[0] = _mm_nt(dy, w_ref[0:256, :])
        dg_ref[0] = _mm_nt(dy, w_ref[768:1024, :])
        ov, z = o_ref[0], z_ref[0].astype(F32)
        acc[0:256, :] += _mm_tn(r_ref[0], dy)
        acc[256:768, :] += _mm_tn((ov * _silu(z)).astype(_MXU), dy)
        acc[768:1024, :] += _mm_tn(gg_ref[0], dy)

        @pl.when((pl.program_id(0) == B - 1) & (pl.program_id(1) == S // tm - 1))
        def _():
            dw_ref[...] = acc[...].astype(_MXU)

        dm = _mm_nt(dy, w_ref[256:768, :])
        do = dm * _silu(z)
        dz_ref[0] = (dm * ov * _dsilu(z)).astype(_MXU)
        do_ref[0] = do.astype(_MXU)
        prod = do * ov
        for pr in range(4):
            pt = prod[:, 128 * pr:128 * pr + 128].T
            se = jnp.sum(pt[0:64], axis=0, keepdims=True)
            so = jnp.sum(pt[64:128], axis=0, keepdims=True)
            for n in range(nt):
                dl_ref[0, pr, n, 0:1, :] = se[:, n * t:(n + 1) * t]
                dl_ref[0, pr, n, 1:2, :] = so[:, n * t:(n + 1) * t]

    tok = lambda w, c=0: pl.BlockSpec((1, tm, w), lambda b, i: (b, i, c))
    per_seq = pl.BlockSpec((1, 1, D), lambda b, i: (b, 0, 0))
    body, extra, extra_specs = _behind(body, 8, after)
    return pl.pallas_call(
        body, name="out_bwd", grid=(B, S // tm),
        in_specs=[tok(D), tok(D), per_seq, tok(256), tok(256), _full((D, D)), tok(512), tok(512, 1)] + extra_specs,
        out_specs=[tok(256), tok(512), tok(512), pl.BlockSpec((1, 4, nt, 2, t), lambda b, i: (b, 0, i, 0, 0)),
                   tok(256), _full((D, D)), per_seq],
        out_shape=[jax.ShapeDtypeStruct((B, S, 256), F32), jax.ShapeDtypeStruct((B, S, 512), _MXU),
                   jax.ShapeDtypeStruct((B, S, 512), _MXU), jax.ShapeDtypeStruct((B, 4, S // t, 2, t), F32),
                   jax.ShapeDtypeStruct((B, S, 256), F32), jax.ShapeDtypeStruct((D, D), _MXU),
                   jax.ShapeDtypeStruct((B, 1, D), F32)],
        scratch_shapes=[pltpu.VMEM((D, D), F32)],
        compiler_params=_cp(("arbitrary", "arbitrary")),
    )(dx, y, gate, r_g, g_g, w_out, o_mla, mla_p, *extra)


def _proj_bwd_x(x, shift, scale, nw, w_arr, d_ret, d_mla, d_mz, d_gla, dx_out, after=None):
    B, S, D = x.shape
    tm = min(S, 512)

    def body(x_ref, sc_ref, nw_ref, w_ref, dr_ref, dm_ref, dz_ref, dg_ref, dxo_ref,
             dx_ref, dsh_ref, dsc_ref, dnw_ref):
        first = (pl.program_id(0) == 0) & (pl.program_id(1) == 0)

        @pl.when(first)
        def _():
            dnw_ref[...] = jnp.zeros_like(dnw_ref)

        @pl.when(pl.program_id(1) == 0)
        def _():
            dsh_ref[...] = jnp.zeros_like(dsh_ref)
            dsc_ref[...] = jnp.zeros_like(dsc_ref)

        dp = jnp.concatenate([dr_ref[0], dm_ref[0], dz_ref[0], dg_ref[0]], axis=1)
        dh = lax.dot_general(dp, w_ref[...], (((1,), (1,)), ((), ())), preferred_element_type=F32)
        xv = x_ref[0]
        rstd = lax.rsqrt(jnp.mean(xv * xv, axis=-1, keepdims=True) + EPS)
        xh = xv * rstd
        nwv = nw_ref[...]
        mod = 1.0 + sc_ref[0]
        dsh_ref[0] += jnp.sum(dh, axis=0, keepdims=True)
        dsc_ref[0] += jnp.sum(dh * xh * nwv, axis=0, keepdims=True)
        dnw_ref[...] += jnp.sum(dh * xh * mod, axis=0, keepdims=True)
        dxh = dh * nwv * mod
        dx_ref[0] = dxo_ref[0] + rstd * (dxh - xh * jnp.mean(dxh * xh, axis=-1, keepdims=True))

    tok = lambda w: pl.BlockSpec((1, tm, w), lambda b, i: (b, i, 0))
    per_seq = pl.BlockSpec((1, 1, D), lambda b, i: (b, 0, 0))
    body, extra, extra_specs = _behind(body, 9, after)
    return pl.pallas_call(
        body, name="proj_bwd_x", grid=(B, S // tm),
        in_specs=[tok(D), per_seq, _full((1, D)), _full((D, ARR_W)), tok(RET_W), tok(512), tok(512),
                  tok(GLA_W), tok(D)] + extra_specs,
        out_specs=[tok(D), per_seq, per_seq, _full((1, D))],
        out_shape=[jax.ShapeDtypeStruct((B, S, D), F32), jax.ShapeDtypeStruct((B, 1, D), F32),
                   jax.ShapeDtypeStruct((B, 1, D), F32), jax.ShapeDtypeStruct((1, D), F32)],
        compiler_params=_cp(("arbitrary", "arbitrary")),
    )(x, scale, nw, w_arr, d_ret, d_mla, d_mz, d_gla, dx_out, *extra)


def _proj_bwd_w(h, d_ret, d_mla, d_mz, d_gla):
    B, S, D = h.shape
    tm = min(S, 512)

    def body(h_ref, dr_ref, dm_ref, dz_ref, dg_ref, dw_ref, acc):
        first = (pl.program_id(0) == 0) & (pl.program_id(1) == 0)

        @pl.when(first)
        def _():
            acc[...] = jnp.zeros_like(acc)

        hv = h_ref[0]
        tn = lambda d_ref: lax.dot_general(hv, d_ref[0], (((0,), (0,)), ((), ())), preferred_element_type=F32)
        acc[:, 0:RET_W] += tn(dr_ref)
        acc[:, RET_W:RET_W + 512] += tn(dm_ref)
        acc[:, RET_W + 512:RET_W + MLA_W] += tn(dz_ref)
        acc[:, RET_W + MLA_W:ARR_W] += tn(dg_ref)

        @pl.when((pl.program_id(0) == B - 1) & (pl.program_id(1) == S // tm - 1))
        def _():
            dw_ref[...] = acc[...].astype(_MXU)

    tok = lambda w: pl.BlockSpec((1, tm, w), lambda b, i: (b, i, 0))
    return pl.pallas_call(
        body, name="proj_bwd_w", grid=(B, S // tm),
        in_specs=[tok(D), tok(RET_W), tok(512), tok(512), tok(GLA_W)],
        out_specs=_full((D, ARR_W)), out_shape=jax.ShapeDtypeStruct((D, ARR_W), _MXU),
        scratch_shapes=[pltpu.VMEM((D, ARR_W), F32)],
        compiler_params=_cp(("arbitrary", "arbitrary"), 56),
    )(h, d_ret, d_mla, d_mz, d_gla)


def _out_fwd_loss(x, gate, r_g, o_mla, mla_p, g_g, w_out, fw, target):
    B, S, D = x.shape
    tm = min(S, 512)

    def body(x_ref, g_ref, r_ref, o_ref, z_ref, gg_ref, w_ref, fw_ref, t_ref, dx_ref, y_ref, loss_ref, dfw_ref):
        first = (pl.program_id(0) == 0) & (pl.program_id(1) == 0)

        @pl.when(first)
        def _():
            loss_ref[...] = jnp.zeros_like(loss_ref)
            dfw_ref[...] = jnp.zeros_like(dfw_ref)

        mm = (o_ref[0] * _silu(z_ref[0].astype(F32))).astype(_MXU)
        y = (jnp.dot(r_ref[0], w_ref[0:256, :], preferred_element_type=F32)
             + jnp.dot(mm, w_ref[256:768, :], preferred_element_type=F32)
             + jnp.dot(gg_ref[0], w_ref[768:1024, :], preferred_element_type=F32))
        y_ref[0] = y.astype(_MXU)
        xv = x_ref[0] + g_ref[0] * y
        fwv = fw_ref[...]
        rstd = lax.rsqrt(jnp.mean(xv * xv, axis=-1, keepdims=True) + EPS)
        xh = xv * rstd
        err = xh * fwv - t_ref[0]
        loss_ref[...] += 0.5 * jnp.sum(jnp.mean(err * err, axis=-1, keepdims=True), axis=0, keepdims=True)
        dy = err * (1.0 / D)
        dfw_ref[...] += jnp.sum(dy * xh, axis=0, keepdims=True)
        dxh = dy * fwv
        dx_ref[0] = rstd * (dxh - xh * jnp.mean(dxh * xh, axis=-1, keepdims=True))

    tok = lambda w, c=0: pl.BlockSpec((1, tm, w), lambda b, i: (b, i, c))
    return pl.pallas_call(
        body, name="out_fwd_loss", grid=(B, S // tm),
        in_specs=[tok(D), pl.BlockSpec((1, 1, D), lambda b, i: (b, 0, 0)), tok(256), tok(512), tok(512, 1),
                  tok(256), _full((D, D)), _full((1, D)), tok(D)],
        out_specs=[tok(D), tok(D), _full((1, 1)), _full((1, D))],
        out_shape=[jax.ShapeDtypeStruct((B, S, D), F32), jax.ShapeDtypeStruct((B, S, D), _MXU),
                   jax.ShapeDtypeStruct((1, 1), F32), jax.ShapeDtypeStruct((1, D), F32)],
        compiler_params=_cp(("arbitrary", "arbitrary")),
    )(x, gate, r_g, o_mla, mla_p, g_g, w_out, fw, target)


def _local_step(x, pos3, mod, loss_target, small, w_in_a, w_uq_a, w_ukv_a, w_out_b):
    B, S, D = x.shape
    tabs = _rope_tables(pos3)
    saved = []
    for l in range(DEPTH):
        last = (small["final_norm"].reshape(1, D), loss_target) if l == DEPTH - 1 else None
        x, s = _layer_fwd(x, tabs, mod[l], {n: a[l] for n, a in small.items() if n != "final_norm"},
                          w_in_a[l], w_uq_a[l], w_ukv_a[l], w_ukv_a[l].T, w_out_b[l], loss_head=last)
        saved.append(s)
    dx, loss, d_fw = x
    grads = dict(final_norm=d_fw.reshape(D))
    per_layer = [None] * DEPTH
    for l in reversed(range(DEPTH)):
        dx, per_layer[l] = _layer_bwd(dx, saved[l], tabs)
    for name in per_layer[0]:
        grads[name] = jnp.stack([per_layer[l][name] for l in range(DEPTH)])
    return loss, dx, grads


def _layer_fwd(x, tabs, mod_l, small_l, w_in_a, w_uq_a=None, w_ukv_a=None, w_ukv_t=None, w_out_b=None, late_weights=None,
               loss_head=None):
    B, S, D = x.shape
    cr, sr, cm, sm = tabs
    shift = mod_l[:, 0:D].reshape(B, 1, D)
    scale = mod_l[:, D:2 * D].reshape(B, 1, D)
    gate = mod_l[:, 2 * D:3 * D].reshape(B, 1, D)
    nw = small_l["norm_w"].reshape(1, D)
    qnw = small_l["mla_q_norm"].reshape(1, 256)
    kvnw = small_l["mla_kv_norm"].reshape(1, 128)
    w_g2p = jnp.pad(small_l["gla_w_g2"], ((0, 112), (0, 0)))
    b_g2 = small_l["gla_b_g2"].reshape(1, 128)
    gnw = jnp.tile(small_l["gla_norm"], 4).reshape(1, 256)
    ret_p, mla_p, gla_p, h = _proj_fwd(x, shift, scale, nw, w_in_a)
    r_g, r_raw, r_st = _ret_fwd(ret_p, cr, sr)
    if late_weights is not None:
        w_uq_a, w_ukv_a, w_ukv_t, w_out_b = late_weights(r_raw)
    q, k, v, kt, vt = _mla_prep_fwd(mla_p, cm, sm, qnw, kvnw, w_uq_a, w_ukv_a, w_ukv_t)
    o_mla, lse = _mla_attn_fwd(q, k, vt)
    g_g, g_raw, g_st = _gla_fwd(gla_p, w_g2p, b_g2, gnw)
    if loss_head is None:
        x_new, y = _out_fwd(x, gate, r_g, o_mla, mla_p, g_g, w_out_b)
    else:
        dx, y, loss, d_fw = _out_fwd_loss(x, gate, r_g, o_mla, mla_p, g_g, w_out_b, *loss_head)
        x_new = (dx, loss, d_fw)
    saved = dict(x=x, shift=shift, scale=scale, gate=gate, nw=nw, qnw=qnw, kvnw=kvnw, w_g2p=w_g2p, b_g2=b_g2,
                 gnw=gnw, ret_p=ret_p, mla_p=mla_p, gla_p=gla_p, h=h, r_g=r_g, r_raw=r_raw, r_st=r_st, q=q, k=k,
                 v=v, kt=kt, o_mla=o_mla, lse=lse, g_g=g_g, g_raw=g_raw, g_st=g_st, y=y,
                 w_in_a=w_in_a, w_uq_a=w_uq_a, w_ukv_a=w_ukv_a, w_out_b=w_out_b)
    return x_new, saved


def _layer_bwd(dx, s, tabs, after=None, early_grads=None, early_w_in=None):
    B, S, D = dx.shape
    cr, sr, cm, sm = tabs
    d_r, do, d_mz, dl, d_g, dw_out, d_gate = _out_bwd(dx, s["y"], s["gate"], s["r_g"], s["g_g"], s["w_out_b"],
                                                      s["o_mla"], s["mla_p"], after=after)
    d_ret = _ret_bwd(s["ret_p"], cr, sr, s["r_raw"], s["r_st"], d_r)
    dq, dk, dv = _mla_attn_bwd(s["q"], s["k"], s["v"], s["kt"], do, s["lse"], dl)
    d_mla, dw_uq, dw_ukv, d_qnw, d_kvnw = _mla_prep_bwd(
        s["mla_p"], cm, sm, s["qnw"], s["kvnw"], s["w_uq_a"], s["w_ukv_a"], dq, dk, dv)
    sent = None if early_grads is None else early_grads(dw_out, dw_uq, dw_ukv)
    d_gla, dw_g2p, db_g2, d_gnw = _gla_bwd(s["gla_p"], s["w_g2p"], s["b_g2"], s["gnw"], s["g_raw"], s["g_st"], d_g,
                                           after=sent)
    dw_in = _proj_bwd_w(s["h"], d_ret, d_mla, d_mz, d_gla)
    sent = None if early_w_in is None else early_w_in(dw_in)
    dx, d_shift, d_scale, d_nw = _proj_bwd_x(s["x"], s["shift"], s["scale"], s["nw"], s["w_in_a"],
                                             d_ret, d_mla, d_mz, d_gla, dx, after=sent)
    grads = dict(
        d_mod=jnp.concatenate([d_shift, d_scale, d_gate], axis=2).reshape(B, 3 * D),
        norm_w=d_nw.reshape(D), mla_q_norm=d_qnw.reshape(256), mla_kv_norm=d_kvnw.reshape(128),
        gla_w_g2=dw_g2p[0:16], gla_b_g2=db_g2.reshape(128), gla_norm256=d_gnw.reshape(256),
        w_in_a=dw_in, w_uq_a=dw_uq, w_ukv_a=dw_ukv, w_out=dw_out)
    return dx, grads


def _peers(ix, iy, ic):
    out = []
    for d in range(1, N_DEV):
        px = 1 - ix if d & 4 else ix
        py = 1 - iy if d & 2 else iy
        pc = 1 - ic if d & 1 else ic
        out.append((d - 1, (px, py, pc), 4 * px + 2 * py + pc))
    return out


def _exchange_start(arrs, gather, name, after=None):
    n = len(arrs)
    lands = [lax.empty(((N_DEV,) + a.shape) if g else a.shape, a.dtype) for a, g in zip(arrs, gather)]
    extra = [] if after is None else [after]

    def body(*refs):
        ins, land_refs = refs[:n], refs[n:2 * n]
        send_sems, recv_sems = refs[2 * n + len(extra)], refs[2 * n + len(extra) + 1]
        token = refs[-1]
        ix, iy, ic = lax.axis_index("x"), lax.axis_index("y"), lax.axis_index("c")
        me = 4 * ix + 2 * iy + ic
        for a in range(n):
            for k, peer, peer_idx in _peers(ix, iy, ic):
                pltpu.make_async_remote_copy(
                    src_ref=ins[a] if gather[a] else ins[a].at[peer_idx], dst_ref=land_refs[a].at[me],
                    send_sem=send_sems.at[7 * a + k], recv_sem=recv_sems.at[7 * a + k], device_id=peer,
                    device_id_type=pl.DeviceIdType.MESH).start()
        token[...] = jnp.zeros_like(token)

    hbm = pl.BlockSpec(memory_space=pltpu.HBM)
    sem = pl.BlockSpec(memory_space=pltpu.SEMAPHORE)
    held = [pltpu.with_memory_space_constraint(a, pltpu.HBM) for a in list(arrs) + lands]
    outs = pl.pallas_call(
        body, name=name,
        out_shape=(pltpu.SemaphoreType.DMA((7 * n,)), pltpu.SemaphoreType.DMA((7 * n,)),
                   *[pltpu.HBM(a.shape, a.dtype) for a in held], jax.ShapeDtypeStruct((8, 128), F32)),
        in_specs=[hbm] * (2 * n) + [pl.BlockSpec(memory_space=pl.ANY)] * len(extra),
        out_specs=(sem, sem, *[hbm] * (2 * n), pl.BlockSpec(memory_space=pltpu.VMEM)),
        input_output_aliases={a: 2 + a for a in range(2 * n)},
        compiler_params=pltpu.CompilerParams(has_side_effects=pltpu.SideEffectType.DATAFLOW_SIDE_EFFECTING),
    )(*held, *extra)
    return dict(send=outs[0], recv=outs[1], srcs=list(outs[2:2 + n]), lands=list(outs[2 + n:2 + 2 * n]),
                token=outs[-1], gather=list(gather))


def _exchange_wait(flight, after, me, name):
    n = len(flight["srcs"])
    gather = flight["gather"]

    def body(*refs):
        srcs, land_refs = refs[:n], refs[n:2 * n]
        send_sems, recv_sems = refs[2 * n], refs[2 * n + 1]
        ix, iy, ic = lax.axis_index("x"), lax.axis_index("y"), lax.axis_index("c")
        mine = 4 * ix + 2 * iy + ic
        for a in range(n):
            for k, peer, peer_idx in _peers(ix, iy, ic):
                cp = pltpu.make_async_remote_copy(
                    src_ref=srcs[a] if gather[a] else srcs[a].at[peer_idx], dst_ref=land_refs[a].at[mine],
                    send_sem=send_sems.at[7 * a + k], recv_sem=recv_sems.at[7 * a + k], device_id=peer,
                    device_id_type=pl.DeviceIdType.MESH)
                cp.wait_send()
                cp.wait_recv()

    hbm = pl.BlockSpec(memory_space=pltpu.HBM)
    sem = pl.BlockSpec(memory_space=pltpu.SEMAPHORE)
    held = flight["srcs"] + flight["lands"]
    outs = pl.pallas_call(
        body, name=name, out_shape=tuple(pltpu.HBM(a.shape, a.dtype) for a in held),
        in_specs=[hbm] * (2 * n) + [sem, sem, pl.BlockSpec(memory_space=pl.ANY)], out_specs=tuple([hbm] * (2 * n)),
        input_output_aliases={a: a for a in range(2 * n)},
        compiler_params=pltpu.CompilerParams(has_side_effects=pltpu.SideEffectType.DATAFLOW_SIDE_EFFECTING),
    )(*held, flight["send"], flight["recv"], after)
    got = []
    for a in range(n):
        src, land = outs[a], outs[n + a]
        own = src if gather[a] else lax.dynamic_index_in_dim(src, me, axis=0, keepdims=False)
        got.append(lax.dynamic_update_index_in_dim(land, own, me, axis=0))
    return got


SEQ_ROWS = 8


def _ada_fwd_exchange(c_rows, ada_w, ada_b_cols):
    D = c_rows.shape[1]
    cols = ada_w.shape[2]

    def body(c_ref, w_ref, b_ref, call_ref, mod_ref, stage, send_c, recv_c, send_m, recv_m):
        ix, iy, ic = lax.axis_index("x"), lax.axis_index("y"), lax.axis_index("c")
        me = 4 * ix + 2 * iy + ic
        peers = _peers(ix, iy, ic)
        gathers = [pltpu.make_async_remote_copy(
            src_ref=c_ref, dst_ref=call_ref.at[me], send_sem=send_c.at[k], recv_sem=recv_c.at[k], device_id=peer,
            device_id_type=pl.DeviceIdType.MESH) for k, peer, _ in peers]
        for cp in gathers:
            cp.start()
        call_ref[me] = c_ref[...]
        for cp in gathers:
            cp.wait()
        ca = _silu(call_ref[...].reshape(N_DEV * SEQ_ROWS, D))
        for l in range(DEPTH):
            mod_l = _mm(ca, w_ref[l]) + b_ref[l:l + 1, :]
            for j in range(N_DEV):
                stage[j, l] = mod_l[SEQ_ROWS * j:SEQ_ROWS * (j + 1)]
        scatters = [pltpu.make_async_remote_copy(
            src_ref=stage.at[idx], dst_ref=mod_ref.at[me], send_sem=send_m.at[k], recv_sem=recv_m.at[k],
            device_id=peer, device_id_type=pl.DeviceIdType.MESH) for k, peer, idx in peers]
        for cp in scatters:
            cp.start()
        mod_ref[me] = stage[me]
        for cp in scatters:
            cp.wait()

    vmem = pl.BlockSpec(memory_space=pltpu.VMEM)
    return pl.pallas_call(
        body, name="ada_fwd_exchange",
        out_shape=[jax.ShapeDtypeStruct((N_DEV, SEQ_ROWS, D), F32),
                   jax.ShapeDtypeStruct((N_DEV, DEPTH, SEQ_ROWS, cols), F32)],
        in_specs=[vmem] * 3, out_specs=[vmem, vmem],
        scratch_shapes=[pltpu.VMEM((N_DEV, DEPTH, SEQ_ROWS, cols), F32)] + [pltpu.SemaphoreType.DMA((N_DEV - 1,))] * 4,
        compiler_params=pltpu.CompilerParams(vmem_limit_bytes=32 * VMEM_MB),
    )(c_rows, ada_w, ada_b_cols)


def _ada_bwd(c_all, d_mod_cols):
    nb, D = c_all.shape
    cols = d_mod_cols.shape[2]

    def body(c_ref, dm_ref, out_ref):
        ca = _silu(c_ref[...])
        for l in range(DEPTH):
            out_ref[l] = _mm_tn(ca, dm_ref[l])

    return pl.pallas_call(
        body, name="ada_bwd", out_shape=jax.ShapeDtypeStruct((DEPTH, D, cols), F32),
        in_specs=[pl.BlockSpec(memory_space=pltpu.VMEM)] * 2, out_specs=pl.BlockSpec(memory_space=pltpu.VMEM),
        compiler_params=pltpu.CompilerParams(vmem_limit_bytes=32 * VMEM_MB),
    )(c_all, d_mod_cols)


def _sum_adamw(parts, w, m, v, name, after=None):
    P, R, C = parts.shape
    tr = 256 if (R % 256 == 0 and R > 256) else R
    extra = [] if after is None else [after]

    def body(p_ref, w_ref, m_ref, v_ref, *rest):
        g_ref, d_ref, nm_ref, nv_ref = rest[-4:]
        g = p_ref[0].astype(F32)
        for k in range(1, P):
            g = g + p_ref[k].astype(F32)
        g_ref[...] = g
        nm = ADAM_B1 * m_ref[...] + (1.0 - ADAM_B1) * g
        nv = ADAM_B2 * v_ref[...] + (1.0 - ADAM_B2) * (g * g)
        nm_ref[...] = nm
        nv_ref[...] = nv
        m_hat = nm / (1.0 - ADAM_B1 ** ADAM_STEP)
        v_hat = nv / (1.0 - ADAM_B2 ** ADAM_STEP)
        d_ref[...] = -ADAM_LR * (m_hat / (jnp.sqrt(v_hat) + ADAM_EPS) + ADAM_WD * w_ref[...])

    blk = pl.BlockSpec((tr, C), lambda i: (i, 0))
    shp = jax.ShapeDtypeStruct((R, C), F32)
    return pl.pallas_call(
        body, name=name, grid=(R // tr,),
        in_specs=[pl.BlockSpec((P, tr, C), lambda i: (0, i, 0)), blk, blk, blk]
        + [pl.BlockSpec(memory_space=pl.ANY)] * len(extra),
        out_specs=[blk, blk, blk, blk], out_shape=[shp, shp, shp, shp],
        compiler_params=_cp(("parallel",)),
    )(parts, w, m, v, *extra)


def _sum_adamw_layer(parts, w, m, v, layer, name, prev=None, after=None):
    P, R, C = parts.shape
    tr = 256 if (R % 256 == 0 and R > 256) else R

    def body(p_ref, w_ref, m_ref, v_ref, *rest):
        g_ref, d_ref, nm_ref, nv_ref = rest[-4:]
        g = p_ref[0].astype(F32)
        for k in range(1, P):
            g = g + p_ref[k].astype(F32)
        g_ref[0] = g
        nm = ADAM_B1 * m_ref[0] + (1.0 - ADAM_B1) * g
        nv = ADAM_B2 * v_ref[0] + (1.0 - ADAM_B2) * (g * g)
        nm_ref[0] = nm
        nv_ref[0] = nv
        m_hat = nm / (1.0 - ADAM_B1 ** ADAM_STEP)
        v_hat = nv / (1.0 - ADAM_B2 ** ADAM_STEP)
        d_ref[0] = -ADAM_LR * (m_hat / (jnp.sqrt(v_hat) + ADAM_EPS) + ADAM_WD * w_ref[0])

    blk = pl.BlockSpec((1, tr, C), lambda i: (layer, i, 0))
    shp = jax.ShapeDtypeStruct(w.shape, F32)
    in_specs = [pl.BlockSpec((P, tr, C), lambda i: (0, i, 0)), blk, blk, blk]
    args = [parts, w, m, v]
    aliases = {}
    if prev is not None:
        in_specs += [pl.BlockSpec(memory_space=pl.ANY)] * 4
        args += list(prev)
        aliases = {4 + k: k for k in range(4)}
    if after is not None:
        in_specs.append(pl.BlockSpec(memory_space=pl.ANY))
        args.append(after)
    return list(pl.pallas_call(
        body, name=name, grid=(R // tr,), in_specs=in_specs, out_specs=[blk] * 4, out_shape=[shp] * 4,
        input_output_aliases=aliases, compiler_params=_cp(("parallel",)),
    )(*args))


SMALL = ["norm_w", "mla_q_norm", "mla_kv_norm", "gla_w_g2", "gla_b_g2", "gla_norm", "final_norm"]


SMALL_ROWS = 72


def _pack_small(loss, part):
    flat = [jnp.pad(loss.reshape(1), (0, 127))] + [part[n].reshape(-1) for n in SMALL]
    used = sum(f.shape[0] for f in flat)
    flat.append(jnp.zeros((SMALL_ROWS * 128 - used,), F32))
    return jnp.concatenate(flat).reshape(SMALL_ROWS, 128)


def _small_adamw(packed_parts, w, m, v, after=None):
    n = len(w)
    extra = [] if after is None else [after]

    def body(*refs):
        p_ref = refs[0]
        w_refs, m_refs, v_refs = refs[1:1 + n], refs[1 + n:1 + 2 * n], refs[1 + 2 * n:1 + 3 * n]
        outs, acc = refs[1 + 3 * n + len(extra):-1], refs[-1]
        total = p_ref[0]
        for k in range(1, N_DEV):
            total = total + p_ref[k]
        acc[...] = total
        outs[0][...] = acc[0:1, :]
        r0 = 1
        for i in range(n):
            shp = w_refs[i].shape
            if len(shp) == 3:
                g = acc[r0:r0 + shp[0] * shp[1], :].reshape(shp)
                r0 += shp[0] * shp[1]
            elif shp[1] < 128:
                g = acc[r0:r0 + shp[0], 0:shp[1]]
                r0 += shp[0]
            else:
                k = shp[1] // 128
                g = jnp.concatenate(
                    [jnp.concatenate([acc[r0 + l * k + j:r0 + l * k + j + 1, :] for j in range(k)], axis=1)
                     for l in range(shp[0])], axis=0)
                r0 += shp[0] * k
            nm = ADAM_B1 * m_refs[i][...] + (1.0 - ADAM_B1) * g
            nv = ADAM_B2 * v_refs[i][...] + (1.0 - ADAM_B2) * (g * g)
            m_hat = nm / (1.0 - ADAM_B1 ** ADAM_STEP)
            v_hat = nv / (1.0 - ADAM_B2 ** ADAM_STEP)
            outs[1 + 4 * i][...] = g
            outs[2 + 4 * i][...] = -ADAM_LR * (m_hat / (jnp.sqrt(v_hat) + ADAM_EPS) + ADAM_WD * w_refs[i][...])
            outs[3 + 4 * i][...] = nm
            outs[4 + 4 * i][...] = nv

    vmem = pl.BlockSpec(memory_space=pltpu.VMEM)
    out_shape = [jax.ShapeDtypeStruct((1, 128), F32)]
    for a in w:
        out_shape += [jax.ShapeDtypeStruct(a.shape, F32)] * 4
    outs = pl.pallas_call(
        body, name="adamw_small", in_specs=[vmem] * (1 + 3 * n) + [pl.BlockSpec(memory_space=pl.ANY)] * len(extra),
        out_specs=[vmem] * (1 + 4 * n), out_shape=out_shape, scratch_shapes=[pltpu.VMEM((SMALL_ROWS, 128), F32)],
    )(packed_parts, *w, *m, *v, *extra)
    return outs[0], [outs[1 + 4 * i:5 + 4 * i] for i in range(n)]


WEIGHTS = ["norm_w", "ada_w", "ada_b", "w_in", "mla_q_norm", "w_uq", "mla_kv_norm", "w_ukv", "gla_w_g2",
           "gla_b_g2", "gla_norm", "w_out", "final_norm"]


def kernel(x, c, positions, norm_w, ada_w, ada_b, w_in, mla_q_norm, w_uq, mla_kv_norm, w_ukv, gla_w_g2, gla_b_g2, gla_norm, w_out, final_norm, loss_target, m_norm_w, m_ada_w, m_ada_b, m_w_in, m_mla_q_norm, m_w_uq, m_mla_kv_norm, m_w_ukv, m_gla_w_g2, m_gla_b_g2, m_gla_norm, m_w_out, m_final_norm, v_norm_w, v_ada_w, v_ada_b, v_w_in, v_mla_q_norm, v_w_uq, v_mla_kv_norm, v_w_ukv, v_gla_w_g2, v_gla_b_g2, v_gla_norm, v_w_out, v_final_norm):
    w = dict(norm_w=norm_w, ada_w=ada_w, ada_b=ada_b, w_in=w_in, mla_q_norm=mla_q_norm, w_uq=w_uq,
             mla_kv_norm=mla_kv_norm, w_ukv=w_ukv, gla_w_g2=gla_w_g2, gla_b_g2=gla_b_g2, gla_norm=gla_norm,
             w_out=w_out, final_norm=final_norm)
    m = dict(norm_w=m_norm_w, ada_w=m_ada_w, ada_b=m_ada_b, w_in=m_w_in, mla_q_norm=m_mla_q_norm, w_uq=m_w_uq,
             mla_kv_norm=m_mla_kv_norm, w_ukv=m_w_ukv, gla_w_g2=m_gla_w_g2, gla_b_g2=m_gla_b_g2,
             gla_norm=m_gla_norm, w_out=m_w_out, final_norm=m_final_norm)
    v = dict(norm_w=v_norm_w, ada_w=v_ada_w, ada_b=v_ada_b, w_in=v_w_in, mla_q_norm=v_mla_q_norm, w_uq=v_w_uq,
             mla_kv_norm=v_mla_kv_norm, w_ukv=v_w_ukv, gla_w_g2=v_gla_w_g2, gla_b_g2=v_gla_b_g2,
             gla_norm=v_gla_norm, w_out=v_w_out, final_norm=v_final_norm)
    B, S, D = x.shape
    me = 4 * lax.axis_index("x") + 2 * lax.axis_index("y") + lax.axis_index("c")
    ada_cols = ada_w.shape[2]
    cast = lambda a: a.astype(_MXU)

    sharded = ["w_in", "w_uq", "w_ukv", "w_out"]

    whole_in = _arrange_w_in
    whole_rest = lambda blks: (*_arrange_mla_weights(blks[0], blks[1]), blks[2].reshape(D, D))
    blocks_in = lambda dw_in_a: _unarrange_w_in(dw_in_a, N_DEV, w_in.shape[2])
    blocks_rest = lambda dw_out, dw_uq_a, dw_ukv_a: [
        *_unarrange_mla_weights(dw_uq_a, dw_ukv_a), dw_out.reshape(N_DEV, D // N_DEV, D).astype(jnp.bfloat16)]

    ada_b_cols = lax.dynamic_slice(ada_b, (0, me * ada_cols), (DEPTH, ada_cols))
    c_g, mod_recv = _ada_fwd_exchange(jnp.pad(c, ((0, SEQ_ROWS - B), (0, 0))), ada_w, ada_b_cols)
    c_all = c_g[:, :B].reshape(N_DEV * B, D)
    mod = jnp.transpose(mod_recv[:, :, :B], (1, 2, 0, 3)).reshape(DEPTH, B, 3 * D)

    flight_i = _exchange_start([cast(w_in[0])], [True], "gather_start_first", after=mod)
    flight_r = _exchange_start([cast(w[n][0]) for n in sharded[1:]], [True] * 3, "gather_start_layer0",
                               after=flight_i["token"])
    flight_w = _exchange_start([cast(w[n][1]) for n in sharded], [True] * 4, "gather_start_layer1",
                               after=flight_r["token"])
    small_w = {n: w[n] for n in SMALL}
    layer_small = lambda l: {n: a[l] for n, a in small_w.items() if n != "final_norm"}
    tabs = _rope_tables(positions.reshape(B, S, 1), flight_w["token"])
    late0 = lambda after: whole_rest(_exchange_wait(flight_r, after, me, "gather_wait_layer0"))
    (w_in0_g,) = _exchange_wait(flight_i, tabs[0], me, "gather_wait_first")
    x1, saved0 = _layer_fwd(x, tabs, mod[0], layer_small(0), whole_in(w_in0_g), late_weights=late0)
    got1 = _exchange_wait(flight_w, x1, me, "gather_wait_layer1")
    (dx, loss, d_fw), saved1 = _layer_fwd(x1, tabs, mod[1], layer_small(1), whole_in(got1[0]), *whole_rest(got1[1:]),
                                          loss_head=(final_norm.reshape(1, D), loss_target))

    dx, g1 = _layer_bwd(dx, saved1, tabs)
    flight_g = _exchange_start([blocks_in(g1["w_in_a"])] + blocks_rest(g1["w_out"], g1["w_uq_a"], g1["w_ukv_a"]),
                               [False] * 4, "grads_start_layer1")
    flights = {}

    def early0(dw_out, dw_uq_a, dw_ukv_a):
        flights["rest0"] = _exchange_start(blocks_rest(dw_out, dw_uq_a, dw_ukv_a), [False] * 3, "grads_start_layer0")
        return flights["rest0"]["token"]

    def early_in0(dw_in_a):
        flights["in0"] = _exchange_start([blocks_in(dw_in_a)], [False], "exchange_start_last")
        return flights["in0"]["token"]

    grad_x, g0 = _layer_bwd(dx, saved0, tabs, after=flight_g["token"], early_grads=early0, early_w_in=early_in0)
    parts1 = _exchange_wait(flight_g, grad_x, me, "grads_wait_layer1")
    rest0 = _exchange_wait(flights["rest0"], g0["w_in_a"], me, "grads_wait_layer0")

    both = lambda n: jnp.stack([g0[n], g1[n]])
    d_mod = both("d_mod")
    part = dict(norm_w=both("norm_w"), mla_q_norm=both("mla_q_norm"), mla_kv_norm=both("mla_kv_norm"),
                gla_w_g2=both("gla_w_g2"), gla_b_g2=both("gla_b_g2"), gla_norm=both("gla_norm256")[:, 0:128],
                final_norm=d_fw)
    flight_s = _exchange_start([d_mod, _pack_small(loss, part)], [True, True], "gather_small_start")
    flight_l = flights["in0"]
    res = {}
    behind = flight_s["token"]
    for a, name in enumerate(sharded):
        res[name] = _sum_adamw_layer(parts1[a], w[name], m[name], v[name], 1, "adamw_%s_layer1" % name, after=behind)
        behind = res[name][1]
    for a, name in enumerate(sharded[1:]):
        res[name] = _sum_adamw_layer(rest0[a], w[name], m[name], v[name], 0, "adamw_%s_layer0" % name,
                                     prev=res[name], after=behind)
        behind = res[name][1]
    (in0,) = _exchange_wait(flight_l, behind, me, "exchange_wait_last")
    res["w_in"] = _sum_adamw_layer(in0, w_in, m_w_in, v_w_in, 0, "adamw_w_in_layer0", prev=res["w_in"])
    behind = res["w_in"][1]

    d_mod_g, small_g = _exchange_wait(flight_s, behind, me, "gather_small_wait")
    d_mod_all = jnp.transpose(d_mod_g, (1, 0, 2, 3)).reshape(DEPTH, N_DEV * B, 3 * D)
    d_mod_cols = lax.dynamic_slice(d_mod_all, (0, 0, me * ada_cols), (DEPTH, N_DEV * B, ada_cols))
    g_ada_w = _ada_bwd(c_all, d_mod_cols)

    def update(name, parts2d, after):
        shp = w[name].shape
        two = lambda a: a.reshape(parts2d.shape[1:])
        out = _sum_adamw(parts2d, two(w[name]), two(m[name]), two(v[name]), "adamw_" + name, after=after)
        res[name] = [o.reshape(shp) for o in out]
        return out[1]

    behind = update("ada_w", g_ada_w.reshape(1, DEPTH * D, ada_cols), behind)
    behind = update("ada_b", jnp.transpose(d_mod_g, (0, 2, 1, 3)).reshape(N_DEV * B, DEPTH * 3 * D // 128, 128), behind)
    row = lambda a: a.reshape(1, D) if a.ndim == 1 else a
    loss_sum, small_out = _small_adamw(small_g, [row(w[n]) for n in SMALL], [row(m[n]) for n in SMALL],
                                       [row(v[n]) for n in SMALL], after=behind)
    for n, outs in zip(SMALL, small_out):
        res[n] = [o.reshape(w[n].shape) for o in outs]
    loss_out = loss_sum[0, 0]
    return (loss_out, grad_x, *[res[n][0] for n in WEIGHTS], *[res[n][1] for n in WEIGHTS],
            *[res[n][2] for n in WEIGHTS], *[res[n][3] for n in WEIGHTS])
```
